```python
import jax, jax.numpy as jnp
from jax import lax
import numpy as np

D_MODEL = 1024
BATCH = 8
SEQ = 8192
DEPTH = 1

N_META = 16
RET_HEADS = 4
RET_QK_DIM = D_MODEL // RET_HEADS
RET_WIDTH = 2 * D_MODEL
RET_V_DIM = RET_WIDTH // RET_HEADS
RET_CHUNK = 64
GLA_HEADS = 4
GLA_K_DIM = (D_MODEL // 2) // GLA_HEADS
GLA_WIDTH = D_MODEL
GLA_V_DIM = GLA_WIDTH // GLA_HEADS
GLA_GATE_RANK = 16
GLA_GATE_TAU = 16.0
GLA_CHUNK = 16
ROPE_BASE = 10000.0
EPS = 1e-6
IN_SIZES = (RET_HEADS * RET_QK_DIM, RET_HEADS * RET_QK_DIM, RET_WIDTH, RET_WIDTH,
            GLA_HEADS * GLA_K_DIM, GLA_HEADS * GLA_K_DIM, GLA_WIDTH, GLA_WIDTH,
            GLA_GATE_RANK, D_MODEL, D_MODEL)
IN_COLS = sum(IN_SIZES)

kernel_name = "hybrid_retention_gla_gated_merge"


def rms_norm(x, gain):
    xf = x.astype(jnp.float32)
    y = xf * lax.rsqrt(jnp.mean(xf * xf, axis=-1, keepdims=True) + EPS) * gain.astype(jnp.float32)
    return y.astype(x.dtype)


def head_group_norm(o, gain):
    of = o.astype(jnp.float32)
    mu = jnp.mean(of, axis=-1, keepdims=True)
    var = jnp.mean(jnp.square(of - mu), axis=-1, keepdims=True)
    return ((of - mu) * lax.rsqrt(var + EPS) * gain.astype(jnp.float32)).astype(o.dtype)


def head_rms_norm(o, gain):
    of = o.astype(jnp.float32)
    return (of * lax.rsqrt(jnp.mean(of * of, axis=-1, keepdims=True) + EPS) * gain.astype(jnp.float32)).astype(o.dtype)


def rope(t, pos):
    half = t.shape[-1] // 2
    inv = ROPE_BASE ** (-jnp.arange(half, dtype=jnp.float32) / half)
    ang = pos[:, None] * inv[None, :]
    cos = jnp.cos(ang)[None, :, None, :]
    sin = jnp.sin(ang)[None, :, None, :]
    t1 = t[..., :half].astype(jnp.float32)
    t2 = t[..., half:].astype(jnp.float32)
    return jnp.concatenate([t1 * cos - t2 * sin, t2 * cos + t1 * sin], axis=-1).astype(t.dtype)


def to_chunks(t, c):
    pad = (-N_META) % c
    t = jnp.pad(t, ((0, 0), (pad, 0), (0, 0), (0, 0)))
    b, lp, h, d = t.shape
    return t.reshape(b, lp // c, c, h, d)


def from_chunks(t, c):
    b, n, _, h, d = t.shape
    pad = (-N_META) % c
    return t.reshape(b, n * c, h, d)[:, pad:]


def retention_chunked(q, k, v):
    c = RET_CHUNK
    qc, kc, vc = to_chunks(q, c), to_chunks(k, c), to_chunks(v, c)
    log_gamma = jnp.log1p(-(2.0 ** (-5.0 - jnp.arange(RET_HEADS, dtype=jnp.float32))))
    idx = jnp.arange(c, dtype=jnp.float32)
    rel = idx[:, None] - idx[None, :]
    decay = jnp.where(rel[None] >= 0, jnp.exp(jnp.maximum(rel, 0.0)[None] * log_gamma[:, None, None]), 0.0)
    scores = jnp.einsum('bnihd,bnjhd->bnhij', qc, kc) * decay[None, None]
    intra = jnp.einsum('bnhij,bnjhe->bnihe', scores, vc)
    xi = jnp.exp((idx[:, None] + 1.0) * log_gamma[None, :])
    zeta = jnp.exp((c - 1.0 - idx[:, None]) * log_gamma[None, :])
    gamma_c = jnp.exp(c * log_gamma)

    def step(state, xs):
        q_n, k_n, v_n = xs
        inter = jnp.einsum('bihd,bhde->bihe', q_n, state) * xi[None, :, :, None]
        state = state * gamma_c[None, :, None, None] + jnp.einsum('bjhd,bjhe->bhde', k_n * zeta[None, :, :, None], v_n)
        return state, inter

    bsz = q.shape[0]
    state0 = jnp.zeros((bsz, RET_HEADS, RET_QK_DIM, RET_V_DIM), jnp.float32)
    xs = (jnp.moveaxis(qc, 1, 0), jnp.moveaxis(kc, 1, 0), jnp.moveaxis(vc, 1, 0))
    _, inter = lax.scan(step, state0, xs)
    out = intra + jnp.moveaxis(inter, 0, 1)
    return from_chunks(out, c).astype(v.dtype)


def gla_chunked(q, k, v, log_a):
    c = GLA_CHUNK
    qc, kc, vc = to_chunks(q, c), to_chunks(k, c), to_chunks(v, c)
    ac = to_chunks(log_a.astype(jnp.float32), c)
    b = jnp.cumsum(ac, axis=2)
    b_last = b[:, :, -1]
    q_dec = qc * jnp.exp(b)
    k_inv = kc * jnp.exp(-b)
    k_end = kc * jnp.exp(b_last[:, :, None] - b)
    mask = jnp.tril(jnp.ones((c, c), dtype=bool))
    scores = jnp.where(mask, jnp.einsum('bnihd,bnjhd->bnhij', q_dec, k_inv), 0.0)
    intra = jnp.einsum('bnhij,bnjhe->bnihe', scores, vc)

    def step(state, xs):
        q_n, k_n, v_n, a_n = xs
        inter = jnp.einsum('bihd,bhde->bihe', q_n, state)
        state = state * jnp.exp(a_n)[..., None] + jnp.einsum('bjhd,bjhe->bhde', k_n, v_n)
        return state, inter

    bsz = q.shape[0]
    state0 = jnp.zeros((bsz, GLA_HEADS, GLA_K_DIM, GLA_V_DIM), jnp.float32)
    xs = (jnp.moveaxis(q_dec, 1, 0), jnp.moveaxis(k_end, 1, 0), jnp.moveaxis(vc, 1, 0), jnp.moveaxis(b_last, 1, 0))
    _, inter = lax.scan(step, state0, xs)
    out = intra + jnp.moveaxis(inter, 0, 1)
    return from_chunks(out, c).astype(v.dtype)


def hybrid_layer(h, norm_gain, w_in, w_gate_up, b_gate, ret_norm_gain, gla_norm_gain,
                 w_branch_ret, w_branch_gla, w_out):
    bsz, length, _ = h.shape
    u = rms_norm(h, norm_gain)
    proj = u @ w_in
    points = [int(p) for p in np.cumsum(IN_SIZES)[:-1]]
    (rq, rk, rv, rg, gq, gk, gv, gg, glr, m_ret, m_gla) = jnp.split(proj, points, axis=-1)

    pos = jnp.arange(length, dtype=jnp.float32)
    rq = rope(rq.reshape(bsz, length, RET_HEADS, RET_QK_DIM), pos)
    rk = rope(rk.reshape(bsz, length, RET_HEADS, RET_QK_DIM), pos) * (RET_QK_DIM ** -0.5)
    rv = rv.reshape(bsz, length, RET_HEADS, RET_V_DIM)
    o_ret = retention_chunked(rq, rk, rv)
    o_ret = head_group_norm(o_ret, ret_norm_gain.reshape(RET_HEADS, RET_V_DIM)).reshape(bsz, length, RET_WIDTH)
    o_ret = o_ret * jax.nn.silu(rg)

    gq = gq.reshape(bsz, length, GLA_HEADS, GLA_K_DIM) * (GLA_K_DIM ** -0.5)
    gk = gk.reshape(bsz, length, GLA_HEADS, GLA_K_DIM)
    gv = gv.reshape(bsz, length, GLA_HEADS, GLA_V_DIM)
    log_a = jax.nn.log_sigmoid((glr @ w_gate_up + b_gate).astype(jnp.float32)) / GLA_GATE_TAU
    log_a = log_a.reshape(bsz, length, GLA_HEADS, GLA_K_DIM)
    o_gla = gla_chunked(gq, gk, gv, log_a)
    o_gla = head_rms_norm(o_gla, gla_norm_gain.reshape(GLA_HEADS, GLA_V_DIM)).reshape(bsz, length, GLA_WIDTH)
    o_gla = o_gla * jax.nn.silu(gg)

    merged = jax.nn.sigmoid(m_ret) * (o_ret @ w_branch_ret) + jax.nn.sigmoid(m_gla) * (o_gla @ w_branch_gla)
    return h + merged @ w_out


def _fwd_setup_inputs(seed: int = 0) -> dict:
    key = jax.random.key(seed)
    ks = jax.random.split(key, 12)
    f = jnp.float32
    gk_dim = GLA_HEADS * GLA_K_DIM
    return {
        "x": jax.random.normal(ks[0], (BATCH, SEQ, D_MODEL), f),
        "meta_tokens": jax.random.normal(ks[1], (N_META, D_MODEL), f),
        "norm_gain": 1.0 + 0.02 * jax.random.normal(ks[2], (DEPTH, D_MODEL), f),
        "w_in": jax.random.normal(ks[3], (DEPTH, D_MODEL, IN_COLS), f) * D_MODEL ** -0.5,
        "w_gate_up": jax.random.normal(ks[4], (DEPTH, GLA_GATE_RANK, gk_dim), f) * GLA_GATE_RANK ** -0.5,
        "b_gate": 0.01 * jax.random.normal(ks[5], (DEPTH, gk_dim), f),
        "ret_norm_gain": 1.0 + 0.02 * jax.random.normal(ks[6], (DEPTH, RET_WIDTH), f),
        "gla_norm_gain": 1.0 + 0.02 * jax.random.normal(ks[7], (DEPTH, GLA_WIDTH), f),
        "w_branch_ret": jax.random.normal(ks[8], (DEPTH, RET_WIDTH, D_MODEL), f) * RET_WIDTH ** -0.5,
        "w_branch_gla": jax.random.normal(ks[9], (DEPTH, GLA_WIDTH, D_MODEL), f) * GLA_WIDTH ** -0.5,
        "w_out": jax.random.normal(ks[10], (DEPTH, D_MODEL, D_MODEL), f) * D_MODEL ** -0.5,
        "final_norm_gain": 1.0 + 0.02 * jax.random.normal(ks[11], (D_MODEL,), f),
    }


def _fwd_reference(x, meta_tokens, norm_gain, w_in, w_gate_up, b_gate, ret_norm_gain, gla_norm_gain,
              w_branch_ret, w_branch_gla, w_out, final_norm_gain):
    bsz = x.shape[0]
    meta = jnp.broadcast_to(meta_tokens.astype(x.dtype)[None], (bsz, N_META, D_MODEL))
    h = jnp.concatenate([meta, x], axis=1)
    for layer in range(DEPTH):
        h = hybrid_layer(h, norm_gain[layer], w_in[layer], w_gate_up[layer], b_gate[layer],
                         ret_norm_gain[layer], gla_norm_gain[layer], w_branch_ret[layer],
                         w_branch_gla[layer], w_out[layer])
    h = rms_norm(h, final_norm_gain)
    return h[:, N_META:]


import jax as _jax
import jax.numpy as _jnp

TWIN_FORMAT = 'train_step'
FWD_PARAMS = ['x', 'meta_tokens', 'norm_gain', 'w_in', 'w_gate_up', 'b_gate', 'ret_norm_gain', 'gla_norm_gain', 'w_branch_ret', 'w_branch_gla', 'w_out', 'final_norm_gain']
TWIN_WEIGHTS = ['meta_tokens', 'norm_gain', 'w_in', 'w_gate_up', 'b_gate', 'ret_norm_gain', 'gla_norm_gain', 'w_branch_ret', 'w_branch_gla', 'w_out', 'final_norm_gain']
TWIN_DIFF_INPUT = 'x'
TWIN_INPUTS = ['x', 'meta_tokens', 'norm_gain', 'w_in', 'w_gate_up', 'b_gate', 'ret_norm_gain', 'gla_norm_gain', 'w_branch_ret', 'w_branch_gla', 'w_out', 'final_norm_gain', 'loss_target', 'm_meta_tokens', 'm_norm_gain', 'm_w_in', 'm_w_gate_up', 'm_b_gate', 'm_ret_norm_gain', 'm_gla_norm_gain', 'm_w_branch_ret', 'm_w_branch_gla', 'm_w_out', 'm_final_norm_gain', 'v_meta_tokens', 'v_norm_gain', 'v_w_in', 'v_w_gate_up', 'v_b_gate', 'v_ret_norm_gain', 'v_gla_norm_gain', 'v_w_branch_ret', 'v_w_branch_gla', 'v_w_out', 'v_final_norm_gain']
TWIN_OUTPUTS = ['loss', 'grad_x', 'grad_meta_tokens', 'grad_norm_gain', 'grad_w_in', 'grad_w_gate_up', 'grad_b_gate', 'grad_ret_norm_gain', 'grad_gla_norm_gain', 'grad_w_branch_ret', 'grad_w_branch_gla', 'grad_w_out', 'grad_final_norm_gain', 'delta_meta_tokens', 'delta_norm_gain', 'delta_w_in', 'delta_w_gate_up', 'delta_b_gate', 'delta_ret_norm_gain', 'delta_gla_norm_gain', 'delta_w_branch_ret', 'delta_w_branch_gla', 'delta_w_out', 'delta_final_norm_gain', 'new_m_meta_tokens', 'new_m_norm_gain', 'new_m_w_in', 'new_m_w_gate_up', 'new_m_b_gate', 'new_m_ret_norm_gain', 'new_m_gla_norm_gain', 'new_m_w_branch_ret', 'new_m_w_branch_gla', 'new_m_w_out', 'new_m_final_norm_gain', 'new_v_meta_tokens', 'new_v_norm_gain', 'new_v_w_in', 'new_v_w_gate_up', 'new_v_b_gate', 'new_v_ret_norm_gain', 'new_v_gla_norm_gain', 'new_v_w_branch_ret', 'new_v_w_branch_gla', 'new_v_w_out', 'new_v_final_norm_gain']
TWIN_LEAF_KINDS = {'loss': 'loss', 'grad_x': 'grad_x', 'grad_meta_tokens': 'grad_w', 'grad_norm_gain': 'grad_w', 'grad_w_in': 'grad_w', 'grad_w_gate_up': 'grad_w', 'grad_b_gate': 'grad_w', 'grad_ret_norm_gain': 'grad_w', 'grad_gla_norm_gain': 'grad_w', 'grad_w_branch_ret': 'grad_w', 'grad_w_branch_gla': 'grad_w', 'grad_w_out': 'grad_w', 'grad_final_norm_gain': 'grad_w', 'delta_meta_tokens': 'delta_w', 'delta_norm_gain': 'delta_w', 'delta_w_in': 'delta_w', 'delta_w_gate_up': 'delta_w', 'delta_b_gate': 'delta_w', 'delta_ret_norm_gain': 'delta_w', 'delta_gla_norm_gain': 'delta_w', 'delta_w_branch_ret': 'delta_w', 'delta_w_branch_gla': 'delta_w', 'delta_w_out': 'delta_w', 'delta_final_norm_gain': 'delta_w', 'new_m_meta_tokens': 'new_m', 'new_m_norm_gain': 'new_m', 'new_m_w_in': 'new_m', 'new_m_w_gate_up': 'new_m', 'new_m_b_gate': 'new_m', 'new_m_ret_norm_gain': 'new_m', 'new_m_gla_norm_gain': 'new_m', 'new_m_w_branch_ret': 'new_m', 'new_m_w_branch_gla': 'new_m', 'new_m_w_out': 'new_m', 'new_m_final_norm_gain': 'new_m', 'new_v_meta_tokens': 'new_v', 'new_v_norm_gain': 'new_v', 'new_v_w_in': 'new_v', 'new_v_w_gate_up': 'new_v', 'new_v_b_gate': 'new_v', 'new_v_ret_norm_gain': 'new_v', 'new_v_gla_norm_gain': 'new_v', 'new_v_w_branch_ret': 'new_v', 'new_v_w_branch_gla': 'new_v', 'new_v_w_out': 'new_v', 'new_v_final_norm_gain': 'new_v'}


def _forward(args):
    return _fwd_reference(*[args[k] for k in FWD_PARAMS])


def _output_shape():
    out = _jax.eval_shape(lambda: _forward(_fwd_setup_inputs(0)))
    return out.shape, out.dtype

N_MICROBATCH = 1
ADAM_LR = 0.001
ADAM_B1 = 0.9
ADAM_B2 = 0.999
ADAM_EPS = 1e-08
ADAM_WD = 0.01
ADAM_STEP = 10
PER_EXAMPLE_BATCH_AXIS = {'x': 0, 'loss_target': 0}
SHARED_INPUTS = []
_WEIGHT_DTYPES = {'meta_tokens': _jnp.float32, 'norm_gain': _jnp.float32, 'w_in': _jnp.float32, 'w_gate_up': _jnp.float32, 'b_gate': _jnp.float32, 'ret_norm_gain': _jnp.float32, 'gla_norm_gain': _jnp.float32, 'w_branch_ret': _jnp.float32, 'w_branch_gla': _jnp.float32, 'w_out': _jnp.float32, 'final_norm_gain': _jnp.float32}
MOMENT_SCALE = {'meta_tokens': 9.294789e-03, 'norm_gain': 2.488442e-01, 'w_in': 6.508535e-02, 'w_gate_up': 1.194586e-02, 'b_gate': 5.483859e-02, 'ret_norm_gain': 5.350222e-02, 'gla_norm_gain': 7.502294e-02, 'w_branch_ret': 7.356786e-02, 'w_branch_gla': 7.408790e-02, 'w_out': 1.046991e-01, 'final_norm_gain': 6.394252e+01}


def _to_microbatches(a, axis):
    t = _jnp.moveaxis(a, axis, 0)
    t = t.reshape((N_MICROBATCH, t.shape[0] // N_MICROBATCH) + t.shape[1:])
    return _jnp.moveaxis(t, 1, axis + 1)


def setup_inputs(seed: int = 0) -> dict:
    inp = _fwd_setup_inputs(seed)
    key = _jax.random.fold_in(_jax.random.key(seed), 7919)
    shape, _ = _output_shape()
    out = dict(inp)
    out["loss_target"] = _jax.random.normal(_jax.random.fold_in(key, 0), shape, _jnp.float32)
    for i, name in enumerate(TWIN_WEIGHTS):
        w = inp[name].astype(_jnp.float32)
        if MOMENT_SCALE is None:
            s = _jnp.sqrt(_jnp.mean(_jnp.square(w)) + 1e-30)
        else:
            s = MOMENT_SCALE[name]
        km, kv = _jax.random.split(_jax.random.fold_in(key, i + 1))
        out[name] = w
        out["m_" + name] = s * _jax.random.normal(km, w.shape, _jnp.float32)
        out["v_" + name] = (s * s) * _jax.random.uniform(kv, w.shape, _jnp.float32, 0.5, 1.5)
    if N_MICROBATCH > 1:
        for name, axis in PER_EXAMPLE_BATCH_AXIS.items():
            out[name] = _to_microbatches(out[name], axis)
    return {'x': out['x'], 'meta_tokens': out['meta_tokens'], 'norm_gain': out['norm_gain'], 'w_in': out['w_in'], 'w_gate_up': out['w_gate_up'], 'b_gate': out['b_gate'], 'ret_norm_gain': out['ret_norm_gain'], 'gla_norm_gain': out['gla_norm_gain'], 'w_branch_ret': out['w_branch_ret'], 'w_branch_gla': out['w_branch_gla'], 'w_out': out['w_out'], 'final_norm_gain': out['final_norm_gain'], 'loss_target': out['loss_target'], 'm_meta_tokens': out['m_meta_tokens'], 'm_norm_gain': out['m_norm_gain'], 'm_w_in': out['m_w_in'], 'm_w_gate_up': out['m_w_gate_up'], 'm_b_gate': out['m_b_gate'], 'm_ret_norm_gain': out['m_ret_norm_gain'], 'm_gla_norm_gain': out['m_gla_norm_gain'], 'm_w_branch_ret': out['m_w_branch_ret'], 'm_w_branch_gla': out['m_w_branch_gla'], 'm_w_out': out['m_w_out'], 'm_final_norm_gain': out['m_final_norm_gain'], 'v_meta_tokens': out['v_meta_tokens'], 'v_norm_gain': out['v_norm_gain'], 'v_w_in': out['v_w_in'], 'v_w_gate_up': out['v_w_gate_up'], 'v_b_gate': out['v_b_gate'], 'v_ret_norm_gain': out['v_ret_norm_gain'], 'v_gla_norm_gain': out['v_gla_norm_gain'], 'v_w_branch_ret': out['v_w_branch_ret'], 'v_w_branch_gla': out['v_w_branch_gla'], 'v_w_out': out['v_w_out'], 'v_final_norm_gain': out['v_final_norm_gain']}


def _loss(weights, diff, rest, loss_target):
    with _jax.named_scope("forward"):
        args = {**rest, TWIN_DIFF_INPUT: diff, **{k: w.astype(_WEIGHT_DTYPES[k]) for k, w in weights.items()}}
        y = _forward(args)
    with _jax.named_scope("loss_head"):
        err = _jnp.square(y.astype(_jnp.float32) - loss_target)
        return 0.5 * _jnp.sum(_jnp.mean(err, axis=-1)) if err.ndim else 0.5 * err


def _adamw(w, g, m, v):
    m = ADAM_B1 * m + (1.0 - ADAM_B1) * g
    v = ADAM_B2 * v + (1.0 - ADAM_B2) * _jnp.square(g)
    m_hat = m / (1.0 - ADAM_B1 ** ADAM_STEP)
    v_hat = v / (1.0 - ADAM_B2 ** ADAM_STEP)
    delta = -ADAM_LR * (m_hat / (_jnp.sqrt(v_hat) + ADAM_EPS) + ADAM_WD * w)
    return delta, m, v


def reference(x, meta_tokens, norm_gain, w_in, w_gate_up, b_gate, ret_norm_gain, gla_norm_gain, w_branch_ret, w_branch_gla, w_out, final_norm_gain, loss_target, m_meta_tokens, m_norm_gain, m_w_in, m_w_gate_up, m_b_gate, m_ret_norm_gain, m_gla_norm_gain, m_w_branch_ret, m_w_branch_gla, m_w_out, m_final_norm_gain, v_meta_tokens, v_norm_gain, v_w_in, v_w_gate_up, v_b_gate, v_ret_norm_gain, v_gla_norm_gain, v_w_branch_ret, v_w_branch_gla, v_w_out, v_final_norm_gain):
    given = dict(x=x, meta_tokens=meta_tokens, norm_gain=norm_gain, w_in=w_in, w_gate_up=w_gate_up, b_gate=b_gate, ret_norm_gain=ret_norm_gain, gla_norm_gain=gla_norm_gain, w_branch_ret=w_branch_ret, w_branch_gla=w_branch_gla, w_out=w_out, final_norm_gain=final_norm_gain, loss_target=loss_target, m_meta_tokens=m_meta_tokens, m_norm_gain=m_norm_gain, m_w_in=m_w_in, m_w_gate_up=m_w_gate_up, m_b_gate=m_b_gate, m_ret_norm_gain=m_ret_norm_gain, m_gla_norm_gain=m_gla_norm_gain, m_w_branch_ret=m_w_branch_ret, m_w_branch_gla=m_w_branch_gla, m_w_out=m_w_out, m_final_norm_gain=m_final_norm_gain, v_meta_tokens=v_meta_tokens, v_norm_gain=v_norm_gain, v_w_in=v_w_in, v_w_gate_up=v_w_gate_up, v_b_gate=v_b_gate, v_ret_norm_gain=v_ret_norm_gain, v_gla_norm_gain=v_gla_norm_gain, v_w_branch_ret=v_w_branch_ret, v_w_branch_gla=v_w_branch_gla, v_w_out=v_w_out, v_final_norm_gain=v_final_norm_gain)
    weights = {n: given[n] for n in TWIN_WEIGHTS}
    shared = {n: given[n] for n in SHARED_INPUTS}
    per_example = {n: given[n] for n in ['x']}
    grad_fn = _jax.value_and_grad(_loss, argnums=(0, 1))

    def one_microbatch(ex, loss_target):
        ex = dict(ex)
        diff = ex.pop(TWIN_DIFF_INPUT)
        return grad_fn(weights, diff, {**shared, **ex}, loss_target)

    if N_MICROBATCH == 1:
        loss, (grad_w, grad_x) = one_microbatch(per_example, given["loss_target"])
    else:
        def body(carry, xs):
            loss_sum, grad_sum = carry
            l_k, (gw_k, gx_k) = one_microbatch(xs[0], xs[1])
            with _jax.named_scope("update"):
                return (loss_sum + l_k, _jax.tree.map(_jnp.add, grad_sum, gw_k)), gx_k

        init = (_jnp.zeros((), _jnp.float32), _jax.tree.map(_jnp.zeros_like, weights))
        (loss, grad_w), grad_x = _jax.lax.scan(body, init, (per_example, given["loss_target"]))
    with _jax.named_scope("update"):
        delta_w, new_m, new_v = {}, {}, {}
        for n in TWIN_WEIGHTS:
            delta_w[n], new_m[n], new_v[n] = _adamw(weights[n], grad_w[n], given["m_" + n], given["v_" + n])
    return (loss, grad_x, *[grad_w[n] for n in TWIN_WEIGHTS], *[delta_w[n] for n in TWIN_WEIGHTS],
            *[new_m[n] for n in TWIN_WEIGHTS], *[new_v[n] for n in TWIN_WEIGHTS])
```

```python
import functools

import jax
import jax.numpy as jnp
from jax import lax
from jax.experimental import pallas as pl
from jax.experimental.pallas import tpu as pltpu

F32 = jnp.float32
BF16 = jnp.bfloat16

D_MODEL = 1024
N_META = 16
TILE = 256
PAD_ROWS = TILE - N_META
RET_HEADS = 4
RET_QK = 256
RET_V = 512
RET_W = RET_HEADS * RET_V
GLA_HEADS = 4
GLA_K = 128
GLA_V = 256
GLA_W = GLA_HEADS * GLA_V
GLA_RANK = 16
GLA_TAU = 16.0
GLA_CHUNK = 16
N_CHUNK = TILE // GLA_CHUNK
ROPE_BASE = 10000.0
EPS = 1e-6
LANES = 128
N_DEV = 8
SEG_NAMES = ("rq", "rk", "rv", "rg", "gq", "gk", "gv", "gg", "mr", "mg")
SEG_W = (1024, 1024, 2048, 2048, 512, 512, 1024, 1024, 1024, 1024)
SEG_OFF = tuple(sum(SEG_W[:i]) for i in range(len(SEG_W)))
AL_COLS = sum(SEG_W)
IN_COLS = AL_COLS + GLA_RANK
GLR_OFF = sum(SEG_W[:8])
VMEM_LIMIT = 58 * 1024 * 1024
ADAM_LR, ADAM_B1, ADAM_B2, ADAM_EPS, ADAM_WD, ADAM_STEP = 0.001, 0.9, 0.999, 1e-08, 0.01, 10
ANY = pl.BlockSpec(memory_space=pl.ANY)
MESH = pl.DeviceIdType.MESH


def _call(body, name, **kw):
    return pl.pallas_call(body, name=name, **kw)


def _params(sem=None):
    return pltpu.CompilerParams(dimension_semantics=sem, vmem_limit_bytes=VMEM_LIMIT)


def _mm(a, b):
    return jnp.dot(a, b, preferred_element_type=F32)


def _mm_nt(a, b):
    return lax.dot_general(a, b, (((1,), (1,)), ((), ())), preferred_element_type=F32)


def _mm_tn(a, b):
    return lax.dot_general(a, b, (((0,), (0,)), ((), ())), preferred_element_type=F32)


def _mm_hi(a, b):
    return jnp.dot(a, b, preferred_element_type=F32, precision=lax.Precision.HIGHEST)


def _sigmoid(x):
    return 1.0 / (1.0 + jnp.exp(-x))


def _rope(t, cos, sin):
    half = t.shape[-1] // 2
    t1, t2 = t[:, :half], t[:, half:]
    return jnp.concatenate([t1 * cos - t2 * sin, t2 * cos + t1 * sin], axis=-1)


def _rope_bwd(g, cos, sin):
    half = g.shape[-1] // 2
    g1, g2 = g[:, :half], g[:, half:]
    return jnp.concatenate([g1 * cos + g2 * sin, g2 * cos - g1 * sin], axis=-1)


def _row_mean(x):
    return jnp.mean(x, axis=-1, keepdims=True)


def _col_sum(x):
    return jnp.sum(x, axis=0, keepdims=True)


def _inproj_fwd(h0, g_norm, w_al, w_glr):
    t_rows = h0.shape[0]
    nt = t_rows // TILE

    def body(h_ref, g_ref, w_hbm, wg_hbm, u_ref, ut_ref, *rest):
        seg_refs, glr_ref = rest[:10], rest[10]
        w_vm, wg_vm, sem = rest[11:]

        @pl.when(pl.program_id(0) == 0)
        def _():
            c1 = pltpu.make_async_copy(w_hbm, w_vm, sem.at[0])
            c2 = pltpu.make_async_copy(wg_hbm, wg_vm, sem.at[1])
            c1.start()
            c2.start()
            c1.wait()
            c2.wait()

        x = h_ref[...]
        r = lax.rsqrt(_row_mean(x * x) + EPS)
        u32 = (x * r * g_ref[...]).astype(BF16).astype(F32)
        u = u32.astype(BF16)
        u_ref[...] = u
        ut_ref[...] = u32.T.astype(BF16)
        for s, o_ref in enumerate(seg_refs):
            o_ref[...] = _mm(u, w_vm[:, SEG_OFF[s]:SEG_OFF[s] + SEG_W[s]]).astype(BF16)
        glr_ref[...] = _mm(u, wg_vm[...])

    row = lambda w: pl.BlockSpec((TILE, w), lambda i: (i, 0))
    out_shape = ([jax.ShapeDtypeStruct((t_rows, D_MODEL), BF16), jax.ShapeDtypeStruct((D_MODEL, t_rows), BF16)]
                 + [jax.ShapeDtypeStruct((t_rows, w), BF16) for w in SEG_W]
                 + [jax.ShapeDtypeStruct((t_rows, LANES), F32)])
    out_specs = [row(D_MODEL), pl.BlockSpec((D_MODEL, TILE), lambda i: (0, i))] + [row(w) for w in SEG_W] + [row(LANES)]
    outs = _call(
        body, "inproj_fwd", grid=(nt,), out_shape=out_shape,
        in_specs=[row(D_MODEL), pl.BlockSpec((1, D_MODEL), lambda i: (0, 0)), ANY, ANY],
        out_specs=out_specs,
        scratch_shapes=[pltpu.VMEM((D_MODEL, AL_COLS), BF16), pltpu.VMEM((D_MODEL, LANES), BF16), pltpu.SemaphoreType.DMA((2,))],
        compiler_params=_params(("arbitrary",)),
    )(h0, g_norm, w_al, w_glr)
    return outs[0], outs[1], dict(zip(SEG_NAMES, outs[2:12])), outs[12]


def _ret_decay(lgh):
    i = lax.broadcasted_iota(jnp.int32, (TILE, TILE), 0)
    j = lax.broadcasted_iota(jnp.int32, (TILE, TILE), 1)
    rel = (i - j).astype(F32)
    return jnp.where(rel >= 0, jnp.exp(jnp.maximum(rel, 0.0) * lgh), 0.0)


def _ret_vectors(lgh):
    idx = lax.broadcasted_iota(jnp.int32, (TILE, 1), 0).astype(F32)
    xi = jnp.exp((idx + 1.0) * lgh)
    zeta = jnp.exp((TILE - 1.0 - idx) * lgh)
    gc = jnp.exp(jnp.full((1, 1), float(TILE), F32) * lgh)
    return xi, zeta, gc


def _ret_fwd(seg, cos, sin, gain, lg):
    t_rows = cos.shape[0]
    nt = t_rows // TILE

    def body(lg_ref, q_ref, k_ref, v_ref, g_ref, cos_ref, sin_ref, gain_ref, oraw_ref, oret_ref, st_ref, s_acc, dm):
        h, t = pl.program_id(0), pl.program_id(1)
        lgh = lg_ref[h]

        @pl.when(t == 0)
        def _():
            s_acc[...] = jnp.zeros_like(s_acc)
            dm[...] = _ret_decay(lgh)

        cos_t, sin_t = cos_ref[...], sin_ref[...]
        q = _rope(q_ref[...].astype(F32), cos_t, sin_t)
        k = _rope(k_ref[...].astype(F32), cos_t, sin_t) * (RET_QK ** -0.5)
        xi, zeta, gc = _ret_vectors(lgh)
        v = v_ref[...]
        s_in = s_acc[...]
        p = (_mm_nt(q.astype(BF16), k.astype(BF16)) * dm[...]).astype(BF16)
        o = _mm(p, v) + _mm((q * xi).astype(BF16), s_in.astype(BF16))
        st_ref[...] = s_in.astype(BF16)
        s_acc[...] = s_in * gc + _mm_tn((k * zeta).astype(BF16), v)
        oraw_ref[...] = o
        oc = o - _row_mean(o)
        n = oc * lax.rsqrt(_row_mean(oc * oc) + EPS) * gain_ref[...]
        g = g_ref[...].astype(F32)
        oret_ref[...] = (n * g * _sigmoid(g)).astype(BF16)

    blk = lambda w: pl.BlockSpec((TILE, w), lambda h, t: (t, h))
    tab = pl.BlockSpec((TILE, LANES), lambda h, t: (t, 0))
    return _call(
        body, "ret_fwd", grid=(RET_HEADS, nt),
        out_shape=[jax.ShapeDtypeStruct((t_rows, RET_W), F32), jax.ShapeDtypeStruct((t_rows, RET_W), BF16),
                   jax.ShapeDtypeStruct((RET_HEADS, nt, RET_QK, RET_V), BF16)],
        in_specs=[pl.BlockSpec(memory_space=pltpu.SMEM), blk(RET_QK), blk(RET_QK), blk(RET_V), blk(RET_V), tab, tab,
                  pl.BlockSpec((1, RET_V), lambda h, t: (0, h))],
        out_specs=[blk(RET_V), blk(RET_V), pl.BlockSpec((None, None, RET_QK, RET_V), lambda h, t: (h, t, 0, 0))],
        scratch_shapes=[pltpu.VMEM((RET_QK, RET_V), F32), pltpu.VMEM((TILE, TILE), F32)],
        compiler_params=_params(("arbitrary", "arbitrary")),
    )(lg, seg["rq"], seg["rk"], seg["rv"], seg["rg"], cos, sin, gain)


def _ret_bwd(seg, cos, sin, gain, lg, o_raw, do_ret, states):
    t_rows = cos.shape[0]
    nt = t_rows // TILE

    def body(lg_ref, q_ref, k_ref, v_ref, g_ref, cos_ref, sin_ref, gain_ref, oraw_ref, do_ref, st_ref,
             dq_ref, dk_ref, dv_ref, dg_ref, dgain_ref, e_acc, dm):
        h, j = pl.program_id(0), pl.program_id(1)
        lgh = lg_ref[h]

        @pl.when(j == 0)
        def _():
            e_acc[...] = jnp.zeros_like(e_acc)
            dm[...] = _ret_decay(lgh)
            dgain_ref[...] = jnp.zeros_like(dgain_ref)

        cos_t, sin_t = cos_ref[...], sin_ref[...]
        q = _rope(q_ref[...].astype(F32), cos_t, sin_t)
        k = _rope(k_ref[...].astype(F32), cos_t, sin_t) * (RET_QK ** -0.5)
        xi, zeta, gc = _ret_vectors(lgh)
        v = v_ref[...]
        g = g_ref[...].astype(F32)
        o = oraw_ref[...]
        do = do_ref[...].astype(F32)
        oc = o - _row_mean(o)
        rstd = lax.rsqrt(_row_mean(oc * oc) + EPS)
        xh = oc * rstd
        gain_t = gain_ref[...]
        sg = _sigmoid(g)
        dn = do * (g * sg)
        dg_ref[...] = (do * (xh * gain_t) * (sg * (1.0 + g * (1.0 - sg)))).astype(BF16)
        dgain_ref[...] += _col_sum(dn * xh)
        dxh = dn * gain_t
        dob = (rstd * (dxh - _row_mean(dxh) - xh * _row_mean(dxh * xh))).astype(BF16)
        dmat = dm[...]
        qb, kb = q.astype(BF16), k.astype(BF16)
        p = (_mm_nt(qb, kb) * dmat).astype(BF16)
        dp = (_mm_nt(dob, v) * dmat).astype(BF16)
        s_in = st_ref[...]
        e_in = e_acc[...]
        e_b = e_in.astype(BF16)
        dq = _mm(dp, kb) + _mm_nt(dob, s_in) * xi
        dk = _mm_tn(dp, qb) + _mm_nt(v, e_b) * zeta
        dv_ref[...] = (_mm_tn(p, dob) + _mm((k * zeta).astype(BF16), e_b)).astype(BF16)
        e_acc[...] = e_in * gc + _mm_tn((q * xi).astype(BF16), dob)
        dq_ref[...] = _rope_bwd(dq, cos_t, sin_t).astype(BF16)
        dk_ref[...] = (_rope_bwd(dk, cos_t, sin_t) * (RET_QK ** -0.5)).astype(BF16)

    blk = lambda w: pl.BlockSpec((TILE, w), lambda h, j: (nt - 1 - j, h))
    tab = pl.BlockSpec((TILE, LANES), lambda h, j: (nt - 1 - j, 0))
    vec = pl.BlockSpec((1, RET_V), lambda h, j: (0, h))
    return _call(
        body, "ret_bwd", grid=(RET_HEADS, nt),
        out_shape=[jax.ShapeDtypeStruct((t_rows, RET_HEADS * RET_QK), BF16), jax.ShapeDtypeStruct((t_rows, RET_HEADS * RET_QK), BF16),
                   jax.ShapeDtypeStruct((t_rows, RET_W), BF16), jax.ShapeDtypeStruct((t_rows, RET_W), BF16),
                   jax.ShapeDtypeStruct((1, RET_W), F32)],
        in_specs=[pl.BlockSpec(memory_space=pltpu.SMEM), blk(RET_QK), blk(RET_QK), blk(RET_V), blk(RET_V), tab, tab, vec,
                  blk(RET_V), blk(RET_V), pl.BlockSpec((None, None, RET_QK, RET_V), lambda h, j: (h, nt - 1 - j, 0, 0))],
        out_specs=[blk(RET_QK), blk(RET_QK), blk(RET_V), blk(RET_V), vec],
        scratch_shapes=[pltpu.VMEM((RET_QK, RET_V), F32), pltpu.VMEM((TILE, TILE), F32)],
        compiler_params=_params(("arbitrary", "arbitrary")),
    )(lg, seg["rq"], seg["rk"], seg["rv"], seg["rg"], cos, sin, gain, o_raw, do_ret, states)


def _gla_masks():
    i = lax.broadcasted_iota(jnp.int32, (TILE, TILE), 0)
    j = lax.broadcasted_iota(jnp.int32, (TILE, TILE), 1)
    same = (i // GLA_CHUNK) == (j // GLA_CHUNK)
    return same & (j <= i), same & (j >= i), same


def _gla_expand_mask():
    r = lax.broadcasted_iota(jnp.int32, (TILE, N_CHUNK * GLA_K), 0)
    c = lax.broadcasted_iota(jnp.int32, (TILE, N_CHUNK * GLA_K), 1)
    return (r // GLA_CHUNK) == (c // GLA_K)


def _gla_expand(x_bf16, mask):
    return jnp.where(mask, jnp.tile(x_bf16, (1, N_CHUNK)), jnp.zeros((), BF16))


def _gla_own_chunk(x):
    rc = lax.broadcasted_iota(jnp.int32, (TILE, GLA_K), 0) // GLA_CHUNK
    out = jnp.zeros((TILE, GLA_K), F32)
    for c in range(N_CHUNK):
        out = out + jnp.where(rc == c, x[:, c * GLA_K:(c + 1) * GLA_K], 0.0)
    return out


def _gla_gates(glr_ref, wgu_ref, b_ref, tril, same):
    z = _mm_hi(glr_ref[...], wgu_ref[...]) + b_ref[...]
    la = (jnp.minimum(z, 0.0) - jnp.log(1.0 + jnp.exp(-jnp.abs(z)))) / GLA_TAU
    b = _mm_hi(tril.astype(F32), la)
    bl = _mm_hi(same.astype(F32), la)
    return z, b, bl


def _gla_fwd(seg, glr, wgu_pad, b_gate, gain):
    t_rows = glr.shape[0]
    nt = t_rows // TILE

    def body(q_ref, k_ref, v_ref, g_ref, glr_ref, wgu_ref, b_ref, gain_ref, oraw_ref, ogla_ref, st_ref, s_acc, sst):
        t = pl.program_id(1)

        @pl.when(t == 0)
        def _():
            s_acc[...] = jnp.zeros_like(s_acc)

        tril, _, same = _gla_masks()
        emask = _gla_expand_mask()
        _, b, bl = _gla_gates(glr_ref, wgu_ref, b_ref, tril, same)
        q = q_ref[...].astype(F32) * (GLA_K ** -0.5)
        k = k_ref[...].astype(F32)
        v = v_ref[...]
        qd = (q * jnp.exp(b)).astype(BF16)
        ki = (k * jnp.exp(-b)).astype(BF16)
        ke = (k * jnp.exp(bl - b)).astype(BF16)
        al = jnp.exp(bl)
        p = jnp.where(tril, _mm_nt(qd, ki), 0.0).astype(BF16)
        o = _mm(p, v)
        vt = v.astype(F32).T.astype(BF16)
        ut = _mm(vt, _gla_expand(ke, emask))
        st = s_acc[...]
        st_ref[...] = st
        for c in range(N_CHUNK):
            sst[:, c * GLA_K:(c + 1) * GLA_K] = st.astype(BF16)
            st = st * al[c * GLA_CHUNK:c * GLA_CHUNK + 1, :] + ut[:, c * GLA_K:(c + 1) * GLA_K]
        s_acc[...] = st
        o = o + _mm_nt(_gla_expand(qd, emask), sst[...])
        oraw_ref[...] = o
        n = o * lax.rsqrt(_row_mean(o * o) + EPS) * gain_ref[...]
        g = g_ref[...].astype(F32)
        ogla_ref[...] = (n * g * _sigmoid(g)).astype(BF16)

    blk = lambda w: pl.BlockSpec((TILE, w), lambda h, t: (t, h))
    return _call(
        body, "gla_fwd", grid=(GLA_HEADS, nt),
        out_shape=[jax.ShapeDtypeStruct((t_rows, GLA_W), F32), jax.ShapeDtypeStruct((t_rows, GLA_W), BF16),
                   jax.ShapeDtypeStruct((GLA_HEADS, nt, GLA_V, GLA_K), F32)],
        in_specs=[blk(GLA_K), blk(GLA_K), blk(GLA_V), blk(GLA_V), pl.BlockSpec((TILE, LANES), lambda h, t: (t, 0)),
                  pl.BlockSpec((LANES, GLA_K), lambda h, t: (0, h)), pl.BlockSpec((1, GLA_K), lambda h, t: (0, h)),
                  pl.BlockSpec((1, GLA_V), lambda h, t: (0, h))],
        out_specs=[blk(GLA_V), blk(GLA_V), pl.BlockSpec((None, None, GLA_V, GLA_K), lambda h, t: (h, t, 0, 0))],
        scratch_shapes=[pltpu.VMEM((GLA_V, GLA_K), F32), pltpu.VMEM((GLA_V, N_CHUNK * GLA_K), BF16)],
        compiler_params=_params(("arbitrary", "arbitrary")),
    )(seg["gq"], seg["gk"], seg["gv"], seg["gg"], glr, wgu_pad, b_gate, gain)


def _gla_bwd(seg, glr, wgu_pad, b_gate, gain, o_raw, do_gla, states):
    t_rows = glr.shape[0]
    nt = t_rows // TILE

    def body(q_ref, k_ref, v_ref, g_ref, glr_ref, wgu_ref, b_ref, gain_ref, oraw_ref, do_ref, st_ref,
             dq_ref, dk_ref, dv_ref, dg_ref, dglr_ref, dwgu_ref, dbg_ref, dgain_ref, d_acc, sst, sst32, dst, dalf):
        j = pl.program_id(1)

        @pl.when(j == 0)
        def _():
            d_acc[...] = jnp.zeros_like(d_acc)
            dwgu_ref[...] = jnp.zeros_like(dwgu_ref)
            dbg_ref[...] = jnp.zeros_like(dbg_ref)
            dgain_ref[...] = jnp.zeros_like(dgain_ref)

        tril, triu, same = _gla_masks()
        emask = _gla_expand_mask()
        z, b, bl = _gla_gates(glr_ref, wgu_ref, b_ref, tril, same)
        q = q_ref[...].astype(F32) * (GLA_K ** -0.5)
        k = k_ref[...].astype(F32)
        v = v_ref[...]
        eb, enb, eke, al = jnp.exp(b), jnp.exp(-b), jnp.exp(bl - b), jnp.exp(bl)
        qd, ki, ke = q * eb, k * enb, k * eke
        qdb, kib, keb = qd.astype(BF16), ki.astype(BF16), ke.astype(BF16)
        o = oraw_ref[...]
        do = do_ref[...].astype(F32)
        g = g_ref[...].astype(F32)
        rinv = lax.rsqrt(_row_mean(o * o) + EPS)
        nh = o * rinv
        gain_t = gain_ref[...]
        sg = _sigmoid(g)
        dn = do * (g * sg)
        dg_ref[...] = (do * (nh * gain_t) * (sg * (1.0 + g * (1.0 - sg)))).astype(BF16)
        dgain_ref[...] += _col_sum(dn * nh)
        dnh = dn * gain_t
        dor = rinv * (dnh - nh * _row_mean(dnh * nh))
        dob = dor.astype(BF16)
        p = jnp.where(tril, _mm_nt(qdb, kib), 0.0).astype(BF16)
        dp = jnp.where(tril, _mm_nt(dob, v), 0.0).astype(BF16)
        dqd = _mm(dp, kib)
        dki = _mm_tn(dp, qdb)
        dv = _mm_tn(p, dob)
        vt = v.astype(F32).T.astype(BF16)
        ke_exp = _gla_expand(keb, emask)
        ut = _mm(vt, ke_exp)
        st = st_ref[...]
        for c in range(N_CHUNK):
            cols = slice(c * GLA_K, (c + 1) * GLA_K)
            sst[:, cols] = st.astype(BF16)
            sst32[:, cols] = st
            st = st * al[c * GLA_CHUNK:c * GLA_CHUNK + 1, :] + ut[:, cols]
        dqd = dqd + _gla_own_chunk(_mm(dob, sst[...]))
        gt = _mm(dor.T.astype(BF16), _gla_expand(qdb, emask))
        dt = d_acc[...]
        for c in reversed(range(N_CHUNK)):
            cols = slice(c * GLA_K, (c + 1) * GLA_K)
            dst[:, cols] = dt.astype(BF16)
            dalf[c * GLA_CHUNK:(c + 1) * GLA_CHUNK, :] = jnp.broadcast_to(_col_sum(dt * sst32[:, cols]), (GLA_CHUNK, GLA_K))
            dt = gt[:, cols] + dt * al[c * GLA_CHUNK:c * GLA_CHUNK + 1, :]
        d_acc[...] = dt
        dke = _gla_own_chunk(_mm(v, dst[...]))
        dv = dv + _mm_nt(ke_exp, dst[...])
        dv_ref[...] = dv.astype(BF16)
        dke_ke = dke * ke
        db = dqd * qd - dki * ki - dke_ke
        dbl = al * dalf[...] + _mm_hi(same.astype(F32), dke_ke)
        dla = _mm_hi(triu.astype(F32), db) + dbl
        dq_ref[...] = (dqd * eb * (GLA_K ** -0.5)).astype(BF16)
        dk_ref[...] = (dki * enb + dke * eke).astype(BF16)
        dz = dla * (1.0 / GLA_TAU) * _sigmoid(-z)
        glr_t = glr_ref[...]
        dglr_ref[...] = lax.dot_general(dz, wgu_ref[...], (((1,), (1,)), ((), ())), preferred_element_type=F32,
                                        precision=lax.Precision.HIGHEST)
        dwgu_ref[...] += _mm_hi(glr_t.T, dz)
        dbg_ref[...] += _col_sum(dz)

    blk = lambda w: pl.BlockSpec((TILE, w), lambda h, j: (nt - 1 - j, h))
    vec = lambda w: pl.BlockSpec((1, w), lambda h, j: (0, h))
    wspec = pl.BlockSpec((LANES, GLA_K), lambda h, j: (0, h))
    return _call(
        body, "gla_bwd", grid=(GLA_HEADS, nt),
        out_shape=[jax.ShapeDtypeStruct((t_rows, GLA_HEADS * GLA_K), BF16), jax.ShapeDtypeStruct((t_rows, GLA_HEADS * GLA_K), BF16),
                   jax.ShapeDtypeStruct((t_rows, GLA_W), BF16), jax.ShapeDtypeStruct((t_rows, GLA_W), BF16),
                   jax.ShapeDtypeStruct((GLA_HEADS, t_rows, LANES), F32), jax.ShapeDtypeStruct((LANES, GLA_HEADS * GLA_K), F32),
                   jax.ShapeDtypeStruct((1, GLA_HEADS * GLA_K), F32), jax.ShapeDtypeStruct((1, GLA_W), F32)],
        in_specs=[blk(GLA_K), blk(GLA_K), blk(GLA_V), blk(GLA_V), pl.BlockSpec((TILE, LANES), lambda h, j: (nt - 1 - j, 0)),
                  wspec, vec(GLA_K), vec(GLA_V), blk(GLA_V), blk(GLA_V),
                  pl.BlockSpec((None, None, GLA_V, GLA_K), lambda h, j: (h, nt - 1 - j, 0, 0))],
        out_specs=[blk(GLA_K), blk(GLA_K), blk(GLA_V), blk(GLA_V),
                   pl.BlockSpec((None, TILE, LANES), lambda h, j: (h, nt - 1 - j, 0)), wspec, vec(GLA_K), vec(GLA_V)],
        scratch_shapes=[pltpu.VMEM((GLA_V, GLA_K), F32), pltpu.VMEM((GLA_V, N_CHUNK * GLA_K), BF16),
                        pltpu.VMEM((GLA_V, N_CHUNK * GLA_K), F32), pltpu.VMEM((GLA_V, N_CHUNK * GLA_K), BF16),
                        pltpu.VMEM((TILE, GLA_K), F32)],
        compiler_params=_params(("arbitrary", "arbitrary")),
    )(seg["gq"], seg["gk"], seg["gv"], seg["gg"], glr, wgu_pad, b_gate, gain, o_raw, do_gla, states)


def _merge_fwd_bwd(o_ret, o_gla, seg, h0, target, g_final, w_br, w_bg, w_out):
    t_rows = h0.shape[0]
    nt = t_rows // TILE

    def body(oret_ref, ogla_ref, mr_ref, mg_ref, h0_ref, tgt_ref, gf_ref, wbr_hbm, wbg_hbm, wout_hbm,
             dh1_ref, dmr_ref, dmg_ref, doret_ref, dogla_ref, loss_ref, dgf_ref, dwbr_hbm, dwbg_hbm, dwout_hbm,
             wbr, wbg, wout, abr, abg, aout, sem):
        i = pl.program_id(0)

        @pl.when(i == 0)
        def _():
            cps = [pltpu.make_async_copy(s, d, sem.at[n]) for n, (s, d) in enumerate(((wbr_hbm, wbr), (wbg_hbm, wbg), (wout_hbm, wout)))]
            for cp in cps:
                cp.start()
            abr[...] = jnp.zeros_like(abr)
            abg[...] = jnp.zeros_like(abg)
            aout[...] = jnp.zeros_like(aout)
            loss_ref[...] = jnp.zeros_like(loss_ref)
            dgf_ref[...] = jnp.zeros_like(dgf_ref)
            for cp in cps:
                cp.wait()
            dh1_ref[...] = jnp.zeros_like(dh1_ref)
            dmr_ref[...] = jnp.zeros_like(dmr_ref)
            dmg_ref[...] = jnp.zeros_like(dmg_ref)
            doret_ref[...] = jnp.zeros_like(doret_ref)
            dogla_ref[...] = jnp.zeros_like(dogla_ref)

        @pl.when(i > 0)
        def _():
            oret, ogla = oret_ref[...], ogla_ref[...]
            br, bg = _mm(oret, wbr[...]), _mm(ogla, wbg[...])
            sr, sg = _sigmoid(mr_ref[...].astype(F32)), _sigmoid(mg_ref[...].astype(F32))
            mb = (sr * br + sg * bg).astype(BF16)
            h1 = h0_ref[...] + _mm(mb, wout[...])
            r2 = lax.rsqrt(_row_mean(h1 * h1) + EPS)
            hn = h1 * r2
            gf = gf_ref[...]
            diff = hn * gf - tgt_ref[...]
            loss_ref[...] += 0.5 * jnp.sum(_row_mean(diff * diff))
            dy = diff * (1.0 / D_MODEL)
            dgf_ref[...] += _col_sum(dy * hn)
            dyg = dy * gf
            dh1 = r2 * (dyg - hn * _row_mean(dyg * hn))
            dh1_ref[...] = dh1
            dh1b = dh1.astype(BF16)
            dm = _mm_nt(dh1b, wout[...])
            aout[...] += _mm_tn(mb, dh1b)
            dbr = (dm * sr).astype(BF16)
            dbg = (dm * sg).astype(BF16)
            dmr_ref[...] = (dm * br * sr * (1.0 - sr)).astype(BF16)
            dmg_ref[...] = (dm * bg * sg * (1.0 - sg)).astype(BF16)
            doret_ref[...] = _mm_nt(dbr, wbr[...]).astype(BF16)
            dogla_ref[...] = _mm_nt(dbg, wbg[...]).astype(BF16)
            abr[...] += _mm_tn(oret, dbr)
            abg[...] += _mm_tn(ogla, dbg)

        @pl.when(i == nt - 1)
        def _():
            pltpu.sync_copy(abr, dwbr_hbm)
            pltpu.sync_copy(abg, dwbg_hbm)
            pltpu.sync_copy(aout, dwout_hbm)

    row = lambda w: pl.BlockSpec((TILE, w), lambda i: (i, 0))
    one = lambda w: pl.BlockSpec((1, w), lambda i: (0, 0))
    return _call(
        body, "merge_fwd_bwd", grid=(nt,),
        out_shape=[jax.ShapeDtypeStruct((t_rows, D_MODEL), F32), jax.ShapeDtypeStruct((t_rows, D_MODEL), BF16),
                   jax.ShapeDtypeStruct((t_rows, D_MODEL), BF16), jax.ShapeDtypeStruct((t_rows, RET_W), BF16),
                   jax.ShapeDtypeStruct((t_rows, GLA_W), BF16), jax.ShapeDtypeStruct((1, LANES), F32),
                   jax.ShapeDtypeStruct((1, D_MODEL), F32), jax.ShapeDtypeStruct((RET_W, D_MODEL), F32),
                   jax.ShapeDtypeStruct((GLA_W, D_MODEL), F32), jax.ShapeDtypeStruct((D_MODEL, D_MODEL), F32)],
        in_specs=[row(RET_W), row(GLA_W), row(D_MODEL), row(D_MODEL), row(D_MODEL),
                  pl.BlockSpec((TILE, D_MODEL), lambda i: (jnp.maximum(i - 1, 0), 0)), one(D_MODEL), ANY, ANY, ANY],
        out_specs=[row(D_MODEL), row(D_MODEL), row(D_MODEL), row(RET_W), row(GLA_W), one(LANES), one(D_MODEL), ANY, ANY, ANY],
        scratch_shapes=[pltpu.VMEM((RET_W, D_MODEL), BF16), pltpu.VMEM((GLA_W, D_MODEL), BF16), pltpu.VMEM((D_MODEL, D_MODEL), BF16),
                        pltpu.VMEM((RET_W, D_MODEL), F32), pltpu.VMEM((GLA_W, D_MODEL), F32), pltpu.VMEM((D_MODEL, D_MODEL), F32),
                        pltpu.SemaphoreType.DMA((3,))],
        compiler_params=_params(("arbitrary",)),
    )(o_ret, o_gla, seg["mr"], seg["mg"], h0, target, g_final, w_br, w_bg, w_out)


def _inproj_bwd_x(dseg, dglr_parts, h0, dh1, g_norm, w_al, w_glr):
    t_rows = h0.shape[0]
    nt = t_rows // TILE

    def body(*refs):
        d_refs = refs[:10]
        dglr_ref, h_ref, dh1_ref, g_ref, w_hbm, wg_hbm, dh0_ref, dgn_ref, dglr_out, w_vm, wg_vm, sem = refs[10:]

        @pl.when(pl.program_id(0) == 0)
        def _():
            c1 = pltpu.make_async_copy(w_hbm, w_vm, sem.at[0])
            c2 = pltpu.make_async_copy(wg_hbm, wg_vm, sem.at[1])
            c1.start()
            c2.start()
            dgn_ref[...] = jnp.zeros_like(dgn_ref)
            c1.wait()
            c2.wait()

        dglr = (dglr_ref[0] + dglr_ref[1] + dglr_ref[2] + dglr_ref[3]).astype(BF16)
        dglr_out[...] = dglr
        du = _mm_nt(dglr, wg_vm[...])
        for s, d_ref in enumerate(d_refs):
            du = du + _mm_nt(d_ref[...], w_vm[:, SEG_OFF[s]:SEG_OFF[s] + SEG_W[s]])
        x = h_ref[...]
        r = lax.rsqrt(_row_mean(x * x) + EPS)
        hn = x * r
        dgn_ref[...] += _col_sum(du * hn)
        dug = du * g_ref[...]
        dh0_ref[...] = dh1_ref[...] + r * (dug - hn * _row_mean(dug * hn))

    row = lambda w: pl.BlockSpec((TILE, w), lambda i: (i, 0))
    one = pl.BlockSpec((1, D_MODEL), lambda i: (0, 0))
    return _call(
        body, "inproj_bwd_x", grid=(nt,),
        out_shape=[jax.ShapeDtypeStruct((t_rows, D_MODEL), F32), jax.ShapeDtypeStruct((1, D_MODEL), F32),
                   jax.ShapeDtypeStruct((t_rows, LANES), BF16)],
        in_specs=[row(w) for w in SEG_W] + [pl.BlockSpec((GLA_HEADS, TILE, LANES), lambda i: (0, i, 0)),
                                            row(D_MODEL), row(D_MODEL), one, ANY, ANY],
        out_specs=[row(D_MODEL), one, row(LANES)],
        scratch_shapes=[pltpu.VMEM((D_MODEL, AL_COLS), BF16), pltpu.VMEM((D_MODEL, LANES), BF16), pltpu.SemaphoreType.DMA((2,))],
        compiler_params=_params(("arbitrary",)),
    )(*[dseg[n] for n in SEG_NAMES], dglr_parts, h0, dh1, g_norm, w_al, w_glr)


def _inproj_bwd_w(ut, d, name):
    t_rows, width = d.shape
    nt = t_rows // TILE
    rk = 3 * TILE if nt % 3 == 0 else TILE
    tn = min(width, 512)

    def body(ut_ref, d_ref, o_ref):
        @pl.when(pl.program_id(1) == 0)
        def _():
            o_ref[...] = jnp.zeros_like(o_ref)

        o_ref[...] += _mm(ut_ref[...], d_ref[...])

    return _call(
        body, name, grid=(width // tn, t_rows // rk),
        out_shape=jax.ShapeDtypeStruct((D_MODEL, width), F32),
        in_specs=[pl.BlockSpec((D_MODEL, rk), lambda j, k: (0, k)), pl.BlockSpec((rk, tn), lambda j, k: (k, j))],
        out_specs=pl.BlockSpec((D_MODEL, tn), lambda j, k: (0, j)),
        compiler_params=_params(("arbitrary", "arbitrary")),
    )(ut, d)


def _position():
    x, y, c = lax.axis_index("x"), lax.axis_index("y"), lax.axis_index("c")
    return x, y, c


def _index(px, py, pc):
    return 4 * px + 2 * py + pc


def _all_gather(arrs):
    n = len(arrs)

    def body(*refs):
        ins, outs = refs[:n], refs[n:2 * n]
        send_sems, recv_sems, local_sems = refs[2 * n:]
        x, y, c = _position()
        me, sibling = (x, y, c), (x, y, 1 - c)
        chips = [(1 - x, y), (x, 1 - y), (1 - x, 1 - y)]

        def copy(a, k, block, to, src=None):
            dst = outs[a].at[_index(*block)]
            return pltpu.make_async_remote_copy(src_ref=dst if src is None else src, dst_ref=dst,
                                                send_sem=send_sems.at[7 * a + k], recv_sem=recv_sems.at[7 * a + k],
                                                device_id=to, device_id_type=MESH)

        mine = [pltpu.make_async_copy(ins[a], outs[a].at[_index(*me)], local_sems.at[a]) for a in range(n)]
        for cp in mine:
            cp.start()
        first = []
        for a in range(n):
            first.append(copy(a, 0, me, sibling, src=ins[a]))
            first += [copy(a, 1 + j, me, (*chip, c), src=ins[a]) for j, chip in enumerate(chips)]
        for cp in first:
            cp.start()
        passed = []
        for j, chip in enumerate(chips):
            for a in range(n):
                copy(a, 1 + j, (*chip, c), me).wait_recv()
                cp = copy(a, 4 + j, (*chip, c), sibling)
                cp.start()
                passed.append(cp)
        for a in range(n):
            copy(a, 0, sibling, me).wait_recv()
            for j, chip in enumerate(chips):
                copy(a, 4 + j, (*chip, 1 - c), me).wait_recv()
        for cp in first + passed:
            cp.wait_send()
        for cp in mine:
            cp.wait()

    return _call(
        body, "all_gather_shards",
        out_shape=[jax.ShapeDtypeStruct((N_DEV, *a.shape), a.dtype) for a in arrs],
        in_specs=[ANY] * n, out_specs=[ANY] * n,
        scratch_shapes=[pltpu.SemaphoreType.DMA((7 * n,)), pltpu.SemaphoreType.DMA((7 * n,)), pltpu.SemaphoreType.DMA((n,))],
    )(*arrs)


def _exchange(blocks, shared):
    arrs = list(blocks) + list(shared)
    n, nb = len(arrs), len(blocks)

    def body(*refs):
        ins, outs = refs[:n], refs[n:2 * n]
        send_sems, recv_sems, local_sems = refs[2 * n:]
        x, y, c = _position()
        me = _index(x, y, c)

        def src_of(a, dev):
            return ins[a].at[dev] if a < nb else ins[a]

        mine = [pltpu.make_async_copy(src_of(a, me), outs[a].at[me], local_sems.at[a]) for a in range(n)]
        for cp in mine:
            cp.start()
        sends, peers = [], []
        for m in range(1, N_DEV):
            px = 1 - x if m & 4 else x
            py = 1 - y if m & 2 else y
            pc = 1 - c if m & 1 else c
            peers.append((m, (px, py, pc)))
        for m, peer in peers:
            for a in range(n):
                cp = pltpu.make_async_remote_copy(src_ref=src_of(a, _index(*peer)), dst_ref=outs[a].at[me],
                                                  send_sem=send_sems.at[7 * a + m - 1], recv_sem=recv_sems.at[7 * a + m - 1],
                                                  device_id=peer, device_id_type=MESH)
                cp.start()
                sends.append(cp)
        for m, peer in peers:
            for a in range(n):
                pltpu.make_async_remote_copy(src_ref=src_of(a, me), dst_ref=outs[a].at[_index(*peer)],
                                             send_sem=send_sems.at[7 * a + m - 1], recv_sem=recv_sems.at[7 * a + m - 1],
                                             device_id=peer, device_id_type=MESH).wait_recv()
        for cp in sends:
            cp.wait_send()
        for cp in mine:
            cp.wait()

    return _call(
        body, "exchange_partials",
        out_shape=[jax.ShapeDtypeStruct(a.shape, a.dtype) for a in blocks]
                  + [jax.ShapeDtypeStruct((N_DEV, *a.shape), a.dtype) for a in shared],
        in_specs=[ANY] * n, out_specs=[ANY] * n,
        scratch_shapes=[pltpu.SemaphoreType.DMA((7 * n,)), pltpu.SemaphoreType.DMA((7 * n,)), pltpu.SemaphoreType.DMA((n,))],
    )(*arrs)


def _adamw(g, w, m, v):
    m_new = ADAM_B1 * m + (1.0 - ADAM_B1) * g
    v_new = ADAM_B2 * v + (1.0 - ADAM_B2) * (g * g)
    m_hat = m_new / (1.0 - ADAM_B1 ** ADAM_STEP)
    v_hat = v_new / (1.0 - ADAM_B2 ** ADAM_STEP)
    delta = -ADAM_LR * (m_hat / (jnp.sqrt(v_hat) + ADAM_EPS) + ADAM_WD * w)
    return delta, m_new, v_new


def _sum_partials(p_ref):
    g = p_ref[0].astype(F32)
    for d in range(1, N_DEV):
        g = g + p_ref[d].astype(F32)
    return g


def _reduce_adam(parts, w, m, v, name, block_rows, row_off=0):
    rows, cols = w.shape
    off = row_off // block_rows

    def body(p_ref, w_ref, m_ref, v_ref, g_ref, d_ref, mo_ref, vo_ref):
        g = _sum_partials(p_ref)
        g_ref[...] = g
        d_ref[...], mo_ref[...], vo_ref[...] = _adamw(g, w_ref[...], m_ref[...], v_ref[...])

    blk = pl.BlockSpec((block_rows, cols), lambda i: (i, 0))
    return _call(
        body, name, grid=(rows // block_rows,),
        out_shape=[jax.ShapeDtypeStruct((rows, cols), F32)] * 4,
        in_specs=[pl.BlockSpec((N_DEV, block_rows, cols), lambda i: (0, i + off, 0)), blk, blk, blk],
        out_specs=[blk] * 4,
        compiler_params=_params(("arbitrary",)),
    )(parts, w, m, v)


def _reduce_small(parts):
    def body(p_ref, o_ref):
        o_ref[...] = _sum_partials(p_ref)

    return _call(body, "reduce_small", out_shape=jax.ShapeDtypeStruct(parts.shape[1:], F32))(parts)


def _adam_small(g, w, m, v):
    def body(g_ref, w_ref, m_ref, v_ref, d_ref, mo_ref, vo_ref):
        d_ref[...], mo_ref[...], vo_ref[...] = _adamw(g_ref[...], w_ref[...], m_ref[...], v_ref[...])

    return _call(body, "adam_small", out_shape=[jax.ShapeDtypeStruct(g.shape, F32)] * 3)(g, w, m, v)


def _pack_rows(arrs):
    rows = []
    for a in arrs:
        flat = a.reshape(-1).astype(F32)
        pad = (-flat.shape[0]) % LANES
        rows.append(jnp.pad(flat, (0, pad)).reshape(-1, LANES))
    packed = jnp.concatenate(rows, axis=0)
    return jnp.pad(packed, ((0, (-packed.shape[0]) % 8), (0, 0)))


def _unpack_rows(packed, shapes):
    out, r = [], 0
    for shp in shapes:
        size = 1
        for s in shp:
            size *= s
        nrows = -(-size // LANES)
        out.append(packed[r:r + nrows].reshape(-1)[:size].reshape(shp))
        r += nrows
    return out


def kernel(x, meta_tokens, norm_gain, w_in, w_gate_up, b_gate, ret_norm_gain, gla_norm_gain, w_branch_ret, w_branch_gla, w_out, final_norm_gain, loss_target, m_meta_tokens, m_norm_gain, m_w_in, m_w_gate_up, m_b_gate, m_ret_norm_gain, m_gla_norm_gain, m_w_branch_ret, m_w_branch_gla, m_w_out, m_final_norm_gain, v_meta_tokens, v_norm_gain, v_w_in, v_w_gate_up, v_b_gate, v_ret_norm_gain, v_gla_norm_gain, v_w_branch_ret, v_w_branch_gla, v_w_out, v_final_norm_gain):
    xi, yi, ci = _position()
    me = _index(xi, yi, ci)
    seq = x.shape[1]
    t_rows = seq + TILE
    in_shard = w_in.shape[2]
    gu_shard = w_gate_up.shape[2]
    meta_shard = meta_tokens.shape[1]
    ret_rows, gla_rows, out_rows = w_branch_ret.shape[1], w_branch_gla.shape[1], w_out.shape[1]

    rows_local = jnp.concatenate([w_branch_ret[0], w_branch_gla[0], w_out[0]], axis=0).astype(BF16)
    small_local = jnp.concatenate([meta_tokens, jnp.pad(w_gate_up[0], ((0, 0), (0, LANES - gu_shard)))], axis=0)
    g_in, g_rows, g_small = _all_gather([w_in[0].astype(BF16), rows_local, small_local])
    w_full = jnp.transpose(g_in, (1, 0, 2)).reshape(D_MODEL, IN_COLS)
    w_al = jnp.concatenate([w_full[:, :GLR_OFF], w_full[:, GLR_OFF + GLA_RANK:]], axis=1)
    w_glr = jnp.pad(w_full[:, GLR_OFF:GLR_OFF + GLA_RANK], ((0, 0), (0, LANES - GLA_RANK)))
    w_br = g_rows[:, :ret_rows].reshape(RET_W, D_MODEL)
    w_bg = g_rows[:, ret_rows:ret_rows + gla_rows].reshape(GLA_W, D_MODEL)
    w_o = g_rows[:, ret_rows + gla_rows:].reshape(D_MODEL, D_MODEL)
    meta_full = jnp.transpose(g_small[:, :N_META, :], (1, 0, 2)).reshape(N_META, D_MODEL)
    wgu_full = jnp.transpose(g_small[:, N_META:, :gu_shard], (1, 0, 2)).reshape(GLA_RANK, GLA_HEADS * GLA_K)
    wgu_pad = jnp.pad(wgu_full, ((0, LANES - GLA_RANK), (0, 0)))

    pos = jnp.arange(t_rows, dtype=F32) - float(PAD_ROWS)
    half = RET_QK // 2
    inv = ROPE_BASE ** (-jnp.arange(half, dtype=F32) / half)
    ang = pos[:, None] * inv[None, :]
    cos, sin = jnp.cos(ang), jnp.sin(ang)
    lg = jnp.log1p(-(2.0 ** (-5.0 - jnp.arange(RET_HEADS, dtype=F32))))

    h0 = jnp.concatenate([jnp.zeros((PAD_ROWS, D_MODEL), F32), meta_full, x[0]], axis=0)
    u, ut, seg, glr = _inproj_fwd(h0, norm_gain, w_al, w_glr)
    o_ret_raw, o_ret, ret_states = _ret_fwd(seg, cos, sin, ret_norm_gain, lg)
    o_gla_raw, o_gla, gla_states = _gla_fwd(seg, glr, wgu_pad, b_gate, gla_norm_gain)
    (dh1, d_mr, d_mg, do_ret, do_gla, loss_part, d_gfinal, dw_br, dw_bg, dw_o) = _merge_fwd_bwd(
        o_ret, o_gla, seg, h0, loss_target[0], final_norm_gain.reshape(1, D_MODEL), w_br, w_bg, w_o)

    d_rq, d_rk, d_rv, d_rg, d_gret = _ret_bwd(seg, cos, sin, ret_norm_gain, lg, o_ret_raw, do_ret, ret_states)
    d_gq, d_gk, d_gv, d_gg, dglr_parts, d_wgu, d_bgate, d_ggla = _gla_bwd(
        seg, glr, wgu_pad, b_gate, gla_norm_gain, o_gla_raw, do_gla, gla_states)
    dseg = dict(rq=d_rq, rk=d_rk, rv=d_rv, rg=d_rg, gq=d_gq, gk=d_gk, gv=d_gv, gg=d_gg, mr=d_mr, mg=d_mg)
    dh0, d_gnorm, dglr = _inproj_bwd_x(dseg, dglr_parts, h0, dh1, norm_gain, w_al, w_glr)
    dw_seg = [_inproj_bwd_w(ut, dseg[n], "inproj_bwd_w_" + n) for n in SEG_NAMES]
    dw_glr = _inproj_bwd_w(ut, dglr, "inproj_bwd_w_glr")

    dw_in_full = jnp.concatenate(dw_seg[:8] + [dw_glr[:, :GLA_RANK]] + dw_seg[8:], axis=1)
    send_in = jnp.transpose(dw_in_full.astype(BF16).reshape(D_MODEL, N_DEV, in_shard), (1, 0, 2))
    send_rows = jnp.concatenate([dw_br.reshape(N_DEV, ret_rows, D_MODEL), dw_bg.reshape(N_DEV, gla_rows, D_MODEL),
                                 dw_o.reshape(N_DEV, out_rows, D_MODEL)], axis=1).astype(BF16)
    small_shapes = [(N_META, D_MODEL), (1, D_MODEL), (GLA_RANK, GLA_HEADS * GLA_K), (1, GLA_HEADS * GLA_K),
                    (1, RET_W), (1, GLA_W), (1, D_MODEL)]
    small_part = _pack_rows([dh0[PAD_ROWS:TILE], d_gnorm, d_wgu[:GLA_RANK], d_bgate, d_gret, d_ggla, d_gfinal])
    p_in, p_rows, p_small = _exchange([send_in, send_rows], [small_part])

    g_w_in, d_w_in, nm_w_in, nv_w_in = _reduce_adam(p_in, w_in[0], m_w_in[0], v_w_in[0], "adam_w_in", LANES)
    rb = gla_rows
    g_w_br, d_w_br, nm_w_br, nv_w_br = _reduce_adam(p_rows, w_branch_ret[0], m_w_branch_ret[0], v_w_branch_ret[0], "adam_w_branch_ret", rb, 0)
    g_w_bg, d_w_bg, nm_w_bg, nv_w_bg = _reduce_adam(p_rows, w_branch_gla[0], m_w_branch_gla[0], v_w_branch_gla[0], "adam_w_branch_gla", rb, ret_rows)
    g_w_o, d_w_o, nm_w_o, nv_w_o = _reduce_adam(p_rows, w_out[0], m_w_out[0], v_w_out[0], "adam_w_out", rb, ret_rows + gla_rows)
    g_meta_f, g_gnorm, g_wgu_f, g_bgate, g_gret, g_ggla, g_gfinal = _unpack_rows(_reduce_small(p_small), small_shapes)
    g_meta = lax.dynamic_slice_in_dim(g_meta_f, me * meta_shard, meta_shard, axis=1)
    g_wgu = lax.dynamic_slice_in_dim(g_wgu_f, me * gu_shard, gu_shard, axis=1)
    s_g = [g_meta, g_gnorm, g_wgu, g_bgate, g_gret, g_ggla, g_gfinal]
    s_w = [meta_tokens, norm_gain, w_gate_up[0], b_gate, ret_norm_gain, gla_norm_gain, final_norm_gain]
    s_m = [m_meta_tokens, m_norm_gain, m_w_gate_up[0], m_b_gate, m_ret_norm_gain, m_gla_norm_gain, m_final_norm_gain]
    s_v = [v_meta_tokens, v_norm_gain, v_w_gate_up[0], v_b_gate, v_ret_norm_gain, v_gla_norm_gain, v_final_norm_gain]
    shapes = [a.shape for a in s_g]
    s_d, s_nm, s_nv = [_unpack_rows(p, shapes) for p in _adam_small(*[_pack_rows(l) for l in (s_g, s_w, s_m, s_v)])]

    loss = lax.psum(loss_part[0, 0], ("x", "y", "c"))
    grad_x = dh0[TILE:][None]

    def order(meta, gnorm, win, wgu, bgate, gret, ggla, wbr, wbg, wo, gfin):
        return (meta, gnorm, win[None], wgu[None], bgate, gret, ggla, wbr[None], wbg[None], wo[None], gfin.reshape(final_norm_gain.shape))

    def small(l):
        return dict(meta=l[0], gnorm=l[1], wgu=l[2], bgate=l[3], gret=l[4], ggla=l[5], gfin=l[6])

    grads = order(win=g_w_in, wbr=g_w_br, wbg=g_w_bg, wo=g_w_o, **small(s_g))
    deltas = order(win=d_w_in, wbr=d_w_br, wbg=d_w_bg, wo=d_w_o, **small(s_d))
    new_m = order(win=nm_w_in, wbr=nm_w_br, wbg=nm_w_bg, wo=nm_w_o, **small(s_nm))
    new_v = order(win=nv_w_in, wbr=nv_w_br, wbg=nv_w_bg, wo=nv_w_o, **small(s_nv))
    return (loss, grad_x, *grads, *deltas, *new_m, *new_v)
```

```python
import functools

import jax
import jax.numpy as jnp
from jax import lax
from jax.experimental import pallas as pl
from jax.experimental.pallas import tpu as pltpu

F32 = jnp.float32
BF16 = jnp.bfloat16

D_MODEL = 1024
N_META = 16
TILE = 256
PAD_ROWS = TILE - N_META
RET_HEADS = 4
RET_QK = 256
RET_V = 512
RET_W = RET_HEADS * RET_V
GLA_HEADS = 4
GLA_K = 128
GLA_V = 256
GLA_W = GLA_HEADS * GLA_V
GLA_RANK = 16
GLA_TAU = 16.0
GLA_CHUNK = 16
ROPE_BASE = 10000.0
EPS = 1e-6
LANES = 128
N_DEV = 8
SEG_NAMES = ("rq", "rk", "rv", "rg", "gq", "gk", "gv", "gg", "mr", "mg")
SEG_W = (1024, 1024, 2048, 2048, 512, 512, 1024, 1024, 1024, 1024)
SEG_OFF = tuple(sum(SEG_W[:i]) for i in range(len(SEG_W)))
AL_COLS = sum(SEG_W)
IN_COLS = AL_COLS + GLA_RANK
GLR_OFF = sum(SEG_W[:8])
VMEM_LIMIT = 58 * 1024 * 1024
ADAM_LR, ADAM_B1, ADAM_B2, ADAM_EPS, ADAM_WD, ADAM_STEP = 0.001, 0.9, 0.999, 1e-08, 0.01, 10
ANY = pl.BlockSpec(memory_space=pl.ANY)
MESH = pl.DeviceIdType.MESH


def _call(body, name, **kw):
    return pl.pallas_call(body, name=name, **kw)


def _params(sem=None):
    return pltpu.CompilerParams(dimension_semantics=sem, vmem_limit_bytes=VMEM_LIMIT)


def _mm(a, b):
    return jnp.dot(a, b, preferred_element_type=F32)


def _mm_nt(a, b):
    return lax.dot_general(a, b, (((1,), (1,)), ((), ())), preferred_element_type=F32)


def _mm_tn(a, b):
    return lax.dot_general(a, b, (((0,), (0,)), ((), ())), preferred_element_type=F32)


def _sigmoid(x):
    return 1.0 / (1.0 + jnp.exp(-x))


def _rope(t, cos, sin):
    half = t.shape[-1] // 2
    t1, t2 = t[:, :half], t[:, half:]
    return jnp.concatenate([t1 * cos - t2 * sin, t2 * cos + t1 * sin], axis=-1)


def _rope_bwd(g, cos, sin):
    half = g.shape[-1] // 2
    g1, g2 = g[:, :half], g[:, half:]
    return jnp.concatenate([g1 * cos + g2 * sin, g2 * cos - g1 * sin], axis=-1)


def _row_mean(x):
    return jnp.mean(x, axis=-1, keepdims=True)


def _col_sum(x):
    return jnp.sum(x, axis=0, keepdims=True)


def _tile_rows(head_ref, x_ref):
    return jnp.where(pl.program_id(0) == 0, head_ref[...], x_ref[...])


def _head_spec():
    return pl.BlockSpec((TILE, D_MODEL), lambda i: (0, 0))


def _x_spec():
    return pl.BlockSpec((TILE, D_MODEL), lambda i: (jnp.maximum(i - 1, 0), 0))


def _inproj_fwd(head, x, g_norm, w_al, w_glr):
    t_rows = x.shape[0] + TILE
    nt = t_rows // TILE

    def body(head_ref, x_ref, g_ref, w_hbm, wg_hbm, u_ref, ut_ref, *rest):
        seg_refs, glr_ref = rest[:10], rest[10]
        w_vm, wg_vm, sem = rest[11:]

        @pl.when(pl.program_id(0) == 0)
        def _():
            c1 = pltpu.make_async_copy(w_hbm, w_vm, sem.at[0])
            c2 = pltpu.make_async_copy(wg_hbm, wg_vm, sem.at[1])
            c1.start()
            c2.start()
            c1.wait()
            c2.wait()

        x = _tile_rows(head_ref, x_ref)
        r = lax.rsqrt(_row_mean(x * x) + EPS)
        u32 = (x * r * g_ref[...]).astype(BF16).astype(F32)
        u = u32.astype(BF16)
        u_ref[...] = u
        ut_ref[...] = u32.T.astype(BF16)
        for s, o_ref in enumerate(seg_refs):
            o_ref[...] = _mm(u, w_vm[:, SEG_OFF[s]:SEG_OFF[s] + SEG_W[s]]).astype(BF16)
        glr_ref[...] = _mm(u, wg_vm[...])

    row = lambda w: pl.BlockSpec((TILE, w), lambda i: (i, 0))
    out_shape = ([jax.ShapeDtypeStruct((t_rows, D_MODEL), BF16), jax.ShapeDtypeStruct((D_MODEL, t_rows), BF16)]
                 + [jax.ShapeDtypeStruct((t_rows, w), BF16) for w in SEG_W]
                 + [jax.ShapeDtypeStruct((t_rows, LANES), F32)])
    out_specs = [row(D_MODEL), pl.BlockSpec((D_MODEL, TILE), lambda i: (0, i))] + [row(w) for w in SEG_W] + [row(LANES)]
    outs = _call(
        body, "inproj_fwd", grid=(nt,), out_shape=out_shape,
        in_specs=[_head_spec(), _x_spec(), pl.BlockSpec((1, D_MODEL), lambda i: (0, 0)), ANY, ANY],
        out_specs=out_specs,
        scratch_shapes=[pltpu.VMEM((D_MODEL, AL_COLS), BF16), pltpu.VMEM((D_MODEL, LANES), BF16), pltpu.SemaphoreType.DMA((2,))],
        compiler_params=_params(("arbitrary",)),
    )(head, x, g_norm, w_al, w_glr)
    return outs[0], outs[1], dict(zip(SEG_NAMES, outs[2:12])), outs[12]


def _ret_decay(lgh):
    i = lax.broadcasted_iota(jnp.int32, (TILE, TILE), 0)
    j = lax.broadcasted_iota(jnp.int32, (TILE, TILE), 1)
    rel = (i - j).astype(F32)
    return jnp.where(rel >= 0, jnp.exp(jnp.maximum(rel, 0.0) * lgh), 0.0)


def _ret_vectors(lgh):
    idx = lax.broadcasted_iota(jnp.int32, (TILE, 1), 0).astype(F32)
    xi = jnp.exp((idx + 1.0) * lgh)
    zeta = jnp.exp((TILE - 1.0 - idx) * lgh)
    gc = jnp.exp(jnp.full((1, 1), float(TILE), F32) * lgh)
    return xi, zeta, gc


def _ret_fwd(seg, cos, sin, gain, lg):
    t_rows = cos.shape[0]
    nt = t_rows // TILE

    def body(lg_ref, q_ref, k_ref, v_ref, g_ref, cos_ref, sin_ref, gain_ref, oraw_ref, oret_ref, st_ref, s_acc, dm):
        h, t = pl.program_id(0), pl.program_id(1)
        lgh = lg_ref[h]

        @pl.when(t == 0)
        def _():
            s_acc[...] = jnp.zeros_like(s_acc)
            dm[...] = _ret_decay(lgh)

        cos_t, sin_t = cos_ref[...], sin_ref[...]
        q = _rope(q_ref[...].astype(F32), cos_t, sin_t)
        k = _rope(k_ref[...].astype(F32), cos_t, sin_t) * (RET_QK ** -0.5)
        xi, zeta, gc = _ret_vectors(lgh)
        v = v_ref[...]
        s_in = s_acc[...]
        p = (_mm_nt(q.astype(BF16), k.astype(BF16)) * dm[...]).astype(BF16)
        o = _mm(p, v) + _mm((q * xi).astype(BF16), s_in.astype(BF16))
        st_ref[...] = s_in.astype(BF16)
        s_acc[...] = s_in * gc + _mm_tn((k * zeta).astype(BF16), v)
        oraw_ref[...] = o
        oc = o - _row_mean(o)
        n = oc * lax.rsqrt(_row_mean(oc * oc) + EPS) * gain_ref[...]
        g = g_ref[...].astype(F32)
        oret_ref[...] = (n * g * _sigmoid(g)).astype(BF16)

    blk = lambda w: pl.BlockSpec((TILE, w), lambda h, t: (t, h))
    tab = pl.BlockSpec((TILE, LANES), lambda h, t: (t, 0))
    return _call(
        body, "ret_fwd", grid=(RET_HEADS, nt),
        out_shape=[jax.ShapeDtypeStruct((t_rows, RET_W), F32), jax.ShapeDtypeStruct((t_rows, RET_W), BF16),
                   jax.ShapeDtypeStruct((RET_HEADS, nt, RET_QK, RET_V), BF16)],
        in_specs=[pl.BlockSpec(memory_space=pltpu.SMEM), blk(RET_QK), blk(RET_QK), blk(RET_V), blk(RET_V), tab, tab,
                  pl.BlockSpec((1, RET_V), lambda h, t: (0, h))],
        out_specs=[blk(RET_V), blk(RET_V), pl.BlockSpec((None, None, RET_QK, RET_V), lambda h, t: (h, t, 0, 0))],
        scratch_shapes=[pltpu.VMEM((RET_QK, RET_V), F32), pltpu.VMEM((TILE, TILE), F32)],
        compiler_params=_params(("arbitrary", "arbitrary")),
    )(lg, seg["rq"], seg["rk"], seg["rv"], seg["rg"], cos, sin, gain)


def _ret_bwd(seg, cos, sin, gain, lg, o_raw, do_ret, states):
    t_rows = cos.shape[0]
    nt = t_rows // TILE

    def body(lg_ref, q_ref, k_ref, v_ref, g_ref, cos_ref, sin_ref, gain_ref, oraw_ref, do_ref, st_ref,
             dq_ref, dk_ref, dv_ref, dg_ref, dgain_ref, e_acc, dm):
        h, j = pl.program_id(0), pl.program_id(1)
        lgh = lg_ref[h]

        @pl.when(j == 0)
        def _():
            e_acc[...] = jnp.zeros_like(e_acc)
            dm[...] = _ret_decay(lgh)
            dgain_ref[...] = jnp.zeros_like(dgain_ref)

        cos_t, sin_t = cos_ref[...], sin_ref[...]
        q = _rope(q_ref[...].astype(F32), cos_t, sin_t)
        k = _rope(k_ref[...].astype(F32), cos_t, sin_t) * (RET_QK ** -0.5)
        xi, zeta, gc = _ret_vectors(lgh)
        v = v_ref[...]
        g = g_ref[...].astype(F32)
        o = oraw_ref[...]
        do = do_ref[...].astype(F32)
        oc = o - _row_mean(o)
        rstd = lax.rsqrt(_row_mean(oc * oc) + EPS)
        xh = oc * rstd
        gain_t = gain_ref[...]
        sg = _sigmoid(g)
        dn = do * (g * sg)
        dg_ref[...] = (do * (xh * gain_t) * (sg * (1.0 + g * (1.0 - sg)))).astype(BF16)
        dgain_ref[...] += _col_sum(dn * xh)
        dxh = dn * gain_t
        dob = (rstd * (dxh - _row_mean(dxh) - xh * _row_mean(dxh * xh))).astype(BF16)
        dmat = dm[...]
        qb, kb = q.astype(BF16), k.astype(BF16)
        p = (_mm_nt(qb, kb) * dmat).astype(BF16)
        dp = (_mm_nt(dob, v) * dmat).astype(BF16)
        s_in = st_ref[...]
        e_in = e_acc[...]
        e_b = e_in.astype(BF16)
        dq = _mm(dp, kb) + _mm_nt(dob, s_in) * xi
        dk = _mm_tn(dp, qb) + _mm_nt(v, e_b) * zeta
        dv_ref[...] = (_mm_tn(p, dob) + _mm((k * zeta).astype(BF16), e_b)).astype(BF16)
        e_acc[...] = e_in * gc + _mm_tn((q * xi).astype(BF16), dob)
        dq_ref[...] = _rope_bwd(dq, cos_t, sin_t).astype(BF16)
        dk_ref[...] = (_rope_bwd(dk, cos_t, sin_t) * (RET_QK ** -0.5)).astype(BF16)

    blk = lambda w: pl.BlockSpec((TILE, w), lambda h, j: (nt - 1 - j, h))
    tab = pl.BlockSpec((TILE, LANES), lambda h, j: (nt - 1 - j, 0))
    vec = pl.BlockSpec((1, RET_V), lambda h, j: (0, h))
    return _call(
        body, "ret_bwd", grid=(RET_HEADS, nt),
        out_shape=[jax.ShapeDtypeStruct((t_rows, RET_HEADS * RET_QK), BF16), jax.ShapeDtypeStruct((t_rows, RET_HEADS * RET_QK), BF16),
                   jax.ShapeDtypeStruct((t_rows, RET_W), BF16), jax.ShapeDtypeStruct((t_rows, RET_W), BF16),
                   jax.ShapeDtypeStruct((1, RET_W), F32)],
        in_specs=[pl.BlockSpec(memory_space=pltpu.SMEM), blk(RET_QK), blk(RET_QK), blk(RET_V), blk(RET_V), tab, tab, vec,
                  blk(RET_V), blk(RET_V), pl.BlockSpec((None, None, RET_QK, RET_V), lambda h, j: (h, nt - 1 - j, 0, 0))],
        out_specs=[blk(RET_QK), blk(RET_QK), blk(RET_V), blk(RET_V), vec],
        scratch_shapes=[pltpu.VMEM((RET_QK, RET_V), F32), pltpu.VMEM((TILE, TILE), F32)],
        compiler_params=_params(("arbitrary", "arbitrary")),
    )(lg, seg["rq"], seg["rk"], seg["rv"], seg["rg"], cos, sin, gain, o_raw, do_ret, states)


GLA_LEVELS = (32, 64, 128, 256)
N_TERMS = 1 + len(GLA_LEVELS)


def _gla_tables():
    p = jnp.arange(TILE)[:, None]
    r = jnp.arange(TILE)[None, :]
    masks = [(p // GLA_CHUNK == r // GLA_CHUNK) & (r <= p)]
    for blk in GLA_LEVELS:
        masks.append((p // blk == r // blk) & (p % blk >= blk // 2) & (r % blk < blk // 2))
    masks = jnp.stack(masks + [m.T for m in masks]).astype(F32)
    cum_fwd = jnp.concatenate([r <= p, masks[0] > 0], axis=0).astype(BF16)
    cum_bwd = jnp.concatenate([r >= p, masks[N_TERMS] > 0], axis=1).astype(BF16)
    return masks, cum_fwd, cum_bwd


def _split3(x):
    hi = x.astype(BF16)
    rest = x - hi.astype(F32)
    mid = rest.astype(BF16)
    lo = (rest - mid.astype(F32)).astype(BF16)
    return jnp.concatenate([hi, mid, lo], axis=1)


def _join3(y):
    w = y.shape[1] // 3
    return (y[:, 2 * w:] + y[:, w:2 * w]) + y[:, :w]


def _gla_prep(q_ref, k_ref, glr_ref, wgu_ref, b_ref, cum_ref, g_scr, ref_scr):
    z = _mm(glr_ref[...].astype(BF16), wgu_ref[...].astype(BF16)) + b_ref[...]
    la = (jnp.minimum(z, 0.0) - jnp.log(1.0 + jnp.exp(-jnp.abs(z)))) / GLA_TAU
    gb = _join3(_mm(cum_ref[...], _split3(la)))
    g, b = gb[:TILE], gb[TILE:]
    g_scr[...] = g
    factors = [(jnp.exp(b), jnp.exp(-b))]
    for lvl, blk in enumerate(GLA_LEVELS):
        for n in range(TILE // blk):
            ref_scr[lvl, n * blk:(n + 1) * blk, :] = jnp.broadcast_to(g_scr[pl.ds(n * blk + blk // 2 - 1, 1), :], (blk, GLA_K))
        x = g - ref_scr[lvl]
        factors.append((jnp.exp(jnp.minimum(x, 0.0)), jnp.exp(jnp.minimum(-x, 0.0))))
    g_last = g_scr[pl.ds(TILE - 1, 1), :]
    q = q_ref[...].astype(F32) * (GLA_K ** -0.5)
    k = k_ref[...].astype(F32)
    return z, q, k, factors, jnp.exp(g), jnp.exp(g_last), jnp.exp(g_last - g)


def _gla_scores(q, k, factors, m_ref):
    a = jnp.zeros((TILE, TILE), F32)
    for l, (fq, fk) in enumerate(factors):
        s = _mm_nt((q * fq).astype(BF16), (k * fk).astype(BF16))
        a = jnp.where(m_ref[l] > 0.0, s, a)
    return a


def _gla_fwd(seg, glr, wgu_pad, b_gate, gain, masks, cum_fwd):
    t_rows = glr.shape[0]
    nt = t_rows // TILE

    def body(q_ref, k_ref, v_ref, g_ref, glr_ref, wgu_ref, b_ref, gain_ref, m_ref, cum_ref, oraw_ref, ogla_ref, st_ref,
             s_acc, g_scr, ref_scr):
        @pl.when(pl.program_id(1) == 0)
        def _():
            s_acc[...] = jnp.zeros_like(s_acc)

        _, q, k, factors, e_g, e_last, e_end = _gla_prep(q_ref, k_ref, glr_ref, wgu_ref, b_ref, cum_ref, g_scr, ref_scr)
        v = v_ref[...]
        st = s_acc[...]
        st_ref[...] = st
        a = _gla_scores(q, k, factors, m_ref)
        o = _mm(a.astype(BF16), v) + _mm_nt((q * e_g).astype(BF16), st.astype(BF16))
        s_acc[...] = st * e_last + _mm(v.astype(F32).T.astype(BF16), (k * e_end).astype(BF16))
        oraw_ref[...] = o
        n = o * lax.rsqrt(_row_mean(o * o) + EPS) * gain_ref[...]
        g = g_ref[...].astype(F32)
        ogla_ref[...] = (n * g * _sigmoid(g)).astype(BF16)

    blk = lambda w: pl.BlockSpec((TILE, w), lambda h, t: (t, h))
    return _call(
        body, "gla_fwd", grid=(GLA_HEADS, nt),
        out_shape=[jax.ShapeDtypeStruct((t_rows, GLA_W), F32), jax.ShapeDtypeStruct((t_rows, GLA_W), BF16),
                   jax.ShapeDtypeStruct((GLA_HEADS, nt, GLA_V, GLA_K), F32)],
        in_specs=[blk(GLA_K), blk(GLA_K), blk(GLA_V), blk(GLA_V), pl.BlockSpec((TILE, LANES), lambda h, t: (t, 0)),
                  pl.BlockSpec((LANES, GLA_K), lambda h, t: (0, h)), pl.BlockSpec((1, GLA_K), lambda h, t: (0, h)),
                  pl.BlockSpec((1, GLA_V), lambda h, t: (0, h)),
                  pl.BlockSpec((N_TERMS, TILE, TILE), lambda h, t: (0, 0, 0)), pl.BlockSpec((2 * TILE, TILE), lambda h, t: (0, 0))],
        out_specs=[blk(GLA_V), blk(GLA_V), pl.BlockSpec((None, None, GLA_V, GLA_K), lambda h, t: (h, t, 0, 0))],
        scratch_shapes=[pltpu.VMEM((GLA_V, GLA_K), F32), pltpu.VMEM((TILE, GLA_K), F32),
                        pltpu.VMEM((len(GLA_LEVELS), TILE, GLA_K), F32)],
        compiler_params=_params(("arbitrary", "arbitrary")),
    )(seg["gq"], seg["gk"], seg["gv"], seg["gg"], glr, wgu_pad, b_gate, gain, masks, cum_fwd)


def _gla_bwd(seg, glr, wgu_pad, b_gate, gain, o_raw, do_gla, states, masks, cum_fwd, cum_bwd):
    t_rows = glr.shape[0]
    nt = t_rows // TILE

    def body(q_ref, k_ref, v_ref, g_ref, glr_ref, wgu_ref, b_ref, gain_ref, m_ref, cum_ref, cumb_ref, oraw_ref, do_ref, st_ref,
             dq_ref, dk_ref, dv_ref, dg_ref, dglr_ref, dwgu_ref, dbg_ref, dgain_ref, d_acc, g_scr, ref_scr, dref_scr):
        @pl.when(pl.program_id(1) == 0)
        def _():
            d_acc[...] = jnp.zeros_like(d_acc)
            dwgu_ref[...] = jnp.zeros_like(dwgu_ref)
            dbg_ref[...] = jnp.zeros_like(dbg_ref)
            dgain_ref[...] = jnp.zeros_like(dgain_ref)

        z, q, k, factors, e_g, e_last, e_end = _gla_prep(q_ref, k_ref, glr_ref, wgu_ref, b_ref, cum_ref, g_scr, ref_scr)
        v = v_ref[...]
        o = oraw_ref[...]
        do = do_ref[...].astype(F32)
        g = g_ref[...].astype(F32)
        rinv = lax.rsqrt(_row_mean(o * o) + EPS)
        nh = o * rinv
        gain_t = gain_ref[...]
        sg = _sigmoid(g)
        dn = do * (g * sg)
        dg_ref[...] = (do * (nh * gain_t) * (sg * (1.0 + g * (1.0 - sg)))).astype(BF16)
        dgain_ref[...] += _col_sum(dn * nh)
        dnh = dn * gain_t
        dor = rinv * (dnh - nh * _row_mean(dnh * nh))
        dob = dor.astype(BF16)
        a_t = _gla_scores(q, k, factors, m_ref).T.astype(BF16)
        da = _mm_nt(dob, v)
        da_t = _mm_nt(v, dob)
        st_in = st_ref[...]
        d_out = d_acc[...]
        d_out_b = d_out.astype(BF16)
        qg, kg = q * e_g, k * e_end
        dqg = _mm(dob, st_in.astype(BF16))
        dkg = _mm(v, d_out_b)
        dv_ref[...] = (_mm(a_t, dob) + _mm_nt(kg.astype(BF16), d_out_b)).astype(BF16)
        d_acc[...] = d_out * e_last + _mm(dor.T.astype(BF16), qg.astype(BF16))
        dq = dqg * e_g
        dk = dkg * e_end
        dkg_kg = dkg * kg
        dg_cum = dqg * qg - dkg_kg
        db = None
        for l, (fq, fk) in enumerate(factors):
            qt, kt = q * fq, k * fk
            dqt = _mm(jnp.where(m_ref[l] > 0.0, da, 0.0).astype(BF16), kt.astype(BF16))
            dkt = _mm(jnp.where(m_ref[N_TERMS + l] > 0.0, da_t, 0.0).astype(BF16), qt.astype(BF16))
            dq = dq + dqt * fq
            dk = dk + dkt * fk
            diff = dqt * qt - dkt * kt
            if l == 0:
                db = diff
            else:
                dg_cum = dg_cum + diff
                dref_scr[l - 1] = diff
        dq_ref[...] = (dq * (GLA_K ** -0.5)).astype(BF16)
        dk_ref[...] = dk.astype(BF16)
        g_scr[...] = dg_cum
        g_scr[pl.ds(TILE - 1, 1), :] += e_last * _col_sum(d_out * st_in) + _col_sum(dkg_kg)
        for lvl, blk in enumerate(GLA_LEVELS):
            for n in range(TILE // blk):
                g_scr[pl.ds(n * blk + blk // 2 - 1, 1), :] -= _col_sum(dref_scr[lvl, n * blk:(n + 1) * blk, :])
        dla = _join3(_mm(cumb_ref[...], jnp.concatenate([_split3(g_scr[...]), _split3(db)], axis=0)))
        dz = dla * (1.0 / GLA_TAU) * _sigmoid(-z)
        dzb = dz.astype(BF16)
        dglr_ref[...] = _mm_nt(dzb, wgu_ref[...].astype(BF16))
        dwgu_ref[...] += _mm(glr_ref[...].T.astype(BF16), dzb)
        dbg_ref[...] += _col_sum(dz)

    blk = lambda w: pl.BlockSpec((TILE, w), lambda h, j: (nt - 1 - j, h))
    vec = lambda w: pl.BlockSpec((1, w), lambda h, j: (0, h))
    wspec = pl.BlockSpec((LANES, GLA_K), lambda h, j: (0, h))
    return _call(
        body, "gla_bwd", grid=(GLA_HEADS, nt),
        out_shape=[jax.ShapeDtypeStruct((t_rows, GLA_HEADS * GLA_K), BF16), jax.ShapeDtypeStruct((t_rows, GLA_HEADS * GLA_K), BF16),
                   jax.ShapeDtypeStruct((t_rows, GLA_W), BF16), jax.ShapeDtypeStruct((t_rows, GLA_W), BF16),
                   jax.ShapeDtypeStruct((GLA_HEADS, t_rows, LANES), F32), jax.ShapeDtypeStruct((LANES, GLA_HEADS * GLA_K), F32),
                   jax.ShapeDtypeStruct((1, GLA_HEADS * GLA_K), F32), jax.ShapeDtypeStruct((1, GLA_W), F32)],
        in_specs=[blk(GLA_K), blk(GLA_K), blk(GLA_V), blk(GLA_V), pl.BlockSpec((TILE, LANES), lambda h, j: (nt - 1 - j, 0)),
                  wspec, vec(GLA_K), vec(GLA_V),
                  pl.BlockSpec((2 * N_TERMS, TILE, TILE), lambda h, j: (0, 0, 0)), pl.BlockSpec((2 * TILE, TILE), lambda h, j: (0, 0)),
                  pl.BlockSpec((TILE, 2 * TILE), lambda h, j: (0, 0)), blk(GLA_V), blk(GLA_V),
                  pl.BlockSpec((None, None, GLA_V, GLA_K), lambda h, j: (h, nt - 1 - j, 0, 0))],
        out_specs=[blk(GLA_K), blk(GLA_K), blk(GLA_V), blk(GLA_V),
                   pl.BlockSpec((None, TILE, LANES), lambda h, j: (h, nt - 1 - j, 0)), wspec, vec(GLA_K), vec(GLA_V)],
        scratch_shapes=[pltpu.VMEM((GLA_V, GLA_K), F32), pltpu.VMEM((TILE, GLA_K), F32),
                        pltpu.VMEM((len(GLA_LEVELS), TILE, GLA_K), F32), pltpu.VMEM((len(GLA_LEVELS), TILE, GLA_K), F32)],
        compiler_params=_params(("arbitrary", "arbitrary")),
    )(seg["gq"], seg["gk"], seg["gv"], seg["gg"], glr, wgu_pad, b_gate, gain, masks, cum_fwd, cum_bwd, o_raw, do_gla, states)


def _merge_fwd_bwd(o_ret, o_gla, seg, x, target, g_final, w_br, w_bg, w_out):
    t_rows = x.shape[0] + TILE
    nt = t_rows // TILE

    def body(oret_ref, ogla_ref, mr_ref, mg_ref, h0_ref, tgt_ref, gf_ref, wbr_hbm, wbg_hbm, wout_hbm,
             dh1_ref, dmr_ref, dmg_ref, doret_ref, dogla_ref, loss_ref, dgf_ref, dwbr_hbm, dwbg_hbm, dwout_hbm,
             wbr, wbg, wout, abr, abg, aout, sem):
        i = pl.program_id(0)

        @pl.when(i == 0)
        def _():
            cps = [pltpu.make_async_copy(s, d, sem.at[n]) for n, (s, d) in enumerate(((wbr_hbm, wbr), (wbg_hbm, wbg), (wout_hbm, wout)))]
            for cp in cps:
                cp.start()
            abr[...] = jnp.zeros_like(abr)
            abg[...] = jnp.zeros_like(abg)
            aout[...] = jnp.zeros_like(aout)
            loss_ref[...] = jnp.zeros_like(loss_ref)
            dgf_ref[...] = jnp.zeros_like(dgf_ref)
            for cp in cps:
                cp.wait()
            dh1_ref[...] = jnp.zeros_like(dh1_ref)
            dmr_ref[...] = jnp.zeros_like(dmr_ref)
            dmg_ref[...] = jnp.zeros_like(dmg_ref)
            doret_ref[...] = jnp.zeros_like(doret_ref)
            dogla_ref[...] = jnp.zeros_like(dogla_ref)

        @pl.when(i > 0)
        def _():
            oret, ogla = oret_ref[...], ogla_ref[...]
            br, bg = _mm(oret, wbr[...]), _mm(ogla, wbg[...])
            sr, sg = _sigmoid(mr_ref[...].astype(F32)), _sigmoid(mg_ref[...].astype(F32))
            mb = (sr * br + sg * bg).astype(BF16)
            h1 = h0_ref[...] + _mm(mb, wout[...])
            r2 = lax.rsqrt(_row_mean(h1 * h1) + EPS)
            hn = h1 * r2
            gf = gf_ref[...]
            diff = hn * gf - tgt_ref[...]
            loss_ref[...] += 0.5 * jnp.sum(_row_mean(diff * diff))
            dy = diff * (1.0 / D_MODEL)
            dgf_ref[...] += _col_sum(dy * hn)
            dyg = dy * gf
            dh1 = r2 * (dyg - hn * _row_mean(dyg * hn))
            dh1_ref[...] = dh1
            dh1b = dh1.astype(BF16)
            dm = _mm_nt(dh1b, wout[...])
            aout[...] += _mm_tn(mb, dh1b)
            dbr = (dm * sr).astype(BF16)
            dbg = (dm * sg).astype(BF16)
            dmr_ref[...] = (dm * br * sr * (1.0 - sr)).astype(BF16)
            dmg_ref[...] = (dm * bg * sg * (1.0 - sg)).astype(BF16)
            doret_ref[...] = _mm_nt(dbr, wbr[...]).astype(BF16)
            dogla_ref[...] = _mm_nt(dbg, wbg[...]).astype(BF16)
            abr[...] += _mm_tn(oret, dbr)
            abg[...] += _mm_tn(ogla, dbg)

        @pl.when(i == nt - 1)
        def _():
            pltpu.sync_copy(abr, dwbr_hbm)
            pltpu.sync_copy(abg, dwbg_hbm)
            pltpu.sync_copy(aout, dwout_hbm)

    row = lambda w: pl.BlockSpec((TILE, w), lambda i: (i, 0))
    one = lambda w: pl.BlockSpec((1, w), lambda i: (0, 0))
    return _call(
        body, "merge_fwd_bwd", grid=(nt,),
        out_shape=[jax.ShapeDtypeStruct((t_rows, D_MODEL), F32), jax.ShapeDtypeStruct((t_rows, D_MODEL), BF16),
                   jax.ShapeDtypeStruct((t_rows, D_MODEL), BF16), jax.ShapeDtypeStruct((t_rows, RET_W), BF16),
                   jax.ShapeDtypeStruct((t_rows, GLA_W), BF16), jax.ShapeDtypeStruct((1, LANES), F32),
                   jax.ShapeDtypeStruct((1, D_MODEL), F32), jax.ShapeDtypeStruct((RET_W, D_MODEL), F32),
                   jax.ShapeDtypeStruct((GLA_W, D_MODEL), F32), jax.ShapeDtypeStruct((D_MODEL, D_MODEL), F32)],
        in_specs=[row(RET_W), row(GLA_W), row(D_MODEL), row(D_MODEL), _x_spec(), _x_spec(), one(D_MODEL), ANY, ANY, ANY],
        out_specs=[row(D_MODEL), row(D_MODEL), row(D_MODEL), row(RET_W), row(GLA_W), one(LANES), one(D_MODEL), ANY, ANY, ANY],
        scratch_shapes=[pltpu.VMEM((RET_W, D_MODEL), BF16), pltpu.VMEM((GLA_W, D_MODEL), BF16), pltpu.VMEM((D_MODEL, D_MODEL), BF16),
                        pltpu.VMEM((RET_W, D_MODEL), F32), pltpu.VMEM((GLA_W, D_MODEL), F32), pltpu.VMEM((D_MODEL, D_MODEL), F32),
                        pltpu.SemaphoreType.DMA((3,))],
        compiler_params=_params(("arbitrary",)),
    )(o_ret, o_gla, seg["mr"], seg["mg"], x, target, g_final, w_br, w_bg, w_out)


def _inproj_bwd_x(dseg, dglr_parts, head, x, dh1, g_norm, w_al, w_glr):
    t_rows = x.shape[0] + TILE
    nt = t_rows // TILE

    def body(*refs):
        d_refs = refs[:10]
        (dglr_ref, head_ref, x_ref, dh1_ref, g_ref, w_hbm, wg_hbm, dx_ref, dhead_ref, dgn_ref, dglr_out,
         w_vm, wg_vm, sem) = refs[10:]

        @pl.when(pl.program_id(0) == 0)
        def _():
            c1 = pltpu.make_async_copy(w_hbm, w_vm, sem.at[0])
            c2 = pltpu.make_async_copy(wg_hbm, wg_vm, sem.at[1])
            c1.start()
            c2.start()
            dgn_ref[...] = jnp.zeros_like(dgn_ref)
            c1.wait()
            c2.wait()

        dglr = (dglr_ref[0] + dglr_ref[1] + dglr_ref[2] + dglr_ref[3]).astype(BF16)
        dglr_out[...] = dglr
        du = _mm_nt(dglr, wg_vm[...])
        for s, d_ref in enumerate(d_refs):
            du = du + _mm_nt(d_ref[...], w_vm[:, SEG_OFF[s]:SEG_OFF[s] + SEG_W[s]])
        x = _tile_rows(head_ref, x_ref)
        r = lax.rsqrt(_row_mean(x * x) + EPS)
        hn = x * r
        dgn_ref[...] += _col_sum(du * hn)
        dug = du * g_ref[...]
        dh0 = dh1_ref[...] + r * (dug - hn * _row_mean(dug * hn))
        dx_ref[...] = dh0

        @pl.when(pl.program_id(0) == 0)
        def _():
            dhead_ref[...] = dh0

    row = lambda w: pl.BlockSpec((TILE, w), lambda i: (i, 0))
    one = pl.BlockSpec((1, D_MODEL), lambda i: (0, 0))
    return _call(
        body, "inproj_bwd_x", grid=(nt,),
        out_shape=[jax.ShapeDtypeStruct((t_rows - TILE, D_MODEL), F32), jax.ShapeDtypeStruct((TILE, D_MODEL), F32),
                   jax.ShapeDtypeStruct((1, D_MODEL), F32), jax.ShapeDtypeStruct((t_rows, LANES), BF16)],
        in_specs=[row(w) for w in SEG_W] + [pl.BlockSpec((GLA_HEADS, TILE, LANES), lambda i: (0, i, 0)),
                                            _head_spec(), _x_spec(), row(D_MODEL), one, ANY, ANY],
        out_specs=[_x_spec(), _head_spec(), one, row(LANES)],
        scratch_shapes=[pltpu.VMEM((D_MODEL, AL_COLS), BF16), pltpu.VMEM((D_MODEL, LANES), BF16), pltpu.SemaphoreType.DMA((2,))],
        compiler_params=_params(("arbitrary",)),
    )(*[dseg[n] for n in SEG_NAMES], dglr_parts, head, x, dh1, g_norm, w_al, w_glr)


def _inproj_bwd_w(ut, d, name):
    t_rows, width = d.shape
    nt = t_rows // TILE
    rk = 3 * TILE if nt % 3 == 0 else TILE
    tn = min(width, 512)

    def body(ut_ref, d_ref, o_ref):
        @pl.when(pl.program_id(1) == 0)
        def _():
            o_ref[...] = jnp.zeros_like(o_ref)

        o_ref[...] += _mm(ut_ref[...], d_ref[...])

    return _call(
        body, name, grid=(width // tn, t_rows // rk),
        out_shape=jax.ShapeDtypeStruct((D_MODEL, width), F32),
        in_specs=[pl.BlockSpec((D_MODEL, rk), lambda j, k: (0, k)), pl.BlockSpec((rk, tn), lambda j, k: (k, j))],
        out_specs=pl.BlockSpec((D_MODEL, tn), lambda j, k: (0, j)),
        compiler_params=_params(("arbitrary", "arbitrary")),
    )(ut, d)


def _position():
    x, y, c = lax.axis_index("x"), lax.axis_index("y"), lax.axis_index("c")
    return x, y, c


def _index(px, py, pc):
    return 4 * px + 2 * py + pc


def _all_gather(arrs):
    n = len(arrs)

    def body(*refs):
        ins, outs = refs[:n], refs[n:2 * n]
        send_sems, recv_sems, local_sems = refs[2 * n:]
        x, y, c = _position()
        me, sibling = (x, y, c), (x, y, 1 - c)
        chips = [(1 - x, y), (x, 1 - y), (1 - x, 1 - y)]

        def copy(a, k, block, to, src=None):
            dst = outs[a].at[_index(*block)]
            return pltpu.make_async_remote_copy(src_ref=dst if src is None else src, dst_ref=dst,
                                                send_sem=send_sems.at[7 * a + k], recv_sem=recv_sems.at[7 * a + k],
                                                device_id=to, device_id_type=MESH)

        mine = [pltpu.make_async_copy(ins[a], outs[a].at[_index(*me)], local_sems.at[a]) for a in range(n)]
        for cp in mine:
            cp.start()
        first = []
        for a in range(n):
            first.append(copy(a, 0, me, sibling, src=ins[a]))
            first += [copy(a, 1 + j, me, (*chip, c), src=ins[a]) for j, chip in enumerate(chips)]
        for cp in first:
            cp.start()
        passed = []
        for j, chip in enumerate(chips):
            for a in range(n):
                copy(a, 1 + j, (*chip, c), me).wait_recv()
                cp = copy(a, 4 + j, (*chip, c), sibling)
                cp.start()
                passed.append(cp)
        for a in range(n):
            copy(a, 0, sibling, me).wait_recv()
            for j, chip in enumerate(chips):
                copy(a, 4 + j, (*chip, 1 - c), me).wait_recv()
        for cp in first + passed:
            cp.wait_send()
        for cp in mine:
            cp.wait()

    return _call(
        body, "all_gather_shards",
        out_shape=[jax.ShapeDtypeStruct((N_DEV, *a.shape), a.dtype) for a in arrs],
        in_specs=[ANY] * n, out_specs=[ANY] * n,
        scratch_shapes=[pltpu.SemaphoreType.DMA((7 * n,)), pltpu.SemaphoreType.DMA((7 * n,)), pltpu.SemaphoreType.DMA((n,))],
    )(*arrs)


def _exchange(blocks, shared):
    arrs = list(blocks) + list(shared)
    n, nb = len(arrs), len(blocks)

    def body(*refs):
        ins, outs = refs[:n], refs[n:2 * n]
        send_sems, recv_sems, local_sems = refs[2 * n:]
        x, y, c = _position()
        me = _index(x, y, c)

        def src_of(a, dev):
            return ins[a].at[dev] if a < nb else ins[a]

        mine = [pltpu.make_async_copy(src_of(a, me), outs[a].at[me], local_sems.at[a]) for a in range(n)]
        for cp in mine:
            cp.start()
        sends, peers = [], []
        for m in range(1, N_DEV):
            px = 1 - x if m & 4 else x
            py = 1 - y if m & 2 else y
            pc = 1 - c if m & 1 else c
            peers.append((m, (px, py, pc)))
        for m, peer in peers:
            for a in range(n):
                cp = pltpu.make_async_remote_copy(src_ref=src_of(a, _index(*peer)), dst_ref=outs[a].at[me],
                                                  send_sem=send_sems.at[7 * a + m - 1], recv_sem=recv_sems.at[7 * a + m - 1],
                                                  device_id=peer, device_id_type=MESH)
                cp.start()
                sends.append(cp)
        for m, peer in peers:
            for a in range(n):
                pltpu.make_async_remote_copy(src_ref=src_of(a, me), dst_ref=outs[a].at[_index(*peer)],
                                             send_sem=send_sems.at[7 * a + m - 1], recv_sem=recv_sems.at[7 * a + m - 1],
                                             device_id=peer, device_id_type=MESH).wait_recv()
        for cp in sends:
            cp.wait_send()
        for cp in mine:
            cp.wait()

    return _call(
        body, "exchange_partials",
        out_shape=[jax.ShapeDtypeStruct(a.shape, a.dtype) for a in blocks]
                  + [jax.ShapeDtypeStruct((N_DEV, *a.shape), a.dtype) for a in shared],
        in_specs=[ANY] * n, out_specs=[ANY] * n,
        scratch_shapes=[pltpu.SemaphoreType.DMA((7 * n,)), pltpu.SemaphoreType.DMA((7 * n,)), pltpu.SemaphoreType.DMA((n,))],
    )(*arrs)


def _adamw(g, w, m, v):
    m_new = ADAM_B1 * m + (1.0 - ADAM_B1) * g
    v_new = ADAM_B2 * v + (1.0 - ADAM_B2) * (g * g)
    m_hat = m_new / (1.0 - ADAM_B1 ** ADAM_STEP)
    v_hat = v_new / (1.0 - ADAM_B2 ** ADAM_STEP)
    delta = -ADAM_LR * (m_hat / (jnp.sqrt(v_hat) + ADAM_EPS) + ADAM_WD * w)
    return delta, m_new, v_new


def _sum_partials(p_ref):
    g = p_ref[0].astype(F32)
    for d in range(1, N_DEV):
        g = g + p_ref[d].astype(F32)
    return g


def _reduce_adam(parts, w, m, v, name, block_rows, row_off=0):
    rows, cols = w.shape
    off = row_off // block_rows

    def body(p_ref, w_ref, m_ref, v_ref, g_ref, d_ref, mo_ref, vo_ref):
        g = _sum_partials(p_ref)
        g_ref[...] = g
        d_ref[...], mo_ref[...], vo_ref[...] = _adamw(g, w_ref[...], m_ref[...], v_ref[...])

    blk = pl.BlockSpec((block_rows, cols), lambda i: (i, 0))
    return _call(
        body, name, grid=(rows // block_rows,),
        out_shape=[jax.ShapeDtypeStruct((rows, cols), F32)] * 4,
        in_specs=[pl.BlockSpec((N_DEV, block_rows, cols), lambda i: (0, i + off, 0)), blk, blk, blk],
        out_specs=[blk] * 4,
        compiler_params=_params(("arbitrary",)),
    )(parts, w, m, v)


def _reduce_small(parts):
    def body(p_ref, o_ref):
        o_ref[...] = _sum_partials(p_ref)

    return _call(body, "reduce_small", out_shape=jax.ShapeDtypeStruct(parts.shape[1:], F32))(parts)


def _adam_small(g, w, m, v):
    def body(g_ref, w_ref, m_ref, v_ref, d_ref, mo_ref, vo_ref):
        d_ref[...], mo_ref[...], vo_ref[...] = _adamw(g_ref[...], w_ref[...], m_ref[...], v_ref[...])

    return _call(body, "adam_small", out_shape=[jax.ShapeDtypeStruct(g.shape, F32)] * 3)(g, w, m, v)


def _pack_rows(arrs):
    rows = []
    for a in arrs:
        flat = a.reshape(-1).astype(F32)
        pad = (-flat.shape[0]) % LANES
        rows.append(jnp.pad(flat, (0, pad)).reshape(-1, LANES))
    packed = jnp.concatenate(rows, axis=0)
    return jnp.pad(packed, ((0, (-packed.shape[0]) % 8), (0, 0)))


def _unpack_rows(packed, shapes):
    out, r = [], 0
    for shp in shapes:
        size = 1
        for s in shp:
            size *= s
        nrows = -(-size // LANES)
        out.append(packed[r:r + nrows].reshape(-1)[:size].reshape(shp))
        r += nrows
    return out


def kernel(x, meta_tokens, norm_gain, w_in, w_gate_up, b_gate, ret_norm_gain, gla_norm_gain, w_branch_ret, w_branch_gla, w_out, final_norm_gain, loss_target, m_meta_tokens, m_norm_gain, m_w_in, m_w_gate_up, m_b_gate, m_ret_norm_gain, m_gla_norm_gain, m_w_branch_ret, m_w_branch_gla, m_w_out, m_final_norm_gain, v_meta_tokens, v_norm_gain, v_w_in, v_w_gate_up, v_b_gate, v_ret_norm_gain, v_gla_norm_gain, v_w_branch_ret, v_w_branch_gla, v_w_out, v_final_norm_gain):
    xi, yi, ci = _position()
    me = _index(xi, yi, ci)
    seq = x.shape[1]
    t_rows = seq + TILE
    in_shard = w_in.shape[2]
    gu_shard = w_gate_up.shape[2]
    meta_shard = meta_tokens.shape[1]
    ret_rows, gla_rows, out_rows = w_branch_ret.shape[1], w_branch_gla.shape[1], w_out.shape[1]

    rows_local = jnp.concatenate([w_branch_ret[0], w_branch_gla[0], w_out[0]], axis=0).astype(BF16)
    small_local = jnp.concatenate([meta_tokens, jnp.pad(w_gate_up[0], ((0, 0), (0, LANES - gu_shard)))], axis=0)
    g_in, g_rows, g_small = _all_gather([w_in[0].astype(BF16), rows_local, small_local])
    w_full = jnp.transpose(g_in, (1, 0, 2)).reshape(D_MODEL, IN_COLS)
    w_al = jnp.concatenate([w_full[:, :GLR_OFF], w_full[:, GLR_OFF + GLA_RANK:]], axis=1)
    w_glr = jnp.pad(w_full[:, GLR_OFF:GLR_OFF + GLA_RANK], ((0, 0), (0, LANES - GLA_RANK)))
    w_br = g_rows[:, :ret_rows].reshape(RET_W, D_MODEL)
    w_bg = g_rows[:, ret_rows:ret_rows + gla_rows].reshape(GLA_W, D_MODEL)
    w_o = g_rows[:, ret_rows + gla_rows:].reshape(D_MODEL, D_MODEL)
    meta_full = jnp.transpose(g_small[:, :N_META, :], (1, 0, 2)).reshape(N_META, D_MODEL)
    wgu_full = jnp.transpose(g_small[:, N_META:, :gu_shard], (1, 0, 2)).reshape(GLA_RANK, GLA_HEADS * GLA_K)
    wgu_pad = jnp.pad(wgu_full, ((0, LANES - GLA_RANK), (0, 0)))

    pos = jnp.arange(t_rows, dtype=F32) - float(PAD_ROWS)
    half = RET_QK // 2
    inv = ROPE_BASE ** (-jnp.arange(half, dtype=F32) / half)
    ang = pos[:, None] * inv[None, :]
    cos, sin = jnp.cos(ang), jnp.sin(ang)
    lg = jnp.log1p(-(2.0 ** (-5.0 - jnp.arange(RET_HEADS, dtype=F32))))

    head = jnp.concatenate([jnp.zeros((PAD_ROWS, D_MODEL), F32), meta_full], axis=0)
    u, ut, seg, glr = _inproj_fwd(head, x[0], norm_gain, w_al, w_glr)
    o_ret_raw, o_ret, ret_states = _ret_fwd(seg, cos, sin, ret_norm_gain, lg)
    masks, cum_fwd, cum_bwd = _gla_tables()
    o_gla_raw, o_gla, gla_states = _gla_fwd(seg, glr, wgu_pad, b_gate, gla_norm_gain, masks, cum_fwd)
    (dh1, d_mr, d_mg, do_ret, do_gla, loss_part, d_gfinal, dw_br, dw_bg, dw_o) = _merge_fwd_bwd(
        o_ret, o_gla, seg, x[0], loss_target[0], final_norm_gain.reshape(1, D_MODEL), w_br, w_bg, w_o)

    d_rq, d_rk, d_rv, d_rg, d_gret = _ret_bwd(seg, cos, sin, ret_norm_gain, lg, o_ret_raw, do_ret, ret_states)
    d_gq, d_gk, d_gv, d_gg, dglr_parts, d_wgu, d_bgate, d_ggla = _gla_bwd(
        seg, glr, wgu_pad, b_gate, gla_norm_gain, o_gla_raw, do_gla, gla_states, masks, cum_fwd, cum_bwd)
    dseg = dict(rq=d_rq, rk=d_rk, rv=d_rv, rg=d_rg, gq=d_gq, gk=d_gk, gv=d_gv, gg=d_gg, mr=d_mr, mg=d_mg)
    grad_x, d_head, d_gnorm, dglr = _inproj_bwd_x(dseg, dglr_parts, head, x[0], dh1, norm_gain, w_al, w_glr)
    dw_seg = [_inproj_bwd_w(ut, dseg[n], "inproj_bwd_w_" + n) for n in SEG_NAMES]
    dw_glr = _inproj_bwd_w(ut, dglr, "inproj_bwd_w_glr")

    dw_in_full = jnp.concatenate(dw_seg[:8] + [dw_glr[:, :GLA_RANK]] + dw_seg[8:], axis=1)
    send_in = jnp.transpose(dw_in_full.astype(BF16).reshape(D_MODEL, N_DEV, in_shard), (1, 0, 2))
    send_rows = jnp.concatenate([dw_br.reshape(N_DEV, ret_rows, D_MODEL), dw_bg.reshape(N_DEV, gla_rows, D_MODEL),
                                 dw_o.reshape(N_DEV, out_rows, D_MODEL)], axis=1).astype(BF16)
    small_shapes = [(N_META, D_MODEL), (1, D_MODEL), (GLA_RANK, GLA_HEADS * GLA_K), (1, GLA_HEADS * GLA_K),
                    (1, RET_W), (1, GLA_W), (1, D_MODEL), (1, LANES)]
    small_part = _pack_rows([d_head[PAD_ROWS:], d_gnorm, d_wgu[:GLA_RANK], d_bgate, d_gret, d_ggla, d_gfinal, loss_part])
    p_in, p_rows, p_small = _exchange([send_in, send_rows], [small_part])

    g_w_in, d_w_in, nm_w_in, nv_w_in = _reduce_adam(p_in, w_in[0], m_w_in[0], v_w_in[0], "adam_w_in", LANES)
    rb = gla_rows
    g_w_br, d_w_br, nm_w_br, nv_w_br = _reduce_adam(p_rows, w_branch_ret[0], m_w_branch_ret[0], v_w_branch_ret[0], "adam_w_branch_ret", rb, 0)
    g_w_bg, d_w_bg, nm_w_bg, nv_w_bg = _reduce_adam(p_rows, w_branch_gla[0], m_w_branch_gla[0], v_w_branch_gla[0], "adam_w_branch_gla", rb, ret_rows)
    g_w_o, d_w_o, nm_w_o, nv_w_o = _reduce_adam(p_rows, w_out[0], m_w_out[0], v_w_out[0], "adam_w_out", rb, ret_rows + gla_rows)
    g_meta_f, g_gnorm, g_wgu_f, g_bgate, g_gret, g_ggla, g_gfinal, loss_all = _unpack_rows(_reduce_small(p_small), small_shapes)
    g_meta = lax.dynamic_slice_in_dim(g_meta_f, me * meta_shard, meta_shard, axis=1)
    g_wgu = lax.dynamic_slice_in_dim(g_wgu_f, me * gu_shard, gu_shard, axis=1)
    s_g = [g_meta, g_gnorm, g_wgu, g_bgate, g_gret, g_ggla, g_gfinal]
    s_w = [meta_tokens, norm_gain, w_gate_up[0], b_gate, ret_norm_gain, gla_norm_gain, final_norm_gain]
    s_m = [m_meta_tokens, m_norm_gain, m_w_gate_up[0], m_b_gate, m_ret_norm_gain, m_gla_norm_gain, m_final_norm_gain]
    s_v = [v_meta_tokens, v_norm_gain, v_w_gate_up[0], v_b_gate, v_ret_norm_gain, v_gla_norm_gain, v_final_norm_gain]
    shapes = [a.shape for a in s_g]
    s_d, s_nm, s_nv = [_unpack_rows(p, shapes) for p in _adam_small(*[_pack_rows(l) for l in (s_g, s_w, s_m, s_v)])]

    loss = loss_all[0, 0]
    grad_x = grad_x[None]

    def order(meta, gnorm, win, wgu, bgate, gret, ggla, wbr, wbg, wo, gfin):
        return (meta, gnorm, win[None], wgu[None], bgate, gret, ggla, wbr[None], wbg[None], wo[None], gfin.reshape(final_norm_gain.shape))

    def small(l):
        return dict(meta=l[0], gnorm=l[1], wgu=l[2], bgate=l[3], gret=l[4], ggla=l[5], gfin=l[6])

    grads = order(win=g_w_in, wbr=g_w_br, wbg=g_w_bg, wo=g_w_o, **small(s_g))
    deltas = order(win=d_w_in, wbr=d_w_br, wbg=d_w_bg, wo=d_w_o, **small(s_d))
    new_m = order(win=nm_w_in, wbr=nm_w_br, wbg=nm_w_bg, wo=nm_w_o, **small(s_nm))
    new_v = order(win=nv_w_in, wbr=nv_w_br, wbg=nv_w_bg, wo=nv_w_o, **small(s_nv))
    return (loss, grad_x, *grads, *deltas, *new_m, *new_v)
```

```python
import functools

import jax
import jax.numpy as jnp
from jax import lax
from jax.experimental import pallas as pl
from jax.experimental.pallas import tpu as pltpu

F32 = jnp.float32
BF16 = jnp.bfloat16

D_MODEL = 1024
N_META = 16
TILE = 256
PAD_ROWS = TILE - N_META
RET_HEADS = 4
RET_QK = 256
RET_V = 512
RET_W = RET_HEADS * RET_V
GLA_HEADS = 4
GLA_K = 128
GLA_V = 256
GLA_W = GLA_HEADS * GLA_V
GLA_RANK = 16
GLA_TAU = 16.0
GLA_CHUNK = 16
ROPE_BASE = 10000.0
EPS = 1e-6
LANES = 128
N_DEV = 8
SEG_NAMES = ("rq", "rk", "rv", "rg", "gq", "gk", "gv", "gg", "mr", "mg")
SEG_W = (1024, 1024, 2048, 2048, 512, 512, 1024, 1024, 1024, 1024)
SEG_OFF = tuple(sum(SEG_W[:i]) for i in range(len(SEG_W)))
AL_COLS = sum(SEG_W)
IN_COLS = AL_COLS + GLA_RANK
GLR_OFF = sum(SEG_W[:8])
IN_SHARD = IN_COLS // N_DEV


def _aligned_col(c):
    assert c <= GLR_OFF or c >= GLR_OFF + GLA_RANK
    return c if c <= GLR_OFF else c - GLA_RANK


SLAB_BOUND = tuple(_aligned_col(IN_SHARD * d) for d in range(N_DEV + 1))
SLAB_BLK0 = tuple(b // LANES for b in SLAB_BOUND[:-1])
SLAB_SHIFT = tuple(b % LANES for b in SLAB_BOUND[:-1])
SLAB_BLOCKS = max(-(-SLAB_BOUND[d + 1] // LANES) - SLAB_BLK0[d] for d in range(N_DEV))
SLAB_W = SLAB_BLOCKS * LANES
GLR_DEV = GLR_OFF // IN_SHARD
GLR_LOCAL = GLR_OFF - GLR_DEV * IN_SHARD
assert all(SLAB_BLK0[d] + SLAB_BLOCKS <= AL_COLS // LANES for d in range(N_DEV))
VMEM_LIMIT = 58 * 1024 * 1024
ADAM_LR, ADAM_B1, ADAM_B2, ADAM_EPS, ADAM_WD, ADAM_STEP = 0.001, 0.9, 0.999, 1e-08, 0.01, 10
ANY = pl.BlockSpec(memory_space=pl.ANY)
MESH = pl.DeviceIdType.MESH


def _call(body, name, **kw):
    return pl.pallas_call(body, name=name, **kw)


def _params(sem=None):
    return pltpu.CompilerParams(dimension_semantics=sem, vmem_limit_bytes=VMEM_LIMIT)


def _mm(a, b):
    return jnp.dot(a, b, preferred_element_type=F32)


def _mm_nt(a, b):
    return lax.dot_general(a, b, (((1,), (1,)), ((), ())), preferred_element_type=F32)


def _mm_tn(a, b):
    return lax.dot_general(a, b, (((0,), (0,)), ((), ())), preferred_element_type=F32)


def _sigmoid(x):
    return 1.0 / (1.0 + jnp.exp(-x))


def _rope(t, cos, sin):
    half = t.shape[-1] // 2
    t1, t2 = t[:, :half], t[:, half:]
    return jnp.concatenate([t1 * cos - t2 * sin, t2 * cos + t1 * sin], axis=-1)


def _rope_bwd(g, cos, sin):
    half = g.shape[-1] // 2
    g1, g2 = g[:, :half], g[:, half:]
    return jnp.concatenate([g1 * cos + g2 * sin, g2 * cos - g1 * sin], axis=-1)


def _row_mean(x):
    return jnp.mean(x, axis=-1, keepdims=True)


def _col_sum(x):
    return jnp.sum(x, axis=0, keepdims=True)


def _tile_rows(head_ref, x_ref):
    return jnp.where(pl.program_id(0) == 0, head_ref[...], x_ref[...])


def _head_spec():
    return pl.BlockSpec((TILE, D_MODEL), lambda i: (0, 0))


def _x_spec():
    return pl.BlockSpec((TILE, D_MODEL), lambda i: (jnp.maximum(i - 1, 0), 0))


def _slab_plan():
    interior, shared = [], []
    for d in range(N_DEV):
        lo, hi = -(-SLAB_BOUND[d] // LANES), SLAB_BOUND[d + 1] // LANES
        interior.append((d, LANES * (lo - SLAB_BLK0[d]), LANES * lo, LANES * (hi - lo)))
        if d + 1 < N_DEV and SLAB_BOUND[d + 1] % LANES:
            shared.append((hi, d, hi - SLAB_BLK0[d]))
    return interior, shared


W_SCRATCH = lambda: [pltpu.VMEM((D_MODEL, AL_COLS), BF16), pltpu.VMEM((D_MODEL, LANES), BF16),
                     pltpu.VMEM((2 * (N_DEV - 1), D_MODEL, LANES), BF16), pltpu.SemaphoreType.DMA((3 * N_DEV,))]


def _load_weight(slabs_hbm, wg_hbm, w_vm, wg_vm, edge_vm, sem):
    interior, shared = _slab_plan()
    copies = [pltpu.make_async_copy(wg_hbm, wg_vm, sem.at[0])]
    for d, src, dst, width in interior:
        copies.append(pltpu.make_async_copy(slabs_hbm.at[d, :, pl.ds(src, width)], w_vm.at[:, pl.ds(dst, width)], sem.at[1 + d]))
    for n, (_, d, blk) in enumerate(shared):
        copies.append(pltpu.make_async_copy(slabs_hbm.at[d, :, pl.ds(LANES * blk, LANES)], edge_vm.at[2 * n], sem.at[1 + N_DEV + 2 * n]))
        copies.append(pltpu.make_async_copy(slabs_hbm.at[d + 1, :, pl.ds(0, LANES)], edge_vm.at[2 * n + 1], sem.at[2 + N_DEV + 2 * n]))
    for cp in copies:
        cp.start()
    for cp in copies:
        cp.wait()
    for n, (blk, _, _) in enumerate(shared):
        w_vm[:, LANES * blk:LANES * (blk + 1)] = edge_vm[2 * n] + edge_vm[2 * n + 1]


def _inproj_fwd(head, x, g_norm, slabs, w_glr):
    t_rows = x.shape[0] + TILE
    nt = t_rows // TILE

    def body(head_ref, x_ref, g_ref, slabs_hbm, wg_hbm, ut_ref, *rest):
        seg_refs, glr_ref = rest[:10], rest[10]
        w_vm, wg_vm, edge_vm, sem = rest[11:]

        @pl.when(pl.program_id(0) == 0)
        def _():
            _load_weight(slabs_hbm, wg_hbm, w_vm, wg_vm, edge_vm, sem)

        x = _tile_rows(head_ref, x_ref)
        r = lax.rsqrt(_row_mean(x * x) + EPS)
        u32 = (x * r * g_ref[...]).astype(BF16).astype(F32)
        u = u32.astype(BF16)
        ut_ref[...] = u32.T.astype(BF16)
        for s, o_ref in enumerate(seg_refs):
            o_ref[...] = _mm(u, w_vm[:, SEG_OFF[s]:SEG_OFF[s] + SEG_W[s]]).astype(BF16)
        glr_ref[...] = _mm(u, wg_vm[...])

    row = lambda w: pl.BlockSpec((TILE, w), lambda i: (i, 0))
    out_shape = ([jax.ShapeDtypeStruct((nt, D_MODEL, TILE), BF16)] + [jax.ShapeDtypeStruct((t_rows, w), BF16) for w in SEG_W]
                 + [jax.ShapeDtypeStruct((t_rows, LANES), F32)])
    out_specs = [pl.BlockSpec((None, D_MODEL, TILE), lambda i: (i, 0, 0))] + [row(w) for w in SEG_W] + [row(LANES)]
    outs = _call(
        body, "inproj_fwd", grid=(nt,), out_shape=out_shape,
        in_specs=[_head_spec(), _x_spec(), pl.BlockSpec((1, D_MODEL), lambda i: (0, 0)), ANY, ANY],
        out_specs=out_specs, scratch_shapes=W_SCRATCH(),
        compiler_params=_params(("arbitrary",)),
    )(head, x, g_norm, slabs, w_glr)
    return outs[0], dict(zip(SEG_NAMES, outs[1:11])), outs[11]


def _ret_decay(lgh):
    i = lax.broadcasted_iota(jnp.int32, (TILE, TILE), 0)
    j = lax.broadcasted_iota(jnp.int32, (TILE, TILE), 1)
    rel = (i - j).astype(F32)
    return jnp.where(rel >= 0, jnp.exp(jnp.maximum(rel, 0.0) * lgh), 0.0)


def _ret_vectors(lgh):
    idx = lax.broadcasted_iota(jnp.int32, (TILE, 1), 0).astype(F32)
    xi = jnp.exp((idx + 1.0) * lgh)
    zeta = jnp.exp((TILE - 1.0 - idx) * lgh)
    gc = jnp.exp(jnp.full((1, 1), float(TILE), F32) * lgh)
    return xi, zeta, gc


def _ret_fwd(seg, cos, sin, gain, lg):
    t_rows = cos.shape[0]
    nt = t_rows // TILE

    def body(lg_ref, q_ref, k_ref, v_ref, g_ref, cos_ref, sin_ref, gain_ref, oraw_ref, oret_ref, st_ref, s_acc, dm):
        h, t = pl.program_id(0), pl.program_id(1)
        lgh = lg_ref[h]

        @pl.when(t == 0)
        def _():
            s_acc[...] = jnp.zeros_like(s_acc)
            dm[...] = _ret_decay(lgh)

        cos_t, sin_t = cos_ref[...], sin_ref[...]
        q = _rope(q_ref[...].astype(F32), cos_t, sin_t)
        k = _rope(k_ref[...].astype(F32), cos_t, sin_t) * (RET_QK ** -0.5)
        xi, zeta, gc = _ret_vectors(lgh)
        v = v_ref[...]
        s_in = s_acc[...]
        p = (_mm_nt(q.astype(BF16), k.astype(BF16)) * dm[...]).astype(BF16)
        o = _mm(p, v) + _mm((q * xi).astype(BF16), s_in.astype(BF16))
        st_ref[...] = s_in.astype(BF16)
        s_acc[...] = s_in * gc + _mm_tn((k * zeta).astype(BF16), v)
        oraw_ref[...] = o
        oc = o - _row_mean(o)
        n = oc * lax.rsqrt(_row_mean(oc * oc) + EPS) * gain_ref[...]
        g = g_ref[...].astype(F32)
        oret_ref[...] = (n * g * _sigmoid(g)).astype(BF16)

    blk = lambda w: pl.BlockSpec((TILE, w), lambda h, t: (t, h))
    tab = pl.BlockSpec((TILE, LANES), lambda h, t: (t, 0))
    return _call(
        body, "ret_fwd", grid=(RET_HEADS, nt),
        out_shape=[jax.ShapeDtypeStruct((t_rows, RET_W), F32), jax.ShapeDtypeStruct((t_rows, RET_W), BF16),
                   jax.ShapeDtypeStruct((RET_HEADS, nt, RET_QK, RET_V), BF16)],
        in_specs=[pl.BlockSpec(memory_space=pltpu.SMEM), blk(RET_QK), blk(RET_QK), blk(RET_V), blk(RET_V), tab, tab,
                  pl.BlockSpec((1, RET_V), lambda h, t: (0, h))],
        out_specs=[blk(RET_V), blk(RET_V), pl.BlockSpec((None, None, RET_QK, RET_V), lambda h, t: (h, t, 0, 0))],
        scratch_shapes=[pltpu.VMEM((RET_QK, RET_V), F32), pltpu.VMEM((TILE, TILE), F32)],
        compiler_params=_params(("arbitrary", "arbitrary")),
    )(lg, seg["rq"], seg["rk"], seg["rv"], seg["rg"], cos, sin, gain)


def _ret_bwd(seg, cos, sin, gain, lg, o_raw, do_ret, states):
    t_rows = cos.shape[0]
    nt = t_rows // TILE

    def body(lg_ref, q_ref, k_ref, v_ref, g_ref, cos_ref, sin_ref, gain_ref, oraw_ref, do_ref, st_ref,
             dq_ref, dk_ref, dv_ref, dg_ref, dgain_ref, e_acc, dm):
        h, j = pl.program_id(0), pl.program_id(1)
        lgh = lg_ref[h]

        @pl.when(j == 0)
        def _():
            e_acc[...] = jnp.zeros_like(e_acc)
            dm[...] = _ret_decay(lgh)
            dgain_ref[...] = jnp.zeros_like(dgain_ref)

        cos_t, sin_t = cos_ref[...], sin_ref[...]
        q = _rope(q_ref[...].astype(F32), cos_t, sin_t)
        k = _rope(k_ref[...].astype(F32), cos_t, sin_t) * (RET_QK ** -0.5)
        xi, zeta, gc = _ret_vectors(lgh)
        v = v_ref[...]
        g = g_ref[...].astype(F32)
        o = oraw_ref[...]
        do = do_ref[...].astype(F32)
        oc = o - _row_mean(o)
        rstd = lax.rsqrt(_row_mean(oc * oc) + EPS)
        xh = oc * rstd
        gain_t = gain_ref[...]
        sg = _sigmoid(g)
        dn = do * (g * sg)
        dg_ref[...] = (do * (xh * gain_t) * (sg * (1.0 + g * (1.0 - sg)))).astype(BF16)
        dgain_ref[...] += _col_sum(dn * xh)
        dxh = dn * gain_t
        dob = (rstd * (dxh - _row_mean(dxh) - xh * _row_mean(dxh * xh))).astype(BF16)
        dmat = dm[...]
        qb, kb = q.astype(BF16), k.astype(BF16)
        p = (_mm_nt(qb, kb) * dmat).astype(BF16)
        dp = (_mm_nt(dob, v) * dmat).astype(BF16)
        s_in = st_ref[...]
        e_in = e_acc[...]
        e_b = e_in.astype(BF16)
        dq = _mm(dp, kb) + _mm_nt(dob, s_in) * xi
        dk = _mm_tn(dp, qb) + _mm_nt(v, e_b) * zeta
        dv_ref[...] = (_mm_tn(p, dob) + _mm((k * zeta).astype(BF16), e_b)).astype(BF16)
        e_acc[...] = e_in * gc + _mm_tn((q * xi).astype(BF16), dob)
        dq_ref[...] = _rope_bwd(dq, cos_t, sin_t).astype(BF16)
        dk_ref[...] = (_rope_bwd(dk, cos_t, sin_t) * (RET_QK ** -0.5)).astype(BF16)

    blk = lambda w: pl.BlockSpec((TILE, w), lambda h, j: (nt - 1 - j, h))
    tab = pl.BlockSpec((TILE, LANES), lambda h, j: (nt - 1 - j, 0))
    vec = pl.BlockSpec((1, RET_V), lambda h, j: (0, h))
    return _call(
        body, "ret_bwd", grid=(RET_HEADS, nt),
        out_shape=[jax.ShapeDtypeStruct((t_rows, RET_HEADS * RET_QK), BF16), jax.ShapeDtypeStruct((t_rows, RET_HEADS * RET_QK), BF16),
                   jax.ShapeDtypeStruct((t_rows, RET_W), BF16), jax.ShapeDtypeStruct((t_rows, RET_W), BF16),
                   jax.ShapeDtypeStruct((1, RET_W), F32)],
        in_specs=[pl.BlockSpec(memory_space=pltpu.SMEM), blk(RET_QK), blk(RET_QK), blk(RET_V), blk(RET_V), tab, tab, vec,
                  blk(RET_V), blk(RET_V), pl.BlockSpec((None, None, RET_QK, RET_V), lambda h, j: (h, nt - 1 - j, 0, 0))],
        out_specs=[blk(RET_QK), blk(RET_QK), blk(RET_V), blk(RET_V), vec],
        scratch_shapes=[pltpu.VMEM((RET_QK, RET_V), F32), pltpu.VMEM((TILE, TILE), F32)],
        compiler_params=_params(("arbitrary", "arbitrary")),
    )(lg, seg["rq"], seg["rk"], seg["rv"], seg["rg"], cos, sin, gain, o_raw, do_ret, states)


GLA_LEVELS = (32, 64, 128, 256)
N_TERMS = 1 + len(GLA_LEVELS)


def _gla_tables():
    p = jnp.arange(TILE)[:, None]
    r = jnp.arange(TILE)[None, :]
    masks = [(p // GLA_CHUNK == r // GLA_CHUNK) & (r <= p)]
    for blk in GLA_LEVELS:
        masks.append((p // blk == r // blk) & (p % blk >= blk // 2) & (r % blk < blk // 2))
    masks = jnp.stack(masks + [m.T for m in masks]).astype(F32)
    cum_fwd = jnp.concatenate([r <= p, masks[0] > 0], axis=0).astype(BF16)
    cum_bwd = jnp.concatenate([r >= p, masks[N_TERMS] > 0], axis=1).astype(BF16)
    return masks, cum_fwd, cum_bwd


def _split3(x):
    hi = x.astype(BF16)
    rest = x - hi.astype(F32)
    mid = rest.astype(BF16)
    lo = (rest - mid.astype(F32)).astype(BF16)
    return jnp.concatenate([hi, mid, lo], axis=1)


def _join3(y):
    w = y.shape[1] // 3
    return (y[:, 2 * w:] + y[:, w:2 * w]) + y[:, :w]


def _gla_prep(q_ref, k_ref, glr_ref, wgu_ref, b_ref, cum_ref, g_scr, ref_scr):
    z = _mm(glr_ref[...].astype(BF16), wgu_ref[...].astype(BF16)) + b_ref[...]
    la = (jnp.minimum(z, 0.0) - jnp.log(1.0 + jnp.exp(-jnp.abs(z)))) / GLA_TAU
    gb = _join3(_mm(cum_ref[...], _split3(la)))
    g, b = gb[:TILE], gb[TILE:]
    g_scr[...] = g
    factors = [(jnp.exp(b), jnp.exp(-b))]
    for lvl, blk in enumerate(GLA_LEVELS):
        for n in range(TILE // blk):
            ref_scr[lvl, n * blk:(n + 1) * blk, :] = jnp.broadcast_to(g_scr[pl.ds(n * blk + blk // 2 - 1, 1), :], (blk, GLA_K))
        x = g - ref_scr[lvl]
        factors.append((jnp.exp(jnp.minimum(x, 0.0)), jnp.exp(jnp.minimum(-x, 0.0))))
    g_last = g_scr[pl.ds(TILE - 1, 1), :]
    q = q_ref[...].astype(F32) * (GLA_K ** -0.5)
    k = k_ref[...].astype(F32)
    return z, q, k, factors, jnp.exp(g), jnp.exp(g_last), jnp.exp(g_last - g)


def _gla_scores(q, k, factors, m_ref):
    a = jnp.zeros((TILE, TILE), F32)
    for l, (fq, fk) in enumerate(factors):
        s = _mm_nt((q * fq).astype(BF16), (k * fk).astype(BF16))
        a = jnp.where(m_ref[l] > 0.0, s, a)
    return a


def _gla_fwd(seg, glr, wgu_pad, b_gate, gain, masks, cum_fwd):
    t_rows = glr.shape[0]
    nt = t_rows // TILE

    def body(q_ref, k_ref, v_ref, g_ref, glr_ref, wgu_ref, b_ref, gain_ref, m_ref, cum_ref, oraw_ref, ogla_ref, st_ref,
             s_acc, g_scr, ref_scr):
        @pl.when(pl.program_id(1) == 0)
        def _():
            s_acc[...] = jnp.zeros_like(s_acc)

        _, q, k, factors, e_g, e_last, e_end = _gla_prep(q_ref, k_ref, glr_ref, wgu_ref, b_ref, cum_ref, g_scr, ref_scr)
        v = v_ref[...]
        st = s_acc[...]
        st_ref[...] = st
        a = _gla_scores(q, k, factors, m_ref)
        o = _mm(a.astype(BF16), v) + _mm_nt((q * e_g).astype(BF16), st.astype(BF16))
        s_acc[...] = st * e_last + _mm(v.astype(F32).T.astype(BF16), (k * e_end).astype(BF16))
        oraw_ref[...] = o
        n = o * lax.rsqrt(_row_mean(o * o) + EPS) * gain_ref[...]
        g = g_ref[...].astype(F32)
        ogla_ref[...] = (n * g * _sigmoid(g)).astype(BF16)

    blk = lambda w: pl.BlockSpec((TILE, w), lambda h, t: (t, h))
    return _call(
        body, "gla_fwd", grid=(GLA_HEADS, nt),
        out_shape=[jax.ShapeDtypeStruct((t_rows, GLA_W), F32), jax.ShapeDtypeStruct((t_rows, GLA_W), BF16),
                   jax.ShapeDtypeStruct((GLA_HEADS, nt, GLA_V, GLA_K), F32)],
        in_specs=[blk(GLA_K), blk(GLA_K), blk(GLA_V), blk(GLA_V), pl.BlockSpec((TILE, LANES), lambda h, t: (t, 0)),
                  pl.BlockSpec((LANES, GLA_K), lambda h, t: (0, h)), pl.BlockSpec((1, GLA_K), lambda h, t: (0, h)),
                  pl.BlockSpec((1, GLA_V), lambda h, t: (0, h)),
                  pl.BlockSpec((N_TERMS, TILE, TILE), lambda h, t: (0, 0, 0)), pl.BlockSpec((2 * TILE, TILE), lambda h, t: (0, 0))],
        out_specs=[blk(GLA_V), blk(GLA_V), pl.BlockSpec((None, None, GLA_V, GLA_K), lambda h, t: (h, t, 0, 0))],
        scratch_shapes=[pltpu.VMEM((GLA_V, GLA_K), F32), pltpu.VMEM((TILE, GLA_K), F32),
                        pltpu.VMEM((len(GLA_LEVELS), TILE, GLA_K), F32)],
        compiler_params=_params(("arbitrary", "arbitrary")),
    )(seg["gq"], seg["gk"], seg["gv"], seg["gg"], glr, wgu_pad, b_gate, gain, masks, cum_fwd)


def _gla_bwd(seg, glr, wgu_pad, b_gate, gain, o_raw, do_gla, states, masks, cum_fwd, cum_bwd):
    t_rows = glr.shape[0]
    nt = t_rows // TILE

    def body(q_ref, k_ref, v_ref, g_ref, glr_ref, wgu_ref, b_ref, gain_ref, m_ref, cum_ref, cumb_ref, oraw_ref, do_ref, st_ref,
             dq_ref, dk_ref, dv_ref, dg_ref, dglr_ref, dwgu_ref, dbg_ref, dgain_ref, d_acc, g_scr, ref_scr, dref_scr):
        @pl.when(pl.program_id(1) == 0)
        def _():
            d_acc[...] = jnp.zeros_like(d_acc)
            dwgu_ref[...] = jnp.zeros_like(dwgu_ref)
            dbg_ref[...] = jnp.zeros_like(dbg_ref)
            dgain_ref[...] = jnp.zeros_like(dgain_ref)

        z, q, k, factors, e_g, e_last, e_end = _gla_prep(q_ref, k_ref, glr_ref, wgu_ref, b_ref, cum_ref, g_scr, ref_scr)
        v = v_ref[...]
        o = oraw_ref[...]
        do = do_ref[...].astype(F32)
        g = g_ref[...].astype(F32)
        rinv = lax.rsqrt(_row_mean(o * o) + EPS)
        nh = o * rinv
        gain_t = gain_ref[...]
        sg = _sigmoid(g)
        dn = do * (g * sg)
        dg_ref[...] = (do * (nh * gain_t) * (sg * (1.0 + g * (1.0 - sg)))).astype(BF16)
        dgain_ref[...] += _col_sum(dn * nh)
        dnh = dn * gain_t
        dor = rinv * (dnh - nh * _row_mean(dnh * nh))
        dob = dor.astype(BF16)
        a_t = _gla_scores(q, k, factors, m_ref).T.astype(BF16)
        da = _mm_nt(dob, v)
        da_t = _mm_nt(v, dob)
        st_in = st_ref[...]
        d_out = d_acc[...]
        d_out_b = d_out.astype(BF16)
        qg, kg = q * e_g, k * e_end
        dqg = _mm(dob, st_in.astype(BF16))
        dkg = _mm(v, d_out_b)
        dv_ref[...] = (_mm(a_t, dob) + _mm_nt(kg.astype(BF16), d_out_b)).astype(BF16)
        d_acc[...] = d_out * e_last + _mm(dor.T.astype(BF16), qg.astype(BF16))
        dq = dqg * e_g
        dk = dkg * e_end
        dkg_kg = dkg * kg
        dg_cum = dqg * qg - dkg_kg
        db = None
        for l, (fq, fk) in enumerate(factors):
            qt, kt = q * fq, k * fk
            dqt = _mm(jnp.where(m_ref[l] > 0.0, da, 0.0).astype(BF16), kt.astype(BF16))
            dkt = _mm(jnp.where(m_ref[N_TERMS + l] > 0.0, da_t, 0.0).astype(BF16), qt.astype(BF16))
            dq = dq + dqt * fq
            dk = dk + dkt * fk
            diff = dqt * qt - dkt * kt
            if l == 0:
                db = diff
            else:
                dg_cum = dg_cum + diff
                dref_scr[l - 1] = diff
        dq_ref[...] = (dq * (GLA_K ** -0.5)).astype(BF16)
        dk_ref[...] = dk.astype(BF16)
        g_scr[...] = dg_cum
        g_scr[pl.ds(TILE - 1, 1), :] += e_last * _col_sum(d_out * st_in) + _col_sum(dkg_kg)
        for lvl, blk in enumerate(GLA_LEVELS):
            for n in range(TILE // blk):
                g_scr[pl.ds(n * blk + blk // 2 - 1, 1), :] -= _col_sum(dref_scr[lvl, n * blk:(n + 1) * blk, :])
        dla = _join3(_mm(cumb_ref[...], jnp.concatenate([_split3(g_scr[...]), _split3(db)], axis=0)))
        dz = dla * (1.0 / GLA_TAU) * _sigmoid(-z)
        dzb = dz.astype(BF16)
        dglr_ref[...] = _mm_nt(dzb, wgu_ref[...].astype(BF16))
        dwgu_ref[...] += _mm(glr_ref[...].T.astype(BF16), dzb)
        dbg_ref[...] += _col_sum(dz)

    blk = lambda w: pl.BlockSpec((TILE, w), lambda h, j: (nt - 1 - j, h))
    vec = lambda w: pl.BlockSpec((1, w), lambda h, j: (0, h))
    wspec = pl.BlockSpec((LANES, GLA_K), lambda h, j: (0, h))
    return _call(
        body, "gla_bwd", grid=(GLA_HEADS, nt),
        out_shape=[jax.ShapeDtypeStruct((t_rows, GLA_HEADS * GLA_K), BF16), jax.ShapeDtypeStruct((t_rows, GLA_HEADS * GLA_K), BF16),
                   jax.ShapeDtypeStruct((t_rows, GLA_W), BF16), jax.ShapeDtypeStruct((t_rows, GLA_W), BF16),
                   jax.ShapeDtypeStruct((GLA_HEADS, t_rows, LANES), F32), jax.ShapeDtypeStruct((LANES, GLA_HEADS * GLA_K), F32),
                   jax.ShapeDtypeStruct((1, GLA_HEADS * GLA_K), F32), jax.ShapeDtypeStruct((1, GLA_W), F32)],
        in_specs=[blk(GLA_K), blk(GLA_K), blk(GLA_V), blk(GLA_V), pl.BlockSpec((TILE, LANES), lambda h, j: (nt - 1 - j, 0)),
                  wspec, vec(GLA_K), vec(GLA_V),
                  pl.BlockSpec((2 * N_TERMS, TILE, TILE), lambda h, j: (0, 0, 0)), pl.BlockSpec((2 * TILE, TILE), lambda h, j: (0, 0)),
                  pl.BlockSpec((TILE, 2 * TILE), lambda h, j: (0, 0)), blk(GLA_V), blk(GLA_V),
                  pl.BlockSpec((None, None, GLA_V, GLA_K), lambda h, j: (h, nt - 1 - j, 0, 0))],
        out_specs=[blk(GLA_K), blk(GLA_K), blk(GLA_V), blk(GLA_V),
                   pl.BlockSpec((None, TILE, LANES), lambda h, j: (h, nt - 1 - j, 0)), wspec, vec(GLA_K), vec(GLA_V)],
        scratch_shapes=[pltpu.VMEM((GLA_V, GLA_K), F32), pltpu.VMEM((TILE, GLA_K), F32),
                        pltpu.VMEM((len(GLA_LEVELS), TILE, GLA_K), F32), pltpu.VMEM((len(GLA_LEVELS), TILE, GLA_K), F32)],
        compiler_params=_params(("arbitrary", "arbitrary")),
    )(seg["gq"], seg["gk"], seg["gv"], seg["gg"], glr, wgu_pad, b_gate, gain, masks, cum_fwd, cum_bwd, o_raw, do_gla, states)


def _merge_fwd_bwd(o_ret, o_gla, seg, x, target, g_final, w_br, w_bg, w_out):
    t_rows = x.shape[0] + TILE
    nt = t_rows // TILE

    def body(oret_ref, ogla_ref, mr_ref, mg_ref, h0_ref, tgt_ref, gf_ref, wbr_hbm, wbg_hbm, wout_hbm,
             dh1_ref, dmr_ref, dmg_ref, doret_ref, dogla_ref, loss_ref, dgf_ref, dwbr_hbm, dwbg_hbm, dwout_hbm,
             wbr, wbg, wout, abr, abg, aout, sem):
        i = pl.program_id(0)

        @pl.when(i == 0)
        def _():
            cps = [pltpu.make_async_copy(s, d, sem.at[n]) for n, (s, d) in enumerate(((wbr_hbm, wbr), (wbg_hbm, wbg), (wout_hbm, wout)))]
            for cp in cps:
                cp.start()
            abr[...] = jnp.zeros_like(abr)
            abg[...] = jnp.zeros_like(abg)
            aout[...] = jnp.zeros_like(aout)
            loss_ref[...] = jnp.zeros_like(loss_ref)
            dgf_ref[...] = jnp.zeros_like(dgf_ref)
            for cp in cps:
                cp.wait()
            dh1_ref[...] = jnp.zeros_like(dh1_ref)
            dmr_ref[...] = jnp.zeros_like(dmr_ref)
            dmg_ref[...] = jnp.zeros_like(dmg_ref)
            doret_ref[...] = jnp.zeros_like(doret_ref)
            dogla_ref[...] = jnp.zeros_like(dogla_ref)

        @pl.when(i > 0)
        def _():
            oret, ogla = oret_ref[...], ogla_ref[...]
            br, bg = _mm(oret, wbr[...]), _mm(ogla, wbg[...])
            sr, sg = _sigmoid(mr_ref[...].astype(F32)), _sigmoid(mg_ref[...].astype(F32))
            mb = (sr * br + sg * bg).astype(BF16)
            h1 = h0_ref[...] + _mm(mb, wout[...])
            r2 = lax.rsqrt(_row_mean(h1 * h1) + EPS)
            hn = h1 * r2
            gf = gf_ref[...]
            diff = hn * gf - tgt_ref[...]
            loss_ref[...] += 0.5 * jnp.sum(_row_mean(diff * diff))
            dy = diff * (1.0 / D_MODEL)
            dgf_ref[...] += _col_sum(dy * hn)
            dyg = dy * gf
            dh1 = r2 * (dyg - hn * _row_mean(dyg * hn))
            dh1_ref[...] = dh1
            dh1b = dh1.astype(BF16)
            dm = _mm_nt(dh1b, wout[...])
            aout[...] += _mm_tn(mb, dh1b)
            dbr = (dm * sr).astype(BF16)
            dbg = (dm * sg).astype(BF16)
            dmr_ref[...] = (dm * br * sr * (1.0 - sr)).astype(BF16)
            dmg_ref[...] = (dm * bg * sg * (1.0 - sg)).astype(BF16)
            doret_ref[...] = _mm_nt(dbr, wbr[...]).astype(BF16)
            dogla_ref[...] = _mm_nt(dbg, wbg[...]).astype(BF16)
            abr[...] += _mm_tn(oret, dbr)
            abg[...] += _mm_tn(ogla, dbg)

        @pl.when(i == nt - 1)
        def _():
            pltpu.sync_copy(abr, dwbr_hbm)
            pltpu.sync_copy(abg, dwbg_hbm)
            pltpu.sync_copy(aout, dwout_hbm)

    row = lambda w: pl.BlockSpec((TILE, w), lambda i: (i, 0))
    one = lambda w: pl.BlockSpec((1, w), lambda i: (0, 0))
    return _call(
        body, "merge_fwd_bwd", grid=(nt,),
        out_shape=[jax.ShapeDtypeStruct((t_rows, D_MODEL), F32), jax.ShapeDtypeStruct((t_rows, D_MODEL), BF16),
                   jax.ShapeDtypeStruct((t_rows, D_MODEL), BF16), jax.ShapeDtypeStruct((t_rows, RET_W), BF16),
                   jax.ShapeDtypeStruct((t_rows, GLA_W), BF16), jax.ShapeDtypeStruct((1, LANES), F32),
                   jax.ShapeDtypeStruct((1, D_MODEL), F32), jax.ShapeDtypeStruct((RET_W, D_MODEL), F32),
                   jax.ShapeDtypeStruct((GLA_W, D_MODEL), F32), jax.ShapeDtypeStruct((D_MODEL, D_MODEL), F32)],
        in_specs=[row(RET_W), row(GLA_W), row(D_MODEL), row(D_MODEL), _x_spec(), _x_spec(), one(D_MODEL), ANY, ANY, ANY],
        out_specs=[row(D_MODEL), row(D_MODEL), row(D_MODEL), row(RET_W), row(GLA_W), one(LANES), one(D_MODEL), ANY, ANY, ANY],
        scratch_shapes=[pltpu.VMEM((RET_W, D_MODEL), BF16), pltpu.VMEM((GLA_W, D_MODEL), BF16), pltpu.VMEM((D_MODEL, D_MODEL), BF16),
                        pltpu.VMEM((RET_W, D_MODEL), F32), pltpu.VMEM((GLA_W, D_MODEL), F32), pltpu.VMEM((D_MODEL, D_MODEL), F32),
                        pltpu.SemaphoreType.DMA((3,))],
        compiler_params=_params(("arbitrary",)),
    )(o_ret, o_gla, seg["mr"], seg["mg"], x, target, g_final, w_br, w_bg, w_out)


def _inproj_bwd_x(dseg, dglr_parts, head, x, dh1, g_norm, slabs, w_glr):
    t_rows = x.shape[0] + TILE
    nt = t_rows // TILE

    def body(*refs):
        d_refs = refs[:10]
        (dglr_ref, head_ref, x_ref, dh1_ref, g_ref, slabs_hbm, wg_hbm, dx_ref, dhead_ref, dgn_ref, dglr_out,
         w_vm, wg_vm, edge_vm, sem) = refs[10:]

        @pl.when(pl.program_id(0) == 0)
        def _():
            dgn_ref[...] = jnp.zeros_like(dgn_ref)
            _load_weight(slabs_hbm, wg_hbm, w_vm, wg_vm, edge_vm, sem)

        dglr = (dglr_ref[0] + dglr_ref[1] + dglr_ref[2] + dglr_ref[3]).astype(BF16)
        dglr_out[...] = dglr
        du = _mm_nt(dglr, wg_vm[...])
        for s, d_ref in enumerate(d_refs):
            du = du + _mm_nt(d_ref[...], w_vm[:, SEG_OFF[s]:SEG_OFF[s] + SEG_W[s]])
        x = _tile_rows(head_ref, x_ref)
        r = lax.rsqrt(_row_mean(x * x) + EPS)
        hn = x * r
        dgn_ref[...] += _col_sum(du * hn)
        dug = du * g_ref[...]
        dh0 = dh1_ref[...] + r * (dug - hn * _row_mean(dug * hn))
        dx_ref[...] = dh0

        @pl.when(pl.program_id(0) == 0)
        def _():
            dhead_ref[...] = dh0

    row = lambda w: pl.BlockSpec((TILE, w), lambda i: (i, 0))
    one = pl.BlockSpec((1, D_MODEL), lambda i: (0, 0))
    return _call(
        body, "inproj_bwd_x", grid=(nt,),
        out_shape=[jax.ShapeDtypeStruct((t_rows - TILE, D_MODEL), F32), jax.ShapeDtypeStruct((TILE, D_MODEL), F32),
                   jax.ShapeDtypeStruct((1, D_MODEL), F32), jax.ShapeDtypeStruct((t_rows, LANES), BF16)],
        in_specs=[row(w) for w in SEG_W] + [pl.BlockSpec((GLA_HEADS, TILE, LANES), lambda i: (0, i, 0)),
                                            _head_spec(), _x_spec(), row(D_MODEL), one, ANY, ANY],
        out_specs=[_x_spec(), _head_spec(), one, row(LANES)],
        scratch_shapes=W_SCRATCH(),
        compiler_params=_params(("arbitrary",)),
    )(*[dseg[n] for n in SEG_NAMES], dglr_parts, head, x, dh1, g_norm, slabs, w_glr)


W_TILE = 512


def _inproj_bwd_w(ut, dseg, dglr):
    nt = ut.shape[0]
    t_rows = nt * TILE
    kc = 3 if nt % 3 == 0 else 1
    tiles = [(s, c) for s in range(len(SEG_W)) for c in range(0, SEG_W[s], W_TILE)]
    bpt = W_TILE // LANES

    def body(ut_hbm, *refs):
        d_refs, dglr_hbm, out_hbm, oglr_ref = refs[:10], refs[10], refs[11], refs[12]
        ut_vm, dbuf, obuf, acc, gbuf, sem = refs[13:]

        def fetch(i):
            s, c = tiles[i]
            return pltpu.make_async_copy(d_refs[s].at[:, pl.ds(c, W_TILE)], dbuf.at[i % 2], sem.at[1 + i % 2])

        def contract(rhs_ref, width):
            acc[:, :width] = jnp.zeros((D_MODEL, width), F32)

            def step(k, carry):
                part = None
                for j in range(kc):
                    kk = k * kc + j
                    prod = _mm(ut_vm[kk], rhs_ref[pl.ds(pl.multiple_of(kk * TILE, TILE), TILE), :])
                    part = prod if part is None else part + prod
                acc[:, :width] += part
                return carry

            lax.fori_loop(0, nt // kc, step, 0)
            return acc[:, :width]

        load_ut = pltpu.make_async_copy(ut_hbm, ut_vm, sem.at[0])
        load_glr = pltpu.make_async_copy(dglr_hbm, gbuf, sem.at[5])
        load_ut.start()
        load_glr.start()
        fetch(0).start()
        load_ut.wait()
        stores = {}
        for i, (s, c) in enumerate(tiles):
            if i + 1 < len(tiles):
                fetch(i + 1).start()
            fetch(i).wait()
            if i >= 2:
                stores[i - 2].wait()
            total = contract(dbuf.at[i % 2], W_TILE)
            for j in range(bpt):
                obuf[i % 2, j] = total[:, j * LANES:(j + 1) * LANES].astype(BF16)
            blk0 = (SEG_OFF[s] + c) // LANES
            stores[i] = pltpu.make_async_copy(obuf.at[i % 2], out_hbm.at[pl.ds(blk0, bpt)], sem.at[3 + i % 2])
            stores[i].start()
        load_glr.wait()
        oglr_ref[...] = contract(gbuf, LANES)
        for i in range(max(0, len(tiles) - 2), len(tiles)):
            stores[i].wait()

    return _call(
        body, "inproj_bwd_w",
        out_shape=[jax.ShapeDtypeStruct((AL_COLS // LANES, D_MODEL, LANES), BF16), jax.ShapeDtypeStruct((D_MODEL, LANES), F32)],
        in_specs=[ANY] * 12, out_specs=[ANY, pl.BlockSpec(memory_space=pltpu.VMEM)],
        scratch_shapes=[pltpu.VMEM((nt, D_MODEL, TILE), BF16), pltpu.VMEM((2, t_rows, W_TILE), BF16),
                        pltpu.VMEM((2, bpt, D_MODEL, LANES), BF16), pltpu.VMEM((D_MODEL, W_TILE), F32),
                        pltpu.VMEM((t_rows, LANES), BF16), pltpu.SemaphoreType.DMA((6,))],
        compiler_params=_params(),
    )(ut, *[dseg[n] for n in SEG_NAMES], dglr)


def _position():
    x, y, c = lax.axis_index("x"), lax.axis_index("y"), lax.axis_index("c")
    return x, y, c


def _index(px, py, pc):
    return 4 * px + 2 * py + pc


def _all_gather(arrs, name):
    n = len(arrs)

    def body(*refs):
        ins, outs = refs[:n], refs[n:2 * n]
        send_sems, recv_sems, local_sems = refs[2 * n:]
        x, y, c = _position()
        me, sibling = (x, y, c), (x, y, 1 - c)
        chips = [(1 - x, y), (x, 1 - y), (1 - x, 1 - y)]

        def copy(a, k, block, to, src=None):
            dst = outs[a].at[_index(*block)]
            return pltpu.make_async_remote_copy(src_ref=dst if src is None else src, dst_ref=dst,
                                                send_sem=send_sems.at[7 * a + k], recv_sem=recv_sems.at[7 * a + k],
                                                device_id=to, device_id_type=MESH)

        mine = [pltpu.make_async_copy(ins[a], outs[a].at[_index(*me)], local_sems.at[a]) for a in range(n)]
        for cp in mine:
            cp.start()
        first = []
        for a in range(n):
            first.append(copy(a, 0, me, sibling, src=ins[a]))
            first += [copy(a, 1 + j, me, (*chip, c), src=ins[a]) for j, chip in enumerate(chips)]
        for cp in first:
            cp.start()
        passed = []
        for j, chip in enumerate(chips):
            for a in range(n):
                copy(a, 1 + j, (*chip, c), me).wait_recv()
                cp = copy(a, 4 + j, (*chip, c), sibling)
                cp.start()
                passed.append(cp)
        for a in range(n):
            copy(a, 0, sibling, me).wait_recv()
            for j, chip in enumerate(chips):
                copy(a, 4 + j, (*chip, 1 - c), me).wait_recv()
        for cp in first + passed:
            cp.wait_send()
        for cp in mine:
            cp.wait()

    return _call(
        body, name,
        out_shape=[jax.ShapeDtypeStruct((N_DEV, *a.shape), a.dtype) for a in arrs],
        in_specs=[ANY] * n, out_specs=[ANY] * n,
        scratch_shapes=[pltpu.SemaphoreType.DMA((7 * n,)), pltpu.SemaphoreType.DMA((7 * n,)), pltpu.SemaphoreType.DMA((n,))],
    )(*arrs)


N_CHIP = N_DEV // 2


def _slab_block0(owner):
    step = SLAB_BLK0[1]
    assert all(SLAB_BLK0[d] == step * d - (d == N_DEV - 1) for d in range(N_DEV))
    return step * owner - jnp.where(owner == N_DEV - 1, 1, 0)


def _exchange_sibling(dw_blocks, rows_send):
    def body(dw_ref, rows_ref, rin_ref, rrows_ref, send_sems, recv_sems):
        x, y, c = _position()
        copies = []
        for q in range(N_CHIP):
            owner = 2 * q + (1 - c)
            for k, (src, dst) in enumerate(((dw_ref.at[pl.ds(_slab_block0(owner), SLAB_BLOCKS)], rin_ref.at[q]),
                                            (rows_ref.at[owner], rrows_ref.at[q]))):
                copies.append(pltpu.make_async_remote_copy(src_ref=src, dst_ref=dst, send_sem=send_sems.at[2 * q + k],
                                                           recv_sem=recv_sems.at[2 * q + k], device_id=(x, y, 1 - c), device_id_type=MESH))
        for cp in copies:
            cp.start()
        for cp in copies:
            cp.wait()

    return _call(
        body, "exchange_sibling",
        out_shape=[jax.ShapeDtypeStruct((N_CHIP, SLAB_BLOCKS, D_MODEL, LANES), BF16),
                   jax.ShapeDtypeStruct((N_CHIP, *rows_send.shape[1:]), BF16)],
        in_specs=[ANY] * 2, out_specs=[ANY] * 2,
        scratch_shapes=[pltpu.SemaphoreType.DMA((2 * N_CHIP,)), pltpu.SemaphoreType.DMA((2 * N_CHIP,))],
    )(dw_blocks, rows_send)


def _chip_partial(dw_blocks, rows_send, sib_in, sib_rows, core):
    def add(a_ref, b_ref, o_ref):
        o_ref[...] = (a_ref[...].astype(F32) + b_ref[...].astype(F32)).astype(BF16)

    def body_in(c_ref, a_ref, b_ref, o_ref):
        add(a_ref, b_ref, o_ref)

    def body_rows(c_ref, a_ref, b_ref, o_ref):
        add(a_ref, b_ref, o_ref)

    blk = pl.BlockSpec((None, 1, D_MODEL, LANES), lambda q, j, c_ref: (q, j, 0, 0))
    cp_in = _call(
        body_in, "chip_partial_in", out_shape=jax.ShapeDtypeStruct(sib_in.shape, BF16),
        grid_spec=pltpu.PrefetchScalarGridSpec(
            num_scalar_prefetch=1, grid=(N_CHIP, SLAB_BLOCKS),
            in_specs=[pl.BlockSpec((1, D_MODEL, LANES), lambda q, j, c_ref: (_slab_block0(2 * q + c_ref[0]) + j, 0, 0)), blk],
            out_specs=blk),
        compiler_params=_params(("arbitrary", "arbitrary")),
    )(core, dw_blocks, sib_in)
    rows, cols = rows_send.shape[1:]
    rblk = pl.BlockSpec((None, rows, cols), lambda q, c_ref: (q, 0, 0))
    cp_rows = _call(
        body_rows, "chip_partial_rows", out_shape=jax.ShapeDtypeStruct(sib_rows.shape, BF16),
        grid_spec=pltpu.PrefetchScalarGridSpec(
            num_scalar_prefetch=1, grid=(N_CHIP,),
            in_specs=[pl.BlockSpec((None, rows, cols), lambda q, c_ref: (2 * q + c_ref[0], 0, 0)), rblk],
            out_specs=rblk),
        compiler_params=_params(("arbitrary",)),
    )(core, rows_send, sib_rows)
    return cp_in, cp_rows


def _exchange_chips(cp_in, cp_rows):
    def body(in_ref, rows_ref, oin_ref, orows_ref, send_sems, recv_sems, local_sems):
        x, y, c = _position()
        my_chip = 2 * x + y
        arrs = ((in_ref, oin_ref), (rows_ref, orows_ref))
        mine = [pltpu.make_async_copy(src.at[my_chip], dst.at[my_chip], local_sems.at[k]) for k, (src, dst) in enumerate(arrs)]
        for cp in mine:
            cp.start()

        def copy(q, k):
            src, dst = arrs[k]
            return pltpu.make_async_remote_copy(src_ref=src.at[q], dst_ref=dst.at[my_chip], send_sem=send_sems.at[2 * q + k],
                                                recv_sem=recv_sems.at[2 * my_chip + k], device_id=(q // 2, q % 2, c), device_id_type=MESH)

        def arrival(q, k):
            src, dst = arrs[k]
            return pltpu.make_async_remote_copy(src_ref=src.at[q], dst_ref=dst.at[q], send_sem=send_sems.at[2 * q + k],
                                                recv_sem=recv_sems.at[2 * q + k], device_id=(q // 2, q % 2, c), device_id_type=MESH)

        for q in range(N_CHIP):
            @pl.when(q != my_chip)
            def _():
                for k in range(2):
                    copy(q, k).start()
        for q in range(N_CHIP):
            @pl.when(q != my_chip)
            def _():
                for k in range(2):
                    arrival(q, k).wait_recv()
        for q in range(N_CHIP):
            @pl.when(q != my_chip)
            def _():
                for k in range(2):
                    copy(q, k).wait_send()
        for cp in mine:
            cp.wait()

    return _call(
        body, "exchange_chips",
        out_shape=[jax.ShapeDtypeStruct(cp_in.shape, BF16), jax.ShapeDtypeStruct(cp_rows.shape, BF16)],
        in_specs=[ANY] * 2, out_specs=[ANY] * 2,
        scratch_shapes=[pltpu.SemaphoreType.DMA((2 * N_CHIP,)), pltpu.SemaphoreType.DMA((2 * N_CHIP,)), pltpu.SemaphoreType.DMA((2,))],
    )(cp_in, cp_rows)


def _adamw(g, w, m, v):
    m_new = ADAM_B1 * m + (1.0 - ADAM_B1) * g
    v_new = ADAM_B2 * v + (1.0 - ADAM_B2) * (g * g)
    m_hat = m_new / (1.0 - ADAM_B1 ** ADAM_STEP)
    v_hat = v_new / (1.0 - ADAM_B2 ** ADAM_STEP)
    delta = -ADAM_LR * (m_hat / (jnp.sqrt(v_hat) + ADAM_EPS) + ADAM_WD * w)
    return delta, m_new, v_new


def _sum_partials(p_ref):
    g = p_ref[0].astype(F32)
    for d in range(1, p_ref.shape[0]):
        g = g + p_ref[d].astype(F32)
    return g


def _reduce_adam(parts, w, m, v, name, block_rows, row_off=0):
    rows, cols = w.shape
    off = row_off // block_rows

    def body(p_ref, w_ref, m_ref, v_ref, g_ref, d_ref, mo_ref, vo_ref):
        g = _sum_partials(p_ref)
        g_ref[...] = g
        d_ref[...], mo_ref[...], vo_ref[...] = _adamw(g, w_ref[...], m_ref[...], v_ref[...])

    blk = pl.BlockSpec((block_rows, cols), lambda i: (i, 0))
    return _call(
        body, name, grid=(rows // block_rows,),
        out_shape=[jax.ShapeDtypeStruct((rows, cols), F32)] * 4,
        in_specs=[pl.BlockSpec((parts.shape[0], block_rows, cols), lambda i: (0, i + off, 0)), blk, blk, blk],
        out_specs=[blk] * 4,
        compiler_params=_params(("arbitrary",)),
    )(parts, w, m, v)


def _reduce_blocks(parts):
    def body(p_ref, o_ref):
        o_ref[...] = _sum_partials(p_ref)

    return _call(
        body, "reduce_w_in", grid=(SLAB_BLOCKS,), out_shape=jax.ShapeDtypeStruct(parts.shape[1:], F32),
        in_specs=[pl.BlockSpec((parts.shape[0], None, D_MODEL, LANES), lambda j: (0, j, 0, 0))],
        out_specs=pl.BlockSpec((None, D_MODEL, LANES), lambda j: (j, 0, 0)),
        compiler_params=_params(("arbitrary",)),
    )(parts)


def _adam(g, w, m, v, name, block_rows):
    rows, cols = w.shape

    def body(g_ref, w_ref, m_ref, v_ref, d_ref, mo_ref, vo_ref):
        d_ref[...], mo_ref[...], vo_ref[...] = _adamw(g_ref[...], w_ref[...], m_ref[...], v_ref[...])

    blk = pl.BlockSpec((block_rows, cols), lambda i: (i, 0))
    return _call(
        body, name, grid=(rows // block_rows,), out_shape=[jax.ShapeDtypeStruct((rows, cols), F32)] * 3,
        in_specs=[blk] * 4, out_specs=[blk] * 3, compiler_params=_params(("arbitrary",)),
    )(g, w, m, v)


def _reduce_small(parts):
    def body(p_ref, o_ref):
        o_ref[...] = _sum_partials(p_ref)

    return _call(body, "reduce_small", out_shape=jax.ShapeDtypeStruct(parts.shape[1:], F32))(parts)


def _adam_small(g, w, m, v):
    def body(g_ref, w_ref, m_ref, v_ref, d_ref, mo_ref, vo_ref):
        d_ref[...], mo_ref[...], vo_ref[...] = _adamw(g_ref[...], w_ref[...], m_ref[...], v_ref[...])

    return _call(body, "adam_small", out_shape=[jax.ShapeDtypeStruct(g.shape, F32)] * 3)(g, w, m, v)


def _pack_rows(arrs):
    rows = []
    for a in arrs:
        flat = a.reshape(-1).astype(F32)
        pad = (-flat.shape[0]) % LANES
        rows.append(jnp.pad(flat, (0, pad)).reshape(-1, LANES))
    packed = jnp.concatenate(rows, axis=0)
    return jnp.pad(packed, ((0, (-packed.shape[0]) % 8), (0, 0)))


def _unpack_rows(packed, shapes):
    out, r = [], 0
    for shp in shapes:
        size = 1
        for s in shp:
            size *= s
        nrows = -(-size // LANES)
        out.append(packed[r:r + nrows].reshape(-1)[:size].reshape(shp))
        r += nrows
    return out


def _shard_to_slab(shard, d):
    glr = jnp.zeros((D_MODEL, GLA_RANK), shard.dtype)
    if d == GLR_DEV:
        glr = shard[:, GLR_LOCAL:GLR_LOCAL + GLA_RANK]
        shard = jnp.concatenate([shard[:, :GLR_LOCAL], shard[:, GLR_LOCAL + GLA_RANK:]], axis=1)
    return jnp.pad(shard, ((0, 0), (SLAB_SHIFT[d], SLAB_W - SLAB_SHIFT[d] - shard.shape[1]))), glr


def _slab_to_shard(slab, glr, d):
    width = SLAB_BOUND[d + 1] - SLAB_BOUND[d]
    cols = slab[:, SLAB_SHIFT[d]:SLAB_SHIFT[d] + width]
    if d == GLR_DEV:
        cols = jnp.concatenate([cols[:, :GLR_LOCAL], glr, cols[:, GLR_LOCAL:]], axis=1)
    return cols


def kernel(x, meta_tokens, norm_gain, w_in, w_gate_up, b_gate, ret_norm_gain, gla_norm_gain, w_branch_ret, w_branch_gla, w_out, final_norm_gain, loss_target, m_meta_tokens, m_norm_gain, m_w_in, m_w_gate_up, m_b_gate, m_ret_norm_gain, m_gla_norm_gain, m_w_branch_ret, m_w_branch_gla, m_w_out, m_final_norm_gain, v_meta_tokens, v_norm_gain, v_w_in, v_w_gate_up, v_b_gate, v_ret_norm_gain, v_gla_norm_gain, v_w_branch_ret, v_w_branch_gla, v_w_out, v_final_norm_gain):
    xi, yi, ci = _position()
    me = _index(xi, yi, ci)
    seq = x.shape[1]
    t_rows = seq + TILE
    in_shard = w_in.shape[2]
    gu_shard = w_gate_up.shape[2]
    meta_shard = meta_tokens.shape[1]
    ret_rows, gla_rows, out_rows = w_branch_ret.shape[1], w_branch_gla.shape[1], w_out.shape[1]

    assert in_shard == IN_SHARD
    rows_local = jnp.concatenate([w_branch_ret[0], w_branch_gla[0], w_out[0]], axis=0).astype(BF16)
    slab_local, glr_local = lax.switch(me, [functools.partial(_shard_to_slab, d=d) for d in range(N_DEV)], w_in[0])
    small_local = jnp.concatenate([meta_tokens, jnp.pad(w_gate_up[0], ((0, 0), (0, LANES - gu_shard))),
                                   glr_local.reshape(-1, LANES)], axis=0)
    slabs, g_rows, g_small = _all_gather([slab_local.astype(BF16), rows_local, small_local], "all_gather_shards")
    n_small = N_META + GLA_RANK
    w_glr = jnp.pad(g_small[GLR_DEV, n_small:].reshape(D_MODEL, GLA_RANK), ((0, 0), (0, LANES - GLA_RANK))).astype(BF16)
    w_br = g_rows[:, :ret_rows].reshape(RET_W, D_MODEL)
    w_bg = g_rows[:, ret_rows:ret_rows + gla_rows].reshape(GLA_W, D_MODEL)
    w_o = g_rows[:, ret_rows + gla_rows:].reshape(D_MODEL, D_MODEL)
    meta_full = jnp.transpose(g_small[:, :N_META, :], (1, 0, 2)).reshape(N_META, D_MODEL)
    wgu_full = jnp.transpose(g_small[:, N_META:n_small, :gu_shard], (1, 0, 2)).reshape(GLA_RANK, GLA_HEADS * GLA_K)
    wgu_pad = jnp.pad(wgu_full, ((0, LANES - GLA_RANK), (0, 0)))

    pos = jnp.arange(t_rows, dtype=F32) - float(PAD_ROWS)
    half = RET_QK // 2
    inv = ROPE_BASE ** (-jnp.arange(half, dtype=F32) / half)
    ang = pos[:, None] * inv[None, :]
    cos, sin = jnp.cos(ang), jnp.sin(ang)
    lg = jnp.log1p(-(2.0 ** (-5.0 - jnp.arange(RET_HEADS, dtype=F32))))

    head = jnp.concatenate([jnp.zeros((PAD_ROWS, D_MODEL), F32), meta_full], axis=0)
    ut, seg, glr = _inproj_fwd(head, x[0], norm_gain, slabs, w_glr)
    o_ret_raw, o_ret, ret_states = _ret_fwd(seg, cos, sin, ret_norm_gain, lg)
    masks, cum_fwd, cum_bwd = _gla_tables()
    o_gla_raw, o_gla, gla_states = _gla_fwd(seg, glr, wgu_pad, b_gate, gla_norm_gain, masks, cum_fwd)
    (dh1, d_mr, d_mg, do_ret, do_gla, loss_part, d_gfinal, dw_br, dw_bg, dw_o) = _merge_fwd_bwd(
        o_ret, o_gla, seg, x[0], loss_target[0], final_norm_gain.reshape(1, D_MODEL), w_br, w_bg, w_o)

    d_rq, d_rk, d_rv, d_rg, d_gret = _ret_bwd(seg, cos, sin, ret_norm_gain, lg, o_ret_raw, do_ret, ret_states)
    d_gq, d_gk, d_gv, d_gg, dglr_parts, d_wgu, d_bgate, d_ggla = _gla_bwd(
        seg, glr, wgu_pad, b_gate, gla_norm_gain, o_gla_raw, do_gla, gla_states, masks, cum_fwd, cum_bwd)
    dseg = dict(rq=d_rq, rk=d_rk, rv=d_rv, rg=d_rg, gq=d_gq, gk=d_gk, gv=d_gv, gg=d_gg, mr=d_mr, mg=d_mg)
    grad_x, d_head, d_gnorm, dglr = _inproj_bwd_x(dseg, dglr_parts, head, x[0], dh1, norm_gain, slabs, w_glr)
    dw_blocks, dw_glr = _inproj_bwd_w(ut, dseg, dglr)

    send_rows = jnp.concatenate([dw_br.reshape(N_DEV, ret_rows, D_MODEL), dw_bg.reshape(N_DEV, gla_rows, D_MODEL),
                                 dw_o.reshape(N_DEV, out_rows, D_MODEL)], axis=1).astype(BF16)
    sib_in, sib_rows = _exchange_sibling(dw_blocks, send_rows)
    cp_in, cp_rows = _chip_partial(dw_blocks, send_rows, sib_in, sib_rows, ci.astype(jnp.int32).reshape(1))
    p_in, p_rows = _exchange_chips(cp_in, cp_rows)
    small_shapes = [(N_META, D_MODEL), (1, D_MODEL), (GLA_RANK, GLA_HEADS * GLA_K), (1, GLA_HEADS * GLA_K),
                    (1, RET_W), (1, GLA_W), (1, D_MODEL), (1, LANES), (D_MODEL, GLA_RANK)]
    small_part = _pack_rows([d_head[PAD_ROWS:], d_gnorm, d_wgu[:GLA_RANK], d_bgate, d_gret, d_ggla, d_gfinal, loss_part,
                             dw_glr[:, :GLA_RANK]])
    (p_small,) = _all_gather([small_part], "all_gather_small_partials")

    (g_meta_f, g_gnorm, g_wgu_f, g_bgate, g_gret, g_ggla, g_gfinal, loss_all,
     g_wglr) = _unpack_rows(_reduce_small(p_small), small_shapes)
    g_slab = jnp.transpose(_reduce_blocks(p_in), (1, 0, 2)).reshape(D_MODEL, SLAB_W)
    g_w_in = lax.switch(me, [functools.partial(_slab_to_shard, d=d) for d in range(N_DEV)], g_slab, g_wglr)
    d_w_in, nm_w_in, nv_w_in = _adam(g_w_in, w_in[0], m_w_in[0], v_w_in[0], "adam_w_in", LANES)
    rb = gla_rows
    g_w_br, d_w_br, nm_w_br, nv_w_br = _reduce_adam(p_rows, w_branch_ret[0], m_w_branch_ret[0], v_w_branch_ret[0], "adam_w_branch_ret", rb, 0)
    g_w_bg, d_w_bg, nm_w_bg, nv_w_bg = _reduce_adam(p_rows, w_branch_gla[0], m_w_branch_gla[0], v_w_branch_gla[0], "adam_w_branch_gla", rb, ret_rows)
    g_w_o, d_w_o, nm_w_o, nv_w_o = _reduce_adam(p_rows, w_out[0], m_w_out[0], v_w_out[0], "adam_w_out", rb, ret_rows + gla_rows)
    g_meta = lax.dynamic_slice_in_dim(g_meta_f, me * meta_shard, meta_shard, axis=1)
    g_wgu = lax.dynamic_slice_in_dim(g_wgu_f, me * gu_shard, gu_shard, axis=1)
    s_g = [g_meta, g_gnorm, g_wgu, g_bgate, g_gret, g_ggla, g_gfinal]
    s_w = [meta_tokens, norm_gain, w_gate_up[0], b_gate, ret_norm_gain, gla_norm_gain, final_norm_gain]
    s_m = [m_meta_tokens, m_norm_gain, m_w_gate_up[0], m_b_gate, m_ret_norm_gain, m_gla_norm_gain, m_final_norm_gain]
    s_v = [v_meta_tokens, v_norm_gain, v_w_gate_up[0], v_b_gate, v_ret_norm_gain, v_gla_norm_gain, v_final_norm_gain]
    shapes = [a.shape for a in s_g]
    s_d, s_nm, s_nv = [_unpack_rows(p, shapes) for p in _adam_small(*[_pack_rows(l) for l in (s_g, s_w, s_m, s_v)])]

    loss = loss_all[0, 0]
    grad_x = grad_x[None]

    def order(meta, gnorm, win, wgu, bgate, gret, ggla, wbr, wbg, wo, gfin):
        return (meta, gnorm, win[None], wgu[None], bgate, gret, ggla, wbr[None], wbg[None], wo[None], gfin.reshape(final_norm_gain.shape))

    def small(l):
        return dict(meta=l[0], gnorm=l[1], wgu=l[2], bgate=l[3], gret=l[4], ggla=l[5], gfin=l[6])

    grads = order(win=g_w_in, wbr=g_w_br, wbg=g_w_bg, wo=g_w_o, **small(s_g))
    deltas = order(win=d_w_in, wbr=d_w_br, wbg=d_w_bg, wo=d_w_o, **small(s_d))
    new_m = order(win=nm_w_in, wbr=nm_w_br, wbg=nm_w_bg, wo=nm_w_o, **small(s_nm))
    new_v = order(win=nv_w_in, wbr=nv_w_br, wbg=nv_w_bg, wo=nv_w_o, **small(s_nv))
    return (loss, grad_x, *grads, *deltas, *new_m, *new_v)
```

```python
import functools

import jax
import jax.numpy as jnp
from jax import lax
from jax.experimental import pallas as pl
from jax.experimental.pallas import tpu as pltpu

F32 = jnp.float32
BF16 = jnp.bfloat16

D_MODEL = 1024
N_META = 16
TILE = 256
PAD_ROWS = TILE - N_META
RET_HEADS = 4
RET_QK = 256
RET_V = 512
RET_W = RET_HEADS * RET_V
GLA_HEADS = 4
GLA_K = 128
GLA_V = 256
GLA_W = GLA_HEADS * GLA_V
GLA_RANK = 16
GLA_TAU = 16.0
GLA_CHUNK = 16
ROPE_BASE = 10000.0
EPS = 1e-6
LANES = 128
N_DEV = 8
SEG_NAMES = ("rq", "rk", "rv", "rg", "gq", "gk", "gv", "gg", "mr", "mg")
SEG_W = (1024, 1024, 2048, 2048, 512, 512, 1024, 1024, 1024, 1024)
SEG_OFF = tuple(sum(SEG_W[:i]) for i in range(len(SEG_W)))
AL_COLS = sum(SEG_W)
IN_COLS = AL_COLS + GLA_RANK
GLR_OFF = sum(SEG_W[:8])
IN_SHARD = IN_COLS // N_DEV


def _aligned_col(c):
    assert c <= GLR_OFF or c >= GLR_OFF + GLA_RANK
    return c if c <= GLR_OFF else c - GLA_RANK


SLAB_BOUND = tuple(_aligned_col(IN_SHARD * d) for d in range(N_DEV + 1))
SLAB_BLK0 = tuple(b // LANES for b in SLAB_BOUND[:-1])
SLAB_SHIFT = tuple(b % LANES for b in SLAB_BOUND[:-1])
SLAB_BLOCKS = max(-(-SLAB_BOUND[d + 1] // LANES) - SLAB_BLK0[d] for d in range(N_DEV))
SLAB_W = SLAB_BLOCKS * LANES
GLR_DEV = GLR_OFF // IN_SHARD
GLR_LOCAL = GLR_OFF - GLR_DEV * IN_SHARD
assert all(SLAB_BLK0[d] + SLAB_BLOCKS <= AL_COLS // LANES for d in range(N_DEV))
VMEM_LIMIT = 58 * 1024 * 1024
ADAM_LR, ADAM_B1, ADAM_B2, ADAM_EPS, ADAM_WD, ADAM_STEP = 0.001, 0.9, 0.999, 1e-08, 0.01, 10
ANY = pl.BlockSpec(memory_space=pl.ANY)
MESH = pl.DeviceIdType.MESH


def _call(body, name, **kw):
    return pl.pallas_call(body, name=name, **kw)


def _params(sem=None):
    return pltpu.CompilerParams(dimension_semantics=sem, vmem_limit_bytes=VMEM_LIMIT)


def _mm(a, b):
    return jnp.dot(a, b, preferred_element_type=F32)


def _mm_nt(a, b):
    return lax.dot_general(a, b, (((1,), (1,)), ((), ())), preferred_element_type=F32)


def _mm_tn(a, b):
    return lax.dot_general(a, b, (((0,), (0,)), ((), ())), preferred_element_type=F32)


def _sigmoid(x):
    return 1.0 / (1.0 + jnp.exp(-x))


def _rope(t, cos, sin):
    half = t.shape[-1] // 2
    t1, t2 = t[:, :half], t[:, half:]
    return jnp.concatenate([t1 * cos - t2 * sin, t2 * cos + t1 * sin], axis=-1)


def _rope_bwd(g, cos, sin):
    half = g.shape[-1] // 2
    g1, g2 = g[:, :half], g[:, half:]
    return jnp.concatenate([g1 * cos + g2 * sin, g2 * cos - g1 * sin], axis=-1)


def _row_mean(x):
    return jnp.mean(x, axis=-1, keepdims=True)


def _col_sum(x):
    return jnp.sum(x, axis=0, keepdims=True)


def _tile_rows(head_ref, x_ref):
    return jnp.where(pl.program_id(0) == 0, head_ref[...], x_ref[...])


def _head_spec():
    return pl.BlockSpec((TILE, D_MODEL), lambda i: (0, 0))


def _x_spec():
    return pl.BlockSpec((TILE, D_MODEL), lambda i: (jnp.maximum(i - 1, 0), 0))


def _slab_plan():
    interior, shared = [], []
    for d in range(N_DEV):
        lo, hi = -(-SLAB_BOUND[d] // LANES), SLAB_BOUND[d + 1] // LANES
        interior.append((d, LANES * (lo - SLAB_BLK0[d]), LANES * lo, LANES * (hi - lo)))
        if d + 1 < N_DEV and SLAB_BOUND[d + 1] % LANES:
            shared.append((hi, d, hi - SLAB_BLK0[d]))
    return interior, shared


W_SCRATCH = lambda: [pltpu.VMEM((D_MODEL, AL_COLS), BF16), pltpu.VMEM((D_MODEL, LANES), BF16),
                     pltpu.VMEM((2 * (N_DEV - 1), D_MODEL, LANES), BF16), pltpu.SemaphoreType.DMA((3 * N_DEV,))]


def _load_weight(slabs_hbm, wg_hbm, w_vm, wg_vm, edge_vm, sem):
    interior, shared = _slab_plan()
    copies = [pltpu.make_async_copy(wg_hbm, wg_vm, sem.at[0])]
    for d, src, dst, width in interior:
        copies.append(pltpu.make_async_copy(slabs_hbm.at[d, :, pl.ds(src, width)], w_vm.at[:, pl.ds(dst, width)], sem.at[1 + d]))
    for n, (_, d, blk) in enumerate(shared):
        copies.append(pltpu.make_async_copy(slabs_hbm.at[d, :, pl.ds(LANES * blk, LANES)], edge_vm.at[2 * n], sem.at[1 + N_DEV + 2 * n]))
        copies.append(pltpu.make_async_copy(slabs_hbm.at[d + 1, :, pl.ds(0, LANES)], edge_vm.at[2 * n + 1], sem.at[2 + N_DEV + 2 * n]))
    for cp in copies:
        cp.start()
    for cp in copies:
        cp.wait()
    for n, (blk, _, _) in enumerate(shared):
        w_vm[:, LANES * blk:LANES * (blk + 1)] = edge_vm[2 * n] + edge_vm[2 * n + 1]


def _inproj_fwd(head, x, g_norm, slabs, w_glr, row_shards):
    t_rows = x.shape[0] + TILE
    nt = t_rows // TILE
    ns = len(row_shards)

    def body(head_ref, x_ref, g_ref, slabs_hbm, wg_hbm, *rest):
        shard_refs, ut_ref, seg_refs, glr_ref = rest[:ns], rest[ns], rest[ns + 1:ns + 11], rest[ns + 11]
        gathered = rest[ns + 12:2 * ns + 12]
        w_vm, wg_vm, edge_vm, sem = rest[2 * ns + 12:2 * ns + 16]
        gather = _Exchange(shard_refs, gathered, rest[2 * ns + 16:], among_chips=False)

        @pl.when(pl.program_id(0) == 0)
        def _():
            gather.start()
            _load_weight(slabs_hbm, wg_hbm, w_vm, wg_vm, edge_vm, sem)

        @pl.when(pl.program_id(0) == nt - 1)
        def _():
            gather.finish()

        x = _tile_rows(head_ref, x_ref)
        r = lax.rsqrt(_row_mean(x * x) + EPS)
        u32 = (x * r * g_ref[...]).astype(BF16).astype(F32)
        u = u32.astype(BF16)
        ut_ref[...] = u32.T.astype(BF16)
        for s, o_ref in enumerate(seg_refs):
            o_ref[...] = _mm(u, w_vm[:, SEG_OFF[s]:SEG_OFF[s] + SEG_W[s]]).astype(BF16)
        glr_ref[...] = _mm(u, wg_vm[...])

    row = lambda w: pl.BlockSpec((TILE, w), lambda i: (i, 0))
    out_shape = ([jax.ShapeDtypeStruct((nt, D_MODEL, TILE), BF16)] + [jax.ShapeDtypeStruct((t_rows, w), BF16) for w in SEG_W]
                 + [jax.ShapeDtypeStruct((t_rows, LANES), F32)])
    out_shape += [jax.ShapeDtypeStruct((N_DEV, *a.shape), a.dtype) for a in row_shards]
    out_specs = [pl.BlockSpec((None, D_MODEL, TILE), lambda i: (i, 0, 0))] + [row(w) for w in SEG_W] + [row(LANES)] + [ANY] * ns
    outs = _call(
        body, "inproj_fwd", grid=(nt,), out_shape=out_shape,
        in_specs=[_head_spec(), _x_spec(), pl.BlockSpec((1, D_MODEL), lambda i: (0, 0)), ANY, ANY] + [ANY] * ns,
        out_specs=out_specs, scratch_shapes=W_SCRATCH() + _exchange_sems(ns, N_DEV),
        compiler_params=_params(("arbitrary",)),
    )(head, x, g_norm, slabs, w_glr, *row_shards)
    return outs[0], dict(zip(SEG_NAMES, outs[1:11])), outs[11], outs[12:]


def _ret_decay(lgh):
    i = lax.broadcasted_iota(jnp.int32, (TILE, TILE), 0)
    j = lax.broadcasted_iota(jnp.int32, (TILE, TILE), 1)
    rel = (i - j).astype(F32)
    return jnp.where(rel >= 0, jnp.exp(jnp.maximum(rel, 0.0) * lgh), 0.0)


def _ret_vectors(lgh):
    idx = lax.broadcasted_iota(jnp.int32, (TILE, 1), 0).astype(F32)
    xi = jnp.exp((idx + 1.0) * lgh)
    zeta = jnp.exp((TILE - 1.0 - idx) * lgh)
    gc = jnp.exp(jnp.full((1, 1), float(TILE), F32) * lgh)
    return xi, zeta, gc


def _ret_fwd(seg, cos, sin, gain, lg):
    t_rows = cos.shape[0]
    nt = t_rows // TILE

    def body(lg_ref, q_ref, k_ref, v_ref, g_ref, cos_ref, sin_ref, gain_ref, oraw_ref, oret_ref, st_ref, s_acc, dm):
        h, t = pl.program_id(0), pl.program_id(1)
        lgh = lg_ref[h]

        @pl.when(t == 0)
        def _():
            s_acc[...] = jnp.zeros_like(s_acc)
            dm[...] = _ret_decay(lgh)

        cos_t, sin_t = cos_ref[...], sin_ref[...]
        q = _rope(q_ref[...].astype(F32), cos_t, sin_t)
        k = _rope(k_ref[...].astype(F32), cos_t, sin_t) * (RET_QK ** -0.5)
        xi, zeta, gc = _ret_vectors(lgh)
        v = v_ref[...]
        s_in = s_acc[...]
        p = (_mm_nt(q.astype(BF16), k.astype(BF16)) * dm[...]).astype(BF16)
        o = _mm(p, v) + _mm((q * xi).astype(BF16), s_in.astype(BF16))
        st_ref[...] = s_in.astype(BF16)
        s_acc[...] = s_in * gc + _mm_tn((k * zeta).astype(BF16), v)
        oraw_ref[...] = o
        oc = o - _row_mean(o)
        n = oc * lax.rsqrt(_row_mean(oc * oc) + EPS) * gain_ref[...]
        g = g_ref[...].astype(F32)
        oret_ref[...] = (n * g * _sigmoid(g)).astype(BF16)

    blk = lambda w: pl.BlockSpec((TILE, w), lambda h, t: (t, h))
    tab = pl.BlockSpec((TILE, LANES), lambda h, t: (t, 0))
    return _call(
        body, "ret_fwd", grid=(RET_HEADS, nt),
        out_shape=[jax.ShapeDtypeStruct((t_rows, RET_W), F32), jax.ShapeDtypeStruct((t_rows, RET_W), BF16),
                   jax.ShapeDtypeStruct((RET_HEADS, nt, RET_QK, RET_V), BF16)],
        in_specs=[pl.BlockSpec(memory_space=pltpu.SMEM), blk(RET_QK), blk(RET_QK), blk(RET_V), blk(RET_V), tab, tab,
                  pl.BlockSpec((1, RET_V), lambda h, t: (0, h))],
        out_specs=[blk(RET_V), blk(RET_V), pl.BlockSpec((None, None, RET_QK, RET_V), lambda h, t: (h, t, 0, 0))],
        scratch_shapes=[pltpu.VMEM((RET_QK, RET_V), F32), pltpu.VMEM((TILE, TILE), F32)],
        compiler_params=_params(("arbitrary", "arbitrary")),
    )(lg, seg["rq"], seg["rk"], seg["rv"], seg["rg"], cos, sin, gain)


def _ret_bwd(seg, cos, sin, gain, lg, o_raw, do_ret, states):
    t_rows = cos.shape[0]
    nt = t_rows // TILE

    def body(lg_ref, q_ref, k_ref, v_ref, g_ref, cos_ref, sin_ref, gain_ref, oraw_ref, do_ref, st_ref,
             dq_ref, dk_ref, dv_ref, dg_ref, dgain_ref, e_acc, dm):
        h, j = pl.program_id(0), pl.program_id(1)
        lgh = lg_ref[h]

        @pl.when(j == 0)
        def _():
            e_acc[...] = jnp.zeros_like(e_acc)
            dm[...] = _ret_decay(lgh)
            dgain_ref[...] = jnp.zeros_like(dgain_ref)

        cos_t, sin_t = cos_ref[...], sin_ref[...]
        q = _rope(q_ref[...].astype(F32), cos_t, sin_t)
        k = _rope(k_ref[...].astype(F32), cos_t, sin_t) * (RET_QK ** -0.5)
        xi, zeta, gc = _ret_vectors(lgh)
        v = v_ref[...]
        g = g_ref[...].astype(F32)
        o = oraw_ref[...]
        do = do_ref[...].astype(F32)
        oc = o - _row_mean(o)
        rstd = lax.rsqrt(_row_mean(oc * oc) + EPS)
        xh = oc * rstd
        gain_t = gain_ref[...]
        sg = _sigmoid(g)
        dn = do * (g * sg)
        dg_ref[...] = (do * (xh * gain_t) * (sg * (1.0 + g * (1.0 - sg)))).astype(BF16)
        dgain_ref[...] += _col_sum(dn * xh)
        dxh = dn * gain_t
        dob = (rstd * (dxh - _row_mean(dxh) - xh * _row_mean(dxh * xh))).astype(BF16)
        dmat = dm[...]
        qb, kb = q.astype(BF16), k.astype(BF16)
        p = (_mm_nt(qb, kb) * dmat).astype(BF16)
        dp = (_mm_nt(dob, v) * dmat).astype(BF16)
        s_in = st_ref[...]
        e_in = e_acc[...]
        e_b = e_in.astype(BF16)
        dq = _mm(dp, kb) + _mm_nt(dob, s_in) * xi
        dk = _mm_tn(dp, qb) + _mm_nt(v, e_b) * zeta
        dv_ref[...] = (_mm_tn(p, dob) + _mm((k * zeta).astype(BF16), e_b)).astype(BF16)
        e_acc[...] = e_in * gc + _mm_tn((q * xi).astype(BF16), dob)
        dq_ref[...] = _rope_bwd(dq, cos_t, sin_t).astype(BF16)
        dk_ref[...] = (_rope_bwd(dk, cos_t, sin_t) * (RET_QK ** -0.5)).astype(BF16)

    blk = lambda w: pl.BlockSpec((TILE, w), lambda h, j: (nt - 1 - j, h))
    tab = pl.BlockSpec((TILE, LANES), lambda h, j: (nt - 1 - j, 0))
    vec = pl.BlockSpec((1, RET_V), lambda h, j: (0, h))
    return _call(
        body, "ret_bwd", grid=(RET_HEADS, nt),
        out_shape=[jax.ShapeDtypeStruct((t_rows, RET_HEADS * RET_QK), BF16), jax.ShapeDtypeStruct((t_rows, RET_HEADS * RET_QK), BF16),
                   jax.ShapeDtypeStruct((t_rows, RET_W), BF16), jax.ShapeDtypeStruct((t_rows, RET_W), BF16),
                   jax.ShapeDtypeStruct((1, RET_W), F32)],
        in_specs=[pl.BlockSpec(memory_space=pltpu.SMEM), blk(RET_QK), blk(RET_QK), blk(RET_V), blk(RET_V), tab, tab, vec,
                  blk(RET_V), blk(RET_V), pl.BlockSpec((None, None, RET_QK, RET_V), lambda h, j: (h, nt - 1 - j, 0, 0))],
        out_specs=[blk(RET_QK), blk(RET_QK), blk(RET_V), blk(RET_V), vec],
        scratch_shapes=[pltpu.VMEM((RET_QK, RET_V), F32), pltpu.VMEM((TILE, TILE), F32)],
        compiler_params=_params(("arbitrary", "arbitrary")),
    )(lg, seg["rq"], seg["rk"], seg["rv"], seg["rg"], cos, sin, gain, o_raw, do_ret, states)


GLA_LEVELS = (32, 64, 128, 256)
N_TERMS = 1 + len(GLA_LEVELS)


def _gla_tables():
    p = jnp.arange(TILE)[:, None]
    r = jnp.arange(TILE)[None, :]
    masks = [(p // GLA_CHUNK == r // GLA_CHUNK) & (r <= p)]
    for blk in GLA_LEVELS:
        masks.append((p // blk == r // blk) & (p % blk >= blk // 2) & (r % blk < blk // 2))
    masks = jnp.stack(masks + [m.T for m in masks]).astype(F32)
    cum_fwd = jnp.concatenate([r <= p, masks[0] > 0], axis=0).astype(BF16)
    cum_bwd = jnp.concatenate([r >= p, masks[N_TERMS] > 0], axis=1).astype(BF16)
    return masks, cum_fwd, cum_bwd


def _split3(x):
    hi = x.astype(BF16)
    rest = x - hi.astype(F32)
    mid = rest.astype(BF16)
    lo = (rest - mid.astype(F32)).astype(BF16)
    return jnp.concatenate([hi, mid, lo], axis=1)


def _join3(y):
    w = y.shape[1] // 3
    return (y[:, 2 * w:] + y[:, w:2 * w]) + y[:, :w]


def _gla_prep(q_ref, k_ref, glr_ref, wgu_ref, b_ref, cum_ref, g_scr, ref_scr):
    z = _mm(glr_ref[...].astype(BF16), wgu_ref[...].astype(BF16)) + b_ref[...]
    la = (jnp.minimum(z, 0.0) - jnp.log(1.0 + jnp.exp(-jnp.abs(z)))) / GLA_TAU
    gb = _join3(_mm(cum_ref[...], _split3(la)))
    g, b = gb[:TILE], gb[TILE:]
    g_scr[...] = g
    factors = [(jnp.exp(b), jnp.exp(-b))]
    for lvl, blk in enumerate(GLA_LEVELS):
        for n in range(TILE // blk):
            ref_scr[lvl, n * blk:(n + 1) * blk, :] = jnp.broadcast_to(g_scr[pl.ds(n * blk + blk // 2 - 1, 1), :], (blk, GLA_K))
        x = g - ref_scr[lvl]
        factors.append((jnp.exp(jnp.minimum(x, 0.0)), jnp.exp(jnp.minimum(-x, 0.0))))
    g_last = g_scr[pl.ds(TILE - 1, 1), :]
    q = q_ref[...].astype(F32) * (GLA_K ** -0.5)
    k = k_ref[...].astype(F32)
    return z, q, k, factors, jnp.exp(g), jnp.exp(g_last), jnp.exp(g_last - g)


def _gla_scores(q, k, factors, m_ref):
    a = jnp.zeros((TILE, TILE), F32)
    for l, (fq, fk) in enumerate(factors):
        s = _mm_nt((q * fq).astype(BF16), (k * fk).astype(BF16))
        a = jnp.where(m_ref[l] > 0.0, s, a)
    return a


def _gla_fwd(seg, glr, wgu_pad, b_gate, gain, masks, cum_fwd):
    t_rows = glr.shape[0]
    nt = t_rows // TILE

    def body(q_ref, k_ref, v_ref, g_ref, glr_ref, wgu_ref, b_ref, gain_ref, m_ref, cum_ref, oraw_ref, ogla_ref, st_ref,
             s_acc, g_scr, ref_scr):
        @pl.when(pl.program_id(1) == 0)
        def _():
            s_acc[...] = jnp.zeros_like(s_acc)

        _, q, k, factors, e_g, e_last, e_end = _gla_prep(q_ref, k_ref, glr_ref, wgu_ref, b_ref, cum_ref, g_scr, ref_scr)
        v = v_ref[...]
        st = s_acc[...]
        st_ref[...] = st
        a = _gla_scores(q, k, factors, m_ref)
        o = _mm(a.astype(BF16), v) + _mm_nt((q * e_g).astype(BF16), st.astype(BF16))
        s_acc[...] = st * e_last + _mm(v.astype(F32).T.astype(BF16), (k * e_end).astype(BF16))
        oraw_ref[...] = o
        n = o * lax.rsqrt(_row_mean(o * o) + EPS) * gain_ref[...]
        g = g_ref[...].astype(F32)
        ogla_ref[...] = (n * g * _sigmoid(g)).astype(BF16)

    blk = lambda w: pl.BlockSpec((TILE, w), lambda h, t: (t, h))
    return _call(
        body, "gla_fwd", grid=(GLA_HEADS, nt),
        out_shape=[jax.ShapeDtypeStruct((t_rows, GLA_W), F32), jax.ShapeDtypeStruct((t_rows, GLA_W), BF16),
                   jax.ShapeDtypeStruct((GLA_HEADS, nt, GLA_V, GLA_K), F32)],
        in_specs=[blk(GLA_K), blk(GLA_K), blk(GLA_V), blk(GLA_V), pl.BlockSpec((TILE, LANES), lambda h, t: (t, 0)),
                  pl.BlockSpec((LANES, GLA_K), lambda h, t: (0, h)), pl.BlockSpec((1, GLA_K), lambda h, t: (0, h)),
                  pl.BlockSpec((1, GLA_V), lambda h, t: (0, h)),
                  pl.BlockSpec((N_TERMS, TILE, TILE), lambda h, t: (0, 0, 0)), pl.BlockSpec((2 * TILE, TILE), lambda h, t: (0, 0))],
        out_specs=[blk(GLA_V), blk(GLA_V), pl.BlockSpec((None, None, GLA_V, GLA_K), lambda h, t: (h, t, 0, 0))],
        scratch_shapes=[pltpu.VMEM((GLA_V, GLA_K), F32), pltpu.VMEM((TILE, GLA_K), F32),
                        pltpu.VMEM((len(GLA_LEVELS), TILE, GLA_K), F32)],
        compiler_params=_params(("arbitrary", "arbitrary")),
    )(seg["gq"], seg["gk"], seg["gv"], seg["gg"], glr, wgu_pad, b_gate, gain, masks, cum_fwd)


def _gla_bwd(seg, glr, wgu_pad, b_gate, gain, o_raw, do_gla, states, masks, cum_fwd, cum_bwd):
    t_rows = glr.shape[0]
    nt = t_rows // TILE

    def body(q_ref, k_ref, v_ref, g_ref, glr_ref, wgu_ref, b_ref, gain_ref, m_ref, cum_ref, cumb_ref, oraw_ref, do_ref, st_ref,
             dq_ref, dk_ref, dv_ref, dg_ref, dglr_ref, dwgu_ref, dbg_ref, dgain_ref, d_acc, g_scr, ref_scr, dref_scr):
        @pl.when(pl.program_id(1) == 0)
        def _():
            d_acc[...] = jnp.zeros_like(d_acc)
            dwgu_ref[...] = jnp.zeros_like(dwgu_ref)
            dbg_ref[...] = jnp.zeros_like(dbg_ref)
            dgain_ref[...] = jnp.zeros_like(dgain_ref)

        z, q, k, factors, e_g, e_last, e_end = _gla_prep(q_ref, k_ref, glr_ref, wgu_ref, b_ref, cum_ref, g_scr, ref_scr)
        v = v_ref[...]
        o = oraw_ref[...]
        do = do_ref[...].astype(F32)
        g = g_ref[...].astype(F32)
        rinv = lax.rsqrt(_row_mean(o * o) + EPS)
        nh = o * rinv
        gain_t = gain_ref[...]
        sg = _sigmoid(g)
        dn = do * (g * sg)
        dg_ref[...] = (do * (nh * gain_t) * (sg * (1.0 + g * (1.0 - sg)))).astype(BF16)
        dgain_ref[...] += _col_sum(dn * nh)
        dnh = dn * gain_t
        dor = rinv * (dnh - nh * _row_mean(dnh * nh))
        dob = dor.astype(BF16)
        a_t = _gla_scores(q, k, factors, m_ref).T.astype(BF16)
        da = _mm_nt(dob, v)
        da_t = _mm_nt(v, dob)
        st_in = st_ref[...]
        d_out = d_acc[...]
        d_out_b = d_out.astype(BF16)
        qg, kg = q * e_g, k * e_end
        dqg = _mm(dob, st_in.astype(BF16))
        dkg = _mm(v, d_out_b)
        dv_ref[...] = (_mm(a_t, dob) + _mm_nt(kg.astype(BF16), d_out_b)).astype(BF16)
        d_acc[...] = d_out * e_last + _mm(dor.T.astype(BF16), qg.astype(BF16))
        dq = dqg * e_g
        dk = dkg * e_end
        dkg_kg = dkg * kg
        dg_cum = dqg * qg - dkg_kg
        db = None
        for l, (fq, fk) in enumerate(factors):
            qt, kt = q * fq, k * fk
            dqt = _mm(jnp.where(m_ref[l] > 0.0, da, 0.0).astype(BF16), kt.astype(BF16))
            dkt = _mm(jnp.where(m_ref[N_TERMS + l] > 0.0, da_t, 0.0).astype(BF16), qt.astype(BF16))
            dq = dq + dqt * fq
            dk = dk + dkt * fk
            diff = dqt * qt - dkt * kt
            if l == 0:
                db = diff
            else:
                dg_cum = dg_cum + diff
                dref_scr[l - 1] = diff
        dq_ref[...] = (dq * (GLA_K ** -0.5)).astype(BF16)
        dk_ref[...] = dk.astype(BF16)
        g_scr[...] = dg_cum
        g_scr[pl.ds(TILE - 1, 1), :] += e_last * _col_sum(d_out * st_in) + _col_sum(dkg_kg)
        for lvl, blk in enumerate(GLA_LEVELS):
            for n in range(TILE // blk):
                g_scr[pl.ds(n * blk + blk // 2 - 1, 1), :] -= _col_sum(dref_scr[lvl, n * blk:(n + 1) * blk, :])
        dla = _join3(_mm(cumb_ref[...], jnp.concatenate([_split3(g_scr[...]), _split3(db)], axis=0)))
        dz = dla * (1.0 / GLA_TAU) * _sigmoid(-z)
        dzb = dz.astype(BF16)
        dglr_ref[...] = _mm_nt(dzb, wgu_ref[...].astype(BF16))
        dwgu_ref[...] += _mm(glr_ref[...].T.astype(BF16), dzb)
        dbg_ref[...] += _col_sum(dz)

    blk = lambda w: pl.BlockSpec((TILE, w), lambda h, j: (nt - 1 - j, h))
    vec = lambda w: pl.BlockSpec((1, w), lambda h, j: (0, h))
    wspec = pl.BlockSpec((LANES, GLA_K), lambda h, j: (0, h))
    return _call(
        body, "gla_bwd", grid=(GLA_HEADS, nt),
        out_shape=[jax.ShapeDtypeStruct((t_rows, GLA_HEADS * GLA_K), BF16), jax.ShapeDtypeStruct((t_rows, GLA_HEADS * GLA_K), BF16),
                   jax.ShapeDtypeStruct((t_rows, GLA_W), BF16), jax.ShapeDtypeStruct((t_rows, GLA_W), BF16),
                   jax.ShapeDtypeStruct((GLA_HEADS, t_rows, LANES), F32), jax.ShapeDtypeStruct((LANES, GLA_HEADS * GLA_K), F32),
                   jax.ShapeDtypeStruct((1, GLA_HEADS * GLA_K), F32), jax.ShapeDtypeStruct((1, GLA_W), F32)],
        in_specs=[blk(GLA_K), blk(GLA_K), blk(GLA_V), blk(GLA_V), pl.BlockSpec((TILE, LANES), lambda h, j: (nt - 1 - j, 0)),
                  wspec, vec(GLA_K), vec(GLA_V),
                  pl.BlockSpec((2 * N_TERMS, TILE, TILE), lambda h, j: (0, 0, 0)), pl.BlockSpec((2 * TILE, TILE), lambda h, j: (0, 0)),
                  pl.BlockSpec((TILE, 2 * TILE), lambda h, j: (0, 0)), blk(GLA_V), blk(GLA_V),
                  pl.BlockSpec((None, None, GLA_V, GLA_K), lambda h, j: (h, nt - 1 - j, 0, 0))],
        out_specs=[blk(GLA_K), blk(GLA_K), blk(GLA_V), blk(GLA_V),
                   pl.BlockSpec((None, TILE, LANES), lambda h, j: (h, nt - 1 - j, 0)), wspec, vec(GLA_K), vec(GLA_V)],
        scratch_shapes=[pltpu.VMEM((GLA_V, GLA_K), F32), pltpu.VMEM((TILE, GLA_K), F32),
                        pltpu.VMEM((len(GLA_LEVELS), TILE, GLA_K), F32), pltpu.VMEM((len(GLA_LEVELS), TILE, GLA_K), F32)],
        compiler_params=_params(("arbitrary", "arbitrary")),
    )(seg["gq"], seg["gk"], seg["gv"], seg["gg"], glr, wgu_pad, b_gate, gain, masks, cum_fwd, cum_bwd, o_raw, do_gla, states)


def _merge_fwd_bwd(o_ret, o_gla, seg, x, target, g_final, w_br, w_bg, w_out):
    t_rows = x.shape[0] + TILE
    nt = t_rows // TILE

    def body(oret_ref, ogla_ref, mr_ref, mg_ref, h0_ref, tgt_ref, gf_ref, wbr_hbm, wbg_hbm, wout_hbm,
             dh1_ref, dmr_ref, dmg_ref, doret_ref, dogla_ref, loss_ref, dgf_ref, dwbr_hbm, dwbg_hbm, dwout_hbm,
             wbr, wbg, wout, abr, abg, aout, sem):
        i = pl.program_id(0)

        @pl.when(i == 0)
        def _():
            cps = [pltpu.make_async_copy(s, d, sem.at[n]) for n, (s, d) in enumerate(((wbr_hbm, wbr), (wbg_hbm, wbg), (wout_hbm, wout)))]
            for cp in cps:
                cp.start()
            abr[...] = jnp.zeros_like(abr)
            abg[...] = jnp.zeros_like(abg)
            aout[...] = jnp.zeros_like(aout)
            loss_ref[...] = jnp.zeros_like(loss_ref)
            dgf_ref[...] = jnp.zeros_like(dgf_ref)
            for cp in cps:
                cp.wait()
            dh1_ref[...] = jnp.zeros_like(dh1_ref)
            dmr_ref[...] = jnp.zeros_like(dmr_ref)
            dmg_ref[...] = jnp.zeros_like(dmg_ref)
            doret_ref[...] = jnp.zeros_like(doret_ref)
            dogla_ref[...] = jnp.zeros_like(dogla_ref)

        @pl.when(i > 0)
        def _():
            oret, ogla = oret_ref[...], ogla_ref[...]
            br, bg = _mm(oret, wbr[...]), _mm(ogla, wbg[...])
            sr, sg = _sigmoid(mr_ref[...].astype(F32)), _sigmoid(mg_ref[...].astype(F32))
            mb = (sr * br + sg * bg).astype(BF16)
            h1 = h0_ref[...] + _mm(mb, wout[...])
            r2 = lax.rsqrt(_row_mean(h1 * h1) + EPS)
            hn = h1 * r2
            gf = gf_ref[...]
            diff = hn * gf - tgt_ref[...]
            loss_ref[...] += 0.5 * jnp.sum(_row_mean(diff * diff))
            dy = diff * (1.0 / D_MODEL)
            dgf_ref[...] += _col_sum(dy * hn)
            dyg = dy * gf
            dh1 = r2 * (dyg - hn * _row_mean(dyg * hn))
            dh1_ref[...] = dh1
            dh1b = dh1.astype(BF16)
            dm = _mm_nt(dh1b, wout[...])
            aout[...] += _mm_tn(mb, dh1b)
            dbr = (dm * sr).astype(BF16)
            dbg = (dm * sg).astype(BF16)
            dmr_ref[...] = (dm * br * sr * (1.0 - sr)).astype(BF16)
            dmg_ref[...] = (dm * bg * sg * (1.0 - sg)).astype(BF16)
            doret_ref[...] = _mm_nt(dbr, wbr[...]).astype(BF16)
            dogla_ref[...] = _mm_nt(dbg, wbg[...]).astype(BF16)
            abr[...] += _mm_tn(oret, dbr)
            abg[...] += _mm_tn(ogla, dbg)

        @pl.when(i == nt - 1)
        def _():
            wbr[...] = abr[...].astype(BF16)
            wbg[...] = abg[...].astype(BF16)
            wout[...] = aout[...].astype(BF16)
            pltpu.sync_copy(wbr, dwbr_hbm)
            pltpu.sync_copy(wbg, dwbg_hbm)
            pltpu.sync_copy(wout, dwout_hbm)

    row = lambda w: pl.BlockSpec((TILE, w), lambda i: (i, 0))
    one = lambda w: pl.BlockSpec((1, w), lambda i: (0, 0))
    return _call(
        body, "merge_fwd_bwd", grid=(nt,),
        out_shape=[jax.ShapeDtypeStruct((t_rows, D_MODEL), F32), jax.ShapeDtypeStruct((t_rows, D_MODEL), BF16),
                   jax.ShapeDtypeStruct((t_rows, D_MODEL), BF16), jax.ShapeDtypeStruct((t_rows, RET_W), BF16),
                   jax.ShapeDtypeStruct((t_rows, GLA_W), BF16), jax.ShapeDtypeStruct((1, LANES), F32),
                   jax.ShapeDtypeStruct((1, D_MODEL), F32), jax.ShapeDtypeStruct((RET_W, D_MODEL), BF16),
                   jax.ShapeDtypeStruct((GLA_W, D_MODEL), BF16), jax.ShapeDtypeStruct((D_MODEL, D_MODEL), BF16)],
        in_specs=[row(RET_W), row(GLA_W), row(D_MODEL), row(D_MODEL), _x_spec(), _x_spec(), one(D_MODEL), ANY, ANY, ANY],
        out_specs=[row(D_MODEL), row(D_MODEL), row(D_MODEL), row(RET_W), row(GLA_W), one(LANES), one(D_MODEL), ANY, ANY, ANY],
        scratch_shapes=[pltpu.VMEM((RET_W, D_MODEL), BF16), pltpu.VMEM((GLA_W, D_MODEL), BF16), pltpu.VMEM((D_MODEL, D_MODEL), BF16),
                        pltpu.VMEM((RET_W, D_MODEL), F32), pltpu.VMEM((GLA_W, D_MODEL), F32), pltpu.VMEM((D_MODEL, D_MODEL), F32),
                        pltpu.SemaphoreType.DMA((3,))],
        compiler_params=_params(("arbitrary",)),
    )(o_ret, o_gla, seg["mr"], seg["mg"], x, target, g_final, w_br, w_bg, w_out)


def _sum_heads(parts):
    def body(p_ref, o_ref):
        o_ref[...] = (p_ref[0] + p_ref[1] + p_ref[2] + p_ref[3]).astype(BF16)

    t_rows = parts.shape[1]
    return _call(
        body, "sum_heads_dglr", grid=(t_rows // TILE,), out_shape=jax.ShapeDtypeStruct((t_rows, LANES), BF16),
        in_specs=[pl.BlockSpec((GLA_HEADS, TILE, LANES), lambda i: (0, i, 0))], out_specs=pl.BlockSpec((TILE, LANES), lambda i: (i, 0)),
        compiler_params=_params(("arbitrary",)),
    )(parts)


def _inproj_bwd_x(dseg, dglr, head, x, dh1, g_norm, slabs, w_glr, chip_partials):
    t_rows = x.shape[0] + TILE
    nt = t_rows // TILE
    ne = len(chip_partials)

    def body(*refs):
        d_refs = refs[:10]
        dglr_ref, head_ref, x_ref, dh1_ref, g_ref, slabs_hbm, wg_hbm = refs[10:17]
        part_refs = refs[17:17 + ne]
        dx_ref, dhead_ref, dgn_ref = refs[17 + ne:20 + ne]
        landed = refs[20 + ne:20 + 2 * ne]
        w_vm, wg_vm, edge_vm, sem = refs[20 + 2 * ne:24 + 2 * ne]
        exchange = _Exchange(part_refs, landed, refs[24 + 2 * ne:], among_chips=True)

        @pl.when(pl.program_id(0) == 0)
        def _():
            exchange.start()
            dgn_ref[...] = jnp.zeros_like(dgn_ref)
            _load_weight(slabs_hbm, wg_hbm, w_vm, wg_vm, edge_vm, sem)

        @pl.when(pl.program_id(0) == nt - 1)
        def _():
            exchange.finish()

        du = _mm_nt(dglr_ref[...], wg_vm[...])
        for s, d_ref in enumerate(d_refs):
            du = du + _mm_nt(d_ref[...], w_vm[:, SEG_OFF[s]:SEG_OFF[s] + SEG_W[s]])
        x = _tile_rows(head_ref, x_ref)
        r = lax.rsqrt(_row_mean(x * x) + EPS)
        hn = x * r
        dgn_ref[...] += _col_sum(du * hn)
        dug = du * g_ref[...]
        dh0 = dh1_ref[...] + r * (dug - hn * _row_mean(dug * hn))
        dx_ref[...] = dh0

        @pl.when(pl.program_id(0) == 0)
        def _():
            dhead_ref[...] = dh0

    row = lambda w: pl.BlockSpec((TILE, w), lambda i: (i, 0))
    one = pl.BlockSpec((1, D_MODEL), lambda i: (0, 0))
    return _call(
        body, "inproj_bwd_x", grid=(nt,),
        out_shape=[jax.ShapeDtypeStruct((t_rows - TILE, D_MODEL), F32), jax.ShapeDtypeStruct((TILE, D_MODEL), F32),
                   jax.ShapeDtypeStruct((1, D_MODEL), F32)] + [jax.ShapeDtypeStruct(a.shape, a.dtype) for a in chip_partials],
        in_specs=[row(w) for w in SEG_W] + [row(LANES), _head_spec(), _x_spec(), row(D_MODEL), one, ANY, ANY] + [ANY] * ne,
        out_specs=[_x_spec(), _head_spec(), one] + [ANY] * ne,
        scratch_shapes=W_SCRATCH() + _exchange_sems(ne, N_CHIP),
        compiler_params=_params(("arbitrary",)),
    )(*[dseg[n] for n in SEG_NAMES], dglr, head, x, dh1, g_norm, slabs, w_glr, *chip_partials)


W_TILE = 512


def _inproj_bwd_w(ut, dseg, dglr):
    nt = ut.shape[0]
    t_rows = nt * TILE
    kc = 3 if nt % 3 == 0 else 1
    tiles = [(s, c) for s in range(len(SEG_W)) for c in range(0, SEG_W[s], W_TILE)]
    bpt = W_TILE // LANES

    def body(ut_hbm, *refs):
        d_refs, dglr_hbm, out_hbm, oglr_ref = refs[:10], refs[10], refs[11], refs[12]
        ut_vm, dbuf, obuf, acc, gbuf, sem = refs[13:]

        def fetch(i):
            s, c = tiles[i]
            return pltpu.make_async_copy(d_refs[s].at[:, pl.ds(c, W_TILE)], dbuf.at[i % 2], sem.at[1 + i % 2])

        def contract(rhs_ref, width):
            acc[:, :width] = jnp.zeros((D_MODEL, width), F32)

            def step(k, carry):
                part = None
                for j in range(kc):
                    kk = k * kc + j
                    prod = _mm(ut_vm[kk], rhs_ref[pl.ds(pl.multiple_of(kk * TILE, TILE), TILE), :])
                    part = prod if part is None else part + prod
                acc[:, :width] += part
                return carry

            lax.fori_loop(0, nt // kc, step, 0)
            return acc[:, :width]

        load_ut = pltpu.make_async_copy(ut_hbm, ut_vm, sem.at[0])
        load_glr = pltpu.make_async_copy(dglr_hbm, gbuf, sem.at[5])
        load_ut.start()
        load_glr.start()
        fetch(0).start()
        load_ut.wait()
        stores = {}
        for i, (s, c) in enumerate(tiles):
            if i + 1 < len(tiles):
                fetch(i + 1).start()
            fetch(i).wait()
            if i >= 2:
                stores[i - 2].wait()
            total = contract(dbuf.at[i % 2], W_TILE)
            for j in range(bpt):
                obuf[i % 2, j] = total[:, j * LANES:(j + 1) * LANES].astype(BF16)
            blk0 = (SEG_OFF[s] + c) // LANES
            stores[i] = pltpu.make_async_copy(obuf.at[i % 2], out_hbm.at[pl.ds(blk0, bpt)], sem.at[3 + i % 2])
            stores[i].start()
        load_glr.wait()
        oglr_ref[...] = contract(gbuf, LANES)
        for i in range(max(0, len(tiles) - 2), len(tiles)):
            stores[i].wait()

    return _call(
        body, "inproj_bwd_w",
        out_shape=[jax.ShapeDtypeStruct((AL_COLS // LANES, D_MODEL, LANES), BF16), jax.ShapeDtypeStruct((D_MODEL, LANES), F32)],
        in_specs=[ANY] * 12, out_specs=[ANY, pl.BlockSpec(memory_space=pltpu.VMEM)],
        scratch_shapes=[pltpu.VMEM((nt, D_MODEL, TILE), BF16), pltpu.VMEM((2, t_rows, W_TILE), BF16),
                        pltpu.VMEM((2, bpt, D_MODEL, LANES), BF16), pltpu.VMEM((D_MODEL, W_TILE), F32),
                        pltpu.VMEM((t_rows, LANES), BF16), pltpu.SemaphoreType.DMA((6,))],
        compiler_params=_params(),
    )(ut, *[dseg[n] for n in SEG_NAMES], dglr)


def _position():
    x, y, c = lax.axis_index("x"), lax.axis_index("y"), lax.axis_index("c")
    return x, y, c


def _index(px, py, pc):
    return 4 * px + 2 * py + pc


def _all_gather(arrs, name):
    n = len(arrs)

    def body(*refs):
        ins, outs = refs[:n], refs[n:2 * n]
        send_sems, recv_sems, local_sems = refs[2 * n:]
        x, y, c = _position()
        me, sibling = (x, y, c), (x, y, 1 - c)
        chips = [(1 - x, y), (x, 1 - y), (1 - x, 1 - y)]

        def copy(a, k, block, to, src=None):
            dst = outs[a].at[_index(*block)]
            return pltpu.make_async_remote_copy(src_ref=dst if src is None else src, dst_ref=dst,
                                                send_sem=send_sems.at[7 * a + k], recv_sem=recv_sems.at[7 * a + k],
                                                device_id=to, device_id_type=MESH)

        mine = [pltpu.make_async_copy(ins[a], outs[a].at[_index(*me)], local_sems.at[a]) for a in range(n)]
        for cp in mine:
            cp.start()
        first = []
        for a in range(n):
            first.append(copy(a, 0, me, sibling, src=ins[a]))
            first += [copy(a, 1 + j, me, (*chip, c), src=ins[a]) for j, chip in enumerate(chips)]
        for cp in first:
            cp.start()
        passed = []
        for j, chip in enumerate(chips):
            for a in range(n):
                copy(a, 1 + j, (*chip, c), me).wait_recv()
                cp = copy(a, 4 + j, (*chip, c), sibling)
                cp.start()
                passed.append(cp)
        for a in range(n):
            copy(a, 0, sibling, me).wait_recv()
            for j, chip in enumerate(chips):
                copy(a, 4 + j, (*chip, 1 - c), me).wait_recv()
        for cp in first + passed:
            cp.wait_send()
        for cp in mine:
            cp.wait()

    return _call(
        body, name,
        out_shape=[jax.ShapeDtypeStruct((N_DEV, *a.shape), a.dtype) for a in arrs],
        in_specs=[ANY] * n, out_specs=[ANY] * n,
        scratch_shapes=[pltpu.SemaphoreType.DMA((7 * n,)), pltpu.SemaphoreType.DMA((7 * n,)), pltpu.SemaphoreType.DMA((n,))],
    )(*arrs)


N_CHIP = N_DEV // 2


def _slab_block0(owner):
    step = SLAB_BLK0[1]
    assert all(SLAB_BLK0[d] == step * d - (d == N_DEV - 1) for d in range(N_DEV))
    return step * owner - jnp.where(owner == N_DEV - 1, 1, 0)


def _exchange_sibling(dw_blocks, row_sends):
    n = 1 + len(row_sends)

    def body(*refs):
        dw_ref, row_refs, outs, (send_sems, recv_sems) = refs[0], refs[1:n], refs[n:2 * n], refs[2 * n:]
        x, y, c = _position()
        copies = []
        for q in range(N_CHIP):
            owner = 2 * q + (1 - c)
            srcs = [dw_ref.at[pl.ds(_slab_block0(owner), SLAB_BLOCKS)]] + [r.at[owner] for r in row_refs]
            for k, src in enumerate(srcs):
                copies.append(pltpu.make_async_remote_copy(src_ref=src, dst_ref=outs[k].at[q], send_sem=send_sems.at[n * q + k],
                                                           recv_sem=recv_sems.at[n * q + k], device_id=(x, y, 1 - c), device_id_type=MESH))
        for cp in copies:
            cp.start()
        for cp in copies:
            cp.wait()

    return _call(
        body, "exchange_sibling",
        out_shape=[jax.ShapeDtypeStruct((N_CHIP, SLAB_BLOCKS, D_MODEL, LANES), BF16)]
                  + [jax.ShapeDtypeStruct((N_CHIP, *r.shape[1:]), BF16) for r in row_sends],
        in_specs=[ANY] * n, out_specs=[ANY] * n,
        scratch_shapes=[pltpu.SemaphoreType.DMA((n * N_CHIP,)), pltpu.SemaphoreType.DMA((n * N_CHIP,))],
    )(dw_blocks, *row_sends)


def _add_bf16(c_ref, a_ref, b_ref, o_ref):
    o_ref[...] = (a_ref[...].astype(F32) + b_ref[...].astype(F32)).astype(BF16)


def _chip_partial_slab(dw_blocks, sib, core):
    blk = pl.BlockSpec((None, 1, D_MODEL, LANES), lambda q, j, c_ref: (q, j, 0, 0))
    return _call(
        functools.partial(_add_bf16), "chip_partial_w_in", out_shape=jax.ShapeDtypeStruct(sib.shape, BF16),
        grid_spec=pltpu.PrefetchScalarGridSpec(
            num_scalar_prefetch=1, grid=(N_CHIP, SLAB_BLOCKS),
            in_specs=[pl.BlockSpec((1, D_MODEL, LANES), lambda q, j, c_ref: (_slab_block0(2 * q + c_ref[0]) + j, 0, 0)), blk],
            out_specs=blk),
        compiler_params=_params(("arbitrary", "arbitrary")),
    )(core, dw_blocks, sib)


def _chip_partial_rows(send, sib, core, name):
    rows, cols = send.shape[1:]
    blk = pl.BlockSpec((None, rows, cols), lambda q, c_ref: (q, 0, 0))
    return _call(
        functools.partial(_add_bf16), name, out_shape=jax.ShapeDtypeStruct(sib.shape, BF16),
        grid_spec=pltpu.PrefetchScalarGridSpec(
            num_scalar_prefetch=1, grid=(N_CHIP,),
            in_specs=[pl.BlockSpec((None, rows, cols), lambda q, c_ref: (2 * q + c_ref[0], 0, 0)), blk], out_specs=blk),
        compiler_params=_params(("arbitrary",)),
    )(core, send, sib)


def _exchange_sems(n_arrays, n_peers):
    return [pltpu.SemaphoreType.DMA((n_arrays * n_peers,)), pltpu.SemaphoreType.DMA((n_arrays * n_peers,)),
            pltpu.SemaphoreType.DMA((n_arrays,))]


class _Exchange:
    def __init__(self, srcs, dsts, sems, among_chips):
        self.arrs = list(zip(srcs, dsts))
        self.n = len(self.arrs)
        self.send_sems, self.recv_sems, self.local_sems = sems
        self.among_chips = among_chips
        x, y, c = _position()
        self.c = c
        self.me = 2 * x + y if among_chips else _index(x, y, c)
        self.n_peers = N_CHIP if among_chips else N_DEV

    def _device(self, p):
        return (p // 2, p % 2, self.c) if self.among_chips else (p // 4, (p // 2) % 2, p % 2)

    def _src(self, k, p):
        src = self.arrs[k][0]
        return src.at[p] if self.among_chips else src

    def _mine(self):
        return [pltpu.make_async_copy(self._src(k, self.me), self.arrs[k][1].at[self.me], self.local_sems.at[k]) for k in range(self.n)]

    def _copy(self, p, k, landing):
        return pltpu.make_async_remote_copy(
            src_ref=self._src(k, p), dst_ref=self.arrs[k][1].at[landing], send_sem=self.send_sems.at[self.n * p + k],
            recv_sem=self.recv_sems.at[self.n * landing + k], device_id=self._device(p), device_id_type=MESH)

    def _others(self, fn):
        for p in range(self.n_peers):
            @pl.when(p != self.me)
            def _():
                for k in range(self.n):
                    fn(p, k)

    def start(self):
        for cp in self._mine():
            cp.start()
        self._others(lambda p, k: self._copy(p, k, self.me).start())

    def finish(self):
        self._others(lambda p, k: self._copy(p, k, p).wait_recv())
        self._others(lambda p, k: self._copy(p, k, self.me).wait_send())
        for cp in self._mine():
            cp.wait()


def _adamw(g, w, m, v):
    m_new = ADAM_B1 * m + (1.0 - ADAM_B1) * g
    v_new = ADAM_B2 * v + (1.0 - ADAM_B2) * (g * g)
    m_hat = m_new / (1.0 - ADAM_B1 ** ADAM_STEP)
    v_hat = v_new / (1.0 - ADAM_B2 ** ADAM_STEP)
    delta = -ADAM_LR * (m_hat / (jnp.sqrt(v_hat) + ADAM_EPS) + ADAM_WD * w)
    return delta, m_new, v_new


def _sum_partials(p_ref):
    g = p_ref[0].astype(F32)
    for d in range(1, p_ref.shape[0]):
        g = g + p_ref[d].astype(F32)
    return g


def _reduce_adam(parts, w, m, v, name, block_rows, row_off=0):
    rows, cols = w.shape
    off = row_off // block_rows

    def body(p_ref, w_ref, m_ref, v_ref, g_ref, d_ref, mo_ref, vo_ref):
        g = _sum_partials(p_ref)
        g_ref[...] = g
        d_ref[...], mo_ref[...], vo_ref[...] = _adamw(g, w_ref[...], m_ref[...], v_ref[...])

    blk = pl.BlockSpec((block_rows, cols), lambda i: (i, 0))
    return _call(
        body, name, grid=(rows // block_rows,),
        out_shape=[jax.ShapeDtypeStruct((rows, cols), F32)] * 4,
        in_specs=[pl.BlockSpec((parts.shape[0], block_rows, cols), lambda i: (0, i + off, 0)), blk, blk, blk],
        out_specs=[blk] * 4,
        compiler_params=_params(("arbitrary",)),
    )(parts, w, m, v)


def _reduce_adam_slab(parts, glr, w, m, v, me):
    rows, cols = w.shape
    shift = jnp.asarray(SLAB_SHIFT, jnp.int32)[me]
    glr_at = jnp.where(me == GLR_DEV, GLR_LOCAL, cols).astype(jnp.int32)

    def body(s_ref, p_ref, glr_ref, w_ref, m_ref, v_ref, g_ref, d_ref, mo_ref, vo_ref):
        shift, glr_at = s_ref[0], s_ref[1]
        slab = jnp.concatenate([_sum_partials(p_ref.at[:, j]) for j in range(SLAB_BLOCKS)], axis=1)
        before = pltpu.roll(slab, SLAB_W - shift, 1)
        after = pltpu.roll(slab, lax.rem(SLAB_W - shift + GLA_RANK, SLAB_W), 1)
        wide = jnp.concatenate([glr_ref[...], jnp.zeros((LANES, SLAB_W - LANES), F32)], axis=1)
        placed = pltpu.roll(wide, lax.rem(glr_at, SLAB_W), 1)
        lane = lax.broadcasted_iota(jnp.int32, (LANES, SLAB_W), 1)
        g = jnp.where(lane < glr_at, before, jnp.where(lane < glr_at + GLA_RANK, placed, after))[:, :cols]
        g_ref[...] = g
        d_ref[...], mo_ref[...], vo_ref[...] = _adamw(g, w_ref[...], m_ref[...], v_ref[...])

    blk = pl.BlockSpec((LANES, cols), lambda i, s: (i, 0))
    return _call(
        body, "adam_w_in", out_shape=[jax.ShapeDtypeStruct((rows, cols), F32)] * 4,
        grid_spec=pltpu.PrefetchScalarGridSpec(
            num_scalar_prefetch=1, grid=(rows // LANES,),
            in_specs=[pl.BlockSpec((parts.shape[0], SLAB_BLOCKS, LANES, LANES), lambda i, s: (0, 0, i, 0)),
                      pl.BlockSpec((LANES, LANES), lambda i, s: (i, 0)), blk, blk, blk],
            out_specs=[blk] * 4),
        compiler_params=_params(("arbitrary",)),
    )(jnp.stack([shift, glr_at]), parts, glr, w, m, v)


def _reduce_small(parts):
    def body(p_ref, o_ref):
        o_ref[...] = _sum_partials(p_ref)

    return _call(body, "reduce_small", out_shape=jax.ShapeDtypeStruct(parts.shape[1:], F32))(parts)


def _adam_small(g, w, m, v):
    def body(g_ref, w_ref, m_ref, v_ref, d_ref, mo_ref, vo_ref):
        d_ref[...], mo_ref[...], vo_ref[...] = _adamw(g_ref[...], w_ref[...], m_ref[...], v_ref[...])

    return _call(body, "adam_small", out_shape=[jax.ShapeDtypeStruct(g.shape, F32)] * 3)(g, w, m, v)


def _pack_rows(arrs):
    rows = []
    for a in arrs:
        flat = a.reshape(-1).astype(F32)
        pad = (-flat.shape[0]) % LANES
        rows.append(jnp.pad(flat, (0, pad)).reshape(-1, LANES))
    packed = jnp.concatenate(rows, axis=0)
    return jnp.pad(packed, ((0, (-packed.shape[0]) % 8), (0, 0)))


def _unpack_rows(packed, shapes):
    out, r = [], 0
    for shp in shapes:
        size = 1
        for s in shp:
            size *= s
        nrows = -(-size // LANES)
        out.append(packed[r:r + nrows].reshape(-1)[:size].reshape(shp))
        r += nrows
    return out


def _shard_to_slab(shard, d):
    glr = jnp.zeros((D_MODEL, GLA_RANK), shard.dtype)
    if d == GLR_DEV:
        glr = shard[:, GLR_LOCAL:GLR_LOCAL + GLA_RANK]
        shard = jnp.concatenate([shard[:, :GLR_LOCAL], shard[:, GLR_LOCAL + GLA_RANK:]], axis=1)
    return jnp.pad(shard, ((0, 0), (SLAB_SHIFT[d], SLAB_W - SLAB_SHIFT[d] - shard.shape[1]))), glr


def kernel(x, meta_tokens, norm_gain, w_in, w_gate_up, b_gate, ret_norm_gain, gla_norm_gain, w_branch_ret, w_branch_gla, w_out, final_norm_gain, loss_target, m_meta_tokens, m_norm_gain, m_w_in, m_w_gate_up, m_b_gate, m_ret_norm_gain, m_gla_norm_gain, m_w_branch_ret, m_w_branch_gla, m_w_out, m_final_norm_gain, v_meta_tokens, v_norm_gain, v_w_in, v_w_gate_up, v_b_gate, v_ret_norm_gain, v_gla_norm_gain, v_w_branch_ret, v_w_branch_gla, v_w_out, v_final_norm_gain):
    xi, yi, ci = _position()
    me = _index(xi, yi, ci)
    seq = x.shape[1]
    t_rows = seq + TILE
    in_shard = w_in.shape[2]
    gu_shard = w_gate_up.shape[2]
    meta_shard = meta_tokens.shape[1]
    ret_rows, gla_rows, out_rows = w_branch_ret.shape[1], w_branch_gla.shape[1], w_out.shape[1]

    assert in_shard == IN_SHARD
    slab_local, glr_local = lax.switch(me, [functools.partial(_shard_to_slab, d=d) for d in range(N_DEV)], w_in[0])
    small_local = jnp.concatenate([meta_tokens, jnp.pad(w_gate_up[0], ((0, 0), (0, LANES - gu_shard))),
                                   glr_local.reshape(-1, LANES)], axis=0)
    slabs, g_small = _all_gather([slab_local.astype(BF16), small_local], "all_gather_shards")
    n_small = N_META + GLA_RANK
    w_glr = jnp.pad(g_small[GLR_DEV, n_small:].reshape(D_MODEL, GLA_RANK), ((0, 0), (0, LANES - GLA_RANK))).astype(BF16)
    meta_full = jnp.transpose(g_small[:, :N_META, :], (1, 0, 2)).reshape(N_META, D_MODEL)
    wgu_full = jnp.transpose(g_small[:, N_META:n_small, :gu_shard], (1, 0, 2)).reshape(GLA_RANK, GLA_HEADS * GLA_K)
    wgu_pad = jnp.pad(wgu_full, ((0, LANES - GLA_RANK), (0, 0)))

    pos = jnp.arange(t_rows, dtype=F32) - float(PAD_ROWS)
    half = RET_QK // 2
    inv = ROPE_BASE ** (-jnp.arange(half, dtype=F32) / half)
    ang = pos[:, None] * inv[None, :]
    cos, sin = jnp.cos(ang), jnp.sin(ang)
    lg = jnp.log1p(-(2.0 ** (-5.0 - jnp.arange(RET_HEADS, dtype=F32))))

    head = jnp.concatenate([jnp.zeros((PAD_ROWS, D_MODEL), F32), meta_full], axis=0)
    ut, seg, glr, (g_br, g_bg, g_o) = _inproj_fwd(
        head, x[0], norm_gain, slabs, w_glr, [w_branch_ret[0].astype(BF16), w_branch_gla[0].astype(BF16), w_out[0].astype(BF16)])
    w_br, w_bg, w_o = g_br.reshape(RET_W, D_MODEL), g_bg.reshape(GLA_W, D_MODEL), g_o.reshape(D_MODEL, D_MODEL)
    o_ret_raw, o_ret, ret_states = _ret_fwd(seg, cos, sin, ret_norm_gain, lg)
    masks, cum_fwd, cum_bwd = _gla_tables()
    o_gla_raw, o_gla, gla_states = _gla_fwd(seg, glr, wgu_pad, b_gate, gla_norm_gain, masks, cum_fwd)
    (dh1, d_mr, d_mg, do_ret, do_gla, loss_part, d_gfinal, dw_br, dw_bg, dw_o) = _merge_fwd_bwd(
        o_ret, o_gla, seg, x[0], loss_target[0], final_norm_gain.reshape(1, D_MODEL), w_br, w_bg, w_o)

    d_rq, d_rk, d_rv, d_rg, d_gret = _ret_bwd(seg, cos, sin, ret_norm_gain, lg, o_ret_raw, do_ret, ret_states)
    d_gq, d_gk, d_gv, d_gg, dglr_parts, d_wgu, d_bgate, d_ggla = _gla_bwd(
        seg, glr, wgu_pad, b_gate, gla_norm_gain, o_gla_raw, do_gla, gla_states, masks, cum_fwd, cum_bwd)
    dseg = dict(rq=d_rq, rk=d_rk, rv=d_rv, rg=d_rg, gq=d_gq, gk=d_gk, gv=d_gv, gg=d_gg, mr=d_mr, mg=d_mg)
    dglr = _sum_heads(dglr_parts)
    dw_blocks, dw_glr = _inproj_bwd_w(ut, dseg, dglr)

    row_sends = [dw_br.reshape(N_DEV, ret_rows, D_MODEL), dw_bg.reshape(N_DEV, gla_rows, D_MODEL),
                 dw_o.reshape(N_DEV, out_rows, D_MODEL)]
    sib_in, *sib_rows = _exchange_sibling(dw_blocks, row_sends)
    core = ci.astype(jnp.int32).reshape(1)
    chip_partials = [_chip_partial_slab(dw_blocks, sib_in, core)] + [
        _chip_partial_rows(send, sib, core, "chip_partial_" + name)
        for send, sib, name in zip(row_sends, sib_rows, ("w_branch_ret", "w_branch_gla", "w_out"))]
    grad_x, d_head, d_gnorm, p_in, p_br, p_bg, p_o = _inproj_bwd_x(
        dseg, dglr, head, x[0], dh1, norm_gain, slabs, w_glr, chip_partials)
    small_shapes = [(N_META, D_MODEL), (1, D_MODEL), (GLA_RANK, GLA_HEADS * GLA_K), (1, GLA_HEADS * GLA_K),
                    (1, RET_W), (1, GLA_W), (1, D_MODEL), (1, LANES), (D_MODEL, GLA_RANK)]
    small_part = _pack_rows([d_head[PAD_ROWS:], d_gnorm, d_wgu[:GLA_RANK], d_bgate, d_gret, d_ggla, d_gfinal, loss_part,
                             dw_glr[:, :GLA_RANK]])
    (p_small,) = _all_gather([small_part], "all_gather_small_partials")

    (g_meta_f, g_gnorm, g_wgu_f, g_bgate, g_gret, g_ggla, g_gfinal, loss_all,
     g_wglr) = _unpack_rows(_reduce_small(p_small), small_shapes)
    g_w_in, d_w_in, nm_w_in, nv_w_in = _reduce_adam_slab(
        p_in, jnp.pad(g_wglr, ((0, 0), (0, LANES - GLA_RANK))), w_in[0], m_w_in[0], v_w_in[0], me)
    rb = gla_rows
    g_w_br, d_w_br, nm_w_br, nv_w_br = _reduce_adam(p_br, w_branch_ret[0], m_w_branch_ret[0], v_w_branch_ret[0], "adam_w_branch_ret", rb)
    g_w_bg, d_w_bg, nm_w_bg, nv_w_bg = _reduce_adam(p_bg, w_branch_gla[0], m_w_branch_gla[0], v_w_branch_gla[0], "adam_w_branch_gla", rb)
    g_w_o, d_w_o, nm_w_o, nv_w_o = _reduce_adam(p_o, w_out[0], m_w_out[0], v_w_out[0], "adam_w_out", rb)
    g_meta = lax.dynamic_slice_in_dim(g_meta_f, me * meta_shard, meta_shard, axis=1)
    g_wgu = lax.dynamic_slice_in_dim(g_wgu_f, me * gu_shard, gu_shard, axis=1)
    s_g = [g_meta, g_gnorm, g_wgu, g_bgate, g_gret, g_ggla, g_gfinal]
    s_w = [meta_tokens, norm_gain, w_gate_up[0], b_gate, ret_norm_gain, gla_norm_gain, final_norm_gain]
    s_m = [m_meta_tokens, m_norm_gain, m_w_gate_up[0], m_b_gate, m_ret_norm_gain, m_gla_norm_gain, m_final_norm_gain]
    s_v = [v_meta_tokens, v_norm_gain, v_w_gate_up[0], v_b_gate, v_ret_norm_gain, v_gla_norm_gain, v_final_norm_gain]
    shapes = [a.shape for a in s_g]
    s_d, s_nm, s_nv = [_unpack_rows(p, shapes) for p in _adam_small(*[_pack_rows(l) for l in (s_g, s_w, s_m, s_v)])]

    loss = loss_all[0, 0]
    grad_x = grad_x[None]

    def order(meta, gnorm, win, wgu, bgate, gret, ggla, wbr, wbg, wo, gfin):
        return (meta, gnorm, win[None], wgu[None], bgate, gret, ggla, wbr[None], wbg[None], wo[None], gfin.reshape(final_norm_gain.shape))

    def small(l):
        return dict(meta=l[0], gnorm=l[1], wgu=l[2], bgate=l[3], gret=l[4], ggla=l[5], gfin=l[6])

    grads = order(win=g_w_in, wbr=g_w_br, wbg=g_w_bg, wo=g_w_o, **small(s_g))
    deltas = order(win=d_w_in, wbr=d_w_br, wbg=d_w_bg, wo=d_w_o, **small(s_d))
    new_m = order(win=nm_w_in, wbr=nm_w_br, wbg=nm_w_bg, wo=nm_w_o, **small(s_nm))
    new_v = order(win=nv_w_in, wbr=nv_w_br, wbg=nv_w_bg, wo=nv_w_o, **small(s_nv))
    return (loss, grad_x, *grads, *deltas, *new_m, *new_v)
```

```python
import functools

import jax
import jax.numpy as jnp
from jax import lax
from jax.experimental import pallas as pl
from jax.experimental.pallas import tpu as pltpu

F32 = jnp.float32
BF16 = jnp.bfloat16

D_MODEL = 1024
N_META = 16
TILE = 256
PAD_ROWS = TILE - N_META
RET_HEADS = 4
RET_QK = 256
RET_V = 512
RET_W = RET_HEADS * RET_V
GLA_HEADS = 4
GLA_K = 128
GLA_V = 256
GLA_W = GLA_HEADS * GLA_V
GLA_RANK = 16
GLA_TAU = 16.0
GLA_CHUNK = 16
ROPE_BASE = 10000.0
EPS = 1e-6
LANES = 128
N_DEV = 8
SEG_NAMES = ("rq", "rk", "rv", "rg", "gq", "gk", "gv", "gg", "mr", "mg")
SEG_W = (1024, 1024, 2048, 2048, 512, 512, 1024, 1024, 1024, 1024)
SEG_OFF = tuple(sum(SEG_W[:i]) for i in range(len(SEG_W)))
AL_COLS = sum(SEG_W)
IN_COLS = AL_COLS + GLA_RANK
GLR_OFF = sum(SEG_W[:8])
IN_SHARD = IN_COLS // N_DEV


def _aligned_col(c):
    assert c <= GLR_OFF or c >= GLR_OFF + GLA_RANK
    return c if c <= GLR_OFF else c - GLA_RANK


SLAB_BOUND = tuple(_aligned_col(IN_SHARD * d) for d in range(N_DEV + 1))
SLAB_BLK0 = tuple(b // LANES for b in SLAB_BOUND[:-1])
SLAB_SHIFT = tuple(b % LANES for b in SLAB_BOUND[:-1])
SLAB_BLOCKS = max(-(-SLAB_BOUND[d + 1] // LANES) - SLAB_BLK0[d] for d in range(N_DEV))
SLAB_W = SLAB_BLOCKS * LANES
GLR_DEV = GLR_OFF // IN_SHARD
GLR_LOCAL = GLR_OFF - GLR_DEV * IN_SHARD
assert all(SLAB_BLK0[d] + SLAB_BLOCKS <= AL_COLS // LANES for d in range(N_DEV))
VMEM_LIMIT = 58 * 1024 * 1024
ADAM_LR, ADAM_B1, ADAM_B2, ADAM_EPS, ADAM_WD, ADAM_STEP = 0.001, 0.9, 0.999, 1e-08, 0.01, 10
ANY = pl.BlockSpec(memory_space=pl.ANY)
MESH = pl.DeviceIdType.MESH


def _call(body, name, **kw):
    return pl.pallas_call(body, name=name, **kw)


def _params(sem=None):
    return pltpu.CompilerParams(dimension_semantics=sem, vmem_limit_bytes=VMEM_LIMIT)


def _mm(a, b):
    return jnp.dot(a, b, preferred_element_type=F32)


def _mm_nt(a, b):
    return lax.dot_general(a, b, (((1,), (1,)), ((), ())), preferred_element_type=F32)


def _mm_tn(a, b):
    return lax.dot_general(a, b, (((0,), (0,)), ((), ())), preferred_element_type=F32)


def _sigmoid(x):
    return 1.0 / (1.0 + jnp.exp(-x))


def _rope(t, cos, sin):
    half = t.shape[-1] // 2
    t1, t2 = t[:, :half], t[:, half:]
    return jnp.concatenate([t1 * cos - t2 * sin, t2 * cos + t1 * sin], axis=-1)


def _rope_bwd(g, cos, sin):
    half = g.shape[-1] // 2
    g1, g2 = g[:, :half], g[:, half:]
    return jnp.concatenate([g1 * cos + g2 * sin, g2 * cos - g1 * sin], axis=-1)


def _row_mean(x):
    return jnp.mean(x, axis=-1, keepdims=True)


def _col_sum(x):
    return jnp.sum(x, axis=0, keepdims=True)


def _tile_rows(head_ref, x_ref):
    return jnp.where(pl.program_id(0) == 0, head_ref[...], x_ref[...])


def _head_spec():
    return pl.BlockSpec((TILE, D_MODEL), lambda i: (0, 0))


def _x_spec():
    return pl.BlockSpec((TILE, D_MODEL), lambda i: (jnp.maximum(i - 1, 0), 0))


def _slab_plan():
    interior, shared = [], []
    for d in range(N_DEV):
        lo, hi = -(-SLAB_BOUND[d] // LANES), SLAB_BOUND[d + 1] // LANES
        interior.append((d, LANES * (lo - SLAB_BLK0[d]), LANES * lo, LANES * (hi - lo)))
        if d + 1 < N_DEV and SLAB_BOUND[d + 1] % LANES:
            shared.append((hi, d, hi - SLAB_BLK0[d]))
    return interior, shared


W_SCRATCH = lambda: [pltpu.VMEM((D_MODEL, AL_COLS), BF16), pltpu.VMEM((D_MODEL, LANES), BF16),
                     pltpu.VMEM((2 * (N_DEV - 1), D_MODEL, LANES), BF16), pltpu.SemaphoreType.DMA((3 * N_DEV,))]


def _load_weight(slabs_hbm, wg_hbm, w_vm, wg_vm, edge_vm, sem):
    interior, shared = _slab_plan()
    copies = [pltpu.make_async_copy(wg_hbm, wg_vm, sem.at[0])]
    for d, src, dst, width in interior:
        copies.append(pltpu.make_async_copy(slabs_hbm.at[d, :, pl.ds(src, width)], w_vm.at[:, pl.ds(dst, width)], sem.at[1 + d]))
    for n, (_, d, blk) in enumerate(shared):
        copies.append(pltpu.make_async_copy(slabs_hbm.at[d, :, pl.ds(LANES * blk, LANES)], edge_vm.at[2 * n], sem.at[1 + N_DEV + 2 * n]))
        copies.append(pltpu.make_async_copy(slabs_hbm.at[d + 1, :, pl.ds(0, LANES)], edge_vm.at[2 * n + 1], sem.at[2 + N_DEV + 2 * n]))
    for cp in copies:
        cp.start()
    for cp in copies:
        cp.wait()
    for n, (blk, _, _) in enumerate(shared):
        w_vm[:, LANES * blk:LANES * (blk + 1)] = edge_vm[2 * n] + edge_vm[2 * n + 1]


def _proj_specs(names, n_units, where):
    specs = []
    for name in names:
        s = SEG_NAMES.index(name)
        nblk = SEG_W[s] // n_units // LANES
        base = SEG_OFF[s] // LANES
        assert base % nblk == 0
        specs.append(pl.BlockSpec((nblk, TILE, LANES), lambda *g, base=base, nblk=nblk: (base // nblk + where(*g)[0], where(*g)[1], 0)))
    return specs


def _cols(ref):
    return ref[0] if ref.shape[0] == 1 else jnp.concatenate([ref[j] for j in range(ref.shape[0])], axis=1)


def _prenorm(head, x, g_norm, w_glr):
    t_rows = x.shape[0] + TILE
    nt = t_rows // TILE

    def body(head_ref, x_ref, g_ref, wg_ref, u_ref, ut_ref, glr_ref):
        x = _tile_rows(head_ref, x_ref)
        r = lax.rsqrt(_row_mean(x * x) + EPS)
        u32 = (x * r * g_ref[...]).astype(BF16).astype(F32)
        u = u32.astype(BF16)
        u_ref[...] = u
        ut_ref[...] = u32.T.astype(BF16)
        glr_ref[...] = _mm(u, wg_ref[...])

    row = lambda w: pl.BlockSpec((TILE, w), lambda i: (i, 0))
    return _call(
        body, "prenorm", grid=(nt,),
        out_shape=[jax.ShapeDtypeStruct((t_rows, D_MODEL), BF16), jax.ShapeDtypeStruct((nt, D_MODEL, TILE), BF16),
                   jax.ShapeDtypeStruct((t_rows, LANES), F32)],
        in_specs=[_head_spec(), _x_spec(), pl.BlockSpec((1, D_MODEL), lambda i: (0, 0)), pl.BlockSpec((D_MODEL, LANES), lambda i: (0, 0))],
        out_specs=[row(D_MODEL), pl.BlockSpec((None, D_MODEL, TILE), lambda i: (i, 0, 0)), row(LANES)],
        compiler_params=_params(("arbitrary",)),
    )(head, x, g_norm, w_glr)


SLAB_INNER = 9


def _edge_blocks():
    inner = {SLAB_BLK0[d] + j for d in range(N_DEV) for j in range(1, 1 + SLAB_INNER)}
    edges = []
    for blk in range(AL_COLS // LANES):
        if blk not in inner:
            srcs = [(d, blk - SLAB_BLK0[d]) for d in range(N_DEV)
                    if SLAB_BLK0[d] <= blk < SLAB_BLK0[d] + SLAB_BLOCKS and SLAB_BOUND[d] < LANES * (blk + 1) and LANES * blk < SLAB_BOUND[d + 1]]
            edges.append((blk, srcs))
    return edges


def _inproj_fwd(u, slab_local, row_shards):
    t_rows = u.shape[0]
    nt = t_rows // TILE
    rc = (3 if nt % 3 == 0 else 1) * TILE
    n_chunks = t_rows // rc
    ns = len(row_shards)
    edges = _edge_blocks()
    ne = len(edges)
    runs = []
    for k, (blk, _) in enumerate(edges):
        if runs and edges[runs[-1][0] + runs[-1][1] - 1][0] + 1 == blk:
            runs[-1] = (runs[-1][0], runs[-1][1] + 1)
        else:
            runs.append((k, 1))
    n_stage = sum(len(srcs) for _, srcs in edges)

    def body(u_hbm, slab_hbm, *rest):
        shard_refs, proj_hbm, slabs_hbm, gathered = rest[:ns], rest[ns], rest[ns + 1], rest[ns + 2:2 * ns + 2]
        (u_vm, wbuf, obuf, ebuf, stage, ebuf_out, sem_u, sem_w, sem_o, sem_s, sem_eo, send_sems, recv_sems,
         sem_l) = rest[2 * ns + 2:2 * ns + 16]
        x, y, c = _position()
        me, sibling = (x, y, c), (x, y, 1 - c)
        chips = [(1 - x, y), (x, 1 - y), (1 - x, 1 - y)]

        def slab_copy(k, block, to, src=None):
            dst = slabs_hbm.at[_index(*block)]
            return pltpu.make_async_remote_copy(src_ref=dst if src is None else src, dst_ref=dst, send_sem=send_sems.at[k],
                                                recv_sem=recv_sems.at[k], device_id=to, device_id_type=MESH)

        mine = pltpu.make_async_copy(slab_hbm, slabs_hbm.at[_index(*me)], sem_l)
        mine.start()
        first = [slab_copy(0, me, sibling, src=slab_hbm)] + [slab_copy(1 + j, me, (*chip, c), src=slab_hbm) for j, chip in enumerate(chips)]
        for cp in first:
            cp.start()
        rows_gather = _Exchange(shard_refs, gathered, rest[2 * ns + 16:], among_chips=False)
        rows_gather.start()
        load_u = pltpu.make_async_copy(u_hbm, u_vm, sem_u)
        load_u.start()
        load_u.wait()

        def store(slot, block0, rows0):
            return pltpu.make_async_copy(obuf.at[slot], proj_hbm.at[pl.ds(block0, SLAB_INNER), pl.ds(rows0, rc)], sem_o.at[slot])

        def multiply(dev):
            load_w = pltpu.make_async_copy(slabs_hbm.at[dev, :, pl.ds(LANES, SLAB_INNER * LANES)], wbuf, sem_w)
            load_w.start()
            load_w.wait()
            block0 = _slab_block0(dev) + 1

            def chunk(r, carry):
                slot = lax.rem(r, 2)
                rows0 = pl.multiple_of(r * rc, rc)

                @pl.when(r >= 2)
                def _():
                    store(slot, block0, rows0).wait()

                res = _mm(u_vm[pl.ds(rows0, rc), :], wbuf[...])
                for j in range(SLAB_INNER):
                    obuf[slot, j] = res[:, j * LANES:(j + 1) * LANES].astype(BF16)
                store(slot, block0, rows0).start()
                return carry

            lax.fori_loop(0, n_chunks, chunk, 0)
            for r in range(max(0, n_chunks - 2), n_chunks):
                store(r % 2, block0, r * rc).wait()

        mine.wait()
        multiply(_index(*me))
        slab_copy(0, sibling, me).wait_recv()
        multiply(_index(*sibling))
        passed = []
        for j, chip in enumerate(chips[:2]):
            slab_copy(1 + j, (*chip, c), me).wait_recv()
            passed.append(slab_copy(4 + j, (*chip, c), sibling))
            passed[-1].start()
        multiply(_index(*chips[0], c))
        multiply(_index(*chips[1], c))
        for j, chip in enumerate(chips[:2]):
            slab_copy(4 + j, (*chip, 1 - c), me).wait_recv()
            if j == 1:
                slab_copy(3, (*chips[2], c), me).wait_recv()
                passed.append(slab_copy(6, (*chips[2], c), sibling))
                passed[-1].start()
            multiply(_index(*chip, 1 - c))
        multiply(_index(*chips[2], c))
        slab_copy(6, (*chips[2], 1 - c), me).wait_recv()
        multiply(_index(*chips[2], 1 - c))

        loads, n = [], 0
        for k, (_, srcs) in enumerate(edges):
            for d, j in srcs:
                dst = ebuf.at[:, pl.ds(k * LANES, LANES)] if len(srcs) == 1 else stage.at[n]
                loads.append(pltpu.make_async_copy(slabs_hbm.at[d, :, pl.ds(j * LANES, LANES)], dst, sem_s.at[n]))
                n += 1
        for cp in loads:
            cp.start()
        for cp in loads:
            cp.wait()
        n = 0
        for k, (_, srcs) in enumerate(edges):
            if len(srcs) == 2:
                ebuf[:, k * LANES:(k + 1) * LANES] = stage[n] + stage[n + 1]
            n += len(srcs)

        def edge_stores(slot, rows0):
            return [pltpu.make_async_copy(ebuf_out.at[slot, pl.ds(k0, length)],
                                          proj_hbm.at[pl.ds(edges[k0][0], length), pl.ds(rows0, rc)], sem_eo.at[slot, i])
                    for i, (k0, length) in enumerate(runs)]

        def edge_chunk(r, carry):
            slot = lax.rem(r, 2)
            rows0 = pl.multiple_of(r * rc, rc)

            @pl.when(r >= 2)
            def _():
                for cp in edge_stores(slot, rows0):
                    cp.wait()

            res = _mm(u_vm[pl.ds(rows0, rc), :], ebuf[...])
            for k in range(ne):
                ebuf_out[slot, k] = res[:, k * LANES:(k + 1) * LANES].astype(BF16)
            for cp in edge_stores(slot, rows0):
                cp.start()
            return carry

        lax.fori_loop(0, n_chunks, edge_chunk, 0)
        for r in range(max(0, n_chunks - 2), n_chunks):
            for cp in edge_stores(r % 2, r * rc):
                cp.wait()

        for cp in first + passed:
            cp.wait_send()
        rows_gather.finish()

    outs = _call(
        body, "inproj_fwd",
        out_shape=[jax.ShapeDtypeStruct((AL_COLS // LANES, t_rows, LANES), BF16), jax.ShapeDtypeStruct((N_DEV, D_MODEL, SLAB_W), BF16)]
                  + [jax.ShapeDtypeStruct((N_DEV, *a.shape), a.dtype) for a in row_shards],
        in_specs=[ANY] * (2 + ns), out_specs=[ANY] * (2 + ns),
        scratch_shapes=[pltpu.VMEM((t_rows, D_MODEL), BF16), pltpu.VMEM((D_MODEL, SLAB_INNER * LANES), BF16),
                        pltpu.VMEM((2, SLAB_INNER, rc, LANES), BF16), pltpu.VMEM((D_MODEL, ne * LANES), BF16),
                        pltpu.VMEM((n_stage, D_MODEL, LANES), BF16), pltpu.VMEM((2, ne, rc, LANES), BF16),
                        pltpu.SemaphoreType.DMA, pltpu.SemaphoreType.DMA, pltpu.SemaphoreType.DMA((2,)),
                        pltpu.SemaphoreType.DMA((n_stage,)), pltpu.SemaphoreType.DMA((2, len(runs))),
                        pltpu.SemaphoreType.DMA((7,)), pltpu.SemaphoreType.DMA((7,)), pltpu.SemaphoreType.DMA]
                       + _exchange_sems(ns, N_DEV),
        compiler_params=_params(),
    )(u, slab_local, *row_shards)
    return outs[0], outs[1], outs[2:]


def _ret_decay(lgh):
    i = lax.broadcasted_iota(jnp.int32, (TILE, TILE), 0)
    j = lax.broadcasted_iota(jnp.int32, (TILE, TILE), 1)
    rel = (i - j).astype(F32)
    return jnp.where(rel >= 0, jnp.exp(jnp.maximum(rel, 0.0) * lgh), 0.0)


def _ret_vectors(lgh):
    idx = lax.broadcasted_iota(jnp.int32, (TILE, 1), 0).astype(F32)
    xi = jnp.exp((idx + 1.0) * lgh)
    zeta = jnp.exp((TILE - 1.0 - idx) * lgh)
    gc = jnp.exp(jnp.full((1, 1), float(TILE), F32) * lgh)
    return xi, zeta, gc


def _ret_fwd(proj, cos, sin, gain, lg):
    t_rows = cos.shape[0]
    nt = t_rows // TILE

    def body(lg_ref, q_ref, k_ref, v_ref, g_ref, cos_ref, sin_ref, gain_ref, oraw_ref, oret_ref, st_ref, s_acc, dm):
        h, t = pl.program_id(0), pl.program_id(1)
        lgh = lg_ref[h]

        @pl.when(t == 0)
        def _():
            s_acc[...] = jnp.zeros_like(s_acc)
            dm[...] = _ret_decay(lgh)

        cos_t, sin_t = cos_ref[...], sin_ref[...]
        q = _rope(_cols(q_ref).astype(F32), cos_t, sin_t)
        k = _rope(_cols(k_ref).astype(F32), cos_t, sin_t) * (RET_QK ** -0.5)
        xi, zeta, gc = _ret_vectors(lgh)
        v = _cols(v_ref)
        s_in = s_acc[...]
        p = (_mm_nt(q.astype(BF16), k.astype(BF16)) * dm[...]).astype(BF16)
        o = _mm(p, v) + _mm((q * xi).astype(BF16), s_in.astype(BF16))
        st_ref[...] = s_in.astype(BF16)
        s_acc[...] = s_in * gc + _mm_tn((k * zeta).astype(BF16), v)
        oraw_ref[...] = o
        oc = o - _row_mean(o)
        n = oc * lax.rsqrt(_row_mean(oc * oc) + EPS) * gain_ref[...]
        g = _cols(g_ref).astype(F32)
        oret_ref[...] = (n * g * _sigmoid(g)).astype(BF16)

    blk = lambda w: pl.BlockSpec((TILE, w), lambda h, t: (t, h))
    tab = pl.BlockSpec((TILE, LANES), lambda h, t: (t, 0))
    return _call(
        body, "ret_fwd", grid=(RET_HEADS, nt),
        out_shape=[jax.ShapeDtypeStruct((t_rows, RET_W), F32), jax.ShapeDtypeStruct((t_rows, RET_W), BF16),
                   jax.ShapeDtypeStruct((RET_HEADS, nt, RET_QK, RET_V), BF16)],
        in_specs=[pl.BlockSpec(memory_space=pltpu.SMEM)] + _proj_specs(("rq", "rk", "rv", "rg"), RET_HEADS, lambda h, t: (h, t)) + [tab, tab,
                  pl.BlockSpec((1, RET_V), lambda h, t: (0, h))],
        out_specs=[blk(RET_V), blk(RET_V), pl.BlockSpec((None, None, RET_QK, RET_V), lambda h, t: (h, t, 0, 0))],
        scratch_shapes=[pltpu.VMEM((RET_QK, RET_V), F32), pltpu.VMEM((TILE, TILE), F32)],
        compiler_params=_params(("arbitrary", "arbitrary")),
    )(lg, proj, proj, proj, proj, cos, sin, gain)


def _ret_bwd(proj, cos, sin, gain, lg, o_raw, do_ret, states):
    t_rows = cos.shape[0]
    nt = t_rows // TILE

    def body(lg_ref, q_ref, k_ref, v_ref, g_ref, cos_ref, sin_ref, gain_ref, oraw_ref, do_ref, st_ref,
             dq_ref, dk_ref, dv_ref, dg_ref, dgain_ref, e_acc, dm):
        h, j = pl.program_id(0), pl.program_id(1)
        lgh = lg_ref[h]

        @pl.when(j == 0)
        def _():
            e_acc[...] = jnp.zeros_like(e_acc)
            dm[...] = _ret_decay(lgh)
            dgain_ref[...] = jnp.zeros_like(dgain_ref)

        cos_t, sin_t = cos_ref[...], sin_ref[...]
        q = _rope(_cols(q_ref).astype(F32), cos_t, sin_t)
        k = _rope(_cols(k_ref).astype(F32), cos_t, sin_t) * (RET_QK ** -0.5)
        xi, zeta, gc = _ret_vectors(lgh)
        v = _cols(v_ref)
        g = _cols(g_ref).astype(F32)
        o = oraw_ref[...]
        do = do_ref[...].astype(F32)
        oc = o - _row_mean(o)
        rstd = lax.rsqrt(_row_mean(oc * oc) + EPS)
        xh = oc * rstd
        gain_t = gain_ref[...]
        sg = _sigmoid(g)
        dn = do * (g * sg)
        dg_ref[...] = (do * (xh * gain_t) * (sg * (1.0 + g * (1.0 - sg)))).astype(BF16)
        dgain_ref[...] += _col_sum(dn * xh)
        dxh = dn * gain_t
        dob = (rstd * (dxh - _row_mean(dxh) - xh * _row_mean(dxh * xh))).astype(BF16)
        dmat = dm[...]
        qb, kb = q.astype(BF16), k.astype(BF16)
        p = (_mm_nt(qb, kb) * dmat).astype(BF16)
        dp = (_mm_nt(dob, v) * dmat).astype(BF16)
        s_in = st_ref[...]
        e_in = e_acc[...]
        e_b = e_in.astype(BF16)
        dq = _mm(dp, kb) + _mm_nt(dob, s_in) * xi
        dk = _mm_tn(dp, qb) + _mm_nt(v, e_b) * zeta
        dv_ref[...] = (_mm_tn(p, dob) + _mm((k * zeta).astype(BF16), e_b)).astype(BF16)
        e_acc[...] = e_in * gc + _mm_tn((q * xi).astype(BF16), dob)
        dq_ref[...] = _rope_bwd(dq, cos_t, sin_t).astype(BF16)
        dk_ref[...] = (_rope_bwd(dk, cos_t, sin_t) * (RET_QK ** -0.5)).astype(BF16)

    blk = lambda w: pl.BlockSpec((TILE, w), lambda h, j: (nt - 1 - j, h))
    tab = pl.BlockSpec((TILE, LANES), lambda h, j: (nt - 1 - j, 0))
    vec = pl.BlockSpec((1, RET_V), lambda h, j: (0, h))
    return _call(
        body, "ret_bwd", grid=(RET_HEADS, nt),
        out_shape=[jax.ShapeDtypeStruct((t_rows, RET_HEADS * RET_QK), BF16), jax.ShapeDtypeStruct((t_rows, RET_HEADS * RET_QK), BF16),
                   jax.ShapeDtypeStruct((t_rows, RET_W), BF16), jax.ShapeDtypeStruct((t_rows, RET_W), BF16),
                   jax.ShapeDtypeStruct((1, RET_W), F32)],
        in_specs=[pl.BlockSpec(memory_space=pltpu.SMEM)] + _proj_specs(("rq", "rk", "rv", "rg"), RET_HEADS, lambda h, j: (h, nt - 1 - j)) + [tab, tab, vec,
                  blk(RET_V), blk(RET_V), pl.BlockSpec((None, None, RET_QK, RET_V), lambda h, j: (h, nt - 1 - j, 0, 0))],
        out_specs=[blk(RET_QK), blk(RET_QK), blk(RET_V), blk(RET_V), vec],
        scratch_shapes=[pltpu.VMEM((RET_QK, RET_V), F32), pltpu.VMEM((TILE, TILE), F32)],
        compiler_params=_params(("arbitrary", "arbitrary")),
    )(lg, proj, proj, proj, proj, cos, sin, gain, o_raw, do_ret, states)


GLA_LEVELS = (32, 64, 128, 256)
N_TERMS = 1 + len(GLA_LEVELS)


def _gla_tables():
    p = jnp.arange(TILE)[:, None]
    r = jnp.arange(TILE)[None, :]
    masks = [(p // GLA_CHUNK == r // GLA_CHUNK) & (r <= p)]
    for blk in GLA_LEVELS:
        masks.append((p // blk == r // blk) & (p % blk >= blk // 2) & (r % blk < blk // 2))
    masks = jnp.stack(masks + [m.T for m in masks]).astype(F32)
    cum_fwd = jnp.concatenate([r <= p, masks[0] > 0], axis=0).astype(BF16)
    cum_bwd = jnp.concatenate([r >= p, masks[N_TERMS] > 0], axis=1).astype(BF16)
    return masks, cum_fwd, cum_bwd


def _split3(x):
    hi = x.astype(BF16)
    rest = x - hi.astype(F32)
    mid = rest.astype(BF16)
    lo = (rest - mid.astype(F32)).astype(BF16)
    return jnp.concatenate([hi, mid, lo], axis=1)


def _join3(y):
    w = y.shape[1] // 3
    return (y[:, 2 * w:] + y[:, w:2 * w]) + y[:, :w]


def _gla_prep(q_ref, k_ref, glr_ref, wgu_ref, b_ref, cum_ref, g_scr, ref_scr):
    z = _mm(glr_ref[...].astype(BF16), wgu_ref[...].astype(BF16)) + b_ref[...]
    la = (jnp.minimum(z, 0.0) - jnp.log(1.0 + jnp.exp(-jnp.abs(z)))) / GLA_TAU
    gb = _join3(_mm(cum_ref[...], _split3(la)))
    g, b = gb[:TILE], gb[TILE:]
    g_scr[...] = g
    factors = [(jnp.exp(b), jnp.exp(-b))]
    for lvl, blk in enumerate(GLA_LEVELS):
        for n in range(TILE // blk):
            ref_scr[lvl, n * blk:(n + 1) * blk, :] = jnp.broadcast_to(g_scr[pl.ds(n * blk + blk // 2 - 1, 1), :], (blk, GLA_K))
        x = g - ref_scr[lvl]
        factors.append((jnp.exp(jnp.minimum(x, 0.0)), jnp.exp(jnp.minimum(-x, 0.0))))
    g_last = g_scr[pl.ds(TILE - 1, 1), :]
    q = _cols(q_ref).astype(F32) * (GLA_K ** -0.5)
    k = _cols(k_ref).astype(F32)
    return z, q, k, factors, jnp.exp(g), jnp.exp(g_last), jnp.exp(g_last - g)


def _gla_scores(q, k, factors, m_ref):
    a = jnp.zeros((TILE, TILE), F32)
    for l, (fq, fk) in enumerate(factors):
        s = _mm_nt((q * fq).astype(BF16), (k * fk).astype(BF16))
        a = jnp.where(m_ref[l] > 0.0, s, a)
    return a


def _gla_fwd(proj, glr, wgu_pad, b_gate, gain, masks, cum_fwd):
    t_rows = glr.shape[0]
    nt = t_rows // TILE

    def body(q_ref, k_ref, v_ref, g_ref, glr_ref, wgu_ref, b_ref, gain_ref, m_ref, cum_ref, oraw_ref, ogla_ref, st_ref,
             s_acc, g_scr, ref_scr):
        @pl.when(pl.program_id(1) == 0)
        def _():
            s_acc[...] = jnp.zeros_like(s_acc)

        _, q, k, factors, e_g, e_last, e_end = _gla_prep(q_ref, k_ref, glr_ref, wgu_ref, b_ref, cum_ref, g_scr, ref_scr)
        v = _cols(v_ref)
        st = s_acc[...]
        st_ref[...] = st
        a = _gla_scores(q, k, factors, m_ref)
        o = _mm(a.astype(BF16), v) + _mm_nt((q * e_g).astype(BF16), st.astype(BF16))
        s_acc[...] = st * e_last + _mm(v.astype(F32).T.astype(BF16), (k * e_end).astype(BF16))
        oraw_ref[...] = o
        n = o * lax.rsqrt(_row_mean(o * o) + EPS) * gain_ref[...]
        g = _cols(g_ref).astype(F32)
        ogla_ref[...] = (n * g * _sigmoid(g)).astype(BF16)

    blk = lambda w: pl.BlockSpec((TILE, w), lambda h, t: (t, h))
    return _call(
        body, "gla_fwd", grid=(GLA_HEADS, nt),
        out_shape=[jax.ShapeDtypeStruct((t_rows, GLA_W), F32), jax.ShapeDtypeStruct((t_rows, GLA_W), BF16),
                   jax.ShapeDtypeStruct((GLA_HEADS, nt, GLA_V, GLA_K), F32)],
        in_specs=_proj_specs(("gq", "gk", "gv", "gg"), GLA_HEADS, lambda h, t: (h, t)) + [pl.BlockSpec((TILE, LANES), lambda h, t: (t, 0)),
                  pl.BlockSpec((LANES, GLA_K), lambda h, t: (0, h)), pl.BlockSpec((1, GLA_K), lambda h, t: (0, h)),
                  pl.BlockSpec((1, GLA_V), lambda h, t: (0, h)),
                  pl.BlockSpec((N_TERMS, TILE, TILE), lambda h, t: (0, 0, 0)), pl.BlockSpec((2 * TILE, TILE), lambda h, t: (0, 0))],
        out_specs=[blk(GLA_V), blk(GLA_V), pl.BlockSpec((None, None, GLA_V, GLA_K), lambda h, t: (h, t, 0, 0))],
        scratch_shapes=[pltpu.VMEM((GLA_V, GLA_K), F32), pltpu.VMEM((TILE, GLA_K), F32),
                        pltpu.VMEM((len(GLA_LEVELS), TILE, GLA_K), F32)],
        compiler_params=_params(("arbitrary", "arbitrary")),
    )(proj, proj, proj, proj, glr, wgu_pad, b_gate, gain, masks, cum_fwd)


def _gla_bwd(proj, glr, wgu_pad, b_gate, gain, o_raw, do_gla, states, masks, cum_fwd, cum_bwd):
    t_rows = glr.shape[0]
    nt = t_rows // TILE

    def body(q_ref, k_ref, v_ref, g_ref, glr_ref, wgu_ref, b_ref, gain_ref, m_ref, cum_ref, cumb_ref, oraw_ref, do_ref, st_ref,
             dq_ref, dk_ref, dv_ref, dg_ref, dglr_ref, dwgu_ref, dbg_ref, dgain_ref, d_acc, g_scr, ref_scr, dref_scr):
        @pl.when(pl.program_id(1) == 0)
        def _():
            d_acc[...] = jnp.zeros_like(d_acc)
            dwgu_ref[...] = jnp.zeros_like(dwgu_ref)
            dbg_ref[...] = jnp.zeros_like(dbg_ref)
            dgain_ref[...] = jnp.zeros_like(dgain_ref)

        z, q, k, factors, e_g, e_last, e_end = _gla_prep(q_ref, k_ref, glr_ref, wgu_ref, b_ref, cum_ref, g_scr, ref_scr)
        v = _cols(v_ref)
        o = oraw_ref[...]
        do = do_ref[...].astype(F32)
        g = _cols(g_ref).astype(F32)
        rinv = lax.rsqrt(_row_mean(o * o) + EPS)
        nh = o * rinv
        gain_t = gain_ref[...]
        sg = _sigmoid(g)
        dn = do * (g * sg)
        dg_ref[...] = (do * (nh * gain_t) * (sg * (1.0 + g * (1.0 - sg)))).astype(BF16)
        dgain_ref[...] += _col_sum(dn * nh)
        dnh = dn * gain_t
        dor = rinv * (dnh - nh * _row_mean(dnh * nh))
        dob = dor.astype(BF16)
        a_t = _gla_scores(q, k, factors, m_ref).T.astype(BF16)
        da = _mm_nt(dob, v)
        da_t = _mm_nt(v, dob)
        st_in = st_ref[...]
        d_out = d_acc[...]
        d_out_b = d_out.astype(BF16)
        qg, kg = q * e_g, k * e_end
        dqg = _mm(dob, st_in.astype(BF16))
        dkg = _mm(v, d_out_b)
        dv_ref[...] = (_mm(a_t, dob) + _mm_nt(kg.astype(BF16), d_out_b)).astype(BF16)
        d_acc[...] = d_out * e_last + _mm(dor.T.astype(BF16), qg.astype(BF16))
        dq = dqg * e_g
        dk = dkg * e_end
        dkg_kg = dkg * kg
        dg_cum = dqg * qg - dkg_kg
        db = None
        for l, (fq, fk) in enumerate(factors):
            qt, kt = q * fq, k * fk
            dqt = _mm(jnp.where(m_ref[l] > 0.0, da, 0.0).astype(BF16), kt.astype(BF16))
            dkt = _mm(jnp.where(m_ref[N_TERMS + l] > 0.0, da_t, 0.0).astype(BF16), qt.astype(BF16))
            dq = dq + dqt * fq
            dk = dk + dkt * fk
            diff = dqt * qt - dkt * kt
            if l == 0:
                db = diff
            else:
                dg_cum = dg_cum + diff
                dref_scr[l - 1] = diff
        dq_ref[...] = (dq * (GLA_K ** -0.5)).astype(BF16)
        dk_ref[...] = dk.astype(BF16)
        g_scr[...] = dg_cum
        g_scr[pl.ds(TILE - 1, 1), :] += e_last * _col_sum(d_out * st_in) + _col_sum(dkg_kg)
        for lvl, blk in enumerate(GLA_LEVELS):
            for n in range(TILE // blk):
                g_scr[pl.ds(n * blk + blk // 2 - 1, 1), :] -= _col_sum(dref_scr[lvl, n * blk:(n + 1) * blk, :])
        dla = _join3(_mm(cumb_ref[...], jnp.concatenate([_split3(g_scr[...]), _split3(db)], axis=0)))
        dz = dla * (1.0 / GLA_TAU) * _sigmoid(-z)
        dzb = dz.astype(BF16)
        dglr_ref[...] = _mm_nt(dzb, wgu_ref[...].astype(BF16))
        dwgu_ref[...] += _mm(glr_ref[...].T.astype(BF16), dzb)
        dbg_ref[...] += _col_sum(dz)

    blk = lambda w: pl.BlockSpec((TILE, w), lambda h, j: (nt - 1 - j, h))
    vec = lambda w: pl.BlockSpec((1, w), lambda h, j: (0, h))
    wspec = pl.BlockSpec((LANES, GLA_K), lambda h, j: (0, h))
    return _call(
        body, "gla_bwd", grid=(GLA_HEADS, nt),
        out_shape=[jax.ShapeDtypeStruct((t_rows, GLA_HEADS * GLA_K), BF16), jax.ShapeDtypeStruct((t_rows, GLA_HEADS * GLA_K), BF16),
                   jax.ShapeDtypeStruct((t_rows, GLA_W), BF16), jax.ShapeDtypeStruct((t_rows, GLA_W), BF16),
                   jax.ShapeDtypeStruct((GLA_HEADS, t_rows, LANES), F32), jax.ShapeDtypeStruct((LANES, GLA_HEADS * GLA_K), F32),
                   jax.ShapeDtypeStruct((1, GLA_HEADS * GLA_K), F32), jax.ShapeDtypeStruct((1, GLA_W), F32)],
        in_specs=_proj_specs(("gq", "gk", "gv", "gg"), GLA_HEADS, lambda h, j: (h, nt - 1 - j)) + [pl.BlockSpec((TILE, LANES), lambda h, j: (nt - 1 - j, 0)),
                  wspec, vec(GLA_K), vec(GLA_V),
                  pl.BlockSpec((2 * N_TERMS, TILE, TILE), lambda h, j: (0, 0, 0)), pl.BlockSpec((2 * TILE, TILE), lambda h, j: (0, 0)),
                  pl.BlockSpec((TILE, 2 * TILE), lambda h, j: (0, 0)), blk(GLA_V), blk(GLA_V),
                  pl.BlockSpec((None, None, GLA_V, GLA_K), lambda h, j: (h, nt - 1 - j, 0, 0))],
        out_specs=[blk(GLA_K), blk(GLA_K), blk(GLA_V), blk(GLA_V),
                   pl.BlockSpec((None, TILE, LANES), lambda h, j: (h, nt - 1 - j, 0)), wspec, vec(GLA_K), vec(GLA_V)],
        scratch_shapes=[pltpu.VMEM((GLA_V, GLA_K), F32), pltpu.VMEM((TILE, GLA_K), F32),
                        pltpu.VMEM((len(GLA_LEVELS), TILE, GLA_K), F32), pltpu.VMEM((len(GLA_LEVELS), TILE, GLA_K), F32)],
        compiler_params=_params(("arbitrary", "arbitrary")),
    )(proj, proj, proj, proj, glr, wgu_pad, b_gate, gain, masks, cum_fwd, cum_bwd, o_raw, do_gla, states)


def _merge_fwd_bwd(o_ret, o_gla, proj, x, target, g_final, w_br, w_bg, w_out):
    t_rows = x.shape[0] + TILE
    nt = t_rows // TILE

    def body(oret_ref, ogla_ref, mr_ref, mg_ref, h0_ref, tgt_ref, gf_ref, wbr_hbm, wbg_hbm, wout_hbm,
             dh1_ref, dmr_ref, dmg_ref, doret_ref, dogla_ref, loss_ref, dgf_ref, dwbr_hbm, dwbg_hbm, dwout_hbm,
             wbr, wbg, wout, abr, abg, aout, sem):
        i = pl.program_id(0)

        @pl.when(i == 0)
        def _():
            cps = [pltpu.make_async_copy(s, d, sem.at[n]) for n, (s, d) in enumerate(((wbr_hbm, wbr), (wbg_hbm, wbg), (wout_hbm, wout)))]
            for cp in cps:
                cp.start()
            abr[...] = jnp.zeros_like(abr)
            abg[...] = jnp.zeros_like(abg)
            aout[...] = jnp.zeros_like(aout)
            loss_ref[...] = jnp.zeros_like(loss_ref)
            dgf_ref[...] = jnp.zeros_like(dgf_ref)
            for cp in cps:
                cp.wait()
            dh1_ref[...] = jnp.zeros_like(dh1_ref)
            dmr_ref[...] = jnp.zeros_like(dmr_ref)
            dmg_ref[...] = jnp.zeros_like(dmg_ref)
            doret_ref[...] = jnp.zeros_like(doret_ref)
            dogla_ref[...] = jnp.zeros_like(dogla_ref)

        @pl.when(i > 0)
        def _():
            oret, ogla = oret_ref[...], ogla_ref[...]
            br, bg = _mm(oret, wbr[...]), _mm(ogla, wbg[...])
            sr, sg = _sigmoid(_cols(mr_ref).astype(F32)), _sigmoid(_cols(mg_ref).astype(F32))
            mb = (sr * br + sg * bg).astype(BF16)
            h1 = h0_ref[...] + _mm(mb, wout[...])
            r2 = lax.rsqrt(_row_mean(h1 * h1) + EPS)
            hn = h1 * r2
            gf = gf_ref[...]
            diff = hn * gf - tgt_ref[...]
            loss_ref[...] += 0.5 * jnp.sum(_row_mean(diff * diff))
            dy = diff * (1.0 / D_MODEL)
            dgf_ref[...] += _col_sum(dy * hn)
            dyg = dy * gf
            dh1 = r2 * (dyg - hn * _row_mean(dyg * hn))
            dh1_ref[...] = dh1
            dh1b = dh1.astype(BF16)
            dm = _mm_nt(dh1b, wout[...])
            aout[...] += _mm_tn(mb, dh1b)
            dbr = (dm * sr).astype(BF16)
            dbg = (dm * sg).astype(BF16)
            dmr_ref[...] = (dm * br * sr * (1.0 - sr)).astype(BF16)
            dmg_ref[...] = (dm * bg * sg * (1.0 - sg)).astype(BF16)
            doret_ref[...] = _mm_nt(dbr, wbr[...]).astype(BF16)
            dogla_ref[...] = _mm_nt(dbg, wbg[...]).astype(BF16)
            abr[...] += _mm_tn(oret, dbr)
            abg[...] += _mm_tn(ogla, dbg)

        @pl.when(i == nt - 1)
        def _():
            wbr[...] = abr[...].astype(BF16)
            wbg[...] = abg[...].astype(BF16)
            wout[...] = aout[...].astype(BF16)
            pltpu.sync_copy(wbr, dwbr_hbm)
            pltpu.sync_copy(wbg, dwbg_hbm)
            pltpu.sync_copy(wout, dwout_hbm)

    row = lambda w: pl.BlockSpec((TILE, w), lambda i: (i, 0))
    one = lambda w: pl.BlockSpec((1, w), lambda i: (0, 0))
    return _call(
        body, "merge_fwd_bwd", grid=(nt,),
        out_shape=[jax.ShapeDtypeStruct((t_rows, D_MODEL), F32), jax.ShapeDtypeStruct((t_rows, D_MODEL), BF16),
                   jax.ShapeDtypeStruct((t_rows, D_MODEL), BF16), jax.ShapeDtypeStruct((t_rows, RET_W), BF16),
                   jax.ShapeDtypeStruct((t_rows, GLA_W), BF16), jax.ShapeDtypeStruct((1, LANES), F32),
                   jax.ShapeDtypeStruct((1, D_MODEL), F32), jax.ShapeDtypeStruct((RET_W, D_MODEL), BF16),
                   jax.ShapeDtypeStruct((GLA_W, D_MODEL), BF16), jax.ShapeDtypeStruct((D_MODEL, D_MODEL), BF16)],
        in_specs=[row(RET_W), row(GLA_W)] + _proj_specs(("mr", "mg"), 1, lambda i: (0, i)) + [_x_spec(), _x_spec(), one(D_MODEL), ANY, ANY, ANY],
        out_specs=[row(D_MODEL), row(D_MODEL), row(D_MODEL), row(RET_W), row(GLA_W), one(LANES), one(D_MODEL), ANY, ANY, ANY],
        scratch_shapes=[pltpu.VMEM((RET_W, D_MODEL), BF16), pltpu.VMEM((GLA_W, D_MODEL), BF16), pltpu.VMEM((D_MODEL, D_MODEL), BF16),
                        pltpu.VMEM((RET_W, D_MODEL), F32), pltpu.VMEM((GLA_W, D_MODEL), F32), pltpu.VMEM((D_MODEL, D_MODEL), F32),
                        pltpu.SemaphoreType.DMA((3,))],
        compiler_params=_params(("arbitrary",)),
    )(o_ret, o_gla, proj, proj, x, target, g_final, w_br, w_bg, w_out)


def _sum_heads(parts):
    def body(p_ref, o_ref):
        o_ref[...] = (p_ref[0] + p_ref[1] + p_ref[2] + p_ref[3]).astype(BF16)

    t_rows = parts.shape[1]
    return _call(
        body, "sum_heads_dglr", grid=(t_rows // TILE,), out_shape=jax.ShapeDtypeStruct((t_rows, LANES), BF16),
        in_specs=[pl.BlockSpec((GLA_HEADS, TILE, LANES), lambda i: (0, i, 0))], out_specs=pl.BlockSpec((TILE, LANES), lambda i: (i, 0)),
        compiler_params=_params(("arbitrary",)),
    )(parts)


def _inproj_bwd_x(dseg, dglr, head, x, dh1, g_norm, slabs, w_glr, chip_partials):
    t_rows = x.shape[0] + TILE
    nt = t_rows // TILE
    ne = len(chip_partials)

    def body(*refs):
        d_refs = refs[:10]
        dglr_ref, head_ref, x_ref, dh1_ref, g_ref, slabs_hbm, wg_hbm = refs[10:17]
        part_refs = refs[17:17 + ne]
        dx_ref, dhead_ref, dgn_ref = refs[17 + ne:20 + ne]
        landed = refs[20 + ne:20 + 2 * ne]
        w_vm, wg_vm, edge_vm, sem = refs[20 + 2 * ne:24 + 2 * ne]
        exchange = _Exchange(part_refs, landed, refs[24 + 2 * ne:], among_chips=True)

        @pl.when(pl.program_id(0) == 0)
        def _():
            exchange.start()
            dgn_ref[...] = jnp.zeros_like(dgn_ref)
            _load_weight(slabs_hbm, wg_hbm, w_vm, wg_vm, edge_vm, sem)

        @pl.when(pl.program_id(0) == nt - 1)
        def _():
            exchange.finish()

        du = _mm_nt(dglr_ref[...], wg_vm[...])
        for s, d_ref in enumerate(d_refs):
            du = du + _mm_nt(d_ref[...], w_vm[:, SEG_OFF[s]:SEG_OFF[s] + SEG_W[s]])
        x = _tile_rows(head_ref, x_ref)
        r = lax.rsqrt(_row_mean(x * x) + EPS)
        hn = x * r
        dgn_ref[...] += _col_sum(du * hn)
        dug = du * g_ref[...]
        dh0 = dh1_ref[...] + r * (dug - hn * _row_mean(dug * hn))
        dx_ref[...] = dh0

        @pl.when(pl.program_id(0) == 0)
        def _():
            dhead_ref[...] = dh0

    row = lambda w: pl.BlockSpec((TILE, w), lambda i: (i, 0))
    one = pl.BlockSpec((1, D_MODEL), lambda i: (0, 0))
    return _call(
        body, "inproj_bwd_x", grid=(nt,),
        out_shape=[jax.ShapeDtypeStruct((t_rows - TILE, D_MODEL), F32), jax.ShapeDtypeStruct((TILE, D_MODEL), F32),
                   jax.ShapeDtypeStruct((1, D_MODEL), F32)] + [jax.ShapeDtypeStruct(a.shape, a.dtype) for a in chip_partials],
        in_specs=[row(w) for w in SEG_W] + [row(LANES), _head_spec(), _x_spec(), row(D_MODEL), one, ANY, ANY] + [ANY] * ne,
        out_specs=[_x_spec(), _head_spec(), one] + [ANY] * ne,
        scratch_shapes=W_SCRATCH() + _exchange_sems(ne, N_CHIP),
        compiler_params=_params(("arbitrary",)),
    )(*[dseg[n] for n in SEG_NAMES], dglr, head, x, dh1, g_norm, slabs, w_glr, *chip_partials)


W_TILE = 512


def _inproj_bwd_w(ut, dseg, dglr):
    nt = ut.shape[0]
    t_rows = nt * TILE
    kc = 3 if nt % 3 == 0 else 1
    tiles = [(s, c) for s in range(len(SEG_W)) for c in range(0, SEG_W[s], W_TILE)]
    bpt = W_TILE // LANES

    def body(ut_hbm, *refs):
        d_refs, dglr_hbm, out_hbm, oglr_ref = refs[:10], refs[10], refs[11], refs[12]
        ut_vm, dbuf, obuf, acc, gbuf, sem = refs[13:]

        def fetch(i):
            s, c = tiles[i]
            return pltpu.make_async_copy(d_refs[s].at[:, pl.ds(c, W_TILE)], dbuf.at[i % 2], sem.at[1 + i % 2])

        def contract(rhs_ref, width):
            acc[:, :width] = jnp.zeros((D_MODEL, width), F32)

            def step(k, carry):
                part = None
                for j in range(kc):
                    kk = k * kc + j
                    prod = _mm(ut_vm[kk], rhs_ref[pl.ds(pl.multiple_of(kk * TILE, TILE), TILE), :])
                    part = prod if part is None else part + prod
                acc[:, :width] += part
                return carry

            lax.fori_loop(0, nt // kc, step, 0)
            return acc[:, :width]

        load_ut = pltpu.make_async_copy(ut_hbm, ut_vm, sem.at[0])
        load_glr = pltpu.make_async_copy(dglr_hbm, gbuf, sem.at[5])
        load_ut.start()
        load_glr.start()
        fetch(0).start()
        load_ut.wait()
        stores = {}
        for i, (s, c) in enumerate(tiles):
            if i + 1 < len(tiles):
                fetch(i + 1).start()
            fetch(i).wait()
            if i >= 2:
                stores[i - 2].wait()
            total = contract(dbuf.at[i % 2], W_TILE)
            for j in range(bpt):
                obuf[i % 2, j] = total[:, j * LANES:(j + 1) * LANES].astype(BF16)
            blk0 = (SEG_OFF[s] + c) // LANES
            stores[i] = pltpu.make_async_copy(obuf.at[i % 2], out_hbm.at[pl.ds(blk0, bpt)], sem.at[3 + i % 2])
            stores[i].start()
        load_glr.wait()
        oglr_ref[...] = contract(gbuf, LANES)
        for i in range(max(0, len(tiles) - 2), len(tiles)):
            stores[i].wait()

    return _call(
        body, "inproj_bwd_w",
        out_shape=[jax.ShapeDtypeStruct((AL_COLS // LANES, D_MODEL, LANES), BF16), jax.ShapeDtypeStruct((D_MODEL, LANES), F32)],
        in_specs=[ANY] * 12, out_specs=[ANY, pl.BlockSpec(memory_space=pltpu.VMEM)],
        scratch_shapes=[pltpu.VMEM((nt, D_MODEL, TILE), BF16), pltpu.VMEM((2, t_rows, W_TILE), BF16),
                        pltpu.VMEM((2, bpt, D_MODEL, LANES), BF16), pltpu.VMEM((D_MODEL, W_TILE), F32),
                        pltpu.VMEM((t_rows, LANES), BF16), pltpu.SemaphoreType.DMA((6,))],
        compiler_params=_params(),
    )(ut, *[dseg[n] for n in SEG_NAMES], dglr)


def _position():
    x, y, c = lax.axis_index("x"), lax.axis_index("y"), lax.axis_index("c")
    return x, y, c


def _index(px, py, pc):
    return 4 * px + 2 * py + pc


def _all_gather(arrs, name):
    n = len(arrs)

    def body(*refs):
        ins, outs = refs[:n], refs[n:2 * n]
        send_sems, recv_sems, local_sems = refs[2 * n:]
        x, y, c = _position()
        me, sibling = (x, y, c), (x, y, 1 - c)
        chips = [(1 - x, y), (x, 1 - y), (1 - x, 1 - y)]

        def copy(a, k, block, to, src=None):
            dst = outs[a].at[_index(*block)]
            return pltpu.make_async_remote_copy(src_ref=dst if src is None else src, dst_ref=dst,
                                                send_sem=send_sems.at[7 * a + k], recv_sem=recv_sems.at[7 * a + k],
                                                device_id=to, device_id_type=MESH)

        mine = [pltpu.make_async_copy(ins[a], outs[a].at[_index(*me)], local_sems.at[a]) for a in range(n)]
        for cp in mine:
            cp.start()
        first = []
        for a in range(n):
            first.append(copy(a, 0, me, sibling, src=ins[a]))
            first += [copy(a, 1 + j, me, (*chip, c), src=ins[a]) for j, chip in enumerate(chips)]
        for cp in first:
            cp.start()
        passed = []
        for j, chip in enumerate(chips):
            for a in range(n):
                copy(a, 1 + j, (*chip, c), me).wait_recv()
                cp = copy(a, 4 + j, (*chip, c), sibling)
                cp.start()
                passed.append(cp)
        for a in range(n):
            copy(a, 0, sibling, me).wait_recv()
            for j, chip in enumerate(chips):
                copy(a, 4 + j, (*chip, 1 - c), me).wait_recv()
        for cp in first + passed:
            cp.wait_send()
        for cp in mine:
            cp.wait()

    return _call(
        body, name,
        out_shape=[jax.ShapeDtypeStruct((N_DEV, *a.shape), a.dtype) for a in arrs],
        in_specs=[ANY] * n, out_specs=[ANY] * n,
        scratch_shapes=[pltpu.SemaphoreType.DMA((7 * n,)), pltpu.SemaphoreType.DMA((7 * n,)), pltpu.SemaphoreType.DMA((n,))],
    )(*arrs)


N_CHIP = N_DEV // 2


def _slab_block0(owner):
    step = SLAB_BLK0[1]
    assert all(SLAB_BLK0[d] == step * d - (d == N_DEV - 1) for d in range(N_DEV))
    return step * owner - jnp.where(owner == N_DEV - 1, 1, 0)


def _exchange_sibling(dw_blocks, row_sends):
    n = 1 + len(row_sends)

    def body(*refs):
        dw_ref, row_refs, outs, (send_sems, recv_sems) = refs[0], refs[1:n], refs[n:2 * n], refs[2 * n:]
        x, y, c = _position()
        copies = []
        for q in range(N_CHIP):
            owner = 2 * q + (1 - c)
            srcs = [dw_ref.at[pl.ds(_slab_block0(owner), SLAB_BLOCKS)]] + [r.at[owner] for r in row_refs]
            for k, src in enumerate(srcs):
                copies.append(pltpu.make_async_remote_copy(src_ref=src, dst_ref=outs[k].at[q], send_sem=send_sems.at[n * q + k],
                                                           recv_sem=recv_sems.at[n * q + k], device_id=(x, y, 1 - c), device_id_type=MESH))
        for cp in copies:
            cp.start()
        for cp in copies:
            cp.wait()

    return _call(
        body, "exchange_sibling",
        out_shape=[jax.ShapeDtypeStruct((N_CHIP, SLAB_BLOCKS, D_MODEL, LANES), BF16)]
                  + [jax.ShapeDtypeStruct((N_CHIP, *r.shape[1:]), BF16) for r in row_sends],
        in_specs=[ANY] * n, out_specs=[ANY] * n,
        scratch_shapes=[pltpu.SemaphoreType.DMA((n * N_CHIP,)), pltpu.SemaphoreType.DMA((n * N_CHIP,))],
    )(dw_blocks, *row_sends)


def _add_bf16(c_ref, a_ref, b_ref, o_ref):
    o_ref[...] = (a_ref[...].astype(F32) + b_ref[...].astype(F32)).astype(BF16)


def _chip_partial_slab(dw_blocks, sib, core):
    blk = pl.BlockSpec((None, 1, D_MODEL, LANES), lambda q, j, c_ref: (q, j, 0, 0))
    return _call(
        functools.partial(_add_bf16), "chip_partial_w_in", out_shape=jax.ShapeDtypeStruct(sib.shape, BF16),
        grid_spec=pltpu.PrefetchScalarGridSpec(
            num_scalar_prefetch=1, grid=(N_CHIP, SLAB_BLOCKS),
            in_specs=[pl.BlockSpec((1, D_MODEL, LANES), lambda q, j, c_ref: (_slab_block0(2 * q + c_ref[0]) + j, 0, 0)), blk],
            out_specs=blk),
        compiler_params=_params(("arbitrary", "arbitrary")),
    )(core, dw_blocks, sib)


def _chip_partial_rows(send, sib, core, name):
    rows, cols = send.shape[1:]
    blk = pl.BlockSpec((None, rows, cols), lambda q, c_ref: (q, 0, 0))
    return _call(
        functools.partial(_add_bf16), name, out_shape=jax.ShapeDtypeStruct(sib.shape, BF16),
        grid_spec=pltpu.PrefetchScalarGridSpec(
            num_scalar_prefetch=1, grid=(N_CHIP,),
            in_specs=[pl.BlockSpec((None, rows, cols), lambda q, c_ref: (2 * q + c_ref[0], 0, 0)), blk], out_specs=blk),
        compiler_params=_params(("arbitrary",)),
    )(core, send, sib)


def _exchange_sems(n_arrays, n_peers):
    return [pltpu.SemaphoreType.DMA((n_arrays * n_peers,)), pltpu.SemaphoreType.DMA((n_arrays * n_peers,)),
            pltpu.SemaphoreType.DMA((n_arrays,))]


class _Exchange:
    def __init__(self, srcs, dsts, sems, among_chips):
        self.arrs = list(zip(srcs, dsts))
        self.n = len(self.arrs)
        self.send_sems, self.recv_sems, self.local_sems = sems
        self.among_chips = among_chips
        x, y, c = _position()
        self.c = c
        self.me = 2 * x + y if among_chips else _index(x, y, c)
        self.n_peers = N_CHIP if among_chips else N_DEV

    def _device(self, p):
        return (p // 2, p % 2, self.c) if self.among_chips else (p // 4, (p // 2) % 2, p % 2)

    def _src(self, k, p):
        src = self.arrs[k][0]
        return src.at[p] if self.among_chips else src

    def _mine(self):
        return [pltpu.make_async_copy(self._src(k, self.me), self.arrs[k][1].at[self.me], self.local_sems.at[k]) for k in range(self.n)]

    def _copy(self, p, k, landing):
        return pltpu.make_async_remote_copy(
            src_ref=self._src(k, p), dst_ref=self.arrs[k][1].at[landing], send_sem=self.send_sems.at[self.n * p + k],
            recv_sem=self.recv_sems.at[self.n * landing + k], device_id=self._device(p), device_id_type=MESH)

    def _others(self, fn):
        for p in range(self.n_peers):
            @pl.when(p != self.me)
            def _():
                for k in range(self.n):
                    fn(p, k)

    def start(self):
        for cp in self._mine():
            cp.start()
        self._others(lambda p, k: self._copy(p, k, self.me).start())

    def finish(self):
        self._others(lambda p, k: self._copy(p, k, p).wait_recv())
        self._others(lambda p, k: self._copy(p, k, self.me).wait_send())
        for cp in self._mine():
            cp.wait()


def _adamw(g, w, m, v):
    m_new = ADAM_B1 * m + (1.0 - ADAM_B1) * g
    v_new = ADAM_B2 * v + (1.0 - ADAM_B2) * (g * g)
    m_hat = m_new / (1.0 - ADAM_B1 ** ADAM_STEP)
    v_hat = v_new / (1.0 - ADAM_B2 ** ADAM_STEP)
    delta = -ADAM_LR * (m_hat / (jnp.sqrt(v_hat) + ADAM_EPS) + ADAM_WD * w)
    return delta, m_new, v_new


def _sum_partials(p_ref):
    g = p_ref[0].astype(F32)
    for d in range(1, p_ref.shape[0]):
        g = g + p_ref[d].astype(F32)
    return g


def _reduce_adam(parts, w, m, v, name, block_rows, row_off=0):
    rows, cols = w.shape
    off = row_off // block_rows

    def body(p_ref, w_ref, m_ref, v_ref, g_ref, d_ref, mo_ref, vo_ref):
        g = _sum_partials(p_ref)
        g_ref[...] = g
        d_ref[...], mo_ref[...], vo_ref[...] = _adamw(g, w_ref[...], m_ref[...], v_ref[...])

    blk = pl.BlockSpec((block_rows, cols), lambda i: (i, 0))
    return _call(
        body, name, grid=(rows // block_rows,),
        out_shape=[jax.ShapeDtypeStruct((rows, cols), F32)] * 4,
        in_specs=[pl.BlockSpec((parts.shape[0], block_rows, cols), lambda i: (0, i + off, 0)), blk, blk, blk],
        out_specs=[blk] * 4,
        compiler_params=_params(("arbitrary",)),
    )(parts, w, m, v)


def _reduce_adam_slab(parts, glr, w, m, v, me):
    rows, cols = w.shape
    shift = jnp.asarray(SLAB_SHIFT, jnp.int32)[me]
    glr_at = jnp.where(me == GLR_DEV, GLR_LOCAL, cols).astype(jnp.int32)

    def body(s_ref, p_ref, glr_ref, w_ref, m_ref, v_ref, g_ref, d_ref, mo_ref, vo_ref):
        shift, glr_at = s_ref[0], s_ref[1]
        slab = jnp.concatenate([_sum_partials(p_ref.at[:, j]) for j in range(SLAB_BLOCKS)], axis=1)
        before = pltpu.roll(slab, SLAB_W - shift, 1)
        after = pltpu.roll(slab, lax.rem(SLAB_W - shift + GLA_RANK, SLAB_W), 1)
        wide = jnp.concatenate([glr_ref[...], jnp.zeros((LANES, SLAB_W - LANES), F32)], axis=1)
        placed = pltpu.roll(wide, lax.rem(glr_at, SLAB_W), 1)
        lane = lax.broadcasted_iota(jnp.int32, (LANES, SLAB_W), 1)
        g = jnp.where(lane < glr_at, before, jnp.where(lane < glr_at + GLA_RANK, placed, after))[:, :cols]
        g_ref[...] = g
        d_ref[...], mo_ref[...], vo_ref[...] = _adamw(g, w_ref[...], m_ref[...], v_ref[...])

    blk = pl.BlockSpec((LANES, cols), lambda i, s: (i, 0))
    return _call(
        body, "adam_w_in", out_shape=[jax.ShapeDtypeStruct((rows, cols), F32)] * 4,
        grid_spec=pltpu.PrefetchScalarGridSpec(
            num_scalar_prefetch=1, grid=(rows // LANES,),
            in_specs=[pl.BlockSpec((parts.shape[0], SLAB_BLOCKS, LANES, LANES), lambda i, s: (0, 0, i, 0)),
                      pl.BlockSpec((LANES, LANES), lambda i, s: (i, 0)), blk, blk, blk],
            out_specs=[blk] * 4),
        compiler_params=_params(("arbitrary",)),
    )(jnp.stack([shift, glr_at]), parts, glr, w, m, v)


def _reduce_small(parts):
    def body(p_ref, o_ref):
        o_ref[...] = _sum_partials(p_ref)

    return _call(body, "reduce_small", out_shape=jax.ShapeDtypeStruct(parts.shape[1:], F32))(parts)


def _adam_small(g, w, m, v):
    def body(g_ref, w_ref, m_ref, v_ref, d_ref, mo_ref, vo_ref):
        d_ref[...], mo_ref[...], vo_ref[...] = _adamw(g_ref[...], w_ref[...], m_ref[...], v_ref[...])

    return _call(body, "adam_small", out_shape=[jax.ShapeDtypeStruct(g.shape, F32)] * 3)(g, w, m, v)


def _pack_rows(arrs):
    rows = []
    for a in arrs:
        flat = a.reshape(-1).astype(F32)
        pad = (-flat.shape[0]) % LANES
        rows.append(jnp.pad(flat, (0, pad)).reshape(-1, LANES))
    packed = jnp.concatenate(rows, axis=0)
    return jnp.pad(packed, ((0, (-packed.shape[0]) % 8), (0, 0)))


def _unpack_rows(packed, shapes):
    out, r = [], 0
    for shp in shapes:
        size = 1
        for s in shp:
            size *= s
        nrows = -(-size // LANES)
        out.append(packed[r:r + nrows].reshape(-1)[:size].reshape(shp))
        r += nrows
    return out


def _shard_to_slab(shard, d):
    glr = jnp.zeros((D_MODEL, GLA_RANK), shard.dtype)
    if d == GLR_DEV:
        glr = shard[:, GLR_LOCAL:GLR_LOCAL + GLA_RANK]
        shard = jnp.concatenate([shard[:, :GLR_LOCAL], shard[:, GLR_LOCAL + GLA_RANK:]], axis=1)
    return jnp.pad(shard, ((0, 0), (SLAB_SHIFT[d], SLAB_W - SLAB_SHIFT[d] - shard.shape[1]))), glr


def kernel(x, meta_tokens, norm_gain, w_in, w_gate_up, b_gate, ret_norm_gain, gla_norm_gain, w_branch_ret, w_branch_gla, w_out, final_norm_gain, loss_target, m_meta_tokens, m_norm_gain, m_w_in, m_w_gate_up, m_b_gate, m_ret_norm_gain, m_gla_norm_gain, m_w_branch_ret, m_w_branch_gla, m_w_out, m_final_norm_gain, v_meta_tokens, v_norm_gain, v_w_in, v_w_gate_up, v_b_gate, v_ret_norm_gain, v_gla_norm_gain, v_w_branch_ret, v_w_branch_gla, v_w_out, v_final_norm_gain):
    xi, yi, ci = _position()
    me = _index(xi, yi, ci)
    seq = x.shape[1]
    t_rows = seq + TILE
    in_shard = w_in.shape[2]
    gu_shard = w_gate_up.shape[2]
    meta_shard = meta_tokens.shape[1]
    ret_rows, gla_rows, out_rows = w_branch_ret.shape[1], w_branch_gla.shape[1], w_out.shape[1]

    assert in_shard == IN_SHARD
    slab_local, glr_local = lax.switch(me, [functools.partial(_shard_to_slab, d=d) for d in range(N_DEV)], w_in[0])
    small_local = jnp.concatenate([meta_tokens, jnp.pad(w_gate_up[0], ((0, 0), (0, LANES - gu_shard))),
                                   glr_local.reshape(-1, LANES)], axis=0)
    (g_small,) = _all_gather([small_local], "all_gather_small_weights")
    n_small = N_META + GLA_RANK
    w_glr = jnp.pad(g_small[GLR_DEV, n_small:].reshape(D_MODEL, GLA_RANK), ((0, 0), (0, LANES - GLA_RANK))).astype(BF16)
    meta_full = jnp.transpose(g_small[:, :N_META, :], (1, 0, 2)).reshape(N_META, D_MODEL)
    wgu_full = jnp.transpose(g_small[:, N_META:n_small, :gu_shard], (1, 0, 2)).reshape(GLA_RANK, GLA_HEADS * GLA_K)
    wgu_pad = jnp.pad(wgu_full, ((0, LANES - GLA_RANK), (0, 0)))

    pos = jnp.arange(t_rows, dtype=F32) - float(PAD_ROWS)
    half = RET_QK // 2
    inv = ROPE_BASE ** (-jnp.arange(half, dtype=F32) / half)
    ang = pos[:, None] * inv[None, :]
    cos, sin = jnp.cos(ang), jnp.sin(ang)
    lg = jnp.log1p(-(2.0 ** (-5.0 - jnp.arange(RET_HEADS, dtype=F32))))

    head = jnp.concatenate([jnp.zeros((PAD_ROWS, D_MODEL), F32), meta_full], axis=0)
    u, ut, glr = _prenorm(head, x[0], norm_gain, w_glr)
    proj, slabs, (g_br, g_bg, g_o) = _inproj_fwd(
        u, slab_local.astype(BF16), [w_branch_ret[0].astype(BF16), w_branch_gla[0].astype(BF16), w_out[0].astype(BF16)])
    w_br, w_bg, w_o = g_br.reshape(RET_W, D_MODEL), g_bg.reshape(GLA_W, D_MODEL), g_o.reshape(D_MODEL, D_MODEL)
    o_ret_raw, o_ret, ret_states = _ret_fwd(proj, cos, sin, ret_norm_gain, lg)
    masks, cum_fwd, cum_bwd = _gla_tables()
    o_gla_raw, o_gla, gla_states = _gla_fwd(proj, glr, wgu_pad, b_gate, gla_norm_gain, masks, cum_fwd)
    (dh1, d_mr, d_mg, do_ret, do_gla, loss_part, d_gfinal, dw_br, dw_bg, dw_o) = _merge_fwd_bwd(
        o_ret, o_gla, proj, x[0], loss_target[0], final_norm_gain.reshape(1, D_MODEL), w_br, w_bg, w_o)

    d_rq, d_rk, d_rv, d_rg, d_gret = _ret_bwd(proj, cos, sin, ret_norm_gain, lg, o_ret_raw, do_ret, ret_states)
    d_gq, d_gk, d_gv, d_gg, dglr_parts, d_wgu, d_bgate, d_ggla = _gla_bwd(
        proj, glr, wgu_pad, b_gate, gla_norm_gain, o_gla_raw, do_gla, gla_states, masks, cum_fwd, cum_bwd)
    dseg = dict(rq=d_rq, rk=d_rk, rv=d_rv, rg=d_rg, gq=d_gq, gk=d_gk, gv=d_gv, gg=d_gg, mr=d_mr, mg=d_mg)
    dglr = _sum_heads(dglr_parts)
    dw_blocks, dw_glr = _inproj_bwd_w(ut, dseg, dglr)

    row_sends = [dw_br.reshape(N_DEV, ret_rows, D_MODEL), dw_bg.reshape(N_DEV, gla_rows, D_MODEL),
                 dw_o.reshape(N_DEV, out_rows, D_MODEL)]
    sib_in, *sib_rows = _exchange_sibling(dw_blocks, row_sends)
    core = ci.astype(jnp.int32).reshape(1)
    chip_partials = [_chip_partial_slab(dw_blocks, sib_in, core)] + [
        _chip_partial_rows(send, sib, core, "chip_partial_" + name)
        for send, sib, name in zip(row_sends, sib_rows, ("w_branch_ret", "w_branch_gla", "w_out"))]
    grad_x, d_head, d_gnorm, p_in, p_br, p_bg, p_o = _inproj_bwd_x(
        dseg, dglr, head, x[0], dh1, norm_gain, slabs, w_glr, chip_partials)
    small_shapes = [(N_META, D_MODEL), (1, D_MODEL), (GLA_RANK, GLA_HEADS * GLA_K), (1, GLA_HEADS * GLA_K),
                    (1, RET_W), (1, GLA_W), (1, D_MODEL), (1, LANES), (D_MODEL, GLA_RANK)]
    small_part = _pack_rows([d_head[PAD_ROWS:], d_gnorm, d_wgu[:GLA_RANK], d_bgate, d_gret, d_ggla, d_gfinal, loss_part,
                             dw_glr[:, :GLA_RANK]])
    (p_small,) = _all_gather([small_part], "all_gather_small_partials")

    (g_meta_f, g_gnorm, g_wgu_f, g_bgate, g_gret, g_ggla, g_gfinal, loss_all,
     g_wglr) = _unpack_rows(_reduce_small(p_small), small_shapes)
    g_w_in, d_w_in, nm_w_in, nv_w_in = _reduce_adam_slab(
        p_in, jnp.pad(g_wglr, ((0, 0), (0, LANES - GLA_RANK))), w_in[0], m_w_in[0], v_w_in[0], me)
    rb = gla_rows
    g_w_br, d_w_br, nm_w_br, nv_w_br = _reduce_adam(p_br, w_branch_ret[0], m_w_branch_ret[0], v_w_branch_ret[0], "adam_w_branch_ret", rb)
    g_w_bg, d_w_bg, nm_w_bg, nv_w_bg = _reduce_adam(p_bg, w_branch_gla[0], m_w_branch_gla[0], v_w_branch_gla[0], "adam_w_branch_gla", rb)
    g_w_o, d_w_o, nm_w_o, nv_w_o = _reduce_adam(p_o, w_out[0], m_w_out[0], v_w_out[0], "adam_w_out", rb)
    g_meta = lax.dynamic_slice_in_dim(g_meta_f, me * meta_shard, meta_shard, axis=1)
    g_wgu = lax.dynamic_slice_in_dim(g_wgu_f, me * gu_shard, gu_shard, axis=1)
    s_g = [g_meta, g_gnorm, g_wgu, g_bgate, g_gret, g_ggla, g_gfinal]
    s_w = [meta_tokens, norm_gain, w_gate_up[0], b_gate, ret_norm_gain, gla_norm_gain, final_norm_gain]
    s_m = [m_meta_tokens, m_norm_gain, m_w_gate_up[0], m_b_gate, m_ret_norm_gain, m_gla_norm_gain, m_final_norm_gain]
    s_v = [v_meta_tokens, v_norm_gain, v_w_gate_up[0], v_b_gate, v_ret_norm_gain, v_gla_norm_gain, v_final_norm_gain]
    shapes = [a.shape for a in s_g]
    s_d, s_nm, s_nv = [_unpack_rows(p, shapes) for p in _adam_small(*[_pack_rows(l) for l in (s_g, s_w, s_m, s_v)])]

    loss = loss_all[0, 0]
    grad_x = grad_x[None]

    def order(meta, gnorm, win, wgu, bgate, gret, ggla, wbr, wbg, wo, gfin):
        return (meta, gnorm, win[None], wgu[None], bgate, gret, ggla, wbr[None], wbg[None], wo[None], gfin.reshape(final_norm_gain.shape))

    def small(l):
        return dict(meta=l[0], gnorm=l[1], wgu=l[2], bgate=l[3], gret=l[4], ggla=l[5], gfin=l[6])

    grads = order(win=g_w_in, wbr=g_w_br, wbg=g_w_bg, wo=g_w_o, **small(s_g))
    deltas = order(win=d_w_in, wbr=d_w_br, wbg=d_w_bg, wo=d_w_o, **small(s_d))
    new_m = order(win=nm_w_in, wbr=nm_w_br, wbg=nm_w_bg, wo=nm_w_o, **small(s_nm))
    new_v = order(win=nv_w_in, wbr=nv_w_br, wbg=nv_w_bg, wo=nv_w_o, **small(s_nv))
    return (loss, grad_x, *grads, *deltas, *new_m, *new_v)
```

```python
import functools

import jax
import jax.numpy as jnp
from jax import lax
from jax.experimental import pallas as pl
from jax.experimental.pallas import tpu as pltpu

F32 = jnp.float32
BF16 = jnp.bfloat16

D_MODEL = 1024
N_META = 16
TILE = 256
PAD_ROWS = TILE - N_META
RET_HEADS = 4
RET_QK = 256
RET_V = 512
RET_W = RET_HEADS * RET_V
GLA_HEADS = 4
GLA_K = 128
GLA_V = 256
GLA_W = GLA_HEADS * GLA_V
GLA_RANK = 16
GLA_TAU = 16.0
GLA_CHUNK = 16
ROPE_BASE = 10000.0
EPS = 1e-6
LANES = 128
N_DEV = 8
SEG_NAMES = ("rq", "rk", "rv", "rg", "gq", "gk", "gv", "gg", "mr", "mg")
SEG_W = (1024, 1024, 2048, 2048, 512, 512, 1024, 1024, 1024, 1024)
SEG_OFF = tuple(sum(SEG_W[:i]) for i in range(len(SEG_W)))
AL_COLS = sum(SEG_W)
IN_COLS = AL_COLS + GLA_RANK
GLR_OFF = sum(SEG_W[:8])
IN_SHARD = IN_COLS // N_DEV


def _aligned_col(c):
    assert c <= GLR_OFF or c >= GLR_OFF + GLA_RANK
    return c if c <= GLR_OFF else c - GLA_RANK


SLAB_BOUND = tuple(_aligned_col(IN_SHARD * d) for d in range(N_DEV + 1))
SLAB_BLK0 = tuple(b // LANES for b in SLAB_BOUND[:-1])
SLAB_SHIFT = tuple(b % LANES for b in SLAB_BOUND[:-1])
SLAB_BLOCKS = max(-(-SLAB_BOUND[d + 1] // LANES) - SLAB_BLK0[d] for d in range(N_DEV))
SLAB_W = SLAB_BLOCKS * LANES
GLR_DEV = GLR_OFF // IN_SHARD
GLR_LOCAL = GLR_OFF - GLR_DEV * IN_SHARD
assert all(SLAB_BLK0[d] + SLAB_BLOCKS <= AL_COLS // LANES for d in range(N_DEV))
VMEM_LIMIT = 58 * 1024 * 1024
ADAM_LR, ADAM_B1, ADAM_B2, ADAM_EPS, ADAM_WD, ADAM_STEP = 0.001, 0.9, 0.999, 1e-08, 0.01, 10
ANY = pl.BlockSpec(memory_space=pl.ANY)
MESH = pl.DeviceIdType.MESH


def _call(body, name, **kw):
    return pl.pallas_call(body, name=name, **kw)


def _params(sem=None):
    return pltpu.CompilerParams(dimension_semantics=sem, vmem_limit_bytes=VMEM_LIMIT)


def _mm(a, b):
    return jnp.dot(a, b, preferred_element_type=F32)


def _mm_nt(a, b):
    return lax.dot_general(a, b, (((1,), (1,)), ((), ())), preferred_element_type=F32)


def _mm_tn(a, b):
    return lax.dot_general(a, b, (((0,), (0,)), ((), ())), preferred_element_type=F32)


def _sigmoid(x):
    return 1.0 / (1.0 + jnp.exp(-x))


def _rope(t, cos, sin):
    half = t.shape[-1] // 2
    t1, t2 = t[:, :half], t[:, half:]
    return jnp.concatenate([t1 * cos - t2 * sin, t2 * cos + t1 * sin], axis=-1)


def _rope_bwd(g, cos, sin):
    half = g.shape[-1] // 2
    g1, g2 = g[:, :half], g[:, half:]
    return jnp.concatenate([g1 * cos + g2 * sin, g2 * cos - g1 * sin], axis=-1)


def _row_mean(x):
    return jnp.mean(x, axis=-1, keepdims=True)


def _col_sum(x):
    return jnp.sum(x, axis=0, keepdims=True)


def _tile_rows(head_ref, x_ref):
    return jnp.where(pl.program_id(0) == 0, head_ref[...], x_ref[...])


def _head_spec():
    return pl.BlockSpec((TILE, D_MODEL), lambda i: (0, 0))


def _x_spec():
    return pl.BlockSpec((TILE, D_MODEL), lambda i: (jnp.maximum(i - 1, 0), 0))


def _slab_plan():
    interior, shared = [], []
    for d in range(N_DEV):
        lo, hi = -(-SLAB_BOUND[d] // LANES), SLAB_BOUND[d + 1] // LANES
        interior.append((d, LANES * (lo - SLAB_BLK0[d]), LANES * lo, LANES * (hi - lo)))
        if d + 1 < N_DEV and SLAB_BOUND[d + 1] % LANES:
            shared.append((hi, d, hi - SLAB_BLK0[d]))
    return interior, shared


W_SCRATCH = lambda: [pltpu.VMEM((D_MODEL, AL_COLS), BF16), pltpu.VMEM((D_MODEL, LANES), BF16),
                     pltpu.VMEM((2 * (N_DEV - 1), D_MODEL, LANES), BF16), pltpu.SemaphoreType.DMA((3 * N_DEV,))]


def _load_weight(slabs_hbm, wg_hbm, w_vm, wg_vm, edge_vm, sem):
    interior, shared = _slab_plan()
    copies = [pltpu.make_async_copy(wg_hbm, wg_vm, sem.at[0])]
    for d, src, dst, width in interior:
        copies.append(pltpu.make_async_copy(slabs_hbm.at[d, :, pl.ds(src, width)], w_vm.at[:, pl.ds(dst, width)], sem.at[1 + d]))
    for n, (_, d, blk) in enumerate(shared):
        copies.append(pltpu.make_async_copy(slabs_hbm.at[d, :, pl.ds(LANES * blk, LANES)], edge_vm.at[2 * n], sem.at[1 + N_DEV + 2 * n]))
        copies.append(pltpu.make_async_copy(slabs_hbm.at[d + 1, :, pl.ds(0, LANES)], edge_vm.at[2 * n + 1], sem.at[2 + N_DEV + 2 * n]))
    for cp in copies:
        cp.start()
    for cp in copies:
        cp.wait()
    for n, (blk, _, _) in enumerate(shared):
        w_vm[:, LANES * blk:LANES * (blk + 1)] = edge_vm[2 * n] + edge_vm[2 * n + 1]


def _proj_specs(names, n_units, where):
    specs = []
    for name in names:
        s = SEG_NAMES.index(name)
        nblk = SEG_W[s] // n_units // LANES
        base = SEG_OFF[s] // LANES
        assert base % nblk == 0
        specs.append(pl.BlockSpec((nblk, TILE, LANES), lambda *g, base=base, nblk=nblk: (base // nblk + where(*g)[0], where(*g)[1], 0)))
    return specs


def _cols(ref):
    return ref[0] if ref.shape[0] == 1 else jnp.concatenate([ref[j] for j in range(ref.shape[0])], axis=1)


def _prenorm(head, x, g_norm, w_glr):
    t_rows = x.shape[0] + TILE
    nt = t_rows // TILE

    def body(head_ref, x_ref, g_ref, wg_ref, u_ref, ut_ref, glr_ref):
        x = _tile_rows(head_ref, x_ref)
        r = lax.rsqrt(_row_mean(x * x) + EPS)
        u32 = (x * r * g_ref[...]).astype(BF16).astype(F32)
        u = u32.astype(BF16)
        u_ref[...] = u
        ut_ref[...] = u32.T.astype(BF16)
        glr_ref[...] = _mm(u, wg_ref[...])

    row = lambda w: pl.BlockSpec((TILE, w), lambda i: (i, 0))
    return _call(
        body, "prenorm", grid=(nt,),
        out_shape=[jax.ShapeDtypeStruct((t_rows, D_MODEL), BF16), jax.ShapeDtypeStruct((nt, D_MODEL, TILE), BF16),
                   jax.ShapeDtypeStruct((t_rows, LANES), F32)],
        in_specs=[_head_spec(), _x_spec(), pl.BlockSpec((1, D_MODEL), lambda i: (0, 0)), pl.BlockSpec((D_MODEL, LANES), lambda i: (0, 0))],
        out_specs=[row(D_MODEL), pl.BlockSpec((None, D_MODEL, TILE), lambda i: (i, 0, 0)), row(LANES)],
        compiler_params=_params(("arbitrary",)),
    )(head, x, g_norm, w_glr)


SLAB_INNER = 9


def _edge_blocks():
    inner = {SLAB_BLK0[d] + j for d in range(N_DEV) for j in range(1, 1 + SLAB_INNER)}
    edges = []
    for blk in range(AL_COLS // LANES):
        if blk not in inner:
            srcs = [(d, blk - SLAB_BLK0[d]) for d in range(N_DEV)
                    if SLAB_BLK0[d] <= blk < SLAB_BLK0[d] + SLAB_BLOCKS and SLAB_BOUND[d] < LANES * (blk + 1) and LANES * blk < SLAB_BOUND[d + 1]]
            edges.append((blk, srcs))
    return edges


def _inproj_fwd(u, slab_local):
    t_rows = u.shape[0]
    nt = t_rows // TILE
    rc = (3 if nt % 3 == 0 else 1) * TILE
    n_chunks = t_rows // rc
    edges = _edge_blocks()
    ne = len(edges)
    runs = []
    for k, (blk, _) in enumerate(edges):
        if runs and edges[runs[-1][0] + runs[-1][1] - 1][0] + 1 == blk:
            runs[-1] = (runs[-1][0], runs[-1][1] + 1)
        else:
            runs.append((k, 1))
    n_stage = sum(len(srcs) for _, srcs in edges)

    def body(u_hbm, slab_hbm, proj_hbm, slabs_hbm, u_vm, wbuf, obuf, ebuf, stage, ebuf_out, sem_u, sem_w, sem_o, sem_s, sem_eo,
             send_sems, recv_sems, sem_l):
        x, y, c = _position()
        me, sibling = (x, y, c), (x, y, 1 - c)
        chips = [(1 - x, y), (x, 1 - y), (1 - x, 1 - y)]

        def slab_copy(k, block, to, src=None):
            dst = slabs_hbm.at[_index(*block)]
            return pltpu.make_async_remote_copy(src_ref=dst if src is None else src, dst_ref=dst, send_sem=send_sems.at[k],
                                                recv_sem=recv_sems.at[k], device_id=to, device_id_type=MESH)

        mine = pltpu.make_async_copy(slab_hbm, slabs_hbm.at[_index(*me)], sem_l)
        mine.start()
        first = [slab_copy(0, me, sibling, src=slab_hbm)] + [slab_copy(1 + j, me, (*chip, c), src=slab_hbm) for j, chip in enumerate(chips)]
        for cp in first:
            cp.start()
        load_u = pltpu.make_async_copy(u_hbm, u_vm, sem_u)
        load_u.start()
        load_u.wait()

        def store(slot, block0, rows0):
            return pltpu.make_async_copy(obuf.at[slot], proj_hbm.at[pl.ds(block0, SLAB_INNER), pl.ds(rows0, rc)], sem_o.at[slot])

        def multiply(dev):
            load_w = pltpu.make_async_copy(slabs_hbm.at[dev, :, pl.ds(LANES, SLAB_INNER * LANES)], wbuf, sem_w)
            load_w.start()
            load_w.wait()
            block0 = _slab_block0(dev) + 1

            def chunk(r, carry):
                slot = lax.rem(r, 2)
                rows0 = pl.multiple_of(r * rc, rc)

                @pl.when(r >= 2)
                def _():
                    store(slot, block0, rows0).wait()

                res = _mm(u_vm[pl.ds(rows0, rc), :], wbuf[...])
                for j in range(SLAB_INNER):
                    obuf[slot, j] = res[:, j * LANES:(j + 1) * LANES].astype(BF16)
                store(slot, block0, rows0).start()
                return carry

            lax.fori_loop(0, n_chunks, chunk, 0)
            for r in range(max(0, n_chunks - 2), n_chunks):
                store(r % 2, block0, r * rc).wait()

        mine.wait()
        multiply(_index(*me))
        slab_copy(0, sibling, me).wait_recv()
        multiply(_index(*sibling))
        passed = []
        for j, chip in enumerate(chips[:2]):
            slab_copy(1 + j, (*chip, c), me).wait_recv()
            passed.append(slab_copy(4 + j, (*chip, c), sibling))
            passed[-1].start()
        multiply(_index(*chips[0], c))
        multiply(_index(*chips[1], c))
        for j, chip in enumerate(chips[:2]):
            slab_copy(4 + j, (*chip, 1 - c), me).wait_recv()
            if j == 1:
                slab_copy(3, (*chips[2], c), me).wait_recv()
                passed.append(slab_copy(6, (*chips[2], c), sibling))
                passed[-1].start()
            multiply(_index(*chip, 1 - c))
        multiply(_index(*chips[2], c))
        slab_copy(6, (*chips[2], 1 - c), me).wait_recv()
        multiply(_index(*chips[2], 1 - c))

        loads, n = [], 0
        for k, (_, srcs) in enumerate(edges):
            for d, j in srcs:
                dst = ebuf.at[:, pl.ds(k * LANES, LANES)] if len(srcs) == 1 else stage.at[n]
                loads.append(pltpu.make_async_copy(slabs_hbm.at[d, :, pl.ds(j * LANES, LANES)], dst, sem_s.at[n]))
                n += 1
        for cp in loads:
            cp.start()
        for cp in loads:
            cp.wait()
        n = 0
        for k, (_, srcs) in enumerate(edges):
            if len(srcs) == 2:
                ebuf[:, k * LANES:(k + 1) * LANES] = stage[n] + stage[n + 1]
            n += len(srcs)

        def edge_stores(slot, rows0):
            return [pltpu.make_async_copy(ebuf_out.at[slot, pl.ds(k0, length)],
                                          proj_hbm.at[pl.ds(edges[k0][0], length), pl.ds(rows0, rc)], sem_eo.at[slot, i])
                    for i, (k0, length) in enumerate(runs)]

        def edge_chunk(r, carry):
            slot = lax.rem(r, 2)
            rows0 = pl.multiple_of(r * rc, rc)

            @pl.when(r >= 2)
            def _():
                for cp in edge_stores(slot, rows0):
                    cp.wait()

            res = _mm(u_vm[pl.ds(rows0, rc), :], ebuf[...])
            for k in range(ne):
                ebuf_out[slot, k] = res[:, k * LANES:(k + 1) * LANES].astype(BF16)
            for cp in edge_stores(slot, rows0):
                cp.start()
            return carry

        lax.fori_loop(0, n_chunks, edge_chunk, 0)
        for r in range(max(0, n_chunks - 2), n_chunks):
            for cp in edge_stores(r % 2, r * rc):
                cp.wait()

        for cp in first + passed:
            cp.wait_send()

    return _call(
        body, "inproj_fwd",
        out_shape=[jax.ShapeDtypeStruct((AL_COLS // LANES, t_rows, LANES), BF16), jax.ShapeDtypeStruct((N_DEV, D_MODEL, SLAB_W), BF16)],
        in_specs=[ANY] * 2, out_specs=[ANY] * 2,
        scratch_shapes=[pltpu.VMEM((t_rows, D_MODEL), BF16), pltpu.VMEM((D_MODEL, SLAB_INNER * LANES), BF16),
                        pltpu.VMEM((2, SLAB_INNER, rc, LANES), BF16), pltpu.VMEM((D_MODEL, ne * LANES), BF16),
                        pltpu.VMEM((n_stage, D_MODEL, LANES), BF16), pltpu.VMEM((2, ne, rc, LANES), BF16),
                        pltpu.SemaphoreType.DMA, pltpu.SemaphoreType.DMA, pltpu.SemaphoreType.DMA((2,)),
                        pltpu.SemaphoreType.DMA((n_stage,)), pltpu.SemaphoreType.DMA((2, len(runs))),
                        pltpu.SemaphoreType.DMA((7,)), pltpu.SemaphoreType.DMA((7,)), pltpu.SemaphoreType.DMA],
        compiler_params=_params(),
    )(u, slab_local)


def _ret_decay(lgh):
    i = lax.broadcasted_iota(jnp.int32, (TILE, TILE), 0)
    j = lax.broadcasted_iota(jnp.int32, (TILE, TILE), 1)
    rel = (i - j).astype(F32)
    return jnp.where(rel >= 0, jnp.exp(jnp.maximum(rel, 0.0) * lgh), 0.0)


def _ret_vectors(lgh):
    idx = lax.broadcasted_iota(jnp.int32, (TILE, 1), 0).astype(F32)
    xi = jnp.exp((idx + 1.0) * lgh)
    zeta = jnp.exp((TILE - 1.0 - idx) * lgh)
    gc = jnp.exp(jnp.full((1, 1), float(TILE), F32) * lgh)
    return xi, zeta, gc


def _ret_fwd(proj, cos, sin, gain, lg, row_shards):
    t_rows = cos.shape[0]
    nt = t_rows // TILE
    ns = len(row_shards)

    def body(lg_ref, q_ref, k_ref, v_ref, g_ref, cos_ref, sin_ref, gain_ref, *rest):
        shard_refs, (oraw_ref, oret_ref, st_ref), gathered = rest[:ns], rest[ns:ns + 3], rest[ns + 3:2 * ns + 3]
        s_acc, dm = rest[2 * ns + 3:2 * ns + 5]
        gather = _Exchange(shard_refs, gathered, rest[2 * ns + 5:], among_chips=False)
        h, t = pl.program_id(0), pl.program_id(1)
        lgh = lg_ref[h]

        @pl.when((h == 0) & (t == 0))
        def _():
            gather.start()

        @pl.when((h == RET_HEADS - 1) & (t == nt - 1))
        def _():
            gather.finish()

        @pl.when(t == 0)
        def _():
            s_acc[...] = jnp.zeros_like(s_acc)
            dm[...] = _ret_decay(lgh)

        cos_t, sin_t = cos_ref[...], sin_ref[...]
        q = _rope(_cols(q_ref).astype(F32), cos_t, sin_t)
        k = _rope(_cols(k_ref).astype(F32), cos_t, sin_t) * (RET_QK ** -0.5)
        xi, zeta, gc = _ret_vectors(lgh)
        v = _cols(v_ref)
        s_in = s_acc[...]
        p = (_mm_nt(q.astype(BF16), k.astype(BF16)) * dm[...]).astype(BF16)
        o = _mm(p, v) + _mm((q * xi).astype(BF16), s_in.astype(BF16))
        st_ref[...] = s_in.astype(BF16)
        s_acc[...] = s_in * gc + _mm_tn((k * zeta).astype(BF16), v)
        oraw_ref[...] = o
        oc = o - _row_mean(o)
        n = oc * lax.rsqrt(_row_mean(oc * oc) + EPS) * gain_ref[...]
        g = _cols(g_ref).astype(F32)
        oret_ref[...] = (n * g * _sigmoid(g)).astype(BF16)

    blk = lambda w: pl.BlockSpec((TILE, w), lambda h, t: (t, h))
    tab = pl.BlockSpec((TILE, LANES), lambda h, t: (t, 0))
    outs = _call(
        body, "ret_fwd", grid=(RET_HEADS, nt),
        out_shape=[jax.ShapeDtypeStruct((t_rows, RET_W), F32), jax.ShapeDtypeStruct((t_rows, RET_W), BF16),
                   jax.ShapeDtypeStruct((RET_HEADS, nt, RET_QK, RET_V), BF16)]
                  + [jax.ShapeDtypeStruct((N_DEV, *a.shape), a.dtype) for a in row_shards],
        in_specs=[pl.BlockSpec(memory_space=pltpu.SMEM)] + _proj_specs(("rq", "rk", "rv", "rg"), RET_HEADS, lambda h, t: (h, t)) + [tab, tab,
                  pl.BlockSpec((1, RET_V), lambda h, t: (0, h))] + [ANY] * ns,
        out_specs=[blk(RET_V), blk(RET_V), pl.BlockSpec((None, None, RET_QK, RET_V), lambda h, t: (h, t, 0, 0))] + [ANY] * ns,
        scratch_shapes=[pltpu.VMEM((RET_QK, RET_V), F32), pltpu.VMEM((TILE, TILE), F32)] + _exchange_sems(ns, N_DEV),
        compiler_params=_params(("arbitrary", "arbitrary")),
    )(lg, proj, proj, proj, proj, cos, sin, gain, *row_shards)
    return outs[0], outs[1], outs[2], outs[3:]


def _ret_bwd(proj, cos, sin, gain, lg, o_raw, do_ret, states):
    t_rows = cos.shape[0]
    nt = t_rows // TILE

    def body(lg_ref, q_ref, k_ref, v_ref, g_ref, cos_ref, sin_ref, gain_ref, oraw_ref, do_ref, st_ref,
             dq_ref, dk_ref, dv_ref, dg_ref, dgain_ref, e_acc, dm):
        h, j = pl.program_id(0), pl.program_id(1)
        lgh = lg_ref[h]

        @pl.when(j == 0)
        def _():
            e_acc[...] = jnp.zeros_like(e_acc)
            dm[...] = _ret_decay(lgh)
            dgain_ref[...] = jnp.zeros_like(dgain_ref)

        cos_t, sin_t = cos_ref[...], sin_ref[...]
        q = _rope(_cols(q_ref).astype(F32), cos_t, sin_t)
        k = _rope(_cols(k_ref).astype(F32), cos_t, sin_t) * (RET_QK ** -0.5)
        xi, zeta, gc = _ret_vectors(lgh)
        v = _cols(v_ref)
        g = _cols(g_ref).astype(F32)
        o = oraw_ref[...]
        do = do_ref[...].astype(F32)
        oc = o - _row_mean(o)
        rstd = lax.rsqrt(_row_mean(oc * oc) + EPS)
        xh = oc * rstd
        gain_t = gain_ref[...]
        sg = _sigmoid(g)
        dn = do * (g * sg)
        dg_ref[...] = (do * (xh * gain_t) * (sg * (1.0 + g * (1.0 - sg)))).astype(BF16)
        dgain_ref[...] += _col_sum(dn * xh)
        dxh = dn * gain_t
        dob = (rstd * (dxh - _row_mean(dxh) - xh * _row_mean(dxh * xh))).astype(BF16)
        dmat = dm[...]
        qb, kb = q.astype(BF16), k.astype(BF16)
        p = (_mm_nt(qb, kb) * dmat).astype(BF16)
        dp = (_mm_nt(dob, v) * dmat).astype(BF16)
        s_in = st_ref[...]
        e_in = e_acc[...]
        e_b = e_in.astype(BF16)
        dq = _mm(dp, kb) + _mm_nt(dob, s_in) * xi
        dk = _mm_tn(dp, qb) + _mm_nt(v, e_b) * zeta
        dv_ref[...] = (_mm_tn(p, dob) + _mm((k * zeta).astype(BF16), e_b)).astype(BF16)
        e_acc[...] = e_in * gc + _mm_tn((q * xi).astype(BF16), dob)
        dq_ref[...] = _rope_bwd(dq, cos_t, sin_t).astype(BF16)
        dk_ref[...] = (_rope_bwd(dk, cos_t, sin_t) * (RET_QK ** -0.5)).astype(BF16)

    blk = lambda w: pl.BlockSpec((TILE, w), lambda h, j: (nt - 1 - j, h))
    tab = pl.BlockSpec((TILE, LANES), lambda h, j: (nt - 1 - j, 0))
    vec = pl.BlockSpec((1, RET_V), lambda h, j: (0, h))
    return _call(
        body, "ret_bwd", grid=(RET_HEADS, nt),
        out_shape=[jax.ShapeDtypeStruct((t_rows, RET_HEADS * RET_QK), BF16), jax.ShapeDtypeStruct((t_rows, RET_HEADS * RET_QK), BF16),
                   jax.ShapeDtypeStruct((t_rows, RET_W), BF16), jax.ShapeDtypeStruct((t_rows, RET_W), BF16),
                   jax.ShapeDtypeStruct((1, RET_W), F32)],
        in_specs=[pl.BlockSpec(memory_space=pltpu.SMEM)] + _proj_specs(("rq", "rk", "rv", "rg"), RET_HEADS, lambda h, j: (h, nt - 1 - j)) + [tab, tab, vec,
                  blk(RET_V), blk(RET_V), pl.BlockSpec((None, None, RET_QK, RET_V), lambda h, j: (h, nt - 1 - j, 0, 0))],
        out_specs=[blk(RET_QK), blk(RET_QK), blk(RET_V), blk(RET_V), vec],
        scratch_shapes=[pltpu.VMEM((RET_QK, RET_V), F32), pltpu.VMEM((TILE, TILE), F32)],
        compiler_params=_params(("arbitrary", "arbitrary")),
    )(lg, proj, proj, proj, proj, cos, sin, gain, o_raw, do_ret, states)


GLA_LEVELS = (32, 64, 128, 256)
N_TERMS = 1 + len(GLA_LEVELS)


def _gla_tables():
    p = jnp.arange(TILE)[:, None]
    r = jnp.arange(TILE)[None, :]
    masks = [(p // GLA_CHUNK == r // GLA_CHUNK) & (r <= p)]
    for blk in GLA_LEVELS:
        masks.append((p // blk == r // blk) & (p % blk >= blk // 2) & (r % blk < blk // 2))
    masks = jnp.stack(masks + [m.T for m in masks]).astype(F32)
    cum_fwd = jnp.concatenate([r <= p, masks[0] > 0], axis=0).astype(BF16)
    cum_bwd = jnp.concatenate([r >= p, masks[N_TERMS] > 0], axis=1).astype(BF16)
    return masks, cum_fwd, cum_bwd


def _split3(x):
    hi = x.astype(BF16)
    rest = x - hi.astype(F32)
    mid = rest.astype(BF16)
    lo = (rest - mid.astype(F32)).astype(BF16)
    return jnp.concatenate([hi, mid, lo], axis=1)


def _join3(y):
    w = y.shape[1] // 3
    return (y[:, 2 * w:] + y[:, w:2 * w]) + y[:, :w]


def _gla_prep(q_ref, k_ref, glr_ref, wgu_ref, b_ref, cum_ref, g_scr, ref_scr):
    z = _mm(glr_ref[...].astype(BF16), wgu_ref[...].astype(BF16)) + b_ref[...]
    la = (jnp.minimum(z, 0.0) - jnp.log(1.0 + jnp.exp(-jnp.abs(z)))) / GLA_TAU
    gb = _join3(_mm(cum_ref[...], _split3(la)))
    g, b = gb[:TILE], gb[TILE:]
    g_scr[...] = g
    factors = [(jnp.exp(b), jnp.exp(-b))]
    for lvl, blk in enumerate(GLA_LEVELS):
        for n in range(TILE // blk):
            ref_scr[lvl, n * blk:(n + 1) * blk, :] = jnp.broadcast_to(g_scr[pl.ds(n * blk + blk // 2 - 1, 1), :], (blk, GLA_K))
        x = g - ref_scr[lvl]
        factors.append((jnp.exp(jnp.minimum(x, 0.0)), jnp.exp(jnp.minimum(-x, 0.0))))
    g_last = g_scr[pl.ds(TILE - 1, 1), :]
    q = _cols(q_ref).astype(F32) * (GLA_K ** -0.5)
    k = _cols(k_ref).astype(F32)
    return z, q, k, factors, jnp.exp(g), jnp.exp(g_last), jnp.exp(g_last - g)


def _gla_scores(q, k, factors, m_ref):
    a = jnp.zeros((TILE, TILE), F32)
    for l, (fq, fk) in enumerate(factors):
        s = _mm_nt((q * fq).astype(BF16), (k * fk).astype(BF16))
        a = jnp.where(m_ref[l] > 0.0, s, a)
    return a


def _gla_fwd(proj, glr, wgu_pad, b_gate, gain, masks, cum_fwd):
    t_rows = glr.shape[0]
    nt = t_rows // TILE

    def body(q_ref, k_ref, v_ref, g_ref, glr_ref, wgu_ref, b_ref, gain_ref, m_ref, cum_ref, oraw_ref, ogla_ref, st_ref,
             s_acc, g_scr, ref_scr):
        @pl.when(pl.program_id(1) == 0)
        def _():
            s_acc[...] = jnp.zeros_like(s_acc)

        _, q, k, factors, e_g, e_last, e_end = _gla_prep(q_ref, k_ref, glr_ref, wgu_ref, b_ref, cum_ref, g_scr, ref_scr)
        v = _cols(v_ref)
        st = s_acc[...]
        st_ref[...] = st
        a = _gla_scores(q, k, factors, m_ref)
        o = _mm(a.astype(BF16), v) + _mm_nt((q * e_g).astype(BF16), st.astype(BF16))
        s_acc[...] = st * e_last + _mm(v.astype(F32).T.astype(BF16), (k * e_end).astype(BF16))
        oraw_ref[...] = o
        n = o * lax.rsqrt(_row_mean(o * o) + EPS) * gain_ref[...]
        g = _cols(g_ref).astype(F32)
        ogla_ref[...] = (n * g * _sigmoid(g)).astype(BF16)

    blk = lambda w: pl.BlockSpec((TILE, w), lambda h, t: (t, h))
    return _call(
        body, "gla_fwd", grid=(GLA_HEADS, nt),
        out_shape=[jax.ShapeDtypeStruct((t_rows, GLA_W), F32), jax.ShapeDtypeStruct((t_rows, GLA_W), BF16),
                   jax.ShapeDtypeStruct((GLA_HEADS, nt, GLA_V, GLA_K), F32)],
        in_specs=_proj_specs(("gq", "gk", "gv", "gg"), GLA_HEADS, lambda h, t: (h, t)) + [pl.BlockSpec((TILE, LANES), lambda h, t: (t, 0)),
                  pl.BlockSpec((LANES, GLA_K), lambda h, t: (0, h)), pl.BlockSpec((1, GLA_K), lambda h, t: (0, h)),
                  pl.BlockSpec((1, GLA_V), lambda h, t: (0, h)),
                  pl.BlockSpec((N_TERMS, TILE, TILE), lambda h, t: (0, 0, 0)), pl.BlockSpec((2 * TILE, TILE), lambda h, t: (0, 0))],
        out_specs=[blk(GLA_V), blk(GLA_V), pl.BlockSpec((None, None, GLA_V, GLA_K), lambda h, t: (h, t, 0, 0))],
        scratch_shapes=[pltpu.VMEM((GLA_V, GLA_K), F32), pltpu.VMEM((TILE, GLA_K), F32),
                        pltpu.VMEM((len(GLA_LEVELS), TILE, GLA_K), F32)],
        compiler_params=_params(("arbitrary", "arbitrary")),
    )(proj, proj, proj, proj, glr, wgu_pad, b_gate, gain, masks, cum_fwd)


def _gla_bwd(proj, glr, wgu_pad, b_gate, gain, o_raw, do_gla, states, masks, cum_fwd, cum_bwd):
    t_rows = glr.shape[0]
    nt = t_rows // TILE

    def body(q_ref, k_ref, v_ref, g_ref, glr_ref, wgu_ref, b_ref, gain_ref, m_ref, cum_ref, cumb_ref, oraw_ref, do_ref, st_ref,
             dq_ref, dk_ref, dv_ref, dg_ref, dglr_ref, dwgu_ref, dbg_ref, dgain_ref, d_acc, g_scr, ref_scr, dref_scr):
        @pl.when(pl.program_id(1) == 0)
        def _():
            d_acc[...] = jnp.zeros_like(d_acc)
            dwgu_ref[...] = jnp.zeros_like(dwgu_ref)
            dbg_ref[...] = jnp.zeros_like(dbg_ref)
            dgain_ref[...] = jnp.zeros_like(dgain_ref)

        z, q, k, factors, e_g, e_last, e_end = _gla_prep(q_ref, k_ref, glr_ref, wgu_ref, b_ref, cum_ref, g_scr, ref_scr)
        v = _cols(v_ref)
        o = oraw_ref[...]
        do = do_ref[...].astype(F32)
        g = _cols(g_ref).astype(F32)
        rinv = lax.rsqrt(_row_mean(o * o) + EPS)
        nh = o * rinv
        gain_t = gain_ref[...]
        sg = _sigmoid(g)
        dn = do * (g * sg)
        dg_ref[...] = (do * (nh * gain_t) * (sg * (1.0 + g * (1.0 - sg)))).astype(BF16)
        dgain_ref[...] += _col_sum(dn * nh)
        dnh = dn * gain_t
        dor = rinv * (dnh - nh * _row_mean(dnh * nh))
        dob = dor.astype(BF16)
        a_t = _gla_scores(q, k, factors, m_ref).T.astype(BF16)
        da = _mm_nt(dob, v)
        da_t = _mm_nt(v, dob)
        st_in = st_ref[...]
        d_out = d_acc[...]
        d_out_b = d_out.astype(BF16)
        qg, kg = q * e_g, k * e_end
        dqg = _mm(dob, st_in.astype(BF16))
        dkg = _mm(v, d_out_b)
        dv_ref[...] = (_mm(a_t, dob) + _mm_nt(kg.astype(BF16), d_out_b)).astype(BF16)
        d_acc[...] = d_out * e_last + _mm(dor.T.astype(BF16), qg.astype(BF16))
        dq = dqg * e_g
        dk = dkg * e_end
        dkg_kg = dkg * kg
        dg_cum = dqg * qg - dkg_kg
        db = None
        for l, (fq, fk) in enumerate(factors):
            qt, kt = q * fq, k * fk
            dqt = _mm(jnp.where(m_ref[l] > 0.0, da, 0.0).astype(BF16), kt.astype(BF16))
            dkt = _mm(jnp.where(m_ref[N_TERMS + l] > 0.0, da_t, 0.0).astype(BF16), qt.astype(BF16))
            dq = dq + dqt * fq
            dk = dk + dkt * fk
            diff = dqt * qt - dkt * kt
            if l == 0:
                db = diff
            else:
                dg_cum = dg_cum + diff
                dref_scr[l - 1] = diff
        dq_ref[...] = (dq * (GLA_K ** -0.5)).astype(BF16)
        dk_ref[...] = dk.astype(BF16)
        g_scr[...] = dg_cum
        g_scr[pl.ds(TILE - 1, 1), :] += e_last * _col_sum(d_out * st_in) + _col_sum(dkg_kg)
        for lvl, blk in enumerate(GLA_LEVELS):
            for n in range(TILE // blk):
                g_scr[pl.ds(n * blk + blk // 2 - 1, 1), :] -= _col_sum(dref_scr[lvl, n * blk:(n + 1) * blk, :])
        dla = _join3(_mm(cumb_ref[...], jnp.concatenate([_split3(g_scr[...]), _split3(db)], axis=0)))
        dz = dla * (1.0 / GLA_TAU) * _sigmoid(-z)
        dzb = dz.astype(BF16)
        dglr_ref[...] = _mm_nt(dzb, wgu_ref[...].astype(BF16))
        dwgu_ref[...] += _mm(glr_ref[...].T.astype(BF16), dzb)
        dbg_ref[...] += _col_sum(dz)

    blk = lambda w: pl.BlockSpec((TILE, w), lambda h, j: (nt - 1 - j, h))
    vec = lambda w: pl.BlockSpec((1, w), lambda h, j: (0, h))
    wspec = pl.BlockSpec((LANES, GLA_K), lambda h, j: (0, h))
    return _call(
        body, "gla_bwd", grid=(GLA_HEADS, nt),
        out_shape=[jax.ShapeDtypeStruct((t_rows, GLA_HEADS * GLA_K), BF16), jax.ShapeDtypeStruct((t_rows, GLA_HEADS * GLA_K), BF16),
                   jax.ShapeDtypeStruct((t_rows, GLA_W), BF16), jax.ShapeDtypeStruct((t_rows, GLA_W), BF16),
                   jax.ShapeDtypeStruct((GLA_HEADS, t_rows, LANES), F32), jax.ShapeDtypeStruct((LANES, GLA_HEADS * GLA_K), F32),
                   jax.ShapeDtypeStruct((1, GLA_HEADS * GLA_K), F32), jax.ShapeDtypeStruct((1, GLA_W), F32)],
        in_specs=_proj_specs(("gq", "gk", "gv", "gg"), GLA_HEADS, lambda h, j: (h, nt - 1 - j)) + [pl.BlockSpec((TILE, LANES), lambda h, j: (nt - 1 - j, 0)),
                  wspec, vec(GLA_K), vec(GLA_V),
                  pl.BlockSpec((2 * N_TERMS, TILE, TILE), lambda h, j: (0, 0, 0)), pl.BlockSpec((2 * TILE, TILE), lambda h, j: (0, 0)),
                  pl.BlockSpec((TILE, 2 * TILE), lambda h, j: (0, 0)), blk(GLA_V), blk(GLA_V),
                  pl.BlockSpec((None, None, GLA_V, GLA_K), lambda h, j: (h, nt - 1 - j, 0, 0))],
        out_specs=[blk(GLA_K), blk(GLA_K), blk(GLA_V), blk(GLA_V),
                   pl.BlockSpec((None, TILE, LANES), lambda h, j: (h, nt - 1 - j, 0)), wspec, vec(GLA_K), vec(GLA_V)],
        scratch_shapes=[pltpu.VMEM((GLA_V, GLA_K), F32), pltpu.VMEM((TILE, GLA_K), F32),
                        pltpu.VMEM((len(GLA_LEVELS), TILE, GLA_K), F32), pltpu.VMEM((len(GLA_LEVELS), TILE, GLA_K), F32)],
        compiler_params=_params(("arbitrary", "arbitrary")),
    )(proj, proj, proj, proj, glr, wgu_pad, b_gate, gain, masks, cum_fwd, cum_bwd, o_raw, do_gla, states)


def _merge_fwd_bwd(o_ret, o_gla, proj, x, target, g_final, w_br, w_bg, w_out):
    t_rows = x.shape[0] + TILE
    nt = t_rows // TILE

    def body(oret_ref, ogla_ref, mr_ref, mg_ref, h0_ref, tgt_ref, gf_ref, wbr_hbm, wbg_hbm, wout_hbm,
             dh1_ref, dmr_ref, dmg_ref, doret_ref, dogla_ref, loss_ref, dgf_ref, dwbr_hbm, dwbg_hbm, dwout_hbm,
             wbr, wbg, wout, abr, abg, aout, sem):
        i = pl.program_id(0)

        @pl.when(i == 0)
        def _():
            cps = [pltpu.make_async_copy(s, d, sem.at[n]) for n, (s, d) in enumerate(((wbr_hbm, wbr), (wbg_hbm, wbg), (wout_hbm, wout)))]
            for cp in cps:
                cp.start()
            abr[...] = jnp.zeros_like(abr)
            abg[...] = jnp.zeros_like(abg)
            aout[...] = jnp.zeros_like(aout)
            loss_ref[...] = jnp.zeros_like(loss_ref)
            dgf_ref[...] = jnp.zeros_like(dgf_ref)
            for cp in cps:
                cp.wait()
            dh1_ref[...] = jnp.zeros_like(dh1_ref)
            dmr_ref[...] = jnp.zeros_like(dmr_ref)
            dmg_ref[...] = jnp.zeros_like(dmg_ref)
            doret_ref[...] = jnp.zeros_like(doret_ref)
            dogla_ref[...] = jnp.zeros_like(dogla_ref)

        @pl.when(i > 0)
        def _():
            oret, ogla = oret_ref[...], ogla_ref[...]
            br, bg = _mm(oret, wbr[...]), _mm(ogla, wbg[...])
            sr, sg = _sigmoid(_cols(mr_ref).astype(F32)), _sigmoid(_cols(mg_ref).astype(F32))
            mb = (sr * br + sg * bg).astype(BF16)
            h1 = h0_ref[...] + _mm(mb, wout[...])
            r2 = lax.rsqrt(_row_mean(h1 * h1) + EPS)
            hn = h1 * r2
            gf = gf_ref[...]
            diff = hn * gf - tgt_ref[...]
            loss_ref[...] += 0.5 * jnp.sum(_row_mean(diff * diff))
            dy = diff * (1.0 / D_MODEL)
            dgf_ref[...] += _col_sum(dy * hn)
            dyg = dy * gf
            dh1 = r2 * (dyg - hn * _row_mean(dyg * hn))
            dh1_ref[...] = dh1
            dh1b = dh1.astype(BF16)
            dm = _mm_nt(dh1b, wout[...])
            aout[...] += _mm_tn(mb, dh1b)
            dbr = (dm * sr).astype(BF16)
            dbg = (dm * sg).astype(BF16)
            dmr_ref[...] = (dm * br * sr * (1.0 - sr)).astype(BF16)
            dmg_ref[...] = (dm * bg * sg * (1.0 - sg)).astype(BF16)
            doret_ref[...] = _mm_nt(dbr, wbr[...]).astype(BF16)
            dogla_ref[...] = _mm_nt(dbg, wbg[...]).astype(BF16)
            abr[...] += _mm_tn(oret, dbr)
            abg[...] += _mm_tn(ogla, dbg)

        @pl.when(i == nt - 1)
        def _():
            wbr[...] = abr[...].astype(BF16)
            wbg[...] = abg[...].astype(BF16)
            wout[...] = aout[...].astype(BF16)
            pltpu.sync_copy(wbr, dwbr_hbm)
            pltpu.sync_copy(wbg, dwbg_hbm)
            pltpu.sync_copy(wout, dwout_hbm)

    row = lambda w: pl.BlockSpec((TILE, w), lambda i: (i, 0))
    one = lambda w: pl.BlockSpec((1, w), lambda i: (0, 0))
    return _call(
        body, "merge_fwd_bwd", grid=(nt,),
        out_shape=[jax.ShapeDtypeStruct((t_rows, D_MODEL), F32), jax.ShapeDtypeStruct((t_rows, D_MODEL), BF16),
                   jax.ShapeDtypeStruct((t_rows, D_MODEL), BF16), jax.ShapeDtypeStruct((t_rows, RET_W), BF16),
                   jax.ShapeDtypeStruct((t_rows, GLA_W), BF16), jax.ShapeDtypeStruct((1, LANES), F32),
                   jax.ShapeDtypeStruct((1, D_MODEL), F32), jax.ShapeDtypeStruct((RET_W, D_MODEL), BF16),
                   jax.ShapeDtypeStruct((GLA_W, D_MODEL), BF16), jax.ShapeDtypeStruct((D_MODEL, D_MODEL), BF16)],
        in_specs=[row(RET_W), row(GLA_W)] + _proj_specs(("mr", "mg"), 1, lambda i: (0, i)) + [_x_spec(), _x_spec(), one(D_MODEL), ANY, ANY, ANY],
        out_specs=[row(D_MODEL), row(D_MODEL), row(D_MODEL), row(RET_W), row(GLA_W), one(LANES), one(D_MODEL), ANY, ANY, ANY],
        scratch_shapes=[pltpu.VMEM((RET_W, D_MODEL), BF16), pltpu.VMEM((GLA_W, D_MODEL), BF16), pltpu.VMEM((D_MODEL, D_MODEL), BF16),
                        pltpu.VMEM((RET_W, D_MODEL), F32), pltpu.VMEM((GLA_W, D_MODEL), F32), pltpu.VMEM((D_MODEL, D_MODEL), F32),
                        pltpu.SemaphoreType.DMA((3,))],
        compiler_params=_params(("arbitrary",)),
    )(o_ret, o_gla, proj, proj, x, target, g_final, w_br, w_bg, w_out)


def _sum_heads(parts):
    def body(p_ref, o_ref):
        o_ref[...] = (p_ref[0] + p_ref[1] + p_ref[2] + p_ref[3]).astype(BF16)

    t_rows = parts.shape[1]
    return _call(
        body, "sum_heads_dglr", grid=(t_rows // TILE,), out_shape=jax.ShapeDtypeStruct((t_rows, LANES), BF16),
        in_specs=[pl.BlockSpec((GLA_HEADS, TILE, LANES), lambda i: (0, i, 0))], out_specs=pl.BlockSpec((TILE, LANES), lambda i: (i, 0)),
        compiler_params=_params(("arbitrary",)),
    )(parts)


def _inproj_bwd_x(dseg, dglr, head, x, dh1, g_norm, slabs, w_glr, chip_partials):
    t_rows = x.shape[0] + TILE
    nt = t_rows // TILE
    ne = len(chip_partials)

    def body(*refs):
        d_refs = refs[:10]
        dglr_ref, head_ref, x_ref, dh1_ref, g_ref, slabs_hbm, wg_hbm = refs[10:17]
        part_refs = refs[17:17 + ne]
        dx_ref, dhead_ref, dgn_ref = refs[17 + ne:20 + ne]
        landed = refs[20 + ne:20 + 2 * ne]
        w_vm, wg_vm, edge_vm, sem = refs[20 + 2 * ne:24 + 2 * ne]
        exchange = _Exchange(part_refs, landed, refs[24 + 2 * ne:], among_chips=True)

        @pl.when(pl.program_id(0) == 0)
        def _():
            exchange.start()
            dgn_ref[...] = jnp.zeros_like(dgn_ref)
            _load_weight(slabs_hbm, wg_hbm, w_vm, wg_vm, edge_vm, sem)

        @pl.when(pl.program_id(0) == nt - 1)
        def _():
            exchange.finish()

        du = _mm_nt(dglr_ref[...], wg_vm[...])
        for s, d_ref in enumerate(d_refs):
            du = du + _mm_nt(d_ref[...], w_vm[:, SEG_OFF[s]:SEG_OFF[s] + SEG_W[s]])
        x = _tile_rows(head_ref, x_ref)
        r = lax.rsqrt(_row_mean(x * x) + EPS)
        hn = x * r
        dgn_ref[...] += _col_sum(du * hn)
        dug = du * g_ref[...]
        dh0 = dh1_ref[...] + r * (dug - hn * _row_mean(dug * hn))
        dx_ref[...] = dh0

        @pl.when(pl.program_id(0) == 0)
        def _():
            dhead_ref[...] = dh0

    row = lambda w: pl.BlockSpec((TILE, w), lambda i: (i, 0))
    one = pl.BlockSpec((1, D_MODEL), lambda i: (0, 0))
    return _call(
        body, "inproj_bwd_x", grid=(nt,),
        out_shape=[jax.ShapeDtypeStruct((t_rows - TILE, D_MODEL), F32), jax.ShapeDtypeStruct((TILE, D_MODEL), F32),
                   jax.ShapeDtypeStruct((1, D_MODEL), F32)] + [jax.ShapeDtypeStruct(a.shape, a.dtype) for a in chip_partials],
        in_specs=[row(w) for w in SEG_W] + [row(LANES), _head_spec(), _x_spec(), row(D_MODEL), one, ANY, ANY] + [ANY] * ne,
        out_specs=[_x_spec(), _head_spec(), one] + [ANY] * ne,
        scratch_shapes=W_SCRATCH() + _exchange_sems(ne, N_CHIP),
        compiler_params=_params(("arbitrary",)),
    )(*[dseg[n] for n in SEG_NAMES], dglr, head, x, dh1, g_norm, slabs, w_glr, *chip_partials)


W_TILE = 512


def _inproj_bwd_w(ut, dseg, dglr):
    nt = ut.shape[0]
    t_rows = nt * TILE
    kc = 3 if nt % 3 == 0 else 1
    tiles = [(s, c) for s in range(len(SEG_W)) for c in range(0, SEG_W[s], W_TILE)]
    bpt = W_TILE // LANES

    def body(ut_hbm, *refs):
        d_refs, dglr_hbm, out_hbm, oglr_ref = refs[:10], refs[10], refs[11], refs[12]
        ut_vm, dbuf, obuf, acc, gbuf, sem = refs[13:]

        def fetch(i):
            s, c = tiles[i]
            return pltpu.make_async_copy(d_refs[s].at[:, pl.ds(c, W_TILE)], dbuf.at[i % 2], sem.at[1 + i % 2])

        def contract(rhs_ref, width):
            acc[:, :width] = jnp.zeros((D_MODEL, width), F32)

            def step(k, carry):
                part = None
                for j in range(kc):
                    kk = k * kc + j
                    prod = _mm(ut_vm[kk], rhs_ref[pl.ds(pl.multiple_of(kk * TILE, TILE), TILE), :])
                    part = prod if part is None else part + prod
                acc[:, :width] += part
                return carry

            lax.fori_loop(0, nt // kc, step, 0)
            return acc[:, :width]

        load_ut = pltpu.make_async_copy(ut_hbm, ut_vm, sem.at[0])
        load_glr = pltpu.make_async_copy(dglr_hbm, gbuf, sem.at[5])
        load_ut.start()
        load_glr.start()
        fetch(0).start()
        load_ut.wait()
        stores = {}
        for i, (s, c) in enumerate(tiles):
            if i + 1 < len(tiles):
                fetch(i + 1).start()
            fetch(i).wait()
            if i >= 2:
                stores[i - 2].wait()
            total = contract(dbuf.at[i % 2], W_TILE)
            for j in range(bpt):
                obuf[i % 2, j] = total[:, j * LANES:(j + 1) * LANES].astype(BF16)
            blk0 = (SEG_OFF[s] + c) // LANES
            stores[i] = pltpu.make_async_copy(obuf.at[i % 2], out_hbm.at[pl.ds(blk0, bpt)], sem.at[3 + i % 2])
            stores[i].start()
        load_glr.wait()
        oglr_ref[...] = contract(gbuf, LANES)
        for i in range(max(0, len(tiles) - 2), len(tiles)):
            stores[i].wait()

    return _call(
        body, "inproj_bwd_w",
        out_shape=[jax.ShapeDtypeStruct((AL_COLS // LANES, D_MODEL, LANES), BF16), jax.ShapeDtypeStruct((D_MODEL, LANES), F32)],
        in_specs=[ANY] * 12, out_specs=[ANY, pl.BlockSpec(memory_space=pltpu.VMEM)],
        scratch_shapes=[pltpu.VMEM((nt, D_MODEL, TILE), BF16), pltpu.VMEM((2, t_rows, W_TILE), BF16),
                        pltpu.VMEM((2, bpt, D_MODEL, LANES), BF16), pltpu.VMEM((D_MODEL, W_TILE), F32),
                        pltpu.VMEM((t_rows, LANES), BF16), pltpu.SemaphoreType.DMA((6,))],
        compiler_params=_params(),
    )(ut, *[dseg[n] for n in SEG_NAMES], dglr)


def _position():
    x, y, c = lax.axis_index("x"), lax.axis_index("y"), lax.axis_index("c")
    return x, y, c


def _index(px, py, pc):
    return 4 * px + 2 * py + pc


def _all_gather(arrs, name):
    n = len(arrs)

    def body(*refs):
        ins, outs = refs[:n], refs[n:2 * n]
        send_sems, recv_sems, local_sems = refs[2 * n:]
        x, y, c = _position()
        me, sibling = (x, y, c), (x, y, 1 - c)
        chips = [(1 - x, y), (x, 1 - y), (1 - x, 1 - y)]

        def copy(a, k, block, to, src=None):
            dst = outs[a].at[_index(*block)]
            return pltpu.make_async_remote_copy(src_ref=dst if src is None else src, dst_ref=dst,
                                                send_sem=send_sems.at[7 * a + k], recv_sem=recv_sems.at[7 * a + k],
                                                device_id=to, device_id_type=MESH)

        mine = [pltpu.make_async_copy(ins[a], outs[a].at[_index(*me)], local_sems.at[a]) for a in range(n)]
        for cp in mine:
            cp.start()
        first = []
        for a in range(n):
            first.append(copy(a, 0, me, sibling, src=ins[a]))
            first += [copy(a, 1 + j, me, (*chip, c), src=ins[a]) for j, chip in enumerate(chips)]
        for cp in first:
            cp.start()
        passed = []
        for j, chip in enumerate(chips):
            for a in range(n):
                copy(a, 1 + j, (*chip, c), me).wait_recv()
                cp = copy(a, 4 + j, (*chip, c), sibling)
                cp.start()
                passed.append(cp)
        for a in range(n):
            copy(a, 0, sibling, me).wait_recv()
            for j, chip in enumerate(chips):
                copy(a, 4 + j, (*chip, 1 - c), me).wait_recv()
        for cp in first + passed:
            cp.wait_send()
        for cp in mine:
            cp.wait()

    return _call(
        body, name,
        out_shape=[jax.ShapeDtypeStruct((N_DEV, *a.shape), a.dtype) for a in arrs],
        in_specs=[ANY] * n, out_specs=[ANY] * n,
        scratch_shapes=[pltpu.SemaphoreType.DMA((7 * n,)), pltpu.SemaphoreType.DMA((7 * n,)), pltpu.SemaphoreType.DMA((n,))],
    )(*arrs)


N_CHIP = N_DEV // 2


def _slab_block0(owner):
    step = SLAB_BLK0[1]
    assert all(SLAB_BLK0[d] == step * d - (d == N_DEV - 1) for d in range(N_DEV))
    return step * owner - jnp.where(owner == N_DEV - 1, 1, 0)


def _exchange_sibling(dw_blocks, row_sends):
    n = 1 + len(row_sends)

    def body(*refs):
        dw_ref, row_refs, outs, (send_sems, recv_sems) = refs[0], refs[1:n], refs[n:2 * n], refs[2 * n:]
        x, y, c = _position()
        copies = []
        for q in range(N_CHIP):
            owner = 2 * q + (1 - c)
            srcs = [dw_ref.at[pl.ds(_slab_block0(owner), SLAB_BLOCKS)]] + [r.at[owner] for r in row_refs]
            for k, src in enumerate(srcs):
                copies.append(pltpu.make_async_remote_copy(src_ref=src, dst_ref=outs[k].at[q], send_sem=send_sems.at[n * q + k],
                                                           recv_sem=recv_sems.at[n * q + k], device_id=(x, y, 1 - c), device_id_type=MESH))
        for cp in copies:
            cp.start()
        for cp in copies:
            cp.wait()

    return _call(
        body, "exchange_sibling",
        out_shape=[jax.ShapeDtypeStruct((N_CHIP, SLAB_BLOCKS, D_MODEL, LANES), BF16)]
                  + [jax.ShapeDtypeStruct((N_CHIP, *r.shape[1:]), BF16) for r in row_sends],
        in_specs=[ANY] * n, out_specs=[ANY] * n,
        scratch_shapes=[pltpu.SemaphoreType.DMA((n * N_CHIP,)), pltpu.SemaphoreType.DMA((n * N_CHIP,))],
    )(dw_blocks, *row_sends)


def _add_bf16(c_ref, a_ref, b_ref, o_ref):
    o_ref[...] = (a_ref[...].astype(F32) + b_ref[...].astype(F32)).astype(BF16)


def _chip_partial_slab(dw_blocks, sib, core):
    blk = pl.BlockSpec((None, 1, D_MODEL, LANES), lambda q, j, c_ref: (q, j, 0, 0))
    return _call(
        functools.partial(_add_bf16), "chip_partial_w_in", out_shape=jax.ShapeDtypeStruct(sib.shape, BF16),
        grid_spec=pltpu.PrefetchScalarGridSpec(
            num_scalar_prefetch=1, grid=(N_CHIP, SLAB_BLOCKS),
            in_specs=[pl.BlockSpec((1, D_MODEL, LANES), lambda q, j, c_ref: (_slab_block0(2 * q + c_ref[0]) + j, 0, 0)), blk],
            out_specs=blk),
        compiler_params=_params(("arbitrary", "arbitrary")),
    )(core, dw_blocks, sib)


def _chip_partial_rows(send, sib, core, name):
    rows, cols = send.shape[1:]
    blk = pl.BlockSpec((None, rows, cols), lambda q, c_ref: (q, 0, 0))
    return _call(
        functools.partial(_add_bf16), name, out_shape=jax.ShapeDtypeStruct(sib.shape, BF16),
        grid_spec=pltpu.PrefetchScalarGridSpec(
            num_scalar_prefetch=1, grid=(N_CHIP,),
            in_specs=[pl.BlockSpec((None, rows, cols), lambda q, c_ref: (2 * q + c_ref[0], 0, 0)), blk], out_specs=blk),
        compiler_params=_params(("arbitrary",)),
    )(core, send, sib)


def _exchange_sems(n_arrays, n_peers):
    return [pltpu.SemaphoreType.DMA((n_arrays * n_peers,)), pltpu.SemaphoreType.DMA((n_arrays * n_peers,)),
            pltpu.SemaphoreType.DMA((n_arrays,))]


class _Exchange:
    def __init__(self, srcs, dsts, sems, among_chips):
        self.arrs = list(zip(srcs, dsts))
        self.n = len(self.arrs)
        self.send_sems, self.recv_sems, self.local_sems = sems
        self.among_chips = among_chips
        x, y, c = _position()
        self.c = c
        self.me = 2 * x + y if among_chips else _index(x, y, c)
        self.n_peers = N_CHIP if among_chips else N_DEV

    def _device(self, p):
        return (p // 2, p % 2, self.c) if self.among_chips else (p // 4, (p // 2) % 2, p % 2)

    def _src(self, k, p):
        src = self.arrs[k][0]
        return src.at[p] if self.among_chips else src

    def _mine(self):
        return [pltpu.make_async_copy(self._src(k, self.me), self.arrs[k][1].at[self.me], self.local_sems.at[k]) for k in range(self.n)]

    def _copy(self, p, k, landing):
        return pltpu.make_async_remote_copy(
            src_ref=self._src(k, p), dst_ref=self.arrs[k][1].at[landing], send_sem=self.send_sems.at[self.n * p + k],
            recv_sem=self.recv_sems.at[self.n * landing + k], device_id=self._device(p), device_id_type=MESH)

    def _others(self, fn):
        for p in range(self.n_peers):
            @pl.when(p != self.me)
            def _():
                for k in range(self.n):
                    fn(p, k)

    def start(self):
        for cp in self._mine():
            cp.start()
        self._others(lambda p, k: self._copy(p, k, self.me).start())

    def finish(self):
        self._others(lambda p, k: self._copy(p, k, p).wait_recv())
        self._others(lambda p, k: self._copy(p, k, self.me).wait_send())
        for cp in self._mine():
            cp.wait()


def _adamw(g, w, m, v):
    m_new = ADAM_B1 * m + (1.0 - ADAM_B1) * g
    v_new = ADAM_B2 * v + (1.0 - ADAM_B2) * (g * g)
    m_hat = m_new / (1.0 - ADAM_B1 ** ADAM_STEP)
    v_hat = v_new / (1.0 - ADAM_B2 ** ADAM_STEP)
    delta = -ADAM_LR * (m_hat / (jnp.sqrt(v_hat) + ADAM_EPS) + ADAM_WD * w)
    return delta, m_new, v_new


def _sum_partials(p_ref):
    g = p_ref[0].astype(F32)
    for d in range(1, p_ref.shape[0]):
        g = g + p_ref[d].astype(F32)
    return g


def _reduce_adam(parts, w, m, v, name, block_rows, row_off=0):
    rows, cols = w.shape
    off = row_off // block_rows

    def body(p_ref, w_ref, m_ref, v_ref, g_ref, d_ref, mo_ref, vo_ref):
        g = _sum_partials(p_ref)
        g_ref[...] = g
        d_ref[...], mo_ref[...], vo_ref[...] = _adamw(g, w_ref[...], m_ref[...], v_ref[...])

    blk = pl.BlockSpec((block_rows, cols), lambda i: (i, 0))
    return _call(
        body, name, grid=(rows // block_rows,),
        out_shape=[jax.ShapeDtypeStruct((rows, cols), F32)] * 4,
        in_specs=[pl.BlockSpec((parts.shape[0], block_rows, cols), lambda i: (0, i + off, 0)), blk, blk, blk],
        out_specs=[blk] * 4,
        compiler_params=_params(("arbitrary",)),
    )(parts, w, m, v)


def _reduce_adam_slab(parts, glr, w, m, v, me):
    rows, cols = w.shape
    shift = jnp.asarray(SLAB_SHIFT, jnp.int32)[me]
    glr_at = jnp.where(me == GLR_DEV, GLR_LOCAL, cols).astype(jnp.int32)

    def body(s_ref, p_ref, glr_ref, w_ref, m_ref, v_ref, g_ref, d_ref, mo_ref, vo_ref):
        shift, glr_at = s_ref[0], s_ref[1]
        slab = jnp.concatenate([_sum_partials(p_ref.at[:, j]) for j in range(SLAB_BLOCKS)], axis=1)
        before = pltpu.roll(slab, SLAB_W - shift, 1)
        after = pltpu.roll(slab, lax.rem(SLAB_W - shift + GLA_RANK, SLAB_W), 1)
        wide = jnp.concatenate([glr_ref[...], jnp.zeros((LANES, SLAB_W - LANES), F32)], axis=1)
        placed = pltpu.roll(wide, lax.rem(glr_at, SLAB_W), 1)
        lane = lax.broadcasted_iota(jnp.int32, (LANES, SLAB_W), 1)
        g = jnp.where(lane < glr_at, before, jnp.where(lane < glr_at + GLA_RANK, placed, after))[:, :cols]
        g_ref[...] = g
        d_ref[...], mo_ref[...], vo_ref[...] = _adamw(g, w_ref[...], m_ref[...], v_ref[...])

    blk = pl.BlockSpec((LANES, cols), lambda i, s: (i, 0))
    return _call(
        body, "adam_w_in", out_shape=[jax.ShapeDtypeStruct((rows, cols), F32)] * 4,
        grid_spec=pltpu.PrefetchScalarGridSpec(
            num_scalar_prefetch=1, grid=(rows // LANES,),
            in_specs=[pl.BlockSpec((parts.shape[0], SLAB_BLOCKS, LANES, LANES), lambda i, s: (0, 0, i, 0)),
                      pl.BlockSpec((LANES, LANES), lambda i, s: (i, 0)), blk, blk, blk],
            out_specs=[blk] * 4),
        compiler_params=_params(("arbitrary",)),
    )(jnp.stack([shift, glr_at]), parts, glr, w, m, v)


def _reduce_small(parts):
    def body(p_ref, o_ref):
        o_ref[...] = _sum_partials(p_ref)

    return _call(body, "reduce_small", out_shape=jax.ShapeDtypeStruct(parts.shape[1:], F32))(parts)


def _adam_small(g, w, m, v):
    def body(g_ref, w_ref, m_ref, v_ref, d_ref, mo_ref, vo_ref):
        d_ref[...], mo_ref[...], vo_ref[...] = _adamw(g_ref[...], w_ref[...], m_ref[...], v_ref[...])

    return _call(body, "adam_small", out_shape=[jax.ShapeDtypeStruct(g.shape, F32)] * 3)(g, w, m, v)


def _pack_rows(arrs):
    rows = []
    for a in arrs:
        flat = a.reshape(-1).astype(F32)
        pad = (-flat.shape[0]) % LANES
        rows.append(jnp.pad(flat, (0, pad)).reshape(-1, LANES))
    packed = jnp.concatenate(rows, axis=0)
    return jnp.pad(packed, ((0, (-packed.shape[0]) % 8), (0, 0)))


def _unpack_rows(packed, shapes):
    out, r = [], 0
    for shp in shapes:
        size = 1
        for s in shp:
            size *= s
        nrows = -(-size // LANES)
        out.append(packed[r:r + nrows].reshape(-1)[:size].reshape(shp))
        r += nrows
    return out


def _shard_to_slab(shard, d):
    glr = jnp.zeros((D_MODEL, GLA_RANK), shard.dtype)
    if d == GLR_DEV:
        glr = shard[:, GLR_LOCAL:GLR_LOCAL + GLA_RANK]
        shard = jnp.concatenate([shard[:, :GLR_LOCAL], shard[:, GLR_LOCAL + GLA_RANK:]], axis=1)
    return jnp.pad(shard, ((0, 0), (SLAB_SHIFT[d], SLAB_W - SLAB_SHIFT[d] - shard.shape[1]))), glr


def kernel(x, meta_tokens, norm_gain, w_in, w_gate_up, b_gate, ret_norm_gain, gla_norm_gain, w_branch_ret, w_branch_gla, w_out, final_norm_gain, loss_target, m_meta_tokens, m_norm_gain, m_w_in, m_w_gate_up, m_b_gate, m_ret_norm_gain, m_gla_norm_gain, m_w_branch_ret, m_w_branch_gla, m_w_out, m_final_norm_gain, v_meta_tokens, v_norm_gain, v_w_in, v_w_gate_up, v_b_gate, v_ret_norm_gain, v_gla_norm_gain, v_w_branch_ret, v_w_branch_gla, v_w_out, v_final_norm_gain):
    xi, yi, ci = _position()
    me = _index(xi, yi, ci)
    seq = x.shape[1]
    t_rows = seq + TILE
    in_shard = w_in.shape[2]
    gu_shard = w_gate_up.shape[2]
    meta_shard = meta_tokens.shape[1]
    ret_rows, gla_rows, out_rows = w_branch_ret.shape[1], w_branch_gla.shape[1], w_out.shape[1]

    assert in_shard == IN_SHARD
    slab_local, glr_local = lax.switch(me, [functools.partial(_shard_to_slab, d=d) for d in range(N_DEV)], w_in[0])
    small_local = jnp.concatenate([meta_tokens, jnp.pad(w_gate_up[0], ((0, 0), (0, LANES - gu_shard))),
                                   glr_local.reshape(-1, LANES)], axis=0)
    (g_small,) = _all_gather([small_local], "all_gather_small_weights")
    n_small = N_META + GLA_RANK
    w_glr = jnp.pad(g_small[GLR_DEV, n_small:].reshape(D_MODEL, GLA_RANK), ((0, 0), (0, LANES - GLA_RANK))).astype(BF16)
    meta_full = jnp.transpose(g_small[:, :N_META, :], (1, 0, 2)).reshape(N_META, D_MODEL)
    wgu_full = jnp.transpose(g_small[:, N_META:n_small, :gu_shard], (1, 0, 2)).reshape(GLA_RANK, GLA_HEADS * GLA_K)
    wgu_pad = jnp.pad(wgu_full, ((0, LANES - GLA_RANK), (0, 0)))

    pos = jnp.arange(t_rows, dtype=F32) - float(PAD_ROWS)
    half = RET_QK // 2
    inv = ROPE_BASE ** (-jnp.arange(half, dtype=F32) / half)
    ang = pos[:, None] * inv[None, :]
    cos, sin = jnp.cos(ang), jnp.sin(ang)
    lg = jnp.log1p(-(2.0 ** (-5.0 - jnp.arange(RET_HEADS, dtype=F32))))

    head = jnp.concatenate([jnp.zeros((PAD_ROWS, D_MODEL), F32), meta_full], axis=0)
    u, ut, glr = _prenorm(head, x[0], norm_gain, w_glr)
    proj, slabs = _inproj_fwd(u, slab_local.astype(BF16))
    o_ret_raw, o_ret, ret_states, (g_br, g_bg, g_o) = _ret_fwd(
        proj, cos, sin, ret_norm_gain, lg, [w_branch_ret[0].astype(BF16), w_branch_gla[0].astype(BF16), w_out[0].astype(BF16)])
    w_br, w_bg, w_o = g_br.reshape(RET_W, D_MODEL), g_bg.reshape(GLA_W, D_MODEL), g_o.reshape(D_MODEL, D_MODEL)
    masks, cum_fwd, cum_bwd = _gla_tables()
    o_gla_raw, o_gla, gla_states = _gla_fwd(proj, glr, wgu_pad, b_gate, gla_norm_gain, masks, cum_fwd)
    (dh1, d_mr, d_mg, do_ret, do_gla, loss_part, d_gfinal, dw_br, dw_bg, dw_o) = _merge_fwd_bwd(
        o_ret, o_gla, proj, x[0], loss_target[0], final_norm_gain.reshape(1, D_MODEL), w_br, w_bg, w_o)

    d_rq, d_rk, d_rv, d_rg, d_gret = _ret_bwd(proj, cos, sin, ret_norm_gain, lg, o_ret_raw, do_ret, ret_states)
    d_gq, d_gk, d_gv, d_gg, dglr_parts, d_wgu, d_bgate, d_ggla = _gla_bwd(
        proj, glr, wgu_pad, b_gate, gla_norm_gain, o_gla_raw, do_gla, gla_states, masks, cum_fwd, cum_bwd)
    dseg = dict(rq=d_rq, rk=d_rk, rv=d_rv, rg=d_rg, gq=d_gq, gk=d_gk, gv=d_gv, gg=d_gg, mr=d_mr, mg=d_mg)
    dglr = _sum_heads(dglr_parts)
    dw_blocks, dw_glr = _inproj_bwd_w(ut, dseg, dglr)

    row_sends = [dw_br.reshape(N_DEV, ret_rows, D_MODEL), dw_bg.reshape(N_DEV, gla_rows, D_MODEL),
                 dw_o.reshape(N_DEV, out_rows, D_MODEL)]
    sib_in, *sib_rows = _exchange_sibling(dw_blocks, row_sends)
    core = ci.astype(jnp.int32).reshape(1)
    chip_partials = [_chip_partial_slab(dw_blocks, sib_in, core)] + [
        _chip_partial_rows(send, sib, core, "chip_partial_" + name)
        for send, sib, name in zip(row_sends, sib_rows, ("w_branch_ret", "w_branch_gla", "w_out"))]
    grad_x, d_head, d_gnorm, p_in, p_br, p_bg, p_o = _inproj_bwd_x(
        dseg, dglr, head, x[0], dh1, norm_gain, slabs, w_glr, chip_partials)
    small_shapes = [(N_META, D_MODEL), (1, D_MODEL), (GLA_RANK, GLA_HEADS * GLA_K), (1, GLA_HEADS * GLA_K),
                    (1, RET_W), (1, GLA_W), (1, D_MODEL), (1, LANES), (D_MODEL, GLA_RANK)]
    small_part = _pack_rows([d_head[PAD_ROWS:], d_gnorm, d_wgu[:GLA_RANK], d_bgate, d_gret, d_ggla, d_gfinal, loss_part,
                             dw_glr[:, :GLA_RANK]])
    (p_small,) = _all_gather([small_part], "all_gather_small_partials")

    (g_meta_f, g_gnorm, g_wgu_f, g_bgate, g_gret, g_ggla, g_gfinal, loss_all,
     g_wglr) = _unpack_rows(_reduce_small(p_small), small_shapes)
    g_w_in, d_w_in, nm_w_in, nv_w_in = _reduce_adam_slab(
        p_in, jnp.pad(g_wglr, ((0, 0), (0, LANES - GLA_RANK))), w_in[0], m_w_in[0], v_w_in[0], me)
    rb = gla_rows
    g_w_br, d_w_br, nm_w_br, nv_w_br = _reduce_adam(p_br, w_branch_ret[0], m_w_branch_ret[0], v_w_branch_ret[0], "adam_w_branch_ret", rb)
    g_w_bg, d_w_bg, nm_w_bg, nv_w_bg = _reduce_adam(p_bg, w_branch_gla[0], m_w_branch_gla[0], v_w_branch_gla[0], "adam_w_branch_gla", rb)
    g_w_o, d_w_o, nm_w_o, nv_w_o = _reduce_adam(p_o, w_out[0], m_w_out[0], v_w_out[0], "adam_w_out", rb)
    g_meta = lax.dynamic_slice_in_dim(g_meta_f, me * meta_shard, meta_shard, axis=1)
    g_wgu = lax.dynamic_slice_in_dim(g_wgu_f, me * gu_shard, gu_shard, axis=1)
    s_g = [g_meta, g_gnorm, g_wgu, g_bgate, g_gret, g_ggla, g_gfinal]
    s_w = [meta_tokens, norm_gain, w_gate_up[0], b_gate, ret_norm_gain, gla_norm_gain, final_norm_gain]
    s_m = [m_meta_tokens, m_norm_gain, m_w_gate_up[0], m_b_gate, m_ret_norm_gain, m_gla_norm_gain, m_final_norm_gain]
    s_v = [v_meta_tokens, v_norm_gain, v_w_gate_up[0], v_b_gate, v_ret_norm_gain, v_gla_norm_gain, v_final_norm_gain]
    shapes = [a.shape for a in s_g]
    s_d, s_nm, s_nv = [_unpack_rows(p, shapes) for p in _adam_small(*[_pack_rows(l) for l in (s_g, s_w, s_m, s_v)])]

    loss = loss_all[0, 0]
    grad_x = grad_x[None]

    def order(meta, gnorm, win, wgu, bgate, gret, ggla, wbr, wbg, wo, gfin):
        return (meta, gnorm, win[None], wgu[None], bgate, gret, ggla, wbr[None], wbg[None], wo[None], gfin.reshape(final_norm_gain.shape))

    def small(l):
        return dict(meta=l[0], gnorm=l[1], wgu=l[2], bgate=l[3], gret=l[4], ggla=l[5], gfin=l[6])

    grads = order(win=g_w_in, wbr=g_w_br, wbg=g_w_bg, wo=g_w_o, **small(s_g))
    deltas = order(win=d_w_in, wbr=d_w_br, wbg=d_w_bg, wo=d_w_o, **small(s_d))
    new_m = order(win=nm_w_in, wbr=nm_w_br, wbg=nm_w_bg, wo=nm_w_o, **small(s_nm))
    new_v = order(win=nv_w_in, wbr=nv_w_br, wbg=nv_w_bg, wo=nv_w_o, **small(s_nv))
    return (loss, grad_x, *grads, *deltas, *new_m, *new_v)
```

```python
import functools

import jax
import jax.numpy as jnp
from jax import lax
from jax.experimental import pallas as pl
from jax.experimental.pallas import tpu as pltpu

F32 = jnp.float32
BF16 = jnp.bfloat16

D_MODEL = 1024
N_META = 16
TILE = 256
PAD_ROWS = TILE - N_META
RET_HEADS = 4
RET_QK = 256
RET_V = 512
RET_W = RET_HEADS * RET_V
GLA_HEADS = 4
GLA_K = 128
GLA_V = 256
GLA_W = GLA_HEADS * GLA_V
GLA_RANK = 16
GLA_TAU = 16.0
GLA_CHUNK = 16
ROPE_BASE = 10000.0
EPS = 1e-6
LANES = 128
N_DEV = 8
SEG_NAMES = ("rq", "rk", "rv", "rg", "gq", "gk", "gv", "gg", "mr", "mg")
SEG_W = (1024, 1024, 2048, 2048, 512, 512, 1024, 1024, 1024, 1024)
SEG_OFF = tuple(sum(SEG_W[:i]) for i in range(len(SEG_W)))
AL_COLS = sum(SEG_W)
IN_COLS = AL_COLS + GLA_RANK
GLR_OFF = sum(SEG_W[:8])
IN_SHARD = IN_COLS // N_DEV


def _aligned_col(c):
    assert c <= GLR_OFF or c >= GLR_OFF + GLA_RANK
    return c if c <= GLR_OFF else c - GLA_RANK


SLAB_BOUND = tuple(_aligned_col(IN_SHARD * d) for d in range(N_DEV + 1))
SLAB_BLK0 = tuple(b // LANES for b in SLAB_BOUND[:-1])
SLAB_SHIFT = tuple(b % LANES for b in SLAB_BOUND[:-1])
SLAB_BLOCKS = max(-(-SLAB_BOUND[d + 1] // LANES) - SLAB_BLK0[d] for d in range(N_DEV))
SLAB_W = SLAB_BLOCKS * LANES
GLR_DEV = GLR_OFF // IN_SHARD
GLR_LOCAL = GLR_OFF - GLR_DEV * IN_SHARD
assert all(SLAB_BLK0[d] + SLAB_BLOCKS <= AL_COLS // LANES for d in range(N_DEV))
VMEM_LIMIT = 58 * 1024 * 1024
ADAM_LR, ADAM_B1, ADAM_B2, ADAM_EPS, ADAM_WD, ADAM_STEP = 0.001, 0.9, 0.999, 1e-08, 0.01, 10
ANY = pl.BlockSpec(memory_space=pl.ANY)
MESH = pl.DeviceIdType.MESH


def _call(body, name, **kw):
    return pl.pallas_call(body, name=name, **kw)


def _params(sem=None):
    return pltpu.CompilerParams(dimension_semantics=sem, vmem_limit_bytes=VMEM_LIMIT)


def _mm(a, b):
    return jnp.dot(a, b, preferred_element_type=F32)


def _mm_nt(a, b):
    return lax.dot_general(a, b, (((1,), (1,)), ((), ())), preferred_element_type=F32)


def _mm_tn(a, b):
    return lax.dot_general(a, b, (((0,), (0,)), ((), ())), preferred_element_type=F32)


def _sigmoid(x):
    return 1.0 / (1.0 + jnp.exp(-x))


def _rope(t, cos, sin):
    half = t.shape[-1] // 2
    t1, t2 = t[:, :half], t[:, half:]
    return jnp.concatenate([t1 * cos - t2 * sin, t2 * cos + t1 * sin], axis=-1)


def _rope_bwd(g, cos, sin):
    half = g.shape[-1] // 2
    g1, g2 = g[:, :half], g[:, half:]
    return jnp.concatenate([g1 * cos + g2 * sin, g2 * cos - g1 * sin], axis=-1)


def _row_mean(x):
    return jnp.mean(x, axis=-1, keepdims=True)


def _col_sum(x):
    return jnp.sum(x, axis=0, keepdims=True)


def _tile_rows(head_ref, x_ref):
    return jnp.where(pl.program_id(0) == 0, head_ref[...], x_ref[...])


def _head_spec():
    return pl.BlockSpec((TILE, D_MODEL), lambda i: (0, 0))


def _x_spec():
    return pl.BlockSpec((TILE, D_MODEL), lambda i: (jnp.maximum(i - 1, 0), 0))


def _slab_plan():
    interior, shared = [], []
    for d in range(N_DEV):
        lo, hi = -(-SLAB_BOUND[d] // LANES), SLAB_BOUND[d + 1] // LANES
        interior.append((d, LANES * (lo - SLAB_BLK0[d]), LANES * lo, LANES * (hi - lo)))
        if d + 1 < N_DEV and SLAB_BOUND[d + 1] % LANES:
            shared.append((hi, d, hi - SLAB_BLK0[d]))
    return interior, shared


W_SCRATCH = lambda: [pltpu.VMEM((D_MODEL, AL_COLS), BF16), pltpu.VMEM((D_MODEL, LANES), BF16),
                     pltpu.VMEM((2 * (N_DEV - 1), D_MODEL, LANES), BF16), pltpu.SemaphoreType.DMA((3 * N_DEV,))]


def _load_weight(slabs_hbm, wg_hbm, w_vm, wg_vm, edge_vm, sem):
    interior, shared = _slab_plan()
    copies = [pltpu.make_async_copy(wg_hbm, wg_vm, sem.at[0])]
    for d, src, dst, width in interior:
        copies.append(pltpu.make_async_copy(slabs_hbm.at[d, :, pl.ds(src, width)], w_vm.at[:, pl.ds(dst, width)], sem.at[1 + d]))
    for n, (_, d, blk) in enumerate(shared):
        copies.append(pltpu.make_async_copy(slabs_hbm.at[d, :, pl.ds(LANES * blk, LANES)], edge_vm.at[2 * n], sem.at[1 + N_DEV + 2 * n]))
        copies.append(pltpu.make_async_copy(slabs_hbm.at[d + 1, :, pl.ds(0, LANES)], edge_vm.at[2 * n + 1], sem.at[2 + N_DEV + 2 * n]))
    for cp in copies:
        cp.start()
    for cp in copies:
        cp.wait()
    for n, (blk, _, _) in enumerate(shared):
        w_vm[:, LANES * blk:LANES * (blk + 1)] = edge_vm[2 * n] + edge_vm[2 * n + 1]


def _proj_specs(names, n_units, where):
    specs = []
    for name in names:
        s = SEG_NAMES.index(name)
        nblk = SEG_W[s] // n_units // LANES
        base = SEG_OFF[s] // LANES
        assert base % nblk == 0
        specs.append(pl.BlockSpec((nblk, TILE, LANES), lambda *g, base=base, nblk=nblk: (base // nblk + where(*g)[0], where(*g)[1], 0)))
    return specs


def _cols(ref):
    return ref[0] if ref.shape[0] == 1 else jnp.concatenate([ref[j] for j in range(ref.shape[0])], axis=1)


def _prenorm(head, x, g_norm, w_glr):
    t_rows = x.shape[0] + TILE
    nt = t_rows // TILE

    def body(head_ref, x_ref, g_ref, wg_ref, u_ref, ut_ref, glr_ref):
        x = _tile_rows(head_ref, x_ref)
        r = lax.rsqrt(_row_mean(x * x) + EPS)
        u32 = (x * r * g_ref[...]).astype(BF16).astype(F32)
        u = u32.astype(BF16)
        u_ref[...] = u
        ut_ref[...] = u32.T.astype(BF16)
        glr_ref[...] = _mm(u, wg_ref[...])

    row = lambda w: pl.BlockSpec((TILE, w), lambda i: (i, 0))
    return _call(
        body, "prenorm", grid=(nt,),
        out_shape=[jax.ShapeDtypeStruct((t_rows, D_MODEL), BF16), jax.ShapeDtypeStruct((nt, D_MODEL, TILE), BF16),
                   jax.ShapeDtypeStruct((t_rows, LANES), F32)],
        in_specs=[_head_spec(), _x_spec(), pl.BlockSpec((1, D_MODEL), lambda i: (0, 0)), pl.BlockSpec((D_MODEL, LANES), lambda i: (0, 0))],
        out_specs=[row(D_MODEL), pl.BlockSpec((None, D_MODEL, TILE), lambda i: (i, 0, 0)), row(LANES)],
        compiler_params=_params(("arbitrary",)),
    )(head, x, g_norm, w_glr)


SLAB_INNER = 9


def _edge_blocks():
    inner = {SLAB_BLK0[d] + j for d in range(N_DEV) for j in range(1, 1 + SLAB_INNER)}
    edges = []
    for blk in range(AL_COLS // LANES):
        if blk not in inner:
            srcs = [(d, blk - SLAB_BLK0[d]) for d in range(N_DEV)
                    if SLAB_BLK0[d] <= blk < SLAB_BLK0[d] + SLAB_BLOCKS and SLAB_BOUND[d] < LANES * (blk + 1) and LANES * blk < SLAB_BOUND[d + 1]]
            edges.append((blk, srcs))
    return edges


def _inproj_fwd(u, slab_local):
    t_rows = u.shape[0]
    nt = t_rows // TILE
    rc = (3 if nt % 3 == 0 else 1) * TILE
    n_chunks = t_rows // rc
    edges = _edge_blocks()
    ne = len(edges)
    runs = []
    for k, (blk, _) in enumerate(edges):
        if runs and edges[runs[-1][0] + runs[-1][1] - 1][0] + 1 == blk:
            runs[-1] = (runs[-1][0], runs[-1][1] + 1)
        else:
            runs.append((k, 1))
    n_stage = sum(len(srcs) for _, srcs in edges)

    def body(u_hbm, slab_hbm, proj_hbm, slabs_hbm, u_vm, wbuf, obuf, ebuf, stage, ebuf_out, sem_u, sem_w, sem_o, sem_s, sem_eo,
             send_sems, recv_sems, sem_l):
        x, y, c = _position()
        me, sibling = (x, y, c), (x, y, 1 - c)
        chips = [(1 - x, y), (x, 1 - y), (1 - x, 1 - y)]

        def slab_copy(k, block, to, src=None):
            dst = slabs_hbm.at[_index(*block)]
            return pltpu.make_async_remote_copy(src_ref=dst if src is None else src, dst_ref=dst, send_sem=send_sems.at[k],
                                                recv_sem=recv_sems.at[k], device_id=to, device_id_type=MESH)

        mine = pltpu.make_async_copy(slab_hbm, slabs_hbm.at[_index(*me)], sem_l)
        mine.start()
        first = [slab_copy(0, me, sibling, src=slab_hbm)] + [slab_copy(1 + j, me, (*chip, c), src=slab_hbm) for j, chip in enumerate(chips)]
        for cp in first[:3]:
            cp.start()
        load_u = pltpu.make_async_copy(u_hbm, u_vm, sem_u)
        load_u.start()
        load_u.wait()

        def store(slot, block0, rows0):
            return pltpu.make_async_copy(obuf.at[slot], proj_hbm.at[pl.ds(block0, SLAB_INNER), pl.ds(rows0, rc)], sem_o.at[slot])

        def multiply(dev):
            load_w = pltpu.make_async_copy(slabs_hbm.at[dev, :, pl.ds(LANES, SLAB_INNER * LANES)], wbuf, sem_w)
            load_w.start()
            load_w.wait()
            block0 = _slab_block0(dev) + 1

            def chunk(r, carry):
                slot = lax.rem(r, 2)
                rows0 = pl.multiple_of(r * rc, rc)

                @pl.when(r >= 2)
                def _():
                    store(slot, block0, rows0).wait()

                res = _mm(u_vm[pl.ds(rows0, rc), :], wbuf[...])
                for j in range(SLAB_INNER):
                    obuf[slot, j] = res[:, j * LANES:(j + 1) * LANES].astype(BF16)
                store(slot, block0, rows0).start()
                return carry

            lax.fori_loop(0, n_chunks, chunk, 0)
            for r in range(max(0, n_chunks - 2), n_chunks):
                store(r % 2, block0, r * rc).wait()

        mine.wait()
        multiply(_index(*me))
        slab_copy(0, sibling, me).wait_recv()
        multiply(_index(*sibling))
        passed = []
        for j, chip in enumerate(chips[:2]):
            slab_copy(1 + j, (*chip, c), me).wait_recv()
            passed.append(slab_copy(4 + j, (*chip, c), sibling))
            passed[-1].start()
        first[1].wait_send()
        first[2].wait_send()
        first[3].start()
        multiply(_index(*chips[0], c))
        multiply(_index(*chips[1], c))
        for j, chip in enumerate(chips[:2]):
            slab_copy(4 + j, (*chip, 1 - c), me).wait_recv()
            if j == 1:
                slab_copy(3, (*chips[2], c), me).wait_recv()
                passed.append(slab_copy(6, (*chips[2], c), sibling))
                passed[-1].start()
            multiply(_index(*chip, 1 - c))
        multiply(_index(*chips[2], c))
        slab_copy(6, (*chips[2], 1 - c), me).wait_recv()
        multiply(_index(*chips[2], 1 - c))

        loads, n = [], 0
        for k, (_, srcs) in enumerate(edges):
            for d, j in srcs:
                dst = ebuf.at[:, pl.ds(k * LANES, LANES)] if len(srcs) == 1 else stage.at[n]
                loads.append(pltpu.make_async_copy(slabs_hbm.at[d, :, pl.ds(j * LANES, LANES)], dst, sem_s.at[n]))
                n += 1
        for cp in loads:
            cp.start()
        for cp in loads:
            cp.wait()
        n = 0
        for k, (_, srcs) in enumerate(edges):
            if len(srcs) == 2:
                ebuf[:, k * LANES:(k + 1) * LANES] = stage[n] + stage[n + 1]
            n += len(srcs)

        def edge_stores(slot, rows0):
            return [pltpu.make_async_copy(ebuf_out.at[slot, pl.ds(k0, length)],
                                          proj_hbm.at[pl.ds(edges[k0][0], length), pl.ds(rows0, rc)], sem_eo.at[slot, i])
                    for i, (k0, length) in enumerate(runs)]

        def edge_chunk(r, carry):
            slot = lax.rem(r, 2)
            rows0 = pl.multiple_of(r * rc, rc)

            @pl.when(r >= 2)
            def _():
                for cp in edge_stores(slot, rows0):
                    cp.wait()

            res = _mm(u_vm[pl.ds(rows0, rc), :], ebuf[...])
            for k in range(ne):
                ebuf_out[slot, k] = res[:, k * LANES:(k + 1) * LANES].astype(BF16)
            for cp in edge_stores(slot, rows0):
                cp.start()
            return carry

        lax.fori_loop(0, n_chunks, edge_chunk, 0)
        for r in range(max(0, n_chunks - 2), n_chunks):
            for cp in edge_stores(r % 2, r * rc):
                cp.wait()

        for cp in [first[0], first[3]] + passed:
            cp.wait_send()

    return _call(
        body, "inproj_fwd",
        out_shape=[jax.ShapeDtypeStruct((AL_COLS // LANES, t_rows, LANES), BF16), jax.ShapeDtypeStruct((N_DEV, D_MODEL, SLAB_W), BF16)],
        in_specs=[ANY] * 2, out_specs=[ANY] * 2,
        scratch_shapes=[pltpu.VMEM((t_rows, D_MODEL), BF16), pltpu.VMEM((D_MODEL, SLAB_INNER * LANES), BF16),
                        pltpu.VMEM((2, SLAB_INNER, rc, LANES), BF16), pltpu.VMEM((D_MODEL, ne * LANES), BF16),
                        pltpu.VMEM((n_stage, D_MODEL, LANES), BF16), pltpu.VMEM((2, ne, rc, LANES), BF16),
                        pltpu.SemaphoreType.DMA, pltpu.SemaphoreType.DMA, pltpu.SemaphoreType.DMA((2,)),
                        pltpu.SemaphoreType.DMA((n_stage,)), pltpu.SemaphoreType.DMA((2, len(runs))),
                        pltpu.SemaphoreType.DMA((7,)), pltpu.SemaphoreType.DMA((7,)), pltpu.SemaphoreType.DMA],
        compiler_params=_params(),
    )(u, slab_local)


def _ret_decay(lgh):
    i = lax.broadcasted_iota(jnp.int32, (TILE, TILE), 0)
    j = lax.broadcasted_iota(jnp.int32, (TILE, TILE), 1)
    rel = (i - j).astype(F32)
    return jnp.where(rel >= 0, jnp.exp(jnp.maximum(rel, 0.0) * lgh), 0.0)


def _ret_vectors(lgh):
    idx = lax.broadcasted_iota(jnp.int32, (TILE, 1), 0).astype(F32)
    xi = jnp.exp((idx + 1.0) * lgh)
    zeta = jnp.exp((TILE - 1.0 - idx) * lgh)
    gc = jnp.exp(jnp.full((1, 1), float(TILE), F32) * lgh)
    return xi, zeta, gc


def _ret_fwd(proj, cos, sin, gain, lg, row_shards):
    t_rows = cos.shape[0]
    nt = t_rows // TILE
    ns = len(row_shards)

    def body(lg_ref, q_ref, k_ref, v_ref, g_ref, cos_ref, sin_ref, gain_ref, *rest):
        shard_refs, (oraw_ref, oret_ref, st_ref), gathered = rest[:ns], rest[ns:ns + 3], rest[ns + 3:2 * ns + 3]
        s_acc, dm = rest[2 * ns + 3:2 * ns + 5]
        gather = _Exchange(shard_refs, gathered, rest[2 * ns + 5:], among_chips=False)
        h, t = pl.program_id(0), pl.program_id(1)
        lgh = lg_ref[h]

        @pl.when((h == 0) & (t == 0))
        def _():
            gather.start()

        @pl.when((h == RET_HEADS - 1) & (t == nt - 1))
        def _():
            gather.finish()

        @pl.when(t == 0)
        def _():
            s_acc[...] = jnp.zeros_like(s_acc)
            dm[...] = _ret_decay(lgh)

        cos_t, sin_t = cos_ref[...], sin_ref[...]
        q = _rope(_cols(q_ref).astype(F32), cos_t, sin_t)
        k = _rope(_cols(k_ref).astype(F32), cos_t, sin_t) * (RET_QK ** -0.5)
        xi, zeta, gc = _ret_vectors(lgh)
        v = _cols(v_ref)
        s_in = s_acc[...]
        p = (_mm_nt(q.astype(BF16), k.astype(BF16)) * dm[...]).astype(BF16)
        o = _mm(p, v) + _mm((q * xi).astype(BF16), s_in.astype(BF16))
        st_ref[...] = s_in.astype(BF16)
        s_acc[...] = s_in * gc + _mm_tn((k * zeta).astype(BF16), v)
        oraw_ref[...] = o
        oc = o - _row_mean(o)
        n = oc * lax.rsqrt(_row_mean(oc * oc) + EPS) * gain_ref[...]
        g = _cols(g_ref).astype(F32)
        oret_ref[...] = (n * g * _sigmoid(g)).astype(BF16)

    blk = lambda w: pl.BlockSpec((TILE, w), lambda h, t: (t, h))
    tab = pl.BlockSpec((TILE, LANES), lambda h, t: (t, 0))
    outs = _call(
        body, "ret_fwd", grid=(RET_HEADS, nt),
        out_shape=[jax.ShapeDtypeStruct((t_rows, RET_W), F32), jax.ShapeDtypeStruct((t_rows, RET_W), BF16),
                   jax.ShapeDtypeStruct((RET_HEADS, nt, RET_QK, RET_V), BF16)]
                  + [jax.ShapeDtypeStruct((N_DEV, *a.shape), a.dtype) for a in row_shards],
        in_specs=[pl.BlockSpec(memory_space=pltpu.SMEM)] + _proj_specs(("rq", "rk", "rv", "rg"), RET_HEADS, lambda h, t: (h, t)) + [tab, tab,
                  pl.BlockSpec((1, RET_V), lambda h, t: (0, h))] + [ANY] * ns,
        out_specs=[blk(RET_V), blk(RET_V), pl.BlockSpec((None, None, RET_QK, RET_V), lambda h, t: (h, t, 0, 0))] + [ANY] * ns,
        scratch_shapes=[pltpu.VMEM((RET_QK, RET_V), F32), pltpu.VMEM((TILE, TILE), F32)] + _exchange_sems(ns, N_DEV),
        compiler_params=_params(("arbitrary", "arbitrary")),
    )(lg, proj, proj, proj, proj, cos, sin, gain, *row_shards)
    return outs[0], outs[1], outs[2], outs[3:]


def _ret_bwd(proj, cos, sin, gain, lg, o_raw, do_ret, states):
    t_rows = cos.shape[0]
    nt = t_rows // TILE

    def body(lg_ref, q_ref, k_ref, v_ref, g_ref, cos_ref, sin_ref, gain_ref, oraw_ref, do_ref, st_ref,
             dq_ref, dk_ref, dv_ref, dg_ref, dgain_ref, e_acc, dm):
        h, j = pl.program_id(0), pl.program_id(1)
        lgh = lg_ref[h]

        @pl.when(j == 0)
        def _():
            e_acc[...] = jnp.zeros_like(e_acc)
            dm[...] = _ret_decay(lgh)
            dgain_ref[...] = jnp.zeros_like(dgain_ref)

        cos_t, sin_t = cos_ref[...], sin_ref[...]
        q = _rope(_cols(q_ref).astype(F32), cos_t, sin_t)
        k = _rope(_cols(k_ref).astype(F32), cos_t, sin_t) * (RET_QK ** -0.5)
        xi, zeta, gc = _ret_vectors(lgh)
        v = _cols(v_ref)
        g = _cols(g_ref).astype(F32)
        o = oraw_ref[...]
        do = do_ref[...].astype(F32)
        oc = o - _row_mean(o)
        rstd = lax.rsqrt(_row_mean(oc * oc) + EPS)
        xh = oc * rstd
        gain_t = gain_ref[...]
        sg = _sigmoid(g)
        dn = do * (g * sg)
        dg_ref[...] = (do * (xh * gain_t) * (sg * (1.0 + g * (1.0 - sg)))).astype(BF16)
        dgain_ref[...] += _col_sum(dn * xh)
        dxh = dn * gain_t
        dob = (rstd * (dxh - _row_mean(dxh) - xh * _row_mean(dxh * xh))).astype(BF16)
        dmat = dm[...]
        qb, kb = q.astype(BF16), k.astype(BF16)
        p = (_mm_nt(qb, kb) * dmat).astype(BF16)
        dp = (_mm_nt(dob, v) * dmat).astype(BF16)
        s_in = st_ref[...]
        e_in = e_acc[...]
        e_b = e_in.astype(BF16)
        dq = _mm(dp, kb) + _mm_nt(dob, s_in) * xi
        dk = _mm_tn(dp, qb) + _mm_nt(v, e_b) * zeta
        dv_ref[...] = (_mm_tn(p, dob) + _mm((k * zeta).astype(BF16), e_b)).astype(BF16)
        e_acc[...] = e_in * gc + _mm_tn((q * xi).astype(BF16), dob)
        dq_ref[...] = _rope_bwd(dq, cos_t, sin_t).astype(BF16)
        dk_ref[...] = (_rope_bwd(dk, cos_t, sin_t) * (RET_QK ** -0.5)).astype(BF16)

    blk = lambda w: pl.BlockSpec((TILE, w), lambda h, j: (nt - 1 - j, h))
    tab = pl.BlockSpec((TILE, LANES), lambda h, j: (nt - 1 - j, 0))
    vec = pl.BlockSpec((1, RET_V), lambda h, j: (0, h))
    return _call(
        body, "ret_bwd", grid=(RET_HEADS, nt),
        out_shape=[jax.ShapeDtypeStruct((t_rows, RET_HEADS * RET_QK), BF16), jax.ShapeDtypeStruct((t_rows, RET_HEADS * RET_QK), BF16),
                   jax.ShapeDtypeStruct((t_rows, RET_W), BF16), jax.ShapeDtypeStruct((t_rows, RET_W), BF16),
                   jax.ShapeDtypeStruct((1, RET_W), F32)],
        in_specs=[pl.BlockSpec(memory_space=pltpu.SMEM)] + _proj_specs(("rq", "rk", "rv", "rg"), RET_HEADS, lambda h, j: (h, nt - 1 - j)) + [tab, tab, vec,
                  blk(RET_V), blk(RET_V), pl.BlockSpec((None, None, RET_QK, RET_V), lambda h, j: (h, nt - 1 - j, 0, 0))],
        out_specs=[blk(RET_QK), blk(RET_QK), blk(RET_V), blk(RET_V), vec],
        scratch_shapes=[pltpu.VMEM((RET_QK, RET_V), F32), pltpu.VMEM((TILE, TILE), F32)],
        compiler_params=_params(("arbitrary", "arbitrary")),
    )(lg, proj, proj, proj, proj, cos, sin, gain, o_raw, do_ret, states)


GLA_LEVELS = (32, 64, 128, 256)
N_TERMS = 1 + len(GLA_LEVELS)


def _gla_tables():
    p = jnp.arange(TILE)[:, None]
    r = jnp.arange(TILE)[None, :]
    masks = [(p // GLA_CHUNK == r // GLA_CHUNK) & (r <= p)]
    for blk in GLA_LEVELS:
        masks.append((p // blk == r // blk) & (p % blk >= blk // 2) & (r % blk < blk // 2))
    masks = jnp.stack(masks + [m.T for m in masks]).astype(F32)
    cum_fwd = jnp.concatenate([r <= p, masks[0] > 0], axis=0).astype(BF16)
    cum_bwd = jnp.concatenate([r >= p, masks[N_TERMS] > 0], axis=1).astype(BF16)
    return masks, cum_fwd, cum_bwd


def _split3(x):
    hi = x.astype(BF16)
    rest = x - hi.astype(F32)
    mid = rest.astype(BF16)
    lo = (rest - mid.astype(F32)).astype(BF16)
    return jnp.concatenate([hi, mid, lo], axis=1)


def _join3(y):
    w = y.shape[1] // 3
    return (y[:, 2 * w:] + y[:, w:2 * w]) + y[:, :w]


def _gla_prep(q_ref, k_ref, glr_ref, wgu_ref, b_ref, cum_ref, g_scr, ref_scr):
    z = _mm(glr_ref[...].astype(BF16), wgu_ref[...].astype(BF16)) + b_ref[...]
    la = (jnp.minimum(z, 0.0) - jnp.log(1.0 + jnp.exp(-jnp.abs(z)))) / GLA_TAU
    gb = _join3(_mm(cum_ref[...], _split3(la)))
    g, b = gb[:TILE], gb[TILE:]
    g_scr[...] = g
    factors = [(jnp.exp(b), jnp.exp(-b))]
    for lvl, blk in enumerate(GLA_LEVELS):
        for n in range(TILE // blk):
            ref_scr[lvl, n * blk:(n + 1) * blk, :] = jnp.broadcast_to(g_scr[pl.ds(n * blk + blk // 2 - 1, 1), :], (blk, GLA_K))
        x = g - ref_scr[lvl]
        factors.append((jnp.exp(jnp.minimum(x, 0.0)), jnp.exp(jnp.minimum(-x, 0.0))))
    g_last = g_scr[pl.ds(TILE - 1, 1), :]
    q = _cols(q_ref).astype(F32) * (GLA_K ** -0.5)
    k = _cols(k_ref).astype(F32)
    return z, q, k, factors, jnp.exp(g), jnp.exp(g_last), jnp.exp(g_last - g)


def _gla_scores(q, k, factors, m_ref):
    a = jnp.zeros((TILE, TILE), F32)
    for l, (fq, fk) in enumerate(factors):
        s = _mm_nt((q * fq).astype(BF16), (k * fk).astype(BF16))
        a = jnp.where(m_ref[l] > 0.0, s, a)
    return a


def _gla_fwd(proj, glr, wgu_pad, b_gate, gain, masks, cum_fwd):
    t_rows = glr.shape[0]
    nt = t_rows // TILE

    def body(q_ref, k_ref, v_ref, g_ref, glr_ref, wgu_ref, b_ref, gain_ref, m_ref, cum_ref, oraw_ref, ogla_ref, st_ref,
             s_acc, g_scr, ref_scr):
        @pl.when(pl.program_id(1) == 0)
        def _():
            s_acc[...] = jnp.zeros_like(s_acc)

        _, q, k, factors, e_g, e_last, e_end = _gla_prep(q_ref, k_ref, glr_ref, wgu_ref, b_ref, cum_ref, g_scr, ref_scr)
        v = _cols(v_ref)
        st = s_acc[...]
        st_ref[...] = st
        a = _gla_scores(q, k, factors, m_ref)
        o = _mm(a.astype(BF16), v) + _mm_nt((q * e_g).astype(BF16), st.astype(BF16))
        s_acc[...] = st * e_last + _mm(v.astype(F32).T.astype(BF16), (k * e_end).astype(BF16))
        oraw_ref[...] = o
        n = o * lax.rsqrt(_row_mean(o * o) + EPS) * gain_ref[...]
        g = _cols(g_ref).astype(F32)
        ogla_ref[...] = (n * g * _sigmoid(g)).astype(BF16)

    blk = lambda w: pl.BlockSpec((TILE, w), lambda h, t: (t, h))
    return _call(
        body, "gla_fwd", grid=(GLA_HEADS, nt),
        out_shape=[jax.ShapeDtypeStruct((t_rows, GLA_W), F32), jax.ShapeDtypeStruct((t_rows, GLA_W), BF16),
                   jax.ShapeDtypeStruct((GLA_HEADS, nt, GLA_V, GLA_K), F32)],
        in_specs=_proj_specs(("gq", "gk", "gv", "gg"), GLA_HEADS, lambda h, t: (h, t)) + [pl.BlockSpec((TILE, LANES), lambda h, t: (t, 0)),
                  pl.BlockSpec((LANES, GLA_K), lambda h, t: (0, h)), pl.BlockSpec((1, GLA_K), lambda h, t: (0, h)),
                  pl.BlockSpec((1, GLA_V), lambda h, t: (0, h)),
                  pl.BlockSpec((N_TERMS, TILE, TILE), lambda h, t: (0, 0, 0)), pl.BlockSpec((2 * TILE, TILE), lambda h, t: (0, 0))],
        out_specs=[blk(GLA_V), blk(GLA_V), pl.BlockSpec((None, None, GLA_V, GLA_K), lambda h, t: (h, t, 0, 0))],
        scratch_shapes=[pltpu.VMEM((GLA_V, GLA_K), F32), pltpu.VMEM((TILE, GLA_K), F32),
                        pltpu.VMEM((len(GLA_LEVELS), TILE, GLA_K), F32)],
        compiler_params=_params(("arbitrary", "arbitrary")),
    )(proj, proj, proj, proj, glr, wgu_pad, b_gate, gain, masks, cum_fwd)


def _gla_bwd(proj, glr, wgu_pad, b_gate, gain, o_raw, do_gla, states, masks, cum_fwd, cum_bwd):
    t_rows = glr.shape[0]
    nt = t_rows // TILE

    def body(q_ref, k_ref, v_ref, g_ref, glr_ref, wgu_ref, b_ref, gain_ref, m_ref, cum_ref, cumb_ref, oraw_ref, do_ref, st_ref,
             dq_ref, dk_ref, dv_ref, dg_ref, dglr_ref, dwgu_ref, dbg_ref, dgain_ref, d_acc, g_scr, ref_scr, dref_scr):
        @pl.when(pl.program_id(1) == 0)
        def _():
            d_acc[...] = jnp.zeros_like(d_acc)
            dwgu_ref[...] = jnp.zeros_like(dwgu_ref)
            dbg_ref[...] = jnp.zeros_like(dbg_ref)
            dgain_ref[...] = jnp.zeros_like(dgain_ref)

        z, q, k, factors, e_g, e_last, e_end = _gla_prep(q_ref, k_ref, glr_ref, wgu_ref, b_ref, cum_ref, g_scr, ref_scr)
        v = _cols(v_ref)
        o = oraw_ref[...]
        do = do_ref[...].astype(F32)
        g = _cols(g_ref).astype(F32)
        rinv = lax.rsqrt(_row_mean(o * o) + EPS)
        nh = o * rinv
        gain_t = gain_ref[...]
        sg = _sigmoid(g)
        dn = do * (g * sg)
        dg_ref[...] = (do * (nh * gain_t) * (sg * (1.0 + g * (1.0 - sg)))).astype(BF16)
        dgain_ref[...] += _col_sum(dn * nh)
        dnh = dn * gain_t
        dor = rinv * (dnh - nh * _row_mean(dnh * nh))
        dob = dor.astype(BF16)
        a_t = _gla_scores(q, k, factors, m_ref).T.astype(BF16)
        da = _mm_nt(dob, v)
        da_t = _mm_nt(v, dob)
        st_in = st_ref[...]
        d_out = d_acc[...]
        d_out_b = d_out.astype(BF16)
        qg, kg = q * e_g, k * e_end
        dqg = _mm(dob, st_in.astype(BF16))
        dkg = _mm(v, d_out_b)
        dv_ref[...] = (_mm(a_t, dob) + _mm_nt(kg.astype(BF16), d_out_b)).astype(BF16)
        d_acc[...] = d_out * e_last + _mm(dor.T.astype(BF16), qg.astype(BF16))
        dq = dqg * e_g
        dk = dkg * e_end
        dkg_kg = dkg * kg
        dg_cum = dqg * qg - dkg_kg
        db = None
        for l, (fq, fk) in enumerate(factors):
            qt, kt = q * fq, k * fk
            dqt = _mm(jnp.where(m_ref[l] > 0.0, da, 0.0).astype(BF16), kt.astype(BF16))
            dkt = _mm(jnp.where(m_ref[N_TERMS + l] > 0.0, da_t, 0.0).astype(BF16), qt.astype(BF16))
            dq = dq + dqt * fq
            dk = dk + dkt * fk
            diff = dqt * qt - dkt * kt
            if l == 0:
                db = diff
            else:
                dg_cum = dg_cum + diff
                dref_scr[l - 1] = diff
        dq_ref[...] = (dq * (GLA_K ** -0.5)).astype(BF16)
        dk_ref[...] = dk.astype(BF16)
        g_scr[...] = dg_cum
        g_scr[pl.ds(TILE - 1, 1), :] += e_last * _col_sum(d_out * st_in) + _col_sum(dkg_kg)
        for lvl, blk in enumerate(GLA_LEVELS):
            for n in range(TILE // blk):
                g_scr[pl.ds(n * blk + blk // 2 - 1, 1), :] -= _col_sum(dref_scr[lvl, n * blk:(n + 1) * blk, :])
        dla = _join3(_mm(cumb_ref[...], jnp.concatenate([_split3(g_scr[...]), _split3(db)], axis=0)))
        dz = dla * (1.0 / GLA_TAU) * _sigmoid(-z)
        dzb = dz.astype(BF16)
        dglr_ref[...] = _mm_nt(dzb, wgu_ref[...].astype(BF16)).astype(BF16)
        dwgu_ref[...] += _mm(glr_ref[...].T.astype(BF16), dzb)
        dbg_ref[...] += _col_sum(dz)

    blk = lambda w: pl.BlockSpec((TILE, w), lambda h, j: (nt - 1 - j, h))
    vec = lambda w: pl.BlockSpec((1, w), lambda h, j: (0, h))
    wspec = pl.BlockSpec((LANES, GLA_K), lambda h, j: (0, h))
    return _call(
        body, "gla_bwd", grid=(GLA_HEADS, nt),
        out_shape=[jax.ShapeDtypeStruct((t_rows, GLA_HEADS * GLA_K), BF16), jax.ShapeDtypeStruct((t_rows, GLA_HEADS * GLA_K), BF16),
                   jax.ShapeDtypeStruct((t_rows, GLA_W), BF16), jax.ShapeDtypeStruct((t_rows, GLA_W), BF16),
                   jax.ShapeDtypeStruct((GLA_HEADS, t_rows, LANES), BF16), jax.ShapeDtypeStruct((LANES, GLA_HEADS * GLA_K), F32),
                   jax.ShapeDtypeStruct((1, GLA_HEADS * GLA_K), F32), jax.ShapeDtypeStruct((1, GLA_W), F32)],
        in_specs=_proj_specs(("gq", "gk", "gv", "gg"), GLA_HEADS, lambda h, j: (h, nt - 1 - j)) + [pl.BlockSpec((TILE, LANES), lambda h, j: (nt - 1 - j, 0)),
                  wspec, vec(GLA_K), vec(GLA_V),
                  pl.BlockSpec((2 * N_TERMS, TILE, TILE), lambda h, j: (0, 0, 0)), pl.BlockSpec((2 * TILE, TILE), lambda h, j: (0, 0)),
                  pl.BlockSpec((TILE, 2 * TILE), lambda h, j: (0, 0)), blk(GLA_V), blk(GLA_V),
                  pl.BlockSpec((None, None, GLA_V, GLA_K), lambda h, j: (h, nt - 1 - j, 0, 0))],
        out_specs=[blk(GLA_K), blk(GLA_K), blk(GLA_V), blk(GLA_V),
                   pl.BlockSpec((None, TILE, LANES), lambda h, j: (h, nt - 1 - j, 0)), wspec, vec(GLA_K), vec(GLA_V)],
        scratch_shapes=[pltpu.VMEM((GLA_V, GLA_K), F32), pltpu.VMEM((TILE, GLA_K), F32),
                        pltpu.VMEM((len(GLA_LEVELS), TILE, GLA_K), F32), pltpu.VMEM((len(GLA_LEVELS), TILE, GLA_K), F32)],
        compiler_params=_params(("arbitrary", "arbitrary")),
    )(proj, proj, proj, proj, glr, wgu_pad, b_gate, gain, masks, cum_fwd, cum_bwd, o_raw, do_gla, states)


def _merge_fwd_bwd(o_ret, o_gla, proj, x, target, g_final, w_br, w_bg, w_out):
    t_rows = x.shape[0] + TILE
    nt = t_rows // TILE

    def body(oret_ref, ogla_ref, mr_ref, mg_ref, h0_ref, tgt_ref, gf_ref, wbr_hbm, wbg_hbm, wout_hbm,
             dh1_ref, dmr_ref, dmg_ref, doret_ref, dogla_ref, loss_ref, dgf_ref, dwbr_hbm, dwbg_hbm, dwout_hbm,
             wbr, wbg, wout, abr, abg, aout, sem):
        i = pl.program_id(0)

        @pl.when(i == 0)
        def _():
            cps = [pltpu.make_async_copy(s, d, sem.at[n]) for n, (s, d) in enumerate(((wbr_hbm, wbr), (wbg_hbm, wbg), (wout_hbm, wout)))]
            for cp in cps:
                cp.start()
            abr[...] = jnp.zeros_like(abr)
            abg[...] = jnp.zeros_like(abg)
            aout[...] = jnp.zeros_like(aout)
            loss_ref[...] = jnp.zeros_like(loss_ref)
            dgf_ref[...] = jnp.zeros_like(dgf_ref)
            for cp in cps:
                cp.wait()
            dh1_ref[...] = jnp.zeros_like(dh1_ref)
            dmr_ref[...] = jnp.zeros_like(dmr_ref)
            dmg_ref[...] = jnp.zeros_like(dmg_ref)
            doret_ref[...] = jnp.zeros_like(doret_ref)
            dogla_ref[...] = jnp.zeros_like(dogla_ref)

        @pl.when(i > 0)
        def _():
            oret, ogla = oret_ref[...], ogla_ref[...]
            br, bg = _mm(oret, wbr[...]), _mm(ogla, wbg[...])
            sr, sg = _sigmoid(_cols(mr_ref).astype(F32)), _sigmoid(_cols(mg_ref).astype(F32))
            mb = (sr * br + sg * bg).astype(BF16)
            h1 = h0_ref[...] + _mm(mb, wout[...])
            r2 = lax.rsqrt(_row_mean(h1 * h1) + EPS)
            hn = h1 * r2
            gf = gf_ref[...]
            diff = hn * gf - tgt_ref[...]
            loss_ref[...] += 0.5 * jnp.sum(_row_mean(diff * diff))
            dy = diff * (1.0 / D_MODEL)
            dgf_ref[...] += _col_sum(dy * hn)
            dyg = dy * gf
            dh1 = r2 * (dyg - hn * _row_mean(dyg * hn))
            dh1_ref[...] = dh1
            dh1b = dh1.astype(BF16)
            dm = _mm_nt(dh1b, wout[...])
            aout[...] += _mm_tn(mb, dh1b)
            dbr = (dm * sr).astype(BF16)
            dbg = (dm * sg).astype(BF16)
            dmr_ref[...] = (dm * br * sr * (1.0 - sr)).astype(BF16)
            dmg_ref[...] = (dm * bg * sg * (1.0 - sg)).astype(BF16)
            doret_ref[...] = _mm_nt(dbr, wbr[...]).astype(BF16)
            dogla_ref[...] = _mm_nt(dbg, wbg[...]).astype(BF16)
            abr[...] += _mm_tn(oret, dbr)
            abg[...] += _mm_tn(ogla, dbg)

        @pl.when(i == nt - 1)
        def _():
            wbr[...] = abr[...].astype(BF16)
            wbg[...] = abg[...].astype(BF16)
            wout[...] = aout[...].astype(BF16)
            pltpu.sync_copy(wbr, dwbr_hbm)
            pltpu.sync_copy(wbg, dwbg_hbm)
            pltpu.sync_copy(wout, dwout_hbm)

    row = lambda w: pl.BlockSpec((TILE, w), lambda i: (i, 0))
    one = lambda w: pl.BlockSpec((1, w), lambda i: (0, 0))
    return _call(
        body, "merge_fwd_bwd", grid=(nt,),
        out_shape=[jax.ShapeDtypeStruct((t_rows, D_MODEL), F32), jax.ShapeDtypeStruct((t_rows, D_MODEL), BF16),
                   jax.ShapeDtypeStruct((t_rows, D_MODEL), BF16), jax.ShapeDtypeStruct((t_rows, RET_W), BF16),
                   jax.ShapeDtypeStruct((t_rows, GLA_W), BF16), jax.ShapeDtypeStruct((1, LANES), F32),
                   jax.ShapeDtypeStruct((1, D_MODEL), F32), jax.ShapeDtypeStruct((RET_W, D_MODEL), BF16),
                   jax.ShapeDtypeStruct((GLA_W, D_MODEL), BF16), jax.ShapeDtypeStruct((D_MODEL, D_MODEL), BF16)],
        in_specs=[row(RET_W), row(GLA_W)] + _proj_specs(("mr", "mg"), 1, lambda i: (0, i)) + [_x_spec(), _x_spec(), one(D_MODEL), ANY, ANY, ANY],
        out_specs=[row(D_MODEL), row(D_MODEL), row(D_MODEL), row(RET_W), row(GLA_W), one(LANES), one(D_MODEL), ANY, ANY, ANY],
        scratch_shapes=[pltpu.VMEM((RET_W, D_MODEL), BF16), pltpu.VMEM((GLA_W, D_MODEL), BF16), pltpu.VMEM((D_MODEL, D_MODEL), BF16),
                        pltpu.VMEM((RET_W, D_MODEL), F32), pltpu.VMEM((GLA_W, D_MODEL), F32), pltpu.VMEM((D_MODEL, D_MODEL), F32),
                        pltpu.SemaphoreType.DMA((3,))],
        compiler_params=_params(("arbitrary",)),
    )(o_ret, o_gla, proj, proj, x, target, g_final, w_br, w_bg, w_out)


def _inproj_bwd_x(dseg, dglr, head, x, dh1, g_norm, slabs, w_glr, chip_partials):
    t_rows = x.shape[0] + TILE
    nt = t_rows // TILE
    ne = len(chip_partials)

    def body(*refs):
        d_refs = refs[:10]
        dglr_ref, head_ref, x_ref, dh1_ref, g_ref, slabs_hbm, wg_hbm = refs[10:17]
        part_refs = refs[17:17 + ne]
        dx_ref, dhead_ref, dgn_ref = refs[17 + ne:20 + ne]
        landed = refs[20 + ne:20 + 2 * ne]
        w_vm, wg_vm, edge_vm, sem = refs[20 + 2 * ne:24 + 2 * ne]
        exchange = _Exchange(part_refs, landed, refs[24 + 2 * ne:], among_chips=True)

        @pl.when(pl.program_id(0) == 0)
        def _():
            exchange.start()
            dgn_ref[...] = jnp.zeros_like(dgn_ref)
            _load_weight(slabs_hbm, wg_hbm, w_vm, wg_vm, edge_vm, sem)

        @pl.when(pl.program_id(0) == nt - 1)
        def _():
            exchange.finish()

        dglr = dglr_ref[0].astype(F32)
        for h in range(1, GLA_HEADS):
            dglr = dglr + dglr_ref[h].astype(F32)
        du = _mm_nt(dglr.astype(BF16), wg_vm[...])
        for s, d_ref in enumerate(d_refs):
            du = du + _mm_nt(d_ref[...], w_vm[:, SEG_OFF[s]:SEG_OFF[s] + SEG_W[s]])
        x = _tile_rows(head_ref, x_ref)
        r = lax.rsqrt(_row_mean(x * x) + EPS)
        hn = x * r
        dgn_ref[...] += _col_sum(du * hn)
        dug = du * g_ref[...]
        dh0 = dh1_ref[...] + r * (dug - hn * _row_mean(dug * hn))
        dx_ref[...] = dh0

        @pl.when(pl.program_id(0) == 0)
        def _():
            dhead_ref[...] = dh0

    row = lambda w: pl.BlockSpec((TILE, w), lambda i: (i, 0))
    one = pl.BlockSpec((1, D_MODEL), lambda i: (0, 0))
    return _call(
        body, "inproj_bwd_x", grid=(nt,),
        out_shape=[jax.ShapeDtypeStruct((t_rows - TILE, D_MODEL), F32), jax.ShapeDtypeStruct((TILE, D_MODEL), F32),
                   jax.ShapeDtypeStruct((1, D_MODEL), F32)] + [jax.ShapeDtypeStruct(a.shape, a.dtype) for a in chip_partials],
        in_specs=[row(w) for w in SEG_W] + [pl.BlockSpec((GLA_HEADS, TILE, LANES), lambda i: (0, i, 0)),
                                            _head_spec(), _x_spec(), row(D_MODEL), one, ANY, ANY] + [ANY] * ne,
        out_specs=[_x_spec(), _head_spec(), one] + [ANY] * ne,
        scratch_shapes=W_SCRATCH() + _exchange_sems(ne, N_CHIP),
        compiler_params=_params(("arbitrary",)),
    )(*[dseg[n] for n in SEG_NAMES], dglr, head, x, dh1, g_norm, slabs, w_glr, *chip_partials)


W_TILE = 512


def _inproj_bwd_w(ut, dseg, dglr):
    nt = ut.shape[0]
    t_rows = nt * TILE
    kc = 3 if nt % 3 == 0 else 1
    tiles = [(s, c) for s in range(len(SEG_W)) for c in range(0, SEG_W[s], W_TILE)]
    bpt = W_TILE // LANES

    def body(ut_hbm, *refs):
        d_refs, dglr_hbm, out_hbm, oglr_ref = refs[:10], refs[10], refs[11], refs[12]
        ut_vm, dbuf, obuf, acc, gbuf, sem = refs[13:]

        def fetch(i):
            s, c = tiles[i]
            return pltpu.make_async_copy(d_refs[s].at[:, pl.ds(c, W_TILE)], dbuf.at[i % 2], sem.at[1 + i % 2])

        def contract(rhs_refs, width):
            acc[:, :width] = jnp.zeros((D_MODEL, width), F32)

            def step(k, carry):
                part = None
                for j in range(kc):
                    kk = k * kc + j
                    for rhs_ref in rhs_refs:
                        prod = _mm(ut_vm[kk], rhs_ref[pl.ds(pl.multiple_of(kk * TILE, TILE), TILE), :])
                        part = prod if part is None else part + prod
                acc[:, :width] += part
                return carry

            lax.fori_loop(0, nt // kc, step, 0)
            return acc[:, :width]

        load_ut = pltpu.make_async_copy(ut_hbm, ut_vm, sem.at[0])
        load_glr = pltpu.make_async_copy(dglr_hbm, gbuf, sem.at[5])
        load_ut.start()
        load_glr.start()
        fetch(0).start()
        load_ut.wait()
        stores = {}
        for i, (s, c) in enumerate(tiles):
            if i + 1 < len(tiles):
                fetch(i + 1).start()
            fetch(i).wait()
            if i >= 2:
                stores[i - 2].wait()
            total = contract([dbuf.at[i % 2]], W_TILE)
            for j in range(bpt):
                obuf[i % 2, j] = total[:, j * LANES:(j + 1) * LANES].astype(BF16)
            blk0 = (SEG_OFF[s] + c) // LANES
            stores[i] = pltpu.make_async_copy(obuf.at[i % 2], out_hbm.at[pl.ds(blk0, bpt)], sem.at[3 + i % 2])
            stores[i].start()
        load_glr.wait()
        oglr_ref[...] = contract([gbuf.at[h] for h in range(GLA_HEADS)], LANES)
        for i in range(max(0, len(tiles) - 2), len(tiles)):
            stores[i].wait()

    return _call(
        body, "inproj_bwd_w",
        out_shape=[jax.ShapeDtypeStruct((AL_COLS // LANES, D_MODEL, LANES), BF16), jax.ShapeDtypeStruct((D_MODEL, LANES), F32)],
        in_specs=[ANY] * 12, out_specs=[ANY, pl.BlockSpec(memory_space=pltpu.VMEM)],
        scratch_shapes=[pltpu.VMEM((nt, D_MODEL, TILE), BF16), pltpu.VMEM((2, t_rows, W_TILE), BF16),
                        pltpu.VMEM((2, bpt, D_MODEL, LANES), BF16), pltpu.VMEM((D_MODEL, W_TILE), F32),
                        pltpu.VMEM((GLA_HEADS, t_rows, LANES), BF16), pltpu.SemaphoreType.DMA((6,))],
        compiler_params=_params(),
    )(ut, *[dseg[n] for n in SEG_NAMES], dglr)


def _position():
    x, y, c = lax.axis_index("x"), lax.axis_index("y"), lax.axis_index("c")
    return x, y, c


def _index(px, py, pc):
    return 4 * px + 2 * py + pc


def _all_gather(arrs, name):
    n = len(arrs)

    def body(*refs):
        ins, outs = refs[:n], refs[n:2 * n]
        send_sems, recv_sems, local_sems = refs[2 * n:]
        x, y, c = _position()
        me, sibling = (x, y, c), (x, y, 1 - c)
        chips = [(1 - x, y), (x, 1 - y), (1 - x, 1 - y)]

        def copy(a, k, block, to, src=None):
            dst = outs[a].at[_index(*block)]
            return pltpu.make_async_remote_copy(src_ref=dst if src is None else src, dst_ref=dst,
                                                send_sem=send_sems.at[7 * a + k], recv_sem=recv_sems.at[7 * a + k],
                                                device_id=to, device_id_type=MESH)

        mine = [pltpu.make_async_copy(ins[a], outs[a].at[_index(*me)], local_sems.at[a]) for a in range(n)]
        for cp in mine:
            cp.start()
        first = []
        for a in range(n):
            first.append(copy(a, 0, me, sibling, src=ins[a]))
            first += [copy(a, 1 + j, me, (*chip, c), src=ins[a]) for j, chip in enumerate(chips)]
        for cp in first:
            cp.start()
        passed = []
        for j, chip in enumerate(chips):
            for a in range(n):
                copy(a, 1 + j, (*chip, c), me).wait_recv()
                cp = copy(a, 4 + j, (*chip, c), sibling)
                cp.start()
                passed.append(cp)
        for a in range(n):
            copy(a, 0, sibling, me).wait_recv()
            for j, chip in enumerate(chips):
                copy(a, 4 + j, (*chip, 1 - c), me).wait_recv()
        for cp in first + passed:
            cp.wait_send()
        for cp in mine:
            cp.wait()

    return _call(
        body, name,
        out_shape=[jax.ShapeDtypeStruct((N_DEV, *a.shape), a.dtype) for a in arrs],
        in_specs=[ANY] * n, out_specs=[ANY] * n,
        scratch_shapes=[pltpu.SemaphoreType.DMA((7 * n,)), pltpu.SemaphoreType.DMA((7 * n,)), pltpu.SemaphoreType.DMA((n,))],
    )(*arrs)


N_CHIP = N_DEV // 2


def _slab_block0(owner):
    step = SLAB_BLK0[1]
    assert all(SLAB_BLK0[d] == step * d - (d == N_DEV - 1) for d in range(N_DEV))
    return step * owner - jnp.where(owner == N_DEV - 1, 1, 0)


def _exchange_sibling(dw_blocks, row_sends):
    n = 1 + len(row_sends)

    def body(*refs):
        dw_ref, row_refs, outs, (send_sems, recv_sems) = refs[0], refs[1:n], refs[n:2 * n], refs[2 * n:]
        x, y, c = _position()
        copies = []
        for q in range(N_CHIP):
            owner = 2 * q + (1 - c)
            srcs = [dw_ref.at[pl.ds(_slab_block0(owner), SLAB_BLOCKS)]] + [r.at[owner] for r in row_refs]
            for k, src in enumerate(srcs):
                copies.append(pltpu.make_async_remote_copy(src_ref=src, dst_ref=outs[k].at[q], send_sem=send_sems.at[n * q + k],
                                                           recv_sem=recv_sems.at[n * q + k], device_id=(x, y, 1 - c), device_id_type=MESH))
        for cp in copies:
            cp.start()
        for cp in copies:
            cp.wait()

    return _call(
        body, "exchange_sibling",
        out_shape=[jax.ShapeDtypeStruct((N_CHIP, SLAB_BLOCKS, D_MODEL, LANES), BF16)]
                  + [jax.ShapeDtypeStruct((N_CHIP, *r.shape[1:]), BF16) for r in row_sends],
        in_specs=[ANY] * n, out_specs=[ANY] * n,
        scratch_shapes=[pltpu.SemaphoreType.DMA((n * N_CHIP,)), pltpu.SemaphoreType.DMA((n * N_CHIP,))],
    )(dw_blocks, *row_sends)


def _add_bf16(c_ref, a_ref, b_ref, o_ref):
    o_ref[...] = (a_ref[...].astype(F32) + b_ref[...].astype(F32)).astype(BF16)


def _chip_partial_slab(dw_blocks, sib, core):
    blk = pl.BlockSpec((None, SLAB_BLOCKS, D_MODEL, LANES), lambda q, c_ref: (q, 0, 0, 0))
    return _call(
        functools.partial(_add_bf16), "chip_partial_w_in", out_shape=jax.ShapeDtypeStruct(sib.shape, BF16),
        grid_spec=pltpu.PrefetchScalarGridSpec(
            num_scalar_prefetch=1, grid=(N_CHIP,),
            in_specs=[pl.BlockSpec((pl.Element(SLAB_BLOCKS), pl.Element(D_MODEL), pl.Element(LANES)),
                                   lambda q, c_ref: (_slab_block0(2 * q + c_ref[0]), 0, 0)), blk],
            out_specs=blk),
        compiler_params=_params(("arbitrary",)),
    )(core, dw_blocks, sib)


def _chip_partial_rows(send, sib, core, name):
    rows, cols = send.shape[1:]
    blk = pl.BlockSpec((None, rows, cols), lambda q, c_ref: (q, 0, 0))
    return _call(
        functools.partial(_add_bf16), name, out_shape=jax.ShapeDtypeStruct(sib.shape, BF16),
        grid_spec=pltpu.PrefetchScalarGridSpec(
            num_scalar_prefetch=1, grid=(N_CHIP,),
            in_specs=[pl.BlockSpec((None, rows, cols), lambda q, c_ref: (2 * q + c_ref[0], 0, 0)), blk], out_specs=blk),
        compiler_params=_params(("arbitrary",)),
    )(core, send, sib)


def _exchange_sems(n_arrays, n_peers):
    return [pltpu.SemaphoreType.DMA((n_arrays * n_peers,)), pltpu.SemaphoreType.DMA((n_arrays * n_peers,)),
            pltpu.SemaphoreType.DMA((n_arrays,))]


class _Exchange:
    def __init__(self, srcs, dsts, sems, among_chips):
        self.arrs = list(zip(srcs, dsts))
        self.n = len(self.arrs)
        self.send_sems, self.recv_sems, self.local_sems = sems
        self.among_chips = among_chips
        x, y, c = _position()
        self.c = c
        self.me = 2 * x + y if among_chips else _index(x, y, c)
        self.n_peers = N_CHIP if among_chips else N_DEV

    def _device(self, p):
        return (p // 2, p % 2, self.c) if self.among_chips else (p // 4, (p // 2) % 2, p % 2)

    def _src(self, k, p):
        src = self.arrs[k][0]
        return src.at[p] if self.among_chips else src

    def _mine(self):
        return [pltpu.make_async_copy(self._src(k, self.me), self.arrs[k][1].at[self.me], self.local_sems.at[k]) for k in range(self.n)]

    def _copy(self, p, k, landing):
        return pltpu.make_async_remote_copy(
            src_ref=self._src(k, p), dst_ref=self.arrs[k][1].at[landing], send_sem=self.send_sems.at[self.n * p + k],
            recv_sem=self.recv_sems.at[self.n * landing + k], device_id=self._device(p), device_id_type=MESH)

    def _others(self, fn):
        for p in range(self.n_peers):
            @pl.when(p != self.me)
            def _():
                for k in range(self.n):
                    fn(p, k)

    def start(self):
        for cp in self._mine():
            cp.start()
        self._others(lambda p, k: self._copy(p, k, self.me).start())

    def finish(self):
        self._others(lambda p, k: self._copy(p, k, p).wait_recv())
        self._others(lambda p, k: self._copy(p, k, self.me).wait_send())
        for cp in self._mine():
            cp.wait()


def _adamw(g, w, m, v):
    m_new = ADAM_B1 * m + (1.0 - ADAM_B1) * g
    v_new = ADAM_B2 * v + (1.0 - ADAM_B2) * (g * g)
    m_hat = m_new / (1.0 - ADAM_B1 ** ADAM_STEP)
    v_hat = v_new / (1.0 - ADAM_B2 ** ADAM_STEP)
    delta = -ADAM_LR * (m_hat / (jnp.sqrt(v_hat) + ADAM_EPS) + ADAM_WD * w)
    return delta, m_new, v_new


def _sum_partials(p_ref):
    g = p_ref[0].astype(F32)
    for d in range(1, p_ref.shape[0]):
        g = g + p_ref[d].astype(F32)
    return g


def _reduce_adam(parts, w, m, v, name, block_rows, row_off=0):
    rows, cols = w.shape
    off = row_off // block_rows

    def body(p_ref, w_ref, m_ref, v_ref, g_ref, d_ref, mo_ref, vo_ref):
        g = _sum_partials(p_ref)
        g_ref[...] = g
        d_ref[...], mo_ref[...], vo_ref[...] = _adamw(g, w_ref[...], m_ref[...], v_ref[...])

    blk = pl.BlockSpec((block_rows, cols), lambda i: (i, 0))
    return _call(
        body, name, grid=(rows // block_rows,),
        out_shape=[jax.ShapeDtypeStruct((rows, cols), F32)] * 4,
        in_specs=[pl.BlockSpec((parts.shape[0], block_rows, cols), lambda i: (0, i + off, 0)), blk, blk, blk],
        out_specs=[blk] * 4,
        compiler_params=_params(("arbitrary",)),
    )(parts, w, m, v)


def _reduce_adam_slab(parts, glr, w, m, v, me):
    rows, cols = w.shape
    shift = jnp.asarray(SLAB_SHIFT, jnp.int32)[me]
    glr_at = jnp.where(me == GLR_DEV, GLR_LOCAL, cols).astype(jnp.int32)

    def body(s_ref, p_ref, glr_ref, w_ref, m_ref, v_ref, g_ref, d_ref, mo_ref, vo_ref):
        shift, glr_at = s_ref[0], s_ref[1]
        slab = jnp.concatenate([_sum_partials(p_ref.at[:, j]) for j in range(SLAB_BLOCKS)], axis=1)
        before = pltpu.roll(slab, SLAB_W - shift, 1)
        after = pltpu.roll(slab, lax.rem(SLAB_W - shift + GLA_RANK, SLAB_W), 1)
        wide = jnp.concatenate([glr_ref[...], jnp.zeros((LANES, SLAB_W - LANES), F32)], axis=1)
        placed = pltpu.roll(wide, lax.rem(glr_at, SLAB_W), 1)
        lane = lax.broadcasted_iota(jnp.int32, (LANES, SLAB_W), 1)
        g = jnp.where(lane < glr_at, before, jnp.where(lane < glr_at + GLA_RANK, placed, after))[:, :cols]
        g_ref[...] = g
        d_ref[...], mo_ref[...], vo_ref[...] = _adamw(g, w_ref[...], m_ref[...], v_ref[...])

    blk = pl.BlockSpec((LANES, cols), lambda i, s: (i, 0))
    return _call(
        body, "adam_w_in", out_shape=[jax.ShapeDtypeStruct((rows, cols), F32)] * 4,
        grid_spec=pltpu.PrefetchScalarGridSpec(
            num_scalar_prefetch=1, grid=(rows // LANES,),
            in_specs=[pl.BlockSpec((parts.shape[0], SLAB_BLOCKS, LANES, LANES), lambda i, s: (0, 0, i, 0)),
                      pl.BlockSpec((LANES, LANES), lambda i, s: (i, 0)), blk, blk, blk],
            out_specs=[blk] * 4),
        compiler_params=_params(("arbitrary",)),
    )(jnp.stack([shift, glr_at]), parts, glr, w, m, v)


def _reduce_small(parts):
    def body(p_ref, o_ref):
        o_ref[...] = _sum_partials(p_ref)

    return _call(body, "reduce_small", out_shape=jax.ShapeDtypeStruct(parts.shape[1:], F32))(parts)


def _adam_small(g, w, m, v):
    def body(g_ref, w_ref, m_ref, v_ref, d_ref, mo_ref, vo_ref):
        d_ref[...], mo_ref[...], vo_ref[...] = _adamw(g_ref[...], w_ref[...], m_ref[...], v_ref[...])

    return _call(body, "adam_small", out_shape=[jax.ShapeDtypeStruct(g.shape, F32)] * 3)(g, w, m, v)


def _pack_rows(arrs):
    rows = []
    for a in arrs:
        flat = a.reshape(-1).astype(F32)
        pad = (-flat.shape[0]) % LANES
        rows.append(jnp.pad(flat, (0, pad)).reshape(-1, LANES))
    packed = jnp.concatenate(rows, axis=0)
    return jnp.pad(packed, ((0, (-packed.shape[0]) % 8), (0, 0)))


def _unpack_rows(packed, shapes):
    out, r = [], 0
    for shp in shapes:
        size = 1
        for s in shp:
            size *= s
        nrows = -(-size // LANES)
        out.append(packed[r:r + nrows].reshape(-1)[:size].reshape(shp))
        r += nrows
    return out


def _shard_to_slab(shard, d):
    glr = jnp.zeros((D_MODEL, GLA_RANK), shard.dtype)
    if d == GLR_DEV:
        glr = shard[:, GLR_LOCAL:GLR_LOCAL + GLA_RANK]
        shard = jnp.concatenate([shard[:, :GLR_LOCAL], shard[:, GLR_LOCAL + GLA_RANK:]], axis=1)
    return jnp.pad(shard, ((0, 0), (SLAB_SHIFT[d], SLAB_W - SLAB_SHIFT[d] - shard.shape[1]))), glr


def kernel(x, meta_tokens, norm_gain, w_in, w_gate_up, b_gate, ret_norm_gain, gla_norm_gain, w_branch_ret, w_branch_gla, w_out, final_norm_gain, loss_target, m_meta_tokens, m_norm_gain, m_w_in, m_w_gate_up, m_b_gate, m_ret_norm_gain, m_gla_norm_gain, m_w_branch_ret, m_w_branch_gla, m_w_out, m_final_norm_gain, v_meta_tokens, v_norm_gain, v_w_in, v_w_gate_up, v_b_gate, v_ret_norm_gain, v_gla_norm_gain, v_w_branch_ret, v_w_branch_gla, v_w_out, v_final_norm_gain):
    xi, yi, ci = _position()
    me = _index(xi, yi, ci)
    seq = x.shape[1]
    t_rows = seq + TILE
    in_shard = w_in.shape[2]
    gu_shard = w_gate_up.shape[2]
    meta_shard = meta_tokens.shape[1]
    ret_rows, gla_rows, out_rows = w_branch_ret.shape[1], w_branch_gla.shape[1], w_out.shape[1]

    assert in_shard == IN_SHARD
    slab_local, glr_local = lax.switch(me, [functools.partial(_shard_to_slab, d=d) for d in range(N_DEV)], w_in[0])
    small_local = jnp.concatenate([meta_tokens, jnp.pad(w_gate_up[0], ((0, 0), (0, LANES - gu_shard))),
                                   glr_local.reshape(-1, LANES)], axis=0)
    (g_small,) = _all_gather([small_local], "all_gather_small_weights")
    n_small = N_META + GLA_RANK
    w_glr = jnp.pad(g_small[GLR_DEV, n_small:].reshape(D_MODEL, GLA_RANK), ((0, 0), (0, LANES - GLA_RANK))).astype(BF16)
    meta_full = jnp.transpose(g_small[:, :N_META, :], (1, 0, 2)).reshape(N_META, D_MODEL)
    wgu_full = jnp.transpose(g_small[:, N_META:n_small, :gu_shard], (1, 0, 2)).reshape(GLA_RANK, GLA_HEADS * GLA_K)
    wgu_pad = jnp.pad(wgu_full, ((0, LANES - GLA_RANK), (0, 0)))

    pos = jnp.arange(t_rows, dtype=F32) - float(PAD_ROWS)
    half = RET_QK // 2
    inv = ROPE_BASE ** (-jnp.arange(half, dtype=F32) / half)
    ang = pos[:, None] * inv[None, :]
    cos, sin = jnp.cos(ang), jnp.sin(ang)
    lg = jnp.log1p(-(2.0 ** (-5.0 - jnp.arange(RET_HEADS, dtype=F32))))

    head = jnp.concatenate([jnp.zeros((PAD_ROWS, D_MODEL), F32), meta_full], axis=0)
    u, ut, glr = _prenorm(head, x[0], norm_gain, w_glr)
    proj, slabs = _inproj_fwd(u, slab_local.astype(BF16))
    o_ret_raw, o_ret, ret_states, (g_br, g_bg, g_o) = _ret_fwd(
        proj, cos, sin, ret_norm_gain, lg, [w_branch_ret[0].astype(BF16), w_branch_gla[0].astype(BF16), w_out[0].astype(BF16)])
    w_br, w_bg, w_o = g_br.reshape(RET_W, D_MODEL), g_bg.reshape(GLA_W, D_MODEL), g_o.reshape(D_MODEL, D_MODEL)
    masks, cum_fwd, cum_bwd = _gla_tables()
    o_gla_raw, o_gla, gla_states = _gla_fwd(proj, glr, wgu_pad, b_gate, gla_norm_gain, masks, cum_fwd)
    (dh1, d_mr, d_mg, do_ret, do_gla, loss_part, d_gfinal, dw_br, dw_bg, dw_o) = _merge_fwd_bwd(
        o_ret, o_gla, proj, x[0], loss_target[0], final_norm_gain.reshape(1, D_MODEL), w_br, w_bg, w_o)

    d_rq, d_rk, d_rv, d_rg, d_gret = _ret_bwd(proj, cos, sin, ret_norm_gain, lg, o_ret_raw, do_ret, ret_states)
    d_gq, d_gk, d_gv, d_gg, dglr_parts, d_wgu, d_bgate, d_ggla = _gla_bwd(
        proj, glr, wgu_pad, b_gate, gla_norm_gain, o_gla_raw, do_gla, gla_states, masks, cum_fwd, cum_bwd)
    dseg = dict(rq=d_rq, rk=d_rk, rv=d_rv, rg=d_rg, gq=d_gq, gk=d_gk, gv=d_gv, gg=d_gg, mr=d_mr, mg=d_mg)
    dw_blocks, dw_glr = _inproj_bwd_w(ut, dseg, dglr_parts)

    row_sends = [dw_br.reshape(N_DEV, ret_rows, D_MODEL), dw_bg.reshape(N_DEV, gla_rows, D_MODEL),
                 dw_o.reshape(N_DEV, out_rows, D_MODEL)]
    sib_in, *sib_rows = _exchange_sibling(dw_blocks, row_sends)
    core = ci.astype(jnp.int32).reshape(1)
    chip_partials = [_chip_partial_slab(dw_blocks, sib_in, core)] + [
        _chip_partial_rows(send, sib, core, "chip_partial_" + name)
        for send, sib, name in zip(row_sends, sib_rows, ("w_branch_ret", "w_branch_gla", "w_out"))]
    grad_x, d_head, d_gnorm, p_in, p_br, p_bg, p_o = _inproj_bwd_x(
        dseg, dglr_parts, head, x[0], dh1, norm_gain, slabs, w_glr, chip_partials)
    small_shapes = [(N_META, D_MODEL), (1, D_MODEL), (GLA_RANK, GLA_HEADS * GLA_K), (1, GLA_HEADS * GLA_K),
                    (1, RET_W), (1, GLA_W), (1, D_MODEL), (1, LANES), (D_MODEL, GLA_RANK)]
    small_part = _pack_rows([d_head[PAD_ROWS:], d_gnorm, d_wgu[:GLA_RANK], d_bgate, d_gret, d_ggla, d_gfinal, loss_part,
                             dw_glr[:, :GLA_RANK]])
    (p_small,) = _all_gather([small_part], "all_gather_small_partials")

    (g_meta_f, g_gnorm, g_wgu_f, g_bgate, g_gret, g_ggla, g_gfinal, loss_all,
     g_wglr) = _unpack_rows(_reduce_small(p_small), small_shapes)
    g_w_in, d_w_in, nm_w_in, nv_w_in = _reduce_adam_slab(
        p_in, jnp.pad(g_wglr, ((0, 0), (0, LANES - GLA_RANK))), w_in[0], m_w_in[0], v_w_in[0], me)
    rb = gla_rows
    g_w_br, d_w_br, nm_w_br, nv_w_br = _reduce_adam(p_br, w_branch_ret[0], m_w_branch_ret[0], v_w_branch_ret[0], "adam_w_branch_ret", rb)
    g_w_bg, d_w_bg, nm_w_bg, nv_w_bg = _reduce_adam(p_bg, w_branch_gla[0], m_w_branch_gla[0], v_w_branch_gla[0], "adam_w_branch_gla", rb)
    g_w_o, d_w_o, nm_w_o, nv_w_o = _reduce_adam(p_o, w_out[0], m_w_out[0], v_w_out[0], "adam_w_out", rb)
    g_meta = lax.dynamic_slice_in_dim(g_meta_f, me * meta_shard, meta_shard, axis=1)
    g_wgu = lax.dynamic_slice_in_dim(g_wgu_f, me * gu_shard, gu_shard, axis=1)
    s_g = [g_meta, g_gnorm, g_wgu, g_bgate, g_gret, g_ggla, g_gfinal]
    s_w = [meta_tokens, norm_gain, w_gate_up[0], b_gate, ret_norm_gain, gla_norm_gain, final_norm_gain]
    s_m = [m_meta_tokens, m_norm_gain, m_w_gate_up[0], m_b_gate, m_ret_norm_gain, m_gla_norm_gain, m_final_norm_gain]
    s_v = [v_meta_tokens, v_norm_gain, v_w_gate_up[0], v_b_gate, v_ret_norm_gain, v_gla_norm_gain, v_final_norm_gain]
    shapes = [a.shape for a in s_g]
    s_d, s_nm, s_nv = [_unpack_rows(p, shapes) for p in _adam_small(*[_pack_rows(l) for l in (s_g, s_w, s_m, s_v)])]

    loss = loss_all[0, 0]
    grad_x = grad_x[None]

    def order(meta, gnorm, win, wgu, bgate, gret, ggla, wbr, wbg, wo, gfin):
        return (meta, gnorm, win[None], wgu[None], bgate, gret, ggla, wbr[None], wbg[None], wo[None], gfin.reshape(final_norm_gain.shape))

    def small(l):
        return dict(meta=l[0], gnorm=l[1], wgu=l[2], bgate=l[3], gret=l[4], ggla=l[5], gfin=l[6])

    grads = order(win=g_w_in, wbr=g_w_br, wbg=g_w_bg, wo=g_w_o, **small(s_g))
    deltas = order(win=d_w_in, wbr=d_w_br, wbg=d_w_bg, wo=d_w_o, **small(s_d))
    new_m = order(win=nm_w_in, wbr=nm_w_br, wbg=nm_w_bg, wo=nm_w_o, **small(s_nm))
    new_v = order(win=nv_w_in, wbr=nv_w_br, wbg=nv_w_bg, wo=nv_w_o, **small(s_nv))
    return (loss, grad_x, *grads, *deltas, *new_m, *new_v)
```

```python
import functools

import jax
import jax.numpy as jnp
from jax import lax
from jax.experimental import pallas as pl
from jax.experimental.pallas import tpu as pltpu

F32 = jnp.float32
BF16 = jnp.bfloat16

D_MODEL = 1024
N_META = 16
TILE = 256
PAD_ROWS = TILE - N_META
RET_HEADS = 4
RET_QK = 256
RET_V = 512
RET_W = RET_HEADS * RET_V
GLA_HEADS = 4
GLA_K = 128
GLA_V = 256
GLA_W = GLA_HEADS * GLA_V
GLA_RANK = 16
GLA_TAU = 16.0
GLA_CHUNK = 16
ROPE_BASE = 10000.0
EPS = 1e-6
LANES = 128
N_DEV = 8
SEG_NAMES = ("rq", "rk", "rv", "rg", "gq", "gk", "gv", "gg", "mr", "mg")
SEG_W = (1024, 1024, 2048, 2048, 512, 512, 1024, 1024, 1024, 1024)
SEG_OFF = tuple(sum(SEG_W[:i]) for i in range(len(SEG_W)))
AL_COLS = sum(SEG_W)
IN_COLS = AL_COLS + GLA_RANK
GLR_OFF = sum(SEG_W[:8])
IN_SHARD = IN_COLS // N_DEV


def _aligned_col(c):
    assert c <= GLR_OFF or c >= GLR_OFF + GLA_RANK
    return c if c <= GLR_OFF else c - GLA_RANK


SLAB_BOUND = tuple(_aligned_col(IN_SHARD * d) for d in range(N_DEV + 1))
SLAB_BLK0 = tuple(b // LANES for b in SLAB_BOUND[:-1])
SLAB_SHIFT = tuple(b % LANES for b in SLAB_BOUND[:-1])
SLAB_BLOCKS = max(-(-SLAB_BOUND[d + 1] // LANES) - SLAB_BLK0[d] for d in range(N_DEV))
SLAB_W = SLAB_BLOCKS * LANES
GLR_DEV = GLR_OFF // IN_SHARD
GLR_LOCAL = GLR_OFF - GLR_DEV * IN_SHARD
assert all(SLAB_BLK0[d] + SLAB_BLOCKS <= AL_COLS // LANES for d in range(N_DEV))
VMEM_LIMIT = 58 * 1024 * 1024
ADAM_LR, ADAM_B1, ADAM_B2, ADAM_EPS, ADAM_WD, ADAM_STEP = 0.001, 0.9, 0.999, 1e-08, 0.01, 10
ANY = pl.BlockSpec(memory_space=pl.ANY)
MESH = pl.DeviceIdType.MESH


def _call(body, name, **kw):
    return pl.pallas_call(body, name=name, **kw)


def _params(sem=None):
    return pltpu.CompilerParams(dimension_semantics=sem, vmem_limit_bytes=VMEM_LIMIT)


def _mm(a, b):
    return jnp.dot(a, b, preferred_element_type=F32)


def _mm_nt(a, b):
    return lax.dot_general(a, b, (((1,), (1,)), ((), ())), preferred_element_type=F32)


def _mm_tn(a, b):
    return lax.dot_general(a, b, (((0,), (0,)), ((), ())), preferred_element_type=F32)


def _sigmoid(x):
    return 1.0 / (1.0 + jnp.exp(-x))


def _rope(t, cos, sin):
    half = t.shape[-1] // 2
    t1, t2 = t[:, :half], t[:, half:]
    return jnp.concatenate([t1 * cos - t2 * sin, t2 * cos + t1 * sin], axis=-1)


def _rope_bwd(g, cos, sin):
    half = g.shape[-1] // 2
    g1, g2 = g[:, :half], g[:, half:]
    return jnp.concatenate([g1 * cos + g2 * sin, g2 * cos - g1 * sin], axis=-1)


def _row_mean(x):
    return jnp.mean(x, axis=-1, keepdims=True)


def _col_sum(x):
    return jnp.sum(x, axis=0, keepdims=True)


def _tile_rows(head_ref, x_ref):
    return jnp.where(pl.program_id(0) == 0, head_ref[...], x_ref[...])


def _head_spec():
    return pl.BlockSpec((TILE, D_MODEL), lambda i: (0, 0))


def _x_spec():
    return pl.BlockSpec((TILE, D_MODEL), lambda i: (jnp.maximum(i - 1, 0), 0))


def _slab_plan():
    interior, shared = [], []
    for d in range(N_DEV):
        lo, hi = -(-SLAB_BOUND[d] // LANES), SLAB_BOUND[d + 1] // LANES
        interior.append((d, LANES * (lo - SLAB_BLK0[d]), LANES * lo, LANES * (hi - lo)))
        if d + 1 < N_DEV and SLAB_BOUND[d + 1] % LANES:
            shared.append((hi, d, hi - SLAB_BLK0[d]))
    return interior, shared


W_SCRATCH = lambda: [pltpu.VMEM((D_MODEL, AL_COLS), BF16), pltpu.VMEM((D_MODEL, LANES), BF16),
                     pltpu.VMEM((2 * (N_DEV - 1), D_MODEL, LANES), BF16), pltpu.SemaphoreType.DMA((3 * N_DEV,))]


def _load_weight(slabs_hbm, wg_hbm, w_vm, wg_vm, edge_vm, sem):
    interior, shared = _slab_plan()
    copies = [pltpu.make_async_copy(wg_hbm, wg_vm, sem.at[0])]
    for d, src, dst, width in interior:
        copies.append(pltpu.make_async_copy(slabs_hbm.at[d, :, pl.ds(src, width)], w_vm.at[:, pl.ds(dst, width)], sem.at[1 + d]))
    for n, (_, d, blk) in enumerate(shared):
        copies.append(pltpu.make_async_copy(slabs_hbm.at[d, :, pl.ds(LANES * blk, LANES)], edge_vm.at[2 * n], sem.at[1 + N_DEV + 2 * n]))
        copies.append(pltpu.make_async_copy(slabs_hbm.at[d + 1, :, pl.ds(0, LANES)], edge_vm.at[2 * n + 1], sem.at[2 + N_DEV + 2 * n]))
    for cp in copies:
        cp.start()
    for cp in copies:
        cp.wait()
    for n, (blk, _, _) in enumerate(shared):
        w_vm[:, LANES * blk:LANES * (blk + 1)] = edge_vm[2 * n] + edge_vm[2 * n + 1]


def _proj_specs(names, n_units, where):
    specs = []
    for name in names:
        s = SEG_NAMES.index(name)
        nblk = SEG_W[s] // n_units // LANES
        base = SEG_OFF[s] // LANES
        assert base % nblk == 0
        specs.append(pl.BlockSpec((nblk, TILE, LANES), lambda *g, base=base, nblk=nblk: (base // nblk + where(*g)[0], where(*g)[1], 0)))
    return specs


def _cols(ref):
    return ref[0] if ref.shape[0] == 1 else jnp.concatenate([ref[j] for j in range(ref.shape[0])], axis=1)


def _prenorm(head, x, g_norm, w_glr):
    t_rows = x.shape[0] + TILE
    nt = t_rows // TILE

    def body(head_ref, x_ref, g_ref, wg_ref, u_ref, ut_ref, glr_ref):
        x = _tile_rows(head_ref, x_ref)
        r = lax.rsqrt(_row_mean(x * x) + EPS)
        u32 = (x * r * g_ref[...]).astype(BF16).astype(F32)
        u = u32.astype(BF16)
        u_ref[...] = u
        ut_ref[...] = u32.T.astype(BF16)
        glr_ref[...] = _mm(u, wg_ref[...])

    row = lambda w: pl.BlockSpec((TILE, w), lambda i: (i, 0))
    return _call(
        body, "prenorm", grid=(nt,),
        out_shape=[jax.ShapeDtypeStruct((t_rows, D_MODEL), BF16), jax.ShapeDtypeStruct((nt, D_MODEL, TILE), BF16),
                   jax.ShapeDtypeStruct((t_rows, LANES), F32)],
        in_specs=[_head_spec(), _x_spec(), pl.BlockSpec((1, D_MODEL), lambda i: (0, 0)), pl.BlockSpec((D_MODEL, LANES), lambda i: (0, 0))],
        out_specs=[row(D_MODEL), pl.BlockSpec((None, D_MODEL, TILE), lambda i: (i, 0, 0)), row(LANES)],
        compiler_params=_params(("arbitrary",)),
    )(head, x, g_norm, w_glr)


SLAB_INNER = 9


def _edge_blocks():
    inner = {SLAB_BLK0[d] + j for d in range(N_DEV) for j in range(1, 1 + SLAB_INNER)}
    edges = []
    for blk in range(AL_COLS // LANES):
        if blk not in inner:
            srcs = [(d, blk - SLAB_BLK0[d]) for d in range(N_DEV)
                    if SLAB_BLK0[d] <= blk < SLAB_BLK0[d] + SLAB_BLOCKS and SLAB_BOUND[d] < LANES * (blk + 1) and LANES * blk < SLAB_BOUND[d + 1]]
            edges.append((blk, srcs))
    return edges


def _inproj_fwd(u, slab_local):
    t_rows = u.shape[0]
    nt = t_rows // TILE
    rc = (3 if nt % 3 == 0 else 1) * TILE
    n_chunks = t_rows // rc
    edges = _edge_blocks()
    ne = len(edges)
    runs = []
    for k, (blk, _) in enumerate(edges):
        if runs and edges[runs[-1][0] + runs[-1][1] - 1][0] + 1 == blk:
            runs[-1] = (runs[-1][0], runs[-1][1] + 1)
        else:
            runs.append((k, 1))
    n_stage = sum(len(srcs) for _, srcs in edges)

    def body(u_hbm, slab_hbm, proj_hbm, slabs_hbm, u_vm, wbuf, obuf, ebuf, stage, ebuf_out, sem_u, sem_w, sem_o, sem_s, sem_eo,
             send_sems, recv_sems, sem_l):
        x, y, c = _position()
        me, sibling = (x, y, c), (x, y, 1 - c)
        chips = [(1 - x, y), (x, 1 - y), (1 - x, 1 - y)]

        def slab_copy(k, block, to, src=None):
            dst = slabs_hbm.at[_index(*block)]
            return pltpu.make_async_remote_copy(src_ref=dst if src is None else src, dst_ref=dst, send_sem=send_sems.at[k],
                                                recv_sem=recv_sems.at[k], device_id=to, device_id_type=MESH)

        mine = pltpu.make_async_copy(slab_hbm, slabs_hbm.at[_index(*me)], sem_l)
        mine.start()
        first = [slab_copy(0, me, sibling, src=slab_hbm)] + [slab_copy(1 + j, me, (*chip, c), src=slab_hbm) for j, chip in enumerate(chips)]
        for cp in first[:2]:
            cp.start()
        load_u = pltpu.make_async_copy(u_hbm, u_vm, sem_u)
        load_u.start()
        load_u.wait()

        def store(slot, block0, rows0):
            return pltpu.make_async_copy(obuf.at[slot], proj_hbm.at[pl.ds(block0, SLAB_INNER), pl.ds(rows0, rc)], sem_o.at[slot])

        def multiply(dev):
            load_w = pltpu.make_async_copy(slabs_hbm.at[dev, :, pl.ds(LANES, SLAB_INNER * LANES)], wbuf, sem_w)
            load_w.start()
            load_w.wait()
            block0 = _slab_block0(dev) + 1

            def chunk(r, carry):
                slot = lax.rem(r, 2)
                rows0 = pl.multiple_of(r * rc, rc)

                @pl.when(r >= 2)
                def _():
                    store(slot, block0, rows0).wait()

                res = _mm(u_vm[pl.ds(rows0, rc), :], wbuf[...])
                for j in range(SLAB_INNER):
                    obuf[slot, j] = res[:, j * LANES:(j + 1) * LANES].astype(BF16)
                store(slot, block0, rows0).start()
                return carry

            lax.fori_loop(0, n_chunks, chunk, 0)
            for r in range(max(0, n_chunks - 2), n_chunks):
                store(r % 2, block0, r * rc).wait()

        mine.wait()
        multiply(_index(*me))
        slab_copy(0, sibling, me).wait_recv()
        multiply(_index(*sibling))
        passed = []
        for j, chip in enumerate(chips):
            slab_copy(1 + j, (*chip, c), me).wait_recv()
            passed.append(slab_copy(4 + j, (*chip, c), sibling))
            passed[-1].start()
            if j + 1 < len(chips):
                first[1 + j].wait_send()
                first[2 + j].start()
            multiply(_index(*chip, c))
            slab_copy(4 + j, (*chip, 1 - c), me).wait_recv()
            multiply(_index(*chip, 1 - c))

        loads, n = [], 0
        for k, (_, srcs) in enumerate(edges):
            for d, j in srcs:
                dst = ebuf.at[:, pl.ds(k * LANES, LANES)] if len(srcs) == 1 else stage.at[n]
                loads.append(pltpu.make_async_copy(slabs_hbm.at[d, :, pl.ds(j * LANES, LANES)], dst, sem_s.at[n]))
                n += 1
        for cp in loads:
            cp.start()
        for cp in loads:
            cp.wait()
        n = 0
        for k, (_, srcs) in enumerate(edges):
            if len(srcs) == 2:
                ebuf[:, k * LANES:(k + 1) * LANES] = stage[n] + stage[n + 1]
            n += len(srcs)

        def edge_stores(slot, rows0):
            return [pltpu.make_async_copy(ebuf_out.at[slot, pl.ds(k0, length)],
                                          proj_hbm.at[pl.ds(edges[k0][0], length), pl.ds(rows0, rc)], sem_eo.at[slot, i])
                    for i, (k0, length) in enumerate(runs)]

        def edge_chunk(r, carry):
            slot = lax.rem(r, 2)
            rows0 = pl.multiple_of(r * rc, rc)

            @pl.when(r >= 2)
            def _():
                for cp in edge_stores(slot, rows0):
                    cp.wait()

            res = _mm(u_vm[pl.ds(rows0, rc), :], ebuf[...])
            for k in range(ne):
                ebuf_out[slot, k] = res[:, k * LANES:(k + 1) * LANES].astype(BF16)
            for cp in edge_stores(slot, rows0):
                cp.start()
            return carry

        lax.fori_loop(0, n_chunks, edge_chunk, 0)
        for r in range(max(0, n_chunks - 2), n_chunks):
            for cp in edge_stores(r % 2, r * rc):
                cp.wait()

        for cp in [first[0], first[3]] + passed:
            cp.wait_send()

    return _call(
        body, "inproj_fwd",
        out_shape=[jax.ShapeDtypeStruct((AL_COLS // LANES, t_rows, LANES), BF16), jax.ShapeDtypeStruct((N_DEV, D_MODEL, SLAB_W), BF16)],
        in_specs=[ANY] * 2, out_specs=[ANY] * 2,
        scratch_shapes=[pltpu.VMEM((t_rows, D_MODEL), BF16), pltpu.VMEM((D_MODEL, SLAB_INNER * LANES), BF16),
                        pltpu.VMEM((2, SLAB_INNER, rc, LANES), BF16), pltpu.VMEM((D_MODEL, ne * LANES), BF16),
                        pltpu.VMEM((n_stage, D_MODEL, LANES), BF16), pltpu.VMEM((2, ne, rc, LANES), BF16),
                        pltpu.SemaphoreType.DMA, pltpu.SemaphoreType.DMA, pltpu.SemaphoreType.DMA((2,)),
                        pltpu.SemaphoreType.DMA((n_stage,)), pltpu.SemaphoreType.DMA((2, len(runs))),
                        pltpu.SemaphoreType.DMA((7,)), pltpu.SemaphoreType.DMA((7,)), pltpu.SemaphoreType.DMA],
        compiler_params=_params(),
    )(u, slab_local)


def _ret_decay(lgh):
    i = lax.broadcasted_iota(jnp.int32, (TILE, TILE), 0)
    j = lax.broadcasted_iota(jnp.int32, (TILE, TILE), 1)
    rel = (i - j).astype(F32)
    return jnp.where(rel >= 0, jnp.exp(jnp.maximum(rel, 0.0) * lgh), 0.0)


def _ret_vectors(lgh):
    idx = lax.broadcasted_iota(jnp.int32, (TILE, 1), 0).astype(F32)
    xi = jnp.exp((idx + 1.0) * lgh)
    zeta = jnp.exp((TILE - 1.0 - idx) * lgh)
    gc = jnp.exp(jnp.full((1, 1), float(TILE), F32) * lgh)
    return xi, zeta, gc


def _ret_fwd(proj, cos, sin, gain, lg, row_shards):
    t_rows = cos.shape[0]
    nt = t_rows // TILE
    ns = len(row_shards)

    def body(lg_ref, q_ref, k_ref, v_ref, g_ref, cos_ref, sin_ref, gain_ref, *rest):
        shard_refs, (oraw_ref, oret_ref, st_ref), gathered = rest[:ns], rest[ns:ns + 3], rest[ns + 3:2 * ns + 3]
        s_acc, dm = rest[2 * ns + 3:2 * ns + 5]
        gather = _Exchange(shard_refs, gathered, rest[2 * ns + 5:], among_chips=False)
        h, t = pl.program_id(0), pl.program_id(1)
        lgh = lg_ref[h]

        @pl.when((h == 0) & (t == 0))
        def _():
            gather.start()

        @pl.when((h == RET_HEADS - 1) & (t == nt - 1))
        def _():
            gather.finish()

        @pl.when(t == 0)
        def _():
            s_acc[...] = jnp.zeros_like(s_acc)
            dm[...] = _ret_decay(lgh)

        cos_t, sin_t = cos_ref[...], sin_ref[...]
        q = _rope(_cols(q_ref).astype(F32), cos_t, sin_t)
        k = _rope(_cols(k_ref).astype(F32), cos_t, sin_t) * (RET_QK ** -0.5)
        xi, zeta, gc = _ret_vectors(lgh)
        v = _cols(v_ref)
        s_in = s_acc[...]
        p = (_mm_nt(q.astype(BF16), k.astype(BF16)) * dm[...]).astype(BF16)
        o = _mm(p, v) + _mm((q * xi).astype(BF16), s_in.astype(BF16))
        st_ref[...] = s_in.astype(BF16)
        s_acc[...] = s_in * gc + _mm_tn((k * zeta).astype(BF16), v)
        oraw_ref[...] = o
        oc = o - _row_mean(o)
        n = oc * lax.rsqrt(_row_mean(oc * oc) + EPS) * gain_ref[...]
        g = _cols(g_ref).astype(F32)
        oret_ref[...] = (n * g * _sigmoid(g)).astype(BF16)

    blk = lambda w: pl.BlockSpec((TILE, w), lambda h, t: (t, h))
    tab = pl.BlockSpec((TILE, LANES), lambda h, t: (t, 0))
    outs = _call(
        body, "ret_fwd", grid=(RET_HEADS, nt),
        out_shape=[jax.ShapeDtypeStruct((t_rows, RET_W), F32), jax.ShapeDtypeStruct((t_rows, RET_W), BF16),
                   jax.ShapeDtypeStruct((RET_HEADS, nt, RET_QK, RET_V), BF16)]
                  + [jax.ShapeDtypeStruct((N_DEV, *a.shape), a.dtype) for a in row_shards],
        in_specs=[pl.BlockSpec(memory_space=pltpu.SMEM)] + _proj_specs(("rq", "rk", "rv", "rg"), RET_HEADS, lambda h, t: (h, t)) + [tab, tab,
                  pl.BlockSpec((1, RET_V), lambda h, t: (0, h))] + [ANY] * ns,
        out_specs=[blk(RET_V), blk(RET_V), pl.BlockSpec((None, None, RET_QK, RET_V), lambda h, t: (h, t, 0, 0))] + [ANY] * ns,
        scratch_shapes=[pltpu.VMEM((RET_QK, RET_V), F32), pltpu.VMEM((TILE, TILE), F32)] + _exchange_sems(ns, N_DEV),
        compiler_params=_params(("arbitrary", "arbitrary")),
    )(lg, proj, proj, proj, proj, cos, sin, gain, *row_shards)
    return outs[0], outs[1], outs[2], outs[3:]


def _ret_bwd(proj, cos, sin, gain, lg, o_raw, do_ret, states):
    t_rows = cos.shape[0]
    nt = t_rows // TILE

    def body(lg_ref, q_ref, k_ref, v_ref, g_ref, cos_ref, sin_ref, gain_ref, oraw_ref, do_ref, st_ref,
             dq_ref, dk_ref, dv_ref, dg_ref, dgain_ref, e_acc, dm):
        h, j = pl.program_id(0), pl.program_id(1)
        lgh = lg_ref[h]

        @pl.when(j == 0)
        def _():
            e_acc[...] = jnp.zeros_like(e_acc)
            dm[...] = _ret_decay(lgh)
            dgain_ref[...] = jnp.zeros_like(dgain_ref)

        cos_t, sin_t = cos_ref[...], sin_ref[...]
        q = _rope(_cols(q_ref).astype(F32), cos_t, sin_t)
        k = _rope(_cols(k_ref).astype(F32), cos_t, sin_t) * (RET_QK ** -0.5)
        xi, zeta, gc = _ret_vectors(lgh)
        v = _cols(v_ref)
        g = _cols(g_ref).astype(F32)
        o = oraw_ref[...]
        do = do_ref[...].astype(F32)
        oc = o - _row_mean(o)
        rstd = lax.rsqrt(_row_mean(oc * oc) + EPS)
        xh = oc * rstd
        gain_t = gain_ref[...]
        sg = _sigmoid(g)
        dn = do * (g * sg)
        dg_ref[...] = (do * (xh * gain_t) * (sg * (1.0 + g * (1.0 - sg)))).astype(BF16)
        dgain_ref[...] += _col_sum(dn * xh)
        dxh = dn * gain_t
        dob = (rstd * (dxh - _row_mean(dxh) - xh * _row_mean(dxh * xh))).astype(BF16)
        dmat = dm[...]
        qb, kb = q.astype(BF16), k.astype(BF16)
        p = (_mm_nt(qb, kb) * dmat).astype(BF16)
        dp = (_mm_nt(dob, v) * dmat).astype(BF16)
        s_in = st_ref[...]
        e_in = e_acc[...]
        e_b = e_in.astype(BF16)
        dq = _mm(dp, kb) + _mm_nt(dob, s_in) * xi
        dk = _mm_tn(dp, qb) + _mm_nt(v, e_b) * zeta
        dv_ref[...] = (_mm_tn(p, dob) + _mm((k * zeta).astype(BF16), e_b)).astype(BF16)
        e_acc[...] = e_in * gc + _mm_tn((q * xi).astype(BF16), dob)
        dq_ref[...] = _rope_bwd(dq, cos_t, sin_t).astype(BF16)
        dk_ref[...] = (_rope_bwd(dk, cos_t, sin_t) * (RET_QK ** -0.5)).astype(BF16)

    blk = lambda w: pl.BlockSpec((TILE, w), lambda h, j: (nt - 1 - j, h))
    tab = pl.BlockSpec((TILE, LANES), lambda h, j: (nt - 1 - j, 0))
    vec = pl.BlockSpec((1, RET_V), lambda h, j: (0, h))
    return _call(
        body, "ret_bwd", grid=(RET_HEADS, nt),
        out_shape=[jax.ShapeDtypeStruct((t_rows, RET_HEADS * RET_QK), BF16), jax.ShapeDtypeStruct((t_rows, RET_HEADS * RET_QK), BF16),
                   jax.ShapeDtypeStruct((t_rows, RET_W), BF16), jax.ShapeDtypeStruct((t_rows, RET_W), BF16),
                   jax.ShapeDtypeStruct((1, RET_W), F32)],
        in_specs=[pl.BlockSpec(memory_space=pltpu.SMEM)] + _proj_specs(("rq", "rk", "rv", "rg"), RET_HEADS, lambda h, j: (h, nt - 1 - j)) + [tab, tab, vec,
                  blk(RET_V), blk(RET_V), pl.BlockSpec((None, None, RET_QK, RET_V), lambda h, j: (h, nt - 1 - j, 0, 0))],
        out_specs=[blk(RET_QK), blk(RET_QK), blk(RET_V), blk(RET_V), vec],
        scratch_shapes=[pltpu.VMEM((RET_QK, RET_V), F32), pltpu.VMEM((TILE, TILE), F32)],
        compiler_params=_params(("arbitrary", "arbitrary")),
    )(lg, proj, proj, proj, proj, cos, sin, gain, o_raw, do_ret, states)


GLA_LEVELS = (32, 64, 128, 256)
N_TERMS = 1 + len(GLA_LEVELS)


def _gla_tables():
    p = jnp.arange(TILE)[:, None]
    r = jnp.arange(TILE)[None, :]
    masks = [(p // GLA_CHUNK == r // GLA_CHUNK) & (r <= p)]
    for blk in GLA_LEVELS:
        masks.append((p // blk == r // blk) & (p % blk >= blk // 2) & (r % blk < blk // 2))
    masks = jnp.stack(masks + [m.T for m in masks]).astype(F32)
    cum_fwd = jnp.concatenate([r <= p, masks[0] > 0], axis=0).astype(BF16)
    cum_bwd = jnp.concatenate([r >= p, masks[N_TERMS] > 0], axis=1).astype(BF16)
    return masks, cum_fwd, cum_bwd


def _split3(x):
    hi = x.astype(BF16)
    rest = x - hi.astype(F32)
    mid = rest.astype(BF16)
    lo = (rest - mid.astype(F32)).astype(BF16)
    return jnp.concatenate([hi, mid, lo], axis=1)


def _join3(y):
    w = y.shape[1] // 3
    return (y[:, 2 * w:] + y[:, w:2 * w]) + y[:, :w]


def _gla_prep(q_ref, k_ref, glr_ref, wgu_ref, b_ref, cum_ref, g_scr, ref_scr):
    z = _mm(glr_ref[...].astype(BF16), wgu_ref[...].astype(BF16)) + b_ref[...]
    la = (jnp.minimum(z, 0.0) - jnp.log(1.0 + jnp.exp(-jnp.abs(z)))) / GLA_TAU
    gb = _join3(_mm(cum_ref[...], _split3(la)))
    g, b = gb[:TILE], gb[TILE:]
    g_scr[...] = g
    factors = [(jnp.exp(b), jnp.exp(-b))]
    for lvl, blk in enumerate(GLA_LEVELS):
        for n in range(TILE // blk):
            ref_scr[lvl, n * blk:(n + 1) * blk, :] = jnp.broadcast_to(g_scr[pl.ds(n * blk + blk // 2 - 1, 1), :], (blk, GLA_K))
        x = g - ref_scr[lvl]
        factors.append((jnp.exp(jnp.minimum(x, 0.0)), jnp.exp(jnp.minimum(-x, 0.0))))
    g_last = g_scr[pl.ds(TILE - 1, 1), :]
    q = _cols(q_ref).astype(F32) * (GLA_K ** -0.5)
    k = _cols(k_ref).astype(F32)
    return z, q, k, factors, jnp.exp(g), jnp.exp(g_last), jnp.exp(g_last - g)


def _gla_scores(q, k, factors, m_ref):
    a = jnp.zeros((TILE, TILE), F32)
    for l, (fq, fk) in enumerate(factors):
        s = _mm_nt((q * fq).astype(BF16), (k * fk).astype(BF16))
        a = jnp.where(m_ref[l] > 0.0, s, a)
    return a


def _gla_fwd(proj, glr, wgu_pad, b_gate, gain, masks, cum_fwd):
    t_rows = glr.shape[0]
    nt = t_rows // TILE

    def body(q_ref, k_ref, v_ref, g_ref, glr_ref, wgu_ref, b_ref, gain_ref, m_ref, cum_ref, oraw_ref, ogla_ref, st_ref,
             s_acc, g_scr, ref_scr):
        @pl.when(pl.program_id(1) == 0)
        def _():
            s_acc[...] = jnp.zeros_like(s_acc)

        _, q, k, factors, e_g, e_last, e_end = _gla_prep(q_ref, k_ref, glr_ref, wgu_ref, b_ref, cum_ref, g_scr, ref_scr)
        v = _cols(v_ref)
        st = s_acc[...]
        st_ref[...] = st
        a = _gla_scores(q, k, factors, m_ref)
        o = _mm(a.astype(BF16), v) + _mm_nt((q * e_g).astype(BF16), st.astype(BF16))
        s_acc[...] = st * e_last + _mm(v.astype(F32).T.astype(BF16), (k * e_end).astype(BF16))
        oraw_ref[...] = o
        n = o * lax.rsqrt(_row_mean(o * o) + EPS) * gain_ref[...]
        g = _cols(g_ref).astype(F32)
        ogla_ref[...] = (n * g * _sigmoid(g)).astype(BF16)

    blk = lambda w: pl.BlockSpec((TILE, w), lambda h, t: (t, h))
    return _call(
        body, "gla_fwd", grid=(GLA_HEADS, nt),
        out_shape=[jax.ShapeDtypeStruct((t_rows, GLA_W), F32), jax.ShapeDtypeStruct((t_rows, GLA_W), BF16),
                   jax.ShapeDtypeStruct((GLA_HEADS, nt, GLA_V, GLA_K), F32)],
        in_specs=_proj_specs(("gq", "gk", "gv", "gg"), GLA_HEADS, lambda h, t: (h, t)) + [pl.BlockSpec((TILE, LANES), lambda h, t: (t, 0)),
                  pl.BlockSpec((LANES, GLA_K), lambda h, t: (0, h)), pl.BlockSpec((1, GLA_K), lambda h, t: (0, h)),
                  pl.BlockSpec((1, GLA_V), lambda h, t: (0, h)),
                  pl.BlockSpec((N_TERMS, TILE, TILE), lambda h, t: (0, 0, 0)), pl.BlockSpec((2 * TILE, TILE), lambda h, t: (0, 0))],
        out_specs=[blk(GLA_V), blk(GLA_V), pl.BlockSpec((None, None, GLA_V, GLA_K), lambda h, t: (h, t, 0, 0))],
        scratch_shapes=[pltpu.VMEM((GLA_V, GLA_K), F32), pltpu.VMEM((TILE, GLA_K), F32),
                        pltpu.VMEM((len(GLA_LEVELS), TILE, GLA_K), F32)],
        compiler_params=_params(("arbitrary", "arbitrary")),
    )(proj, proj, proj, proj, glr, wgu_pad, b_gate, gain, masks, cum_fwd)


def _gla_bwd(proj, glr, wgu_pad, b_gate, gain, o_raw, do_gla, states, masks, cum_fwd, cum_bwd):
    t_rows = glr.shape[0]
    nt = t_rows // TILE

    def body(q_ref, k_ref, v_ref, g_ref, glr_ref, wgu_ref, b_ref, gain_ref, m_ref, cum_ref, cumb_ref, oraw_ref, do_ref, st_ref,
             dq_ref, dk_ref, dv_ref, dg_ref, dglr_ref, dwgu_ref, dbg_ref, dgain_ref, d_acc, g_scr, ref_scr, dref_scr):
        @pl.when(pl.program_id(1) == 0)
        def _():
            d_acc[...] = jnp.zeros_like(d_acc)
            dwgu_ref[...] = jnp.zeros_like(dwgu_ref)
            dbg_ref[...] = jnp.zeros_like(dbg_ref)
            dgain_ref[...] = jnp.zeros_like(dgain_ref)

        z, q, k, factors, e_g, e_last, e_end = _gla_prep(q_ref, k_ref, glr_ref, wgu_ref, b_ref, cum_ref, g_scr, ref_scr)
        v = _cols(v_ref)
        o = oraw_ref[...]
        do = do_ref[...].astype(F32)
        g = _cols(g_ref).astype(F32)
        rinv = lax.rsqrt(_row_mean(o * o) + EPS)
        nh = o * rinv
        gain_t = gain_ref[...]
        sg = _sigmoid(g)
        dn = do * (g * sg)
        dg_ref[...] = (do * (nh * gain_t) * (sg * (1.0 + g * (1.0 - sg)))).astype(BF16)
        dgain_ref[...] += _col_sum(dn * nh)
        dnh = dn * gain_t
        dor = rinv * (dnh - nh * _row_mean(dnh * nh))
        dob = dor.astype(BF16)
        a_t = _gla_scores(q, k, factors, m_ref).T.astype(BF16)
        da = _mm_nt(dob, v)
        da_t = _mm_nt(v, dob)
        st_in = st_ref[...]
        d_out = d_acc[...]
        d_out_b = d_out.astype(BF16)
        qg, kg = q * e_g, k * e_end
        dqg = _mm(dob, st_in.astype(BF16))
        dkg = _mm(v, d_out_b)
        dv_ref[...] = (_mm(a_t, dob) + _mm_nt(kg.astype(BF16), d_out_b)).astype(BF16)
        d_acc[...] = d_out * e_last + _mm(dor.T.astype(BF16), qg.astype(BF16))
        dq = dqg * e_g
        dk = dkg * e_end
        dkg_kg = dkg * kg
        dg_cum = dqg * qg - dkg_kg
        db = None
        for l, (fq, fk) in enumerate(factors):
            qt, kt = q * fq, k * fk
            dqt = _mm(jnp.where(m_ref[l] > 0.0, da, 0.0).astype(BF16), kt.astype(BF16))
            dkt = _mm(jnp.where(m_ref[N_TERMS + l] > 0.0, da_t, 0.0).astype(BF16), qt.astype(BF16))
            dq = dq + dqt * fq
            dk = dk + dkt * fk
            diff = dqt * qt - dkt * kt
            if l == 0:
                db = diff
            else:
                dg_cum = dg_cum + diff
                dref_scr[l - 1] = diff
        dq_ref[...] = (dq * (GLA_K ** -0.5)).astype(BF16)
        dk_ref[...] = dk.astype(BF16)
        g_scr[...] = dg_cum
        g_scr[pl.ds(TILE - 1, 1), :] += e_last * _col_sum(d_out * st_in) + _col_sum(dkg_kg)
        for lvl, blk in enumerate(GLA_LEVELS):
            for n in range(TILE // blk):
                g_scr[pl.ds(n * blk + blk // 2 - 1, 1), :] -= _col_sum(dref_scr[lvl, n * blk:(n + 1) * blk, :])
        dla = _join3(_mm(cumb_ref[...], jnp.concatenate([_split3(g_scr[...]), _split3(db)], axis=0)))
        dz = dla * (1.0 / GLA_TAU) * _sigmoid(-z)
        dzb = dz.astype(BF16)
        dglr_ref[...] = _mm_nt(dzb, wgu_ref[...].astype(BF16)).astype(BF16)
        dwgu_ref[...] += _mm(glr_ref[...].T.astype(BF16), dzb)
        dbg_ref[...] += _col_sum(dz)

    blk = lambda w: pl.BlockSpec((TILE, w), lambda h, j: (nt - 1 - j, h))
    vec = lambda w: pl.BlockSpec((1, w), lambda h, j: (0, h))
    wspec = pl.BlockSpec((LANES, GLA_K), lambda h, j: (0, h))
    return _call(
        body, "gla_bwd", grid=(GLA_HEADS, nt),
        out_shape=[jax.ShapeDtypeStruct((t_rows, GLA_HEADS * GLA_K), BF16), jax.ShapeDtypeStruct((t_rows, GLA_HEADS * GLA_K), BF16),
                   jax.ShapeDtypeStruct((t_rows, GLA_W), BF16), jax.ShapeDtypeStruct((t_rows, GLA_W), BF16),
                   jax.ShapeDtypeStruct((GLA_HEADS, t_rows, LANES), BF16), jax.ShapeDtypeStruct((LANES, GLA_HEADS * GLA_K), F32),
                   jax.ShapeDtypeStruct((1, GLA_HEADS * GLA_K), F32), jax.ShapeDtypeStruct((1, GLA_W), F32)],
        in_specs=_proj_specs(("gq", "gk", "gv", "gg"), GLA_HEADS, lambda h, j: (h, nt - 1 - j)) + [pl.BlockSpec((TILE, LANES), lambda h, j: (nt - 1 - j, 0)),
                  wspec, vec(GLA_K), vec(GLA_V),
                  pl.BlockSpec((2 * N_TERMS, TILE, TILE), lambda h, j: (0, 0, 0)), pl.BlockSpec((2 * TILE, TILE), lambda h, j: (0, 0)),
                  pl.BlockSpec((TILE, 2 * TILE), lambda h, j: (0, 0)), blk(GLA_V), blk(GLA_V),
                  pl.BlockSpec((None, None, GLA_V, GLA_K), lambda h, j: (h, nt - 1 - j, 0, 0))],
        out_specs=[blk(GLA_K), blk(GLA_K), blk(GLA_V), blk(GLA_V),
                   pl.BlockSpec((None, TILE, LANES), lambda h, j: (h, nt - 1 - j, 0)), wspec, vec(GLA_K), vec(GLA_V)],
        scratch_shapes=[pltpu.VMEM((GLA_V, GLA_K), F32), pltpu.VMEM((TILE, GLA_K), F32),
                        pltpu.VMEM((len(GLA_LEVELS), TILE, GLA_K), F32), pltpu.VMEM((len(GLA_LEVELS), TILE, GLA_K), F32)],
        compiler_params=_params(("arbitrary", "arbitrary")),
    )(proj, proj, proj, proj, glr, wgu_pad, b_gate, gain, masks, cum_fwd, cum_bwd, o_raw, do_gla, states)


def _merge_fwd_bwd(o_ret, o_gla, proj, x, target, g_final, w_br, w_bg, w_out):
    t_rows = x.shape[0] + TILE
    nt = t_rows // TILE

    def body(oret_ref, ogla_ref, mr_ref, mg_ref, h0_ref, tgt_ref, gf_ref, wbr_hbm, wbg_hbm, wout_hbm,
             dh1_ref, dmr_ref, dmg_ref, doret_ref, dogla_ref, loss_ref, dgf_ref, dwbr_hbm, dwbg_hbm, dwout_hbm,
             wbr, wbg, wout, abr, abg, aout, sem):
        i = pl.program_id(0)

        @pl.when(i == 0)
        def _():
            cps = [pltpu.make_async_copy(s, d, sem.at[n]) for n, (s, d) in enumerate(((wbr_hbm, wbr), (wbg_hbm, wbg), (wout_hbm, wout)))]
            for cp in cps:
                cp.start()
            abr[...] = jnp.zeros_like(abr)
            abg[...] = jnp.zeros_like(abg)
            aout[...] = jnp.zeros_like(aout)
            loss_ref[...] = jnp.zeros_like(loss_ref)
            dgf_ref[...] = jnp.zeros_like(dgf_ref)
            for cp in cps:
                cp.wait()
            dh1_ref[...] = jnp.zeros_like(dh1_ref)
            dmr_ref[...] = jnp.zeros_like(dmr_ref)
            dmg_ref[...] = jnp.zeros_like(dmg_ref)
            doret_ref[...] = jnp.zeros_like(doret_ref)
            dogla_ref[...] = jnp.zeros_like(dogla_ref)

        @pl.when(i > 0)
        def _():
            oret, ogla = oret_ref[...], ogla_ref[...]
            br, bg = _mm(oret, wbr[...]), _mm(ogla, wbg[...])
            sr, sg = _sigmoid(_cols(mr_ref).astype(F32)), _sigmoid(_cols(mg_ref).astype(F32))
            mb = (sr * br + sg * bg).astype(BF16)
            h1 = h0_ref[...] + _mm(mb, wout[...])
            r2 = lax.rsqrt(_row_mean(h1 * h1) + EPS)
            hn = h1 * r2
            gf = gf_ref[...]
            diff = hn * gf - tgt_ref[...]
            loss_ref[...] += 0.5 * jnp.sum(_row_mean(diff * diff))
            dy = diff * (1.0 / D_MODEL)
            dgf_ref[...] += _col_sum(dy * hn)
            dyg = dy * gf
            dh1 = r2 * (dyg - hn * _row_mean(dyg * hn))
            dh1_ref[...] = dh1
            dh1b = dh1.astype(BF16)
            dm = _mm_nt(dh1b, wout[...])
            aout[...] += _mm_tn(mb, dh1b)
            dbr = (dm * sr).astype(BF16)
            dbg = (dm * sg).astype(BF16)
            dmr_ref[...] = (dm * br * sr * (1.0 - sr)).astype(BF16)
            dmg_ref[...] = (dm * bg * sg * (1.0 - sg)).astype(BF16)
            doret_ref[...] = _mm_nt(dbr, wbr[...]).astype(BF16)
            dogla_ref[...] = _mm_nt(dbg, wbg[...]).astype(BF16)
            abr[...] += _mm_tn(oret, dbr)
            abg[...] += _mm_tn(ogla, dbg)

        @pl.when(i == nt - 1)
        def _():
            wbr[...] = abr[...].astype(BF16)
            wbg[...] = abg[...].astype(BF16)
            wout[...] = aout[...].astype(BF16)
            pltpu.sync_copy(wbr, dwbr_hbm)
            pltpu.sync_copy(wbg, dwbg_hbm)
            pltpu.sync_copy(wout, dwout_hbm)

    row = lambda w: pl.BlockSpec((TILE, w), lambda i: (i, 0))
    one = lambda w: pl.BlockSpec((1, w), lambda i: (0, 0))
    return _call(
        body, "merge_fwd_bwd", grid=(nt,),
        out_shape=[jax.ShapeDtypeStruct((t_rows, D_MODEL), F32), jax.ShapeDtypeStruct((t_rows, D_MODEL), BF16),
                   jax.ShapeDtypeStruct((t_rows, D_MODEL), BF16), jax.ShapeDtypeStruct((t_rows, RET_W), BF16),
                   jax.ShapeDtypeStruct((t_rows, GLA_W), BF16), jax.ShapeDtypeStruct((1, LANES), F32),
                   jax.ShapeDtypeStruct((1, D_MODEL), F32), jax.ShapeDtypeStruct((RET_W, D_MODEL), BF16),
                   jax.ShapeDtypeStruct((GLA_W, D_MODEL), BF16), jax.ShapeDtypeStruct((D_MODEL, D_MODEL), BF16)],
        in_specs=[row(RET_W), row(GLA_W)] + _proj_specs(("mr", "mg"), 1, lambda i: (0, i)) + [_x_spec(), _x_spec(), one(D_MODEL), ANY, ANY, ANY],
        out_specs=[row(D_MODEL), row(D_MODEL), row(D_MODEL), row(RET_W), row(GLA_W), one(LANES), one(D_MODEL), ANY, ANY, ANY],
        scratch_shapes=[pltpu.VMEM((RET_W, D_MODEL), BF16), pltpu.VMEM((GLA_W, D_MODEL), BF16), pltpu.VMEM((D_MODEL, D_MODEL), BF16),
                        pltpu.VMEM((RET_W, D_MODEL), F32), pltpu.VMEM((GLA_W, D_MODEL), F32), pltpu.VMEM((D_MODEL, D_MODEL), F32),
                        pltpu.SemaphoreType.DMA((3,))],
        compiler_params=_params(("arbitrary",)),
    )(o_ret, o_gla, proj, proj, x, target, g_final, w_br, w_bg, w_out)


def _inproj_bwd_x(dseg, dglr, head, x, dh1, g_norm, slabs, w_glr, chip_partials):
    t_rows = x.shape[0] + TILE
    nt = t_rows // TILE
    ne = len(chip_partials)

    def body(*refs):
        d_refs = refs[:10]
        dglr_ref, head_ref, x_ref, dh1_ref, g_ref, slabs_hbm, wg_hbm = refs[10:17]
        part_refs = refs[17:17 + ne]
        dx_ref, dhead_ref, dgn_ref = refs[17 + ne:20 + ne]
        landed = refs[20 + ne:20 + 2 * ne]
        w_vm, wg_vm, edge_vm, sem = refs[20 + 2 * ne:24 + 2 * ne]
        exchange = _Exchange(part_refs, landed, refs[24 + 2 * ne:], among_chips=True)

        @pl.when(pl.program_id(0) == 0)
        def _():
            exchange.start()
            dgn_ref[...] = jnp.zeros_like(dgn_ref)
            _load_weight(slabs_hbm, wg_hbm, w_vm, wg_vm, edge_vm, sem)

        @pl.when(pl.program_id(0) == nt - 1)
        def _():
            exchange.finish()

        dglr = dglr_ref[0].astype(F32)
        for h in range(1, GLA_HEADS):
            dglr = dglr + dglr_ref[h].astype(F32)
        du = _mm_nt(dglr.astype(BF16), wg_vm[...])
        for s, d_ref in enumerate(d_refs):
            du = du + _mm_nt(d_ref[...], w_vm[:, SEG_OFF[s]:SEG_OFF[s] + SEG_W[s]])
        x = _tile_rows(head_ref, x_ref)
        r = lax.rsqrt(_row_mean(x * x) + EPS)
        hn = x * r
        dgn_ref[...] += _col_sum(du * hn)
        dug = du * g_ref[...]
        dh0 = dh1_ref[...] + r * (dug - hn * _row_mean(dug * hn))
        dx_ref[...] = dh0

        @pl.when(pl.program_id(0) == 0)
        def _():
            dhead_ref[...] = dh0

    row = lambda w: pl.BlockSpec((TILE, w), lambda i: (i, 0))
    one = pl.BlockSpec((1, D_MODEL), lambda i: (0, 0))
    return _call(
        body, "inproj_bwd_x", grid=(nt,),
        out_shape=[jax.ShapeDtypeStruct((t_rows - TILE, D_MODEL), F32), jax.ShapeDtypeStruct((TILE, D_MODEL), F32),
                   jax.ShapeDtypeStruct((1, D_MODEL), F32)] + [jax.ShapeDtypeStruct(a.shape, a.dtype) for a in chip_partials],
        in_specs=[row(w) for w in SEG_W] + [pl.BlockSpec((GLA_HEADS, TILE, LANES), lambda i: (0, i, 0)),
                                            _head_spec(), _x_spec(), row(D_MODEL), one, ANY, ANY] + [ANY] * ne,
        out_specs=[_x_spec(), _head_spec(), one] + [ANY] * ne,
        scratch_shapes=W_SCRATCH() + _exchange_sems(ne, N_CHIP),
        compiler_params=_params(("arbitrary",)),
    )(*[dseg[n] for n in SEG_NAMES], dglr, head, x, dh1, g_norm, slabs, w_glr, *chip_partials)


W_TILE = 512


def _inproj_bwd_w(ut, dseg, dglr):
    nt = ut.shape[0]
    t_rows = nt * TILE
    kc = 3 if nt % 3 == 0 else 1
    tiles = [(s, c) for s in range(len(SEG_W)) for c in range(0, SEG_W[s], W_TILE)]
    bpt = W_TILE // LANES

    def body(ut_hbm, *refs):
        d_refs, dglr_hbm, out_hbm, oglr_ref = refs[:10], refs[10], refs[11], refs[12]
        ut_vm, dbuf, obuf, acc, gbuf, sem = refs[13:]

        def fetch(i):
            s, c = tiles[i]
            return pltpu.make_async_copy(d_refs[s].at[:, pl.ds(c, W_TILE)], dbuf.at[i % 2], sem.at[1 + i % 2])

        def contract(rhs_refs, width):
            acc[:, :width] = jnp.zeros((D_MODEL, width), F32)

            def step(k, carry):
                part = None
                for j in range(kc):
                    kk = k * kc + j
                    for rhs_ref in rhs_refs:
                        prod = _mm(ut_vm[kk], rhs_ref[pl.ds(pl.multiple_of(kk * TILE, TILE), TILE), :])
                        part = prod if part is None else part + prod
                acc[:, :width] += part
                return carry

            lax.fori_loop(0, nt // kc, step, 0)
            return acc[:, :width]

        load_ut = pltpu.make_async_copy(ut_hbm, ut_vm, sem.at[0])
        load_glr = pltpu.make_async_copy(dglr_hbm, gbuf, sem.at[5])
        load_ut.start()
        load_glr.start()
        fetch(0).start()
        load_ut.wait()
        stores = {}
        for i, (s, c) in enumerate(tiles):
            if i + 1 < len(tiles):
                fetch(i + 1).start()
            fetch(i).wait()
            if i >= 2:
                stores[i - 2].wait()
            total = contract([dbuf.at[i % 2]], W_TILE)
            for j in range(bpt):
                obuf[i % 2, j] = total[:, j * LANES:(j + 1) * LANES].astype(BF16)
            blk0 = (SEG_OFF[s] + c) // LANES
            stores[i] = pltpu.make_async_copy(obuf.at[i % 2], out_hbm.at[pl.ds(blk0, bpt)], sem.at[3 + i % 2])
            stores[i].start()
        load_glr.wait()
        oglr_ref[...] = contract([gbuf.at[h] for h in range(GLA_HEADS)], LANES)
        for i in range(max(0, len(tiles) - 2), len(tiles)):
            stores[i].wait()

    return _call(
        body, "inproj_bwd_w",
        out_shape=[jax.ShapeDtypeStruct((AL_COLS // LANES, D_MODEL, LANES), BF16), jax.ShapeDtypeStruct((D_MODEL, LANES), F32)],
        in_specs=[ANY] * 12, out_specs=[ANY, pl.BlockSpec(memory_space=pltpu.VMEM)],
        scratch_shapes=[pltpu.VMEM((nt, D_MODEL, TILE), BF16), pltpu.VMEM((2, t_rows, W_TILE), BF16),
                        pltpu.VMEM((2, bpt, D_MODEL, LANES), BF16), pltpu.VMEM((D_MODEL, W_TILE), F32),
                        pltpu.VMEM((GLA_HEADS, t_rows, LANES), BF16), pltpu.SemaphoreType.DMA((6,))],
        compiler_params=_params(),
    )(ut, *[dseg[n] for n in SEG_NAMES], dglr)


def _position():
    x, y, c = lax.axis_index("x"), lax.axis_index("y"), lax.axis_index("c")
    return x, y, c


def _index(px, py, pc):
    return 4 * px + 2 * py + pc


def _all_gather(arrs, name):
    n = len(arrs)

    def body(*refs):
        ins, outs = refs[:n], refs[n:2 * n]
        send_sems, recv_sems, local_sems = refs[2 * n:]
        x, y, c = _position()
        me, sibling = (x, y, c), (x, y, 1 - c)
        chips = [(1 - x, y), (x, 1 - y), (1 - x, 1 - y)]

        def copy(a, k, block, to, src=None):
            dst = outs[a].at[_index(*block)]
            return pltpu.make_async_remote_copy(src_ref=dst if src is None else src, dst_ref=dst,
                                                send_sem=send_sems.at[7 * a + k], recv_sem=recv_sems.at[7 * a + k],
                                                device_id=to, device_id_type=MESH)

        mine = [pltpu.make_async_copy(ins[a], outs[a].at[_index(*me)], local_sems.at[a]) for a in range(n)]
        for cp in mine:
            cp.start()
        first = []
        for a in range(n):
            first.append(copy(a, 0, me, sibling, src=ins[a]))
            first += [copy(a, 1 + j, me, (*chip, c), src=ins[a]) for j, chip in enumerate(chips)]
        for cp in first:
            cp.start()
        passed = []
        for j, chip in enumerate(chips):
            for a in range(n):
                copy(a, 1 + j, (*chip, c), me).wait_recv()
                cp = copy(a, 4 + j, (*chip, c), sibling)
                cp.start()
                passed.append(cp)
        for a in range(n):
            copy(a, 0, sibling, me).wait_recv()
            for j, chip in enumerate(chips):
                copy(a, 4 + j, (*chip, 1 - c), me).wait_recv()
        for cp in first + passed:
            cp.wait_send()
        for cp in mine:
            cp.wait()

    return _call(
        body, name,
        out_shape=[jax.ShapeDtypeStruct((N_DEV, *a.shape), a.dtype) for a in arrs],
        in_specs=[ANY] * n, out_specs=[ANY] * n,
        scratch_shapes=[pltpu.SemaphoreType.DMA((7 * n,)), pltpu.SemaphoreType.DMA((7 * n,)), pltpu.SemaphoreType.DMA((n,))],
    )(*arrs)


N_CHIP = N_DEV // 2


def _slab_block0(owner):
    step = SLAB_BLK0[1]
    assert all(SLAB_BLK0[d] == step * d - (d == N_DEV - 1) for d in range(N_DEV))
    return step * owner - jnp.where(owner == N_DEV - 1, 1, 0)


def _exchange_sibling(dw_blocks, row_sends):
    n = 1 + len(row_sends)

    def body(*refs):
        dw_ref, row_refs, outs, (send_sems, recv_sems) = refs[0], refs[1:n], refs[n:2 * n], refs[2 * n:]
        x, y, c = _position()
        copies = []
        for q in range(N_CHIP):
            owner = 2 * q + (1 - c)
            srcs = [dw_ref.at[pl.ds(_slab_block0(owner), SLAB_BLOCKS)]] + [r.at[owner] for r in row_refs]
            for k, src in enumerate(srcs):
                copies.append(pltpu.make_async_remote_copy(src_ref=src, dst_ref=outs[k].at[q], send_sem=send_sems.at[n * q + k],
                                                           recv_sem=recv_sems.at[n * q + k], device_id=(x, y, 1 - c), device_id_type=MESH))
        for cp in copies:
            cp.start()
        for cp in copies:
            cp.wait()

    return _call(
        body, "exchange_sibling",
        out_shape=[jax.ShapeDtypeStruct((N_CHIP, SLAB_BLOCKS, D_MODEL, LANES), BF16)]
                  + [jax.ShapeDtypeStruct((N_CHIP, *r.shape[1:]), BF16) for r in row_sends],
        in_specs=[ANY] * n, out_specs=[ANY] * n,
        scratch_shapes=[pltpu.SemaphoreType.DMA((n * N_CHIP,)), pltpu.SemaphoreType.DMA((n * N_CHIP,))],
    )(dw_blocks, *row_sends)


def _add_bf16(c_ref, a_ref, b_ref, o_ref):
    o_ref[...] = (a_ref[...].astype(F32) + b_ref[...].astype(F32)).astype(BF16)


def _chip_partial_slab(dw_blocks, sib, core):
    blk = pl.BlockSpec((None, SLAB_BLOCKS, D_MODEL, LANES), lambda q, c_ref: (q, 0, 0, 0))
    return _call(
        functools.partial(_add_bf16), "chip_partial_w_in", out_shape=jax.ShapeDtypeStruct(sib.shape, BF16),
        grid_spec=pltpu.PrefetchScalarGridSpec(
            num_scalar_prefetch=1, grid=(N_CHIP,),
            in_specs=[pl.BlockSpec((pl.Element(SLAB_BLOCKS), pl.Element(D_MODEL), pl.Element(LANES)),
                                   lambda q, c_ref: (_slab_block0(2 * q + c_ref[0]), 0, 0)), blk],
            out_specs=blk),
        compiler_params=_params(("arbitrary",)),
    )(core, dw_blocks, sib)


def _chip_partial_rows(send, sib, core, name):
    rows, cols = send.shape[1:]
    blk = pl.BlockSpec((None, rows, cols), lambda q, c_ref: (q, 0, 0))
    return _call(
        functools.partial(_add_bf16), name, out_shape=jax.ShapeDtypeStruct(sib.shape, BF16),
        grid_spec=pltpu.PrefetchScalarGridSpec(
            num_scalar_prefetch=1, grid=(N_CHIP,),
            in_specs=[pl.BlockSpec((None, rows, cols), lambda q, c_ref: (2 * q + c_ref[0], 0, 0)), blk], out_specs=blk),
        compiler_params=_params(("arbitrary",)),
    )(core, send, sib)


def _exchange_sems(n_arrays, n_peers):
    return [pltpu.SemaphoreType.DMA((n_arrays * n_peers,)), pltpu.SemaphoreType.DMA((n_arrays * n_peers,)),
            pltpu.SemaphoreType.DMA((n_arrays,))]


class _Exchange:
    def __init__(self, srcs, dsts, sems, among_chips):
        self.arrs = list(zip(srcs, dsts))
        self.n = len(self.arrs)
        self.send_sems, self.recv_sems, self.local_sems = sems
        self.among_chips = among_chips
        x, y, c = _position()
        self.c = c
        self.me = 2 * x + y if among_chips else _index(x, y, c)
        self.n_peers = N_CHIP if among_chips else N_DEV

    def _device(self, p):
        return (p // 2, p % 2, self.c) if self.among_chips else (p // 4, (p // 2) % 2, p % 2)

    def _src(self, k, p):
        src = self.arrs[k][0]
        return src.at[p] if self.among_chips else src

    def _mine(self):
        return [pltpu.make_async_copy(self._src(k, self.me), self.arrs[k][1].at[self.me], self.local_sems.at[k]) for k in range(self.n)]

    def _copy(self, p, k, landing):
        return pltpu.make_async_remote_copy(
            src_ref=self._src(k, p), dst_ref=self.arrs[k][1].at[landing], send_sem=self.send_sems.at[self.n * p + k],
            recv_sem=self.recv_sems.at[self.n * landing + k], device_id=self._device(p), device_id_type=MESH)

    def _others(self, fn):
        for p in range(self.n_peers):
            @pl.when(p != self.me)
            def _():
                for k in range(self.n):
                    fn(p, k)

    def start(self):
        for cp in self._mine():
            cp.start()
        self._others(lambda p, k: self._copy(p, k, self.me).start())

    def finish(self):
        self._others(lambda p, k: self._copy(p, k, p).wait_recv())
        self._others(lambda p, k: self._copy(p, k, self.me).wait_send())
        for cp in self._mine():
            cp.wait()


def _adamw(g, w, m, v):
    m_new = ADAM_B1 * m + (1.0 - ADAM_B1) * g
    v_new = ADAM_B2 * v + (1.0 - ADAM_B2) * (g * g)
    m_hat = m_new / (1.0 - ADAM_B1 ** ADAM_STEP)
    v_hat = v_new / (1.0 - ADAM_B2 ** ADAM_STEP)
    delta = -ADAM_LR * (m_hat / (jnp.sqrt(v_hat) + ADAM_EPS) + ADAM_WD * w)
    return delta, m_new, v_new


def _sum_partials(p_ref):
    g = p_ref[0].astype(F32)
    for d in range(1, p_ref.shape[0]):
        g = g + p_ref[d].astype(F32)
    return g


def _reduce_adam(parts, w, m, v, name, block_rows, row_off=0):
    rows, cols = w.shape
    off = row_off // block_rows

    def body(p_ref, w_ref, m_ref, v_ref, g_ref, d_ref, mo_ref, vo_ref):
        g = _sum_partials(p_ref)
        g_ref[...] = g
        d_ref[...], mo_ref[...], vo_ref[...] = _adamw(g, w_ref[...], m_ref[...], v_ref[...])

    blk = pl.BlockSpec((block_rows, cols), lambda i: (i, 0))
    return _call(
        body, name, grid=(rows // block_rows,),
        out_shape=[jax.ShapeDtypeStruct((rows, cols), F32)] * 4,
        in_specs=[pl.BlockSpec((parts.shape[0], block_rows, cols), lambda i: (0, i + off, 0)), blk, blk, blk],
        out_specs=[blk] * 4,
        compiler_params=_params(("arbitrary",)),
    )(parts, w, m, v)


def _reduce_adam_slab(parts, glr, w, m, v, me):
    rows, cols = w.shape
    shift = jnp.asarray(SLAB_SHIFT, jnp.int32)[me]
    glr_at = jnp.where(me == GLR_DEV, GLR_LOCAL, cols).astype(jnp.int32)

    def body(s_ref, p_ref, glr_ref, w_ref, m_ref, v_ref, g_ref, d_ref, mo_ref, vo_ref):
        shift, glr_at = s_ref[0], s_ref[1]
        slab = jnp.concatenate([_sum_partials(p_ref.at[:, j]) for j in range(SLAB_BLOCKS)], axis=1)
        before = pltpu.roll(slab, SLAB_W - shift, 1)
        after = pltpu.roll(slab, lax.rem(SLAB_W - shift + GLA_RANK, SLAB_W), 1)
        wide = jnp.concatenate([glr_ref[...], jnp.zeros((LANES, SLAB_W - LANES), F32)], axis=1)
        placed = pltpu.roll(wide, lax.rem(glr_at, SLAB_W), 1)
        lane = lax.broadcasted_iota(jnp.int32, (LANES, SLAB_W), 1)
        g = jnp.where(lane < glr_at, before, jnp.where(lane < glr_at + GLA_RANK, placed, after))[:, :cols]
        g_ref[...] = g
        d_ref[...], mo_ref[...], vo_ref[...] = _adamw(g, w_ref[...], m_ref[...], v_ref[...])

    blk = pl.BlockSpec((LANES, cols), lambda i, s: (i, 0))
    return _call(
        body, "adam_w_in", out_shape=[jax.ShapeDtypeStruct((rows, cols), F32)] * 4,
        grid_spec=pltpu.PrefetchScalarGridSpec(
            num_scalar_prefetch=1, grid=(rows // LANES,),
            in_specs=[pl.BlockSpec((parts.shape[0], SLAB_BLOCKS, LANES, LANES), lambda i, s: (0, 0, i, 0)),
                      pl.BlockSpec((LANES, LANES), lambda i, s: (i, 0)), blk, blk, blk],
            out_specs=[blk] * 4),
        compiler_params=_params(("arbitrary",)),
    )(jnp.stack([shift, glr_at]), parts, glr, w, m, v)


def _reduce_small(parts):
    def body(p_ref, o_ref):
        o_ref[...] = _sum_partials(p_ref)

    return _call(body, "reduce_small", out_shape=jax.ShapeDtypeStruct(parts.shape[1:], F32))(parts)


def _adam_small(g, w, m, v):
    def body(g_ref, w_ref, m_ref, v_ref, d_ref, mo_ref, vo_ref):
        d_ref[...], mo_ref[...], vo_ref[...] = _adamw(g_ref[...], w_ref[...], m_ref[...], v_ref[...])

    return _call(body, "adam_small", out_shape=[jax.ShapeDtypeStruct(g.shape, F32)] * 3)(g, w, m, v)


def _pack_rows(arrs):
    rows = []
    for a in arrs:
        flat = a.reshape(-1).astype(F32)
        pad = (-flat.shape[0]) % LANES
        rows.append(jnp.pad(flat, (0, pad)).reshape(-1, LANES))
    packed = jnp.concatenate(rows, axis=0)
    return jnp.pad(packed, ((0, (-packed.shape[0]) % 8), (0, 0)))


def _unpack_rows(packed, shapes):
    out, r = [], 0
    for shp in shapes:
        size = 1
        for s in shp:
            size *= s
        nrows = -(-size // LANES)
        out.append(packed[r:r + nrows].reshape(-1)[:size].reshape(shp))
        r += nrows
    return out


def _shard_to_slab(shard, d):
    glr = jnp.zeros((D_MODEL, GLA_RANK), shard.dtype)
    if d == GLR_DEV:
        glr = shard[:, GLR_LOCAL:GLR_LOCAL + GLA_RANK]
        shard = jnp.concatenate([shard[:, :GLR_LOCAL], shard[:, GLR_LOCAL + GLA_RANK:]], axis=1)
    return jnp.pad(shard, ((0, 0), (SLAB_SHIFT[d], SLAB_W - SLAB_SHIFT[d] - shard.shape[1]))), glr


def kernel(x, meta_tokens, norm_gain, w_in, w_gate_up, b_gate, ret_norm_gain, gla_norm_gain, w_branch_ret, w_branch_gla, w_out, final_norm_gain, loss_target, m_meta_tokens, m_norm_gain, m_w_in, m_w_gate_up, m_b_gate, m_ret_norm_gain, m_gla_norm_gain, m_w_branch_ret, m_w_branch_gla, m_w_out, m_final_norm_gain, v_meta_tokens, v_norm_gain, v_w_in, v_w_gate_up, v_b_gate, v_ret_norm_gain, v_gla_norm_gain, v_w_branch_ret, v_w_branch_gla, v_w_out, v_final_norm_gain):
    xi, yi, ci = _position()
    me = _index(xi, yi, ci)
    seq = x.shape[1]
    t_rows = seq + TILE
    in_shard = w_in.shape[2]
    gu_shard = w_gate_up.shape[2]
    meta_shard = meta_tokens.shape[1]
    ret_rows, gla_rows, out_rows = w_branch_ret.shape[1], w_branch_gla.shape[1], w_out.shape[1]

    assert in_shard == IN_SHARD
    slab_local, glr_local = lax.switch(me, [functools.partial(_shard_to_slab, d=d) for d in range(N_DEV)], w_in[0])
    small_local = jnp.concatenate([meta_tokens, jnp.pad(w_gate_up[0], ((0, 0), (0, LANES - gu_shard))),
                                   glr_local.reshape(-1, LANES)], axis=0)
    (g_small,) = _all_gather([small_local], "all_gather_small_weights")
    n_small = N_META + GLA_RANK
    w_glr = jnp.pad(g_small[GLR_DEV, n_small:].reshape(D_MODEL, GLA_RANK), ((0, 0), (0, LANES - GLA_RANK))).astype(BF16)
    meta_full = jnp.transpose(g_small[:, :N_META, :], (1, 0, 2)).reshape(N_META, D_MODEL)
    wgu_full = jnp.transpose(g_small[:, N_META:n_small, :gu_shard], (1, 0, 2)).reshape(GLA_RANK, GLA_HEADS * GLA_K)
    wgu_pad = jnp.pad(wgu_full, ((0, LANES - GLA_RANK), (0, 0)))

    pos = jnp.arange(t_rows, dtype=F32) - float(PAD_ROWS)
    half = RET_QK // 2
    inv = ROPE_BASE ** (-jnp.arange(half, dtype=F32) / half)
    ang = pos[:, None] * inv[None, :]
    cos, sin = jnp.cos(ang), jnp.sin(ang)
    lg = jnp.log1p(-(2.0 ** (-5.0 - jnp.arange(RET_HEADS, dtype=F32))))

    head = jnp.concatenate([jnp.zeros((PAD_ROWS, D_MODEL), F32), meta_full], axis=0)
    u, ut, glr = _prenorm(head, x[0], norm_gain, w_glr)
    proj, slabs = _inproj_fwd(u, slab_local.astype(BF16))
    o_ret_raw, o_ret, ret_states, (g_br, g_bg, g_o) = _ret_fwd(
        proj, cos, sin, ret_norm_gain, lg, [w_branch_ret[0].astype(BF16), w_branch_gla[0].astype(BF16), w_out[0].astype(BF16)])
    w_br, w_bg, w_o = g_br.reshape(RET_W, D_MODEL), g_bg.reshape(GLA_W, D_MODEL), g_o.reshape(D_MODEL, D_MODEL)
    masks, cum_fwd, cum_bwd = _gla_tables()
    o_gla_raw, o_gla, gla_states = _gla_fwd(proj, glr, wgu_pad, b_gate, gla_norm_gain, masks, cum_fwd)
    (dh1, d_mr, d_mg, do_ret, do_gla, loss_part, d_gfinal, dw_br, dw_bg, dw_o) = _merge_fwd_bwd(
        o_ret, o_gla, proj, x[0], loss_target[0], final_norm_gain.reshape(1, D_MODEL), w_br, w_bg, w_o)

    d_rq, d_rk, d_rv, d_rg, d_gret = _ret_bwd(proj, cos, sin, ret_norm_gain, lg, o_ret_raw, do_ret, ret_states)
    d_gq, d_gk, d_gv, d_gg, dglr_parts, d_wgu, d_bgate, d_ggla = _gla_bwd(
        proj, glr, wgu_pad, b_gate, gla_norm_gain, o_gla_raw, do_gla, gla_states, masks, cum_fwd, cum_bwd)
    dseg = dict(rq=d_rq, rk=d_rk, rv=d_rv, rg=d_rg, gq=d_gq, gk=d_gk, gv=d_gv, gg=d_gg, mr=d_mr, mg=d_mg)
    dw_blocks, dw_glr = _inproj_bwd_w(ut, dseg, dglr_parts)

    row_sends = [dw_br.reshape(N_DEV, ret_rows, D_MODEL), dw_bg.reshape(N_DEV, gla_rows, D_MODEL),
                 dw_o.reshape(N_DEV, out_rows, D_MODEL)]
    sib_in, *sib_rows = _exchange_sibling(dw_blocks, row_sends)
    core = ci.astype(jnp.int32).reshape(1)
    chip_partials = [_chip_partial_slab(dw_blocks, sib_in, core)] + [
        _chip_partial_rows(send, sib, core, "chip_partial_" + name)
        for send, sib, name in zip(row_sends, sib_rows, ("w_branch_ret", "w_branch_gla", "w_out"))]
    grad_x, d_head, d_gnorm, p_in, p_br, p_bg, p_o = _inproj_bwd_x(
        dseg, dglr_parts, head, x[0], dh1, norm_gain, slabs, w_glr, chip_partials)
    small_shapes = [(N_META, D_MODEL), (1, D_MODEL), (GLA_RANK, GLA_HEADS * GLA_K), (1, GLA_HEADS * GLA_K),
                    (1, RET_W), (1, GLA_W), (1, D_MODEL), (1, LANES), (D_MODEL, GLA_RANK)]
    small_part = _pack_rows([d_head[PAD_ROWS:], d_gnorm, d_wgu[:GLA_RANK], d_bgate, d_gret, d_ggla, d_gfinal, loss_part,
                             dw_glr[:, :GLA_RANK]])
    (p_small,) = _all_gather([small_part], "all_gather_small_partials")

    (g_meta_f, g_gnorm, g_wgu_f, g_bgate, g_gret, g_ggla, g_gfinal, loss_all,
     g_wglr) = _unpack_rows(_reduce_small(p_small), small_shapes)
    g_w_in, d_w_in, nm_w_in, nv_w_in = _reduce_adam_slab(
        p_in, jnp.pad(g_wglr, ((0, 0), (0, LANES - GLA_RANK))), w_in[0], m_w_in[0], v_w_in[0], me)
    rb = gla_rows
    g_w_br, d_w_br, nm_w_br, nv_w_br = _reduce_adam(p_br, w_branch_ret[0], m_w_branch_ret[0], v_w_branch_ret[0], "adam_w_branch_ret", rb)
    g_w_bg, d_w_bg, nm_w_bg, nv_w_bg = _reduce_adam(p_bg, w_branch_gla[0], m_w_branch_gla[0], v_w_branch_gla[0], "adam_w_branch_gla", rb)
    g_w_o, d_w_o, nm_w_o, nv_w_o = _reduce_adam(p_o, w_out[0], m_w_out[0], v_w_out[0], "adam_w_out", rb)
    g_meta = lax.dynamic_slice_in_dim(g_meta_f, me * meta_shard, meta_shard, axis=1)
    g_wgu = lax.dynamic_slice_in_dim(g_wgu_f, me * gu_shard, gu_shard, axis=1)
    s_g = [g_meta, g_gnorm, g_wgu, g_bgate, g_gret, g_ggla, g_gfinal]
    s_w = [meta_tokens, norm_gain, w_gate_up[0], b_gate, ret_norm_gain, gla_norm_gain, final_norm_gain]
    s_m = [m_meta_tokens, m_norm_gain, m_w_gate_up[0], m_b_gate, m_ret_norm_gain, m_gla_norm_gain, m_final_norm_gain]
    s_v = [v_meta_tokens, v_norm_gain, v_w_gate_up[0], v_b_gate, v_ret_norm_gain, v_gla_norm_gain, v_final_norm_gain]
    shapes = [a.shape for a in s_g]
    s_d, s_nm, s_nv = [_unpack_rows(p, shapes) for p in _adam_small(*[_pack_rows(l) for l in (s_g, s_w, s_m, s_v)])]

    loss = loss_all[0, 0]
    grad_x = grad_x[None]

    def order(meta, gnorm, win, wgu, bgate, gret, ggla, wbr, wbg, wo, gfin):
        return (meta, gnorm, win[None], wgu[None], bgate, gret, ggla, wbr[None], wbg[None], wo[None], gfin.reshape(final_norm_gain.shape))

    def small(l):
        return dict(meta=l[0], gnorm=l[1], wgu=l[2], bgate=l[3], gret=l[4], ggla=l[5], gfin=l[6])

    grads = order(win=g_w_in, wbr=g_w_br, wbg=g_w_bg, wo=g_w_o, **small(s_g))
    deltas = order(win=d_w_in, wbr=d_w_br, wbg=d_w_bg, wo=d_w_o, **small(s_d))
    new_m = order(win=nm_w_in, wbr=nm_w_br, wbg=nm_w_bg, wo=nm_w_o, **small(s_nm))
    new_v = order(win=nv_w_in, wbr=nv_w_br, wbg=nv_w_bg, wo=nv_w_o, **small(s_nv))
    return (loss, grad_x, *grads, *deltas, *new_m, *new_v)
```

```python
import functools

import jax
import jax.numpy as jnp
from jax import lax
from jax.experimental import pallas as pl
from jax.experimental.pallas import tpu as pltpu

F32 = jnp.float32
BF16 = jnp.bfloat16

D_MODEL = 1024
N_META = 16
TILE = 256
PAD_ROWS = TILE - N_META
RET_HEADS = 4
RET_QK = 256
RET_V = 512
RET_W = RET_HEADS * RET_V
GLA_HEADS = 4
GLA_K = 128
GLA_V = 256
GLA_W = GLA_HEADS * GLA_V
GLA_RANK = 16
GLA_TAU = 16.0
GLA_CHUNK = 16
ROPE_BASE = 10000.0
EPS = 1e-6
LANES = 128
N_DEV = 8
SEG_NAMES = ("rq", "rk", "rv", "rg", "gq", "gk", "gv", "gg", "mr", "mg")
SEG_W = (1024, 1024, 2048, 2048, 512, 512, 1024, 1024, 1024, 1024)
SEG_OFF = tuple(sum(SEG_W[:i]) for i in range(len(SEG_W)))
AL_COLS = sum(SEG_W)
IN_COLS = AL_COLS + GLA_RANK
GLR_OFF = sum(SEG_W[:8])
IN_SHARD = IN_COLS // N_DEV


def _aligned_col(c):
    assert c <= GLR_OFF or c >= GLR_OFF + GLA_RANK
    return c if c <= GLR_OFF else c - GLA_RANK


SLAB_BOUND = tuple(_aligned_col(IN_SHARD * d) for d in range(N_DEV + 1))
SLAB_BLK0 = tuple(b // LANES for b in SLAB_BOUND[:-1])
SLAB_SHIFT = tuple(b % LANES for b in SLAB_BOUND[:-1])
SLAB_BLOCKS = max(-(-SLAB_BOUND[d + 1] // LANES) - SLAB_BLK0[d] for d in range(N_DEV))
SLAB_W = SLAB_BLOCKS * LANES
GLR_DEV = GLR_OFF // IN_SHARD
GLR_LOCAL = GLR_OFF - GLR_DEV * IN_SHARD
assert all(SLAB_BLK0[d] + SLAB_BLOCKS <= AL_COLS // LANES for d in range(N_DEV))
VMEM_LIMIT = 58 * 1024 * 1024
ADAM_LR, ADAM_B1, ADAM_B2, ADAM_EPS, ADAM_WD, ADAM_STEP = 0.001, 0.9, 0.999, 1e-08, 0.01, 10
ANY = pl.BlockSpec(memory_space=pl.ANY)
MESH = pl.DeviceIdType.MESH


def _call(body, name, **kw):
    return pl.pallas_call(body, name=name, **kw)


def _params(sem=None):
    return pltpu.CompilerParams(dimension_semantics=sem, vmem_limit_bytes=VMEM_LIMIT)


def _mm(a, b):
    return jnp.dot(a, b, preferred_element_type=F32)


def _mm_nt(a, b):
    return lax.dot_general(a, b, (((1,), (1,)), ((), ())), preferred_element_type=F32)


def _mm_tn(a, b):
    return lax.dot_general(a, b, (((0,), (0,)), ((), ())), preferred_element_type=F32)


def _sigmoid(x):
    return 1.0 / (1.0 + jnp.exp(-x))


def _rope(t, cos, sin):
    half = t.shape[-1] // 2
    t1, t2 = t[:, :half], t[:, half:]
    return jnp.concatenate([t1 * cos - t2 * sin, t2 * cos + t1 * sin], axis=-1)


def _rope_bwd(g, cos, sin):
    half = g.shape[-1] // 2
    g1, g2 = g[:, :half], g[:, half:]
    return jnp.concatenate([g1 * cos + g2 * sin, g2 * cos - g1 * sin], axis=-1)


def _row_mean(x):
    return jnp.mean(x, axis=-1, keepdims=True)


def _col_sum(x):
    return jnp.sum(x, axis=0, keepdims=True)


def _tile_rows(head_ref, x_ref):
    return jnp.where(pl.program_id(0) == 0, head_ref[...], x_ref[...])


def _head_spec():
    return pl.BlockSpec((TILE, D_MODEL), lambda i: (0, 0))


def _x_spec():
    return pl.BlockSpec((TILE, D_MODEL), lambda i: (jnp.maximum(i - 1, 0), 0))


def _slab_plan():
    interior, shared = [], []
    for d in range(N_DEV):
        lo, hi = -(-SLAB_BOUND[d] // LANES), SLAB_BOUND[d + 1] // LANES
        interior.append((d, LANES * (lo - SLAB_BLK0[d]), LANES * lo, LANES * (hi - lo)))
        if d + 1 < N_DEV and SLAB_BOUND[d + 1] % LANES:
            shared.append((hi, d, hi - SLAB_BLK0[d]))
    return interior, shared


W_SCRATCH = lambda: [pltpu.VMEM((D_MODEL, AL_COLS), BF16), pltpu.VMEM((D_MODEL, LANES), BF16),
                     pltpu.VMEM((2 * (N_DEV - 1), D_MODEL, LANES), BF16), pltpu.SemaphoreType.DMA((3 * N_DEV,))]


def _load_weight(slabs_hbm, wg_hbm, w_vm, wg_vm, edge_vm, sem):
    interior, shared = _slab_plan()
    copies = [pltpu.make_async_copy(wg_hbm, wg_vm, sem.at[0])]
    for d, src, dst, width in interior:
        copies.append(pltpu.make_async_copy(slabs_hbm.at[d, :, pl.ds(src, width)], w_vm.at[:, pl.ds(dst, width)], sem.at[1 + d]))
    for n, (_, d, blk) in enumerate(shared):
        copies.append(pltpu.make_async_copy(slabs_hbm.at[d, :, pl.ds(LANES * blk, LANES)], edge_vm.at[2 * n], sem.at[1 + N_DEV + 2 * n]))
        copies.append(pltpu.make_async_copy(slabs_hbm.at[d + 1, :, pl.ds(0, LANES)], edge_vm.at[2 * n + 1], sem.at[2 + N_DEV + 2 * n]))
    for cp in copies:
        cp.start()
    for cp in copies:
        cp.wait()
    for n, (blk, _, _) in enumerate(shared):
        w_vm[:, LANES * blk:LANES * (blk + 1)] = edge_vm[2 * n] + edge_vm[2 * n + 1]


def _proj_specs(names, n_units, where):
    specs = []
    for name in names:
        s = SEG_NAMES.index(name)
        nblk = SEG_W[s] // n_units // LANES
        base = SEG_OFF[s] // LANES
        assert base % nblk == 0
        specs.append(pl.BlockSpec((nblk, TILE, LANES), lambda *g, base=base, nblk=nblk: (base // nblk + where(*g)[0], where(*g)[1], 0)))
    return specs


def _cols(ref):
    return ref[0] if ref.shape[0] == 1 else jnp.concatenate([ref[j] for j in range(ref.shape[0])], axis=1)


def _prenorm(head, x, g_norm, w_glr):
    t_rows = x.shape[0] + TILE
    nt = t_rows // TILE

    def body(head_ref, x_ref, g_ref, wg_ref, u_ref, ut_ref, glr_ref):
        x = _tile_rows(head_ref, x_ref)
        r = lax.rsqrt(_row_mean(x * x) + EPS)
        u32 = (x * r * g_ref[...]).astype(BF16).astype(F32)
        u = u32.astype(BF16)
        u_ref[...] = u
        ut_ref[...] = u32.T.astype(BF16)
        glr_ref[...] = _mm(u, wg_ref[...])

    row = lambda w: pl.BlockSpec((TILE, w), lambda i: (i, 0))
    return _call(
        body, "prenorm", grid=(nt,),
        out_shape=[jax.ShapeDtypeStruct((t_rows, D_MODEL), BF16), jax.ShapeDtypeStruct((nt, D_MODEL, TILE), BF16),
                   jax.ShapeDtypeStruct((t_rows, LANES), F32)],
        in_specs=[_head_spec(), _x_spec(), pl.BlockSpec((1, D_MODEL), lambda i: (0, 0)), pl.BlockSpec((D_MODEL, LANES), lambda i: (0, 0))],
        out_specs=[row(D_MODEL), pl.BlockSpec((None, D_MODEL, TILE), lambda i: (i, 0, 0)), row(LANES)],
        compiler_params=_params(("arbitrary",)),
    )(head, x, g_norm, w_glr)


SLAB_INNER = 9


def _edge_blocks():
    inner = {SLAB_BLK0[d] + j for d in range(N_DEV) for j in range(1, 1 + SLAB_INNER)}
    edges = []
    for blk in range(AL_COLS // LANES):
        if blk not in inner:
            srcs = [(d, blk - SLAB_BLK0[d]) for d in range(N_DEV)
                    if SLAB_BLK0[d] <= blk < SLAB_BLK0[d] + SLAB_BLOCKS and SLAB_BOUND[d] < LANES * (blk + 1) and LANES * blk < SLAB_BOUND[d + 1]]
            edges.append((blk, srcs))
    return edges


def _inproj_fwd(u, slab_local):
    t_rows = u.shape[0]
    nt = t_rows // TILE
    rc = (3 if nt % 3 == 0 else 1) * TILE
    n_chunks = t_rows // rc
    edges = _edge_blocks()
    ne = len(edges)
    runs = []
    for k, (blk, _) in enumerate(edges):
        if runs and edges[runs[-1][0] + runs[-1][1] - 1][0] + 1 == blk:
            runs[-1] = (runs[-1][0], runs[-1][1] + 1)
        else:
            runs.append((k, 1))
    n_stage = sum(len(srcs) for _, srcs in edges)

    def body(u_hbm, slab_hbm, proj_hbm, slabs_hbm, u_vm, wbuf, obuf, ebuf, stage, ebuf_out, sem_u, sem_w, sem_o, sem_s, sem_eo,
             send_sems, recv_sems, sem_l):
        x, y, c = _position()
        me, sibling = (x, y, c), (x, y, 1 - c)
        chips = [(1 - x, y), (x, 1 - y), (1 - x, 1 - y)]

        def slab_copy(k, block, to, src=None):
            dst = slabs_hbm.at[_index(*block)]
            return pltpu.make_async_remote_copy(src_ref=dst if src is None else src, dst_ref=dst, send_sem=send_sems.at[k],
                                                recv_sem=recv_sems.at[k], device_id=to, device_id_type=MESH)

        mine = pltpu.make_async_copy(slab_hbm, slabs_hbm.at[_index(*me)], sem_l)
        mine.start()
        first = [slab_copy(0, me, sibling, src=slab_hbm)] + [slab_copy(1 + j, me, (*chip, c), src=slab_hbm) for j, chip in enumerate(chips)]
        for cp in first[:2]:
            cp.start()
        load_u = pltpu.make_async_copy(u_hbm, u_vm, sem_u)
        load_u.start()
        load_u.wait()

        def store(slot, block0, rows0):
            return pltpu.make_async_copy(obuf.at[slot], proj_hbm.at[pl.ds(block0, SLAB_INNER), pl.ds(rows0, rc)], sem_o.at[slot])

        def multiply(dev):
            load_w = pltpu.make_async_copy(slabs_hbm.at[dev, :, pl.ds(LANES, SLAB_INNER * LANES)], wbuf, sem_w)
            load_w.start()
            load_w.wait()
            block0 = _slab_block0(dev) + 1

            def chunk(r, carry):
                slot = lax.rem(r, 2)
                rows0 = pl.multiple_of(r * rc, rc)

                @pl.when(r >= 2)
                def _():
                    store(slot, block0, rows0).wait()

                res = _mm(u_vm[pl.ds(rows0, rc), :], wbuf[...])
                for j in range(SLAB_INNER):
                    obuf[slot, j] = res[:, j * LANES:(j + 1) * LANES].astype(BF16)
                store(slot, block0, rows0).start()
                return carry

            lax.fori_loop(0, n_chunks, chunk, 0)
            for r in range(max(0, n_chunks - 2), n_chunks):
                store(r % 2, block0, r * rc).wait()

        mine.wait()
        multiply(_index(*me))
        slab_copy(0, sibling, me).wait_recv()
        multiply(_index(*sibling))
        passed = []
        for j, chip in enumerate(chips):
            slab_copy(1 + j, (*chip, c), me).wait_recv()
            passed.append(slab_copy(4 + j, (*chip, c), sibling))
            passed[-1].start()
            if j + 1 < len(chips):
                first[1 + j].wait_send()
                first[2 + j].start()
            multiply(_index(*chip, c))
            slab_copy(4 + j, (*chip, 1 - c), me).wait_recv()
            multiply(_index(*chip, 1 - c))

        loads, n = [], 0
        for k, (_, srcs) in enumerate(edges):
            for d, j in srcs:
                dst = ebuf.at[:, pl.ds(k * LANES, LANES)] if len(srcs) == 1 else stage.at[n]
                loads.append(pltpu.make_async_copy(slabs_hbm.at[d, :, pl.ds(j * LANES, LANES)], dst, sem_s.at[n]))
                n += 1
        for cp in loads:
            cp.start()
        for cp in loads:
            cp.wait()
        n = 0
        for k, (_, srcs) in enumerate(edges):
            if len(srcs) == 2:
                ebuf[:, k * LANES:(k + 1) * LANES] = stage[n] + stage[n + 1]
            n += len(srcs)

        def edge_stores(slot, rows0):
            return [pltpu.make_async_copy(ebuf_out.at[slot, pl.ds(k0, length)],
                                          proj_hbm.at[pl.ds(edges[k0][0], length), pl.ds(rows0, rc)], sem_eo.at[slot, i])
                    for i, (k0, length) in enumerate(runs)]

        def edge_chunk(r, carry):
            slot = lax.rem(r, 2)
            rows0 = pl.multiple_of(r * rc, rc)

            @pl.when(r >= 2)
            def _():
                for cp in edge_stores(slot, rows0):
                    cp.wait()

            res = _mm(u_vm[pl.ds(rows0, rc), :], ebuf[...])
            for k in range(ne):
                ebuf_out[slot, k] = res[:, k * LANES:(k + 1) * LANES].astype(BF16)
            for cp in edge_stores(slot, rows0):
                cp.start()
            return carry

        lax.fori_loop(0, n_chunks, edge_chunk, 0)
        for r in range(max(0, n_chunks - 2), n_chunks):
            for cp in edge_stores(r % 2, r * rc):
                cp.wait()

        for cp in [first[0], first[3]] + passed:
            cp.wait_send()

    return _call(
        body, "inproj_fwd",
        out_shape=[jax.ShapeDtypeStruct((AL_COLS // LANES, t_rows, LANES), BF16), jax.ShapeDtypeStruct((N_DEV, D_MODEL, SLAB_W), BF16)],
        in_specs=[ANY] * 2, out_specs=[ANY] * 2,
        scratch_shapes=[pltpu.VMEM((t_rows, D_MODEL), BF16), pltpu.VMEM((D_MODEL, SLAB_INNER * LANES), BF16),
                        pltpu.VMEM((2, SLAB_INNER, rc, LANES), BF16), pltpu.VMEM((D_MODEL, ne * LANES), BF16),
                        pltpu.VMEM((n_stage, D_MODEL, LANES), BF16), pltpu.VMEM((2, ne, rc, LANES), BF16),
                        pltpu.SemaphoreType.DMA, pltpu.SemaphoreType.DMA, pltpu.SemaphoreType.DMA((2,)),
                        pltpu.SemaphoreType.DMA((n_stage,)), pltpu.SemaphoreType.DMA((2, len(runs))),
                        pltpu.SemaphoreType.DMA((7,)), pltpu.SemaphoreType.DMA((7,)), pltpu.SemaphoreType.DMA],
        compiler_params=_params(),
    )(u, slab_local)


def _inproj_tiles(head, x, g_norm, slabs, w_glr):
    t_rows = x.shape[0] + TILE
    nt = t_rows // TILE
    n_blocks = AL_COLS // LANES

    def body(head_ref, x_ref, g_ref, slabs_hbm, wg_hbm, ut_ref, proj_ref, glr_ref, w_vm, wg_vm, edge_vm, sem):
        @pl.when(pl.program_id(0) == 0)
        def _():
            _load_weight(slabs_hbm, wg_hbm, w_vm, wg_vm, edge_vm, sem)

        x = _tile_rows(head_ref, x_ref)
        r = lax.rsqrt(_row_mean(x * x) + EPS)
        u32 = (x * r * g_ref[...]).astype(BF16).astype(F32)
        u = u32.astype(BF16)
        ut_ref[...] = u32.T.astype(BF16)
        for s in range(len(SEG_W)):
            res = _mm(u, w_vm[:, SEG_OFF[s]:SEG_OFF[s] + SEG_W[s]]).astype(BF16)
            for j in range(SEG_W[s] // LANES):
                proj_ref[SEG_OFF[s] // LANES + j] = res[:, j * LANES:(j + 1) * LANES]
        glr_ref[...] = _mm(u, wg_vm[...])

    return _call(
        body, "inproj_fwd_tiles", grid=(nt,),
        out_shape=[jax.ShapeDtypeStruct((nt, D_MODEL, TILE), BF16), jax.ShapeDtypeStruct((n_blocks, t_rows, LANES), BF16),
                   jax.ShapeDtypeStruct((t_rows, LANES), F32)],
        in_specs=[_head_spec(), _x_spec(), pl.BlockSpec((1, D_MODEL), lambda i: (0, 0)), ANY, ANY],
        out_specs=[pl.BlockSpec((None, D_MODEL, TILE), lambda i: (i, 0, 0)), pl.BlockSpec((n_blocks, TILE, LANES), lambda i: (0, i, 0)),
                   pl.BlockSpec((TILE, LANES), lambda i: (i, 0))],
        scratch_shapes=W_SCRATCH(), compiler_params=_params(("arbitrary",)),
    )(head, x, g_norm, slabs, w_glr)


def _ret_decay(lgh):
    i = lax.broadcasted_iota(jnp.int32, (TILE, TILE), 0)
    j = lax.broadcasted_iota(jnp.int32, (TILE, TILE), 1)
    rel = (i - j).astype(F32)
    return jnp.where(rel >= 0, jnp.exp(jnp.maximum(rel, 0.0) * lgh), 0.0)


def _ret_vectors(lgh):
    idx = lax.broadcasted_iota(jnp.int32, (TILE, 1), 0).astype(F32)
    xi = jnp.exp((idx + 1.0) * lgh)
    zeta = jnp.exp((TILE - 1.0 - idx) * lgh)
    gc = jnp.exp(jnp.full((1, 1), float(TILE), F32) * lgh)
    return xi, zeta, gc


def _ret_fwd(proj, cos, sin, gain, lg, row_shards):
    t_rows = cos.shape[0]
    nt = t_rows // TILE
    ns = len(row_shards)

    def body(lg_ref, q_ref, k_ref, v_ref, g_ref, cos_ref, sin_ref, gain_ref, *rest):
        shard_refs, (oraw_ref, oret_ref, st_ref), gathered = rest[:ns], rest[ns:ns + 3], rest[ns + 3:2 * ns + 3]
        s_acc, dm = rest[2 * ns + 3:2 * ns + 5]
        gather = _Exchange(shard_refs, gathered, rest[2 * ns + 5:], among_chips=False)
        h, t = pl.program_id(0), pl.program_id(1)
        lgh = lg_ref[h]

        @pl.when((h == 0) & (t == 0))
        def _():
            gather.start()

        @pl.when((h == RET_HEADS - 1) & (t == nt - 1))
        def _():
            gather.finish()

        @pl.when(t == 0)
        def _():
            s_acc[...] = jnp.zeros_like(s_acc)
            dm[...] = _ret_decay(lgh)

        cos_t, sin_t = cos_ref[...], sin_ref[...]
        q = _rope(_cols(q_ref).astype(F32), cos_t, sin_t)
        k = _rope(_cols(k_ref).astype(F32), cos_t, sin_t) * (RET_QK ** -0.5)
        xi, zeta, gc = _ret_vectors(lgh)
        v = _cols(v_ref)
        s_in = s_acc[...]
        p = (_mm_nt(q.astype(BF16), k.astype(BF16)) * dm[...]).astype(BF16)
        o = _mm(p, v) + _mm((q * xi).astype(BF16), s_in.astype(BF16))
        st_ref[...] = s_in.astype(BF16)
        s_acc[...] = s_in * gc + _mm_tn((k * zeta).astype(BF16), v)
        oraw_ref[...] = o
        oc = o - _row_mean(o)
        n = oc * lax.rsqrt(_row_mean(oc * oc) + EPS) * gain_ref[...]
        g = _cols(g_ref).astype(F32)
        oret_ref[...] = (n * g * _sigmoid(g)).astype(BF16)

    blk = lambda w: pl.BlockSpec((TILE, w), lambda h, t: (t, h))
    tab = pl.BlockSpec((TILE, LANES), lambda h, t: (t, 0))
    outs = _call(
        body, "ret_fwd", grid=(RET_HEADS, nt),
        out_shape=[jax.ShapeDtypeStruct((t_rows, RET_W), F32), jax.ShapeDtypeStruct((t_rows, RET_W), BF16),
                   jax.ShapeDtypeStruct((RET_HEADS, nt, RET_QK, RET_V), BF16)]
                  + [jax.ShapeDtypeStruct((N_DEV, *a.shape), a.dtype) for a in row_shards],
        in_specs=[pl.BlockSpec(memory_space=pltpu.SMEM)] + _proj_specs(("rq", "rk", "rv", "rg"), RET_HEADS, lambda h, t: (h, t)) + [tab, tab,
                  pl.BlockSpec((1, RET_V), lambda h, t: (0, h))] + [ANY] * ns,
        out_specs=[blk(RET_V), blk(RET_V), pl.BlockSpec((None, None, RET_QK, RET_V), lambda h, t: (h, t, 0, 0))] + [ANY] * ns,
        scratch_shapes=[pltpu.VMEM((RET_QK, RET_V), F32), pltpu.VMEM((TILE, TILE), F32)] + _exchange_sems(ns, N_DEV),
        compiler_params=_params(("arbitrary", "arbitrary")),
    )(lg, proj, proj, proj, proj, cos, sin, gain, *row_shards)
    return outs[0], outs[1], outs[2], outs[3:]


def _ret_bwd(proj, cos, sin, gain, lg, o_raw, do_ret, states):
    t_rows = cos.shape[0]
    nt = t_rows // TILE

    def body(lg_ref, q_ref, k_ref, v_ref, g_ref, cos_ref, sin_ref, gain_ref, oraw_ref, do_ref, st_ref,
             dq_ref, dk_ref, dv_ref, dg_ref, dgain_ref, e_acc, dm):
        h, j = pl.program_id(0), pl.program_id(1)
        lgh = lg_ref[h]

        @pl.when(j == 0)
        def _():
            e_acc[...] = jnp.zeros_like(e_acc)
            dm[...] = _ret_decay(lgh)
            dgain_ref[...] = jnp.zeros_like(dgain_ref)

        cos_t, sin_t = cos_ref[...], sin_ref[...]
        q = _rope(_cols(q_ref).astype(F32), cos_t, sin_t)
        k = _rope(_cols(k_ref).astype(F32), cos_t, sin_t) * (RET_QK ** -0.5)
        xi, zeta, gc = _ret_vectors(lgh)
        v = _cols(v_ref)
        g = _cols(g_ref).astype(F32)
        o = oraw_ref[...]
        do = do_ref[...].astype(F32)
        oc = o - _row_mean(o)
        rstd = lax.rsqrt(_row_mean(oc * oc) + EPS)
        xh = oc * rstd
        gain_t = gain_ref[...]
        sg = _sigmoid(g)
        dn = do * (g * sg)
        dg_ref[...] = (do * (xh * gain_t) * (sg * (1.0 + g * (1.0 - sg)))).astype(BF16)
        dgain_ref[...] += _col_sum(dn * xh)
        dxh = dn * gain_t
        dob = (rstd * (dxh - _row_mean(dxh) - xh * _row_mean(dxh * xh))).astype(BF16)
        dmat = dm[...]
        qb, kb = q.astype(BF16), k.astype(BF16)
        p = (_mm_nt(qb, kb) * dmat).astype(BF16)
        dp = (_mm_nt(dob, v) * dmat).astype(BF16)
        s_in = st_ref[...]
        e_in = e_acc[...]
        e_b = e_in.astype(BF16)
        dq = _mm(dp, kb) + _mm_nt(dob, s_in) * xi
        dk = _mm_tn(dp, qb) + _mm_nt(v, e_b) * zeta
        dv_ref[...] = (_mm_tn(p, dob) + _mm((k * zeta).astype(BF16), e_b)).astype(BF16)
        e_acc[...] = e_in * gc + _mm_tn((q * xi).astype(BF16), dob)
        dq_ref[...] = _rope_bwd(dq, cos_t, sin_t).astype(BF16)
        dk_ref[...] = (_rope_bwd(dk, cos_t, sin_t) * (RET_QK ** -0.5)).astype(BF16)

    blk = lambda w: pl.BlockSpec((TILE, w), lambda h, j: (nt - 1 - j, h))
    tab = pl.BlockSpec((TILE, LANES), lambda h, j: (nt - 1 - j, 0))
    vec = pl.BlockSpec((1, RET_V), lambda h, j: (0, h))
    return _call(
        body, "ret_bwd", grid=(RET_HEADS, nt),
        out_shape=[jax.ShapeDtypeStruct((t_rows, RET_HEADS * RET_QK), BF16), jax.ShapeDtypeStruct((t_rows, RET_HEADS * RET_QK), BF16),
                   jax.ShapeDtypeStruct((t_rows, RET_W), BF16), jax.ShapeDtypeStruct((t_rows, RET_W), BF16),
                   jax.ShapeDtypeStruct((1, RET_W), F32)],
        in_specs=[pl.BlockSpec(memory_space=pltpu.SMEM)] + _proj_specs(("rq", "rk", "rv", "rg"), RET_HEADS, lambda h, j: (h, nt - 1 - j)) + [tab, tab, vec,
                  blk(RET_V), blk(RET_V), pl.BlockSpec((None, None, RET_QK, RET_V), lambda h, j: (h, nt - 1 - j, 0, 0))],
        out_specs=[blk(RET_QK), blk(RET_QK), blk(RET_V), blk(RET_V), vec],
        scratch_shapes=[pltpu.VMEM((RET_QK, RET_V), F32), pltpu.VMEM((TILE, TILE), F32)],
        compiler_params=_params(("arbitrary", "arbitrary")),
    )(lg, proj, proj, proj, proj, cos, sin, gain, o_raw, do_ret, states)


GLA_LEVELS = (32, 64, 128, 256)
N_TERMS = 1 + len(GLA_LEVELS)


def _gla_tables():
    p = jnp.arange(TILE)[:, None]
    r = jnp.arange(TILE)[None, :]
    masks = [(p // GLA_CHUNK == r // GLA_CHUNK) & (r <= p)]
    for blk in GLA_LEVELS:
        masks.append((p // blk == r // blk) & (p % blk >= blk // 2) & (r % blk < blk // 2))
    masks = jnp.stack(masks + [m.T for m in masks]).astype(F32)
    cum_fwd = jnp.concatenate([r <= p, masks[0] > 0], axis=0).astype(BF16)
    cum_bwd = jnp.concatenate([r >= p, masks[N_TERMS] > 0], axis=1).astype(BF16)
    return masks, cum_fwd, cum_bwd


def _split3(x):
    hi = x.astype(BF16)
    rest = x - hi.astype(F32)
    mid = rest.astype(BF16)
    lo = (rest - mid.astype(F32)).astype(BF16)
    return jnp.concatenate([hi, mid, lo], axis=1)


def _join3(y):
    w = y.shape[1] // 3
    return (y[:, 2 * w:] + y[:, w:2 * w]) + y[:, :w]


def _gla_prep(q_ref, k_ref, glr_ref, wgu_ref, b_ref, cum_ref, g_scr, ref_scr):
    z = _mm(glr_ref[...].astype(BF16), wgu_ref[...].astype(BF16)) + b_ref[...]
    la = (jnp.minimum(z, 0.0) - jnp.log(1.0 + jnp.exp(-jnp.abs(z)))) / GLA_TAU
    gb = _join3(_mm(cum_ref[...], _split3(la)))
    g, b = gb[:TILE], gb[TILE:]
    g_scr[...] = g
    factors = [(jnp.exp(b), jnp.exp(-b))]
    for lvl, blk in enumerate(GLA_LEVELS):
        for n in range(TILE // blk):
            ref_scr[lvl, n * blk:(n + 1) * blk, :] = jnp.broadcast_to(g_scr[pl.ds(n * blk + blk // 2 - 1, 1), :], (blk, GLA_K))
        x = g - ref_scr[lvl]
        factors.append((jnp.exp(jnp.minimum(x, 0.0)), jnp.exp(jnp.minimum(-x, 0.0))))
    g_last = g_scr[pl.ds(TILE - 1, 1), :]
    q = _cols(q_ref).astype(F32) * (GLA_K ** -0.5)
    k = _cols(k_ref).astype(F32)
    return z, q, k, factors, jnp.exp(g), jnp.exp(g_last), jnp.exp(g_last - g)


def _gla_scores(q, k, factors, m_ref):
    a = jnp.zeros((TILE, TILE), F32)
    for l, (fq, fk) in enumerate(factors):
        s = _mm_nt((q * fq).astype(BF16), (k * fk).astype(BF16))
        a = jnp.where(m_ref[l] > 0.0, s, a)
    return a


def _gla_fwd(proj, glr, wgu_pad, b_gate, gain, masks, cum_fwd):
    t_rows = glr.shape[0]
    nt = t_rows // TILE

    def body(q_ref, k_ref, v_ref, g_ref, glr_ref, wgu_ref, b_ref, gain_ref, m_ref, cum_ref, oraw_ref, ogla_ref, st_ref,
             s_acc, g_scr, ref_scr):
        @pl.when(pl.program_id(1) == 0)
        def _():
            s_acc[...] = jnp.zeros_like(s_acc)

        _, q, k, factors, e_g, e_last, e_end = _gla_prep(q_ref, k_ref, glr_ref, wgu_ref, b_ref, cum_ref, g_scr, ref_scr)
        v = _cols(v_ref)
        st = s_acc[...]
        st_ref[...] = st
        a = _gla_scores(q, k, factors, m_ref)
        o = _mm(a.astype(BF16), v) + _mm_nt((q * e_g).astype(BF16), st.astype(BF16))
        s_acc[...] = st * e_last + _mm(v.astype(F32).T.astype(BF16), (k * e_end).astype(BF16))
        oraw_ref[...] = o
        n = o * lax.rsqrt(_row_mean(o * o) + EPS) * gain_ref[...]
        g = _cols(g_ref).astype(F32)
        ogla_ref[...] = (n * g * _sigmoid(g)).astype(BF16)

    blk = lambda w: pl.BlockSpec((TILE, w), lambda h, t: (t, h))
    return _call(
        body, "gla_fwd", grid=(GLA_HEADS, nt),
        out_shape=[jax.ShapeDtypeStruct((t_rows, GLA_W), F32), jax.ShapeDtypeStruct((t_rows, GLA_W), BF16),
                   jax.ShapeDtypeStruct((GLA_HEADS, nt, GLA_V, GLA_K), F32)],
        in_specs=_proj_specs(("gq", "gk", "gv", "gg"), GLA_HEADS, lambda h, t: (h, t)) + [pl.BlockSpec((TILE, LANES), lambda h, t: (t, 0)),
                  pl.BlockSpec((LANES, GLA_K), lambda h, t: (0, h)), pl.BlockSpec((1, GLA_K), lambda h, t: (0, h)),
                  pl.BlockSpec((1, GLA_V), lambda h, t: (0, h)),
                  pl.BlockSpec((N_TERMS, TILE, TILE), lambda h, t: (0, 0, 0)), pl.BlockSpec((2 * TILE, TILE), lambda h, t: (0, 0))],
        out_specs=[blk(GLA_V), blk(GLA_V), pl.BlockSpec((None, None, GLA_V, GLA_K), lambda h, t: (h, t, 0, 0))],
        scratch_shapes=[pltpu.VMEM((GLA_V, GLA_K), F32), pltpu.VMEM((TILE, GLA_K), F32),
                        pltpu.VMEM((len(GLA_LEVELS), TILE, GLA_K), F32)],
        compiler_params=_params(("arbitrary", "arbitrary")),
    )(proj, proj, proj, proj, glr, wgu_pad, b_gate, gain, masks, cum_fwd)


def _gla_bwd(proj, glr, wgu_pad, b_gate, gain, o_raw, do_gla, states, masks, cum_fwd, cum_bwd):
    t_rows = glr.shape[0]
    nt = t_rows // TILE

    def body(q_ref, k_ref, v_ref, g_ref, glr_ref, wgu_ref, b_ref, gain_ref, m_ref, cum_ref, cumb_ref, oraw_ref, do_ref, st_ref,
             dq_ref, dk_ref, dv_ref, dg_ref, dglr_ref, dwgu_ref, dbg_ref, dgain_ref, d_acc, g_scr, ref_scr, dref_scr):
        @pl.when(pl.program_id(1) == 0)
        def _():
            d_acc[...] = jnp.zeros_like(d_acc)
            dwgu_ref[...] = jnp.zeros_like(dwgu_ref)
            dbg_ref[...] = jnp.zeros_like(dbg_ref)
            dgain_ref[...] = jnp.zeros_like(dgain_ref)

        z, q, k, factors, e_g, e_last, e_end = _gla_prep(q_ref, k_ref, glr_ref, wgu_ref, b_ref, cum_ref, g_scr, ref_scr)
        v = _cols(v_ref)
        o = oraw_ref[...]
        do = do_ref[...].astype(F32)
        g = _cols(g_ref).astype(F32)
        rinv = lax.rsqrt(_row_mean(o * o) + EPS)
        nh = o * rinv
        gain_t = gain_ref[...]
        sg = _sigmoid(g)
        dn = do * (g * sg)
        dg_ref[...] = (do * (nh * gain_t) * (sg * (1.0 + g * (1.0 - sg)))).astype(BF16)
        dgain_ref[...] += _col_sum(dn * nh)
        dnh = dn * gain_t
        dor = rinv * (dnh - nh * _row_mean(dnh * nh))
        dob = dor.astype(BF16)
        a_t = _gla_scores(q, k, factors, m_ref).T.astype(BF16)
        da = _mm_nt(dob, v)
        da_t = _mm_nt(v, dob)
        st_in = st_ref[...]
        d_out = d_acc[...]
        d_out_b = d_out.astype(BF16)
        qg, kg = q * e_g, k * e_end
        dqg = _mm(dob, st_in.astype(BF16))
        dkg = _mm(v, d_out_b)
        dv_ref[...] = (_mm(a_t, dob) + _mm_nt(kg.astype(BF16), d_out_b)).astype(BF16)
        d_acc[...] = d_out * e_last + _mm(dor.T.astype(BF16), qg.astype(BF16))
        dq = dqg * e_g
        dk = dkg * e_end
        dkg_kg = dkg * kg
        dg_cum = dqg * qg - dkg_kg
        db = None
        for l, (fq, fk) in enumerate(factors):
            qt, kt = q * fq, k * fk
            dqt = _mm(jnp.where(m_ref[l] > 0.0, da, 0.0).astype(BF16), kt.astype(BF16))
            dkt = _mm(jnp.where(m_ref[N_TERMS + l] > 0.0, da_t, 0.0).astype(BF16), qt.astype(BF16))
            dq = dq + dqt * fq
            dk = dk + dkt * fk
            diff = dqt * qt - dkt * kt
            if l == 0:
                db = diff
            else:
                dg_cum = dg_cum + diff
                dref_scr[l - 1] = diff
        dq_ref[...] = (dq * (GLA_K ** -0.5)).astype(BF16)
        dk_ref[...] = dk.astype(BF16)
        g_scr[...] = dg_cum
        g_scr[pl.ds(TILE - 1, 1), :] += e_last * _col_sum(d_out * st_in) + _col_sum(dkg_kg)
        for lvl, blk in enumerate(GLA_LEVELS):
            for n in range(TILE // blk):
                g_scr[pl.ds(n * blk + blk // 2 - 1, 1), :] -= _col_sum(dref_scr[lvl, n * blk:(n + 1) * blk, :])
        dla = _join3(_mm(cumb_ref[...], jnp.concatenate([_split3(g_scr[...]), _split3(db)], axis=0)))
        dz = dla * (1.0 / GLA_TAU) * _sigmoid(-z)
        dzb = dz.astype(BF16)
        dglr_ref[...] = _mm_nt(dzb, wgu_ref[...].astype(BF16)).astype(BF16)
        dwgu_ref[...] += _mm(glr_ref[...].T.astype(BF16), dzb)
        dbg_ref[...] += _col_sum(dz)

    blk = lambda w: pl.BlockSpec((TILE, w), lambda h, j: (nt - 1 - j, h))
    vec = lambda w: pl.BlockSpec((1, w), lambda h, j: (0, h))
    wspec = pl.BlockSpec((LANES, GLA_K), lambda h, j: (0, h))
    return _call(
        body, "gla_bwd", grid=(GLA_HEADS, nt),
        out_shape=[jax.ShapeDtypeStruct((t_rows, GLA_HEADS * GLA_K), BF16), jax.ShapeDtypeStruct((t_rows, GLA_HEADS * GLA_K), BF16),
                   jax.ShapeDtypeStruct((t_rows, GLA_W), BF16), jax.ShapeDtypeStruct((t_rows, GLA_W), BF16),
                   jax.ShapeDtypeStruct((GLA_HEADS, t_rows, LANES), BF16), jax.ShapeDtypeStruct((LANES, GLA_HEADS * GLA_K), F32),
                   jax.ShapeDtypeStruct((1, GLA_HEADS * GLA_K), F32), jax.ShapeDtypeStruct((1, GLA_W), F32)],
        in_specs=_proj_specs(("gq", "gk", "gv", "gg"), GLA_HEADS, lambda h, j: (h, nt - 1 - j)) + [pl.BlockSpec((TILE, LANES), lambda h, j: (nt - 1 - j, 0)),
                  wspec, vec(GLA_K), vec(GLA_V),
                  pl.BlockSpec((2 * N_TERMS, TILE, TILE), lambda h, j: (0, 0, 0)), pl.BlockSpec((2 * TILE, TILE), lambda h, j: (0, 0)),
                  pl.BlockSpec((TILE, 2 * TILE), lambda h, j: (0, 0)), blk(GLA_V), blk(GLA_V),
                  pl.BlockSpec((None, None, GLA_V, GLA_K), lambda h, j: (h, nt - 1 - j, 0, 0))],
        out_specs=[blk(GLA_K), blk(GLA_K), blk(GLA_V), blk(GLA_V),
                   pl.BlockSpec((None, TILE, LANES), lambda h, j: (h, nt - 1 - j, 0)), wspec, vec(GLA_K), vec(GLA_V)],
        scratch_shapes=[pltpu.VMEM((GLA_V, GLA_K), F32), pltpu.VMEM((TILE, GLA_K), F32),
                        pltpu.VMEM((len(GLA_LEVELS), TILE, GLA_K), F32), pltpu.VMEM((len(GLA_LEVELS), TILE, GLA_K), F32)],
        compiler_params=_params(("arbitrary", "arbitrary")),
    )(proj, proj, proj, proj, glr, wgu_pad, b_gate, gain, masks, cum_fwd, cum_bwd, o_raw, do_gla, states)


def _merge_fwd_bwd(o_ret, o_gla, proj, x, target, g_final, w_br, w_bg, w_out):
    t_rows = x.shape[0] + TILE
    nt = t_rows // TILE

    def body(oret_ref, ogla_ref, mr_ref, mg_ref, h0_ref, tgt_ref, gf_ref, wbr_hbm, wbg_hbm, wout_hbm,
             dh1_ref, dmr_ref, dmg_ref, doret_ref, dogla_ref, loss_ref, dgf_ref, dwbr_hbm, dwbg_hbm, dwout_hbm,
             wbr, wbg, wout, abr, abg, aout, sem):
        i = pl.program_id(0)

        @pl.when(i == 0)
        def _():
            cps = [pltpu.make_async_copy(s, d, sem.at[n]) for n, (s, d) in enumerate(((wbr_hbm, wbr), (wbg_hbm, wbg), (wout_hbm, wout)))]
            for cp in cps:
                cp.start()
            abr[...] = jnp.zeros_like(abr)
            abg[...] = jnp.zeros_like(abg)
            aout[...] = jnp.zeros_like(aout)
            loss_ref[...] = jnp.zeros_like(loss_ref)
            dgf_ref[...] = jnp.zeros_like(dgf_ref)
            for cp in cps:
                cp.wait()
            dh1_ref[...] = jnp.zeros_like(dh1_ref)
            dmr_ref[...] = jnp.zeros_like(dmr_ref)
            dmg_ref[...] = jnp.zeros_like(dmg_ref)
            doret_ref[...] = jnp.zeros_like(doret_ref)
            dogla_ref[...] = jnp.zeros_like(dogla_ref)

        @pl.when(i > 0)
        def _():
            oret, ogla = oret_ref[...], ogla_ref[...]
            br, bg = _mm(oret, wbr[...]), _mm(ogla, wbg[...])
            sr, sg = _sigmoid(_cols(mr_ref).astype(F32)), _sigmoid(_cols(mg_ref).astype(F32))
            mb = (sr * br + sg * bg).astype(BF16)
            h1 = h0_ref[...] + _mm(mb, wout[...])
            r2 = lax.rsqrt(_row_mean(h1 * h1) + EPS)
            hn = h1 * r2
            gf = gf_ref[...]
            diff = hn * gf - tgt_ref[...]
            loss_ref[...] += 0.5 * jnp.sum(_row_mean(diff * diff))
            dy = diff * (1.0 / D_MODEL)
            dgf_ref[...] += _col_sum(dy * hn)
            dyg = dy * gf
            dh1 = r2 * (dyg - hn * _row_mean(dyg * hn))
            dh1_ref[...] = dh1
            dh1b = dh1.astype(BF16)
            dm = _mm_nt(dh1b, wout[...])
            aout[...] += _mm_tn(mb, dh1b)
            dbr = (dm * sr).astype(BF16)
            dbg = (dm * sg).astype(BF16)
            dmr_ref[...] = (dm * br * sr * (1.0 - sr)).astype(BF16)
            dmg_ref[...] = (dm * bg * sg * (1.0 - sg)).astype(BF16)
            doret_ref[...] = _mm_nt(dbr, wbr[...]).astype(BF16)
            dogla_ref[...] = _mm_nt(dbg, wbg[...]).astype(BF16)
            abr[...] += _mm_tn(oret, dbr)
            abg[...] += _mm_tn(ogla, dbg)

        @pl.when(i == nt - 1)
        def _():
            wbr[...] = abr[...].astype(BF16)
            wbg[...] = abg[...].astype(BF16)
            wout[...] = aout[...].astype(BF16)
            pltpu.sync_copy(wbr, dwbr_hbm)
            pltpu.sync_copy(wbg, dwbg_hbm)
            pltpu.sync_copy(wout, dwout_hbm)

    row = lambda w: pl.BlockSpec((TILE, w), lambda i: (i, 0))
    one = lambda w: pl.BlockSpec((1, w), lambda i: (0, 0))
    return _call(
        body, "merge_fwd_bwd", grid=(nt,),
        out_shape=[jax.ShapeDtypeStruct((t_rows, D_MODEL), F32), jax.ShapeDtypeStruct((t_rows, D_MODEL), BF16),
                   jax.ShapeDtypeStruct((t_rows, D_MODEL), BF16), jax.ShapeDtypeStruct((t_rows, RET_W), BF16),
                   jax.ShapeDtypeStruct((t_rows, GLA_W), BF16), jax.ShapeDtypeStruct((1, LANES), F32),
                   jax.ShapeDtypeStruct((1, D_MODEL), F32), jax.ShapeDtypeStruct((RET_W, D_MODEL), BF16),
                   jax.ShapeDtypeStruct((GLA_W, D_MODEL), BF16), jax.ShapeDtypeStruct((D_MODEL, D_MODEL), BF16)],
        in_specs=[row(RET_W), row(GLA_W)] + _proj_specs(("mr", "mg"), 1, lambda i: (0, i)) + [_x_spec(), _x_spec(), one(D_MODEL), ANY, ANY, ANY],
        out_specs=[row(D_MODEL), row(D_MODEL), row(D_MODEL), row(RET_W), row(GLA_W), one(LANES), one(D_MODEL), ANY, ANY, ANY],
        scratch_shapes=[pltpu.VMEM((RET_W, D_MODEL), BF16), pltpu.VMEM((GLA_W, D_MODEL), BF16), pltpu.VMEM((D_MODEL, D_MODEL), BF16),
                        pltpu.VMEM((RET_W, D_MODEL), F32), pltpu.VMEM((GLA_W, D_MODEL), F32), pltpu.VMEM((D_MODEL, D_MODEL), F32),
                        pltpu.SemaphoreType.DMA((3,))],
        compiler_params=_params(("arbitrary",)),
    )(o_ret, o_gla, proj, proj, x, target, g_final, w_br, w_bg, w_out)


def _inproj_bwd_x(dseg, dglr, head, x, dh1, g_norm, slabs, w_glr, chip_partials):
    t_rows = x.shape[0] + TILE
    nt = t_rows // TILE
    ne = len(chip_partials)

    def body(*refs):
        d_refs = refs[:10]
        dglr_ref, head_ref, x_ref, dh1_ref, g_ref, slabs_hbm, wg_hbm = refs[10:17]
        part_refs = refs[17:17 + ne]
        dx_ref, dhead_ref, dgn_ref = refs[17 + ne:20 + ne]
        landed = refs[20 + ne:20 + 2 * ne]
        w_vm, wg_vm, edge_vm, sem = refs[20 + 2 * ne:24 + 2 * ne]
        exchange = _Exchange(part_refs, landed, refs[24 + 2 * ne:], among_chips=True)

        @pl.when(pl.program_id(0) == 0)
        def _():
            exchange.start()
            dgn_ref[...] = jnp.zeros_like(dgn_ref)
            _load_weight(slabs_hbm, wg_hbm, w_vm, wg_vm, edge_vm, sem)

        @pl.when(pl.program_id(0) == nt - 1)
        def _():
            exchange.finish()

        dglr = dglr_ref[0].astype(F32)
        for h in range(1, GLA_HEADS):
            dglr = dglr + dglr_ref[h].astype(F32)
        du = _mm_nt(dglr.astype(BF16), wg_vm[...])
        for s, d_ref in enumerate(d_refs):
            du = du + _mm_nt(d_ref[...], w_vm[:, SEG_OFF[s]:SEG_OFF[s] + SEG_W[s]])
        x = _tile_rows(head_ref, x_ref)
        r = lax.rsqrt(_row_mean(x * x) + EPS)
        hn = x * r
        dgn_ref[...] += _col_sum(du * hn)
        dug = du * g_ref[...]
        dh0 = dh1_ref[...] + r * (dug - hn * _row_mean(dug * hn))
        dx_ref[...] = dh0

        @pl.when(pl.program_id(0) == 0)
        def _():
            dhead_ref[...] = dh0

    row = lambda w: pl.BlockSpec((TILE, w), lambda i: (i, 0))
    one = pl.BlockSpec((1, D_MODEL), lambda i: (0, 0))
    return _call(
        body, "inproj_bwd_x", grid=(nt,),
        out_shape=[jax.ShapeDtypeStruct((t_rows - TILE, D_MODEL), F32), jax.ShapeDtypeStruct((TILE, D_MODEL), F32),
                   jax.ShapeDtypeStruct((1, D_MODEL), F32)] + [jax.ShapeDtypeStruct(a.shape, a.dtype) for a in chip_partials],
        in_specs=[row(w) for w in SEG_W] + [pl.BlockSpec((GLA_HEADS, TILE, LANES), lambda i: (0, i, 0)),
                                            _head_spec(), _x_spec(), row(D_MODEL), one, ANY, ANY] + [ANY] * ne,
        out_specs=[_x_spec(), _head_spec(), one] + [ANY] * ne,
        scratch_shapes=W_SCRATCH() + _exchange_sems(ne, N_CHIP),
        compiler_params=_params(("arbitrary",)),
    )(*[dseg[n] for n in SEG_NAMES], dglr, head, x, dh1, g_norm, slabs, w_glr, *chip_partials)


W_TILE = 512


def _inproj_bwd_w(ut, dseg, dglr):
    nt = ut.shape[0]
    t_rows = nt * TILE
    kc = 3 if nt % 3 == 0 else 1
    tiles = [(s, c) for s in range(len(SEG_W)) for c in range(0, SEG_W[s], W_TILE)]
    bpt = W_TILE // LANES

    def body(ut_hbm, *refs):
        d_refs, dglr_hbm, out_hbm, oglr_ref = refs[:10], refs[10], refs[11], refs[12]
        ut_vm, dbuf, obuf, acc, gbuf, sem = refs[13:]

        def fetch(i):
            s, c = tiles[i]
            return pltpu.make_async_copy(d_refs[s].at[:, pl.ds(c, W_TILE)], dbuf.at[i % 2], sem.at[1 + i % 2])

        def contract(rhs_refs, width):
            acc[:, :width] = jnp.zeros((D_MODEL, width), F32)

            def step(k, carry):
                part = None
                for j in range(kc):
                    kk = k * kc + j
                    for rhs_ref in rhs_refs:
                        prod = _mm(ut_vm[kk], rhs_ref[pl.ds(pl.multiple_of(kk * TILE, TILE), TILE), :])
                        part = prod if part is None else part + prod
                acc[:, :width] += part
                return carry

            lax.fori_loop(0, nt // kc, step, 0)
            return acc[:, :width]

        load_ut = pltpu.make_async_copy(ut_hbm, ut_vm, sem.at[0])
        load_glr = pltpu.make_async_copy(dglr_hbm, gbuf, sem.at[5])
        load_ut.start()
        load_glr.start()
        fetch(0).start()
        load_ut.wait()
        stores = {}
        for i, (s, c) in enumerate(tiles):
            if i + 1 < len(tiles):
                fetch(i + 1).start()
            fetch(i).wait()
            if i >= 2:
                stores[i - 2].wait()
            total = contract([dbuf.at[i % 2]], W_TILE)
            for j in range(bpt):
                obuf[i % 2, j] = total[:, j * LANES:(j + 1) * LANES].astype(BF16)
            blk0 = (SEG_OFF[s] + c) // LANES
            stores[i] = pltpu.make_async_copy(obuf.at[i % 2], out_hbm.at[pl.ds(blk0, bpt)], sem.at[3 + i % 2])
            stores[i].start()
        load_glr.wait()
        oglr_ref[...] = contract([gbuf.at[h] for h in range(GLA_HEADS)], LANES)
        for i in range(max(0, len(tiles) - 2), len(tiles)):
            stores[i].wait()

    return _call(
        body, "inproj_bwd_w",
        out_shape=[jax.ShapeDtypeStruct((AL_COLS // LANES, D_MODEL, LANES), BF16), jax.ShapeDtypeStruct((D_MODEL, LANES), F32)],
        in_specs=[ANY] * 12, out_specs=[ANY, pl.BlockSpec(memory_space=pltpu.VMEM)],
        scratch_shapes=[pltpu.VMEM((nt, D_MODEL, TILE), BF16), pltpu.VMEM((2, t_rows, W_TILE), BF16),
                        pltpu.VMEM((2, bpt, D_MODEL, LANES), BF16), pltpu.VMEM((D_MODEL, W_TILE), F32),
                        pltpu.VMEM((GLA_HEADS, t_rows, LANES), BF16), pltpu.SemaphoreType.DMA((6,))],
        compiler_params=_params(),
    )(ut, *[dseg[n] for n in SEG_NAMES], dglr)


def _position():
    x, y, c = lax.axis_index("x"), lax.axis_index("y"), lax.axis_index("c")
    return x, y, c


def _index(px, py, pc):
    return 4 * px + 2 * py + pc


def _all_gather(arrs, name):
    n = len(arrs)

    def body(*refs):
        ins, outs = refs[:n], refs[n:2 * n]
        send_sems, recv_sems, local_sems = refs[2 * n:]
        x, y, c = _position()
        me, sibling = (x, y, c), (x, y, 1 - c)
        chips = [(1 - x, y), (x, 1 - y), (1 - x, 1 - y)]

        def copy(a, k, block, to, src=None):
            dst = outs[a].at[_index(*block)]
            return pltpu.make_async_remote_copy(src_ref=dst if src is None else src, dst_ref=dst,
                                                send_sem=send_sems.at[7 * a + k], recv_sem=recv_sems.at[7 * a + k],
                                                device_id=to, device_id_type=MESH)

        mine = [pltpu.make_async_copy(ins[a], outs[a].at[_index(*me)], local_sems.at[a]) for a in range(n)]
        for cp in mine:
            cp.start()
        first = []
        for a in range(n):
            first.append(copy(a, 0, me, sibling, src=ins[a]))
            first += [copy(a, 1 + j, me, (*chip, c), src=ins[a]) for j, chip in enumerate(chips)]
        for cp in first:
            cp.start()
        passed = []
        for j, chip in enumerate(chips):
            for a in range(n):
                copy(a, 1 + j, (*chip, c), me).wait_recv()
                cp = copy(a, 4 + j, (*chip, c), sibling)
                cp.start()
                passed.append(cp)
        for a in range(n):
            copy(a, 0, sibling, me).wait_recv()
            for j, chip in enumerate(chips):
                copy(a, 4 + j, (*chip, 1 - c), me).wait_recv()
        for cp in first + passed:
            cp.wait_send()
        for cp in mine:
            cp.wait()

    return _call(
        body, name,
        out_shape=[jax.ShapeDtypeStruct((N_DEV, *a.shape), a.dtype) for a in arrs],
        in_specs=[ANY] * n, out_specs=[ANY] * n,
        scratch_shapes=[pltpu.SemaphoreType.DMA((7 * n,)), pltpu.SemaphoreType.DMA((7 * n,)), pltpu.SemaphoreType.DMA((n,))],
    )(*arrs)


N_CHIP = N_DEV // 2


def _slab_block0(owner):
    step = SLAB_BLK0[1]
    assert all(SLAB_BLK0[d] == step * d - (d == N_DEV - 1) for d in range(N_DEV))
    return step * owner - jnp.where(owner == N_DEV - 1, 1, 0)


def _exchange_sibling(dw_blocks, row_sends):
    n = 1 + len(row_sends)

    def body(*refs):
        dw_ref, row_refs, outs, (send_sems, recv_sems) = refs[0], refs[1:n], refs[n:2 * n], refs[2 * n:]
        x, y, c = _position()
        copies = []
        for q in range(N_CHIP):
            owner = 2 * q + (1 - c)
            srcs = [dw_ref.at[pl.ds(_slab_block0(owner), SLAB_BLOCKS)]] + [r.at[owner] for r in row_refs]
            for k, src in enumerate(srcs):
                copies.append(pltpu.make_async_remote_copy(src_ref=src, dst_ref=outs[k].at[q], send_sem=send_sems.at[n * q + k],
                                                           recv_sem=recv_sems.at[n * q + k], device_id=(x, y, 1 - c), device_id_type=MESH))
        for cp in copies:
            cp.start()
        for cp in copies:
            cp.wait()

    return _call(
        body, "exchange_sibling",
        out_shape=[jax.ShapeDtypeStruct((N_CHIP, SLAB_BLOCKS, D_MODEL, LANES), BF16)]
                  + [jax.ShapeDtypeStruct((N_CHIP, *r.shape[1:]), BF16) for r in row_sends],
        in_specs=[ANY] * n, out_specs=[ANY] * n,
        scratch_shapes=[pltpu.SemaphoreType.DMA((n * N_CHIP,)), pltpu.SemaphoreType.DMA((n * N_CHIP,))],
    )(dw_blocks, *row_sends)


def _add_bf16(c_ref, a_ref, b_ref, o_ref):
    o_ref[...] = (a_ref[...].astype(F32) + b_ref[...].astype(F32)).astype(BF16)


def _chip_partial_slab(dw_blocks, sib, core):
    blk = pl.BlockSpec((None, SLAB_BLOCKS, D_MODEL, LANES), lambda q, c_ref: (q, 0, 0, 0))
    return _call(
        functools.partial(_add_bf16), "chip_partial_w_in", out_shape=jax.ShapeDtypeStruct(sib.shape, BF16),
        grid_spec=pltpu.PrefetchScalarGridSpec(
            num_scalar_prefetch=1, grid=(N_CHIP,),
            in_specs=[pl.BlockSpec((pl.Element(SLAB_BLOCKS), pl.Element(D_MODEL), pl.Element(LANES)),
                                   lambda q, c_ref: (_slab_block0(2 * q + c_ref[0]), 0, 0)), blk],
            out_specs=blk),
        compiler_params=_params(("arbitrary",)),
    )(core, dw_blocks, sib)


def _chip_partial_rows(send, sib, core, name):
    rows, cols = send.shape[1:]
    blk = pl.BlockSpec((None, rows, cols), lambda q, c_ref: (q, 0, 0))
    return _call(
        functools.partial(_add_bf16), name, out_shape=jax.ShapeDtypeStruct(sib.shape, BF16),
        grid_spec=pltpu.PrefetchScalarGridSpec(
            num_scalar_prefetch=1, grid=(N_CHIP,),
            in_specs=[pl.BlockSpec((None, rows, cols), lambda q, c_ref: (2 * q + c_ref[0], 0, 0)), blk], out_specs=blk),
        compiler_params=_params(("arbitrary",)),
    )(core, send, sib)


def _exchange_sems(n_arrays, n_peers):
    return [pltpu.SemaphoreType.DMA((n_arrays * n_peers,)), pltpu.SemaphoreType.DMA((n_arrays * n_peers,)),
            pltpu.SemaphoreType.DMA((n_arrays,))]


class _Exchange:
    def __init__(self, srcs, dsts, sems, among_chips):
        self.arrs = list(zip(srcs, dsts))
        self.n = len(self.arrs)
        self.send_sems, self.recv_sems, self.local_sems = sems
        self.among_chips = among_chips
        x, y, c = _position()
        self.c = c
        self.me = 2 * x + y if among_chips else _index(x, y, c)
        self.n_peers = N_CHIP if among_chips else N_DEV

    def _device(self, p):
        return (p // 2, p % 2, self.c) if self.among_chips else (p // 4, (p // 2) % 2, p % 2)

    def _src(self, k, p):
        src = self.arrs[k][0]
        return src.at[p] if self.among_chips else src

    def _mine(self):
        return [pltpu.make_async_copy(self._src(k, self.me), self.arrs[k][1].at[self.me], self.local_sems.at[k]) for k in range(self.n)]

    def _copy(self, p, k, landing):
        return pltpu.make_async_remote_copy(
            src_ref=self._src(k, p), dst_ref=self.arrs[k][1].at[landing], send_sem=self.send_sems.at[self.n * p + k],
            recv_sem=self.recv_sems.at[self.n * landing + k], device_id=self._device(p), device_id_type=MESH)

    def _others(self, fn):
        for p in range(self.n_peers):
            @pl.when(p != self.me)
            def _():
                for k in range(self.n):
                    fn(p, k)

    def start(self):
        for cp in self._mine():
            cp.start()
        self._others(lambda p, k: self._copy(p, k, self.me).start())

    def finish(self):
        self._others(lambda p, k: self._copy(p, k, p).wait_recv())
        self._others(lambda p, k: self._copy(p, k, self.me).wait_send())
        for cp in self._mine():
            cp.wait()


def _adamw(g, w, m, v):
    m_new = ADAM_B1 * m + (1.0 - ADAM_B1) * g
    v_new = ADAM_B2 * v + (1.0 - ADAM_B2) * (g * g)
    m_hat = m_new / (1.0 - ADAM_B1 ** ADAM_STEP)
    v_hat = v_new / (1.0 - ADAM_B2 ** ADAM_STEP)
    delta = -ADAM_LR * (m_hat / (jnp.sqrt(v_hat) + ADAM_EPS) + ADAM_WD * w)
    return delta, m_new, v_new


def _sum_partials(p_ref):
    g = p_ref[0].astype(F32)
    for d in range(1, p_ref.shape[0]):
        g = g + p_ref[d].astype(F32)
    return g


def _reduce_adam(parts, w, m, v, name, block_rows, row_off=0):
    rows, cols = w.shape
    off = row_off // block_rows

    def body(p_ref, w_ref, m_ref, v_ref, g_ref, d_ref, mo_ref, vo_ref):
        g = _sum_partials(p_ref)
        g_ref[...] = g
        d_ref[...], mo_ref[...], vo_ref[...] = _adamw(g, w_ref[...], m_ref[...], v_ref[...])

    blk = pl.BlockSpec((block_rows, cols), lambda i: (i, 0))
    return _call(
        body, name, grid=(rows // block_rows,),
        out_shape=[jax.ShapeDtypeStruct((rows, cols), F32)] * 4,
        in_specs=[pl.BlockSpec((parts.shape[0], block_rows, cols), lambda i: (0, i + off, 0)), blk, blk, blk],
        out_specs=[blk] * 4,
        compiler_params=_params(("arbitrary",)),
    )(parts, w, m, v)


def _reduce_adam_slab(parts, glr, w, m, v, me):
    rows, cols = w.shape
    shift = jnp.asarray(SLAB_SHIFT, jnp.int32)[me]
    glr_at = jnp.where(me == GLR_DEV, GLR_LOCAL, cols).astype(jnp.int32)

    def body(s_ref, p_ref, glr_ref, w_ref, m_ref, v_ref, g_ref, d_ref, mo_ref, vo_ref):
        shift, glr_at = s_ref[0], s_ref[1]
        slab = jnp.concatenate([_sum_partials(p_ref.at[:, j]) for j in range(SLAB_BLOCKS)], axis=1)
        before = pltpu.roll(slab, SLAB_W - shift, 1)
        after = pltpu.roll(slab, lax.rem(SLAB_W - shift + GLA_RANK, SLAB_W), 1)
        wide = jnp.concatenate([glr_ref[...], jnp.zeros((LANES, SLAB_W - LANES), F32)], axis=1)
        placed = pltpu.roll(wide, lax.rem(glr_at, SLAB_W), 1)
        lane = lax.broadcasted_iota(jnp.int32, (LANES, SLAB_W), 1)
        g = jnp.where(lane < glr_at, before, jnp.where(lane < glr_at + GLA_RANK, placed, after))[:, :cols]
        g_ref[...] = g
        d_ref[...], mo_ref[...], vo_ref[...] = _adamw(g, w_ref[...], m_ref[...], v_ref[...])

    blk = pl.BlockSpec((LANES, cols), lambda i, s: (i, 0))
    return _call(
        body, "adam_w_in", out_shape=[jax.ShapeDtypeStruct((rows, cols), F32)] * 4,
        grid_spec=pltpu.PrefetchScalarGridSpec(
            num_scalar_prefetch=1, grid=(rows // LANES,),
            in_specs=[pl.BlockSpec((parts.shape[0], SLAB_BLOCKS, LANES, LANES), lambda i, s: (0, 0, i, 0)),
                      pl.BlockSpec((LANES, LANES), lambda i, s: (i, 0)), blk, blk, blk],
            out_specs=[blk] * 4),
        compiler_params=_params(("arbitrary",)),
    )(jnp.stack([shift, glr_at]), parts, glr, w, m, v)


def _reduce_small(parts):
    def body(p_ref, o_ref):
        o_ref[...] = _sum_partials(p_ref)

    return _call(body, "reduce_small", out_shape=jax.ShapeDtypeStruct(parts.shape[1:], F32))(parts)


def _adam_small(g, w, m, v):
    def body(g_ref, w_ref, m_ref, v_ref, d_ref, mo_ref, vo_ref):
        d_ref[...], mo_ref[...], vo_ref[...] = _adamw(g_ref[...], w_ref[...], m_ref[...], v_ref[...])

    return _call(body, "adam_small", out_shape=[jax.ShapeDtypeStruct(g.shape, F32)] * 3)(g, w, m, v)


def _pack_rows(arrs):
    rows = []
    for a in arrs:
        flat = a.reshape(-1).astype(F32)
        pad = (-flat.shape[0]) % LANES
        rows.append(jnp.pad(flat, (0, pad)).reshape(-1, LANES))
    packed = jnp.concatenate(rows, axis=0)
    return jnp.pad(packed, ((0, (-packed.shape[0]) % 8), (0, 0)))


def _unpack_rows(packed, shapes):
    out, r = [], 0
    for shp in shapes:
        size = 1
        for s in shp:
            size *= s
        nrows = -(-size // LANES)
        out.append(packed[r:r + nrows].reshape(-1)[:size].reshape(shp))
        r += nrows
    return out


def _shard_to_slab(shard, d):
    glr = jnp.zeros((D_MODEL, GLA_RANK), shard.dtype)
    if d == GLR_DEV:
        glr = shard[:, GLR_LOCAL:GLR_LOCAL + GLA_RANK]
        shard = jnp.concatenate([shard[:, :GLR_LOCAL], shard[:, GLR_LOCAL + GLA_RANK:]], axis=1)
    return jnp.pad(shard, ((0, 0), (SLAB_SHIFT[d], SLAB_W - SLAB_SHIFT[d] - shard.shape[1]))), glr


def kernel(x, meta_tokens, norm_gain, w_in, w_gate_up, b_gate, ret_norm_gain, gla_norm_gain, w_branch_ret, w_branch_gla, w_out, final_norm_gain, loss_target, m_meta_tokens, m_norm_gain, m_w_in, m_w_gate_up, m_b_gate, m_ret_norm_gain, m_gla_norm_gain, m_w_branch_ret, m_w_branch_gla, m_w_out, m_final_norm_gain, v_meta_tokens, v_norm_gain, v_w_in, v_w_gate_up, v_b_gate, v_ret_norm_gain, v_gla_norm_gain, v_w_branch_ret, v_w_branch_gla, v_w_out, v_final_norm_gain):
    xi, yi, ci = _position()
    me = _index(xi, yi, ci)
    seq = x.shape[1]
    t_rows = seq + TILE
    in_shard = w_in.shape[2]
    gu_shard = w_gate_up.shape[2]
    meta_shard = meta_tokens.shape[1]
    ret_rows, gla_rows, out_rows = w_branch_ret.shape[1], w_branch_gla.shape[1], w_out.shape[1]

    assert in_shard == IN_SHARD
    slab_local, glr_local = lax.switch(me, [functools.partial(_shard_to_slab, d=d) for d in range(N_DEV)], w_in[0])
    small_local = jnp.concatenate([meta_tokens, jnp.pad(w_gate_up[0], ((0, 0), (0, LANES - gu_shard))),
                                   glr_local.reshape(-1, LANES)], axis=0)
    slabs, g_small = _all_gather([slab_local.astype(BF16), small_local], "all_gather_shards")
    n_small = N_META + GLA_RANK
    w_glr = jnp.pad(g_small[GLR_DEV, n_small:].reshape(D_MODEL, GLA_RANK), ((0, 0), (0, LANES - GLA_RANK))).astype(BF16)
    meta_full = jnp.transpose(g_small[:, :N_META, :], (1, 0, 2)).reshape(N_META, D_MODEL)
    wgu_full = jnp.transpose(g_small[:, N_META:n_small, :gu_shard], (1, 0, 2)).reshape(GLA_RANK, GLA_HEADS * GLA_K)
    wgu_pad = jnp.pad(wgu_full, ((0, LANES - GLA_RANK), (0, 0)))

    pos = jnp.arange(t_rows, dtype=F32) - float(PAD_ROWS)
    half = RET_QK // 2
    inv = ROPE_BASE ** (-jnp.arange(half, dtype=F32) / half)
    ang = pos[:, None] * inv[None, :]
    cos, sin = jnp.cos(ang), jnp.sin(ang)
    lg = jnp.log1p(-(2.0 ** (-5.0 - jnp.arange(RET_HEADS, dtype=F32))))

    head = jnp.concatenate([jnp.zeros((PAD_ROWS, D_MODEL), F32), meta_full], axis=0)
    ut, proj, glr = _inproj_tiles(head, x[0], norm_gain, slabs, w_glr)
    o_ret_raw, o_ret, ret_states, (g_br, g_bg, g_o) = _ret_fwd(
        proj, cos, sin, ret_norm_gain, lg, [w_branch_ret[0].astype(BF16), w_branch_gla[0].astype(BF16), w_out[0].astype(BF16)])
    w_br, w_bg, w_o = g_br.reshape(RET_W, D_MODEL), g_bg.reshape(GLA_W, D_MODEL), g_o.reshape(D_MODEL, D_MODEL)
    masks, cum_fwd, cum_bwd = _gla_tables()
    o_gla_raw, o_gla, gla_states = _gla_fwd(proj, glr, wgu_pad, b_gate, gla_norm_gain, masks, cum_fwd)
    (dh1, d_mr, d_mg, do_ret, do_gla, loss_part, d_gfinal, dw_br, dw_bg, dw_o) = _merge_fwd_bwd(
        o_ret, o_gla, proj, x[0], loss_target[0], final_norm_gain.reshape(1, D_MODEL), w_br, w_bg, w_o)

    d_rq, d_rk, d_rv, d_rg, d_gret = _ret_bwd(proj, cos, sin, ret_norm_gain, lg, o_ret_raw, do_ret, ret_states)
    d_gq, d_gk, d_gv, d_gg, dglr_parts, d_wgu, d_bgate, d_ggla = _gla_bwd(
        proj, glr, wgu_pad, b_gate, gla_norm_gain, o_gla_raw, do_gla, gla_states, masks, cum_fwd, cum_bwd)
    dseg = dict(rq=d_rq, rk=d_rk, rv=d_rv, rg=d_rg, gq=d_gq, gk=d_gk, gv=d_gv, gg=d_gg, mr=d_mr, mg=d_mg)
    dw_blocks, dw_glr = _inproj_bwd_w(ut, dseg, dglr_parts)

    row_sends = [dw_br.reshape(N_DEV, ret_rows, D_MODEL), dw_bg.reshape(N_DEV, gla_rows, D_MODEL),
                 dw_o.reshape(N_DEV, out_rows, D_MODEL)]
    sib_in, *sib_rows = _exchange_sibling(dw_blocks, row_sends)
    core = ci.astype(jnp.int32).reshape(1)
    chip_partials = [_chip_partial_slab(dw_blocks, sib_in, core)] + [
        _chip_partial_rows(send, sib, core, "chip_partial_" + name)
        for send, sib, name in zip(row_sends, sib_rows, ("w_branch_ret", "w_branch_gla", "w_out"))]
    grad_x, d_head, d_gnorm, p_in, p_br, p_bg, p_o = _inproj_bwd_x(
        dseg, dglr_parts, head, x[0], dh1, norm_gain, slabs, w_glr, chip_partials)
    small_shapes = [(N_META, D_MODEL), (1, D_MODEL), (GLA_RANK, GLA_HEADS * GLA_K), (1, GLA_HEADS * GLA_K),
                    (1, RET_W), (1, GLA_W), (1, D_MODEL), (1, LANES), (D_MODEL, GLA_RANK)]
    small_part = _pack_rows([d_head[PAD_ROWS:], d_gnorm, d_wgu[:GLA_RANK], d_bgate, d_gret, d_ggla, d_gfinal, loss_part,
                             dw_glr[:, :GLA_RANK]])
    (p_small,) = _all_gather([small_part], "all_gather_small_partials")

    (g_meta_f, g_gnorm, g_wgu_f, g_bgate, g_gret, g_ggla, g_gfinal, loss_all,
     g_wglr) = _unpack_rows(_reduce_small(p_small), small_shapes)
    g_w_in, d_w_in, nm_w_in, nv_w_in = _reduce_adam_slab(
        p_in, jnp.pad(g_wglr, ((0, 0), (0, LANES - GLA_RANK))), w_in[0], m_w_in[0], v_w_in[0], me)
    rb = gla_rows
    g_w_br, d_w_br, nm_w_br, nv_w_br = _reduce_adam(p_br, w_branch_ret[0], m_w_branch_ret[0], v_w_branch_ret[0], "adam_w_branch_ret", rb)
    g_w_bg, d_w_bg, nm_w_bg, nv_w_bg = _reduce_adam(p_bg, w_branch_gla[0], m_w_branch_gla[0], v_w_branch_gla[0], "adam_w_branch_gla", rb)
    g_w_o, d_w_o, nm_w_o, nv_w_o = _reduce_adam(p_o, w_out[0], m_w_out[0], v_w_out[0], "adam_w_out", rb)
    g_meta = lax.dynamic_slice_in_dim(g_meta_f, me * meta_shard, meta_shard, axis=1)
    g_wgu = lax.dynamic_slice_in_dim(g_wgu_f, me * gu_shard, gu_shard, axis=1)
    s_g = [g_meta, g_gnorm, g_wgu, g_bgate, g_gret, g_ggla, g_gfinal]
    s_w = [meta_tokens, norm_gain, w_gate_up[0], b_gate, ret_norm_gain, gla_norm_gain, final_norm_gain]
    s_m = [m_meta_tokens, m_norm_gain, m_w_gate_up[0], m_b_gate, m_ret_norm_gain, m_gla_norm_gain, m_final_norm_gain]
    s_v = [v_meta_tokens, v_norm_gain, v_w_gate_up[0], v_b_gate, v_ret_norm_gain, v_gla_norm_gain, v_final_norm_gain]
    shapes = [a.shape for a in s_g]
    s_d, s_nm, s_nv = [_unpack_rows(p, shapes) for p in _adam_small(*[_pack_rows(l) for l in (s_g, s_w, s_m, s_v)])]

    loss = loss_all[0, 0]
    grad_x = grad_x[None]

    def order(meta, gnorm, win, wgu, bgate, gret, ggla, wbr, wbg, wo, gfin):
        return (meta, gnorm, win[None], wgu[None], bgate, gret, ggla, wbr[None], wbg[None], wo[None], gfin.reshape(final_norm_gain.shape))

    def small(l):
        return dict(meta=l[0], gnorm=l[1], wgu=l[2], bgate=l[3], gret=l[4], ggla=l[5], gfin=l[6])

    grads = order(win=g_w_in, wbr=g_w_br, wbg=g_w_bg, wo=g_w_o, **small(s_g))
    deltas = order(win=d_w_in, wbr=d_w_br, wbg=d_w_bg, wo=d_w_o, **small(s_d))
    new_m = order(win=nm_w_in, wbr=nm_w_br, wbg=nm_w_bg, wo=nm_w_o, **small(s_nm))
    new_v = order(win=nv_w_in, wbr=nv_w_br, wbg=nv_w_bg, wo=nv_w_o, **small(s_nv))
    return (loss, grad_x, *grads, *deltas, *new_m, *new_v)
```

```python
import functools

import jax
import jax.numpy as jnp
from jax import lax
from jax.experimental import pallas as pl
from jax.experimental.pallas import tpu as pltpu

F32 = jnp.float32
BF16 = jnp.bfloat16

D_MODEL = 1024
N_META = 16
TILE = 256
PAD_ROWS = TILE - N_META
RET_HEADS = 4
RET_QK = 256
RET_V = 512
RET_W = RET_HEADS * RET_V
GLA_HEADS = 4
GLA_K = 128
GLA_V = 256
GLA_W = GLA_HEADS * GLA_V
GLA_RANK = 16
GLA_TAU = 16.0
GLA_CHUNK = 16
ROPE_BASE = 10000.0
EPS = 1e-6
LANES = 128
N_DEV = 8
SEG_NAMES = ("rq", "rk", "rv", "rg", "gq", "gk", "gv", "gg", "mr", "mg")
SEG_W = (1024, 1024, 2048, 2048, 512, 512, 1024, 1024, 1024, 1024)
SEG_OFF = tuple(sum(SEG_W[:i]) for i in range(len(SEG_W)))
AL_COLS = sum(SEG_W)
IN_COLS = AL_COLS + GLA_RANK
GLR_OFF = sum(SEG_W[:8])
IN_SHARD = IN_COLS // N_DEV


def _aligned_col(c):
    assert c <= GLR_OFF or c >= GLR_OFF + GLA_RANK
    return c if c <= GLR_OFF else c - GLA_RANK


SLAB_BOUND = tuple(_aligned_col(IN_SHARD * d) for d in range(N_DEV + 1))
SLAB_BLK0 = tuple(b // LANES for b in SLAB_BOUND[:-1])
SLAB_SHIFT = tuple(b % LANES for b in SLAB_BOUND[:-1])
SLAB_BLOCKS = max(-(-SLAB_BOUND[d + 1] // LANES) - SLAB_BLK0[d] for d in range(N_DEV))
SLAB_W = SLAB_BLOCKS * LANES
GLR_DEV = GLR_OFF // IN_SHARD
GLR_LOCAL = GLR_OFF - GLR_DEV * IN_SHARD
assert all(SLAB_BLK0[d] + SLAB_BLOCKS <= AL_COLS // LANES for d in range(N_DEV))
VMEM_LIMIT = 58 * 1024 * 1024
ADAM_LR, ADAM_B1, ADAM_B2, ADAM_EPS, ADAM_WD, ADAM_STEP = 0.001, 0.9, 0.999, 1e-08, 0.01, 10
ANY = pl.BlockSpec(memory_space=pl.ANY)
MESH = pl.DeviceIdType.MESH


def _call(body, name, **kw):
    return pl.pallas_call(body, name=name, **kw)


def _params(sem=None):
    return pltpu.CompilerParams(dimension_semantics=sem, vmem_limit_bytes=VMEM_LIMIT)


def _mm(a, b):
    return jnp.dot(a, b, preferred_element_type=F32)


def _mm_nt(a, b):
    return lax.dot_general(a, b, (((1,), (1,)), ((), ())), preferred_element_type=F32)


def _mm_tn(a, b):
    return lax.dot_general(a, b, (((0,), (0,)), ((), ())), preferred_element_type=F32)


def _sigmoid(x):
    return 1.0 / (1.0 + jnp.exp(-x))


def _rope(t, cos, sin):
    half = t.shape[-1] // 2
    t1, t2 = t[:, :half], t[:, half:]
    return jnp.concatenate([t1 * cos - t2 * sin, t2 * cos + t1 * sin], axis=-1)


def _rope_bwd(g, cos, sin):
    half = g.shape[-1] // 2
    g1, g2 = g[:, :half], g[:, half:]
    return jnp.concatenate([g1 * cos + g2 * sin, g2 * cos - g1 * sin], axis=-1)


def _row_mean(x):
    return jnp.mean(x, axis=-1, keepdims=True)


def _col_sum(x):
    return jnp.sum(x, axis=0, keepdims=True)


def _tile_rows(head_ref, x_ref):
    return jnp.where(pl.program_id(0) == 0, head_ref[...], x_ref[...])


def _head_spec():
    return pl.BlockSpec((TILE, D_MODEL), lambda i: (0, 0))


def _x_spec():
    return pl.BlockSpec((TILE, D_MODEL), lambda i: (jnp.maximum(i - 1, 0), 0))


def _slab_plan():
    interior, shared = [], []
    for d in range(N_DEV):
        lo, hi = -(-SLAB_BOUND[d] // LANES), SLAB_BOUND[d + 1] // LANES
        interior.append((d, LANES * (lo - SLAB_BLK0[d]), LANES * lo, LANES * (hi - lo)))
        if d + 1 < N_DEV and SLAB_BOUND[d + 1] % LANES:
            shared.append((hi, d, hi - SLAB_BLK0[d]))
    return interior, shared


W_SCRATCH = lambda: [pltpu.VMEM((D_MODEL, AL_COLS), BF16), pltpu.VMEM((D_MODEL, LANES), BF16),
                     pltpu.VMEM((2 * (N_DEV - 1), D_MODEL, LANES), BF16), pltpu.SemaphoreType.DMA((3 * N_DEV,))]


def _load_weight(slabs_hbm, wg_hbm, w_vm, wg_vm, edge_vm, sem):
    interior, shared = _slab_plan()
    copies = [pltpu.make_async_copy(wg_hbm, wg_vm, sem.at[0])]
    for d, src, dst, width in interior:
        copies.append(pltpu.make_async_copy(slabs_hbm.at[d, :, pl.ds(src, width)], w_vm.at[:, pl.ds(dst, width)], sem.at[1 + d]))
    for n, (_, d, blk) in enumerate(shared):
        copies.append(pltpu.make_async_copy(slabs_hbm.at[d, :, pl.ds(LANES * blk, LANES)], edge_vm.at[2 * n], sem.at[1 + N_DEV + 2 * n]))
        copies.append(pltpu.make_async_copy(slabs_hbm.at[d + 1, :, pl.ds(0, LANES)], edge_vm.at[2 * n + 1], sem.at[2 + N_DEV + 2 * n]))
    for cp in copies:
        cp.start()
    for cp in copies:
        cp.wait()
    for n, (blk, _, _) in enumerate(shared):
        w_vm[:, LANES * blk:LANES * (blk + 1)] = edge_vm[2 * n] + edge_vm[2 * n + 1]


def _proj_specs(names, n_units, where):
    specs = []
    for name in names:
        s = SEG_NAMES.index(name)
        nblk = SEG_W[s] // n_units // LANES
        base = SEG_OFF[s] // LANES
        assert base % nblk == 0
        specs.append(pl.BlockSpec((nblk, TILE, LANES), lambda *g, base=base, nblk=nblk: (base // nblk + where(*g)[0], where(*g)[1], 0)))
    return specs


def _cols(ref, unit=0, n_units=1):
    n = ref.shape[0] // n_units
    return ref[unit * n] if n == 1 else jnp.concatenate([ref[unit * n + j] for j in range(n)], axis=1)


def _prenorm(head, x, g_norm, w_glr):
    t_rows = x.shape[0] + TILE
    nt = t_rows // TILE

    def body(head_ref, x_ref, g_ref, wg_ref, u_ref, ut_ref, glr_ref):
        x = _tile_rows(head_ref, x_ref)
        r = lax.rsqrt(_row_mean(x * x) + EPS)
        u32 = (x * r * g_ref[...]).astype(BF16).astype(F32)
        u = u32.astype(BF16)
        u_ref[...] = u
        ut_ref[...] = u32.T.astype(BF16)
        glr_ref[...] = _mm(u, wg_ref[...])

    row = lambda w: pl.BlockSpec((TILE, w), lambda i: (i, 0))
    return _call(
        body, "prenorm", grid=(nt,),
        out_shape=[jax.ShapeDtypeStruct((t_rows, D_MODEL), BF16), jax.ShapeDtypeStruct((nt, D_MODEL, TILE), BF16),
                   jax.ShapeDtypeStruct((t_rows, LANES), F32)],
        in_specs=[_head_spec(), _x_spec(), pl.BlockSpec((1, D_MODEL), lambda i: (0, 0)), pl.BlockSpec((D_MODEL, LANES), lambda i: (0, 0))],
        out_specs=[row(D_MODEL), pl.BlockSpec((None, D_MODEL, TILE), lambda i: (i, 0, 0)), row(LANES)],
        compiler_params=_params(("arbitrary",)),
    )(head, x, g_norm, w_glr)


SLAB_INNER = 9


def _edge_blocks():
    inner = {SLAB_BLK0[d] + j for d in range(N_DEV) for j in range(1, 1 + SLAB_INNER)}
    edges = []
    for blk in range(AL_COLS // LANES):
        if blk not in inner:
            srcs = [(d, blk - SLAB_BLK0[d]) for d in range(N_DEV)
                    if SLAB_BLK0[d] <= blk < SLAB_BLK0[d] + SLAB_BLOCKS and SLAB_BOUND[d] < LANES * (blk + 1) and LANES * blk < SLAB_BOUND[d + 1]]
            edges.append((blk, srcs))
    return edges


def _inproj_fwd(u, slab_local):
    t_rows = u.shape[0]
    nt = t_rows // TILE
    rc = (3 if nt % 3 == 0 else 1) * TILE
    n_chunks = t_rows // rc
    edges = _edge_blocks()
    ne = len(edges)
    runs = []
    for k, (blk, _) in enumerate(edges):
        if runs and edges[runs[-1][0] + runs[-1][1] - 1][0] + 1 == blk:
            runs[-1] = (runs[-1][0], runs[-1][1] + 1)
        else:
            runs.append((k, 1))
    n_stage = sum(len(srcs) for _, srcs in edges)

    def body(u_hbm, slab_hbm, proj_hbm, slabs_hbm, u_vm, wbuf, obuf, ebuf, stage, ebuf_out, sem_u, sem_w, sem_o, sem_s, sem_eo,
             send_sems, recv_sems, sem_l):
        x, y, c = _position()
        me, sibling = (x, y, c), (x, y, 1 - c)
        chips = [(1 - x, y), (x, 1 - y), (1 - x, 1 - y)]

        def slab_copy(k, block, to, src=None):
            dst = slabs_hbm.at[_index(*block)]
            return pltpu.make_async_remote_copy(src_ref=dst if src is None else src, dst_ref=dst, send_sem=send_sems.at[k],
                                                recv_sem=recv_sems.at[k], device_id=to, device_id_type=MESH)

        mine = pltpu.make_async_copy(slab_hbm, slabs_hbm.at[_index(*me)], sem_l)
        mine.start()
        first = [slab_copy(0, me, sibling, src=slab_hbm)] + [slab_copy(1 + j, me, (*chip, c), src=slab_hbm) for j, chip in enumerate(chips)]
        for cp in first[:2]:
            cp.start()
        load_u = pltpu.make_async_copy(u_hbm, u_vm, sem_u)
        load_u.start()
        load_u.wait()

        def store(slot, block0, rows0):
            return pltpu.make_async_copy(obuf.at[slot], proj_hbm.at[pl.ds(block0, SLAB_INNER), pl.ds(rows0, rc)], sem_o.at[slot])

        def multiply(dev):
            load_w = pltpu.make_async_copy(slabs_hbm.at[dev, :, pl.ds(LANES, SLAB_INNER * LANES)], wbuf, sem_w)
            load_w.start()
            load_w.wait()
            block0 = _slab_block0(dev) + 1

            def chunk(r, carry):
                slot = lax.rem(r, 2)
                rows0 = pl.multiple_of(r * rc, rc)

                @pl.when(r >= 2)
                def _():
                    store(slot, block0, rows0).wait()

                res = _mm(u_vm[pl.ds(rows0, rc), :], wbuf[...])
                for j in range(SLAB_INNER):
                    obuf[slot, j] = res[:, j * LANES:(j + 1) * LANES].astype(BF16)
                store(slot, block0, rows0).start()
                return carry

            lax.fori_loop(0, n_chunks, chunk, 0)
            for r in range(max(0, n_chunks - 2), n_chunks):
                store(r % 2, block0, r * rc).wait()

        mine.wait()
        multiply(_index(*me))
        slab_copy(0, sibling, me).wait_recv()
        multiply(_index(*sibling))
        passed = []
        for j, chip in enumerate(chips):
            slab_copy(1 + j, (*chip, c), me).wait_recv()
            passed.append(slab_copy(4 + j, (*chip, c), sibling))
            passed[-1].start()
            if j + 1 < len(chips):
                first[1 + j].wait_send()
                first[2 + j].start()
            multiply(_index(*chip, c))
            slab_copy(4 + j, (*chip, 1 - c), me).wait_recv()
            multiply(_index(*chip, 1 - c))

        loads, n = [], 0
        for k, (_, srcs) in enumerate(edges):
            for d, j in srcs:
                dst = ebuf.at[:, pl.ds(k * LANES, LANES)] if len(srcs) == 1 else stage.at[n]
                loads.append(pltpu.make_async_copy(slabs_hbm.at[d, :, pl.ds(j * LANES, LANES)], dst, sem_s.at[n]))
                n += 1
        for cp in loads:
            cp.start()
        for cp in loads:
            cp.wait()
        n = 0
        for k, (_, srcs) in enumerate(edges):
            if len(srcs) == 2:
                ebuf[:, k * LANES:(k + 1) * LANES] = stage[n] + stage[n + 1]
            n += len(srcs)

        def edge_stores(slot, rows0):
            return [pltpu.make_async_copy(ebuf_out.at[slot, pl.ds(k0, length)],
                                          proj_hbm.at[pl.ds(edges[k0][0], length), pl.ds(rows0, rc)], sem_eo.at[slot, i])
                    for i, (k0, length) in enumerate(runs)]

        def edge_chunk(r, carry):
            slot = lax.rem(r, 2)
            rows0 = pl.multiple_of(r * rc, rc)

            @pl.when(r >= 2)
            def _():
                for cp in edge_stores(slot, rows0):
                    cp.wait()

            res = _mm(u_vm[pl.ds(rows0, rc), :], ebuf[...])
            for k in range(ne):
                ebuf_out[slot, k] = res[:, k * LANES:(k + 1) * LANES].astype(BF16)
            for cp in edge_stores(slot, rows0):
                cp.start()
            return carry

        lax.fori_loop(0, n_chunks, edge_chunk, 0)
        for r in range(max(0, n_chunks - 2), n_chunks):
            for cp in edge_stores(r % 2, r * rc):
                cp.wait()

        for cp in [first[0], first[3]] + passed:
            cp.wait_send()

    return _call(
        body, "inproj_fwd",
        out_shape=[jax.ShapeDtypeStruct((AL_COLS // LANES, t_rows, LANES), BF16), jax.ShapeDtypeStruct((N_DEV, D_MODEL, SLAB_W), BF16)],
        in_specs=[ANY] * 2, out_specs=[ANY] * 2,
        scratch_shapes=[pltpu.VMEM((t_rows, D_MODEL), BF16), pltpu.VMEM((D_MODEL, SLAB_INNER * LANES), BF16),
                        pltpu.VMEM((2, SLAB_INNER, rc, LANES), BF16), pltpu.VMEM((D_MODEL, ne * LANES), BF16),
                        pltpu.VMEM((n_stage, D_MODEL, LANES), BF16), pltpu.VMEM((2, ne, rc, LANES), BF16),
                        pltpu.SemaphoreType.DMA, pltpu.SemaphoreType.DMA, pltpu.SemaphoreType.DMA((2,)),
                        pltpu.SemaphoreType.DMA((n_stage,)), pltpu.SemaphoreType.DMA((2, len(runs))),
                        pltpu.SemaphoreType.DMA((7,)), pltpu.SemaphoreType.DMA((7,)), pltpu.SemaphoreType.DMA],
        compiler_params=_params(),
    )(u, slab_local)


def _inproj_tiles(head, x, g_norm, slabs, w_glr):
    t_rows = x.shape[0] + TILE
    nt = t_rows // TILE
    n_blocks = AL_COLS // LANES

    def body(head_ref, x_ref, g_ref, slabs_hbm, wg_hbm, ut_ref, proj_ref, glr_ref, w_vm, wg_vm, edge_vm, sem):
        @pl.when(pl.program_id(0) == 0)
        def _():
            _load_weight(slabs_hbm, wg_hbm, w_vm, wg_vm, edge_vm, sem)

        x = _tile_rows(head_ref, x_ref)
        r = lax.rsqrt(_row_mean(x * x) + EPS)
        u32 = (x * r * g_ref[...]).astype(BF16).astype(F32)
        u = u32.astype(BF16)
        ut_ref[...] = u32.T.astype(BF16)
        for s in range(len(SEG_W)):
            res = _mm(u, w_vm[:, SEG_OFF[s]:SEG_OFF[s] + SEG_W[s]]).astype(BF16)
            for j in range(SEG_W[s] // LANES):
                proj_ref[SEG_OFF[s] // LANES + j] = res[:, j * LANES:(j + 1) * LANES]
        glr_ref[...] = _mm(u, wg_vm[...])

    return _call(
        body, "inproj_fwd_tiles", grid=(nt,),
        out_shape=[jax.ShapeDtypeStruct((nt, D_MODEL, TILE), BF16), jax.ShapeDtypeStruct((n_blocks, t_rows, LANES), BF16),
                   jax.ShapeDtypeStruct((t_rows, LANES), F32)],
        in_specs=[_head_spec(), _x_spec(), pl.BlockSpec((1, D_MODEL), lambda i: (0, 0)), ANY, ANY],
        out_specs=[pl.BlockSpec((None, D_MODEL, TILE), lambda i: (i, 0, 0)), pl.BlockSpec((n_blocks, TILE, LANES), lambda i: (0, i, 0)),
                   pl.BlockSpec((TILE, LANES), lambda i: (i, 0))],
        scratch_shapes=W_SCRATCH(), compiler_params=_params(("arbitrary",)),
    )(head, x, g_norm, slabs, w_glr)


def _ret_decay(lgh):
    i = lax.broadcasted_iota(jnp.int32, (TILE, TILE), 0)
    j = lax.broadcasted_iota(jnp.int32, (TILE, TILE), 1)
    rel = (i - j).astype(F32)
    return jnp.where(rel >= 0, jnp.exp(jnp.maximum(rel, 0.0) * lgh), 0.0)


def _ret_vectors(lgh):
    idx = lax.broadcasted_iota(jnp.int32, (TILE, 1), 0).astype(F32)
    xi = jnp.exp((idx + 1.0) * lgh)
    zeta = jnp.exp((TILE - 1.0 - idx) * lgh)
    gc = jnp.exp(jnp.full((1, 1), float(TILE), F32) * lgh)
    return xi, zeta, gc


def _ret_fwd(proj, cos, sin, gain, lg, row_shards):
    t_rows = cos.shape[0]
    nt = t_rows // TILE
    ns = len(row_shards)

    def body(lg_ref, q_ref, k_ref, v_ref, g_ref, cos_ref, sin_ref, gain_ref, *rest):
        shard_refs, (oraw_ref, oret_ref, st_ref), gathered = rest[:ns], rest[ns:ns + 3], rest[ns + 3:2 * ns + 3]
        s_acc, dm = rest[2 * ns + 3:2 * ns + 5]
        gather = _Exchange(shard_refs, gathered, rest[2 * ns + 5:], among_chips=False)
        t = pl.program_id(0)

        @pl.when(t == 0)
        def _():
            gather.start()
            s_acc[...] = jnp.zeros_like(s_acc)
            for h in range(RET_HEADS):
                dm[h] = _ret_decay(lg_ref[h])

        @pl.when(t == nt - 1)
        def _():
            gather.finish()

        cos_t, sin_t = cos_ref[...], sin_ref[...]
        for h in range(RET_HEADS):
            lgh = lg_ref[h]
            q = _rope(_cols(q_ref, h, RET_HEADS).astype(F32), cos_t, sin_t)
            k = _rope(_cols(k_ref, h, RET_HEADS).astype(F32), cos_t, sin_t) * (RET_QK ** -0.5)
            xi, zeta, gc = _ret_vectors(lgh)
            v = _cols(v_ref, h, RET_HEADS)
            s_in = s_acc[h]
            p = (_mm_nt(q.astype(BF16), k.astype(BF16)) * dm[h]).astype(BF16)
            o = _mm(p, v) + _mm((q * xi).astype(BF16), s_in.astype(BF16))
            st_ref[h] = s_in.astype(BF16)
            s_acc[h] = s_in * gc + _mm_tn((k * zeta).astype(BF16), v)
            cols = slice(h * RET_V, (h + 1) * RET_V)
            oraw_ref[:, cols] = o
            oc = o - _row_mean(o)
            n = oc * lax.rsqrt(_row_mean(oc * oc) + EPS) * gain_ref[:, cols]
            g = _cols(g_ref, h, RET_HEADS).astype(F32)
            oret_ref[:, cols] = (n * g * _sigmoid(g)).astype(BF16)

    row = lambda w: pl.BlockSpec((TILE, w), lambda t: (t, 0))
    outs = _call(
        body, "ret_fwd", grid=(nt,),
        out_shape=[jax.ShapeDtypeStruct((t_rows, RET_W), F32), jax.ShapeDtypeStruct((t_rows, RET_W), BF16),
                   jax.ShapeDtypeStruct((RET_HEADS, nt, RET_QK, RET_V), BF16)]
                  + [jax.ShapeDtypeStruct((N_DEV, *a.shape), a.dtype) for a in row_shards],
        in_specs=[pl.BlockSpec(memory_space=pltpu.SMEM)] + _proj_specs(("rq", "rk", "rv", "rg"), 1, lambda t: (0, t)) + [row(LANES), row(LANES),
                  pl.BlockSpec((1, RET_W), lambda t: (0, 0))] + [ANY] * ns,
        out_specs=[row(RET_W), row(RET_W), pl.BlockSpec((RET_HEADS, None, RET_QK, RET_V), lambda t: (0, t, 0, 0))] + [ANY] * ns,
        scratch_shapes=[pltpu.VMEM((RET_HEADS, RET_QK, RET_V), F32), pltpu.VMEM((RET_HEADS, TILE, TILE), F32)] + _exchange_sems(ns, N_DEV),
        compiler_params=_params(("arbitrary",)),
    )(lg, proj, proj, proj, proj, cos, sin, gain, *row_shards)
    return outs[0], outs[1], outs[2], outs[3:]


def _ret_bwd(proj, cos, sin, gain, lg, o_raw, do_ret, states):
    t_rows = cos.shape[0]
    nt = t_rows // TILE

    def body(lg_ref, q_ref, k_ref, v_ref, g_ref, cos_ref, sin_ref, gain_ref, oraw_ref, do_ref, st_ref,
             dq_ref, dk_ref, dv_ref, dg_ref, dgain_ref, e_acc, dm):
        @pl.when(pl.program_id(0) == 0)
        def _():
            e_acc[...] = jnp.zeros_like(e_acc)
            for h in range(RET_HEADS):
                dm[h] = _ret_decay(lg_ref[h])
            dgain_ref[...] = jnp.zeros_like(dgain_ref)

        cos_t, sin_t = cos_ref[...], sin_ref[...]
        for h in range(RET_HEADS):
            lgh = lg_ref[h]
            cols = slice(h * RET_V, (h + 1) * RET_V)
            qcols = slice(h * RET_QK, (h + 1) * RET_QK)
            q = _rope(_cols(q_ref, h, RET_HEADS).astype(F32), cos_t, sin_t)
            k = _rope(_cols(k_ref, h, RET_HEADS).astype(F32), cos_t, sin_t) * (RET_QK ** -0.5)
            xi, zeta, gc = _ret_vectors(lgh)
            v = _cols(v_ref, h, RET_HEADS)
            g = _cols(g_ref, h, RET_HEADS).astype(F32)
            o = oraw_ref[:, cols]
            do = do_ref[:, cols].astype(F32)
            oc = o - _row_mean(o)
            rstd = lax.rsqrt(_row_mean(oc * oc) + EPS)
            xh = oc * rstd
            gain_t = gain_ref[:, cols]
            sg = _sigmoid(g)
            dn = do * (g * sg)
            dg_ref[:, cols] = (do * (xh * gain_t) * (sg * (1.0 + g * (1.0 - sg)))).astype(BF16)
            dgain_ref[:, cols] += _col_sum(dn * xh)
            dxh = dn * gain_t
            dob = (rstd * (dxh - _row_mean(dxh) - xh * _row_mean(dxh * xh))).astype(BF16)
            dmat = dm[h]
            qb, kb = q.astype(BF16), k.astype(BF16)
            p = (_mm_nt(qb, kb) * dmat).astype(BF16)
            dp = (_mm_nt(dob, v) * dmat).astype(BF16)
            s_in = st_ref[h]
            e_in = e_acc[h]
            e_b = e_in.astype(BF16)
            dq = _mm(dp, kb) + _mm_nt(dob, s_in) * xi
            dk = _mm_tn(dp, qb) + _mm_nt(v, e_b) * zeta
            dv_ref[:, cols] = (_mm_tn(p, dob) + _mm((k * zeta).astype(BF16), e_b)).astype(BF16)
            e_acc[h] = e_in * gc + _mm_tn((q * xi).astype(BF16), dob)
            dq_ref[:, qcols] = _rope_bwd(dq, cos_t, sin_t).astype(BF16)
            dk_ref[:, qcols] = (_rope_bwd(dk, cos_t, sin_t) * (RET_QK ** -0.5)).astype(BF16)

    row = lambda w: pl.BlockSpec((TILE, w), lambda j: (nt - 1 - j, 0))
    vec = pl.BlockSpec((1, RET_W), lambda j: (0, 0))
    return _call(
        body, "ret_bwd", grid=(nt,),
        out_shape=[jax.ShapeDtypeStruct((t_rows, RET_HEADS * RET_QK), BF16), jax.ShapeDtypeStruct((t_rows, RET_HEADS * RET_QK), BF16),
                   jax.ShapeDtypeStruct((t_rows, RET_W), BF16), jax.ShapeDtypeStruct((t_rows, RET_W), BF16),
                   jax.ShapeDtypeStruct((1, RET_W), F32)],
        in_specs=[pl.BlockSpec(memory_space=pltpu.SMEM)] + _proj_specs(("rq", "rk", "rv", "rg"), 1, lambda j: (0, nt - 1 - j)) + [row(LANES), row(LANES), vec,
                  row(RET_W), row(RET_W), pl.BlockSpec((RET_HEADS, None, RET_QK, RET_V), lambda j: (0, nt - 1 - j, 0, 0))],
        out_specs=[row(RET_HEADS * RET_QK), row(RET_HEADS * RET_QK), row(RET_W), row(RET_W), vec],
        scratch_shapes=[pltpu.VMEM((RET_HEADS, RET_QK, RET_V), F32), pltpu.VMEM((RET_HEADS, TILE, TILE), F32)],
        compiler_params=_params(("arbitrary",)),
    )(lg, proj, proj, proj, proj, cos, sin, gain, o_raw, do_ret, states)


GLA_LEVELS = (32, 64, 128, 256)
N_TERMS = 1 + len(GLA_LEVELS)


def _gla_tables():
    p = jnp.arange(TILE)[:, None]
    r = jnp.arange(TILE)[None, :]
    masks = [(p // GLA_CHUNK == r // GLA_CHUNK) & (r <= p)]
    for blk in GLA_LEVELS:
        masks.append((p // blk == r // blk) & (p % blk >= blk // 2) & (r % blk < blk // 2))
    masks = jnp.stack(masks + [m.T for m in masks]).astype(F32)
    cum_fwd = jnp.concatenate([r <= p, masks[0] > 0], axis=0).astype(BF16)
    cum_bwd = jnp.concatenate([r >= p, masks[N_TERMS] > 0], axis=1).astype(BF16)
    return masks, cum_fwd, cum_bwd


def _split3(x):
    hi = x.astype(BF16)
    rest = x - hi.astype(F32)
    mid = rest.astype(BF16)
    lo = (rest - mid.astype(F32)).astype(BF16)
    return jnp.concatenate([hi, mid, lo], axis=1)


def _join3(y):
    w = y.shape[1] // 3
    return (y[:, 2 * w:] + y[:, w:2 * w]) + y[:, :w]


def _gla_decays(glr_ref, wgu_ref, b_ref, cum_ref):
    z = _mm(glr_ref[...].astype(BF16), wgu_ref[...].astype(BF16)) + b_ref[...]
    la = (jnp.minimum(z, 0.0) - jnp.log(1.0 + jnp.exp(-jnp.abs(z)))) / GLA_TAU
    width = la.shape[1]
    hi = la.astype(BF16)
    rest = la - hi.astype(F32)
    mid = rest.astype(BF16)
    lo = (rest - mid.astype(F32)).astype(BF16)
    y = _mm(cum_ref[...], jnp.concatenate([hi, mid, lo], axis=1))
    gb = (y[:, 2 * width:] + y[:, width:2 * width]) + y[:, :width]
    return z, gb[:TILE], gb[TILE:]


def _gla_prep(h, q_ref, k_ref, g_all, b_all, g_scr, ref_scr):
    cols = slice(h * GLA_K, (h + 1) * GLA_K)
    g, b = g_all[:, cols], b_all[:, cols]
    g_scr[h] = g
    factors = [(jnp.exp(b), jnp.exp(-b))]
    for lvl, blk in enumerate(GLA_LEVELS):
        for n in range(TILE // blk):
            ref_scr[h, lvl, n * blk:(n + 1) * blk, :] = jnp.broadcast_to(g_scr[h, pl.ds(n * blk + blk // 2 - 1, 1), :], (blk, GLA_K))
        x = g - ref_scr[h, lvl]
        factors.append((jnp.exp(jnp.minimum(x, 0.0)), jnp.exp(jnp.minimum(-x, 0.0))))
    g_last = g_scr[h, pl.ds(TILE - 1, 1), :]
    q = _cols(q_ref, h, GLA_HEADS).astype(F32) * (GLA_K ** -0.5)
    k = _cols(k_ref, h, GLA_HEADS).astype(F32)
    return q, k, factors, jnp.exp(g), jnp.exp(g_last), jnp.exp(g_last - g)


def _gla_scores(q, k, factors, m_ref):
    a = jnp.zeros((TILE, TILE), F32)
    for l, (fq, fk) in enumerate(factors):
        s = _mm_nt((q * fq).astype(BF16), (k * fk).astype(BF16))
        a = jnp.where(m_ref[l] > 0.0, s, a)
    return a


def _gla_fwd(proj, glr, wgu_pad, b_gate, gain, masks, cum_fwd):
    t_rows = glr.shape[0]
    nt = t_rows // TILE

    def body(q_ref, k_ref, v_ref, g_ref, glr_ref, wgu_ref, b_ref, gain_ref, m_ref, cum_ref, oraw_ref, ogla_ref, st_ref,
             s_acc, g_scr, ref_scr):
        @pl.when(pl.program_id(0) == 0)
        def _():
            s_acc[...] = jnp.zeros_like(s_acc)

        _, g_all, b_all = _gla_decays(glr_ref, wgu_ref, b_ref, cum_ref)
        for h in range(GLA_HEADS):
            q, k, factors, e_g, e_last, e_end = _gla_prep(h, q_ref, k_ref, g_all, b_all, g_scr, ref_scr)
            v = _cols(v_ref, h, GLA_HEADS)
            st = s_acc[h]
            st_ref[h] = st
            a = _gla_scores(q, k, factors, m_ref)
            o = _mm(a.astype(BF16), v) + _mm_nt((q * e_g).astype(BF16), st.astype(BF16))
            s_acc[h] = st * e_last + _mm(v.astype(F32).T.astype(BF16), (k * e_end).astype(BF16))
            cols = slice(h * GLA_V, (h + 1) * GLA_V)
            oraw_ref[:, cols] = o
            n = o * lax.rsqrt(_row_mean(o * o) + EPS) * gain_ref[:, cols]
            g = _cols(g_ref, h, GLA_HEADS).astype(F32)
            ogla_ref[:, cols] = (n * g * _sigmoid(g)).astype(BF16)

    row = lambda w: pl.BlockSpec((TILE, w), lambda t: (t, 0))
    whole = lambda *shape: pl.BlockSpec(shape, lambda t: (0,) * len(shape))
    return _call(
        body, "gla_fwd", grid=(nt,),
        out_shape=[jax.ShapeDtypeStruct((t_rows, GLA_W), F32), jax.ShapeDtypeStruct((t_rows, GLA_W), BF16),
                   jax.ShapeDtypeStruct((GLA_HEADS, nt, GLA_V, GLA_K), F32)],
        in_specs=_proj_specs(("gq", "gk", "gv", "gg"), 1, lambda t: (0, t)) + [row(LANES), whole(LANES, GLA_HEADS * GLA_K),
                  whole(1, GLA_HEADS * GLA_K), whole(1, GLA_W), whole(N_TERMS, TILE, TILE), whole(2 * TILE, TILE)],
        out_specs=[row(GLA_W), row(GLA_W), pl.BlockSpec((GLA_HEADS, None, GLA_V, GLA_K), lambda t: (0, t, 0, 0))],
        scratch_shapes=[pltpu.VMEM((GLA_HEADS, GLA_V, GLA_K), F32), pltpu.VMEM((GLA_HEADS, TILE, GLA_K), F32),
                        pltpu.VMEM((GLA_HEADS, len(GLA_LEVELS), TILE, GLA_K), F32)],
        compiler_params=_params(("arbitrary",)),
    )(proj, proj, proj, proj, glr, wgu_pad, b_gate, gain, masks, cum_fwd)


def _gla_bwd(proj, glr, wgu_pad, b_gate, gain, o_raw, do_gla, states, masks, cum_fwd, cum_bwd):
    t_rows = glr.shape[0]
    nt = t_rows // TILE

    def body(q_ref, k_ref, v_ref, g_ref, glr_ref, wgu_ref, b_ref, gain_ref, m_ref, cum_ref, cumb_ref, oraw_ref, do_ref, st_ref,
             dq_ref, dk_ref, dv_ref, dg_ref, dglr_ref, dwgu_ref, dbg_ref, dgain_ref, d_acc, g_scr, ref_scr, dref_scr):
        @pl.when(pl.program_id(0) == 0)
        def _():
            d_acc[...] = jnp.zeros_like(d_acc)
            dwgu_ref[...] = jnp.zeros_like(dwgu_ref)
            dbg_ref[...] = jnp.zeros_like(dbg_ref)
            dgain_ref[...] = jnp.zeros_like(dgain_ref)

        z_all, g_all, b_all = _gla_decays(glr_ref, wgu_ref, b_ref, cum_ref)
        dla_parts = []
        for h in range(GLA_HEADS):
            q, k, factors, e_g, e_last, e_end = _gla_prep(h, q_ref, k_ref, g_all, b_all, g_scr, ref_scr)
            v = _cols(v_ref, h, GLA_HEADS)
            cols = slice(h * GLA_V, (h + 1) * GLA_V)
            kcols = slice(h * GLA_K, (h + 1) * GLA_K)
            o = oraw_ref[:, cols]
            do = do_ref[:, cols].astype(F32)
            g = _cols(g_ref, h, GLA_HEADS).astype(F32)
            rinv = lax.rsqrt(_row_mean(o * o) + EPS)
            nh = o * rinv
            gain_t = gain_ref[:, cols]
            sg = _sigmoid(g)
            dn = do * (g * sg)
            dg_ref[:, cols] = (do * (nh * gain_t) * (sg * (1.0 + g * (1.0 - sg)))).astype(BF16)
            dgain_ref[:, cols] += _col_sum(dn * nh)
            dnh = dn * gain_t
            dor = rinv * (dnh - nh * _row_mean(dnh * nh))
            dob = dor.astype(BF16)
            a_t = _gla_scores(q, k, factors, m_ref).T.astype(BF16)
            da = _mm_nt(dob, v)
            da_t = _mm_nt(v, dob)
            st_in = st_ref[h]
            d_out = d_acc[h]
            d_out_b = d_out.astype(BF16)
            qg, kg = q * e_g, k * e_end
            dqg = _mm(dob, st_in.astype(BF16))
            dkg = _mm(v, d_out_b)
            dv_ref[:, cols] = (_mm(a_t, dob) + _mm_nt(kg.astype(BF16), d_out_b)).astype(BF16)
            d_acc[h] = d_out * e_last + _mm(dor.T.astype(BF16), qg.astype(BF16))
            dq = dqg * e_g
            dk = dkg * e_end
            dkg_kg = dkg * kg
            dg_cum = dqg * qg - dkg_kg
            db = None
            for l, (fq, fk) in enumerate(factors):
                qt, kt = q * fq, k * fk
                dqt = _mm(jnp.where(m_ref[l] > 0.0, da, 0.0).astype(BF16), kt.astype(BF16))
                dkt = _mm(jnp.where(m_ref[N_TERMS + l] > 0.0, da_t, 0.0).astype(BF16), qt.astype(BF16))
                dq = dq + dqt * fq
                dk = dk + dkt * fk
                diff = dqt * qt - dkt * kt
                if l == 0:
                    db = diff
                else:
                    dg_cum = dg_cum + diff
                    dref_scr[h, l - 1] = diff
            dq_ref[:, kcols] = (dq * (GLA_K ** -0.5)).astype(BF16)
            dk_ref[:, kcols] = dk.astype(BF16)
            g_scr[h] = dg_cum
            g_scr[h, pl.ds(TILE - 1, 1), :] += e_last * _col_sum(d_out * st_in) + _col_sum(dkg_kg)
            for lvl, blk in enumerate(GLA_LEVELS):
                for n in range(TILE // blk):
                    g_scr[h, pl.ds(n * blk + blk // 2 - 1, 1), :] -= _col_sum(dref_scr[h, lvl, n * blk:(n + 1) * blk, :])
            dla_parts.append(_join3(_mm(cumb_ref[...], jnp.concatenate([_split3(g_scr[h]), _split3(db)], axis=0))))
        dz = jnp.concatenate(dla_parts, axis=1) * (1.0 / GLA_TAU) * _sigmoid(-z_all)
        dzb = dz.astype(BF16)
        wgu_b = wgu_ref[...].astype(BF16)
        for h in range(GLA_HEADS):
            kcols = slice(h * GLA_K, (h + 1) * GLA_K)
            dglr_ref[h] = _mm_nt(dzb[:, kcols], wgu_b[:, kcols]).astype(BF16)
        dwgu_ref[...] += _mm(glr_ref[...].T.astype(BF16), dzb)
        dbg_ref[...] += _col_sum(dz)

    row = lambda w: pl.BlockSpec((TILE, w), lambda j: (nt - 1 - j, 0))
    whole = lambda *shape: pl.BlockSpec(shape, lambda j: (0,) * len(shape))
    return _call(
        body, "gla_bwd", grid=(nt,),
        out_shape=[jax.ShapeDtypeStruct((t_rows, GLA_HEADS * GLA_K), BF16), jax.ShapeDtypeStruct((t_rows, GLA_HEADS * GLA_K), BF16),
                   jax.ShapeDtypeStruct((t_rows, GLA_W), BF16), jax.ShapeDtypeStruct((t_rows, GLA_W), BF16),
                   jax.ShapeDtypeStruct((GLA_HEADS, t_rows, LANES), BF16), jax.ShapeDtypeStruct((LANES, GLA_HEADS * GLA_K), F32),
                   jax.ShapeDtypeStruct((1, GLA_HEADS * GLA_K), F32), jax.ShapeDtypeStruct((1, GLA_W), F32)],
        in_specs=_proj_specs(("gq", "gk", "gv", "gg"), 1, lambda j: (0, nt - 1 - j)) + [row(LANES),
                  whole(LANES, GLA_HEADS * GLA_K), whole(1, GLA_HEADS * GLA_K), whole(1, GLA_W),
                  whole(2 * N_TERMS, TILE, TILE), whole(2 * TILE, TILE), whole(TILE, 2 * TILE), row(GLA_W), row(GLA_W),
                  pl.BlockSpec((GLA_HEADS, None, GLA_V, GLA_K), lambda j: (0, nt - 1 - j, 0, 0))],
        out_specs=[row(GLA_HEADS * GLA_K), row(GLA_HEADS * GLA_K), row(GLA_W), row(GLA_W),
                   pl.BlockSpec((GLA_HEADS, TILE, LANES), lambda j: (0, nt - 1 - j, 0)), whole(LANES, GLA_HEADS * GLA_K),
                   whole(1, GLA_HEADS * GLA_K), whole(1, GLA_W)],
        scratch_shapes=[pltpu.VMEM((GLA_HEADS, GLA_V, GLA_K), F32), pltpu.VMEM((GLA_HEADS, TILE, GLA_K), F32),
                        pltpu.VMEM((GLA_HEADS, len(GLA_LEVELS), TILE, GLA_K), F32),
                        pltpu.VMEM((GLA_HEADS, len(GLA_LEVELS), TILE, GLA_K), F32)],
        compiler_params=_params(("arbitrary",)),
    )(proj, proj, proj, proj, glr, wgu_pad, b_gate, gain, masks, cum_fwd, cum_bwd, o_raw, do_gla, states)


def _merge_fwd_bwd(o_ret, o_gla, proj, x, target, g_final, w_br, w_bg, w_out):
    t_rows = x.shape[0] + TILE
    nt = t_rows // TILE

    def body(oret_ref, ogla_ref, mr_ref, mg_ref, h0_ref, tgt_ref, gf_ref, wbr_hbm, wbg_hbm, wout_hbm,
             dh1_ref, dmr_ref, dmg_ref, doret_ref, dogla_ref, loss_ref, dgf_ref, dwbr_hbm, dwbg_hbm, dwout_hbm,
             wbr, wbg, wout, abr, abg, aout, sem):
        i = pl.program_id(0)

        @pl.when(i == 0)
        def _():
            cps = [pltpu.make_async_copy(s, d, sem.at[n]) for n, (s, d) in enumerate(((wbr_hbm, wbr), (wbg_hbm, wbg), (wout_hbm, wout)))]
            for cp in cps:
                cp.start()
            abr[...] = jnp.zeros_like(abr)
            abg[...] = jnp.zeros_like(abg)
            aout[...] = jnp.zeros_like(aout)
            loss_ref[...] = jnp.zeros_like(loss_ref)
            dgf_ref[...] = jnp.zeros_like(dgf_ref)
            for cp in cps:
                cp.wait()
            dh1_ref[...] = jnp.zeros_like(dh1_ref)
            dmr_ref[...] = jnp.zeros_like(dmr_ref)
            dmg_ref[...] = jnp.zeros_like(dmg_ref)
            doret_ref[...] = jnp.zeros_like(doret_ref)
            dogla_ref[...] = jnp.zeros_like(dogla_ref)

        @pl.when(i > 0)
        def _():
            oret, ogla = oret_ref[...], ogla_ref[...]
            br, bg = _mm(oret, wbr[...]), _mm(ogla, wbg[...])
            sr, sg = _sigmoid(_cols(mr_ref).astype(F32)), _sigmoid(_cols(mg_ref).astype(F32))
            mb = (sr * br + sg * bg).astype(BF16)
            h1 = h0_ref[...] + _mm(mb, wout[...])
            r2 = lax.rsqrt(_row_mean(h1 * h1) + EPS)
            hn = h1 * r2
            gf = gf_ref[...]
            diff = hn * gf - tgt_ref[...]
            loss_ref[...] += 0.5 * jnp.sum(_row_mean(diff * diff))
            dy = diff * (1.0 / D_MODEL)
            dgf_ref[...] += _col_sum(dy * hn)
            dyg = dy * gf
            dh1 = r2 * (dyg - hn * _row_mean(dyg * hn))
            dh1_ref[...] = dh1
            dh1b = dh1.astype(BF16)
            dm = _mm_nt(dh1b, wout[...])
            aout[...] += _mm_tn(mb, dh1b)
            dbr = (dm * sr).astype(BF16)
            dbg = (dm * sg).astype(BF16)
            dmr_ref[...] = (dm * br * sr * (1.0 - sr)).astype(BF16)
            dmg_ref[...] = (dm * bg * sg * (1.0 - sg)).astype(BF16)
            doret_ref[...] = _mm_nt(dbr, wbr[...]).astype(BF16)
            dogla_ref[...] = _mm_nt(dbg, wbg[...]).astype(BF16)
            abr[...] += _mm_tn(oret, dbr)
            abg[...] += _mm_tn(ogla, dbg)

        @pl.when(i == nt - 1)
        def _():
            wbr[...] = abr[...].astype(BF16)
            wbg[...] = abg[...].astype(BF16)
            wout[...] = aout[...].astype(BF16)
            pltpu.sync_copy(wbr, dwbr_hbm)
            pltpu.sync_copy(wbg, dwbg_hbm)
            pltpu.sync_copy(wout, dwout_hbm)

    row = lambda w: pl.BlockSpec((TILE, w), lambda i: (i, 0))
    one = lambda w: pl.BlockSpec((1, w), lambda i: (0, 0))
    return _call(
        body, "merge_fwd_bwd", grid=(nt,),
        out_shape=[jax.ShapeDtypeStruct((t_rows, D_MODEL), F32), jax.ShapeDtypeStruct((t_rows, D_MODEL), BF16),
                   jax.ShapeDtypeStruct((t_rows, D_MODEL), BF16), jax.ShapeDtypeStruct((t_rows, RET_W), BF16),
                   jax.ShapeDtypeStruct((t_rows, GLA_W), BF16), jax.ShapeDtypeStruct((1, LANES), F32),
                   jax.ShapeDtypeStruct((1, D_MODEL), F32), jax.ShapeDtypeStruct((RET_W, D_MODEL), BF16),
                   jax.ShapeDtypeStruct((GLA_W, D_MODEL), BF16), jax.ShapeDtypeStruct((D_MODEL, D_MODEL), BF16)],
        in_specs=[row(RET_W), row(GLA_W)] + _proj_specs(("mr", "mg"), 1, lambda i: (0, i)) + [_x_spec(), _x_spec(), one(D_MODEL), ANY, ANY, ANY],
        out_specs=[row(D_MODEL), row(D_MODEL), row(D_MODEL), row(RET_W), row(GLA_W), one(LANES), one(D_MODEL), ANY, ANY, ANY],
        scratch_shapes=[pltpu.VMEM((RET_W, D_MODEL), BF16), pltpu.VMEM((GLA_W, D_MODEL), BF16), pltpu.VMEM((D_MODEL, D_MODEL), BF16),
                        pltpu.VMEM((RET_W, D_MODEL), F32), pltpu.VMEM((GLA_W, D_MODEL), F32), pltpu.VMEM((D_MODEL, D_MODEL), F32),
                        pltpu.SemaphoreType.DMA((3,))],
        compiler_params=_params(("arbitrary",)),
    )(o_ret, o_gla, proj, proj, x, target, g_final, w_br, w_bg, w_out)


def _inproj_bwd_x(dseg, dglr, head, x, dh1, g_norm, slabs, w_glr, chip_partials):
    t_rows = x.shape[0] + TILE
    nt = t_rows // TILE
    ne = len(chip_partials)

    def body(*refs):
        d_refs = refs[:10]
        dglr_ref, head_ref, x_ref, dh1_ref, g_ref, slabs_hbm, wg_hbm = refs[10:17]
        part_refs = refs[17:17 + ne]
        dx_ref, dhead_ref, dgn_ref = refs[17 + ne:20 + ne]
        landed = refs[20 + ne:20 + 2 * ne]
        w_vm, wg_vm, edge_vm, sem = refs[20 + 2 * ne:24 + 2 * ne]
        exchange = _Exchange(part_refs, landed, refs[24 + 2 * ne:], among_chips=True)

        @pl.when(pl.program_id(0) == 0)
        def _():
            exchange.start()
            dgn_ref[...] = jnp.zeros_like(dgn_ref)
            _load_weight(slabs_hbm, wg_hbm, w_vm, wg_vm, edge_vm, sem)

        @pl.when(pl.program_id(0) == nt - 1)
        def _():
            exchange.finish()

        dglr = dglr_ref[0].astype(F32)
        for h in range(1, GLA_HEADS):
            dglr = dglr + dglr_ref[h].astype(F32)
        du = _mm_nt(dglr.astype(BF16), wg_vm[...])
        for s, d_ref in enumerate(d_refs):
            du = du + _mm_nt(d_ref[...], w_vm[:, SEG_OFF[s]:SEG_OFF[s] + SEG_W[s]])
        x = _tile_rows(head_ref, x_ref)
        r = lax.rsqrt(_row_mean(x * x) + EPS)
        hn = x * r
        dgn_ref[...] += _col_sum(du * hn)
        dug = du * g_ref[...]
        dh0 = dh1_ref[...] + r * (dug - hn * _row_mean(dug * hn))
        dx_ref[...] = dh0

        @pl.when(pl.program_id(0) == 0)
        def _():
            dhead_ref[...] = dh0

    row = lambda w: pl.BlockSpec((TILE, w), lambda i: (i, 0))
    one = pl.BlockSpec((1, D_MODEL), lambda i: (0, 0))
    return _call(
        body, "inproj_bwd_x", grid=(nt,),
        out_shape=[jax.ShapeDtypeStruct((t_rows - TILE, D_MODEL), F32), jax.ShapeDtypeStruct((TILE, D_MODEL), F32),
                   jax.ShapeDtypeStruct((1, D_MODEL), F32)] + [jax.ShapeDtypeStruct(a.shape, a.dtype) for a in chip_partials],
        in_specs=[row(w) for w in SEG_W] + [pl.BlockSpec((GLA_HEADS, TILE, LANES), lambda i: (0, i, 0)),
                                            _head_spec(), _x_spec(), row(D_MODEL), one, ANY, ANY] + [ANY] * ne,
        out_specs=[_x_spec(), _head_spec(), one] + [ANY] * ne,
        scratch_shapes=W_SCRATCH() + _exchange_sems(ne, N_CHIP),
        compiler_params=_params(("arbitrary",)),
    )(*[dseg[n] for n in SEG_NAMES], dglr, head, x, dh1, g_norm, slabs, w_glr, *chip_partials)


W_TILE = 512


def _inproj_bwd_w(ut, dseg, dglr):
    nt = ut.shape[0]
    t_rows = nt * TILE
    kc = 3 if nt % 3 == 0 else 1
    tiles = [(s, c) for s in range(len(SEG_W)) for c in range(0, SEG_W[s], W_TILE)]
    bpt = W_TILE // LANES

    def body(ut_hbm, *refs):
        d_refs, dglr_hbm, out_hbm, oglr_ref = refs[:10], refs[10], refs[11], refs[12]
        ut_vm, dbuf, obuf, acc, gbuf, sem = refs[13:]

        def fetch(i):
            s, c = tiles[i]
            return pltpu.make_async_copy(d_refs[s].at[:, pl.ds(c, W_TILE)], dbuf.at[i % 2], sem.at[1 + i % 2])

        def contract(rhs_refs, width):
            acc[:, :width] = jnp.zeros((D_MODEL, width), F32)

            def step(k, carry):
                part = None
                for j in range(kc):
                    kk = k * kc + j
                    for rhs_ref in rhs_refs:
                        prod = _mm(ut_vm[kk], rhs_ref[pl.ds(pl.multiple_of(kk * TILE, TILE), TILE), :])
                        part = prod if part is None else part + prod
                acc[:, :width] += part
                return carry

            lax.fori_loop(0, nt // kc, step, 0)
            return acc[:, :width]

        load_ut = pltpu.make_async_copy(ut_hbm, ut_vm, sem.at[0])
        load_glr = pltpu.make_async_copy(dglr_hbm, gbuf, sem.at[5])
        load_ut.start()
        load_glr.start()
        fetch(0).start()
        load_ut.wait()
        stores = {}
        for i, (s, c) in enumerate(tiles):
            if i + 1 < len(tiles):
                fetch(i + 1).start()
            fetch(i).wait()
            if i >= 2:
                stores[i - 2].wait()
            total = contract([dbuf.at[i % 2]], W_TILE)
            for j in range(bpt):
                obuf[i % 2, j] = total[:, j * LANES:(j + 1) * LANES].astype(BF16)
            blk0 = (SEG_OFF[s] + c) // LANES
            stores[i] = pltpu.make_async_copy(obuf.at[i % 2], out_hbm.at[pl.ds(blk0, bpt)], sem.at[3 + i % 2])
            stores[i].start()
        load_glr.wait()
        oglr_ref[...] = contract([gbuf.at[h] for h in range(GLA_HEADS)], LANES)
        for i in range(max(0, len(tiles) - 2), len(tiles)):
            stores[i].wait()

    return _call(
        body, "inproj_bwd_w",
        out_shape=[jax.ShapeDtypeStruct((AL_COLS // LANES, D_MODEL, LANES), BF16), jax.ShapeDtypeStruct((D_MODEL, LANES), F32)],
        in_specs=[ANY] * 12, out_specs=[ANY, pl.BlockSpec(memory_space=pltpu.VMEM)],
        scratch_shapes=[pltpu.VMEM((nt, D_MODEL, TILE), BF16), pltpu.VMEM((2, t_rows, W_TILE), BF16),
                        pltpu.VMEM((2, bpt, D_MODEL, LANES), BF16), pltpu.VMEM((D_MODEL, W_TILE), F32),
                        pltpu.VMEM((GLA_HEADS, t_rows, LANES), BF16), pltpu.SemaphoreType.DMA((6,))],
        compiler_params=_params(),
    )(ut, *[dseg[n] for n in SEG_NAMES], dglr)


def _position():
    x, y, c = lax.axis_index("x"), lax.axis_index("y"), lax.axis_index("c")
    return x, y, c


def _index(px, py, pc):
    return 4 * px + 2 * py + pc


def _all_gather(arrs, name):
    n = len(arrs)

    def body(*refs):
        ins, outs = refs[:n], refs[n:2 * n]
        send_sems, recv_sems, local_sems = refs[2 * n:]
        x, y, c = _position()
        me, sibling = (x, y, c), (x, y, 1 - c)
        chips = [(1 - x, y), (x, 1 - y), (1 - x, 1 - y)]

        def copy(a, k, block, to, src=None):
            dst = outs[a].at[_index(*block)]
            return pltpu.make_async_remote_copy(src_ref=dst if src is None else src, dst_ref=dst,
                                                send_sem=send_sems.at[7 * a + k], recv_sem=recv_sems.at[7 * a + k],
                                                device_id=to, device_id_type=MESH)

        mine = [pltpu.make_async_copy(ins[a], outs[a].at[_index(*me)], local_sems.at[a]) for a in range(n)]
        for cp in mine:
            cp.start()
        first = []
        for a in range(n):
            first.append(copy(a, 0, me, sibling, src=ins[a]))
            first += [copy(a, 1 + j, me, (*chip, c), src=ins[a]) for j, chip in enumerate(chips)]
        for cp in first:
            cp.start()
        passed = []
        for j, chip in enumerate(chips):
            for a in range(n):
                copy(a, 1 + j, (*chip, c), me).wait_recv()
                cp = copy(a, 4 + j, (*chip, c), sibling)
                cp.start()
                passed.append(cp)
        for a in range(n):
            copy(a, 0, sibling, me).wait_recv()
            for j, chip in enumerate(chips):
                copy(a, 4 + j, (*chip, 1 - c), me).wait_recv()
        for cp in first + passed:
            cp.wait_send()
        for cp in mine:
            cp.wait()

    return _call(
        body, name,
        out_shape=[jax.ShapeDtypeStruct((N_DEV, *a.shape), a.dtype) for a in arrs],
        in_specs=[ANY] * n, out_specs=[ANY] * n,
        scratch_shapes=[pltpu.SemaphoreType.DMA((7 * n,)), pltpu.SemaphoreType.DMA((7 * n,)), pltpu.SemaphoreType.DMA((n,))],
    )(*arrs)


N_CHIP = N_DEV // 2


def _slab_block0(owner):
    step = SLAB_BLK0[1]
    assert all(SLAB_BLK0[d] == step * d - (d == N_DEV - 1) for d in range(N_DEV))
    return step * owner - jnp.where(owner == N_DEV - 1, 1, 0)


def _exchange_sibling(dw_blocks, row_sends):
    n = 1 + len(row_sends)

    def body(*refs):
        dw_ref, row_refs, outs, (send_sems, recv_sems) = refs[0], refs[1:n], refs[n:2 * n], refs[2 * n:]
        x, y, c = _position()
        copies = []
        for q in range(N_CHIP):
            owner = 2 * q + (1 - c)
            srcs = [dw_ref.at[pl.ds(_slab_block0(owner), SLAB_BLOCKS)]] + [r.at[owner] for r in row_refs]
            for k, src in enumerate(srcs):
                copies.append(pltpu.make_async_remote_copy(src_ref=src, dst_ref=outs[k].at[q], send_sem=send_sems.at[n * q + k],
                                                           recv_sem=recv_sems.at[n * q + k], device_id=(x, y, 1 - c), device_id_type=MESH))
        for cp in copies:
            cp.start()
        for cp in copies:
            cp.wait()

    return _call(
        body, "exchange_sibling",
        out_shape=[jax.ShapeDtypeStruct((N_CHIP, SLAB_BLOCKS, D_MODEL, LANES), BF16)]
                  + [jax.ShapeDtypeStruct((N_CHIP, *r.shape[1:]), BF16) for r in row_sends],
        in_specs=[ANY] * n, out_specs=[ANY] * n,
        scratch_shapes=[pltpu.SemaphoreType.DMA((n * N_CHIP,)), pltpu.SemaphoreType.DMA((n * N_CHIP,))],
    )(dw_blocks, *row_sends)


def _add_bf16(c_ref, a_ref, b_ref, o_ref):
    o_ref[...] = (a_ref[...].astype(F32) + b_ref[...].astype(F32)).astype(BF16)


def _chip_partial_slab(dw_blocks, sib, core):
    blk = pl.BlockSpec((None, SLAB_BLOCKS, D_MODEL, LANES), lambda q, c_ref: (q, 0, 0, 0))
    return _call(
        functools.partial(_add_bf16), "chip_partial_w_in", out_shape=jax.ShapeDtypeStruct(sib.shape, BF16),
        grid_spec=pltpu.PrefetchScalarGridSpec(
            num_scalar_prefetch=1, grid=(N_CHIP,),
            in_specs=[pl.BlockSpec((pl.Element(SLAB_BLOCKS), pl.Element(D_MODEL), pl.Element(LANES)),
                                   lambda q, c_ref: (_slab_block0(2 * q + c_ref[0]), 0, 0)), blk],
            out_specs=blk),
        compiler_params=_params(("arbitrary",)),
    )(core, dw_blocks, sib)


def _chip_partial_rows(send, sib, core, name):
    rows, cols = send.shape[1:]
    blk = pl.BlockSpec((None, rows, cols), lambda q, c_ref: (q, 0, 0))
    return _call(
        functools.partial(_add_bf16), name, out_shape=jax.ShapeDtypeStruct(sib.shape, BF16),
        grid_spec=pltpu.PrefetchScalarGridSpec(
            num_scalar_prefetch=1, grid=(N_CHIP,),
            in_specs=[pl.BlockSpec((None, rows, cols), lambda q, c_ref: (2 * q + c_ref[0], 0, 0)), blk], out_specs=blk),
        compiler_params=_params(("arbitrary",)),
    )(core, send, sib)


def _exchange_sems(n_arrays, n_peers):
    return [pltpu.SemaphoreType.DMA((n_arrays * n_peers,)), pltpu.SemaphoreType.DMA((n_arrays * n_peers,)),
            pltpu.SemaphoreType.DMA((n_arrays,))]


class _Exchange:
    def __init__(self, srcs, dsts, sems, among_chips):
        self.arrs = list(zip(srcs, dsts))
        self.n = len(self.arrs)
        self.send_sems, self.recv_sems, self.local_sems = sems
        self.among_chips = among_chips
        x, y, c = _position()
        self.c = c
        self.me = 2 * x + y if among_chips else _index(x, y, c)
        self.n_peers = N_CHIP if among_chips else N_DEV

    def _device(self, p):
        return (p // 2, p % 2, self.c) if self.among_chips else (p // 4, (p // 2) % 2, p % 2)

    def _src(self, k, p):
        src = self.arrs[k][0]
        return src.at[p] if self.among_chips else src

    def _mine(self):
        return [pltpu.make_async_copy(self._src(k, self.me), self.arrs[k][1].at[self.me], self.local_sems.at[k]) for k in range(self.n)]

    def _copy(self, p, k, landing):
        return pltpu.make_async_remote_copy(
            src_ref=self._src(k, p), dst_ref=self.arrs[k][1].at[landing], send_sem=self.send_sems.at[self.n * p + k],
            recv_sem=self.recv_sems.at[self.n * landing + k], device_id=self._device(p), device_id_type=MESH)

    def _others(self, fn):
        for p in range(self.n_peers):
            @pl.when(p != self.me)
            def _():
                for k in range(self.n):
                    fn(p, k)

    def start(self):
        for cp in self._mine():
            cp.start()
        self._others(lambda p, k: self._copy(p, k, self.me).start())

    def finish(self):
        self._others(lambda p, k: self._copy(p, k, p).wait_recv())
        self._others(lambda p, k: self._copy(p, k, self.me).wait_send())
        for cp in self._mine():
            cp.wait()


def _adamw(g, w, m, v):
    m_new = ADAM_B1 * m + (1.0 - ADAM_B1) * g
    v_new = ADAM_B2 * v + (1.0 - ADAM_B2) * (g * g)
    m_hat = m_new / (1.0 - ADAM_B1 ** ADAM_STEP)
    v_hat = v_new / (1.0 - ADAM_B2 ** ADAM_STEP)
    delta = -ADAM_LR * (m_hat / (jnp.sqrt(v_hat) + ADAM_EPS) + ADAM_WD * w)
    return delta, m_new, v_new


def _sum_partials(p_ref):
    g = p_ref[0].astype(F32)
    for d in range(1, p_ref.shape[0]):
        g = g + p_ref[d].astype(F32)
    return g


def _reduce_adam(parts, w, m, v, name, block_rows, row_off=0):
    rows, cols = w.shape
    off = row_off // block_rows

    def body(p_ref, w_ref, m_ref, v_ref, g_ref, d_ref, mo_ref, vo_ref):
        g = _sum_partials(p_ref)
        g_ref[...] = g
        d_ref[...], mo_ref[...], vo_ref[...] = _adamw(g, w_ref[...], m_ref[...], v_ref[...])

    blk = pl.BlockSpec((block_rows, cols), lambda i: (i, 0))
    return _call(
        body, name, grid=(rows // block_rows,),
        out_shape=[jax.ShapeDtypeStruct((rows, cols), F32)] * 4,
        in_specs=[pl.BlockSpec((parts.shape[0], block_rows, cols), lambda i: (0, i + off, 0)), blk, blk, blk],
        out_specs=[blk] * 4,
        compiler_params=_params(("arbitrary",)),
    )(parts, w, m, v)


def _reduce_adam_slab(parts, glr, w, m, v, me):
    rows, cols = w.shape
    shift = jnp.asarray(SLAB_SHIFT, jnp.int32)[me]
    glr_at = jnp.where(me == GLR_DEV, GLR_LOCAL, cols).astype(jnp.int32)

    def body(s_ref, p_ref, glr_ref, w_ref, m_ref, v_ref, g_ref, d_ref, mo_ref, vo_ref):
        shift, glr_at = s_ref[0], s_ref[1]
        slab = jnp.concatenate([_sum_partials(p_ref.at[:, j]) for j in range(SLAB_BLOCKS)], axis=1)
        before = pltpu.roll(slab, SLAB_W - shift, 1)
        after = pltpu.roll(slab, lax.rem(SLAB_W - shift + GLA_RANK, SLAB_W), 1)
        wide = jnp.concatenate([glr_ref[...], jnp.zeros((LANES, SLAB_W - LANES), F32)], axis=1)
        placed = pltpu.roll(wide, lax.rem(glr_at, SLAB_W), 1)
        lane = lax.broadcasted_iota(jnp.int32, (LANES, SLAB_W), 1)
        g = jnp.where(lane < glr_at, before, jnp.where(lane < glr_at + GLA_RANK, placed, after))[:, :cols]
        g_ref[...] = g
        d_ref[...], mo_ref[...], vo_ref[...] = _adamw(g, w_ref[...], m_ref[...], v_ref[...])

    blk = pl.BlockSpec((LANES, cols), lambda i, s: (i, 0))
    return _call(
        body, "adam_w_in", out_shape=[jax.ShapeDtypeStruct((rows, cols), F32)] * 4,
        grid_spec=pltpu.PrefetchScalarGridSpec(
            num_scalar_prefetch=1, grid=(rows // LANES,),
            in_specs=[pl.BlockSpec((parts.shape[0], SLAB_BLOCKS, LANES, LANES), lambda i, s: (0, 0, i, 0)),
                      pl.BlockSpec((LANES, LANES), lambda i, s: (i, 0)), blk, blk, blk],
            out_specs=[blk] * 4),
        compiler_params=_params(("arbitrary",)),
    )(jnp.stack([shift, glr_at]), parts, glr, w, m, v)


def _reduce_small(parts):
    def body(p_ref, o_ref):
        o_ref[...] = _sum_partials(p_ref)

    return _call(body, "reduce_small", out_shape=jax.ShapeDtypeStruct(parts.shape[1:], F32))(parts)


def _adam_small(g, w, m, v):
    def body(g_ref, w_ref, m_ref, v_ref, d_ref, mo_ref, vo_ref):
        d_ref[...], mo_ref[...], vo_ref[...] = _adamw(g_ref[...], w_ref[...], m_ref[...], v_ref[...])

    return _call(body, "adam_small", out_shape=[jax.ShapeDtypeStruct(g.shape, F32)] * 3)(g, w, m, v)


def _pack_rows(arrs):
    rows = []
    for a in arrs:
        flat = a.reshape(-1).astype(F32)
        pad = (-flat.shape[0]) % LANES
        rows.append(jnp.pad(flat, (0, pad)).reshape(-1, LANES))
    packed = jnp.concatenate(rows, axis=0)
    return jnp.pad(packed, ((0, (-packed.shape[0]) % 8), (0, 0)))


def _unpack_rows(packed, shapes):
    out, r = [], 0
    for shp in shapes:
        size = 1
        for s in shp:
            size *= s
        nrows = -(-size // LANES)
        out.append(packed[r:r + nrows].reshape(-1)[:size].reshape(shp))
        r += nrows
    return out


def _shard_to_slab(shard, d):
    glr = jnp.zeros((D_MODEL, GLA_RANK), shard.dtype)
    if d == GLR_DEV:
        glr = shard[:, GLR_LOCAL:GLR_LOCAL + GLA_RANK]
        shard = jnp.concatenate([shard[:, :GLR_LOCAL], shard[:, GLR_LOCAL + GLA_RANK:]], axis=1)
    return jnp.pad(shard, ((0, 0), (SLAB_SHIFT[d], SLAB_W - SLAB_SHIFT[d] - shard.shape[1]))), glr


def kernel(x, meta_tokens, norm_gain, w_in, w_gate_up, b_gate, ret_norm_gain, gla_norm_gain, w_branch_ret, w_branch_gla, w_out, final_norm_gain, loss_target, m_meta_tokens, m_norm_gain, m_w_in, m_w_gate_up, m_b_gate, m_ret_norm_gain, m_gla_norm_gain, m_w_branch_ret, m_w_branch_gla, m_w_out, m_final_norm_gain, v_meta_tokens, v_norm_gain, v_w_in, v_w_gate_up, v_b_gate, v_ret_norm_gain, v_gla_norm_gain, v_w_branch_ret, v_w_branch_gla, v_w_out, v_final_norm_gain):
    xi, yi, ci = _position()
    me = _index(xi, yi, ci)
    seq = x.shape[1]
    t_rows = seq + TILE
    in_shard = w_in.shape[2]
    gu_shard = w_gate_up.shape[2]
    meta_shard = meta_tokens.shape[1]
    ret_rows, gla_rows, out_rows = w_branch_ret.shape[1], w_branch_gla.shape[1], w_out.shape[1]

    assert in_shard == IN_SHARD
    slab_local, glr_local = lax.switch(me, [functools.partial(_shard_to_slab, d=d) for d in range(N_DEV)], w_in[0])
    small_local = jnp.concatenate([meta_tokens, jnp.pad(w_gate_up[0], ((0, 0), (0, LANES - gu_shard))),
                                   glr_local.reshape(-1, LANES)], axis=0)
    slabs, g_small = _all_gather([slab_local.astype(BF16), small_local], "all_gather_shards")
    n_small = N_META + GLA_RANK
    w_glr = jnp.pad(g_small[GLR_DEV, n_small:].reshape(D_MODEL, GLA_RANK), ((0, 0), (0, LANES - GLA_RANK))).astype(BF16)
    meta_full = jnp.transpose(g_small[:, :N_META, :], (1, 0, 2)).reshape(N_META, D_MODEL)
    wgu_full = jnp.transpose(g_small[:, N_META:n_small, :gu_shard], (1, 0, 2)).reshape(GLA_RANK, GLA_HEADS * GLA_K)
    wgu_pad = jnp.pad(wgu_full, ((0, LANES - GLA_RANK), (0, 0)))

    pos = jnp.arange(t_rows, dtype=F32) - float(PAD_ROWS)
    half = RET_QK // 2
    inv = ROPE_BASE ** (-jnp.arange(half, dtype=F32) / half)
    ang = pos[:, None] * inv[None, :]
    cos, sin = jnp.cos(ang), jnp.sin(ang)
    lg = jnp.log1p(-(2.0 ** (-5.0 - jnp.arange(RET_HEADS, dtype=F32))))

    head = jnp.concatenate([jnp.zeros((PAD_ROWS, D_MODEL), F32), meta_full], axis=0)
    ut, proj, glr = _inproj_tiles(head, x[0], norm_gain, slabs, w_glr)
    o_ret_raw, o_ret, ret_states, (g_br, g_bg, g_o) = _ret_fwd(
        proj, cos, sin, ret_norm_gain, lg, [w_branch_ret[0].astype(BF16), w_branch_gla[0].astype(BF16), w_out[0].astype(BF16)])
    w_br, w_bg, w_o = g_br.reshape(RET_W, D_MODEL), g_bg.reshape(GLA_W, D_MODEL), g_o.reshape(D_MODEL, D_MODEL)
    masks, cum_fwd, cum_bwd = _gla_tables()
    o_gla_raw, o_gla, gla_states = _gla_fwd(proj, glr, wgu_pad, b_gate, gla_norm_gain, masks, cum_fwd)
    (dh1, d_mr, d_mg, do_ret, do_gla, loss_part, d_gfinal, dw_br, dw_bg, dw_o) = _merge_fwd_bwd(
        o_ret, o_gla, proj, x[0], loss_target[0], final_norm_gain.reshape(1, D_MODEL), w_br, w_bg, w_o)

    d_rq, d_rk, d_rv, d_rg, d_gret = _ret_bwd(proj, cos, sin, ret_norm_gain, lg, o_ret_raw, do_ret, ret_states)
    d_gq, d_gk, d_gv, d_gg, dglr_parts, d_wgu, d_bgate, d_ggla = _gla_bwd(
        proj, glr, wgu_pad, b_gate, gla_norm_gain, o_gla_raw, do_gla, gla_states, masks, cum_fwd, cum_bwd)
    dseg = dict(rq=d_rq, rk=d_rk, rv=d_rv, rg=d_rg, gq=d_gq, gk=d_gk, gv=d_gv, gg=d_gg, mr=d_mr, mg=d_mg)
    dw_blocks, dw_glr = _inproj_bwd_w(ut, dseg, dglr_parts)

    row_sends = [dw_br.reshape(N_DEV, ret_rows, D_MODEL), dw_bg.reshape(N_DEV, gla_rows, D_MODEL),
                 dw_o.reshape(N_DEV, out_rows, D_MODEL)]
    sib_in, *sib_rows = _exchange_sibling(dw_blocks, row_sends)
    core = ci.astype(jnp.int32).reshape(1)
    chip_partials = [_chip_partial_slab(dw_blocks, sib_in, core)] + [
        _chip_partial_rows(send, sib, core, "chip_partial_" + name)
        for send, sib, name in zip(row_sends, sib_rows, ("w_branch_ret", "w_branch_gla", "w_out"))]
    grad_x, d_head, d_gnorm, p_in, p_br, p_bg, p_o = _inproj_bwd_x(
        dseg, dglr_parts, head, x[0], dh1, norm_gain, slabs, w_glr, chip_partials)
    small_shapes = [(N_META, D_MODEL), (1, D_MODEL), (GLA_RANK, GLA_HEADS * GLA_K), (1, GLA_HEADS * GLA_K),
                    (1, RET_W), (1, GLA_W), (1, D_MODEL), (1, LANES), (D_MODEL, GLA_RANK)]
    small_part = _pack_rows([d_head[PAD_ROWS:], d_gnorm, d_wgu[:GLA_RANK], d_bgate, d_gret, d_ggla, d_gfinal, loss_part,
                             dw_glr[:, :GLA_RANK]])
    (p_small,) = _all_gather([small_part], "all_gather_small_partials")

    (g_meta_f, g_gnorm, g_wgu_f, g_bgate, g_gret, g_ggla, g_gfinal, loss_all,
     g_wglr) = _unpack_rows(_reduce_small(p_small), small_shapes)
    g_w_in, d_w_in, nm_w_in, nv_w_in = _reduce_adam_slab(
        p_in, jnp.pad(g_wglr, ((0, 0), (0, LANES - GLA_RANK))), w_in[0], m_w_in[0], v_w_in[0], me)
    rb = gla_rows
    g_w_br, d_w_br, nm_w_br, nv_w_br = _reduce_adam(p_br, w_branch_ret[0], m_w_branch_ret[0], v_w_branch_ret[0], "adam_w_branch_ret", rb)
    g_w_bg, d_w_bg, nm_w_bg, nv_w_bg = _reduce_adam(p_bg, w_branch_gla[0], m_w_branch_gla[0], v_w_branch_gla[0], "adam_w_branch_gla", rb)
    g_w_o, d_w_o, nm_w_o, nv_w_o = _reduce_adam(p_o, w_out[0], m_w_out[0], v_w_out[0], "adam_w_out", rb)
    g_meta = lax.dynamic_slice_in_dim(g_meta_f, me * meta_shard, meta_shard, axis=1)
    g_wgu = lax.dynamic_slice_in_dim(g_wgu_f, me * gu_shard, gu_shard, axis=1)
    s_g = [g_meta, g_gnorm, g_wgu, g_bgate, g_gret, g_ggla, g_gfinal]
    s_w = [meta_tokens, norm_gain, w_gate_up[0], b_gate, ret_norm_gain, gla_norm_gain, final_norm_gain]
    s_m = [m_meta_tokens, m_norm_gain, m_w_gate_up[0], m_b_gate, m_ret_norm_gain, m_gla_norm_gain, m_final_norm_gain]
    s_v = [v_meta_tokens, v_norm_gain, v_w_gate_up[0], v_b_gate, v_ret_norm_gain, v_gla_norm_gain, v_final_norm_gain]
    shapes = [a.shape for a in s_g]
    s_d, s_nm, s_nv = [_unpack_rows(p, shapes) for p in _adam_small(*[_pack_rows(l) for l in (s_g, s_w, s_m, s_v)])]

    loss = loss_all[0, 0]
    grad_x = grad_x[None]

    def order(meta, gnorm, win, wgu, bgate, gret, ggla, wbr, wbg, wo, gfin):
        return (meta, gnorm, win[None], wgu[None], bgate, gret, ggla, wbr[None], wbg[None], wo[None], gfin.reshape(final_norm_gain.shape))

    def small(l):
        return dict(meta=l[0], gnorm=l[1], wgu=l[2], bgate=l[3], gret=l[4], ggla=l[5], gfin=l[6])

    grads = order(win=g_w_in, wbr=g_w_br, wbg=g_w_bg, wo=g_w_o, **small(s_g))
    deltas = order(win=d_w_in, wbr=d_w_br, wbg=d_w_bg, wo=d_w_o, **small(s_d))
    new_m = order(win=nm_w_in, wbr=nm_w_br, wbg=nm_w_bg, wo=nm_w_o, **small(s_nm))
    new_v = order(win=nv_w_in, wbr=nv_w_br, wbg=nv_w_bg, wo=nv_w_o, **small(s_nv))
    return (loss, grad_x, *grads, *deltas, *new_m, *new_v)
```

```python
import functools

import jax
import jax.numpy as jnp
from jax import lax
from jax.experimental import pallas as pl
from jax.experimental.pallas import tpu as pltpu

F32 = jnp.float32
BF16 = jnp.bfloat16

D_MODEL = 1024
N_META = 16
TILE = 256
PAD_ROWS = TILE - N_META
RET_HEADS = 4
RET_QK = 256
RET_V = 512
RET_W = RET_HEADS * RET_V
GLA_HEADS = 4
GLA_K = 128
GLA_V = 256
GLA_W = GLA_HEADS * GLA_V
GLA_RANK = 16
GLA_TAU = 16.0
GLA_CHUNK = 16
ROPE_BASE = 10000.0
EPS = 1e-6
LANES = 128
N_DEV = 8
SEG_NAMES = ("rq", "rk", "rv", "rg", "gq", "gk", "gv", "gg", "mr", "mg")
SEG_W = (1024, 1024, 2048, 2048, 512, 512, 1024, 1024, 1024, 1024)
SEG_OFF = tuple(sum(SEG_W[:i]) for i in range(len(SEG_W)))
AL_COLS = sum(SEG_W)
IN_COLS = AL_COLS + GLA_RANK
GLR_OFF = sum(SEG_W[:8])
IN_SHARD = IN_COLS // N_DEV


def _aligned_col(c):
    assert c <= GLR_OFF or c >= GLR_OFF + GLA_RANK
    return c if c <= GLR_OFF else c - GLA_RANK


SLAB_BOUND = tuple(_aligned_col(IN_SHARD * d) for d in range(N_DEV + 1))
SLAB_BLK0 = tuple(b // LANES for b in SLAB_BOUND[:-1])
SLAB_SHIFT = tuple(b % LANES for b in SLAB_BOUND[:-1])
SLAB_BLOCKS = max(-(-SLAB_BOUND[d + 1] // LANES) - SLAB_BLK0[d] for d in range(N_DEV))
SLAB_W = SLAB_BLOCKS * LANES
GLR_DEV = GLR_OFF // IN_SHARD
GLR_LOCAL = GLR_OFF - GLR_DEV * IN_SHARD
assert all(SLAB_BLK0[d] + SLAB_BLOCKS <= AL_COLS // LANES for d in range(N_DEV))
VMEM_LIMIT = 58 * 1024 * 1024
ADAM_LR, ADAM_B1, ADAM_B2, ADAM_EPS, ADAM_WD, ADAM_STEP = 0.001, 0.9, 0.999, 1e-08, 0.01, 10
ANY = pl.BlockSpec(memory_space=pl.ANY)
MESH = pl.DeviceIdType.MESH


def _call(body, name, **kw):
    return pl.pallas_call(body, name=name, **kw)


def _params(sem=None):
    return pltpu.CompilerParams(dimension_semantics=sem, vmem_limit_bytes=VMEM_LIMIT)


def _mm(a, b):
    return jnp.dot(a, b, preferred_element_type=F32)


def _mm_nt(a, b):
    return lax.dot_general(a, b, (((1,), (1,)), ((), ())), preferred_element_type=F32)


def _mm_tn(a, b):
    return lax.dot_general(a, b, (((0,), (0,)), ((), ())), preferred_element_type=F32)


def _sigmoid(x):
    return 1.0 / (1.0 + jnp.exp(-x))


def _rope(t, cos, sin):
    half = t.shape[-1] // 2
    t1, t2 = t[:, :half], t[:, half:]
    return jnp.concatenate([t1 * cos - t2 * sin, t2 * cos + t1 * sin], axis=-1)


def _rope_bwd(g, cos, sin):
    half = g.shape[-1] // 2
    g1, g2 = g[:, :half], g[:, half:]
    return jnp.concatenate([g1 * cos + g2 * sin, g2 * cos - g1 * sin], axis=-1)


def _row_mean(x):
    return jnp.mean(x, axis=-1, keepdims=True)


def _col_sum(x):
    return jnp.sum(x, axis=0, keepdims=True)


def _tile_rows(head_ref, x_ref):
    return jnp.where(pl.program_id(0) == 0, head_ref[...], x_ref[...])


def _head_spec():
    return pl.BlockSpec((TILE, D_MODEL), lambda i: (0, 0))


def _x_spec():
    return pl.BlockSpec((TILE, D_MODEL), lambda i: (jnp.maximum(i - 1, 0), 0))


def _slab_plan():
    interior, shared = [], []
    for d in range(N_DEV):
        lo, hi = -(-SLAB_BOUND[d] // LANES), SLAB_BOUND[d + 1] // LANES
        interior.append((d, LANES * (lo - SLAB_BLK0[d]), LANES * lo, LANES * (hi - lo)))
        if d + 1 < N_DEV and SLAB_BOUND[d + 1] % LANES:
            shared.append((hi, d, hi - SLAB_BLK0[d]))
    return interior, shared


W_SCRATCH = lambda: [pltpu.VMEM((D_MODEL, AL_COLS), BF16), pltpu.VMEM((D_MODEL, LANES), BF16),
                     pltpu.VMEM((2 * (N_DEV - 1), D_MODEL, LANES), BF16), pltpu.SemaphoreType.DMA((3 * N_DEV,))]


def _load_weight(slabs_hbm, wg_hbm, w_vm, wg_vm, edge_vm, sem):
    interior, shared = _slab_plan()
    copies = [pltpu.make_async_copy(wg_hbm, wg_vm, sem.at[0])]
    for d, src, dst, width in interior:
        copies.append(pltpu.make_async_copy(slabs_hbm.at[d, :, pl.ds(src, width)], w_vm.at[:, pl.ds(dst, width)], sem.at[1 + d]))
    for n, (_, d, blk) in enumerate(shared):
        copies.append(pltpu.make_async_copy(slabs_hbm.at[d, :, pl.ds(LANES * blk, LANES)], edge_vm.at[2 * n], sem.at[1 + N_DEV + 2 * n]))
        copies.append(pltpu.make_async_copy(slabs_hbm.at[d + 1, :, pl.ds(0, LANES)], edge_vm.at[2 * n + 1], sem.at[2 + N_DEV + 2 * n]))
    for cp in copies:
        cp.start()
    for cp in copies:
        cp.wait()
    for n, (blk, _, _) in enumerate(shared):
        w_vm[:, LANES * blk:LANES * (blk + 1)] = edge_vm[2 * n] + edge_vm[2 * n + 1]


def _proj_specs(names, n_units, where):
    specs = []
    for name in names:
        s = SEG_NAMES.index(name)
        nblk = SEG_W[s] // n_units // LANES
        base = SEG_OFF[s] // LANES
        assert base % nblk == 0
        specs.append(pl.BlockSpec((nblk, TILE, LANES), lambda *g, base=base, nblk=nblk: (base // nblk + where(*g)[0], where(*g)[1], 0)))
    return specs


def _cols(ref, unit=0, n_units=1):
    n = ref.shape[0] // n_units
    return ref[unit * n] if n == 1 else jnp.concatenate([ref[unit * n + j] for j in range(n)], axis=1)


def _prenorm(head, x, g_norm, w_glr):
    t_rows = x.shape[0] + TILE
    nt = t_rows // TILE

    def body(head_ref, x_ref, g_ref, wg_ref, u_ref, ut_ref, glr_ref):
        x = _tile_rows(head_ref, x_ref)
        r = lax.rsqrt(_row_mean(x * x) + EPS)
        u32 = (x * r * g_ref[...]).astype(BF16).astype(F32)
        u = u32.astype(BF16)
        u_ref[...] = u
        ut_ref[...] = u32.T.astype(BF16)
        glr_ref[...] = _mm(u, wg_ref[...])

    row = lambda w: pl.BlockSpec((TILE, w), lambda i: (i, 0))
    return _call(
        body, "prenorm", grid=(nt,),
        out_shape=[jax.ShapeDtypeStruct((t_rows, D_MODEL), BF16), jax.ShapeDtypeStruct((nt, D_MODEL, TILE), BF16),
                   jax.ShapeDtypeStruct((t_rows, LANES), F32)],
        in_specs=[_head_spec(), _x_spec(), pl.BlockSpec((1, D_MODEL), lambda i: (0, 0)), pl.BlockSpec((D_MODEL, LANES), lambda i: (0, 0))],
        out_specs=[row(D_MODEL), pl.BlockSpec((None, D_MODEL, TILE), lambda i: (i, 0, 0)), row(LANES)],
        compiler_params=_params(("arbitrary",)),
    )(head, x, g_norm, w_glr)


SLAB_INNER = 9


def _edge_blocks():
    inner = {SLAB_BLK0[d] + j for d in range(N_DEV) for j in range(1, 1 + SLAB_INNER)}
    edges = []
    for blk in range(AL_COLS // LANES):
        if blk not in inner:
            srcs = [(d, blk - SLAB_BLK0[d]) for d in range(N_DEV)
                    if SLAB_BLK0[d] <= blk < SLAB_BLK0[d] + SLAB_BLOCKS and SLAB_BOUND[d] < LANES * (blk + 1) and LANES * blk < SLAB_BOUND[d + 1]]
            edges.append((blk, srcs))
    return edges


def _inproj_fwd(u, slab_local):
    t_rows = u.shape[0]
    nt = t_rows // TILE
    rc = (3 if nt % 3 == 0 else 1) * TILE
    n_chunks = t_rows // rc
    edges = _edge_blocks()
    ne = len(edges)
    runs = []
    for k, (blk, _) in enumerate(edges):
        if runs and edges[runs[-1][0] + runs[-1][1] - 1][0] + 1 == blk:
            runs[-1] = (runs[-1][0], runs[-1][1] + 1)
        else:
            runs.append((k, 1))
    n_stage = sum(len(srcs) for _, srcs in edges)

    def body(u_hbm, slab_hbm, proj_hbm, slabs_hbm, u_vm, wbuf, obuf, ebuf, stage, ebuf_out, sem_u, sem_w, sem_o, sem_s, sem_eo,
             send_sems, recv_sems, sem_l):
        x, y, c = _position()
        me, sibling = (x, y, c), (x, y, 1 - c)
        chips = [(1 - x, y), (x, 1 - y), (1 - x, 1 - y)]

        def slab_copy(k, block, to, src=None):
            dst = slabs_hbm.at[_index(*block)]
            return pltpu.make_async_remote_copy(src_ref=dst if src is None else src, dst_ref=dst, send_sem=send_sems.at[k],
                                                recv_sem=recv_sems.at[k], device_id=to, device_id_type=MESH)

        mine = pltpu.make_async_copy(slab_hbm, slabs_hbm.at[_index(*me)], sem_l)
        mine.start()
        first = [slab_copy(0, me, sibling, src=slab_hbm)] + [slab_copy(1 + j, me, (*chip, c), src=slab_hbm) for j, chip in enumerate(chips)]
        for cp in first[:2]:
            cp.start()
        load_u = pltpu.make_async_copy(u_hbm, u_vm, sem_u)
        load_u.start()
        load_u.wait()

        def store(slot, block0, rows0):
            return pltpu.make_async_copy(obuf.at[slot], proj_hbm.at[pl.ds(block0, SLAB_INNER), pl.ds(rows0, rc)], sem_o.at[slot])

        def multiply(dev):
            load_w = pltpu.make_async_copy(slabs_hbm.at[dev, :, pl.ds(LANES, SLAB_INNER * LANES)], wbuf, sem_w)
            load_w.start()
            load_w.wait()
            block0 = _slab_block0(dev) + 1

            def chunk(r, carry):
                slot = lax.rem(r, 2)
                rows0 = pl.multiple_of(r * rc, rc)

                @pl.when(r >= 2)
                def _():
                    store(slot, block0, rows0).wait()

                res = _mm(u_vm[pl.ds(rows0, rc), :], wbuf[...])
                for j in range(SLAB_INNER):
                    obuf[slot, j] = res[:, j * LANES:(j + 1) * LANES].astype(BF16)
                store(slot, block0, rows0).start()
                return carry

            lax.fori_loop(0, n_chunks, chunk, 0)
            for r in range(max(0, n_chunks - 2), n_chunks):
                store(r % 2, block0, r * rc).wait()

        mine.wait()
        multiply(_index(*me))
        slab_copy(0, sibling, me).wait_recv()
        multiply(_index(*sibling))
        passed = []
        for j, chip in enumerate(chips):
            slab_copy(1 + j, (*chip, c), me).wait_recv()
            passed.append(slab_copy(4 + j, (*chip, c), sibling))
            passed[-1].start()
            if j + 1 < len(chips):
                first[1 + j].wait_send()
                first[2 + j].start()
            multiply(_index(*chip, c))
            slab_copy(4 + j, (*chip, 1 - c), me).wait_recv()
            multiply(_index(*chip, 1 - c))

        loads, n = [], 0
        for k, (_, srcs) in enumerate(edges):
            for d, j in srcs:
                dst = ebuf.at[:, pl.ds(k * LANES, LANES)] if len(srcs) == 1 else stage.at[n]
                loads.append(pltpu.make_async_copy(slabs_hbm.at[d, :, pl.ds(j * LANES, LANES)], dst, sem_s.at[n]))
                n += 1
        for cp in loads:
            cp.start()
        for cp in loads:
            cp.wait()
        n = 0
        for k, (_, srcs) in enumerate(edges):
            if len(srcs) == 2:
                ebuf[:, k * LANES:(k + 1) * LANES] = stage[n] + stage[n + 1]
            n += len(srcs)

        def edge_stores(slot, rows0):
            return [pltpu.make_async_copy(ebuf_out.at[slot, pl.ds(k0, length)],
                                          proj_hbm.at[pl.ds(edges[k0][0], length), pl.ds(rows0, rc)], sem_eo.at[slot, i])
                    for i, (k0, length) in enumerate(runs)]

        def edge_chunk(r, carry):
            slot = lax.rem(r, 2)
            rows0 = pl.multiple_of(r * rc, rc)

            @pl.when(r >= 2)
            def _():
                for cp in edge_stores(slot, rows0):
                    cp.wait()

            res = _mm(u_vm[pl.ds(rows0, rc), :], ebuf[...])
            for k in range(ne):
                ebuf_out[slot, k] = res[:, k * LANES:(k + 1) * LANES].astype(BF16)
            for cp in edge_stores(slot, rows0):
                cp.start()
            return carry

        lax.fori_loop(0, n_chunks, edge_chunk, 0)
        for r in range(max(0, n_chunks - 2), n_chunks):
            for cp in edge_stores(r % 2, r * rc):
                cp.wait()

        for cp in [first[0], first[3]] + passed:
            cp.wait_send()

    return _call(
        body, "inproj_fwd",
        out_shape=[jax.ShapeDtypeStruct((AL_COLS // LANES, t_rows, LANES), BF16), jax.ShapeDtypeStruct((N_DEV, D_MODEL, SLAB_W), BF16)],
        in_specs=[ANY] * 2, out_specs=[ANY] * 2,
        scratch_shapes=[pltpu.VMEM((t_rows, D_MODEL), BF16), pltpu.VMEM((D_MODEL, SLAB_INNER * LANES), BF16),
                        pltpu.VMEM((2, SLAB_INNER, rc, LANES), BF16), pltpu.VMEM((D_MODEL, ne * LANES), BF16),
                        pltpu.VMEM((n_stage, D_MODEL, LANES), BF16), pltpu.VMEM((2, ne, rc, LANES), BF16),
                        pltpu.SemaphoreType.DMA, pltpu.SemaphoreType.DMA, pltpu.SemaphoreType.DMA((2,)),
                        pltpu.SemaphoreType.DMA((n_stage,)), pltpu.SemaphoreType.DMA((2, len(runs))),
                        pltpu.SemaphoreType.DMA((7,)), pltpu.SemaphoreType.DMA((7,)), pltpu.SemaphoreType.DMA],
        compiler_params=_params(),
    )(u, slab_local)


def _inproj_tiles(head, x, g_norm, slabs, w_glr):
    t_rows = x.shape[0] + TILE
    nt = t_rows // TILE
    n_blocks = AL_COLS // LANES

    def body(head_ref, x_ref, g_ref, slabs_hbm, wg_hbm, ut_ref, proj_ref, glr_ref, w_vm, wg_vm, edge_vm, sem):
        @pl.when(pl.program_id(0) == 0)
        def _():
            _load_weight(slabs_hbm, wg_hbm, w_vm, wg_vm, edge_vm, sem)

        x = _tile_rows(head_ref, x_ref)
        r = lax.rsqrt(_row_mean(x * x) + EPS)
        u32 = (x * r * g_ref[...]).astype(BF16).astype(F32)
        u = u32.astype(BF16)
        ut_ref[...] = u32.T.astype(BF16)
        for s in range(len(SEG_W)):
            res = _mm(u, w_vm[:, SEG_OFF[s]:SEG_OFF[s] + SEG_W[s]]).astype(BF16)
            for j in range(SEG_W[s] // LANES):
                proj_ref[SEG_OFF[s] // LANES + j] = res[:, j * LANES:(j + 1) * LANES]
        glr_ref[...] = _mm(u, wg_vm[...])

    return _call(
        body, "inproj_fwd_tiles", grid=(nt,),
        out_shape=[jax.ShapeDtypeStruct((nt, D_MODEL, TILE), BF16), jax.ShapeDtypeStruct((n_blocks, t_rows, LANES), BF16),
                   jax.ShapeDtypeStruct((t_rows, LANES), F32)],
        in_specs=[_head_spec(), _x_spec(), pl.BlockSpec((1, D_MODEL), lambda i: (0, 0)), ANY, ANY],
        out_specs=[pl.BlockSpec((None, D_MODEL, TILE), lambda i: (i, 0, 0)), pl.BlockSpec((n_blocks, TILE, LANES), lambda i: (0, i, 0)),
                   pl.BlockSpec((TILE, LANES), lambda i: (i, 0))],
        scratch_shapes=W_SCRATCH(), compiler_params=_params(("arbitrary",)),
    )(head, x, g_norm, slabs, w_glr)


def _ret_decay(lgh):
    i = lax.broadcasted_iota(jnp.int32, (TILE, TILE), 0)
    j = lax.broadcasted_iota(jnp.int32, (TILE, TILE), 1)
    rel = (i - j).astype(F32)
    return jnp.where(rel >= 0, jnp.exp(jnp.maximum(rel, 0.0) * lgh), 0.0)


def _ret_vectors(lgh):
    idx = lax.broadcasted_iota(jnp.int32, (TILE, 1), 0).astype(F32)
    xi = jnp.exp((idx + 1.0) * lgh)
    zeta = jnp.exp((TILE - 1.0 - idx) * lgh)
    gc = jnp.exp(jnp.full((1, 1), float(TILE), F32) * lgh)
    return xi, zeta, gc


def _ret_fwd(proj, cos, sin, gain, lg, row_shards):
    t_rows = cos.shape[0]
    nt = t_rows // TILE
    ns = len(row_shards)

    def body(lg_ref, q_ref, k_ref, v_ref, g_ref, cos_ref, sin_ref, gain_ref, *rest):
        shard_refs, (oraw_ref, oret_ref, st_ref), gathered = rest[:ns], rest[ns:ns + 3], rest[ns + 3:2 * ns + 3]
        s_acc, dm = rest[2 * ns + 3:2 * ns + 5]
        gather = _Exchange(shard_refs, gathered, rest[2 * ns + 5:], among_chips=False)
        t = pl.program_id(0)

        @pl.when(t == 0)
        def _():
            gather.start()
            s_acc[...] = jnp.zeros_like(s_acc)
            for h in range(RET_HEADS):
                dm[h] = _ret_decay(lg_ref[h])

        @pl.when(t == nt - 1)
        def _():
            gather.finish()

        cos_t, sin_t = cos_ref[...], sin_ref[...]
        for h in range(RET_HEADS):
            lgh = lg_ref[h]
            q = _rope(_cols(q_ref, h, RET_HEADS).astype(F32), cos_t, sin_t)
            k = _rope(_cols(k_ref, h, RET_HEADS).astype(F32), cos_t, sin_t) * (RET_QK ** -0.5)
            xi, zeta, gc = _ret_vectors(lgh)
            v = _cols(v_ref, h, RET_HEADS)
            s_in = s_acc[h]
            p = (_mm_nt(q.astype(BF16), k.astype(BF16)) * dm[h]).astype(BF16)
            o = _mm(p, v) + _mm((q * xi).astype(BF16), s_in.astype(BF16))
            st_ref[h] = s_in.astype(BF16)
            s_acc[h] = s_in * gc + _mm_tn((k * zeta).astype(BF16), v)
            cols = slice(h * RET_V, (h + 1) * RET_V)
            oraw_ref[:, cols] = o
            oc = o - _row_mean(o)
            n = oc * lax.rsqrt(_row_mean(oc * oc) + EPS) * gain_ref[:, cols]
            g = _cols(g_ref, h, RET_HEADS).astype(F32)
            oret_ref[:, cols] = (n * g * _sigmoid(g)).astype(BF16)

    row = lambda w: pl.BlockSpec((TILE, w), lambda t: (t, 0))
    outs = _call(
        body, "ret_fwd", grid=(nt,),
        out_shape=[jax.ShapeDtypeStruct((t_rows, RET_W), F32), jax.ShapeDtypeStruct((t_rows, RET_W), BF16),
                   jax.ShapeDtypeStruct((RET_HEADS, nt, RET_QK, RET_V), BF16)]
                  + [jax.ShapeDtypeStruct((N_DEV, *a.shape), a.dtype) for a in row_shards],
        in_specs=[pl.BlockSpec(memory_space=pltpu.SMEM)] + _proj_specs(("rq", "rk", "rv", "rg"), 1, lambda t: (0, t)) + [row(LANES), row(LANES),
                  pl.BlockSpec((1, RET_W), lambda t: (0, 0))] + [ANY] * ns,
        out_specs=[row(RET_W), row(RET_W), pl.BlockSpec((RET_HEADS, None, RET_QK, RET_V), lambda t: (0, t, 0, 0))] + [ANY] * ns,
        scratch_shapes=[pltpu.VMEM((RET_HEADS, RET_QK, RET_V), F32), pltpu.VMEM((RET_HEADS, TILE, TILE), F32)] + _exchange_sems(ns, N_DEV),
        compiler_params=_params(("arbitrary",)),
    )(lg, proj, proj, proj, proj, cos, sin, gain, *row_shards)
    return outs[0], outs[1], outs[2], outs[3:]


def _ret_bwd(proj, cos, sin, gain, lg, o_raw, do_ret, states):
    t_rows = cos.shape[0]
    nt = t_rows // TILE

    def body(lg_ref, q_ref, k_ref, v_ref, g_ref, cos_ref, sin_ref, gain_ref, oraw_ref, do_ref, st_ref,
             dq_ref, dk_ref, dv_ref, dg_ref, dgain_ref, e_acc, dm):
        @pl.when(pl.program_id(0) == 0)
        def _():
            e_acc[...] = jnp.zeros_like(e_acc)
            for h in range(RET_HEADS):
                dm[h] = _ret_decay(lg_ref[h])
            dgain_ref[...] = jnp.zeros_like(dgain_ref)

        cos_t, sin_t = cos_ref[...], sin_ref[...]
        for h in range(RET_HEADS):
            lgh = lg_ref[h]
            cols = slice(h * RET_V, (h + 1) * RET_V)
            qcols = slice(h * RET_QK, (h + 1) * RET_QK)
            q = _rope(_cols(q_ref, h, RET_HEADS).astype(F32), cos_t, sin_t)
            k = _rope(_cols(k_ref, h, RET_HEADS).astype(F32), cos_t, sin_t) * (RET_QK ** -0.5)
            xi, zeta, gc = _ret_vectors(lgh)
            v = _cols(v_ref, h, RET_HEADS)
            g = _cols(g_ref, h, RET_HEADS).astype(F32)
            o = oraw_ref[:, cols]
            do = do_ref[:, cols].astype(F32)
            oc = o - _row_mean(o)
            rstd = lax.rsqrt(_row_mean(oc * oc) + EPS)
            xh = oc * rstd
            gain_t = gain_ref[:, cols]
            sg = _sigmoid(g)
            dn = do * (g * sg)
            dg_ref[:, cols] = (do * (xh * gain_t) * (sg * (1.0 + g * (1.0 - sg)))).astype(BF16)
            dgain_ref[:, cols] += _col_sum(dn * xh)
            dxh = dn * gain_t
            dob = (rstd * (dxh - _row_mean(dxh) - xh * _row_mean(dxh * xh))).astype(BF16)
            dmat = dm[h]
            qb, kb = q.astype(BF16), k.astype(BF16)
            p = (_mm_nt(qb, kb) * dmat).astype(BF16)
            dp = (_mm_nt(dob, v) * dmat).astype(BF16)
            s_in = st_ref[h]
            e_in = e_acc[h]
            e_b = e_in.astype(BF16)
            dq = _mm(dp, kb) + _mm_nt(dob, s_in) * xi
            dk = _mm_tn(dp, qb) + _mm_nt(v, e_b) * zeta
            dv_ref[:, cols] = (_mm_tn(p, dob) + _mm((k * zeta).astype(BF16), e_b)).astype(BF16)
            e_acc[h] = e_in * gc + _mm_tn((q * xi).astype(BF16), dob)
            dq_ref[:, qcols] = _rope_bwd(dq, cos_t, sin_t).astype(BF16)
            dk_ref[:, qcols] = (_rope_bwd(dk, cos_t, sin_t) * (RET_QK ** -0.5)).astype(BF16)

    row = lambda w: pl.BlockSpec((TILE, w), lambda j: (nt - 1 - j, 0))
    vec = pl.BlockSpec((1, RET_W), lambda j: (0, 0))
    return _call(
        body, "ret_bwd", grid=(nt,),
        out_shape=[jax.ShapeDtypeStruct((t_rows, RET_HEADS * RET_QK), BF16), jax.ShapeDtypeStruct((t_rows, RET_HEADS * RET_QK), BF16),
                   jax.ShapeDtypeStruct((t_rows, RET_W), BF16), jax.ShapeDtypeStruct((t_rows, RET_W), BF16),
                   jax.ShapeDtypeStruct((1, RET_W), F32)],
        in_specs=[pl.BlockSpec(memory_space=pltpu.SMEM)] + _proj_specs(("rq", "rk", "rv", "rg"), 1, lambda j: (0, nt - 1 - j)) + [row(LANES), row(LANES), vec,
                  row(RET_W), row(RET_W), pl.BlockSpec((RET_HEADS, None, RET_QK, RET_V), lambda j: (0, nt - 1 - j, 0, 0))],
        out_specs=[row(RET_HEADS * RET_QK), row(RET_HEADS * RET_QK), row(RET_W), row(RET_W), vec],
        scratch_shapes=[pltpu.VMEM((RET_HEADS, RET_QK, RET_V), F32), pltpu.VMEM((RET_HEADS, TILE, TILE), F32)],
        compiler_params=_params(("arbitrary",)),
    )(lg, proj, proj, proj, proj, cos, sin, gain, o_raw, do_ret, states)


GLA_LEVELS = (32, 64, 128, 256)
N_TERMS = 1 + len(GLA_LEVELS)


def _gla_tables():
    p = jnp.arange(TILE)[:, None]
    r = jnp.arange(TILE)[None, :]
    masks = [(p // GLA_CHUNK == r // GLA_CHUNK) & (r <= p)]
    for blk in GLA_LEVELS:
        masks.append((p // blk == r // blk) & (p % blk >= blk // 2) & (r % blk < blk // 2))
    masks = jnp.stack(masks + [m.T for m in masks]).astype(F32)
    cum_fwd = jnp.concatenate([r <= p, masks[0] > 0], axis=0).astype(BF16)
    cum_bwd = jnp.concatenate([r >= p, masks[N_TERMS] > 0], axis=1).astype(BF16)
    return masks, cum_fwd, cum_bwd


def _split3(x):
    hi = x.astype(BF16)
    rest = x - hi.astype(F32)
    mid = rest.astype(BF16)
    lo = (rest - mid.astype(F32)).astype(BF16)
    return jnp.concatenate([hi, mid, lo], axis=1)


def _join3(y):
    w = y.shape[1] // 3
    return (y[:, 2 * w:] + y[:, w:2 * w]) + y[:, :w]


def _gla_decays(glr_ref, wgu_ref, b_ref, cum_ref):
    z = _mm(glr_ref[...].astype(BF16), wgu_ref[...].astype(BF16)) + b_ref[...]
    la = (jnp.minimum(z, 0.0) - jnp.log(1.0 + jnp.exp(-jnp.abs(z)))) / GLA_TAU
    width = la.shape[1]
    hi = la.astype(BF16)
    rest = la - hi.astype(F32)
    mid = rest.astype(BF16)
    lo = (rest - mid.astype(F32)).astype(BF16)
    y = _mm(cum_ref[...], jnp.concatenate([hi, mid, lo], axis=1))
    gb = (y[:, 2 * width:] + y[:, width:2 * width]) + y[:, :width]
    return z, gb[:TILE], gb[TILE:]


def _gla_prep(h, q_ref, k_ref, g_all, b_all, g_scr, ref_scr):
    cols = slice(h * GLA_K, (h + 1) * GLA_K)
    g, b = g_all[:, cols], b_all[:, cols]
    g_scr[h] = g
    factors = [(jnp.exp(b), jnp.exp(-b))]
    for lvl, blk in enumerate(GLA_LEVELS):
        for n in range(TILE // blk):
            ref_scr[h, lvl, n * blk:(n + 1) * blk, :] = jnp.broadcast_to(g_scr[h, pl.ds(n * blk + blk // 2 - 1, 1), :], (blk, GLA_K))
        x = g - ref_scr[h, lvl]
        factors.append((jnp.exp(jnp.minimum(x, 0.0)), jnp.exp(jnp.minimum(-x, 0.0))))
    g_last = g_scr[h, pl.ds(TILE - 1, 1), :]
    q = _cols(q_ref, h, GLA_HEADS).astype(F32) * (GLA_K ** -0.5)
    k = _cols(k_ref, h, GLA_HEADS).astype(F32)
    return q, k, factors, jnp.exp(g), jnp.exp(g_last), jnp.exp(g_last - g)


def _gla_scores(q, k, factors, m_ref):
    a = jnp.zeros((TILE, TILE), F32)
    for l, (fq, fk) in enumerate(factors):
        s = _mm_nt((q * fq).astype(BF16), (k * fk).astype(BF16))
        a = jnp.where(m_ref[l] > 0.0, s, a)
    return a


def _gla_fwd(proj, glr, wgu_pad, b_gate, gain, masks, cum_fwd):
    t_rows = glr.shape[0]
    nt = t_rows // TILE

    def body(q_ref, k_ref, v_ref, g_ref, glr_ref, wgu_ref, b_ref, gain_ref, m_ref, cum_ref, oraw_ref, ogla_ref, st_ref, at_ref,
             s_acc, g_scr, ref_scr):
        @pl.when(pl.program_id(0) == 0)
        def _():
            s_acc[...] = jnp.zeros_like(s_acc)

        _, g_all, b_all = _gla_decays(glr_ref, wgu_ref, b_ref, cum_ref)
        for h in range(GLA_HEADS):
            q, k, factors, e_g, e_last, e_end = _gla_prep(h, q_ref, k_ref, g_all, b_all, g_scr, ref_scr)
            v = _cols(v_ref, h, GLA_HEADS)
            st = s_acc[h]
            st_ref[h] = st
            a = _gla_scores(q, k, factors, m_ref)
            at_ref[h] = a.T.astype(BF16)
            o = _mm(a.astype(BF16), v) + _mm_nt((q * e_g).astype(BF16), st.astype(BF16))
            s_acc[h] = st * e_last + _mm(v.astype(F32).T.astype(BF16), (k * e_end).astype(BF16))
            cols = slice(h * GLA_V, (h + 1) * GLA_V)
            oraw_ref[:, cols] = o
            n = o * lax.rsqrt(_row_mean(o * o) + EPS) * gain_ref[:, cols]
            g = _cols(g_ref, h, GLA_HEADS).astype(F32)
            ogla_ref[:, cols] = (n * g * _sigmoid(g)).astype(BF16)

    row = lambda w: pl.BlockSpec((TILE, w), lambda t: (t, 0))
    whole = lambda *shape: pl.BlockSpec(shape, lambda t: (0,) * len(shape))
    return _call(
        body, "gla_fwd", grid=(nt,),
        out_shape=[jax.ShapeDtypeStruct((t_rows, GLA_W), F32), jax.ShapeDtypeStruct((t_rows, GLA_W), BF16),
                   jax.ShapeDtypeStruct((GLA_HEADS, nt, GLA_V, GLA_K), F32), jax.ShapeDtypeStruct((GLA_HEADS, t_rows, TILE), BF16)],
        in_specs=_proj_specs(("gq", "gk", "gv", "gg"), 1, lambda t: (0, t)) + [row(LANES), whole(LANES, GLA_HEADS * GLA_K),
                  whole(1, GLA_HEADS * GLA_K), whole(1, GLA_W), whole(N_TERMS, TILE, TILE), whole(2 * TILE, TILE)],
        out_specs=[row(GLA_W), row(GLA_W), pl.BlockSpec((GLA_HEADS, None, GLA_V, GLA_K), lambda t: (0, t, 0, 0)),
                   pl.BlockSpec((GLA_HEADS, TILE, TILE), lambda t: (0, t, 0))],
        scratch_shapes=[pltpu.VMEM((GLA_HEADS, GLA_V, GLA_K), F32), pltpu.VMEM((GLA_HEADS, TILE, GLA_K), F32),
                        pltpu.VMEM((GLA_HEADS, len(GLA_LEVELS), TILE, GLA_K), F32)],
        compiler_params=_params(("arbitrary",)),
    )(proj, proj, proj, proj, glr, wgu_pad, b_gate, gain, masks, cum_fwd)


def _gla_bwd(proj, glr, wgu_pad, b_gate, gain, o_raw, do_gla, states, a_t, masks, cum_fwd, cum_bwd):
    t_rows = glr.shape[0]
    nt = t_rows // TILE

    def body(q_ref, k_ref, v_ref, g_ref, glr_ref, wgu_ref, b_ref, gain_ref, m_ref, cum_ref, cumb_ref, oraw_ref, do_ref, st_ref, at_ref,
             dq_ref, dk_ref, dv_ref, dg_ref, dglr_ref, dwgu_ref, dbg_ref, dgain_ref, d_acc, g_scr, ref_scr, dref_scr):
        @pl.when(pl.program_id(0) == 0)
        def _():
            d_acc[...] = jnp.zeros_like(d_acc)
            dwgu_ref[...] = jnp.zeros_like(dwgu_ref)
            dbg_ref[...] = jnp.zeros_like(dbg_ref)
            dgain_ref[...] = jnp.zeros_like(dgain_ref)

        z_all, g_all, b_all = _gla_decays(glr_ref, wgu_ref, b_ref, cum_ref)
        dla_parts = []
        for h in range(GLA_HEADS):
            q, k, factors, e_g, e_last, e_end = _gla_prep(h, q_ref, k_ref, g_all, b_all, g_scr, ref_scr)
            v = _cols(v_ref, h, GLA_HEADS)
            cols = slice(h * GLA_V, (h + 1) * GLA_V)
            kcols = slice(h * GLA_K, (h + 1) * GLA_K)
            o = oraw_ref[:, cols]
            do = do_ref[:, cols].astype(F32)
            g = _cols(g_ref, h, GLA_HEADS).astype(F32)
            rinv = lax.rsqrt(_row_mean(o * o) + EPS)
            nh = o * rinv
            gain_t = gain_ref[:, cols]
            sg = _sigmoid(g)
            dn = do * (g * sg)
            dg_ref[:, cols] = (do * (nh * gain_t) * (sg * (1.0 + g * (1.0 - sg)))).astype(BF16)
            dgain_ref[:, cols] += _col_sum(dn * nh)
            dnh = dn * gain_t
            dor = rinv * (dnh - nh * _row_mean(dnh * nh))
            dob = dor.astype(BF16)
            a_t = at_ref[h]
            da = _mm_nt(dob, v).astype(BF16)
            da_t = _mm_nt(v, dob).astype(BF16)
            st_in = st_ref[h]
            d_out = d_acc[h]
            d_out_b = d_out.astype(BF16)
            qg, kg = q * e_g, k * e_end
            dqg = _mm(dob, st_in.astype(BF16))
            dkg = _mm(v, d_out_b)
            dv_ref[:, cols] = (_mm(a_t, dob) + _mm_nt(kg.astype(BF16), d_out_b)).astype(BF16)
            d_acc[h] = d_out * e_last + _mm(dor.T.astype(BF16), qg.astype(BF16))
            dq = dqg * e_g
            dk = dkg * e_end
            dkg_kg = dkg * kg
            dg_cum = dqg * qg - dkg_kg
            db = None
            for l, (fq, fk) in enumerate(factors):
                qt, kt = q * fq, k * fk
                dqt = _mm(da * m_ref[l], kt.astype(BF16))
                dkt = _mm(da_t * m_ref[N_TERMS + l], qt.astype(BF16))
                dq = dq + dqt * fq
                dk = dk + dkt * fk
                diff = dqt * qt - dkt * kt
                if l == 0:
                    db = diff
                else:
                    dg_cum = dg_cum + diff
                    dref_scr[h, l - 1] = diff
            dq_ref[:, kcols] = (dq * (GLA_K ** -0.5)).astype(BF16)
            dk_ref[:, kcols] = dk.astype(BF16)
            g_scr[h] = dg_cum
            g_scr[h, pl.ds(TILE - 1, 1), :] += e_last * _col_sum(d_out * st_in) + _col_sum(dkg_kg)
            for lvl, blk in enumerate(GLA_LEVELS):
                for n in range(TILE // blk):
                    g_scr[h, pl.ds(n * blk + blk // 2 - 1, 1), :] -= _col_sum(dref_scr[h, lvl, n * blk:(n + 1) * blk, :])
            dla_parts.append(_join3(_mm(cumb_ref[...], jnp.concatenate([_split3(g_scr[h]), _split3(db)], axis=0))))
        dz = jnp.concatenate(dla_parts, axis=1) * (1.0 / GLA_TAU) * _sigmoid(-z_all)
        dzb = dz.astype(BF16)
        wgu_b = wgu_ref[...].astype(BF16)
        for h in range(GLA_HEADS):
            kcols = slice(h * GLA_K, (h + 1) * GLA_K)
            dglr_ref[h] = _mm_nt(dzb[:, kcols], wgu_b[:, kcols]).astype(BF16)
        dwgu_ref[...] += _mm(glr_ref[...].T.astype(BF16), dzb)
        dbg_ref[...] += _col_sum(dz)

    row = lambda w: pl.BlockSpec((TILE, w), lambda j: (nt - 1 - j, 0))
    whole = lambda *shape: pl.BlockSpec(shape, lambda j: (0,) * len(shape))
    return _call(
        body, "gla_bwd", grid=(nt,),
        out_shape=[jax.ShapeDtypeStruct((t_rows, GLA_HEADS * GLA_K), BF16), jax.ShapeDtypeStruct((t_rows, GLA_HEADS * GLA_K), BF16),
                   jax.ShapeDtypeStruct((t_rows, GLA_W), BF16), jax.ShapeDtypeStruct((t_rows, GLA_W), BF16),
                   jax.ShapeDtypeStruct((GLA_HEADS, t_rows, LANES), BF16), jax.ShapeDtypeStruct((LANES, GLA_HEADS * GLA_K), F32),
                   jax.ShapeDtypeStruct((1, GLA_HEADS * GLA_K), F32), jax.ShapeDtypeStruct((1, GLA_W), F32)],
        in_specs=_proj_specs(("gq", "gk", "gv", "gg"), 1, lambda j: (0, nt - 1 - j)) + [row(LANES),
                  whole(LANES, GLA_HEADS * GLA_K), whole(1, GLA_HEADS * GLA_K), whole(1, GLA_W),
                  whole(2 * N_TERMS, TILE, TILE), whole(2 * TILE, TILE), whole(TILE, 2 * TILE), row(GLA_W), row(GLA_W),
                  pl.BlockSpec((GLA_HEADS, None, GLA_V, GLA_K), lambda j: (0, nt - 1 - j, 0, 0)),
                  pl.BlockSpec((GLA_HEADS, TILE, TILE), lambda j: (0, nt - 1 - j, 0))],
        out_specs=[row(GLA_HEADS * GLA_K), row(GLA_HEADS * GLA_K), row(GLA_W), row(GLA_W),
                   pl.BlockSpec((GLA_HEADS, TILE, LANES), lambda j: (0, nt - 1 - j, 0)), whole(LANES, GLA_HEADS * GLA_K),
                   whole(1, GLA_HEADS * GLA_K), whole(1, GLA_W)],
        scratch_shapes=[pltpu.VMEM((GLA_HEADS, GLA_V, GLA_K), F32), pltpu.VMEM((GLA_HEADS, TILE, GLA_K), F32),
                        pltpu.VMEM((GLA_HEADS, len(GLA_LEVELS), TILE, GLA_K), F32),
                        pltpu.VMEM((GLA_HEADS, len(GLA_LEVELS), TILE, GLA_K), F32)],
        compiler_params=_params(("arbitrary",)),
    )(proj, proj, proj, proj, glr, wgu_pad, b_gate, gain, masks.astype(BF16), cum_fwd, cum_bwd, o_raw, do_gla, states, a_t)


def _merge_fwd_bwd(o_ret, o_gla, proj, x, target, g_final, w_br, w_bg, w_out):
    t_rows = x.shape[0] + TILE
    nt = t_rows // TILE

    def body(oret_ref, ogla_ref, mr_ref, mg_ref, h0_ref, tgt_ref, gf_ref, wbr_hbm, wbg_hbm, wout_hbm,
             dh1_ref, dmr_ref, dmg_ref, doret_ref, dogla_ref, loss_ref, dgf_ref, dwbr_hbm, dwbg_hbm, dwout_hbm,
             wbr, wbg, wout, abr, abg, aout, sem):
        i = pl.program_id(0)

        @pl.when(i == 0)
        def _():
            cps = [pltpu.make_async_copy(s, d, sem.at[n]) for n, (s, d) in enumerate(((wbr_hbm, wbr), (wbg_hbm, wbg), (wout_hbm, wout)))]
            for cp in cps:
                cp.start()
            abr[...] = jnp.zeros_like(abr)
            abg[...] = jnp.zeros_like(abg)
            aout[...] = jnp.zeros_like(aout)
            loss_ref[...] = jnp.zeros_like(loss_ref)
            dgf_ref[...] = jnp.zeros_like(dgf_ref)
            for cp in cps:
                cp.wait()
            dh1_ref[...] = jnp.zeros_like(dh1_ref)
            dmr_ref[...] = jnp.zeros_like(dmr_ref)
            dmg_ref[...] = jnp.zeros_like(dmg_ref)
            doret_ref[...] = jnp.zeros_like(doret_ref)
            dogla_ref[...] = jnp.zeros_like(dogla_ref)

        @pl.when(i > 0)
        def _():
            oret, ogla = oret_ref[...], ogla_ref[...]
            br, bg = _mm(oret, wbr[...]), _mm(ogla, wbg[...])
            sr, sg = _sigmoid(_cols(mr_ref).astype(F32)), _sigmoid(_cols(mg_ref).astype(F32))
            mb = (sr * br + sg * bg).astype(BF16)
            h1 = h0_ref[...] + _mm(mb, wout[...])
            r2 = lax.rsqrt(_row_mean(h1 * h1) + EPS)
            hn = h1 * r2
            gf = gf_ref[...]
            diff = hn * gf - tgt_ref[...]
            loss_ref[...] += 0.5 * jnp.sum(_row_mean(diff * diff))
            dy = diff * (1.0 / D_MODEL)
            dgf_ref[...] += _col_sum(dy * hn)
            dyg = dy * gf
            dh1 = r2 * (dyg - hn * _row_mean(dyg * hn))
            dh1_ref[...] = dh1
            dh1b = dh1.astype(BF16)
            dm = _mm_nt(dh1b, wout[...])
            aout[...] += _mm_tn(mb, dh1b)
            dbr = (dm * sr).astype(BF16)
            dbg = (dm * sg).astype(BF16)
            dmr_ref[...] = (dm * br * sr * (1.0 - sr)).astype(BF16)
            dmg_ref[...] = (dm * bg * sg * (1.0 - sg)).astype(BF16)
            doret_ref[...] = _mm_nt(dbr, wbr[...]).astype(BF16)
            dogla_ref[...] = _mm_nt(dbg, wbg[...]).astype(BF16)
            abr[...] += _mm_tn(oret, dbr)
            abg[...] += _mm_tn(ogla, dbg)

        @pl.when(i == nt - 1)
        def _():
            wbr[...] = abr[...].astype(BF16)
            wbg[...] = abg[...].astype(BF16)
            wout[...] = aout[...].astype(BF16)
            pltpu.sync_copy(wbr, dwbr_hbm)
            pltpu.sync_copy(wbg, dwbg_hbm)
            pltpu.sync_copy(wout, dwout_hbm)

    row = lambda w: pl.BlockSpec((TILE, w), lambda i: (i, 0))
    one = lambda w: pl.BlockSpec((1, w), lambda i: (0, 0))
    return _call(
        body, "merge_fwd_bwd", grid=(nt,),
        out_shape=[jax.ShapeDtypeStruct((t_rows, D_MODEL), F32), jax.ShapeDtypeStruct((t_rows, D_MODEL), BF16),
                   jax.ShapeDtypeStruct((t_rows, D_MODEL), BF16), jax.ShapeDtypeStruct((t_rows, RET_W), BF16),
                   jax.ShapeDtypeStruct((t_rows, GLA_W), BF16), jax.ShapeDtypeStruct((1, LANES), F32),
                   jax.ShapeDtypeStruct((1, D_MODEL), F32), jax.ShapeDtypeStruct((RET_W, D_MODEL), BF16),
                   jax.ShapeDtypeStruct((GLA_W, D_MODEL), BF16), jax.ShapeDtypeStruct((D_MODEL, D_MODEL), BF16)],
        in_specs=[row(RET_W), row(GLA_W)] + _proj_specs(("mr", "mg"), 1, lambda i: (0, i)) + [_x_spec(), _x_spec(), one(D_MODEL), ANY, ANY, ANY],
        out_specs=[row(D_MODEL), row(D_MODEL), row(D_MODEL), row(RET_W), row(GLA_W), one(LANES), one(D_MODEL), ANY, ANY, ANY],
        scratch_shapes=[pltpu.VMEM((RET_W, D_MODEL), BF16), pltpu.VMEM((GLA_W, D_MODEL), BF16), pltpu.VMEM((D_MODEL, D_MODEL), BF16),
                        pltpu.VMEM((RET_W, D_MODEL), F32), pltpu.VMEM((GLA_W, D_MODEL), F32), pltpu.VMEM((D_MODEL, D_MODEL), F32),
                        pltpu.SemaphoreType.DMA((3,))],
        compiler_params=_params(("arbitrary",)),
    )(o_ret, o_gla, proj, proj, x, target, g_final, w_br, w_bg, w_out)


def _inproj_bwd_x(dseg, dglr, head, x, dh1, g_norm, slabs, w_glr, chip_partials):
    t_rows = x.shape[0] + TILE
    nt = t_rows // TILE
    ne = len(chip_partials)

    def body(*refs):
        d_refs = refs[:10]
        dglr_ref, head_ref, x_ref, dh1_ref, g_ref, slabs_hbm, wg_hbm = refs[10:17]
        part_refs = refs[17:17 + ne]
        dx_ref, dhead_ref, dgn_ref = refs[17 + ne:20 + ne]
        landed = refs[20 + ne:20 + 2 * ne]
        w_vm, wg_vm, edge_vm, sem = refs[20 + 2 * ne:24 + 2 * ne]
        exchange = _Exchange(part_refs, landed, refs[24 + 2 * ne:], among_chips=True)

        @pl.when(pl.program_id(0) == 0)
        def _():
            exchange.start()
            dgn_ref[...] = jnp.zeros_like(dgn_ref)
            _load_weight(slabs_hbm, wg_hbm, w_vm, wg_vm, edge_vm, sem)

        @pl.when(pl.program_id(0) == nt - 1)
        def _():
            exchange.finish()

        dglr = dglr_ref[0].astype(F32)
        for h in range(1, GLA_HEADS):
            dglr = dglr + dglr_ref[h].astype(F32)
        du = _mm_nt(dglr.astype(BF16), wg_vm[...])
        for s, d_ref in enumerate(d_refs):
            du = du + _mm_nt(d_ref[...], w_vm[:, SEG_OFF[s]:SEG_OFF[s] + SEG_W[s]])
        x = _tile_rows(head_ref, x_ref)
        r = lax.rsqrt(_row_mean(x * x) + EPS)
        hn = x * r
        dgn_ref[...] += _col_sum(du * hn)
        dug = du * g_ref[...]
        dh0 = dh1_ref[...] + r * (dug - hn * _row_mean(dug * hn))
        dx_ref[...] = dh0

        @pl.when(pl.program_id(0) == 0)
        def _():
            dhead_ref[...] = dh0

    row = lambda w: pl.BlockSpec((TILE, w), lambda i: (i, 0))
    one = pl.BlockSpec((1, D_MODEL), lambda i: (0, 0))
    return _call(
        body, "inproj_bwd_x", grid=(nt,),
        out_shape=[jax.ShapeDtypeStruct((t_rows - TILE, D_MODEL), F32), jax.ShapeDtypeStruct((TILE, D_MODEL), F32),
                   jax.ShapeDtypeStruct((1, D_MODEL), F32)] + [jax.ShapeDtypeStruct(a.shape, a.dtype) for a in chip_partials],
        in_specs=[row(w) for w in SEG_W] + [pl.BlockSpec((GLA_HEADS, TILE, LANES), lambda i: (0, i, 0)),
                                            _head_spec(), _x_spec(), row(D_MODEL), one, ANY, ANY] + [ANY] * ne,
        out_specs=[_x_spec(), _head_spec(), one] + [ANY] * ne,
        scratch_shapes=W_SCRATCH() + _exchange_sems(ne, N_CHIP),
        compiler_params=_params(("arbitrary",)),
    )(*[dseg[n] for n in SEG_NAMES], dglr, head, x, dh1, g_norm, slabs, w_glr, *chip_partials)


W_TILE = 512


def _inproj_bwd_w(ut, dseg, dglr):
    nt = ut.shape[0]
    t_rows = nt * TILE
    kc = 3 if nt % 3 == 0 else 1
    tiles = [(s, c) for s in range(len(SEG_W)) for c in range(0, SEG_W[s], W_TILE)]
    bpt = W_TILE // LANES

    def body(ut_hbm, *refs):
        d_refs, dglr_hbm, out_hbm, oglr_ref = refs[:10], refs[10], refs[11], refs[12]
        ut_vm, dbuf, obuf, acc, gbuf, sem = refs[13:]

        def fetch(i):
            s, c = tiles[i]
            return pltpu.make_async_copy(d_refs[s].at[:, pl.ds(c, W_TILE)], dbuf.at[i % 2], sem.at[1 + i % 2])

        def contract(rhs_refs, width):
            acc[:, :width] = jnp.zeros((D_MODEL, width), F32)

            def step(k, carry):
                part = None
                for j in range(kc):
                    kk = k * kc + j
                    for rhs_ref in rhs_refs:
                        prod = _mm(ut_vm[kk], rhs_ref[pl.ds(pl.multiple_of(kk * TILE, TILE), TILE), :])
                        part = prod if part is None else part + prod
                acc[:, :width] += part
                return carry

            lax.fori_loop(0, nt // kc, step, 0)
            return acc[:, :width]

        load_ut = pltpu.make_async_copy(ut_hbm, ut_vm, sem.at[0])
        load_glr = pltpu.make_async_copy(dglr_hbm, gbuf, sem.at[5])
        load_ut.start()
        load_glr.start()
        fetch(0).start()
        load_ut.wait()
        stores = {}
        for i, (s, c) in enumerate(tiles):
            if i + 1 < len(tiles):
                fetch(i + 1).start()
            fetch(i).wait()
            if i >= 2:
                stores[i - 2].wait()
            total = contract([dbuf.at[i % 2]], W_TILE)
            for j in range(bpt):
                obuf[i % 2, j] = total[:, j * LANES:(j + 1) * LANES].astype(BF16)
            blk0 = (SEG_OFF[s] + c) // LANES
            stores[i] = pltpu.make_async_copy(obuf.at[i % 2], out_hbm.at[pl.ds(blk0, bpt)], sem.at[3 + i % 2])
            stores[i].start()
        load_glr.wait()
        head_sum = gbuf[0].astype(F32)
        for h in range(1, GLA_HEADS):
            head_sum = head_sum + gbuf[h].astype(F32)
        gbuf[0] = head_sum.astype(BF16)
        oglr_ref[...] = contract([gbuf.at[0]], LANES)
        for i in range(max(0, len(tiles) - 2), len(tiles)):
            stores[i].wait()

    return _call(
        body, "inproj_bwd_w",
        out_shape=[jax.ShapeDtypeStruct((AL_COLS // LANES, D_MODEL, LANES), BF16), jax.ShapeDtypeStruct((D_MODEL, LANES), F32)],
        in_specs=[ANY] * 12, out_specs=[ANY, pl.BlockSpec(memory_space=pltpu.VMEM)],
        scratch_shapes=[pltpu.VMEM((nt, D_MODEL, TILE), BF16), pltpu.VMEM((2, t_rows, W_TILE), BF16),
                        pltpu.VMEM((2, bpt, D_MODEL, LANES), BF16), pltpu.VMEM((D_MODEL, W_TILE), F32),
                        pltpu.VMEM((GLA_HEADS, t_rows, LANES), BF16), pltpu.SemaphoreType.DMA((6,))],
        compiler_params=_params(),
    )(ut, *[dseg[n] for n in SEG_NAMES], dglr)


def _position():
    x, y, c = lax.axis_index("x"), lax.axis_index("y"), lax.axis_index("c")
    return x, y, c


def _index(px, py, pc):
    return 4 * px + 2 * py + pc


def _all_gather(arrs, name):
    n = len(arrs)

    def body(*refs):
        ins, outs = refs[:n], refs[n:2 * n]
        send_sems, recv_sems, local_sems = refs[2 * n:]
        x, y, c = _position()
        me, sibling = (x, y, c), (x, y, 1 - c)
        chips = [(1 - x, y), (x, 1 - y), (1 - x, 1 - y)]

        def copy(a, k, block, to, src=None):
            dst = outs[a].at[_index(*block)]
            return pltpu.make_async_remote_copy(src_ref=dst if src is None else src, dst_ref=dst,
                                                send_sem=send_sems.at[7 * a + k], recv_sem=recv_sems.at[7 * a + k],
                                                device_id=to, device_id_type=MESH)

        mine = [pltpu.make_async_copy(ins[a], outs[a].at[_index(*me)], local_sems.at[a]) for a in range(n)]
        for cp in mine:
            cp.start()
        first = []
        for a in range(n):
            first.append(copy(a, 0, me, sibling, src=ins[a]))
            first += [copy(a, 1 + j, me, (*chip, c), src=ins[a]) for j, chip in enumerate(chips)]
        for cp in first:
            cp.start()
        passed = []
        for j, chip in enumerate(chips):
            for a in range(n):
                copy(a, 1 + j, (*chip, c), me).wait_recv()
                cp = copy(a, 4 + j, (*chip, c), sibling)
                cp.start()
                passed.append(cp)
        for a in range(n):
            copy(a, 0, sibling, me).wait_recv()
            for j, chip in enumerate(chips):
                copy(a, 4 + j, (*chip, 1 - c), me).wait_recv()
        for cp in first + passed:
            cp.wait_send()
        for cp in mine:
            cp.wait()

    return _call(
        body, name,
        out_shape=[jax.ShapeDtypeStruct((N_DEV, *a.shape), a.dtype) for a in arrs],
        in_specs=[ANY] * n, out_specs=[ANY] * n,
        scratch_shapes=[pltpu.SemaphoreType.DMA((7 * n,)), pltpu.SemaphoreType.DMA((7 * n,)), pltpu.SemaphoreType.DMA((n,))],
    )(*arrs)


N_CHIP = N_DEV // 2


def _slab_block0(owner):
    step = SLAB_BLK0[1]
    assert all(SLAB_BLK0[d] == step * d - (d == N_DEV - 1) for d in range(N_DEV))
    return step * owner - jnp.where(owner == N_DEV - 1, 1, 0)


def _exchange_sibling(dw_blocks, row_sends):
    n = 1 + len(row_sends)

    def body(*refs):
        dw_ref, row_refs, outs, (send_sems, recv_sems) = refs[0], refs[1:n], refs[n:2 * n], refs[2 * n:]
        x, y, c = _position()
        copies = []
        for q in range(N_CHIP):
            owner = 2 * q + (1 - c)
            srcs = [dw_ref.at[pl.ds(_slab_block0(owner), SLAB_BLOCKS)]] + [r.at[owner] for r in row_refs]
            for k, src in enumerate(srcs):
                copies.append(pltpu.make_async_remote_copy(src_ref=src, dst_ref=outs[k].at[q], send_sem=send_sems.at[n * q + k],
                                                           recv_sem=recv_sems.at[n * q + k], device_id=(x, y, 1 - c), device_id_type=MESH))
        for cp in copies:
            cp.start()
        for cp in copies:
            cp.wait()

    return _call(
        body, "exchange_sibling",
        out_shape=[jax.ShapeDtypeStruct((N_CHIP, SLAB_BLOCKS, D_MODEL, LANES), BF16)]
                  + [jax.ShapeDtypeStruct((N_CHIP, *r.shape[1:]), BF16) for r in row_sends],
        in_specs=[ANY] * n, out_specs=[ANY] * n,
        scratch_shapes=[pltpu.SemaphoreType.DMA((n * N_CHIP,)), pltpu.SemaphoreType.DMA((n * N_CHIP,))],
    )(dw_blocks, *row_sends)


def _add_bf16(c_ref, a_ref, b_ref, o_ref):
    o_ref[...] = (a_ref[...].astype(F32) + b_ref[...].astype(F32)).astype(BF16)


def _chip_partial_slab(dw_blocks, sib, core):
    blk = pl.BlockSpec((None, SLAB_BLOCKS, D_MODEL, LANES), lambda q, c_ref: (q, 0, 0, 0))
    return _call(
        functools.partial(_add_bf16), "chip_partial_w_in", out_shape=jax.ShapeDtypeStruct(sib.shape, BF16),
        grid_spec=pltpu.PrefetchScalarGridSpec(
            num_scalar_prefetch=1, grid=(N_CHIP,),
            in_specs=[pl.BlockSpec((pl.Element(SLAB_BLOCKS), pl.Element(D_MODEL), pl.Element(LANES)),
                                   lambda q, c_ref: (_slab_block0(2 * q + c_ref[0]), 0, 0)), blk],
            out_specs=blk),
        compiler_params=_params(("arbitrary",)),
    )(core, dw_blocks, sib)


def _chip_partial_rows(send, sib, core, name):
    rows, cols = send.shape[1:]
    blk = pl.BlockSpec((None, rows, cols), lambda q, c_ref: (q, 0, 0))
    return _call(
        functools.partial(_add_bf16), name, out_shape=jax.ShapeDtypeStruct(sib.shape, BF16),
        grid_spec=pltpu.PrefetchScalarGridSpec(
            num_scalar_prefetch=1, grid=(N_CHIP,),
            in_specs=[pl.BlockSpec((None, rows, cols), lambda q, c_ref: (2 * q + c_ref[0], 0, 0)), blk], out_specs=blk),
        compiler_params=_params(("arbitrary",)),
    )(core, send, sib)


def _exchange_sems(n_arrays, n_peers):
    return [pltpu.SemaphoreType.DMA((n_arrays * n_peers,)), pltpu.SemaphoreType.DMA((n_arrays * n_peers,)),
            pltpu.SemaphoreType.DMA((n_arrays,))]


class _Exchange:
    def __init__(self, srcs, dsts, sems, among_chips):
        self.arrs = list(zip(srcs, dsts))
        self.n = len(self.arrs)
        self.send_sems, self.recv_sems, self.local_sems = sems
        self.among_chips = among_chips
        x, y, c = _position()
        self.c = c
        self.me = 2 * x + y if among_chips else _index(x, y, c)
        self.n_peers = N_CHIP if among_chips else N_DEV

    def _device(self, p):
        return (p // 2, p % 2, self.c) if self.among_chips else (p // 4, (p // 2) % 2, p % 2)

    def _src(self, k, p):
        src = self.arrs[k][0]
        return src.at[p] if self.among_chips else src

    def _mine(self):
        return [pltpu.make_async_copy(self._src(k, self.me), self.arrs[k][1].at[self.me], self.local_sems.at[k]) for k in range(self.n)]

    def _copy(self, p, k, landing):
        return pltpu.make_async_remote_copy(
            src_ref=self._src(k, p), dst_ref=self.arrs[k][1].at[landing], send_sem=self.send_sems.at[self.n * p + k],
            recv_sem=self.recv_sems.at[self.n * landing + k], device_id=self._device(p), device_id_type=MESH)

    def _others(self, fn):
        for p in range(self.n_peers):
            @pl.when(p != self.me)
            def _():
                for k in range(self.n):
                    fn(p, k)

    def start(self):
        for cp in self._mine():
            cp.start()
        self._others(lambda p, k: self._copy(p, k, self.me).start())

    def finish(self):
        self._others(lambda p, k: self._copy(p, k, p).wait_recv())
        self._others(lambda p, k: self._copy(p, k, self.me).wait_send())
        for cp in self._mine():
            cp.wait()


def _adamw(g, w, m, v):
    m_new = ADAM_B1 * m + (1.0 - ADAM_B1) * g
    v_new = ADAM_B2 * v + (1.0 - ADAM_B2) * (g * g)
    m_hat = m_new / (1.0 - ADAM_B1 ** ADAM_STEP)
    v_hat = v_new / (1.0 - ADAM_B2 ** ADAM_STEP)
    delta = -ADAM_LR * (m_hat / (jnp.sqrt(v_hat) + ADAM_EPS) + ADAM_WD * w)
    return delta, m_new, v_new


def _sum_partials(p_ref):
    g = p_ref[0].astype(F32)
    for d in range(1, p_ref.shape[0]):
        g = g + p_ref[d].astype(F32)
    return g


def _reduce_adam(parts, w, m, v, name, block_rows, row_off=0):
    rows, cols = w.shape
    off = row_off // block_rows

    def body(p_ref, w_ref, m_ref, v_ref, g_ref, d_ref, mo_ref, vo_ref):
        g = _sum_partials(p_ref)
        g_ref[...] = g
        d_ref[...], mo_ref[...], vo_ref[...] = _adamw(g, w_ref[...], m_ref[...], v_ref[...])

    blk = pl.BlockSpec((block_rows, cols), lambda i: (i, 0))
    return _call(
        body, name, grid=(rows // block_rows,),
        out_shape=[jax.ShapeDtypeStruct((rows, cols), F32)] * 4,
        in_specs=[pl.BlockSpec((parts.shape[0], block_rows, cols), lambda i: (0, i + off, 0)), blk, blk, blk],
        out_specs=[blk] * 4,
        compiler_params=_params(("arbitrary",)),
    )(parts, w, m, v)


def _reduce_adam_slab(parts, glr, w, m, v, me):
    rows, cols = w.shape
    shift = jnp.asarray(SLAB_SHIFT, jnp.int32)[me]
    glr_at = jnp.where(me == GLR_DEV, GLR_LOCAL, cols).astype(jnp.int32)

    def body(s_ref, p_ref, glr_ref, w_ref, m_ref, v_ref, g_ref, d_ref, mo_ref, vo_ref):
        shift, glr_at = s_ref[0], s_ref[1]
        slab = jnp.concatenate([_sum_partials(p_ref.at[:, j]) for j in range(SLAB_BLOCKS)], axis=1)
        before = pltpu.roll(slab, SLAB_W - shift, 1)
        after = pltpu.roll(slab, lax.rem(SLAB_W - shift + GLA_RANK, SLAB_W), 1)
        wide = jnp.concatenate([glr_ref[...], jnp.zeros((LANES, SLAB_W - LANES), F32)], axis=1)
        placed = pltpu.roll(wide, lax.rem(glr_at, SLAB_W), 1)
        lane = lax.broadcasted_iota(jnp.int32, (LANES, SLAB_W), 1)
        g = jnp.where(lane < glr_at, before, jnp.where(lane < glr_at + GLA_RANK, placed, after))[:, :cols]
        g_ref[...] = g
        d_ref[...], mo_ref[...], vo_ref[...] = _adamw(g, w_ref[...], m_ref[...], v_ref[...])

    blk = pl.BlockSpec((LANES, cols), lambda i, s: (i, 0))
    return _call(
        body, "adam_w_in", out_shape=[jax.ShapeDtypeStruct((rows, cols), F32)] * 4,
        grid_spec=pltpu.PrefetchScalarGridSpec(
            num_scalar_prefetch=1, grid=(rows // LANES,),
            in_specs=[pl.BlockSpec((parts.shape[0], SLAB_BLOCKS, LANES, LANES), lambda i, s: (0, 0, i, 0)),
                      pl.BlockSpec((LANES, LANES), lambda i, s: (i, 0)), blk, blk, blk],
            out_specs=[blk] * 4),
        compiler_params=_params(("arbitrary",)),
    )(jnp.stack([shift, glr_at]), parts, glr, w, m, v)


def _reduce_small(parts):
    def body(p_ref, o_ref):
        o_ref[...] = _sum_partials(p_ref)

    return _call(body, "reduce_small", out_shape=jax.ShapeDtypeStruct(parts.shape[1:], F32))(parts)


def _adam_small(g, w, m, v):
    def body(g_ref, w_ref, m_ref, v_ref, d_ref, mo_ref, vo_ref):
        d_ref[...], mo_ref[...], vo_ref[...] = _adamw(g_ref[...], w_ref[...], m_ref[...], v_ref[...])

    return _call(body, "adam_small", out_shape=[jax.ShapeDtypeStruct(g.shape, F32)] * 3)(g, w, m, v)


def _pack_rows(arrs):
    rows = []
    for a in arrs:
        flat = a.reshape(-1).astype(F32)
        pad = (-flat.shape[0]) % LANES
        rows.append(jnp.pad(flat, (0, pad)).reshape(-1, LANES))
    packed = jnp.concatenate(rows, axis=0)
    return jnp.pad(packed, ((0, (-packed.shape[0]) % 8), (0, 0)))


def _unpack_rows(packed, shapes):
    out, r = [], 0
    for shp in shapes:
        size = 1
        for s in shp:
            size *= s
        nrows = -(-size // LANES)
        out.append(packed[r:r + nrows].reshape(-1)[:size].reshape(shp))
        r += nrows
    return out


def _shard_to_slab(shard, d):
    glr = jnp.zeros((D_MODEL, GLA_RANK), shard.dtype)
    if d == GLR_DEV:
        glr = shard[:, GLR_LOCAL:GLR_LOCAL + GLA_RANK]
        shard = jnp.concatenate([shard[:, :GLR_LOCAL], shard[:, GLR_LOCAL + GLA_RANK:]], axis=1)
    return jnp.pad(shard, ((0, 0), (SLAB_SHIFT[d], SLAB_W - SLAB_SHIFT[d] - shard.shape[1]))), glr


def kernel(x, meta_tokens, norm_gain, w_in, w_gate_up, b_gate, ret_norm_gain, gla_norm_gain, w_branch_ret, w_branch_gla, w_out, final_norm_gain, loss_target, m_meta_tokens, m_norm_gain, m_w_in, m_w_gate_up, m_b_gate, m_ret_norm_gain, m_gla_norm_gain, m_w_branch_ret, m_w_branch_gla, m_w_out, m_final_norm_gain, v_meta_tokens, v_norm_gain, v_w_in, v_w_gate_up, v_b_gate, v_ret_norm_gain, v_gla_norm_gain, v_w_branch_ret, v_w_branch_gla, v_w_out, v_final_norm_gain):
    xi, yi, ci = _position()
    me = _index(xi, yi, ci)
    seq = x.shape[1]
    t_rows = seq + TILE
    in_shard = w_in.shape[2]
    gu_shard = w_gate_up.shape[2]
    meta_shard = meta_tokens.shape[1]
    ret_rows, gla_rows, out_rows = w_branch_ret.shape[1], w_branch_gla.shape[1], w_out.shape[1]

    assert in_shard == IN_SHARD
    slab_local, glr_local = lax.switch(me, [functools.partial(_shard_to_slab, d=d) for d in range(N_DEV)], w_in[0])
    small_local = jnp.concatenate([meta_tokens, jnp.pad(w_gate_up[0], ((0, 0), (0, LANES - gu_shard))),
                                   glr_local.reshape(-1, LANES)], axis=0)
    slabs, g_small = _all_gather([slab_local.astype(BF16), small_local], "all_gather_shards")
    n_small = N_META + GLA_RANK
    w_glr = jnp.pad(g_small[GLR_DEV, n_small:].reshape(D_MODEL, GLA_RANK), ((0, 0), (0, LANES - GLA_RANK))).astype(BF16)
    meta_full = jnp.transpose(g_small[:, :N_META, :], (1, 0, 2)).reshape(N_META, D_MODEL)
    wgu_full = jnp.transpose(g_small[:, N_META:n_small, :gu_shard], (1, 0, 2)).reshape(GLA_RANK, GLA_HEADS * GLA_K)
    wgu_pad = jnp.pad(wgu_full, ((0, LANES - GLA_RANK), (0, 0)))

    pos = jnp.arange(t_rows, dtype=F32) - float(PAD_ROWS)
    half = RET_QK // 2
    inv = ROPE_BASE ** (-jnp.arange(half, dtype=F32) / half)
    ang = pos[:, None] * inv[None, :]
    cos, sin = jnp.cos(ang), jnp.sin(ang)
    lg = jnp.log1p(-(2.0 ** (-5.0 - jnp.arange(RET_HEADS, dtype=F32))))

    head = jnp.concatenate([jnp.zeros((PAD_ROWS, D_MODEL), F32), meta_full], axis=0)
    ut, proj, glr = _inproj_tiles(head, x[0], norm_gain, slabs, w_glr)
    o_ret_raw, o_ret, ret_states, (g_br, g_bg, g_o) = _ret_fwd(
        proj, cos, sin, ret_norm_gain, lg, [w_branch_ret[0].astype(BF16), w_branch_gla[0].astype(BF16), w_out[0].astype(BF16)])
    w_br, w_bg, w_o = g_br.reshape(RET_W, D_MODEL), g_bg.reshape(GLA_W, D_MODEL), g_o.reshape(D_MODEL, D_MODEL)
    masks, cum_fwd, cum_bwd = _gla_tables()
    o_gla_raw, o_gla, gla_states, gla_scores_t = _gla_fwd(proj, glr, wgu_pad, b_gate, gla_norm_gain, masks, cum_fwd)
    (dh1, d_mr, d_mg, do_ret, do_gla, loss_part, d_gfinal, dw_br, dw_bg, dw_o) = _merge_fwd_bwd(
        o_ret, o_gla, proj, x[0], loss_target[0], final_norm_gain.reshape(1, D_MODEL), w_br, w_bg, w_o)

    d_rq, d_rk, d_rv, d_rg, d_gret = _ret_bwd(proj, cos, sin, ret_norm_gain, lg, o_ret_raw, do_ret, ret_states)
    d_gq, d_gk, d_gv, d_gg, dglr_parts, d_wgu, d_bgate, d_ggla = _gla_bwd(
        proj, glr, wgu_pad, b_gate, gla_norm_gain, o_gla_raw, do_gla, gla_states, gla_scores_t, masks, cum_fwd, cum_bwd)
    dseg = dict(rq=d_rq, rk=d_rk, rv=d_rv, rg=d_rg, gq=d_gq, gk=d_gk, gv=d_gv, gg=d_gg, mr=d_mr, mg=d_mg)
    dw_blocks, dw_glr = _inproj_bwd_w(ut, dseg, dglr_parts)

    row_sends = [dw_br.reshape(N_DEV, ret_rows, D_MODEL), dw_bg.reshape(N_DEV, gla_rows, D_MODEL),
                 dw_o.reshape(N_DEV, out_rows, D_MODEL)]
    sib_in, *sib_rows = _exchange_sibling(dw_blocks, row_sends)
    core = ci.astype(jnp.int32).reshape(1)
    chip_partials = [_chip_partial_slab(dw_blocks, sib_in, core)] + [
        _chip_partial_rows(send, sib, core, "chip_partial_" + name)
        for send, sib, name in zip(row_sends, sib_rows, ("w_branch_ret", "w_branch_gla", "w_out"))]
    grad_x, d_head, d_gnorm, p_in, p_br, p_bg, p_o = _inproj_bwd_x(
        dseg, dglr_parts, head, x[0], dh1, norm_gain, slabs, w_glr, chip_partials)
    small_shapes = [(N_META, D_MODEL), (1, D_MODEL), (GLA_RANK, GLA_HEADS * GLA_K), (1, GLA_HEADS * GLA_K),
                    (1, RET_W), (1, GLA_W), (1, D_MODEL), (1, LANES), (D_MODEL, GLA_RANK)]
    small_part = _pack_rows([d_head[PAD_ROWS:], d_gnorm, d_wgu[:GLA_RANK], d_bgate, d_gret, d_ggla, d_gfinal, loss_part,
                             dw_glr[:, :GLA_RANK]])
    (p_small,) = _all_gather([small_part], "all_gather_small_partials")

    (g_meta_f, g_gnorm, g_wgu_f, g_bgate, g_gret, g_ggla, g_gfinal, loss_all,
     g_wglr) = _unpack_rows(_reduce_small(p_small), small_shapes)
    g_w_in, d_w_in, nm_w_in, nv_w_in = _reduce_adam_slab(
        p_in, jnp.pad(g_wglr, ((0, 0), (0, LANES - GLA_RANK))), w_in[0], m_w_in[0], v_w_in[0], me)
    rb = gla_rows
    g_w_br, d_w_br, nm_w_br, nv_w_br = _reduce_adam(p_br, w_branch_ret[0], m_w_branch_ret[0], v_w_branch_ret[0], "adam_w_branch_ret", rb)
    g_w_bg, d_w_bg, nm_w_bg, nv_w_bg = _reduce_adam(p_bg, w_branch_gla[0], m_w_branch_gla[0], v_w_branch_gla[0], "adam_w_branch_gla", rb)
    g_w_o, d_w_o, nm_w_o, nv_w_o = _reduce_adam(p_o, w_out[0], m_w_out[0], v_w_out[0], "adam_w_out", rb)
    g_meta = lax.dynamic_slice_in_dim(g_meta_f, me * meta_shard, meta_shard, axis=1)
    g_wgu = lax.dynamic_slice_in_dim(g_wgu_f, me * gu_shard, gu_shard, axis=1)
    s_g = [g_meta, g_gnorm, g_wgu, g_bgate, g_gret, g_ggla, g_gfinal]
    s_w = [meta_tokens, norm_gain, w_gate_up[0], b_gate, ret_norm_gain, gla_norm_gain, final_norm_gain]
    s_m = [m_meta_tokens, m_norm_gain, m_w_gate_up[0], m_b_gate, m_ret_norm_gain, m_gla_norm_gain, m_final_norm_gain]
    s_v = [v_meta_tokens, v_norm_gain, v_w_gate_up[0], v_b_gate, v_ret_norm_gain, v_gla_norm_gain, v_final_norm_gain]
    shapes = [a.shape for a in s_g]
    s_d, s_nm, s_nv = [_unpack_rows(p, shapes) for p in _adam_small(*[_pack_rows(l) for l in (s_g, s_w, s_m, s_v)])]

    loss = loss_all[0, 0]
    grad_x = grad_x[None]

    def order(meta, gnorm, win, wgu, bgate, gret, ggla, wbr, wbg, wo, gfin):
        return (meta, gnorm, win[None], wgu[None], bgate, gret, ggla, wbr[None], wbg[None], wo[None], gfin.reshape(final_norm_gain.shape))

    def small(l):
        return dict(meta=l[0], gnorm=l[1], wgu=l[2], bgate=l[3], gret=l[4], ggla=l[5], gfin=l[6])

    grads = order(win=g_w_in, wbr=g_w_br, wbg=g_w_bg, wo=g_w_o, **small(s_g))
    deltas = order(win=d_w_in, wbr=d_w_br, wbg=d_w_bg, wo=d_w_o, **small(s_d))
    new_m = order(win=nm_w_in, wbr=nm_w_br, wbg=nm_w_bg, wo=nm_w_o, **small(s_nm))
    new_v = order(win=nv_w_in, wbr=nv_w_br, wbg=nv_w_bg, wo=nv_w_o, **small(s_nv))
    return (loss, grad_x, *grads, *deltas, *new_m, *new_v)
```

```python
import functools

import jax
import jax.numpy as jnp
from jax import lax
from jax.experimental import pallas as pl
from jax.experimental.pallas import tpu as pltpu

F32 = jnp.float32
BF16 = jnp.bfloat16

D_MODEL = 1024
N_META = 16
TILE = 256
PAD_ROWS = TILE - N_META
RET_HEADS = 4
RET_QK = 256
RET_V = 512
RET_W = RET_HEADS * RET_V
GLA_HEADS = 4
GLA_K = 128
GLA_V = 256
GLA_W = GLA_HEADS * GLA_V
GLA_RANK = 16
GLA_TAU = 16.0
GLA_CHUNK = 16
ROPE_BASE = 10000.0
EPS = 1e-6
LANES = 128
N_DEV = 8
SEG_NAMES = ("rq", "rk", "rv", "rg", "gq", "gk", "gv", "gg", "mr", "mg")
SEG_W = (1024, 1024, 2048, 2048, 512, 512, 1024, 1024, 1024, 1024)
SEG_OFF = tuple(sum(SEG_W[:i]) for i in range(len(SEG_W)))
AL_COLS = sum(SEG_W)
IN_COLS = AL_COLS + GLA_RANK
GLR_OFF = sum(SEG_W[:8])
IN_SHARD = IN_COLS // N_DEV


def _aligned_col(c):
    assert c <= GLR_OFF or c >= GLR_OFF + GLA_RANK
    return c if c <= GLR_OFF else c - GLA_RANK


SLAB_BOUND = tuple(_aligned_col(IN_SHARD * d) for d in range(N_DEV + 1))
SLAB_BLK0 = tuple(b // LANES for b in SLAB_BOUND[:-1])
SLAB_SHIFT = tuple(b % LANES for b in SLAB_BOUND[:-1])
SLAB_BLOCKS = max(-(-SLAB_BOUND[d + 1] // LANES) - SLAB_BLK0[d] for d in range(N_DEV))
SLAB_W = SLAB_BLOCKS * LANES
GLR_DEV = GLR_OFF // IN_SHARD
GLR_LOCAL = GLR_OFF - GLR_DEV * IN_SHARD
assert all(SLAB_BLK0[d] + SLAB_BLOCKS <= AL_COLS // LANES for d in range(N_DEV))
VMEM_LIMIT = 58 * 1024 * 1024
ADAM_LR, ADAM_B1, ADAM_B2, ADAM_EPS, ADAM_WD, ADAM_STEP = 0.001, 0.9, 0.999, 1e-08, 0.01, 10
ANY = pl.BlockSpec(memory_space=pl.ANY)
MESH = pl.DeviceIdType.MESH


def _call(body, name, **kw):
    return pl.pallas_call(body, name=name, **kw)


def _params(sem=None):
    return pltpu.CompilerParams(dimension_semantics=sem, vmem_limit_bytes=VMEM_LIMIT)


def _mm(a, b):
    return jnp.dot(a, b, preferred_element_type=F32)


def _mm_nt(a, b):
    return lax.dot_general(a, b, (((1,), (1,)), ((), ())), preferred_element_type=F32)


def _mm_tn(a, b):
    return lax.dot_general(a, b, (((0,), (0,)), ((), ())), preferred_element_type=F32)


def _sigmoid(x):
    return 1.0 / (1.0 + jnp.exp(-x))


def _rope(t, cos, sin):
    half = t.shape[-1] // 2
    t1, t2 = t[:, :half], t[:, half:]
    return jnp.concatenate([t1 * cos - t2 * sin, t2 * cos + t1 * sin], axis=-1)


def _rope_bwd(g, cos, sin):
    half = g.shape[-1] // 2
    g1, g2 = g[:, :half], g[:, half:]
    return jnp.concatenate([g1 * cos + g2 * sin, g2 * cos - g1 * sin], axis=-1)


def _row_mean(x):
    return jnp.mean(x, axis=-1, keepdims=True)


def _col_sum(x):
    return jnp.sum(x, axis=0, keepdims=True)


def _tile_rows(head_ref, x_ref):
    return jnp.where(pl.program_id(0) == 0, head_ref[...], x_ref[...])


def _head_spec():
    return pl.BlockSpec((TILE, D_MODEL), lambda i: (0, 0))


def _x_spec():
    return pl.BlockSpec((TILE, D_MODEL), lambda i: (jnp.maximum(i - 1, 0), 0))


def _slab_plan():
    interior, shared = [], []
    for d in range(N_DEV):
        lo, hi = -(-SLAB_BOUND[d] // LANES), SLAB_BOUND[d + 1] // LANES
        interior.append((d, LANES * (lo - SLAB_BLK0[d]), LANES * lo, LANES * (hi - lo)))
        if d + 1 < N_DEV and SLAB_BOUND[d + 1] % LANES:
            shared.append((hi, d, hi - SLAB_BLK0[d]))
    return interior, shared


W_SCRATCH = lambda: [pltpu.VMEM((D_MODEL, AL_COLS), BF16), pltpu.VMEM((D_MODEL, LANES), BF16),
                     pltpu.VMEM((2 * (N_DEV - 1), D_MODEL, LANES), BF16), pltpu.SemaphoreType.DMA((3 * N_DEV,))]


def _load_weight(slabs_hbm, wg_hbm, w_vm, wg_vm, edge_vm, sem):
    interior, shared = _slab_plan()
    copies = [pltpu.make_async_copy(wg_hbm, wg_vm, sem.at[0])]
    for d, src, dst, width in interior:
        copies.append(pltpu.make_async_copy(slabs_hbm.at[d, :, pl.ds(src, width)], w_vm.at[:, pl.ds(dst, width)], sem.at[1 + d]))
    for n, (_, d, blk) in enumerate(shared):
        copies.append(pltpu.make_async_copy(slabs_hbm.at[d, :, pl.ds(LANES * blk, LANES)], edge_vm.at[2 * n], sem.at[1 + N_DEV + 2 * n]))
        copies.append(pltpu.make_async_copy(slabs_hbm.at[d + 1, :, pl.ds(0, LANES)], edge_vm.at[2 * n + 1], sem.at[2 + N_DEV + 2 * n]))
    for cp in copies:
        cp.start()
    for cp in copies:
        cp.wait()
    for n, (blk, _, _) in enumerate(shared):
        w_vm[:, LANES * blk:LANES * (blk + 1)] = edge_vm[2 * n] + edge_vm[2 * n + 1]


def _proj_specs(names, n_units, where):
    specs = []
    for name in names:
        s = SEG_NAMES.index(name)
        nblk = SEG_W[s] // n_units // LANES
        base = SEG_OFF[s] // LANES
        assert base % nblk == 0
        specs.append(pl.BlockSpec((nblk, TILE, LANES), lambda *g, base=base, nblk=nblk: (base // nblk + where(*g)[0], where(*g)[1], 0)))
    return specs


def _cols(ref, unit=0, n_units=1):
    n = ref.shape[0] // n_units
    return ref[unit * n] if n == 1 else jnp.concatenate([ref[unit * n + j] for j in range(n)], axis=1)


def _inproj_tiles(head, x, g_norm, slabs, w_glr):
    t_rows = x.shape[0] + TILE
    nt = t_rows // TILE
    n_blocks = AL_COLS // LANES

    def body(head_ref, x_ref, g_ref, slabs_hbm, wg_hbm, ut_ref, proj_ref, glr_ref, w_vm, wg_vm, edge_vm, sem):
        @pl.when(pl.program_id(0) == 0)
        def _():
            _load_weight(slabs_hbm, wg_hbm, w_vm, wg_vm, edge_vm, sem)

        x = _tile_rows(head_ref, x_ref)
        r = lax.rsqrt(_row_mean(x * x) + EPS)
        u32 = (x * r * g_ref[...]).astype(BF16).astype(F32)
        u = u32.astype(BF16)
        ut_ref[...] = u32.T.astype(BF16)
        for s in range(len(SEG_W)):
            res = _mm(u, w_vm[:, SEG_OFF[s]:SEG_OFF[s] + SEG_W[s]]).astype(BF16)
            for j in range(SEG_W[s] // LANES):
                proj_ref[SEG_OFF[s] // LANES + j] = res[:, j * LANES:(j + 1) * LANES]
        glr_ref[...] = _mm(u, wg_vm[...])

    return _call(
        body, "inproj_fwd_tiles", grid=(nt,),
        out_shape=[jax.ShapeDtypeStruct((nt, D_MODEL, TILE), BF16), jax.ShapeDtypeStruct((n_blocks, t_rows, LANES), BF16),
                   jax.ShapeDtypeStruct((t_rows, LANES), F32)],
        in_specs=[_head_spec(), _x_spec(), pl.BlockSpec((1, D_MODEL), lambda i: (0, 0)), ANY, ANY],
        out_specs=[pl.BlockSpec((None, D_MODEL, TILE), lambda i: (i, 0, 0)), pl.BlockSpec((n_blocks, TILE, LANES), lambda i: (0, i, 0)),
                   pl.BlockSpec((TILE, LANES), lambda i: (i, 0))],
        scratch_shapes=W_SCRATCH(), compiler_params=_params(("arbitrary",)),
    )(head, x, g_norm, slabs, w_glr)


def _ret_decay(lgh):
    i = lax.broadcasted_iota(jnp.int32, (TILE, TILE), 0)
    j = lax.broadcasted_iota(jnp.int32, (TILE, TILE), 1)
    rel = (i - j).astype(F32)
    return jnp.where(rel >= 0, jnp.exp(jnp.maximum(rel, 0.0) * lgh), 0.0)


def _ret_vectors(lgh):
    idx = lax.broadcasted_iota(jnp.int32, (TILE, 1), 0).astype(F32)
    xi = jnp.exp((idx + 1.0) * lgh)
    zeta = jnp.exp((TILE - 1.0 - idx) * lgh)
    gc = jnp.exp(jnp.full((1, 1), float(TILE), F32) * lgh)
    return xi, zeta, gc


def _ret_fwd(proj, cos, sin, gain, lg, row_shards):
    t_rows = cos.shape[0]
    nt = t_rows // TILE
    ns = len(row_shards)

    def body(lg_ref, q_ref, k_ref, v_ref, g_ref, cos_ref, sin_ref, gain_ref, *rest):
        shard_refs, (oraw_ref, oret_ref, st_ref), gathered = rest[:ns], rest[ns:ns + 3], rest[ns + 3:2 * ns + 3]
        s_acc, dm = rest[2 * ns + 3:2 * ns + 5]
        gather = _Exchange(shard_refs, gathered, rest[2 * ns + 5:], among_chips=False)
        t = pl.program_id(0)

        @pl.when(t == 0)
        def _():
            gather.start()
            s_acc[...] = jnp.zeros_like(s_acc)
            for h in range(RET_HEADS):
                dm[h] = _ret_decay(lg_ref[h])

        @pl.when(t == nt - 1)
        def _():
            gather.finish()

        cos_t, sin_t = cos_ref[...], sin_ref[...]
        for h in range(RET_HEADS):
            lgh = lg_ref[h]
            q = _rope(_cols(q_ref, h, RET_HEADS).astype(F32), cos_t, sin_t)
            k = _rope(_cols(k_ref, h, RET_HEADS).astype(F32), cos_t, sin_t) * (RET_QK ** -0.5)
            xi, zeta, gc = _ret_vectors(lgh)
            v = _cols(v_ref, h, RET_HEADS)
            s_in = s_acc[h]
            p = (_mm_nt(q.astype(BF16), k.astype(BF16)) * dm[h]).astype(BF16)
            o = _mm(p, v) + _mm((q * xi).astype(BF16), s_in.astype(BF16))
            st_ref[h] = s_in.astype(BF16)
            s_acc[h] = s_in * gc + _mm_tn((k * zeta).astype(BF16), v)
            cols = slice(h * RET_V, (h + 1) * RET_V)
            oraw_ref[:, cols] = o
            oc = o - _row_mean(o)
            n = oc * lax.rsqrt(_row_mean(oc * oc) + EPS) * gain_ref[:, cols]
            g = _cols(g_ref, h, RET_HEADS).astype(F32)
            oret_ref[:, cols] = (n * g * _sigmoid(g)).astype(BF16)

    row = lambda w: pl.BlockSpec((TILE, w), lambda t: (t, 0))
    outs = _call(
        body, "ret_fwd", grid=(nt,),
        out_shape=[jax.ShapeDtypeStruct((t_rows, RET_W), F32), jax.ShapeDtypeStruct((t_rows, RET_W), BF16),
                   jax.ShapeDtypeStruct((RET_HEADS, nt, RET_QK, RET_V), BF16)]
                  + [jax.ShapeDtypeStruct((N_DEV, *a.shape), a.dtype) for a in row_shards],
        in_specs=[pl.BlockSpec(memory_space=pltpu.SMEM)] + _proj_specs(("rq", "rk", "rv", "rg"), 1, lambda t: (0, t)) + [row(LANES), row(LANES),
                  pl.BlockSpec((1, RET_W), lambda t: (0, 0))] + [ANY] * ns,
        out_specs=[row(RET_W), row(RET_W), pl.BlockSpec((RET_HEADS, None, RET_QK, RET_V), lambda t: (0, t, 0, 0))] + [ANY] * ns,
        scratch_shapes=[pltpu.VMEM((RET_HEADS, RET_QK, RET_V), F32), pltpu.VMEM((RET_HEADS, TILE, TILE), F32)] + _exchange_sems(ns, N_DEV),
        compiler_params=_params(("arbitrary",)),
    )(lg, proj, proj, proj, proj, cos, sin, gain, *row_shards)
    return outs[0], outs[1], outs[2], outs[3:]


def _ret_bwd(proj, cos, sin, gain, lg, o_raw, do_ret, states):
    t_rows = cos.shape[0]
    nt = t_rows // TILE

    def body(lg_ref, q_ref, k_ref, v_ref, g_ref, cos_ref, sin_ref, gain_ref, oraw_ref, do_ref, st_ref,
             dq_ref, dk_ref, dv_ref, dg_ref, dgain_ref, e_acc, dm):
        @pl.when(pl.program_id(0) == 0)
        def _():
            e_acc[...] = jnp.zeros_like(e_acc)
            for h in range(RET_HEADS):
                dm[h] = _ret_decay(lg_ref[h])
            dgain_ref[...] = jnp.zeros_like(dgain_ref)

        cos_t, sin_t = cos_ref[...], sin_ref[...]
        for h in range(RET_HEADS):
            lgh = lg_ref[h]
            cols = slice(h * RET_V, (h + 1) * RET_V)
            qcols = slice(h * RET_QK, (h + 1) * RET_QK)
            q = _rope(_cols(q_ref, h, RET_HEADS).astype(F32), cos_t, sin_t)
            k = _rope(_cols(k_ref, h, RET_HEADS).astype(F32), cos_t, sin_t) * (RET_QK ** -0.5)
            xi, zeta, gc = _ret_vectors(lgh)
            v = _cols(v_ref, h, RET_HEADS)
            g = _cols(g_ref, h, RET_HEADS).astype(F32)
            o = oraw_ref[:, cols]
            do = do_ref[:, cols].astype(F32)
            oc = o - _row_mean(o)
            rstd = lax.rsqrt(_row_mean(oc * oc) + EPS)
            xh = oc * rstd
            gain_t = gain_ref[:, cols]
            sg = _sigmoid(g)
            dn = do * (g * sg)
            dg_ref[:, cols] = (do * (xh * gain_t) * (sg * (1.0 + g * (1.0 - sg)))).astype(BF16)
            dgain_ref[:, cols] += _col_sum(dn * xh)
            dxh = dn * gain_t
            dob = (rstd * (dxh - _row_mean(dxh) - xh * _row_mean(dxh * xh))).astype(BF16)
            dmat = dm[h]
            qb, kb = q.astype(BF16), k.astype(BF16)
            p = (_mm_nt(qb, kb) * dmat).astype(BF16)
            dp = (_mm_nt(dob, v) * dmat).astype(BF16)
            s_in = st_ref[h]
            e_in = e_acc[h]
            e_b = e_in.astype(BF16)
            dq = _mm(dp, kb) + _mm_nt(dob, s_in) * xi
            dk = _mm_tn(dp, qb) + _mm_nt(v, e_b) * zeta
            dv_ref[:, cols] = (_mm_tn(p, dob) + _mm((k * zeta).astype(BF16), e_b)).astype(BF16)
            e_acc[h] = e_in * gc + _mm_tn((q * xi).astype(BF16), dob)
            dq_ref[:, qcols] = _rope_bwd(dq, cos_t, sin_t).astype(BF16)
            dk_ref[:, qcols] = (_rope_bwd(dk, cos_t, sin_t) * (RET_QK ** -0.5)).astype(BF16)

    row = lambda w: pl.BlockSpec((TILE, w), lambda j: (nt - 1 - j, 0))
    vec = pl.BlockSpec((1, RET_W), lambda j: (0, 0))
    return _call(
        body, "ret_bwd", grid=(nt,),
        out_shape=[jax.ShapeDtypeStruct((t_rows, RET_HEADS * RET_QK), BF16), jax.ShapeDtypeStruct((t_rows, RET_HEADS * RET_QK), BF16),
                   jax.ShapeDtypeStruct((t_rows, RET_W), BF16), jax.ShapeDtypeStruct((t_rows, RET_W), BF16),
                   jax.ShapeDtypeStruct((1, RET_W), F32)],
        in_specs=[pl.BlockSpec(memory_space=pltpu.SMEM)] + _proj_specs(("rq", "rk", "rv", "rg"), 1, lambda j: (0, nt - 1 - j)) + [row(LANES), row(LANES), vec,
                  row(RET_W), row(RET_W), pl.BlockSpec((RET_HEADS, None, RET_QK, RET_V), lambda j: (0, nt - 1 - j, 0, 0))],
        out_specs=[row(RET_HEADS * RET_QK), row(RET_HEADS * RET_QK), row(RET_W), row(RET_W), vec],
        scratch_shapes=[pltpu.VMEM((RET_HEADS, RET_QK, RET_V), F32), pltpu.VMEM((RET_HEADS, TILE, TILE), F32)],
        compiler_params=_params(("arbitrary",)),
    )(lg, proj, proj, proj, proj, cos, sin, gain, o_raw, do_ret, states)


GLA_LEVELS = (32, 64, 128, 256)
N_TERMS = 1 + len(GLA_LEVELS)


def _gla_tables():
    p = jnp.arange(TILE)[:, None]
    r = jnp.arange(TILE)[None, :]
    masks = [(p // GLA_CHUNK == r // GLA_CHUNK) & (r <= p)]
    for blk in GLA_LEVELS:
        masks.append((p // blk == r // blk) & (p % blk >= blk // 2) & (r % blk < blk // 2))
    masks = jnp.stack(masks + [m.T for m in masks]).astype(F32)
    cum_fwd = jnp.concatenate([r <= p, masks[0] > 0], axis=0).astype(BF16)
    cum_bwd = jnp.concatenate([r >= p, masks[N_TERMS] > 0], axis=1).astype(BF16)
    return masks, cum_fwd, cum_bwd


def _split3(x):
    hi = x.astype(BF16)
    rest = x - hi.astype(F32)
    mid = rest.astype(BF16)
    lo = (rest - mid.astype(F32)).astype(BF16)
    return jnp.concatenate([hi, mid, lo], axis=1)


def _join3(y):
    w = y.shape[1] // 3
    return (y[:, 2 * w:] + y[:, w:2 * w]) + y[:, :w]


def _gla_decays(glr_ref, wgu_ref, b_ref, cum_ref):
    z = _mm(glr_ref[...].astype(BF16), wgu_ref[...].astype(BF16)) + b_ref[...]
    la = (jnp.minimum(z, 0.0) - jnp.log(1.0 + jnp.exp(-jnp.abs(z)))) / GLA_TAU
    width = la.shape[1]
    hi = la.astype(BF16)
    rest = la - hi.astype(F32)
    mid = rest.astype(BF16)
    lo = (rest - mid.astype(F32)).astype(BF16)
    y = _mm(cum_ref[...], jnp.concatenate([hi, mid, lo], axis=1))
    gb = (y[:, 2 * width:] + y[:, width:2 * width]) + y[:, :width]
    return z, gb[:TILE], gb[TILE:]


def _gla_prep(h, q_ref, k_ref, g_all, b_all, g_scr, ref_scr):
    cols = slice(h * GLA_K, (h + 1) * GLA_K)
    g, b = g_all[:, cols], b_all[:, cols]
    g_scr[h] = g
    factors = [(jnp.exp(b), jnp.exp(-b))]
    for lvl, blk in enumerate(GLA_LEVELS):
        for n in range(TILE // blk):
            ref_scr[h, lvl, n * blk:(n + 1) * blk, :] = jnp.broadcast_to(g_scr[h, pl.ds(n * blk + blk // 2 - 1, 1), :], (blk, GLA_K))
        x = g - ref_scr[h, lvl]
        factors.append((jnp.exp(jnp.minimum(x, 0.0)), jnp.exp(jnp.minimum(-x, 0.0))))
    g_last = g_scr[h, pl.ds(TILE - 1, 1), :]
    q = _cols(q_ref, h, GLA_HEADS).astype(F32) * (GLA_K ** -0.5)
    k = _cols(k_ref, h, GLA_HEADS).astype(F32)
    return q, k, factors, jnp.exp(g), jnp.exp(g_last), jnp.exp(g_last - g)


def _gla_scores(q, k, factors, m_ref):
    a = jnp.zeros((TILE, TILE), F32)
    for l, (fq, fk) in enumerate(factors):
        s = _mm_nt((q * fq).astype(BF16), (k * fk).astype(BF16))
        a = jnp.where(m_ref[l] > 0.0, s, a)
    return a


def _gla_fwd(proj, glr, wgu_pad, b_gate, gain, masks, cum_fwd):
    t_rows = glr.shape[0]
    nt = t_rows // TILE

    def body(q_ref, k_ref, v_ref, g_ref, glr_ref, wgu_ref, b_ref, gain_ref, m_ref, cum_ref, oraw_ref, ogla_ref, st_ref, at_ref,
             s_acc, g_scr, ref_scr):
        @pl.when(pl.program_id(0) == 0)
        def _():
            s_acc[...] = jnp.zeros_like(s_acc)

        _, g_all, b_all = _gla_decays(glr_ref, wgu_ref, b_ref, cum_ref)
        for h in range(GLA_HEADS):
            q, k, factors, e_g, e_last, e_end = _gla_prep(h, q_ref, k_ref, g_all, b_all, g_scr, ref_scr)
            v = _cols(v_ref, h, GLA_HEADS)
            st = s_acc[h]
            st_ref[h] = st
            a = _gla_scores(q, k, factors, m_ref)
            at_ref[h] = a.T.astype(BF16)
            o = _mm(a.astype(BF16), v) + _mm_nt((q * e_g).astype(BF16), st.astype(BF16))
            s_acc[h] = st * e_last + _mm(v.astype(F32).T.astype(BF16), (k * e_end).astype(BF16))
            cols = slice(h * GLA_V, (h + 1) * GLA_V)
            oraw_ref[:, cols] = o
            n = o * lax.rsqrt(_row_mean(o * o) + EPS) * gain_ref[:, cols]
            g = _cols(g_ref, h, GLA_HEADS).astype(F32)
            ogla_ref[:, cols] = (n * g * _sigmoid(g)).astype(BF16)

    row = lambda w: pl.BlockSpec((TILE, w), lambda t: (t, 0))
    whole = lambda *shape: pl.BlockSpec(shape, lambda t: (0,) * len(shape))
    return _call(
        body, "gla_fwd", grid=(nt,),
        out_shape=[jax.ShapeDtypeStruct((t_rows, GLA_W), F32), jax.ShapeDtypeStruct((t_rows, GLA_W), BF16),
                   jax.ShapeDtypeStruct((GLA_HEADS, nt, GLA_V, GLA_K), F32), jax.ShapeDtypeStruct((GLA_HEADS, t_rows, TILE), BF16)],
        in_specs=_proj_specs(("gq", "gk", "gv", "gg"), 1, lambda t: (0, t)) + [row(LANES), whole(LANES, GLA_HEADS * GLA_K),
                  whole(1, GLA_HEADS * GLA_K), whole(1, GLA_W), whole(N_TERMS, TILE, TILE), whole(2 * TILE, TILE)],
        out_specs=[row(GLA_W), row(GLA_W), pl.BlockSpec((GLA_HEADS, None, GLA_V, GLA_K), lambda t: (0, t, 0, 0)),
                   pl.BlockSpec((GLA_HEADS, TILE, TILE), lambda t: (0, t, 0))],
        scratch_shapes=[pltpu.VMEM((GLA_HEADS, GLA_V, GLA_K), F32), pltpu.VMEM((GLA_HEADS, TILE, GLA_K), F32),
                        pltpu.VMEM((GLA_HEADS, len(GLA_LEVELS), TILE, GLA_K), F32)],
        compiler_params=_params(("arbitrary",)),
    )(proj, proj, proj, proj, glr, wgu_pad, b_gate, gain, masks, cum_fwd)


def _gla_bwd(proj, glr, wgu_pad, b_gate, gain, o_raw, do_gla, states, a_t, masks, cum_fwd, cum_bwd):
    t_rows = glr.shape[0]
    nt = t_rows // TILE

    def body(q_ref, k_ref, v_ref, g_ref, glr_ref, wgu_ref, b_ref, gain_ref, m_ref, cum_ref, cumb_ref, oraw_ref, do_ref, st_ref, at_ref,
             dq_ref, dk_ref, dv_ref, dg_ref, dglr_ref, dwgu_ref, dbg_ref, dgain_ref, d_acc, g_scr, ref_scr, dref_scr):
        @pl.when(pl.program_id(0) == 0)
        def _():
            d_acc[...] = jnp.zeros_like(d_acc)
            dwgu_ref[...] = jnp.zeros_like(dwgu_ref)
            dbg_ref[...] = jnp.zeros_like(dbg_ref)
            dgain_ref[...] = jnp.zeros_like(dgain_ref)

        z_all, g_all, b_all = _gla_decays(glr_ref, wgu_ref, b_ref, cum_ref)
        dla_parts = []
        for h in range(GLA_HEADS):
            q, k, factors, e_g, e_last, e_end = _gla_prep(h, q_ref, k_ref, g_all, b_all, g_scr, ref_scr)
            v = _cols(v_ref, h, GLA_HEADS)
            cols = slice(h * GLA_V, (h + 1) * GLA_V)
            kcols = slice(h * GLA_K, (h + 1) * GLA_K)
            o = oraw_ref[:, cols]
            do = do_ref[:, cols].astype(F32)
            g = _cols(g_ref, h, GLA_HEADS).astype(F32)
            rinv = lax.rsqrt(_row_mean(o * o) + EPS)
            nh = o * rinv
            gain_t = gain_ref[:, cols]
            sg = _sigmoid(g)
            dn = do * (g * sg)
            dg_ref[:, cols] = (do * (nh * gain_t) * (sg * (1.0 + g * (1.0 - sg)))).astype(BF16)
            dgain_ref[:, cols] += _col_sum(dn * nh)
            dnh = dn * gain_t
            dor = rinv * (dnh - nh * _row_mean(dnh * nh))
            dob = dor.astype(BF16)
            a_t = at_ref[h]
            da = _mm_nt(dob, v).astype(BF16)
            da_t = _mm_nt(v, dob).astype(BF16)
            st_in = st_ref[h]
            d_out = d_acc[h]
            d_out_b = d_out.astype(BF16)
            qg, kg = q * e_g, k * e_end
            dqg = _mm(dob, st_in.astype(BF16))
            dkg = _mm(v, d_out_b)
            dv_ref[:, cols] = (_mm(a_t, dob) + _mm_nt(kg.astype(BF16), d_out_b)).astype(BF16)
            d_acc[h] = d_out * e_last + _mm(dor.T.astype(BF16), qg.astype(BF16))
            dq = dqg * e_g
            dk = dkg * e_end
            dkg_kg = dkg * kg
            dg_cum = dqg * qg - dkg_kg
            db = None
            for l, (fq, fk) in enumerate(factors):
                qt, kt = q * fq, k * fk
                dqt = _mm(da * m_ref[l], kt.astype(BF16))
                dkt = _mm(da_t * m_ref[N_TERMS + l], qt.astype(BF16))
                dq = dq + dqt * fq
                dk = dk + dkt * fk
                diff = dqt * qt - dkt * kt
                if l == 0:
                    db = diff
                else:
                    dg_cum = dg_cum + diff
                    dref_scr[h, l - 1] = diff
            dq_ref[:, kcols] = (dq * (GLA_K ** -0.5)).astype(BF16)
            dk_ref[:, kcols] = dk.astype(BF16)
            g_scr[h] = dg_cum
            g_scr[h, pl.ds(TILE - 1, 1), :] += e_last * _col_sum(d_out * st_in) + _col_sum(dkg_kg)
            for lvl, blk in enumerate(GLA_LEVELS):
                for n in range(TILE // blk):
                    g_scr[h, pl.ds(n * blk + blk // 2 - 1, 1), :] -= _col_sum(dref_scr[h, lvl, n * blk:(n + 1) * blk, :])
            dla_parts.append(_join3(_mm(cumb_ref[...], jnp.concatenate([_split3(g_scr[h]), _split3(db)], axis=0))))
        dz = jnp.concatenate(dla_parts, axis=1) * (1.0 / GLA_TAU) * _sigmoid(-z_all)
        dzb = dz.astype(BF16)
        wgu_b = wgu_ref[...].astype(BF16)
        for h in range(GLA_HEADS):
            kcols = slice(h * GLA_K, (h + 1) * GLA_K)
            dglr_ref[h] = _mm_nt(dzb[:, kcols], wgu_b[:, kcols]).astype(BF16)
        dwgu_ref[...] += _mm(glr_ref[...].T.astype(BF16), dzb)
        dbg_ref[...] += _col_sum(dz)

    row = lambda w: pl.BlockSpec((TILE, w), lambda j: (nt - 1 - j, 0))
    whole = lambda *shape: pl.BlockSpec(shape, lambda j: (0,) * len(shape))
    return _call(
        body, "gla_bwd", grid=(nt,),
        out_shape=[jax.ShapeDtypeStruct((t_rows, GLA_HEADS * GLA_K), BF16), jax.ShapeDtypeStruct((t_rows, GLA_HEADS * GLA_K), BF16),
                   jax.ShapeDtypeStruct((t_rows, GLA_W), BF16), jax.ShapeDtypeStruct((t_rows, GLA_W), BF16),
                   jax.ShapeDtypeStruct((GLA_HEADS, t_rows, LANES), BF16), jax.ShapeDtypeStruct((LANES, GLA_HEADS * GLA_K), F32),
                   jax.ShapeDtypeStruct((1, GLA_HEADS * GLA_K), F32), jax.ShapeDtypeStruct((1, GLA_W), F32)],
        in_specs=_proj_specs(("gq", "gk", "gv", "gg"), 1, lambda j: (0, nt - 1 - j)) + [row(LANES),
                  whole(LANES, GLA_HEADS * GLA_K), whole(1, GLA_HEADS * GLA_K), whole(1, GLA_W),
                  whole(2 * N_TERMS, TILE, TILE), whole(2 * TILE, TILE), whole(TILE, 2 * TILE), row(GLA_W), row(GLA_W),
                  pl.BlockSpec((GLA_HEADS, None, GLA_V, GLA_K), lambda j: (0, nt - 1 - j, 0, 0)),
                  pl.BlockSpec((GLA_HEADS, TILE, TILE), lambda j: (0, nt - 1 - j, 0))],
        out_specs=[row(GLA_HEADS * GLA_K), row(GLA_HEADS * GLA_K), row(GLA_W), row(GLA_W),
                   pl.BlockSpec((GLA_HEADS, TILE, LANES), lambda j: (0, nt - 1 - j, 0)), whole(LANES, GLA_HEADS * GLA_K),
                   whole(1, GLA_HEADS * GLA_K), whole(1, GLA_W)],
        scratch_shapes=[pltpu.VMEM((GLA_HEADS, GLA_V, GLA_K), F32), pltpu.VMEM((GLA_HEADS, TILE, GLA_K), F32),
                        pltpu.VMEM((GLA_HEADS, len(GLA_LEVELS), TILE, GLA_K), F32),
                        pltpu.VMEM((GLA_HEADS, len(GLA_LEVELS), TILE, GLA_K), F32)],
        compiler_params=_params(("arbitrary",)),
    )(proj, proj, proj, proj, glr, wgu_pad, b_gate, gain, masks.astype(BF16), cum_fwd, cum_bwd, o_raw, do_gla, states, a_t)


def _merge_fwd_bwd(o_ret, o_gla, proj, x, target, g_final, w_br, w_bg, w_out):
    t_rows = x.shape[0] + TILE
    nt = t_rows // TILE

    def body(oret_ref, ogla_ref, mr_ref, mg_ref, h0_ref, tgt_ref, gf_ref, wbr_hbm, wbg_hbm, wout_hbm,
             dh1_ref, dmr_ref, dmg_ref, doret_ref, dogla_ref, loss_ref, dgf_ref, dwbr_hbm, dwbg_hbm, dwout_hbm,
             wbr, wbg, wout, abr, abg, aout, sem):
        i = pl.program_id(0)

        @pl.when(i == 0)
        def _():
            cps = [pltpu.make_async_copy(s, d, sem.at[n]) for n, (s, d) in enumerate(((wbr_hbm, wbr), (wbg_hbm, wbg), (wout_hbm, wout)))]
            for cp in cps:
                cp.start()
            abr[...] = jnp.zeros_like(abr)
            abg[...] = jnp.zeros_like(abg)
            aout[...] = jnp.zeros_like(aout)
            loss_ref[...] = jnp.zeros_like(loss_ref)
            dgf_ref[...] = jnp.zeros_like(dgf_ref)
            for cp in cps:
                cp.wait()
            dh1_ref[...] = jnp.zeros_like(dh1_ref)
            dmr_ref[...] = jnp.zeros_like(dmr_ref)
            dmg_ref[...] = jnp.zeros_like(dmg_ref)
            doret_ref[...] = jnp.zeros_like(doret_ref)
            dogla_ref[...] = jnp.zeros_like(dogla_ref)

        @pl.when(i > 0)
        def _():
            oret, ogla = oret_ref[...], ogla_ref[...]
            br, bg = _mm(oret, wbr[...]), _mm(ogla, wbg[...])
            sr, sg = _sigmoid(_cols(mr_ref).astype(F32)), _sigmoid(_cols(mg_ref).astype(F32))
            mb = (sr * br + sg * bg).astype(BF16)
            h1 = h0_ref[...] + _mm(mb, wout[...])
            r2 = lax.rsqrt(_row_mean(h1 * h1) + EPS)
            hn = h1 * r2
            gf = gf_ref[...]
            diff = hn * gf - tgt_ref[...]
            loss_ref[...] += 0.5 * jnp.sum(_row_mean(diff * diff))
            dy = diff * (1.0 / D_MODEL)
            dgf_ref[...] += _col_sum(dy * hn)
            dyg = dy * gf
            dh1 = r2 * (dyg - hn * _row_mean(dyg * hn))
            dh1_ref[...] = dh1
            dh1b = dh1.astype(BF16)
            dm = _mm_nt(dh1b, wout[...])
            aout[...] += _mm_tn(mb, dh1b)
            dbr = (dm * sr).astype(BF16)
            dbg = (dm * sg).astype(BF16)
            dmr_ref[...] = (dm * br * sr * (1.0 - sr)).astype(BF16)
            dmg_ref[...] = (dm * bg * sg * (1.0 - sg)).astype(BF16)
            doret_ref[...] = _mm_nt(dbr, wbr[...]).astype(BF16)
            dogla_ref[...] = _mm_nt(dbg, wbg[...]).astype(BF16)
            abr[...] += _mm_tn(oret, dbr)
            abg[...] += _mm_tn(ogla, dbg)

        @pl.when(i == nt - 1)
        def _():
            wbr[...] = abr[...].astype(BF16)
            wbg[...] = abg[...].astype(BF16)
            wout[...] = aout[...].astype(BF16)
            pltpu.sync_copy(wbr, dwbr_hbm)
            pltpu.sync_copy(wbg, dwbg_hbm)
            pltpu.sync_copy(wout, dwout_hbm)

    row = lambda w: pl.BlockSpec((TILE, w), lambda i: (i, 0))
    one = lambda w: pl.BlockSpec((1, w), lambda i: (0, 0))
    return _call(
        body, "merge_fwd_bwd", grid=(nt,),
        out_shape=[jax.ShapeDtypeStruct((t_rows, D_MODEL), F32), jax.ShapeDtypeStruct((t_rows, D_MODEL), BF16),
                   jax.ShapeDtypeStruct((t_rows, D_MODEL), BF16), jax.ShapeDtypeStruct((t_rows, RET_W), BF16),
                   jax.ShapeDtypeStruct((t_rows, GLA_W), BF16), jax.ShapeDtypeStruct((1, LANES), F32),
                   jax.ShapeDtypeStruct((1, D_MODEL), F32), jax.ShapeDtypeStruct((RET_W, D_MODEL), BF16),
                   jax.ShapeDtypeStruct((GLA_W, D_MODEL), BF16), jax.ShapeDtypeStruct((D_MODEL, D_MODEL), BF16)],
        in_specs=[row(RET_W), row(GLA_W)] + _proj_specs(("mr", "mg"), 1, lambda i: (0, i)) + [_x_spec(), _x_spec(), one(D_MODEL), ANY, ANY, ANY],
        out_specs=[row(D_MODEL), row(D_MODEL), row(D_MODEL), row(RET_W), row(GLA_W), one(LANES), one(D_MODEL), ANY, ANY, ANY],
        scratch_shapes=[pltpu.VMEM((RET_W, D_MODEL), BF16), pltpu.VMEM((GLA_W, D_MODEL), BF16), pltpu.VMEM((D_MODEL, D_MODEL), BF16),
                        pltpu.VMEM((RET_W, D_MODEL), F32), pltpu.VMEM((GLA_W, D_MODEL), F32), pltpu.VMEM((D_MODEL, D_MODEL), F32),
                        pltpu.SemaphoreType.DMA((3,))],
        compiler_params=_params(("arbitrary",)),
    )(o_ret, o_gla, proj, proj, x, target, g_final, w_br, w_bg, w_out)


def _inproj_bwd_x(dseg, dglr, head, x, dh1, g_norm, slabs, w_glr, chip_partials):
    t_rows = x.shape[0] + TILE
    nt = t_rows // TILE
    ne = len(chip_partials)

    def body(*refs):
        d_refs = refs[:10]
        dglr_ref, head_ref, x_ref, dh1_ref, g_ref, slabs_hbm, wg_hbm = refs[10:17]
        part_refs = refs[17:17 + ne]
        dx_ref, dhead_ref, dgn_ref = refs[17 + ne:20 + ne]
        landed = refs[20 + ne:20 + 2 * ne]
        w_vm, wg_vm, edge_vm, sem = refs[20 + 2 * ne:24 + 2 * ne]
        exchange = _Exchange(part_refs, landed, refs[24 + 2 * ne:], among_chips=True)

        @pl.when(pl.program_id(0) == 0)
        def _():
            exchange.start()
            dgn_ref[...] = jnp.zeros_like(dgn_ref)
            _load_weight(slabs_hbm, wg_hbm, w_vm, wg_vm, edge_vm, sem)

        @pl.when(pl.program_id(0) == nt - 1)
        def _():
            exchange.finish()

        dglr = dglr_ref[0].astype(F32)
        for h in range(1, GLA_HEADS):
            dglr = dglr + dglr_ref[h].astype(F32)
        du = _mm_nt(dglr.astype(BF16), wg_vm[...])
        for s, d_ref in enumerate(d_refs):
            du = du + _mm_nt(d_ref[...], w_vm[:, SEG_OFF[s]:SEG_OFF[s] + SEG_W[s]])
        x = _tile_rows(head_ref, x_ref)
        r = lax.rsqrt(_row_mean(x * x) + EPS)
        hn = x * r
        dgn_ref[...] += _col_sum(du * hn)
        dug = du * g_ref[...]
        dh0 = dh1_ref[...] + r * (dug - hn * _row_mean(dug * hn))
        dx_ref[...] = dh0

        @pl.when(pl.program_id(0) == 0)
        def _():
            dhead_ref[...] = dh0

    row = lambda w: pl.BlockSpec((TILE, w), lambda i: (i, 0))
    one = pl.BlockSpec((1, D_MODEL), lambda i: (0, 0))
    return _call(
        body, "inproj_bwd_x", grid=(nt,),
        out_shape=[jax.ShapeDtypeStruct((t_rows - TILE, D_MODEL), F32), jax.ShapeDtypeStruct((TILE, D_MODEL), F32),
                   jax.ShapeDtypeStruct((1, D_MODEL), F32)] + [jax.ShapeDtypeStruct(a.shape, a.dtype) for a in chip_partials],
        in_specs=[row(w) for w in SEG_W] + [pl.BlockSpec((GLA_HEADS, TILE, LANES), lambda i: (0, i, 0)),
                                            _head_spec(), _x_spec(), row(D_MODEL), one, ANY, ANY] + [ANY] * ne,
        out_specs=[_x_spec(), _head_spec(), one] + [ANY] * ne,
        scratch_shapes=W_SCRATCH() + _exchange_sems(ne, N_CHIP),
        compiler_params=_params(("arbitrary",)),
    )(*[dseg[n] for n in SEG_NAMES], dglr, head, x, dh1, g_norm, slabs, w_glr, *chip_partials)


W_TILE = 512


def _inproj_bwd_w(ut, dseg, dglr, row_sends):
    nt = ut.shape[0]
    t_rows = nt * TILE
    kc = 3 if nt % 3 == 0 else 1
    tiles = [(s, c) for s in range(len(SEG_W)) for c in range(0, SEG_W[s], W_TILE)]
    bpt = W_TILE // LANES
    nr = len(row_sends)
    n = 1 + nr
    last_tile = [(SLAB_BLK0[d] + SLAB_BLOCKS - 1) // bpt for d in range(N_DEV)]

    def body(ut_hbm, *refs):
        d_refs, dglr_hbm, row_refs = refs[:10], refs[10], refs[11:11 + nr]
        out_hbm, oglr_ref, sib = refs[11 + nr], refs[12 + nr], refs[13 + nr:13 + nr + n]
        ut_vm, dbuf, obuf, acc, gbuf, sem, send_sems, recv_sems = refs[13 + nr + n:]
        x, y, core = _position()

        def handover(d, k, landed=False):
            q = d // 2
            src = out_hbm.at[pl.ds(SLAB_BLK0[d], SLAB_BLOCKS)] if k == 0 else row_refs[k - 1].at[d]
            return pltpu.make_async_remote_copy(src_ref=sib[k].at[q] if landed else src, dst_ref=sib[k].at[q],
                                                send_sem=send_sems.at[n * q + k], recv_sem=recv_sems.at[n * q + k],
                                                device_id=(x, y, 1 - core), device_id_type=MESH)

        def for_sibling(d, ks, fn):
            @pl.when(d % 2 != core)
            def _():
                for k in ks:
                    fn(handover(d, k))

        for d in range(N_DEV):
            for_sibling(d, range(1, n), lambda cp: cp.start())

        def fetch(i):
            s, c = tiles[i]
            return pltpu.make_async_copy(d_refs[s].at[:, pl.ds(c, W_TILE)], dbuf.at[i % 2], sem.at[1 + i % 2])

        def contract(rhs_refs, width):
            acc[:, :width] = jnp.zeros((D_MODEL, width), F32)

            def step(k, carry):
                part = None
                for j in range(kc):
                    kk = k * kc + j
                    for rhs_ref in rhs_refs:
                        prod = _mm(ut_vm[kk], rhs_ref[pl.ds(pl.multiple_of(kk * TILE, TILE), TILE), :])
                        part = prod if part is None else part + prod
                acc[:, :width] += part
                return carry

            lax.fori_loop(0, nt // kc, step, 0)
            return acc[:, :width]

        load_ut = pltpu.make_async_copy(ut_hbm, ut_vm, sem.at[0])
        load_glr = pltpu.make_async_copy(dglr_hbm, gbuf, sem.at[5])
        load_ut.start()
        load_glr.start()
        fetch(0).start()
        load_ut.wait()
        stores = {}

        def stored(i):
            stores[i].wait()
            for d in range(N_DEV):
                if last_tile[d] == i:
                    for_sibling(d, [0], lambda cp: cp.start())

        for i, (s, c) in enumerate(tiles):
            if i + 1 < len(tiles):
                fetch(i + 1).start()
            fetch(i).wait()
            if i >= 2:
                stored(i - 2)
            total = contract([dbuf.at[i % 2]], W_TILE)
            for j in range(bpt):
                obuf[i % 2, j] = total[:, j * LANES:(j + 1) * LANES].astype(BF16)
            blk0 = (SEG_OFF[s] + c) // LANES
            stores[i] = pltpu.make_async_copy(obuf.at[i % 2], out_hbm.at[pl.ds(blk0, bpt)], sem.at[3 + i % 2])
            stores[i].start()
        for i in range(max(0, len(tiles) - 2), len(tiles)):
            stored(i)
        load_glr.wait()
        head_sum = gbuf[0].astype(F32)
        for h in range(1, GLA_HEADS):
            head_sum = head_sum + gbuf[h].astype(F32)
        gbuf[0] = head_sum.astype(BF16)
        oglr_ref[...] = contract([gbuf.at[0]], LANES)
        for q in range(N_CHIP):
            for k in range(n):
                handover(2 * q, k, landed=True).wait_recv()
        for d in range(N_DEV):
            for_sibling(d, range(n), lambda cp: cp.wait_send())

    outs = _call(
        body, "inproj_bwd_w",
        out_shape=[jax.ShapeDtypeStruct((AL_COLS // LANES, D_MODEL, LANES), BF16), jax.ShapeDtypeStruct((D_MODEL, LANES), F32),
                   jax.ShapeDtypeStruct((N_CHIP, SLAB_BLOCKS, D_MODEL, LANES), BF16)]
                  + [jax.ShapeDtypeStruct((N_CHIP, *r.shape[1:]), BF16) for r in row_sends],
        in_specs=[ANY] * (12 + nr), out_specs=[ANY, pl.BlockSpec(memory_space=pltpu.VMEM)] + [ANY] * n,
        scratch_shapes=[pltpu.VMEM((nt, D_MODEL, TILE), BF16), pltpu.VMEM((2, t_rows, W_TILE), BF16),
                        pltpu.VMEM((2, bpt, D_MODEL, LANES), BF16), pltpu.VMEM((D_MODEL, W_TILE), F32),
                        pltpu.VMEM((GLA_HEADS, t_rows, LANES), BF16), pltpu.SemaphoreType.DMA((6,)),
                        pltpu.SemaphoreType.DMA((n * N_CHIP,)), pltpu.SemaphoreType.DMA((n * N_CHIP,))],
        compiler_params=_params(),
    )(ut, *[dseg[n_] for n_ in SEG_NAMES], dglr, *row_sends)
    return outs[0], outs[1], outs[2], outs[3:]


def _position():
    x, y, c = lax.axis_index("x"), lax.axis_index("y"), lax.axis_index("c")
    return x, y, c


def _index(px, py, pc):
    return 4 * px + 2 * py + pc


def _all_gather(arrs, name):
    n = len(arrs)

    def body(*refs):
        ins, outs = refs[:n], refs[n:2 * n]
        send_sems, recv_sems, local_sems = refs[2 * n:]
        x, y, c = _position()
        me, sibling = (x, y, c), (x, y, 1 - c)
        chips = [(1 - x, y), (x, 1 - y), (1 - x, 1 - y)]

        def copy(a, k, block, to, src=None):
            dst = outs[a].at[_index(*block)]
            return pltpu.make_async_remote_copy(src_ref=dst if src is None else src, dst_ref=dst,
                                                send_sem=send_sems.at[7 * a + k], recv_sem=recv_sems.at[7 * a + k],
                                                device_id=to, device_id_type=MESH)

        mine = [pltpu.make_async_copy(ins[a], outs[a].at[_index(*me)], local_sems.at[a]) for a in range(n)]
        for cp in mine:
            cp.start()
        first = []
        for a in range(n):
            first.append(copy(a, 0, me, sibling, src=ins[a]))
            first += [copy(a, 1 + j, me, (*chip, c), src=ins[a]) for j, chip in enumerate(chips)]
        for cp in first:
            cp.start()
        passed = []
        for j, chip in enumerate(chips):
            for a in range(n):
                copy(a, 1 + j, (*chip, c), me).wait_recv()
                cp = copy(a, 4 + j, (*chip, c), sibling)
                cp.start()
                passed.append(cp)
        for a in range(n):
            copy(a, 0, sibling, me).wait_recv()
            for j, chip in enumerate(chips):
                copy(a, 4 + j, (*chip, 1 - c), me).wait_recv()
        for cp in first + passed:
            cp.wait_send()
        for cp in mine:
            cp.wait()

    return _call(
        body, name,
        out_shape=[jax.ShapeDtypeStruct((N_DEV, *a.shape), a.dtype) for a in arrs],
        in_specs=[ANY] * n, out_specs=[ANY] * n,
        scratch_shapes=[pltpu.SemaphoreType.DMA((7 * n,)), pltpu.SemaphoreType.DMA((7 * n,)), pltpu.SemaphoreType.DMA((n,))],
    )(*arrs)


N_CHIP = N_DEV // 2


def _slab_block0(owner):
    step = SLAB_BLK0[1]
    assert all(SLAB_BLK0[d] == step * d - (d == N_DEV - 1) for d in range(N_DEV))
    return step * owner - jnp.where(owner == N_DEV - 1, 1, 0)


def _add_bf16(c_ref, a_ref, b_ref, o_ref):
    o_ref[...] = (a_ref[...].astype(F32) + b_ref[...].astype(F32)).astype(BF16)


def _chip_partial_slab(dw_blocks, sib, core):
    blk = pl.BlockSpec((None, SLAB_BLOCKS, D_MODEL, LANES), lambda q, c_ref: (q, 0, 0, 0))
    return _call(
        functools.partial(_add_bf16), "chip_partial_w_in", out_shape=jax.ShapeDtypeStruct(sib.shape, BF16),
        grid_spec=pltpu.PrefetchScalarGridSpec(
            num_scalar_prefetch=1, grid=(N_CHIP,),
            in_specs=[pl.BlockSpec((pl.Element(SLAB_BLOCKS), pl.Element(D_MODEL), pl.Element(LANES)),
                                   lambda q, c_ref: (_slab_block0(2 * q + c_ref[0]), 0, 0)), blk],
            out_specs=blk),
        compiler_params=_params(("arbitrary",)),
    )(core, dw_blocks, sib)


def _chip_partial_rows(send, sib, core, name):
    rows, cols = send.shape[1:]
    blk = pl.BlockSpec((None, rows, cols), lambda q, c_ref: (q, 0, 0))
    return _call(
        functools.partial(_add_bf16), name, out_shape=jax.ShapeDtypeStruct(sib.shape, BF16),
        grid_spec=pltpu.PrefetchScalarGridSpec(
            num_scalar_prefetch=1, grid=(N_CHIP,),
            in_specs=[pl.BlockSpec((None, rows, cols), lambda q, c_ref: (2 * q + c_ref[0], 0, 0)), blk], out_specs=blk),
        compiler_params=_params(("arbitrary",)),
    )(core, send, sib)


def _exchange_sems(n_arrays, n_peers):
    return [pltpu.SemaphoreType.DMA((n_arrays * n_peers,)), pltpu.SemaphoreType.DMA((n_arrays * n_peers,)),
            pltpu.SemaphoreType.DMA((n_arrays,))]


class _Exchange:
    def __init__(self, srcs, dsts, sems, among_chips):
        self.arrs = list(zip(srcs, dsts))
        self.n = len(self.arrs)
        self.send_sems, self.recv_sems, self.local_sems = sems
        self.among_chips = among_chips
        x, y, c = _position()
        self.c = c
        self.me = 2 * x + y if among_chips else _index(x, y, c)
        self.n_peers = N_CHIP if among_chips else N_DEV

    def _device(self, p):
        return (p // 2, p % 2, self.c) if self.among_chips else (p // 4, (p // 2) % 2, p % 2)

    def _src(self, k, p):
        src = self.arrs[k][0]
        return src.at[p] if self.among_chips else src

    def _mine(self):
        return [pltpu.make_async_copy(self._src(k, self.me), self.arrs[k][1].at[self.me], self.local_sems.at[k]) for k in range(self.n)]

    def _copy(self, p, k, landing):
        return pltpu.make_async_remote_copy(
            src_ref=self._src(k, p), dst_ref=self.arrs[k][1].at[landing], send_sem=self.send_sems.at[self.n * p + k],
            recv_sem=self.recv_sems.at[self.n * landing + k], device_id=self._device(p), device_id_type=MESH)

    def _others(self, fn):
        for p in range(self.n_peers):
            @pl.when(p != self.me)
            def _():
                for k in range(self.n):
                    fn(p, k)

    def start(self):
        for cp in self._mine():
            cp.start()
        self._others(lambda p, k: self._copy(p, k, self.me).start())

    def finish(self):
        self._others(lambda p, k: self._copy(p, k, p).wait_recv())
        self._others(lambda p, k: self._copy(p, k, self.me).wait_send())
        for cp in self._mine():
            cp.wait()


def _adamw(g, w, m, v):
    m_new = ADAM_B1 * m + (1.0 - ADAM_B1) * g
    v_new = ADAM_B2 * v + (1.0 - ADAM_B2) * (g * g)
    m_hat = m_new / (1.0 - ADAM_B1 ** ADAM_STEP)
    v_hat = v_new / (1.0 - ADAM_B2 ** ADAM_STEP)
    delta = -ADAM_LR * (m_hat / (jnp.sqrt(v_hat) + ADAM_EPS) + ADAM_WD * w)
    return delta, m_new, v_new


def _sum_partials(p_ref):
    g = p_ref[0].astype(F32)
    for d in range(1, p_ref.shape[0]):
        g = g + p_ref[d].astype(F32)
    return g


def _reduce_adam(parts, w, m, v, name, block_rows, row_off=0):
    rows, cols = w.shape
    off = row_off // block_rows

    def body(p_ref, w_ref, m_ref, v_ref, g_ref, d_ref, mo_ref, vo_ref):
        g = _sum_partials(p_ref)
        g_ref[...] = g
        d_ref[...], mo_ref[...], vo_ref[...] = _adamw(g, w_ref[...], m_ref[...], v_ref[...])

    blk = pl.BlockSpec((block_rows, cols), lambda i: (i, 0))
    return _call(
        body, name, grid=(rows // block_rows,),
        out_shape=[jax.ShapeDtypeStruct((rows, cols), F32)] * 4,
        in_specs=[pl.BlockSpec((parts.shape[0], block_rows, cols), lambda i: (0, i + off, 0)), blk, blk, blk],
        out_specs=[blk] * 4,
        compiler_params=_params(("arbitrary",)),
    )(parts, w, m, v)


def _reduce_adam_slab(parts, glr, w, m, v, me):
    rows, cols = w.shape
    shift = jnp.asarray(SLAB_SHIFT, jnp.int32)[me]
    glr_at = jnp.where(me == GLR_DEV, GLR_LOCAL, cols).astype(jnp.int32)

    def body(s_ref, p_ref, glr_ref, w_ref, m_ref, v_ref, g_ref, d_ref, mo_ref, vo_ref):
        shift, glr_at = s_ref[0], s_ref[1]
        slab = jnp.concatenate([_sum_partials(p_ref.at[:, j]) for j in range(SLAB_BLOCKS)], axis=1)
        before = pltpu.roll(slab, SLAB_W - shift, 1)
        after = pltpu.roll(slab, lax.rem(SLAB_W - shift + GLA_RANK, SLAB_W), 1)
        wide = jnp.concatenate([glr_ref[...], jnp.zeros((LANES, SLAB_W - LANES), F32)], axis=1)
        placed = pltpu.roll(wide, lax.rem(glr_at, SLAB_W), 1)
        lane = lax.broadcasted_iota(jnp.int32, (LANES, SLAB_W), 1)
        g = jnp.where(lane < glr_at, before, jnp.where(lane < glr_at + GLA_RANK, placed, after))[:, :cols]
        g_ref[...] = g
        d_ref[...], mo_ref[...], vo_ref[...] = _adamw(g, w_ref[...], m_ref[...], v_ref[...])

    blk = pl.BlockSpec((LANES, cols), lambda i, s: (i, 0))
    return _call(
        body, "adam_w_in", out_shape=[jax.ShapeDtypeStruct((rows, cols), F32)] * 4,
        grid_spec=pltpu.PrefetchScalarGridSpec(
            num_scalar_prefetch=1, grid=(rows // LANES,),
            in_specs=[pl.BlockSpec((parts.shape[0], SLAB_BLOCKS, LANES, LANES), lambda i, s: (0, 0, i, 0)),
                      pl.BlockSpec((LANES, LANES), lambda i, s: (i, 0)), blk, blk, blk],
            out_specs=[blk] * 4),
        compiler_params=_params(("arbitrary",)),
    )(jnp.stack([shift, glr_at]), parts, glr, w, m, v)


def _reduce_small(parts):
    def body(p_ref, o_ref):
        o_ref[...] = _sum_partials(p_ref)

    return _call(body, "reduce_small", out_shape=jax.ShapeDtypeStruct(parts.shape[1:], F32))(parts)


def _adam_small(g, w, m, v):
    def body(g_ref, w_ref, m_ref, v_ref, d_ref, mo_ref, vo_ref):
        d_ref[...], mo_ref[...], vo_ref[...] = _adamw(g_ref[...], w_ref[...], m_ref[...], v_ref[...])

    return _call(body, "adam_small", out_shape=[jax.ShapeDtypeStruct(g.shape, F32)] * 3)(g, w, m, v)


def _pack_rows(arrs):
    rows = []
    for a in arrs:
        flat = a.reshape(-1).astype(F32)
        pad = (-flat.shape[0]) % LANES
        rows.append(jnp.pad(flat, (0, pad)).reshape(-1, LANES))
    packed = jnp.concatenate(rows, axis=0)
    return jnp.pad(packed, ((0, (-packed.shape[0]) % 8), (0, 0)))


def _unpack_rows(packed, shapes):
    out, r = [], 0
    for shp in shapes:
        size = 1
        for s in shp:
            size *= s
        nrows = -(-size // LANES)
        out.append(packed[r:r + nrows].reshape(-1)[:size].reshape(shp))
        r += nrows
    return out


def _shard_to_slab(shard, d):
    glr = jnp.zeros((D_MODEL, GLA_RANK), shard.dtype)
    if d == GLR_DEV:
        glr = shard[:, GLR_LOCAL:GLR_LOCAL + GLA_RANK]
        shard = jnp.concatenate([shard[:, :GLR_LOCAL], shard[:, GLR_LOCAL + GLA_RANK:]], axis=1)
    return jnp.pad(shard, ((0, 0), (SLAB_SHIFT[d], SLAB_W - SLAB_SHIFT[d] - shard.shape[1]))), glr


def kernel(x, meta_tokens, norm_gain, w_in, w_gate_up, b_gate, ret_norm_gain, gla_norm_gain, w_branch_ret, w_branch_gla, w_out, final_norm_gain, loss_target, m_meta_tokens, m_norm_gain, m_w_in, m_w_gate_up, m_b_gate, m_ret_norm_gain, m_gla_norm_gain, m_w_branch_ret, m_w_branch_gla, m_w_out, m_final_norm_gain, v_meta_tokens, v_norm_gain, v_w_in, v_w_gate_up, v_b_gate, v_ret_norm_gain, v_gla_norm_gain, v_w_branch_ret, v_w_branch_gla, v_w_out, v_final_norm_gain):
    xi, yi, ci = _position()
    me = _index(xi, yi, ci)
    seq = x.shape[1]
    t_rows = seq + TILE
    in_shard = w_in.shape[2]
    gu_shard = w_gate_up.shape[2]
    meta_shard = meta_tokens.shape[1]
    ret_rows, gla_rows, out_rows = w_branch_ret.shape[1], w_branch_gla.shape[1], w_out.shape[1]

    assert in_shard == IN_SHARD
    slab_local, glr_local = lax.switch(me, [functools.partial(_shard_to_slab, d=d) for d in range(N_DEV)], w_in[0])
    small_local = jnp.concatenate([meta_tokens, jnp.pad(w_gate_up[0], ((0, 0), (0, LANES - gu_shard))),
                                   glr_local.reshape(-1, LANES)], axis=0)
    slabs, g_small = _all_gather([slab_local.astype(BF16), small_local], "all_gather_shards")
    n_small = N_META + GLA_RANK
    w_glr = jnp.pad(g_small[GLR_DEV, n_small:].reshape(D_MODEL, GLA_RANK), ((0, 0), (0, LANES - GLA_RANK))).astype(BF16)
    meta_full = jnp.transpose(g_small[:, :N_META, :], (1, 0, 2)).reshape(N_META, D_MODEL)
    wgu_full = jnp.transpose(g_small[:, N_META:n_small, :gu_shard], (1, 0, 2)).reshape(GLA_RANK, GLA_HEADS * GLA_K)
    wgu_pad = jnp.pad(wgu_full, ((0, LANES - GLA_RANK), (0, 0)))

    pos = jnp.arange(t_rows, dtype=F32) - float(PAD_ROWS)
    half = RET_QK // 2
    inv = ROPE_BASE ** (-jnp.arange(half, dtype=F32) / half)
    ang = pos[:, None] * inv[None, :]
    cos, sin = jnp.cos(ang), jnp.sin(ang)
    lg = jnp.log1p(-(2.0 ** (-5.0 - jnp.arange(RET_HEADS, dtype=F32))))

    head = jnp.concatenate([jnp.zeros((PAD_ROWS, D_MODEL), F32), meta_full], axis=0)
    ut, proj, glr = _inproj_tiles(head, x[0], norm_gain, slabs, w_glr)
    o_ret_raw, o_ret, ret_states, (g_br, g_bg, g_o) = _ret_fwd(
        proj, cos, sin, ret_norm_gain, lg, [w_branch_ret[0].astype(BF16), w_branch_gla[0].astype(BF16), w_out[0].astype(BF16)])
    w_br, w_bg, w_o = g_br.reshape(RET_W, D_MODEL), g_bg.reshape(GLA_W, D_MODEL), g_o.reshape(D_MODEL, D_MODEL)
    masks, cum_fwd, cum_bwd = _gla_tables()
    o_gla_raw, o_gla, gla_states, gla_scores_t = _gla_fwd(proj, glr, wgu_pad, b_gate, gla_norm_gain, masks, cum_fwd)
    (dh1, d_mr, d_mg, do_ret, do_gla, loss_part, d_gfinal, dw_br, dw_bg, dw_o) = _merge_fwd_bwd(
        o_ret, o_gla, proj, x[0], loss_target[0], final_norm_gain.reshape(1, D_MODEL), w_br, w_bg, w_o)

    d_rq, d_rk, d_rv, d_rg, d_gret = _ret_bwd(proj, cos, sin, ret_norm_gain, lg, o_ret_raw, do_ret, ret_states)
    d_gq, d_gk, d_gv, d_gg, dglr_parts, d_wgu, d_bgate, d_ggla = _gla_bwd(
        proj, glr, wgu_pad, b_gate, gla_norm_gain, o_gla_raw, do_gla, gla_states, gla_scores_t, masks, cum_fwd, cum_bwd)
    dseg = dict(rq=d_rq, rk=d_rk, rv=d_rv, rg=d_rg, gq=d_gq, gk=d_gk, gv=d_gv, gg=d_gg, mr=d_mr, mg=d_mg)
    row_sends = [dw_br.reshape(N_DEV, ret_rows, D_MODEL), dw_bg.reshape(N_DEV, gla_rows, D_MODEL),
                 dw_o.reshape(N_DEV, out_rows, D_MODEL)]
    dw_blocks, dw_glr, sib_in, sib_rows = _inproj_bwd_w(ut, dseg, dglr_parts, row_sends)
    core = ci.astype(jnp.int32).reshape(1)
    chip_partials = [_chip_partial_slab(dw_blocks, sib_in, core)] + [
        _chip_partial_rows(send, sib, core, "chip_partial_" + name)
        for send, sib, name in zip(row_sends, sib_rows, ("w_branch_ret", "w_branch_gla", "w_out"))]
    grad_x, d_head, d_gnorm, p_in, p_br, p_bg, p_o = _inproj_bwd_x(
        dseg, dglr_parts, head, x[0], dh1, norm_gain, slabs, w_glr, chip_partials)
    small_shapes = [(N_META, D_MODEL), (1, D_MODEL), (GLA_RANK, GLA_HEADS * GLA_K), (1, GLA_HEADS * GLA_K),
                    (1, RET_W), (1, GLA_W), (1, D_MODEL), (1, LANES), (D_MODEL, GLA_RANK)]
    small_part = _pack_rows([d_head[PAD_ROWS:], d_gnorm, d_wgu[:GLA_RANK], d_bgate, d_gret, d_ggla, d_gfinal, loss_part,
                             dw_glr[:, :GLA_RANK]])
    (p_small,) = _all_gather([small_part], "all_gather_small_partials")

    (g_meta_f, g_gnorm, g_wgu_f, g_bgate, g_gret, g_ggla, g_gfinal, loss_all,
     g_wglr) = _unpack_rows(_reduce_small(p_small), small_shapes)
    g_w_in, d_w_in, nm_w_in, nv_w_in = _reduce_adam_slab(
        p_in, jnp.pad(g_wglr, ((0, 0), (0, LANES - GLA_RANK))), w_in[0], m_w_in[0], v_w_in[0], me)
    rb = gla_rows
    g_w_br, d_w_br, nm_w_br, nv_w_br = _reduce_adam(p_br, w_branch_ret[0], m_w_branch_ret[0], v_w_branch_ret[0], "adam_w_branch_ret", rb)
    g_w_bg, d_w_bg, nm_w_bg, nv_w_bg = _reduce_adam(p_bg, w_branch_gla[0], m_w_branch_gla[0], v_w_branch_gla[0], "adam_w_branch_gla", rb)
    g_w_o, d_w_o, nm_w_o, nv_w_o = _reduce_adam(p_o, w_out[0], m_w_out[0], v_w_out[0], "adam_w_out", rb)
    g_meta = lax.dynamic_slice_in_dim(g_meta_f, me * meta_shard, meta_shard, axis=1)
    g_wgu = lax.dynamic_slice_in_dim(g_wgu_f, me * gu_shard, gu_shard, axis=1)
    s_g = [g_meta, g_gnorm, g_wgu, g_bgate, g_gret, g_ggla, g_gfinal]
    s_w = [meta_tokens, norm_gain, w_gate_up[0], b_gate, ret_norm_gain, gla_norm_gain, final_norm_gain]
    s_m = [m_meta_tokens, m_norm_gain, m_w_gate_up[0], m_b_gate, m_ret_norm_gain, m_gla_norm_gain, m_final_norm_gain]
    s_v = [v_meta_tokens, v_norm_gain, v_w_gate_up[0], v_b_gate, v_ret_norm_gain, v_gla_norm_gain, v_final_norm_gain]
    shapes = [a.shape for a in s_g]
    s_d, s_nm, s_nv = [_unpack_rows(p, shapes) for p in _adam_small(*[_pack_rows(l) for l in (s_g, s_w, s_m, s_v)])]

    loss = loss_all[0, 0]
    grad_x = grad_x[None]

    def order(meta, gnorm, win, wgu, bgate, gret, ggla, wbr, wbg, wo, gfin):
        return (meta, gnorm, win[None], wgu[None], bgate, gret, ggla, wbr[None], wbg[None], wo[None], gfin.reshape(final_norm_gain.shape))

    def small(l):
        return dict(meta=l[0], gnorm=l[1], wgu=l[2], bgate=l[3], gret=l[4], ggla=l[5], gfin=l[6])

    grads = order(win=g_w_in, wbr=g_w_br, wbg=g_w_bg, wo=g_w_o, **small(s_g))
    deltas = order(win=d_w_in, wbr=d_w_br, wbg=d_w_bg, wo=d_w_o, **small(s_d))
    new_m = order(win=nm_w_in, wbr=nm_w_br, wbg=nm_w_bg, wo=nm_w_o, **small(s_nm))
    new_v = order(win=nv_w_in, wbr=nv_w_br, wbg=nv_w_bg, wo=nv_w_o, **small(s_nv))
    return (loss, grad_x, *grads, *deltas, *new_m, *new_v)
```

```python
import functools

import jax
import jax.numpy as jnp
from jax import lax
from jax.experimental import pallas as pl
from jax.experimental.pallas import tpu as pltpu

F32 = jnp.float32
BF16 = jnp.bfloat16

D_MODEL = 1024
N_META = 16
TILE = 256
PAD_ROWS = TILE - N_META
RET_HEADS = 4
RET_QK = 256
RET_V = 512
RET_W = RET_HEADS * RET_V
GLA_HEADS = 4
GLA_K = 128
GLA_V = 256
GLA_W = GLA_HEADS * GLA_V
GLA_RANK = 16
GLA_TAU = 16.0
GLA_CHUNK = 16
ROPE_BASE = 10000.0
EPS = 1e-6
LANES = 128
N_DEV = 8
SEG_NAMES = ("rq", "rk", "rv", "rg", "gq", "gk", "gv", "gg", "mr", "mg")
SEG_W = (1024, 1024, 2048, 2048, 512, 512, 1024, 1024, 1024, 1024)
SEG_OFF = tuple(sum(SEG_W[:i]) for i in range(len(SEG_W)))
AL_COLS = sum(SEG_W)
IN_COLS = AL_COLS + GLA_RANK
GLR_OFF = sum(SEG_W[:8])
IN_SHARD = IN_COLS // N_DEV


def _aligned_col(c):
    assert c <= GLR_OFF or c >= GLR_OFF + GLA_RANK
    return c if c <= GLR_OFF else c - GLA_RANK


SLAB_BOUND = tuple(_aligned_col(IN_SHARD * d) for d in range(N_DEV + 1))
SLAB_BLK0 = tuple(b // LANES for b in SLAB_BOUND[:-1])
SLAB_SHIFT = tuple(b % LANES for b in SLAB_BOUND[:-1])
SLAB_BLOCKS = max(-(-SLAB_BOUND[d + 1] // LANES) - SLAB_BLK0[d] for d in range(N_DEV))
SLAB_W = SLAB_BLOCKS * LANES
GLR_DEV = GLR_OFF // IN_SHARD
GLR_LOCAL = GLR_OFF - GLR_DEV * IN_SHARD
assert all(SLAB_BLK0[d] + SLAB_BLOCKS <= AL_COLS // LANES for d in range(N_DEV))
VMEM_LIMIT = 58 * 1024 * 1024
ADAM_LR, ADAM_B1, ADAM_B2, ADAM_EPS, ADAM_WD, ADAM_STEP = 0.001, 0.9, 0.999, 1e-08, 0.01, 10
ANY = pl.BlockSpec(memory_space=pl.ANY)
MESH = pl.DeviceIdType.MESH


def _call(body, name, **kw):
    return pl.pallas_call(body, name=name, **kw)


def _params(sem=None):
    return pltpu.CompilerParams(dimension_semantics=sem, vmem_limit_bytes=VMEM_LIMIT)


def _mm(a, b):
    return jnp.dot(a, b, preferred_element_type=F32)


def _mm_nt(a, b):
    return lax.dot_general(a, b, (((1,), (1,)), ((), ())), preferred_element_type=F32)


def _mm_tn(a, b):
    return lax.dot_general(a, b, (((0,), (0,)), ((), ())), preferred_element_type=F32)


def _sigmoid(x):
    return 1.0 / (1.0 + jnp.exp(-x))


def _rope(t, cos, sin):
    half = t.shape[-1] // 2
    t1, t2 = t[:, :half], t[:, half:]
    return jnp.concatenate([t1 * cos - t2 * sin, t2 * cos + t1 * sin], axis=-1)


def _rope_bwd(g, cos, sin):
    half = g.shape[-1] // 2
    g1, g2 = g[:, :half], g[:, half:]
    return jnp.concatenate([g1 * cos + g2 * sin, g2 * cos - g1 * sin], axis=-1)


def _row_mean(x):
    return jnp.mean(x, axis=-1, keepdims=True)


def _col_sum(x):
    return jnp.sum(x, axis=0, keepdims=True)


def _tile_rows(head_ref, x_ref):
    return jnp.where(pl.program_id(0) == 0, head_ref[...], x_ref[...])


def _head_spec():
    return pl.BlockSpec((TILE, D_MODEL), lambda i: (0, 0))


def _x_spec():
    return pl.BlockSpec((TILE, D_MODEL), lambda i: (jnp.maximum(i - 1, 0), 0))


def _slab_plan():
    interior, shared = [], []
    for d in range(N_DEV):
        lo, hi = -(-SLAB_BOUND[d] // LANES), SLAB_BOUND[d + 1] // LANES
        interior.append((d, LANES * (lo - SLAB_BLK0[d]), LANES * lo, LANES * (hi - lo)))
        if d + 1 < N_DEV and SLAB_BOUND[d + 1] % LANES:
            shared.append((hi, d, hi - SLAB_BLK0[d]))
    return interior, shared


W_SCRATCH = lambda: [pltpu.VMEM((D_MODEL, AL_COLS), BF16), pltpu.VMEM((D_MODEL, LANES), BF16),
                     pltpu.VMEM((2 * (N_DEV - 1), D_MODEL, LANES), BF16), pltpu.SemaphoreType.DMA((3 * N_DEV,))]


def _load_weight(slabs_hbm, wg_hbm, w_vm, wg_vm, edge_vm, sem):
    interior, shared = _slab_plan()
    copies = [pltpu.make_async_copy(wg_hbm, wg_vm, sem.at[0])]
    for d, src, dst, width in interior:
        copies.append(pltpu.make_async_copy(slabs_hbm.at[d, :, pl.ds(src, width)], w_vm.at[:, pl.ds(dst, width)], sem.at[1 + d]))
    for n, (_, d, blk) in enumerate(shared):
        copies.append(pltpu.make_async_copy(slabs_hbm.at[d, :, pl.ds(LANES * blk, LANES)], edge_vm.at[2 * n], sem.at[1 + N_DEV + 2 * n]))
        copies.append(pltpu.make_async_copy(slabs_hbm.at[d + 1, :, pl.ds(0, LANES)], edge_vm.at[2 * n + 1], sem.at[2 + N_DEV + 2 * n]))
    for cp in copies:
        cp.start()
    for cp in copies:
        cp.wait()
    for n, (blk, _, _) in enumerate(shared):
        w_vm[:, LANES * blk:LANES * (blk + 1)] = edge_vm[2 * n] + edge_vm[2 * n + 1]


def _proj_specs(names, n_units, where):
    specs = []
    for name in names:
        s = SEG_NAMES.index(name)
        nblk = SEG_W[s] // n_units // LANES
        base = SEG_OFF[s] // LANES
        assert base % nblk == 0
        specs.append(pl.BlockSpec((nblk, TILE, LANES), lambda *g, base=base, nblk=nblk: (base // nblk + where(*g)[0], where(*g)[1], 0)))
    return specs


def _cols(ref, unit=0, n_units=1):
    n = ref.shape[0] // n_units
    return ref[unit * n] if n == 1 else jnp.concatenate([ref[unit * n + j] for j in range(n)], axis=1)


def _inproj_tiles(head, x, g_norm, slabs, w_glr):
    t_rows = x.shape[0] + TILE
    nt = t_rows // TILE
    n_blocks = AL_COLS // LANES

    def body(head_ref, x_ref, g_ref, slabs_hbm, wg_hbm, ut_ref, proj_ref, glr_ref, w_vm, wg_vm, edge_vm, sem):
        @pl.when(pl.program_id(0) == 0)
        def _():
            _load_weight(slabs_hbm, wg_hbm, w_vm, wg_vm, edge_vm, sem)

        x = _tile_rows(head_ref, x_ref)
        r = lax.rsqrt(_row_mean(x * x) + EPS)
        u32 = (x * r * g_ref[...]).astype(BF16).astype(F32)
        u = u32.astype(BF16)
        ut_ref[...] = u32.T.astype(BF16)
        for s in range(len(SEG_W)):
            res = _mm(u, w_vm[:, SEG_OFF[s]:SEG_OFF[s] + SEG_W[s]]).astype(BF16)
            for j in range(SEG_W[s] // LANES):
                proj_ref[SEG_OFF[s] // LANES + j] = res[:, j * LANES:(j + 1) * LANES]
        glr_ref[...] = _mm(u, wg_vm[...])

    return _call(
        body, "inproj_fwd_tiles", grid=(nt,),
        out_shape=[jax.ShapeDtypeStruct((nt, D_MODEL, TILE), BF16), jax.ShapeDtypeStruct((n_blocks, t_rows, LANES), BF16),
                   jax.ShapeDtypeStruct((t_rows, LANES), F32)],
        in_specs=[_head_spec(), _x_spec(), pl.BlockSpec((1, D_MODEL), lambda i: (0, 0)), ANY, ANY],
        out_specs=[pl.BlockSpec((None, D_MODEL, TILE), lambda i: (i, 0, 0)), pl.BlockSpec((n_blocks, TILE, LANES), lambda i: (0, i, 0)),
                   pl.BlockSpec((TILE, LANES), lambda i: (i, 0))],
        scratch_shapes=W_SCRATCH(), compiler_params=_params(("arbitrary",)),
    )(head, x, g_norm, slabs, w_glr)


def _ret_decay(lgh):
    i = lax.broadcasted_iota(jnp.int32, (TILE, TILE), 0)
    j = lax.broadcasted_iota(jnp.int32, (TILE, TILE), 1)
    rel = (i - j).astype(F32)
    return jnp.where(rel >= 0, jnp.exp(jnp.maximum(rel, 0.0) * lgh), 0.0)


def _ret_vectors(lgh):
    idx = lax.broadcasted_iota(jnp.int32, (TILE, 1), 0).astype(F32)
    xi = jnp.exp((idx + 1.0) * lgh)
    zeta = jnp.exp((TILE - 1.0 - idx) * lgh)
    gc = jnp.exp(jnp.full((1, 1), float(TILE), F32) * lgh)
    return xi, zeta, gc


def _ret_fwd(proj, cos, sin, gain, lg, row_shards):
    t_rows = cos.shape[0]
    nt = t_rows // TILE
    ns = len(row_shards)

    def body(lg_ref, q_ref, k_ref, v_ref, g_ref, cos_ref, sin_ref, gain_ref, *rest):
        shard_refs, (oraw_ref, oret_ref, st_ref), gathered = rest[:ns], rest[ns:ns + 3], rest[ns + 3:2 * ns + 3]
        s_acc, dm = rest[2 * ns + 3:2 * ns + 5]
        gather = _Exchange(shard_refs, gathered, rest[2 * ns + 5:], among_chips=False)
        t = pl.program_id(0)

        @pl.when(t == 0)
        def _():
            gather.start()
            s_acc[...] = jnp.zeros_like(s_acc)
            for h in range(RET_HEADS):
                dm[h] = _ret_decay(lg_ref[h])

        @pl.when(t == nt - 1)
        def _():
            gather.finish()

        cos_t, sin_t = cos_ref[...], sin_ref[...]
        for h in range(RET_HEADS):
            lgh = lg_ref[h]
            q = _rope(_cols(q_ref, h, RET_HEADS).astype(F32), cos_t, sin_t)
            k = _rope(_cols(k_ref, h, RET_HEADS).astype(F32), cos_t, sin_t) * (RET_QK ** -0.5)
            xi, zeta, gc = _ret_vectors(lgh)
            v = _cols(v_ref, h, RET_HEADS)
            s_in = s_acc[h]
            p = (_mm_nt(q.astype(BF16), k.astype(BF16)) * dm[h]).astype(BF16)
            o = _mm(p, v) + _mm((q * xi).astype(BF16), s_in.astype(BF16))
            st_ref[h] = s_in.astype(BF16)
            s_acc[h] = s_in * gc + _mm_tn((k * zeta).astype(BF16), v)
            cols = slice(h * RET_V, (h + 1) * RET_V)
            oraw_ref[:, cols] = o
            oc = o - _row_mean(o)
            n = oc * lax.rsqrt(_row_mean(oc * oc) + EPS) * gain_ref[:, cols]
            g = _cols(g_ref, h, RET_HEADS).astype(F32)
            oret_ref[:, cols] = (n * g * _sigmoid(g)).astype(BF16)

    row = lambda w: pl.BlockSpec((TILE, w), lambda t: (t, 0))
    outs = _call(
        body, "ret_fwd", grid=(nt,),
        out_shape=[jax.ShapeDtypeStruct((t_rows, RET_W), F32), jax.ShapeDtypeStruct((t_rows, RET_W), BF16),
                   jax.ShapeDtypeStruct((RET_HEADS, nt, RET_QK, RET_V), BF16)]
                  + [jax.ShapeDtypeStruct((N_DEV, *a.shape), a.dtype) for a in row_shards],
        in_specs=[pl.BlockSpec(memory_space=pltpu.SMEM)] + _proj_specs(("rq", "rk", "rv", "rg"), 1, lambda t: (0, t)) + [row(LANES), row(LANES),
                  pl.BlockSpec((1, RET_W), lambda t: (0, 0))] + [ANY] * ns,
        out_specs=[row(RET_W), row(RET_W), pl.BlockSpec((RET_HEADS, None, RET_QK, RET_V), lambda t: (0, t, 0, 0))] + [ANY] * ns,
        scratch_shapes=[pltpu.VMEM((RET_HEADS, RET_QK, RET_V), F32), pltpu.VMEM((RET_HEADS, TILE, TILE), F32)] + _exchange_sems(ns, N_DEV),
        compiler_params=_params(("arbitrary",)),
    )(lg, proj, proj, proj, proj, cos, sin, gain, *row_shards)
    return outs[0], outs[1], outs[2], outs[3:]


def _ret_bwd(proj, cos, sin, gain, lg, o_raw, do_ret, states):
    t_rows = cos.shape[0]
    nt = t_rows // TILE

    def body(lg_ref, q_ref, k_ref, v_ref, g_ref, cos_ref, sin_ref, gain_ref, oraw_ref, do_ref, st_ref,
             dq_ref, dk_ref, dv_ref, dg_ref, dgain_ref, e_acc, dm):
        @pl.when(pl.program_id(0) == 0)
        def _():
            e_acc[...] = jnp.zeros_like(e_acc)
            for h in range(RET_HEADS):
                dm[h] = _ret_decay(lg_ref[h])
            dgain_ref[...] = jnp.zeros_like(dgain_ref)

        cos_t, sin_t = cos_ref[...], sin_ref[...]
        for h in range(RET_HEADS):
            lgh = lg_ref[h]
            cols = slice(h * RET_V, (h + 1) * RET_V)
            qcols = slice(h * RET_QK, (h + 1) * RET_QK)
            q = _rope(_cols(q_ref, h, RET_HEADS).astype(F32), cos_t, sin_t)
            k = _rope(_cols(k_ref, h, RET_HEADS).astype(F32), cos_t, sin_t) * (RET_QK ** -0.5)
            xi, zeta, gc = _ret_vectors(lgh)
            v = _cols(v_ref, h, RET_HEADS)
            g = _cols(g_ref, h, RET_HEADS).astype(F32)
            o = oraw_ref[:, cols]
            do = do_ref[:, cols].astype(F32)
            oc = o - _row_mean(o)
            rstd = lax.rsqrt(_row_mean(oc * oc) + EPS)
            xh = oc * rstd
            gain_t = gain_ref[:, cols]
            sg = _sigmoid(g)
            dn = do * (g * sg)
            dg_ref[:, cols] = (do * (xh * gain_t) * (sg * (1.0 + g * (1.0 - sg)))).astype(BF16)
            dgain_ref[:, cols] += _col_sum(dn * xh)
            dxh = dn * gain_t
            dob = (rstd * (dxh - _row_mean(dxh) - xh * _row_mean(dxh * xh))).astype(BF16)
            dmat = dm[h]
            qb, kb = q.astype(BF16), k.astype(BF16)
            p = (_mm_nt(qb, kb) * dmat).astype(BF16)
            dp = (_mm_nt(dob, v) * dmat).astype(BF16)
            s_in = st_ref[h]
            e_in = e_acc[h]
            e_b = e_in.astype(BF16)
            dq = _mm(dp, kb) + _mm_nt(dob, s_in) * xi
            dk = _mm_tn(dp, qb) + _mm_nt(v, e_b) * zeta
            dv_ref[:, cols] = (_mm_tn(p, dob) + _mm((k * zeta).astype(BF16), e_b)).astype(BF16)
            e_acc[h] = e_in * gc + _mm_tn((q * xi).astype(BF16), dob)
            dq_ref[:, qcols] = _rope_bwd(dq, cos_t, sin_t).astype(BF16)
            dk_ref[:, qcols] = (_rope_bwd(dk, cos_t, sin_t) * (RET_QK ** -0.5)).astype(BF16)

    row = lambda w: pl.BlockSpec((TILE, w), lambda j: (nt - 1 - j, 0))
    vec = pl.BlockSpec((1, RET_W), lambda j: (0, 0))
    return _call(
        body, "ret_bwd", grid=(nt,),
        out_shape=[jax.ShapeDtypeStruct((t_rows, RET_HEADS * RET_QK), BF16), jax.ShapeDtypeStruct((t_rows, RET_HEADS * RET_QK), BF16),
                   jax.ShapeDtypeStruct((t_rows, RET_W), BF16), jax.ShapeDtypeStruct((t_rows, RET_W), BF16),
                   jax.ShapeDtypeStruct((1, RET_W), F32)],
        in_specs=[pl.BlockSpec(memory_space=pltpu.SMEM)] + _proj_specs(("rq", "rk", "rv", "rg"), 1, lambda j: (0, nt - 1 - j)) + [row(LANES), row(LANES), vec,
                  row(RET_W), row(RET_W), pl.BlockSpec((RET_HEADS, None, RET_QK, RET_V), lambda j: (0, nt - 1 - j, 0, 0))],
        out_specs=[row(RET_HEADS * RET_QK), row(RET_HEADS * RET_QK), row(RET_W), row(RET_W), vec],
        scratch_shapes=[pltpu.VMEM((RET_HEADS, RET_QK, RET_V), F32), pltpu.VMEM((RET_HEADS, TILE, TILE), F32)],
        compiler_params=_params(("arbitrary",)),
    )(lg, proj, proj, proj, proj, cos, sin, gain, o_raw, do_ret, states)


GLA_LEVELS = (32, 64, 128, 256)
N_TERMS = 1 + len(GLA_LEVELS)


def _gla_tables():
    p = jnp.arange(TILE)[:, None]
    r = jnp.arange(TILE)[None, :]
    masks = [(p // GLA_CHUNK == r // GLA_CHUNK) & (r <= p)]
    for blk in GLA_LEVELS:
        masks.append((p // blk == r // blk) & (p % blk >= blk // 2) & (r % blk < blk // 2))
    masks = jnp.stack(masks + [m.T for m in masks]).astype(F32)
    cum_fwd = jnp.concatenate([r <= p, masks[0] > 0], axis=0).astype(BF16)
    cum_bwd = jnp.concatenate([r >= p, masks[N_TERMS] > 0], axis=1).astype(BF16)
    return masks, cum_fwd, cum_bwd


def _split3(x):
    hi = x.astype(BF16)
    rest = x - hi.astype(F32)
    mid = rest.astype(BF16)
    lo = (rest - mid.astype(F32)).astype(BF16)
    return jnp.concatenate([hi, mid, lo], axis=1)


def _join3(y):
    w = y.shape[1] // 3
    return (y[:, 2 * w:] + y[:, w:2 * w]) + y[:, :w]


def _gla_decays(glr_ref, wgu_ref, b_ref, cum_ref):
    z = _mm(glr_ref[...].astype(BF16), wgu_ref[...].astype(BF16)) + b_ref[...]
    la = (jnp.minimum(z, 0.0) - jnp.log(1.0 + jnp.exp(-jnp.abs(z)))) / GLA_TAU
    width = la.shape[1]
    hi = la.astype(BF16)
    rest = la - hi.astype(F32)
    mid = rest.astype(BF16)
    lo = (rest - mid.astype(F32)).astype(BF16)
    y = _mm(cum_ref[...], jnp.concatenate([hi, mid, lo], axis=1))
    gb = (y[:, 2 * width:] + y[:, width:2 * width]) + y[:, :width]
    return z, gb[:TILE], gb[TILE:]


def _gla_prep(h, q_ref, k_ref, g_all, b_all, g_scr, ref_scr):
    cols = slice(h * GLA_K, (h + 1) * GLA_K)
    g, b = g_all[:, cols], b_all[:, cols]
    g_scr[h] = g
    factors = [(jnp.exp(b), jnp.exp(-b))]
    for lvl, blk in enumerate(GLA_LEVELS):
        for n in range(TILE // blk):
            ref_scr[h, lvl, n * blk:(n + 1) * blk, :] = jnp.broadcast_to(g_scr[h, pl.ds(n * blk + blk // 2 - 1, 1), :], (blk, GLA_K))
        x = g - ref_scr[h, lvl]
        factors.append((jnp.exp(jnp.minimum(x, 0.0)), jnp.exp(jnp.minimum(-x, 0.0))))
    g_last = g_scr[h, pl.ds(TILE - 1, 1), :]
    q = _cols(q_ref, h, GLA_HEADS).astype(F32) * (GLA_K ** -0.5)
    k = _cols(k_ref, h, GLA_HEADS).astype(F32)
    return q, k, factors, jnp.exp(g), jnp.exp(g_last), jnp.exp(g_last - g)


def _gla_scores(q, k, factors, m_ref):
    a = jnp.zeros((TILE, TILE), F32)
    for l, (fq, fk) in enumerate(factors):
        s = _mm_nt((q * fq).astype(BF16), (k * fk).astype(BF16))
        a = jnp.where(m_ref[l] > 0.0, s, a)
    return a


def _gla_fwd(proj, glr, wgu_pad, b_gate, gain, masks, cum_fwd):
    t_rows = glr.shape[0]
    nt = t_rows // TILE

    def body(q_ref, k_ref, v_ref, g_ref, glr_ref, wgu_ref, b_ref, gain_ref, m_ref, cum_ref, oraw_ref, ogla_ref, st_ref, at_ref,
             s_acc, g_scr, ref_scr):
        @pl.when(pl.program_id(0) == 0)
        def _():
            s_acc[...] = jnp.zeros_like(s_acc)

        _, g_all, b_all = _gla_decays(glr_ref, wgu_ref, b_ref, cum_ref)
        for h in range(GLA_HEADS):
            q, k, factors, e_g, e_last, e_end = _gla_prep(h, q_ref, k_ref, g_all, b_all, g_scr, ref_scr)
            v = _cols(v_ref, h, GLA_HEADS)
            st = s_acc[h]
            st_ref[h] = st
            a = _gla_scores(q, k, factors, m_ref)
            at_ref[h] = a.T.astype(BF16)
            o = _mm(a.astype(BF16), v) + _mm_nt((q * e_g).astype(BF16), st.astype(BF16))
            s_acc[h] = st * e_last + _mm(v.astype(F32).T.astype(BF16), (k * e_end).astype(BF16))
            cols = slice(h * GLA_V, (h + 1) * GLA_V)
            oraw_ref[:, cols] = o
            n = o * lax.rsqrt(_row_mean(o * o) + EPS) * gain_ref[:, cols]
            g = _cols(g_ref, h, GLA_HEADS).astype(F32)
            ogla_ref[:, cols] = (n * g * _sigmoid(g)).astype(BF16)

    row = lambda w: pl.BlockSpec((TILE, w), lambda t: (t, 0))
    whole = lambda *shape: pl.BlockSpec(shape, lambda t: (0,) * len(shape))
    return _call(
        body, "gla_fwd", grid=(nt,),
        out_shape=[jax.ShapeDtypeStruct((t_rows, GLA_W), F32), jax.ShapeDtypeStruct((t_rows, GLA_W), BF16),
                   jax.ShapeDtypeStruct((GLA_HEADS, nt, GLA_V, GLA_K), F32), jax.ShapeDtypeStruct((GLA_HEADS, t_rows, TILE), BF16)],
        in_specs=_proj_specs(("gq", "gk", "gv", "gg"), 1, lambda t: (0, t)) + [row(LANES), whole(LANES, GLA_HEADS * GLA_K),
                  whole(1, GLA_HEADS * GLA_K), whole(1, GLA_W), whole(N_TERMS, TILE, TILE), whole(2 * TILE, TILE)],
        out_specs=[row(GLA_W), row(GLA_W), pl.BlockSpec((GLA_HEADS, None, GLA_V, GLA_K), lambda t: (0, t, 0, 0)),
                   pl.BlockSpec((GLA_HEADS, TILE, TILE), lambda t: (0, t, 0))],
        scratch_shapes=[pltpu.VMEM((GLA_HEADS, GLA_V, GLA_K), F32), pltpu.VMEM((GLA_HEADS, TILE, GLA_K), F32),
                        pltpu.VMEM((GLA_HEADS, len(GLA_LEVELS), TILE, GLA_K), F32)],
        compiler_params=_params(("arbitrary",)),
    )(proj, proj, proj, proj, glr, wgu_pad, b_gate, gain, masks, cum_fwd)


def _gla_bwd(proj, glr, wgu_pad, b_gate, gain, o_raw, do_gla, states, a_t, masks, cum_fwd, cum_bwd):
    t_rows = glr.shape[0]
    nt = t_rows // TILE

    def body(q_ref, k_ref, v_ref, g_ref, glr_ref, wgu_ref, b_ref, gain_ref, m_ref, cum_ref, cumb_ref, oraw_ref, do_ref, st_ref, at_ref,
             dq_ref, dk_ref, dv_ref, dg_ref, dglr_ref, dwgu_ref, dbg_ref, dgain_ref, d_acc, g_scr, ref_scr, dref_scr):
        @pl.when(pl.program_id(0) == 0)
        def _():
            d_acc[...] = jnp.zeros_like(d_acc)
            dwgu_ref[...] = jnp.zeros_like(dwgu_ref)
            dbg_ref[...] = jnp.zeros_like(dbg_ref)
            dgain_ref[...] = jnp.zeros_like(dgain_ref)

        z_all, g_all, b_all = _gla_decays(glr_ref, wgu_ref, b_ref, cum_ref)
        dla_parts = []
        for h in range(GLA_HEADS):
            q, k, factors, e_g, e_last, e_end = _gla_prep(h, q_ref, k_ref, g_all, b_all, g_scr, ref_scr)
            v = _cols(v_ref, h, GLA_HEADS)
            cols = slice(h * GLA_V, (h + 1) * GLA_V)
            kcols = slice(h * GLA_K, (h + 1) * GLA_K)
            o = oraw_ref[:, cols]
            do = do_ref[:, cols].astype(F32)
            g = _cols(g_ref, h, GLA_HEADS).astype(F32)
            rinv = lax.rsqrt(_row_mean(o * o) + EPS)
            nh = o * rinv
            gain_t = gain_ref[:, cols]
            sg = _sigmoid(g)
            dn = do * (g * sg)
            dg_ref[:, cols] = (do * (nh * gain_t) * (sg * (1.0 + g * (1.0 - sg)))).astype(BF16)
            dgain_ref[:, cols] += _col_sum(dn * nh)
            dnh = dn * gain_t
            dor = rinv * (dnh - nh * _row_mean(dnh * nh))
            dob = dor.astype(BF16)
            a_t = at_ref[h]
            da = _mm_nt(dob, v).astype(BF16)
            da_t = _mm_nt(v, dob).astype(BF16)
            st_in = st_ref[h]
            d_out = d_acc[h]
            d_out_b = d_out.astype(BF16)
            qg, kg = q * e_g, k * e_end
            dqg = _mm(dob, st_in.astype(BF16))
            dkg = _mm(v, d_out_b)
            dv_ref[:, cols] = (_mm(a_t, dob) + _mm_nt(kg.astype(BF16), d_out_b)).astype(BF16)
            d_acc[h] = d_out * e_last + _mm(dor.T.astype(BF16), qg.astype(BF16))
            dq = dqg * e_g
            dk = dkg * e_end
            dkg_kg = dkg * kg
            dg_cum = dqg * qg - dkg_kg
            db = None
            for l, (fq, fk) in enumerate(factors):
                qt, kt = q * fq, k * fk
                dqt = _mm(da * m_ref[l], kt.astype(BF16))
                dkt = _mm(da_t * m_ref[N_TERMS + l], qt.astype(BF16))
                dq = dq + dqt * fq
                dk = dk + dkt * fk
                diff = dqt * qt - dkt * kt
                if l == 0:
                    db = diff
                else:
                    dg_cum = dg_cum + diff
                    dref_scr[h, l - 1] = diff
            dq_ref[:, kcols] = (dq * (GLA_K ** -0.5)).astype(BF16)
            dk_ref[:, kcols] = dk.astype(BF16)
            g_scr[h] = dg_cum
            g_scr[h, pl.ds(TILE - 1, 1), :] += e_last * _col_sum(d_out * st_in) + _col_sum(dkg_kg)
            for lvl, blk in enumerate(GLA_LEVELS):
                for n in range(TILE // blk):
                    g_scr[h, pl.ds(n * blk + blk // 2 - 1, 1), :] -= _col_sum(dref_scr[h, lvl, n * blk:(n + 1) * blk, :])
            dla_parts.append(_join3(_mm(cumb_ref[...], jnp.concatenate([_split3(g_scr[h]), _split3(db)], axis=0))))
        dz = jnp.concatenate(dla_parts, axis=1) * (1.0 / GLA_TAU) * _sigmoid(-z_all)
        dzb = dz.astype(BF16)
        wgu_b = wgu_ref[...].astype(BF16)
        for h in range(GLA_HEADS):
            kcols = slice(h * GLA_K, (h + 1) * GLA_K)
            dglr_ref[h] = _mm_nt(dzb[:, kcols], wgu_b[:, kcols]).astype(BF16)
        dwgu_ref[...] += _mm(glr_ref[...].T.astype(BF16), dzb)
        dbg_ref[...] += _col_sum(dz)

    row = lambda w: pl.BlockSpec((TILE, w), lambda j: (nt - 1 - j, 0))
    whole = lambda *shape: pl.BlockSpec(shape, lambda j: (0,) * len(shape))
    return _call(
        body, "gla_bwd", grid=(nt,),
        out_shape=[jax.ShapeDtypeStruct((t_rows, GLA_HEADS * GLA_K), BF16), jax.ShapeDtypeStruct((t_rows, GLA_HEADS * GLA_K), BF16),
                   jax.ShapeDtypeStruct((t_rows, GLA_W), BF16), jax.ShapeDtypeStruct((t_rows, GLA_W), BF16),
                   jax.ShapeDtypeStruct((GLA_HEADS, t_rows, LANES), BF16), jax.ShapeDtypeStruct((LANES, GLA_HEADS * GLA_K), F32),
                   jax.ShapeDtypeStruct((1, GLA_HEADS * GLA_K), F32), jax.ShapeDtypeStruct((1, GLA_W), F32)],
        in_specs=_proj_specs(("gq", "gk", "gv", "gg"), 1, lambda j: (0, nt - 1 - j)) + [row(LANES),
                  whole(LANES, GLA_HEADS * GLA_K), whole(1, GLA_HEADS * GLA_K), whole(1, GLA_W),
                  whole(2 * N_TERMS, TILE, TILE), whole(2 * TILE, TILE), whole(TILE, 2 * TILE), row(GLA_W), row(GLA_W),
                  pl.BlockSpec((GLA_HEADS, None, GLA_V, GLA_K), lambda j: (0, nt - 1 - j, 0, 0)),
                  pl.BlockSpec((GLA_HEADS, TILE, TILE), lambda j: (0, nt - 1 - j, 0))],
        out_specs=[row(GLA_HEADS * GLA_K), row(GLA_HEADS * GLA_K), row(GLA_W), row(GLA_W),
                   pl.BlockSpec((GLA_HEADS, TILE, LANES), lambda j: (0, nt - 1 - j, 0)), whole(LANES, GLA_HEADS * GLA_K),
                   whole(1, GLA_HEADS * GLA_K), whole(1, GLA_W)],
        scratch_shapes=[pltpu.VMEM((GLA_HEADS, GLA_V, GLA_K), F32), pltpu.VMEM((GLA_HEADS, TILE, GLA_K), F32),
                        pltpu.VMEM((GLA_HEADS, len(GLA_LEVELS), TILE, GLA_K), F32),
                        pltpu.VMEM((GLA_HEADS, len(GLA_LEVELS), TILE, GLA_K), F32)],
        compiler_params=_params(("arbitrary",)),
    )(proj, proj, proj, proj, glr, wgu_pad, b_gate, gain, masks.astype(BF16), cum_fwd, cum_bwd, o_raw, do_gla, states, a_t)


def _merge_fwd_bwd(o_ret, o_gla, proj, x, target, g_final, w_br, w_bg, w_out):
    t_rows = x.shape[0] + TILE
    nt = t_rows // TILE

    def body(oret_ref, ogla_ref, mr_ref, mg_ref, h0_ref, tgt_ref, gf_ref, wbr_hbm, wbg_hbm, wout_hbm,
             dh1_ref, dmr_ref, dmg_ref, doret_ref, dogla_ref, loss_ref, dgf_ref, dwbr_hbm, dwbg_hbm, dwout_hbm,
             wbr, wbg, wout, abr, abg, aout, sem):
        i = pl.program_id(0)

        @pl.when(i == 0)
        def _():
            cps = [pltpu.make_async_copy(s, d, sem.at[n]) for n, (s, d) in enumerate(((wbr_hbm, wbr), (wbg_hbm, wbg), (wout_hbm, wout)))]
            for cp in cps:
                cp.start()
            abr[...] = jnp.zeros_like(abr)
            abg[...] = jnp.zeros_like(abg)
            aout[...] = jnp.zeros_like(aout)
            loss_ref[...] = jnp.zeros_like(loss_ref)
            dgf_ref[...] = jnp.zeros_like(dgf_ref)
            for cp in cps:
                cp.wait()
            dh1_ref[...] = jnp.zeros_like(dh1_ref)
            dmr_ref[...] = jnp.zeros_like(dmr_ref)
            dmg_ref[...] = jnp.zeros_like(dmg_ref)
            doret_ref[...] = jnp.zeros_like(doret_ref)
            dogla_ref[...] = jnp.zeros_like(dogla_ref)

        @pl.when(i > 0)
        def _():
            oret, ogla = oret_ref[...], ogla_ref[...]
            br, bg = _mm(oret, wbr[...]), _mm(ogla, wbg[...])
            sr, sg = _sigmoid(_cols(mr_ref).astype(F32)), _sigmoid(_cols(mg_ref).astype(F32))
            mb = (sr * br + sg * bg).astype(BF16)
            h1 = h0_ref[...] + _mm(mb, wout[...])
            r2 = lax.rsqrt(_row_mean(h1 * h1) + EPS)
            hn = h1 * r2
            gf = gf_ref[...]
            diff = hn * gf - tgt_ref[...]
            loss_ref[...] += 0.5 * jnp.sum(_row_mean(diff * diff))
            dy = diff * (1.0 / D_MODEL)
            dgf_ref[...] += _col_sum(dy * hn)
            dyg = dy * gf
            dh1 = r2 * (dyg - hn * _row_mean(dyg * hn))
            dh1_ref[...] = dh1
            dh1b = dh1.astype(BF16)
            dm = _mm_nt(dh1b, wout[...])
            aout[...] += _mm_tn(mb, dh1b)
            dbr = (dm * sr).astype(BF16)
            dbg = (dm * sg).astype(BF16)
            dmr_ref[...] = (dm * br * sr * (1.0 - sr)).astype(BF16)
            dmg_ref[...] = (dm * bg * sg * (1.0 - sg)).astype(BF16)
            doret_ref[...] = _mm_nt(dbr, wbr[...]).astype(BF16)
            dogla_ref[...] = _mm_nt(dbg, wbg[...]).astype(BF16)
            abr[...] += _mm_tn(oret, dbr)
            abg[...] += _mm_tn(ogla, dbg)

        @pl.when(i == nt - 1)
        def _():
            wbr[...] = abr[...].astype(BF16)
            wbg[...] = abg[...].astype(BF16)
            wout[...] = aout[...].astype(BF16)
            pltpu.sync_copy(wbr, dwbr_hbm)
            pltpu.sync_copy(wbg, dwbg_hbm)
            pltpu.sync_copy(wout, dwout_hbm)

    row = lambda w: pl.BlockSpec((TILE, w), lambda i: (i, 0))
    one = lambda w: pl.BlockSpec((1, w), lambda i: (0, 0))
    return _call(
        body, "merge_fwd_bwd", grid=(nt,),
        out_shape=[jax.ShapeDtypeStruct((t_rows, D_MODEL), F32), jax.ShapeDtypeStruct((t_rows, D_MODEL), BF16),
                   jax.ShapeDtypeStruct((t_rows, D_MODEL), BF16), jax.ShapeDtypeStruct((t_rows, RET_W), BF16),
                   jax.ShapeDtypeStruct((t_rows, GLA_W), BF16), jax.ShapeDtypeStruct((1, LANES), F32),
                   jax.ShapeDtypeStruct((1, D_MODEL), F32), jax.ShapeDtypeStruct((RET_W, D_MODEL), BF16),
                   jax.ShapeDtypeStruct((GLA_W, D_MODEL), BF16), jax.ShapeDtypeStruct((D_MODEL, D_MODEL), BF16)],
        in_specs=[row(RET_W), row(GLA_W)] + _proj_specs(("mr", "mg"), 1, lambda i: (0, i)) + [_x_spec(), _x_spec(), one(D_MODEL), ANY, ANY, ANY],
        out_specs=[row(D_MODEL), row(D_MODEL), row(D_MODEL), row(RET_W), row(GLA_W), one(LANES), one(D_MODEL), ANY, ANY, ANY],
        scratch_shapes=[pltpu.VMEM((RET_W, D_MODEL), BF16), pltpu.VMEM((GLA_W, D_MODEL), BF16), pltpu.VMEM((D_MODEL, D_MODEL), BF16),
                        pltpu.VMEM((RET_W, D_MODEL), F32), pltpu.VMEM((GLA_W, D_MODEL), F32), pltpu.VMEM((D_MODEL, D_MODEL), F32),
                        pltpu.SemaphoreType.DMA((3,))],
        compiler_params=_params(("arbitrary",)),
    )(o_ret, o_gla, proj, proj, x, target, g_final, w_br, w_bg, w_out)


def _inproj_bwd_x(dseg, dglr, head, x, dh1, g_norm, slabs, w_glr, chip_partials):
    t_rows = x.shape[0] + TILE
    nt = t_rows // TILE
    ne = len(chip_partials)

    def body(*refs):
        d_refs = refs[:10]
        dglr_ref, head_ref, x_ref, dh1_ref, g_ref, slabs_hbm, wg_hbm = refs[10:17]
        part_refs = refs[17:17 + ne]
        dx_ref, dhead_ref, dgn_ref = refs[17 + ne:20 + ne]
        landed = refs[20 + ne:20 + 2 * ne]
        w_vm, wg_vm, edge_vm, sem = refs[20 + 2 * ne:24 + 2 * ne]
        exchange = _Exchange(part_refs, landed, refs[24 + 2 * ne:], among_chips=True)

        @pl.when(pl.program_id(0) == 0)
        def _():
            exchange.start()
            dgn_ref[...] = jnp.zeros_like(dgn_ref)
            _load_weight(slabs_hbm, wg_hbm, w_vm, wg_vm, edge_vm, sem)

        @pl.when(pl.program_id(0) == nt - 1)
        def _():
            exchange.finish()

        dglr = dglr_ref[0].astype(F32)
        for h in range(1, GLA_HEADS):
            dglr = dglr + dglr_ref[h].astype(F32)
        du = _mm_nt(dglr.astype(BF16), wg_vm[...])
        for s, d_ref in enumerate(d_refs):
            du = du + _mm_nt(d_ref[...], w_vm[:, SEG_OFF[s]:SEG_OFF[s] + SEG_W[s]])
        x = _tile_rows(head_ref, x_ref)
        r = lax.rsqrt(_row_mean(x * x) + EPS)
        hn = x * r
        dgn_ref[...] += _col_sum(du * hn)
        dug = du * g_ref[...]
        dh0 = dh1_ref[...] + r * (dug - hn * _row_mean(dug * hn))
        dx_ref[...] = dh0

        @pl.when(pl.program_id(0) == 0)
        def _():
            dhead_ref[...] = dh0

    row = lambda w: pl.BlockSpec((TILE, w), lambda i: (i, 0))
    one = pl.BlockSpec((1, D_MODEL), lambda i: (0, 0))
    return _call(
        body, "inproj_bwd_x", grid=(nt,),
        out_shape=[jax.ShapeDtypeStruct((t_rows - TILE, D_MODEL), F32), jax.ShapeDtypeStruct((TILE, D_MODEL), F32),
                   jax.ShapeDtypeStruct((1, D_MODEL), F32)] + [jax.ShapeDtypeStruct(a.shape, a.dtype) for a in chip_partials],
        in_specs=[row(w) for w in SEG_W] + [pl.BlockSpec((GLA_HEADS, TILE, LANES), lambda i: (0, i, 0)),
                                            _head_spec(), _x_spec(), row(D_MODEL), one, ANY, ANY] + [ANY] * ne,
        out_specs=[_x_spec(), _head_spec(), one] + [ANY] * ne,
        scratch_shapes=W_SCRATCH() + _exchange_sems(ne, N_CHIP),
        compiler_params=_params(("arbitrary",)),
    )(*[dseg[n] for n in SEG_NAMES], dglr, head, x, dh1, g_norm, slabs, w_glr, *chip_partials)


W_TILE = 512


def _inproj_bwd_w(ut, dseg, dglr, row_sends):
    nt = ut.shape[0]
    t_rows = nt * TILE
    kc = 3 if nt % 3 == 0 else 1
    tiles = [(s, c) for s in range(len(SEG_W)) for c in range(0, SEG_W[s], W_TILE)]
    bpt = W_TILE // LANES
    nr = len(row_sends)
    n = 1 + nr
    last_tile = [(SLAB_BLK0[d] + SLAB_BLOCKS - 1) // bpt for d in range(N_DEV)]

    def body(ut_hbm, *refs):
        d_refs, dglr_hbm, row_refs = refs[:10], refs[10], refs[11:11 + nr]
        out_hbm, oglr_ref, sib = refs[11 + nr], refs[12 + nr], refs[13 + nr:13 + nr + n]
        ut_vm, dbuf, obuf, acc, gbuf, sem, send_sems, recv_sems = refs[13 + nr + n:]
        x, y, core = _position()

        def handover(d, k, landed=False):
            q = d // 2
            src = out_hbm.at[pl.ds(SLAB_BLK0[d], SLAB_BLOCKS)] if k == 0 else row_refs[k - 1].at[d]
            return pltpu.make_async_remote_copy(src_ref=sib[k].at[q] if landed else src, dst_ref=sib[k].at[q],
                                                send_sem=send_sems.at[n * q + k], recv_sem=recv_sems.at[n * q + k],
                                                device_id=(x, y, 1 - core), device_id_type=MESH)

        def for_sibling(d, ks, fn):
            @pl.when(d % 2 != core)
            def _():
                for k in ks:
                    fn(handover(d, k))

        for d in range(N_DEV):
            for_sibling(d, range(1, n), lambda cp: cp.start())

        def fetch(i):
            s, c = tiles[i]
            return pltpu.make_async_copy(d_refs[s].at[:, pl.ds(c, W_TILE)], dbuf.at[i % 2], sem.at[1 + i % 2])

        def contract(rhs_refs, width):
            acc[:, :width] = jnp.zeros((D_MODEL, width), F32)

            def step(k, carry):
                part = None
                for j in range(kc):
                    kk = k * kc + j
                    for rhs_ref in rhs_refs:
                        prod = _mm(ut_vm[kk], rhs_ref[pl.ds(pl.multiple_of(kk * TILE, TILE), TILE), :])
                        part = prod if part is None else part + prod
                acc[:, :width] += part
                return carry

            lax.fori_loop(0, nt // kc, step, 0)
            return acc[:, :width]

        load_ut = pltpu.make_async_copy(ut_hbm, ut_vm, sem.at[0])
        load_glr = pltpu.make_async_copy(dglr_hbm, gbuf, sem.at[5])
        load_ut.start()
        load_glr.start()
        fetch(0).start()
        load_ut.wait()
        stores = {}

        def stored(i):
            stores[i].wait()
            for d in range(N_DEV):
                if last_tile[d] == i:
                    for_sibling(d, [0], lambda cp: cp.start())

        for i, (s, c) in enumerate(tiles):
            if i + 1 < len(tiles):
                fetch(i + 1).start()
            fetch(i).wait()
            if i >= 2:
                stored(i - 2)
            total = contract([dbuf.at[i % 2]], W_TILE)
            for j in range(bpt):
                obuf[i % 2, j] = total[:, j * LANES:(j + 1) * LANES].astype(BF16)
            blk0 = (SEG_OFF[s] + c) // LANES
            stores[i] = pltpu.make_async_copy(obuf.at[i % 2], out_hbm.at[pl.ds(blk0, bpt)], sem.at[3 + i % 2])
            stores[i].start()
        for i in range(max(0, len(tiles) - 2), len(tiles)):
            stored(i)
        load_glr.wait()
        head_sum = gbuf[0].astype(F32)
        for h in range(1, GLA_HEADS):
            head_sum = head_sum + gbuf[h].astype(F32)
        gbuf[0] = head_sum.astype(BF16)
        oglr_ref[...] = contract([gbuf.at[0]], LANES)
        for q in range(N_CHIP):
            for k in range(n):
                handover(2 * q, k, landed=True).wait_recv()
        for d in range(N_DEV):
            for_sibling(d, range(n), lambda cp: cp.wait_send())

    outs = _call(
        body, "inproj_bwd_w",
        out_shape=[jax.ShapeDtypeStruct((AL_COLS // LANES, D_MODEL, LANES), BF16), jax.ShapeDtypeStruct((D_MODEL, LANES), F32),
                   jax.ShapeDtypeStruct((N_CHIP, SLAB_BLOCKS, D_MODEL, LANES), BF16)]
                  + [jax.ShapeDtypeStruct((N_CHIP, *r.shape[1:]), BF16) for r in row_sends],
        in_specs=[ANY] * (12 + nr), out_specs=[ANY, pl.BlockSpec(memory_space=pltpu.VMEM)] + [ANY] * n,
        scratch_shapes=[pltpu.VMEM((nt, D_MODEL, TILE), BF16), pltpu.VMEM((2, t_rows, W_TILE), BF16),
                        pltpu.VMEM((2, bpt, D_MODEL, LANES), BF16), pltpu.VMEM((D_MODEL, W_TILE), F32),
                        pltpu.VMEM((GLA_HEADS, t_rows, LANES), BF16), pltpu.SemaphoreType.DMA((6,)),
                        pltpu.SemaphoreType.DMA((n * N_CHIP,)), pltpu.SemaphoreType.DMA((n * N_CHIP,))],
        compiler_params=_params(),
    )(ut, *[dseg[n_] for n_ in SEG_NAMES], dglr, *row_sends)
    return outs[0], outs[1], outs[2], outs[3:]


def _position():
    x, y, c = lax.axis_index("x"), lax.axis_index("y"), lax.axis_index("c")
    return x, y, c


def _index(px, py, pc):
    return 4 * px + 2 * py + pc


def _all_gather(arrs, name):
    n = len(arrs)

    def body(*refs):
        ins, outs = refs[:n], refs[n:2 * n]
        send_sems, recv_sems, local_sems = refs[2 * n:]
        x, y, c = _position()
        me, sibling = (x, y, c), (x, y, 1 - c)
        chips = [(1 - x, y), (x, 1 - y), (1 - x, 1 - y)]

        def copy(a, k, block, to, src=None):
            dst = outs[a].at[_index(*block)]
            return pltpu.make_async_remote_copy(src_ref=dst if src is None else src, dst_ref=dst,
                                                send_sem=send_sems.at[7 * a + k], recv_sem=recv_sems.at[7 * a + k],
                                                device_id=to, device_id_type=MESH)

        mine = [pltpu.make_async_copy(ins[a], outs[a].at[_index(*me)], local_sems.at[a]) for a in range(n)]
        for cp in mine:
            cp.start()
        first = []
        for a in range(n):
            first.append(copy(a, 0, me, sibling, src=ins[a]))
            first += [copy(a, 1 + j, me, (*chip, c), src=ins[a]) for j, chip in enumerate(chips)]
        for cp in first:
            cp.start()
        passed = []
        for j, chip in enumerate(chips):
            for a in range(n):
                copy(a, 1 + j, (*chip, c), me).wait_recv()
                cp = copy(a, 4 + j, (*chip, c), sibling)
                cp.start()
                passed.append(cp)
        for a in range(n):
            copy(a, 0, sibling, me).wait_recv()
            for j, chip in enumerate(chips):
                copy(a, 4 + j, (*chip, 1 - c), me).wait_recv()
        for cp in first + passed:
            cp.wait_send()
        for cp in mine:
            cp.wait()

    return _call(
        body, name,
        out_shape=[jax.ShapeDtypeStruct((N_DEV, *a.shape), a.dtype) for a in arrs],
        in_specs=[ANY] * n, out_specs=[ANY] * n,
        scratch_shapes=[pltpu.SemaphoreType.DMA((7 * n,)), pltpu.SemaphoreType.DMA((7 * n,)), pltpu.SemaphoreType.DMA((n,))],
    )(*arrs)


N_CHIP = N_DEV // 2


def _slab_block0(owner):
    step = SLAB_BLK0[1]
    assert all(SLAB_BLK0[d] == step * d - (d == N_DEV - 1) for d in range(N_DEV))
    return step * owner - jnp.where(owner == N_DEV - 1, 1, 0)


def _add_bf16(c_ref, a_ref, b_ref, o_ref):
    o_ref[...] = (a_ref[...].astype(F32) + b_ref[...].astype(F32)).astype(BF16)


def _chip_partial_slab(dw_blocks, sib, core):
    blk = pl.BlockSpec((None, SLAB_BLOCKS, D_MODEL, LANES), lambda q, c_ref: (q, 0, 0, 0))
    return _call(
        functools.partial(_add_bf16), "chip_partial_w_in", out_shape=jax.ShapeDtypeStruct(sib.shape, BF16),
        grid_spec=pltpu.PrefetchScalarGridSpec(
            num_scalar_prefetch=1, grid=(N_CHIP,),
            in_specs=[pl.BlockSpec((pl.Element(SLAB_BLOCKS), pl.Element(D_MODEL), pl.Element(LANES)),
                                   lambda q, c_ref: (_slab_block0(2 * q + c_ref[0]), 0, 0)), blk],
            out_specs=blk),
        compiler_params=_params(("arbitrary",)),
    )(core, dw_blocks, sib)


def _chip_partial_rows(send, sib, core, name):
    rows, cols = send.shape[1:]
    blk = pl.BlockSpec((None, rows, cols), lambda q, c_ref: (q, 0, 0))
    return _call(
        functools.partial(_add_bf16), name, out_shape=jax.ShapeDtypeStruct(sib.shape, BF16),
        grid_spec=pltpu.PrefetchScalarGridSpec(
            num_scalar_prefetch=1, grid=(N_CHIP,),
            in_specs=[pl.BlockSpec((None, rows, cols), lambda q, c_ref: (2 * q + c_ref[0], 0, 0)), blk], out_specs=blk),
        compiler_params=_params(("arbitrary",)),
    )(core, send, sib)


def _exchange_sems(n_arrays, n_peers):
    return [pltpu.SemaphoreType.DMA((n_arrays * n_peers,)), pltpu.SemaphoreType.DMA((n_arrays * n_peers,)),
            pltpu.SemaphoreType.DMA((n_arrays,))]


class _Exchange:
    def __init__(self, srcs, dsts, sems, among_chips):
        self.arrs = list(zip(srcs, dsts))
        self.n = len(self.arrs)
        self.send_sems, self.recv_sems, self.local_sems = sems
        self.among_chips = among_chips
        x, y, c = _position()
        self.c = c
        self.me = 2 * x + y if among_chips else _index(x, y, c)
        self.n_peers = N_CHIP if among_chips else N_DEV

    def _device(self, p):
        return (p // 2, p % 2, self.c) if self.among_chips else (p // 4, (p // 2) % 2, p % 2)

    def _src(self, k, p):
        src = self.arrs[k][0]
        return src.at[p] if self.among_chips else src

    def _mine(self):
        return [pltpu.make_async_copy(self._src(k, self.me), self.arrs[k][1].at[self.me], self.local_sems.at[k]) for k in range(self.n)]

    def _copy(self, p, k, landing):
        return pltpu.make_async_remote_copy(
            src_ref=self._src(k, p), dst_ref=self.arrs[k][1].at[landing], send_sem=self.send_sems.at[self.n * p + k],
            recv_sem=self.recv_sems.at[self.n * landing + k], device_id=self._device(p), device_id_type=MESH)

    def _others(self, fn):
        for p in range(self.n_peers):
            @pl.when(p != self.me)
            def _():
                for k in range(self.n):
                    fn(p, k)

    def start(self):
        for cp in self._mine():
            cp.start()
        self._others(lambda p, k: self._copy(p, k, self.me).start())

    def finish(self):
        self._others(lambda p, k: self._copy(p, k, p).wait_recv())
        self._others(lambda p, k: self._copy(p, k, self.me).wait_send())
        for cp in self._mine():
            cp.wait()


def _adamw(g, w, m, v):
    m_new = ADAM_B1 * m + (1.0 - ADAM_B1) * g
    v_new = ADAM_B2 * v + (1.0 - ADAM_B2) * (g * g)
    m_hat = m_new / (1.0 - ADAM_B1 ** ADAM_STEP)
    v_hat = v_new / (1.0 - ADAM_B2 ** ADAM_STEP)
    delta = -ADAM_LR * (m_hat / (jnp.sqrt(v_hat) + ADAM_EPS) + ADAM_WD * w)
    return delta, m_new, v_new


def _sum_partials(p_ref):
    g = p_ref[0].astype(F32)
    for d in range(1, p_ref.shape[0]):
        g = g + p_ref[d].astype(F32)
    return g


def _reduce_adam(parts, w, m, v, name, block_rows, row_off=0):
    rows, cols = w.shape
    off = row_off // block_rows

    def body(p_ref, w_ref, m_ref, v_ref, g_ref, d_ref, mo_ref, vo_ref):
        g = _sum_partials(p_ref)
        g_ref[...] = g
        d_ref[...], mo_ref[...], vo_ref[...] = _adamw(g, w_ref[...], m_ref[...], v_ref[...])

    blk = pl.BlockSpec((block_rows, cols), lambda i: (i, 0))
    return _call(
        body, name, grid=(rows // block_rows,),
        out_shape=[jax.ShapeDtypeStruct((rows, cols), F32)] * 4,
        in_specs=[pl.BlockSpec((parts.shape[0], block_rows, cols), lambda i: (0, i + off, 0)), blk, blk, blk],
        out_specs=[blk] * 4,
        compiler_params=_params(("arbitrary",)),
    )(parts, w, m, v)


def _reduce_adam_slab(parts, glr, w_t, m_t, v_t, me):
    cols, rows = w_t.shape
    shift = jnp.asarray(SLAB_SHIFT, jnp.int32)[me]
    glr_at = jnp.where(me == GLR_DEV, GLR_LOCAL, cols).astype(jnp.int32)

    def body(s_ref, p_ref, glr_ref, w_ref, m_ref, v_ref, g_ref, d_ref, mo_ref, vo_ref, slab_t):
        shift, glr_at = s_ref[0], s_ref[1]
        tall = jnp.concatenate([_sum_partials(p_ref.at[:, j]).T for j in range(SLAB_BLOCKS)], axis=0)
        before = pltpu.roll(tall, SLAB_W - shift, 0)
        after = pltpu.roll(tall, lax.rem(SLAB_W - shift + GLA_RANK, SLAB_W), 0)
        wide = jnp.concatenate([glr_ref[...].T, jnp.zeros((SLAB_W - LANES, LANES), F32)], axis=0)
        placed = pltpu.roll(wide, lax.rem(glr_at, SLAB_W), 0)
        row = lax.broadcasted_iota(jnp.int32, (SLAB_W, LANES), 0)
        slab_t[...] = jnp.where(row < glr_at, before, jnp.where(row < glr_at + GLA_RANK, placed, after))
        g = slab_t[pl.ds(0, cols), :]
        g_ref[...] = g
        d_ref[...], mo_ref[...], vo_ref[...] = _adamw(g, w_ref[...], m_ref[...], v_ref[...])

    blk = pl.BlockSpec((cols, LANES), lambda i, s: (0, i))
    return _call(
        body, "adam_w_in", out_shape=[jax.ShapeDtypeStruct((cols, rows), F32)] * 4,
        grid_spec=pltpu.PrefetchScalarGridSpec(
            num_scalar_prefetch=1, grid=(rows // LANES,),
            in_specs=[pl.BlockSpec((parts.shape[0], SLAB_BLOCKS, LANES, LANES), lambda i, s: (0, 0, i, 0)),
                      pl.BlockSpec((LANES, LANES), lambda i, s: (i, 0)), blk, blk, blk],
            out_specs=[blk] * 4, scratch_shapes=[pltpu.VMEM((SLAB_W, LANES), F32)]),
        compiler_params=_params(("arbitrary",)),
    )(jnp.stack([shift, glr_at]), parts, glr, w_t, m_t, v_t)


def _reduce_small(parts):
    def body(p_ref, o_ref):
        o_ref[...] = _sum_partials(p_ref)

    return _call(body, "reduce_small", out_shape=jax.ShapeDtypeStruct(parts.shape[1:], F32))(parts)


def _adam_small(g, w, m, v):
    def body(g_ref, w_ref, m_ref, v_ref, d_ref, mo_ref, vo_ref):
        d_ref[...], mo_ref[...], vo_ref[...] = _adamw(g_ref[...], w_ref[...], m_ref[...], v_ref[...])

    return _call(body, "adam_small", out_shape=[jax.ShapeDtypeStruct(g.shape, F32)] * 3)(g, w, m, v)


def _pack_rows(arrs):
    rows = []
    for a in arrs:
        flat = a.reshape(-1).astype(F32)
        pad = (-flat.shape[0]) % LANES
        rows.append(jnp.pad(flat, (0, pad)).reshape(-1, LANES))
    packed = jnp.concatenate(rows, axis=0)
    return jnp.pad(packed, ((0, (-packed.shape[0]) % 8), (0, 0)))


def _unpack_rows(packed, shapes):
    out, r = [], 0
    for shp in shapes:
        size = 1
        for s in shp:
            size *= s
        nrows = -(-size // LANES)
        out.append(packed[r:r + nrows].reshape(-1)[:size].reshape(shp))
        r += nrows
    return out


def _shard_to_slab(shard, d):
    glr = jnp.zeros((D_MODEL, GLA_RANK), shard.dtype)
    if d == GLR_DEV:
        glr = shard[:, GLR_LOCAL:GLR_LOCAL + GLA_RANK]
        shard = jnp.concatenate([shard[:, :GLR_LOCAL], shard[:, GLR_LOCAL + GLA_RANK:]], axis=1)
    return jnp.pad(shard, ((0, 0), (SLAB_SHIFT[d], SLAB_W - SLAB_SHIFT[d] - shard.shape[1]))), glr


def kernel(x, meta_tokens, norm_gain, w_in, w_gate_up, b_gate, ret_norm_gain, gla_norm_gain, w_branch_ret, w_branch_gla, w_out, final_norm_gain, loss_target, m_meta_tokens, m_norm_gain, m_w_in, m_w_gate_up, m_b_gate, m_ret_norm_gain, m_gla_norm_gain, m_w_branch_ret, m_w_branch_gla, m_w_out, m_final_norm_gain, v_meta_tokens, v_norm_gain, v_w_in, v_w_gate_up, v_b_gate, v_ret_norm_gain, v_gla_norm_gain, v_w_branch_ret, v_w_branch_gla, v_w_out, v_final_norm_gain):
    xi, yi, ci = _position()
    me = _index(xi, yi, ci)
    seq = x.shape[1]
    t_rows = seq + TILE
    in_shard = w_in.shape[2]
    gu_shard = w_gate_up.shape[2]
    meta_shard = meta_tokens.shape[1]
    ret_rows, gla_rows, out_rows = w_branch_ret.shape[1], w_branch_gla.shape[1], w_out.shape[1]

    assert in_shard == IN_SHARD
    slab_local, glr_local = lax.switch(me, [functools.partial(_shard_to_slab, d=d) for d in range(N_DEV)], w_in[0])
    small_local = jnp.concatenate([meta_tokens, jnp.pad(w_gate_up[0], ((0, 0), (0, LANES - gu_shard))),
                                   glr_local.reshape(-1, LANES)], axis=0)
    slabs, g_small = _all_gather([slab_local.astype(BF16), small_local], "all_gather_shards")
    n_small = N_META + GLA_RANK
    w_glr = jnp.pad(g_small[GLR_DEV, n_small:].reshape(D_MODEL, GLA_RANK), ((0, 0), (0, LANES - GLA_RANK))).astype(BF16)
    meta_full = jnp.transpose(g_small[:, :N_META, :], (1, 0, 2)).reshape(N_META, D_MODEL)
    wgu_full = jnp.transpose(g_small[:, N_META:n_small, :gu_shard], (1, 0, 2)).reshape(GLA_RANK, GLA_HEADS * GLA_K)
    wgu_pad = jnp.pad(wgu_full, ((0, LANES - GLA_RANK), (0, 0)))

    pos = jnp.arange(t_rows, dtype=F32) - float(PAD_ROWS)
    half = RET_QK // 2
    inv = ROPE_BASE ** (-jnp.arange(half, dtype=F32) / half)
    ang = pos[:, None] * inv[None, :]
    cos, sin = jnp.cos(ang), jnp.sin(ang)
    lg = jnp.log1p(-(2.0 ** (-5.0 - jnp.arange(RET_HEADS, dtype=F32))))

    head = jnp.concatenate([jnp.zeros((PAD_ROWS, D_MODEL), F32), meta_full], axis=0)
    ut, proj, glr = _inproj_tiles(head, x[0], norm_gain, slabs, w_glr)
    o_ret_raw, o_ret, ret_states, (g_br, g_bg, g_o) = _ret_fwd(
        proj, cos, sin, ret_norm_gain, lg, [w_branch_ret[0].astype(BF16), w_branch_gla[0].astype(BF16), w_out[0].astype(BF16)])
    w_br, w_bg, w_o = g_br.reshape(RET_W, D_MODEL), g_bg.reshape(GLA_W, D_MODEL), g_o.reshape(D_MODEL, D_MODEL)
    masks, cum_fwd, cum_bwd = _gla_tables()
    o_gla_raw, o_gla, gla_states, gla_scores_t = _gla_fwd(proj, glr, wgu_pad, b_gate, gla_norm_gain, masks, cum_fwd)
    (dh1, d_mr, d_mg, do_ret, do_gla, loss_part, d_gfinal, dw_br, dw_bg, dw_o) = _merge_fwd_bwd(
        o_ret, o_gla, proj, x[0], loss_target[0], final_norm_gain.reshape(1, D_MODEL), w_br, w_bg, w_o)

    d_rq, d_rk, d_rv, d_rg, d_gret = _ret_bwd(proj, cos, sin, ret_norm_gain, lg, o_ret_raw, do_ret, ret_states)
    d_gq, d_gk, d_gv, d_gg, dglr_parts, d_wgu, d_bgate, d_ggla = _gla_bwd(
        proj, glr, wgu_pad, b_gate, gla_norm_gain, o_gla_raw, do_gla, gla_states, gla_scores_t, masks, cum_fwd, cum_bwd)
    dseg = dict(rq=d_rq, rk=d_rk, rv=d_rv, rg=d_rg, gq=d_gq, gk=d_gk, gv=d_gv, gg=d_gg, mr=d_mr, mg=d_mg)
    row_sends = [dw_br.reshape(N_DEV, ret_rows, D_MODEL), dw_bg.reshape(N_DEV, gla_rows, D_MODEL),
                 dw_o.reshape(N_DEV, out_rows, D_MODEL)]
    dw_blocks, dw_glr, sib_in, sib_rows = _inproj_bwd_w(ut, dseg, dglr_parts, row_sends)
    core = ci.astype(jnp.int32).reshape(1)
    chip_partials = [_chip_partial_slab(dw_blocks, sib_in, core)] + [
        _chip_partial_rows(send, sib, core, "chip_partial_" + name)
        for send, sib, name in zip(row_sends, sib_rows, ("w_branch_ret", "w_branch_gla", "w_out"))]
    grad_x, d_head, d_gnorm, p_in, p_br, p_bg, p_o = _inproj_bwd_x(
        dseg, dglr_parts, head, x[0], dh1, norm_gain, slabs, w_glr, chip_partials)
    small_shapes = [(N_META, D_MODEL), (1, D_MODEL), (GLA_RANK, GLA_HEADS * GLA_K), (1, GLA_HEADS * GLA_K),
                    (1, RET_W), (1, GLA_W), (1, D_MODEL), (1, LANES), (D_MODEL, GLA_RANK)]
    small_part = _pack_rows([d_head[PAD_ROWS:], d_gnorm, d_wgu[:GLA_RANK], d_bgate, d_gret, d_ggla, d_gfinal, loss_part,
                             dw_glr[:, :GLA_RANK]])
    (p_small,) = _all_gather([small_part], "all_gather_small_partials")

    (g_meta_f, g_gnorm, g_wgu_f, g_bgate, g_gret, g_ggla, g_gfinal, loss_all,
     g_wglr) = _unpack_rows(_reduce_small(p_small), small_shapes)
    g_w_in, d_w_in, nm_w_in, nv_w_in = [a.T for a in _reduce_adam_slab(
        p_in, jnp.pad(g_wglr, ((0, 0), (0, LANES - GLA_RANK))), w_in[0].T, m_w_in[0].T, v_w_in[0].T, me)]
    rb = gla_rows
    g_w_br, d_w_br, nm_w_br, nv_w_br = _reduce_adam(p_br, w_branch_ret[0], m_w_branch_ret[0], v_w_branch_ret[0], "adam_w_branch_ret", rb)
    g_w_bg, d_w_bg, nm_w_bg, nv_w_bg = _reduce_adam(p_bg, w_branch_gla[0], m_w_branch_gla[0], v_w_branch_gla[0], "adam_w_branch_gla", rb)
    g_w_o, d_w_o, nm_w_o, nv_w_o = _reduce_adam(p_o, w_out[0], m_w_out[0], v_w_out[0], "adam_w_out", rb)
    g_meta = lax.dynamic_slice_in_dim(g_meta_f, me * meta_shard, meta_shard, axis=1)
    g_wgu = lax.dynamic_slice_in_dim(g_wgu_f, me * gu_shard, gu_shard, axis=1)
    s_g = [g_meta, g_gnorm, g_wgu, g_bgate, g_gret, g_ggla, g_gfinal]
    s_w = [meta_tokens, norm_gain, w_gate_up[0], b_gate, ret_norm_gain, gla_norm_gain, final_norm_gain]
    s_m = [m_meta_tokens, m_norm_gain, m_w_gate_up[0], m_b_gate, m_ret_norm_gain, m_gla_norm_gain, m_final_norm_gain]
    s_v = [v_meta_tokens, v_norm_gain, v_w_gate_up[0], v_b_gate, v_ret_norm_gain, v_gla_norm_gain, v_final_norm_gain]
    shapes = [a.shape for a in s_g]
    s_d, s_nm, s_nv = [_unpack_rows(p, shapes) for p in _adam_small(*[_pack_rows(l) for l in (s_g, s_w, s_m, s_v)])]

    loss = loss_all[0, 0]
    grad_x = grad_x[None]

    def order(meta, gnorm, win, wgu, bgate, gret, ggla, wbr, wbg, wo, gfin):
        return (meta, gnorm, win[None], wgu[None], bgate, gret, ggla, wbr[None], wbg[None], wo[None], gfin.reshape(final_norm_gain.shape))

    def small(l):
        return dict(meta=l[0], gnorm=l[1], wgu=l[2], bgate=l[3], gret=l[4], ggla=l[5], gfin=l[6])

    grads = order(win=g_w_in, wbr=g_w_br, wbg=g_w_bg, wo=g_w_o, **small(s_g))
    deltas = order(win=d_w_in, wbr=d_w_br, wbg=d_w_bg, wo=d_w_o, **small(s_d))
    new_m = order(win=nm_w_in, wbr=nm_w_br, wbg=nm_w_bg, wo=nm_w_o, **small(s_nm))
    new_v = order(win=nv_w_in, wbr=nv_w_br, wbg=nv_w_bg, wo=nv_w_o, **small(s_nv))
    return (loss, grad_x, *grads, *deltas, *new_m, *new_v)
```

```python
import functools

import jax
import jax.numpy as jnp
from jax import lax
from jax.experimental import pallas as pl
from jax.experimental.pallas import tpu as pltpu

F32 = jnp.float32
BF16 = jnp.bfloat16

D_MODEL = 1024
N_META = 16
TILE = 256
PAD_ROWS = TILE - N_META
RET_HEADS = 4
RET_QK = 256
RET_V = 512
RET_W = RET_HEADS * RET_V
GLA_HEADS = 4
GLA_K = 128
GLA_V = 256
GLA_W = GLA_HEADS * GLA_V
GLA_RANK = 16
GLA_TAU = 16.0
GLA_CHUNK = 16
ROPE_BASE = 10000.0
EPS = 1e-6
LANES = 128
N_DEV = 8
SEG_NAMES = ("rq", "rk", "rv", "rg", "gq", "gk", "gv", "gg", "mr", "mg")
SEG_W = (1024, 1024, 2048, 2048, 512, 512, 1024, 1024, 1024, 1024)
SEG_OFF = tuple(sum(SEG_W[:i]) for i in range(len(SEG_W)))
AL_COLS = sum(SEG_W)
IN_COLS = AL_COLS + GLA_RANK
GLR_OFF = sum(SEG_W[:8])
IN_SHARD = IN_COLS // N_DEV


def _aligned_col(c):
    assert c <= GLR_OFF or c >= GLR_OFF + GLA_RANK
    return c if c <= GLR_OFF else c - GLA_RANK


SLAB_BOUND = tuple(_aligned_col(IN_SHARD * d) for d in range(N_DEV + 1))
SLAB_BLK0 = tuple(b // LANES for b in SLAB_BOUND[:-1])
SLAB_SHIFT = tuple(b % LANES for b in SLAB_BOUND[:-1])
SLAB_BLOCKS = max(-(-SLAB_BOUND[d + 1] // LANES) - SLAB_BLK0[d] for d in range(N_DEV))
SLAB_W = SLAB_BLOCKS * LANES
GLR_DEV = GLR_OFF // IN_SHARD
GLR_LOCAL = GLR_OFF - GLR_DEV * IN_SHARD
assert all(SLAB_BLK0[d] + SLAB_BLOCKS <= AL_COLS // LANES for d in range(N_DEV))
VMEM_LIMIT = 58 * 1024 * 1024
ADAM_LR, ADAM_B1, ADAM_B2, ADAM_EPS, ADAM_WD, ADAM_STEP = 0.001, 0.9, 0.999, 1e-08, 0.01, 10
ANY = pl.BlockSpec(memory_space=pl.ANY)
MESH = pl.DeviceIdType.MESH


def _call(body, name, **kw):
    return pl.pallas_call(body, name=name, **kw)


def _params(sem=None):
    return pltpu.CompilerParams(dimension_semantics=sem, vmem_limit_bytes=VMEM_LIMIT)


def _mm(a, b):
    return jnp.dot(a, b, preferred_element_type=F32)


def _mm_nt(a, b):
    return lax.dot_general(a, b, (((1,), (1,)), ((), ())), preferred_element_type=F32)


def _mm_tn(a, b):
    return lax.dot_general(a, b, (((0,), (0,)), ((), ())), preferred_element_type=F32)


def _sigmoid(x):
    return 1.0 / (1.0 + jnp.exp(-x))


def _rope(t, cos, sin):
    half = t.shape[-1] // 2
    t1, t2 = t[:, :half], t[:, half:]
    return jnp.concatenate([t1 * cos - t2 * sin, t2 * cos + t1 * sin], axis=-1)


def _rope_bwd(g, cos, sin):
    half = g.shape[-1] // 2
    g1, g2 = g[:, :half], g[:, half:]
    return jnp.concatenate([g1 * cos + g2 * sin, g2 * cos - g1 * sin], axis=-1)


def _row_mean(x):
    return jnp.mean(x, axis=-1, keepdims=True)


def _col_sum(x):
    return jnp.sum(x, axis=0, keepdims=True)


def _tile_rows(head_ref, x_ref):
    return jnp.where(pl.program_id(0) == 0, head_ref[...], x_ref[...])


def _head_spec():
    return pl.BlockSpec((TILE, D_MODEL), lambda i: (0, 0))


def _x_spec():
    return pl.BlockSpec((TILE, D_MODEL), lambda i: (jnp.maximum(i - 1, 0), 0))


def _slab_plan():
    interior, shared = [], []
    for d in range(N_DEV):
        lo, hi = -(-SLAB_BOUND[d] // LANES), SLAB_BOUND[d + 1] // LANES
        interior.append((d, LANES * (lo - SLAB_BLK0[d]), LANES * lo, LANES * (hi - lo)))
        if d + 1 < N_DEV and SLAB_BOUND[d + 1] % LANES:
            shared.append((hi, d, hi - SLAB_BLK0[d]))
    return interior, shared


W_SCRATCH = lambda: [pltpu.VMEM((D_MODEL, AL_COLS), BF16), pltpu.VMEM((D_MODEL, LANES), BF16),
                     pltpu.VMEM((2 * (N_DEV - 1), D_MODEL, LANES), BF16), pltpu.SemaphoreType.DMA((3 * N_DEV,))]


def _load_weight(slabs_hbm, wg_hbm, w_vm, wg_vm, edge_vm, sem):
    interior, shared = _slab_plan()
    copies = [pltpu.make_async_copy(wg_hbm, wg_vm, sem.at[0])]
    for d, src, dst, width in interior:
        copies.append(pltpu.make_async_copy(slabs_hbm.at[d, :, pl.ds(src, width)], w_vm.at[:, pl.ds(dst, width)], sem.at[1 + d]))
    for n, (_, d, blk) in enumerate(shared):
        copies.append(pltpu.make_async_copy(slabs_hbm.at[d, :, pl.ds(LANES * blk, LANES)], edge_vm.at[2 * n], sem.at[1 + N_DEV + 2 * n]))
        copies.append(pltpu.make_async_copy(slabs_hbm.at[d + 1, :, pl.ds(0, LANES)], edge_vm.at[2 * n + 1], sem.at[2 + N_DEV + 2 * n]))
    for cp in copies:
        cp.start()
    for cp in copies:
        cp.wait()
    for n, (blk, _, _) in enumerate(shared):
        w_vm[:, LANES * blk:LANES * (blk + 1)] = edge_vm[2 * n] + edge_vm[2 * n + 1]


def _proj_specs(names, n_units, where):
    specs = []
    for name in names:
        s = SEG_NAMES.index(name)
        nblk = SEG_W[s] // n_units // LANES
        base = SEG_OFF[s] // LANES
        assert base % nblk == 0
        specs.append(pl.BlockSpec((nblk, TILE, LANES), lambda *g, base=base, nblk=nblk: (base // nblk + where(*g)[0], where(*g)[1], 0)))
    return specs


def _cols(ref, unit=0, n_units=1):
    n = ref.shape[0] // n_units
    return ref[unit * n] if n == 1 else jnp.concatenate([ref[unit * n + j] for j in range(n)], axis=1)


def _inproj_tiles(head, x, g_norm, slabs, w_glr):
    t_rows = x.shape[0] + TILE
    nt = t_rows // TILE
    n_blocks = AL_COLS // LANES

    def body(head_ref, x_ref, g_ref, slabs_hbm, wg_hbm, ut_ref, proj_ref, glr_ref, w_vm, wg_vm, edge_vm, sem):
        @pl.when(pl.program_id(0) == 0)
        def _():
            _load_weight(slabs_hbm, wg_hbm, w_vm, wg_vm, edge_vm, sem)

        x = _tile_rows(head_ref, x_ref)
        r = lax.rsqrt(_row_mean(x * x) + EPS)
        u32 = (x * r * g_ref[...]).astype(BF16).astype(F32)
        u = u32.astype(BF16)
        ut_ref[...] = u32.T.astype(BF16)
        for s in range(len(SEG_W)):
            res = _mm(u, w_vm[:, SEG_OFF[s]:SEG_OFF[s] + SEG_W[s]]).astype(BF16)
            for j in range(SEG_W[s] // LANES):
                proj_ref[SEG_OFF[s] // LANES + j] = res[:, j * LANES:(j + 1) * LANES]
        glr_ref[...] = _mm(u, wg_vm[...])

    return _call(
        body, "inproj_fwd_tiles", grid=(nt,),
        out_shape=[jax.ShapeDtypeStruct((nt, D_MODEL, TILE), BF16), jax.ShapeDtypeStruct((n_blocks, t_rows, LANES), BF16),
                   jax.ShapeDtypeStruct((t_rows, LANES), F32)],
        in_specs=[_head_spec(), _x_spec(), pl.BlockSpec((1, D_MODEL), lambda i: (0, 0)), ANY, ANY],
        out_specs=[pl.BlockSpec((None, D_MODEL, TILE), lambda i: (i, 0, 0)), pl.BlockSpec((n_blocks, TILE, LANES), lambda i: (0, i, 0)),
                   pl.BlockSpec((TILE, LANES), lambda i: (i, 0))],
        scratch_shapes=W_SCRATCH(), compiler_params=_params(("arbitrary",)),
    )(head, x, g_norm, slabs, w_glr)


def _ret_decay(lgh):
    i = lax.broadcasted_iota(jnp.int32, (TILE, TILE), 0)
    j = lax.broadcasted_iota(jnp.int32, (TILE, TILE), 1)
    rel = (i - j).astype(F32)
    return jnp.where(rel >= 0, jnp.exp(jnp.maximum(rel, 0.0) * lgh), 0.0)


def _ret_vectors(lgh):
    idx = lax.broadcasted_iota(jnp.int32, (TILE, 1), 0).astype(F32)
    xi = jnp.exp((idx + 1.0) * lgh)
    zeta = jnp.exp((TILE - 1.0 - idx) * lgh)
    gc = jnp.exp(jnp.full((1, 1), float(TILE), F32) * lgh)
    return xi, zeta, gc


def _ret_fwd(proj, cos, sin, gain, lg, row_shards):
    t_rows = cos.shape[0]
    nt = t_rows // TILE
    ns = len(row_shards)

    def body(lg_ref, q_ref, k_ref, v_ref, g_ref, cos_ref, sin_ref, gain_ref, *rest):
        shard_refs, (oraw_ref, oret_ref, st_ref), gathered = rest[:ns], rest[ns:ns + 3], rest[ns + 3:2 * ns + 3]
        s_acc, dm = rest[2 * ns + 3:2 * ns + 5]
        gather = _Exchange(shard_refs, gathered, rest[2 * ns + 5:], among_chips=False)
        t = pl.program_id(0)

        @pl.when(t == 0)
        def _():
            gather.start()
            s_acc[...] = jnp.zeros_like(s_acc)
            for h in range(RET_HEADS):
                dm[h] = _ret_decay(lg_ref[h])

        @pl.when(t == nt - 1)
        def _():
            gather.finish()

        cos_t, sin_t = cos_ref[...], sin_ref[...]
        for h in range(RET_HEADS):
            lgh = lg_ref[h]
            q = _rope(_cols(q_ref, h, RET_HEADS).astype(F32), cos_t, sin_t)
            k = _rope(_cols(k_ref, h, RET_HEADS).astype(F32), cos_t, sin_t) * (RET_QK ** -0.5)
            xi, zeta, gc = _ret_vectors(lgh)
            v = _cols(v_ref, h, RET_HEADS)
            s_in = s_acc[h]
            p = (_mm_nt(q.astype(BF16), k.astype(BF16)) * dm[h]).astype(BF16)
            o = _mm(p, v) + _mm((q * xi).astype(BF16), s_in.astype(BF16))
            st_ref[h] = s_in.astype(BF16)
            s_acc[h] = s_in * gc + _mm_tn((k * zeta).astype(BF16), v)
            cols = slice(h * RET_V, (h + 1) * RET_V)
            oraw_ref[:, cols] = o
            oc = o - _row_mean(o)
            n = oc * lax.rsqrt(_row_mean(oc * oc) + EPS) * gain_ref[:, cols]
            g = _cols(g_ref, h, RET_HEADS).astype(F32)
            oret_ref[:, cols] = (n * g * _sigmoid(g)).astype(BF16)

    row = lambda w: pl.BlockSpec((TILE, w), lambda t: (t, 0))
    outs = _call(
        body, "ret_fwd", grid=(nt,),
        out_shape=[jax.ShapeDtypeStruct((t_rows, RET_W), F32), jax.ShapeDtypeStruct((t_rows, RET_W), BF16),
                   jax.ShapeDtypeStruct((RET_HEADS, nt, RET_QK, RET_V), BF16)]
                  + [jax.ShapeDtypeStruct((N_DEV, *a.shape), a.dtype) for a in row_shards],
        in_specs=[pl.BlockSpec(memory_space=pltpu.SMEM)] + _proj_specs(("rq", "rk", "rv", "rg"), 1, lambda t: (0, t)) + [row(LANES), row(LANES),
                  pl.BlockSpec((1, RET_W), lambda t: (0, 0))] + [ANY] * ns,
        out_specs=[row(RET_W), row(RET_W), pl.BlockSpec((RET_HEADS, None, RET_QK, RET_V), lambda t: (0, t, 0, 0))] + [ANY] * ns,
        scratch_shapes=[pltpu.VMEM((RET_HEADS, RET_QK, RET_V), F32), pltpu.VMEM((RET_HEADS, TILE, TILE), F32)] + _exchange_sems(ns, N_DEV),
        compiler_params=_params(("arbitrary",)),
    )(lg, proj, proj, proj, proj, cos, sin, gain, *row_shards)
    return outs[0], outs[1], outs[2], outs[3:]


def _ret_bwd(proj, cos, sin, gain, lg, o_raw, do_ret, states):
    t_rows = cos.shape[0]
    nt = t_rows // TILE

    def body(lg_ref, q_ref, k_ref, v_ref, g_ref, cos_ref, sin_ref, gain_ref, oraw_ref, do_ref, st_ref,
             dq_ref, dk_ref, dv_ref, dg_ref, dgain_ref, e_acc, dm):
        @pl.when(pl.program_id(0) == 0)
        def _():
            e_acc[...] = jnp.zeros_like(e_acc)
            for h in range(RET_HEADS):
                dm[h] = _ret_decay(lg_ref[h])
            dgain_ref[...] = jnp.zeros_like(dgain_ref)

        cos_t, sin_t = cos_ref[...], sin_ref[...]
        for h in range(RET_HEADS):
            lgh = lg_ref[h]
            cols = slice(h * RET_V, (h + 1) * RET_V)
            qcols = slice(h * RET_QK, (h + 1) * RET_QK)
            q = _rope(_cols(q_ref, h, RET_HEADS).astype(F32), cos_t, sin_t)
            k = _rope(_cols(k_ref, h, RET_HEADS).astype(F32), cos_t, sin_t) * (RET_QK ** -0.5)
            xi, zeta, gc = _ret_vectors(lgh)
            v = _cols(v_ref, h, RET_HEADS)
            g = _cols(g_ref, h, RET_HEADS).astype(F32)
            o = oraw_ref[:, cols]
            do = do_ref[:, cols].astype(F32)
            oc = o - _row_mean(o)
            rstd = lax.rsqrt(_row_mean(oc * oc) + EPS)
            xh = oc * rstd
            gain_t = gain_ref[:, cols]
            sg = _sigmoid(g)
            dn = do * (g * sg)
            dg_ref[:, cols] = (do * (xh * gain_t) * (sg * (1.0 + g * (1.0 - sg)))).astype(BF16)
            dgain_ref[:, cols] += _col_sum(dn * xh)
            dxh = dn * gain_t
            dob = (rstd * (dxh - _row_mean(dxh) - xh * _row_mean(dxh * xh))).astype(BF16)
            dmat = dm[h]
            qb, kb = q.astype(BF16), k.astype(BF16)
            p = (_mm_nt(qb, kb) * dmat).astype(BF16)
            dp = (_mm_nt(dob, v) * dmat).astype(BF16)
            s_in = st_ref[h]
            e_in = e_acc[h]
            e_b = e_in.astype(BF16)
            dq = _mm(dp, kb) + _mm_nt(dob, s_in) * xi
            dk = _mm_tn(dp, qb) + _mm_nt(v, e_b) * zeta
            dv_ref[:, cols] = (_mm_tn(p, dob) + _mm((k * zeta).astype(BF16), e_b)).astype(BF16)
            e_acc[h] = e_in * gc + _mm_tn((q * xi).astype(BF16), dob)
            dq_ref[:, qcols] = _rope_bwd(dq, cos_t, sin_t).astype(BF16)
            dk_ref[:, qcols] = (_rope_bwd(dk, cos_t, sin_t) * (RET_QK ** -0.5)).astype(BF16)

    row = lambda w: pl.BlockSpec((TILE, w), lambda j: (nt - 1 - j, 0))
    vec = pl.BlockSpec((1, RET_W), lambda j: (0, 0))
    return _call(
        body, "ret_bwd", grid=(nt,),
        out_shape=[jax.ShapeDtypeStruct((t_rows, RET_HEADS * RET_QK), BF16), jax.ShapeDtypeStruct((t_rows, RET_HEADS * RET_QK), BF16),
                   jax.ShapeDtypeStruct((t_rows, RET_W), BF16), jax.ShapeDtypeStruct((t_rows, RET_W), BF16),
                   jax.ShapeDtypeStruct((1, RET_W), F32)],
        in_specs=[pl.BlockSpec(memory_space=pltpu.SMEM)] + _proj_specs(("rq", "rk", "rv", "rg"), 1, lambda j: (0, nt - 1 - j)) + [row(LANES), row(LANES), vec,
                  row(RET_W), row(RET_W), pl.BlockSpec((RET_HEADS, None, RET_QK, RET_V), lambda j: (0, nt - 1 - j, 0, 0))],
        out_specs=[row(RET_HEADS * RET_QK), row(RET_HEADS * RET_QK), row(RET_W), row(RET_W), vec],
        scratch_shapes=[pltpu.VMEM((RET_HEADS, RET_QK, RET_V), F32), pltpu.VMEM((RET_HEADS, TILE, TILE), F32)],
        compiler_params=_params(("arbitrary",)),
    )(lg, proj, proj, proj, proj, cos, sin, gain, o_raw, do_ret, states)


GLA_LEVELS = (32, 64, 128, 256)
N_TERMS = 1 + len(GLA_LEVELS)


def _gla_tables():
    p = jnp.arange(TILE)[:, None]
    r = jnp.arange(TILE)[None, :]
    masks = [(p // GLA_CHUNK == r // GLA_CHUNK) & (r <= p)]
    for blk in GLA_LEVELS:
        masks.append((p // blk == r // blk) & (p % blk >= blk // 2) & (r % blk < blk // 2))
    masks = jnp.stack(masks + [m.T for m in masks]).astype(F32)
    cum_fwd = jnp.concatenate([r <= p, masks[0] > 0], axis=0).astype(BF16)
    cum_bwd = jnp.concatenate([r >= p, masks[N_TERMS] > 0], axis=1).astype(BF16)
    return masks, cum_fwd, cum_bwd


def _split3(x):
    hi = x.astype(BF16)
    rest = x - hi.astype(F32)
    mid = rest.astype(BF16)
    lo = (rest - mid.astype(F32)).astype(BF16)
    return jnp.concatenate([hi, mid, lo], axis=1)


def _join3(y):
    w = y.shape[1] // 3
    return (y[:, 2 * w:] + y[:, w:2 * w]) + y[:, :w]


def _gla_decays(glr_ref, wgu_ref, b_ref, cum_ref):
    z = _mm(glr_ref[...].astype(BF16), wgu_ref[...].astype(BF16)) + b_ref[...]
    la = (jnp.minimum(z, 0.0) - jnp.log(1.0 + jnp.exp(-jnp.abs(z)))) / GLA_TAU
    width = la.shape[1]
    hi = la.astype(BF16)
    rest = la - hi.astype(F32)
    mid = rest.astype(BF16)
    lo = (rest - mid.astype(F32)).astype(BF16)
    y = _mm(cum_ref[...], jnp.concatenate([hi, mid, lo], axis=1))
    gb = (y[:, 2 * width:] + y[:, width:2 * width]) + y[:, :width]
    return z, gb[:TILE], gb[TILE:]


def _gla_prep(h, q_ref, k_ref, g_all, b_all, g_scr, ref_scr):
    cols = slice(h * GLA_K, (h + 1) * GLA_K)
    g, b = g_all[:, cols], b_all[:, cols]
    g_scr[h] = g
    factors = [(jnp.exp(b), jnp.exp(-b))]
    for lvl, blk in enumerate(GLA_LEVELS):
        for n in range(TILE // blk):
            ref_scr[h, lvl, n * blk:(n + 1) * blk, :] = jnp.broadcast_to(g_scr[h, pl.ds(n * blk + blk // 2 - 1, 1), :], (blk, GLA_K))
        x = g - ref_scr[h, lvl]
        factors.append((jnp.exp(jnp.minimum(x, 0.0)), jnp.exp(jnp.minimum(-x, 0.0))))
    g_last = g_scr[h, pl.ds(TILE - 1, 1), :]
    q = _cols(q_ref, h, GLA_HEADS).astype(F32) * (GLA_K ** -0.5)
    k = _cols(k_ref, h, GLA_HEADS).astype(F32)
    return q, k, factors, jnp.exp(g), jnp.exp(g_last), jnp.exp(g_last - g)


def _gla_scores(q, k, factors, m_ref):
    a = jnp.zeros((TILE, TILE), F32)
    for l, (fq, fk) in enumerate(factors):
        s = _mm_nt((q * fq).astype(BF16), (k * fk).astype(BF16))
        a = jnp.where(m_ref[l] > 0.0, s, a)
    return a


def _gla_fwd(proj, glr, wgu_pad, b_gate, gain, masks, cum_fwd):
    t_rows = glr.shape[0]
    nt = t_rows // TILE

    def body(q_ref, k_ref, v_ref, g_ref, glr_ref, wgu_ref, b_ref, gain_ref, m_ref, cum_ref, oraw_ref, ogla_ref, st_ref, at_ref,
             s_acc, g_scr, ref_scr):
        @pl.when(pl.program_id(0) == 0)
        def _():
            s_acc[...] = jnp.zeros_like(s_acc)

        _, g_all, b_all = _gla_decays(glr_ref, wgu_ref, b_ref, cum_ref)
        for h in range(GLA_HEADS):
            q, k, factors, e_g, e_last, e_end = _gla_prep(h, q_ref, k_ref, g_all, b_all, g_scr, ref_scr)
            v = _cols(v_ref, h, GLA_HEADS)
            st = s_acc[h]
            st_ref[h] = st
            a = _gla_scores(q, k, factors, m_ref)
            at_ref[h] = a.T.astype(BF16)
            o = _mm(a.astype(BF16), v) + _mm_nt((q * e_g).astype(BF16), st.astype(BF16))
            s_acc[h] = st * e_last + _mm(v.astype(F32).T.astype(BF16), (k * e_end).astype(BF16))
            cols = slice(h * GLA_V, (h + 1) * GLA_V)
            oraw_ref[:, cols] = o
            n = o * lax.rsqrt(_row_mean(o * o) + EPS) * gain_ref[:, cols]
            g = _cols(g_ref, h, GLA_HEADS).astype(F32)
            ogla_ref[:, cols] = (n * g * _sigmoid(g)).astype(BF16)

    row = lambda w: pl.BlockSpec((TILE, w), lambda t: (t, 0))
    whole = lambda *shape: pl.BlockSpec(shape, lambda t: (0,) * len(shape))
    return _call(
        body, "gla_fwd", grid=(nt,),
        out_shape=[jax.ShapeDtypeStruct((t_rows, GLA_W), F32), jax.ShapeDtypeStruct((t_rows, GLA_W), BF16),
                   jax.ShapeDtypeStruct((GLA_HEADS, nt, GLA_V, GLA_K), F32), jax.ShapeDtypeStruct((GLA_HEADS, t_rows, TILE), BF16)],
        in_specs=_proj_specs(("gq", "gk", "gv", "gg"), 1, lambda t: (0, t)) + [row(LANES), whole(LANES, GLA_HEADS * GLA_K),
                  whole(1, GLA_HEADS * GLA_K), whole(1, GLA_W), whole(N_TERMS, TILE, TILE), whole(2 * TILE, TILE)],
        out_specs=[row(GLA_W), row(GLA_W), pl.BlockSpec((GLA_HEADS, None, GLA_V, GLA_K), lambda t: (0, t, 0, 0)),
                   pl.BlockSpec((GLA_HEADS, TILE, TILE), lambda t: (0, t, 0))],
        scratch_shapes=[pltpu.VMEM((GLA_HEADS, GLA_V, GLA_K), F32), pltpu.VMEM((GLA_HEADS, TILE, GLA_K), F32),
                        pltpu.VMEM((GLA_HEADS, len(GLA_LEVELS), TILE, GLA_K), F32)],
        compiler_params=_params(("arbitrary",)),
    )(proj, proj, proj, proj, glr, wgu_pad, b_gate, gain, masks, cum_fwd)


def _gla_bwd(proj, glr, wgu_pad, b_gate, gain, o_raw, do_gla, states, a_t, masks, cum_fwd, cum_bwd):
    t_rows = glr.shape[0]
    nt = t_rows // TILE

    def body(q_ref, k_ref, v_ref, g_ref, glr_ref, wgu_ref, b_ref, gain_ref, m_ref, cum_ref, cumb_ref, oraw_ref, do_ref, st_ref, at_ref,
             dq_ref, dk_ref, dv_ref, dg_ref, dglr_ref, dwgu_ref, dbg_ref, dgain_ref, d_acc, g_scr, ref_scr, dref_scr):
        @pl.when(pl.program_id(0) == 0)
        def _():
            d_acc[...] = jnp.zeros_like(d_acc)
            dwgu_ref[...] = jnp.zeros_like(dwgu_ref)
            dbg_ref[...] = jnp.zeros_like(dbg_ref)
            dgain_ref[...] = jnp.zeros_like(dgain_ref)

        z_all, g_all, b_all = _gla_decays(glr_ref, wgu_ref, b_ref, cum_ref)
        dla_parts = []
        for h in range(GLA_HEADS):
            q, k, factors, e_g, e_last, e_end = _gla_prep(h, q_ref, k_ref, g_all, b_all, g_scr, ref_scr)
            v = _cols(v_ref, h, GLA_HEADS)
            cols = slice(h * GLA_V, (h + 1) * GLA_V)
            kcols = slice(h * GLA_K, (h + 1) * GLA_K)
            o = oraw_ref[:, cols]
            do = do_ref[:, cols].astype(F32)
            g = _cols(g_ref, h, GLA_HEADS).astype(F32)
            rinv = lax.rsqrt(_row_mean(o * o) + EPS)
            nh = o * rinv
            gain_t = gain_ref[:, cols]
            sg = _sigmoid(g)
            dn = do * (g * sg)
            dg_ref[:, cols] = (do * (nh * gain_t) * (sg * (1.0 + g * (1.0 - sg)))).astype(BF16)
            dgain_ref[:, cols] += _col_sum(dn * nh)
            dnh = dn * gain_t
            dor = rinv * (dnh - nh * _row_mean(dnh * nh))
            dob = dor.astype(BF16)
            a_t = at_ref[h]
            da = _mm_nt(dob, v).astype(BF16)
            da_t = _mm_nt(v, dob).astype(BF16)
            st_in = st_ref[h]
            d_out = d_acc[h]
            d_out_b = d_out.astype(BF16)
            qg, kg = q * e_g, k * e_end
            dqg = _mm(dob, st_in.astype(BF16))
            dkg = _mm(v, d_out_b)
            dv_ref[:, cols] = (_mm(a_t, dob) + _mm_nt(kg.astype(BF16), d_out_b)).astype(BF16)
            d_acc[h] = d_out * e_last + _mm(dor.T.astype(BF16), qg.astype(BF16))
            dq = dqg * e_g
            dk = dkg * e_end
            dkg_kg = dkg * kg
            dg_cum = dqg * qg - dkg_kg
            db = None
            for l, (fq, fk) in enumerate(factors):
                qt, kt = q * fq, k * fk
                dqt = _mm(da * m_ref[l], kt.astype(BF16))
                dkt = _mm(da_t * m_ref[N_TERMS + l], qt.astype(BF16))
                dq = dq + dqt * fq
                dk = dk + dkt * fk
                diff = dqt * qt - dkt * kt
                if l == 0:
                    db = diff
                else:
                    dg_cum = dg_cum + diff
                    dref_scr[h, l - 1] = diff
            dq_ref[:, kcols] = (dq * (GLA_K ** -0.5)).astype(BF16)
            dk_ref[:, kcols] = dk.astype(BF16)
            g_scr[h] = dg_cum
            g_scr[h, pl.ds(TILE - 1, 1), :] += e_last * _col_sum(d_out * st_in) + _col_sum(dkg_kg)
            for lvl, blk in enumerate(GLA_LEVELS):
                for n in range(TILE // blk):
                    g_scr[h, pl.ds(n * blk + blk // 2 - 1, 1), :] -= _col_sum(dref_scr[h, lvl, n * blk:(n + 1) * blk, :])
            dla_parts.append(_join3(_mm(cumb_ref[...], jnp.concatenate([_split3(g_scr[h]), _split3(db)], axis=0))))
        dz = jnp.concatenate(dla_parts, axis=1) * (1.0 / GLA_TAU) * _sigmoid(-z_all)
        dzb = dz.astype(BF16)
        wgu_b = wgu_ref[...].astype(BF16)
        for h in range(GLA_HEADS):
            kcols = slice(h * GLA_K, (h + 1) * GLA_K)
            dglr_ref[h] = _mm_nt(dzb[:, kcols], wgu_b[:, kcols]).astype(BF16)
        dwgu_ref[...] += _mm(glr_ref[...].T.astype(BF16), dzb)
        dbg_ref[...] += _col_sum(dz)

    row = lambda w: pl.BlockSpec((TILE, w), lambda j: (nt - 1 - j, 0))
    whole = lambda *shape: pl.BlockSpec(shape, lambda j: (0,) * len(shape))
    return _call(
        body, "gla_bwd", grid=(nt,),
        out_shape=[jax.ShapeDtypeStruct((t_rows, GLA_HEADS * GLA_K), BF16), jax.ShapeDtypeStruct((t_rows, GLA_HEADS * GLA_K), BF16),
                   jax.ShapeDtypeStruct((t_rows, GLA_W), BF16), jax.ShapeDtypeStruct((t_rows, GLA_W), BF16),
                   jax.ShapeDtypeStruct((GLA_HEADS, t_rows, LANES), BF16), jax.ShapeDtypeStruct((LANES, GLA_HEADS * GLA_K), F32),
                   jax.ShapeDtypeStruct((1, GLA_HEADS * GLA_K), F32), jax.ShapeDtypeStruct((1, GLA_W), F32)],
        in_specs=_proj_specs(("gq", "gk", "gv", "gg"), 1, lambda j: (0, nt - 1 - j)) + [row(LANES),
                  whole(LANES, GLA_HEADS * GLA_K), whole(1, GLA_HEADS * GLA_K), whole(1, GLA_W),
                  whole(2 * N_TERMS, TILE, TILE), whole(2 * TILE, TILE), whole(TILE, 2 * TILE), row(GLA_W), row(GLA_W),
                  pl.BlockSpec((GLA_HEADS, None, GLA_V, GLA_K), lambda j: (0, nt - 1 - j, 0, 0)),
                  pl.BlockSpec((GLA_HEADS, TILE, TILE), lambda j: (0, nt - 1 - j, 0))],
        out_specs=[row(GLA_HEADS * GLA_K), row(GLA_HEADS * GLA_K), row(GLA_W), row(GLA_W),
                   pl.BlockSpec((GLA_HEADS, TILE, LANES), lambda j: (0, nt - 1 - j, 0)), whole(LANES, GLA_HEADS * GLA_K),
                   whole(1, GLA_HEADS * GLA_K), whole(1, GLA_W)],
        scratch_shapes=[pltpu.VMEM((GLA_HEADS, GLA_V, GLA_K), F32), pltpu.VMEM((GLA_HEADS, TILE, GLA_K), F32),
                        pltpu.VMEM((GLA_HEADS, len(GLA_LEVELS), TILE, GLA_K), F32),
                        pltpu.VMEM((GLA_HEADS, len(GLA_LEVELS), TILE, GLA_K), F32)],
        compiler_params=_params(("arbitrary",)),
    )(proj, proj, proj, proj, glr, wgu_pad, b_gate, gain, masks.astype(BF16), cum_fwd, cum_bwd, o_raw, do_gla, states, a_t)


def _merge_fwd_bwd(o_ret, o_gla, proj, x, target, g_final, w_br, w_bg, w_out):
    t_rows = x.shape[0] + TILE
    nt = t_rows // TILE

    def body(oret_ref, ogla_ref, mr_ref, mg_ref, h0_ref, tgt_ref, gf_ref, wbr_hbm, wbg_hbm, wout_hbm,
             dh1_ref, dmr_ref, dmg_ref, doret_ref, dogla_ref, loss_ref, dgf_ref, dwbr_hbm, dwbg_hbm, dwout_hbm,
             wbr, wbg, wout, abr, abg, aout, sem):
        i = pl.program_id(0)

        @pl.when(i == 0)
        def _():
            cps = [pltpu.make_async_copy(s, d, sem.at[n]) for n, (s, d) in enumerate(((wbr_hbm, wbr), (wbg_hbm, wbg), (wout_hbm, wout)))]
            for cp in cps:
                cp.start()
            abr[...] = jnp.zeros_like(abr)
            abg[...] = jnp.zeros_like(abg)
            aout[...] = jnp.zeros_like(aout)
            loss_ref[...] = jnp.zeros_like(loss_ref)
            dgf_ref[...] = jnp.zeros_like(dgf_ref)
            for cp in cps:
                cp.wait()
            dh1_ref[...] = jnp.zeros_like(dh1_ref)
            dmr_ref[...] = jnp.zeros_like(dmr_ref)
            dmg_ref[...] = jnp.zeros_like(dmg_ref)
            doret_ref[...] = jnp.zeros_like(doret_ref)
            dogla_ref[...] = jnp.zeros_like(dogla_ref)

        @pl.when(i > 0)
        def _():
            oret, ogla = oret_ref[...], ogla_ref[...]
            br, bg = _mm(oret, wbr[...]), _mm(ogla, wbg[...])
            sr, sg = _sigmoid(_cols(mr_ref).astype(F32)), _sigmoid(_cols(mg_ref).astype(F32))
            mb = (sr * br + sg * bg).astype(BF16)
            h1 = h0_ref[...] + _mm(mb, wout[...])
            r2 = lax.rsqrt(_row_mean(h1 * h1) + EPS)
            hn = h1 * r2
            gf = gf_ref[...]
            diff = hn * gf - tgt_ref[...]
            loss_ref[...] += 0.5 * jnp.sum(_row_mean(diff * diff))
            dy = diff * (1.0 / D_MODEL)
            dgf_ref[...] += _col_sum(dy * hn)
            dyg = dy * gf
            dh1 = r2 * (dyg - hn * _row_mean(dyg * hn))
            dh1_ref[...] = dh1
            dh1b = dh1.astype(BF16)
            dm = _mm_nt(dh1b, wout[...])
            aout[...] += _mm_tn(mb, dh1b)
            dbr = (dm * sr).astype(BF16)
            dbg = (dm * sg).astype(BF16)
            dmr_ref[...] = (dm * br * sr * (1.0 - sr)).astype(BF16)
            dmg_ref[...] = (dm * bg * sg * (1.0 - sg)).astype(BF16)
            doret_ref[...] = _mm_nt(dbr, wbr[...]).astype(BF16)
            dogla_ref[...] = _mm_nt(dbg, wbg[...]).astype(BF16)
            abr[...] += _mm_tn(oret, dbr)
            abg[...] += _mm_tn(ogla, dbg)

        @pl.when(i == nt - 1)
        def _():
            wbr[...] = abr[...].astype(BF16)
            wbg[...] = abg[...].astype(BF16)
            wout[...] = aout[...].astype(BF16)
            pltpu.sync_copy(wbr, dwbr_hbm)
            pltpu.sync_copy(wbg, dwbg_hbm)
            pltpu.sync_copy(wout, dwout_hbm)

    row = lambda w: pl.BlockSpec((TILE, w), lambda i: (i, 0))
    one = lambda w: pl.BlockSpec((1, w), lambda i: (0, 0))
    return _call(
        body, "merge_fwd_bwd", grid=(nt,),
        out_shape=[jax.ShapeDtypeStruct((t_rows, D_MODEL), F32), jax.ShapeDtypeStruct((t_rows, D_MODEL), BF16),
                   jax.ShapeDtypeStruct((t_rows, D_MODEL), BF16), jax.ShapeDtypeStruct((t_rows, RET_W), BF16),
                   jax.ShapeDtypeStruct((t_rows, GLA_W), BF16), jax.ShapeDtypeStruct((1, LANES), F32),
                   jax.ShapeDtypeStruct((1, D_MODEL), F32), jax.ShapeDtypeStruct((RET_W, D_MODEL), BF16),
                   jax.ShapeDtypeStruct((GLA_W, D_MODEL), BF16), jax.ShapeDtypeStruct((D_MODEL, D_MODEL), BF16)],
        in_specs=[row(RET_W), row(GLA_W)] + _proj_specs(("mr", "mg"), 1, lambda i: (0, i)) + [_x_spec(), _x_spec(), one(D_MODEL), ANY, ANY, ANY],
        out_specs=[row(D_MODEL), row(D_MODEL), row(D_MODEL), row(RET_W), row(GLA_W), one(LANES), one(D_MODEL), ANY, ANY, ANY],
        scratch_shapes=[pltpu.VMEM((RET_W, D_MODEL), BF16), pltpu.VMEM((GLA_W, D_MODEL), BF16), pltpu.VMEM((D_MODEL, D_MODEL), BF16),
                        pltpu.VMEM((RET_W, D_MODEL), F32), pltpu.VMEM((GLA_W, D_MODEL), F32), pltpu.VMEM((D_MODEL, D_MODEL), F32),
                        pltpu.SemaphoreType.DMA((3,))],
        compiler_params=_params(("arbitrary",)),
    )(o_ret, o_gla, proj, proj, x, target, g_final, w_br, w_bg, w_out)


def _inproj_bwd_x(dseg, dglr, head, x, dh1, g_norm, slabs, w_glr, chip_partials):
    t_rows = x.shape[0] + TILE
    nt = t_rows // TILE
    ne = len(chip_partials)

    def body(*refs):
        d_refs = refs[:10]
        dglr_ref, head_ref, x_ref, dh1_ref, g_ref, slabs_hbm, wg_hbm = refs[10:17]
        part_refs = refs[17:17 + ne]
        dx_ref, dhead_ref, dgn_ref = refs[17 + ne:20 + ne]
        landed = refs[20 + ne:20 + 2 * ne]
        w_vm, wg_vm, edge_vm, sem = refs[20 + 2 * ne:24 + 2 * ne]
        exchange = _Exchange(part_refs, landed, refs[24 + 2 * ne:], among_chips=True)

        @pl.when(pl.program_id(0) == 0)
        def _():
            exchange.start()
            dgn_ref[...] = jnp.zeros_like(dgn_ref)
            _load_weight(slabs_hbm, wg_hbm, w_vm, wg_vm, edge_vm, sem)

        @pl.when(pl.program_id(0) == nt - 1)
        def _():
            exchange.finish()

        dglr = dglr_ref[0].astype(F32)
        for h in range(1, GLA_HEADS):
            dglr = dglr + dglr_ref[h].astype(F32)
        du = _mm_nt(dglr.astype(BF16), wg_vm[...])
        for s, d_ref in enumerate(d_refs):
            du = du + _mm_nt(d_ref[...], w_vm[:, SEG_OFF[s]:SEG_OFF[s] + SEG_W[s]])
        x = _tile_rows(head_ref, x_ref)
        r = lax.rsqrt(_row_mean(x * x) + EPS)
        hn = x * r
        dgn_ref[...] += _col_sum(du * hn)
        dug = du * g_ref[...]
        dh0 = dh1_ref[...] + r * (dug - hn * _row_mean(dug * hn))
        dx_ref[...] = dh0

        @pl.when(pl.program_id(0) == 0)
        def _():
            dhead_ref[...] = dh0

    row = lambda w: pl.BlockSpec((TILE, w), lambda i: (i, 0))
    one = pl.BlockSpec((1, D_MODEL), lambda i: (0, 0))
    return _call(
        body, "inproj_bwd_x", grid=(nt,),
        out_shape=[jax.ShapeDtypeStruct((t_rows - TILE, D_MODEL), F32), jax.ShapeDtypeStruct((TILE, D_MODEL), F32),
                   jax.ShapeDtypeStruct((1, D_MODEL), F32)] + [jax.ShapeDtypeStruct(a.shape, a.dtype) for a in chip_partials],
        in_specs=[row(w) for w in SEG_W] + [pl.BlockSpec((GLA_HEADS, TILE, LANES), lambda i: (0, i, 0)),
                                            _head_spec(), _x_spec(), row(D_MODEL), one, ANY, ANY] + [ANY] * ne,
        out_specs=[_x_spec(), _head_spec(), one] + [ANY] * ne,
        scratch_shapes=W_SCRATCH() + _exchange_sems(ne, N_CHIP),
        compiler_params=_params(("arbitrary",)),
    )(*[dseg[n] for n in SEG_NAMES], dglr, head, x, dh1, g_norm, slabs, w_glr, *chip_partials)


W_TILE = 512


def _inproj_bwd_w(ut, dseg, dglr, row_sends):
    nt = ut.shape[0]
    t_rows = nt * TILE
    kc = 3 if nt % 3 == 0 else 1
    tiles = [(s, c) for s in range(len(SEG_W)) for c in range(0, SEG_W[s], W_TILE)]
    bpt = W_TILE // LANES
    nr = len(row_sends)
    n = 1 + nr
    last_tile = [(SLAB_BLK0[d] + SLAB_BLOCKS - 1) // bpt for d in range(N_DEV)]

    def body(ut_hbm, *refs):
        d_refs, dglr_hbm, row_refs = refs[:10], refs[10], refs[11:11 + nr]
        out_hbm, oglr_ref, sib = refs[11 + nr], refs[12 + nr], refs[13 + nr:13 + nr + n]
        ut_vm, dbuf, obuf, acc, gbuf, sem, send_sems, recv_sems = refs[13 + nr + n:]
        x, y, core = _position()

        def handover(d, k, landed=False):
            q = d // 2
            src = out_hbm.at[pl.ds(SLAB_BLK0[d], SLAB_BLOCKS)] if k == 0 else row_refs[k - 1].at[d]
            return pltpu.make_async_remote_copy(src_ref=sib[k].at[q] if landed else src, dst_ref=sib[k].at[q],
                                                send_sem=send_sems.at[n * q + k], recv_sem=recv_sems.at[n * q + k],
                                                device_id=(x, y, 1 - core), device_id_type=MESH)

        def for_sibling(d, ks, fn):
            @pl.when(d % 2 != core)
            def _():
                for k in ks:
                    fn(handover(d, k))

        for d in range(N_DEV):
            for_sibling(d, range(1, n), lambda cp: cp.start())

        def fetch(i):
            s, c = tiles[i]
            return pltpu.make_async_copy(d_refs[s].at[:, pl.ds(c, W_TILE)], dbuf.at[i % 2], sem.at[1 + i % 2])

        def contract(rhs_refs, width):
            acc[:, :width] = jnp.zeros((D_MODEL, width), F32)

            def step(k, carry):
                part = None
                for j in range(kc):
                    kk = k * kc + j
                    for rhs_ref in rhs_refs:
                        prod = _mm(ut_vm[kk], rhs_ref[pl.ds(pl.multiple_of(kk * TILE, TILE), TILE), :])
                        part = prod if part is None else part + prod
                acc[:, :width] += part
                return carry

            lax.fori_loop(0, nt // kc, step, 0)
            return acc[:, :width]

        load_ut = pltpu.make_async_copy(ut_hbm, ut_vm, sem.at[0])
        load_glr = pltpu.make_async_copy(dglr_hbm, gbuf, sem.at[5])
        load_ut.start()
        load_glr.start()
        fetch(0).start()
        load_ut.wait()
        stores = {}

        def stored(i):
            stores[i].wait()
            for d in range(N_DEV):
                if last_tile[d] == i:
                    for_sibling(d, [0], lambda cp: cp.start())

        for i, (s, c) in enumerate(tiles):
            if i + 1 < len(tiles):
                fetch(i + 1).start()
            fetch(i).wait()
            if i >= 2:
                stored(i - 2)
            total = contract([dbuf.at[i % 2]], W_TILE)
            for j in range(bpt):
                obuf[i % 2, j] = total[:, j * LANES:(j + 1) * LANES].astype(BF16)
            blk0 = (SEG_OFF[s] + c) // LANES
            stores[i] = pltpu.make_async_copy(obuf.at[i % 2], out_hbm.at[pl.ds(blk0, bpt)], sem.at[3 + i % 2])
            stores[i].start()
        for i in range(max(0, len(tiles) - 2), len(tiles)):
            stored(i)
        load_glr.wait()
        head_sum = gbuf[0].astype(F32)
        for h in range(1, GLA_HEADS):
            head_sum = head_sum + gbuf[h].astype(F32)
        gbuf[0] = head_sum.astype(BF16)
        oglr_ref[...] = contract([gbuf.at[0]], LANES)
        for q in range(N_CHIP):
            for k in range(n):
                handover(2 * q, k, landed=True).wait_recv()
        for d in range(N_DEV):
            for_sibling(d, range(n), lambda cp: cp.wait_send())

    outs = _call(
        body, "inproj_bwd_w",
        out_shape=[jax.ShapeDtypeStruct((AL_COLS // LANES, D_MODEL, LANES), BF16), jax.ShapeDtypeStruct((D_MODEL, LANES), F32),
                   jax.ShapeDtypeStruct((N_CHIP, SLAB_BLOCKS, D_MODEL, LANES), BF16)]
                  + [jax.ShapeDtypeStruct((N_CHIP, *r.shape[1:]), BF16) for r in row_sends],
        in_specs=[ANY] * (12 + nr), out_specs=[ANY, pl.BlockSpec(memory_space=pltpu.VMEM)] + [ANY] * n,
        scratch_shapes=[pltpu.VMEM((nt, D_MODEL, TILE), BF16), pltpu.VMEM((2, t_rows, W_TILE), BF16),
                        pltpu.VMEM((2, bpt, D_MODEL, LANES), BF16), pltpu.VMEM((D_MODEL, W_TILE), F32),
                        pltpu.VMEM((GLA_HEADS, t_rows, LANES), BF16), pltpu.SemaphoreType.DMA((6,)),
                        pltpu.SemaphoreType.DMA((n * N_CHIP,)), pltpu.SemaphoreType.DMA((n * N_CHIP,))],
        compiler_params=_params(),
    )(ut, *[dseg[n_] for n_ in SEG_NAMES], dglr, *row_sends)
    return outs[0], outs[1], outs[2], outs[3:]


def _position():
    x, y, c = lax.axis_index("x"), lax.axis_index("y"), lax.axis_index("c")
    return x, y, c


def _index(px, py, pc):
    return 4 * px + 2 * py + pc


def _all_gather(arrs, name):
    n = len(arrs)

    def body(*refs):
        ins, outs = refs[:n], refs[n:2 * n]
        send_sems, recv_sems, local_sems = refs[2 * n:]
        x, y, c = _position()
        me, sibling = (x, y, c), (x, y, 1 - c)
        chips = [(1 - x, y), (x, 1 - y), (1 - x, 1 - y)]

        def copy(a, k, block, to, src=None):
            dst = outs[a].at[_index(*block)]
            return pltpu.make_async_remote_copy(src_ref=dst if src is None else src, dst_ref=dst,
                                                send_sem=send_sems.at[7 * a + k], recv_sem=recv_sems.at[7 * a + k],
                                                device_id=to, device_id_type=MESH)

        def relay(a, j):
            return copy(a, 3, (*chips[j], c), (*chips[1 - j], c))

        mine = [pltpu.make_async_copy(ins[a], outs[a].at[_index(*me)], local_sems.at[a]) for a in range(n)]
        for cp in mine:
            cp.start()
        first = []
        for a in range(n):
            first.append(copy(a, 0, me, sibling, src=ins[a]))
            first += [copy(a, 1 + j, me, (*chips[j], c), src=ins[a]) for j in range(2)]
        for cp in first:
            cp.start()
        passed = []
        for j in range(3):
            for a in range(n):
                copy(a, 1 + j, (*chips[j], c), me).wait_recv()
                cp = copy(a, 4 + j, (*chips[j], c), sibling)
                cp.start()
                passed.append(cp)
            if j < 2:
                @pl.when(c == j)
                def _():
                    for a in range(n):
                        relay(a, j).start()
        for a in range(n):
            copy(a, 0, sibling, me).wait_recv()
            for j in range(3):
                copy(a, 4 + j, (*chips[j], 1 - c), me).wait_recv()
        for cp in first + passed:
            cp.wait_send()
        for j in range(2):
            @pl.when(c == j)
            def _():
                for a in range(n):
                    relay(a, j).wait_send()
        for cp in mine:
            cp.wait()

    return _call(
        body, name,
        out_shape=[jax.ShapeDtypeStruct((N_DEV, *a.shape), a.dtype) for a in arrs],
        in_specs=[ANY] * n, out_specs=[ANY] * n,
        scratch_shapes=[pltpu.SemaphoreType.DMA((7 * n,)), pltpu.SemaphoreType.DMA((7 * n,)), pltpu.SemaphoreType.DMA((n,))],
    )(*arrs)


N_CHIP = N_DEV // 2


def _slab_block0(owner):
    step = SLAB_BLK0[1]
    assert all(SLAB_BLK0[d] == step * d - (d == N_DEV - 1) for d in range(N_DEV))
    return step * owner - jnp.where(owner == N_DEV - 1, 1, 0)


def _add_bf16(c_ref, a_ref, b_ref, o_ref):
    o_ref[...] = (a_ref[...].astype(F32) + b_ref[...].astype(F32)).astype(BF16)


def _chip_partial_slab(dw_blocks, sib, core):
    blk = pl.BlockSpec((None, SLAB_BLOCKS, D_MODEL, LANES), lambda q, c_ref: (q, 0, 0, 0))
    return _call(
        functools.partial(_add_bf16), "chip_partial_w_in", out_shape=jax.ShapeDtypeStruct(sib.shape, BF16),
        grid_spec=pltpu.PrefetchScalarGridSpec(
            num_scalar_prefetch=1, grid=(N_CHIP,),
            in_specs=[pl.BlockSpec((pl.Element(SLAB_BLOCKS), pl.Element(D_MODEL), pl.Element(LANES)),
                                   lambda q, c_ref: (_slab_block0(2 * q + c_ref[0]), 0, 0)), blk],
            out_specs=blk),
        compiler_params=_params(("arbitrary",)),
    )(core, dw_blocks, sib)


def _chip_partial_rows(send, sib, core, name):
    rows, cols = send.shape[1:]
    blk = pl.BlockSpec((None, rows, cols), lambda q, c_ref: (q, 0, 0))
    return _call(
        functools.partial(_add_bf16), name, out_shape=jax.ShapeDtypeStruct(sib.shape, BF16),
        grid_spec=pltpu.PrefetchScalarGridSpec(
            num_scalar_prefetch=1, grid=(N_CHIP,),
            in_specs=[pl.BlockSpec((None, rows, cols), lambda q, c_ref: (2 * q + c_ref[0], 0, 0)), blk], out_specs=blk),
        compiler_params=_params(("arbitrary",)),
    )(core, send, sib)


def _exchange_sems(n_arrays, n_peers):
    return [pltpu.SemaphoreType.DMA((n_arrays * n_peers,)), pltpu.SemaphoreType.DMA((n_arrays * n_peers,)),
            pltpu.SemaphoreType.DMA((n_arrays,))]


class _Exchange:
    def __init__(self, srcs, dsts, sems, among_chips):
        self.arrs = list(zip(srcs, dsts))
        self.n = len(self.arrs)
        self.send_sems, self.recv_sems, self.local_sems = sems
        self.among_chips = among_chips
        x, y, c = _position()
        self.c = c
        self.me = 2 * x + y if among_chips else _index(x, y, c)
        self.n_peers = N_CHIP if among_chips else N_DEV

    def _device(self, p):
        return (p // 2, p % 2, self.c) if self.among_chips else (p // 4, (p // 2) % 2, p % 2)

    def _src(self, k, p):
        src = self.arrs[k][0]
        return src.at[p] if self.among_chips else src

    def _mine(self):
        return [pltpu.make_async_copy(self._src(k, self.me), self.arrs[k][1].at[self.me], self.local_sems.at[k]) for k in range(self.n)]

    def _copy(self, p, k, landing):
        return pltpu.make_async_remote_copy(
            src_ref=self._src(k, p), dst_ref=self.arrs[k][1].at[landing], send_sem=self.send_sems.at[self.n * p + k],
            recv_sem=self.recv_sems.at[self.n * landing + k], device_id=self._device(p), device_id_type=MESH)

    def _others(self, fn):
        for p in range(self.n_peers):
            @pl.when(p != self.me)
            def _():
                for k in range(self.n):
                    fn(p, k)

    def start(self):
        for cp in self._mine():
            cp.start()
        self._others(lambda p, k: self._copy(p, k, self.me).start())

    def finish(self):
        self._others(lambda p, k: self._copy(p, k, p).wait_recv())
        self._others(lambda p, k: self._copy(p, k, self.me).wait_send())
        for cp in self._mine():
            cp.wait()


def _adamw(g, w, m, v):
    m_new = ADAM_B1 * m + (1.0 - ADAM_B1) * g
    v_new = ADAM_B2 * v + (1.0 - ADAM_B2) * (g * g)
    m_hat = m_new / (1.0 - ADAM_B1 ** ADAM_STEP)
    v_hat = v_new / (1.0 - ADAM_B2 ** ADAM_STEP)
    delta = -ADAM_LR * (m_hat / (jnp.sqrt(v_hat) + ADAM_EPS) + ADAM_WD * w)
    return delta, m_new, v_new


def _sum_partials(p_ref):
    g = p_ref[0].astype(F32)
    for d in range(1, p_ref.shape[0]):
        g = g + p_ref[d].astype(F32)
    return g


def _reduce_adam(parts, w, m, v, name, block_rows, row_off=0):
    rows, cols = w.shape
    off = row_off // block_rows

    def body(p_ref, w_ref, m_ref, v_ref, g_ref, d_ref, mo_ref, vo_ref):
        g = _sum_partials(p_ref)
        g_ref[...] = g
        d_ref[...], mo_ref[...], vo_ref[...] = _adamw(g, w_ref[...], m_ref[...], v_ref[...])

    blk = pl.BlockSpec((block_rows, cols), lambda i: (i, 0))
    return _call(
        body, name, grid=(rows // block_rows,),
        out_shape=[jax.ShapeDtypeStruct((rows, cols), F32)] * 4,
        in_specs=[pl.BlockSpec((parts.shape[0], block_rows, cols), lambda i: (0, i + off, 0)), blk, blk, blk],
        out_specs=[blk] * 4,
        compiler_params=_params(("arbitrary",)),
    )(parts, w, m, v)


def _reduce_adam_slab(parts, glr, w_t, m_t, v_t, me):
    cols, rows = w_t.shape
    shift = jnp.asarray(SLAB_SHIFT, jnp.int32)[me]
    glr_at = jnp.where(me == GLR_DEV, GLR_LOCAL, cols).astype(jnp.int32)

    def body(s_ref, p_ref, glr_ref, w_ref, m_ref, v_ref, g_ref, d_ref, mo_ref, vo_ref, slab_t):
        shift, glr_at = s_ref[0], s_ref[1]
        tall = jnp.concatenate([_sum_partials(p_ref.at[:, j]).T for j in range(SLAB_BLOCKS)], axis=0)
        before = pltpu.roll(tall, SLAB_W - shift, 0)
        after = pltpu.roll(tall, lax.rem(SLAB_W - shift + GLA_RANK, SLAB_W), 0)
        wide = jnp.concatenate([glr_ref[...].T, jnp.zeros((SLAB_W - LANES, LANES), F32)], axis=0)
        placed = pltpu.roll(wide, lax.rem(glr_at, SLAB_W), 0)
        row = lax.broadcasted_iota(jnp.int32, (SLAB_W, LANES), 0)
        slab_t[...] = jnp.where(row < glr_at, before, jnp.where(row < glr_at + GLA_RANK, placed, after))
        g = slab_t[pl.ds(0, cols), :]
        g_ref[...] = g
        d_ref[...], mo_ref[...], vo_ref[...] = _adamw(g, w_ref[...], m_ref[...], v_ref[...])

    blk = pl.BlockSpec((cols, LANES), lambda i, s: (0, i))
    return _call(
        body, "adam_w_in", out_shape=[jax.ShapeDtypeStruct((cols, rows), F32)] * 4,
        grid_spec=pltpu.PrefetchScalarGridSpec(
            num_scalar_prefetch=1, grid=(rows // LANES,),
            in_specs=[pl.BlockSpec((parts.shape[0], SLAB_BLOCKS, LANES, LANES), lambda i, s: (0, 0, i, 0)),
                      pl.BlockSpec((LANES, LANES), lambda i, s: (i, 0)), blk, blk, blk],
            out_specs=[blk] * 4, scratch_shapes=[pltpu.VMEM((SLAB_W, LANES), F32)]),
        compiler_params=_params(("arbitrary",)),
    )(jnp.stack([shift, glr_at]), parts, glr, w_t, m_t, v_t)


def _reduce_small(parts):
    def body(p_ref, o_ref):
        o_ref[...] = _sum_partials(p_ref)

    return _call(body, "reduce_small", out_shape=jax.ShapeDtypeStruct(parts.shape[1:], F32))(parts)


def _adam_small(g, w, m, v):
    def body(g_ref, w_ref, m_ref, v_ref, d_ref, mo_ref, vo_ref):
        d_ref[...], mo_ref[...], vo_ref[...] = _adamw(g_ref[...], w_ref[...], m_ref[...], v_ref[...])

    return _call(body, "adam_small", out_shape=[jax.ShapeDtypeStruct(g.shape, F32)] * 3)(g, w, m, v)


def _pack_rows(arrs):
    rows = []
    for a in arrs:
        flat = a.reshape(-1).astype(F32)
        pad = (-flat.shape[0]) % LANES
        rows.append(jnp.pad(flat, (0, pad)).reshape(-1, LANES))
    packed = jnp.concatenate(rows, axis=0)
    return jnp.pad(packed, ((0, (-packed.shape[0]) % 8), (0, 0)))


def _unpack_rows(packed, shapes):
    out, r = [], 0
    for shp in shapes:
        size = 1
        for s in shp:
            size *= s
        nrows = -(-size // LANES)
        out.append(packed[r:r + nrows].reshape(-1)[:size].reshape(shp))
        r += nrows
    return out


def _shard_to_slab(shard, d):
    glr = jnp.zeros((D_MODEL, GLA_RANK), shard.dtype)
    if d == GLR_DEV:
        glr = shard[:, GLR_LOCAL:GLR_LOCAL + GLA_RANK]
        shard = jnp.concatenate([shard[:, :GLR_LOCAL], shard[:, GLR_LOCAL + GLA_RANK:]], axis=1)
    return jnp.pad(shard, ((0, 0), (SLAB_SHIFT[d], SLAB_W - SLAB_SHIFT[d] - shard.shape[1]))), glr


def kernel(x, meta_tokens, norm_gain, w_in, w_gate_up, b_gate, ret_norm_gain, gla_norm_gain, w_branch_ret, w_branch_gla, w_out, final_norm_gain, loss_target, m_meta_tokens, m_norm_gain, m_w_in, m_w_gate_up, m_b_gate, m_ret_norm_gain, m_gla_norm_gain, m_w_branch_ret, m_w_branch_gla, m_w_out, m_final_norm_gain, v_meta_tokens, v_norm_gain, v_w_in, v_w_gate_up, v_b_gate, v_ret_norm_gain, v_gla_norm_gain, v_w_branch_ret, v_w_branch_gla, v_w_out, v_final_norm_gain):
    xi, yi, ci = _position()
    me = _index(xi, yi, ci)
    seq = x.shape[1]
    t_rows = seq + TILE
    in_shard = w_in.shape[2]
    gu_shard = w_gate_up.shape[2]
    meta_shard = meta_tokens.shape[1]
    ret_rows, gla_rows, out_rows = w_branch_ret.shape[1], w_branch_gla.shape[1], w_out.shape[1]

    assert in_shard == IN_SHARD
    slab_local, glr_local = lax.switch(me, [functools.partial(_shard_to_slab, d=d) for d in range(N_DEV)], w_in[0])
    small_local = jnp.concatenate([meta_tokens, jnp.pad(w_gate_up[0], ((0, 0), (0, LANES - gu_shard))),
                                   glr_local.reshape(-1, LANES)], axis=0)
    slabs, g_small = _all_gather([slab_local.astype(BF16), small_local], "all_gather_shards")
    n_small = N_META + GLA_RANK
    w_glr = jnp.pad(g_small[GLR_DEV, n_small:].reshape(D_MODEL, GLA_RANK), ((0, 0), (0, LANES - GLA_RANK))).astype(BF16)
    meta_full = jnp.transpose(g_small[:, :N_META, :], (1, 0, 2)).reshape(N_META, D_MODEL)
    wgu_full = jnp.transpose(g_small[:, N_META:n_small, :gu_shard], (1, 0, 2)).reshape(GLA_RANK, GLA_HEADS * GLA_K)
    wgu_pad = jnp.pad(wgu_full, ((0, LANES - GLA_RANK), (0, 0)))

    pos = jnp.arange(t_rows, dtype=F32) - float(PAD_ROWS)
    half = RET_QK // 2
    inv = ROPE_BASE ** (-jnp.arange(half, dtype=F32) / half)
    ang = pos[:, None] * inv[None, :]
    cos, sin = jnp.cos(ang), jnp.sin(ang)
    lg = jnp.log1p(-(2.0 ** (-5.0 - jnp.arange(RET_HEADS, dtype=F32))))

    head = jnp.concatenate([jnp.zeros((PAD_ROWS, D_MODEL), F32), meta_full], axis=0)
    ut, proj, glr = _inproj_tiles(head, x[0], norm_gain, slabs, w_glr)
    o_ret_raw, o_ret, ret_states, (g_br, g_bg, g_o) = _ret_fwd(
        proj, cos, sin, ret_norm_gain, lg, [w_branch_ret[0].astype(BF16), w_branch_gla[0].astype(BF16), w_out[0].astype(BF16)])
    w_br, w_bg, w_o = g_br.reshape(RET_W, D_MODEL), g_bg.reshape(GLA_W, D_MODEL), g_o.reshape(D_MODEL, D_MODEL)
    masks, cum_fwd, cum_bwd = _gla_tables()
    o_gla_raw, o_gla, gla_states, gla_scores_t = _gla_fwd(proj, glr, wgu_pad, b_gate, gla_norm_gain, masks, cum_fwd)
    (dh1, d_mr, d_mg, do_ret, do_gla, loss_part, d_gfinal, dw_br, dw_bg, dw_o) = _merge_fwd_bwd(
        o_ret, o_gla, proj, x[0], loss_target[0], final_norm_gain.reshape(1, D_MODEL), w_br, w_bg, w_o)

    d_rq, d_rk, d_rv, d_rg, d_gret = _ret_bwd(proj, cos, sin, ret_norm_gain, lg, o_ret_raw, do_ret, ret_states)
    d_gq, d_gk, d_gv, d_gg, dglr_parts, d_wgu, d_bgate, d_ggla = _gla_bwd(
        proj, glr, wgu_pad, b_gate, gla_norm_gain, o_gla_raw, do_gla, gla_states, gla_scores_t, masks, cum_fwd, cum_bwd)
    dseg = dict(rq=d_rq, rk=d_rk, rv=d_rv, rg=d_rg, gq=d_gq, gk=d_gk, gv=d_gv, gg=d_gg, mr=d_mr, mg=d_mg)
    row_sends = [dw_br.reshape(N_DEV, ret_rows, D_MODEL), dw_bg.reshape(N_DEV, gla_rows, D_MODEL),
                 dw_o.reshape(N_DEV, out_rows, D_MODEL)]
    dw_blocks, dw_glr, sib_in, sib_rows = _inproj_bwd_w(ut, dseg, dglr_parts, row_sends)
    core = ci.astype(jnp.int32).reshape(1)
    chip_partials = [_chip_partial_slab(dw_blocks, sib_in, core)] + [
        _chip_partial_rows(send, sib, core, "chip_partial_" + name)
        for send, sib, name in zip(row_sends, sib_rows, ("w_branch_ret", "w_branch_gla", "w_out"))]
    grad_x, d_head, d_gnorm, p_in, p_br, p_bg, p_o = _inproj_bwd_x(
        dseg, dglr_parts, head, x[0], dh1, norm_gain, slabs, w_glr, chip_partials)
    small_shapes = [(N_META, D_MODEL), (1, D_MODEL), (GLA_RANK, GLA_HEADS * GLA_K), (1, GLA_HEADS * GLA_K),
                    (1, RET_W), (1, GLA_W), (1, D_MODEL), (1, LANES), (D_MODEL, GLA_RANK)]
    small_part = _pack_rows([d_head[PAD_ROWS:], d_gnorm, d_wgu[:GLA_RANK], d_bgate, d_gret, d_ggla, d_gfinal, loss_part,
                             dw_glr[:, :GLA_RANK]])
    (p_small,) = _all_gather([small_part], "all_gather_small_partials")

    (g_meta_f, g_gnorm, g_wgu_f, g_bgate, g_gret, g_ggla, g_gfinal, loss_all,
     g_wglr) = _unpack_rows(_reduce_small(p_small), small_shapes)
    g_w_in, d_w_in, nm_w_in, nv_w_in = [a.T for a in _reduce_adam_slab(
        p_in, jnp.pad(g_wglr, ((0, 0), (0, LANES - GLA_RANK))), w_in[0].T, m_w_in[0].T, v_w_in[0].T, me)]
    rb = gla_rows
    g_w_br, d_w_br, nm_w_br, nv_w_br = _reduce_adam(p_br, w_branch_ret[0], m_w_branch_ret[0], v_w_branch_ret[0], "adam_w_branch_ret", rb)
    g_w_bg, d_w_bg, nm_w_bg, nv_w_bg = _reduce_adam(p_bg, w_branch_gla[0], m_w_branch_gla[0], v_w_branch_gla[0], "adam_w_branch_gla", rb)
    g_w_o, d_w_o, nm_w_o, nv_w_o = _reduce_adam(p_o, w_out[0], m_w_out[0], v_w_out[0], "adam_w_out", rb)
    g_meta = lax.dynamic_slice_in_dim(g_meta_f, me * meta_shard, meta_shard, axis=1)
    g_wgu = lax.dynamic_slice_in_dim(g_wgu_f, me * gu_shard, gu_shard, axis=1)
    s_g = [g_meta, g_gnorm, g_wgu, g_bgate, g_gret, g_ggla, g_gfinal]
    s_w = [meta_tokens, norm_gain, w_gate_up[0], b_gate, ret_norm_gain, gla_norm_gain, final_norm_gain]
    s_m = [m_meta_tokens, m_norm_gain, m_w_gate_up[0], m_b_gate, m_ret_norm_gain, m_gla_norm_gain, m_final_norm_gain]
    s_v = [v_meta_tokens, v_norm_gain, v_w_gate_up[0], v_b_gate, v_ret_norm_gain, v_gla_norm_gain, v_final_norm_gain]
    shapes = [a.shape for a in s_g]
    s_d, s_nm, s_nv = [_unpack_rows(p, shapes) for p in _adam_small(*[_pack_rows(l) for l in (s_g, s_w, s_m, s_v)])]

    loss = loss_all[0, 0]
    grad_x = grad_x[None]

    def order(meta, gnorm, win, wgu, bgate, gret, ggla, wbr, wbg, wo, gfin):
        return (meta, gnorm, win[None], wgu[None], bgate, gret, ggla, wbr[None], wbg[None], wo[None], gfin.reshape(final_norm_gain.shape))

    def small(l):
        return dict(meta=l[0], gnorm=l[1], wgu=l[2], bgate=l[3], gret=l[4], ggla=l[5], gfin=l[6])

    grads = order(win=g_w_in, wbr=g_w_br, wbg=g_w_bg, wo=g_w_o, **small(s_g))
    deltas = order(win=d_w_in, wbr=d_w_br, wbg=d_w_bg, wo=d_w_o, **small(s_d))
    new_m = order(win=nm_w_in, wbr=nm_w_br, wbg=nm_w_bg, wo=nm_w_o, **small(s_nm))
    new_v = order(win=nv_w_in, wbr=nv_w_br, wbg=nv_w_bg, wo=nv_w_o, **small(s_nv))
    return (loss, grad_x, *grads, *deltas, *new_m, *new_v)
```

```python
import functools

import jax
import jax.numpy as jnp
from jax import lax
from jax.experimental import pallas as pl
from jax.experimental.pallas import tpu as pltpu

F32 = jnp.float32
BF16 = jnp.bfloat16

D_MODEL = 1024
N_META = 16
TILE = 256
PAD_ROWS = TILE - N_META
RET_HEADS = 4
RET_QK = 256
RET_V = 512
RET_W = RET_HEADS * RET_V
GLA_HEADS = 4
GLA_K = 128
GLA_V = 256
GLA_W = GLA_HEADS * GLA_V
GLA_RANK = 16
GLA_TAU = 16.0
GLA_CHUNK = 16
ROPE_BASE = 10000.0
EPS = 1e-6
LANES = 128
N_DEV = 8
SEG_NAMES = ("rq", "rk", "rv", "rg", "gq", "gk", "gv", "gg", "mr", "mg")
SEG_W = (1024, 1024, 2048, 2048, 512, 512, 1024, 1024, 1024, 1024)
SEG_OFF = tuple(sum(SEG_W[:i]) for i in range(len(SEG_W)))
AL_COLS = sum(SEG_W)
IN_COLS = AL_COLS + GLA_RANK
GLR_OFF = sum(SEG_W[:8])
IN_SHARD = IN_COLS // N_DEV


def _aligned_col(c):
    assert c <= GLR_OFF or c >= GLR_OFF + GLA_RANK
    return c if c <= GLR_OFF else c - GLA_RANK


SLAB_BOUND = tuple(_aligned_col(IN_SHARD * d) for d in range(N_DEV + 1))
SLAB_BLK0 = tuple(b // LANES for b in SLAB_BOUND[:-1])
SLAB_SHIFT = tuple(b % LANES for b in SLAB_BOUND[:-1])
SLAB_BLOCKS = max(-(-SLAB_BOUND[d + 1] // LANES) - SLAB_BLK0[d] for d in range(N_DEV))
SLAB_W = SLAB_BLOCKS * LANES
GLR_DEV = GLR_OFF // IN_SHARD
GLR_LOCAL = GLR_OFF - GLR_DEV * IN_SHARD
assert all(SLAB_BLK0[d] + SLAB_BLOCKS <= AL_COLS // LANES for d in range(N_DEV))
VMEM_LIMIT = 58 * 1024 * 1024
ADAM_LR, ADAM_B1, ADAM_B2, ADAM_EPS, ADAM_WD, ADAM_STEP = 0.001, 0.9, 0.999, 1e-08, 0.01, 10
ANY = pl.BlockSpec(memory_space=pl.ANY)
MESH = pl.DeviceIdType.MESH


def _call(body, name, **kw):
    return pl.pallas_call(body, name=name, **kw)


def _params(sem=None):
    return pltpu.CompilerParams(dimension_semantics=sem, vmem_limit_bytes=VMEM_LIMIT)


def _mm(a, b):
    return jnp.dot(a, b, preferred_element_type=F32)


def _mm_nt(a, b):
    return lax.dot_general(a, b, (((1,), (1,)), ((), ())), preferred_element_type=F32)


def _mm_tn(a, b):
    return lax.dot_general(a, b, (((0,), (0,)), ((), ())), preferred_element_type=F32)


def _sigmoid(x):
    return 1.0 / (1.0 + jnp.exp(-x))


def _rope(t, cos, sin):
    half = t.shape[-1] // 2
    t1, t2 = t[:, :half], t[:, half:]
    return jnp.concatenate([t1 * cos - t2 * sin, t2 * cos + t1 * sin], axis=-1)


def _rope_bwd(g, cos, sin):
    half = g.shape[-1] // 2
    g1, g2 = g[:, :half], g[:, half:]
    return jnp.concatenate([g1 * cos + g2 * sin, g2 * cos - g1 * sin], axis=-1)


def _row_mean(x):
    return jnp.mean(x, axis=-1, keepdims=True)


def _col_sum(x):
    return jnp.sum(x, axis=0, keepdims=True)


def _tile_rows(head_ref, x_ref):
    return jnp.where(pl.program_id(0) == 0, head_ref[...], x_ref[...])


def _head_spec():
    return pl.BlockSpec((TILE, D_MODEL), lambda i: (0, 0))


def _x_spec():
    return pl.BlockSpec((TILE, D_MODEL), lambda i: (jnp.maximum(i - 1, 0), 0))


def _slab_plan():
    interior, shared = [], []
    for d in range(N_DEV):
        lo, hi = -(-SLAB_BOUND[d] // LANES), SLAB_BOUND[d + 1] // LANES
        interior.append((d, LANES * (lo - SLAB_BLK0[d]), LANES * lo, LANES * (hi - lo)))
        if d + 1 < N_DEV and SLAB_BOUND[d + 1] % LANES:
            shared.append((hi, d, hi - SLAB_BLK0[d]))
    return interior, shared


W_SCRATCH = lambda: [pltpu.VMEM((D_MODEL, AL_COLS), BF16), pltpu.VMEM((D_MODEL, LANES), BF16),
                     pltpu.VMEM((2 * (N_DEV - 1), D_MODEL, LANES), BF16), pltpu.SemaphoreType.DMA((3 * N_DEV,))]


def _load_weight(slabs_hbm, wg_hbm, w_vm, wg_vm, edge_vm, sem):
    interior, shared = _slab_plan()
    copies = [pltpu.make_async_copy(wg_hbm, wg_vm, sem.at[0])]
    for d, src, dst, width in interior:
        copies.append(pltpu.make_async_copy(slabs_hbm.at[d, :, pl.ds(src, width)], w_vm.at[:, pl.ds(dst, width)], sem.at[1 + d]))
    for n, (_, d, blk) in enumerate(shared):
        copies.append(pltpu.make_async_copy(slabs_hbm.at[d, :, pl.ds(LANES * blk, LANES)], edge_vm.at[2 * n], sem.at[1 + N_DEV + 2 * n]))
        copies.append(pltpu.make_async_copy(slabs_hbm.at[d + 1, :, pl.ds(0, LANES)], edge_vm.at[2 * n + 1], sem.at[2 + N_DEV + 2 * n]))
    for cp in copies:
        cp.start()
    for cp in copies:
        cp.wait()
    for n, (blk, _, _) in enumerate(shared):
        w_vm[:, LANES * blk:LANES * (blk + 1)] = edge_vm[2 * n] + edge_vm[2 * n + 1]


def _proj_specs(names, n_units, where):
    specs = []
    for name in names:
        s = SEG_NAMES.index(name)
        nblk = SEG_W[s] // n_units // LANES
        base = SEG_OFF[s] // LANES
        assert base % nblk == 0
        specs.append(pl.BlockSpec((nblk, TILE, LANES), lambda *g, base=base, nblk=nblk: (base // nblk + where(*g)[0], where(*g)[1], 0)))
    return specs


def _cols(ref, unit=0, n_units=1):
    n = ref.shape[0] // n_units
    return ref[unit * n] if n == 1 else jnp.concatenate([ref[unit * n + j] for j in range(n)], axis=1)


def _inproj_tiles(head, x, g_norm, slabs, w_glr):
    t_rows = x.shape[0] + TILE
    nt = t_rows // TILE
    n_blocks = AL_COLS // LANES

    def body(head_ref, x_ref, g_ref, slabs_hbm, wg_hbm, ut_ref, proj_ref, glr_ref, w_vm, wg_vm, edge_vm, sem):
        @pl.when(pl.program_id(0) == 0)
        def _():
            _load_weight(slabs_hbm, wg_hbm, w_vm, wg_vm, edge_vm, sem)

        x = _tile_rows(head_ref, x_ref)
        r = lax.rsqrt(_row_mean(x * x) + EPS)
        u32 = (x * r * g_ref[...]).astype(BF16).astype(F32)
        u = u32.astype(BF16)
        ut_ref[...] = u32.T.astype(BF16)
        for s in range(len(SEG_W)):
            res = _mm(u, w_vm[:, SEG_OFF[s]:SEG_OFF[s] + SEG_W[s]]).astype(BF16)
            for j in range(SEG_W[s] // LANES):
                proj_ref[SEG_OFF[s] // LANES + j] = res[:, j * LANES:(j + 1) * LANES]
        glr_ref[...] = _mm(u, wg_vm[...])

    return _call(
        body, "inproj_fwd_tiles", grid=(nt,),
        out_shape=[jax.ShapeDtypeStruct((nt, D_MODEL, TILE), BF16), jax.ShapeDtypeStruct((n_blocks, t_rows, LANES), BF16),
                   jax.ShapeDtypeStruct((t_rows, LANES), F32)],
        in_specs=[_head_spec(), _x_spec(), pl.BlockSpec((1, D_MODEL), lambda i: (0, 0)), ANY, ANY],
        out_specs=[pl.BlockSpec((None, D_MODEL, TILE), lambda i: (i, 0, 0)), pl.BlockSpec((n_blocks, TILE, LANES), lambda i: (0, i, 0)),
                   pl.BlockSpec((TILE, LANES), lambda i: (i, 0))],
        scratch_shapes=W_SCRATCH(), compiler_params=_params(("arbitrary",)),
    )(head, x, g_norm, slabs, w_glr)


def _ret_decay(lgh):
    i = lax.broadcasted_iota(jnp.int32, (TILE, TILE), 0)
    j = lax.broadcasted_iota(jnp.int32, (TILE, TILE), 1)
    rel = (i - j).astype(F32)
    return jnp.where(rel >= 0, jnp.exp(jnp.maximum(rel, 0.0) * lgh), 0.0)


def _ret_vectors(lgh):
    idx = lax.broadcasted_iota(jnp.int32, (TILE, 1), 0).astype(F32)
    xi = jnp.exp((idx + 1.0) * lgh)
    zeta = jnp.exp((TILE - 1.0 - idx) * lgh)
    gc = jnp.exp(jnp.full((1, 1), float(TILE), F32) * lgh)
    return xi, zeta, gc


def _rope_tables(nt):
    half = RET_QK // 2
    inv = ROPE_BASE ** (-jnp.arange(half, dtype=F32) / half)
    base = (jnp.arange(nt, dtype=F32) * TILE - float(PAD_ROWS))[:, None, None] * inv[None, None, :]
    off = jnp.arange(TILE, dtype=F32)[:, None] * inv[None, :]
    return jnp.cos(base), jnp.sin(base), jnp.cos(off), jnp.sin(off)


def _rope_specs(tile_of):
    return [pl.BlockSpec((None, 1, RET_QK // 2), lambda i: (tile_of(i), 0, 0))] * 2 + [pl.BlockSpec((TILE, RET_QK // 2), lambda i: (0, 0))] * 2


def _rope_angles(cb_ref, sb_ref, co_ref, so_ref):
    cb, sb, co, so = cb_ref[...], sb_ref[...], co_ref[...], so_ref[...]
    return cb * co - sb * so, sb * co + cb * so


def _ret_fwd(proj, rope, gain, lg, row_shards):
    t_rows = proj.shape[1]
    nt = t_rows // TILE
    ns = len(row_shards)

    def body(lg_ref, q_ref, k_ref, v_ref, g_ref, cb_ref, sb_ref, co_ref, so_ref, gain_ref, *rest):
        shard_refs, (oraw_ref, oret_ref, st_ref), gathered = rest[:ns], rest[ns:ns + 3], rest[ns + 3:2 * ns + 3]
        s_acc, dm = rest[2 * ns + 3:2 * ns + 5]
        gather = _Exchange(shard_refs, gathered, rest[2 * ns + 5:], among_chips=False)
        t = pl.program_id(0)

        @pl.when(t == 0)
        def _():
            gather.start()
            s_acc[...] = jnp.zeros_like(s_acc)
            for h in range(RET_HEADS):
                dm[h] = _ret_decay(lg_ref[h])

        @pl.when(t == nt - 1)
        def _():
            gather.finish()

        cos_t, sin_t = _rope_angles(cb_ref, sb_ref, co_ref, so_ref)
        for h in range(RET_HEADS):
            lgh = lg_ref[h]
            q = _rope(_cols(q_ref, h, RET_HEADS).astype(F32), cos_t, sin_t)
            k = _rope(_cols(k_ref, h, RET_HEADS).astype(F32), cos_t, sin_t) * (RET_QK ** -0.5)
            xi, zeta, gc = _ret_vectors(lgh)
            v = _cols(v_ref, h, RET_HEADS)
            s_in = s_acc[h]
            p = (_mm_nt(q.astype(BF16), k.astype(BF16)) * dm[h]).astype(BF16)
            o = _mm(p, v) + _mm((q * xi).astype(BF16), s_in.astype(BF16))
            st_ref[h] = s_in.astype(BF16)
            s_acc[h] = s_in * gc + _mm_tn((k * zeta).astype(BF16), v)
            cols = slice(h * RET_V, (h + 1) * RET_V)
            oraw_ref[:, cols] = o
            oc = o - _row_mean(o)
            n = oc * lax.rsqrt(_row_mean(oc * oc) + EPS) * gain_ref[:, cols]
            g = _cols(g_ref, h, RET_HEADS).astype(F32)
            oret_ref[:, cols] = (n * g * _sigmoid(g)).astype(BF16)

    row = lambda w: pl.BlockSpec((TILE, w), lambda t: (t, 0))
    outs = _call(
        body, "ret_fwd", grid=(nt,),
        out_shape=[jax.ShapeDtypeStruct((t_rows, RET_W), F32), jax.ShapeDtypeStruct((t_rows, RET_W), BF16),
                   jax.ShapeDtypeStruct((RET_HEADS, nt, RET_QK, RET_V), BF16)]
                  + [jax.ShapeDtypeStruct((N_DEV, *a.shape), a.dtype) for a in row_shards],
        in_specs=[pl.BlockSpec(memory_space=pltpu.SMEM)] + _proj_specs(("rq", "rk", "rv", "rg"), 1, lambda t: (0, t)) + _rope_specs(lambda t: t) + [
                  pl.BlockSpec((1, RET_W), lambda t: (0, 0))] + [ANY] * ns,
        out_specs=[row(RET_W), row(RET_W), pl.BlockSpec((RET_HEADS, None, RET_QK, RET_V), lambda t: (0, t, 0, 0))] + [ANY] * ns,
        scratch_shapes=[pltpu.VMEM((RET_HEADS, RET_QK, RET_V), F32), pltpu.VMEM((RET_HEADS, TILE, TILE), F32)] + _exchange_sems(ns, N_DEV),
        compiler_params=_params(("arbitrary",)),
    )(lg, proj, proj, proj, proj, *rope, gain, *row_shards)
    return outs[0], outs[1], outs[2], outs[3:]


def _ret_bwd(proj, rope, gain, lg, o_raw, do_ret, states):
    t_rows = proj.shape[1]
    nt = t_rows // TILE

    def body(lg_ref, q_ref, k_ref, v_ref, g_ref, cb_ref, sb_ref, co_ref, so_ref, gain_ref, oraw_ref, do_ref, st_ref,
             dq_ref, dk_ref, dv_ref, dg_ref, dgain_ref, e_acc, dm):
        @pl.when(pl.program_id(0) == 0)
        def _():
            e_acc[...] = jnp.zeros_like(e_acc)
            for h in range(RET_HEADS):
                dm[h] = _ret_decay(lg_ref[h])
            dgain_ref[...] = jnp.zeros_like(dgain_ref)

        cos_t, sin_t = _rope_angles(cb_ref, sb_ref, co_ref, so_ref)
        for h in range(RET_HEADS):
            lgh = lg_ref[h]
            cols = slice(h * RET_V, (h + 1) * RET_V)
            qcols = slice(h * RET_QK, (h + 1) * RET_QK)
            q = _rope(_cols(q_ref, h, RET_HEADS).astype(F32), cos_t, sin_t)
            k = _rope(_cols(k_ref, h, RET_HEADS).astype(F32), cos_t, sin_t) * (RET_QK ** -0.5)
            xi, zeta, gc = _ret_vectors(lgh)
            v = _cols(v_ref, h, RET_HEADS)
            g = _cols(g_ref, h, RET_HEADS).astype(F32)
            o = oraw_ref[:, cols]
            do = do_ref[:, cols].astype(F32)
            oc = o - _row_mean(o)
            rstd = lax.rsqrt(_row_mean(oc * oc) + EPS)
            xh = oc * rstd
            gain_t = gain_ref[:, cols]
            sg = _sigmoid(g)
            dn = do * (g * sg)
            dg_ref[:, cols] = (do * (xh * gain_t) * (sg * (1.0 + g * (1.0 - sg)))).astype(BF16)
            dgain_ref[:, cols] += _col_sum(dn * xh)
            dxh = dn * gain_t
            dob = (rstd * (dxh - _row_mean(dxh) - xh * _row_mean(dxh * xh))).astype(BF16)
            dmat = dm[h]
            qb, kb = q.astype(BF16), k.astype(BF16)
            p = (_mm_nt(qb, kb) * dmat).astype(BF16)
            dp = (_mm_nt(dob, v) * dmat).astype(BF16)
            s_in = st_ref[h]
            e_in = e_acc[h]
            e_b = e_in.astype(BF16)
            dq = _mm(dp, kb) + _mm_nt(dob, s_in) * xi
            dk = _mm_tn(dp, qb) + _mm_nt(v, e_b) * zeta
            dv_ref[:, cols] = (_mm_tn(p, dob) + _mm((k * zeta).astype(BF16), e_b)).astype(BF16)
            e_acc[h] = e_in * gc + _mm_tn((q * xi).astype(BF16), dob)
            dq_ref[:, qcols] = _rope_bwd(dq, cos_t, sin_t).astype(BF16)
            dk_ref[:, qcols] = (_rope_bwd(dk, cos_t, sin_t) * (RET_QK ** -0.5)).astype(BF16)

    row = lambda w: pl.BlockSpec((TILE, w), lambda j: (nt - 1 - j, 0))
    vec = pl.BlockSpec((1, RET_W), lambda j: (0, 0))
    return _call(
        body, "ret_bwd", grid=(nt,),
        out_shape=[jax.ShapeDtypeStruct((t_rows, RET_HEADS * RET_QK), BF16), jax.ShapeDtypeStruct((t_rows, RET_HEADS * RET_QK), BF16),
                   jax.ShapeDtypeStruct((t_rows, RET_W), BF16), jax.ShapeDtypeStruct((t_rows, RET_W), BF16),
                   jax.ShapeDtypeStruct((1, RET_W), F32)],
        in_specs=[pl.BlockSpec(memory_space=pltpu.SMEM)] + _proj_specs(("rq", "rk", "rv", "rg"), 1, lambda j: (0, nt - 1 - j)) + _rope_specs(lambda j: nt - 1 - j) + [vec,
                  row(RET_W), row(RET_W), pl.BlockSpec((RET_HEADS, None, RET_QK, RET_V), lambda j: (0, nt - 1 - j, 0, 0))],
        out_specs=[row(RET_HEADS * RET_QK), row(RET_HEADS * RET_QK), row(RET_W), row(RET_W), vec],
        scratch_shapes=[pltpu.VMEM((RET_HEADS, RET_QK, RET_V), F32), pltpu.VMEM((RET_HEADS, TILE, TILE), F32)],
        compiler_params=_params(("arbitrary",)),
    )(lg, proj, proj, proj, proj, *rope, gain, o_raw, do_ret, states)


GLA_LEVELS = (32, 64, 128, 256)
N_TERMS = 1 + len(GLA_LEVELS)


def _gla_tables():
    p = jnp.arange(TILE)[:, None]
    r = jnp.arange(TILE)[None, :]
    masks = [(p // GLA_CHUNK == r // GLA_CHUNK) & (r <= p)]
    for blk in GLA_LEVELS:
        masks.append((p // blk == r // blk) & (p % blk >= blk // 2) & (r % blk < blk // 2))
    masks = jnp.stack(masks + [m.T for m in masks]).astype(F32)
    cum_fwd = jnp.concatenate([r <= p, masks[0] > 0], axis=0).astype(BF16)
    cum_bwd = jnp.concatenate([r >= p, masks[N_TERMS] > 0], axis=1).astype(BF16)
    return masks, cum_fwd, cum_bwd


def _split3(x):
    hi = x.astype(BF16)
    rest = x - hi.astype(F32)
    mid = rest.astype(BF16)
    lo = (rest - mid.astype(F32)).astype(BF16)
    return jnp.concatenate([hi, mid, lo], axis=1)


def _join3(y):
    w = y.shape[1] // 3
    return (y[:, 2 * w:] + y[:, w:2 * w]) + y[:, :w]


def _gla_decays(glr_ref, wgu_ref, b_ref, cum_ref):
    z = _mm(glr_ref[...].astype(BF16), wgu_ref[...].astype(BF16)) + b_ref[...]
    la = (jnp.minimum(z, 0.0) - jnp.log(1.0 + jnp.exp(-jnp.abs(z)))) / GLA_TAU
    width = la.shape[1]
    hi = la.astype(BF16)
    rest = la - hi.astype(F32)
    mid = rest.astype(BF16)
    lo = (rest - mid.astype(F32)).astype(BF16)
    y = _mm(cum_ref[...], jnp.concatenate([hi, mid, lo], axis=1))
    gb = (y[:, 2 * width:] + y[:, width:2 * width]) + y[:, :width]
    return z, gb[:TILE], gb[TILE:]


def _gla_prep(h, q_ref, k_ref, g_all, b_all, g_scr, ref_scr):
    cols = slice(h * GLA_K, (h + 1) * GLA_K)
    g, b = g_all[:, cols], b_all[:, cols]
    g_scr[h] = g
    factors = [(jnp.exp(b), jnp.exp(-b))]
    for lvl, blk in enumerate(GLA_LEVELS):
        for n in range(TILE // blk):
            ref_scr[h, lvl, n * blk:(n + 1) * blk, :] = jnp.broadcast_to(g_scr[h, pl.ds(n * blk + blk // 2 - 1, 1), :], (blk, GLA_K))
        x = g - ref_scr[h, lvl]
        factors.append((jnp.exp(jnp.minimum(x, 0.0)), jnp.exp(jnp.minimum(-x, 0.0))))
    g_last = g_scr[h, pl.ds(TILE - 1, 1), :]
    q = _cols(q_ref, h, GLA_HEADS).astype(F32) * (GLA_K ** -0.5)
    k = _cols(k_ref, h, GLA_HEADS).astype(F32)
    return q, k, factors, jnp.exp(g), jnp.exp(g_last), jnp.exp(g_last - g)


def _gla_scores(q, k, factors, m_ref):
    a = jnp.zeros((TILE, TILE), F32)
    for l, (fq, fk) in enumerate(factors):
        s = _mm_nt((q * fq).astype(BF16), (k * fk).astype(BF16))
        a = jnp.where(m_ref[l] > 0.0, s, a)
    return a


def _gla_fwd(proj, glr, wgu_pad, b_gate, gain, masks, cum_fwd):
    t_rows = glr.shape[0]
    nt = t_rows // TILE

    def body(q_ref, k_ref, v_ref, g_ref, glr_ref, wgu_ref, b_ref, gain_ref, m_ref, cum_ref, oraw_ref, ogla_ref, st_ref, at_ref,
             s_acc, g_scr, ref_scr):
        @pl.when(pl.program_id(0) == 0)
        def _():
            s_acc[...] = jnp.zeros_like(s_acc)

        _, g_all, b_all = _gla_decays(glr_ref, wgu_ref, b_ref, cum_ref)
        for h in range(GLA_HEADS):
            q, k, factors, e_g, e_last, e_end = _gla_prep(h, q_ref, k_ref, g_all, b_all, g_scr, ref_scr)
            v = _cols(v_ref, h, GLA_HEADS)
            st = s_acc[h]
            st_ref[h] = st
            a = _gla_scores(q, k, factors, m_ref)
            at_ref[h] = a.T.astype(BF16)
            o = _mm(a.astype(BF16), v) + _mm_nt((q * e_g).astype(BF16), st.astype(BF16))
            s_acc[h] = st * e_last + _mm(v.astype(F32).T.astype(BF16), (k * e_end).astype(BF16))
            cols = slice(h * GLA_V, (h + 1) * GLA_V)
            oraw_ref[:, cols] = o
            n = o * lax.rsqrt(_row_mean(o * o) + EPS) * gain_ref[:, cols]
            g = _cols(g_ref, h, GLA_HEADS).astype(F32)
            ogla_ref[:, cols] = (n * g * _sigmoid(g)).astype(BF16)

    row = lambda w: pl.BlockSpec((TILE, w), lambda t: (t, 0))
    whole = lambda *shape: pl.BlockSpec(shape, lambda t: (0,) * len(shape))
    return _call(
        body, "gla_fwd", grid=(nt,),
        out_shape=[jax.ShapeDtypeStruct((t_rows, GLA_W), F32), jax.ShapeDtypeStruct((t_rows, GLA_W), BF16),
                   jax.ShapeDtypeStruct((GLA_HEADS, nt, GLA_V, GLA_K), F32), jax.ShapeDtypeStruct((GLA_HEADS, t_rows, TILE), BF16)],
        in_specs=_proj_specs(("gq", "gk", "gv", "gg"), 1, lambda t: (0, t)) + [row(LANES), whole(LANES, GLA_HEADS * GLA_K),
                  whole(1, GLA_HEADS * GLA_K), whole(1, GLA_W), whole(N_TERMS, TILE, TILE), whole(2 * TILE, TILE)],
        out_specs=[row(GLA_W), row(GLA_W), pl.BlockSpec((GLA_HEADS, None, GLA_V, GLA_K), lambda t: (0, t, 0, 0)),
                   pl.BlockSpec((GLA_HEADS, TILE, TILE), lambda t: (0, t, 0))],
        scratch_shapes=[pltpu.VMEM((GLA_HEADS, GLA_V, GLA_K), F32), pltpu.VMEM((GLA_HEADS, TILE, GLA_K), F32),
                        pltpu.VMEM((GLA_HEADS, len(GLA_LEVELS), TILE, GLA_K), F32)],
        compiler_params=_params(("arbitrary",)),
    )(proj, proj, proj, proj, glr, wgu_pad, b_gate, gain, masks, cum_fwd)


def _gla_bwd(proj, glr, wgu_pad, b_gate, gain, o_raw, do_gla, states, a_t, masks, cum_fwd, cum_bwd):
    t_rows = glr.shape[0]
    nt = t_rows // TILE

    def body(q_ref, k_ref, v_ref, g_ref, glr_ref, wgu_ref, b_ref, gain_ref, m_ref, cum_ref, cumb_ref, oraw_ref, do_ref, st_ref, at_ref,
             dq_ref, dk_ref, dv_ref, dg_ref, dglr_ref, dwgu_ref, dbg_ref, dgain_ref, d_acc, g_scr, ref_scr, dref_scr):
        @pl.when(pl.program_id(0) == 0)
        def _():
            d_acc[...] = jnp.zeros_like(d_acc)
            dwgu_ref[...] = jnp.zeros_like(dwgu_ref)
            dbg_ref[...] = jnp.zeros_like(dbg_ref)
            dgain_ref[...] = jnp.zeros_like(dgain_ref)

        z_all, g_all, b_all = _gla_decays(glr_ref, wgu_ref, b_ref, cum_ref)
        dla_parts = []
        for h in range(GLA_HEADS):
            q, k, factors, e_g, e_last, e_end = _gla_prep(h, q_ref, k_ref, g_all, b_all, g_scr, ref_scr)
            v = _cols(v_ref, h, GLA_HEADS)
            cols = slice(h * GLA_V, (h + 1) * GLA_V)
            kcols = slice(h * GLA_K, (h + 1) * GLA_K)
            o = oraw_ref[:, cols]
            do = do_ref[:, cols].astype(F32)
            g = _cols(g_ref, h, GLA_HEADS).astype(F32)
            rinv = lax.rsqrt(_row_mean(o * o) + EPS)
            nh = o * rinv
            gain_t = gain_ref[:, cols]
            sg = _sigmoid(g)
            dn = do * (g * sg)
            dg_ref[:, cols] = (do * (nh * gain_t) * (sg * (1.0 + g * (1.0 - sg)))).astype(BF16)
            dgain_ref[:, cols] += _col_sum(dn * nh)
            dnh = dn * gain_t
            dor = rinv * (dnh - nh * _row_mean(dnh * nh))
            dob = dor.astype(BF16)
            a_t = at_ref[h]
            da = _mm_nt(dob, v).astype(BF16)
            da_t = _mm_nt(v, dob).astype(BF16)
            st_in = st_ref[h]
            d_out = d_acc[h]
            d_out_b = d_out.astype(BF16)
            qg, kg = q * e_g, k * e_end
            dqg = _mm(dob, st_in.astype(BF16))
            dkg = _mm(v, d_out_b)
            dv_ref[:, cols] = (_mm(a_t, dob) + _mm_nt(kg.astype(BF16), d_out_b)).astype(BF16)
            d_acc[h] = d_out * e_last + _mm(dor.T.astype(BF16), qg.astype(BF16))
            dq = dqg * e_g
            dk = dkg * e_end
            dkg_kg = dkg * kg
            dg_cum = dqg * qg - dkg_kg
            db = None
            for l, (fq, fk) in enumerate(factors):
                qt, kt = q * fq, k * fk
                dqt = _mm(da * m_ref[l], kt.astype(BF16))
                dkt = _mm(da_t * m_ref[N_TERMS + l], qt.astype(BF16))
                dq = dq + dqt * fq
                dk = dk + dkt * fk
                diff = dqt * qt - dkt * kt
                if l == 0:
                    db = diff
                else:
                    dg_cum = dg_cum + diff
                    dref_scr[h, l - 1] = diff
            dq_ref[:, kcols] = (dq * (GLA_K ** -0.5)).astype(BF16)
            dk_ref[:, kcols] = dk.astype(BF16)
            g_scr[h] = dg_cum
            g_scr[h, pl.ds(TILE - 1, 1), :] += e_last * _col_sum(d_out * st_in) + _col_sum(dkg_kg)
            for lvl, blk in enumerate(GLA_LEVELS):
                for n in range(TILE // blk):
                    g_scr[h, pl.ds(n * blk + blk // 2 - 1, 1), :] -= _col_sum(dref_scr[h, lvl, n * blk:(n + 1) * blk, :])
            dla_parts.append(_join3(_mm(cumb_ref[...], jnp.concatenate([_split3(g_scr[h]), _split3(db)], axis=0))))
        dz = jnp.concatenate(dla_parts, axis=1) * (1.0 / GLA_TAU) * _sigmoid(-z_all)
        dzb = dz.astype(BF16)
        wgu_b = wgu_ref[...].astype(BF16)
        for h in range(GLA_HEADS):
            kcols = slice(h * GLA_K, (h + 1) * GLA_K)
            dglr_ref[h] = _mm_nt(dzb[:, kcols], wgu_b[:, kcols]).astype(BF16)
        dwgu_ref[...] += _mm(glr_ref[...].T.astype(BF16), dzb)
        dbg_ref[...] += _col_sum(dz)

    row = lambda w: pl.BlockSpec((TILE, w), lambda j: (nt - 1 - j, 0))
    whole = lambda *shape: pl.BlockSpec(shape, lambda j: (0,) * len(shape))
    return _call(
        body, "gla_bwd", grid=(nt,),
        out_shape=[jax.ShapeDtypeStruct((t_rows, GLA_HEADS * GLA_K), BF16), jax.ShapeDtypeStruct((t_rows, GLA_HEADS * GLA_K), BF16),
                   jax.ShapeDtypeStruct((t_rows, GLA_W), BF16), jax.ShapeDtypeStruct((t_rows, GLA_W), BF16),
                   jax.ShapeDtypeStruct((GLA_HEADS, t_rows, LANES), BF16), jax.ShapeDtypeStruct((LANES, GLA_HEADS * GLA_K), F32),
                   jax.ShapeDtypeStruct((1, GLA_HEADS * GLA_K), F32), jax.ShapeDtypeStruct((1, GLA_W), F32)],
        in_specs=_proj_specs(("gq", "gk", "gv", "gg"), 1, lambda j: (0, nt - 1 - j)) + [row(LANES),
                  whole(LANES, GLA_HEADS * GLA_K), whole(1, GLA_HEADS * GLA_K), whole(1, GLA_W),
                  whole(2 * N_TERMS, TILE, TILE), whole(2 * TILE, TILE), whole(TILE, 2 * TILE), row(GLA_W), row(GLA_W),
                  pl.BlockSpec((GLA_HEADS, None, GLA_V, GLA_K), lambda j: (0, nt - 1 - j, 0, 0)),
                  pl.BlockSpec((GLA_HEADS, TILE, TILE), lambda j: (0, nt - 1 - j, 0))],
        out_specs=[row(GLA_HEADS * GLA_K), row(GLA_HEADS * GLA_K), row(GLA_W), row(GLA_W),
                   pl.BlockSpec((GLA_HEADS, TILE, LANES), lambda j: (0, nt - 1 - j, 0)), whole(LANES, GLA_HEADS * GLA_K),
                   whole(1, GLA_HEADS * GLA_K), whole(1, GLA_W)],
        scratch_shapes=[pltpu.VMEM((GLA_HEADS, GLA_V, GLA_K), F32), pltpu.VMEM((GLA_HEADS, TILE, GLA_K), F32),
                        pltpu.VMEM((GLA_HEADS, len(GLA_LEVELS), TILE, GLA_K), F32),
                        pltpu.VMEM((GLA_HEADS, len(GLA_LEVELS), TILE, GLA_K), F32)],
        compiler_params=_params(("arbitrary",)),
    )(proj, proj, proj, proj, glr, wgu_pad, b_gate, gain, masks.astype(BF16), cum_fwd, cum_bwd, o_raw, do_gla, states, a_t)


def _merge_fwd_bwd(o_ret, o_gla, proj, x, target, g_final, w_br, w_bg, w_out):
    t_rows = x.shape[0] + TILE
    nt = t_rows // TILE

    def body(oret_ref, ogla_ref, mr_ref, mg_ref, h0_ref, tgt_ref, gf_ref, wbr_hbm, wbg_hbm, wout_hbm,
             dh1_ref, dmr_ref, dmg_ref, doret_ref, dogla_ref, loss_ref, dgf_ref, dwbr_hbm, dwbg_hbm, dwout_hbm,
             wbr, wbg, wout, abr, abg, aout, sem):
        i = pl.program_id(0)

        @pl.when(i == 0)
        def _():
            cps = [pltpu.make_async_copy(s, d, sem.at[n]) for n, (s, d) in enumerate(((wbr_hbm, wbr), (wbg_hbm, wbg), (wout_hbm, wout)))]
            for cp in cps:
                cp.start()
            abr[...] = jnp.zeros_like(abr)
            abg[...] = jnp.zeros_like(abg)
            aout[...] = jnp.zeros_like(aout)
            loss_ref[...] = jnp.zeros_like(loss_ref)
            dgf_ref[...] = jnp.zeros_like(dgf_ref)
            for cp in cps:
                cp.wait()
            dh1_ref[...] = jnp.zeros_like(dh1_ref)
            dmr_ref[...] = jnp.zeros_like(dmr_ref)
            dmg_ref[...] = jnp.zeros_like(dmg_ref)
            doret_ref[...] = jnp.zeros_like(doret_ref)
            dogla_ref[...] = jnp.zeros_like(dogla_ref)

        @pl.when(i > 0)
        def _():
            oret, ogla = oret_ref[...], ogla_ref[...]
            br, bg = _mm(oret, wbr[...]), _mm(ogla, wbg[...])
            sr, sg = _sigmoid(_cols(mr_ref).astype(F32)), _sigmoid(_cols(mg_ref).astype(F32))
            mb = (sr * br + sg * bg).astype(BF16)
            h1 = h0_ref[...] + _mm(mb, wout[...])
            r2 = lax.rsqrt(_row_mean(h1 * h1) + EPS)
            hn = h1 * r2
            gf = gf_ref[...]
            diff = hn * gf - tgt_ref[...]
            loss_ref[...] += 0.5 * jnp.sum(_row_mean(diff * diff))
            dy = diff * (1.0 / D_MODEL)
            dgf_ref[...] += _col_sum(dy * hn)
            dyg = dy * gf
            dh1 = r2 * (dyg - hn * _row_mean(dyg * hn))
            dh1_ref[...] = dh1
            dh1b = dh1.astype(BF16)
            dm = _mm_nt(dh1b, wout[...])
            aout[...] += _mm_tn(mb, dh1b)
            dbr = (dm * sr).astype(BF16)
            dbg = (dm * sg).astype(BF16)
            dmr_ref[...] = (dm * br * sr * (1.0 - sr)).astype(BF16)
            dmg_ref[...] = (dm * bg * sg * (1.0 - sg)).astype(BF16)
            doret_ref[...] = _mm_nt(dbr, wbr[...]).astype(BF16)
            dogla_ref[...] = _mm_nt(dbg, wbg[...]).astype(BF16)
            abr[...] += _mm_tn(oret, dbr)
            abg[...] += _mm_tn(ogla, dbg)

        @pl.when(i == nt - 1)
        def _():
            wbr[...] = abr[...].astype(BF16)
            wbg[...] = abg[...].astype(BF16)
            wout[...] = aout[...].astype(BF16)
            pltpu.sync_copy(wbr, dwbr_hbm)
            pltpu.sync_copy(wbg, dwbg_hbm)
            pltpu.sync_copy(wout, dwout_hbm)

    row = lambda w: pl.BlockSpec((TILE, w), lambda i: (i, 0))
    one = lambda w: pl.BlockSpec((1, w), lambda i: (0, 0))
    return _call(
        body, "merge_fwd_bwd", grid=(nt,),
        out_shape=[jax.ShapeDtypeStruct((t_rows, D_MODEL), F32), jax.ShapeDtypeStruct((t_rows, D_MODEL), BF16),
                   jax.ShapeDtypeStruct((t_rows, D_MODEL), BF16), jax.ShapeDtypeStruct((t_rows, RET_W), BF16),
                   jax.ShapeDtypeStruct((t_rows, GLA_W), BF16), jax.ShapeDtypeStruct((1, LANES), F32),
                   jax.ShapeDtypeStruct((1, D_MODEL), F32), jax.ShapeDtypeStruct((RET_W, D_MODEL), BF16),
                   jax.ShapeDtypeStruct((GLA_W, D_MODEL), BF16), jax.ShapeDtypeStruct((D_MODEL, D_MODEL), BF16)],
        in_specs=[row(RET_W), row(GLA_W)] + _proj_specs(("mr", "mg"), 1, lambda i: (0, i)) + [_x_spec(), _x_spec(), one(D_MODEL), ANY, ANY, ANY],
        out_specs=[row(D_MODEL), row(D_MODEL), row(D_MODEL), row(RET_W), row(GLA_W), one(LANES), one(D_MODEL), ANY, ANY, ANY],
        scratch_shapes=[pltpu.VMEM((RET_W, D_MODEL), BF16), pltpu.VMEM((GLA_W, D_MODEL), BF16), pltpu.VMEM((D_MODEL, D_MODEL), BF16),
                        pltpu.VMEM((RET_W, D_MODEL), F32), pltpu.VMEM((GLA_W, D_MODEL), F32), pltpu.VMEM((D_MODEL, D_MODEL), F32),
                        pltpu.SemaphoreType.DMA((3,))],
        compiler_params=_params(("arbitrary",)),
    )(o_ret, o_gla, proj, proj, x, target, g_final, w_br, w_bg, w_out)


def _inproj_bwd_x(dseg, dglr, head, x, dh1, g_norm, slabs, w_glr, chip_partials):
    t_rows = x.shape[0] + TILE
    nt = t_rows // TILE
    ne = len(chip_partials)

    def body(*refs):
        d_refs = refs[:10]
        dglr_ref, head_ref, x_ref, dh1_ref, g_ref, slabs_hbm, wg_hbm = refs[10:17]
        part_refs = refs[17:17 + ne]
        dx_ref, dhead_ref, dgn_ref = refs[17 + ne:20 + ne]
        landed = refs[20 + ne:20 + 2 * ne]
        w_vm, wg_vm, edge_vm, sem = refs[20 + 2 * ne:24 + 2 * ne]
        exchange = _Exchange(part_refs, landed, refs[24 + 2 * ne:], among_chips=True)

        @pl.when(pl.program_id(0) == 0)
        def _():
            exchange.start()
            dgn_ref[...] = jnp.zeros_like(dgn_ref)
            _load_weight(slabs_hbm, wg_hbm, w_vm, wg_vm, edge_vm, sem)

        @pl.when(pl.program_id(0) == nt - 1)
        def _():
            exchange.finish()

        dglr = dglr_ref[0].astype(F32)
        for h in range(1, GLA_HEADS):
            dglr = dglr + dglr_ref[h].astype(F32)
        du = _mm_nt(dglr.astype(BF16), wg_vm[...])
        for s, d_ref in enumerate(d_refs):
            du = du + _mm_nt(d_ref[...], w_vm[:, SEG_OFF[s]:SEG_OFF[s] + SEG_W[s]])
        x = _tile_rows(head_ref, x_ref)
        r = lax.rsqrt(_row_mean(x * x) + EPS)
        hn = x * r
        dgn_ref[...] += _col_sum(du * hn)
        dug = du * g_ref[...]
        dh0 = dh1_ref[...] + r * (dug - hn * _row_mean(dug * hn))
        dx_ref[...] = dh0

        @pl.when(pl.program_id(0) == 0)
        def _():
            dhead_ref[...] = dh0

    row = lambda w: pl.BlockSpec((TILE, w), lambda i: (i, 0))
    one = pl.BlockSpec((1, D_MODEL), lambda i: (0, 0))
    return _call(
        body, "inproj_bwd_x", grid=(nt,),
        out_shape=[jax.ShapeDtypeStruct((t_rows - TILE, D_MODEL), F32), jax.ShapeDtypeStruct((TILE, D_MODEL), F32),
                   jax.ShapeDtypeStruct((1, D_MODEL), F32)] + [jax.ShapeDtypeStruct(a.shape, a.dtype) for a in chip_partials],
        in_specs=[row(w) for w in SEG_W] + [pl.BlockSpec((GLA_HEADS, TILE, LANES), lambda i: (0, i, 0)),
                                            _head_spec(), _x_spec(), row(D_MODEL), one, ANY, ANY] + [ANY] * ne,
        out_specs=[_x_spec(), _head_spec(), one] + [ANY] * ne,
        scratch_shapes=W_SCRATCH() + _exchange_sems(ne, N_CHIP),
        compiler_params=_params(("arbitrary",)),
    )(*[dseg[n] for n in SEG_NAMES], dglr, head, x, dh1, g_norm, slabs, w_glr, *chip_partials)


W_TILE = 512


def _inproj_bwd_w(ut, dseg, dglr, row_sends):
    nt = ut.shape[0]
    t_rows = nt * TILE
    kc = 3 if nt % 3 == 0 else 1
    tiles = [(s, c) for s in range(len(SEG_W)) for c in range(0, SEG_W[s], W_TILE)]
    bpt = W_TILE // LANES
    nr = len(row_sends)
    n = 1 + nr
    last_tile = [(SLAB_BLK0[d] + SLAB_BLOCKS - 1) // bpt for d in range(N_DEV)]

    def body(ut_hbm, *refs):
        d_refs, dglr_hbm, row_refs = refs[:10], refs[10], refs[11:11 + nr]
        out_hbm, oglr_ref, sib = refs[11 + nr], refs[12 + nr], refs[13 + nr:13 + nr + n]
        ut_vm, dbuf, obuf, acc, gbuf, sem, send_sems, recv_sems = refs[13 + nr + n:]
        x, y, core = _position()

        def handover(d, k, landed=False):
            q = d // 2
            src = out_hbm.at[pl.ds(SLAB_BLK0[d], SLAB_BLOCKS)] if k == 0 else row_refs[k - 1].at[d]
            return pltpu.make_async_remote_copy(src_ref=sib[k].at[q] if landed else src, dst_ref=sib[k].at[q],
                                                send_sem=send_sems.at[n * q + k], recv_sem=recv_sems.at[n * q + k],
                                                device_id=(x, y, 1 - core), device_id_type=MESH)

        def for_sibling(d, ks, fn):
            @pl.when(d % 2 != core)
            def _():
                for k in ks:
                    fn(handover(d, k))

        for d in range(N_DEV):
            for_sibling(d, range(1, n), lambda cp: cp.start())

        def fetch(i):
            s, c = tiles[i]
            return pltpu.make_async_copy(d_refs[s].at[:, pl.ds(c, W_TILE)], dbuf.at[i % 2], sem.at[1 + i % 2])

        def contract(rhs_refs, width):
            acc[:, :width] = jnp.zeros((D_MODEL, width), F32)

            def step(k, carry):
                part = None
                for j in range(kc):
                    kk = k * kc + j
                    for rhs_ref in rhs_refs:
                        prod = _mm(ut_vm[kk], rhs_ref[pl.ds(pl.multiple_of(kk * TILE, TILE), TILE), :])
                        part = prod if part is None else part + prod
                acc[:, :width] += part
                return carry

            lax.fori_loop(0, nt // kc, step, 0)
            return acc[:, :width]

        load_ut = pltpu.make_async_copy(ut_hbm, ut_vm, sem.at[0])
        load_glr = pltpu.make_async_copy(dglr_hbm, gbuf, sem.at[5])
        load_ut.start()
        load_glr.start()
        fetch(0).start()
        load_ut.wait()
        stores = {}

        def stored(i):
            stores[i].wait()
            for d in range(N_DEV):
                if last_tile[d] == i:
                    for_sibling(d, [0], lambda cp: cp.start())

        for i, (s, c) in enumerate(tiles):
            if i + 1 < len(tiles):
                fetch(i + 1).start()
            fetch(i).wait()
            if i >= 2:
                stored(i - 2)
            total = contract([dbuf.at[i % 2]], W_TILE)
            for j in range(bpt):
                obuf[i % 2, j] = total[:, j * LANES:(j + 1) * LANES].astype(BF16)
            blk0 = (SEG_OFF[s] + c) // LANES
            stores[i] = pltpu.make_async_copy(obuf.at[i % 2], out_hbm.at[pl.ds(blk0, bpt)], sem.at[3 + i % 2])
            stores[i].start()
        for i in range(max(0, len(tiles) - 2), len(tiles)):
            stored(i)
        load_glr.wait()
        head_sum = gbuf[0].astype(F32)
        for h in range(1, GLA_HEADS):
            head_sum = head_sum + gbuf[h].astype(F32)
        gbuf[0] = head_sum.astype(BF16)
        oglr_ref[...] = contract([gbuf.at[0]], LANES)
        for q in range(N_CHIP):
            for k in range(n):
                handover(2 * q, k, landed=True).wait_recv()
        for d in range(N_DEV):
            for_sibling(d, range(n), lambda cp: cp.wait_send())

    outs = _call(
        body, "inproj_bwd_w",
        out_shape=[jax.ShapeDtypeStruct((AL_COLS // LANES, D_MODEL, LANES), BF16), jax.ShapeDtypeStruct((D_MODEL, LANES), F32),
                   jax.ShapeDtypeStruct((N_CHIP, SLAB_BLOCKS, D_MODEL, LANES), BF16)]
                  + [jax.ShapeDtypeStruct((N_CHIP, *r.shape[1:]), BF16) for r in row_sends],
        in_specs=[ANY] * (12 + nr), out_specs=[ANY, pl.BlockSpec(memory_space=pltpu.VMEM)] + [ANY] * n,
        scratch_shapes=[pltpu.VMEM((nt, D_MODEL, TILE), BF16), pltpu.VMEM((2, t_rows, W_TILE), BF16),
                        pltpu.VMEM((2, bpt, D_MODEL, LANES), BF16), pltpu.VMEM((D_MODEL, W_TILE), F32),
                        pltpu.VMEM((GLA_HEADS, t_rows, LANES), BF16), pltpu.SemaphoreType.DMA((6,)),
                        pltpu.SemaphoreType.DMA((n * N_CHIP,)), pltpu.SemaphoreType.DMA((n * N_CHIP,))],
        compiler_params=_params(),
    )(ut, *[dseg[n_] for n_ in SEG_NAMES], dglr, *row_sends)
    return outs[0], outs[1], outs[2], outs[3:]


def _position():
    x, y, c = lax.axis_index("x"), lax.axis_index("y"), lax.axis_index("c")
    return x, y, c


def _index(px, py, pc):
    return 4 * px + 2 * py + pc


def _all_gather(arrs, name):
    n = len(arrs)

    def body(*refs):
        ins, outs = refs[:n], refs[n:2 * n]
        send_sems, recv_sems, local_sems = refs[2 * n:]
        x, y, c = _position()
        me, sibling = (x, y, c), (x, y, 1 - c)
        chips = [(1 - x, y), (x, 1 - y), (1 - x, 1 - y)]

        def copy(a, k, block, to, src=None):
            dst = outs[a].at[_index(*block)]
            return pltpu.make_async_remote_copy(src_ref=dst if src is None else src, dst_ref=dst,
                                                send_sem=send_sems.at[7 * a + k], recv_sem=recv_sems.at[7 * a + k],
                                                device_id=to, device_id_type=MESH)

        def relay(a, j):
            return copy(a, 3, (*chips[j], c), (*chips[1 - j], c))

        mine = [pltpu.make_async_copy(ins[a], outs[a].at[_index(*me)], local_sems.at[a]) for a in range(n)]
        for cp in mine:
            cp.start()
        first = []
        for a in range(n):
            first.append(copy(a, 0, me, sibling, src=ins[a]))
            first += [copy(a, 1 + j, me, (*chips[j], c), src=ins[a]) for j in range(2)]
        for cp in first:
            cp.start()
        passed = []
        for j in range(3):
            for a in range(n):
                copy(a, 1 + j, (*chips[j], c), me).wait_recv()
                cp = copy(a, 4 + j, (*chips[j], c), sibling)
                cp.start()
                passed.append(cp)
            if j < 2:
                @pl.when(c == j)
                def _():
                    for a in range(n):
                        relay(a, j).start()
        for a in range(n):
            copy(a, 0, sibling, me).wait_recv()
            for j in range(3):
                copy(a, 4 + j, (*chips[j], 1 - c), me).wait_recv()
        for cp in first + passed:
            cp.wait_send()
        for j in range(2):
            @pl.when(c == j)
            def _():
                for a in range(n):
                    relay(a, j).wait_send()
        for cp in mine:
            cp.wait()

    return _call(
        body, name,
        out_shape=[jax.ShapeDtypeStruct((N_DEV, *a.shape), a.dtype) for a in arrs],
        in_specs=[ANY] * n, out_specs=[ANY] * n,
        scratch_shapes=[pltpu.SemaphoreType.DMA((7 * n,)), pltpu.SemaphoreType.DMA((7 * n,)), pltpu.SemaphoreType.DMA((n,))],
    )(*arrs)


N_CHIP = N_DEV // 2


def _slab_block0(owner):
    step = SLAB_BLK0[1]
    assert all(SLAB_BLK0[d] == step * d - (d == N_DEV - 1) for d in range(N_DEV))
    return step * owner - jnp.where(owner == N_DEV - 1, 1, 0)


def _add_bf16(c_ref, a_ref, b_ref, o_ref):
    o_ref[...] = (a_ref[...].astype(F32) + b_ref[...].astype(F32)).astype(BF16)


def _chip_partial_slab(dw_blocks, sib, core):
    blk = pl.BlockSpec((None, SLAB_BLOCKS, D_MODEL, LANES), lambda q, c_ref: (q, 0, 0, 0))
    return _call(
        functools.partial(_add_bf16), "chip_partial_w_in", out_shape=jax.ShapeDtypeStruct(sib.shape, BF16),
        grid_spec=pltpu.PrefetchScalarGridSpec(
            num_scalar_prefetch=1, grid=(N_CHIP,),
            in_specs=[pl.BlockSpec((pl.Element(SLAB_BLOCKS), pl.Element(D_MODEL), pl.Element(LANES)),
                                   lambda q, c_ref: (_slab_block0(2 * q + c_ref[0]), 0, 0)), blk],
            out_specs=blk),
        compiler_params=_params(("arbitrary",)),
    )(core, dw_blocks, sib)


def _chip_partial_rows(send, sib, core, name):
    rows, cols = send.shape[1:]
    blk = pl.BlockSpec((None, rows, cols), lambda q, c_ref: (q, 0, 0))
    return _call(
        functools.partial(_add_bf16), name, out_shape=jax.ShapeDtypeStruct(sib.shape, BF16),
        grid_spec=pltpu.PrefetchScalarGridSpec(
            num_scalar_prefetch=1, grid=(N_CHIP,),
            in_specs=[pl.BlockSpec((None, rows, cols), lambda q, c_ref: (2 * q + c_ref[0], 0, 0)), blk], out_specs=blk),
        compiler_params=_params(("arbitrary",)),
    )(core, send, sib)


def _exchange_sems(n_arrays, n_peers):
    return [pltpu.SemaphoreType.DMA((n_arrays * n_peers,)), pltpu.SemaphoreType.DMA((n_arrays * n_peers,)),
            pltpu.SemaphoreType.DMA((n_arrays,))]


class _Exchange:
    def __init__(self, srcs, dsts, sems, among_chips):
        self.arrs = list(zip(srcs, dsts))
        self.n = len(self.arrs)
        self.send_sems, self.recv_sems, self.local_sems = sems
        self.among_chips = among_chips
        x, y, c = _position()
        self.c = c
        self.me = 2 * x + y if among_chips else _index(x, y, c)
        self.n_peers = N_CHIP if among_chips else N_DEV

    def _device(self, p):
        return (p // 2, p % 2, self.c) if self.among_chips else (p // 4, (p // 2) % 2, p % 2)

    def _src(self, k, p):
        src = self.arrs[k][0]
        return src.at[p] if self.among_chips else src

    def _mine(self):
        return [pltpu.make_async_copy(self._src(k, self.me), self.arrs[k][1].at[self.me], self.local_sems.at[k]) for k in range(self.n)]

    def _copy(self, p, k, landing):
        return pltpu.make_async_remote_copy(
            src_ref=self._src(k, p), dst_ref=self.arrs[k][1].at[landing], send_sem=self.send_sems.at[self.n * p + k],
            recv_sem=self.recv_sems.at[self.n * landing + k], device_id=self._device(p), device_id_type=MESH)

    def _others(self, fn):
        for p in range(self.n_peers):
            @pl.when(p != self.me)
            def _():
                for k in range(self.n):
                    fn(p, k)

    def start(self):
        for cp in self._mine():
            cp.start()
        self._others(lambda p, k: self._copy(p, k, self.me).start())

    def finish(self):
        self._others(lambda p, k: self._copy(p, k, p).wait_recv())
        self._others(lambda p, k: self._copy(p, k, self.me).wait_send())
        for cp in self._mine():
            cp.wait()


def _adamw(g, w, m, v):
    m_new = ADAM_B1 * m + (1.0 - ADAM_B1) * g
    v_new = ADAM_B2 * v + (1.0 - ADAM_B2) * (g * g)
    m_hat = m_new / (1.0 - ADAM_B1 ** ADAM_STEP)
    v_hat = v_new / (1.0 - ADAM_B2 ** ADAM_STEP)
    delta = -ADAM_LR * (m_hat / (jnp.sqrt(v_hat) + ADAM_EPS) + ADAM_WD * w)
    return delta, m_new, v_new


def _sum_partials(p_ref):
    g = p_ref[0].astype(F32)
    for d in range(1, p_ref.shape[0]):
        g = g + p_ref[d].astype(F32)
    return g


def _reduce_adam(parts, w, m, v, name, block_rows, row_off=0):
    rows, cols = w.shape
    off = row_off // block_rows

    def body(p_ref, w_ref, m_ref, v_ref, g_ref, d_ref, mo_ref, vo_ref):
        g = _sum_partials(p_ref)
        g_ref[...] = g
        d_ref[...], mo_ref[...], vo_ref[...] = _adamw(g, w_ref[...], m_ref[...], v_ref[...])

    blk = pl.BlockSpec((block_rows, cols), lambda i: (i, 0))
    return _call(
        body, name, grid=(rows // block_rows,),
        out_shape=[jax.ShapeDtypeStruct((rows, cols), F32)] * 4,
        in_specs=[pl.BlockSpec((parts.shape[0], block_rows, cols), lambda i: (0, i + off, 0)), blk, blk, blk],
        out_specs=[blk] * 4,
        compiler_params=_params(("arbitrary",)),
    )(parts, w, m, v)


def _reduce_adam_slab(parts, glr, w_t, m_t, v_t, me):
    cols, rows = w_t.shape
    shift = jnp.asarray(SLAB_SHIFT, jnp.int32)[me]
    glr_at = jnp.where(me == GLR_DEV, GLR_LOCAL, cols).astype(jnp.int32)

    def body(s_ref, p_ref, glr_ref, w_ref, m_ref, v_ref, g_ref, d_ref, mo_ref, vo_ref, slab_t):
        shift, glr_at = s_ref[0], s_ref[1]
        tall = jnp.concatenate([_sum_partials(p_ref.at[:, j]).T for j in range(SLAB_BLOCKS)], axis=0)
        before = pltpu.roll(tall, SLAB_W - shift, 0)
        after = pltpu.roll(tall, lax.rem(SLAB_W - shift + GLA_RANK, SLAB_W), 0)
        wide = jnp.concatenate([glr_ref[...].T, jnp.zeros((SLAB_W - LANES, LANES), F32)], axis=0)
        placed = pltpu.roll(wide, lax.rem(glr_at, SLAB_W), 0)
        row = lax.broadcasted_iota(jnp.int32, (SLAB_W, LANES), 0)
        slab_t[...] = jnp.where(row < glr_at, before, jnp.where(row < glr_at + GLA_RANK, placed, after))
        g = slab_t[pl.ds(0, cols), :]
        g_ref[...] = g
        d_ref[...], mo_ref[...], vo_ref[...] = _adamw(g, w_ref[...], m_ref[...], v_ref[...])

    blk = pl.BlockSpec((cols, LANES), lambda i, s: (0, i))
    return _call(
        body, "adam_w_in", out_shape=[jax.ShapeDtypeStruct((cols, rows), F32)] * 4,
        grid_spec=pltpu.PrefetchScalarGridSpec(
            num_scalar_prefetch=1, grid=(rows // LANES,),
            in_specs=[pl.BlockSpec((parts.shape[0], SLAB_BLOCKS, LANES, LANES), lambda i, s: (0, 0, i, 0)),
                      pl.BlockSpec((LANES, LANES), lambda i, s: (i, 0)), blk, blk, blk],
            out_specs=[blk] * 4, scratch_shapes=[pltpu.VMEM((SLAB_W, LANES), F32)]),
        compiler_params=_params(("arbitrary",)),
    )(jnp.stack([shift, glr_at]), parts, glr, w_t, m_t, v_t)


def _reduce_small(parts):
    def body(p_ref, o_ref):
        o_ref[...] = _sum_partials(p_ref)

    return _call(body, "reduce_small", out_shape=jax.ShapeDtypeStruct(parts.shape[1:], F32))(parts)


def _adam_small(g, w, m, v):
    def body(g_ref, w_ref, m_ref, v_ref, d_ref, mo_ref, vo_ref):
        d_ref[...], mo_ref[...], vo_ref[...] = _adamw(g_ref[...], w_ref[...], m_ref[...], v_ref[...])

    return _call(body, "adam_small", out_shape=[jax.ShapeDtypeStruct(g.shape, F32)] * 3)(g, w, m, v)


def _pack_rows(arrs):
    rows = []
    for a in arrs:
        flat = a.reshape(-1).astype(F32)
        pad = (-flat.shape[0]) % LANES
        rows.append(jnp.pad(flat, (0, pad)).reshape(-1, LANES))
    packed = jnp.concatenate(rows, axis=0)
    return jnp.pad(packed, ((0, (-packed.shape[0]) % 8), (0, 0)))


def _unpack_rows(packed, shapes):
    out, r = [], 0
    for shp in shapes:
        size = 1
        for s in shp:
            size *= s
        nrows = -(-size // LANES)
        out.append(packed[r:r + nrows].reshape(-1)[:size].reshape(shp))
        r += nrows
    return out


def _shard_to_slab(shard, d):
    glr = jnp.zeros((D_MODEL, GLA_RANK), shard.dtype)
    if d == GLR_DEV:
        glr = shard[:, GLR_LOCAL:GLR_LOCAL + GLA_RANK]
        shard = jnp.concatenate([shard[:, :GLR_LOCAL], shard[:, GLR_LOCAL + GLA_RANK:]], axis=1)
    return jnp.pad(shard, ((0, 0), (SLAB_SHIFT[d], SLAB_W - SLAB_SHIFT[d] - shard.shape[1]))), glr


def kernel(x, meta_tokens, norm_gain, w_in, w_gate_up, b_gate, ret_norm_gain, gla_norm_gain, w_branch_ret, w_branch_gla, w_out, final_norm_gain, loss_target, m_meta_tokens, m_norm_gain, m_w_in, m_w_gate_up, m_b_gate, m_ret_norm_gain, m_gla_norm_gain, m_w_branch_ret, m_w_branch_gla, m_w_out, m_final_norm_gain, v_meta_tokens, v_norm_gain, v_w_in, v_w_gate_up, v_b_gate, v_ret_norm_gain, v_gla_norm_gain, v_w_branch_ret, v_w_branch_gla, v_w_out, v_final_norm_gain):
    xi, yi, ci = _position()
    me = _index(xi, yi, ci)
    seq = x.shape[1]
    t_rows = seq + TILE
    in_shard = w_in.shape[2]
    gu_shard = w_gate_up.shape[2]
    meta_shard = meta_tokens.shape[1]
    ret_rows, gla_rows, out_rows = w_branch_ret.shape[1], w_branch_gla.shape[1], w_out.shape[1]

    assert in_shard == IN_SHARD
    slab_local, glr_local = lax.switch(me, [functools.partial(_shard_to_slab, d=d) for d in range(N_DEV)], w_in[0])
    small_local = jnp.concatenate([meta_tokens, jnp.pad(w_gate_up[0], ((0, 0), (0, LANES - gu_shard))),
                                   glr_local.reshape(-1, LANES)], axis=0)
    slabs, g_small = _all_gather([slab_local.astype(BF16), small_local], "all_gather_shards")
    n_small = N_META + GLA_RANK
    w_glr = jnp.pad(g_small[GLR_DEV, n_small:].reshape(D_MODEL, GLA_RANK), ((0, 0), (0, LANES - GLA_RANK))).astype(BF16)
    meta_full = jnp.transpose(g_small[:, :N_META, :], (1, 0, 2)).reshape(N_META, D_MODEL)
    wgu_full = jnp.transpose(g_small[:, N_META:n_small, :gu_shard], (1, 0, 2)).reshape(GLA_RANK, GLA_HEADS * GLA_K)
    wgu_pad = jnp.pad(wgu_full, ((0, LANES - GLA_RANK), (0, 0)))

    rope = _rope_tables(t_rows // TILE)
    lg = jnp.log1p(-(2.0 ** (-5.0 - jnp.arange(RET_HEADS, dtype=F32))))

    head = jnp.concatenate([jnp.zeros((PAD_ROWS, D_MODEL), F32), meta_full], axis=0)
    ut, proj, glr = _inproj_tiles(head, x[0], norm_gain, slabs, w_glr)
    o_ret_raw, o_ret, ret_states, (g_br, g_bg, g_o) = _ret_fwd(
        proj, rope, ret_norm_gain, lg, [w_branch_ret[0].astype(BF16), w_branch_gla[0].astype(BF16), w_out[0].astype(BF16)])
    w_br, w_bg, w_o = g_br.reshape(RET_W, D_MODEL), g_bg.reshape(GLA_W, D_MODEL), g_o.reshape(D_MODEL, D_MODEL)
    masks, cum_fwd, cum_bwd = _gla_tables()
    o_gla_raw, o_gla, gla_states, gla_scores_t = _gla_fwd(proj, glr, wgu_pad, b_gate, gla_norm_gain, masks, cum_fwd)
    (dh1, d_mr, d_mg, do_ret, do_gla, loss_part, d_gfinal, dw_br, dw_bg, dw_o) = _merge_fwd_bwd(
        o_ret, o_gla, proj, x[0], loss_target[0], final_norm_gain.reshape(1, D_MODEL), w_br, w_bg, w_o)

    d_rq, d_rk, d_rv, d_rg, d_gret = _ret_bwd(proj, rope, ret_norm_gain, lg, o_ret_raw, do_ret, ret_states)
    d_gq, d_gk, d_gv, d_gg, dglr_parts, d_wgu, d_bgate, d_ggla = _gla_bwd(
        proj, glr, wgu_pad, b_gate, gla_norm_gain, o_gla_raw, do_gla, gla_states, gla_scores_t, masks, cum_fwd, cum_bwd)
    dseg = dict(rq=d_rq, rk=d_rk, rv=d_rv, rg=d_rg, gq=d_gq, gk=d_gk, gv=d_gv, gg=d_gg, mr=d_mr, mg=d_mg)
    row_sends = [dw_br.reshape(N_DEV, ret_rows, D_MODEL), dw_bg.reshape(N_DEV, gla_rows, D_MODEL),
                 dw_o.reshape(N_DEV, out_rows, D_MODEL)]
    dw_blocks, dw_glr, sib_in, sib_rows = _inproj_bwd_w(ut, dseg, dglr_parts, row_sends)
    core = ci.astype(jnp.int32).reshape(1)
    chip_partials = [_chip_partial_slab(dw_blocks, sib_in, core)] + [
        _chip_partial_rows(send, sib, core, "chip_partial_" + name)
        for send, sib, name in zip(row_sends, sib_rows, ("w_branch_ret", "w_branch_gla", "w_out"))]
    grad_x, d_head, d_gnorm, p_in, p_br, p_bg, p_o = _inproj_bwd_x(
        dseg, dglr_parts, head, x[0], dh1, norm_gain, slabs, w_glr, chip_partials)
    small_shapes = [(N_META, D_MODEL), (1, D_MODEL), (GLA_RANK, GLA_HEADS * GLA_K), (1, GLA_HEADS * GLA_K),
                    (1, RET_W), (1, GLA_W), (1, D_MODEL), (1, LANES), (D_MODEL, GLA_RANK)]
    small_part = _pack_rows([d_head[PAD_ROWS:], d_gnorm, d_wgu[:GLA_RANK], d_bgate, d_gret, d_ggla, d_gfinal, loss_part,
                             dw_glr[:, :GLA_RANK]])
    (p_small,) = _all_gather([small_part], "all_gather_small_partials")

    (g_meta_f, g_gnorm, g_wgu_f, g_bgate, g_gret, g_ggla, g_gfinal, loss_all,
     g_wglr) = _unpack_rows(_reduce_small(p_small), small_shapes)
    g_w_in, d_w_in, nm_w_in, nv_w_in = [a.T for a in _reduce_adam_slab(
        p_in, jnp.pad(g_wglr, ((0, 0), (0, LANES - GLA_RANK))), w_in[0].T, m_w_in[0].T, v_w_in[0].T, me)]
    rb = gla_rows
    g_w_br, d_w_br, nm_w_br, nv_w_br = _reduce_adam(p_br, w_branch_ret[0], m_w_branch_ret[0], v_w_branch_ret[0], "adam_w_branch_ret", rb)
    g_w_bg, d_w_bg, nm_w_bg, nv_w_bg = _reduce_adam(p_bg, w_branch_gla[0], m_w_branch_gla[0], v_w_branch_gla[0], "adam_w_branch_gla", rb)
    g_w_o, d_w_o, nm_w_o, nv_w_o = _reduce_adam(p_o, w_out[0], m_w_out[0], v_w_out[0], "adam_w_out", rb)
    g_meta = lax.dynamic_slice_in_dim(g_meta_f, me * meta_shard, meta_shard, axis=1)
    g_wgu = lax.dynamic_slice_in_dim(g_wgu_f, me * gu_shard, gu_shard, axis=1)
    s_g = [g_meta, g_gnorm, g_wgu, g_bgate, g_gret, g_ggla, g_gfinal]
    s_w = [meta_tokens, norm_gain, w_gate_up[0], b_gate, ret_norm_gain, gla_norm_gain, final_norm_gain]
    s_m = [m_meta_tokens, m_norm_gain, m_w_gate_up[0], m_b_gate, m_ret_norm_gain, m_gla_norm_gain, m_final_norm_gain]
    s_v = [v_meta_tokens, v_norm_gain, v_w_gate_up[0], v_b_gate, v_ret_norm_gain, v_gla_norm_gain, v_final_norm_gain]
    shapes = [a.shape for a in s_g]
    s_d, s_nm, s_nv = [_unpack_rows(p, shapes) for p in _adam_small(*[_pack_rows(l) for l in (s_g, s_w, s_m, s_v)])]

    loss = loss_all[0, 0]
    grad_x = grad_x[None]

    def order(meta, gnorm, win, wgu, bgate, gret, ggla, wbr, wbg, wo, gfin):
        return (meta, gnorm, win[None], wgu[None], bgate, gret, ggla, wbr[None], wbg[None], wo[None], gfin.reshape(final_norm_gain.shape))

    def small(l):
        return dict(meta=l[0], gnorm=l[1], wgu=l[2], bgate=l[3], gret=l[4], ggla=l[5], gfin=l[6])

    grads = order(win=g_w_in, wbr=g_w_br, wbg=g_w_bg, wo=g_w_o, **small(s_g))
    deltas = order(win=d_w_in, wbr=d_w_br, wbg=d_w_bg, wo=d_w_o, **small(s_d))
    new_m = order(win=nm_w_in, wbr=nm_w_br, wbg=nm_w_bg, wo=nm_w_o, **small(s_nm))
    new_v = order(win=nv_w_in, wbr=nv_w_br, wbg=nv_w_bg, wo=nv_w_o, **small(s_nv))
    return (loss, grad_x, *grads, *deltas, *new_m, *new_v)
```

```python
import functools

import jax
import jax.numpy as jnp
from jax import lax
from jax.experimental import pallas as pl
from jax.experimental.pallas import tpu as pltpu

F32 = jnp.float32
BF16 = jnp.bfloat16

D_MODEL = 1024
N_META = 16
TILE = 256
PAD_ROWS = TILE - N_META
RET_HEADS = 4
RET_QK = 256
RET_V = 512
RET_W = RET_HEADS * RET_V
GLA_HEADS = 4
GLA_K = 128
GLA_V = 256
GLA_W = GLA_HEADS * GLA_V
GLA_RANK = 16
GLA_TAU = 16.0
GLA_CHUNK = 16
ROPE_BASE = 10000.0
EPS = 1e-6
LANES = 128
N_DEV = 8
SEG_NAMES = ("rq", "rk", "rv", "rg", "gq", "gk", "gv", "gg", "mr", "mg")
SEG_W = (1024, 1024, 2048, 2048, 512, 512, 1024, 1024, 1024, 1024)
SEG_OFF = tuple(sum(SEG_W[:i]) for i in range(len(SEG_W)))
AL_COLS = sum(SEG_W)
IN_COLS = AL_COLS + GLA_RANK
GLR_OFF = sum(SEG_W[:8])
IN_SHARD = IN_COLS // N_DEV


def _aligned_col(c):
    assert c <= GLR_OFF or c >= GLR_OFF + GLA_RANK
    return c if c <= GLR_OFF else c - GLA_RANK


SLAB_BOUND = tuple(_aligned_col(IN_SHARD * d) for d in range(N_DEV + 1))
SLAB_BLK0 = tuple(b // LANES for b in SLAB_BOUND[:-1])
SLAB_SHIFT = tuple(b % LANES for b in SLAB_BOUND[:-1])
SLAB_BLOCKS = max(-(-SLAB_BOUND[d + 1] // LANES) - SLAB_BLK0[d] for d in range(N_DEV))
SLAB_W = SLAB_BLOCKS * LANES
GLR_DEV = GLR_OFF // IN_SHARD
GLR_LOCAL = GLR_OFF - GLR_DEV * IN_SHARD
assert all(SLAB_BLK0[d] + SLAB_BLOCKS <= AL_COLS // LANES for d in range(N_DEV))
VMEM_LIMIT = 58 * 1024 * 1024
ADAM_LR, ADAM_B1, ADAM_B2, ADAM_EPS, ADAM_WD, ADAM_STEP = 0.001, 0.9, 0.999, 1e-08, 0.01, 10
ANY = pl.BlockSpec(memory_space=pl.ANY)
MESH = pl.DeviceIdType.MESH


def _call(body, name, **kw):
    return pl.pallas_call(body, name=name, **kw)


def _params(sem=None):
    return pltpu.CompilerParams(dimension_semantics=sem, vmem_limit_bytes=VMEM_LIMIT)


def _mm(a, b):
    return jnp.dot(a, b, preferred_element_type=F32)


def _mm_nt(a, b):
    return lax.dot_general(a, b, (((1,), (1,)), ((), ())), preferred_element_type=F32)


def _mm_tn(a, b):
    return lax.dot_general(a, b, (((0,), (0,)), ((), ())), preferred_element_type=F32)


def _sigmoid(x):
    return 1.0 / (1.0 + jnp.exp(-x))


def _rope(t, cos, sin):
    half = t.shape[-1] // 2
    t1, t2 = t[:, :half], t[:, half:]
    return jnp.concatenate([t1 * cos - t2 * sin, t2 * cos + t1 * sin], axis=-1)


def _rope_bwd(g, cos, sin):
    half = g.shape[-1] // 2
    g1, g2 = g[:, :half], g[:, half:]
    return jnp.concatenate([g1 * cos + g2 * sin, g2 * cos - g1 * sin], axis=-1)


def _row_mean(x):
    return jnp.mean(x, axis=-1, keepdims=True)


def _col_sum(x):
    return jnp.sum(x, axis=0, keepdims=True)


def _tile_rows(head_ref, x_ref):
    return jnp.where(pl.program_id(0) == 0, head_ref[...], x_ref[...])


def _head_spec():
    return pl.BlockSpec((TILE, D_MODEL), lambda i: (0, 0))


def _x_spec():
    return pl.BlockSpec((TILE, D_MODEL), lambda i: (jnp.maximum(i - 1, 0), 0))


def _slab_plan():
    interior, shared = [], []
    for d in range(N_DEV):
        lo, hi = -(-SLAB_BOUND[d] // LANES), SLAB_BOUND[d + 1] // LANES
        interior.append((d, LANES * (lo - SLAB_BLK0[d]), LANES * lo, LANES * (hi - lo)))
        if d + 1 < N_DEV and SLAB_BOUND[d + 1] % LANES:
            shared.append((hi, d, hi - SLAB_BLK0[d]))
    return interior, shared


W_SCRATCH = lambda: [pltpu.VMEM((D_MODEL, AL_COLS), BF16), pltpu.VMEM((D_MODEL, LANES), BF16),
                     pltpu.VMEM((2 * (N_DEV - 1), D_MODEL, LANES), BF16), pltpu.SemaphoreType.DMA((3 * N_DEV,))]


def _load_weight(slabs_hbm, wg_hbm, w_vm, wg_vm, edge_vm, sem):
    interior, shared = _slab_plan()
    copies = [pltpu.make_async_copy(wg_hbm, wg_vm, sem.at[0])]
    for d, src, dst, width in interior:
        copies.append(pltpu.make_async_copy(slabs_hbm.at[d, :, pl.ds(src, width)], w_vm.at[:, pl.ds(dst, width)], sem.at[1 + d]))
    for n, (_, d, blk) in enumerate(shared):
        copies.append(pltpu.make_async_copy(slabs_hbm.at[d, :, pl.ds(LANES * blk, LANES)], edge_vm.at[2 * n], sem.at[1 + N_DEV + 2 * n]))
        copies.append(pltpu.make_async_copy(slabs_hbm.at[d + 1, :, pl.ds(0, LANES)], edge_vm.at[2 * n + 1], sem.at[2 + N_DEV + 2 * n]))
    for cp in copies:
        cp.start()
    for cp in copies:
        cp.wait()
    for n, (blk, _, _) in enumerate(shared):
        w_vm[:, LANES * blk:LANES * (blk + 1)] = edge_vm[2 * n] + edge_vm[2 * n + 1]


def _proj_specs(names, n_units, where):
    specs = []
    for name in names:
        s = SEG_NAMES.index(name)
        nblk = SEG_W[s] // n_units // LANES
        base = SEG_OFF[s] // LANES
        assert base % nblk == 0
        specs.append(pl.BlockSpec((nblk, TILE, LANES), lambda *g, base=base, nblk=nblk: (base // nblk + where(*g)[0], where(*g)[1], 0)))
    return specs


def _cols(ref, unit=0, n_units=1):
    n = ref.shape[0] // n_units
    return ref[unit * n] if n == 1 else jnp.concatenate([ref[unit * n + j] for j in range(n)], axis=1)


def _inproj_tiles(head, x, g_norm, slabs, w_glr):
    t_rows = x.shape[0] + TILE
    nt = t_rows // TILE
    n_blocks = AL_COLS // LANES

    def body(head_ref, x_ref, g_ref, slabs_hbm, wg_hbm, ut_ref, proj_ref, glr_ref, w_vm, wg_vm, edge_vm, sem):
        @pl.when(pl.program_id(0) == 0)
        def _():
            _load_weight(slabs_hbm, wg_hbm, w_vm, wg_vm, edge_vm, sem)

        x = _tile_rows(head_ref, x_ref)
        r = lax.rsqrt(_row_mean(x * x) + EPS)
        u32 = (x * r * g_ref[...]).astype(BF16).astype(F32)
        u = u32.astype(BF16)
        ut_ref[...] = u32.T.astype(BF16)
        for s in range(len(SEG_W)):
            res = _mm(u, w_vm[:, SEG_OFF[s]:SEG_OFF[s] + SEG_W[s]]).astype(BF16)
            for j in range(SEG_W[s] // LANES):
                proj_ref[SEG_OFF[s] // LANES + j] = res[:, j * LANES:(j + 1) * LANES]
        glr_ref[...] = _mm(u, wg_vm[...])

    return _call(
        body, "inproj_fwd_tiles", grid=(nt,),
        out_shape=[jax.ShapeDtypeStruct((nt, D_MODEL, TILE), BF16), jax.ShapeDtypeStruct((n_blocks, t_rows, LANES), BF16),
                   jax.ShapeDtypeStruct((t_rows, LANES), F32)],
        in_specs=[_head_spec(), _x_spec(), pl.BlockSpec((1, D_MODEL), lambda i: (0, 0)), ANY, ANY],
        out_specs=[pl.BlockSpec((None, D_MODEL, TILE), lambda i: (i, 0, 0)), pl.BlockSpec((n_blocks, TILE, LANES), lambda i: (0, i, 0)),
                   pl.BlockSpec((TILE, LANES), lambda i: (i, 0))],
        scratch_shapes=W_SCRATCH(), compiler_params=_params(("arbitrary",)),
    )(head, x, g_norm, slabs, w_glr)


def _ret_decay(lgh):
    i = lax.broadcasted_iota(jnp.int32, (TILE, TILE), 0)
    j = lax.broadcasted_iota(jnp.int32, (TILE, TILE), 1)
    rel = (i - j).astype(F32)
    return jnp.where(rel >= 0, jnp.exp(jnp.maximum(rel, 0.0) * lgh), 0.0)


def _ret_vectors(lgh):
    idx = lax.broadcasted_iota(jnp.int32, (TILE, 1), 0).astype(F32)
    xi = jnp.exp((idx + 1.0) * lgh)
    zeta = jnp.exp((TILE - 1.0 - idx) * lgh)
    gc = jnp.exp(jnp.full((1, 1), float(TILE), F32) * lgh)
    return xi, zeta, gc


def _rope_tables(nt):
    half = RET_QK // 2
    inv = ROPE_BASE ** (-jnp.arange(half, dtype=F32) / half)
    base = (jnp.arange(nt, dtype=F32) * TILE - float(PAD_ROWS))[:, None, None] * inv[None, None, :]
    off = jnp.arange(TILE, dtype=F32)[:, None] * inv[None, :]
    return jnp.cos(base), jnp.sin(base), jnp.cos(off), jnp.sin(off)


def _rope_specs(tile_of):
    return [pl.BlockSpec((None, 1, RET_QK // 2), lambda i: (tile_of(i), 0, 0))] * 2 + [pl.BlockSpec((TILE, RET_QK // 2), lambda i: (0, 0))] * 2


def _rope_angles(cb_ref, sb_ref, co_ref, so_ref):
    cb, sb, co, so = cb_ref[...], sb_ref[...], co_ref[...], so_ref[...]
    return cb * co - sb * so, sb * co + cb * so


def _ret_fwd(proj, rope, gain, lg, row_shards):
    t_rows = proj.shape[1]
    nt = t_rows // TILE
    ns = len(row_shards)

    def body(lg_ref, q_ref, k_ref, v_ref, g_ref, cb_ref, sb_ref, co_ref, so_ref, gain_ref, *rest):
        shard_refs, (oraw_ref, oret_ref, st_ref), gathered = rest[:ns], rest[ns:ns + 3], rest[ns + 3:2 * ns + 3]
        s_acc, dm = rest[2 * ns + 3:2 * ns + 5]
        gather = _Exchange(shard_refs, gathered, rest[2 * ns + 5:], among_chips=False)
        t = pl.program_id(0)

        @pl.when(t == 0)
        def _():
            gather.start()
            s_acc[...] = jnp.zeros_like(s_acc)
            for h in range(RET_HEADS):
                dm[h] = _ret_decay(lg_ref[h])

        @pl.when(t == nt - 1)
        def _():
            gather.finish()

        cos_t, sin_t = _rope_angles(cb_ref, sb_ref, co_ref, so_ref)
        for h in range(RET_HEADS):
            lgh = lg_ref[h]
            q = _rope(_cols(q_ref, h, RET_HEADS).astype(F32), cos_t, sin_t)
            k = _rope(_cols(k_ref, h, RET_HEADS).astype(F32), cos_t, sin_t) * (RET_QK ** -0.5)
            xi, zeta, gc = _ret_vectors(lgh)
            v = _cols(v_ref, h, RET_HEADS)
            s_in = s_acc[h]
            p = (_mm_nt(q.astype(BF16), k.astype(BF16)) * dm[h]).astype(BF16)
            o = _mm(p, v) + _mm((q * xi).astype(BF16), s_in.astype(BF16))
            st_ref[h] = s_in.astype(BF16)
            s_acc[h] = s_in * gc + _mm_tn((k * zeta).astype(BF16), v)
            cols = slice(h * RET_V, (h + 1) * RET_V)
            oraw_ref[:, cols] = o
            oc = o - _row_mean(o)
            n = oc * lax.rsqrt(_row_mean(oc * oc) + EPS) * gain_ref[:, cols]
            g = _cols(g_ref, h, RET_HEADS).astype(F32)
            oret_ref[:, cols] = (n * g * _sigmoid(g)).astype(BF16)

    row = lambda w: pl.BlockSpec((TILE, w), lambda t: (t, 0))
    outs = _call(
        body, "ret_fwd", grid=(nt,),
        out_shape=[jax.ShapeDtypeStruct((t_rows, RET_W), F32), jax.ShapeDtypeStruct((t_rows, RET_W), BF16),
                   jax.ShapeDtypeStruct((RET_HEADS, nt, RET_QK, RET_V), BF16)]
                  + [jax.ShapeDtypeStruct((N_DEV, *a.shape), a.dtype) for a in row_shards],
        in_specs=[pl.BlockSpec(memory_space=pltpu.SMEM)] + _proj_specs(("rq", "rk", "rv", "rg"), 1, lambda t: (0, t)) + _rope_specs(lambda t: t) + [
                  pl.BlockSpec((1, RET_W), lambda t: (0, 0))] + [ANY] * ns,
        out_specs=[row(RET_W), row(RET_W), pl.BlockSpec((RET_HEADS, None, RET_QK, RET_V), lambda t: (0, t, 0, 0))] + [ANY] * ns,
        scratch_shapes=[pltpu.VMEM((RET_HEADS, RET_QK, RET_V), F32), pltpu.VMEM((RET_HEADS, TILE, TILE), F32)] + _exchange_sems(ns, N_DEV),
        compiler_params=_params(("arbitrary",)),
    )(lg, proj, proj, proj, proj, *rope, gain, *row_shards)
    return outs[0], outs[1], outs[2], outs[3:]


def _ret_bwd(proj, rope, gain, lg, o_raw, do_ret, states):
    t_rows = proj.shape[1]
    nt = t_rows // TILE

    def body(lg_ref, q_ref, k_ref, v_ref, g_ref, cb_ref, sb_ref, co_ref, so_ref, gain_ref, oraw_ref, do_ref, st_ref,
             dq_ref, dk_ref, dv_ref, dg_ref, dgain_ref, e_acc, dm):
        @pl.when(pl.program_id(0) == 0)
        def _():
            e_acc[...] = jnp.zeros_like(e_acc)
            for h in range(RET_HEADS):
                dm[h] = _ret_decay(lg_ref[h])
            dgain_ref[...] = jnp.zeros_like(dgain_ref)

        cos_t, sin_t = _rope_angles(cb_ref, sb_ref, co_ref, so_ref)
        for h in range(RET_HEADS):
            lgh = lg_ref[h]
            cols = slice(h * RET_V, (h + 1) * RET_V)
            qcols = slice(h * RET_QK, (h + 1) * RET_QK)
            q = _rope(_cols(q_ref, h, RET_HEADS).astype(F32), cos_t, sin_t)
            k = _rope(_cols(k_ref, h, RET_HEADS).astype(F32), cos_t, sin_t) * (RET_QK ** -0.5)
            xi, zeta, gc = _ret_vectors(lgh)
            v = _cols(v_ref, h, RET_HEADS)
            g = _cols(g_ref, h, RET_HEADS).astype(F32)
            o = oraw_ref[:, cols]
            do = do_ref[:, cols].astype(F32)
            oc = o - _row_mean(o)
            rstd = lax.rsqrt(_row_mean(oc * oc) + EPS)
            xh = oc * rstd
            gain_t = gain_ref[:, cols]
            sg = _sigmoid(g)
            dn = do * (g * sg)
            dg_ref[:, cols] = (do * (xh * gain_t) * (sg * (1.0 + g * (1.0 - sg)))).astype(BF16)
            dgain_ref[:, cols] += _col_sum(dn * xh)
            dxh = dn * gain_t
            dob = (rstd * (dxh - _row_mean(dxh) - xh * _row_mean(dxh * xh))).astype(BF16)
            dmat = dm[h]
            qb, kb = q.astype(BF16), k.astype(BF16)
            p = (_mm_nt(qb, kb) * dmat).astype(BF16)
            dp = (_mm_nt(dob, v) * dmat).astype(BF16)
            s_in = st_ref[h]
            e_in = e_acc[h]
            e_b = e_in.astype(BF16)
            dq = _mm(dp, kb) + _mm_nt(dob, s_in) * xi
            dk = _mm_tn(dp, qb) + _mm_nt(v, e_b) * zeta
            dv_ref[:, cols] = (_mm_tn(p, dob) + _mm((k * zeta).astype(BF16), e_b)).astype(BF16)
            e_acc[h] = e_in * gc + _mm_tn((q * xi).astype(BF16), dob)
            dq_ref[:, qcols] = _rope_bwd(dq, cos_t, sin_t).astype(BF16)
            dk_ref[:, qcols] = (_rope_bwd(dk, cos_t, sin_t) * (RET_QK ** -0.5)).astype(BF16)

    row = lambda w: pl.BlockSpec((TILE, w), lambda j: (nt - 1 - j, 0))
    vec = pl.BlockSpec((1, RET_W), lambda j: (0, 0))
    return _call(
        body, "ret_bwd", grid=(nt,),
        out_shape=[jax.ShapeDtypeStruct((t_rows, RET_HEADS * RET_QK), BF16), jax.ShapeDtypeStruct((t_rows, RET_HEADS * RET_QK), BF16),
                   jax.ShapeDtypeStruct((t_rows, RET_W), BF16), jax.ShapeDtypeStruct((t_rows, RET_W), BF16),
                   jax.ShapeDtypeStruct((1, RET_W), F32)],
        in_specs=[pl.BlockSpec(memory_space=pltpu.SMEM)] + _proj_specs(("rq", "rk", "rv", "rg"), 1, lambda j: (0, nt - 1 - j)) + _rope_specs(lambda j: nt - 1 - j) + [vec,
                  row(RET_W), row(RET_W), pl.BlockSpec((RET_HEADS, None, RET_QK, RET_V), lambda j: (0, nt - 1 - j, 0, 0))],
        out_specs=[row(RET_HEADS * RET_QK), row(RET_HEADS * RET_QK), row(RET_W), row(RET_W), vec],
        scratch_shapes=[pltpu.VMEM((RET_HEADS, RET_QK, RET_V), F32), pltpu.VMEM((RET_HEADS, TILE, TILE), F32)],
        compiler_params=_params(("arbitrary",)),
    )(lg, proj, proj, proj, proj, *rope, gain, o_raw, do_ret, states)


GLA_LEVELS = (32, 64, 128, 256)
N_TERMS = 1 + len(GLA_LEVELS)


def _gla_tables():
    p = jnp.arange(TILE)[:, None]
    r = jnp.arange(TILE)[None, :]
    masks = [(p // GLA_CHUNK == r // GLA_CHUNK) & (r <= p)]
    for blk in GLA_LEVELS:
        masks.append((p // blk == r // blk) & (p % blk >= blk // 2) & (r % blk < blk // 2))
    masks = jnp.stack(masks + [m.T for m in masks]).astype(F32)
    cum_fwd = jnp.concatenate([r <= p, masks[0] > 0], axis=0).astype(BF16)
    cum_bwd = jnp.concatenate([r >= p, masks[N_TERMS] > 0], axis=1).astype(BF16)
    return masks, cum_fwd, cum_bwd


def _split3(x):
    hi = x.astype(BF16)
    rest = x - hi.astype(F32)
    mid = rest.astype(BF16)
    lo = (rest - mid.astype(F32)).astype(BF16)
    return jnp.concatenate([hi, mid, lo], axis=1)


def _join3(y):
    w = y.shape[1] // 3
    return (y[:, 2 * w:] + y[:, w:2 * w]) + y[:, :w]


def _gla_decays(glr_ref, wgu_ref, b_ref, cum_ref):
    z = _mm(glr_ref[...].astype(BF16), wgu_ref[...].astype(BF16)) + b_ref[...]
    la = (jnp.minimum(z, 0.0) - jnp.log(1.0 + jnp.exp(-jnp.abs(z)))) / GLA_TAU
    width = la.shape[1]
    hi = la.astype(BF16)
    rest = la - hi.astype(F32)
    mid = rest.astype(BF16)
    lo = (rest - mid.astype(F32)).astype(BF16)
    y = _mm(cum_ref[...], jnp.concatenate([hi, mid, lo], axis=1))
    gb = (y[:, 2 * width:] + y[:, width:2 * width]) + y[:, :width]
    return z, gb[:TILE], gb[TILE:]


def _gla_prep(h, q_ref, k_ref, g_all, b_all, g_scr, ref_scr):
    cols = slice(h * GLA_K, (h + 1) * GLA_K)
    g, b = g_all[:, cols], b_all[:, cols]
    g_scr[h] = g
    factors = [(jnp.exp(b), jnp.exp(-b))]
    for lvl, blk in enumerate(GLA_LEVELS):
        for n in range(TILE // blk):
            ref_scr[h, lvl, n * blk:(n + 1) * blk, :] = jnp.broadcast_to(g_scr[h, pl.ds(n * blk + blk // 2 - 1, 1), :], (blk, GLA_K))
        x = g - ref_scr[h, lvl]
        factors.append((jnp.exp(jnp.minimum(x, 0.0)), jnp.exp(jnp.minimum(-x, 0.0))))
    g_last = g_scr[h, pl.ds(TILE - 1, 1), :]
    q = _cols(q_ref, h, GLA_HEADS).astype(F32) * (GLA_K ** -0.5)
    k = _cols(k_ref, h, GLA_HEADS).astype(F32)
    return q, k, factors, jnp.exp(g), jnp.exp(g_last), jnp.exp(g_last - g)


def _gla_scores(q, k, factors, m_ref):
    a = jnp.zeros((TILE, TILE), F32)
    for l, (fq, fk) in enumerate(factors):
        s = _mm_nt((q * fq).astype(BF16), (k * fk).astype(BF16))
        a = jnp.where(m_ref[l] > 0.0, s, a)
    return a


def _gla_fwd(proj, glr, wgu_pad, b_gate, gain, masks, cum_fwd):
    t_rows = glr.shape[0]
    nt = t_rows // TILE

    def body(q_ref, k_ref, v_ref, g_ref, glr_ref, wgu_ref, b_ref, gain_ref, m_ref, cum_ref, oraw_ref, ogla_ref, st_ref, at_ref,
             s_acc, g_scr, ref_scr):
        @pl.when(pl.program_id(0) == 0)
        def _():
            s_acc[...] = jnp.zeros_like(s_acc)

        _, g_all, b_all = _gla_decays(glr_ref, wgu_ref, b_ref, cum_ref)
        for h in range(GLA_HEADS):
            q, k, factors, e_g, e_last, e_end = _gla_prep(h, q_ref, k_ref, g_all, b_all, g_scr, ref_scr)
            v = _cols(v_ref, h, GLA_HEADS)
            st = s_acc[h]
            st_ref[h] = st
            a = _gla_scores(q, k, factors, m_ref)
            at_ref[h] = a.T.astype(BF16)
            o = _mm(a.astype(BF16), v) + _mm_nt((q * e_g).astype(BF16), st.astype(BF16))
            s_acc[h] = st * e_last + _mm(v.astype(F32).T.astype(BF16), (k * e_end).astype(BF16))
            cols = slice(h * GLA_V, (h + 1) * GLA_V)
            oraw_ref[:, cols] = o
            n = o * lax.rsqrt(_row_mean(o * o) + EPS) * gain_ref[:, cols]
            g = _cols(g_ref, h, GLA_HEADS).astype(F32)
            ogla_ref[:, cols] = (n * g * _sigmoid(g)).astype(BF16)

    row = lambda w: pl.BlockSpec((TILE, w), lambda t: (t, 0))
    whole = lambda *shape: pl.BlockSpec(shape, lambda t: (0,) * len(shape))
    return _call(
        body, "gla_fwd", grid=(nt,),
        out_shape=[jax.ShapeDtypeStruct((t_rows, GLA_W), F32), jax.ShapeDtypeStruct((t_rows, GLA_W), BF16),
                   jax.ShapeDtypeStruct((GLA_HEADS, nt, GLA_V, GLA_K), F32), jax.ShapeDtypeStruct((GLA_HEADS, t_rows, TILE), BF16)],
        in_specs=_proj_specs(("gq", "gk", "gv", "gg"), 1, lambda t: (0, t)) + [row(LANES), whole(LANES, GLA_HEADS * GLA_K),
                  whole(1, GLA_HEADS * GLA_K), whole(1, GLA_W), whole(N_TERMS, TILE, TILE), whole(2 * TILE, TILE)],
        out_specs=[row(GLA_W), row(GLA_W), pl.BlockSpec((GLA_HEADS, None, GLA_V, GLA_K), lambda t: (0, t, 0, 0)),
                   pl.BlockSpec((GLA_HEADS, TILE, TILE), lambda t: (0, t, 0))],
        scratch_shapes=[pltpu.VMEM((GLA_HEADS, GLA_V, GLA_K), F32), pltpu.VMEM((GLA_HEADS, TILE, GLA_K), F32),
                        pltpu.VMEM((GLA_HEADS, len(GLA_LEVELS), TILE, GLA_K), F32)],
        compiler_params=_params(("arbitrary",)),
    )(proj, proj, proj, proj, glr, wgu_pad, b_gate, gain, masks, cum_fwd)


def _gla_bwd(proj, glr, wgu_pad, b_gate, gain, o_raw, do_gla, states, a_t, masks, cum_fwd, cum_bwd):
    t_rows = glr.shape[0]
    nt = t_rows // TILE

    def body(q_ref, k_ref, v_ref, g_ref, glr_ref, wgu_ref, b_ref, gain_ref, m_ref, cum_ref, cumb_ref, oraw_ref, do_ref, st_ref, at_ref,
             dq_ref, dk_ref, dv_ref, dg_ref, dglr_ref, dwgu_ref, dbg_ref, dgain_ref, d_acc, g_scr, ref_scr, dref_scr):
        @pl.when(pl.program_id(0) == 0)
        def _():
            d_acc[...] = jnp.zeros_like(d_acc)
            dwgu_ref[...] = jnp.zeros_like(dwgu_ref)
            dbg_ref[...] = jnp.zeros_like(dbg_ref)
            dgain_ref[...] = jnp.zeros_like(dgain_ref)

        z_all, g_all, b_all = _gla_decays(glr_ref, wgu_ref, b_ref, cum_ref)
        dla_parts = []
        for h in range(GLA_HEADS):
            q, k, factors, e_g, e_last, e_end = _gla_prep(h, q_ref, k_ref, g_all, b_all, g_scr, ref_scr)
            v = _cols(v_ref, h, GLA_HEADS)
            cols = slice(h * GLA_V, (h + 1) * GLA_V)
            kcols = slice(h * GLA_K, (h + 1) * GLA_K)
            o = oraw_ref[:, cols]
            do = do_ref[:, cols].astype(F32)
            g = _cols(g_ref, h, GLA_HEADS).astype(F32)
            rinv = lax.rsqrt(_row_mean(o * o) + EPS)
            nh = o * rinv
            gain_t = gain_ref[:, cols]
            sg = _sigmoid(g)
            dn = do * (g * sg)
            dg_ref[:, cols] = (do * (nh * gain_t) * (sg * (1.0 + g * (1.0 - sg)))).astype(BF16)
            dgain_ref[:, cols] += _col_sum(dn * nh)
            dnh = dn * gain_t
            dor = rinv * (dnh - nh * _row_mean(dnh * nh))
            dob = dor.astype(BF16)
            a_t = at_ref[h]
            da = _mm_nt(dob, v).astype(BF16)
            da_t = _mm_nt(v, dob).astype(BF16)
            st_in = st_ref[h]
            d_out = d_acc[h]
            d_out_b = d_out.astype(BF16)
            qg, kg = q * e_g, k * e_end
            dqg = _mm(dob, st_in.astype(BF16))
            dkg = _mm(v, d_out_b)
            dv_ref[:, cols] = (_mm(a_t, dob) + _mm_nt(kg.astype(BF16), d_out_b)).astype(BF16)
            d_acc[h] = d_out * e_last + _mm(dor.T.astype(BF16), qg.astype(BF16))
            dq = dqg * e_g
            dk = dkg * e_end
            dkg_kg = dkg * kg
            dg_cum = dqg * qg - dkg_kg
            db = None
            for l, (fq, fk) in enumerate(factors):
                qt, kt = q * fq, k * fk
                dqt = _mm(da * m_ref[l], kt.astype(BF16))
                dkt = _mm(da_t * m_ref[N_TERMS + l], qt.astype(BF16))
                dq = dq + dqt * fq
                dk = dk + dkt * fk
                diff = dqt * qt - dkt * kt
                if l == 0:
                    db = diff
                else:
                    dg_cum = dg_cum + diff
                    dref_scr[h, l - 1] = diff
            dq_ref[:, kcols] = (dq * (GLA_K ** -0.5)).astype(BF16)
            dk_ref[:, kcols] = dk.astype(BF16)
            g_scr[h] = dg_cum
            g_scr[h, pl.ds(TILE - 1, 1), :] += e_last * _col_sum(d_out * st_in) + _col_sum(dkg_kg)
            for lvl, blk in enumerate(GLA_LEVELS):
                for n in range(TILE // blk):
                    g_scr[h, pl.ds(n * blk + blk // 2 - 1, 1), :] -= _col_sum(dref_scr[h, lvl, n * blk:(n + 1) * blk, :])
            dla_parts.append(_join3(_mm(cumb_ref[...], jnp.concatenate([_split3(g_scr[h]), _split3(db)], axis=0))))
        dz = jnp.concatenate(dla_parts, axis=1) * (1.0 / GLA_TAU) * _sigmoid(-z_all)
        dzb = dz.astype(BF16)
        wgu_b = wgu_ref[...].astype(BF16)
        for h in range(GLA_HEADS):
            kcols = slice(h * GLA_K, (h + 1) * GLA_K)
            dglr_ref[h] = _mm_nt(dzb[:, kcols], wgu_b[:, kcols]).astype(BF16)
        dwgu_ref[...] += _mm(glr_ref[...].T.astype(BF16), dzb)
        dbg_ref[...] += _col_sum(dz)

    row = lambda w: pl.BlockSpec((TILE, w), lambda j: (nt - 1 - j, 0))
    whole = lambda *shape: pl.BlockSpec(shape, lambda j: (0,) * len(shape))
    return _call(
        body, "gla_bwd", grid=(nt,),
        out_shape=[jax.ShapeDtypeStruct((t_rows, GLA_HEADS * GLA_K), BF16), jax.ShapeDtypeStruct((t_rows, GLA_HEADS * GLA_K), BF16),
                   jax.ShapeDtypeStruct((t_rows, GLA_W), BF16), jax.ShapeDtypeStruct((t_rows, GLA_W), BF16),
                   jax.ShapeDtypeStruct((GLA_HEADS, t_rows, LANES), BF16), jax.ShapeDtypeStruct((LANES, GLA_HEADS * GLA_K), F32),
                   jax.ShapeDtypeStruct((1, GLA_HEADS * GLA_K), F32), jax.ShapeDtypeStruct((1, GLA_W), F32)],
        in_specs=_proj_specs(("gq", "gk", "gv", "gg"), 1, lambda j: (0, nt - 1 - j)) + [row(LANES),
                  whole(LANES, GLA_HEADS * GLA_K), whole(1, GLA_HEADS * GLA_K), whole(1, GLA_W),
                  whole(2 * N_TERMS, TILE, TILE), whole(2 * TILE, TILE), whole(TILE, 2 * TILE), row(GLA_W), row(GLA_W),
                  pl.BlockSpec((GLA_HEADS, None, GLA_V, GLA_K), lambda j: (0, nt - 1 - j, 0, 0)),
                  pl.BlockSpec((GLA_HEADS, TILE, TILE), lambda j: (0, nt - 1 - j, 0))],
        out_specs=[row(GLA_HEADS * GLA_K), row(GLA_HEADS * GLA_K), row(GLA_W), row(GLA_W),
                   pl.BlockSpec((GLA_HEADS, TILE, LANES), lambda j: (0, nt - 1 - j, 0)), whole(LANES, GLA_HEADS * GLA_K),
                   whole(1, GLA_HEADS * GLA_K), whole(1, GLA_W)],
        scratch_shapes=[pltpu.VMEM((GLA_HEADS, GLA_V, GLA_K), F32), pltpu.VMEM((GLA_HEADS, TILE, GLA_K), F32),
                        pltpu.VMEM((GLA_HEADS, len(GLA_LEVELS), TILE, GLA_K), F32),
                        pltpu.VMEM((GLA_HEADS, len(GLA_LEVELS), TILE, GLA_K), F32)],
        compiler_params=_params(("arbitrary",)),
    )(proj, proj, proj, proj, glr, wgu_pad, b_gate, gain, masks.astype(BF16), cum_fwd, cum_bwd, o_raw, do_gla, states, a_t)


def _merge_fwd_bwd(o_ret, o_gla, proj, x, target, g_final, w_br, w_bg, w_out):
    t_rows = x.shape[0] + TILE
    nt = t_rows // TILE

    def body(oret_ref, ogla_ref, mr_ref, mg_ref, h0_ref, tgt_ref, gf_ref, wbr_hbm, wbg_hbm, wout_hbm,
             dh1_ref, dmr_ref, dmg_ref, doret_ref, dogla_ref, loss_ref, dgf_ref, dwbr_hbm, dwbg_hbm, dwout_hbm,
             wbr, wbg, wout, abr, abg, aout, sem):
        i = pl.program_id(0)

        @pl.when(i == 0)
        def _():
            cps = [pltpu.make_async_copy(s, d, sem.at[n]) for n, (s, d) in enumerate(((wbr_hbm, wbr), (wbg_hbm, wbg), (wout_hbm, wout)))]
            for cp in cps:
                cp.start()
            abr[...] = jnp.zeros_like(abr)
            abg[...] = jnp.zeros_like(abg)
            aout[...] = jnp.zeros_like(aout)
            loss_ref[...] = jnp.zeros_like(loss_ref)
            dgf_ref[...] = jnp.zeros_like(dgf_ref)
            for cp in cps:
                cp.wait()
            dh1_ref[...] = jnp.zeros_like(dh1_ref)
            dmr_ref[...] = jnp.zeros_like(dmr_ref)
            dmg_ref[...] = jnp.zeros_like(dmg_ref)
            doret_ref[...] = jnp.zeros_like(doret_ref)
            dogla_ref[...] = jnp.zeros_like(dogla_ref)

        @pl.when(i > 0)
        def _():
            oret, ogla = oret_ref[...], ogla_ref[...]
            br, bg = _mm(oret, wbr[...]), _mm(ogla, wbg[...])
            sr, sg = _sigmoid(_cols(mr_ref).astype(F32)), _sigmoid(_cols(mg_ref).astype(F32))
            mb = (sr * br + sg * bg).astype(BF16)
            h1 = h0_ref[...] + _mm(mb, wout[...])
            r2 = lax.rsqrt(_row_mean(h1 * h1) + EPS)
            hn = h1 * r2
            gf = gf_ref[...]
            diff = hn * gf - tgt_ref[...]
            loss_ref[...] += 0.5 * jnp.sum(_row_mean(diff * diff))
            dy = diff * (1.0 / D_MODEL)
            dgf_ref[...] += _col_sum(dy * hn)
            dyg = dy * gf
            dh1 = r2 * (dyg - hn * _row_mean(dyg * hn))
            dh1_ref[...] = dh1
            dh1b = dh1.astype(BF16)
            dm = _mm_nt(dh1b, wout[...])
            aout[...] += _mm_tn(mb, dh1b)
            dbr = (dm * sr).astype(BF16)
            dbg = (dm * sg).astype(BF16)
            dmr_ref[...] = (dm * br * sr * (1.0 - sr)).astype(BF16)
            dmg_ref[...] = (dm * bg * sg * (1.0 - sg)).astype(BF16)
            doret_ref[...] = _mm_nt(dbr, wbr[...]).astype(BF16)
            dogla_ref[...] = _mm_nt(dbg, wbg[...]).astype(BF16)
            abr[...] += _mm_tn(oret, dbr)
            abg[...] += _mm_tn(ogla, dbg)

        @pl.when(i == nt - 1)
        def _():
            wbr[...] = abr[...].astype(BF16)
            wbg[...] = abg[...].astype(BF16)
            wout[...] = aout[...].astype(BF16)
            pltpu.sync_copy(wbr, dwbr_hbm)
            pltpu.sync_copy(wbg, dwbg_hbm)
            pltpu.sync_copy(wout, dwout_hbm)

    row = lambda w: pl.BlockSpec((TILE, w), lambda i: (i, 0))
    one = lambda w: pl.BlockSpec((1, w), lambda i: (0, 0))
    return _call(
        body, "merge_fwd_bwd", grid=(nt,),
        out_shape=[jax.ShapeDtypeStruct((t_rows, D_MODEL), F32), jax.ShapeDtypeStruct((t_rows, D_MODEL), BF16),
                   jax.ShapeDtypeStruct((t_rows, D_MODEL), BF16), jax.ShapeDtypeStruct((t_rows, RET_W), BF16),
                   jax.ShapeDtypeStruct((t_rows, GLA_W), BF16), jax.ShapeDtypeStruct((1, LANES), F32),
                   jax.ShapeDtypeStruct((1, D_MODEL), F32), jax.ShapeDtypeStruct((RET_W, D_MODEL), BF16),
                   jax.ShapeDtypeStruct((GLA_W, D_MODEL), BF16), jax.ShapeDtypeStruct((D_MODEL, D_MODEL), BF16)],
        in_specs=[row(RET_W), row(GLA_W)] + _proj_specs(("mr", "mg"), 1, lambda i: (0, i)) + [_x_spec(), _x_spec(), one(D_MODEL), ANY, ANY, ANY],
        out_specs=[row(D_MODEL), row(D_MODEL), row(D_MODEL), row(RET_W), row(GLA_W), one(LANES), one(D_MODEL), ANY, ANY, ANY],
        scratch_shapes=[pltpu.VMEM((RET_W, D_MODEL), BF16), pltpu.VMEM((GLA_W, D_MODEL), BF16), pltpu.VMEM((D_MODEL, D_MODEL), BF16),
                        pltpu.VMEM((RET_W, D_MODEL), F32), pltpu.VMEM((GLA_W, D_MODEL), F32), pltpu.VMEM((D_MODEL, D_MODEL), F32),
                        pltpu.SemaphoreType.DMA((3,))],
        compiler_params=_params(("arbitrary",)),
    )(o_ret, o_gla, proj, proj, x, target, g_final, w_br, w_bg, w_out)


def _inproj_bwd_x(dseg, dglr, head, x, dh1, g_norm, slabs, w_glr, chip_partials):
    t_rows = x.shape[0] + TILE
    nt = t_rows // TILE
    ne = len(chip_partials)

    def body(*refs):
        d_refs = refs[:10]
        dglr_ref, head_ref, x_ref, dh1_ref, g_ref, slabs_hbm, wg_hbm = refs[10:17]
        part_refs = refs[17:17 + ne]
        dx_ref, dhead_ref, dgn_ref = refs[17 + ne:20 + ne]
        landed = refs[20 + ne:20 + 2 * ne]
        w_vm, wg_vm, edge_vm, sem = refs[20 + 2 * ne:24 + 2 * ne]
        exchange = _Exchange(part_refs, landed, refs[24 + 2 * ne:], among_chips=True)

        @pl.when(pl.program_id(0) == 0)
        def _():
            exchange.start()
            dgn_ref[...] = jnp.zeros_like(dgn_ref)
            _load_weight(slabs_hbm, wg_hbm, w_vm, wg_vm, edge_vm, sem)

        @pl.when(pl.program_id(0) == nt - 1)
        def _():
            exchange.finish()

        dglr = dglr_ref[0].astype(F32)
        for h in range(1, GLA_HEADS):
            dglr = dglr + dglr_ref[h].astype(F32)
        du = _mm_nt(dglr.astype(BF16), wg_vm[...])
        for s, d_ref in enumerate(d_refs):
            du = du + _mm_nt(d_ref[...], w_vm[:, SEG_OFF[s]:SEG_OFF[s] + SEG_W[s]])
        x = _tile_rows(head_ref, x_ref)
        r = lax.rsqrt(_row_mean(x * x) + EPS)
        hn = x * r
        dgn_ref[...] += _col_sum(du * hn)
        dug = du * g_ref[...]
        dh0 = dh1_ref[...] + r * (dug - hn * _row_mean(dug * hn))
        dx_ref[...] = dh0

        @pl.when(pl.program_id(0) == 0)
        def _():
            dhead_ref[...] = dh0

    row = lambda w: pl.BlockSpec((TILE, w), lambda i: (i, 0))
    one = pl.BlockSpec((1, D_MODEL), lambda i: (0, 0))
    return _call(
        body, "inproj_bwd_x", grid=(nt,),
        out_shape=[jax.ShapeDtypeStruct((t_rows - TILE, D_MODEL), F32), jax.ShapeDtypeStruct((TILE, D_MODEL), F32),
                   jax.ShapeDtypeStruct((1, D_MODEL), F32)] + [jax.ShapeDtypeStruct(a.shape, a.dtype) for a in chip_partials],
        in_specs=[row(w) for w in SEG_W] + [pl.BlockSpec((GLA_HEADS, TILE, LANES), lambda i: (0, i, 0)),
                                            _head_spec(), _x_spec(), row(D_MODEL), one, ANY, ANY] + [ANY] * ne,
        out_specs=[_x_spec(), _head_spec(), one] + [ANY] * ne,
        scratch_shapes=W_SCRATCH() + _exchange_sems(ne, N_CHIP),
        compiler_params=_params(("arbitrary",)),
    )(*[dseg[n] for n in SEG_NAMES], dglr, head, x, dh1, g_norm, slabs, w_glr, *chip_partials)


W_TILE = 512


def _inproj_bwd_w(ut, dseg, dglr, row_sends):
    nt = ut.shape[0]
    t_rows = nt * TILE
    kc = 3 if nt % 3 == 0 else 1
    tiles = [(s, c) for s in range(len(SEG_W)) for c in range(0, SEG_W[s], W_TILE)]
    bpt = W_TILE // LANES
    nr = len(row_sends)
    n = 1 + nr
    last_tile = [(SLAB_BLK0[d] + SLAB_BLOCKS - 1) // bpt for d in range(N_DEV)]

    def body(ut_hbm, *refs):
        d_refs, dglr_hbm, row_refs = refs[:10], refs[10], refs[11:11 + nr]
        out_hbm, oglr_ref, sib = refs[11 + nr], refs[12 + nr], refs[13 + nr:13 + nr + n]
        ut_vm, dbuf, obuf, acc, gbuf, sem, send_sems, recv_sems = refs[13 + nr + n:]
        x, y, core = _position()

        def handover(d, k, landed=False):
            q = d // 2
            src = out_hbm.at[pl.ds(SLAB_BLK0[d], SLAB_BLOCKS)] if k == 0 else row_refs[k - 1].at[d]
            return pltpu.make_async_remote_copy(src_ref=sib[k].at[q] if landed else src, dst_ref=sib[k].at[q],
                                                send_sem=send_sems.at[n * q + k], recv_sem=recv_sems.at[n * q + k],
                                                device_id=(x, y, 1 - core), device_id_type=MESH)

        def for_sibling(d, ks, fn):
            @pl.when(d % 2 != core)
            def _():
                for k in ks:
                    fn(handover(d, k))

        for d in range(N_DEV):
            for_sibling(d, range(1, n), lambda cp: cp.start())

        def fetch(i):
            s, c = tiles[i]
            return pltpu.make_async_copy(d_refs[s].at[:, pl.ds(c, W_TILE)], dbuf.at[i % 2], sem.at[1 + i % 2])

        def contract(rhs_refs, width):
            acc[:, :width] = jnp.zeros((D_MODEL, width), F32)

            def step(k, carry):
                part = None
                for j in range(kc):
                    kk = k * kc + j
                    for rhs_ref in rhs_refs:
                        prod = _mm(ut_vm[kk], rhs_ref[pl.ds(pl.multiple_of(kk * TILE, TILE), TILE), :])
                        part = prod if part is None else part + prod
                acc[:, :width] += part
                return carry

            lax.fori_loop(0, nt // kc, step, 0)
            return acc[:, :width]

        load_ut = pltpu.make_async_copy(ut_hbm, ut_vm, sem.at[0])
        load_glr = pltpu.make_async_copy(dglr_hbm, gbuf, sem.at[5])
        load_ut.start()
        load_glr.start()
        fetch(0).start()
        load_ut.wait()
        stores = {}

        def stored(i):
            stores[i].wait()
            for d in range(N_DEV):
                if last_tile[d] == i:
                    for_sibling(d, [0], lambda cp: cp.start())

        for i, (s, c) in enumerate(tiles):
            if i + 1 < len(tiles):
                fetch(i + 1).start()
            fetch(i).wait()
            if i >= 2:
                stored(i - 2)
            total = contract([dbuf.at[i % 2]], W_TILE)
            for j in range(bpt):
                obuf[i % 2, j] = total[:, j * LANES:(j + 1) * LANES].astype(BF16)
            blk0 = (SEG_OFF[s] + c) // LANES
            stores[i] = pltpu.make_async_copy(obuf.at[i % 2], out_hbm.at[pl.ds(blk0, bpt)], sem.at[3 + i % 2])
            stores[i].start()
        for i in range(max(0, len(tiles) - 2), len(tiles)):
            stored(i)
        load_glr.wait()
        head_sum = gbuf[0].astype(F32)
        for h in range(1, GLA_HEADS):
            head_sum = head_sum + gbuf[h].astype(F32)
        gbuf[0] = head_sum.astype(BF16)
        oglr_ref[...] = contract([gbuf.at[0]], LANES)
        for q in range(N_CHIP):
            for k in range(n):
                handover(2 * q, k, landed=True).wait_recv()
        for d in range(N_DEV):
            for_sibling(d, range(n), lambda cp: cp.wait_send())

    outs = _call(
        body, "inproj_bwd_w",
        out_shape=[jax.ShapeDtypeStruct((AL_COLS // LANES, D_MODEL, LANES), BF16), jax.ShapeDtypeStruct((D_MODEL, LANES), F32),
                   jax.ShapeDtypeStruct((N_CHIP, SLAB_BLOCKS, D_MODEL, LANES), BF16)]
                  + [jax.ShapeDtypeStruct((N_CHIP, *r.shape[1:]), BF16) for r in row_sends],
        in_specs=[ANY] * (12 + nr), out_specs=[ANY, pl.BlockSpec(memory_space=pltpu.VMEM)] + [ANY] * n,
        scratch_shapes=[pltpu.VMEM((nt, D_MODEL, TILE), BF16), pltpu.VMEM((2, t_rows, W_TILE), BF16),
                        pltpu.VMEM((2, bpt, D_MODEL, LANES), BF16), pltpu.VMEM((D_MODEL, W_TILE), F32),
                        pltpu.VMEM((GLA_HEADS, t_rows, LANES), BF16), pltpu.SemaphoreType.DMA((6,)),
                        pltpu.SemaphoreType.DMA((n * N_CHIP,)), pltpu.SemaphoreType.DMA((n * N_CHIP,))],
        compiler_params=_params(),
    )(ut, *[dseg[n_] for n_ in SEG_NAMES], dglr, *row_sends)
    return outs[0], outs[1], outs[2], outs[3:]


def _position():
    x, y, c = lax.axis_index("x"), lax.axis_index("y"), lax.axis_index("c")
    return x, y, c


def _index(px, py, pc):
    return 4 * px + 2 * py + pc


def _all_gather(arrs, name):
    n = len(arrs)

    def body(*refs):
        ins, outs = refs[:n], refs[n:2 * n]
        send_sems, recv_sems, local_sems = refs[2 * n:]
        x, y, c = _position()
        me, sibling = (x, y, c), (x, y, 1 - c)
        chips = [(1 - x, y), (x, 1 - y), (1 - x, 1 - y)]

        def copy(a, k, block, to, src=None):
            dst = outs[a].at[_index(*block)]
            return pltpu.make_async_remote_copy(src_ref=dst if src is None else src, dst_ref=dst,
                                                send_sem=send_sems.at[7 * a + k], recv_sem=recv_sems.at[7 * a + k],
                                                device_id=to, device_id_type=MESH)

        def relay(a, j):
            return copy(a, 3, (*chips[j], c), (*chips[1 - j], c))

        mine = [pltpu.make_async_copy(ins[a], outs[a].at[_index(*me)], local_sems.at[a]) for a in range(n)]
        for cp in mine:
            cp.start()
        first = []
        for a in range(n):
            first.append(copy(a, 0, me, sibling, src=ins[a]))
            first += [copy(a, 1 + j, me, (*chips[j], c), src=ins[a]) for j in range(2)]
        for cp in first:
            cp.start()
        passed = []
        for j in range(3):
            for a in range(n):
                copy(a, 1 + j, (*chips[j], c), me).wait_recv()
                cp = copy(a, 4 + j, (*chips[j], c), sibling)
                cp.start()
                passed.append(cp)
            if j < 2:
                @pl.when(c == j)
                def _():
                    for a in range(n):
                        relay(a, j).start()
        for a in range(n):
            copy(a, 0, sibling, me).wait_recv()
            for j in range(3):
                copy(a, 4 + j, (*chips[j], 1 - c), me).wait_recv()
        for cp in first + passed:
            cp.wait_send()
        for j in range(2):
            @pl.when(c == j)
            def _():
                for a in range(n):
                    relay(a, j).wait_send()
        for cp in mine:
            cp.wait()

    return _call(
        body, name,
        out_shape=[jax.ShapeDtypeStruct((N_DEV, *a.shape), a.dtype) for a in arrs],
        in_specs=[ANY] * n, out_specs=[ANY] * n,
        scratch_shapes=[pltpu.SemaphoreType.DMA((7 * n,)), pltpu.SemaphoreType.DMA((7 * n,)), pltpu.SemaphoreType.DMA((n,))],
    )(*arrs)


N_CHIP = N_DEV // 2


def _slab_block0(owner):
    step = SLAB_BLK0[1]
    assert all(SLAB_BLK0[d] == step * d - (d == N_DEV - 1) for d in range(N_DEV))
    return step * owner - jnp.where(owner == N_DEV - 1, 1, 0)


def _add_bf16(c_ref, a_ref, b_ref, o_ref):
    o_ref[...] = (a_ref[...].astype(F32) + b_ref[...].astype(F32)).astype(BF16)


def _chip_partial_slab(dw_blocks, sib, core):
    blk = pl.BlockSpec((None, SLAB_BLOCKS, D_MODEL, LANES), lambda q, c_ref: (q, 0, 0, 0))
    return _call(
        functools.partial(_add_bf16), "chip_partial_w_in", out_shape=jax.ShapeDtypeStruct(sib.shape, BF16),
        grid_spec=pltpu.PrefetchScalarGridSpec(
            num_scalar_prefetch=1, grid=(N_CHIP,),
            in_specs=[pl.BlockSpec((pl.Element(SLAB_BLOCKS), pl.Element(D_MODEL), pl.Element(LANES)),
                                   lambda q, c_ref: (_slab_block0(2 * q + c_ref[0]), 0, 0)), blk],
            out_specs=blk),
        compiler_params=_params(("arbitrary",)),
    )(core, dw_blocks, sib)


def _chip_partial_rows(sends, sibs, core):
    n = len(sends)

    def body(c_ref, *refs):
        for k in range(n):
            _add_bf16(c_ref, refs[k], refs[n + k], refs[2 * n + k])

    own = [pl.BlockSpec((None, *a.shape[1:]), lambda q, c_ref: (2 * q + c_ref[0], 0, 0)) for a in sends]
    blk = [pl.BlockSpec((None, *a.shape[1:]), lambda q, c_ref: (q, 0, 0)) for a in sibs]
    return _call(
        body, "chip_partial_rows", out_shape=[jax.ShapeDtypeStruct(a.shape, BF16) for a in sibs],
        grid_spec=pltpu.PrefetchScalarGridSpec(num_scalar_prefetch=1, grid=(N_CHIP,), in_specs=own + blk, out_specs=blk),
        compiler_params=_params(("arbitrary",)),
    )(core, *sends, *sibs)


def _exchange_sems(n_arrays, n_peers):
    return [pltpu.SemaphoreType.DMA((n_arrays * n_peers,)), pltpu.SemaphoreType.DMA((n_arrays * n_peers,)),
            pltpu.SemaphoreType.DMA((n_arrays,))]


class _Exchange:
    def __init__(self, srcs, dsts, sems, among_chips):
        self.arrs = list(zip(srcs, dsts))
        self.n = len(self.arrs)
        self.send_sems, self.recv_sems, self.local_sems = sems
        self.among_chips = among_chips
        x, y, c = _position()
        self.c = c
        self.me = 2 * x + y if among_chips else _index(x, y, c)
        self.n_peers = N_CHIP if among_chips else N_DEV

    def _device(self, p):
        return (p // 2, p % 2, self.c) if self.among_chips else (p // 4, (p // 2) % 2, p % 2)

    def _src(self, k, p):
        src = self.arrs[k][0]
        return src.at[p] if self.among_chips else src

    def _mine(self):
        return [pltpu.make_async_copy(self._src(k, self.me), self.arrs[k][1].at[self.me], self.local_sems.at[k]) for k in range(self.n)]

    def _copy(self, p, k, landing):
        return pltpu.make_async_remote_copy(
            src_ref=self._src(k, p), dst_ref=self.arrs[k][1].at[landing], send_sem=self.send_sems.at[self.n * p + k],
            recv_sem=self.recv_sems.at[self.n * landing + k], device_id=self._device(p), device_id_type=MESH)

    def _others(self, fn):
        for p in range(self.n_peers):
            @pl.when(p != self.me)
            def _():
                for k in range(self.n):
                    fn(p, k)

    def start(self):
        for cp in self._mine():
            cp.start()
        self._others(lambda p, k: self._copy(p, k, self.me).start())

    def finish(self):
        self._others(lambda p, k: self._copy(p, k, p).wait_recv())
        self._others(lambda p, k: self._copy(p, k, self.me).wait_send())
        for cp in self._mine():
            cp.wait()


def _adamw(g, w, m, v):
    m_new = ADAM_B1 * m + (1.0 - ADAM_B1) * g
    v_new = ADAM_B2 * v + (1.0 - ADAM_B2) * (g * g)
    m_hat = m_new / (1.0 - ADAM_B1 ** ADAM_STEP)
    v_hat = v_new / (1.0 - ADAM_B2 ** ADAM_STEP)
    delta = -ADAM_LR * (m_hat / (jnp.sqrt(v_hat) + ADAM_EPS) + ADAM_WD * w)
    return delta, m_new, v_new


def _sum_partials(p_ref):
    g = p_ref[0].astype(F32)
    for d in range(1, p_ref.shape[0]):
        g = g + p_ref[d].astype(F32)
    return g


def _reduce_adam_rows(parts, ws, ms, vs):
    n = len(ws)

    def body(*refs):
        ins, outs = refs[:4 * n], refs[4 * n:]
        for k in range(n):
            p_ref, w_ref, m_ref, v_ref = ins[k], ins[n + k], ins[2 * n + k], ins[3 * n + k]
            g = _sum_partials(p_ref)
            outs[4 * k][...] = g
            outs[4 * k + 1][...], outs[4 * k + 2][...], outs[4 * k + 3][...] = _adamw(g, w_ref[...], m_ref[...], v_ref[...])

    outs = _call(
        body, "adam_row_weights", out_shape=[jax.ShapeDtypeStruct(w.shape, F32) for w in ws for _ in range(4)],
        compiler_params=_params(),
    )(*parts, *ws, *ms, *vs)
    return [tuple(outs[4 * k:4 * k + 4]) for k in range(n)]


def _reduce_adam_slab(parts, glr, w_t, m_t, v_t, me):
    cols, rows = w_t.shape
    shift = jnp.asarray(SLAB_SHIFT, jnp.int32)[me]
    glr_at = jnp.where(me == GLR_DEV, GLR_LOCAL, cols).astype(jnp.int32)

    def body(s_ref, p_ref, glr_ref, w_ref, m_ref, v_ref, g_ref, d_ref, mo_ref, vo_ref, slab_t):
        shift, glr_at = s_ref[0], s_ref[1]
        tall = jnp.concatenate([_sum_partials(p_ref.at[:, j]).T for j in range(SLAB_BLOCKS)], axis=0)
        before = pltpu.roll(tall, SLAB_W - shift, 0)
        after = pltpu.roll(tall, lax.rem(SLAB_W - shift + GLA_RANK, SLAB_W), 0)
        wide = jnp.concatenate([glr_ref[...].T, jnp.zeros((SLAB_W - LANES, LANES), F32)], axis=0)
        placed = pltpu.roll(wide, lax.rem(glr_at, SLAB_W), 0)
        row = lax.broadcasted_iota(jnp.int32, (SLAB_W, LANES), 0)
        slab_t[...] = jnp.where(row < glr_at, before, jnp.where(row < glr_at + GLA_RANK, placed, after))
        g = slab_t[pl.ds(0, cols), :]
        g_ref[...] = g
        d_ref[...], mo_ref[...], vo_ref[...] = _adamw(g, w_ref[...], m_ref[...], v_ref[...])

    blk = pl.BlockSpec((cols, LANES), lambda i, s: (0, i))
    return _call(
        body, "adam_w_in", out_shape=[jax.ShapeDtypeStruct((cols, rows), F32)] * 4,
        grid_spec=pltpu.PrefetchScalarGridSpec(
            num_scalar_prefetch=1, grid=(rows // LANES,),
            in_specs=[pl.BlockSpec((parts.shape[0], SLAB_BLOCKS, LANES, LANES), lambda i, s: (0, 0, i, 0)),
                      pl.BlockSpec((LANES, LANES), lambda i, s: (i, 0)), blk, blk, blk],
            out_specs=[blk] * 4, scratch_shapes=[pltpu.VMEM((SLAB_W, LANES), F32)]),
        compiler_params=_params(("arbitrary",)),
    )(jnp.stack([shift, glr_at]), parts, glr, w_t, m_t, v_t)


def _reduce_small(parts):
    def body(p_ref, o_ref):
        o_ref[...] = _sum_partials(p_ref)

    return _call(body, "reduce_small", out_shape=jax.ShapeDtypeStruct(parts.shape[1:], F32))(parts)


def _adam_small(g, w, m, v):
    def body(g_ref, w_ref, m_ref, v_ref, d_ref, mo_ref, vo_ref):
        d_ref[...], mo_ref[...], vo_ref[...] = _adamw(g_ref[...], w_ref[...], m_ref[...], v_ref[...])

    return _call(body, "adam_small", out_shape=[jax.ShapeDtypeStruct(g.shape, F32)] * 3)(g, w, m, v)


def _pack_rows(arrs):
    rows = []
    for a in arrs:
        flat = a.reshape(-1).astype(F32)
        pad = (-flat.shape[0]) % LANES
        rows.append(jnp.pad(flat, (0, pad)).reshape(-1, LANES))
    packed = jnp.concatenate(rows, axis=0)
    return jnp.pad(packed, ((0, (-packed.shape[0]) % 8), (0, 0)))


def _unpack_rows(packed, shapes):
    out, r = [], 0
    for shp in shapes:
        size = 1
        for s in shp:
            size *= s
        nrows = -(-size // LANES)
        out.append(packed[r:r + nrows].reshape(-1)[:size].reshape(shp))
        r += nrows
    return out


def _shard_to_slab(shard, d):
    glr = jnp.zeros((D_MODEL, GLA_RANK), shard.dtype)
    if d == GLR_DEV:
        glr = shard[:, GLR_LOCAL:GLR_LOCAL + GLA_RANK]
        shard = jnp.concatenate([shard[:, :GLR_LOCAL], shard[:, GLR_LOCAL + GLA_RANK:]], axis=1)
    return jnp.pad(shard, ((0, 0), (SLAB_SHIFT[d], SLAB_W - SLAB_SHIFT[d] - shard.shape[1]))), glr


def kernel(x, meta_tokens, norm_gain, w_in, w_gate_up, b_gate, ret_norm_gain, gla_norm_gain, w_branch_ret, w_branch_gla, w_out, final_norm_gain, loss_target, m_meta_tokens, m_norm_gain, m_w_in, m_w_gate_up, m_b_gate, m_ret_norm_gain, m_gla_norm_gain, m_w_branch_ret, m_w_branch_gla, m_w_out, m_final_norm_gain, v_meta_tokens, v_norm_gain, v_w_in, v_w_gate_up, v_b_gate, v_ret_norm_gain, v_gla_norm_gain, v_w_branch_ret, v_w_branch_gla, v_w_out, v_final_norm_gain):
    xi, yi, ci = _position()
    me = _index(xi, yi, ci)
    seq = x.shape[1]
    t_rows = seq + TILE
    in_shard = w_in.shape[2]
    gu_shard = w_gate_up.shape[2]
    meta_shard = meta_tokens.shape[1]
    ret_rows, gla_rows, out_rows = w_branch_ret.shape[1], w_branch_gla.shape[1], w_out.shape[1]

    assert in_shard == IN_SHARD
    slab_local, glr_local = lax.switch(me, [functools.partial(_shard_to_slab, d=d) for d in range(N_DEV)], w_in[0])
    small_local = jnp.concatenate([meta_tokens, jnp.pad(w_gate_up[0], ((0, 0), (0, LANES - gu_shard))),
                                   glr_local.reshape(-1, LANES)], axis=0)
    slabs, g_small = _all_gather([slab_local.astype(BF16), small_local], "all_gather_shards")
    n_small = N_META + GLA_RANK
    w_glr = jnp.pad(g_small[GLR_DEV, n_small:].reshape(D_MODEL, GLA_RANK), ((0, 0), (0, LANES - GLA_RANK))).astype(BF16)
    meta_full = jnp.transpose(g_small[:, :N_META, :], (1, 0, 2)).reshape(N_META, D_MODEL)
    wgu_full = jnp.transpose(g_small[:, N_META:n_small, :gu_shard], (1, 0, 2)).reshape(GLA_RANK, GLA_HEADS * GLA_K)
    wgu_pad = jnp.pad(wgu_full, ((0, LANES - GLA_RANK), (0, 0)))

    rope = _rope_tables(t_rows // TILE)
    lg = jnp.log1p(-(2.0 ** (-5.0 - jnp.arange(RET_HEADS, dtype=F32))))

    head = jnp.concatenate([jnp.zeros((PAD_ROWS, D_MODEL), F32), meta_full], axis=0)
    ut, proj, glr = _inproj_tiles(head, x[0], norm_gain, slabs, w_glr)
    o_ret_raw, o_ret, ret_states, (g_br, g_bg, g_o) = _ret_fwd(
        proj, rope, ret_norm_gain, lg, [w_branch_ret[0].astype(BF16), w_branch_gla[0].astype(BF16), w_out[0].astype(BF16)])
    w_br, w_bg, w_o = g_br.reshape(RET_W, D_MODEL), g_bg.reshape(GLA_W, D_MODEL), g_o.reshape(D_MODEL, D_MODEL)
    masks, cum_fwd, cum_bwd = _gla_tables()
    o_gla_raw, o_gla, gla_states, gla_scores_t = _gla_fwd(proj, glr, wgu_pad, b_gate, gla_norm_gain, masks, cum_fwd)
    (dh1, d_mr, d_mg, do_ret, do_gla, loss_part, d_gfinal, dw_br, dw_bg, dw_o) = _merge_fwd_bwd(
        o_ret, o_gla, proj, x[0], loss_target[0], final_norm_gain.reshape(1, D_MODEL), w_br, w_bg, w_o)

    d_rq, d_rk, d_rv, d_rg, d_gret = _ret_bwd(proj, rope, ret_norm_gain, lg, o_ret_raw, do_ret, ret_states)
    d_gq, d_gk, d_gv, d_gg, dglr_parts, d_wgu, d_bgate, d_ggla = _gla_bwd(
        proj, glr, wgu_pad, b_gate, gla_norm_gain, o_gla_raw, do_gla, gla_states, gla_scores_t, masks, cum_fwd, cum_bwd)
    dseg = dict(rq=d_rq, rk=d_rk, rv=d_rv, rg=d_rg, gq=d_gq, gk=d_gk, gv=d_gv, gg=d_gg, mr=d_mr, mg=d_mg)
    row_sends = [dw_br.reshape(N_DEV, ret_rows, D_MODEL), dw_bg.reshape(N_DEV, gla_rows, D_MODEL),
                 dw_o.reshape(N_DEV, out_rows, D_MODEL)]
    dw_blocks, dw_glr, sib_in, sib_rows = _inproj_bwd_w(ut, dseg, dglr_parts, row_sends)
    core = ci.astype(jnp.int32).reshape(1)
    chip_partials = [_chip_partial_slab(dw_blocks, sib_in, core)] + list(_chip_partial_rows(row_sends, list(sib_rows), core))
    grad_x, d_head, d_gnorm, p_in, p_br, p_bg, p_o = _inproj_bwd_x(
        dseg, dglr_parts, head, x[0], dh1, norm_gain, slabs, w_glr, chip_partials)
    small_shapes = [(N_META, D_MODEL), (1, D_MODEL), (GLA_RANK, GLA_HEADS * GLA_K), (1, GLA_HEADS * GLA_K),
                    (1, RET_W), (1, GLA_W), (1, D_MODEL), (1, LANES), (D_MODEL, GLA_RANK)]
    small_part = _pack_rows([d_head[PAD_ROWS:], d_gnorm, d_wgu[:GLA_RANK], d_bgate, d_gret, d_ggla, d_gfinal, loss_part,
                             dw_glr[:, :GLA_RANK]])
    (p_small,) = _all_gather([small_part], "all_gather_small_partials")

    (g_meta_f, g_gnorm, g_wgu_f, g_bgate, g_gret, g_ggla, g_gfinal, loss_all,
     g_wglr) = _unpack_rows(_reduce_small(p_small), small_shapes)
    g_w_in, d_w_in, nm_w_in, nv_w_in = [a.T for a in _reduce_adam_slab(
        p_in, jnp.pad(g_wglr, ((0, 0), (0, LANES - GLA_RANK))), w_in[0].T, m_w_in[0].T, v_w_in[0].T, me)]
    ((g_w_br, d_w_br, nm_w_br, nv_w_br), (g_w_bg, d_w_bg, nm_w_bg, nv_w_bg), (g_w_o, d_w_o, nm_w_o, nv_w_o)) = _reduce_adam_rows(
        [p_br, p_bg, p_o], [w_branch_ret[0], w_branch_gla[0], w_out[0]], [m_w_branch_ret[0], m_w_branch_gla[0], m_w_out[0]],
        [v_w_branch_ret[0], v_w_branch_gla[0], v_w_out[0]])
    g_meta = lax.dynamic_slice_in_dim(g_meta_f, me * meta_shard, meta_shard, axis=1)
    g_wgu = lax.dynamic_slice_in_dim(g_wgu_f, me * gu_shard, gu_shard, axis=1)
    s_g = [g_meta, g_gnorm, g_wgu, g_bgate, g_gret, g_ggla, g_gfinal]
    s_w = [meta_tokens, norm_gain, w_gate_up[0], b_gate, ret_norm_gain, gla_norm_gain, final_norm_gain]
    s_m = [m_meta_tokens, m_norm_gain, m_w_gate_up[0], m_b_gate, m_ret_norm_gain, m_gla_norm_gain, m_final_norm_gain]
    s_v = [v_meta_tokens, v_norm_gain, v_w_gate_up[0], v_b_gate, v_ret_norm_gain, v_gla_norm_gain, v_final_norm_gain]
    shapes = [a.shape for a in s_g]
    s_d, s_nm, s_nv = [_unpack_rows(p, shapes) for p in _adam_small(*[_pack_rows(l) for l in (s_g, s_w, s_m, s_v)])]

    loss = loss_all[0, 0]
    grad_x = grad_x[None]

    def order(meta, gnorm, win, wgu, bgate, gret, ggla, wbr, wbg, wo, gfin):
        return (meta, gnorm, win[None], wgu[None], bgate, gret, ggla, wbr[None], wbg[None], wo[None], gfin.reshape(final_norm_gain.shape))

    def small(l):
        return dict(meta=l[0], gnorm=l[1], wgu=l[2], bgate=l[3], gret=l[4], ggla=l[5], gfin=l[6])

    grads = order(win=g_w_in, wbr=g_w_br, wbg=g_w_bg, wo=g_w_o, **small(s_g))
    deltas = order(win=d_w_in, wbr=d_w_br, wbg=d_w_bg, wo=d_w_o, **small(s_d))
    new_m = order(win=nm_w_in, wbr=nm_w_br, wbg=nm_w_bg, wo=nm_w_o, **small(s_nm))
    new_v = order(win=nv_w_in, wbr=nv_w_br, wbg=nv_w_bg, wo=nv_w_o, **small(s_nv))
    return (loss, grad_x, *grads, *deltas, *new_m, *new_v)
```

```python
import functools

import jax
import jax.numpy as jnp
from jax import lax
from jax.experimental import pallas as pl
from jax.experimental.pallas import tpu as pltpu

F32 = jnp.float32
BF16 = jnp.bfloat16

D_MODEL = 1024
N_META = 16
TILE = 256
PAD_ROWS = TILE - N_META
RET_HEADS = 4
RET_QK = 256
RET_V = 512
RET_W = RET_HEADS * RET_V
GLA_HEADS = 4
GLA_K = 128
GLA_V = 256
GLA_W = GLA_HEADS * GLA_V
GLA_RANK = 16
GLA_TAU = 16.0
GLA_CHUNK = 16
ROPE_BASE = 10000.0
EPS = 1e-6
LANES = 128
N_DEV = 8
SEG_NAMES = ("rq", "rk", "rv", "rg", "gq", "gk", "gv", "gg", "mr", "mg")
SEG_W = (1024, 1024, 2048, 2048, 512, 512, 1024, 1024, 1024, 1024)
SEG_OFF = tuple(sum(SEG_W[:i]) for i in range(len(SEG_W)))
AL_COLS = sum(SEG_W)
IN_COLS = AL_COLS + GLA_RANK
GLR_OFF = sum(SEG_W[:8])
IN_SHARD = IN_COLS // N_DEV


def _aligned_col(c):
    assert c <= GLR_OFF or c >= GLR_OFF + GLA_RANK
    return c if c <= GLR_OFF else c - GLA_RANK


SLAB_BOUND = tuple(_aligned_col(IN_SHARD * d) for d in range(N_DEV + 1))
SLAB_BLK0 = tuple(b // LANES for b in SLAB_BOUND[:-1])
SLAB_SHIFT = tuple(b % LANES for b in SLAB_BOUND[:-1])
SLAB_BLOCKS = max(-(-SLAB_BOUND[d + 1] // LANES) - SLAB_BLK0[d] for d in range(N_DEV))
SLAB_W = SLAB_BLOCKS * LANES
GLR_DEV = GLR_OFF // IN_SHARD
GLR_LOCAL = GLR_OFF - GLR_DEV * IN_SHARD
assert all(SLAB_BLK0[d] + SLAB_BLOCKS <= AL_COLS // LANES for d in range(N_DEV))
VMEM_LIMIT = 58 * 1024 * 1024
ADAM_LR, ADAM_B1, ADAM_B2, ADAM_EPS, ADAM_WD, ADAM_STEP = 0.001, 0.9, 0.999, 1e-08, 0.01, 10
ANY = pl.BlockSpec(memory_space=pl.ANY)
MESH = pl.DeviceIdType.MESH


def _call(body, name, **kw):
    return pl.pallas_call(body, name=name, **kw)


def _params(sem=None):
    return pltpu.CompilerParams(dimension_semantics=sem, vmem_limit_bytes=VMEM_LIMIT)


def _mm(a, b):
    return jnp.dot(a, b, preferred_element_type=F32)


def _mm_nt(a, b):
    return lax.dot_general(a, b, (((1,), (1,)), ((), ())), preferred_element_type=F32)


def _mm_tn(a, b):
    return lax.dot_general(a, b, (((0,), (0,)), ((), ())), preferred_element_type=F32)


def _sigmoid(x):
    return jax.nn.sigmoid(x)


def _rope(t, cos, sin):
    half = t.shape[-1] // 2
    t1, t2 = t[:, :half], t[:, half:]
    return jnp.concatenate([t1 * cos - t2 * sin, t2 * cos + t1 * sin], axis=-1)


def _rope_bwd(g, cos, sin):
    half = g.shape[-1] // 2
    g1, g2 = g[:, :half], g[:, half:]
    return jnp.concatenate([g1 * cos + g2 * sin, g2 * cos - g1 * sin], axis=-1)


def _row_mean(x):
    return jnp.mean(x, axis=-1, keepdims=True)


def _col_sum(x):
    return jnp.sum(x, axis=0, keepdims=True)


def _tile_rows(head_ref, x_ref):
    return jnp.where(pl.program_id(0) == 0, head_ref[...], x_ref[...])


def _head_spec():
    return pl.BlockSpec((TILE, D_MODEL), lambda i: (0, 0))


def _x_spec():
    return pl.BlockSpec((TILE, D_MODEL), lambda i: (jnp.maximum(i - 1, 0), 0))


def _slab_plan():
    interior, shared = [], []
    for d in range(N_DEV):
        lo, hi = -(-SLAB_BOUND[d] // LANES), SLAB_BOUND[d + 1] // LANES
        interior.append((d, LANES * (lo - SLAB_BLK0[d]), LANES * lo, LANES * (hi - lo)))
        if d + 1 < N_DEV and SLAB_BOUND[d + 1] % LANES:
            shared.append((hi, d, hi - SLAB_BLK0[d]))
    return interior, shared


W_SCRATCH = lambda: [pltpu.VMEM((D_MODEL, AL_COLS), BF16), pltpu.VMEM((D_MODEL, LANES), BF16),
                     pltpu.VMEM((2 * (N_DEV - 1), D_MODEL, LANES), BF16), pltpu.SemaphoreType.DMA((3 * N_DEV,))]


def _load_weight(slabs_hbm, wg_hbm, w_vm, wg_vm, edge_vm, sem):
    interior, shared = _slab_plan()
    copies = [pltpu.make_async_copy(wg_hbm, wg_vm, sem.at[0])]
    for d, src, dst, width in interior:
        copies.append(pltpu.make_async_copy(slabs_hbm.at[d, :, pl.ds(src, width)], w_vm.at[:, pl.ds(dst, width)], sem.at[1 + d]))
    for n, (_, d, blk) in enumerate(shared):
        copies.append(pltpu.make_async_copy(slabs_hbm.at[d, :, pl.ds(LANES * blk, LANES)], edge_vm.at[2 * n], sem.at[1 + N_DEV + 2 * n]))
        copies.append(pltpu.make_async_copy(slabs_hbm.at[d + 1, :, pl.ds(0, LANES)], edge_vm.at[2 * n + 1], sem.at[2 + N_DEV + 2 * n]))
    for cp in copies:
        cp.start()
    for cp in copies:
        cp.wait()
    for n, (blk, _, _) in enumerate(shared):
        w_vm[:, LANES * blk:LANES * (blk + 1)] = edge_vm[2 * n] + edge_vm[2 * n + 1]


def _proj_specs(names, n_units, where):
    specs = []
    for name in names:
        s = SEG_NAMES.index(name)
        nblk = SEG_W[s] // n_units // LANES
        base = SEG_OFF[s] // LANES
        assert base % nblk == 0
        specs.append(pl.BlockSpec((nblk, TILE, LANES), lambda *g, base=base, nblk=nblk: (base // nblk + where(*g)[0], where(*g)[1], 0)))
    return specs


def _cols(ref, unit=0, n_units=1):
    n = ref.shape[0] // n_units
    return ref[unit * n] if n == 1 else jnp.concatenate([ref[unit * n + j] for j in range(n)], axis=1)


def _inproj_tiles(head, x, g_norm, slabs, w_glr):
    t_rows = x.shape[0] + TILE
    nt = t_rows // TILE
    n_blocks = AL_COLS // LANES

    def body(head_ref, x_ref, g_ref, slabs_hbm, wg_hbm, ut_ref, proj_ref, glr_ref, w_vm, wg_vm, edge_vm, sem):
        @pl.when(pl.program_id(0) == 0)
        def _():
            _load_weight(slabs_hbm, wg_hbm, w_vm, wg_vm, edge_vm, sem)

        x = _tile_rows(head_ref, x_ref)
        r = lax.rsqrt(_row_mean(x * x) + EPS)
        u32 = (x * r * g_ref[...]).astype(BF16).astype(F32)
        u = u32.astype(BF16)
        ut_ref[...] = u32.T.astype(BF16)
        for s in range(len(SEG_W)):
            res = _mm(u, w_vm[:, SEG_OFF[s]:SEG_OFF[s] + SEG_W[s]]).astype(BF16)
            for j in range(SEG_W[s] // LANES):
                proj_ref[SEG_OFF[s] // LANES + j] = res[:, j * LANES:(j + 1) * LANES]
        glr_ref[...] = _mm(u, wg_vm[...])

    return _call(
        body, "inproj_fwd_tiles", grid=(nt,),
        out_shape=[jax.ShapeDtypeStruct((nt, D_MODEL, TILE), BF16), jax.ShapeDtypeStruct((n_blocks, t_rows, LANES), BF16),
                   jax.ShapeDtypeStruct((t_rows, LANES), F32)],
        in_specs=[_head_spec(), _x_spec(), pl.BlockSpec((1, D_MODEL), lambda i: (0, 0)), ANY, ANY],
        out_specs=[pl.BlockSpec((None, D_MODEL, TILE), lambda i: (i, 0, 0)), pl.BlockSpec((n_blocks, TILE, LANES), lambda i: (0, i, 0)),
                   pl.BlockSpec((TILE, LANES), lambda i: (i, 0))],
        scratch_shapes=W_SCRATCH(), compiler_params=_params(("arbitrary",)),
    )(head, x, g_norm, slabs, w_glr)


def _ret_decay(lgh):
    i = lax.broadcasted_iota(jnp.int32, (TILE, TILE), 0)
    j = lax.broadcasted_iota(jnp.int32, (TILE, TILE), 1)
    rel = (i - j).astype(F32)
    return jnp.where(rel >= 0, jnp.exp(jnp.maximum(rel, 0.0) * lgh), 0.0)


def _ret_vectors(lgh):
    idx = lax.broadcasted_iota(jnp.int32, (TILE, 1), 0).astype(F32)
    xi = jnp.exp((idx + 1.0) * lgh)
    zeta = jnp.exp((TILE - 1.0 - idx) * lgh)
    gc = jnp.exp(jnp.full((1, 1), float(TILE), F32) * lgh)
    return xi, zeta, gc


def _rope_tables(nt):
    half = RET_QK // 2
    inv = ROPE_BASE ** (-jnp.arange(half, dtype=F32) / half)
    base = (jnp.arange(nt, dtype=F32) * TILE - float(PAD_ROWS))[:, None, None] * inv[None, None, :]
    off = jnp.arange(TILE, dtype=F32)[:, None] * inv[None, :]
    return jnp.cos(base), jnp.sin(base), jnp.cos(off), jnp.sin(off)


def _rope_specs(tile_of):
    return [pl.BlockSpec((None, 1, RET_QK // 2), lambda i: (tile_of(i), 0, 0))] * 2 + [pl.BlockSpec((TILE, RET_QK // 2), lambda i: (0, 0))] * 2


def _rope_angles(cb_ref, sb_ref, co_ref, so_ref):
    cb, sb, co, so = cb_ref[...], sb_ref[...], co_ref[...], so_ref[...]
    return cb * co - sb * so, sb * co + cb * so


def _ret_fwd(proj, rope, gain, lg, row_shards):
    t_rows = proj.shape[1]
    nt = t_rows // TILE
    ns = len(row_shards)

    def body(lg_ref, q_ref, k_ref, v_ref, g_ref, cb_ref, sb_ref, co_ref, so_ref, gain_ref, *rest):
        shard_refs, (oraw_ref, oret_ref, st_ref), gathered = rest[:ns], rest[ns:ns + 3], rest[ns + 3:2 * ns + 3]
        s_acc, dm = rest[2 * ns + 3:2 * ns + 5]
        gather = _Exchange(shard_refs, gathered, rest[2 * ns + 5:], among_chips=False)
        t = pl.program_id(0)

        @pl.when(t == 0)
        def _():
            gather.start()
            s_acc[...] = jnp.zeros_like(s_acc)
            for h in range(RET_HEADS):
                dm[h] = _ret_decay(lg_ref[h])

        @pl.when(t == nt - 1)
        def _():
            gather.finish()

        cos_t, sin_t = _rope_angles(cb_ref, sb_ref, co_ref, so_ref)
        for h in range(RET_HEADS):
            lgh = lg_ref[h]
            q = _rope(_cols(q_ref, h, RET_HEADS).astype(F32), cos_t, sin_t)
            k = _rope(_cols(k_ref, h, RET_HEADS).astype(F32), cos_t, sin_t) * (RET_QK ** -0.5)
            xi, zeta, gc = _ret_vectors(lgh)
            v = _cols(v_ref, h, RET_HEADS)
            s_in = s_acc[h]
            p = (_mm_nt(q.astype(BF16), k.astype(BF16)) * dm[h]).astype(BF16)
            o = _mm(p, v) + _mm((q * xi).astype(BF16), s_in.astype(BF16))
            st_ref[h] = s_in.astype(BF16)
            s_acc[h] = s_in * gc + _mm_tn((k * zeta).astype(BF16), v)
            cols = slice(h * RET_V, (h + 1) * RET_V)
            oraw_ref[:, cols] = o
            oc = o - _row_mean(o)
            n = oc * lax.rsqrt(_row_mean(oc * oc) + EPS) * gain_ref[:, cols]
            g = _cols(g_ref, h, RET_HEADS).astype(F32)
            oret_ref[:, cols] = (n * g * _sigmoid(g)).astype(BF16)

    row = lambda w: pl.BlockSpec((TILE, w), lambda t: (t, 0))
    outs = _call(
        body, "ret_fwd", grid=(nt,),
        out_shape=[jax.ShapeDtypeStruct((t_rows, RET_W), F32), jax.ShapeDtypeStruct((t_rows, RET_W), BF16),
                   jax.ShapeDtypeStruct((RET_HEADS, nt, RET_QK, RET_V), BF16)]
                  + [jax.ShapeDtypeStruct((N_DEV, *a.shape), a.dtype) for a in row_shards],
        in_specs=[pl.BlockSpec(memory_space=pltpu.SMEM)] + _proj_specs(("rq", "rk", "rv", "rg"), 1, lambda t: (0, t)) + _rope_specs(lambda t: t) + [
                  pl.BlockSpec((1, RET_W), lambda t: (0, 0))] + [ANY] * ns,
        out_specs=[row(RET_W), row(RET_W), pl.BlockSpec((RET_HEADS, None, RET_QK, RET_V), lambda t: (0, t, 0, 0))] + [ANY] * ns,
        scratch_shapes=[pltpu.VMEM((RET_HEADS, RET_QK, RET_V), F32), pltpu.VMEM((RET_HEADS, TILE, TILE), F32)] + _exchange_sems(ns, N_DEV),
        compiler_params=_params(("arbitrary",)),
    )(lg, proj, proj, proj, proj, *rope, gain, *row_shards)
    return outs[0], outs[1], outs[2], outs[3:]


def _ret_bwd(proj, rope, gain, lg, o_raw, do_ret, states):
    t_rows = proj.shape[1]
    nt = t_rows // TILE

    def body(lg_ref, q_ref, k_ref, v_ref, g_ref, cb_ref, sb_ref, co_ref, so_ref, gain_ref, oraw_ref, do_ref, st_ref,
             dq_ref, dk_ref, dv_ref, dg_ref, dgain_ref, e_acc, dm):
        @pl.when(pl.program_id(0) == 0)
        def _():
            e_acc[...] = jnp.zeros_like(e_acc)
            for h in range(RET_HEADS):
                dm[h] = _ret_decay(lg_ref[h])
            dgain_ref[...] = jnp.zeros_like(dgain_ref)

        cos_t, sin_t = _rope_angles(cb_ref, sb_ref, co_ref, so_ref)
        for h in range(RET_HEADS):
            lgh = lg_ref[h]
            cols = slice(h * RET_V, (h + 1) * RET_V)
            qcols = slice(h * RET_QK, (h + 1) * RET_QK)
            q = _rope(_cols(q_ref, h, RET_HEADS).astype(F32), cos_t, sin_t)
            k = _rope(_cols(k_ref, h, RET_HEADS).astype(F32), cos_t, sin_t) * (RET_QK ** -0.5)
            xi, zeta, gc = _ret_vectors(lgh)
            v = _cols(v_ref, h, RET_HEADS)
            g = _cols(g_ref, h, RET_HEADS).astype(F32)
            o = oraw_ref[:, cols]
            do = do_ref[:, cols].astype(F32)
            oc = o - _row_mean(o)
            rstd = lax.rsqrt(_row_mean(oc * oc) + EPS)
            xh = oc * rstd
            gain_t = gain_ref[:, cols]
            sg = _sigmoid(g)
            dn = do * (g * sg)
            dg_ref[:, cols] = (do * (xh * gain_t) * (sg * (1.0 + g * (1.0 - sg)))).astype(BF16)
            dgain_ref[:, cols] += _col_sum(dn * xh)
            dxh = dn * gain_t
            dob = (rstd * (dxh - _row_mean(dxh) - xh * _row_mean(dxh * xh))).astype(BF16)
            dmat = dm[h]
            qb, kb = q.astype(BF16), k.astype(BF16)
            p = (_mm_nt(qb, kb) * dmat).astype(BF16)
            dp = (_mm_nt(dob, v) * dmat).astype(BF16)
            s_in = st_ref[h]
            e_in = e_acc[h]
            e_b = e_in.astype(BF16)
            dq = _mm(dp, kb) + _mm_nt(dob, s_in) * xi
            dk = _mm_tn(dp, qb) + _mm_nt(v, e_b) * zeta
            dv_ref[:, cols] = (_mm_tn(p, dob) + _mm((k * zeta).astype(BF16), e_b)).astype(BF16)
            e_acc[h] = e_in * gc + _mm_tn((q * xi).astype(BF16), dob)
            dq_ref[:, qcols] = _rope_bwd(dq, cos_t, sin_t).astype(BF16)
            dk_ref[:, qcols] = (_rope_bwd(dk, cos_t, sin_t) * (RET_QK ** -0.5)).astype(BF16)

    row = lambda w: pl.BlockSpec((TILE, w), lambda j: (nt - 1 - j, 0))
    vec = pl.BlockSpec((1, RET_W), lambda j: (0, 0))
    return _call(
        body, "ret_bwd", grid=(nt,),
        out_shape=[jax.ShapeDtypeStruct((t_rows, RET_HEADS * RET_QK), BF16), jax.ShapeDtypeStruct((t_rows, RET_HEADS * RET_QK), BF16),
                   jax.ShapeDtypeStruct((t_rows, RET_W), BF16), jax.ShapeDtypeStruct((t_rows, RET_W), BF16),
                   jax.ShapeDtypeStruct((1, RET_W), F32)],
        in_specs=[pl.BlockSpec(memory_space=pltpu.SMEM)] + _proj_specs(("rq", "rk", "rv", "rg"), 1, lambda j: (0, nt - 1 - j)) + _rope_specs(lambda j: nt - 1 - j) + [vec,
                  row(RET_W), row(RET_W), pl.BlockSpec((RET_HEADS, None, RET_QK, RET_V), lambda j: (0, nt - 1 - j, 0, 0))],
        out_specs=[row(RET_HEADS * RET_QK), row(RET_HEADS * RET_QK), row(RET_W), row(RET_W), vec],
        scratch_shapes=[pltpu.VMEM((RET_HEADS, RET_QK, RET_V), F32), pltpu.VMEM((RET_HEADS, TILE, TILE), F32)],
        compiler_params=_params(("arbitrary",)),
    )(lg, proj, proj, proj, proj, *rope, gain, o_raw, do_ret, states)


GLA_LEVELS = (32, 64, 128, 256)
N_TERMS = 1 + len(GLA_LEVELS)


def _gla_tables():
    p = jnp.arange(TILE)[:, None]
    r = jnp.arange(TILE)[None, :]
    masks = [(p // GLA_CHUNK == r // GLA_CHUNK) & (r <= p)]
    for blk in GLA_LEVELS:
        masks.append((p // blk == r // blk) & (p % blk >= blk // 2) & (r % blk < blk // 2))
    masks = jnp.stack(masks + [m.T for m in masks]).astype(F32)
    cum_fwd = jnp.concatenate([r <= p, masks[0] > 0], axis=0).astype(BF16)
    cum_bwd = jnp.concatenate([r >= p, masks[N_TERMS] > 0], axis=1).astype(BF16)
    return masks, cum_fwd, cum_bwd


def _split3(x):
    hi = x.astype(BF16)
    rest = x - hi.astype(F32)
    mid = rest.astype(BF16)
    lo = (rest - mid.astype(F32)).astype(BF16)
    return jnp.concatenate([hi, mid, lo], axis=1)


def _join3(y):
    w = y.shape[1] // 3
    return (y[:, 2 * w:] + y[:, w:2 * w]) + y[:, :w]


def _gla_decays(glr_ref, wgu_ref, b_ref, cum_ref):
    z = _mm(glr_ref[...].astype(BF16), wgu_ref[...].astype(BF16)) + b_ref[...]
    la = (jnp.minimum(z, 0.0) - jnp.log(1.0 + jnp.exp(-jnp.abs(z)))) / GLA_TAU
    width = la.shape[1]
    hi = la.astype(BF16)
    rest = la - hi.astype(F32)
    mid = rest.astype(BF16)
    lo = (rest - mid.astype(F32)).astype(BF16)
    y = _mm(cum_ref[...], jnp.concatenate([hi, mid, lo], axis=1))
    gb = (y[:, 2 * width:] + y[:, width:2 * width]) + y[:, :width]
    return z, gb[:TILE], gb[TILE:]


def _gla_prep(h, q_ref, k_ref, g_all, b_all, g_scr, ref_scr):
    cols = slice(h * GLA_K, (h + 1) * GLA_K)
    g, b = g_all[:, cols], b_all[:, cols]
    g_scr[h] = g
    factors = [(jnp.exp(b), jnp.exp(-b))]
    for lvl, blk in enumerate(GLA_LEVELS):
        for n in range(TILE // blk):
            ref_scr[h, lvl, n * blk:(n + 1) * blk, :] = jnp.broadcast_to(g_scr[h, pl.ds(n * blk + blk // 2 - 1, 1), :], (blk, GLA_K))
        x = g - ref_scr[h, lvl]
        factors.append((jnp.exp(jnp.minimum(x, 0.0)), jnp.exp(jnp.minimum(-x, 0.0))))
    g_last = g_scr[h, pl.ds(TILE - 1, 1), :]
    q = _cols(q_ref, h, GLA_HEADS).astype(F32) * (GLA_K ** -0.5)
    k = _cols(k_ref, h, GLA_HEADS).astype(F32)
    return q, k, factors, jnp.exp(g), jnp.exp(g_last), jnp.exp(g_last - g)


def _gla_scores(q, k, factors, m_ref):
    a = jnp.zeros((TILE, TILE), F32)
    for l, (fq, fk) in enumerate(factors):
        s = _mm_nt((q * fq).astype(BF16), (k * fk).astype(BF16))
        a = jnp.where(m_ref[l] > 0.0, s, a)
    return a


def _gla_fwd(proj, glr, wgu_pad, b_gate, gain, masks, cum_fwd):
    t_rows = glr.shape[0]
    nt = t_rows // TILE

    def body(q_ref, k_ref, v_ref, g_ref, glr_ref, wgu_ref, b_ref, gain_ref, m_ref, cum_ref, oraw_ref, ogla_ref, st_ref, at_ref,
             s_acc, g_scr, ref_scr):
        @pl.when(pl.program_id(0) == 0)
        def _():
            s_acc[...] = jnp.zeros_like(s_acc)

        _, g_all, b_all = _gla_decays(glr_ref, wgu_ref, b_ref, cum_ref)
        for h in range(GLA_HEADS):
            q, k, factors, e_g, e_last, e_end = _gla_prep(h, q_ref, k_ref, g_all, b_all, g_scr, ref_scr)
            v = _cols(v_ref, h, GLA_HEADS)
            st = s_acc[h]
            st_ref[h] = st
            a = _gla_scores(q, k, factors, m_ref)
            at_ref[h] = a.T.astype(BF16)
            o = _mm(a.astype(BF16), v) + _mm_nt((q * e_g).astype(BF16), st.astype(BF16))
            s_acc[h] = st * e_last + _mm(v.astype(F32).T.astype(BF16), (k * e_end).astype(BF16))
            cols = slice(h * GLA_V, (h + 1) * GLA_V)
            oraw_ref[:, cols] = o
            n = o * lax.rsqrt(_row_mean(o * o) + EPS) * gain_ref[:, cols]
            g = _cols(g_ref, h, GLA_HEADS).astype(F32)
            ogla_ref[:, cols] = (n * g * _sigmoid(g)).astype(BF16)

    row = lambda w: pl.BlockSpec((TILE, w), lambda t: (t, 0))
    whole = lambda *shape: pl.BlockSpec(shape, lambda t: (0,) * len(shape))
    return _call(
        body, "gla_fwd", grid=(nt,),
        out_shape=[jax.ShapeDtypeStruct((t_rows, GLA_W), F32), jax.ShapeDtypeStruct((t_rows, GLA_W), BF16),
                   jax.ShapeDtypeStruct((GLA_HEADS, nt, GLA_V, GLA_K), F32), jax.ShapeDtypeStruct((GLA_HEADS, t_rows, TILE), BF16)],
        in_specs=_proj_specs(("gq", "gk", "gv", "gg"), 1, lambda t: (0, t)) + [row(LANES), whole(LANES, GLA_HEADS * GLA_K),
                  whole(1, GLA_HEADS * GLA_K), whole(1, GLA_W), whole(N_TERMS, TILE, TILE), whole(2 * TILE, TILE)],
        out_specs=[row(GLA_W), row(GLA_W), pl.BlockSpec((GLA_HEADS, None, GLA_V, GLA_K), lambda t: (0, t, 0, 0)),
                   pl.BlockSpec((GLA_HEADS, TILE, TILE), lambda t: (0, t, 0))],
        scratch_shapes=[pltpu.VMEM((GLA_HEADS, GLA_V, GLA_K), F32), pltpu.VMEM((GLA_HEADS, TILE, GLA_K), F32),
                        pltpu.VMEM((GLA_HEADS, len(GLA_LEVELS), TILE, GLA_K), F32)],
        compiler_params=_params(("arbitrary",)),
    )(proj, proj, proj, proj, glr, wgu_pad, b_gate, gain, masks, cum_fwd)


def _gla_bwd(proj, glr, wgu_pad, b_gate, gain, o_raw, do_gla, states, a_t, masks, cum_fwd, cum_bwd):
    t_rows = glr.shape[0]
    nt = t_rows // TILE

    def body(q_ref, k_ref, v_ref, g_ref, glr_ref, wgu_ref, b_ref, gain_ref, m_ref, cum_ref, cumb_ref, oraw_ref, do_ref, st_ref, at_ref,
             dq_ref, dk_ref, dv_ref, dg_ref, dglr_ref, dwgu_ref, dbg_ref, dgain_ref, d_acc, g_scr, ref_scr, dref_scr):
        @pl.when(pl.program_id(0) == 0)
        def _():
            d_acc[...] = jnp.zeros_like(d_acc)
            dwgu_ref[...] = jnp.zeros_like(dwgu_ref)
            dbg_ref[...] = jnp.zeros_like(dbg_ref)
            dgain_ref[...] = jnp.zeros_like(dgain_ref)

        z_all, g_all, b_all = _gla_decays(glr_ref, wgu_ref, b_ref, cum_ref)
        dla_parts = []
        for h in range(GLA_HEADS):
            q, k, factors, e_g, e_last, e_end = _gla_prep(h, q_ref, k_ref, g_all, b_all, g_scr, ref_scr)
            v = _cols(v_ref, h, GLA_HEADS)
            cols = slice(h * GLA_V, (h + 1) * GLA_V)
            kcols = slice(h * GLA_K, (h + 1) * GLA_K)
            o = oraw_ref[:, cols]
            do = do_ref[:, cols].astype(F32)
            g = _cols(g_ref, h, GLA_HEADS).astype(F32)
            rinv = lax.rsqrt(_row_mean(o * o) + EPS)
            nh = o * rinv
            gain_t = gain_ref[:, cols]
            sg = _sigmoid(g)
            dn = do * (g * sg)
            dg_ref[:, cols] = (do * (nh * gain_t) * (sg * (1.0 + g * (1.0 - sg)))).astype(BF16)
            dgain_ref[:, cols] += _col_sum(dn * nh)
            dnh = dn * gain_t
            dor = rinv * (dnh - nh * _row_mean(dnh * nh))
            dob = dor.astype(BF16)
            a_t = at_ref[h]
            da = _mm_nt(dob, v).astype(BF16)
            da_t = _mm_nt(v, dob).astype(BF16)
            st_in = st_ref[h]
            d_out = d_acc[h]
            d_out_b = d_out.astype(BF16)
            qg, kg = q * e_g, k * e_end
            dqg = _mm(dob, st_in.astype(BF16))
            dkg = _mm(v, d_out_b)
            dv_ref[:, cols] = (_mm(a_t, dob) + _mm_nt(kg.astype(BF16), d_out_b)).astype(BF16)
            d_acc[h] = d_out * e_last + _mm(dor.T.astype(BF16), qg.astype(BF16))
            dq = dqg * e_g
            dk = dkg * e_end
            dkg_kg = dkg * kg
            dg_cum = dqg * qg - dkg_kg
            db = None
            for l, (fq, fk) in enumerate(factors):
                qt, kt = q * fq, k * fk
                dqt = _mm(da * m_ref[l], kt.astype(BF16))
                dkt = _mm(da_t * m_ref[N_TERMS + l], qt.astype(BF16))
                dq = dq + dqt * fq
                dk = dk + dkt * fk
                diff = dqt * qt - dkt * kt
                if l == 0:
                    db = diff
                else:
                    dg_cum = dg_cum + diff
                    dref_scr[h, l - 1] = diff
            dq_ref[:, kcols] = (dq * (GLA_K ** -0.5)).astype(BF16)
            dk_ref[:, kcols] = dk.astype(BF16)
            g_scr[h] = dg_cum
            g_scr[h, pl.ds(TILE - 1, 1), :] += e_last * _col_sum(d_out * st_in) + _col_sum(dkg_kg)
            for lvl, blk in enumerate(GLA_LEVELS):
                for n in range(TILE // blk):
                    g_scr[h, pl.ds(n * blk + blk // 2 - 1, 1), :] -= _col_sum(dref_scr[h, lvl, n * blk:(n + 1) * blk, :])
            dla_parts.append(_join3(_mm(cumb_ref[...], jnp.concatenate([_split3(g_scr[h]), _split3(db)], axis=0))))
        dz = jnp.concatenate(dla_parts, axis=1) * (1.0 / GLA_TAU) * _sigmoid(-z_all)
        dzb = dz.astype(BF16)
        wgu_b = wgu_ref[...].astype(BF16)
        for h in range(GLA_HEADS):
            kcols = slice(h * GLA_K, (h + 1) * GLA_K)
            dglr_ref[h] = _mm_nt(dzb[:, kcols], wgu_b[:, kcols]).astype(BF16)
        dwgu_ref[...] += _mm(glr_ref[...].T.astype(BF16), dzb)
        dbg_ref[...] += _col_sum(dz)

    row = lambda w: pl.BlockSpec((TILE, w), lambda j: (nt - 1 - j, 0))
    whole = lambda *shape: pl.BlockSpec(shape, lambda j: (0,) * len(shape))
    return _call(
        body, "gla_bwd", grid=(nt,),
        out_shape=[jax.ShapeDtypeStruct((t_rows, GLA_HEADS * GLA_K), BF16), jax.ShapeDtypeStruct((t_rows, GLA_HEADS * GLA_K), BF16),
                   jax.ShapeDtypeStruct((t_rows, GLA_W), BF16), jax.ShapeDtypeStruct((t_rows, GLA_W), BF16),
                   jax.ShapeDtypeStruct((GLA_HEADS, t_rows, LANES), BF16), jax.ShapeDtypeStruct((LANES, GLA_HEADS * GLA_K), F32),
                   jax.ShapeDtypeStruct((1, GLA_HEADS * GLA_K), F32), jax.ShapeDtypeStruct((1, GLA_W), F32)],
        in_specs=_proj_specs(("gq", "gk", "gv", "gg"), 1, lambda j: (0, nt - 1 - j)) + [row(LANES),
                  whole(LANES, GLA_HEADS * GLA_K), whole(1, GLA_HEADS * GLA_K), whole(1, GLA_W),
                  whole(2 * N_TERMS, TILE, TILE), whole(2 * TILE, TILE), whole(TILE, 2 * TILE), row(GLA_W), row(GLA_W),
                  pl.BlockSpec((GLA_HEADS, None, GLA_V, GLA_K), lambda j: (0, nt - 1 - j, 0, 0)),
                  pl.BlockSpec((GLA_HEADS, TILE, TILE), lambda j: (0, nt - 1 - j, 0))],
        out_specs=[row(GLA_HEADS * GLA_K), row(GLA_HEADS * GLA_K), row(GLA_W), row(GLA_W),
                   pl.BlockSpec((GLA_HEADS, TILE, LANES), lambda j: (0, nt - 1 - j, 0)), whole(LANES, GLA_HEADS * GLA_K),
                   whole(1, GLA_HEADS * GLA_K), whole(1, GLA_W)],
        scratch_shapes=[pltpu.VMEM((GLA_HEADS, GLA_V, GLA_K), F32), pltpu.VMEM((GLA_HEADS, TILE, GLA_K), F32),
                        pltpu.VMEM((GLA_HEADS, len(GLA_LEVELS), TILE, GLA_K), F32),
                        pltpu.VMEM((GLA_HEADS, len(GLA_LEVELS), TILE, GLA_K), F32)],
        compiler_params=_params(("arbitrary",)),
    )(proj, proj, proj, proj, glr, wgu_pad, b_gate, gain, masks.astype(BF16), cum_fwd, cum_bwd, o_raw, do_gla, states, a_t)


def _merge_fwd_bwd(o_ret, o_gla, proj, x, target, g_final, w_br, w_bg, w_out):
    t_rows = x.shape[0] + TILE
    nt = t_rows // TILE

    def body(oret_ref, ogla_ref, mr_ref, mg_ref, h0_ref, tgt_ref, gf_ref, wbr_hbm, wbg_hbm, wout_hbm,
             dh1_ref, dmr_ref, dmg_ref, doret_ref, dogla_ref, loss_ref, dgf_ref, dwbr_hbm, dwbg_hbm, dwout_hbm,
             wbr, wbg, wout, abr, abg, aout, sem):
        i = pl.program_id(0)

        @pl.when(i == 0)
        def _():
            cps = [pltpu.make_async_copy(s, d, sem.at[n]) for n, (s, d) in enumerate(((wbr_hbm, wbr), (wbg_hbm, wbg), (wout_hbm, wout)))]
            for cp in cps:
                cp.start()
            abr[...] = jnp.zeros_like(abr)
            abg[...] = jnp.zeros_like(abg)
            aout[...] = jnp.zeros_like(aout)
            loss_ref[...] = jnp.zeros_like(loss_ref)
            dgf_ref[...] = jnp.zeros_like(dgf_ref)
            for cp in cps:
                cp.wait()
            dh1_ref[...] = jnp.zeros_like(dh1_ref)
            dmr_ref[...] = jnp.zeros_like(dmr_ref)
            dmg_ref[...] = jnp.zeros_like(dmg_ref)
            doret_ref[...] = jnp.zeros_like(doret_ref)
            dogla_ref[...] = jnp.zeros_like(dogla_ref)

        @pl.when(i > 0)
        def _():
            oret, ogla = oret_ref[...], ogla_ref[...]
            br, bg = _mm(oret, wbr[...]), _mm(ogla, wbg[...])
            sr, sg = _sigmoid(_cols(mr_ref).astype(F32)), _sigmoid(_cols(mg_ref).astype(F32))
            mb = (sr * br + sg * bg).astype(BF16)
            h1 = h0_ref[...] + _mm(mb, wout[...])
            r2 = lax.rsqrt(_row_mean(h1 * h1) + EPS)
            hn = h1 * r2
            gf = gf_ref[...]
            diff = hn * gf - tgt_ref[...]
            loss_ref[...] += 0.5 * jnp.sum(_row_mean(diff * diff))
            dy = diff * (1.0 / D_MODEL)
            dgf_ref[...] += _col_sum(dy * hn)
            dyg = dy * gf
            dh1 = r2 * (dyg - hn * _row_mean(dyg * hn))
            dh1_ref[...] = dh1
            dh1b = dh1.astype(BF16)
            dm = _mm_nt(dh1b, wout[...])
            aout[...] += _mm_tn(mb, dh1b)
            dbr = (dm * sr).astype(BF16)
            dbg = (dm * sg).astype(BF16)
            dmr_ref[...] = (dm * br * sr * (1.0 - sr)).astype(BF16)
            dmg_ref[...] = (dm * bg * sg * (1.0 - sg)).astype(BF16)
            doret_ref[...] = _mm_nt(dbr, wbr[...]).astype(BF16)
            dogla_ref[...] = _mm_nt(dbg, wbg[...]).astype(BF16)
            abr[...] += _mm_tn(oret, dbr)
            abg[...] += _mm_tn(ogla, dbg)

        @pl.when(i == nt - 1)
        def _():
            wbr[...] = abr[...].astype(BF16)
            wbg[...] = abg[...].astype(BF16)
            wout[...] = aout[...].astype(BF16)
            pltpu.sync_copy(wbr, dwbr_hbm)
            pltpu.sync_copy(wbg, dwbg_hbm)
            pltpu.sync_copy(wout, dwout_hbm)

    row = lambda w: pl.BlockSpec((TILE, w), lambda i: (i, 0))
    one = lambda w: pl.BlockSpec((1, w), lambda i: (0, 0))
    return _call(
        body, "merge_fwd_bwd", grid=(nt,),
        out_shape=[jax.ShapeDtypeStruct((t_rows, D_MODEL), F32), jax.ShapeDtypeStruct((t_rows, D_MODEL), BF16),
                   jax.ShapeDtypeStruct((t_rows, D_MODEL), BF16), jax.ShapeDtypeStruct((t_rows, RET_W), BF16),
                   jax.ShapeDtypeStruct((t_rows, GLA_W), BF16), jax.ShapeDtypeStruct((1, LANES), F32),
                   jax.ShapeDtypeStruct((1, D_MODEL), F32), jax.ShapeDtypeStruct((RET_W, D_MODEL), BF16),
                   jax.ShapeDtypeStruct((GLA_W, D_MODEL), BF16), jax.ShapeDtypeStruct((D_MODEL, D_MODEL), BF16)],
        in_specs=[row(RET_W), row(GLA_W)] + _proj_specs(("mr", "mg"), 1, lambda i: (0, i)) + [_x_spec(), _x_spec(), one(D_MODEL), ANY, ANY, ANY],
        out_specs=[row(D_MODEL), row(D_MODEL), row(D_MODEL), row(RET_W), row(GLA_W), one(LANES), one(D_MODEL), ANY, ANY, ANY],
        scratch_shapes=[pltpu.VMEM((RET_W, D_MODEL), BF16), pltpu.VMEM((GLA_W, D_MODEL), BF16), pltpu.VMEM((D_MODEL, D_MODEL), BF16),
                        pltpu.VMEM((RET_W, D_MODEL), F32), pltpu.VMEM((GLA_W, D_MODEL), F32), pltpu.VMEM((D_MODEL, D_MODEL), F32),
                        pltpu.SemaphoreType.DMA((3,))],
        compiler_params=_params(("arbitrary",)),
    )(o_ret, o_gla, proj, proj, x, target, g_final, w_br, w_bg, w_out)


def _inproj_bwd_x(dseg, dglr, head, x, dh1, g_norm, slabs, w_glr, chip_partials):
    t_rows = x.shape[0] + TILE
    nt = t_rows // TILE
    ne = len(chip_partials)

    def body(*refs):
        d_refs = refs[:10]
        dglr_ref, head_ref, x_ref, dh1_ref, g_ref, slabs_hbm, wg_hbm = refs[10:17]
        part_refs = refs[17:17 + ne]
        dx_ref, dhead_ref, dgn_ref = refs[17 + ne:20 + ne]
        landed = refs[20 + ne:20 + 2 * ne]
        w_vm, wg_vm, edge_vm, sem = refs[20 + 2 * ne:24 + 2 * ne]
        exchange = _Exchange(part_refs, landed, refs[24 + 2 * ne:], among_chips=True)

        @pl.when(pl.program_id(0) == 0)
        def _():
            exchange.start()
            dgn_ref[...] = jnp.zeros_like(dgn_ref)
            _load_weight(slabs_hbm, wg_hbm, w_vm, wg_vm, edge_vm, sem)

        @pl.when(pl.program_id(0) == nt - 1)
        def _():
            exchange.finish()

        dglr = dglr_ref[0].astype(F32)
        for h in range(1, GLA_HEADS):
            dglr = dglr + dglr_ref[h].astype(F32)
        du = _mm_nt(dglr.astype(BF16), wg_vm[...])
        for s, d_ref in enumerate(d_refs):
            du = du + _mm_nt(d_ref[...], w_vm[:, SEG_OFF[s]:SEG_OFF[s] + SEG_W[s]])
        x = _tile_rows(head_ref, x_ref)
        r = lax.rsqrt(_row_mean(x * x) + EPS)
        hn = x * r
        dgn_ref[...] += _col_sum(du * hn)
        dug = du * g_ref[...]
        dh0 = dh1_ref[...] + r * (dug - hn * _row_mean(dug * hn))
        dx_ref[...] = dh0

        @pl.when(pl.program_id(0) == 0)
        def _():
            dhead_ref[...] = dh0

    row = lambda w: pl.BlockSpec((TILE, w), lambda i: (i, 0))
    one = pl.BlockSpec((1, D_MODEL), lambda i: (0, 0))
    return _call(
        body, "inproj_bwd_x", grid=(nt,),
        out_shape=[jax.ShapeDtypeStruct((t_rows - TILE, D_MODEL), F32), jax.ShapeDtypeStruct((TILE, D_MODEL), F32),
                   jax.ShapeDtypeStruct((1, D_MODEL), F32)] + [jax.ShapeDtypeStruct(a.shape, a.dtype) for a in chip_partials],
        in_specs=[row(w) for w in SEG_W] + [pl.BlockSpec((GLA_HEADS, TILE, LANES), lambda i: (0, i, 0)),
                                            _head_spec(), _x_spec(), row(D_MODEL), one, ANY, ANY] + [ANY] * ne,
        out_specs=[_x_spec(), _head_spec(), one] + [ANY] * ne,
        scratch_shapes=W_SCRATCH() + _exchange_sems(ne, N_CHIP),
        compiler_params=_params(("arbitrary",)),
    )(*[dseg[n] for n in SEG_NAMES], dglr, head, x, dh1, g_norm, slabs, w_glr, *chip_partials)


W_TILE = 512


def _inproj_bwd_w(ut, dseg, dglr, row_sends):
    nt = ut.shape[0]
    t_rows = nt * TILE
    kc = 3 if nt % 3 == 0 else 1
    tiles = [(s, c) for s in range(len(SEG_W)) for c in range(0, SEG_W[s], W_TILE)]
    bpt = W_TILE // LANES
    nr = len(row_sends)
    n = 1 + nr
    last_tile = [(SLAB_BLK0[d] + SLAB_BLOCKS - 1) // bpt for d in range(N_DEV)]

    def body(ut_hbm, *refs):
        d_refs, dglr_hbm, row_refs = refs[:10], refs[10], refs[11:11 + nr]
        out_hbm, oglr_ref, sib = refs[11 + nr], refs[12 + nr], refs[13 + nr:13 + nr + n]
        ut_vm, dbuf, obuf, acc, gbuf, sem, send_sems, recv_sems = refs[13 + nr + n:]
        x, y, core = _position()

        def handover(d, k, landed=False):
            q = d // 2
            src = out_hbm.at[pl.ds(SLAB_BLK0[d], SLAB_BLOCKS)] if k == 0 else row_refs[k - 1].at[d]
            return pltpu.make_async_remote_copy(src_ref=sib[k].at[q] if landed else src, dst_ref=sib[k].at[q],
                                                send_sem=send_sems.at[n * q + k], recv_sem=recv_sems.at[n * q + k],
                                                device_id=(x, y, 1 - core), device_id_type=MESH)

        def for_sibling(d, ks, fn):
            @pl.when(d % 2 != core)
            def _():
                for k in ks:
                    fn(handover(d, k))

        for d in range(N_DEV):
            for_sibling(d, range(1, n), lambda cp: cp.start())

        def fetch(i):
            s, c = tiles[i]
            return pltpu.make_async_copy(d_refs[s].at[:, pl.ds(c, W_TILE)], dbuf.at[i % 2], sem.at[1 + i % 2])

        def contract(rhs_refs, width):
            acc[:, :width] = jnp.zeros((D_MODEL, width), F32)

            def step(k, carry):
                part = None
                for j in range(kc):
                    kk = k * kc + j
                    for rhs_ref in rhs_refs:
                        prod = _mm(ut_vm[kk], rhs_ref[pl.ds(pl.multiple_of(kk * TILE, TILE), TILE), :])
                        part = prod if part is None else part + prod
                acc[:, :width] += part
                return carry

            lax.fori_loop(0, nt // kc, step, 0)
            return acc[:, :width]

        load_ut = pltpu.make_async_copy(ut_hbm, ut_vm, sem.at[0])
        load_glr = pltpu.make_async_copy(dglr_hbm, gbuf, sem.at[5])
        load_ut.start()
        load_glr.start()
        fetch(0).start()
        load_ut.wait()
        stores = {}

        def stored(i):
            stores[i].wait()
            for d in range(N_DEV):
                if last_tile[d] == i:
                    for_sibling(d, [0], lambda cp: cp.start())

        for i, (s, c) in enumerate(tiles):
            if i + 1 < len(tiles):
                fetch(i + 1).start()
            fetch(i).wait()
            if i >= 2:
                stored(i - 2)
            total = contract([dbuf.at[i % 2]], W_TILE)
            for j in range(bpt):
                obuf[i % 2, j] = total[:, j * LANES:(j + 1) * LANES].astype(BF16)
            blk0 = (SEG_OFF[s] + c) // LANES
            stores[i] = pltpu.make_async_copy(obuf.at[i % 2], out_hbm.at[pl.ds(blk0, bpt)], sem.at[3 + i % 2])
            stores[i].start()
        for i in range(max(0, len(tiles) - 2), len(tiles)):
            stored(i)
        load_glr.wait()
        head_sum = gbuf[0].astype(F32)
        for h in range(1, GLA_HEADS):
            head_sum = head_sum + gbuf[h].astype(F32)
        gbuf[0] = head_sum.astype(BF16)
        oglr_ref[...] = contract([gbuf.at[0]], LANES)
        for q in range(N_CHIP):
            for k in range(n):
                handover(2 * q, k, landed=True).wait_recv()
        for d in range(N_DEV):
            for_sibling(d, range(n), lambda cp: cp.wait_send())

    outs = _call(
        body, "inproj_bwd_w",
        out_shape=[jax.ShapeDtypeStruct((AL_COLS // LANES, D_MODEL, LANES), BF16), jax.ShapeDtypeStruct((D_MODEL, LANES), F32),
                   jax.ShapeDtypeStruct((N_CHIP, SLAB_BLOCKS, D_MODEL, LANES), BF16)]
                  + [jax.ShapeDtypeStruct((N_CHIP, *r.shape[1:]), BF16) for r in row_sends],
        in_specs=[ANY] * (12 + nr), out_specs=[ANY, pl.BlockSpec(memory_space=pltpu.VMEM)] + [ANY] * n,
        scratch_shapes=[pltpu.VMEM((nt, D_MODEL, TILE), BF16), pltpu.VMEM((2, t_rows, W_TILE), BF16),
                        pltpu.VMEM((2, bpt, D_MODEL, LANES), BF16), pltpu.VMEM((D_MODEL, W_TILE), F32),
                        pltpu.VMEM((GLA_HEADS, t_rows, LANES), BF16), pltpu.SemaphoreType.DMA((6,)),
                        pltpu.SemaphoreType.DMA((n * N_CHIP,)), pltpu.SemaphoreType.DMA((n * N_CHIP,))],
        compiler_params=_params(),
    )(ut, *[dseg[n_] for n_ in SEG_NAMES], dglr, *row_sends)
    return outs[0], outs[1], outs[2], outs[3:]


def _position():
    x, y, c = lax.axis_index("x"), lax.axis_index("y"), lax.axis_index("c")
    return x, y, c


def _index(px, py, pc):
    return 4 * px + 2 * py + pc


def _all_gather(arrs, name):
    n = len(arrs)

    def body(*refs):
        ins, outs = refs[:n], refs[n:2 * n]
        send_sems, recv_sems, local_sems = refs[2 * n:]
        x, y, c = _position()
        me, sibling = (x, y, c), (x, y, 1 - c)
        chips = [(1 - x, y), (x, 1 - y), (1 - x, 1 - y)]

        def copy(a, k, block, to, src=None):
            dst = outs[a].at[_index(*block)]
            return pltpu.make_async_remote_copy(src_ref=dst if src is None else src, dst_ref=dst,
                                                send_sem=send_sems.at[7 * a + k], recv_sem=recv_sems.at[7 * a + k],
                                                device_id=to, device_id_type=MESH)

        def relay(a, j):
            return copy(a, 3, (*chips[j], c), (*chips[1 - j], c))

        mine = [pltpu.make_async_copy(ins[a], outs[a].at[_index(*me)], local_sems.at[a]) for a in range(n)]
        for cp in mine:
            cp.start()
        first = []
        for a in range(n):
            first.append(copy(a, 0, me, sibling, src=ins[a]))
            first += [copy(a, 1 + j, me, (*chips[j], c), src=ins[a]) for j in range(2)]
        for cp in first:
            cp.start()
        passed = []
        for j in range(3):
            for a in range(n):
                copy(a, 1 + j, (*chips[j], c), me).wait_recv()
                cp = copy(a, 4 + j, (*chips[j], c), sibling)
                cp.start()
                passed.append(cp)
            if j < 2:
                @pl.when(c == j)
                def _():
                    for a in range(n):
                        relay(a, j).start()
        for a in range(n):
            copy(a, 0, sibling, me).wait_recv()
            for j in range(3):
                copy(a, 4 + j, (*chips[j], 1 - c), me).wait_recv()
        for cp in first + passed:
            cp.wait_send()
        for j in range(2):
            @pl.when(c == j)
            def _():
                for a in range(n):
                    relay(a, j).wait_send()
        for cp in mine:
            cp.wait()

    return _call(
        body, name,
        out_shape=[jax.ShapeDtypeStruct((N_DEV, *a.shape), a.dtype) for a in arrs],
        in_specs=[ANY] * n, out_specs=[ANY] * n,
        scratch_shapes=[pltpu.SemaphoreType.DMA((7 * n,)), pltpu.SemaphoreType.DMA((7 * n,)), pltpu.SemaphoreType.DMA((n,))],
    )(*arrs)


N_CHIP = N_DEV // 2


def _slab_block0(owner):
    step = SLAB_BLK0[1]
    assert all(SLAB_BLK0[d] == step * d - (d == N_DEV - 1) for d in range(N_DEV))
    return step * owner - jnp.where(owner == N_DEV - 1, 1, 0)


def _add_bf16(c_ref, a_ref, b_ref, o_ref):
    o_ref[...] = (a_ref[...].astype(F32) + b_ref[...].astype(F32)).astype(BF16)


def _chip_partial_slab(dw_blocks, sib, core):
    blk = pl.BlockSpec((None, SLAB_BLOCKS, D_MODEL, LANES), lambda q, c_ref: (q, 0, 0, 0))
    return _call(
        functools.partial(_add_bf16), "chip_partial_w_in", out_shape=jax.ShapeDtypeStruct(sib.shape, BF16),
        grid_spec=pltpu.PrefetchScalarGridSpec(
            num_scalar_prefetch=1, grid=(N_CHIP,),
            in_specs=[pl.BlockSpec((pl.Element(SLAB_BLOCKS), pl.Element(D_MODEL), pl.Element(LANES)),
                                   lambda q, c_ref: (_slab_block0(2 * q + c_ref[0]), 0, 0)), blk],
            out_specs=blk),
        compiler_params=_params(("arbitrary",)),
    )(core, dw_blocks, sib)


def _chip_partial_rows(sends, sibs, core):
    n = len(sends)

    def body(c_ref, *refs):
        for k in range(n):
            _add_bf16(c_ref, refs[k], refs[n + k], refs[2 * n + k])

    own = [pl.BlockSpec((None, *a.shape[1:]), lambda q, c_ref: (2 * q + c_ref[0], 0, 0)) for a in sends]
    blk = [pl.BlockSpec((None, *a.shape[1:]), lambda q, c_ref: (q, 0, 0)) for a in sibs]
    return _call(
        body, "chip_partial_rows", out_shape=[jax.ShapeDtypeStruct(a.shape, BF16) for a in sibs],
        grid_spec=pltpu.PrefetchScalarGridSpec(num_scalar_prefetch=1, grid=(N_CHIP,), in_specs=own + blk, out_specs=blk),
        compiler_params=_params(("arbitrary",)),
    )(core, *sends, *sibs)


def _exchange_sems(n_arrays, n_peers):
    return [pltpu.SemaphoreType.DMA((n_arrays * n_peers,)), pltpu.SemaphoreType.DMA((n_arrays * n_peers,)),
            pltpu.SemaphoreType.DMA((n_arrays,))]


class _Exchange:
    def __init__(self, srcs, dsts, sems, among_chips):
        self.arrs = list(zip(srcs, dsts))
        self.n = len(self.arrs)
        self.send_sems, self.recv_sems, self.local_sems = sems
        self.among_chips = among_chips
        x, y, c = _position()
        self.c = c
        self.me = 2 * x + y if among_chips else _index(x, y, c)
        self.n_peers = N_CHIP if among_chips else N_DEV

    def _device(self, p):
        return (p // 2, p % 2, self.c) if self.among_chips else (p // 4, (p // 2) % 2, p % 2)

    def _src(self, k, p):
        src = self.arrs[k][0]
        return src.at[p] if self.among_chips else src

    def _mine(self):
        return [pltpu.make_async_copy(self._src(k, self.me), self.arrs[k][1].at[self.me], self.local_sems.at[k]) for k in range(self.n)]

    def _copy(self, p, k, landing):
        return pltpu.make_async_remote_copy(
            src_ref=self._src(k, p), dst_ref=self.arrs[k][1].at[landing], send_sem=self.send_sems.at[self.n * p + k],
            recv_sem=self.recv_sems.at[self.n * landing + k], device_id=self._device(p), device_id_type=MESH)

    def _others(self, fn):
        for p in range(self.n_peers):
            @pl.when(p != self.me)
            def _():
                for k in range(self.n):
                    fn(p, k)

    def start(self):
        for cp in self._mine():
            cp.start()
        self._others(lambda p, k: self._copy(p, k, self.me).start())

    def finish(self):
        self._others(lambda p, k: self._copy(p, k, p).wait_recv())
        self._others(lambda p, k: self._copy(p, k, self.me).wait_send())
        for cp in self._mine():
            cp.wait()


def _adamw(g, w, m, v):
    m_new = ADAM_B1 * m + (1.0 - ADAM_B1) * g
    v_new = ADAM_B2 * v + (1.0 - ADAM_B2) * (g * g)
    m_hat = m_new / (1.0 - ADAM_B1 ** ADAM_STEP)
    v_hat = v_new / (1.0 - ADAM_B2 ** ADAM_STEP)
    delta = -ADAM_LR * (m_hat / (jnp.sqrt(v_hat) + ADAM_EPS) + ADAM_WD * w)
    return delta, m_new, v_new


def _sum_partials(p_ref):
    g = p_ref[0].astype(F32)
    for d in range(1, p_ref.shape[0]):
        g = g + p_ref[d].astype(F32)
    return g


def _reduce_adam_rows(parts, ws, ms, vs):
    n = len(ws)

    def body(*refs):
        ins, outs = refs[:4 * n], refs[4 * n:]
        for k in range(n):
            p_ref, w_ref, m_ref, v_ref = ins[k], ins[n + k], ins[2 * n + k], ins[3 * n + k]
            g = _sum_partials(p_ref)
            outs[4 * k][...] = g
            outs[4 * k + 1][...], outs[4 * k + 2][...], outs[4 * k + 3][...] = _adamw(g, w_ref[...], m_ref[...], v_ref[...])

    outs = _call(
        body, "adam_row_weights", out_shape=[jax.ShapeDtypeStruct(w.shape, F32) for w in ws for _ in range(4)],
        compiler_params=_params(),
    )(*parts, *ws, *ms, *vs)
    return [tuple(outs[4 * k:4 * k + 4]) for k in range(n)]


def _reduce_adam_slab(parts, glr, w_t, m_t, v_t, me):
    cols, rows = w_t.shape
    shift = jnp.asarray(SLAB_SHIFT, jnp.int32)[me]
    glr_at = jnp.where(me == GLR_DEV, GLR_LOCAL, cols).astype(jnp.int32)

    def body(s_ref, p_ref, glr_ref, w_ref, m_ref, v_ref, g_ref, d_ref, mo_ref, vo_ref, slab_t):
        shift, glr_at = s_ref[0], s_ref[1]
        tall = jnp.concatenate([_sum_partials(p_ref.at[:, j]).T for j in range(SLAB_BLOCKS)], axis=0)
        before = pltpu.roll(tall, SLAB_W - shift, 0)
        after = pltpu.roll(tall, lax.rem(SLAB_W - shift + GLA_RANK, SLAB_W), 0)
        wide = jnp.concatenate([glr_ref[...].T, jnp.zeros((SLAB_W - LANES, LANES), F32)], axis=0)
        placed = pltpu.roll(wide, lax.rem(glr_at, SLAB_W), 0)
        row = lax.broadcasted_iota(jnp.int32, (SLAB_W, LANES), 0)
        slab_t[...] = jnp.where(row < glr_at, before, jnp.where(row < glr_at + GLA_RANK, placed, after))
        g = slab_t[pl.ds(0, cols), :]
        g_ref[...] = g
        d_ref[...], mo_ref[...], vo_ref[...] = _adamw(g, w_ref[...], m_ref[...], v_ref[...])

    blk = pl.BlockSpec((cols, LANES), lambda i, s: (0, i))
    return _call(
        body, "adam_w_in", out_shape=[jax.ShapeDtypeStruct((cols, rows), F32)] * 4,
        grid_spec=pltpu.PrefetchScalarGridSpec(
            num_scalar_prefetch=1, grid=(rows // LANES,),
            in_specs=[pl.BlockSpec((parts.shape[0], SLAB_BLOCKS, LANES, LANES), lambda i, s: (0, 0, i, 0)),
                      pl.BlockSpec((LANES, LANES), lambda i, s: (i, 0)), blk, blk, blk],
            out_specs=[blk] * 4, scratch_shapes=[pltpu.VMEM((SLAB_W, LANES), F32)]),
        compiler_params=_params(("arbitrary",)),
    )(jnp.stack([shift, glr_at]), parts, glr, w_t, m_t, v_t)


def _reduce_small(parts):
    def body(p_ref, o_ref):
        o_ref[...] = _sum_partials(p_ref)

    return _call(body, "reduce_small", out_shape=jax.ShapeDtypeStruct(parts.shape[1:], F32))(parts)


def _adam_small(g, w, m, v):
    def body(g_ref, w_ref, m_ref, v_ref, d_ref, mo_ref, vo_ref):
        d_ref[...], mo_ref[...], vo_ref[...] = _adamw(g_ref[...], w_ref[...], m_ref[...], v_ref[...])

    return _call(body, "adam_small", out_shape=[jax.ShapeDtypeStruct(g.shape, F32)] * 3)(g, w, m, v)


def _pack_rows(arrs):
    rows = []
    for a in arrs:
        flat = a.reshape(-1).astype(F32)
        pad = (-flat.shape[0]) % LANES
        rows.append(jnp.pad(flat, (0, pad)).reshape(-1, LANES))
    packed = jnp.concatenate(rows, axis=0)
    return jnp.pad(packed, ((0, (-packed.shape[0]) % 8), (0, 0)))


def _unpack_rows(packed, shapes):
    out, r = [], 0
    for shp in shapes:
        size = 1
        for s in shp:
            size *= s
        nrows = -(-size // LANES)
        out.append(packed[r:r + nrows].reshape(-1)[:size].reshape(shp))
        r += nrows
    return out


def _shard_to_slab(shard, d):
    glr = jnp.zeros((D_MODEL, GLA_RANK), shard.dtype)
    if d == GLR_DEV:
        glr = shard[:, GLR_LOCAL:GLR_LOCAL + GLA_RANK]
        shard = jnp.concatenate([shard[:, :GLR_LOCAL], shard[:, GLR_LOCAL + GLA_RANK:]], axis=1)
    return jnp.pad(shard, ((0, 0), (SLAB_SHIFT[d], SLAB_W - SLAB_SHIFT[d] - shard.shape[1]))), glr


def kernel(x, meta_tokens, norm_gain, w_in, w_gate_up, b_gate, ret_norm_gain, gla_norm_gain, w_branch_ret, w_branch_gla, w_out, final_norm_gain, loss_target, m_meta_tokens, m_norm_gain, m_w_in, m_w_gate_up, m_b_gate, m_ret_norm_gain, m_gla_norm_gain, m_w_branch_ret, m_w_branch_gla, m_w_out, m_final_norm_gain, v_meta_tokens, v_norm_gain, v_w_in, v_w_gate_up, v_b_gate, v_ret_norm_gain, v_gla_norm_gain, v_w_branch_ret, v_w_branch_gla, v_w_out, v_final_norm_gain):
    xi, yi, ci = _position()
    me = _index(xi, yi, ci)
    seq = x.shape[1]
    t_rows = seq + TILE
    in_shard = w_in.shape[2]
    gu_shard = w_gate_up.shape[2]
    meta_shard = meta_tokens.shape[1]
    ret_rows, gla_rows, out_rows = w_branch_ret.shape[1], w_branch_gla.shape[1], w_out.shape[1]

    assert in_shard == IN_SHARD
    slab_local, glr_local = lax.switch(me, [functools.partial(_shard_to_slab, d=d) for d in range(N_DEV)], w_in[0])
    small_local = jnp.concatenate([meta_tokens, jnp.pad(w_gate_up[0], ((0, 0), (0, LANES - gu_shard))),
                                   glr_local.reshape(-1, LANES)], axis=0)
    slabs, g_small = _all_gather([slab_local.astype(BF16), small_local], "all_gather_shards")
    n_small = N_META + GLA_RANK
    w_glr = jnp.pad(g_small[GLR_DEV, n_small:].reshape(D_MODEL, GLA_RANK), ((0, 0), (0, LANES - GLA_RANK))).astype(BF16)
    meta_full = jnp.transpose(g_small[:, :N_META, :], (1, 0, 2)).reshape(N_META, D_MODEL)
    wgu_full = jnp.transpose(g_small[:, N_META:n_small, :gu_shard], (1, 0, 2)).reshape(GLA_RANK, GLA_HEADS * GLA_K)
    wgu_pad = jnp.pad(wgu_full, ((0, LANES - GLA_RANK), (0, 0)))

    rope = _rope_tables(t_rows // TILE)
    lg = jnp.log1p(-(2.0 ** (-5.0 - jnp.arange(RET_HEADS, dtype=F32))))

    head = jnp.concatenate([jnp.zeros((PAD_ROWS, D_MODEL), F32), meta_full], axis=0)
    ut, proj, glr = _inproj_tiles(head, x[0], norm_gain, slabs, w_glr)
    o_ret_raw, o_ret, ret_states, (g_br, g_bg, g_o) = _ret_fwd(
        proj, rope, ret_norm_gain, lg, [w_branch_ret[0].astype(BF16), w_branch_gla[0].astype(BF16), w_out[0].astype(BF16)])
    w_br, w_bg, w_o = g_br.reshape(RET_W, D_MODEL), g_bg.reshape(GLA_W, D_MODEL), g_o.reshape(D_MODEL, D_MODEL)
    masks, cum_fwd, cum_bwd = _gla_tables()
    o_gla_raw, o_gla, gla_states, gla_scores_t = _gla_fwd(proj, glr, wgu_pad, b_gate, gla_norm_gain, masks, cum_fwd)
    (dh1, d_mr, d_mg, do_ret, do_gla, loss_part, d_gfinal, dw_br, dw_bg, dw_o) = _merge_fwd_bwd(
        o_ret, o_gla, proj, x[0], loss_target[0], final_norm_gain.reshape(1, D_MODEL), w_br, w_bg, w_o)

    d_rq, d_rk, d_rv, d_rg, d_gret = _ret_bwd(proj, rope, ret_norm_gain, lg, o_ret_raw, do_ret, ret_states)
    d_gq, d_gk, d_gv, d_gg, dglr_parts, d_wgu, d_bgate, d_ggla = _gla_bwd(
        proj, glr, wgu_pad, b_gate, gla_norm_gain, o_gla_raw, do_gla, gla_states, gla_scores_t, masks, cum_fwd, cum_bwd)
    dseg = dict(rq=d_rq, rk=d_rk, rv=d_rv, rg=d_rg, gq=d_gq, gk=d_gk, gv=d_gv, gg=d_gg, mr=d_mr, mg=d_mg)
    row_sends = [dw_br.reshape(N_DEV, ret_rows, D_MODEL), dw_bg.reshape(N_DEV, gla_rows, D_MODEL),
                 dw_o.reshape(N_DEV, out_rows, D_MODEL)]
    dw_blocks, dw_glr, sib_in, sib_rows = _inproj_bwd_w(ut, dseg, dglr_parts, row_sends)
    core = ci.astype(jnp.int32).reshape(1)
    chip_partials = [_chip_partial_slab(dw_blocks, sib_in, core)] + list(_chip_partial_rows(row_sends, list(sib_rows), core))
    grad_x, d_head, d_gnorm, p_in, p_br, p_bg, p_o = _inproj_bwd_x(
        dseg, dglr_parts, head, x[0], dh1, norm_gain, slabs, w_glr, chip_partials)
    small_shapes = [(N_META, D_MODEL), (1, D_MODEL), (GLA_RANK, GLA_HEADS * GLA_K), (1, GLA_HEADS * GLA_K),
                    (1, RET_W), (1, GLA_W), (1, D_MODEL), (1, LANES), (D_MODEL, GLA_RANK)]
    small_part = _pack_rows([d_head[PAD_ROWS:], d_gnorm, d_wgu[:GLA_RANK], d_bgate, d_gret, d_ggla, d_gfinal, loss_part,
                             dw_glr[:, :GLA_RANK]])
    (p_small,) = _all_gather([small_part], "all_gather_small_partials")

    (g_meta_f, g_gnorm, g_wgu_f, g_bgate, g_gret, g_ggla, g_gfinal, loss_all,
     g_wglr) = _unpack_rows(_reduce_small(p_small), small_shapes)
    g_w_in, d_w_in, nm_w_in, nv_w_in = [a.T for a in _reduce_adam_slab(
        p_in, jnp.pad(g_wglr, ((0, 0), (0, LANES - GLA_RANK))), w_in[0].T, m_w_in[0].T, v_w_in[0].T, me)]
    ((g_w_br, d_w_br, nm_w_br, nv_w_br), (g_w_bg, d_w_bg, nm_w_bg, nv_w_bg), (g_w_o, d_w_o, nm_w_o, nv_w_o)) = _reduce_adam_rows(
        [p_br, p_bg, p_o], [w_branch_ret[0], w_branch_gla[0], w_out[0]], [m_w_branch_ret[0], m_w_branch_gla[0], m_w_out[0]],
        [v_w_branch_ret[0], v_w_branch_gla[0], v_w_out[0]])
    g_meta = lax.dynamic_slice_in_dim(g_meta_f, me * meta_shard, meta_shard, axis=1)
    g_wgu = lax.dynamic_slice_in_dim(g_wgu_f, me * gu_shard, gu_shard, axis=1)
    s_g = [g_meta, g_gnorm, g_wgu, g_bgate, g_gret, g_ggla, g_gfinal]
    s_w = [meta_tokens, norm_gain, w_gate_up[0], b_gate, ret_norm_gain, gla_norm_gain, final_norm_gain]
    s_m = [m_meta_tokens, m_norm_gain, m_w_gate_up[0], m_b_gate, m_ret_norm_gain, m_gla_norm_gain, m_final_norm_gain]
    s_v = [v_meta_tokens, v_norm_gain, v_w_gate_up[0], v_b_gate, v_ret_norm_gain, v_gla_norm_gain, v_final_norm_gain]
    shapes = [a.shape for a in s_g]
    s_d, s_nm, s_nv = [_unpack_rows(p, shapes) for p in _adam_small(*[_pack_rows(l) for l in (s_g, s_w, s_m, s_v)])]

    loss = loss_all[0, 0]
    grad_x = grad_x[None]

    def order(meta, gnorm, win, wgu, bgate, gret, ggla, wbr, wbg, wo, gfin):
        return (meta, gnorm, win[None], wgu[None], bgate, gret, ggla, wbr[None], wbg[None], wo[None], gfin.reshape(final_norm_gain.shape))

    def small(l):
        return dict(meta=l[0], gnorm=l[1], wgu=l[2], bgate=l[3], gret=l[4], ggla=l[5], gfin=l[6])

    grads = order(win=g_w_in, wbr=g_w_br, wbg=g_w_bg, wo=g_w_o, **small(s_g))
    deltas = order(win=d_w_in, wbr=d_w_br, wbg=d_w_bg, wo=d_w_o, **small(s_d))
    new_m = order(win=nm_w_in, wbr=nm_w_br, wbg=nm_w_bg, wo=nm_w_o, **small(s_nm))
    new_v = order(win=nv_w_in, wbr=nv_w_br, wbg=nv_w_bg, wo=nv_w_o, **small(s_nv))
    return (loss, grad_x, *grads, *deltas, *new_m, *new_v)
```

```python
import functools

import jax
import jax.numpy as jnp
from jax import lax
from jax.experimental import pallas as pl
from jax.experimental.pallas import tpu as pltpu

F32 = jnp.float32
BF16 = jnp.bfloat16

D_MODEL = 1024
N_META = 16
TILE = 256
PAD_ROWS = TILE - N_META
RET_HEADS = 4
RET_QK = 256
RET_V = 512
RET_W = RET_HEADS * RET_V
GLA_HEADS = 4
GLA_K = 128
GLA_V = 256
GLA_W = GLA_HEADS * GLA_V
GLA_RANK = 16
GLA_TAU = 16.0
GLA_CHUNK = 16
ROPE_BASE = 10000.0
EPS = 1e-6
LANES = 128
N_DEV = 8
SEG_NAMES = ("rq", "rk", "rv", "rg", "gq", "gk", "gv", "gg", "mr", "mg")
SEG_W = (1024, 1024, 2048, 2048, 512, 512, 1024, 1024, 1024, 1024)
SEG_OFF = tuple(sum(SEG_W[:i]) for i in range(len(SEG_W)))
AL_COLS = sum(SEG_W)
IN_COLS = AL_COLS + GLA_RANK
GLR_OFF = sum(SEG_W[:8])
IN_SHARD = IN_COLS // N_DEV


def _aligned_col(c):
    assert c <= GLR_OFF or c >= GLR_OFF + GLA_RANK
    return c if c <= GLR_OFF else c - GLA_RANK


SLAB_BOUND = tuple(_aligned_col(IN_SHARD * d) for d in range(N_DEV + 1))
SLAB_BLK0 = tuple(b // LANES for b in SLAB_BOUND[:-1])
SLAB_SHIFT = tuple(b % LANES for b in SLAB_BOUND[:-1])
SLAB_BLOCKS = max(-(-SLAB_BOUND[d + 1] // LANES) - SLAB_BLK0[d] for d in range(N_DEV))
SLAB_W = SLAB_BLOCKS * LANES
GLR_DEV = GLR_OFF // IN_SHARD
GLR_LOCAL = GLR_OFF - GLR_DEV * IN_SHARD
assert all(SLAB_BLK0[d] + SLAB_BLOCKS <= AL_COLS // LANES for d in range(N_DEV))
VMEM_LIMIT = 58 * 1024 * 1024
ADAM_LR, ADAM_B1, ADAM_B2, ADAM_EPS, ADAM_WD, ADAM_STEP = 0.001, 0.9, 0.999, 1e-08, 0.01, 10
ANY = pl.BlockSpec(memory_space=pl.ANY)
MESH = pl.DeviceIdType.MESH


def _call(body, name, **kw):
    return pl.pallas_call(body, name=name, **kw)


def _params(sem=None):
    return pltpu.CompilerParams(dimension_semantics=sem, vmem_limit_bytes=VMEM_LIMIT)


def _mm(a, b):
    return jnp.dot(a, b, preferred_element_type=F32)


def _mm_nt(a, b):
    return lax.dot_general(a, b, (((1,), (1,)), ((), ())), preferred_element_type=F32)


def _mm_tn(a, b):
    return lax.dot_general(a, b, (((0,), (0,)), ((), ())), preferred_element_type=F32)


def _sigmoid(x):
    return jax.nn.sigmoid(x)


def _rope(t, cos, sin):
    half = t.shape[-1] // 2
    t1, t2 = t[:, :half], t[:, half:]
    return jnp.concatenate([t1 * cos - t2 * sin, t2 * cos + t1 * sin], axis=-1)


def _rope_bwd(g, cos, sin):
    half = g.shape[-1] // 2
    g1, g2 = g[:, :half], g[:, half:]
    return jnp.concatenate([g1 * cos + g2 * sin, g2 * cos - g1 * sin], axis=-1)


def _row_mean(x):
    return jnp.mean(x, axis=-1, keepdims=True)


def _col_sum(x):
    return jnp.sum(x, axis=0, keepdims=True)


def _tile_rows(head_ref, x_ref):
    return jnp.where(pl.program_id(0) == 0, head_ref[...], x_ref[...])


def _head_spec():
    return pl.BlockSpec((TILE, D_MODEL), lambda i: (0, 0))


def _x_spec():
    return pl.BlockSpec((TILE, D_MODEL), lambda i: (jnp.maximum(i - 1, 0), 0))


def _slab_plan():
    interior, shared = [], []
    for d in range(N_DEV):
        lo, hi = -(-SLAB_BOUND[d] // LANES), SLAB_BOUND[d + 1] // LANES
        interior.append((d, LANES * (lo - SLAB_BLK0[d]), LANES * lo, LANES * (hi - lo)))
        if d + 1 < N_DEV and SLAB_BOUND[d + 1] % LANES:
            shared.append((hi, d, hi - SLAB_BLK0[d]))
    return interior, shared


RET_COLS = sum(SEG_W[:4])


def _w_scratch(cols=AL_COLS):
    return [pltpu.VMEM((D_MODEL, cols), BF16), pltpu.VMEM((D_MODEL, LANES), BF16),
            pltpu.VMEM((2 * (N_DEV - 1), D_MODEL, LANES), BF16), pltpu.SemaphoreType.DMA((3 * N_DEV,))]


W_SCRATCH = _w_scratch


def _load_weight(slabs_hbm, wg_hbm, w_vm, wg_vm, edge_vm, sem, col_lo=0, col_hi=AL_COLS):
    interior, shared = _slab_plan()
    copies = [] if wg_hbm is None else [pltpu.make_async_copy(wg_hbm, wg_vm, sem.at[0])]
    for d, src, dst, width in interior:
        lo, hi = max(dst, col_lo), min(dst + width, col_hi)
        if lo < hi:
            copies.append(pltpu.make_async_copy(slabs_hbm.at[d, :, pl.ds(src + lo - dst, hi - lo)],
                                                w_vm.at[:, pl.ds(lo - col_lo, hi - lo)], sem.at[1 + d]))
    shared = [(n, blk, d, j) for n, (blk, d, j) in enumerate(shared) if col_lo <= LANES * blk < col_hi]
    for n, _, d, j in shared:
        copies.append(pltpu.make_async_copy(slabs_hbm.at[d, :, pl.ds(LANES * j, LANES)], edge_vm.at[2 * n], sem.at[1 + N_DEV + 2 * n]))
        copies.append(pltpu.make_async_copy(slabs_hbm.at[d + 1, :, pl.ds(0, LANES)], edge_vm.at[2 * n + 1], sem.at[2 + N_DEV + 2 * n]))
    for cp in copies:
        cp.start()
    for cp in copies:
        cp.wait()
    for n, blk, _, _ in shared:
        w_vm[:, LANES * blk - col_lo:LANES * (blk + 1) - col_lo] = edge_vm[2 * n] + edge_vm[2 * n + 1]


def _proj_specs(names, n_units, where):
    specs = []
    for name in names:
        s = SEG_NAMES.index(name)
        nblk = SEG_W[s] // n_units // LANES
        base = SEG_OFF[s] // LANES
        assert base % nblk == 0
        specs.append(pl.BlockSpec((nblk, TILE, LANES), lambda *g, base=base, nblk=nblk: (base // nblk + where(*g)[0], where(*g)[1], 0)))
    return specs


def _cols(ref, unit=0, n_units=1):
    n = ref.shape[0] // n_units
    return ref[unit * n] if n == 1 else jnp.concatenate([ref[unit * n + j] for j in range(n)], axis=1)


def _inproj_tiles(head, x, g_norm, slabs, w_glr):
    t_rows = x.shape[0] + TILE
    nt = t_rows // TILE
    n_blocks = AL_COLS // LANES

    def body(head_ref, x_ref, g_ref, slabs_hbm, wg_hbm, ut_ref, proj_ref, glr_ref, w_vm, wg_vm, edge_vm, sem):
        @pl.when(pl.program_id(0) == 0)
        def _():
            _load_weight(slabs_hbm, wg_hbm, w_vm, wg_vm, edge_vm, sem)

        x = _tile_rows(head_ref, x_ref)
        r = lax.rsqrt(_row_mean(x * x) + EPS)
        u32 = (x * r * g_ref[...]).astype(BF16).astype(F32)
        u = u32.astype(BF16)
        ut_ref[...] = u32.T.astype(BF16)
        for s in range(len(SEG_W)):
            res = _mm(u, w_vm[:, SEG_OFF[s]:SEG_OFF[s] + SEG_W[s]]).astype(BF16)
            for j in range(SEG_W[s] // LANES):
                proj_ref[SEG_OFF[s] // LANES + j] = res[:, j * LANES:(j + 1) * LANES]
        glr_ref[...] = _mm(u, wg_vm[...])

    return _call(
        body, "inproj_fwd_tiles", grid=(nt,),
        out_shape=[jax.ShapeDtypeStruct((nt, D_MODEL, TILE), BF16), jax.ShapeDtypeStruct((n_blocks, t_rows, LANES), BF16),
                   jax.ShapeDtypeStruct((t_rows, LANES), F32)],
        in_specs=[_head_spec(), _x_spec(), pl.BlockSpec((1, D_MODEL), lambda i: (0, 0)), ANY, ANY],
        out_specs=[pl.BlockSpec((None, D_MODEL, TILE), lambda i: (i, 0, 0)), pl.BlockSpec((n_blocks, TILE, LANES), lambda i: (0, i, 0)),
                   pl.BlockSpec((TILE, LANES), lambda i: (i, 0))],
        scratch_shapes=W_SCRATCH(), compiler_params=_params(("arbitrary",)),
    )(head, x, g_norm, slabs, w_glr)


def _ret_decay(lgh):
    i = lax.broadcasted_iota(jnp.int32, (TILE, TILE), 0)
    j = lax.broadcasted_iota(jnp.int32, (TILE, TILE), 1)
    rel = (i - j).astype(F32)
    return jnp.where(rel >= 0, jnp.exp(jnp.maximum(rel, 0.0) * lgh), 0.0)


def _ret_vectors(lgh):
    idx = lax.broadcasted_iota(jnp.int32, (TILE, 1), 0).astype(F32)
    xi = jnp.exp((idx + 1.0) * lgh)
    zeta = jnp.exp((TILE - 1.0 - idx) * lgh)
    gc = jnp.exp(jnp.full((1, 1), float(TILE), F32) * lgh)
    return xi, zeta, gc


def _rope_tables(nt):
    half = RET_QK // 2
    inv = ROPE_BASE ** (-jnp.arange(half, dtype=F32) / half)
    base = (jnp.arange(nt, dtype=F32) * TILE - float(PAD_ROWS))[:, None, None] * inv[None, None, :]
    off = jnp.arange(TILE, dtype=F32)[:, None] * inv[None, :]
    return jnp.cos(base), jnp.sin(base), jnp.cos(off), jnp.sin(off)


def _rope_specs(tile_of):
    return [pl.BlockSpec((None, 1, RET_QK // 2), lambda i: (tile_of(i), 0, 0))] * 2 + [pl.BlockSpec((TILE, RET_QK // 2), lambda i: (0, 0))] * 2


def _rope_angles(cb_ref, sb_ref, co_ref, so_ref):
    cb, sb, co, so = cb_ref[...], sb_ref[...], co_ref[...], so_ref[...]
    return cb * co - sb * so, sb * co + cb * so


def _ret_fwd(proj, rope, gain, lg, row_shards):
    t_rows = proj.shape[1]
    nt = t_rows // TILE
    ns = len(row_shards)

    def body(lg_ref, q_ref, k_ref, v_ref, g_ref, cb_ref, sb_ref, co_ref, so_ref, gain_ref, *rest):
        shard_refs, (oraw_ref, oret_ref, st_ref), gathered = rest[:ns], rest[ns:ns + 3], rest[ns + 3:2 * ns + 3]
        s_acc, dm = rest[2 * ns + 3:2 * ns + 5]
        gather = _Exchange(shard_refs, gathered, rest[2 * ns + 5:], among_chips=False)
        t = pl.program_id(0)

        @pl.when(t == 0)
        def _():
            gather.start()
            s_acc[...] = jnp.zeros_like(s_acc)
            for h in range(RET_HEADS):
                dm[h] = _ret_decay(lg_ref[h])

        @pl.when(t == nt - 1)
        def _():
            gather.finish()

        cos_t, sin_t = _rope_angles(cb_ref, sb_ref, co_ref, so_ref)
        for h in range(RET_HEADS):
            lgh = lg_ref[h]
            q = _rope(_cols(q_ref, h, RET_HEADS).astype(F32), cos_t, sin_t)
            k = _rope(_cols(k_ref, h, RET_HEADS).astype(F32), cos_t, sin_t) * (RET_QK ** -0.5)
            xi, zeta, gc = _ret_vectors(lgh)
            v = _cols(v_ref, h, RET_HEADS)
            s_in = s_acc[h]
            p = (_mm_nt(q.astype(BF16), k.astype(BF16)) * dm[h]).astype(BF16)
            o = _mm(p, v) + _mm((q * xi).astype(BF16), s_in.astype(BF16))
            st_ref[h] = s_in.astype(BF16)
            s_acc[h] = s_in * gc + _mm_tn((k * zeta).astype(BF16), v)
            cols = slice(h * RET_V, (h + 1) * RET_V)
            oraw_ref[:, cols] = o
            oc = o - _row_mean(o)
            n = oc * lax.rsqrt(_row_mean(oc * oc) + EPS) * gain_ref[:, cols]
            g = _cols(g_ref, h, RET_HEADS).astype(F32)
            oret_ref[:, cols] = (n * g * _sigmoid(g)).astype(BF16)

    row = lambda w: pl.BlockSpec((TILE, w), lambda t: (t, 0))
    outs = _call(
        body, "ret_fwd", grid=(nt,),
        out_shape=[jax.ShapeDtypeStruct((t_rows, RET_W), F32), jax.ShapeDtypeStruct((t_rows, RET_W), BF16),
                   jax.ShapeDtypeStruct((RET_HEADS, nt, RET_QK, RET_V), BF16)]
                  + [jax.ShapeDtypeStruct((N_DEV, *a.shape), a.dtype) for a in row_shards],
        in_specs=[pl.BlockSpec(memory_space=pltpu.SMEM)] + _proj_specs(("rq", "rk", "rv", "rg"), 1, lambda t: (0, t)) + _rope_specs(lambda t: t) + [
                  pl.BlockSpec((1, RET_W), lambda t: (0, 0))] + [ANY] * ns,
        out_specs=[row(RET_W), row(RET_W), pl.BlockSpec((RET_HEADS, None, RET_QK, RET_V), lambda t: (0, t, 0, 0))] + [ANY] * ns,
        scratch_shapes=[pltpu.VMEM((RET_HEADS, RET_QK, RET_V), F32), pltpu.VMEM((RET_HEADS, TILE, TILE), F32)] + _exchange_sems(ns, N_DEV),
        compiler_params=_params(("arbitrary",)),
    )(lg, proj, proj, proj, proj, *rope, gain, *row_shards)
    return outs[0], outs[1], outs[2], outs[3:]


def _ret_bwd(proj, rope, gain, lg, o_raw, do_ret, states, slabs):
    t_rows = proj.shape[1]
    nt = t_rows // TILE

    def body(lg_ref, q_ref, k_ref, v_ref, g_ref, cb_ref, sb_ref, co_ref, so_ref, gain_ref, oraw_ref, do_ref, st_ref, slabs_hbm,
             dq_ref, dk_ref, dv_ref, dg_ref, dgain_ref, du_ref, e_acc, dm, w_vm, wg_vm, edge_vm, sem):
        @pl.when(pl.program_id(0) == 0)
        def _():
            _load_weight(slabs_hbm, None, w_vm, wg_vm, edge_vm, sem, 0, RET_COLS)
            e_acc[...] = jnp.zeros_like(e_acc)
            for h in range(RET_HEADS):
                dm[h] = _ret_decay(lg_ref[h])
            dgain_ref[...] = jnp.zeros_like(dgain_ref)

        cos_t, sin_t = _rope_angles(cb_ref, sb_ref, co_ref, so_ref)
        for h in range(RET_HEADS):
            lgh = lg_ref[h]
            cols = slice(h * RET_V, (h + 1) * RET_V)
            qcols = slice(h * RET_QK, (h + 1) * RET_QK)
            q = _rope(_cols(q_ref, h, RET_HEADS).astype(F32), cos_t, sin_t)
            k = _rope(_cols(k_ref, h, RET_HEADS).astype(F32), cos_t, sin_t) * (RET_QK ** -0.5)
            xi, zeta, gc = _ret_vectors(lgh)
            v = _cols(v_ref, h, RET_HEADS)
            g = _cols(g_ref, h, RET_HEADS).astype(F32)
            o = oraw_ref[:, cols]
            do = do_ref[:, cols].astype(F32)
            oc = o - _row_mean(o)
            rstd = lax.rsqrt(_row_mean(oc * oc) + EPS)
            xh = oc * rstd
            gain_t = gain_ref[:, cols]
            sg = _sigmoid(g)
            dn = do * (g * sg)
            dg_b = (do * (xh * gain_t) * (sg * (1.0 + g * (1.0 - sg)))).astype(BF16)
            dg_ref[:, cols] = dg_b
            dgain_ref[:, cols] += _col_sum(dn * xh)
            dxh = dn * gain_t
            dob = (rstd * (dxh - _row_mean(dxh) - xh * _row_mean(dxh * xh))).astype(BF16)
            dmat = dm[h]
            qb, kb = q.astype(BF16), k.astype(BF16)
            p = (_mm_nt(qb, kb) * dmat).astype(BF16)
            dp = (_mm_nt(dob, v) * dmat).astype(BF16)
            s_in = st_ref[h]
            e_in = e_acc[h]
            e_b = e_in.astype(BF16)
            dq = _mm(dp, kb) + _mm_nt(dob, s_in) * xi
            dk = _mm_tn(dp, qb) + _mm_nt(v, e_b) * zeta
            dv_b = (_mm_tn(p, dob) + _mm((k * zeta).astype(BF16), e_b)).astype(BF16)
            dv_ref[:, cols] = dv_b
            e_acc[h] = e_in * gc + _mm_tn((q * xi).astype(BF16), dob)
            dq_b = _rope_bwd(dq, cos_t, sin_t).astype(BF16)
            dk_b = (_rope_bwd(dk, cos_t, sin_t) * (RET_QK ** -0.5)).astype(BF16)
            dq_ref[:, qcols] = dq_b
            dk_ref[:, qcols] = dk_b
            du = None
            for s, d_b in enumerate((dq_b, dk_b, dv_b, dg_b)):
                w = d_b.shape[1]
                part = _mm_nt(d_b, w_vm[:, SEG_OFF[s] + h * w:SEG_OFF[s] + (h + 1) * w])
                du = part if du is None else du + part
            if h == 0:
                du_ref[...] = du
            else:
                du_ref[...] += du

    row = lambda w: pl.BlockSpec((TILE, w), lambda j: (nt - 1 - j, 0))
    vec = pl.BlockSpec((1, RET_W), lambda j: (0, 0))
    return _call(
        body, "ret_bwd", grid=(nt,),
        out_shape=[jax.ShapeDtypeStruct((t_rows, RET_HEADS * RET_QK), BF16), jax.ShapeDtypeStruct((t_rows, RET_HEADS * RET_QK), BF16),
                   jax.ShapeDtypeStruct((t_rows, RET_W), BF16), jax.ShapeDtypeStruct((t_rows, RET_W), BF16),
                   jax.ShapeDtypeStruct((1, RET_W), F32), jax.ShapeDtypeStruct((t_rows, D_MODEL), F32)],
        in_specs=[pl.BlockSpec(memory_space=pltpu.SMEM)] + _proj_specs(("rq", "rk", "rv", "rg"), 1, lambda j: (0, nt - 1 - j)) + _rope_specs(lambda j: nt - 1 - j) + [vec,
                  row(RET_W), row(RET_W), pl.BlockSpec((RET_HEADS, None, RET_QK, RET_V), lambda j: (0, nt - 1 - j, 0, 0)), ANY],
        out_specs=[row(RET_HEADS * RET_QK), row(RET_HEADS * RET_QK), row(RET_W), row(RET_W), vec, row(D_MODEL)],
        scratch_shapes=[pltpu.VMEM((RET_HEADS, RET_QK, RET_V), F32), pltpu.VMEM((RET_HEADS, TILE, TILE), F32)] + _w_scratch(RET_COLS),
        compiler_params=_params(("arbitrary",)),
    )(lg, proj, proj, proj, proj, *rope, gain, o_raw, do_ret, states, slabs)


GLA_LEVELS = (32, 64, 128, 256)
N_TERMS = 1 + len(GLA_LEVELS)


def _gla_tables():
    p = jnp.arange(TILE)[:, None]
    r = jnp.arange(TILE)[None, :]
    masks = [(p // GLA_CHUNK == r // GLA_CHUNK) & (r <= p)]
    for blk in GLA_LEVELS:
        masks.append((p // blk == r // blk) & (p % blk >= blk // 2) & (r % blk < blk // 2))
    masks = jnp.stack(masks + [m.T for m in masks]).astype(F32)
    cum_fwd = jnp.concatenate([r <= p, masks[0] > 0], axis=0).astype(BF16)
    cum_bwd = jnp.concatenate([r >= p, masks[N_TERMS] > 0], axis=1).astype(BF16)
    return masks, cum_fwd, cum_bwd


def _split3(x):
    hi = x.astype(BF16)
    rest = x - hi.astype(F32)
    mid = rest.astype(BF16)
    lo = (rest - mid.astype(F32)).astype(BF16)
    return jnp.concatenate([hi, mid, lo], axis=1)


def _join3(y):
    w = y.shape[1] // 3
    return (y[:, 2 * w:] + y[:, w:2 * w]) + y[:, :w]


def _gla_decays(glr_ref, wgu_ref, b_ref, cum_ref):
    z = _mm(glr_ref[...].astype(BF16), wgu_ref[...].astype(BF16)) + b_ref[...]
    la = (jnp.minimum(z, 0.0) - jnp.log(1.0 + jnp.exp(-jnp.abs(z)))) / GLA_TAU
    width = la.shape[1]
    hi = la.astype(BF16)
    rest = la - hi.astype(F32)
    mid = rest.astype(BF16)
    lo = (rest - mid.astype(F32)).astype(BF16)
    y = _mm(cum_ref[...], jnp.concatenate([hi, mid, lo], axis=1))
    gb = (y[:, 2 * width:] + y[:, width:2 * width]) + y[:, :width]
    return z, gb[:TILE], gb[TILE:]


def _gla_prep(h, q_ref, k_ref, g_all, b_all, g_scr, ref_scr):
    cols = slice(h * GLA_K, (h + 1) * GLA_K)
    g, b = g_all[:, cols], b_all[:, cols]
    g_scr[h] = g
    factors = [(jnp.exp(b), jnp.exp(-b))]
    for lvl, blk in enumerate(GLA_LEVELS):
        for n in range(TILE // blk):
            ref_scr[h, lvl, n * blk:(n + 1) * blk, :] = jnp.broadcast_to(g_scr[h, pl.ds(n * blk + blk // 2 - 1, 1), :], (blk, GLA_K))
        x = g - ref_scr[h, lvl]
        factors.append((jnp.exp(jnp.minimum(x, 0.0)), jnp.exp(jnp.minimum(-x, 0.0))))
    g_last = g_scr[h, pl.ds(TILE - 1, 1), :]
    q = _cols(q_ref, h, GLA_HEADS).astype(F32) * (GLA_K ** -0.5)
    k = _cols(k_ref, h, GLA_HEADS).astype(F32)
    return q, k, factors, jnp.exp(g), jnp.exp(g_last), jnp.exp(g_last - g)


def _gla_scores(q, k, factors, m_ref):
    a = jnp.zeros((TILE, TILE), F32)
    for l, (fq, fk) in enumerate(factors):
        s = _mm_nt((q * fq).astype(BF16), (k * fk).astype(BF16))
        a = jnp.where(m_ref[l] > 0.0, s, a)
    return a


def _gla_fwd(proj, glr, wgu_pad, b_gate, gain, masks, cum_fwd):
    t_rows = glr.shape[0]
    nt = t_rows // TILE

    def body(q_ref, k_ref, v_ref, g_ref, glr_ref, wgu_ref, b_ref, gain_ref, m_ref, cum_ref, oraw_ref, ogla_ref, st_ref, at_ref,
             s_acc, g_scr, ref_scr):
        @pl.when(pl.program_id(0) == 0)
        def _():
            s_acc[...] = jnp.zeros_like(s_acc)

        _, g_all, b_all = _gla_decays(glr_ref, wgu_ref, b_ref, cum_ref)
        for h in range(GLA_HEADS):
            q, k, factors, e_g, e_last, e_end = _gla_prep(h, q_ref, k_ref, g_all, b_all, g_scr, ref_scr)
            v = _cols(v_ref, h, GLA_HEADS)
            st = s_acc[h]
            st_ref[h] = st
            a = _gla_scores(q, k, factors, m_ref)
            at_ref[h] = a.T.astype(BF16)
            o = _mm(a.astype(BF16), v) + _mm_nt((q * e_g).astype(BF16), st.astype(BF16))
            s_acc[h] = st * e_last + _mm(v.astype(F32).T.astype(BF16), (k * e_end).astype(BF16))
            cols = slice(h * GLA_V, (h + 1) * GLA_V)
            oraw_ref[:, cols] = o
            n = o * lax.rsqrt(_row_mean(o * o) + EPS) * gain_ref[:, cols]
            g = _cols(g_ref, h, GLA_HEADS).astype(F32)
            ogla_ref[:, cols] = (n * g * _sigmoid(g)).astype(BF16)

    row = lambda w: pl.BlockSpec((TILE, w), lambda t: (t, 0))
    whole = lambda *shape: pl.BlockSpec(shape, lambda t: (0,) * len(shape))
    return _call(
        body, "gla_fwd", grid=(nt,),
        out_shape=[jax.ShapeDtypeStruct((t_rows, GLA_W), F32), jax.ShapeDtypeStruct((t_rows, GLA_W), BF16),
                   jax.ShapeDtypeStruct((GLA_HEADS, nt, GLA_V, GLA_K), F32), jax.ShapeDtypeStruct((GLA_HEADS, t_rows, TILE), BF16)],
        in_specs=_proj_specs(("gq", "gk", "gv", "gg"), 1, lambda t: (0, t)) + [row(LANES), whole(LANES, GLA_HEADS * GLA_K),
                  whole(1, GLA_HEADS * GLA_K), whole(1, GLA_W), whole(N_TERMS, TILE, TILE), whole(2 * TILE, TILE)],
        out_specs=[row(GLA_W), row(GLA_W), pl.BlockSpec((GLA_HEADS, None, GLA_V, GLA_K), lambda t: (0, t, 0, 0)),
                   pl.BlockSpec((GLA_HEADS, TILE, TILE), lambda t: (0, t, 0))],
        scratch_shapes=[pltpu.VMEM((GLA_HEADS, GLA_V, GLA_K), F32), pltpu.VMEM((GLA_HEADS, TILE, GLA_K), F32),
                        pltpu.VMEM((GLA_HEADS, len(GLA_LEVELS), TILE, GLA_K), F32)],
        compiler_params=_params(("arbitrary",)),
    )(proj, proj, proj, proj, glr, wgu_pad, b_gate, gain, masks, cum_fwd)


def _gla_bwd(proj, glr, wgu_pad, b_gate, gain, o_raw, do_gla, states, a_t, masks, cum_fwd, cum_bwd):
    t_rows = glr.shape[0]
    nt = t_rows // TILE

    def body(q_ref, k_ref, v_ref, g_ref, glr_ref, wgu_ref, b_ref, gain_ref, m_ref, cum_ref, cumb_ref, oraw_ref, do_ref, st_ref, at_ref,
             dq_ref, dk_ref, dv_ref, dg_ref, dglr_ref, dwgu_ref, dbg_ref, dgain_ref, d_acc, g_scr, ref_scr, dref_scr):
        @pl.when(pl.program_id(0) == 0)
        def _():
            d_acc[...] = jnp.zeros_like(d_acc)
            dwgu_ref[...] = jnp.zeros_like(dwgu_ref)
            dbg_ref[...] = jnp.zeros_like(dbg_ref)
            dgain_ref[...] = jnp.zeros_like(dgain_ref)

        z_all, g_all, b_all = _gla_decays(glr_ref, wgu_ref, b_ref, cum_ref)
        dla_parts = []
        for h in range(GLA_HEADS):
            q, k, factors, e_g, e_last, e_end = _gla_prep(h, q_ref, k_ref, g_all, b_all, g_scr, ref_scr)
            v = _cols(v_ref, h, GLA_HEADS)
            cols = slice(h * GLA_V, (h + 1) * GLA_V)
            kcols = slice(h * GLA_K, (h + 1) * GLA_K)
            o = oraw_ref[:, cols]
            do = do_ref[:, cols].astype(F32)
            g = _cols(g_ref, h, GLA_HEADS).astype(F32)
            rinv = lax.rsqrt(_row_mean(o * o) + EPS)
            nh = o * rinv
            gain_t = gain_ref[:, cols]
            sg = _sigmoid(g)
            dn = do * (g * sg)
            dg_ref[:, cols] = (do * (nh * gain_t) * (sg * (1.0 + g * (1.0 - sg)))).astype(BF16)
            dgain_ref[:, cols] += _col_sum(dn * nh)
            dnh = dn * gain_t
            dor = rinv * (dnh - nh * _row_mean(dnh * nh))
            dob = dor.astype(BF16)
            a_t = at_ref[h]
            da = _mm_nt(dob, v).astype(BF16)
            da_t = _mm_nt(v, dob).astype(BF16)
            st_in = st_ref[h]
            d_out = d_acc[h]
            d_out_b = d_out.astype(BF16)
            qg, kg = q * e_g, k * e_end
            dqg = _mm(dob, st_in.astype(BF16))
            dkg = _mm(v, d_out_b)
            dv_ref[:, cols] = (_mm(a_t, dob) + _mm_nt(kg.astype(BF16), d_out_b)).astype(BF16)
            d_acc[h] = d_out * e_last + _mm(dor.T.astype(BF16), qg.astype(BF16))
            dq = dqg * e_g
            dk = dkg * e_end
            dkg_kg = dkg * kg
            dg_cum = dqg * qg - dkg_kg
            db = None
            for l, (fq, fk) in enumerate(factors):
                qt, kt = q * fq, k * fk
                dqt = _mm(da * m_ref[l], kt.astype(BF16))
                dkt = _mm(da_t * m_ref[N_TERMS + l], qt.astype(BF16))
                dq = dq + dqt * fq
                dk = dk + dkt * fk
                diff = dqt * qt - dkt * kt
                if l == 0:
                    db = diff
                else:
                    dg_cum = dg_cum + diff
                    dref_scr[h, l - 1] = diff
            dq_ref[:, kcols] = (dq * (GLA_K ** -0.5)).astype(BF16)
            dk_ref[:, kcols] = dk.astype(BF16)
            g_scr[h] = dg_cum
            g_scr[h, pl.ds(TILE - 1, 1), :] += e_last * _col_sum(d_out * st_in) + _col_sum(dkg_kg)
            for lvl, blk in enumerate(GLA_LEVELS):
                for n in range(TILE // blk):
                    g_scr[h, pl.ds(n * blk + blk // 2 - 1, 1), :] -= _col_sum(dref_scr[h, lvl, n * blk:(n + 1) * blk, :])
            dla_parts.append(_join3(_mm(cumb_ref[...], jnp.concatenate([_split3(g_scr[h]), _split3(db)], axis=0))))
        dz = jnp.concatenate(dla_parts, axis=1) * (1.0 / GLA_TAU) * _sigmoid(-z_all)
        dzb = dz.astype(BF16)
        wgu_b = wgu_ref[...].astype(BF16)
        for h in range(GLA_HEADS):
            kcols = slice(h * GLA_K, (h + 1) * GLA_K)
            dglr_ref[h] = _mm_nt(dzb[:, kcols], wgu_b[:, kcols]).astype(BF16)
        dwgu_ref[...] += _mm(glr_ref[...].T.astype(BF16), dzb)
        dbg_ref[...] += _col_sum(dz)

    row = lambda w: pl.BlockSpec((TILE, w), lambda j: (nt - 1 - j, 0))
    whole = lambda *shape: pl.BlockSpec(shape, lambda j: (0,) * len(shape))
    return _call(
        body, "gla_bwd", grid=(nt,),
        out_shape=[jax.ShapeDtypeStruct((t_rows, GLA_HEADS * GLA_K), BF16), jax.ShapeDtypeStruct((t_rows, GLA_HEADS * GLA_K), BF16),
                   jax.ShapeDtypeStruct((t_rows, GLA_W), BF16), jax.ShapeDtypeStruct((t_rows, GLA_W), BF16),
                   jax.ShapeDtypeStruct((GLA_HEADS, t_rows, LANES), BF16), jax.ShapeDtypeStruct((LANES, GLA_HEADS * GLA_K), F32),
                   jax.ShapeDtypeStruct((1, GLA_HEADS * GLA_K), F32), jax.ShapeDtypeStruct((1, GLA_W), F32)],
        in_specs=_proj_specs(("gq", "gk", "gv", "gg"), 1, lambda j: (0, nt - 1 - j)) + [row(LANES),
                  whole(LANES, GLA_HEADS * GLA_K), whole(1, GLA_HEADS * GLA_K), whole(1, GLA_W),
                  whole(2 * N_TERMS, TILE, TILE), whole(2 * TILE, TILE), whole(TILE, 2 * TILE), row(GLA_W), row(GLA_W),
                  pl.BlockSpec((GLA_HEADS, None, GLA_V, GLA_K), lambda j: (0, nt - 1 - j, 0, 0)),
                  pl.BlockSpec((GLA_HEADS, TILE, TILE), lambda j: (0, nt - 1 - j, 0))],
        out_specs=[row(GLA_HEADS * GLA_K), row(GLA_HEADS * GLA_K), row(GLA_W), row(GLA_W),
                   pl.BlockSpec((GLA_HEADS, TILE, LANES), lambda j: (0, nt - 1 - j, 0)), whole(LANES, GLA_HEADS * GLA_K),
                   whole(1, GLA_HEADS * GLA_K), whole(1, GLA_W)],
        scratch_shapes=[pltpu.VMEM((GLA_HEADS, GLA_V, GLA_K), F32), pltpu.VMEM((GLA_HEADS, TILE, GLA_K), F32),
                        pltpu.VMEM((GLA_HEADS, len(GLA_LEVELS), TILE, GLA_K), F32),
                        pltpu.VMEM((GLA_HEADS, len(GLA_LEVELS), TILE, GLA_K), F32)],
        compiler_params=_params(("arbitrary",)),
    )(proj, proj, proj, proj, glr, wgu_pad, b_gate, gain, masks.astype(BF16), cum_fwd, cum_bwd, o_raw, do_gla, states, a_t)


def _merge_fwd_bwd(o_ret, o_gla, proj, x, target, g_final, w_br, w_bg, w_out):
    t_rows = x.shape[0] + TILE
    nt = t_rows // TILE

    def body(oret_ref, ogla_ref, mr_ref, mg_ref, h0_ref, tgt_ref, gf_ref, wbr_hbm, wbg_hbm, wout_hbm,
             dh1_ref, dmr_ref, dmg_ref, doret_ref, dogla_ref, loss_ref, dgf_ref, dwbr_hbm, dwbg_hbm, dwout_hbm,
             wbr, wbg, wout, abr, abg, aout, sem):
        i = pl.program_id(0)

        @pl.when(i == 0)
        def _():
            cps = [pltpu.make_async_copy(s, d, sem.at[n]) for n, (s, d) in enumerate(((wbr_hbm, wbr), (wbg_hbm, wbg), (wout_hbm, wout)))]
            for cp in cps:
                cp.start()
            abr[...] = jnp.zeros_like(abr)
            abg[...] = jnp.zeros_like(abg)
            aout[...] = jnp.zeros_like(aout)
            loss_ref[...] = jnp.zeros_like(loss_ref)
            dgf_ref[...] = jnp.zeros_like(dgf_ref)
            for cp in cps:
                cp.wait()
            dh1_ref[...] = jnp.zeros_like(dh1_ref)
            dmr_ref[...] = jnp.zeros_like(dmr_ref)
            dmg_ref[...] = jnp.zeros_like(dmg_ref)
            doret_ref[...] = jnp.zeros_like(doret_ref)
            dogla_ref[...] = jnp.zeros_like(dogla_ref)

        @pl.when(i > 0)
        def _():
            oret, ogla = oret_ref[...], ogla_ref[...]
            br, bg = _mm(oret, wbr[...]), _mm(ogla, wbg[...])
            sr, sg = _sigmoid(_cols(mr_ref).astype(F32)), _sigmoid(_cols(mg_ref).astype(F32))
            mb = (sr * br + sg * bg).astype(BF16)
            h1 = h0_ref[...] + _mm(mb, wout[...])
            r2 = lax.rsqrt(_row_mean(h1 * h1) + EPS)
            hn = h1 * r2
            gf = gf_ref[...]
            diff = hn * gf - tgt_ref[...]
            loss_ref[...] += 0.5 * jnp.sum(_row_mean(diff * diff))
            dy = diff * (1.0 / D_MODEL)
            dgf_ref[...] += _col_sum(dy * hn)
            dyg = dy * gf
            dh1 = r2 * (dyg - hn * _row_mean(dyg * hn))
            dh1_ref[...] = dh1
            dh1b = dh1.astype(BF16)
            dm = _mm_nt(dh1b, wout[...])
            aout[...] += _mm_tn(mb, dh1b)
            dbr = (dm * sr).astype(BF16)
            dbg = (dm * sg).astype(BF16)
            dmr_ref[...] = (dm * br * sr * (1.0 - sr)).astype(BF16)
            dmg_ref[...] = (dm * bg * sg * (1.0 - sg)).astype(BF16)
            doret_ref[...] = _mm_nt(dbr, wbr[...]).astype(BF16)
            dogla_ref[...] = _mm_nt(dbg, wbg[...]).astype(BF16)
            abr[...] += _mm_tn(oret, dbr)
            abg[...] += _mm_tn(ogla, dbg)

        @pl.when(i == nt - 1)
        def _():
            wbr[...] = abr[...].astype(BF16)
            wbg[...] = abg[...].astype(BF16)
            wout[...] = aout[...].astype(BF16)
            pltpu.sync_copy(wbr, dwbr_hbm)
            pltpu.sync_copy(wbg, dwbg_hbm)
            pltpu.sync_copy(wout, dwout_hbm)

    row = lambda w: pl.BlockSpec((TILE, w), lambda i: (i, 0))
    one = lambda w: pl.BlockSpec((1, w), lambda i: (0, 0))
    return _call(
        body, "merge_fwd_bwd", grid=(nt,),
        out_shape=[jax.ShapeDtypeStruct((t_rows, D_MODEL), F32), jax.ShapeDtypeStruct((t_rows, D_MODEL), BF16),
                   jax.ShapeDtypeStruct((t_rows, D_MODEL), BF16), jax.ShapeDtypeStruct((t_rows, RET_W), BF16),
                   jax.ShapeDtypeStruct((t_rows, GLA_W), BF16), jax.ShapeDtypeStruct((1, LANES), F32),
                   jax.ShapeDtypeStruct((1, D_MODEL), F32), jax.ShapeDtypeStruct((RET_W, D_MODEL), BF16),
                   jax.ShapeDtypeStruct((GLA_W, D_MODEL), BF16), jax.ShapeDtypeStruct((D_MODEL, D_MODEL), BF16)],
        in_specs=[row(RET_W), row(GLA_W)] + _proj_specs(("mr", "mg"), 1, lambda i: (0, i)) + [_x_spec(), _x_spec(), one(D_MODEL), ANY, ANY, ANY],
        out_specs=[row(D_MODEL), row(D_MODEL), row(D_MODEL), row(RET_W), row(GLA_W), one(LANES), one(D_MODEL), ANY, ANY, ANY],
        scratch_shapes=[pltpu.VMEM((RET_W, D_MODEL), BF16), pltpu.VMEM((GLA_W, D_MODEL), BF16), pltpu.VMEM((D_MODEL, D_MODEL), BF16),
                        pltpu.VMEM((RET_W, D_MODEL), F32), pltpu.VMEM((GLA_W, D_MODEL), F32), pltpu.VMEM((D_MODEL, D_MODEL), F32),
                        pltpu.SemaphoreType.DMA((3,))],
        compiler_params=_params(("arbitrary",)),
    )(o_ret, o_gla, proj, proj, x, target, g_final, w_br, w_bg, w_out)


def _inproj_bwd_x(dseg, du_ret, dglr, head, x, dh1, g_norm, slabs, w_glr, chip_partials):
    t_rows = x.shape[0] + TILE
    nt = t_rows // TILE
    ne = len(chip_partials)
    segs = range(4, len(SEG_NAMES))
    ns = len(segs)

    def body(*refs):
        d_refs = refs[:ns]
        du_ref, dglr_ref, head_ref, x_ref, dh1_ref, g_ref, slabs_hbm, wg_hbm = refs[ns:ns + 8]
        part_refs = refs[ns + 8:ns + 8 + ne]
        dx_ref, dhead_ref, dgn_ref = refs[ns + 8 + ne:ns + 11 + ne]
        landed = refs[ns + 11 + ne:ns + 11 + 2 * ne]
        w_vm, wg_vm, edge_vm, sem = refs[ns + 11 + 2 * ne:ns + 15 + 2 * ne]
        exchange = _Exchange(part_refs, landed, refs[ns + 15 + 2 * ne:], among_chips=True)

        @pl.when(pl.program_id(0) == 0)
        def _():
            exchange.start()
            dgn_ref[...] = jnp.zeros_like(dgn_ref)
            _load_weight(slabs_hbm, wg_hbm, w_vm, wg_vm, edge_vm, sem, RET_COLS, AL_COLS)

        @pl.when(pl.program_id(0) == nt - 1)
        def _():
            exchange.finish()

        dglr = dglr_ref[0].astype(F32)
        for h in range(1, GLA_HEADS):
            dglr = dglr + dglr_ref[h].astype(F32)
        du = du_ref[...] + _mm_nt(dglr.astype(BF16), wg_vm[...])
        for s, d_ref in zip(segs, d_refs):
            du = du + _mm_nt(d_ref[...], w_vm[:, SEG_OFF[s] - RET_COLS:SEG_OFF[s] - RET_COLS + SEG_W[s]])
        x = _tile_rows(head_ref, x_ref)
        r = lax.rsqrt(_row_mean(x * x) + EPS)
        hn = x * r
        dgn_ref[...] += _col_sum(du * hn)
        dug = du * g_ref[...]
        dh0 = dh1_ref[...] + r * (dug - hn * _row_mean(dug * hn))
        dx_ref[...] = dh0

        @pl.when(pl.program_id(0) == 0)
        def _():
            dhead_ref[...] = dh0

    row = lambda w: pl.BlockSpec((TILE, w), lambda i: (i, 0))
    one = pl.BlockSpec((1, D_MODEL), lambda i: (0, 0))
    return _call(
        body, "inproj_bwd_x", grid=(nt,),
        out_shape=[jax.ShapeDtypeStruct((t_rows - TILE, D_MODEL), F32), jax.ShapeDtypeStruct((TILE, D_MODEL), F32),
                   jax.ShapeDtypeStruct((1, D_MODEL), F32)] + [jax.ShapeDtypeStruct(a.shape, a.dtype) for a in chip_partials],
        in_specs=[row(SEG_W[s]) for s in segs] + [row(D_MODEL), pl.BlockSpec((GLA_HEADS, TILE, LANES), lambda i: (0, i, 0)),
                                                  _head_spec(), _x_spec(), row(D_MODEL), one, ANY, ANY] + [ANY] * ne,
        out_specs=[_x_spec(), _head_spec(), one] + [ANY] * ne,
        scratch_shapes=_w_scratch(AL_COLS - RET_COLS) + _exchange_sems(ne, N_CHIP),
        compiler_params=_params(("arbitrary",)),
    )(*[dseg[SEG_NAMES[s]] for s in segs], du_ret, dglr, head, x, dh1, g_norm, slabs, w_glr, *chip_partials)


W_TILE = 512


def _inproj_bwd_w(ut, dseg, dglr, row_sends):
    nt = ut.shape[0]
    t_rows = nt * TILE
    kc = 3 if nt % 3 == 0 else 1
    tiles = [(s, c) for s in range(len(SEG_W)) for c in range(0, SEG_W[s], W_TILE)]
    bpt = W_TILE // LANES
    nr = len(row_sends)
    n = 1 + nr
    last_tile = [(SLAB_BLK0[d] + SLAB_BLOCKS - 1) // bpt for d in range(N_DEV)]

    def body(ut_hbm, *refs):
        d_refs, dglr_hbm, row_refs = refs[:10], refs[10], refs[11:11 + nr]
        out_hbm, oglr_ref, sib = refs[11 + nr], refs[12 + nr], refs[13 + nr:13 + nr + n]
        ut_vm, dbuf, obuf, acc, gbuf, sem, send_sems, recv_sems = refs[13 + nr + n:]
        x, y, core = _position()

        def handover(d, k, landed=False):
            q = d // 2
            src = out_hbm.at[pl.ds(SLAB_BLK0[d], SLAB_BLOCKS)] if k == 0 else row_refs[k - 1].at[d]
            return pltpu.make_async_remote_copy(src_ref=sib[k].at[q] if landed else src, dst_ref=sib[k].at[q],
                                                send_sem=send_sems.at[n * q + k], recv_sem=recv_sems.at[n * q + k],
                                                device_id=(x, y, 1 - core), device_id_type=MESH)

        def for_sibling(d, ks, fn):
            @pl.when(d % 2 != core)
            def _():
                for k in ks:
                    fn(handover(d, k))

        for d in range(N_DEV):
            for_sibling(d, range(1, n), lambda cp: cp.start())

        def fetch(i):
            s, c = tiles[i]
            return pltpu.make_async_copy(d_refs[s].at[:, pl.ds(c, W_TILE)], dbuf.at[i % 2], sem.at[1 + i % 2])

        def contract(rhs_refs, width):
            acc[:, :width] = jnp.zeros((D_MODEL, width), F32)

            def step(k, carry):
                part = None
                for j in range(kc):
                    kk = k * kc + j
                    for rhs_ref in rhs_refs:
                        prod = _mm(ut_vm[kk], rhs_ref[pl.ds(pl.multiple_of(kk * TILE, TILE), TILE), :])
                        part = prod if part is None else part + prod
                acc[:, :width] += part
                return carry

            lax.fori_loop(0, nt // kc, step, 0)
            return acc[:, :width]

        load_ut = pltpu.make_async_copy(ut_hbm, ut_vm, sem.at[0])
        load_glr = pltpu.make_async_copy(dglr_hbm, gbuf, sem.at[5])
        load_ut.start()
        load_glr.start()
        fetch(0).start()
        load_ut.wait()
        stores = {}

        def stored(i):
            stores[i].wait()
            for d in range(N_DEV):
                if last_tile[d] == i:
                    for_sibling(d, [0], lambda cp: cp.start())

        for i, (s, c) in enumerate(tiles):
            if i + 1 < len(tiles):
                fetch(i + 1).start()
            fetch(i).wait()
            if i >= 2:
                stored(i - 2)
            total = contract([dbuf.at[i % 2]], W_TILE)
            for j in range(bpt):
                obuf[i % 2, j] = total[:, j * LANES:(j + 1) * LANES].astype(BF16)
            blk0 = (SEG_OFF[s] + c) // LANES
            stores[i] = pltpu.make_async_copy(obuf.at[i % 2], out_hbm.at[pl.ds(blk0, bpt)], sem.at[3 + i % 2])
            stores[i].start()
        for i in range(max(0, len(tiles) - 2), len(tiles)):
            stored(i)
        load_glr.wait()
        head_sum = gbuf[0].astype(F32)
        for h in range(1, GLA_HEADS):
            head_sum = head_sum + gbuf[h].astype(F32)
        gbuf[0] = head_sum.astype(BF16)
        oglr_ref[...] = contract([gbuf.at[0]], LANES)
        for q in range(N_CHIP):
            for k in range(n):
                handover(2 * q, k, landed=True).wait_recv()
        for d in range(N_DEV):
            for_sibling(d, range(n), lambda cp: cp.wait_send())

    outs = _call(
        body, "inproj_bwd_w",
        out_shape=[jax.ShapeDtypeStruct((AL_COLS // LANES, D_MODEL, LANES), BF16), jax.ShapeDtypeStruct((D_MODEL, LANES), F32),
                   jax.ShapeDtypeStruct((N_CHIP, SLAB_BLOCKS, D_MODEL, LANES), BF16)]
                  + [jax.ShapeDtypeStruct((N_CHIP, *r.shape[1:]), BF16) for r in row_sends],
        in_specs=[ANY] * (12 + nr), out_specs=[ANY, pl.BlockSpec(memory_space=pltpu.VMEM)] + [ANY] * n,
        scratch_shapes=[pltpu.VMEM((nt, D_MODEL, TILE), BF16), pltpu.VMEM((2, t_rows, W_TILE), BF16),
                        pltpu.VMEM((2, bpt, D_MODEL, LANES), BF16), pltpu.VMEM((D_MODEL, W_TILE), F32),
                        pltpu.VMEM((GLA_HEADS, t_rows, LANES), BF16), pltpu.SemaphoreType.DMA((6,)),
                        pltpu.SemaphoreType.DMA((n * N_CHIP,)), pltpu.SemaphoreType.DMA((n * N_CHIP,))],
        compiler_params=_params(),
    )(ut, *[dseg[n_] for n_ in SEG_NAMES], dglr, *row_sends)
    return outs[0], outs[1], outs[2], outs[3:]


def _position():
    x, y, c = lax.axis_index("x"), lax.axis_index("y"), lax.axis_index("c")
    return x, y, c


def _index(px, py, pc):
    return 4 * px + 2 * py + pc


def _all_gather(arrs, name):
    n = len(arrs)

    def body(*refs):
        ins, outs = refs[:n], refs[n:2 * n]
        send_sems, recv_sems, local_sems = refs[2 * n:]
        x, y, c = _position()
        me, sibling = (x, y, c), (x, y, 1 - c)
        chips = [(1 - x, y), (x, 1 - y), (1 - x, 1 - y)]

        def copy(a, k, block, to, src=None):
            dst = outs[a].at[_index(*block)]
            return pltpu.make_async_remote_copy(src_ref=dst if src is None else src, dst_ref=dst,
                                                send_sem=send_sems.at[7 * a + k], recv_sem=recv_sems.at[7 * a + k],
                                                device_id=to, device_id_type=MESH)

        def relay(a, j):
            return copy(a, 3, (*chips[j], c), (*chips[1 - j], c))

        mine = [pltpu.make_async_copy(ins[a], outs[a].at[_index(*me)], local_sems.at[a]) for a in range(n)]
        for cp in mine:
            cp.start()
        first = []
        for a in range(n):
            first.append(copy(a, 0, me, sibling, src=ins[a]))
            first += [copy(a, 1 + j, me, (*chips[j], c), src=ins[a]) for j in range(2)]
        for cp in first:
            cp.start()
        passed = []
        for j in range(3):
            for a in range(n):
                copy(a, 1 + j, (*chips[j], c), me).wait_recv()
                cp = copy(a, 4 + j, (*chips[j], c), sibling)
                cp.start()
                passed.append(cp)
            if j < 2:
                @pl.when(c == j)
                def _():
                    for a in range(n):
                        relay(a, j).start()
        for a in range(n):
            copy(a, 0, sibling, me).wait_recv()
            for j in range(3):
                copy(a, 4 + j, (*chips[j], 1 - c), me).wait_recv()
        for cp in first + passed:
            cp.wait_send()
        for j in range(2):
            @pl.when(c == j)
            def _():
                for a in range(n):
                    relay(a, j).wait_send()
        for cp in mine:
            cp.wait()

    return _call(
        body, name,
        out_shape=[jax.ShapeDtypeStruct((N_DEV, *a.shape), a.dtype) for a in arrs],
        in_specs=[ANY] * n, out_specs=[ANY] * n,
        scratch_shapes=[pltpu.SemaphoreType.DMA((7 * n,)), pltpu.SemaphoreType.DMA((7 * n,)), pltpu.SemaphoreType.DMA((n,))],
    )(*arrs)


N_CHIP = N_DEV // 2


def _slab_block0(owner):
    step = SLAB_BLK0[1]
    assert all(SLAB_BLK0[d] == step * d - (d == N_DEV - 1) for d in range(N_DEV))
    return step * owner - jnp.where(owner == N_DEV - 1, 1, 0)


def _add_bf16(c_ref, a_ref, b_ref, o_ref):
    o_ref[...] = (a_ref[...].astype(F32) + b_ref[...].astype(F32)).astype(BF16)


def _chip_partial_slab(dw_blocks, sib, core):
    blk = pl.BlockSpec((None, SLAB_BLOCKS, D_MODEL, LANES), lambda q, c_ref: (q, 0, 0, 0))
    return _call(
        functools.partial(_add_bf16), "chip_partial_w_in", out_shape=jax.ShapeDtypeStruct(sib.shape, BF16),
        grid_spec=pltpu.PrefetchScalarGridSpec(
            num_scalar_prefetch=1, grid=(N_CHIP,),
            in_specs=[pl.BlockSpec((pl.Element(SLAB_BLOCKS), pl.Element(D_MODEL), pl.Element(LANES)),
                                   lambda q, c_ref: (_slab_block0(2 * q + c_ref[0]), 0, 0)), blk],
            out_specs=blk),
        compiler_params=_params(("arbitrary",)),
    )(core, dw_blocks, sib)


def _chip_partial_rows(sends, sibs, core):
    n = len(sends)

    def body(c_ref, *refs):
        for k in range(n):
            _add_bf16(c_ref, refs[k], refs[n + k], refs[2 * n + k])

    own = [pl.BlockSpec((None, *a.shape[1:]), lambda q, c_ref: (2 * q + c_ref[0], 0, 0)) for a in sends]
    blk = [pl.BlockSpec((None, *a.shape[1:]), lambda q, c_ref: (q, 0, 0)) for a in sibs]
    return _call(
        body, "chip_partial_rows", out_shape=[jax.ShapeDtypeStruct(a.shape, BF16) for a in sibs],
        grid_spec=pltpu.PrefetchScalarGridSpec(num_scalar_prefetch=1, grid=(N_CHIP,), in_specs=own + blk, out_specs=blk),
        compiler_params=_params(("arbitrary",)),
    )(core, *sends, *sibs)


def _exchange_sems(n_arrays, n_peers):
    return [pltpu.SemaphoreType.DMA((n_arrays * n_peers,)), pltpu.SemaphoreType.DMA((n_arrays * n_peers,)),
            pltpu.SemaphoreType.DMA((n_arrays,))]


class _Exchange:
    def __init__(self, srcs, dsts, sems, among_chips):
        self.arrs = list(zip(srcs, dsts))
        self.n = len(self.arrs)
        self.send_sems, self.recv_sems, self.local_sems = sems
        self.among_chips = among_chips
        x, y, c = _position()
        self.c = c
        self.me = 2 * x + y if among_chips else _index(x, y, c)
        self.n_peers = N_CHIP if among_chips else N_DEV

    def _device(self, p):
        return (p // 2, p % 2, self.c) if self.among_chips else (p // 4, (p // 2) % 2, p % 2)

    def _src(self, k, p):
        src = self.arrs[k][0]
        return src.at[p] if self.among_chips else src

    def _mine(self):
        return [pltpu.make_async_copy(self._src(k, self.me), self.arrs[k][1].at[self.me], self.local_sems.at[k]) for k in range(self.n)]

    def _copy(self, p, k, landing):
        return pltpu.make_async_remote_copy(
            src_ref=self._src(k, p), dst_ref=self.arrs[k][1].at[landing], send_sem=self.send_sems.at[self.n * p + k],
            recv_sem=self.recv_sems.at[self.n * landing + k], device_id=self._device(p), device_id_type=MESH)

    def _others(self, fn):
        for p in range(self.n_peers):
            @pl.when(p != self.me)
            def _():
                for k in range(self.n):
                    fn(p, k)

    def start(self):
        for cp in self._mine():
            cp.start()
        self._others(lambda p, k: self._copy(p, k, self.me).start())

    def finish(self):
        self._others(lambda p, k: self._copy(p, k, p).wait_recv())
        self._others(lambda p, k: self._copy(p, k, self.me).wait_send())
        for cp in self._mine():
            cp.wait()


def _adamw(g, w, m, v):
    m_new = ADAM_B1 * m + (1.0 - ADAM_B1) * g
    v_new = ADAM_B2 * v + (1.0 - ADAM_B2) * (g * g)
    m_hat = m_new / (1.0 - ADAM_B1 ** ADAM_STEP)
    v_hat = v_new / (1.0 - ADAM_B2 ** ADAM_STEP)
    delta = -ADAM_LR * (m_hat / (jnp.sqrt(v_hat) + ADAM_EPS) + ADAM_WD * w)
    return delta, m_new, v_new


def _sum_partials(p_ref):
    g = p_ref[0].astype(F32)
    for d in range(1, p_ref.shape[0]):
        g = g + p_ref[d].astype(F32)
    return g


def _reduce_adam_rows(parts, ws, ms, vs):
    n = len(ws)

    def body(*refs):
        ins, outs = refs[:4 * n], refs[4 * n:]
        for k in range(n):
            p_ref, w_ref, m_ref, v_ref = ins[k], ins[n + k], ins[2 * n + k], ins[3 * n + k]
            g = _sum_partials(p_ref)
            outs[4 * k][...] = g
            outs[4 * k + 1][...], outs[4 * k + 2][...], outs[4 * k + 3][...] = _adamw(g, w_ref[...], m_ref[...], v_ref[...])

    outs = _call(
        body, "adam_row_weights", out_shape=[jax.ShapeDtypeStruct(w.shape, F32) for w in ws for _ in range(4)],
        compiler_params=_params(),
    )(*parts, *ws, *ms, *vs)
    return [tuple(outs[4 * k:4 * k + 4]) for k in range(n)]


def _reduce_adam_slab(parts, glr, w_t, m_t, v_t, me):
    cols, rows = w_t.shape
    shift = jnp.asarray(SLAB_SHIFT, jnp.int32)[me]
    glr_at = jnp.where(me == GLR_DEV, GLR_LOCAL, cols).astype(jnp.int32)

    def body(s_ref, p_ref, glr_ref, w_ref, m_ref, v_ref, g_ref, d_ref, mo_ref, vo_ref, slab_t):
        shift, glr_at = s_ref[0], s_ref[1]
        tall = jnp.concatenate([_sum_partials(p_ref.at[:, j]).T for j in range(SLAB_BLOCKS)], axis=0)
        before = pltpu.roll(tall, SLAB_W - shift, 0)
        after = pltpu.roll(tall, lax.rem(SLAB_W - shift + GLA_RANK, SLAB_W), 0)
        wide = jnp.concatenate([glr_ref[...].T, jnp.zeros((SLAB_W - LANES, LANES), F32)], axis=0)
        placed = pltpu.roll(wide, lax.rem(glr_at, SLAB_W), 0)
        row = lax.broadcasted_iota(jnp.int32, (SLAB_W, LANES), 0)
        slab_t[...] = jnp.where(row < glr_at, before, jnp.where(row < glr_at + GLA_RANK, placed, after))
        g = slab_t[pl.ds(0, cols), :]
        g_ref[...] = g
        d_ref[...], mo_ref[...], vo_ref[...] = _adamw(g, w_ref[...], m_ref[...], v_ref[...])

    blk = pl.BlockSpec((cols, LANES), lambda i, s: (0, i))
    return _call(
        body, "adam_w_in", out_shape=[jax.ShapeDtypeStruct((cols, rows), F32)] * 4,
        grid_spec=pltpu.PrefetchScalarGridSpec(
            num_scalar_prefetch=1, grid=(rows // LANES,),
            in_specs=[pl.BlockSpec((parts.shape[0], SLAB_BLOCKS, LANES, LANES), lambda i, s: (0, 0, i, 0)),
                      pl.BlockSpec((LANES, LANES), lambda i, s: (i, 0)), blk, blk, blk],
            out_specs=[blk] * 4, scratch_shapes=[pltpu.VMEM((SLAB_W, LANES), F32)]),
        compiler_params=_params(("arbitrary",)),
    )(jnp.stack([shift, glr_at]), parts, glr, w_t, m_t, v_t)


def _reduce_small(parts):
    def body(p_ref, o_ref):
        o_ref[...] = _sum_partials(p_ref)

    return _call(body, "reduce_small", out_shape=jax.ShapeDtypeStruct(parts.shape[1:], F32))(parts)


def _adam_small(g, w, m, v):
    def body(g_ref, w_ref, m_ref, v_ref, d_ref, mo_ref, vo_ref):
        d_ref[...], mo_ref[...], vo_ref[...] = _adamw(g_ref[...], w_ref[...], m_ref[...], v_ref[...])

    return _call(body, "adam_small", out_shape=[jax.ShapeDtypeStruct(g.shape, F32)] * 3)(g, w, m, v)


def _pack_rows(arrs):
    rows = []
    for a in arrs:
        flat = a.reshape(-1).astype(F32)
        pad = (-flat.shape[0]) % LANES
        rows.append(jnp.pad(flat, (0, pad)).reshape(-1, LANES))
    packed = jnp.concatenate(rows, axis=0)
    return jnp.pad(packed, ((0, (-packed.shape[0]) % 8), (0, 0)))


def _unpack_rows(packed, shapes):
    out, r = [], 0
    for shp in shapes:
        size = 1
        for s in shp:
            size *= s
        nrows = -(-size // LANES)
        out.append(packed[r:r + nrows].reshape(-1)[:size].reshape(shp))
        r += nrows
    return out


def _shard_to_slab(shard, d):
    glr = jnp.zeros((D_MODEL, GLA_RANK), shard.dtype)
    if d == GLR_DEV:
        glr = shard[:, GLR_LOCAL:GLR_LOCAL + GLA_RANK]
        shard = jnp.concatenate([shard[:, :GLR_LOCAL], shard[:, GLR_LOCAL + GLA_RANK:]], axis=1)
    return jnp.pad(shard, ((0, 0), (SLAB_SHIFT[d], SLAB_W - SLAB_SHIFT[d] - shard.shape[1]))), glr


def kernel(x, meta_tokens, norm_gain, w_in, w_gate_up, b_gate, ret_norm_gain, gla_norm_gain, w_branch_ret, w_branch_gla, w_out, final_norm_gain, loss_target, m_meta_tokens, m_norm_gain, m_w_in, m_w_gate_up, m_b_gate, m_ret_norm_gain, m_gla_norm_gain, m_w_branch_ret, m_w_branch_gla, m_w_out, m_final_norm_gain, v_meta_tokens, v_norm_gain, v_w_in, v_w_gate_up, v_b_gate, v_ret_norm_gain, v_gla_norm_gain, v_w_branch_ret, v_w_branch_gla, v_w_out, v_final_norm_gain):
    xi, yi, ci = _position()
    me = _index(xi, yi, ci)
    seq = x.shape[1]
    t_rows = seq + TILE
    in_shard = w_in.shape[2]
    gu_shard = w_gate_up.shape[2]
    meta_shard = meta_tokens.shape[1]
    ret_rows, gla_rows, out_rows = w_branch_ret.shape[1], w_branch_gla.shape[1], w_out.shape[1]

    assert in_shard == IN_SHARD
    slab_local, glr_local = lax.switch(me, [functools.partial(_shard_to_slab, d=d) for d in range(N_DEV)], w_in[0])
    small_local = jnp.concatenate([meta_tokens, jnp.pad(w_gate_up[0], ((0, 0), (0, LANES - gu_shard))),
                                   glr_local.reshape(-1, LANES)], axis=0)
    slabs, g_small = _all_gather([slab_local.astype(BF16), small_local], "all_gather_shards")
    n_small = N_META + GLA_RANK
    w_glr = jnp.pad(g_small[GLR_DEV, n_small:].reshape(D_MODEL, GLA_RANK), ((0, 0), (0, LANES - GLA_RANK))).astype(BF16)
    meta_full = jnp.transpose(g_small[:, :N_META, :], (1, 0, 2)).reshape(N_META, D_MODEL)
    wgu_full = jnp.transpose(g_small[:, N_META:n_small, :gu_shard], (1, 0, 2)).reshape(GLA_RANK, GLA_HEADS * GLA_K)
    wgu_pad = jnp.pad(wgu_full, ((0, LANES - GLA_RANK), (0, 0)))

    rope = _rope_tables(t_rows // TILE)
    lg = jnp.log1p(-(2.0 ** (-5.0 - jnp.arange(RET_HEADS, dtype=F32))))

    head = jnp.concatenate([jnp.zeros((PAD_ROWS, D_MODEL), F32), meta_full], axis=0)
    ut, proj, glr = _inproj_tiles(head, x[0], norm_gain, slabs, w_glr)
    o_ret_raw, o_ret, ret_states, (g_br, g_bg, g_o) = _ret_fwd(
        proj, rope, ret_norm_gain, lg, [w_branch_ret[0].astype(BF16), w_branch_gla[0].astype(BF16), w_out[0].astype(BF16)])
    w_br, w_bg, w_o = g_br.reshape(RET_W, D_MODEL), g_bg.reshape(GLA_W, D_MODEL), g_o.reshape(D_MODEL, D_MODEL)
    masks, cum_fwd, cum_bwd = _gla_tables()
    o_gla_raw, o_gla, gla_states, gla_scores_t = _gla_fwd(proj, glr, wgu_pad, b_gate, gla_norm_gain, masks, cum_fwd)
    (dh1, d_mr, d_mg, do_ret, do_gla, loss_part, d_gfinal, dw_br, dw_bg, dw_o) = _merge_fwd_bwd(
        o_ret, o_gla, proj, x[0], loss_target[0], final_norm_gain.reshape(1, D_MODEL), w_br, w_bg, w_o)

    d_rq, d_rk, d_rv, d_rg, d_gret, du_ret = _ret_bwd(proj, rope, ret_norm_gain, lg, o_ret_raw, do_ret, ret_states, slabs)
    d_gq, d_gk, d_gv, d_gg, dglr_parts, d_wgu, d_bgate, d_ggla = _gla_bwd(
        proj, glr, wgu_pad, b_gate, gla_norm_gain, o_gla_raw, do_gla, gla_states, gla_scores_t, masks, cum_fwd, cum_bwd)
    dseg = dict(rq=d_rq, rk=d_rk, rv=d_rv, rg=d_rg, gq=d_gq, gk=d_gk, gv=d_gv, gg=d_gg, mr=d_mr, mg=d_mg)
    row_sends = [dw_br.reshape(N_DEV, ret_rows, D_MODEL), dw_bg.reshape(N_DEV, gla_rows, D_MODEL),
                 dw_o.reshape(N_DEV, out_rows, D_MODEL)]
    dw_blocks, dw_glr, sib_in, sib_rows = _inproj_bwd_w(ut, dseg, dglr_parts, row_sends)
    core = ci.astype(jnp.int32).reshape(1)
    chip_partials = [_chip_partial_slab(dw_blocks, sib_in, core)] + list(_chip_partial_rows(row_sends, list(sib_rows), core))
    grad_x, d_head, d_gnorm, p_in, p_br, p_bg, p_o = _inproj_bwd_x(
        dseg, du_ret, dglr_parts, head, x[0], dh1, norm_gain, slabs, w_glr, chip_partials)
    small_shapes = [(N_META, D_MODEL), (1, D_MODEL), (GLA_RANK, GLA_HEADS * GLA_K), (1, GLA_HEADS * GLA_K),
                    (1, RET_W), (1, GLA_W), (1, D_MODEL), (1, LANES), (D_MODEL, GLA_RANK)]
    small_part = _pack_rows([d_head[PAD_ROWS:], d_gnorm, d_wgu[:GLA_RANK], d_bgate, d_gret, d_ggla, d_gfinal, loss_part,
                             dw_glr[:, :GLA_RANK]])
    (p_small,) = _all_gather([small_part], "all_gather_small_partials")

    (g_meta_f, g_gnorm, g_wgu_f, g_bgate, g_gret, g_ggla, g_gfinal, loss_all,
     g_wglr) = _unpack_rows(_reduce_small(p_small), small_shapes)
    g_w_in, d_w_in, nm_w_in, nv_w_in = [a.T for a in _reduce_adam_slab(
        p_in, jnp.pad(g_wglr, ((0, 0), (0, LANES - GLA_RANK))), w_in[0].T, m_w_in[0].T, v_w_in[0].T, me)]
    ((g_w_br, d_w_br, nm_w_br, nv_w_br), (g_w_bg, d_w_bg, nm_w_bg, nv_w_bg), (g_w_o, d_w_o, nm_w_o, nv_w_o)) = _reduce_adam_rows(
        [p_br, p_bg, p_o], [w_branch_ret[0], w_branch_gla[0], w_out[0]], [m_w_branch_ret[0], m_w_branch_gla[0], m_w_out[0]],
        [v_w_branch_ret[0], v_w_branch_gla[0], v_w_out[0]])
    g_meta = lax.dynamic_slice_in_dim(g_meta_f, me * meta_shard, meta_shard, axis=1)
    g_wgu = lax.dynamic_slice_in_dim(g_wgu_f, me * gu_shard, gu_shard, axis=1)
    s_g = [g_meta, g_gnorm, g_wgu, g_bgate, g_gret, g_ggla, g_gfinal]
    s_w = [meta_tokens, norm_gain, w_gate_up[0], b_gate, ret_norm_gain, gla_norm_gain, final_norm_gain]
    s_m = [m_meta_tokens, m_norm_gain, m_w_gate_up[0], m_b_gate, m_ret_norm_gain, m_gla_norm_gain, m_final_norm_gain]
    s_v = [v_meta_tokens, v_norm_gain, v_w_gate_up[0], v_b_gate, v_ret_norm_gain, v_gla_norm_gain, v_final_norm_gain]
    shapes = [a.shape for a in s_g]
    s_d, s_nm, s_nv = [_unpack_rows(p, shapes) for p in _adam_small(*[_pack_rows(l) for l in (s_g, s_w, s_m, s_v)])]

    loss = loss_all[0, 0]
    grad_x = grad_x[None]

    def order(meta, gnorm, win, wgu, bgate, gret, ggla, wbr, wbg, wo, gfin):
        return (meta, gnorm, win[None], wgu[None], bgate, gret, ggla, wbr[None], wbg[None], wo[None], gfin.reshape(final_norm_gain.shape))

    def small(l):
        return dict(meta=l[0], gnorm=l[1], wgu=l[2], bgate=l[3], gret=l[4], ggla=l[5], gfin=l[6])

    grads = order(win=g_w_in, wbr=g_w_br, wbg=g_w_bg, wo=g_w_o, **small(s_g))
    deltas = order(win=d_w_in, wbr=d_w_br, wbg=d_w_bg, wo=d_w_o, **small(s_d))
    new_m = order(win=nm_w_in, wbr=nm_w_br, wbg=nm_w_bg, wo=nm_w_o, **small(s_nm))
    new_v = order(win=nv_w_in, wbr=nv_w_br, wbg=nv_w_bg, wo=nv_w_o, **small(s_nv))
    return (loss, grad_x, *grads, *deltas, *new_m, *new_v)
```

```python
import functools

import jax
import jax.numpy as jnp
from jax import lax
from jax.experimental import pallas as pl
from jax.experimental.pallas import tpu as pltpu

F32 = jnp.float32
BF16 = jnp.bfloat16

D_MODEL = 1024
N_META = 16
TILE = 256
PAD_ROWS = TILE - N_META
RET_HEADS = 4
RET_QK = 256
RET_V = 512
RET_W = RET_HEADS * RET_V
GLA_HEADS = 4
GLA_K = 128
GLA_V = 256
GLA_W = GLA_HEADS * GLA_V
GLA_RANK = 16
GLA_TAU = 16.0
GLA_CHUNK = 16
ROPE_BASE = 10000.0
EPS = 1e-6
LANES = 128
N_DEV = 8
SEG_NAMES = ("rq", "rk", "rv", "rg", "gq", "gk", "gv", "gg", "mr", "mg")
SEG_W = (1024, 1024, 2048, 2048, 512, 512, 1024, 1024, 1024, 1024)
SEG_OFF = tuple(sum(SEG_W[:i]) for i in range(len(SEG_W)))
AL_COLS = sum(SEG_W)
IN_COLS = AL_COLS + GLA_RANK
GLR_OFF = sum(SEG_W[:8])
IN_SHARD = IN_COLS // N_DEV


def _aligned_col(c):
    assert c <= GLR_OFF or c >= GLR_OFF + GLA_RANK
    return c if c <= GLR_OFF else c - GLA_RANK


SLAB_BOUND = tuple(_aligned_col(IN_SHARD * d) for d in range(N_DEV + 1))
SLAB_BLK0 = tuple(b // LANES for b in SLAB_BOUND[:-1])
SLAB_SHIFT = tuple(b % LANES for b in SLAB_BOUND[:-1])
SLAB_BLOCKS = max(-(-SLAB_BOUND[d + 1] // LANES) - SLAB_BLK0[d] for d in range(N_DEV))
SLAB_W = SLAB_BLOCKS * LANES
GLR_DEV = GLR_OFF // IN_SHARD
GLR_LOCAL = GLR_OFF - GLR_DEV * IN_SHARD
assert all(SLAB_BLK0[d] + SLAB_BLOCKS <= AL_COLS // LANES for d in range(N_DEV))
VMEM_LIMIT = 58 * 1024 * 1024
ADAM_LR, ADAM_B1, ADAM_B2, ADAM_EPS, ADAM_WD, ADAM_STEP = 0.001, 0.9, 0.999, 1e-08, 0.01, 10
ANY = pl.BlockSpec(memory_space=pl.ANY)
MESH = pl.DeviceIdType.MESH


def _call(body, name, **kw):
    return pl.pallas_call(body, name=name, **kw)


def _params(sem=None):
    return pltpu.CompilerParams(dimension_semantics=sem, vmem_limit_bytes=VMEM_LIMIT)


def _mm(a, b):
    return jnp.dot(a, b, preferred_element_type=F32)


def _mm_nt(a, b):
    return lax.dot_general(a, b, (((1,), (1,)), ((), ())), preferred_element_type=F32)


def _mm_tn(a, b):
    return lax.dot_general(a, b, (((0,), (0,)), ((), ())), preferred_element_type=F32)


def _sigmoid(x):
    return jax.nn.sigmoid(x)


def _rope(t, cos, sin):
    half = t.shape[-1] // 2
    t1, t2 = t[:, :half], t[:, half:]
    return jnp.concatenate([t1 * cos - t2 * sin, t2 * cos + t1 * sin], axis=-1)


def _rope_bwd(g, cos, sin):
    half = g.shape[-1] // 2
    g1, g2 = g[:, :half], g[:, half:]
    return jnp.concatenate([g1 * cos + g2 * sin, g2 * cos - g1 * sin], axis=-1)


def _row_mean(x):
    return jnp.mean(x, axis=-1, keepdims=True)


def _col_sum(x):
    return jnp.sum(x, axis=0, keepdims=True)


def _tile_rows(head_ref, x_ref):
    return jnp.where(pl.program_id(0) == 0, head_ref[...], x_ref[...])


def _head_spec():
    return pl.BlockSpec((TILE, D_MODEL), lambda i: (0, 0))


def _x_spec():
    return pl.BlockSpec((TILE, D_MODEL), lambda i: (jnp.maximum(i - 1, 0), 0))


def _slab_plan():
    interior, shared = [], []
    for d in range(N_DEV):
        lo, hi = -(-SLAB_BOUND[d] // LANES), SLAB_BOUND[d + 1] // LANES
        interior.append((d, LANES * (lo - SLAB_BLK0[d]), LANES * lo, LANES * (hi - lo)))
        if d + 1 < N_DEV and SLAB_BOUND[d + 1] % LANES:
            shared.append((hi, d, hi - SLAB_BLK0[d]))
    return interior, shared


N_BLOCKS = AL_COLS // LANES
HALF_BLOCKS = SLAB_BLOCKS // 2
HALF_W = SLAB_W // 2


def _half_blocks():
    interior, _ = _slab_plan()
    first = [dst // LANES + j for _, src, dst, width in interior for j in range(width // LANES) if src // LANES + j < HALF_BLOCKS]
    return first, [b for b in range(N_BLOCKS) if b not in first]


def _w_scratch(n_blocks=N_BLOCKS):
    return [pltpu.VMEM((D_MODEL, LANES * n_blocks), BF16), pltpu.VMEM((D_MODEL, LANES), BF16),
            pltpu.VMEM((2 * (N_DEV - 1), D_MODEL, LANES), BF16), pltpu.SemaphoreType.DMA((4 * N_DEV,))]


W_SCRATCH = _w_scratch


def _slab_cols(halves, d, lo, n):
    out = []
    for k, half in enumerate(halves):
        a, b = max(lo, k * HALF_W), min(lo + n, (k + 1) * HALF_W)
        if a < b:
            out.append((half.at[d, :, pl.ds(a - k * HALF_W, b - a)], a - lo, b - a))
    return out


def _load_weight(halves, wg_hbm, w_vm, wg_vm, edge_vm, sem, blocks=None):
    blocks = list(range(N_BLOCKS) if blocks is None else blocks)
    place = {b: i for i, b in enumerate(blocks)}
    interior, shared = _slab_plan()
    copies = [] if wg_hbm is None else [(wg_hbm, wg_vm)]
    for d, src, dst, width in interior:
        b0 = dst // LANES
        runs = []
        for b in range(b0, b0 + width // LANES):
            if b in place and runs and b == sum(runs[-1]):
                runs[-1][1] += 1
            elif b in place:
                runs.append([b, 1])
        for b, n in runs:
            for piece, off, w in _slab_cols(halves, d, src + LANES * (b - b0), LANES * n):
                copies.append((piece, w_vm.at[:, pl.ds(LANES * place[b] + off, w)]))
    edges = []
    for blk, d, j in shared:
        if blk in place:
            ((low, _, _),), ((high, _, _),) = _slab_cols(halves, d, LANES * j, LANES), _slab_cols(halves, d + 1, 0, LANES)
            copies += [(low, edge_vm.at[2 * len(edges)]), (high, edge_vm.at[2 * len(edges) + 1])]
            edges.append(blk)
    copies = [pltpu.make_async_copy(a, b, sem.at[i]) for i, (a, b) in enumerate(copies)]
    for cp in copies:
        cp.start()
    for cp in copies:
        cp.wait()
    for n, blk in enumerate(edges):
        w_vm[:, LANES * place[blk]:LANES * (place[blk] + 1)] = edge_vm[2 * n] + edge_vm[2 * n + 1]


def _proj_specs(names, n_units, where):
    specs = []
    for name in names:
        s = SEG_NAMES.index(name)
        nblk = SEG_W[s] // n_units // LANES
        base = SEG_OFF[s] // LANES
        assert base % nblk == 0
        specs.append(pl.BlockSpec((nblk, TILE, LANES), lambda *g, base=base, nblk=nblk: (base // nblk + where(*g)[0], where(*g)[1], 0)))
    return specs


def _cols(ref, unit=0, n_units=1):
    n = ref.shape[0] // n_units
    return ref[unit * n] if n == 1 else jnp.concatenate([ref[unit * n + j] for j in range(n)], axis=1)


def _prenorm(head_ref, x_ref, g_ref):
    x = _tile_rows(head_ref, x_ref)
    r = lax.rsqrt(_row_mean(x * x) + EPS)
    return (x * r * g_ref[...]).astype(BF16).astype(F32)


def _project(u, w_vm, n, store):
    cuts = [8 * i for i in range(max(n // 8, 1))] + [n]
    for lo, hi in zip(cuts[:-1], cuts[1:]):
        res = _mm(u, w_vm[:, LANES * lo:LANES * hi]).astype(BF16)
        for j in range(lo, hi):
            store(j, res[:, LANES * (j - lo):LANES * (j - lo + 1)])


def _inproj_first(head, x, g_norm, first_halves, second_half):
    t_rows = x.shape[0] + TILE
    nt = t_rows // TILE
    first, _ = _half_blocks()
    n = len(first)

    def body(head_ref, x_ref, g_ref, first_hbm, second_hbm, proj_ref, gathered, w_vm, wg_vm, edge_vm, sem, *sems):
        gather = _Exchange([second_hbm], [gathered], sems, among_chips=False)

        @pl.when(pl.program_id(0) == 0)
        def _():
            gather.start()
            _load_weight((first_hbm,), None, w_vm, wg_vm, edge_vm, sem, first)

        @pl.when(pl.program_id(0) == nt - 1)
        def _():
            gather.finish()

        def store(i, block):
            proj_ref[i] = block
        _project(_prenorm(head_ref, x_ref, g_ref).astype(BF16), w_vm, n, store)

    return _call(
        body, "inproj_fwd_first", grid=(nt,),
        out_shape=[jax.ShapeDtypeStruct((n, t_rows, LANES), BF16), jax.ShapeDtypeStruct((N_DEV, *second_half.shape), BF16)],
        in_specs=[_head_spec(), _x_spec(), pl.BlockSpec((1, D_MODEL), lambda i: (0, 0)), ANY, ANY],
        out_specs=[pl.BlockSpec((n, TILE, LANES), lambda i: (0, i, 0)), ANY],
        scratch_shapes=_w_scratch(n) + _exchange_sems(1, N_DEV), compiler_params=_params(("arbitrary",)),
    )(head, x, g_norm, first_halves, second_half)


def _inproj_tiles(head, x, g_norm, halves, w_glr, proj_first):
    t_rows = x.shape[0] + TILE
    nt = t_rows // TILE
    first, rest = _half_blocks()

    def body(head_ref, x_ref, g_ref, first_hbm, second_hbm, wg_hbm, pf_ref, ut_ref, proj_ref, glr_ref, w_vm, wg_vm, edge_vm, sem):
        @pl.when(pl.program_id(0) == 0)
        def _():
            _load_weight((first_hbm, second_hbm), wg_hbm, w_vm, wg_vm, edge_vm, sem, rest)

        u32 = _prenorm(head_ref, x_ref, g_ref)
        u = u32.astype(BF16)
        ut_ref[...] = u32.T.astype(BF16)

        def store(i, block):
            proj_ref[rest[i]] = block
        _project(u, w_vm, len(rest), store)
        for i, b in enumerate(first):
            proj_ref[b] = pf_ref[i]
        glr_ref[...] = _mm(u, wg_vm[...])

    return _call(
        body, "inproj_fwd_tiles", grid=(nt,),
        out_shape=[jax.ShapeDtypeStruct((nt, D_MODEL, TILE), BF16), jax.ShapeDtypeStruct((N_BLOCKS, t_rows, LANES), BF16),
                   jax.ShapeDtypeStruct((t_rows, LANES), F32)],
        in_specs=[_head_spec(), _x_spec(), pl.BlockSpec((1, D_MODEL), lambda i: (0, 0)), ANY, ANY, ANY,
                  pl.BlockSpec((len(first), TILE, LANES), lambda i: (0, i, 0))],
        out_specs=[pl.BlockSpec((None, D_MODEL, TILE), lambda i: (i, 0, 0)), pl.BlockSpec((N_BLOCKS, TILE, LANES), lambda i: (0, i, 0)),
                   pl.BlockSpec((TILE, LANES), lambda i: (i, 0))],
        scratch_shapes=_w_scratch(len(rest)), compiler_params=_params(("arbitrary",)),
    )(head, x, g_norm, *halves, w_glr, proj_first)


def _ret_decay(lgh):
    i = lax.broadcasted_iota(jnp.int32, (TILE, TILE), 0)
    j = lax.broadcasted_iota(jnp.int32, (TILE, TILE), 1)
    rel = (i - j).astype(F32)
    return jnp.where(rel >= 0, jnp.exp(jnp.maximum(rel, 0.0) * lgh), 0.0)


def _ret_vectors(lgh):
    idx = lax.broadcasted_iota(jnp.int32, (TILE, 1), 0).astype(F32)
    xi = jnp.exp((idx + 1.0) * lgh)
    zeta = jnp.exp((TILE - 1.0 - idx) * lgh)
    gc = jnp.exp(jnp.full((1, 1), float(TILE), F32) * lgh)
    return xi, zeta, gc


def _rope_tables(nt):
    half = RET_QK // 2
    inv = ROPE_BASE ** (-jnp.arange(half, dtype=F32) / half)
    base = (jnp.arange(nt, dtype=F32) * TILE - float(PAD_ROWS))[:, None, None] * inv[None, None, :]
    off = jnp.arange(TILE, dtype=F32)[:, None] * inv[None, :]
    return jnp.cos(base), jnp.sin(base), jnp.cos(off), jnp.sin(off)


def _rope_specs(tile_of):
    return [pl.BlockSpec((None, 1, RET_QK // 2), lambda i: (tile_of(i), 0, 0))] * 2 + [pl.BlockSpec((TILE, RET_QK // 2), lambda i: (0, 0))] * 2


def _rope_angles(cb_ref, sb_ref, co_ref, so_ref):
    cb, sb, co, so = cb_ref[...], sb_ref[...], co_ref[...], so_ref[...]
    return cb * co - sb * so, sb * co + cb * so


def _ret_fwd(proj, rope, gain, lg, row_shards):
    t_rows = proj.shape[1]
    nt = t_rows // TILE
    ns = len(row_shards)

    def body(lg_ref, q_ref, k_ref, v_ref, g_ref, cb_ref, sb_ref, co_ref, so_ref, gain_ref, *rest):
        shard_refs, (oraw_ref, oret_ref, st_ref), gathered = rest[:ns], rest[ns:ns + 3], rest[ns + 3:2 * ns + 3]
        s_acc, dm = rest[2 * ns + 3:2 * ns + 5]
        gather = _Exchange(shard_refs, gathered, rest[2 * ns + 5:], among_chips=False)
        t = pl.program_id(0)

        @pl.when(t == 0)
        def _():
            gather.start()
            s_acc[...] = jnp.zeros_like(s_acc)
            for h in range(RET_HEADS):
                dm[h] = _ret_decay(lg_ref[h])

        @pl.when(t == nt - 1)
        def _():
            gather.finish()

        cos_t, sin_t = _rope_angles(cb_ref, sb_ref, co_ref, so_ref)
        for h in range(RET_HEADS):
            lgh = lg_ref[h]
            q = _rope(_cols(q_ref, h, RET_HEADS).astype(F32), cos_t, sin_t)
            k = _rope(_cols(k_ref, h, RET_HEADS).astype(F32), cos_t, sin_t) * (RET_QK ** -0.5)
            xi, zeta, gc = _ret_vectors(lgh)
            v = _cols(v_ref, h, RET_HEADS)
            s_in = s_acc[h]
            p = (_mm_nt(q.astype(BF16), k.astype(BF16)) * dm[h]).astype(BF16)
            o = _mm(p, v) + _mm((q * xi).astype(BF16), s_in.astype(BF16))
            st_ref[h] = s_in.astype(BF16)
            s_acc[h] = s_in * gc + _mm_tn((k * zeta).astype(BF16), v)
            cols = slice(h * RET_V, (h + 1) * RET_V)
            oraw_ref[:, cols] = o
            oc = o - _row_mean(o)
            n = oc * lax.rsqrt(_row_mean(oc * oc) + EPS) * gain_ref[:, cols]
            g = _cols(g_ref, h, RET_HEADS).astype(F32)
            oret_ref[:, cols] = (n * g * _sigmoid(g)).astype(BF16)

    row = lambda w: pl.BlockSpec((TILE, w), lambda t: (t, 0))
    outs = _call(
        body, "ret_fwd", grid=(nt,),
        out_shape=[jax.ShapeDtypeStruct((t_rows, RET_W), F32), jax.ShapeDtypeStruct((t_rows, RET_W), BF16),
                   jax.ShapeDtypeStruct((RET_HEADS, nt, RET_QK, RET_V), BF16)]
                  + [jax.ShapeDtypeStruct((N_DEV, *a.shape), a.dtype) for a in row_shards],
        in_specs=[pl.BlockSpec(memory_space=pltpu.SMEM)] + _proj_specs(("rq", "rk", "rv", "rg"), 1, lambda t: (0, t)) + _rope_specs(lambda t: t) + [
                  pl.BlockSpec((1, RET_W), lambda t: (0, 0))] + [ANY] * ns,
        out_specs=[row(RET_W), row(RET_W), pl.BlockSpec((RET_HEADS, None, RET_QK, RET_V), lambda t: (0, t, 0, 0))] + [ANY] * ns,
        scratch_shapes=[pltpu.VMEM((RET_HEADS, RET_QK, RET_V), F32), pltpu.VMEM((RET_HEADS, TILE, TILE), F32)] + _exchange_sems(ns, N_DEV),
        compiler_params=_params(("arbitrary",)),
    )(lg, proj, proj, proj, proj, *rope, gain, *row_shards)
    return outs[0], outs[1], outs[2], outs[3:]


def _ret_bwd(proj, rope, gain, lg, o_raw, do_ret, states):
    t_rows = proj.shape[1]
    nt = t_rows // TILE

    def body(lg_ref, q_ref, k_ref, v_ref, g_ref, cb_ref, sb_ref, co_ref, so_ref, gain_ref, oraw_ref, do_ref, st_ref,
             dq_ref, dk_ref, dv_ref, dg_ref, dgain_ref, e_acc, dm):
        @pl.when(pl.program_id(0) == 0)
        def _():
            e_acc[...] = jnp.zeros_like(e_acc)
            for h in range(RET_HEADS):
                dm[h] = _ret_decay(lg_ref[h])
            dgain_ref[...] = jnp.zeros_like(dgain_ref)

        cos_t, sin_t = _rope_angles(cb_ref, sb_ref, co_ref, so_ref)
        for h in range(RET_HEADS):
            lgh = lg_ref[h]
            cols = slice(h * RET_V, (h + 1) * RET_V)
            qcols = slice(h * RET_QK, (h + 1) * RET_QK)
            q = _rope(_cols(q_ref, h, RET_HEADS).astype(F32), cos_t, sin_t)
            k = _rope(_cols(k_ref, h, RET_HEADS).astype(F32), cos_t, sin_t) * (RET_QK ** -0.5)
            xi, zeta, gc = _ret_vectors(lgh)
            v = _cols(v_ref, h, RET_HEADS)
            g = _cols(g_ref, h, RET_HEADS).astype(F32)
            o = oraw_ref[:, cols]
            do = do_ref[:, cols].astype(F32)
            oc = o - _row_mean(o)
            rstd = lax.rsqrt(_row_mean(oc * oc) + EPS)
            xh = oc * rstd
            gain_t = gain_ref[:, cols]
            sg = _sigmoid(g)
            dn = do * (g * sg)
            dg_ref[:, cols] = (do * (xh * gain_t) * (sg * (1.0 + g * (1.0 - sg)))).astype(BF16)
            dgain_ref[:, cols] += _col_sum(dn * xh)
            dxh = dn * gain_t
            dob = (rstd * (dxh - _row_mean(dxh) - xh * _row_mean(dxh * xh))).astype(BF16)
            dmat = dm[h]
            qb, kb = q.astype(BF16), k.astype(BF16)
            p = (_mm_nt(qb, kb) * dmat).astype(BF16)
            dp = (_mm_nt(dob, v) * dmat).astype(BF16)
            s_in = st_ref[h]
            e_in = e_acc[h]
            e_b = e_in.astype(BF16)
            dq = _mm(dp, kb) + _mm_nt(dob, s_in) * xi
            dk = _mm_tn(dp, qb) + _mm_nt(v, e_b) * zeta
            dv_ref[:, cols] = (_mm_tn(p, dob) + _mm((k * zeta).astype(BF16), e_b)).astype(BF16)
            e_acc[h] = e_in * gc + _mm_tn((q * xi).astype(BF16), dob)
            dq_ref[:, qcols] = _rope_bwd(dq, cos_t, sin_t).astype(BF16)
            dk_ref[:, qcols] = (_rope_bwd(dk, cos_t, sin_t) * (RET_QK ** -0.5)).astype(BF16)

    row = lambda w: pl.BlockSpec((TILE, w), lambda j: (nt - 1 - j, 0))
    vec = pl.BlockSpec((1, RET_W), lambda j: (0, 0))
    return _call(
        body, "ret_bwd", grid=(nt,),
        out_shape=[jax.ShapeDtypeStruct((t_rows, RET_HEADS * RET_QK), BF16), jax.ShapeDtypeStruct((t_rows, RET_HEADS * RET_QK), BF16),
                   jax.ShapeDtypeStruct((t_rows, RET_W), BF16), jax.ShapeDtypeStruct((t_rows, RET_W), BF16),
                   jax.ShapeDtypeStruct((1, RET_W), F32)],
        in_specs=[pl.BlockSpec(memory_space=pltpu.SMEM)] + _proj_specs(("rq", "rk", "rv", "rg"), 1, lambda j: (0, nt - 1 - j)) + _rope_specs(lambda j: nt - 1 - j) + [vec,
                  row(RET_W), row(RET_W), pl.BlockSpec((RET_HEADS, None, RET_QK, RET_V), lambda j: (0, nt - 1 - j, 0, 0))],
        out_specs=[row(RET_HEADS * RET_QK), row(RET_HEADS * RET_QK), row(RET_W), row(RET_W), vec],
        scratch_shapes=[pltpu.VMEM((RET_HEADS, RET_QK, RET_V), F32), pltpu.VMEM((RET_HEADS, TILE, TILE), F32)],
        compiler_params=_params(("arbitrary",)),
    )(lg, proj, proj, proj, proj, *rope, gain, o_raw, do_ret, states)


GLA_LEVELS = (32, 64, 128, 256)
N_TERMS = 1 + len(GLA_LEVELS)


def _gla_tables():
    p = jnp.arange(TILE)[:, None]
    r = jnp.arange(TILE)[None, :]
    masks = [(p // GLA_CHUNK == r // GLA_CHUNK) & (r <= p)]
    for blk in GLA_LEVELS:
        masks.append((p // blk == r // blk) & (p % blk >= blk // 2) & (r % blk < blk // 2))
    masks = jnp.stack(masks + [m.T for m in masks]).astype(F32)
    cum_fwd = jnp.concatenate([r <= p, masks[0] > 0], axis=0).astype(BF16)
    cum_bwd = jnp.concatenate([r >= p, masks[N_TERMS] > 0], axis=1).astype(BF16)
    return masks, cum_fwd, cum_bwd


def _split3(x):
    hi = x.astype(BF16)
    rest = x - hi.astype(F32)
    mid = rest.astype(BF16)
    lo = (rest - mid.astype(F32)).astype(BF16)
    return jnp.concatenate([hi, mid, lo], axis=1)


def _join3(y):
    w = y.shape[1] // 3
    return (y[:, 2 * w:] + y[:, w:2 * w]) + y[:, :w]


def _gla_decays(glr_ref, wgu_ref, b_ref, cum_ref):
    z = _mm(glr_ref[...].astype(BF16), wgu_ref[...].astype(BF16)) + b_ref[...]
    la = (jnp.minimum(z, 0.0) - jnp.log(1.0 + jnp.exp(-jnp.abs(z)))) / GLA_TAU
    width = la.shape[1]
    hi = la.astype(BF16)
    rest = la - hi.astype(F32)
    mid = rest.astype(BF16)
    lo = (rest - mid.astype(F32)).astype(BF16)
    y = _mm(cum_ref[...], jnp.concatenate([hi, mid, lo], axis=1))
    gb = (y[:, 2 * width:] + y[:, width:2 * width]) + y[:, :width]
    return z, gb[:TILE], gb[TILE:]


def _gla_prep(h, q_ref, k_ref, g_all, b_all, g_scr, ref_scr):
    cols = slice(h * GLA_K, (h + 1) * GLA_K)
    g, b = g_all[:, cols], b_all[:, cols]
    g_scr[h] = g
    factors = [(jnp.exp(b), jnp.exp(-b))]
    for lvl, blk in enumerate(GLA_LEVELS):
        for n in range(TILE // blk):
            ref_scr[h, lvl, n * blk:(n + 1) * blk, :] = jnp.broadcast_to(g_scr[h, pl.ds(n * blk + blk // 2 - 1, 1), :], (blk, GLA_K))
        x = g - ref_scr[h, lvl]
        factors.append((jnp.exp(jnp.minimum(x, 0.0)), jnp.exp(jnp.minimum(-x, 0.0))))
    g_last = g_scr[h, pl.ds(TILE - 1, 1), :]
    q = _cols(q_ref, h, GLA_HEADS).astype(F32) * (GLA_K ** -0.5)
    k = _cols(k_ref, h, GLA_HEADS).astype(F32)
    return q, k, factors, jnp.exp(g), jnp.exp(g_last), jnp.exp(g_last - g)


def _gla_scores(q, k, factors, m_ref):
    a = jnp.zeros((TILE, TILE), F32)
    for l, (fq, fk) in enumerate(factors):
        s = _mm_nt((q * fq).astype(BF16), (k * fk).astype(BF16))
        a = jnp.where(m_ref[l] > 0.0, s, a)
    return a


def _gla_fwd(proj, glr, wgu_pad, b_gate, gain, masks, cum_fwd):
    t_rows = glr.shape[0]
    nt = t_rows // TILE

    def body(q_ref, k_ref, v_ref, g_ref, glr_ref, wgu_ref, b_ref, gain_ref, m_ref, cum_ref, oraw_ref, ogla_ref, st_ref, at_ref,
             s_acc, g_scr, ref_scr):
        @pl.when(pl.program_id(0) == 0)
        def _():
            s_acc[...] = jnp.zeros_like(s_acc)

        _, g_all, b_all = _gla_decays(glr_ref, wgu_ref, b_ref, cum_ref)
        for h in range(GLA_HEADS):
            q, k, factors, e_g, e_last, e_end = _gla_prep(h, q_ref, k_ref, g_all, b_all, g_scr, ref_scr)
            v = _cols(v_ref, h, GLA_HEADS)
            st = s_acc[h]
            st_ref[h] = st
            a = _gla_scores(q, k, factors, m_ref)
            at_ref[h] = a.T.astype(BF16)
            o = _mm(a.astype(BF16), v) + _mm_nt((q * e_g).astype(BF16), st.astype(BF16))
            s_acc[h] = st * e_last + _mm(v.astype(F32).T.astype(BF16), (k * e_end).astype(BF16))
            cols = slice(h * GLA_V, (h + 1) * GLA_V)
            oraw_ref[:, cols] = o
            n = o * lax.rsqrt(_row_mean(o * o) + EPS) * gain_ref[:, cols]
            g = _cols(g_ref, h, GLA_HEADS).astype(F32)
            ogla_ref[:, cols] = (n * g * _sigmoid(g)).astype(BF16)

    row = lambda w: pl.BlockSpec((TILE, w), lambda t: (t, 0))
    whole = lambda *shape: pl.BlockSpec(shape, lambda t: (0,) * len(shape))
    return _call(
        body, "gla_fwd", grid=(nt,),
        out_shape=[jax.ShapeDtypeStruct((t_rows, GLA_W), F32), jax.ShapeDtypeStruct((t_rows, GLA_W), BF16),
                   jax.ShapeDtypeStruct((GLA_HEADS, nt, GLA_V, GLA_K), F32), jax.ShapeDtypeStruct((GLA_HEADS, t_rows, TILE), BF16)],
        in_specs=_proj_specs(("gq", "gk", "gv", "gg"), 1, lambda t: (0, t)) + [row(LANES), whole(LANES, GLA_HEADS * GLA_K),
                  whole(1, GLA_HEADS * GLA_K), whole(1, GLA_W), whole(N_TERMS, TILE, TILE), whole(2 * TILE, TILE)],
        out_specs=[row(GLA_W), row(GLA_W), pl.BlockSpec((GLA_HEADS, None, GLA_V, GLA_K), lambda t: (0, t, 0, 0)),
                   pl.BlockSpec((GLA_HEADS, TILE, TILE), lambda t: (0, t, 0))],
        scratch_shapes=[pltpu.VMEM((GLA_HEADS, GLA_V, GLA_K), F32), pltpu.VMEM((GLA_HEADS, TILE, GLA_K), F32),
                        pltpu.VMEM((GLA_HEADS, len(GLA_LEVELS), TILE, GLA_K), F32)],
        compiler_params=_params(("arbitrary",)),
    )(proj, proj, proj, proj, glr, wgu_pad, b_gate, gain, masks, cum_fwd)


def _gla_bwd(proj, glr, wgu_pad, b_gate, gain, o_raw, do_gla, states, a_t, masks, cum_fwd, cum_bwd):
    t_rows = glr.shape[0]
    nt = t_rows // TILE

    def body(q_ref, k_ref, v_ref, g_ref, glr_ref, wgu_ref, b_ref, gain_ref, m_ref, cum_ref, cumb_ref, oraw_ref, do_ref, st_ref, at_ref,
             dq_ref, dk_ref, dv_ref, dg_ref, dglr_ref, dwgu_ref, dbg_ref, dgain_ref, d_acc, g_scr, ref_scr, dref_scr):
        @pl.when(pl.program_id(0) == 0)
        def _():
            d_acc[...] = jnp.zeros_like(d_acc)
            dwgu_ref[...] = jnp.zeros_like(dwgu_ref)
            dbg_ref[...] = jnp.zeros_like(dbg_ref)
            dgain_ref[...] = jnp.zeros_like(dgain_ref)

        z_all, g_all, b_all = _gla_decays(glr_ref, wgu_ref, b_ref, cum_ref)
        dla_parts = []
        for h in range(GLA_HEADS):
            q, k, factors, e_g, e_last, e_end = _gla_prep(h, q_ref, k_ref, g_all, b_all, g_scr, ref_scr)
            v = _cols(v_ref, h, GLA_HEADS)
            cols = slice(h * GLA_V, (h + 1) * GLA_V)
            kcols = slice(h * GLA_K, (h + 1) * GLA_K)
            o = oraw_ref[:, cols]
            do = do_ref[:, cols].astype(F32)
            g = _cols(g_ref, h, GLA_HEADS).astype(F32)
            rinv = lax.rsqrt(_row_mean(o * o) + EPS)
            nh = o * rinv
            gain_t = gain_ref[:, cols]
            sg = _sigmoid(g)
            dn = do * (g * sg)
            dg_ref[:, cols] = (do * (nh * gain_t) * (sg * (1.0 + g * (1.0 - sg)))).astype(BF16)
            dgain_ref[:, cols] += _col_sum(dn * nh)
            dnh = dn * gain_t
            dor = rinv * (dnh - nh * _row_mean(dnh * nh))
            dob = dor.astype(BF16)
            a_t = at_ref[h]
            da = _mm_nt(dob, v).astype(BF16)
            da_t = _mm_nt(v, dob).astype(BF16)
            st_in = st_ref[h]
            d_out = d_acc[h]
            d_out_b = d_out.astype(BF16)
            qg, kg = q * e_g, k * e_end
            dqg = _mm(dob, st_in.astype(BF16))
            dkg = _mm(v, d_out_b)
            dv_ref[:, cols] = (_mm(a_t, dob) + _mm_nt(kg.astype(BF16), d_out_b)).astype(BF16)
            d_acc[h] = d_out * e_last + _mm(dor.T.astype(BF16), qg.astype(BF16))
            dq = dqg * e_g
            dk = dkg * e_end
            dkg_kg = dkg * kg
            dg_cum = dqg * qg - dkg_kg
            db = None
            for l, (fq, fk) in enumerate(factors):
                qt, kt = q * fq, k * fk
                dqt = _mm(da * m_ref[l], kt.astype(BF16))
                dkt = _mm(da_t * m_ref[N_TERMS + l], qt.astype(BF16))
                dq = dq + dqt * fq
                dk = dk + dkt * fk
                diff = dqt * qt - dkt * kt
                if l == 0:
                    db = diff
                else:
                    dg_cum = dg_cum + diff
                    dref_scr[h, l - 1] = diff
            dq_ref[:, kcols] = (dq * (GLA_K ** -0.5)).astype(BF16)
            dk_ref[:, kcols] = dk.astype(BF16)
            g_scr[h] = dg_cum
            g_scr[h, pl.ds(TILE - 1, 1), :] += e_last * _col_sum(d_out * st_in) + _col_sum(dkg_kg)
            for lvl, blk in enumerate(GLA_LEVELS):
                for n in range(TILE // blk):
                    g_scr[h, pl.ds(n * blk + blk // 2 - 1, 1), :] -= _col_sum(dref_scr[h, lvl, n * blk:(n + 1) * blk, :])
            dla_parts.append(_join3(_mm(cumb_ref[...], jnp.concatenate([_split3(g_scr[h]), _split3(db)], axis=0))))
        dz = jnp.concatenate(dla_parts, axis=1) * (1.0 / GLA_TAU) * _sigmoid(-z_all)
        dzb = dz.astype(BF16)
        wgu_b = wgu_ref[...].astype(BF16)
        for h in range(GLA_HEADS):
            kcols = slice(h * GLA_K, (h + 1) * GLA_K)
            dglr_ref[h] = _mm_nt(dzb[:, kcols], wgu_b[:, kcols]).astype(BF16)
        dwgu_ref[...] += _mm(glr_ref[...].T.astype(BF16), dzb)
        dbg_ref[...] += _col_sum(dz)

    row = lambda w: pl.BlockSpec((TILE, w), lambda j: (nt - 1 - j, 0))
    whole = lambda *shape: pl.BlockSpec(shape, lambda j: (0,) * len(shape))
    return _call(
        body, "gla_bwd", grid=(nt,),
        out_shape=[jax.ShapeDtypeStruct((t_rows, GLA_HEADS * GLA_K), BF16), jax.ShapeDtypeStruct((t_rows, GLA_HEADS * GLA_K), BF16),
                   jax.ShapeDtypeStruct((t_rows, GLA_W), BF16), jax.ShapeDtypeStruct((t_rows, GLA_W), BF16),
                   jax.ShapeDtypeStruct((GLA_HEADS, t_rows, LANES), BF16), jax.ShapeDtypeStruct((LANES, GLA_HEADS * GLA_K), F32),
                   jax.ShapeDtypeStruct((1, GLA_HEADS * GLA_K), F32), jax.ShapeDtypeStruct((1, GLA_W), F32)],
        in_specs=_proj_specs(("gq", "gk", "gv", "gg"), 1, lambda j: (0, nt - 1 - j)) + [row(LANES),
                  whole(LANES, GLA_HEADS * GLA_K), whole(1, GLA_HEADS * GLA_K), whole(1, GLA_W),
                  whole(2 * N_TERMS, TILE, TILE), whole(2 * TILE, TILE), whole(TILE, 2 * TILE), row(GLA_W), row(GLA_W),
                  pl.BlockSpec((GLA_HEADS, None, GLA_V, GLA_K), lambda j: (0, nt - 1 - j, 0, 0)),
                  pl.BlockSpec((GLA_HEADS, TILE, TILE), lambda j: (0, nt - 1 - j, 0))],
        out_specs=[row(GLA_HEADS * GLA_K), row(GLA_HEADS * GLA_K), row(GLA_W), row(GLA_W),
                   pl.BlockSpec((GLA_HEADS, TILE, LANES), lambda j: (0, nt - 1 - j, 0)), whole(LANES, GLA_HEADS * GLA_K),
                   whole(1, GLA_HEADS * GLA_K), whole(1, GLA_W)],
        scratch_shapes=[pltpu.VMEM((GLA_HEADS, GLA_V, GLA_K), F32), pltpu.VMEM((GLA_HEADS, TILE, GLA_K), F32),
                        pltpu.VMEM((GLA_HEADS, len(GLA_LEVELS), TILE, GLA_K), F32),
                        pltpu.VMEM((GLA_HEADS, len(GLA_LEVELS), TILE, GLA_K), F32)],
        compiler_params=_params(("arbitrary",)),
    )(proj, proj, proj, proj, glr, wgu_pad, b_gate, gain, masks.astype(BF16), cum_fwd, cum_bwd, o_raw, do_gla, states, a_t)


def _merge_fwd_bwd(o_ret, o_gla, proj, x, target, g_final, w_br, w_bg, w_out):
    t_rows = x.shape[0] + TILE
    nt = t_rows // TILE

    def body(oret_ref, ogla_ref, mr_ref, mg_ref, h0_ref, tgt_ref, gf_ref, wbr_hbm, wbg_hbm, wout_hbm,
             dh1_ref, dmr_ref, dmg_ref, doret_ref, dogla_ref, loss_ref, dgf_ref, dwbr_hbm, dwbg_hbm, dwout_hbm,
             wbr, wbg, wout, abr, abg, aout, sem):
        i = pl.program_id(0)

        @pl.when(i == 0)
        def _():
            cps = [pltpu.make_async_copy(s, d, sem.at[n]) for n, (s, d) in enumerate(((wbr_hbm, wbr), (wbg_hbm, wbg), (wout_hbm, wout)))]
            for cp in cps:
                cp.start()
            abr[...] = jnp.zeros_like(abr)
            abg[...] = jnp.zeros_like(abg)
            aout[...] = jnp.zeros_like(aout)
            loss_ref[...] = jnp.zeros_like(loss_ref)
            dgf_ref[...] = jnp.zeros_like(dgf_ref)
            for cp in cps:
                cp.wait()
            dh1_ref[...] = jnp.zeros_like(dh1_ref)
            dmr_ref[...] = jnp.zeros_like(dmr_ref)
            dmg_ref[...] = jnp.zeros_like(dmg_ref)
            doret_ref[...] = jnp.zeros_like(doret_ref)
            dogla_ref[...] = jnp.zeros_like(dogla_ref)

        @pl.when(i > 0)
        def _():
            oret, ogla = oret_ref[...], ogla_ref[...]
            br, bg = _mm(oret, wbr[...]), _mm(ogla, wbg[...])
            sr, sg = _sigmoid(_cols(mr_ref).astype(F32)), _sigmoid(_cols(mg_ref).astype(F32))
            mb = (sr * br + sg * bg).astype(BF16)
            h1 = h0_ref[...] + _mm(mb, wout[...])
            r2 = lax.rsqrt(_row_mean(h1 * h1) + EPS)
            hn = h1 * r2
            gf = gf_ref[...]
            diff = hn * gf - tgt_ref[...]
            loss_ref[...] += 0.5 * jnp.sum(_row_mean(diff * diff))
            dy = diff * (1.0 / D_MODEL)
            dgf_ref[...] += _col_sum(dy * hn)
            dyg = dy * gf
            dh1 = r2 * (dyg - hn * _row_mean(dyg * hn))
            dh1_ref[...] = dh1
            dh1b = dh1.astype(BF16)
            dm = _mm_nt(dh1b, wout[...])
            aout[...] += _mm_tn(mb, dh1b)
            dbr = (dm * sr).astype(BF16)
            dbg = (dm * sg).astype(BF16)
            dmr_ref[...] = (dm * br * sr * (1.0 - sr)).astype(BF16)
            dmg_ref[...] = (dm * bg * sg * (1.0 - sg)).astype(BF16)
            doret_ref[...] = _mm_nt(dbr, wbr[...]).astype(BF16)
            dogla_ref[...] = _mm_nt(dbg, wbg[...]).astype(BF16)
            abr[...] += _mm_tn(oret, dbr)
            abg[...] += _mm_tn(ogla, dbg)

        @pl.when(i == nt - 1)
        def _():
            wbr[...] = abr[...].astype(BF16)
            wbg[...] = abg[...].astype(BF16)
            wout[...] = aout[...].astype(BF16)
            pltpu.sync_copy(wbr, dwbr_hbm)
            pltpu.sync_copy(wbg, dwbg_hbm)
            pltpu.sync_copy(wout, dwout_hbm)

    row = lambda w: pl.BlockSpec((TILE, w), lambda i: (i, 0))
    one = lambda w: pl.BlockSpec((1, w), lambda i: (0, 0))
    return _call(
        body, "merge_fwd_bwd", grid=(nt,),
        out_shape=[jax.ShapeDtypeStruct((t_rows, D_MODEL), F32), jax.ShapeDtypeStruct((t_rows, D_MODEL), BF16),
                   jax.ShapeDtypeStruct((t_rows, D_MODEL), BF16), jax.ShapeDtypeStruct((t_rows, RET_W), BF16),
                   jax.ShapeDtypeStruct((t_rows, GLA_W), BF16), jax.ShapeDtypeStruct((1, LANES), F32),
                   jax.ShapeDtypeStruct((1, D_MODEL), F32), jax.ShapeDtypeStruct((RET_W, D_MODEL), BF16),
                   jax.ShapeDtypeStruct((GLA_W, D_MODEL), BF16), jax.ShapeDtypeStruct((D_MODEL, D_MODEL), BF16)],
        in_specs=[row(RET_W), row(GLA_W)] + _proj_specs(("mr", "mg"), 1, lambda i: (0, i)) + [_x_spec(), _x_spec(), one(D_MODEL), ANY, ANY, ANY],
        out_specs=[row(D_MODEL), row(D_MODEL), row(D_MODEL), row(RET_W), row(GLA_W), one(LANES), one(D_MODEL), ANY, ANY, ANY],
        scratch_shapes=[pltpu.VMEM((RET_W, D_MODEL), BF16), pltpu.VMEM((GLA_W, D_MODEL), BF16), pltpu.VMEM((D_MODEL, D_MODEL), BF16),
                        pltpu.VMEM((RET_W, D_MODEL), F32), pltpu.VMEM((GLA_W, D_MODEL), F32), pltpu.VMEM((D_MODEL, D_MODEL), F32),
                        pltpu.SemaphoreType.DMA((3,))],
        compiler_params=_params(("arbitrary",)),
    )(o_ret, o_gla, proj, proj, x, target, g_final, w_br, w_bg, w_out)


def _inproj_bwd_x(dseg, dglr, head, x, dh1, g_norm, slabs, w_glr, chip_partials):
    t_rows = x.shape[0] + TILE
    nt = t_rows // TILE
    ne = len(chip_partials)

    def body(*refs):
        d_refs = refs[:10]
        dglr_ref, head_ref, x_ref, dh1_ref, g_ref, slabs_a, slabs_b, wg_hbm = refs[10:18]
        part_refs = refs[18:18 + ne]
        dx_ref, dhead_ref, dgn_ref = refs[18 + ne:21 + ne]
        landed = refs[21 + ne:21 + 2 * ne]
        w_vm, wg_vm, edge_vm, sem = refs[21 + 2 * ne:25 + 2 * ne]
        exchange = _Exchange(part_refs, landed, refs[25 + 2 * ne:], among_chips=True)

        @pl.when(pl.program_id(0) == 0)
        def _():
            exchange.start()
            dgn_ref[...] = jnp.zeros_like(dgn_ref)
            _load_weight((slabs_a, slabs_b), wg_hbm, w_vm, wg_vm, edge_vm, sem)

        @pl.when(pl.program_id(0) == nt - 1)
        def _():
            exchange.finish()

        dglr = dglr_ref[0].astype(F32)
        for h in range(1, GLA_HEADS):
            dglr = dglr + dglr_ref[h].astype(F32)
        du = _mm_nt(dglr.astype(BF16), wg_vm[...])
        for s, d_ref in enumerate(d_refs):
            du = du + _mm_nt(d_ref[...], w_vm[:, SEG_OFF[s]:SEG_OFF[s] + SEG_W[s]])
        x = _tile_rows(head_ref, x_ref)
        r = lax.rsqrt(_row_mean(x * x) + EPS)
        hn = x * r
        dgn_ref[...] += _col_sum(du * hn)
        dug = du * g_ref[...]
        dh0 = dh1_ref[...] + r * (dug - hn * _row_mean(dug * hn))
        dx_ref[...] = dh0

        @pl.when(pl.program_id(0) == 0)
        def _():
            dhead_ref[...] = dh0

    row = lambda w: pl.BlockSpec((TILE, w), lambda i: (i, 0))
    one = pl.BlockSpec((1, D_MODEL), lambda i: (0, 0))
    return _call(
        body, "inproj_bwd_x", grid=(nt,),
        out_shape=[jax.ShapeDtypeStruct((t_rows - TILE, D_MODEL), F32), jax.ShapeDtypeStruct((TILE, D_MODEL), F32),
                   jax.ShapeDtypeStruct((1, D_MODEL), F32)] + [jax.ShapeDtypeStruct(a.shape, a.dtype) for a in chip_partials],
        in_specs=[row(w) for w in SEG_W] + [pl.BlockSpec((GLA_HEADS, TILE, LANES), lambda i: (0, i, 0)),
                                            _head_spec(), _x_spec(), row(D_MODEL), one, ANY, ANY, ANY] + [ANY] * ne,
        out_specs=[_x_spec(), _head_spec(), one] + [ANY] * ne,
        scratch_shapes=W_SCRATCH() + _exchange_sems(ne, N_CHIP),
        compiler_params=_params(("arbitrary",)),
    )(*[dseg[n] for n in SEG_NAMES], dglr, head, x, dh1, g_norm, *slabs, w_glr, *chip_partials)


W_TILE = 512


def _inproj_bwd_w(ut, dseg, dglr, row_sends):
    nt = ut.shape[0]
    t_rows = nt * TILE
    kc = 3 if nt % 3 == 0 else 1
    tiles = [(s, c) for s in range(len(SEG_W)) for c in range(0, SEG_W[s], W_TILE)]
    bpt = W_TILE // LANES
    nr = len(row_sends)
    n = 1 + nr
    last_tile = [(SLAB_BLK0[d] + SLAB_BLOCKS - 1) // bpt for d in range(N_DEV)]

    def body(ut_hbm, *refs):
        d_refs, dglr_hbm, row_refs = refs[:10], refs[10], refs[11:11 + nr]
        out_hbm, oglr_ref, sib = refs[11 + nr], refs[12 + nr], refs[13 + nr:13 + nr + n]
        ut_vm, dbuf, obuf, acc, gbuf, sem, send_sems, recv_sems = refs[13 + nr + n:]
        x, y, core = _position()

        def handover(d, k, landed=False):
            q = d // 2
            src = out_hbm.at[pl.ds(SLAB_BLK0[d], SLAB_BLOCKS)] if k == 0 else row_refs[k - 1].at[d]
            return pltpu.make_async_remote_copy(src_ref=sib[k].at[q] if landed else src, dst_ref=sib[k].at[q],
                                                send_sem=send_sems.at[n * q + k], recv_sem=recv_sems.at[n * q + k],
                                                device_id=(x, y, 1 - core), device_id_type=MESH)

        def for_sibling(d, ks, fn):
            @pl.when(d % 2 != core)
            def _():
                for k in ks:
                    fn(handover(d, k))

        for d in range(N_DEV):
            for_sibling(d, range(1, n), lambda cp: cp.start())

        def fetch(i):
            s, c = tiles[i]
            return pltpu.make_async_copy(d_refs[s].at[:, pl.ds(c, W_TILE)], dbuf.at[i % 2], sem.at[1 + i % 2])

        def contract(rhs_refs, width):
            acc[:, :width] = jnp.zeros((D_MODEL, width), F32)

            def step(k, carry):
                part = None
                for j in range(kc):
                    kk = k * kc + j
                    for rhs_ref in rhs_refs:
                        prod = _mm(ut_vm[kk], rhs_ref[pl.ds(pl.multiple_of(kk * TILE, TILE), TILE), :])
                        part = prod if part is None else part + prod
                acc[:, :width] += part
                return carry

            lax.fori_loop(0, nt // kc, step, 0)
            return acc[:, :width]

        load_ut = pltpu.make_async_copy(ut_hbm, ut_vm, sem.at[0])
        load_glr = pltpu.make_async_copy(dglr_hbm, gbuf, sem.at[5])
        load_ut.start()
        load_glr.start()
        fetch(0).start()
        load_ut.wait()
        stores = {}

        def stored(i):
            stores[i].wait()
            for d in range(N_DEV):
                if last_tile[d] == i:
                    for_sibling(d, [0], lambda cp: cp.start())

        for i, (s, c) in enumerate(tiles):
            if i + 1 < len(tiles):
                fetch(i + 1).start()
            fetch(i).wait()
            if i >= 2:
                stored(i - 2)
            total = contract([dbuf.at[i % 2]], W_TILE)
            for j in range(bpt):
                obuf[i % 2, j] = total[:, j * LANES:(j + 1) * LANES].astype(BF16)
            blk0 = (SEG_OFF[s] + c) // LANES
            stores[i] = pltpu.make_async_copy(obuf.at[i % 2], out_hbm.at[pl.ds(blk0, bpt)], sem.at[3 + i % 2])
            stores[i].start()
        for i in range(max(0, len(tiles) - 2), len(tiles)):
            stored(i)
        load_glr.wait()
        head_sum = gbuf[0].astype(F32)
        for h in range(1, GLA_HEADS):
            head_sum = head_sum + gbuf[h].astype(F32)
        gbuf[0] = head_sum.astype(BF16)
        oglr_ref[...] = contract([gbuf.at[0]], LANES)
        for q in range(N_CHIP):
            for k in range(n):
                handover(2 * q, k, landed=True).wait_recv()
        for d in range(N_DEV):
            for_sibling(d, range(n), lambda cp: cp.wait_send())

    outs = _call(
        body, "inproj_bwd_w",
        out_shape=[jax.ShapeDtypeStruct((AL_COLS // LANES, D_MODEL, LANES), BF16), jax.ShapeDtypeStruct((D_MODEL, LANES), F32),
                   jax.ShapeDtypeStruct((N_CHIP, SLAB_BLOCKS, D_MODEL, LANES), BF16)]
                  + [jax.ShapeDtypeStruct((N_CHIP, *r.shape[1:]), BF16) for r in row_sends],
        in_specs=[ANY] * (12 + nr), out_specs=[ANY, pl.BlockSpec(memory_space=pltpu.VMEM)] + [ANY] * n,
        scratch_shapes=[pltpu.VMEM((nt, D_MODEL, TILE), BF16), pltpu.VMEM((2, t_rows, W_TILE), BF16),
                        pltpu.VMEM((2, bpt, D_MODEL, LANES), BF16), pltpu.VMEM((D_MODEL, W_TILE), F32),
                        pltpu.VMEM((GLA_HEADS, t_rows, LANES), BF16), pltpu.SemaphoreType.DMA((6,)),
                        pltpu.SemaphoreType.DMA((n * N_CHIP,)), pltpu.SemaphoreType.DMA((n * N_CHIP,))],
        compiler_params=_params(),
    )(ut, *[dseg[n_] for n_ in SEG_NAMES], dglr, *row_sends)
    return outs[0], outs[1], outs[2], outs[3:]


def _position():
    x, y, c = lax.axis_index("x"), lax.axis_index("y"), lax.axis_index("c")
    return x, y, c


def _index(px, py, pc):
    return 4 * px + 2 * py + pc


def _all_gather(arrs, name):
    n = len(arrs)

    def body(*refs):
        ins, outs = refs[:n], refs[n:2 * n]
        send_sems, recv_sems, local_sems = refs[2 * n:]
        x, y, c = _position()
        me, sibling = (x, y, c), (x, y, 1 - c)
        chips = [(1 - x, y), (x, 1 - y), (1 - x, 1 - y)]

        def copy(a, k, block, to, src=None):
            dst = outs[a].at[_index(*block)]
            return pltpu.make_async_remote_copy(src_ref=dst if src is None else src, dst_ref=dst,
                                                send_sem=send_sems.at[7 * a + k], recv_sem=recv_sems.at[7 * a + k],
                                                device_id=to, device_id_type=MESH)

        def relay(a, j):
            return copy(a, 3, (*chips[j], c), (*chips[1 - j], c))

        mine = [pltpu.make_async_copy(ins[a], outs[a].at[_index(*me)], local_sems.at[a]) for a in range(n)]
        for cp in mine:
            cp.start()
        first = []
        for a in range(n):
            first.append(copy(a, 0, me, sibling, src=ins[a]))
            first += [copy(a, 1 + j, me, (*chips[j], c), src=ins[a]) for j in range(2)]
        for cp in first:
            cp.start()
        passed = []
        for j in range(3):
            for a in range(n):
                copy(a, 1 + j, (*chips[j], c), me).wait_recv()
                cp = copy(a, 4 + j, (*chips[j], c), sibling)
                cp.start()
                passed.append(cp)
            if j < 2:
                @pl.when(c == j)
                def _():
                    for a in range(n):
                        relay(a, j).start()
        for a in range(n):
            copy(a, 0, sibling, me).wait_recv()
            for j in range(3):
                copy(a, 4 + j, (*chips[j], 1 - c), me).wait_recv()
        for cp in first + passed:
            cp.wait_send()
        for j in range(2):
            @pl.when(c == j)
            def _():
                for a in range(n):
                    relay(a, j).wait_send()
        for cp in mine:
            cp.wait()

    return _call(
        body, name,
        out_shape=[jax.ShapeDtypeStruct((N_DEV, *a.shape), a.dtype) for a in arrs],
        in_specs=[ANY] * n, out_specs=[ANY] * n,
        scratch_shapes=[pltpu.SemaphoreType.DMA((7 * n,)), pltpu.SemaphoreType.DMA((7 * n,)), pltpu.SemaphoreType.DMA((n,))],
    )(*arrs)


N_CHIP = N_DEV // 2


def _slab_block0(owner):
    step = SLAB_BLK0[1]
    assert all(SLAB_BLK0[d] == step * d - (d == N_DEV - 1) for d in range(N_DEV))
    return step * owner - jnp.where(owner == N_DEV - 1, 1, 0)


def _add_bf16(c_ref, a_ref, b_ref, o_ref):
    o_ref[...] = (a_ref[...].astype(F32) + b_ref[...].astype(F32)).astype(BF16)


def _chip_partial_slab(dw_blocks, sib, core):
    blk = pl.BlockSpec((None, SLAB_BLOCKS, D_MODEL, LANES), lambda q, c_ref: (q, 0, 0, 0))
    return _call(
        functools.partial(_add_bf16), "chip_partial_w_in", out_shape=jax.ShapeDtypeStruct(sib.shape, BF16),
        grid_spec=pltpu.PrefetchScalarGridSpec(
            num_scalar_prefetch=1, grid=(N_CHIP,),
            in_specs=[pl.BlockSpec((pl.Element(SLAB_BLOCKS), pl.Element(D_MODEL), pl.Element(LANES)),
                                   lambda q, c_ref: (_slab_block0(2 * q + c_ref[0]), 0, 0)), blk],
            out_specs=blk),
        compiler_params=_params(("arbitrary",)),
    )(core, dw_blocks, sib)


def _chip_partial_rows(sends, sibs, core):
    n = len(sends)

    def body(c_ref, *refs):
        for k in range(n):
            _add_bf16(c_ref, refs[k], refs[n + k], refs[2 * n + k])

    own = [pl.BlockSpec((None, *a.shape[1:]), lambda q, c_ref: (2 * q + c_ref[0], 0, 0)) for a in sends]
    blk = [pl.BlockSpec((None, *a.shape[1:]), lambda q, c_ref: (q, 0, 0)) for a in sibs]
    return _call(
        body, "chip_partial_rows", out_shape=[jax.ShapeDtypeStruct(a.shape, BF16) for a in sibs],
        grid_spec=pltpu.PrefetchScalarGridSpec(num_scalar_prefetch=1, grid=(N_CHIP,), in_specs=own + blk, out_specs=blk),
        compiler_params=_params(("arbitrary",)),
    )(core, *sends, *sibs)


def _exchange_sems(n_arrays, n_peers):
    return [pltpu.SemaphoreType.DMA((n_arrays * n_peers,)), pltpu.SemaphoreType.DMA((n_arrays * n_peers,)),
            pltpu.SemaphoreType.DMA((n_arrays,))]


class _Exchange:
    def __init__(self, srcs, dsts, sems, among_chips):
        self.arrs = list(zip(srcs, dsts))
        self.n = len(self.arrs)
        self.send_sems, self.recv_sems, self.local_sems = sems
        self.among_chips = among_chips
        x, y, c = _position()
        self.c = c
        self.me = 2 * x + y if among_chips else _index(x, y, c)
        self.n_peers = N_CHIP if among_chips else N_DEV

    def _device(self, p):
        return (p // 2, p % 2, self.c) if self.among_chips else (p // 4, (p // 2) % 2, p % 2)

    def _src(self, k, p):
        src = self.arrs[k][0]
        return src.at[p] if self.among_chips else src

    def _mine(self):
        return [pltpu.make_async_copy(self._src(k, self.me), self.arrs[k][1].at[self.me], self.local_sems.at[k]) for k in range(self.n)]

    def _copy(self, p, k, landing):
        return pltpu.make_async_remote_copy(
            src_ref=self._src(k, p), dst_ref=self.arrs[k][1].at[landing], send_sem=self.send_sems.at[self.n * p + k],
            recv_sem=self.recv_sems.at[self.n * landing + k], device_id=self._device(p), device_id_type=MESH)

    def _others(self, fn):
        for p in range(self.n_peers):
            @pl.when(p != self.me)
            def _():
                for k in range(self.n):
                    fn(p, k)

    def start(self):
        for cp in self._mine():
            cp.start()
        self._others(lambda p, k: self._copy(p, k, self.me).start())

    def finish(self):
        self._others(lambda p, k: self._copy(p, k, p).wait_recv())
        self._others(lambda p, k: self._copy(p, k, self.me).wait_send())
        for cp in self._mine():
            cp.wait()


def _adamw(g, w, m, v):
    m_new = ADAM_B1 * m + (1.0 - ADAM_B1) * g
    v_new = ADAM_B2 * v + (1.0 - ADAM_B2) * (g * g)
    m_hat = m_new / (1.0 - ADAM_B1 ** ADAM_STEP)
    v_hat = v_new / (1.0 - ADAM_B2 ** ADAM_STEP)
    delta = -ADAM_LR * (m_hat / (jnp.sqrt(v_hat) + ADAM_EPS) + ADAM_WD * w)
    return delta, m_new, v_new


def _sum_partials(p_ref):
    g = p_ref[0].astype(F32)
    for d in range(1, p_ref.shape[0]):
        g = g + p_ref[d].astype(F32)
    return g


def _reduce_adam_rows(parts, ws, ms, vs):
    n = len(ws)

    def body(*refs):
        ins, outs = refs[:4 * n], refs[4 * n:]
        for k in range(n):
            p_ref, w_ref, m_ref, v_ref = ins[k], ins[n + k], ins[2 * n + k], ins[3 * n + k]
            g = _sum_partials(p_ref)
            outs[4 * k][...] = g
            outs[4 * k + 1][...], outs[4 * k + 2][...], outs[4 * k + 3][...] = _adamw(g, w_ref[...], m_ref[...], v_ref[...])

    outs = _call(
        body, "adam_row_weights", out_shape=[jax.ShapeDtypeStruct(w.shape, F32) for w in ws for _ in range(4)],
        compiler_params=_params(),
    )(*parts, *ws, *ms, *vs)
    return [tuple(outs[4 * k:4 * k + 4]) for k in range(n)]


def _reduce_adam_slab(parts, glr, w_t, m_t, v_t, me):
    cols, rows = w_t.shape
    shift = jnp.asarray(SLAB_SHIFT, jnp.int32)[me]
    glr_at = jnp.where(me == GLR_DEV, GLR_LOCAL, cols).astype(jnp.int32)

    def body(s_ref, p_ref, glr_ref, w_ref, m_ref, v_ref, g_ref, d_ref, mo_ref, vo_ref, slab_t):
        shift, glr_at = s_ref[0], s_ref[1]
        tall = jnp.concatenate([_sum_partials(p_ref.at[:, j]).T for j in range(SLAB_BLOCKS)], axis=0)
        before = pltpu.roll(tall, SLAB_W - shift, 0)
        after = pltpu.roll(tall, lax.rem(SLAB_W - shift + GLA_RANK, SLAB_W), 0)
        wide = jnp.concatenate([glr_ref[...].T, jnp.zeros((SLAB_W - LANES, LANES), F32)], axis=0)
        placed = pltpu.roll(wide, lax.rem(glr_at, SLAB_W), 0)
        row = lax.broadcasted_iota(jnp.int32, (SLAB_W, LANES), 0)
        slab_t[...] = jnp.where(row < glr_at, before, jnp.where(row < glr_at + GLA_RANK, placed, after))
        g = slab_t[pl.ds(0, cols), :]
        g_ref[...] = g
        d_ref[...], mo_ref[...], vo_ref[...] = _adamw(g, w_ref[...], m_ref[...], v_ref[...])

    blk = pl.BlockSpec((cols, LANES), lambda i, s: (0, i))
    return _call(
        body, "adam_w_in", out_shape=[jax.ShapeDtypeStruct((cols, rows), F32)] * 4,
        grid_spec=pltpu.PrefetchScalarGridSpec(
            num_scalar_prefetch=1, grid=(rows // LANES,),
            in_specs=[pl.BlockSpec((parts.shape[0], SLAB_BLOCKS, LANES, LANES), lambda i, s: (0, 0, i, 0)),
                      pl.BlockSpec((LANES, LANES), lambda i, s: (i, 0)), blk, blk, blk],
            out_specs=[blk] * 4, scratch_shapes=[pltpu.VMEM((SLAB_W, LANES), F32)]),
        compiler_params=_params(("arbitrary",)),
    )(jnp.stack([shift, glr_at]), parts, glr, w_t, m_t, v_t)


def _reduce_small(parts):
    def body(p_ref, o_ref):
        o_ref[...] = _sum_partials(p_ref)

    return _call(body, "reduce_small", out_shape=jax.ShapeDtypeStruct(parts.shape[1:], F32))(parts)


def _adam_small(g, w, m, v):
    def body(g_ref, w_ref, m_ref, v_ref, d_ref, mo_ref, vo_ref):
        d_ref[...], mo_ref[...], vo_ref[...] = _adamw(g_ref[...], w_ref[...], m_ref[...], v_ref[...])

    return _call(body, "adam_small", out_shape=[jax.ShapeDtypeStruct(g.shape, F32)] * 3)(g, w, m, v)


def _pack_rows(arrs):
    rows = []
    for a in arrs:
        flat = a.reshape(-1).astype(F32)
        pad = (-flat.shape[0]) % LANES
        rows.append(jnp.pad(flat, (0, pad)).reshape(-1, LANES))
    packed = jnp.concatenate(rows, axis=0)
    return jnp.pad(packed, ((0, (-packed.shape[0]) % 8), (0, 0)))


def _unpack_rows(packed, shapes):
    out, r = [], 0
    for shp in shapes:
        size = 1
        for s in shp:
            size *= s
        nrows = -(-size // LANES)
        out.append(packed[r:r + nrows].reshape(-1)[:size].reshape(shp))
        r += nrows
    return out


def _shard_to_slab(shard, d):
    glr = jnp.zeros((D_MODEL, GLA_RANK), shard.dtype)
    if d == GLR_DEV:
        glr = shard[:, GLR_LOCAL:GLR_LOCAL + GLA_RANK]
        shard = jnp.concatenate([shard[:, :GLR_LOCAL], shard[:, GLR_LOCAL + GLA_RANK:]], axis=1)
    return jnp.pad(shard, ((0, 0), (SLAB_SHIFT[d], SLAB_W - SLAB_SHIFT[d] - shard.shape[1]))), glr


def kernel(x, meta_tokens, norm_gain, w_in, w_gate_up, b_gate, ret_norm_gain, gla_norm_gain, w_branch_ret, w_branch_gla, w_out, final_norm_gain, loss_target, m_meta_tokens, m_norm_gain, m_w_in, m_w_gate_up, m_b_gate, m_ret_norm_gain, m_gla_norm_gain, m_w_branch_ret, m_w_branch_gla, m_w_out, m_final_norm_gain, v_meta_tokens, v_norm_gain, v_w_in, v_w_gate_up, v_b_gate, v_ret_norm_gain, v_gla_norm_gain, v_w_branch_ret, v_w_branch_gla, v_w_out, v_final_norm_gain):
    xi, yi, ci = _position()
    me = _index(xi, yi, ci)
    seq = x.shape[1]
    t_rows = seq + TILE
    in_shard = w_in.shape[2]
    gu_shard = w_gate_up.shape[2]
    meta_shard = meta_tokens.shape[1]
    ret_rows, gla_rows, out_rows = w_branch_ret.shape[1], w_branch_gla.shape[1], w_out.shape[1]

    assert in_shard == IN_SHARD
    slab_local, glr_local = lax.switch(me, [functools.partial(_shard_to_slab, d=d) for d in range(N_DEV)], w_in[0])
    small_local = jnp.concatenate([meta_tokens, jnp.pad(w_gate_up[0], ((0, 0), (0, LANES - gu_shard))),
                                   glr_local.reshape(-1, LANES)], axis=0)
    slab_local = slab_local.astype(BF16)
    first_halves, g_small = _all_gather([slab_local[:, :HALF_W], small_local], "all_gather_shards")
    n_small = N_META + GLA_RANK
    w_glr = jnp.pad(g_small[GLR_DEV, n_small:].reshape(D_MODEL, GLA_RANK), ((0, 0), (0, LANES - GLA_RANK))).astype(BF16)
    meta_full = jnp.transpose(g_small[:, :N_META, :], (1, 0, 2)).reshape(N_META, D_MODEL)
    wgu_full = jnp.transpose(g_small[:, N_META:n_small, :gu_shard], (1, 0, 2)).reshape(GLA_RANK, GLA_HEADS * GLA_K)
    wgu_pad = jnp.pad(wgu_full, ((0, LANES - GLA_RANK), (0, 0)))

    rope = _rope_tables(t_rows // TILE)
    lg = jnp.log1p(-(2.0 ** (-5.0 - jnp.arange(RET_HEADS, dtype=F32))))

    head = jnp.concatenate([jnp.zeros((PAD_ROWS, D_MODEL), F32), meta_full], axis=0)
    proj_first, second_halves = _inproj_first(head, x[0], norm_gain, first_halves, slab_local[:, HALF_W:])
    slabs = (first_halves, second_halves)
    ut, proj, glr = _inproj_tiles(head, x[0], norm_gain, slabs, w_glr, proj_first)
    o_ret_raw, o_ret, ret_states, (g_br, g_bg, g_o) = _ret_fwd(
        proj, rope, ret_norm_gain, lg, [w_branch_ret[0].astype(BF16), w_branch_gla[0].astype(BF16), w_out[0].astype(BF16)])
    w_br, w_bg, w_o = g_br.reshape(RET_W, D_MODEL), g_bg.reshape(GLA_W, D_MODEL), g_o.reshape(D_MODEL, D_MODEL)
    masks, cum_fwd, cum_bwd = _gla_tables()
    o_gla_raw, o_gla, gla_states, gla_scores_t = _gla_fwd(proj, glr, wgu_pad, b_gate, gla_norm_gain, masks, cum_fwd)
    (dh1, d_mr, d_mg, do_ret, do_gla, loss_part, d_gfinal, dw_br, dw_bg, dw_o) = _merge_fwd_bwd(
        o_ret, o_gla, proj, x[0], loss_target[0], final_norm_gain.reshape(1, D_MODEL), w_br, w_bg, w_o)

    d_rq, d_rk, d_rv, d_rg, d_gret = _ret_bwd(proj, rope, ret_norm_gain, lg, o_ret_raw, do_ret, ret_states)
    d_gq, d_gk, d_gv, d_gg, dglr_parts, d_wgu, d_bgate, d_ggla = _gla_bwd(
        proj, glr, wgu_pad, b_gate, gla_norm_gain, o_gla_raw, do_gla, gla_states, gla_scores_t, masks, cum_fwd, cum_bwd)
    dseg = dict(rq=d_rq, rk=d_rk, rv=d_rv, rg=d_rg, gq=d_gq, gk=d_gk, gv=d_gv, gg=d_gg, mr=d_mr, mg=d_mg)
    row_sends = [dw_br.reshape(N_DEV, ret_rows, D_MODEL), dw_bg.reshape(N_DEV, gla_rows, D_MODEL),
                 dw_o.reshape(N_DEV, out_rows, D_MODEL)]
    dw_blocks, dw_glr, sib_in, sib_rows = _inproj_bwd_w(ut, dseg, dglr_parts, row_sends)
    core = ci.astype(jnp.int32).reshape(1)
    chip_partials = [_chip_partial_slab(dw_blocks, sib_in, core)] + list(_chip_partial_rows(row_sends, list(sib_rows), core))
    grad_x, d_head, d_gnorm, p_in, p_br, p_bg, p_o = _inproj_bwd_x(
        dseg, dglr_parts, head, x[0], dh1, norm_gain, slabs, w_glr, chip_partials)
    small_shapes = [(N_META, D_MODEL), (1, D_MODEL), (GLA_RANK, GLA_HEADS * GLA_K), (1, GLA_HEADS * GLA_K),
                    (1, RET_W), (1, GLA_W), (1, D_MODEL), (1, LANES), (D_MODEL, GLA_RANK)]
    small_part = _pack_rows([d_head[PAD_ROWS:], d_gnorm, d_wgu[:GLA_RANK], d_bgate, d_gret, d_ggla, d_gfinal, loss_part,
                             dw_glr[:, :GLA_RANK]])
    (p_small,) = _all_gather([small_part], "all_gather_small_partials")

    (g_meta_f, g_gnorm, g_wgu_f, g_bgate, g_gret, g_ggla, g_gfinal, loss_all,
     g_wglr) = _unpack_rows(_reduce_small(p_small), small_shapes)
    g_w_in, d_w_in, nm_w_in, nv_w_in = [a.T for a in _reduce_adam_slab(
        p_in, jnp.pad(g_wglr, ((0, 0), (0, LANES - GLA_RANK))), w_in[0].T, m_w_in[0].T, v_w_in[0].T, me)]
    ((g_w_br, d_w_br, nm_w_br, nv_w_br), (g_w_bg, d_w_bg, nm_w_bg, nv_w_bg), (g_w_o, d_w_o, nm_w_o, nv_w_o)) = _reduce_adam_rows(
        [p_br, p_bg, p_o], [w_branch_ret[0], w_branch_gla[0], w_out[0]], [m_w_branch_ret[0], m_w_branch_gla[0], m_w_out[0]],
        [v_w_branch_ret[0], v_w_branch_gla[0], v_w_out[0]])
    g_meta = lax.dynamic_slice_in_dim(g_meta_f, me * meta_shard, meta_shard, axis=1)
    g_wgu = lax.dynamic_slice_in_dim(g_wgu_f, me * gu_shard, gu_shard, axis=1)
    s_g = [g_meta, g_gnorm, g_wgu, g_bgate, g_gret, g_ggla, g_gfinal]
    s_w = [meta_tokens, norm_gain, w_gate_up[0], b_gate, ret_norm_gain, gla_norm_gain, final_norm_gain]
    s_m = [m_meta_tokens, m_norm_gain, m_w_gate_up[0], m_b_gate, m_ret_norm_gain, m_gla_norm_gain, m_final_norm_gain]
    s_v = [v_meta_tokens, v_norm_gain, v_w_gate_up[0], v_b_gate, v_ret_norm_gain, v_gla_norm_gain, v_final_norm_gain]
    shapes = [a.shape for a in s_g]
    s_d, s_nm, s_nv = [_unpack_rows(p, shapes) for p in _adam_small(*[_pack_rows(l) for l in (s_g, s_w, s_m, s_v)])]

    loss = loss_all[0, 0]
    grad_x = grad_x[None]

    def order(meta, gnorm, win, wgu, bgate, gret, ggla, wbr, wbg, wo, gfin):
        return (meta, gnorm, win[None], wgu[None], bgate, gret, ggla, wbr[None], wbg[None], wo[None], gfin.reshape(final_norm_gain.shape))

    def small(l):
        return dict(meta=l[0], gnorm=l[1], wgu=l[2], bgate=l[3], gret=l[4], ggla=l[5], gfin=l[6])

    grads = order(win=g_w_in, wbr=g_w_br, wbg=g_w_bg, wo=g_w_o, **small(s_g))
    deltas = order(win=d_w_in, wbr=d_w_br, wbg=d_w_bg, wo=d_w_o, **small(s_d))
    new_m = order(win=nm_w_in, wbr=nm_w_br, wbg=nm_w_bg, wo=nm_w_o, **small(s_nm))
    new_v = order(win=nv_w_in, wbr=nv_w_br, wbg=nv_w_bg, wo=nv_w_o, **small(s_nv))
    return (loss, grad_x, *grads, *deltas, *new_m, *new_v)
```

```python
import functools

import jax
import jax.numpy as jnp
from jax import lax
from jax.experimental import pallas as pl
from jax.experimental.pallas import tpu as pltpu

F32 = jnp.float32
BF16 = jnp.bfloat16

D_MODEL = 1024
N_META = 16
TILE = 256
PAD_ROWS = TILE - N_META
RET_HEADS = 4
RET_QK = 256
RET_V = 512
RET_W = RET_HEADS * RET_V
GLA_HEADS = 4
GLA_K = 128
GLA_V = 256
GLA_W = GLA_HEADS * GLA_V
GLA_RANK = 16
GLA_TAU = 16.0
GLA_CHUNK = 16
ROPE_BASE = 10000.0
EPS = 1e-6
LANES = 128
N_DEV = 8
SEG_NAMES = ("rq", "rk", "rv", "rg", "gq", "gk", "gv", "gg", "mr", "mg")
SEG_W = (1024, 1024, 2048, 2048, 512, 512, 1024, 1024, 1024, 1024)
SEG_OFF = tuple(sum(SEG_W[:i]) for i in range(len(SEG_W)))
AL_COLS = sum(SEG_W)
IN_COLS = AL_COLS + GLA_RANK
GLR_OFF = sum(SEG_W[:8])
IN_SHARD = IN_COLS // N_DEV


def _aligned_col(c):
    assert c <= GLR_OFF or c >= GLR_OFF + GLA_RANK
    return c if c <= GLR_OFF else c - GLA_RANK


SLAB_BOUND = tuple(_aligned_col(IN_SHARD * d) for d in range(N_DEV + 1))
SLAB_BLK0 = tuple(b // LANES for b in SLAB_BOUND[:-1])
SLAB_SHIFT = tuple(b % LANES for b in SLAB_BOUND[:-1])
SLAB_BLOCKS = max(-(-SLAB_BOUND[d + 1] // LANES) - SLAB_BLK0[d] for d in range(N_DEV))
SLAB_W = SLAB_BLOCKS * LANES
GLR_DEV = GLR_OFF // IN_SHARD
GLR_LOCAL = GLR_OFF - GLR_DEV * IN_SHARD
assert all(SLAB_BLK0[d] + SLAB_BLOCKS <= AL_COLS // LANES for d in range(N_DEV))
VMEM_LIMIT = 58 * 1024 * 1024
ADAM_LR, ADAM_B1, ADAM_B2, ADAM_EPS, ADAM_WD, ADAM_STEP = 0.001, 0.9, 0.999, 1e-08, 0.01, 10
ANY = pl.BlockSpec(memory_space=pl.ANY)
MESH = pl.DeviceIdType.MESH


def _call(body, name, **kw):
    return pl.pallas_call(body, name=name, **kw)


def _params(sem=None):
    return pltpu.CompilerParams(dimension_semantics=sem, vmem_limit_bytes=VMEM_LIMIT)


def _mm(a, b):
    return jnp.dot(a, b, preferred_element_type=F32)


def _mm_nt(a, b):
    return lax.dot_general(a, b, (((1,), (1,)), ((), ())), preferred_element_type=F32)


def _mm_tn(a, b):
    return lax.dot_general(a, b, (((0,), (0,)), ((), ())), preferred_element_type=F32)


def _sigmoid(x):
    return jax.nn.sigmoid(x)


def _rope(t, cos, sin):
    half = t.shape[-1] // 2
    t1, t2 = t[:, :half], t[:, half:]
    return jnp.concatenate([t1 * cos - t2 * sin, t2 * cos + t1 * sin], axis=-1)


def _rope_bwd(g, cos, sin):
    half = g.shape[-1] // 2
    g1, g2 = g[:, :half], g[:, half:]
    return jnp.concatenate([g1 * cos + g2 * sin, g2 * cos - g1 * sin], axis=-1)


def _row_mean(x):
    return jnp.mean(x, axis=-1, keepdims=True)


def _col_sum(x):
    return jnp.sum(x, axis=0, keepdims=True)


def _tile_rows(head_ref, x_ref):
    return jnp.where(pl.program_id(0) == 0, head_ref[...], x_ref[...])


def _head_spec():
    return pl.BlockSpec((TILE, D_MODEL), lambda i: (0, 0))


def _x_spec():
    return pl.BlockSpec((TILE, D_MODEL), lambda i: (jnp.maximum(i - 1, 0), 0))


def _slab_plan():
    interior, shared = [], []
    for d in range(N_DEV):
        lo, hi = -(-SLAB_BOUND[d] // LANES), SLAB_BOUND[d + 1] // LANES
        interior.append((d, LANES * (lo - SLAB_BLK0[d]), LANES * lo, LANES * (hi - lo)))
        if d + 1 < N_DEV and SLAB_BOUND[d + 1] % LANES:
            shared.append((hi, d, hi - SLAB_BLK0[d]))
    return interior, shared


N_BLOCKS = AL_COLS // LANES
HALF_BLOCKS = SLAB_BLOCKS // 2
HALF_W = SLAB_W // 2


def _half_blocks():
    interior, _ = _slab_plan()
    first = [dst // LANES + j for _, src, dst, width in interior for j in range(width // LANES) if src // LANES + j < HALF_BLOCKS]
    return first, [b for b in range(N_BLOCKS) if b not in first]


def _w_scratch(n_blocks=N_BLOCKS):
    return [pltpu.VMEM((D_MODEL, LANES * n_blocks), BF16), pltpu.VMEM((D_MODEL, LANES), BF16),
            pltpu.VMEM((2 * (N_DEV - 1), D_MODEL, LANES), BF16), pltpu.SemaphoreType.DMA((4 * N_DEV,))]


W_SCRATCH = _w_scratch


def _slab_cols(halves, d, lo, n):
    out = []
    for k, half in enumerate(halves):
        a, b = max(lo, k * HALF_W), min(lo + n, (k + 1) * HALF_W)
        if a < b:
            out.append((half.at[d, :, pl.ds(a - k * HALF_W, b - a)], a - lo, b - a))
    return out


def _load_weight(halves, wg_hbm, w_vm, wg_vm, edge_vm, sem, blocks=None):
    blocks = list(range(N_BLOCKS) if blocks is None else blocks)
    place = {b: i for i, b in enumerate(blocks)}
    interior, shared = _slab_plan()
    copies = [] if wg_hbm is None else [(wg_hbm, wg_vm)]
    for d, src, dst, width in interior:
        b0 = dst // LANES
        runs = []
        for b in range(b0, b0 + width // LANES):
            if b in place and runs and b == sum(runs[-1]):
                runs[-1][1] += 1
            elif b in place:
                runs.append([b, 1])
        for b, n in runs:
            for piece, off, w in _slab_cols(halves, d, src + LANES * (b - b0), LANES * n):
                copies.append((piece, w_vm.at[:, pl.ds(LANES * place[b] + off, w)]))
    edges = []
    for blk, d, j in shared:
        if blk in place:
            ((low, _, _),), ((high, _, _),) = _slab_cols(halves, d, LANES * j, LANES), _slab_cols(halves, d + 1, 0, LANES)
            copies += [(low, edge_vm.at[2 * len(edges)]), (high, edge_vm.at[2 * len(edges) + 1])]
            edges.append(blk)
    copies = [pltpu.make_async_copy(a, b, sem.at[i]) for i, (a, b) in enumerate(copies)]
    for cp in copies:
        cp.start()
    for cp in copies:
        cp.wait()
    for n, blk in enumerate(edges):
        w_vm[:, LANES * place[blk]:LANES * (place[blk] + 1)] = edge_vm[2 * n] + edge_vm[2 * n + 1]


def _proj_specs(names, n_units, where):
    specs = []
    for name in names:
        s = SEG_NAMES.index(name)
        nblk = SEG_W[s] // n_units // LANES
        base = SEG_OFF[s] // LANES
        assert base % nblk == 0
        specs.append(pl.BlockSpec((nblk, TILE, LANES), lambda *g, base=base, nblk=nblk: (base // nblk + where(*g)[0], where(*g)[1], 0)))
    return specs


def _cols(ref, unit=0, n_units=1):
    n = ref.shape[0] // n_units
    return ref[unit * n] if n == 1 else jnp.concatenate([ref[unit * n + j] for j in range(n)], axis=1)


def _prenorm(head_ref, x_ref, g_ref):
    x = _tile_rows(head_ref, x_ref)
    r = lax.rsqrt(_row_mean(x * x) + EPS)
    return (x * r * g_ref[...]).astype(BF16).astype(F32)


def _project(u, w_vm, n, store):
    cuts = [8 * i for i in range(max(n // 8, 1))] + [n]
    for lo, hi in zip(cuts[:-1], cuts[1:]):
        res = _mm(u, w_vm[:, LANES * lo:LANES * hi]).astype(BF16)
        for j in range(lo, hi):
            store(j, res[:, LANES * (j - lo):LANES * (j - lo + 1)])


def _inproj_first(head, x, g_norm, first_halves, second_half):
    t_rows = x.shape[0] + TILE
    nt = t_rows // TILE
    first, _ = _half_blocks()
    n = len(first)

    def body(head_ref, x_ref, g_ref, first_hbm, second_hbm, proj_ref, gathered, w_vm, wg_vm, edge_vm, sem, *sems):
        gather = _RelayGather([second_hbm], [gathered], sems)
        for step, stages in ((0, (0,)), (nt // 3, (1, 2)), (2 * nt // 3, (3,)), (nt - 1, (4,))):
            @pl.when(pl.program_id(0) == step)
            def _():
                for s in stages:
                    gather.stage(s)

        @pl.when(pl.program_id(0) == 0)
        def _():
            _load_weight((first_hbm,), None, w_vm, wg_vm, edge_vm, sem, first)

        def store(i, block):
            proj_ref[i] = block
        _project(_prenorm(head_ref, x_ref, g_ref).astype(BF16), w_vm, n, store)

    return _call(
        body, "inproj_fwd_first", grid=(nt,),
        out_shape=[jax.ShapeDtypeStruct((n, t_rows, LANES), BF16), jax.ShapeDtypeStruct((N_DEV, *second_half.shape), BF16)],
        in_specs=[_head_spec(), _x_spec(), pl.BlockSpec((1, D_MODEL), lambda i: (0, 0)), ANY, ANY],
        out_specs=[pl.BlockSpec((n, TILE, LANES), lambda i: (0, i, 0)), ANY],
        scratch_shapes=_w_scratch(n) + _gather_sems(1), compiler_params=_params(("arbitrary",)),
    )(head, x, g_norm, first_halves, second_half)


def _inproj_tiles(head, x, g_norm, halves, w_glr, proj_first):
    t_rows = x.shape[0] + TILE
    nt = t_rows // TILE
    first, rest = _half_blocks()

    def body(head_ref, x_ref, g_ref, first_hbm, second_hbm, wg_hbm, pf_ref, ut_ref, proj_ref, glr_ref, w_vm, wg_vm, edge_vm, sem):
        @pl.when(pl.program_id(0) == 0)
        def _():
            _load_weight((first_hbm, second_hbm), wg_hbm, w_vm, wg_vm, edge_vm, sem, rest)

        u32 = _prenorm(head_ref, x_ref, g_ref)
        u = u32.astype(BF16)
        ut_ref[...] = u32.T.astype(BF16)

        def store(i, block):
            proj_ref[rest[i]] = block
        _project(u, w_vm, len(rest), store)
        for i, b in enumerate(first):
            proj_ref[b] = pf_ref[i]
        glr_ref[...] = _mm(u, wg_vm[...])

    return _call(
        body, "inproj_fwd_tiles", grid=(nt,),
        out_shape=[jax.ShapeDtypeStruct((nt, D_MODEL, TILE), BF16), jax.ShapeDtypeStruct((N_BLOCKS, t_rows, LANES), BF16),
                   jax.ShapeDtypeStruct((t_rows, LANES), F32)],
        in_specs=[_head_spec(), _x_spec(), pl.BlockSpec((1, D_MODEL), lambda i: (0, 0)), ANY, ANY, ANY,
                  pl.BlockSpec((len(first), TILE, LANES), lambda i: (0, i, 0))],
        out_specs=[pl.BlockSpec((None, D_MODEL, TILE), lambda i: (i, 0, 0)), pl.BlockSpec((N_BLOCKS, TILE, LANES), lambda i: (0, i, 0)),
                   pl.BlockSpec((TILE, LANES), lambda i: (i, 0))],
        scratch_shapes=_w_scratch(len(rest)), compiler_params=_params(("arbitrary",)),
    )(head, x, g_norm, *halves, w_glr, proj_first)


def _ret_decay(lgh):
    i = lax.broadcasted_iota(jnp.int32, (TILE, TILE), 0)
    j = lax.broadcasted_iota(jnp.int32, (TILE, TILE), 1)
    rel = (i - j).astype(F32)
    return jnp.where(rel >= 0, jnp.exp(jnp.maximum(rel, 0.0) * lgh), 0.0)


def _ret_vectors(lgh):
    idx = lax.broadcasted_iota(jnp.int32, (TILE, 1), 0).astype(F32)
    xi = jnp.exp((idx + 1.0) * lgh)
    zeta = jnp.exp((TILE - 1.0 - idx) * lgh)
    gc = jnp.exp(jnp.full((1, 1), float(TILE), F32) * lgh)
    return xi, zeta, gc


def _rope_tables(nt):
    half = RET_QK // 2
    inv = ROPE_BASE ** (-jnp.arange(half, dtype=F32) / half)
    base = (jnp.arange(nt, dtype=F32) * TILE - float(PAD_ROWS))[:, None, None] * inv[None, None, :]
    off = jnp.arange(TILE, dtype=F32)[:, None] * inv[None, :]
    return jnp.cos(base), jnp.sin(base), jnp.cos(off), jnp.sin(off)


def _rope_specs(tile_of):
    return [pl.BlockSpec((None, 1, RET_QK // 2), lambda i: (tile_of(i), 0, 0))] * 2 + [pl.BlockSpec((TILE, RET_QK // 2), lambda i: (0, 0))] * 2


def _rope_angles(cb_ref, sb_ref, co_ref, so_ref):
    cb, sb, co, so = cb_ref[...], sb_ref[...], co_ref[...], so_ref[...]
    return cb * co - sb * so, sb * co + cb * so


def _ret_fwd(proj, rope, gain, lg, row_shards):
    t_rows = proj.shape[1]
    nt = t_rows // TILE
    ns = len(row_shards)

    def body(lg_ref, q_ref, k_ref, v_ref, g_ref, cb_ref, sb_ref, co_ref, so_ref, gain_ref, *rest):
        shard_refs, (oraw_ref, oret_ref, st_ref), gathered = rest[:ns], rest[ns:ns + 3], rest[ns + 3:2 * ns + 3]
        s_acc, dm = rest[2 * ns + 3:2 * ns + 5]
        gather = _Exchange(shard_refs, gathered, rest[2 * ns + 5:], among_chips=False)
        t = pl.program_id(0)

        @pl.when(t == 0)
        def _():
            gather.start()
            s_acc[...] = jnp.zeros_like(s_acc)
            for h in range(RET_HEADS):
                dm[h] = _ret_decay(lg_ref[h])

        @pl.when(t == nt - 1)
        def _():
            gather.finish()

        cos_t, sin_t = _rope_angles(cb_ref, sb_ref, co_ref, so_ref)
        for h in range(RET_HEADS):
            lgh = lg_ref[h]
            q = _rope(_cols(q_ref, h, RET_HEADS).astype(F32), cos_t, sin_t)
            k = _rope(_cols(k_ref, h, RET_HEADS).astype(F32), cos_t, sin_t) * (RET_QK ** -0.5)
            xi, zeta, gc = _ret_vectors(lgh)
            v = _cols(v_ref, h, RET_HEADS)
            s_in = s_acc[h]
            p = (_mm_nt(q.astype(BF16), k.astype(BF16)) * dm[h]).astype(BF16)
            o = _mm(p, v) + _mm((q * xi).astype(BF16), s_in.astype(BF16))
            st_ref[h] = s_in.astype(BF16)
            s_acc[h] = s_in * gc + _mm_tn((k * zeta).astype(BF16), v)
            cols = slice(h * RET_V, (h + 1) * RET_V)
            oraw_ref[:, cols] = o
            oc = o - _row_mean(o)
            n = oc * lax.rsqrt(_row_mean(oc * oc) + EPS) * gain_ref[:, cols]
            g = _cols(g_ref, h, RET_HEADS).astype(F32)
            oret_ref[:, cols] = (n * g * _sigmoid(g)).astype(BF16)

    row = lambda w: pl.BlockSpec((TILE, w), lambda t: (t, 0))
    outs = _call(
        body, "ret_fwd", grid=(nt,),
        out_shape=[jax.ShapeDtypeStruct((t_rows, RET_W), F32), jax.ShapeDtypeStruct((t_rows, RET_W), BF16),
                   jax.ShapeDtypeStruct((RET_HEADS, nt, RET_QK, RET_V), BF16)]
                  + [jax.ShapeDtypeStruct((N_DEV, *a.shape), a.dtype) for a in row_shards],
        in_specs=[pl.BlockSpec(memory_space=pltpu.SMEM)] + _proj_specs(("rq", "rk", "rv", "rg"), 1, lambda t: (0, t)) + _rope_specs(lambda t: t) + [
                  pl.BlockSpec((1, RET_W), lambda t: (0, 0))] + [ANY] * ns,
        out_specs=[row(RET_W), row(RET_W), pl.BlockSpec((RET_HEADS, None, RET_QK, RET_V), lambda t: (0, t, 0, 0))] + [ANY] * ns,
        scratch_shapes=[pltpu.VMEM((RET_HEADS, RET_QK, RET_V), F32), pltpu.VMEM((RET_HEADS, TILE, TILE), F32)] + _exchange_sems(ns, N_DEV),
        compiler_params=_params(("arbitrary",)),
    )(lg, proj, proj, proj, proj, *rope, gain, *row_shards)
    return outs[0], outs[1], outs[2], outs[3:]


def _ret_bwd(proj, rope, gain, lg, o_raw, do_ret, states):
    t_rows = proj.shape[1]
    nt = t_rows // TILE

    def body(lg_ref, q_ref, k_ref, v_ref, g_ref, cb_ref, sb_ref, co_ref, so_ref, gain_ref, oraw_ref, do_ref, st_ref,
             dq_ref, dk_ref, dv_ref, dg_ref, dgain_ref, e_acc, dm):
        @pl.when(pl.program_id(0) == 0)
        def _():
            e_acc[...] = jnp.zeros_like(e_acc)
            for h in range(RET_HEADS):
                dm[h] = _ret_decay(lg_ref[h])
            dgain_ref[...] = jnp.zeros_like(dgain_ref)

        cos_t, sin_t = _rope_angles(cb_ref, sb_ref, co_ref, so_ref)
        for h in range(RET_HEADS):
            lgh = lg_ref[h]
            cols = slice(h * RET_V, (h + 1) * RET_V)
            qcols = slice(h * RET_QK, (h + 1) * RET_QK)
            q = _rope(_cols(q_ref, h, RET_HEADS).astype(F32), cos_t, sin_t)
            k = _rope(_cols(k_ref, h, RET_HEADS).astype(F32), cos_t, sin_t) * (RET_QK ** -0.5)
            xi, zeta, gc = _ret_vectors(lgh)
            v = _cols(v_ref, h, RET_HEADS)
            g = _cols(g_ref, h, RET_HEADS).astype(F32)
            o = oraw_ref[:, cols]
            do = do_ref[:, cols].astype(F32)
            oc = o - _row_mean(o)
            rstd = lax.rsqrt(_row_mean(oc * oc) + EPS)
            xh = oc * rstd
            gain_t = gain_ref[:, cols]
            sg = _sigmoid(g)
            dn = do * (g * sg)
            dg_ref[:, cols] = (do * (xh * gain_t) * (sg * (1.0 + g * (1.0 - sg)))).astype(BF16)
            dgain_ref[:, cols] += _col_sum(dn * xh)
            dxh = dn * gain_t
            dob = (rstd * (dxh - _row_mean(dxh) - xh * _row_mean(dxh * xh))).astype(BF16)
            dmat = dm[h]
            qb, kb = q.astype(BF16), k.astype(BF16)
            p = (_mm_nt(qb, kb) * dmat).astype(BF16)
            dp = (_mm_nt(dob, v) * dmat).astype(BF16)
            s_in = st_ref[h]
            e_in = e_acc[h]
            e_b = e_in.astype(BF16)
            dq = _mm(dp, kb) + _mm_nt(dob, s_in) * xi
            dk = _mm_tn(dp, qb) + _mm_nt(v, e_b) * zeta
            dv_ref[:, cols] = (_mm_tn(p, dob) + _mm((k * zeta).astype(BF16), e_b)).astype(BF16)
            e_acc[h] = e_in * gc + _mm_tn((q * xi).astype(BF16), dob)
            dq_ref[:, qcols] = _rope_bwd(dq, cos_t, sin_t).astype(BF16)
            dk_ref[:, qcols] = (_rope_bwd(dk, cos_t, sin_t) * (RET_QK ** -0.5)).astype(BF16)

    row = lambda w: pl.BlockSpec((TILE, w), lambda j: (nt - 1 - j, 0))
    vec = pl.BlockSpec((1, RET_W), lambda j: (0, 0))
    return _call(
        body, "ret_bwd", grid=(nt,),
        out_shape=[jax.ShapeDtypeStruct((t_rows, RET_HEADS * RET_QK), BF16), jax.ShapeDtypeStruct((t_rows, RET_HEADS * RET_QK), BF16),
                   jax.ShapeDtypeStruct((t_rows, RET_W), BF16), jax.ShapeDtypeStruct((t_rows, RET_W), BF16),
                   jax.ShapeDtypeStruct((1, RET_W), F32)],
        in_specs=[pl.BlockSpec(memory_space=pltpu.SMEM)] + _proj_specs(("rq", "rk", "rv", "rg"), 1, lambda j: (0, nt - 1 - j)) + _rope_specs(lambda j: nt - 1 - j) + [vec,
                  row(RET_W), row(RET_W), pl.BlockSpec((RET_HEADS, None, RET_QK, RET_V), lambda j: (0, nt - 1 - j, 0, 0))],
        out_specs=[row(RET_HEADS * RET_QK), row(RET_HEADS * RET_QK), row(RET_W), row(RET_W), vec],
        scratch_shapes=[pltpu.VMEM((RET_HEADS, RET_QK, RET_V), F32), pltpu.VMEM((RET_HEADS, TILE, TILE), F32)],
        compiler_params=_params(("arbitrary",)),
    )(lg, proj, proj, proj, proj, *rope, gain, o_raw, do_ret, states)


GLA_LEVELS = (32, 64, 128, 256)
N_TERMS = 1 + len(GLA_LEVELS)


def _gla_tables():
    p = jnp.arange(TILE)[:, None]
    r = jnp.arange(TILE)[None, :]
    masks = [(p // GLA_CHUNK == r // GLA_CHUNK) & (r <= p)]
    for blk in GLA_LEVELS:
        masks.append((p // blk == r // blk) & (p % blk >= blk // 2) & (r % blk < blk // 2))
    masks = jnp.stack(masks + [m.T for m in masks]).astype(F32)
    cum_fwd = jnp.concatenate([r <= p, masks[0] > 0], axis=0).astype(BF16)
    cum_bwd = jnp.concatenate([r >= p, masks[N_TERMS] > 0], axis=1).astype(BF16)
    return masks, cum_fwd, cum_bwd


def _split3(x):
    hi = x.astype(BF16)
    rest = x - hi.astype(F32)
    mid = rest.astype(BF16)
    lo = (rest - mid.astype(F32)).astype(BF16)
    return jnp.concatenate([hi, mid, lo], axis=1)


def _join3(y):
    w = y.shape[1] // 3
    return (y[:, 2 * w:] + y[:, w:2 * w]) + y[:, :w]


def _gla_decays(glr_ref, wgu_ref, b_ref, cum_ref):
    z = _mm(glr_ref[...].astype(BF16), wgu_ref[...].astype(BF16)) + b_ref[...]
    la = (jnp.minimum(z, 0.0) - jnp.log(1.0 + jnp.exp(-jnp.abs(z)))) / GLA_TAU
    width = la.shape[1]
    hi = la.astype(BF16)
    rest = la - hi.astype(F32)
    mid = rest.astype(BF16)
    lo = (rest - mid.astype(F32)).astype(BF16)
    y = _mm(cum_ref[...], jnp.concatenate([hi, mid, lo], axis=1))
    gb = (y[:, 2 * width:] + y[:, width:2 * width]) + y[:, :width]
    return z, gb[:TILE], gb[TILE:]


def _gla_prep(h, q_ref, k_ref, g_all, b_all, g_scr, ref_scr):
    cols = slice(h * GLA_K, (h + 1) * GLA_K)
    g, b = g_all[:, cols], b_all[:, cols]
    g_scr[h] = g
    factors = [(jnp.exp(b), jnp.exp(-b))]
    for lvl, blk in enumerate(GLA_LEVELS):
        for n in range(TILE // blk):
            ref_scr[h, lvl, n * blk:(n + 1) * blk, :] = jnp.broadcast_to(g_scr[h, pl.ds(n * blk + blk // 2 - 1, 1), :], (blk, GLA_K))
        x = g - ref_scr[h, lvl]
        factors.append((jnp.exp(jnp.minimum(x, 0.0)), jnp.exp(jnp.minimum(-x, 0.0))))
    g_last = g_scr[h, pl.ds(TILE - 1, 1), :]
    q = _cols(q_ref, h, GLA_HEADS).astype(F32) * (GLA_K ** -0.5)
    k = _cols(k_ref, h, GLA_HEADS).astype(F32)
    return q, k, factors, jnp.exp(g), jnp.exp(g_last), jnp.exp(g_last - g)


def _gla_scores(q, k, factors, m_ref):
    a = jnp.zeros((TILE, TILE), F32)
    for l, (fq, fk) in enumerate(factors):
        s = _mm_nt((q * fq).astype(BF16), (k * fk).astype(BF16))
        a = jnp.where(m_ref[l] > 0.0, s, a)
    return a


def _gla_fwd(proj, glr, wgu_pad, b_gate, gain, masks, cum_fwd):
    t_rows = glr.shape[0]
    nt = t_rows // TILE

    def body(q_ref, k_ref, v_ref, g_ref, glr_ref, wgu_ref, b_ref, gain_ref, m_ref, cum_ref, oraw_ref, ogla_ref, st_ref, at_ref,
             s_acc, g_scr, ref_scr):
        @pl.when(pl.program_id(0) == 0)
        def _():
            s_acc[...] = jnp.zeros_like(s_acc)

        _, g_all, b_all = _gla_decays(glr_ref, wgu_ref, b_ref, cum_ref)
        for h in range(GLA_HEADS):
            q, k, factors, e_g, e_last, e_end = _gla_prep(h, q_ref, k_ref, g_all, b_all, g_scr, ref_scr)
            v = _cols(v_ref, h, GLA_HEADS)
            st = s_acc[h]
            st_ref[h] = st
            a = _gla_scores(q, k, factors, m_ref)
            at_ref[h] = a.T.astype(BF16)
            o = _mm(a.astype(BF16), v) + _mm_nt((q * e_g).astype(BF16), st.astype(BF16))
            s_acc[h] = st * e_last + _mm(v.astype(F32).T.astype(BF16), (k * e_end).astype(BF16))
            cols = slice(h * GLA_V, (h + 1) * GLA_V)
            oraw_ref[:, cols] = o
            n = o * lax.rsqrt(_row_mean(o * o) + EPS) * gain_ref[:, cols]
            g = _cols(g_ref, h, GLA_HEADS).astype(F32)
            ogla_ref[:, cols] = (n * g * _sigmoid(g)).astype(BF16)

    row = lambda w: pl.BlockSpec((TILE, w), lambda t: (t, 0))
    whole = lambda *shape: pl.BlockSpec(shape, lambda t: (0,) * len(shape))
    return _call(
        body, "gla_fwd", grid=(nt,),
        out_shape=[jax.ShapeDtypeStruct((t_rows, GLA_W), F32), jax.ShapeDtypeStruct((t_rows, GLA_W), BF16),
                   jax.ShapeDtypeStruct((GLA_HEADS, nt, GLA_V, GLA_K), F32), jax.ShapeDtypeStruct((GLA_HEADS, t_rows, TILE), BF16)],
        in_specs=_proj_specs(("gq", "gk", "gv", "gg"), 1, lambda t: (0, t)) + [row(LANES), whole(LANES, GLA_HEADS * GLA_K),
                  whole(1, GLA_HEADS * GLA_K), whole(1, GLA_W), whole(N_TERMS, TILE, TILE), whole(2 * TILE, TILE)],
        out_specs=[row(GLA_W), row(GLA_W), pl.BlockSpec((GLA_HEADS, None, GLA_V, GLA_K), lambda t: (0, t, 0, 0)),
                   pl.BlockSpec((GLA_HEADS, TILE, TILE), lambda t: (0, t, 0))],
        scratch_shapes=[pltpu.VMEM((GLA_HEADS, GLA_V, GLA_K), F32), pltpu.VMEM((GLA_HEADS, TILE, GLA_K), F32),
                        pltpu.VMEM((GLA_HEADS, len(GLA_LEVELS), TILE, GLA_K), F32)],
        compiler_params=_params(("arbitrary",)),
    )(proj, proj, proj, proj, glr, wgu_pad, b_gate, gain, masks, cum_fwd)


def _gla_bwd(proj, glr, wgu_pad, b_gate, gain, o_raw, do_gla, states, a_t, masks, cum_fwd, cum_bwd):
    t_rows = glr.shape[0]
    nt = t_rows // TILE

    def body(q_ref, k_ref, v_ref, g_ref, glr_ref, wgu_ref, b_ref, gain_ref, m_ref, cum_ref, cumb_ref, oraw_ref, do_ref, st_ref, at_ref,
             dq_ref, dk_ref, dv_ref, dg_ref, dglr_ref, dwgu_ref, dbg_ref, dgain_ref, d_acc, g_scr, ref_scr, dref_scr):
        @pl.when(pl.program_id(0) == 0)
        def _():
            d_acc[...] = jnp.zeros_like(d_acc)
            dwgu_ref[...] = jnp.zeros_like(dwgu_ref)
            dbg_ref[...] = jnp.zeros_like(dbg_ref)
            dgain_ref[...] = jnp.zeros_like(dgain_ref)

        z_all, g_all, b_all = _gla_decays(glr_ref, wgu_ref, b_ref, cum_ref)
        dla_parts = []
        for h in range(GLA_HEADS):
            q, k, factors, e_g, e_last, e_end = _gla_prep(h, q_ref, k_ref, g_all, b_all, g_scr, ref_scr)
            v = _cols(v_ref, h, GLA_HEADS)
            cols = slice(h * GLA_V, (h + 1) * GLA_V)
            kcols = slice(h * GLA_K, (h + 1) * GLA_K)
            o = oraw_ref[:, cols]
            do = do_ref[:, cols].astype(F32)
            g = _cols(g_ref, h, GLA_HEADS).astype(F32)
            rinv = lax.rsqrt(_row_mean(o * o) + EPS)
            nh = o * rinv
            gain_t = gain_ref[:, cols]
            sg = _sigmoid(g)
            dn = do * (g * sg)
            dg_ref[:, cols] = (do * (nh * gain_t) * (sg * (1.0 + g * (1.0 - sg)))).astype(BF16)
            dgain_ref[:, cols] += _col_sum(dn * nh)
            dnh = dn * gain_t
            dor = rinv * (dnh - nh * _row_mean(dnh * nh))
            dob = dor.astype(BF16)
            a_t = at_ref[h]
            da = _mm_nt(dob, v).astype(BF16)
            da_t = _mm_nt(v, dob).astype(BF16)
            st_in = st_ref[h]
            d_out = d_acc[h]
            d_out_b = d_out.astype(BF16)
            qg, kg = q * e_g, k * e_end
            dqg = _mm(dob, st_in.astype(BF16))
            dkg = _mm(v, d_out_b)
            dv_ref[:, cols] = (_mm(a_t, dob) + _mm_nt(kg.astype(BF16), d_out_b)).astype(BF16)
            d_acc[h] = d_out * e_last + _mm(dor.T.astype(BF16), qg.astype(BF16))
            dq = dqg * e_g
            dk = dkg * e_end
            dkg_kg = dkg * kg
            dg_cum = dqg * qg - dkg_kg
            db = None
            for l, (fq, fk) in enumerate(factors):
                qt, kt = q * fq, k * fk
                dqt = _mm(da * m_ref[l], kt.astype(BF16))
                dkt = _mm(da_t * m_ref[N_TERMS + l], qt.astype(BF16))
                dq = dq + dqt * fq
                dk = dk + dkt * fk
                diff = dqt * qt - dkt * kt
                if l == 0:
                    db = diff
                else:
                    dg_cum = dg_cum + diff
                    dref_scr[h, l - 1] = diff
            dq_ref[:, kcols] = (dq * (GLA_K ** -0.5)).astype(BF16)
            dk_ref[:, kcols] = dk.astype(BF16)
            g_scr[h] = dg_cum
            g_scr[h, pl.ds(TILE - 1, 1), :] += e_last * _col_sum(d_out * st_in) + _col_sum(dkg_kg)
            for lvl, blk in enumerate(GLA_LEVELS):
                for n in range(TILE // blk):
                    g_scr[h, pl.ds(n * blk + blk // 2 - 1, 1), :] -= _col_sum(dref_scr[h, lvl, n * blk:(n + 1) * blk, :])
            dla_parts.append(_join3(_mm(cumb_ref[...], jnp.concatenate([_split3(g_scr[h]), _split3(db)], axis=0))))
        dz = jnp.concatenate(dla_parts, axis=1) * (1.0 / GLA_TAU) * _sigmoid(-z_all)
        dzb = dz.astype(BF16)
        wgu_b = wgu_ref[...].astype(BF16)
        for h in range(GLA_HEADS):
            kcols = slice(h * GLA_K, (h + 1) * GLA_K)
            dglr_ref[h] = _mm_nt(dzb[:, kcols], wgu_b[:, kcols]).astype(BF16)
        dwgu_ref[...] += _mm(glr_ref[...].T.astype(BF16), dzb)
        dbg_ref[...] += _col_sum(dz)

    row = lambda w: pl.BlockSpec((TILE, w), lambda j: (nt - 1 - j, 0))
    whole = lambda *shape: pl.BlockSpec(shape, lambda j: (0,) * len(shape))
    return _call(
        body, "gla_bwd", grid=(nt,),
        out_shape=[jax.ShapeDtypeStruct((t_rows, GLA_HEADS * GLA_K), BF16), jax.ShapeDtypeStruct((t_rows, GLA_HEADS * GLA_K), BF16),
                   jax.ShapeDtypeStruct((t_rows, GLA_W), BF16), jax.ShapeDtypeStruct((t_rows, GLA_W), BF16),
                   jax.ShapeDtypeStruct((GLA_HEADS, t_rows, LANES), BF16), jax.ShapeDtypeStruct((LANES, GLA_HEADS * GLA_K), F32),
                   jax.ShapeDtypeStruct((1, GLA_HEADS * GLA_K), F32), jax.ShapeDtypeStruct((1, GLA_W), F32)],
        in_specs=_proj_specs(("gq", "gk", "gv", "gg"), 1, lambda j: (0, nt - 1 - j)) + [row(LANES),
                  whole(LANES, GLA_HEADS * GLA_K), whole(1, GLA_HEADS * GLA_K), whole(1, GLA_W),
                  whole(2 * N_TERMS, TILE, TILE), whole(2 * TILE, TILE), whole(TILE, 2 * TILE), row(GLA_W), row(GLA_W),
                  pl.BlockSpec((GLA_HEADS, None, GLA_V, GLA_K), lambda j: (0, nt - 1 - j, 0, 0)),
                  pl.BlockSpec((GLA_HEADS, TILE, TILE), lambda j: (0, nt - 1 - j, 0))],
        out_specs=[row(GLA_HEADS * GLA_K), row(GLA_HEADS * GLA_K), row(GLA_W), row(GLA_W),
                   pl.BlockSpec((GLA_HEADS, TILE, LANES), lambda j: (0, nt - 1 - j, 0)), whole(LANES, GLA_HEADS * GLA_K),
                   whole(1, GLA_HEADS * GLA_K), whole(1, GLA_W)],
        scratch_shapes=[pltpu.VMEM((GLA_HEADS, GLA_V, GLA_K), F32), pltpu.VMEM((GLA_HEADS, TILE, GLA_K), F32),
                        pltpu.VMEM((GLA_HEADS, len(GLA_LEVELS), TILE, GLA_K), F32),
                        pltpu.VMEM((GLA_HEADS, len(GLA_LEVELS), TILE, GLA_K), F32)],
        compiler_params=_params(("arbitrary",)),
    )(proj, proj, proj, proj, glr, wgu_pad, b_gate, gain, masks.astype(BF16), cum_fwd, cum_bwd, o_raw, do_gla, states, a_t)


def _merge_fwd_bwd(o_ret, o_gla, proj, x, target, g_final, w_br, w_bg, w_out):
    t_rows = x.shape[0] + TILE
    nt = t_rows // TILE

    def body(oret_ref, ogla_ref, mr_ref, mg_ref, h0_ref, tgt_ref, gf_ref, wbr_hbm, wbg_hbm, wout_hbm,
             dh1_ref, dmr_ref, dmg_ref, doret_ref, dogla_ref, loss_ref, dgf_ref, dwbr_hbm, dwbg_hbm, dwout_hbm,
             wbr, wbg, wout, abr, abg, aout, sem):
        i = pl.program_id(0)

        @pl.when(i == 0)
        def _():
            cps = [pltpu.make_async_copy(s, d, sem.at[n]) for n, (s, d) in enumerate(((wbr_hbm, wbr), (wbg_hbm, wbg), (wout_hbm, wout)))]
            for cp in cps:
                cp.start()
            abr[...] = jnp.zeros_like(abr)
            abg[...] = jnp.zeros_like(abg)
            aout[...] = jnp.zeros_like(aout)
            loss_ref[...] = jnp.zeros_like(loss_ref)
            dgf_ref[...] = jnp.zeros_like(dgf_ref)
            for cp in cps:
                cp.wait()
            dh1_ref[...] = jnp.zeros_like(dh1_ref)
            dmr_ref[...] = jnp.zeros_like(dmr_ref)
            dmg_ref[...] = jnp.zeros_like(dmg_ref)
            doret_ref[...] = jnp.zeros_like(doret_ref)
            dogla_ref[...] = jnp.zeros_like(dogla_ref)

        @pl.when(i > 0)
        def _():
            oret, ogla = oret_ref[...], ogla_ref[...]
            br, bg = _mm(oret, wbr[...]), _mm(ogla, wbg[...])
            sr, sg = _sigmoid(_cols(mr_ref).astype(F32)), _sigmoid(_cols(mg_ref).astype(F32))
            mb = (sr * br + sg * bg).astype(BF16)
            h1 = h0_ref[...] + _mm(mb, wout[...])
            r2 = lax.rsqrt(_row_mean(h1 * h1) + EPS)
            hn = h1 * r2
            gf = gf_ref[...]
            diff = hn * gf - tgt_ref[...]
            loss_ref[...] += 0.5 * jnp.sum(_row_mean(diff * diff))
            dy = diff * (1.0 / D_MODEL)
            dgf_ref[...] += _col_sum(dy * hn)
            dyg = dy * gf
            dh1 = r2 * (dyg - hn * _row_mean(dyg * hn))
            dh1_ref[...] = dh1
            dh1b = dh1.astype(BF16)
            dm = _mm_nt(dh1b, wout[...])
            aout[...] += _mm_tn(mb, dh1b)
            dbr = (dm * sr).astype(BF16)
            dbg = (dm * sg).astype(BF16)
            dmr_ref[...] = (dm * br * sr * (1.0 - sr)).astype(BF16)
            dmg_ref[...] = (dm * bg * sg * (1.0 - sg)).astype(BF16)
            doret_ref[...] = _mm_nt(dbr, wbr[...]).astype(BF16)
            dogla_ref[...] = _mm_nt(dbg, wbg[...]).astype(BF16)
            abr[...] += _mm_tn(oret, dbr)
            abg[...] += _mm_tn(ogla, dbg)

        @pl.when(i == nt - 1)
        def _():
            wbr[...] = abr[...].astype(BF16)
            wbg[...] = abg[...].astype(BF16)
            wout[...] = aout[...].astype(BF16)
            pltpu.sync_copy(wbr, dwbr_hbm)
            pltpu.sync_copy(wbg, dwbg_hbm)
            pltpu.sync_copy(wout, dwout_hbm)

    row = lambda w: pl.BlockSpec((TILE, w), lambda i: (i, 0))
    one = lambda w: pl.BlockSpec((1, w), lambda i: (0, 0))
    return _call(
        body, "merge_fwd_bwd", grid=(nt,),
        out_shape=[jax.ShapeDtypeStruct((t_rows, D_MODEL), F32), jax.ShapeDtypeStruct((t_rows, D_MODEL), BF16),
                   jax.ShapeDtypeStruct((t_rows, D_MODEL), BF16), jax.ShapeDtypeStruct((t_rows, RET_W), BF16),
                   jax.ShapeDtypeStruct((t_rows, GLA_W), BF16), jax.ShapeDtypeStruct((1, LANES), F32),
                   jax.ShapeDtypeStruct((1, D_MODEL), F32), jax.ShapeDtypeStruct((RET_W, D_MODEL), BF16),
                   jax.ShapeDtypeStruct((GLA_W, D_MODEL), BF16), jax.ShapeDtypeStruct((D_MODEL, D_MODEL), BF16)],
        in_specs=[row(RET_W), row(GLA_W)] + _proj_specs(("mr", "mg"), 1, lambda i: (0, i)) + [_x_spec(), _x_spec(), one(D_MODEL), ANY, ANY, ANY],
        out_specs=[row(D_MODEL), row(D_MODEL), row(D_MODEL), row(RET_W), row(GLA_W), one(LANES), one(D_MODEL), ANY, ANY, ANY],
        scratch_shapes=[pltpu.VMEM((RET_W, D_MODEL), BF16), pltpu.VMEM((GLA_W, D_MODEL), BF16), pltpu.VMEM((D_MODEL, D_MODEL), BF16),
                        pltpu.VMEM((RET_W, D_MODEL), F32), pltpu.VMEM((GLA_W, D_MODEL), F32), pltpu.VMEM((D_MODEL, D_MODEL), F32),
                        pltpu.SemaphoreType.DMA((3,))],
        compiler_params=_params(("arbitrary",)),
    )(o_ret, o_gla, proj, proj, x, target, g_final, w_br, w_bg, w_out)


def _inproj_bwd_x(dseg, dglr, head, x, dh1, g_norm, slabs, w_glr, chip_partials):
    t_rows = x.shape[0] + TILE
    nt = t_rows // TILE
    ne = len(chip_partials)

    def body(*refs):
        d_refs = refs[:10]
        dglr_ref, head_ref, x_ref, dh1_ref, g_ref, slabs_a, slabs_b, wg_hbm = refs[10:18]
        part_refs = refs[18:18 + ne]
        dx_ref, dhead_ref, dgn_ref = refs[18 + ne:21 + ne]
        landed = refs[21 + ne:21 + 2 * ne]
        w_vm, wg_vm, edge_vm, sem = refs[21 + 2 * ne:25 + 2 * ne]
        exchange = _Exchange(part_refs, landed, refs[25 + 2 * ne:], among_chips=True)

        @pl.when(pl.program_id(0) == 0)
        def _():
            exchange.start()
            dgn_ref[...] = jnp.zeros_like(dgn_ref)
            _load_weight((slabs_a, slabs_b), wg_hbm, w_vm, wg_vm, edge_vm, sem)

        @pl.when(pl.program_id(0) == nt - 1)
        def _():
            exchange.finish()

        dglr = dglr_ref[0].astype(F32)
        for h in range(1, GLA_HEADS):
            dglr = dglr + dglr_ref[h].astype(F32)
        du = _mm_nt(dglr.astype(BF16), wg_vm[...])
        for s, d_ref in enumerate(d_refs):
            du = du + _mm_nt(d_ref[...], w_vm[:, SEG_OFF[s]:SEG_OFF[s] + SEG_W[s]])
        x = _tile_rows(head_ref, x_ref)
        r = lax.rsqrt(_row_mean(x * x) + EPS)
        hn = x * r
        dgn_ref[...] += _col_sum(du * hn)
        dug = du * g_ref[...]
        dh0 = dh1_ref[...] + r * (dug - hn * _row_mean(dug * hn))
        dx_ref[...] = dh0

        @pl.when(pl.program_id(0) == 0)
        def _():
            dhead_ref[...] = dh0

    row = lambda w: pl.BlockSpec((TILE, w), lambda i: (i, 0))
    one = pl.BlockSpec((1, D_MODEL), lambda i: (0, 0))
    return _call(
        body, "inproj_bwd_x", grid=(nt,),
        out_shape=[jax.ShapeDtypeStruct((t_rows - TILE, D_MODEL), F32), jax.ShapeDtypeStruct((TILE, D_MODEL), F32),
                   jax.ShapeDtypeStruct((1, D_MODEL), F32)] + [jax.ShapeDtypeStruct(a.shape, a.dtype) for a in chip_partials],
        in_specs=[row(w) for w in SEG_W] + [pl.BlockSpec((GLA_HEADS, TILE, LANES), lambda i: (0, i, 0)),
                                            _head_spec(), _x_spec(), row(D_MODEL), one, ANY, ANY, ANY] + [ANY] * ne,
        out_specs=[_x_spec(), _head_spec(), one] + [ANY] * ne,
        scratch_shapes=W_SCRATCH() + _exchange_sems(ne, N_CHIP),
        compiler_params=_params(("arbitrary",)),
    )(*[dseg[n] for n in SEG_NAMES], dglr, head, x, dh1, g_norm, *slabs, w_glr, *chip_partials)


W_TILE = 512


def _inproj_bwd_w(ut, dseg, dglr, row_sends):
    nt = ut.shape[0]
    t_rows = nt * TILE
    kc = 3 if nt % 3 == 0 else 1
    tiles = [(s, c) for s in range(len(SEG_W)) for c in range(0, SEG_W[s], W_TILE)]
    bpt = W_TILE // LANES
    nr = len(row_sends)
    n = 1 + nr
    last_tile = [(SLAB_BLK0[d] + SLAB_BLOCKS - 1) // bpt for d in range(N_DEV)]

    def body(ut_hbm, *refs):
        d_refs, dglr_hbm, row_refs = refs[:10], refs[10], refs[11:11 + nr]
        out_hbm, oglr_ref, sib = refs[11 + nr], refs[12 + nr], refs[13 + nr:13 + nr + n]
        ut_vm, dbuf, obuf, acc, gbuf, sem, send_sems, recv_sems = refs[13 + nr + n:]
        x, y, core = _position()

        def handover(d, k, landed=False):
            q = d // 2
            src = out_hbm.at[pl.ds(SLAB_BLK0[d], SLAB_BLOCKS)] if k == 0 else row_refs[k - 1].at[d]
            return pltpu.make_async_remote_copy(src_ref=sib[k].at[q] if landed else src, dst_ref=sib[k].at[q],
                                                send_sem=send_sems.at[n * q + k], recv_sem=recv_sems.at[n * q + k],
                                                device_id=(x, y, 1 - core), device_id_type=MESH)

        def for_sibling(d, ks, fn):
            @pl.when(d % 2 != core)
            def _():
                for k in ks:
                    fn(handover(d, k))

        for d in range(N_DEV):
            for_sibling(d, range(1, n), lambda cp: cp.start())

        def fetch(i):
            s, c = tiles[i]
            return pltpu.make_async_copy(d_refs[s].at[:, pl.ds(c, W_TILE)], dbuf.at[i % 2], sem.at[1 + i % 2])

        def contract(rhs_refs, width):
            acc[:, :width] = jnp.zeros((D_MODEL, width), F32)

            def step(k, carry):
                part = None
                for j in range(kc):
                    kk = k * kc + j
                    for rhs_ref in rhs_refs:
                        prod = _mm(ut_vm[kk], rhs_ref[pl.ds(pl.multiple_of(kk * TILE, TILE), TILE), :])
                        part = prod if part is None else part + prod
                acc[:, :width] += part
                return carry

            lax.fori_loop(0, nt // kc, step, 0)
            return acc[:, :width]

        load_ut = pltpu.make_async_copy(ut_hbm, ut_vm, sem.at[0])
        load_glr = pltpu.make_async_copy(dglr_hbm, gbuf, sem.at[5])
        load_ut.start()
        load_glr.start()
        fetch(0).start()
        load_ut.wait()
        stores = {}

        def stored(i):
            stores[i].wait()
            for d in range(N_DEV):
                if last_tile[d] == i:
                    for_sibling(d, [0], lambda cp: cp.start())

        for i, (s, c) in enumerate(tiles):
            if i + 1 < len(tiles):
                fetch(i + 1).start()
            fetch(i).wait()
            if i >= 2:
                stored(i - 2)
            total = contract([dbuf.at[i % 2]], W_TILE)
            for j in range(bpt):
                obuf[i % 2, j] = total[:, j * LANES:(j + 1) * LANES].astype(BF16)
            blk0 = (SEG_OFF[s] + c) // LANES
            stores[i] = pltpu.make_async_copy(obuf.at[i % 2], out_hbm.at[pl.ds(blk0, bpt)], sem.at[3 + i % 2])
            stores[i].start()
        for i in range(max(0, len(tiles) - 2), len(tiles)):
            stored(i)
        load_glr.wait()
        head_sum = gbuf[0].astype(F32)
        for h in range(1, GLA_HEADS):
            head_sum = head_sum + gbuf[h].astype(F32)
        gbuf[0] = head_sum.astype(BF16)
        oglr_ref[...] = contract([gbuf.at[0]], LANES)
        for q in range(N_CHIP):
            for k in range(n):
                handover(2 * q, k, landed=True).wait_recv()
        for d in range(N_DEV):
            for_sibling(d, range(n), lambda cp: cp.wait_send())

    outs = _call(
        body, "inproj_bwd_w",
        out_shape=[jax.ShapeDtypeStruct((AL_COLS // LANES, D_MODEL, LANES), BF16), jax.ShapeDtypeStruct((D_MODEL, LANES), F32),
                   jax.ShapeDtypeStruct((N_CHIP, SLAB_BLOCKS, D_MODEL, LANES), BF16)]
                  + [jax.ShapeDtypeStruct((N_CHIP, *r.shape[1:]), BF16) for r in row_sends],
        in_specs=[ANY] * (12 + nr), out_specs=[ANY, pl.BlockSpec(memory_space=pltpu.VMEM)] + [ANY] * n,
        scratch_shapes=[pltpu.VMEM((nt, D_MODEL, TILE), BF16), pltpu.VMEM((2, t_rows, W_TILE), BF16),
                        pltpu.VMEM((2, bpt, D_MODEL, LANES), BF16), pltpu.VMEM((D_MODEL, W_TILE), F32),
                        pltpu.VMEM((GLA_HEADS, t_rows, LANES), BF16), pltpu.SemaphoreType.DMA((6,)),
                        pltpu.SemaphoreType.DMA((n * N_CHIP,)), pltpu.SemaphoreType.DMA((n * N_CHIP,))],
        compiler_params=_params(),
    )(ut, *[dseg[n_] for n_ in SEG_NAMES], dglr, *row_sends)
    return outs[0], outs[1], outs[2], outs[3:]


def _position():
    x, y, c = lax.axis_index("x"), lax.axis_index("y"), lax.axis_index("c")
    return x, y, c


def _index(px, py, pc):
    return 4 * px + 2 * py + pc


def _gather_sems(n):
    return [pltpu.SemaphoreType.DMA((7 * n,)), pltpu.SemaphoreType.DMA((7 * n,)), pltpu.SemaphoreType.DMA((n,))]


class _RelayGather:
    STAGES = 5

    def __init__(self, ins, outs, sems):
        self.ins, self.outs, self.n = ins, outs, len(ins)
        self.send_sems, self.recv_sems, self.local_sems = sems
        x, y, c = _position()
        self.c, self.me, self.sibling = c, (x, y, c), (x, y, 1 - c)
        self.chips = [(1 - x, y), (x, 1 - y), (1 - x, 1 - y)]

    def _copy(self, a, k, block, to, src=None):
        dst = self.outs[a].at[_index(*block)]
        return pltpu.make_async_remote_copy(src_ref=dst if src is None else src, dst_ref=dst,
                                            send_sem=self.send_sems.at[7 * a + k], recv_sem=self.recv_sems.at[7 * a + k],
                                            device_id=to, device_id_type=MESH)

    def _relay(self, a, j):
        return self._copy(a, 3, (*self.chips[j], self.c), (*self.chips[1 - j], self.c))

    def _mine(self):
        return [pltpu.make_async_copy(self.ins[a], self.outs[a].at[_index(*self.me)], self.local_sems.at[a]) for a in range(self.n)]

    def _first(self):
        first = []
        for a in range(self.n):
            first.append(self._copy(a, 0, self.me, self.sibling, src=self.ins[a]))
            first += [self._copy(a, 1 + j, self.me, (*self.chips[j], self.c), src=self.ins[a]) for j in range(2)]
        return first

    def _passed(self, j):
        return [self._copy(a, 4 + j, (*self.chips[j], self.c), self.sibling) for a in range(self.n)]

    def stage(self, s):
        n, c = self.n, self.c
        if s == 0:
            for cp in self._mine() + self._first():
                cp.start()
        elif s < 4:
            j = s - 1
            for a in range(n):
                self._copy(a, 1 + j, (*self.chips[j], c), self.me).wait_recv()
            for cp in self._passed(j):
                cp.start()
            if j < 2:
                @pl.when(c == j)
                def _():
                    for a in range(n):
                        self._relay(a, j).start()
        else:
            for a in range(n):
                self._copy(a, 0, self.sibling, self.me).wait_recv()
                for j in range(3):
                    self._copy(a, 4 + j, (*self.chips[j], 1 - c), self.me).wait_recv()
            for cp in self._first() + self._passed(0) + self._passed(1) + self._passed(2):
                cp.wait_send()
            for j in range(2):
                @pl.when(c == j)
                def _():
                    for a in range(n):
                        self._relay(a, j).wait_send()
            for cp in self._mine():
                cp.wait()


def _all_gather(arrs, name):
    n = len(arrs)

    def body(*refs):
        gather = _RelayGather(refs[:n], refs[n:2 * n], refs[2 * n:])
        for s in range(gather.STAGES):
            gather.stage(s)

    return _call(
        body, name,
        out_shape=[jax.ShapeDtypeStruct((N_DEV, *a.shape), a.dtype) for a in arrs],
        in_specs=[ANY] * n, out_specs=[ANY] * n, scratch_shapes=_gather_sems(n),
    )(*arrs)


N_CHIP = N_DEV // 2


def _slab_block0(owner):
    step = SLAB_BLK0[1]
    assert all(SLAB_BLK0[d] == step * d - (d == N_DEV - 1) for d in range(N_DEV))
    return step * owner - jnp.where(owner == N_DEV - 1, 1, 0)


def _add_bf16(c_ref, a_ref, b_ref, o_ref):
    o_ref[...] = (a_ref[...].astype(F32) + b_ref[...].astype(F32)).astype(BF16)


def _chip_partial_slab(dw_blocks, sib, core):
    blk = pl.BlockSpec((None, SLAB_BLOCKS, D_MODEL, LANES), lambda q, c_ref: (q, 0, 0, 0))
    return _call(
        functools.partial(_add_bf16), "chip_partial_w_in", out_shape=jax.ShapeDtypeStruct(sib.shape, BF16),
        grid_spec=pltpu.PrefetchScalarGridSpec(
            num_scalar_prefetch=1, grid=(N_CHIP,),
            in_specs=[pl.BlockSpec((pl.Element(SLAB_BLOCKS), pl.Element(D_MODEL), pl.Element(LANES)),
                                   lambda q, c_ref: (_slab_block0(2 * q + c_ref[0]), 0, 0)), blk],
            out_specs=blk),
        compiler_params=_params(("arbitrary",)),
    )(core, dw_blocks, sib)


def _chip_partial_rows(sends, sibs, core):
    n = len(sends)

    def body(c_ref, *refs):
        for k in range(n):
            _add_bf16(c_ref, refs[k], refs[n + k], refs[2 * n + k])

    own = [pl.BlockSpec((None, *a.shape[1:]), lambda q, c_ref: (2 * q + c_ref[0], 0, 0)) for a in sends]
    blk = [pl.BlockSpec((None, *a.shape[1:]), lambda q, c_ref: (q, 0, 0)) for a in sibs]
    return _call(
        body, "chip_partial_rows", out_shape=[jax.ShapeDtypeStruct(a.shape, BF16) for a in sibs],
        grid_spec=pltpu.PrefetchScalarGridSpec(num_scalar_prefetch=1, grid=(N_CHIP,), in_specs=own + blk, out_specs=blk),
        compiler_params=_params(("arbitrary",)),
    )(core, *sends, *sibs)


def _exchange_sems(n_arrays, n_peers):
    return [pltpu.SemaphoreType.DMA((n_arrays * n_peers,)), pltpu.SemaphoreType.DMA((n_arrays * n_peers,)),
            pltpu.SemaphoreType.DMA((n_arrays,))]


class _Exchange:
    def __init__(self, srcs, dsts, sems, among_chips):
        self.arrs = list(zip(srcs, dsts))
        self.n = len(self.arrs)
        self.send_sems, self.recv_sems, self.local_sems = sems
        self.among_chips = among_chips
        x, y, c = _position()
        self.c = c
        self.me = 2 * x + y if among_chips else _index(x, y, c)
        self.n_peers = N_CHIP if among_chips else N_DEV

    def _device(self, p):
        return (p // 2, p % 2, self.c) if self.among_chips else (p // 4, (p // 2) % 2, p % 2)

    def _src(self, k, p):
        src = self.arrs[k][0]
        return src.at[p] if self.among_chips else src

    def _mine(self):
        return [pltpu.make_async_copy(self._src(k, self.me), self.arrs[k][1].at[self.me], self.local_sems.at[k]) for k in range(self.n)]

    def _copy(self, p, k, landing):
        return pltpu.make_async_remote_copy(
            src_ref=self._src(k, p), dst_ref=self.arrs[k][1].at[landing], send_sem=self.send_sems.at[self.n * p + k],
            recv_sem=self.recv_sems.at[self.n * landing + k], device_id=self._device(p), device_id_type=MESH)

    def _others(self, fn):
        for p in range(self.n_peers):
            @pl.when(p != self.me)
            def _():
                for k in range(self.n):
                    fn(p, k)

    def start(self):
        for cp in self._mine():
            cp.start()
        self._others(lambda p, k: self._copy(p, k, self.me).start())

    def finish(self):
        self._others(lambda p, k: self._copy(p, k, p).wait_recv())
        self._others(lambda p, k: self._copy(p, k, self.me).wait_send())
        for cp in self._mine():
            cp.wait()


def _adamw(g, w, m, v):
    m_new = ADAM_B1 * m + (1.0 - ADAM_B1) * g
    v_new = ADAM_B2 * v + (1.0 - ADAM_B2) * (g * g)
    m_hat = m_new / (1.0 - ADAM_B1 ** ADAM_STEP)
    v_hat = v_new / (1.0 - ADAM_B2 ** ADAM_STEP)
    delta = -ADAM_LR * (m_hat / (jnp.sqrt(v_hat) + ADAM_EPS) + ADAM_WD * w)
    return delta, m_new, v_new


def _sum_partials(p_ref):
    g = p_ref[0].astype(F32)
    for d in range(1, p_ref.shape[0]):
        g = g + p_ref[d].astype(F32)
    return g


def _reduce_adam_rows(parts, ws, ms, vs):
    n = len(ws)

    def body(*refs):
        ins, outs = refs[:4 * n], refs[4 * n:]
        for k in range(n):
            p_ref, w_ref, m_ref, v_ref = ins[k], ins[n + k], ins[2 * n + k], ins[3 * n + k]
            g = _sum_partials(p_ref)
            outs[4 * k][...] = g
            outs[4 * k + 1][...], outs[4 * k + 2][...], outs[4 * k + 3][...] = _adamw(g, w_ref[...], m_ref[...], v_ref[...])

    outs = _call(
        body, "adam_row_weights", out_shape=[jax.ShapeDtypeStruct(w.shape, F32) for w in ws for _ in range(4)],
        compiler_params=_params(),
    )(*parts, *ws, *ms, *vs)
    return [tuple(outs[4 * k:4 * k + 4]) for k in range(n)]


def _reduce_adam_slab(parts, glr, w_t, m_t, v_t, me):
    cols, rows = w_t.shape
    shift = jnp.asarray(SLAB_SHIFT, jnp.int32)[me]
    glr_at = jnp.where(me == GLR_DEV, GLR_LOCAL, cols).astype(jnp.int32)

    def body(s_ref, p_ref, glr_ref, w_ref, m_ref, v_ref, g_ref, d_ref, mo_ref, vo_ref, slab_t):
        shift, glr_at = s_ref[0], s_ref[1]
        tall = jnp.concatenate([_sum_partials(p_ref.at[:, j]).T for j in range(SLAB_BLOCKS)], axis=0)
        before = pltpu.roll(tall, SLAB_W - shift, 0)
        after = pltpu.roll(tall, lax.rem(SLAB_W - shift + GLA_RANK, SLAB_W), 0)
        wide = jnp.concatenate([glr_ref[...].T, jnp.zeros((SLAB_W - LANES, LANES), F32)], axis=0)
        placed = pltpu.roll(wide, lax.rem(glr_at, SLAB_W), 0)
        row = lax.broadcasted_iota(jnp.int32, (SLAB_W, LANES), 0)
        slab_t[...] = jnp.where(row < glr_at, before, jnp.where(row < glr_at + GLA_RANK, placed, after))
        g = slab_t[pl.ds(0, cols), :]
        g_ref[...] = g
        d_ref[...], mo_ref[...], vo_ref[...] = _adamw(g, w_ref[...], m_ref[...], v_ref[...])

    blk = pl.BlockSpec((cols, LANES), lambda i, s: (0, i))
    return _call(
        body, "adam_w_in", out_shape=[jax.ShapeDtypeStruct((cols, rows), F32)] * 4,
        grid_spec=pltpu.PrefetchScalarGridSpec(
            num_scalar_prefetch=1, grid=(rows // LANES,),
            in_specs=[pl.BlockSpec((parts.shape[0], SLAB_BLOCKS, LANES, LANES), lambda i, s: (0, 0, i, 0)),
                      pl.BlockSpec((LANES, LANES), lambda i, s: (i, 0)), blk, blk, blk],
            out_specs=[blk] * 4, scratch_shapes=[pltpu.VMEM((SLAB_W, LANES), F32)]),
        compiler_params=_params(("arbitrary",)),
    )(jnp.stack([shift, glr_at]), parts, glr, w_t, m_t, v_t)


def _reduce_small(parts):
    def body(p_ref, o_ref):
        o_ref[...] = _sum_partials(p_ref)

    return _call(body, "reduce_small", out_shape=jax.ShapeDtypeStruct(parts.shape[1:], F32))(parts)


def _adam_small(g, w, m, v):
    def body(g_ref, w_ref, m_ref, v_ref, d_ref, mo_ref, vo_ref):
        d_ref[...], mo_ref[...], vo_ref[...] = _adamw(g_ref[...], w_ref[...], m_ref[...], v_ref[...])

    return _call(body, "adam_small", out_shape=[jax.ShapeDtypeStruct(g.shape, F32)] * 3)(g, w, m, v)


def _pack_rows(arrs):
    rows = []
    for a in arrs:
        flat = a.reshape(-1).astype(F32)
        pad = (-flat.shape[0]) % LANES
        rows.append(jnp.pad(flat, (0, pad)).reshape(-1, LANES))
    packed = jnp.concatenate(rows, axis=0)
    return jnp.pad(packed, ((0, (-packed.shape[0]) % 8), (0, 0)))


def _unpack_rows(packed, shapes):
    out, r = [], 0
    for shp in shapes:
        size = 1
        for s in shp:
            size *= s
        nrows = -(-size // LANES)
        out.append(packed[r:r + nrows].reshape(-1)[:size].reshape(shp))
        r += nrows
    return out


def _shard_to_slab(shard, d):
    glr = jnp.zeros((D_MODEL, GLA_RANK), shard.dtype)
    if d == GLR_DEV:
        glr = shard[:, GLR_LOCAL:GLR_LOCAL + GLA_RANK]
        shard = jnp.concatenate([shard[:, :GLR_LOCAL], shard[:, GLR_LOCAL + GLA_RANK:]], axis=1)
    return jnp.pad(shard, ((0, 0), (SLAB_SHIFT[d], SLAB_W - SLAB_SHIFT[d] - shard.shape[1]))), glr


def kernel(x, meta_tokens, norm_gain, w_in, w_gate_up, b_gate, ret_norm_gain, gla_norm_gain, w_branch_ret, w_branch_gla, w_out, final_norm_gain, loss_target, m_meta_tokens, m_norm_gain, m_w_in, m_w_gate_up, m_b_gate, m_ret_norm_gain, m_gla_norm_gain, m_w_branch_ret, m_w_branch_gla, m_w_out, m_final_norm_gain, v_meta_tokens, v_norm_gain, v_w_in, v_w_gate_up, v_b_gate, v_ret_norm_gain, v_gla_norm_gain, v_w_branch_ret, v_w_branch_gla, v_w_out, v_final_norm_gain):
    xi, yi, ci = _position()
    me = _index(xi, yi, ci)
    seq = x.shape[1]
    t_rows = seq + TILE
    in_shard = w_in.shape[2]
    gu_shard = w_gate_up.shape[2]
    meta_shard = meta_tokens.shape[1]
    ret_rows, gla_rows, out_rows = w_branch_ret.shape[1], w_branch_gla.shape[1], w_out.shape[1]

    assert in_shard == IN_SHARD
    slab_local, glr_local = lax.switch(me, [functools.partial(_shard_to_slab, d=d) for d in range(N_DEV)], w_in[0])
    small_local = jnp.concatenate([meta_tokens, jnp.pad(w_gate_up[0], ((0, 0), (0, LANES - gu_shard))),
                                   glr_local.reshape(-1, LANES)], axis=0)
    slab_local = slab_local.astype(BF16)
    first_halves, g_small = _all_gather([slab_local[:, :HALF_W], small_local], "all_gather_shards")
    n_small = N_META + GLA_RANK
    w_glr = jnp.pad(g_small[GLR_DEV, n_small:].reshape(D_MODEL, GLA_RANK), ((0, 0), (0, LANES - GLA_RANK))).astype(BF16)
    meta_full = jnp.transpose(g_small[:, :N_META, :], (1, 0, 2)).reshape(N_META, D_MODEL)
    wgu_full = jnp.transpose(g_small[:, N_META:n_small, :gu_shard], (1, 0, 2)).reshape(GLA_RANK, GLA_HEADS * GLA_K)
    wgu_pad = jnp.pad(wgu_full, ((0, LANES - GLA_RANK), (0, 0)))

    rope = _rope_tables(t_rows // TILE)
    lg = jnp.log1p(-(2.0 ** (-5.0 - jnp.arange(RET_HEADS, dtype=F32))))

    head = jnp.concatenate([jnp.zeros((PAD_ROWS, D_MODEL), F32), meta_full], axis=0)
    proj_first, second_halves = _inproj_first(head, x[0], norm_gain, first_halves, slab_local[:, HALF_W:])
    slabs = (first_halves, second_halves)
    ut, proj, glr = _inproj_tiles(head, x[0], norm_gain, slabs, w_glr, proj_first)
    o_ret_raw, o_ret, ret_states, (g_br, g_bg, g_o) = _ret_fwd(
        proj, rope, ret_norm_gain, lg, [w_branch_ret[0].astype(BF16), w_branch_gla[0].astype(BF16), w_out[0].astype(BF16)])
    w_br, w_bg, w_o = g_br.reshape(RET_W, D_MODEL), g_bg.reshape(GLA_W, D_MODEL), g_o.reshape(D_MODEL, D_MODEL)
    masks, cum_fwd, cum_bwd = _gla_tables()
    o_gla_raw, o_gla, gla_states, gla_scores_t = _gla_fwd(proj, glr, wgu_pad, b_gate, gla_norm_gain, masks, cum_fwd)
    (dh1, d_mr, d_mg, do_ret, do_gla, loss_part, d_gfinal, dw_br, dw_bg, dw_o) = _merge_fwd_bwd(
        o_ret, o_gla, proj, x[0], loss_target[0], final_norm_gain.reshape(1, D_MODEL), w_br, w_bg, w_o)

    d_rq, d_rk, d_rv, d_rg, d_gret = _ret_bwd(proj, rope, ret_norm_gain, lg, o_ret_raw, do_ret, ret_states)
    d_gq, d_gk, d_gv, d_gg, dglr_parts, d_wgu, d_bgate, d_ggla = _gla_bwd(
        proj, glr, wgu_pad, b_gate, gla_norm_gain, o_gla_raw, do_gla, gla_states, gla_scores_t, masks, cum_fwd, cum_bwd)
    dseg = dict(rq=d_rq, rk=d_rk, rv=d_rv, rg=d_rg, gq=d_gq, gk=d_gk, gv=d_gv, gg=d_gg, mr=d_mr, mg=d_mg)
    row_sends = [dw_br.reshape(N_DEV, ret_rows, D_MODEL), dw_bg.reshape(N_DEV, gla_rows, D_MODEL),
                 dw_o.reshape(N_DEV, out_rows, D_MODEL)]
    dw_blocks, dw_glr, sib_in, sib_rows = _inproj_bwd_w(ut, dseg, dglr_parts, row_sends)
    core = ci.astype(jnp.int32).reshape(1)
    chip_partials = [_chip_partial_slab(dw_blocks, sib_in, core)] + list(_chip_partial_rows(row_sends, list(sib_rows), core))
    grad_x, d_head, d_gnorm, p_in, p_br, p_bg, p_o = _inproj_bwd_x(
        dseg, dglr_parts, head, x[0], dh1, norm_gain, slabs, w_glr, chip_partials)
    small_shapes = [(N_META, D_MODEL), (1, D_MODEL), (GLA_RANK, GLA_HEADS * GLA_K), (1, GLA_HEADS * GLA_K),
                    (1, RET_W), (1, GLA_W), (1, D_MODEL), (1, LANES), (D_MODEL, GLA_RANK)]
    small_part = _pack_rows([d_head[PAD_ROWS:], d_gnorm, d_wgu[:GLA_RANK], d_bgate, d_gret, d_ggla, d_gfinal, loss_part,
                             dw_glr[:, :GLA_RANK]])
    (p_small,) = _all_gather([small_part], "all_gather_small_partials")

    (g_meta_f, g_gnorm, g_wgu_f, g_bgate, g_gret, g_ggla, g_gfinal, loss_all,
     g_wglr) = _unpack_rows(_reduce_small(p_small), small_shapes)
    g_w_in, d_w_in, nm_w_in, nv_w_in = [a.T for a in _reduce_adam_slab(
        p_in, jnp.pad(g_wglr, ((0, 0), (0, LANES - GLA_RANK))), w_in[0].T, m_w_in[0].T, v_w_in[0].T, me)]
    ((g_w_br, d_w_br, nm_w_br, nv_w_br), (g_w_bg, d_w_bg, nm_w_bg, nv_w_bg), (g_w_o, d_w_o, nm_w_o, nv_w_o)) = _reduce_adam_rows(
        [p_br, p_bg, p_o], [w_branch_ret[0], w_branch_gla[0], w_out[0]], [m_w_branch_ret[0], m_w_branch_gla[0], m_w_out[0]],
        [v_w_branch_ret[0], v_w_branch_gla[0], v_w_out[0]])
    g_meta = lax.dynamic_slice_in_dim(g_meta_f, me * meta_shard, meta_shard, axis=1)
    g_wgu = lax.dynamic_slice_in_dim(g_wgu_f, me * gu_shard, gu_shard, axis=1)
    s_g = [g_meta, g_gnorm, g_wgu, g_bgate, g_gret, g_ggla, g_gfinal]
    s_w = [meta_tokens, norm_gain, w_gate_up[0], b_gate, ret_norm_gain, gla_norm_gain, final_norm_gain]
    s_m = [m_meta_tokens, m_norm_gain, m_w_gate_up[0], m_b_gate, m_ret_norm_gain, m_gla_norm_gain, m_final_norm_gain]
    s_v = [v_meta_tokens, v_norm_gain, v_w_gate_up[0], v_b_gate, v_ret_norm_gain, v_gla_norm_gain, v_final_norm_gain]
    shapes = [a.shape for a in s_g]
    s_d, s_nm, s_nv = [_unpack_rows(p, shapes) for p in _adam_small(*[_pack_rows(l) for l in (s_g, s_w, s_m, s_v)])]

    loss = loss_all[0, 0]
    grad_x = grad_x[None]

    def order(meta, gnorm, win, wgu, bgate, gret, ggla, wbr, wbg, wo, gfin):
        return (meta, gnorm, win[None], wgu[None], bgate, gret, ggla, wbr[None], wbg[None], wo[None], gfin.reshape(final_norm_gain.shape))

    def small(l):
        return dict(meta=l[0], gnorm=l[1], wgu=l[2], bgate=l[3], gret=l[4], ggla=l[5], gfin=l[6])

    grads = order(win=g_w_in, wbr=g_w_br, wbg=g_w_bg, wo=g_w_o, **small(s_g))
    deltas = order(win=d_w_in, wbr=d_w_br, wbg=d_w_bg, wo=d_w_o, **small(s_d))
    new_m = order(win=nm_w_in, wbr=nm_w_br, wbg=nm_w_bg, wo=nm_w_o, **small(s_nm))
    new_v = order(win=nv_w_in, wbr=nv_w_br, wbg=nv_w_bg, wo=nv_w_o, **small(s_nv))
    return (loss, grad_x, *grads, *deltas, *new_m, *new_v)
```

```python
import functools

import jax
import jax.numpy as jnp
from jax import lax
from jax.experimental import pallas as pl
from jax.experimental.pallas import tpu as pltpu

F32 = jnp.float32
BF16 = jnp.bfloat16

D_MODEL = 1024
N_META = 16
TILE = 256
PAD_ROWS = TILE - N_META
RET_HEADS = 4
RET_QK = 256
RET_V = 512
RET_W = RET_HEADS * RET_V
GLA_HEADS = 4
GLA_K = 128
GLA_V = 256
GLA_W = GLA_HEADS * GLA_V
GLA_RANK = 16
GLA_TAU = 16.0
GLA_CHUNK = 16
ROPE_BASE = 10000.0
EPS = 1e-6
LANES = 128
N_DEV = 8
SEG_NAMES = ("rq", "rk", "rv", "rg", "gq", "gk", "gv", "gg", "mr", "mg")
SEG_W = (1024, 1024, 2048, 2048, 512, 512, 1024, 1024, 1024, 1024)
SEG_OFF = tuple(sum(SEG_W[:i]) for i in range(len(SEG_W)))
AL_COLS = sum(SEG_W)
IN_COLS = AL_COLS + GLA_RANK
GLR_OFF = sum(SEG_W[:8])
IN_SHARD = IN_COLS // N_DEV


def _aligned_col(c):
    assert c <= GLR_OFF or c >= GLR_OFF + GLA_RANK
    return c if c <= GLR_OFF else c - GLA_RANK


SLAB_BOUND = tuple(_aligned_col(IN_SHARD * d) for d in range(N_DEV + 1))
SLAB_BLK0 = tuple(b // LANES for b in SLAB_BOUND[:-1])
SLAB_SHIFT = tuple(b % LANES for b in SLAB_BOUND[:-1])
SLAB_BLOCKS = max(-(-SLAB_BOUND[d + 1] // LANES) - SLAB_BLK0[d] for d in range(N_DEV))
SLAB_W = SLAB_BLOCKS * LANES
GLR_DEV = GLR_OFF // IN_SHARD
GLR_LOCAL = GLR_OFF - GLR_DEV * IN_SHARD
assert all(SLAB_BLK0[d] + SLAB_BLOCKS <= AL_COLS // LANES for d in range(N_DEV))
VMEM_LIMIT = 58 * 1024 * 1024
ADAM_LR, ADAM_B1, ADAM_B2, ADAM_EPS, ADAM_WD, ADAM_STEP = 0.001, 0.9, 0.999, 1e-08, 0.01, 10
ANY = pl.BlockSpec(memory_space=pl.ANY)
MESH = pl.DeviceIdType.MESH


def _call(body, name, **kw):
    return pl.pallas_call(body, name=name, **kw)


def _params(sem=None):
    return pltpu.CompilerParams(dimension_semantics=sem, vmem_limit_bytes=VMEM_LIMIT)


def _mm(a, b):
    return jnp.dot(a, b, preferred_element_type=F32)


def _mm_nt(a, b):
    return lax.dot_general(a, b, (((1,), (1,)), ((), ())), preferred_element_type=F32)


def _mm_tn(a, b):
    return lax.dot_general(a, b, (((0,), (0,)), ((), ())), preferred_element_type=F32)


def _sigmoid(x):
    return jax.nn.sigmoid(x)


def _rope(t, cos, sin):
    half = t.shape[-1] // 2
    t1, t2 = t[:, :half], t[:, half:]
    return jnp.concatenate([t1 * cos - t2 * sin, t2 * cos + t1 * sin], axis=-1)


def _rope_bwd(g, cos, sin):
    half = g.shape[-1] // 2
    g1, g2 = g[:, :half], g[:, half:]
    return jnp.concatenate([g1 * cos + g2 * sin, g2 * cos - g1 * sin], axis=-1)


def _row_mean(x):
    return jnp.mean(x, axis=-1, keepdims=True)


def _col_sum(x):
    return jnp.sum(x, axis=0, keepdims=True)


def _tile_rows(head_ref, x_ref):
    return jnp.where(pl.program_id(0) == 0, head_ref[...], x_ref[...])


def _head_spec():
    return pl.BlockSpec((TILE, D_MODEL), lambda i: (0, 0))


def _x_spec():
    return pl.BlockSpec((TILE, D_MODEL), lambda i: (jnp.maximum(i - 1, 0), 0))


def _slab_plan():
    interior, shared = [], []
    for d in range(N_DEV):
        lo, hi = -(-SLAB_BOUND[d] // LANES), SLAB_BOUND[d + 1] // LANES
        interior.append((d, LANES * (lo - SLAB_BLK0[d]), LANES * lo, LANES * (hi - lo)))
        if d + 1 < N_DEV and SLAB_BOUND[d + 1] % LANES:
            shared.append((hi, d, hi - SLAB_BLK0[d]))
    return interior, shared


N_BLOCKS = AL_COLS // LANES
HALF_BLOCKS = SLAB_BLOCKS // 2
HALF_W = SLAB_W // 2


def _half_blocks():
    interior, _ = _slab_plan()
    first = [dst // LANES + j for _, src, dst, width in interior for j in range(width // LANES) if src // LANES + j < HALF_BLOCKS]
    return first, [b for b in range(N_BLOCKS) if b not in first]


def _w_scratch(n_blocks=N_BLOCKS):
    return [pltpu.VMEM((D_MODEL, LANES * n_blocks), BF16), pltpu.VMEM((D_MODEL, LANES), BF16),
            pltpu.VMEM((2 * (N_DEV - 1), D_MODEL, LANES), BF16), pltpu.SemaphoreType.DMA((4 * N_DEV,))]


W_SCRATCH = _w_scratch


def _slab_cols(halves, d, lo, n):
    out = []
    for k, half in enumerate(halves):
        a, b = max(lo, k * HALF_W), min(lo + n, (k + 1) * HALF_W)
        if a < b:
            out.append((half.at[d, :, pl.ds(a - k * HALF_W, b - a)], a - lo, b - a))
    return out


def _load_weight(halves, wg_hbm, w_vm, wg_vm, edge_vm, sem, blocks=None):
    blocks = list(range(N_BLOCKS) if blocks is None else blocks)
    place = {b: i for i, b in enumerate(blocks)}
    interior, shared = _slab_plan()
    copies = [] if wg_hbm is None else [(wg_hbm, wg_vm)]
    for d, src, dst, width in interior:
        b0 = dst // LANES
        runs = []
        for b in range(b0, b0 + width // LANES):
            if b in place and runs and b == sum(runs[-1]):
                runs[-1][1] += 1
            elif b in place:
                runs.append([b, 1])
        for b, n in runs:
            for piece, off, w in _slab_cols(halves, d, src + LANES * (b - b0), LANES * n):
                copies.append((piece, w_vm.at[:, pl.ds(LANES * place[b] + off, w)]))
    edges = []
    for blk, d, j in shared:
        if blk in place:
            ((low, _, _),), ((high, _, _),) = _slab_cols(halves, d, LANES * j, LANES), _slab_cols(halves, d + 1, 0, LANES)
            copies += [(low, edge_vm.at[2 * len(edges)]), (high, edge_vm.at[2 * len(edges) + 1])]
            edges.append(blk)
    copies = [pltpu.make_async_copy(a, b, sem.at[i]) for i, (a, b) in enumerate(copies)]
    for cp in copies:
        cp.start()
    for cp in copies:
        cp.wait()
    for n, blk in enumerate(edges):
        w_vm[:, LANES * place[blk]:LANES * (place[blk] + 1)] = edge_vm[2 * n] + edge_vm[2 * n + 1]


def _proj_specs(names, n_units, where):
    specs = []
    for name in names:
        s = SEG_NAMES.index(name)
        nblk = SEG_W[s] // n_units // LANES
        base = SEG_OFF[s] // LANES
        assert base % nblk == 0
        specs.append(pl.BlockSpec((nblk, TILE, LANES), lambda *g, base=base, nblk=nblk: (base // nblk + where(*g)[0], where(*g)[1], 0)))
    return specs


def _cols(ref, unit=0, n_units=1):
    n = ref.shape[0] // n_units
    return ref[unit * n] if n == 1 else jnp.concatenate([ref[unit * n + j] for j in range(n)], axis=1)


def _prenorm(head_ref, x_ref, g_ref):
    x = _tile_rows(head_ref, x_ref)
    r = lax.rsqrt(_row_mean(x * x) + EPS)
    return (x * r * g_ref[...]).astype(BF16).astype(F32)


def _project(u, w_vm, n, store):
    cuts = [8 * i for i in range(max(n // 8, 1))] + [n]
    for lo, hi in zip(cuts[:-1], cuts[1:]):
        res = _mm(u, w_vm[:, LANES * lo:LANES * hi]).astype(BF16)
        for j in range(lo, hi):
            store(j, res[:, LANES * (j - lo):LANES * (j - lo + 1)])


def _inproj_first(head, x, g_norm, first_halves, slab):
    t_rows = x.shape[0] + TILE
    nt = t_rows // TILE
    first, _ = _half_blocks()
    n = len(first)

    def body(head_ref, x_ref, g_ref, first_hbm, slab_hbm, proj_ref, gathered, w_vm, wg_vm, edge_vm, sem, *sems):
        gather = _RelayGather([slab_hbm.at[:, pl.ds(HALF_W, HALF_W)]], [gathered], sems)
        for step, stages in ((0, (0,)), (nt // 3, (1, 2)), (2 * nt // 3, (3,)), (nt - 1, (4,))):
            @pl.when(pl.program_id(0) == step)
            def _():
                for s in stages:
                    gather.stage(s)

        @pl.when(pl.program_id(0) == 0)
        def _():
            _load_weight((first_hbm,), None, w_vm, wg_vm, edge_vm, sem, first)

        def store(i, block):
            proj_ref[i] = block
        _project(_prenorm(head_ref, x_ref, g_ref).astype(BF16), w_vm, n, store)

    return _call(
        body, "inproj_fwd_first", grid=(nt,),
        out_shape=[jax.ShapeDtypeStruct((n, t_rows, LANES), BF16), jax.ShapeDtypeStruct((N_DEV, D_MODEL, HALF_W), BF16)],
        in_specs=[_head_spec(), _x_spec(), pl.BlockSpec((1, D_MODEL), lambda i: (0, 0)), ANY, ANY],
        out_specs=[pl.BlockSpec((n, TILE, LANES), lambda i: (0, i, 0)), ANY],
        scratch_shapes=_w_scratch(n) + _gather_sems(1), compiler_params=_params(("arbitrary",)),
    )(head, x, g_norm, first_halves, slab)


def _inproj_tiles(head, x, g_norm, halves, w_glr, proj_first):
    t_rows = x.shape[0] + TILE
    nt = t_rows // TILE
    first, rest = _half_blocks()

    runs = []
    for i, b in enumerate(first):
        if runs and b == runs[-1][1] + runs[-1][2]:
            runs[-1][2] += 1
        else:
            runs.append([i, b, 1])

    def body(head_ref, x_ref, g_ref, first_hbm, second_hbm, wg_hbm, pf_hbm, ut_ref, proj_ref, glr_ref, w_vm, wg_vm, edge_vm, sem, move_sem):
        rows = pl.ds(pl.multiple_of(pl.program_id(0) * TILE, TILE), TILE)
        moves = [pltpu.make_async_copy(pf_hbm.at[pl.ds(i, n), rows, :], proj_ref.at[pl.ds(b, n)], move_sem.at[k])
                 for k, (i, b, n) in enumerate(runs)]
        for cp in moves:
            cp.start()

        @pl.when(pl.program_id(0) == 0)
        def _():
            _load_weight((first_hbm, second_hbm), wg_hbm, w_vm, wg_vm, edge_vm, sem, rest)

        u32 = _prenorm(head_ref, x_ref, g_ref)
        u = u32.astype(BF16)
        ut_ref[...] = u32.T.astype(BF16)

        def store(i, block):
            proj_ref[rest[i]] = block
        _project(u, w_vm, len(rest), store)
        glr_ref[...] = _mm(u, wg_vm[...])
        for cp in moves:
            cp.wait()

    return _call(
        body, "inproj_fwd_tiles", grid=(nt,),
        out_shape=[jax.ShapeDtypeStruct((nt, D_MODEL, TILE), BF16), jax.ShapeDtypeStruct((N_BLOCKS, t_rows, LANES), BF16),
                   jax.ShapeDtypeStruct((t_rows, LANES), F32)],
        in_specs=[_head_spec(), _x_spec(), pl.BlockSpec((1, D_MODEL), lambda i: (0, 0)), ANY, ANY, ANY, ANY],
        out_specs=[pl.BlockSpec((None, D_MODEL, TILE), lambda i: (i, 0, 0)), pl.BlockSpec((N_BLOCKS, TILE, LANES), lambda i: (0, i, 0)),
                   pl.BlockSpec((TILE, LANES), lambda i: (i, 0))],
        scratch_shapes=_w_scratch(len(rest)) + [pltpu.SemaphoreType.DMA((len(runs),))], compiler_params=_params(("arbitrary",)),
    )(head, x, g_norm, *halves, w_glr, proj_first)


def _ret_decay(lgh):
    i = lax.broadcasted_iota(jnp.int32, (TILE, TILE), 0)
    j = lax.broadcasted_iota(jnp.int32, (TILE, TILE), 1)
    rel = (i - j).astype(F32)
    return jnp.where(rel >= 0, jnp.exp(jnp.maximum(rel, 0.0) * lgh), 0.0)


def _ret_vectors(lgh):
    idx = lax.broadcasted_iota(jnp.int32, (TILE, 1), 0).astype(F32)
    xi = jnp.exp((idx + 1.0) * lgh)
    zeta = jnp.exp((TILE - 1.0 - idx) * lgh)
    gc = jnp.exp(jnp.full((1, 1), float(TILE), F32) * lgh)
    return xi, zeta, gc


def _rope_tables(nt):
    half = RET_QK // 2
    inv = ROPE_BASE ** (-jnp.arange(half, dtype=F32) / half)
    base = (jnp.arange(nt, dtype=F32) * TILE - float(PAD_ROWS))[:, None, None] * inv[None, None, :]
    off = jnp.arange(TILE, dtype=F32)[:, None] * inv[None, :]
    return jnp.cos(base), jnp.sin(base), jnp.cos(off), jnp.sin(off)


def _rope_specs(tile_of):
    return [pl.BlockSpec((None, 1, RET_QK // 2), lambda i: (tile_of(i), 0, 0))] * 2 + [pl.BlockSpec((TILE, RET_QK // 2), lambda i: (0, 0))] * 2


def _rope_angles(cb_ref, sb_ref, co_ref, so_ref):
    cb, sb, co, so = cb_ref[...], sb_ref[...], co_ref[...], so_ref[...]
    return cb * co - sb * so, sb * co + cb * so


def _ret_fwd(proj, rope, gain, lg, row_shards):
    t_rows = proj.shape[1]
    nt = t_rows // TILE
    ns = len(row_shards)

    def body(lg_ref, q_ref, k_ref, v_ref, g_ref, cb_ref, sb_ref, co_ref, so_ref, gain_ref, *rest):
        shard_refs, (oraw_ref, oret_ref, st_ref), gathered = rest[:ns], rest[ns:ns + 3], rest[ns + 3:2 * ns + 3]
        s_acc, dm = rest[2 * ns + 3:2 * ns + 5]
        gather = _Exchange(shard_refs, gathered, rest[2 * ns + 5:], among_chips=False)
        t = pl.program_id(0)

        @pl.when(t == 0)
        def _():
            gather.start()
            s_acc[...] = jnp.zeros_like(s_acc)
            for h in range(RET_HEADS):
                dm[h] = _ret_decay(lg_ref[h])

        @pl.when(t == nt - 1)
        def _():
            gather.finish()

        cos_t, sin_t = _rope_angles(cb_ref, sb_ref, co_ref, so_ref)
        for h in range(RET_HEADS):
            lgh = lg_ref[h]
            q = _rope(_cols(q_ref, h, RET_HEADS).astype(F32), cos_t, sin_t)
            k = _rope(_cols(k_ref, h, RET_HEADS).astype(F32), cos_t, sin_t) * (RET_QK ** -0.5)
            xi, zeta, gc = _ret_vectors(lgh)
            v = _cols(v_ref, h, RET_HEADS)
            s_in = s_acc[h]
            p = (_mm_nt(q.astype(BF16), k.astype(BF16)) * dm[h]).astype(BF16)
            o = _mm(p, v) + _mm((q * xi).astype(BF16), s_in.astype(BF16))
            st_ref[h] = s_in.astype(BF16)
            s_acc[h] = s_in * gc + _mm_tn((k * zeta).astype(BF16), v)
            cols = slice(h * RET_V, (h + 1) * RET_V)
            oraw_ref[:, cols] = o
            oc = o - _row_mean(o)
            n = oc * lax.rsqrt(_row_mean(oc * oc) + EPS) * gain_ref[:, cols]
            g = _cols(g_ref, h, RET_HEADS).astype(F32)
            oret_ref[:, cols] = (n * g * _sigmoid(g)).astype(BF16)

    row = lambda w: pl.BlockSpec((TILE, w), lambda t: (t, 0))
    outs = _call(
        body, "ret_fwd", grid=(nt,),
        out_shape=[jax.ShapeDtypeStruct((t_rows, RET_W), F32), jax.ShapeDtypeStruct((t_rows, RET_W), BF16),
                   jax.ShapeDtypeStruct((RET_HEADS, nt, RET_QK, RET_V), BF16)]
                  + [jax.ShapeDtypeStruct((N_DEV, *a.shape), a.dtype) for a in row_shards],
        in_specs=[pl.BlockSpec(memory_space=pltpu.SMEM)] + _proj_specs(("rq", "rk", "rv", "rg"), 1, lambda t: (0, t)) + _rope_specs(lambda t: t) + [
                  pl.BlockSpec((1, RET_W), lambda t: (0, 0))] + [ANY] * ns,
        out_specs=[row(RET_W), row(RET_W), pl.BlockSpec((RET_HEADS, None, RET_QK, RET_V), lambda t: (0, t, 0, 0))] + [ANY] * ns,
        scratch_shapes=[pltpu.VMEM((RET_HEADS, RET_QK, RET_V), F32), pltpu.VMEM((RET_HEADS, TILE, TILE), F32)] + _exchange_sems(ns, N_DEV),
        compiler_params=_params(("arbitrary",)),
    )(lg, proj, proj, proj, proj, *rope, gain, *row_shards)
    return outs[0], outs[1], outs[2], outs[3:]


def _ret_bwd(proj, rope, gain, lg, o_raw, do_ret, states):
    t_rows = proj.shape[1]
    nt = t_rows // TILE

    def body(lg_ref, q_ref, k_ref, v_ref, g_ref, cb_ref, sb_ref, co_ref, so_ref, gain_ref, oraw_ref, do_ref, st_ref,
             dq_ref, dk_ref, dv_ref, dg_ref, dgain_ref, e_acc, dm):
        @pl.when(pl.program_id(0) == 0)
        def _():
            e_acc[...] = jnp.zeros_like(e_acc)
            for h in range(RET_HEADS):
                dm[h] = _ret_decay(lg_ref[h])
            dgain_ref[...] = jnp.zeros_like(dgain_ref)

        cos_t, sin_t = _rope_angles(cb_ref, sb_ref, co_ref, so_ref)
        for h in range(RET_HEADS):
            lgh = lg_ref[h]
            cols = slice(h * RET_V, (h + 1) * RET_V)
            qcols = slice(h * RET_QK, (h + 1) * RET_QK)
            q = _rope(_cols(q_ref, h, RET_HEADS).astype(F32), cos_t, sin_t)
            k = _rope(_cols(k_ref, h, RET_HEADS).astype(F32), cos_t, sin_t) * (RET_QK ** -0.5)
            xi, zeta, gc = _ret_vectors(lgh)
            v = _cols(v_ref, h, RET_HEADS)
            g = _cols(g_ref, h, RET_HEADS).astype(F32)
            o = oraw_ref[:, cols]
            do = do_ref[:, cols].astype(F32)
            oc = o - _row_mean(o)
            rstd = lax.rsqrt(_row_mean(oc * oc) + EPS)
            xh = oc * rstd
            gain_t = gain_ref[:, cols]
            sg = _sigmoid(g)
            dn = do * (g * sg)
            dg_ref[:, cols] = (do * (xh * gain_t) * (sg * (1.0 + g * (1.0 - sg)))).astype(BF16)
            dgain_ref[:, cols] += _col_sum(dn * xh)
            dxh = dn * gain_t
            dob = (rstd * (dxh - _row_mean(dxh) - xh * _row_mean(dxh * xh))).astype(BF16)
            dmat = dm[h]
            qb, kb = q.astype(BF16), k.astype(BF16)
            p = (_mm_nt(qb, kb) * dmat).astype(BF16)
            dp = (_mm_nt(dob, v) * dmat).astype(BF16)
            s_in = st_ref[h]
            e_in = e_acc[h]
            e_b = e_in.astype(BF16)
            dq = _mm(dp, kb) + _mm_nt(dob, s_in) * xi
            dk = _mm_tn(dp, qb) + _mm_nt(v, e_b) * zeta
            dv_ref[:, cols] = (_mm_tn(p, dob) + _mm((k * zeta).astype(BF16), e_b)).astype(BF16)
            e_acc[h] = e_in * gc + _mm_tn((q * xi).astype(BF16), dob)
            dq_ref[:, qcols] = _rope_bwd(dq, cos_t, sin_t).astype(BF16)
            dk_ref[:, qcols] = (_rope_bwd(dk, cos_t, sin_t) * (RET_QK ** -0.5)).astype(BF16)

    row = lambda w: pl.BlockSpec((TILE, w), lambda j: (nt - 1 - j, 0))
    vec = pl.BlockSpec((1, RET_W), lambda j: (0, 0))
    return _call(
        body, "ret_bwd", grid=(nt,),
        out_shape=[jax.ShapeDtypeStruct((t_rows, RET_HEADS * RET_QK), BF16), jax.ShapeDtypeStruct((t_rows, RET_HEADS * RET_QK), BF16),
                   jax.ShapeDtypeStruct((t_rows, RET_W), BF16), jax.ShapeDtypeStruct((t_rows, RET_W), BF16),
                   jax.ShapeDtypeStruct((1, RET_W), F32)],
        in_specs=[pl.BlockSpec(memory_space=pltpu.SMEM)] + _proj_specs(("rq", "rk", "rv", "rg"), 1, lambda j: (0, nt - 1 - j)) + _rope_specs(lambda j: nt - 1 - j) + [vec,
                  row(RET_W), row(RET_W), pl.BlockSpec((RET_HEADS, None, RET_QK, RET_V), lambda j: (0, nt - 1 - j, 0, 0))],
        out_specs=[row(RET_HEADS * RET_QK), row(RET_HEADS * RET_QK), row(RET_W), row(RET_W), vec],
        scratch_shapes=[pltpu.VMEM((RET_HEADS, RET_QK, RET_V), F32), pltpu.VMEM((RET_HEADS, TILE, TILE), F32)],
        compiler_params=_params(("arbitrary",)),
    )(lg, proj, proj, proj, proj, *rope, gain, o_raw, do_ret, states)


GLA_LEVELS = (32, 64, 128, 256)
N_TERMS = 1 + len(GLA_LEVELS)


def _gla_tables():
    p = jnp.arange(TILE)[:, None]
    r = jnp.arange(TILE)[None, :]
    masks = [(p // GLA_CHUNK == r // GLA_CHUNK) & (r <= p)]
    for blk in GLA_LEVELS:
        masks.append((p // blk == r // blk) & (p % blk >= blk // 2) & (r % blk < blk // 2))
    masks = jnp.stack(masks + [m.T for m in masks]).astype(F32)
    cum_fwd = jnp.concatenate([r <= p, masks[0] > 0], axis=0).astype(BF16)
    cum_bwd = jnp.concatenate([r >= p, masks[N_TERMS] > 0], axis=1).astype(BF16)
    return masks, cum_fwd, cum_bwd


def _split3(x):
    hi = x.astype(BF16)
    rest = x - hi.astype(F32)
    mid = rest.astype(BF16)
    lo = (rest - mid.astype(F32)).astype(BF16)
    return jnp.concatenate([hi, mid, lo], axis=1)


def _join3(y):
    w = y.shape[1] // 3
    return (y[:, 2 * w:] + y[:, w:2 * w]) + y[:, :w]


def _gla_decays(glr_ref, wgu_ref, b_ref, cum_ref):
    z = _mm(glr_ref[...].astype(BF16), wgu_ref[...].astype(BF16)) + b_ref[...]
    la = (jnp.minimum(z, 0.0) - jnp.log(1.0 + jnp.exp(-jnp.abs(z)))) / GLA_TAU
    width = la.shape[1]
    hi = la.astype(BF16)
    rest = la - hi.astype(F32)
    mid = rest.astype(BF16)
    lo = (rest - mid.astype(F32)).astype(BF16)
    y = _mm(cum_ref[...], jnp.concatenate([hi, mid, lo], axis=1))
    gb = (y[:, 2 * width:] + y[:, width:2 * width]) + y[:, :width]
    return z, gb[:TILE], gb[TILE:]


def _gla_prep(h, q_ref, k_ref, g_all, b_all, g_scr, ref_scr):
    cols = slice(h * GLA_K, (h + 1) * GLA_K)
    g, b = g_all[:, cols], b_all[:, cols]
    g_scr[h] = g
    factors = [(jnp.exp(b), jnp.exp(-b))]
    for lvl, blk in enumerate(GLA_LEVELS):
        for n in range(TILE // blk):
            ref_scr[h, lvl, n * blk:(n + 1) * blk, :] = jnp.broadcast_to(g_scr[h, pl.ds(n * blk + blk // 2 - 1, 1), :], (blk, GLA_K))
        x = g - ref_scr[h, lvl]
        factors.append((jnp.exp(jnp.minimum(x, 0.0)), jnp.exp(jnp.minimum(-x, 0.0))))
    g_last = g_scr[h, pl.ds(TILE - 1, 1), :]
    q = _cols(q_ref, h, GLA_HEADS).astype(F32) * (GLA_K ** -0.5)
    k = _cols(k_ref, h, GLA_HEADS).astype(F32)
    return q, k, factors, jnp.exp(g), jnp.exp(g_last), jnp.exp(g_last - g)


def _gla_scores(q, k, factors, m_ref):
    a = jnp.zeros((TILE, TILE), F32)
    for l, (fq, fk) in enumerate(factors):
        s = _mm_nt((q * fq).astype(BF16), (k * fk).astype(BF16))
        a = jnp.where(m_ref[l] > 0.0, s, a)
    return a


def _gla_fwd(proj, glr, wgu_pad, b_gate, gain, masks, cum_fwd):
    t_rows = glr.shape[0]
    nt = t_rows // TILE

    def body(q_ref, k_ref, v_ref, g_ref, glr_ref, wgu_ref, b_ref, gain_ref, m_ref, cum_ref, oraw_ref, ogla_ref, st_ref, at_ref,
             s_acc, g_scr, ref_scr):
        @pl.when(pl.program_id(0) == 0)
        def _():
            s_acc[...] = jnp.zeros_like(s_acc)

        _, g_all, b_all = _gla_decays(glr_ref, wgu_ref, b_ref, cum_ref)
        for h in range(GLA_HEADS):
            q, k, factors, e_g, e_last, e_end = _gla_prep(h, q_ref, k_ref, g_all, b_all, g_scr, ref_scr)
            v = _cols(v_ref, h, GLA_HEADS)
            st = s_acc[h]
            st_ref[h] = st
            a = _gla_scores(q, k, factors, m_ref)
            at_ref[h] = a.T.astype(BF16)
            o = _mm(a.astype(BF16), v) + _mm_nt((q * e_g).astype(BF16), st.astype(BF16))
            s_acc[h] = st * e_last + _mm(v.astype(F32).T.astype(BF16), (k * e_end).astype(BF16))
            cols = slice(h * GLA_V, (h + 1) * GLA_V)
            oraw_ref[:, cols] = o
            n = o * lax.rsqrt(_row_mean(o * o) + EPS) * gain_ref[:, cols]
            g = _cols(g_ref, h, GLA_HEADS).astype(F32)
            ogla_ref[:, cols] = (n * g * _sigmoid(g)).astype(BF16)

    row = lambda w: pl.BlockSpec((TILE, w), lambda t: (t, 0))
    whole = lambda *shape: pl.BlockSpec(shape, lambda t: (0,) * len(shape))
    return _call(
        body, "gla_fwd", grid=(nt,),
        out_shape=[jax.ShapeDtypeStruct((t_rows, GLA_W), F32), jax.ShapeDtypeStruct((t_rows, GLA_W), BF16),
                   jax.ShapeDtypeStruct((GLA_HEADS, nt, GLA_V, GLA_K), F32), jax.ShapeDtypeStruct((GLA_HEADS, t_rows, TILE), BF16)],
        in_specs=_proj_specs(("gq", "gk", "gv", "gg"), 1, lambda t: (0, t)) + [row(LANES), whole(LANES, GLA_HEADS * GLA_K),
                  whole(1, GLA_HEADS * GLA_K), whole(1, GLA_W), whole(N_TERMS, TILE, TILE), whole(2 * TILE, TILE)],
        out_specs=[row(GLA_W), row(GLA_W), pl.BlockSpec((GLA_HEADS, None, GLA_V, GLA_K), lambda t: (0, t, 0, 0)),
                   pl.BlockSpec((GLA_HEADS, TILE, TILE), lambda t: (0, t, 0))],
        scratch_shapes=[pltpu.VMEM((GLA_HEADS, GLA_V, GLA_K), F32), pltpu.VMEM((GLA_HEADS, TILE, GLA_K), F32),
                        pltpu.VMEM((GLA_HEADS, len(GLA_LEVELS), TILE, GLA_K), F32)],
        compiler_params=_params(("arbitrary",)),
    )(proj, proj, proj, proj, glr, wgu_pad, b_gate, gain, masks, cum_fwd)


def _gla_bwd(proj, glr, wgu_pad, b_gate, gain, o_raw, do_gla, states, a_t, masks, cum_fwd, cum_bwd):
    t_rows = glr.shape[0]
    nt = t_rows // TILE

    def body(q_ref, k_ref, v_ref, g_ref, glr_ref, wgu_ref, b_ref, gain_ref, m_ref, cum_ref, cumb_ref, oraw_ref, do_ref, st_ref, at_ref,
             dq_ref, dk_ref, dv_ref, dg_ref, dglr_ref, dwgu_ref, dbg_ref, dgain_ref, d_acc, g_scr, ref_scr, dref_scr):
        @pl.when(pl.program_id(0) == 0)
        def _():
            d_acc[...] = jnp.zeros_like(d_acc)
            dwgu_ref[...] = jnp.zeros_like(dwgu_ref)
            dbg_ref[...] = jnp.zeros_like(dbg_ref)
            dgain_ref[...] = jnp.zeros_like(dgain_ref)

        z_all, g_all, b_all = _gla_decays(glr_ref, wgu_ref, b_ref, cum_ref)
        dla_parts = []
        for h in range(GLA_HEADS):
            q, k, factors, e_g, e_last, e_end = _gla_prep(h, q_ref, k_ref, g_all, b_all, g_scr, ref_scr)
            v = _cols(v_ref, h, GLA_HEADS)
            cols = slice(h * GLA_V, (h + 1) * GLA_V)
            kcols = slice(h * GLA_K, (h + 1) * GLA_K)
            o = oraw_ref[:, cols]
            do = do_ref[:, cols].astype(F32)
            g = _cols(g_ref, h, GLA_HEADS).astype(F32)
            rinv = lax.rsqrt(_row_mean(o * o) + EPS)
            nh = o * rinv
            gain_t = gain_ref[:, cols]
            sg = _sigmoid(g)
            dn = do * (g * sg)
            dg_ref[:, cols] = (do * (nh * gain_t) * (sg * (1.0 + g * (1.0 - sg)))).astype(BF16)
            dgain_ref[:, cols] += _col_sum(dn * nh)
            dnh = dn * gain_t
            dor = rinv * (dnh - nh * _row_mean(dnh * nh))
            dob = dor.astype(BF16)
            a_t = at_ref[h]
            da = _mm_nt(dob, v).astype(BF16)
            da_t = _mm_nt(v, dob).astype(BF16)
            st_in = st_ref[h]
            d_out = d_acc[h]
            d_out_b = d_out.astype(BF16)
            qg, kg = q * e_g, k * e_end
            dqg = _mm(dob, st_in.astype(BF16))
            dkg = _mm(v, d_out_b)
            dv_ref[:, cols] = (_mm(a_t, dob) + _mm_nt(kg.astype(BF16), d_out_b)).astype(BF16)
            d_acc[h] = d_out * e_last + _mm(dor.T.astype(BF16), qg.astype(BF16))
            dq = dqg * e_g
            dk = dkg * e_end
            dkg_kg = dkg * kg
            dg_cum = dqg * qg - dkg_kg
            db = None
            for l, (fq, fk) in enumerate(factors):
                qt, kt = q * fq, k * fk
                dqt = _mm(da * m_ref[l], kt.astype(BF16))
                dkt = _mm(da_t * m_ref[N_TERMS + l], qt.astype(BF16))
                dq = dq + dqt * fq
                dk = dk + dkt * fk
                diff = dqt * qt - dkt * kt
                if l == 0:
                    db = diff
                else:
                    dg_cum = dg_cum + diff
                    dref_scr[h, l - 1] = diff
            dq_ref[:, kcols] = (dq * (GLA_K ** -0.5)).astype(BF16)
            dk_ref[:, kcols] = dk.astype(BF16)
            g_scr[h] = dg_cum
            g_scr[h, pl.ds(TILE - 1, 1), :] += e_last * _col_sum(d_out * st_in) + _col_sum(dkg_kg)
            for lvl, blk in enumerate(GLA_LEVELS):
                for n in range(TILE // blk):
                    g_scr[h, pl.ds(n * blk + blk // 2 - 1, 1), :] -= _col_sum(dref_scr[h, lvl, n * blk:(n + 1) * blk, :])
            dla_parts.append(_join3(_mm(cumb_ref[...], jnp.concatenate([_split3(g_scr[h]), _split3(db)], axis=0))))
        dz = jnp.concatenate(dla_parts, axis=1) * (1.0 / GLA_TAU) * _sigmoid(-z_all)
        dzb = dz.astype(BF16)
        wgu_b = wgu_ref[...].astype(BF16)
        for h in range(GLA_HEADS):
            kcols = slice(h * GLA_K, (h + 1) * GLA_K)
            dglr_ref[h] = _mm_nt(dzb[:, kcols], wgu_b[:, kcols]).astype(BF16)
        dwgu_ref[...] += _mm(glr_ref[...].T.astype(BF16), dzb)
        dbg_ref[...] += _col_sum(dz)

    row = lambda w: pl.BlockSpec((TILE, w), lambda j: (nt - 1 - j, 0))
    whole = lambda *shape: pl.BlockSpec(shape, lambda j: (0,) * len(shape))
    return _call(
        body, "gla_bwd", grid=(nt,),
        out_shape=[jax.ShapeDtypeStruct((t_rows, GLA_HEADS * GLA_K), BF16), jax.ShapeDtypeStruct((t_rows, GLA_HEADS * GLA_K), BF16),
                   jax.ShapeDtypeStruct((t_rows, GLA_W), BF16), jax.ShapeDtypeStruct((t_rows, GLA_W), BF16),
                   jax.ShapeDtypeStruct((GLA_HEADS, t_rows, LANES), BF16), jax.ShapeDtypeStruct((LANES, GLA_HEADS * GLA_K), F32),
                   jax.ShapeDtypeStruct((1, GLA_HEADS * GLA_K), F32), jax.ShapeDtypeStruct((1, GLA_W), F32)],
        in_specs=_proj_specs(("gq", "gk", "gv", "gg"), 1, lambda j: (0, nt - 1 - j)) + [row(LANES),
                  whole(LANES, GLA_HEADS * GLA_K), whole(1, GLA_HEADS * GLA_K), whole(1, GLA_W),
                  whole(2 * N_TERMS, TILE, TILE), whole(2 * TILE, TILE), whole(TILE, 2 * TILE), row(GLA_W), row(GLA_W),
                  pl.BlockSpec((GLA_HEADS, None, GLA_V, GLA_K), lambda j: (0, nt - 1 - j, 0, 0)),
                  pl.BlockSpec((GLA_HEADS, TILE, TILE), lambda j: (0, nt - 1 - j, 0))],
        out_specs=[row(GLA_HEADS * GLA_K), row(GLA_HEADS * GLA_K), row(GLA_W), row(GLA_W),
                   pl.BlockSpec((GLA_HEADS, TILE, LANES), lambda j: (0, nt - 1 - j, 0)), whole(LANES, GLA_HEADS * GLA_K),
                   whole(1, GLA_HEADS * GLA_K), whole(1, GLA_W)],
        scratch_shapes=[pltpu.VMEM((GLA_HEADS, GLA_V, GLA_K), F32), pltpu.VMEM((GLA_HEADS, TILE, GLA_K), F32),
                        pltpu.VMEM((GLA_HEADS, len(GLA_LEVELS), TILE, GLA_K), F32),
                        pltpu.VMEM((GLA_HEADS, len(GLA_LEVELS), TILE, GLA_K), F32)],
        compiler_params=_params(("arbitrary",)),
    )(proj, proj, proj, proj, glr, wgu_pad, b_gate, gain, masks.astype(BF16), cum_fwd, cum_bwd, o_raw, do_gla, states, a_t)


def _merge_fwd_bwd(o_ret, o_gla, proj, x, target, g_final, w_br, w_bg, w_out):
    t_rows = x.shape[0] + TILE
    nt = t_rows // TILE

    def body(oret_ref, ogla_ref, mr_ref, mg_ref, h0_ref, tgt_ref, gf_ref, wbr_hbm, wbg_hbm, wout_hbm,
             dh1_ref, dmr_ref, dmg_ref, doret_ref, dogla_ref, loss_ref, dgf_ref, dwbr_hbm, dwbg_hbm, dwout_hbm,
             wbr, wbg, wout, abr, abg, aout, sem):
        i = pl.program_id(0)

        @pl.when(i == 0)
        def _():
            cps = [pltpu.make_async_copy(s, d, sem.at[n]) for n, (s, d) in enumerate(((wbr_hbm, wbr), (wbg_hbm, wbg), (wout_hbm, wout)))]
            for cp in cps:
                cp.start()
            abr[...] = jnp.zeros_like(abr)
            abg[...] = jnp.zeros_like(abg)
            aout[...] = jnp.zeros_like(aout)
            loss_ref[...] = jnp.zeros_like(loss_ref)
            dgf_ref[...] = jnp.zeros_like(dgf_ref)
            for cp in cps:
                cp.wait()
            dh1_ref[...] = jnp.zeros_like(dh1_ref)
            dmr_ref[...] = jnp.zeros_like(dmr_ref)
            dmg_ref[...] = jnp.zeros_like(dmg_ref)
            doret_ref[...] = jnp.zeros_like(doret_ref)
            dogla_ref[...] = jnp.zeros_like(dogla_ref)

        @pl.when(i > 0)
        def _():
            oret, ogla = oret_ref[...], ogla_ref[...]
            br, bg = _mm(oret, wbr[...]), _mm(ogla, wbg[...])
            sr, sg = _sigmoid(_cols(mr_ref).astype(F32)), _sigmoid(_cols(mg_ref).astype(F32))
            mb = (sr * br + sg * bg).astype(BF16)
            h1 = h0_ref[...] + _mm(mb, wout[...])
            r2 = lax.rsqrt(_row_mean(h1 * h1) + EPS)
            hn = h1 * r2
            gf = gf_ref[...]
            diff = hn * gf - tgt_ref[...]
            loss_ref[...] += 0.5 * jnp.sum(_row_mean(diff * diff))
            dy = diff * (1.0 / D_MODEL)
            dgf_ref[...] += _col_sum(dy * hn)
            dyg = dy * gf
            dh1 = r2 * (dyg - hn * _row_mean(dyg * hn))
            dh1_ref[...] = dh1
            dh1b = dh1.astype(BF16)
            dm = _mm_nt(dh1b, wout[...])
            aout[...] += _mm_tn(mb, dh1b)
            dbr = (dm * sr).astype(BF16)
            dbg = (dm * sg).astype(BF16)
            dmr_ref[...] = (dm * br * sr * (1.0 - sr)).astype(BF16)
            dmg_ref[...] = (dm * bg * sg * (1.0 - sg)).astype(BF16)
            doret_ref[...] = _mm_nt(dbr, wbr[...]).astype(BF16)
            dogla_ref[...] = _mm_nt(dbg, wbg[...]).astype(BF16)
            abr[...] += _mm_tn(oret, dbr)
            abg[...] += _mm_tn(ogla, dbg)

        @pl.when(i == nt - 1)
        def _():
            wbr[...] = abr[...].astype(BF16)
            wbg[...] = abg[...].astype(BF16)
            wout[...] = aout[...].astype(BF16)
            pltpu.sync_copy(wbr, dwbr_hbm)
            pltpu.sync_copy(wbg, dwbg_hbm)
            pltpu.sync_copy(wout, dwout_hbm)

    row = lambda w: pl.BlockSpec((TILE, w), lambda i: (i, 0))
    one = lambda w: pl.BlockSpec((1, w), lambda i: (0, 0))
    return _call(
        body, "merge_fwd_bwd", grid=(nt,),
        out_shape=[jax.ShapeDtypeStruct((t_rows, D_MODEL), F32), jax.ShapeDtypeStruct((t_rows, D_MODEL), BF16),
                   jax.ShapeDtypeStruct((t_rows, D_MODEL), BF16), jax.ShapeDtypeStruct((t_rows, RET_W), BF16),
                   jax.ShapeDtypeStruct((t_rows, GLA_W), BF16), jax.ShapeDtypeStruct((1, LANES), F32),
                   jax.ShapeDtypeStruct((1, D_MODEL), F32), jax.ShapeDtypeStruct((RET_W, D_MODEL), BF16),
                   jax.ShapeDtypeStruct((GLA_W, D_MODEL), BF16), jax.ShapeDtypeStruct((D_MODEL, D_MODEL), BF16)],
        in_specs=[row(RET_W), row(GLA_W)] + _proj_specs(("mr", "mg"), 1, lambda i: (0, i)) + [_x_spec(), _x_spec(), one(D_MODEL), ANY, ANY, ANY],
        out_specs=[row(D_MODEL), row(D_MODEL), row(D_MODEL), row(RET_W), row(GLA_W), one(LANES), one(D_MODEL), ANY, ANY, ANY],
        scratch_shapes=[pltpu.VMEM((RET_W, D_MODEL), BF16), pltpu.VMEM((GLA_W, D_MODEL), BF16), pltpu.VMEM((D_MODEL, D_MODEL), BF16),
                        pltpu.VMEM((RET_W, D_MODEL), F32), pltpu.VMEM((GLA_W, D_MODEL), F32), pltpu.VMEM((D_MODEL, D_MODEL), F32),
                        pltpu.SemaphoreType.DMA((3,))],
        compiler_params=_params(("arbitrary",)),
    )(o_ret, o_gla, proj, proj, x, target, g_final, w_br, w_bg, w_out)


def _inproj_bwd_x(dseg, dglr, head, x, dh1, g_norm, slabs, w_glr, chip_partials):
    t_rows = x.shape[0] + TILE
    nt = t_rows // TILE
    ne = len(chip_partials)

    def body(*refs):
        d_refs = refs[:10]
        dglr_ref, head_ref, x_ref, dh1_ref, g_ref, slabs_a, slabs_b, wg_hbm = refs[10:18]
        part_refs = refs[18:18 + ne]
        dx_ref, dhead_ref, dgn_ref = refs[18 + ne:21 + ne]
        landed = refs[21 + ne:21 + 2 * ne]
        w_vm, wg_vm, edge_vm, sem = refs[21 + 2 * ne:25 + 2 * ne]
        exchange = _Exchange(part_refs, landed, refs[25 + 2 * ne:], among_chips=True)

        @pl.when(pl.program_id(0) == 0)
        def _():
            exchange.start()
            dgn_ref[...] = jnp.zeros_like(dgn_ref)
            _load_weight((slabs_a, slabs_b), wg_hbm, w_vm, wg_vm, edge_vm, sem)

        @pl.when(pl.program_id(0) == nt - 1)
        def _():
            exchange.finish()

        dglr = dglr_ref[0].astype(F32)
        for h in range(1, GLA_HEADS):
            dglr = dglr + dglr_ref[h].astype(F32)
        du = _mm_nt(dglr.astype(BF16), wg_vm[...])
        for s, d_ref in enumerate(d_refs):
            du = du + _mm_nt(d_ref[...], w_vm[:, SEG_OFF[s]:SEG_OFF[s] + SEG_W[s]])
        x = _tile_rows(head_ref, x_ref)
        r = lax.rsqrt(_row_mean(x * x) + EPS)
        hn = x * r
        dgn_ref[...] += _col_sum(du * hn)
        dug = du * g_ref[...]
        dh0 = dh1_ref[...] + r * (dug - hn * _row_mean(dug * hn))
        dx_ref[...] = dh0

        @pl.when(pl.program_id(0) == 0)
        def _():
            dhead_ref[...] = dh0

    row = lambda w: pl.BlockSpec((TILE, w), lambda i: (i, 0))
    one = pl.BlockSpec((1, D_MODEL), lambda i: (0, 0))
    return _call(
        body, "inproj_bwd_x", grid=(nt,),
        out_shape=[jax.ShapeDtypeStruct((t_rows - TILE, D_MODEL), F32), jax.ShapeDtypeStruct((TILE, D_MODEL), F32),
                   jax.ShapeDtypeStruct((1, D_MODEL), F32)] + [jax.ShapeDtypeStruct(a.shape, a.dtype) for a in chip_partials],
        in_specs=[row(w) for w in SEG_W] + [pl.BlockSpec((GLA_HEADS, TILE, LANES), lambda i: (0, i, 0)),
                                            _head_spec(), _x_spec(), row(D_MODEL), one, ANY, ANY, ANY] + [ANY] * ne,
        out_specs=[_x_spec(), _head_spec(), one] + [ANY] * ne,
        scratch_shapes=W_SCRATCH() + _exchange_sems(ne, N_CHIP),
        compiler_params=_params(("arbitrary",)),
    )(*[dseg[n] for n in SEG_NAMES], dglr, head, x, dh1, g_norm, *slabs, w_glr, *chip_partials)


W_TILE = 512


def _inproj_bwd_w(ut, dseg, dglr, row_sends):
    nt = ut.shape[0]
    t_rows = nt * TILE
    kc = 3 if nt % 3 == 0 else 1
    tiles = [(s, c) for s in range(len(SEG_W)) for c in range(0, SEG_W[s], W_TILE)]
    bpt = W_TILE // LANES
    nr = len(row_sends)
    n = 1 + nr
    last_tile = [(SLAB_BLK0[d] + SLAB_BLOCKS - 1) // bpt for d in range(N_DEV)]

    def body(ut_hbm, *refs):
        d_refs, dglr_hbm, row_refs = refs[:10], refs[10], refs[11:11 + nr]
        out_hbm, oglr_ref, sib = refs[11 + nr], refs[12 + nr], refs[13 + nr:13 + nr + n]
        ut_vm, dbuf, obuf, acc, gbuf, sem, send_sems, recv_sems = refs[13 + nr + n:]
        x, y, core = _position()

        def handover(d, k, landed=False):
            q = d // 2
            src = out_hbm.at[pl.ds(SLAB_BLK0[d], SLAB_BLOCKS)] if k == 0 else row_refs[k - 1].at[d]
            return pltpu.make_async_remote_copy(src_ref=sib[k].at[q] if landed else src, dst_ref=sib[k].at[q],
                                                send_sem=send_sems.at[n * q + k], recv_sem=recv_sems.at[n * q + k],
                                                device_id=(x, y, 1 - core), device_id_type=MESH)

        def for_sibling(d, ks, fn):
            @pl.when(d % 2 != core)
            def _():
                for k in ks:
                    fn(handover(d, k))

        for d in range(N_DEV):
            for_sibling(d, range(1, n), lambda cp: cp.start())

        def fetch(i):
            s, c = tiles[i]
            return pltpu.make_async_copy(d_refs[s].at[:, pl.ds(c, W_TILE)], dbuf.at[i % 2], sem.at[1 + i % 2])

        def contract(rhs_refs, width):
            acc[:, :width] = jnp.zeros((D_MODEL, width), F32)

            def step(k, carry):
                part = None
                for j in range(kc):
                    kk = k * kc + j
                    for rhs_ref in rhs_refs:
                        prod = _mm(ut_vm[kk], rhs_ref[pl.ds(pl.multiple_of(kk * TILE, TILE), TILE), :])
                        part = prod if part is None else part + prod
                acc[:, :width] += part
                return carry

            lax.fori_loop(0, nt // kc, step, 0)
            return acc[:, :width]

        load_ut = pltpu.make_async_copy(ut_hbm, ut_vm, sem.at[0])
        load_glr = pltpu.make_async_copy(dglr_hbm, gbuf, sem.at[5])
        load_ut.start()
        load_glr.start()
        fetch(0).start()
        load_ut.wait()
        stores = {}

        def stored(i):
            stores[i].wait()
            for d in range(N_DEV):
                if last_tile[d] == i:
                    for_sibling(d, [0], lambda cp: cp.start())

        for i, (s, c) in enumerate(tiles):
            if i + 1 < len(tiles):
                fetch(i + 1).start()
            fetch(i).wait()
            if i >= 2:
                stored(i - 2)
            total = contract([dbuf.at[i % 2]], W_TILE)
            for j in range(bpt):
                obuf[i % 2, j] = total[:, j * LANES:(j + 1) * LANES].astype(BF16)
            blk0 = (SEG_OFF[s] + c) // LANES
            stores[i] = pltpu.make_async_copy(obuf.at[i % 2], out_hbm.at[pl.ds(blk0, bpt)], sem.at[3 + i % 2])
            stores[i].start()
        for i in range(max(0, len(tiles) - 2), len(tiles)):
            stored(i)
        load_glr.wait()
        head_sum = gbuf[0].astype(F32)
        for h in range(1, GLA_HEADS):
            head_sum = head_sum + gbuf[h].astype(F32)
        gbuf[0] = head_sum.astype(BF16)
        oglr_ref[...] = contract([gbuf.at[0]], LANES)
        for q in range(N_CHIP):
            for k in range(n):
                handover(2 * q, k, landed=True).wait_recv()
        for d in range(N_DEV):
            for_sibling(d, range(n), lambda cp: cp.wait_send())

    outs = _call(
        body, "inproj_bwd_w",
        out_shape=[jax.ShapeDtypeStruct((AL_COLS // LANES, D_MODEL, LANES), BF16), jax.ShapeDtypeStruct((D_MODEL, LANES), F32),
                   jax.ShapeDtypeStruct((N_CHIP, SLAB_BLOCKS, D_MODEL, LANES), BF16)]
                  + [jax.ShapeDtypeStruct((N_CHIP, *r.shape[1:]), BF16) for r in row_sends],
        in_specs=[ANY] * (12 + nr), out_specs=[ANY, pl.BlockSpec(memory_space=pltpu.VMEM)] + [ANY] * n,
        scratch_shapes=[pltpu.VMEM((nt, D_MODEL, TILE), BF16), pltpu.VMEM((2, t_rows, W_TILE), BF16),
                        pltpu.VMEM((2, bpt, D_MODEL, LANES), BF16), pltpu.VMEM((D_MODEL, W_TILE), F32),
                        pltpu.VMEM((GLA_HEADS, t_rows, LANES), BF16), pltpu.SemaphoreType.DMA((6,)),
                        pltpu.SemaphoreType.DMA((n * N_CHIP,)), pltpu.SemaphoreType.DMA((n * N_CHIP,))],
        compiler_params=_params(),
    )(ut, *[dseg[n_] for n_ in SEG_NAMES], dglr, *row_sends)
    return outs[0], outs[1], outs[2], outs[3:]


def _position():
    x, y, c = lax.axis_index("x"), lax.axis_index("y"), lax.axis_index("c")
    return x, y, c


def _index(px, py, pc):
    return 4 * px + 2 * py + pc


def _gather_sems(n):
    return [pltpu.SemaphoreType.DMA((7 * n,)), pltpu.SemaphoreType.DMA((7 * n,)), pltpu.SemaphoreType.DMA((n,))]


class _RelayGather:
    STAGES = 5

    def __init__(self, ins, outs, sems):
        self.ins, self.outs, self.n = ins, outs, len(ins)
        self.send_sems, self.recv_sems, self.local_sems = sems
        x, y, c = _position()
        self.c, self.me, self.sibling = c, (x, y, c), (x, y, 1 - c)
        self.chips = [(1 - x, y), (x, 1 - y), (1 - x, 1 - y)]

    def _copy(self, a, k, block, to, src=None):
        dst = self.outs[a].at[_index(*block)]
        return pltpu.make_async_remote_copy(src_ref=dst if src is None else src, dst_ref=dst,
                                            send_sem=self.send_sems.at[7 * a + k], recv_sem=self.recv_sems.at[7 * a + k],
                                            device_id=to, device_id_type=MESH)

    def _relay(self, a, j):
        return self._copy(a, 3, (*self.chips[j], self.c), (*self.chips[1 - j], self.c))

    def _mine(self):
        return [pltpu.make_async_copy(self.ins[a], self.outs[a].at[_index(*self.me)], self.local_sems.at[a]) for a in range(self.n)]

    def _first(self):
        first = []
        for a in range(self.n):
            first.append(self._copy(a, 0, self.me, self.sibling, src=self.ins[a]))
            first += [self._copy(a, 1 + j, self.me, (*self.chips[j], self.c), src=self.ins[a]) for j in range(2)]
        return first

    def _passed(self, j):
        return [self._copy(a, 4 + j, (*self.chips[j], self.c), self.sibling) for a in range(self.n)]

    def stage(self, s):
        n, c = self.n, self.c
        if s == 0:
            for cp in self._mine() + self._first():
                cp.start()
        elif s < 4:
            j = s - 1
            for a in range(n):
                self._copy(a, 1 + j, (*self.chips[j], c), self.me).wait_recv()
            for cp in self._passed(j):
                cp.start()
            if j < 2:
                @pl.when(c == j)
                def _():
                    for a in range(n):
                        self._relay(a, j).start()
        else:
            for a in range(n):
                self._copy(a, 0, self.sibling, self.me).wait_recv()
                for j in range(3):
                    self._copy(a, 4 + j, (*self.chips[j], 1 - c), self.me).wait_recv()
            for cp in self._first() + self._passed(0) + self._passed(1) + self._passed(2):
                cp.wait_send()
            for j in range(2):
                @pl.when(c == j)
                def _():
                    for a in range(n):
                        self._relay(a, j).wait_send()
            for cp in self._mine():
                cp.wait()


def _all_gather(arrs, name, first_cols=None):
    n = len(arrs)
    shapes = [a.shape for a in arrs]
    if first_cols is not None:
        shapes[0] = (*shapes[0][:-1], first_cols)

    def body(*refs):
        ins = list(refs[:n])
        if first_cols is not None:
            ins[0] = ins[0].at[:, pl.ds(0, first_cols)]
        gather = _RelayGather(ins, refs[n:2 * n], refs[2 * n:])
        for s in range(gather.STAGES):
            gather.stage(s)

    return _call(
        body, name,
        out_shape=[jax.ShapeDtypeStruct((N_DEV, *s), a.dtype) for s, a in zip(shapes, arrs)],
        in_specs=[ANY] * n, out_specs=[ANY] * n, scratch_shapes=_gather_sems(n),
    )(*arrs)


N_CHIP = N_DEV // 2


def _slab_block0(owner):
    step = SLAB_BLK0[1]
    assert all(SLAB_BLK0[d] == step * d - (d == N_DEV - 1) for d in range(N_DEV))
    return step * owner - jnp.where(owner == N_DEV - 1, 1, 0)


def _add_bf16(c_ref, a_ref, b_ref, o_ref):
    o_ref[...] = (a_ref[...].astype(F32) + b_ref[...].astype(F32)).astype(BF16)


def _chip_partial_slab(dw_blocks, sib, core):
    blk = pl.BlockSpec((None, SLAB_BLOCKS, D_MODEL, LANES), lambda q, c_ref: (q, 0, 0, 0))
    return _call(
        functools.partial(_add_bf16), "chip_partial_w_in", out_shape=jax.ShapeDtypeStruct(sib.shape, BF16),
        grid_spec=pltpu.PrefetchScalarGridSpec(
            num_scalar_prefetch=1, grid=(N_CHIP,),
            in_specs=[pl.BlockSpec((pl.Element(SLAB_BLOCKS), pl.Element(D_MODEL), pl.Element(LANES)),
                                   lambda q, c_ref: (_slab_block0(2 * q + c_ref[0]), 0, 0)), blk],
            out_specs=blk),
        compiler_params=_params(("arbitrary",)),
    )(core, dw_blocks, sib)


def _chip_partial_rows(sends, sibs, core):
    n = len(sends)

    def body(c_ref, *refs):
        for k in range(n):
            _add_bf16(c_ref, refs[k], refs[n + k], refs[2 * n + k])

    own = [pl.BlockSpec((None, *a.shape[1:]), lambda q, c_ref: (2 * q + c_ref[0], 0, 0)) for a in sends]
    blk = [pl.BlockSpec((None, *a.shape[1:]), lambda q, c_ref: (q, 0, 0)) for a in sibs]
    return _call(
        body, "chip_partial_rows", out_shape=[jax.ShapeDtypeStruct(a.shape, BF16) for a in sibs],
        grid_spec=pltpu.PrefetchScalarGridSpec(num_scalar_prefetch=1, grid=(N_CHIP,), in_specs=own + blk, out_specs=blk),
        compiler_params=_params(("arbitrary",)),
    )(core, *sends, *sibs)


def _exchange_sems(n_arrays, n_peers):
    return [pltpu.SemaphoreType.DMA((n_arrays * n_peers,)), pltpu.SemaphoreType.DMA((n_arrays * n_peers,)),
            pltpu.SemaphoreType.DMA((n_arrays,))]


class _Exchange:
    def __init__(self, srcs, dsts, sems, among_chips):
        self.arrs = list(zip(srcs, dsts))
        self.n = len(self.arrs)
        self.send_sems, self.recv_sems, self.local_sems = sems
        self.among_chips = among_chips
        x, y, c = _position()
        self.c = c
        self.me = 2 * x + y if among_chips else _index(x, y, c)
        self.n_peers = N_CHIP if among_chips else N_DEV

    def _device(self, p):
        return (p // 2, p % 2, self.c) if self.among_chips else (p // 4, (p // 2) % 2, p % 2)

    def _src(self, k, p):
        src = self.arrs[k][0]
        return src.at[p] if self.among_chips else src

    def _mine(self):
        return [pltpu.make_async_copy(self._src(k, self.me), self.arrs[k][1].at[self.me], self.local_sems.at[k]) for k in range(self.n)]

    def _copy(self, p, k, landing):
        return pltpu.make_async_remote_copy(
            src_ref=self._src(k, p), dst_ref=self.arrs[k][1].at[landing], send_sem=self.send_sems.at[self.n * p + k],
            recv_sem=self.recv_sems.at[self.n * landing + k], device_id=self._device(p), device_id_type=MESH)

    def _others(self, fn):
        for p in range(self.n_peers):
            @pl.when(p != self.me)
            def _():
                for k in range(self.n):
                    fn(p, k)

    def start(self):
        for cp in self._mine():
            cp.start()
        self._others(lambda p, k: self._copy(p, k, self.me).start())

    def finish(self):
        self._others(lambda p, k: self._copy(p, k, p).wait_recv())
        self._others(lambda p, k: self._copy(p, k, self.me).wait_send())
        for cp in self._mine():
            cp.wait()


def _adamw(g, w, m, v):
    m_new = ADAM_B1 * m + (1.0 - ADAM_B1) * g
    v_new = ADAM_B2 * v + (1.0 - ADAM_B2) * (g * g)
    m_hat = m_new / (1.0 - ADAM_B1 ** ADAM_STEP)
    v_hat = v_new / (1.0 - ADAM_B2 ** ADAM_STEP)
    delta = -ADAM_LR * (m_hat / (jnp.sqrt(v_hat) + ADAM_EPS) + ADAM_WD * w)
    return delta, m_new, v_new


def _sum_partials(p_ref):
    g = p_ref[0].astype(F32)
    for d in range(1, p_ref.shape[0]):
        g = g + p_ref[d].astype(F32)
    return g


def _reduce_adam_rows(parts, ws, ms, vs):
    n = len(ws)

    def body(*refs):
        ins, outs = refs[:4 * n], refs[4 * n:]
        for k in range(n):
            p_ref, w_ref, m_ref, v_ref = ins[k], ins[n + k], ins[2 * n + k], ins[3 * n + k]
            g = _sum_partials(p_ref)
            outs[4 * k][...] = g
            outs[4 * k + 1][...], outs[4 * k + 2][...], outs[4 * k + 3][...] = _adamw(g, w_ref[...], m_ref[...], v_ref[...])

    outs = _call(
        body, "adam_row_weights", out_shape=[jax.ShapeDtypeStruct(w.shape, F32) for w in ws for _ in range(4)],
        compiler_params=_params(),
    )(*parts, *ws, *ms, *vs)
    return [tuple(outs[4 * k:4 * k + 4]) for k in range(n)]


def _reduce_adam_slab(parts, glr, w_t, m_t, v_t, me):
    cols, rows = w_t.shape
    shift = jnp.asarray(SLAB_SHIFT, jnp.int32)[me]
    glr_at = jnp.where(me == GLR_DEV, GLR_LOCAL, cols).astype(jnp.int32)

    def body(s_ref, p_ref, glr_ref, w_ref, m_ref, v_ref, g_ref, d_ref, mo_ref, vo_ref, slab_t):
        shift, glr_at = s_ref[0], s_ref[1]
        tall = jnp.concatenate([_sum_partials(p_ref.at[:, j]).T for j in range(SLAB_BLOCKS)], axis=0)
        before = pltpu.roll(tall, SLAB_W - shift, 0)
        after = pltpu.roll(tall, lax.rem(SLAB_W - shift + GLA_RANK, SLAB_W), 0)
        wide = jnp.concatenate([glr_ref[...].T, jnp.zeros((SLAB_W - LANES, LANES), F32)], axis=0)
        placed = pltpu.roll(wide, lax.rem(glr_at, SLAB_W), 0)
        row = lax.broadcasted_iota(jnp.int32, (SLAB_W, LANES), 0)
        slab_t[...] = jnp.where(row < glr_at, before, jnp.where(row < glr_at + GLA_RANK, placed, after))
        g = slab_t[pl.ds(0, cols), :]
        g_ref[...] = g
        d_ref[...], mo_ref[...], vo_ref[...] = _adamw(g, w_ref[...], m_ref[...], v_ref[...])

    blk = pl.BlockSpec((cols, LANES), lambda i, s: (0, i))
    return _call(
        body, "adam_w_in", out_shape=[jax.ShapeDtypeStruct((cols, rows), F32)] * 4,
        grid_spec=pltpu.PrefetchScalarGridSpec(
            num_scalar_prefetch=1, grid=(rows // LANES,),
            in_specs=[pl.BlockSpec((parts.shape[0], SLAB_BLOCKS, LANES, LANES), lambda i, s: (0, 0, i, 0)),
                      pl.BlockSpec((LANES, LANES), lambda i, s: (i, 0)), blk, blk, blk],
            out_specs=[blk] * 4, scratch_shapes=[pltpu.VMEM((SLAB_W, LANES), F32)]),
        compiler_params=_params(("arbitrary",)),
    )(jnp.stack([shift, glr_at]), parts, glr, w_t, m_t, v_t)


def _reduce_small(parts):
    def body(p_ref, o_ref):
        o_ref[...] = _sum_partials(p_ref)

    return _call(body, "reduce_small", out_shape=jax.ShapeDtypeStruct(parts.shape[1:], F32))(parts)


def _adam_small(g, w, m, v):
    def body(g_ref, w_ref, m_ref, v_ref, d_ref, mo_ref, vo_ref):
        d_ref[...], mo_ref[...], vo_ref[...] = _adamw(g_ref[...], w_ref[...], m_ref[...], v_ref[...])

    return _call(body, "adam_small", out_shape=[jax.ShapeDtypeStruct(g.shape, F32)] * 3)(g, w, m, v)


def _pack_rows(arrs):
    rows = []
    for a in arrs:
        flat = a.reshape(-1).astype(F32)
        pad = (-flat.shape[0]) % LANES
        rows.append(jnp.pad(flat, (0, pad)).reshape(-1, LANES))
    packed = jnp.concatenate(rows, axis=0)
    return jnp.pad(packed, ((0, (-packed.shape[0]) % 8), (0, 0)))


def _unpack_rows(packed, shapes):
    out, r = [], 0
    for shp in shapes:
        size = 1
        for s in shp:
            size *= s
        nrows = -(-size // LANES)
        out.append(packed[r:r + nrows].reshape(-1)[:size].reshape(shp))
        r += nrows
    return out


def _shard_to_slab(shard, d):
    glr = jnp.zeros((D_MODEL, GLA_RANK), shard.dtype)
    if d == GLR_DEV:
        glr = shard[:, GLR_LOCAL:GLR_LOCAL + GLA_RANK]
        shard = jnp.concatenate([shard[:, :GLR_LOCAL], shard[:, GLR_LOCAL + GLA_RANK:]], axis=1)
    return jnp.pad(shard, ((0, 0), (SLAB_SHIFT[d], SLAB_W - SLAB_SHIFT[d] - shard.shape[1]))), glr


def kernel(x, meta_tokens, norm_gain, w_in, w_gate_up, b_gate, ret_norm_gain, gla_norm_gain, w_branch_ret, w_branch_gla, w_out, final_norm_gain, loss_target, m_meta_tokens, m_norm_gain, m_w_in, m_w_gate_up, m_b_gate, m_ret_norm_gain, m_gla_norm_gain, m_w_branch_ret, m_w_branch_gla, m_w_out, m_final_norm_gain, v_meta_tokens, v_norm_gain, v_w_in, v_w_gate_up, v_b_gate, v_ret_norm_gain, v_gla_norm_gain, v_w_branch_ret, v_w_branch_gla, v_w_out, v_final_norm_gain):
    xi, yi, ci = _position()
    me = _index(xi, yi, ci)
    seq = x.shape[1]
    t_rows = seq + TILE
    in_shard = w_in.shape[2]
    gu_shard = w_gate_up.shape[2]
    meta_shard = meta_tokens.shape[1]
    ret_rows, gla_rows, out_rows = w_branch_ret.shape[1], w_branch_gla.shape[1], w_out.shape[1]

    assert in_shard == IN_SHARD
    slab_local, glr_local = lax.switch(me, [functools.partial(_shard_to_slab, d=d) for d in range(N_DEV)], w_in[0])
    small_local = jnp.concatenate([meta_tokens, jnp.pad(w_gate_up[0], ((0, 0), (0, LANES - gu_shard))),
                                   glr_local.reshape(-1, LANES)], axis=0)
    slab_local = slab_local.astype(BF16)
    first_halves, g_small = _all_gather([slab_local, small_local], "all_gather_shards", first_cols=HALF_W)
    n_small = N_META + GLA_RANK
    w_glr = jnp.pad(g_small[GLR_DEV, n_small:].reshape(D_MODEL, GLA_RANK), ((0, 0), (0, LANES - GLA_RANK))).astype(BF16)
    meta_full = jnp.transpose(g_small[:, :N_META, :], (1, 0, 2)).reshape(N_META, D_MODEL)
    wgu_full = jnp.transpose(g_small[:, N_META:n_small, :gu_shard], (1, 0, 2)).reshape(GLA_RANK, GLA_HEADS * GLA_K)
    wgu_pad = jnp.pad(wgu_full, ((0, LANES - GLA_RANK), (0, 0)))

    rope = _rope_tables(t_rows // TILE)
    lg = jnp.log1p(-(2.0 ** (-5.0 - jnp.arange(RET_HEADS, dtype=F32))))

    head = jnp.concatenate([jnp.zeros((PAD_ROWS, D_MODEL), F32), meta_full], axis=0)
    proj_first, second_halves = _inproj_first(head, x[0], norm_gain, first_halves, slab_local)
    slabs = (first_halves, second_halves)
    ut, proj, glr = _inproj_tiles(head, x[0], norm_gain, slabs, w_glr, proj_first)
    o_ret_raw, o_ret, ret_states, (g_br, g_bg, g_o) = _ret_fwd(
        proj, rope, ret_norm_gain, lg, [w_branch_ret[0].astype(BF16), w_branch_gla[0].astype(BF16), w_out[0].astype(BF16)])
    w_br, w_bg, w_o = g_br.reshape(RET_W, D_MODEL), g_bg.reshape(GLA_W, D_MODEL), g_o.reshape(D_MODEL, D_MODEL)
    masks, cum_fwd, cum_bwd = _gla_tables()
    o_gla_raw, o_gla, gla_states, gla_scores_t = _gla_fwd(proj, glr, wgu_pad, b_gate, gla_norm_gain, masks, cum_fwd)
    (dh1, d_mr, d_mg, do_ret, do_gla, loss_part, d_gfinal, dw_br, dw_bg, dw_o) = _merge_fwd_bwd(
        o_ret, o_gla, proj, x[0], loss_target[0], final_norm_gain.reshape(1, D_MODEL), w_br, w_bg, w_o)

    d_rq, d_rk, d_rv, d_rg, d_gret = _ret_bwd(proj, rope, ret_norm_gain, lg, o_ret_raw, do_ret, ret_states)
    d_gq, d_gk, d_gv, d_gg, dglr_parts, d_wgu, d_bgate, d_ggla = _gla_bwd(
        proj, glr, wgu_pad, b_gate, gla_norm_gain, o_gla_raw, do_gla, gla_states, gla_scores_t, masks, cum_fwd, cum_bwd)
    dseg = dict(rq=d_rq, rk=d_rk, rv=d_rv, rg=d_rg, gq=d_gq, gk=d_gk, gv=d_gv, gg=d_gg, mr=d_mr, mg=d_mg)
    row_sends = [dw_br.reshape(N_DEV, ret_rows, D_MODEL), dw_bg.reshape(N_DEV, gla_rows, D_MODEL),
                 dw_o.reshape(N_DEV, out_rows, D_MODEL)]
    dw_blocks, dw_glr, sib_in, sib_rows = _inproj_bwd_w(ut, dseg, dglr_parts, row_sends)
    core = ci.astype(jnp.int32).reshape(1)
    chip_partials = [_chip_partial_slab(dw_blocks, sib_in, core)] + list(_chip_partial_rows(row_sends, list(sib_rows), core))
    grad_x, d_head, d_gnorm, p_in, p_br, p_bg, p_o = _inproj_bwd_x(
        dseg, dglr_parts, head, x[0], dh1, norm_gain, slabs, w_glr, chip_partials)
    small_shapes = [(N_META, D_MODEL), (1, D_MODEL), (GLA_RANK, GLA_HEADS * GLA_K), (1, GLA_HEADS * GLA_K),
                    (1, RET_W), (1, GLA_W), (1, D_MODEL), (1, LANES), (D_MODEL, GLA_RANK)]
    small_part = _pack_rows([d_head[PAD_ROWS:], d_gnorm, d_wgu[:GLA_RANK], d_bgate, d_gret, d_ggla, d_gfinal, loss_part,
                             dw_glr[:, :GLA_RANK]])
    (p_small,) = _all_gather([small_part], "all_gather_small_partials")

    (g_meta_f, g_gnorm, g_wgu_f, g_bgate, g_gret, g_ggla, g_gfinal, loss_all,
     g_wglr) = _unpack_rows(_reduce_small(p_small), small_shapes)
    g_w_in, d_w_in, nm_w_in, nv_w_in = [a.T for a in _reduce_adam_slab(
        p_in, jnp.pad(g_wglr, ((0, 0), (0, LANES - GLA_RANK))), w_in[0].T, m_w_in[0].T, v_w_in[0].T, me)]
    ((g_w_br, d_w_br, nm_w_br, nv_w_br), (g_w_bg, d_w_bg, nm_w_bg, nv_w_bg), (g_w_o, d_w_o, nm_w_o, nv_w_o)) = _reduce_adam_rows(
        [p_br, p_bg, p_o], [w_branch_ret[0], w_branch_gla[0], w_out[0]], [m_w_branch_ret[0], m_w_branch_gla[0], m_w_out[0]],
        [v_w_branch_ret[0], v_w_branch_gla[0], v_w_out[0]])
    g_meta = lax.dynamic_slice_in_dim(g_meta_f, me * meta_shard, meta_shard, axis=1)
    g_wgu = lax.dynamic_slice_in_dim(g_wgu_f, me * gu_shard, gu_shard, axis=1)
    s_g = [g_meta, g_gnorm, g_wgu, g_bgate, g_gret, g_ggla, g_gfinal]
    s_w = [meta_tokens, norm_gain, w_gate_up[0], b_gate, ret_norm_gain, gla_norm_gain, final_norm_gain]
    s_m = [m_meta_tokens, m_norm_gain, m_w_gate_up[0], m_b_gate, m_ret_norm_gain, m_gla_norm_gain, m_final_norm_gain]
    s_v = [v_meta_tokens, v_norm_gain, v_w_gate_up[0], v_b_gate, v_ret_norm_gain, v_gla_norm_gain, v_final_norm_gain]
    shapes = [a.shape for a in s_g]
    s_d, s_nm, s_nv = [_unpack_rows(p, shapes) for p in _adam_small(*[_pack_rows(l) for l in (s_g, s_w, s_m, s_v)])]

    loss = loss_all[0, 0]
    grad_x = grad_x[None]

    def order(meta, gnorm, win, wgu, bgate, gret, ggla, wbr, wbg, wo, gfin):
        return (meta, gnorm, win[None], wgu[None], bgate, gret, ggla, wbr[None], wbg[None], wo[None], gfin.reshape(final_norm_gain.shape))

    def small(l):
        return dict(meta=l[0], gnorm=l[1], wgu=l[2], bgate=l[3], gret=l[4], ggla=l[5], gfin=l[6])

    grads = order(win=g_w_in, wbr=g_w_br, wbg=g_w_bg, wo=g_w_o, **small(s_g))
    deltas = order(win=d_w_in, wbr=d_w_br, wbg=d_w_bg, wo=d_w_o, **small(s_d))
    new_m = order(win=nm_w_in, wbr=nm_w_br, wbg=nm_w_bg, wo=nm_w_o, **small(s_nm))
    new_v = order(win=nv_w_in, wbr=nv_w_br, wbg=nv_w_bg, wo=nv_w_o, **small(s_nv))
    return (loss, grad_x, *grads, *deltas, *new_m, *new_v)
```

```python
import functools

import jax
import jax.numpy as jnp
from jax import lax
from jax.experimental import pallas as pl
from jax.experimental.pallas import tpu as pltpu

F32 = jnp.float32
BF16 = jnp.bfloat16

D_MODEL = 1024
N_META = 16
TILE = 256
PAD_ROWS = TILE - N_META
RET_HEADS = 4
RET_QK = 256
RET_V = 512
RET_W = RET_HEADS * RET_V
GLA_HEADS = 4
GLA_K = 128
GLA_V = 256
GLA_W = GLA_HEADS * GLA_V
GLA_RANK = 16
GLA_TAU = 16.0
GLA_CHUNK = 16
ROPE_BASE = 10000.0
EPS = 1e-6
LANES = 128
N_DEV = 8
SEG_NAMES = ("rq", "rk", "rv", "rg", "gq", "gk", "gv", "gg", "mr", "mg")
SEG_W = (1024, 1024, 2048, 2048, 512, 512, 1024, 1024, 1024, 1024)
SEG_OFF = tuple(sum(SEG_W[:i]) for i in range(len(SEG_W)))
AL_COLS = sum(SEG_W)
IN_COLS = AL_COLS + GLA_RANK
GLR_OFF = sum(SEG_W[:8])
IN_SHARD = IN_COLS // N_DEV


def _aligned_col(c):
    assert c <= GLR_OFF or c >= GLR_OFF + GLA_RANK
    return c if c <= GLR_OFF else c - GLA_RANK


SLAB_BOUND = tuple(_aligned_col(IN_SHARD * d) for d in range(N_DEV + 1))
SLAB_BLK0 = tuple(b // LANES for b in SLAB_BOUND[:-1])
SLAB_SHIFT = tuple(b % LANES for b in SLAB_BOUND[:-1])
SLAB_BLOCKS = max(-(-SLAB_BOUND[d + 1] // LANES) - SLAB_BLK0[d] for d in range(N_DEV))
SLAB_W = SLAB_BLOCKS * LANES
GLR_DEV = GLR_OFF // IN_SHARD
GLR_LOCAL = GLR_OFF - GLR_DEV * IN_SHARD
assert all(SLAB_BLK0[d] + SLAB_BLOCKS <= AL_COLS // LANES for d in range(N_DEV))
VMEM_LIMIT = 58 * 1024 * 1024
ADAM_LR, ADAM_B1, ADAM_B2, ADAM_EPS, ADAM_WD, ADAM_STEP = 0.001, 0.9, 0.999, 1e-08, 0.01, 10
ANY = pl.BlockSpec(memory_space=pl.ANY)
MESH = pl.DeviceIdType.MESH


def _call(body, name, **kw):
    return pl.pallas_call(body, name=name, **kw)


def _params(sem=None):
    return pltpu.CompilerParams(dimension_semantics=sem, vmem_limit_bytes=VMEM_LIMIT)


def _mm(a, b):
    return jnp.dot(a, b, preferred_element_type=F32)


def _mm_nt(a, b):
    return lax.dot_general(a, b, (((1,), (1,)), ((), ())), preferred_element_type=F32)


def _mm_tn(a, b):
    return lax.dot_general(a, b, (((0,), (0,)), ((), ())), preferred_element_type=F32)


def _sigmoid(x):
    return jax.nn.sigmoid(x)


def _rope(t, cos, sin):
    half = t.shape[-1] // 2
    t1, t2 = t[:, :half], t[:, half:]
    return jnp.concatenate([t1 * cos - t2 * sin, t2 * cos + t1 * sin], axis=-1)


def _rope_bwd(g, cos, sin):
    half = g.shape[-1] // 2
    g1, g2 = g[:, :half], g[:, half:]
    return jnp.concatenate([g1 * cos + g2 * sin, g2 * cos - g1 * sin], axis=-1)


def _row_mean(x):
    return jnp.mean(x, axis=-1, keepdims=True)


def _col_sum(x):
    return jnp.sum(x, axis=0, keepdims=True)


def _tile_rows(head_ref, x_ref):
    return jnp.where(pl.program_id(0) == 0, head_ref[...], x_ref[...])


def _head_spec():
    return pl.BlockSpec((TILE, D_MODEL), lambda i: (0, 0))


def _x_spec():
    return pl.BlockSpec((TILE, D_MODEL), lambda i: (jnp.maximum(i - 1, 0), 0))


def _slab_plan():
    interior, shared = [], []
    for d in range(N_DEV):
        lo, hi = -(-SLAB_BOUND[d] // LANES), SLAB_BOUND[d + 1] // LANES
        interior.append((d, LANES * (lo - SLAB_BLK0[d]), LANES * lo, LANES * (hi - lo)))
        if d + 1 < N_DEV and SLAB_BOUND[d + 1] % LANES:
            shared.append((hi, d, hi - SLAB_BLK0[d]))
    return interior, shared


N_BLOCKS = AL_COLS // LANES
HALF_BLOCKS = SLAB_BLOCKS // 2
HALF_W = SLAB_W // 2


def _half_blocks():
    interior, _ = _slab_plan()
    first = [dst // LANES + j for _, src, dst, width in interior for j in range(width // LANES) if src // LANES + j < HALF_BLOCKS]
    return first, [b for b in range(N_BLOCKS) if b not in first]


def _w_scratch(n_blocks=N_BLOCKS):
    return [pltpu.VMEM((D_MODEL, LANES * n_blocks), BF16), pltpu.VMEM((D_MODEL, LANES), BF16),
            pltpu.VMEM((2 * (N_DEV - 1), D_MODEL, LANES), BF16), pltpu.SemaphoreType.DMA((4 * N_DEV,))]


W_SCRATCH = _w_scratch


def _slab_cols(halves, d, lo, n):
    out = []
    for k, half in enumerate(halves):
        a, b = max(lo, k * HALF_W), min(lo + n, (k + 1) * HALF_W)
        if a < b:
            out.append((half.at[d, :, pl.ds(a - k * HALF_W, b - a)], a - lo, b - a))
    return out


def _load_weight(halves, wg_hbm, w_vm, wg_vm, edge_vm, sem, blocks=None):
    blocks = list(range(N_BLOCKS) if blocks is None else blocks)
    place = {b: i for i, b in enumerate(blocks)}
    interior, shared = _slab_plan()
    copies = [] if wg_hbm is None else [(wg_hbm, wg_vm)]
    for d, src, dst, width in interior:
        b0 = dst // LANES
        runs = []
        for b in range(b0, b0 + width // LANES):
            if b in place and runs and b == sum(runs[-1]):
                runs[-1][1] += 1
            elif b in place:
                runs.append([b, 1])
        for b, n in runs:
            for piece, off, w in _slab_cols(halves, d, src + LANES * (b - b0), LANES * n):
                copies.append((piece, w_vm.at[:, pl.ds(LANES * place[b] + off, w)]))
    edges = []
    for blk, d, j in shared:
        if blk in place:
            ((low, _, _),), ((high, _, _),) = _slab_cols(halves, d, LANES * j, LANES), _slab_cols(halves, d + 1, 0, LANES)
            copies += [(low, edge_vm.at[2 * len(edges)]), (high, edge_vm.at[2 * len(edges) + 1])]
            edges.append(blk)
    copies = [pltpu.make_async_copy(a, b, sem.at[i]) for i, (a, b) in enumerate(copies)]
    for cp in copies:
        cp.start()
    for cp in copies:
        cp.wait()
    for n, blk in enumerate(edges):
        w_vm[:, LANES * place[blk]:LANES * (place[blk] + 1)] = edge_vm[2 * n] + edge_vm[2 * n + 1]


def _proj_specs(names, n_units, where):
    specs = []
    for name in names:
        s = SEG_NAMES.index(name)
        nblk = SEG_W[s] // n_units // LANES
        base = SEG_OFF[s] // LANES
        assert base % nblk == 0
        specs.append(pl.BlockSpec((nblk, TILE, LANES), lambda *g, base=base, nblk=nblk: (base // nblk + where(*g)[0], where(*g)[1], 0)))
    return specs


def _cols(ref, unit=0, n_units=1):
    n = ref.shape[0] // n_units
    return ref[unit * n] if n == 1 else jnp.concatenate([ref[unit * n + j] for j in range(n)], axis=1)


def _prenorm(head_ref, x_ref, g_ref):
    x = _tile_rows(head_ref, x_ref)
    r = lax.rsqrt(_row_mean(x * x) + EPS)
    return (x * r * g_ref[...]).astype(BF16).astype(F32)


def _project(u, w_vm, n, store):
    cuts = [8 * i for i in range(max(n // 8, 1))] + [n]
    for lo, hi in zip(cuts[:-1], cuts[1:]):
        res = _mm(u, w_vm[:, LANES * lo:LANES * hi]).astype(BF16)
        for j in range(lo, hi):
            store(j, res[:, LANES * (j - lo):LANES * (j - lo + 1)])


def _inproj_first(head, x, g_norm, first_halves, second_half):
    t_rows = x.shape[0] + TILE
    nt = t_rows // TILE
    first, _ = _half_blocks()
    n = len(first)

    def body(head_ref, x_ref, g_ref, first_hbm, second_hbm, proj_ref, gathered, w_vm, wg_vm, edge_vm, sem, *sems):
        _RelayGather([second_hbm], [gathered], sems).carried(pl.program_id(0), nt)

        @pl.when(pl.program_id(0) == 0)
        def _():
            _load_weight((first_hbm,), None, w_vm, wg_vm, edge_vm, sem, first)

        def store(i, block):
            proj_ref[i] = block
        _project(_prenorm(head_ref, x_ref, g_ref).astype(BF16), w_vm, n, store)

    return _call(
        body, "inproj_fwd_first", grid=(nt,),
        out_shape=[jax.ShapeDtypeStruct((n, t_rows, LANES), BF16), jax.ShapeDtypeStruct((N_DEV, *second_half.shape), BF16)],
        in_specs=[_head_spec(), _x_spec(), pl.BlockSpec((1, D_MODEL), lambda i: (0, 0)), ANY, ANY],
        out_specs=[pl.BlockSpec((n, TILE, LANES), lambda i: (0, i, 0)), ANY],
        scratch_shapes=_w_scratch(n) + _gather_sems(1), compiler_params=_params(("arbitrary",)),
    )(head, x, g_norm, first_halves, second_half)


def _inproj_tiles(head, x, g_norm, halves, w_glr, proj_first):
    t_rows = x.shape[0] + TILE
    nt = t_rows // TILE
    first, rest = _half_blocks()

    def body(head_ref, x_ref, g_ref, first_hbm, second_hbm, wg_hbm, pf_ref, ut_ref, proj_ref, glr_ref, w_vm, wg_vm, edge_vm, sem):
        @pl.when(pl.program_id(0) == 0)
        def _():
            _load_weight((first_hbm, second_hbm), wg_hbm, w_vm, wg_vm, edge_vm, sem, rest)

        u32 = _prenorm(head_ref, x_ref, g_ref)
        u = u32.astype(BF16)
        ut_ref[...] = u32.T.astype(BF16)

        def store(i, block):
            proj_ref[rest[i]] = block
        _project(u, w_vm, len(rest), store)
        for i, b in enumerate(first):
            proj_ref[b] = pf_ref[i]
        glr_ref[...] = _mm(u, wg_vm[...])

    return _call(
        body, "inproj_fwd_tiles", grid=(nt,),
        out_shape=[jax.ShapeDtypeStruct((nt, D_MODEL, TILE), BF16), jax.ShapeDtypeStruct((N_BLOCKS, t_rows, LANES), BF16),
                   jax.ShapeDtypeStruct((t_rows, LANES), F32)],
        in_specs=[_head_spec(), _x_spec(), pl.BlockSpec((1, D_MODEL), lambda i: (0, 0)), ANY, ANY, ANY,
                  pl.BlockSpec((len(first), TILE, LANES), lambda i: (0, i, 0))],
        out_specs=[pl.BlockSpec((None, D_MODEL, TILE), lambda i: (i, 0, 0)), pl.BlockSpec((N_BLOCKS, TILE, LANES), lambda i: (0, i, 0)),
                   pl.BlockSpec((TILE, LANES), lambda i: (i, 0))],
        scratch_shapes=_w_scratch(len(rest)), compiler_params=_params(("arbitrary",)),
    )(head, x, g_norm, *halves, w_glr, proj_first)


def _ret_decay(lgh):
    i = lax.broadcasted_iota(jnp.int32, (TILE, TILE), 0)
    j = lax.broadcasted_iota(jnp.int32, (TILE, TILE), 1)
    rel = (i - j).astype(F32)
    return jnp.where(rel >= 0, jnp.exp(jnp.maximum(rel, 0.0) * lgh), 0.0)


def _ret_vectors(lgh):
    idx = lax.broadcasted_iota(jnp.int32, (TILE, 1), 0).astype(F32)
    xi = jnp.exp((idx + 1.0) * lgh)
    zeta = jnp.exp((TILE - 1.0 - idx) * lgh)
    gc = jnp.exp(jnp.full((1, 1), float(TILE), F32) * lgh)
    return xi, zeta, gc


def _rope_tables(nt):
    half = RET_QK // 2
    inv = ROPE_BASE ** (-jnp.arange(half, dtype=F32) / half)
    base = (jnp.arange(nt, dtype=F32) * TILE - float(PAD_ROWS))[:, None, None] * inv[None, None, :]
    off = jnp.arange(TILE, dtype=F32)[:, None] * inv[None, :]
    return jnp.cos(base), jnp.sin(base), jnp.cos(off), jnp.sin(off)


def _rope_specs(tile_of):
    return [pl.BlockSpec((None, 1, RET_QK // 2), lambda i: (tile_of(i), 0, 0))] * 2 + [pl.BlockSpec((TILE, RET_QK // 2), lambda i: (0, 0))] * 2


def _rope_angles(cb_ref, sb_ref, co_ref, so_ref):
    cb, sb, co, so = cb_ref[...], sb_ref[...], co_ref[...], so_ref[...]
    return cb * co - sb * so, sb * co + cb * so


def _ret_fwd(proj, rope, gain, lg, row_shards):
    t_rows = proj.shape[1]
    nt = t_rows // TILE
    ns = len(row_shards)

    def body(lg_ref, q_ref, k_ref, v_ref, g_ref, cb_ref, sb_ref, co_ref, so_ref, gain_ref, *rest):
        shard_refs, (oraw_ref, oret_ref, st_ref), gathered = rest[:ns], rest[ns:ns + 3], rest[ns + 3:2 * ns + 3]
        s_acc, dm = rest[2 * ns + 3:2 * ns + 5]
        t = pl.program_id(0)
        _RelayGather(shard_refs, gathered, rest[2 * ns + 5:]).carried(t, nt)

        @pl.when(t == 0)
        def _():
            s_acc[...] = jnp.zeros_like(s_acc)
            for h in range(RET_HEADS):
                dm[h] = _ret_decay(lg_ref[h])

        cos_t, sin_t = _rope_angles(cb_ref, sb_ref, co_ref, so_ref)
        for h in range(RET_HEADS):
            lgh = lg_ref[h]
            q = _rope(_cols(q_ref, h, RET_HEADS).astype(F32), cos_t, sin_t)
            k = _rope(_cols(k_ref, h, RET_HEADS).astype(F32), cos_t, sin_t) * (RET_QK ** -0.5)
            xi, zeta, gc = _ret_vectors(lgh)
            v = _cols(v_ref, h, RET_HEADS)
            s_in = s_acc[h]
            p = (_mm_nt(q.astype(BF16), k.astype(BF16)) * dm[h]).astype(BF16)
            o = _mm(p, v) + _mm((q * xi).astype(BF16), s_in.astype(BF16))
            st_ref[h] = s_in.astype(BF16)
            s_acc[h] = s_in * gc + _mm_tn((k * zeta).astype(BF16), v)
            cols = slice(h * RET_V, (h + 1) * RET_V)
            oraw_ref[:, cols] = o
            oc = o - _row_mean(o)
            n = oc * lax.rsqrt(_row_mean(oc * oc) + EPS) * gain_ref[:, cols]
            g = _cols(g_ref, h, RET_HEADS).astype(F32)
            oret_ref[:, cols] = (n * g * _sigmoid(g)).astype(BF16)

    row = lambda w: pl.BlockSpec((TILE, w), lambda t: (t, 0))
    outs = _call(
        body, "ret_fwd", grid=(nt,),
        out_shape=[jax.ShapeDtypeStruct((t_rows, RET_W), F32), jax.ShapeDtypeStruct((t_rows, RET_W), BF16),
                   jax.ShapeDtypeStruct((RET_HEADS, nt, RET_QK, RET_V), BF16)]
                  + [jax.ShapeDtypeStruct((N_DEV, *a.shape), a.dtype) for a in row_shards],
        in_specs=[pl.BlockSpec(memory_space=pltpu.SMEM)] + _proj_specs(("rq", "rk", "rv", "rg"), 1, lambda t: (0, t)) + _rope_specs(lambda t: t) + [
                  pl.BlockSpec((1, RET_W), lambda t: (0, 0))] + [ANY] * ns,
        out_specs=[row(RET_W), row(RET_W), pl.BlockSpec((RET_HEADS, None, RET_QK, RET_V), lambda t: (0, t, 0, 0))] + [ANY] * ns,
        scratch_shapes=[pltpu.VMEM((RET_HEADS, RET_QK, RET_V), F32), pltpu.VMEM((RET_HEADS, TILE, TILE), F32)] + _gather_sems(ns),
        compiler_params=_params(("arbitrary",)),
    )(lg, proj, proj, proj, proj, *rope, gain, *row_shards)
    return outs[0], outs[1], outs[2], outs[3:]


def _ret_bwd(proj, rope, gain, lg, o_raw, do_ret, states):
    t_rows = proj.shape[1]
    nt = t_rows // TILE

    def body(lg_ref, q_ref, k_ref, v_ref, g_ref, cb_ref, sb_ref, co_ref, so_ref, gain_ref, oraw_ref, do_ref, st_ref,
             dq_ref, dk_ref, dv_ref, dg_ref, dgain_ref, e_acc, dm):
        @pl.when(pl.program_id(0) == 0)
        def _():
            e_acc[...] = jnp.zeros_like(e_acc)
            for h in range(RET_HEADS):
                dm[h] = _ret_decay(lg_ref[h])
            dgain_ref[...] = jnp.zeros_like(dgain_ref)

        cos_t, sin_t = _rope_angles(cb_ref, sb_ref, co_ref, so_ref)
        for h in range(RET_HEADS):
            lgh = lg_ref[h]
            cols = slice(h * RET_V, (h + 1) * RET_V)
            qcols = slice(h * RET_QK, (h + 1) * RET_QK)
            q = _rope(_cols(q_ref, h, RET_HEADS).astype(F32), cos_t, sin_t)
            k = _rope(_cols(k_ref, h, RET_HEADS).astype(F32), cos_t, sin_t) * (RET_QK ** -0.5)
            xi, zeta, gc = _ret_vectors(lgh)
            v = _cols(v_ref, h, RET_HEADS)
            g = _cols(g_ref, h, RET_HEADS).astype(F32)
            o = oraw_ref[:, cols]
            do = do_ref[:, cols].astype(F32)
            oc = o - _row_mean(o)
            rstd = lax.rsqrt(_row_mean(oc * oc) + EPS)
            xh = oc * rstd
            gain_t = gain_ref[:, cols]
            sg = _sigmoid(g)
            dn = do * (g * sg)
            dg_ref[:, cols] = (do * (xh * gain_t) * (sg * (1.0 + g * (1.0 - sg)))).astype(BF16)
            dgain_ref[:, cols] += _col_sum(dn * xh)
            dxh = dn * gain_t
            dob = (rstd * (dxh - _row_mean(dxh) - xh * _row_mean(dxh * xh))).astype(BF16)
            dmat = dm[h]
            qb, kb = q.astype(BF16), k.astype(BF16)
            p = (_mm_nt(qb, kb) * dmat).astype(BF16)
            dp = (_mm_nt(dob, v) * dmat).astype(BF16)
            s_in = st_ref[h]
            e_in = e_acc[h]
            e_b = e_in.astype(BF16)
            dq = _mm(dp, kb) + _mm_nt(dob, s_in) * xi
            dk = _mm_tn(dp, qb) + _mm_nt(v, e_b) * zeta
            dv_ref[:, cols] = (_mm_tn(p, dob) + _mm((k * zeta).astype(BF16), e_b)).astype(BF16)
            e_acc[h] = e_in * gc + _mm_tn((q * xi).astype(BF16), dob)
            dq_ref[:, qcols] = _rope_bwd(dq, cos_t, sin_t).astype(BF16)
            dk_ref[:, qcols] = (_rope_bwd(dk, cos_t, sin_t) * (RET_QK ** -0.5)).astype(BF16)

    row = lambda w: pl.BlockSpec((TILE, w), lambda j: (nt - 1 - j, 0))
    vec = pl.BlockSpec((1, RET_W), lambda j: (0, 0))
    return _call(
        body, "ret_bwd", grid=(nt,),
        out_shape=[jax.ShapeDtypeStruct((t_rows, RET_HEADS * RET_QK), BF16), jax.ShapeDtypeStruct((t_rows, RET_HEADS * RET_QK), BF16),
                   jax.ShapeDtypeStruct((t_rows, RET_W), BF16), jax.ShapeDtypeStruct((t_rows, RET_W), BF16),
                   jax.ShapeDtypeStruct((1, RET_W), F32)],
        in_specs=[pl.BlockSpec(memory_space=pltpu.SMEM)] + _proj_specs(("rq", "rk", "rv", "rg"), 1, lambda j: (0, nt - 1 - j)) + _rope_specs(lambda j: nt - 1 - j) + [vec,
                  row(RET_W), row(RET_W), pl.BlockSpec((RET_HEADS, None, RET_QK, RET_V), lambda j: (0, nt - 1 - j, 0, 0))],
        out_specs=[row(RET_HEADS * RET_QK), row(RET_HEADS * RET_QK), row(RET_W), row(RET_W), vec],
        scratch_shapes=[pltpu.VMEM((RET_HEADS, RET_QK, RET_V), F32), pltpu.VMEM((RET_HEADS, TILE, TILE), F32)],
        compiler_params=_params(("arbitrary",)),
    )(lg, proj, proj, proj, proj, *rope, gain, o_raw, do_ret, states)


GLA_LEVELS = (32, 64, 128, 256)
N_TERMS = 1 + len(GLA_LEVELS)


def _gla_tables():
    p = jnp.arange(TILE)[:, None]
    r = jnp.arange(TILE)[None, :]
    masks = [(p // GLA_CHUNK == r // GLA_CHUNK) & (r <= p)]
    for blk in GLA_LEVELS:
        masks.append((p // blk == r // blk) & (p % blk >= blk // 2) & (r % blk < blk // 2))
    masks = jnp.stack(masks + [m.T for m in masks]).astype(F32)
    cum_fwd = jnp.concatenate([r <= p, masks[0] > 0], axis=0).astype(BF16)
    cum_bwd = jnp.concatenate([r >= p, masks[N_TERMS] > 0], axis=1).astype(BF16)
    return masks, cum_fwd, cum_bwd


def _split3(x):
    hi = x.astype(BF16)
    rest = x - hi.astype(F32)
    mid = rest.astype(BF16)
    lo = (rest - mid.astype(F32)).astype(BF16)
    return jnp.concatenate([hi, mid, lo], axis=1)


def _join3(y):
    w = y.shape[1] // 3
    return (y[:, 2 * w:] + y[:, w:2 * w]) + y[:, :w]


def _gla_decays(glr_ref, wgu_ref, b_ref, cum_ref):
    z = _mm(glr_ref[...].astype(BF16), wgu_ref[...].astype(BF16)) + b_ref[...]
    la = (jnp.minimum(z, 0.0) - jnp.log(1.0 + jnp.exp(-jnp.abs(z)))) / GLA_TAU
    width = la.shape[1]
    hi = la.astype(BF16)
    rest = la - hi.astype(F32)
    mid = rest.astype(BF16)
    lo = (rest - mid.astype(F32)).astype(BF16)
    y = _mm(cum_ref[...], jnp.concatenate([hi, mid, lo], axis=1))
    gb = (y[:, 2 * width:] + y[:, width:2 * width]) + y[:, :width]
    return z, gb[:TILE], gb[TILE:]


def _gla_prep(h, q_ref, k_ref, g_all, b_all, g_scr, ref_scr):
    cols = slice(h * GLA_K, (h + 1) * GLA_K)
    g, b = g_all[:, cols], b_all[:, cols]
    g_scr[h] = g
    factors = [(jnp.exp(b), jnp.exp(-b))]
    for lvl, blk in enumerate(GLA_LEVELS):
        for n in range(TILE // blk):
            ref_scr[h, lvl, n * blk:(n + 1) * blk, :] = jnp.broadcast_to(g_scr[h, pl.ds(n * blk + blk // 2 - 1, 1), :], (blk, GLA_K))
        x = g - ref_scr[h, lvl]
        factors.append((jnp.exp(jnp.minimum(x, 0.0)), jnp.exp(jnp.minimum(-x, 0.0))))
    g_last = g_scr[h, pl.ds(TILE - 1, 1), :]
    q = _cols(q_ref, h, GLA_HEADS).astype(F32) * (GLA_K ** -0.5)
    k = _cols(k_ref, h, GLA_HEADS).astype(F32)
    return q, k, factors, jnp.exp(g), jnp.exp(g_last), jnp.exp(g_last - g)


def _gla_scores(q, k, factors, m_ref):
    a = jnp.zeros((TILE, TILE), F32)
    for l, (fq, fk) in enumerate(factors):
        s = _mm_nt((q * fq).astype(BF16), (k * fk).astype(BF16))
        a = jnp.where(m_ref[l] > 0.0, s, a)
    return a


def _gla_fwd(proj, glr, wgu_pad, b_gate, gain, masks, cum_fwd):
    t_rows = glr.shape[0]
    nt = t_rows // TILE

    def body(q_ref, k_ref, v_ref, g_ref, glr_ref, wgu_ref, b_ref, gain_ref, m_ref, cum_ref, oraw_ref, ogla_ref, st_ref, at_ref,
             s_acc, g_scr, ref_scr):
        @pl.when(pl.program_id(0) == 0)
        def _():
            s_acc[...] = jnp.zeros_like(s_acc)

        _, g_all, b_all = _gla_decays(glr_ref, wgu_ref, b_ref, cum_ref)
        for h in range(GLA_HEADS):
            q, k, factors, e_g, e_last, e_end = _gla_prep(h, q_ref, k_ref, g_all, b_all, g_scr, ref_scr)
            v = _cols(v_ref, h, GLA_HEADS)
            st = s_acc[h]
            st_ref[h] = st
            a = _gla_scores(q, k, factors, m_ref)
            at_ref[h] = a.T.astype(BF16)
            o = _mm(a.astype(BF16), v) + _mm_nt((q * e_g).astype(BF16), st.astype(BF16))
            s_acc[h] = st * e_last + _mm(v.astype(F32).T.astype(BF16), (k * e_end).astype(BF16))
            cols = slice(h * GLA_V, (h + 1) * GLA_V)
            oraw_ref[:, cols] = o
            n = o * lax.rsqrt(_row_mean(o * o) + EPS) * gain_ref[:, cols]
            g = _cols(g_ref, h, GLA_HEADS).astype(F32)
            ogla_ref[:, cols] = (n * g * _sigmoid(g)).astype(BF16)

    row = lambda w: pl.BlockSpec((TILE, w), lambda t: (t, 0))
    whole = lambda *shape: pl.BlockSpec(shape, lambda t: (0,) * len(shape))
    return _call(
        body, "gla_fwd", grid=(nt,),
        out_shape=[jax.ShapeDtypeStruct((t_rows, GLA_W), F32), jax.ShapeDtypeStruct((t_rows, GLA_W), BF16),
                   jax.ShapeDtypeStruct((GLA_HEADS, nt, GLA_V, GLA_K), F32), jax.ShapeDtypeStruct((GLA_HEADS, t_rows, TILE), BF16)],
        in_specs=_proj_specs(("gq", "gk", "gv", "gg"), 1, lambda t: (0, t)) + [row(LANES), whole(LANES, GLA_HEADS * GLA_K),
                  whole(1, GLA_HEADS * GLA_K), whole(1, GLA_W), whole(N_TERMS, TILE, TILE), whole(2 * TILE, TILE)],
        out_specs=[row(GLA_W), row(GLA_W), pl.BlockSpec((GLA_HEADS, None, GLA_V, GLA_K), lambda t: (0, t, 0, 0)),
                   pl.BlockSpec((GLA_HEADS, TILE, TILE), lambda t: (0, t, 0))],
        scratch_shapes=[pltpu.VMEM((GLA_HEADS, GLA_V, GLA_K), F32), pltpu.VMEM((GLA_HEADS, TILE, GLA_K), F32),
                        pltpu.VMEM((GLA_HEADS, len(GLA_LEVELS), TILE, GLA_K), F32)],
        compiler_params=_params(("arbitrary",)),
    )(proj, proj, proj, proj, glr, wgu_pad, b_gate, gain, masks, cum_fwd)


def _gla_bwd(proj, glr, wgu_pad, b_gate, gain, o_raw, do_gla, states, a_t, masks, cum_fwd, cum_bwd):
    t_rows = glr.shape[0]
    nt = t_rows // TILE

    def body(q_ref, k_ref, v_ref, g_ref, glr_ref, wgu_ref, b_ref, gain_ref, m_ref, cum_ref, cumb_ref, oraw_ref, do_ref, st_ref, at_ref,
             dq_ref, dk_ref, dv_ref, dg_ref, dglr_ref, dwgu_ref, dbg_ref, dgain_ref, d_acc, g_scr, ref_scr, dref_scr):
        @pl.when(pl.program_id(0) == 0)
        def _():
            d_acc[...] = jnp.zeros_like(d_acc)
            dwgu_ref[...] = jnp.zeros_like(dwgu_ref)
            dbg_ref[...] = jnp.zeros_like(dbg_ref)
            dgain_ref[...] = jnp.zeros_like(dgain_ref)

        z_all, g_all, b_all = _gla_decays(glr_ref, wgu_ref, b_ref, cum_ref)
        dla_parts = []
        for h in range(GLA_HEADS):
            q, k, factors, e_g, e_last, e_end = _gla_prep(h, q_ref, k_ref, g_all, b_all, g_scr, ref_scr)
            v = _cols(v_ref, h, GLA_HEADS)
            cols = slice(h * GLA_V, (h + 1) * GLA_V)
            kcols = slice(h * GLA_K, (h + 1) * GLA_K)
            o = oraw_ref[:, cols]
            do = do_ref[:, cols].astype(F32)
            g = _cols(g_ref, h, GLA_HEADS).astype(F32)
            rinv = lax.rsqrt(_row_mean(o * o) + EPS)
            nh = o * rinv
            gain_t = gain_ref[:, cols]
            sg = _sigmoid(g)
            dn = do * (g * sg)
            dg_ref[:, cols] = (do * (nh * gain_t) * (sg * (1.0 + g * (1.0 - sg)))).astype(BF16)
            dgain_ref[:, cols] += _col_sum(dn * nh)
            dnh = dn * gain_t
            dor = rinv * (dnh - nh * _row_mean(dnh * nh))
            dob = dor.astype(BF16)
            a_t = at_ref[h]
            da = _mm_nt(dob, v).astype(BF16)
            da_t = _mm_nt(v, dob).astype(BF16)
            st_in = st_ref[h]
            d_out = d_acc[h]
            d_out_b = d_out.astype(BF16)
            qg, kg = q * e_g, k * e_end
            dqg = _mm(dob, st_in.astype(BF16))
            dkg = _mm(v, d_out_b)
            dv_ref[:, cols] = (_mm(a_t, dob) + _mm_nt(kg.astype(BF16), d_out_b)).astype(BF16)
            d_acc[h] = d_out * e_last + _mm(dor.T.astype(BF16), qg.astype(BF16))
            dq = dqg * e_g
            dk = dkg * e_end
            dkg_kg = dkg * kg
            dg_cum = dqg * qg - dkg_kg
            db = None
            for l, (fq, fk) in enumerate(factors):
                qt, kt = q * fq, k * fk
                dqt = _mm(da * m_ref[l], kt.astype(BF16))
                dkt = _mm(da_t * m_ref[N_TERMS + l], qt.astype(BF16))
                dq = dq + dqt * fq
                dk = dk + dkt * fk
                diff = dqt * qt - dkt * kt
                if l == 0:
                    db = diff
                else:
                    dg_cum = dg_cum + diff
                    dref_scr[h, l - 1] = diff
            dq_ref[:, kcols] = (dq * (GLA_K ** -0.5)).astype(BF16)
            dk_ref[:, kcols] = dk.astype(BF16)
            g_scr[h] = dg_cum
            g_scr[h, pl.ds(TILE - 1, 1), :] += e_last * _col_sum(d_out * st_in) + _col_sum(dkg_kg)
            for lvl, blk in enumerate(GLA_LEVELS):
                for n in range(TILE // blk):
                    g_scr[h, pl.ds(n * blk + blk // 2 - 1, 1), :] -= _col_sum(dref_scr[h, lvl, n * blk:(n + 1) * blk, :])
            dla_parts.append(_join3(_mm(cumb_ref[...], jnp.concatenate([_split3(g_scr[h]), _split3(db)], axis=0))))
        dz = jnp.concatenate(dla_parts, axis=1) * (1.0 / GLA_TAU) * _sigmoid(-z_all)
        dzb = dz.astype(BF16)
        wgu_b = wgu_ref[...].astype(BF16)
        for h in range(GLA_HEADS):
            kcols = slice(h * GLA_K, (h + 1) * GLA_K)
            dglr_ref[h] = _mm_nt(dzb[:, kcols], wgu_b[:, kcols]).astype(BF16)
        dwgu_ref[...] += _mm(glr_ref[...].T.astype(BF16), dzb)
        dbg_ref[...] += _col_sum(dz)

    row = lambda w: pl.BlockSpec((TILE, w), lambda j: (nt - 1 - j, 0))
    whole = lambda *shape: pl.BlockSpec(shape, lambda j: (0,) * len(shape))
    return _call(
        body, "gla_bwd", grid=(nt,),
        out_shape=[jax.ShapeDtypeStruct((t_rows, GLA_HEADS * GLA_K), BF16), jax.ShapeDtypeStruct((t_rows, GLA_HEADS * GLA_K), BF16),
                   jax.ShapeDtypeStruct((t_rows, GLA_W), BF16), jax.ShapeDtypeStruct((t_rows, GLA_W), BF16),
                   jax.ShapeDtypeStruct((GLA_HEADS, t_rows, LANES), BF16), jax.ShapeDtypeStruct((LANES, GLA_HEADS * GLA_K), F32),
                   jax.ShapeDtypeStruct((1, GLA_HEADS * GLA_K), F32), jax.ShapeDtypeStruct((1, GLA_W), F32)],
        in_specs=_proj_specs(("gq", "gk", "gv", "gg"), 1, lambda j: (0, nt - 1 - j)) + [row(LANES),
                  whole(LANES, GLA_HEADS * GLA_K), whole(1, GLA_HEADS * GLA_K), whole(1, GLA_W),
                  whole(2 * N_TERMS, TILE, TILE), whole(2 * TILE, TILE), whole(TILE, 2 * TILE), row(GLA_W), row(GLA_W),
                  pl.BlockSpec((GLA_HEADS, None, GLA_V, GLA_K), lambda j: (0, nt - 1 - j, 0, 0)),
                  pl.BlockSpec((GLA_HEADS, TILE, TILE), lambda j: (0, nt - 1 - j, 0))],
        out_specs=[row(GLA_HEADS * GLA_K), row(GLA_HEADS * GLA_K), row(GLA_W), row(GLA_W),
                   pl.BlockSpec((GLA_HEADS, TILE, LANES), lambda j: (0, nt - 1 - j, 0)), whole(LANES, GLA_HEADS * GLA_K),
                   whole(1, GLA_HEADS * GLA_K), whole(1, GLA_W)],
        scratch_shapes=[pltpu.VMEM((GLA_HEADS, GLA_V, GLA_K), F32), pltpu.VMEM((GLA_HEADS, TILE, GLA_K), F32),
                        pltpu.VMEM((GLA_HEADS, len(GLA_LEVELS), TILE, GLA_K), F32),
                        pltpu.VMEM((GLA_HEADS, len(GLA_LEVELS), TILE, GLA_K), F32)],
        compiler_params=_params(("arbitrary",)),
    )(proj, proj, proj, proj, glr, wgu_pad, b_gate, gain, masks.astype(BF16), cum_fwd, cum_bwd, o_raw, do_gla, states, a_t)


def _merge_fwd_bwd(o_ret, o_gla, proj, x, target, g_final, w_br, w_bg, w_out):
    t_rows = x.shape[0] + TILE
    nt = t_rows // TILE

    def body(oret_ref, ogla_ref, mr_ref, mg_ref, h0_ref, tgt_ref, gf_ref, wbr_hbm, wbg_hbm, wout_hbm,
             dh1_ref, dmr_ref, dmg_ref, doret_ref, dogla_ref, loss_ref, dgf_ref, dwbr_hbm, dwbg_hbm, dwout_hbm,
             wbr, wbg, wout, abr, abg, aout, sem):
        i = pl.program_id(0)

        @pl.when(i == 0)
        def _():
            cps = [pltpu.make_async_copy(s, d, sem.at[n]) for n, (s, d) in enumerate(((wbr_hbm, wbr), (wbg_hbm, wbg), (wout_hbm, wout)))]
            for cp in cps:
                cp.start()
            abr[...] = jnp.zeros_like(abr)
            abg[...] = jnp.zeros_like(abg)
            aout[...] = jnp.zeros_like(aout)
            loss_ref[...] = jnp.zeros_like(loss_ref)
            dgf_ref[...] = jnp.zeros_like(dgf_ref)
            for cp in cps:
                cp.wait()
            dh1_ref[...] = jnp.zeros_like(dh1_ref)
            dmr_ref[...] = jnp.zeros_like(dmr_ref)
            dmg_ref[...] = jnp.zeros_like(dmg_ref)
            doret_ref[...] = jnp.zeros_like(doret_ref)
            dogla_ref[...] = jnp.zeros_like(dogla_ref)

        @pl.when(i > 0)
        def _():
            oret, ogla = oret_ref[...], ogla_ref[...]
            br, bg = _mm(oret, wbr[...]), _mm(ogla, wbg[...])
            sr, sg = _sigmoid(_cols(mr_ref).astype(F32)), _sigmoid(_cols(mg_ref).astype(F32))
            mb = (sr * br + sg * bg).astype(BF16)
            h1 = h0_ref[...] + _mm(mb, wout[...])
            r2 = lax.rsqrt(_row_mean(h1 * h1) + EPS)
            hn = h1 * r2
            gf = gf_ref[...]
            diff = hn * gf - tgt_ref[...]
            loss_ref[...] += 0.5 * jnp.sum(_row_mean(diff * diff))
            dy = diff * (1.0 / D_MODEL)
            dgf_ref[...] += _col_sum(dy * hn)
            dyg = dy * gf
            dh1 = r2 * (dyg - hn * _row_mean(dyg * hn))
            dh1_ref[...] = dh1
            dh1b = dh1.astype(BF16)
            dm = _mm_nt(dh1b, wout[...])
            aout[...] += _mm_tn(mb, dh1b)
            dbr = (dm * sr).astype(BF16)
            dbg = (dm * sg).astype(BF16)
            dmr_ref[...] = (dm * br * sr * (1.0 - sr)).astype(BF16)
            dmg_ref[...] = (dm * bg * sg * (1.0 - sg)).astype(BF16)
            doret_ref[...] = _mm_nt(dbr, wbr[...]).astype(BF16)
            dogla_ref[...] = _mm_nt(dbg, wbg[...]).astype(BF16)
            abr[...] += _mm_tn(oret, dbr)
            abg[...] += _mm_tn(ogla, dbg)

        @pl.when(i == nt - 1)
        def _():
            wbr[...] = abr[...].astype(BF16)
            wbg[...] = abg[...].astype(BF16)
            wout[...] = aout[...].astype(BF16)
            pltpu.sync_copy(wbr, dwbr_hbm)
            pltpu.sync_copy(wbg, dwbg_hbm)
            pltpu.sync_copy(wout, dwout_hbm)

    row = lambda w: pl.BlockSpec((TILE, w), lambda i: (i, 0))
    one = lambda w: pl.BlockSpec((1, w), lambda i: (0, 0))
    return _call(
        body, "merge_fwd_bwd", grid=(nt,),
        out_shape=[jax.ShapeDtypeStruct((t_rows, D_MODEL), F32), jax.ShapeDtypeStruct((t_rows, D_MODEL), BF16),
                   jax.ShapeDtypeStruct((t_rows, D_MODEL), BF16), jax.ShapeDtypeStruct((t_rows, RET_W), BF16),
                   jax.ShapeDtypeStruct((t_rows, GLA_W), BF16), jax.ShapeDtypeStruct((1, LANES), F32),
                   jax.ShapeDtypeStruct((1, D_MODEL), F32), jax.ShapeDtypeStruct((RET_W, D_MODEL), BF16),
                   jax.ShapeDtypeStruct((GLA_W, D_MODEL), BF16), jax.ShapeDtypeStruct((D_MODEL, D_MODEL), BF16)],
        in_specs=[row(RET_W), row(GLA_W)] + _proj_specs(("mr", "mg"), 1, lambda i: (0, i)) + [_x_spec(), _x_spec(), one(D_MODEL), ANY, ANY, ANY],
        out_specs=[row(D_MODEL), row(D_MODEL), row(D_MODEL), row(RET_W), row(GLA_W), one(LANES), one(D_MODEL), ANY, ANY, ANY],
        scratch_shapes=[pltpu.VMEM((RET_W, D_MODEL), BF16), pltpu.VMEM((GLA_W, D_MODEL), BF16), pltpu.VMEM((D_MODEL, D_MODEL), BF16),
                        pltpu.VMEM((RET_W, D_MODEL), F32), pltpu.VMEM((GLA_W, D_MODEL), F32), pltpu.VMEM((D_MODEL, D_MODEL), F32),
                        pltpu.SemaphoreType.DMA((3,))],
        compiler_params=_params(("arbitrary",)),
    )(o_ret, o_gla, proj, proj, x, target, g_final, w_br, w_bg, w_out)


def _inproj_bwd_x(dseg, dglr, head, x, dh1, g_norm, slabs, w_glr, chip_partials):
    t_rows = x.shape[0] + TILE
    nt = t_rows // TILE
    ne = len(chip_partials)

    def body(*refs):
        d_refs = refs[:10]
        dglr_ref, head_ref, x_ref, dh1_ref, g_ref, slabs_a, slabs_b, wg_hbm = refs[10:18]
        part_refs = refs[18:18 + ne]
        dx_ref, dhead_ref, dgn_ref = refs[18 + ne:21 + ne]
        landed = refs[21 + ne:21 + 2 * ne]
        w_vm, wg_vm, edge_vm, sem = refs[21 + 2 * ne:25 + 2 * ne]
        exchange = _Exchange(part_refs, landed, refs[25 + 2 * ne:], among_chips=True)

        @pl.when(pl.program_id(0) == 0)
        def _():
            exchange.start()
            dgn_ref[...] = jnp.zeros_like(dgn_ref)
            _load_weight((slabs_a, slabs_b), wg_hbm, w_vm, wg_vm, edge_vm, sem)

        @pl.when(pl.program_id(0) == nt - 1)
        def _():
            exchange.finish()

        dglr = dglr_ref[0].astype(F32)
        for h in range(1, GLA_HEADS):
            dglr = dglr + dglr_ref[h].astype(F32)
        du = _mm_nt(dglr.astype(BF16), wg_vm[...])
        for s, d_ref in enumerate(d_refs):
            du = du + _mm_nt(d_ref[...], w_vm[:, SEG_OFF[s]:SEG_OFF[s] + SEG_W[s]])
        x = _tile_rows(head_ref, x_ref)
        r = lax.rsqrt(_row_mean(x * x) + EPS)
        hn = x * r
        dgn_ref[...] += _col_sum(du * hn)
        dug = du * g_ref[...]
        dh0 = dh1_ref[...] + r * (dug - hn * _row_mean(dug * hn))
        dx_ref[...] = dh0

        @pl.when(pl.program_id(0) == 0)
        def _():
            dhead_ref[...] = dh0

    row = lambda w: pl.BlockSpec((TILE, w), lambda i: (i, 0))
    one = pl.BlockSpec((1, D_MODEL), lambda i: (0, 0))
    return _call(
        body, "inproj_bwd_x", grid=(nt,),
        out_shape=[jax.ShapeDtypeStruct((t_rows - TILE, D_MODEL), F32), jax.ShapeDtypeStruct((TILE, D_MODEL), F32),
                   jax.ShapeDtypeStruct((1, D_MODEL), F32)] + [jax.ShapeDtypeStruct(a.shape, a.dtype) for a in chip_partials],
        in_specs=[row(w) for w in SEG_W] + [pl.BlockSpec((GLA_HEADS, TILE, LANES), lambda i: (0, i, 0)),
                                            _head_spec(), _x_spec(), row(D_MODEL), one, ANY, ANY, ANY] + [ANY] * ne,
        out_specs=[_x_spec(), _head_spec(), one] + [ANY] * ne,
        scratch_shapes=W_SCRATCH() + _exchange_sems(ne, N_CHIP),
        compiler_params=_params(("arbitrary",)),
    )(*[dseg[n] for n in SEG_NAMES], dglr, head, x, dh1, g_norm, *slabs, w_glr, *chip_partials)


W_TILE = 512


def _inproj_bwd_w(ut, dseg, dglr, row_sends):
    nt = ut.shape[0]
    t_rows = nt * TILE
    kc = 3 if nt % 3 == 0 else 1
    tiles = [(s, c) for s in range(len(SEG_W)) for c in range(0, SEG_W[s], W_TILE)]
    bpt = W_TILE // LANES
    nr = len(row_sends)
    n = 1 + nr
    last_tile = [(SLAB_BLK0[d] + SLAB_BLOCKS - 1) // bpt for d in range(N_DEV)]

    def body(ut_hbm, *refs):
        d_refs, dglr_hbm, row_refs = refs[:10], refs[10], refs[11:11 + nr]
        out_hbm, oglr_ref, sib = refs[11 + nr], refs[12 + nr], refs[13 + nr:13 + nr + n]
        ut_vm, dbuf, obuf, acc, gbuf, sem, send_sems, recv_sems = refs[13 + nr + n:]
        x, y, core = _position()

        def handover(d, k, landed=False):
            q = d // 2
            src = out_hbm.at[pl.ds(SLAB_BLK0[d], SLAB_BLOCKS)] if k == 0 else row_refs[k - 1].at[d]
            return pltpu.make_async_remote_copy(src_ref=sib[k].at[q] if landed else src, dst_ref=sib[k].at[q],
                                                send_sem=send_sems.at[n * q + k], recv_sem=recv_sems.at[n * q + k],
                                                device_id=(x, y, 1 - core), device_id_type=MESH)

        def for_sibling(d, ks, fn):
            @pl.when(d % 2 != core)
            def _():
                for k in ks:
                    fn(handover(d, k))

        for d in range(N_DEV):
            for_sibling(d, range(1, n), lambda cp: cp.start())

        def fetch(i):
            s, c = tiles[i]
            return pltpu.make_async_copy(d_refs[s].at[:, pl.ds(c, W_TILE)], dbuf.at[i % 2], sem.at[1 + i % 2])

        def contract(rhs_refs, width):
            acc[:, :width] = jnp.zeros((D_MODEL, width), F32)

            def step(k, carry):
                part = None
                for j in range(kc):
                    kk = k * kc + j
                    for rhs_ref in rhs_refs:
                        prod = _mm(ut_vm[kk], rhs_ref[pl.ds(pl.multiple_of(kk * TILE, TILE), TILE), :])
                        part = prod if part is None else part + prod
                acc[:, :width] += part
                return carry

            lax.fori_loop(0, nt // kc, step, 0)
            return acc[:, :width]

        load_ut = pltpu.make_async_copy(ut_hbm, ut_vm, sem.at[0])
        load_glr = pltpu.make_async_copy(dglr_hbm, gbuf, sem.at[5])
        load_ut.start()
        load_glr.start()
        fetch(0).start()
        load_ut.wait()
        stores = {}

        def stored(i):
            stores[i].wait()
            for d in range(N_DEV):
                if last_tile[d] == i:
                    for_sibling(d, [0], lambda cp: cp.start())

        for i, (s, c) in enumerate(tiles):
            if i + 1 < len(tiles):
                fetch(i + 1).start()
            fetch(i).wait()
            if i >= 2:
                stored(i - 2)
            total = contract([dbuf.at[i % 2]], W_TILE)
            for j in range(bpt):
                obuf[i % 2, j] = total[:, j * LANES:(j + 1) * LANES].astype(BF16)
            blk0 = (SEG_OFF[s] + c) // LANES
            stores[i] = pltpu.make_async_copy(obuf.at[i % 2], out_hbm.at[pl.ds(blk0, bpt)], sem.at[3 + i % 2])
            stores[i].start()
        for i in range(max(0, len(tiles) - 2), len(tiles)):
            stored(i)
        load_glr.wait()
        head_sum = gbuf[0].astype(F32)
        for h in range(1, GLA_HEADS):
            head_sum = head_sum + gbuf[h].astype(F32)
        gbuf[0] = head_sum.astype(BF16)
        oglr_ref[...] = contract([gbuf.at[0]], LANES)
        for q in range(N_CHIP):
            for k in range(n):
                handover(2 * q, k, landed=True).wait_recv()
        for d in range(N_DEV):
            for_sibling(d, range(n), lambda cp: cp.wait_send())

    outs = _call(
        body, "inproj_bwd_w",
        out_shape=[jax.ShapeDtypeStruct((AL_COLS // LANES, D_MODEL, LANES), BF16), jax.ShapeDtypeStruct((D_MODEL, LANES), F32),
                   jax.ShapeDtypeStruct((N_CHIP, SLAB_BLOCKS, D_MODEL, LANES), BF16)]
                  + [jax.ShapeDtypeStruct((N_CHIP, *r.shape[1:]), BF16) for r in row_sends],
        in_specs=[ANY] * (12 + nr), out_specs=[ANY, pl.BlockSpec(memory_space=pltpu.VMEM)] + [ANY] * n,
        scratch_shapes=[pltpu.VMEM((nt, D_MODEL, TILE), BF16), pltpu.VMEM((2, t_rows, W_TILE), BF16),
                        pltpu.VMEM((2, bpt, D_MODEL, LANES), BF16), pltpu.VMEM((D_MODEL, W_TILE), F32),
                        pltpu.VMEM((GLA_HEADS, t_rows, LANES), BF16), pltpu.SemaphoreType.DMA((6,)),
                        pltpu.SemaphoreType.DMA((n * N_CHIP,)), pltpu.SemaphoreType.DMA((n * N_CHIP,))],
        compiler_params=_params(),
    )(ut, *[dseg[n_] for n_ in SEG_NAMES], dglr, *row_sends)
    return outs[0], outs[1], outs[2], outs[3:]


def _position():
    x, y, c = lax.axis_index("x"), lax.axis_index("y"), lax.axis_index("c")
    return x, y, c


def _index(px, py, pc):
    return 4 * px + 2 * py + pc


def _gather_sems(n):
    return [pltpu.SemaphoreType.DMA((7 * n,)), pltpu.SemaphoreType.DMA((7 * n,)), pltpu.SemaphoreType.DMA((n,))]


class _RelayGather:
    STAGES = 5

    def __init__(self, ins, outs, sems):
        self.ins, self.outs, self.n = ins, outs, len(ins)
        self.send_sems, self.recv_sems, self.local_sems = sems
        x, y, c = _position()
        self.c, self.me, self.sibling = c, (x, y, c), (x, y, 1 - c)
        self.chips = [(1 - x, y), (x, 1 - y), (1 - x, 1 - y)]

    def _copy(self, a, k, block, to, src=None):
        dst = self.outs[a].at[_index(*block)]
        return pltpu.make_async_remote_copy(src_ref=dst if src is None else src, dst_ref=dst,
                                            send_sem=self.send_sems.at[7 * a + k], recv_sem=self.recv_sems.at[7 * a + k],
                                            device_id=to, device_id_type=MESH)

    def _relay(self, a, j):
        return self._copy(a, 3, (*self.chips[j], self.c), (*self.chips[1 - j], self.c))

    def _mine(self):
        return [pltpu.make_async_copy(self.ins[a], self.outs[a].at[_index(*self.me)], self.local_sems.at[a]) for a in range(self.n)]

    def _first(self):
        first = []
        for a in range(self.n):
            first.append(self._copy(a, 0, self.me, self.sibling, src=self.ins[a]))
            first += [self._copy(a, 1 + j, self.me, (*self.chips[j], self.c), src=self.ins[a]) for j in range(2)]
        return first

    def _passed(self, j):
        return [self._copy(a, 4 + j, (*self.chips[j], self.c), self.sibling) for a in range(self.n)]

    def carried(self, step, nt):
        for at, stages in ((0, (0,)), (nt // 3, (1, 2)), (2 * nt // 3, (3,)), (nt - 1, (4,))):
            @pl.when(step == at)
            def _():
                for s in stages:
                    self.stage(s)

    def stage(self, s):
        n, c = self.n, self.c
        if s == 0:
            for cp in self._mine() + self._first():
                cp.start()
        elif s < 4:
            j = s - 1
            for a in range(n):
                self._copy(a, 1 + j, (*self.chips[j], c), self.me).wait_recv()
            for cp in self._passed(j):
                cp.start()
            if j < 2:
                @pl.when(c == j)
                def _():
                    for a in range(n):
                        self._relay(a, j).start()
        else:
            for a in range(n):
                self._copy(a, 0, self.sibling, self.me).wait_recv()
                for j in range(3):
                    self._copy(a, 4 + j, (*self.chips[j], 1 - c), self.me).wait_recv()
            for cp in self._first() + self._passed(0) + self._passed(1) + self._passed(2):
                cp.wait_send()
            for j in range(2):
                @pl.when(c == j)
                def _():
                    for a in range(n):
                        self._relay(a, j).wait_send()
            for cp in self._mine():
                cp.wait()


def _all_gather(arrs, name):
    n = len(arrs)

    def body(*refs):
        gather = _RelayGather(refs[:n], refs[n:2 * n], refs[2 * n:])
        for s in range(gather.STAGES):
            gather.stage(s)

    return _call(
        body, name,
        out_shape=[jax.ShapeDtypeStruct((N_DEV, *a.shape), a.dtype) for a in arrs],
        in_specs=[ANY] * n, out_specs=[ANY] * n, scratch_shapes=_gather_sems(n),
    )(*arrs)


N_CHIP = N_DEV // 2


def _slab_block0(owner):
    step = SLAB_BLK0[1]
    assert all(SLAB_BLK0[d] == step * d - (d == N_DEV - 1) for d in range(N_DEV))
    return step * owner - jnp.where(owner == N_DEV - 1, 1, 0)


def _add_bf16(c_ref, a_ref, b_ref, o_ref):
    o_ref[...] = (a_ref[...].astype(F32) + b_ref[...].astype(F32)).astype(BF16)


def _chip_partial_slab(dw_blocks, sib, core):
    blk = pl.BlockSpec((None, SLAB_BLOCKS, D_MODEL, LANES), lambda q, c_ref: (q, 0, 0, 0))
    return _call(
        functools.partial(_add_bf16), "chip_partial_w_in", out_shape=jax.ShapeDtypeStruct(sib.shape, BF16),
        grid_spec=pltpu.PrefetchScalarGridSpec(
            num_scalar_prefetch=1, grid=(N_CHIP,),
            in_specs=[pl.BlockSpec((pl.Element(SLAB_BLOCKS), pl.Element(D_MODEL), pl.Element(LANES)),
                                   lambda q, c_ref: (_slab_block0(2 * q + c_ref[0]), 0, 0)), blk],
            out_specs=blk),
        compiler_params=_params(("arbitrary",)),
    )(core, dw_blocks, sib)


def _chip_partial_rows(sends, sibs, core):
    n = len(sends)

    def body(c_ref, *refs):
        for k in range(n):
            _add_bf16(c_ref, refs[k], refs[n + k], refs[2 * n + k])

    own = [pl.BlockSpec((None, *a.shape[1:]), lambda q, c_ref: (2 * q + c_ref[0], 0, 0)) for a in sends]
    blk = [pl.BlockSpec((None, *a.shape[1:]), lambda q, c_ref: (q, 0, 0)) for a in sibs]
    return _call(
        body, "chip_partial_rows", out_shape=[jax.ShapeDtypeStruct(a.shape, BF16) for a in sibs],
        grid_spec=pltpu.PrefetchScalarGridSpec(num_scalar_prefetch=1, grid=(N_CHIP,), in_specs=own + blk, out_specs=blk),
        compiler_params=_params(("arbitrary",)),
    )(core, *sends, *sibs)


def _exchange_sems(n_arrays, n_peers):
    return [pltpu.SemaphoreType.DMA((n_arrays * n_peers,)), pltpu.SemaphoreType.DMA((n_arrays * n_peers,)),
            pltpu.SemaphoreType.DMA((n_arrays,))]


class _Exchange:
    def __init__(self, srcs, dsts, sems, among_chips):
        self.arrs = list(zip(srcs, dsts))
        self.n = len(self.arrs)
        self.send_sems, self.recv_sems, self.local_sems = sems
        self.among_chips = among_chips
        x, y, c = _position()
        self.c = c
        self.me = 2 * x + y if among_chips else _index(x, y, c)
        self.n_peers = N_CHIP if among_chips else N_DEV

    def _device(self, p):
        return (p // 2, p % 2, self.c) if self.among_chips else (p // 4, (p // 2) % 2, p % 2)

    def _src(self, k, p):
        src = self.arrs[k][0]
        return src.at[p] if self.among_chips else src

    def _mine(self):
        return [pltpu.make_async_copy(self._src(k, self.me), self.arrs[k][1].at[self.me], self.local_sems.at[k]) for k in range(self.n)]

    def _copy(self, p, k, landing):
        return pltpu.make_async_remote_copy(
            src_ref=self._src(k, p), dst_ref=self.arrs[k][1].at[landing], send_sem=self.send_sems.at[self.n * p + k],
            recv_sem=self.recv_sems.at[self.n * landing + k], device_id=self._device(p), device_id_type=MESH)

    def _others(self, fn):
        for p in range(self.n_peers):
            @pl.when(p != self.me)
            def _():
                for k in range(self.n):
                    fn(p, k)

    def start(self):
        for cp in self._mine():
            cp.start()
        self._others(lambda p, k: self._copy(p, k, self.me).start())

    def finish(self):
        self._others(lambda p, k: self._copy(p, k, p).wait_recv())
        self._others(lambda p, k: self._copy(p, k, self.me).wait_send())
        for cp in self._mine():
            cp.wait()


def _adamw(g, w, m, v):
    m_new = ADAM_B1 * m + (1.0 - ADAM_B1) * g
    v_new = ADAM_B2 * v + (1.0 - ADAM_B2) * (g * g)
    m_hat = m_new / (1.0 - ADAM_B1 ** ADAM_STEP)
    v_hat = v_new / (1.0 - ADAM_B2 ** ADAM_STEP)
    delta = -ADAM_LR * (m_hat / (jnp.sqrt(v_hat) + ADAM_EPS) + ADAM_WD * w)
    return delta, m_new, v_new


def _sum_partials(p_ref):
    g = p_ref[0].astype(F32)
    for d in range(1, p_ref.shape[0]):
        g = g + p_ref[d].astype(F32)
    return g


def _reduce_adam_rows(parts, ws, ms, vs):
    n = len(ws)

    def body(*refs):
        ins, outs = refs[:4 * n], refs[4 * n:]
        for k in range(n):
            p_ref, w_ref, m_ref, v_ref = ins[k], ins[n + k], ins[2 * n + k], ins[3 * n + k]
            g = _sum_partials(p_ref)
            outs[4 * k][...] = g
            outs[4 * k + 1][...], outs[4 * k + 2][...], outs[4 * k + 3][...] = _adamw(g, w_ref[...], m_ref[...], v_ref[...])

    outs = _call(
        body, "adam_row_weights", out_shape=[jax.ShapeDtypeStruct(w.shape, F32) for w in ws for _ in range(4)],
        compiler_params=_params(),
    )(*parts, *ws, *ms, *vs)
    return [tuple(outs[4 * k:4 * k + 4]) for k in range(n)]


def _reduce_adam_slab(parts, glr, w_t, m_t, v_t, me):
    cols, rows = w_t.shape
    shift = jnp.asarray(SLAB_SHIFT, jnp.int32)[me]
    glr_at = jnp.where(me == GLR_DEV, GLR_LOCAL, cols).astype(jnp.int32)

    def body(s_ref, p_ref, glr_ref, w_ref, m_ref, v_ref, g_ref, d_ref, mo_ref, vo_ref, slab_t):
        shift, glr_at = s_ref[0], s_ref[1]
        tall = jnp.concatenate([_sum_partials(p_ref.at[:, j]).T for j in range(SLAB_BLOCKS)], axis=0)
        before = pltpu.roll(tall, SLAB_W - shift, 0)
        after = pltpu.roll(tall, lax.rem(SLAB_W - shift + GLA_RANK, SLAB_W), 0)
        wide = jnp.concatenate([glr_ref[...].T, jnp.zeros((SLAB_W - LANES, LANES), F32)], axis=0)
        placed = pltpu.roll(wide, lax.rem(glr_at, SLAB_W), 0)
        row = lax.broadcasted_iota(jnp.int32, (SLAB_W, LANES), 0)
        slab_t[...] = jnp.where(row < glr_at, before, jnp.where(row < glr_at + GLA_RANK, placed, after))
        g = slab_t[pl.ds(0, cols), :]
        g_ref[...] = g
        d_ref[...], mo_ref[...], vo_ref[...] = _adamw(g, w_ref[...], m_ref[...], v_ref[...])

    blk = pl.BlockSpec((cols, LANES), lambda i, s: (0, i))
    return _call(
        body, "adam_w_in", out_shape=[jax.ShapeDtypeStruct((cols, rows), F32)] * 4,
        grid_spec=pltpu.PrefetchScalarGridSpec(
            num_scalar_prefetch=1, grid=(rows // LANES,),
            in_specs=[pl.BlockSpec((parts.shape[0], SLAB_BLOCKS, LANES, LANES), lambda i, s: (0, 0, i, 0)),
                      pl.BlockSpec((LANES, LANES), lambda i, s: (i, 0)), blk, blk, blk],
            out_specs=[blk] * 4, scratch_shapes=[pltpu.VMEM((SLAB_W, LANES), F32)]),
        compiler_params=_params(("arbitrary",)),
    )(jnp.stack([shift, glr_at]), parts, glr, w_t, m_t, v_t)


def _reduce_small(parts):
    def body(p_ref, o_ref):
        o_ref[...] = _sum_partials(p_ref)

    return _call(body, "reduce_small", out_shape=jax.ShapeDtypeStruct(parts.shape[1:], F32))(parts)


def _adam_small(g, w, m, v):
    def body(g_ref, w_ref, m_ref, v_ref, d_ref, mo_ref, vo_ref):
        d_ref[...], mo_ref[...], vo_ref[...] = _adamw(g_ref[...], w_ref[...], m_ref[...], v_ref[...])

    return _call(body, "adam_small", out_shape=[jax.ShapeDtypeStruct(g.shape, F32)] * 3)(g, w, m, v)


def _pack_rows(arrs):
    rows = []
    for a in arrs:
        flat = a.reshape(-1).astype(F32)
        pad = (-flat.shape[0]) % LANES
        rows.append(jnp.pad(flat, (0, pad)).reshape(-1, LANES))
    packed = jnp.concatenate(rows, axis=0)
    return jnp.pad(packed, ((0, (-packed.shape[0]) % 8), (0, 0)))


def _unpack_rows(packed, shapes):
    out, r = [], 0
    for shp in shapes:
        size = 1
        for s in shp:
            size *= s
        nrows = -(-size // LANES)
        out.append(packed[r:r + nrows].reshape(-1)[:size].reshape(shp))
        r += nrows
    return out


def _shard_to_slab(shard, d):
    glr = jnp.zeros((D_MODEL, GLA_RANK), shard.dtype)
    if d == GLR_DEV:
        glr = shard[:, GLR_LOCAL:GLR_LOCAL + GLA_RANK]
        shard = jnp.concatenate([shard[:, :GLR_LOCAL], shard[:, GLR_LOCAL + GLA_RANK:]], axis=1)
    return jnp.pad(shard, ((0, 0), (SLAB_SHIFT[d], SLAB_W - SLAB_SHIFT[d] - shard.shape[1]))), glr


def kernel(x, meta_tokens, norm_gain, w_in, w_gate_up, b_gate, ret_norm_gain, gla_norm_gain, w_branch_ret, w_branch_gla, w_out, final_norm_gain, loss_target, m_meta_tokens, m_norm_gain, m_w_in, m_w_gate_up, m_b_gate, m_ret_norm_gain, m_gla_norm_gain, m_w_branch_ret, m_w_branch_gla, m_w_out, m_final_norm_gain, v_meta_tokens, v_norm_gain, v_w_in, v_w_gate_up, v_b_gate, v_ret_norm_gain, v_gla_norm_gain, v_w_branch_ret, v_w_branch_gla, v_w_out, v_final_norm_gain):
    xi, yi, ci = _position()
    me = _index(xi, yi, ci)
    seq = x.shape[1]
    t_rows = seq + TILE
    in_shard = w_in.shape[2]
    gu_shard = w_gate_up.shape[2]
    meta_shard = meta_tokens.shape[1]
    ret_rows, gla_rows, out_rows = w_branch_ret.shape[1], w_branch_gla.shape[1], w_out.shape[1]

    assert in_shard == IN_SHARD
    slab_local, glr_local = lax.switch(me, [functools.partial(_shard_to_slab, d=d) for d in range(N_DEV)], w_in[0])
    small_local = jnp.concatenate([meta_tokens, jnp.pad(w_gate_up[0], ((0, 0), (0, LANES - gu_shard))),
                                   glr_local.reshape(-1, LANES)], axis=0)
    slab_local = slab_local.astype(BF16)
    first_halves, g_small = _all_gather([slab_local[:, :HALF_W], small_local], "all_gather_shards")
    n_small = N_META + GLA_RANK
    w_glr = jnp.pad(g_small[GLR_DEV, n_small:].reshape(D_MODEL, GLA_RANK), ((0, 0), (0, LANES - GLA_RANK))).astype(BF16)
    meta_full = jnp.transpose(g_small[:, :N_META, :], (1, 0, 2)).reshape(N_META, D_MODEL)
    wgu_full = jnp.transpose(g_small[:, N_META:n_small, :gu_shard], (1, 0, 2)).reshape(GLA_RANK, GLA_HEADS * GLA_K)
    wgu_pad = jnp.pad(wgu_full, ((0, LANES - GLA_RANK), (0, 0)))

    rope = _rope_tables(t_rows // TILE)
    lg = jnp.log1p(-(2.0 ** (-5.0 - jnp.arange(RET_HEADS, dtype=F32))))

    head = jnp.concatenate([jnp.zeros((PAD_ROWS, D_MODEL), F32), meta_full], axis=0)
    proj_first, second_halves = _inproj_first(head, x[0], norm_gain, first_halves, slab_local[:, HALF_W:])
    slabs = (first_halves, second_halves)
    ut, proj, glr = _inproj_tiles(head, x[0], norm_gain, slabs, w_glr, proj_first)
    o_ret_raw, o_ret, ret_states, (g_br, g_bg, g_o) = _ret_fwd(
        proj, rope, ret_norm_gain, lg, [w_branch_ret[0].astype(BF16), w_branch_gla[0].astype(BF16), w_out[0].astype(BF16)])
    w_br, w_bg, w_o = g_br.reshape(RET_W, D_MODEL), g_bg.reshape(GLA_W, D_MODEL), g_o.reshape(D_MODEL, D_MODEL)
    masks, cum_fwd, cum_bwd = _gla_tables()
    o_gla_raw, o_gla, gla_states, gla_scores_t = _gla_fwd(proj, glr, wgu_pad, b_gate, gla_norm_gain, masks, cum_fwd)
    (dh1, d_mr, d_mg, do_ret, do_gla, loss_part, d_gfinal, dw_br, dw_bg, dw_o) = _merge_fwd_bwd(
        o_ret, o_gla, proj, x[0], loss_target[0], final_norm_gain.reshape(1, D_MODEL), w_br, w_bg, w_o)

    d_rq, d_rk, d_rv, d_rg, d_gret = _ret_bwd(proj, rope, ret_norm_gain, lg, o_ret_raw, do_ret, ret_states)
    d_gq, d_gk, d_gv, d_gg, dglr_parts, d_wgu, d_bgate, d_ggla = _gla_bwd(
        proj, glr, wgu_pad, b_gate, gla_norm_gain, o_gla_raw, do_gla, gla_states, gla_scores_t, masks, cum_fwd, cum_bwd)
    dseg = dict(rq=d_rq, rk=d_rk, rv=d_rv, rg=d_rg, gq=d_gq, gk=d_gk, gv=d_gv, gg=d_gg, mr=d_mr, mg=d_mg)
    row_sends = [dw_br.reshape(N_DEV, ret_rows, D_MODEL), dw_bg.reshape(N_DEV, gla_rows, D_MODEL),
                 dw_o.reshape(N_DEV, out_rows, D_MODEL)]
    dw_blocks, dw_glr, sib_in, sib_rows = _inproj_bwd_w(ut, dseg, dglr_parts, row_sends)
    core = ci.astype(jnp.int32).reshape(1)
    chip_partials = [_chip_partial_slab(dw_blocks, sib_in, core)] + list(_chip_partial_rows(row_sends, list(sib_rows), core))
    grad_x, d_head, d_gnorm, p_in, p_br, p_bg, p_o = _inproj_bwd_x(
        dseg, dglr_parts, head, x[0], dh1, norm_gain, slabs, w_glr, chip_partials)
    small_shapes = [(N_META, D_MODEL), (1, D_MODEL), (GLA_RANK, GLA_HEADS * GLA_K), (1, GLA_HEADS * GLA_K),
                    (1, RET_W), (1, GLA_W), (1, D_MODEL), (1, LANES), (D_MODEL, GLA_RANK)]
    small_part = _pack_rows([d_head[PAD_ROWS:], d_gnorm, d_wgu[:GLA_RANK], d_bgate, d_gret, d_ggla, d_gfinal, loss_part,
                             dw_glr[:, :GLA_RANK]])
    (p_small,) = _all_gather([small_part], "all_gather_small_partials")

    (g_meta_f, g_gnorm, g_wgu_f, g_bgate, g_gret, g_ggla, g_gfinal, loss_all,
     g_wglr) = _unpack_rows(_reduce_small(p_small), small_shapes)
    g_w_in, d_w_in, nm_w_in, nv_w_in = [a.T for a in _reduce_adam_slab(
        p_in, jnp.pad(g_wglr, ((0, 0), (0, LANES - GLA_RANK))), w_in[0].T, m_w_in[0].T, v_w_in[0].T, me)]
    ((g_w_br, d_w_br, nm_w_br, nv_w_br), (g_w_bg, d_w_bg, nm_w_bg, nv_w_bg), (g_w_o, d_w_o, nm_w_o, nv_w_o)) = _reduce_adam_rows(
        [p_br, p_bg, p_o], [w_branch_ret[0], w_branch_gla[0], w_out[0]], [m_w_branch_ret[0], m_w_branch_gla[0], m_w_out[0]],
        [v_w_branch_ret[0], v_w_branch_gla[0], v_w_out[0]])
    g_meta = lax.dynamic_slice_in_dim(g_meta_f, me * meta_shard, meta_shard, axis=1)
    g_wgu = lax.dynamic_slice_in_dim(g_wgu_f, me * gu_shard, gu_shard, axis=1)
    s_g = [g_meta, g_gnorm, g_wgu, g_bgate, g_gret, g_ggla, g_gfinal]
    s_w = [meta_tokens, norm_gain, w_gate_up[0], b_gate, ret_norm_gain, gla_norm_gain, final_norm_gain]
    s_m = [m_meta_tokens, m_norm_gain, m_w_gate_up[0], m_b_gate, m_ret_norm_gain, m_gla_norm_gain, m_final_norm_gain]
    s_v = [v_meta_tokens, v_norm_gain, v_w_gate_up[0], v_b_gate, v_ret_norm_gain, v_gla_norm_gain, v_final_norm_gain]
    shapes = [a.shape for a in s_g]
    s_d, s_nm, s_nv = [_unpack_rows(p, shapes) for p in _adam_small(*[_pack_rows(l) for l in (s_g, s_w, s_m, s_v)])]

    loss = loss_all[0, 0]
    grad_x = grad_x[None]

    def order(meta, gnorm, win, wgu, bgate, gret, ggla, wbr, wbg, wo, gfin):
        return (meta, gnorm, win[None], wgu[None], bgate, gret, ggla, wbr[None], wbg[None], wo[None], gfin.reshape(final_norm_gain.shape))

    def small(l):
        return dict(meta=l[0], gnorm=l[1], wgu=l[2], bgate=l[3], gret=l[4], ggla=l[5], gfin=l[6])

    grads = order(win=g_w_in, wbr=g_w_br, wbg=g_w_bg, wo=g_w_o, **small(s_g))
    deltas = order(win=d_w_in, wbr=d_w_br, wbg=d_w_bg, wo=d_w_o, **small(s_d))
    new_m = order(win=nm_w_in, wbr=nm_w_br, wbg=nm_w_bg, wo=nm_w_o, **small(s_nm))
    new_v = order(win=nv_w_in, wbr=nv_w_br, wbg=nv_w_bg, wo=nv_w_o, **small(s_nv))
    return (loss, grad_x, *grads, *deltas, *new_m, *new_v)
```

```python
import functools

import jax
import jax.numpy as jnp
from jax import lax
from jax.experimental import pallas as pl
from jax.experimental.pallas import tpu as pltpu

F32 = jnp.float32
BF16 = jnp.bfloat16

D_MODEL = 1024
N_META = 16
TILE = 256
PAD_ROWS = TILE - N_META
RET_HEADS = 4
RET_QK = 256
RET_V = 512
RET_W = RET_HEADS * RET_V
GLA_HEADS = 4
GLA_K = 128
GLA_V = 256
GLA_W = GLA_HEADS * GLA_V
GLA_RANK = 16
GLA_TAU = 16.0
GLA_CHUNK = 16
ROPE_BASE = 10000.0
EPS = 1e-6
LANES = 128
N_DEV = 8
SEG_NAMES = ("rq", "rk", "rv", "rg", "gq", "gk", "gv", "gg", "mr", "mg")
SEG_W = (1024, 1024, 2048, 2048, 512, 512, 1024, 1024, 1024, 1024)
SEG_OFF = tuple(sum(SEG_W[:i]) for i in range(len(SEG_W)))
AL_COLS = sum(SEG_W)
IN_COLS = AL_COLS + GLA_RANK
GLR_OFF = sum(SEG_W[:8])
IN_SHARD = IN_COLS // N_DEV


def _aligned_col(c):
    assert c <= GLR_OFF or c >= GLR_OFF + GLA_RANK
    return c if c <= GLR_OFF else c - GLA_RANK


SLAB_BOUND = tuple(_aligned_col(IN_SHARD * d) for d in range(N_DEV + 1))
SLAB_BLK0 = tuple(b // LANES for b in SLAB_BOUND[:-1])
SLAB_SHIFT = tuple(b % LANES for b in SLAB_BOUND[:-1])
SLAB_BLOCKS = max(-(-SLAB_BOUND[d + 1] // LANES) - SLAB_BLK0[d] for d in range(N_DEV))
SLAB_W = SLAB_BLOCKS * LANES
GLR_DEV = GLR_OFF // IN_SHARD
GLR_LOCAL = GLR_OFF - GLR_DEV * IN_SHARD
assert all(SLAB_BLK0[d] + SLAB_BLOCKS <= AL_COLS // LANES for d in range(N_DEV))
VMEM_LIMIT = 58 * 1024 * 1024
ADAM_LR, ADAM_B1, ADAM_B2, ADAM_EPS, ADAM_WD, ADAM_STEP = 0.001, 0.9, 0.999, 1e-08, 0.01, 10
ANY = pl.BlockSpec(memory_space=pl.ANY)
MESH = pl.DeviceIdType.MESH


def _call(body, name, **kw):
    return pl.pallas_call(body, name=name, **kw)


def _params(sem=None):
    return pltpu.CompilerParams(dimension_semantics=sem, vmem_limit_bytes=VMEM_LIMIT)


def _mm(a, b):
    return jnp.dot(a, b, preferred_element_type=F32)


def _mm_nt(a, b):
    return lax.dot_general(a, b, (((1,), (1,)), ((), ())), preferred_element_type=F32)


def _mm_tn(a, b):
    return lax.dot_general(a, b, (((0,), (0,)), ((), ())), preferred_element_type=F32)


def _sigmoid(x):
    return jax.nn.sigmoid(x)


def _rope(t, cos, sin):
    half = t.shape[-1] // 2
    t1, t2 = t[:, :half], t[:, half:]
    return jnp.concatenate([t1 * cos - t2 * sin, t2 * cos + t1 * sin], axis=-1)


def _rope_bwd(g, cos, sin):
    half = g.shape[-1] // 2
    g1, g2 = g[:, :half], g[:, half:]
    return jnp.concatenate([g1 * cos + g2 * sin, g2 * cos - g1 * sin], axis=-1)


def _row_mean(x):
    return jnp.mean(x, axis=-1, keepdims=True)


def _col_sum(x):
    return jnp.sum(x, axis=0, keepdims=True)


def _tile_rows(head_ref, x_ref):
    return jnp.where(pl.program_id(0) == 0, head_ref[...], x_ref[...])


def _head_spec():
    return pl.BlockSpec((TILE, D_MODEL), lambda i: (0, 0))


def _x_spec():
    return pl.BlockSpec((TILE, D_MODEL), lambda i: (jnp.maximum(i - 1, 0), 0))


def _slab_plan():
    interior, shared = [], []
    for d in range(N_DEV):
        lo, hi = -(-SLAB_BOUND[d] // LANES), SLAB_BOUND[d + 1] // LANES
        interior.append((d, LANES * (lo - SLAB_BLK0[d]), LANES * lo, LANES * (hi - lo)))
        if d + 1 < N_DEV and SLAB_BOUND[d + 1] % LANES:
            shared.append((hi, d, hi - SLAB_BLK0[d]))
    return interior, shared


N_BLOCKS = AL_COLS // LANES
HALF_BLOCKS = 5
HALF_W = LANES * HALF_BLOCKS


def _half_blocks():
    interior, _ = _slab_plan()
    first = [dst // LANES + j for _, src, dst, width in interior for j in range(width // LANES) if src // LANES + j < HALF_BLOCKS]
    return first, [b for b in range(N_BLOCKS) if b not in first]


def _w_scratch(n_blocks=N_BLOCKS):
    return [pltpu.VMEM((D_MODEL, LANES * n_blocks), BF16), pltpu.VMEM((D_MODEL, LANES), BF16),
            pltpu.VMEM((2 * (N_DEV - 1), D_MODEL, LANES), BF16), pltpu.SemaphoreType.DMA((4 * N_DEV,))]


W_SCRATCH = _w_scratch


def _slab_cols(halves, d, lo, n):
    out = []
    for half, start, end in zip(halves, (0, HALF_W), (HALF_W, SLAB_W)):
        a, b = max(lo, start), min(lo + n, end)
        if a < b:
            out.append((half.at[d, :, pl.ds(a - start, b - a)], a - lo, b - a))
    return out


def _load_weight(halves, wg_hbm, w_vm, wg_vm, edge_vm, sem, blocks=None):
    blocks = list(range(N_BLOCKS) if blocks is None else blocks)
    place = {b: i for i, b in enumerate(blocks)}
    interior, shared = _slab_plan()
    copies = [] if wg_hbm is None else [(wg_hbm, wg_vm)]
    for d, src, dst, width in interior:
        b0 = dst // LANES
        runs = []
        for b in range(b0, b0 + width // LANES):
            if b in place and runs and b == sum(runs[-1]):
                runs[-1][1] += 1
            elif b in place:
                runs.append([b, 1])
        for b, n in runs:
            for piece, off, w in _slab_cols(halves, d, src + LANES * (b - b0), LANES * n):
                copies.append((piece, w_vm.at[:, pl.ds(LANES * place[b] + off, w)]))
    edges = []
    for blk, d, j in shared:
        if blk in place:
            ((low, _, _),), ((high, _, _),) = _slab_cols(halves, d, LANES * j, LANES), _slab_cols(halves, d + 1, 0, LANES)
            copies += [(low, edge_vm.at[2 * len(edges)]), (high, edge_vm.at[2 * len(edges) + 1])]
            edges.append(blk)
    copies = [pltpu.make_async_copy(a, b, sem.at[i]) for i, (a, b) in enumerate(copies)]
    for cp in copies:
        cp.start()
    for cp in copies:
        cp.wait()
    for n, blk in enumerate(edges):
        w_vm[:, LANES * place[blk]:LANES * (place[blk] + 1)] = edge_vm[2 * n] + edge_vm[2 * n + 1]


def _proj_specs(names, n_units, where):
    specs = []
    for name in names:
        s = SEG_NAMES.index(name)
        nblk = SEG_W[s] // n_units // LANES
        base = SEG_OFF[s] // LANES
        assert base % nblk == 0
        specs.append(pl.BlockSpec((nblk, TILE, LANES), lambda *g, base=base, nblk=nblk: (base // nblk + where(*g)[0], where(*g)[1], 0)))
    return specs


def _cols(ref, unit=0, n_units=1):
    n = ref.shape[0] // n_units
    return ref[unit * n] if n == 1 else jnp.concatenate([ref[unit * n + j] for j in range(n)], axis=1)


def _prenorm(head_ref, x_ref, g_ref):
    x = _tile_rows(head_ref, x_ref)
    r = lax.rsqrt(_row_mean(x * x) + EPS)
    return (x * r * g_ref[...]).astype(BF16).astype(F32)


def _project(u, w_vm, n, store):
    cuts = [8 * i for i in range(max(n // 8, 1))] + [n]
    for lo, hi in zip(cuts[:-1], cuts[1:]):
        res = _mm(u, w_vm[:, LANES * lo:LANES * hi]).astype(BF16)
        for j in range(lo, hi):
            store(j, res[:, LANES * (j - lo):LANES * (j - lo + 1)])


def _inproj_first(head, x, g_norm, first_halves, second_half):
    t_rows = x.shape[0] + TILE
    nt = t_rows // TILE
    first, _ = _half_blocks()
    n = len(first)

    def body(head_ref, x_ref, g_ref, first_hbm, second_hbm, proj_ref, gathered, w_vm, wg_vm, edge_vm, sem, *sems):
        gather = _RelayGather([second_hbm], [gathered], sems)
        for step, stages in ((0, (0,)), (2 * nt // 5, (1, 2)), (4 * nt // 5, (3,)), (nt - 1, (4,))):
            @pl.when(pl.program_id(0) == step)
            def _():
                for s in stages:
                    gather.stage(s)

        @pl.when(pl.program_id(0) == 0)
        def _():
            _load_weight((first_hbm,), None, w_vm, wg_vm, edge_vm, sem, first)

        def store(i, block):
            proj_ref[i] = block
        _project(_prenorm(head_ref, x_ref, g_ref).astype(BF16), w_vm, n, store)

    return _call(
        body, "inproj_fwd_first", grid=(nt,),
        out_shape=[jax.ShapeDtypeStruct((n, t_rows, LANES), BF16), jax.ShapeDtypeStruct((N_DEV, *second_half.shape), BF16)],
        in_specs=[_head_spec(), _x_spec(), pl.BlockSpec((1, D_MODEL), lambda i: (0, 0)), ANY, ANY],
        out_specs=[pl.BlockSpec((n, TILE, LANES), lambda i: (0, i, 0)), ANY],
        scratch_shapes=_w_scratch(n) + _gather_sems(1), compiler_params=_params(("arbitrary",)),
    )(head, x, g_norm, first_halves, second_half)


def _inproj_tiles(head, x, g_norm, halves, w_glr, proj_first):
    t_rows = x.shape[0] + TILE
    nt = t_rows // TILE
    first, rest = _half_blocks()

    def body(head_ref, x_ref, g_ref, first_hbm, second_hbm, wg_hbm, pf_ref, ut_ref, proj_ref, glr_ref, w_vm, wg_vm, edge_vm, sem):
        @pl.when(pl.program_id(0) == 0)
        def _():
            _load_weight((first_hbm, second_hbm), wg_hbm, w_vm, wg_vm, edge_vm, sem, rest)

        u32 = _prenorm(head_ref, x_ref, g_ref)
        u = u32.astype(BF16)
        ut_ref[...] = u32.T.astype(BF16)

        def store(i, block):
            proj_ref[rest[i]] = block
        _project(u, w_vm, len(rest), store)
        for i, b in enumerate(first):
            proj_ref[b] = pf_ref[i]
        glr_ref[...] = _mm(u, wg_vm[...])

    return _call(
        body, "inproj_fwd_tiles", grid=(nt,),
        out_shape=[jax.ShapeDtypeStruct((nt, D_MODEL, TILE), BF16), jax.ShapeDtypeStruct((N_BLOCKS, t_rows, LANES), BF16),
                   jax.ShapeDtypeStruct((t_rows, LANES), F32)],
        in_specs=[_head_spec(), _x_spec(), pl.BlockSpec((1, D_MODEL), lambda i: (0, 0)), ANY, ANY, ANY,
                  pl.BlockSpec((len(first), TILE, LANES), lambda i: (0, i, 0))],
        out_specs=[pl.BlockSpec((None, D_MODEL, TILE), lambda i: (i, 0, 0)), pl.BlockSpec((N_BLOCKS, TILE, LANES), lambda i: (0, i, 0)),
                   pl.BlockSpec((TILE, LANES), lambda i: (i, 0))],
        scratch_shapes=_w_scratch(len(rest)), compiler_params=_params(("arbitrary",)),
    )(head, x, g_norm, *halves, w_glr, proj_first)


def _ret_decay(lgh):
    i = lax.broadcasted_iota(jnp.int32, (TILE, TILE), 0)
    j = lax.broadcasted_iota(jnp.int32, (TILE, TILE), 1)
    rel = (i - j).astype(F32)
    return jnp.where(rel >= 0, jnp.exp(jnp.maximum(rel, 0.0) * lgh), 0.0)


def _ret_vectors(lgh):
    idx = lax.broadcasted_iota(jnp.int32, (TILE, 1), 0).astype(F32)
    xi = jnp.exp((idx + 1.0) * lgh)
    zeta = jnp.exp((TILE - 1.0 - idx) * lgh)
    gc = jnp.exp(jnp.full((1, 1), float(TILE), F32) * lgh)
    return xi, zeta, gc


def _rope_tables(nt):
    half = RET_QK // 2
    inv = ROPE_BASE ** (-jnp.arange(half, dtype=F32) / half)
    base = (jnp.arange(nt, dtype=F32) * TILE - float(PAD_ROWS))[:, None, None] * inv[None, None, :]
    off = jnp.arange(TILE, dtype=F32)[:, None] * inv[None, :]
    return jnp.cos(base), jnp.sin(base), jnp.cos(off), jnp.sin(off)


def _rope_specs(tile_of):
    return [pl.BlockSpec((None, 1, RET_QK // 2), lambda i: (tile_of(i), 0, 0))] * 2 + [pl.BlockSpec((TILE, RET_QK // 2), lambda i: (0, 0))] * 2


def _rope_angles(cb_ref, sb_ref, co_ref, so_ref):
    cb, sb, co, so = cb_ref[...], sb_ref[...], co_ref[...], so_ref[...]
    return cb * co - sb * so, sb * co + cb * so


def _ret_fwd(proj, rope, gain, lg, row_shards):
    t_rows = proj.shape[1]
    nt = t_rows // TILE
    ns = len(row_shards)

    def body(lg_ref, q_ref, k_ref, v_ref, g_ref, cb_ref, sb_ref, co_ref, so_ref, gain_ref, *rest):
        shard_refs, (oraw_ref, oret_ref, st_ref), gathered = rest[:ns], rest[ns:ns + 3], rest[ns + 3:2 * ns + 3]
        s_acc, dm = rest[2 * ns + 3:2 * ns + 5]
        gather = _Exchange(shard_refs, gathered, rest[2 * ns + 5:], among_chips=False)
        t = pl.program_id(0)

        @pl.when(t == 0)
        def _():
            gather.start()
            s_acc[...] = jnp.zeros_like(s_acc)
            for h in range(RET_HEADS):
                dm[h] = _ret_decay(lg_ref[h])

        @pl.when(t == nt - 1)
        def _():
            gather.finish()

        cos_t, sin_t = _rope_angles(cb_ref, sb_ref, co_ref, so_ref)
        for h in range(RET_HEADS):
            lgh = lg_ref[h]
            q = _rope(_cols(q_ref, h, RET_HEADS).astype(F32), cos_t, sin_t)
            k = _rope(_cols(k_ref, h, RET_HEADS).astype(F32), cos_t, sin_t) * (RET_QK ** -0.5)
            xi, zeta, gc = _ret_vectors(lgh)
            v = _cols(v_ref, h, RET_HEADS)
            s_in = s_acc[h]
            p = (_mm_nt(q.astype(BF16), k.astype(BF16)) * dm[h]).astype(BF16)
            o = _mm(p, v) + _mm((q * xi).astype(BF16), s_in.astype(BF16))
            st_ref[h] = s_in.astype(BF16)
            s_acc[h] = s_in * gc + _mm_tn((k * zeta).astype(BF16), v)
            cols = slice(h * RET_V, (h + 1) * RET_V)
            oraw_ref[:, cols] = o
            oc = o - _row_mean(o)
            n = oc * lax.rsqrt(_row_mean(oc * oc) + EPS) * gain_ref[:, cols]
            g = _cols(g_ref, h, RET_HEADS).astype(F32)
            oret_ref[:, cols] = (n * g * _sigmoid(g)).astype(BF16)

    row = lambda w: pl.BlockSpec((TILE, w), lambda t: (t, 0))
    outs = _call(
        body, "ret_fwd", grid=(nt,),
        out_shape=[jax.ShapeDtypeStruct((t_rows, RET_W), F32), jax.ShapeDtypeStruct((t_rows, RET_W), BF16),
                   jax.ShapeDtypeStruct((RET_HEADS, nt, RET_QK, RET_V), BF16)]
                  + [jax.ShapeDtypeStruct((N_DEV, *a.shape), a.dtype) for a in row_shards],
        in_specs=[pl.BlockSpec(memory_space=pltpu.SMEM)] + _proj_specs(("rq", "rk", "rv", "rg"), 1, lambda t: (0, t)) + _rope_specs(lambda t: t) + [
                  pl.BlockSpec((1, RET_W), lambda t: (0, 0))] + [ANY] * ns,
        out_specs=[row(RET_W), row(RET_W), pl.BlockSpec((RET_HEADS, None, RET_QK, RET_V), lambda t: (0, t, 0, 0))] + [ANY] * ns,
        scratch_shapes=[pltpu.VMEM((RET_HEADS, RET_QK, RET_V), F32), pltpu.VMEM((RET_HEADS, TILE, TILE), F32)] + _exchange_sems(ns, N_DEV),
        compiler_params=_params(("arbitrary",)),
    )(lg, proj, proj, proj, proj, *rope, gain, *row_shards)
    return outs[0], outs[1], outs[2], outs[3:]


def _ret_bwd(proj, rope, gain, lg, o_raw, do_ret, states):
    t_rows = proj.shape[1]
    nt = t_rows // TILE

    def body(lg_ref, q_ref, k_ref, v_ref, g_ref, cb_ref, sb_ref, co_ref, so_ref, gain_ref, oraw_ref, do_ref, st_ref,
             dq_ref, dk_ref, dv_ref, dg_ref, dgain_ref, e_acc, dm):
        @pl.when(pl.program_id(0) == 0)
        def _():
            e_acc[...] = jnp.zeros_like(e_acc)
            for h in range(RET_HEADS):
                dm[h] = _ret_decay(lg_ref[h])
            dgain_ref[...] = jnp.zeros_like(dgain_ref)

        cos_t, sin_t = _rope_angles(cb_ref, sb_ref, co_ref, so_ref)
        for h in range(RET_HEADS):
            lgh = lg_ref[h]
            cols = slice(h * RET_V, (h + 1) * RET_V)
            qcols = slice(h * RET_QK, (h + 1) * RET_QK)
            q = _rope(_cols(q_ref, h, RET_HEADS).astype(F32), cos_t, sin_t)
            k = _rope(_cols(k_ref, h, RET_HEADS).astype(F32), cos_t, sin_t) * (RET_QK ** -0.5)
            xi, zeta, gc = _ret_vectors(lgh)
            v = _cols(v_ref, h, RET_HEADS)
            g = _cols(g_ref, h, RET_HEADS).astype(F32)
            o = oraw_ref[:, cols]
            do = do_ref[:, cols].astype(F32)
            oc = o - _row_mean(o)
            rstd = lax.rsqrt(_row_mean(oc * oc) + EPS)
            xh = oc * rstd
            gain_t = gain_ref[:, cols]
            sg = _sigmoid(g)
            dn = do * (g * sg)
            dg_ref[:, cols] = (do * (xh * gain_t) * (sg * (1.0 + g * (1.0 - sg)))).astype(BF16)
            dgain_ref[:, cols] += _col_sum(dn * xh)
            dxh = dn * gain_t
            dob = (rstd * (dxh - _row_mean(dxh) - xh * _row_mean(dxh * xh))).astype(BF16)
            dmat = dm[h]
            qb, kb = q.astype(BF16), k.astype(BF16)
            p = (_mm_nt(qb, kb) * dmat).astype(BF16)
            dp = (_mm_nt(dob, v) * dmat).astype(BF16)
            s_in = st_ref[h]
            e_in = e_acc[h]
            e_b = e_in.astype(BF16)
            dq = _mm(dp, kb) + _mm_nt(dob, s_in) * xi
            dk = _mm_tn(dp, qb) + _mm_nt(v, e_b) * zeta
            dv_ref[:, cols] = (_mm_tn(p, dob) + _mm((k * zeta).astype(BF16), e_b)).astype(BF16)
            e_acc[h] = e_in * gc + _mm_tn((q * xi).astype(BF16), dob)
            dq_ref[:, qcols] = _rope_bwd(dq, cos_t, sin_t).astype(BF16)
            dk_ref[:, qcols] = (_rope_bwd(dk, cos_t, sin_t) * (RET_QK ** -0.5)).astype(BF16)

    row = lambda w: pl.BlockSpec((TILE, w), lambda j: (nt - 1 - j, 0))
    vec = pl.BlockSpec((1, RET_W), lambda j: (0, 0))
    return _call(
        body, "ret_bwd", grid=(nt,),
        out_shape=[jax.ShapeDtypeStruct((t_rows, RET_HEADS * RET_QK), BF16), jax.ShapeDtypeStruct((t_rows, RET_HEADS * RET_QK), BF16),
                   jax.ShapeDtypeStruct((t_rows, RET_W), BF16), jax.ShapeDtypeStruct((t_rows, RET_W), BF16),
                   jax.ShapeDtypeStruct((1, RET_W), F32)],
        in_specs=[pl.BlockSpec(memory_space=pltpu.SMEM)] + _proj_specs(("rq", "rk", "rv", "rg"), 1, lambda j: (0, nt - 1 - j)) + _rope_specs(lambda j: nt - 1 - j) + [vec,
                  row(RET_W), row(RET_W), pl.BlockSpec((RET_HEADS, None, RET_QK, RET_V), lambda j: (0, nt - 1 - j, 0, 0))],
        out_specs=[row(RET_HEADS * RET_QK), row(RET_HEADS * RET_QK), row(RET_W), row(RET_W), vec],
        scratch_shapes=[pltpu.VMEM((RET_HEADS, RET_QK, RET_V), F32), pltpu.VMEM((RET_HEADS, TILE, TILE), F32)],
        compiler_params=_params(("arbitrary",)),
    )(lg, proj, proj, proj, proj, *rope, gain, o_raw, do_ret, states)


GLA_LEVELS = (32, 64, 128, 256)
N_TERMS = 1 + len(GLA_LEVELS)


def _gla_tables():
    p = jnp.arange(TILE)[:, None]
    r = jnp.arange(TILE)[None, :]
    masks = [(p // GLA_CHUNK == r // GLA_CHUNK) & (r <= p)]
    for blk in GLA_LEVELS:
        masks.append((p // blk == r // blk) & (p % blk >= blk // 2) & (r % blk < blk // 2))
    masks = jnp.stack(masks + [m.T for m in masks]).astype(F32)
    cum_fwd = jnp.concatenate([r <= p, masks[0] > 0], axis=0).astype(BF16)
    cum_bwd = jnp.concatenate([r >= p, masks[N_TERMS] > 0], axis=1).astype(BF16)
    return masks, cum_fwd, cum_bwd


def _split3(x):
    hi = x.astype(BF16)
    rest = x - hi.astype(F32)
    mid = rest.astype(BF16)
    lo = (rest - mid.astype(F32)).astype(BF16)
    return jnp.concatenate([hi, mid, lo], axis=1)


def _join3(y):
    w = y.shape[1] // 3
    return (y[:, 2 * w:] + y[:, w:2 * w]) + y[:, :w]


def _gla_decays(glr_ref, wgu_ref, b_ref, cum_ref):
    z = _mm(glr_ref[...].astype(BF16), wgu_ref[...].astype(BF16)) + b_ref[...]
    la = (jnp.minimum(z, 0.0) - jnp.log(1.0 + jnp.exp(-jnp.abs(z)))) / GLA_TAU
    width = la.shape[1]
    hi = la.astype(BF16)
    rest = la - hi.astype(F32)
    mid = rest.astype(BF16)
    lo = (rest - mid.astype(F32)).astype(BF16)
    y = _mm(cum_ref[...], jnp.concatenate([hi, mid, lo], axis=1))
    gb = (y[:, 2 * width:] + y[:, width:2 * width]) + y[:, :width]
    return z, gb[:TILE], gb[TILE:]


def _gla_prep(h, q_ref, k_ref, g_all, b_all, g_scr, ref_scr):
    cols = slice(h * GLA_K, (h + 1) * GLA_K)
    g, b = g_all[:, cols], b_all[:, cols]
    g_scr[h] = g
    factors = [(jnp.exp(b), jnp.exp(-b))]
    for lvl, blk in enumerate(GLA_LEVELS):
        for n in range(TILE // blk):
            ref_scr[h, lvl, n * blk:(n + 1) * blk, :] = jnp.broadcast_to(g_scr[h, pl.ds(n * blk + blk // 2 - 1, 1), :], (blk, GLA_K))
        x = g - ref_scr[h, lvl]
        factors.append((jnp.exp(jnp.minimum(x, 0.0)), jnp.exp(jnp.minimum(-x, 0.0))))
    g_last = g_scr[h, pl.ds(TILE - 1, 1), :]
    q = _cols(q_ref, h, GLA_HEADS).astype(F32) * (GLA_K ** -0.5)
    k = _cols(k_ref, h, GLA_HEADS).astype(F32)
    return q, k, factors, jnp.exp(g), jnp.exp(g_last), jnp.exp(g_last - g)


def _gla_scores(q, k, factors, m_ref):
    a = jnp.zeros((TILE, TILE), F32)
    for l, (fq, fk) in enumerate(factors):
        s = _mm_nt((q * fq).astype(BF16), (k * fk).astype(BF16))
        a = jnp.where(m_ref[l] > 0.0, s, a)
    return a


def _gla_fwd(proj, glr, wgu_pad, b_gate, gain, masks, cum_fwd):
    t_rows = glr.shape[0]
    nt = t_rows // TILE

    def body(q_ref, k_ref, v_ref, g_ref, glr_ref, wgu_ref, b_ref, gain_ref, m_ref, cum_ref, oraw_ref, ogla_ref, st_ref, at_ref,
             s_acc, g_scr, ref_scr):
        @pl.when(pl.program_id(0) == 0)
        def _():
            s_acc[...] = jnp.zeros_like(s_acc)

        _, g_all, b_all = _gla_decays(glr_ref, wgu_ref, b_ref, cum_ref)
        for h in range(GLA_HEADS):
            q, k, factors, e_g, e_last, e_end = _gla_prep(h, q_ref, k_ref, g_all, b_all, g_scr, ref_scr)
            v = _cols(v_ref, h, GLA_HEADS)
            st = s_acc[h]
            st_ref[h] = st
            a = _gla_scores(q, k, factors, m_ref)
            at_ref[h] = a.T.astype(BF16)
            o = _mm(a.astype(BF16), v) + _mm_nt((q * e_g).astype(BF16), st.astype(BF16))
            s_acc[h] = st * e_last + _mm(v.astype(F32).T.astype(BF16), (k * e_end).astype(BF16))
            cols = slice(h * GLA_V, (h + 1) * GLA_V)
            oraw_ref[:, cols] = o
            n = o * lax.rsqrt(_row_mean(o * o) + EPS) * gain_ref[:, cols]
            g = _cols(g_ref, h, GLA_HEADS).astype(F32)
            ogla_ref[:, cols] = (n * g * _sigmoid(g)).astype(BF16)

    row = lambda w: pl.BlockSpec((TILE, w), lambda t: (t, 0))
    whole = lambda *shape: pl.BlockSpec(shape, lambda t: (0,) * len(shape))
    return _call(
        body, "gla_fwd", grid=(nt,),
        out_shape=[jax.ShapeDtypeStruct((t_rows, GLA_W), F32), jax.ShapeDtypeStruct((t_rows, GLA_W), BF16),
                   jax.ShapeDtypeStruct((GLA_HEADS, nt, GLA_V, GLA_K), F32), jax.ShapeDtypeStruct((GLA_HEADS, t_rows, TILE), BF16)],
        in_specs=_proj_specs(("gq", "gk", "gv", "gg"), 1, lambda t: (0, t)) + [row(LANES), whole(LANES, GLA_HEADS * GLA_K),
                  whole(1, GLA_HEADS * GLA_K), whole(1, GLA_W), whole(N_TERMS, TILE, TILE), whole(2 * TILE, TILE)],
        out_specs=[row(GLA_W), row(GLA_W), pl.BlockSpec((GLA_HEADS, None, GLA_V, GLA_K), lambda t: (0, t, 0, 0)),
                   pl.BlockSpec((GLA_HEADS, TILE, TILE), lambda t: (0, t, 0))],
        scratch_shapes=[pltpu.VMEM((GLA_HEADS, GLA_V, GLA_K), F32), pltpu.VMEM((GLA_HEADS, TILE, GLA_K), F32),
                        pltpu.VMEM((GLA_HEADS, len(GLA_LEVELS), TILE, GLA_K), F32)],
        compiler_params=_params(("arbitrary",)),
    )(proj, proj, proj, proj, glr, wgu_pad, b_gate, gain, masks, cum_fwd)


def _gla_bwd(proj, glr, wgu_pad, b_gate, gain, o_raw, do_gla, states, a_t, masks, cum_fwd, cum_bwd):
    t_rows = glr.shape[0]
    nt = t_rows // TILE

    def body(q_ref, k_ref, v_ref, g_ref, glr_ref, wgu_ref, b_ref, gain_ref, m_ref, cum_ref, cumb_ref, oraw_ref, do_ref, st_ref, at_ref,
             dq_ref, dk_ref, dv_ref, dg_ref, dglr_ref, dwgu_ref, dbg_ref, dgain_ref, d_acc, g_scr, ref_scr, dref_scr):
        @pl.when(pl.program_id(0) == 0)
        def _():
            d_acc[...] = jnp.zeros_like(d_acc)
            dwgu_ref[...] = jnp.zeros_like(dwgu_ref)
            dbg_ref[...] = jnp.zeros_like(dbg_ref)
            dgain_ref[...] = jnp.zeros_like(dgain_ref)

        z_all, g_all, b_all = _gla_decays(glr_ref, wgu_ref, b_ref, cum_ref)
        dla_parts = []
        for h in range(GLA_HEADS):
            q, k, factors, e_g, e_last, e_end = _gla_prep(h, q_ref, k_ref, g_all, b_all, g_scr, ref_scr)
            v = _cols(v_ref, h, GLA_HEADS)
            cols = slice(h * GLA_V, (h + 1) * GLA_V)
            kcols = slice(h * GLA_K, (h + 1) * GLA_K)
            o = oraw_ref[:, cols]
            do = do_ref[:, cols].astype(F32)
            g = _cols(g_ref, h, GLA_HEADS).astype(F32)
            rinv = lax.rsqrt(_row_mean(o * o) + EPS)
            nh = o * rinv
            gain_t = gain_ref[:, cols]
            sg = _sigmoid(g)
            dn = do * (g * sg)
            dg_ref[:, cols] = (do * (nh * gain_t) * (sg * (1.0 + g * (1.0 - sg)))).astype(BF16)
            dgain_ref[:, cols] += _col_sum(dn * nh)
            dnh = dn * gain_t
            dor = rinv * (dnh - nh * _row_mean(dnh * nh))
            dob = dor.astype(BF16)
            a_t = at_ref[h]
            da = _mm_nt(dob, v).astype(BF16)
            da_t = _mm_nt(v, dob).astype(BF16)
            st_in = st_ref[h]
            d_out = d_acc[h]
            d_out_b = d_out.astype(BF16)
            qg, kg = q * e_g, k * e_end
            dqg = _mm(dob, st_in.astype(BF16))
            dkg = _mm(v, d_out_b)
            dv_ref[:, cols] = (_mm(a_t, dob) + _mm_nt(kg.astype(BF16), d_out_b)).astype(BF16)
            d_acc[h] = d_out * e_last + _mm(dor.T.astype(BF16), qg.astype(BF16))
            dq = dqg * e_g
            dk = dkg * e_end
            dkg_kg = dkg * kg
            dg_cum = dqg * qg - dkg_kg
            db = None
            for l, (fq, fk) in enumerate(factors):
                qt, kt = q * fq, k * fk
                dqt = _mm(da * m_ref[l], kt.astype(BF16))
                dkt = _mm(da_t * m_ref[N_TERMS + l], qt.astype(BF16))
                dq = dq + dqt * fq
                dk = dk + dkt * fk
                diff = dqt * qt - dkt * kt
                if l == 0:
                    db = diff
                else:
                    dg_cum = dg_cum + diff
                    dref_scr[h, l - 1] = diff
            dq_ref[:, kcols] = (dq * (GLA_K ** -0.5)).astype(BF16)
            dk_ref[:, kcols] = dk.astype(BF16)
            g_scr[h] = dg_cum
            g_scr[h, pl.ds(TILE - 1, 1), :] += e_last * _col_sum(d_out * st_in) + _col_sum(dkg_kg)
            for lvl, blk in enumerate(GLA_LEVELS):
                for n in range(TILE // blk):
                    g_scr[h, pl.ds(n * blk + blk // 2 - 1, 1), :] -= _col_sum(dref_scr[h, lvl, n * blk:(n + 1) * blk, :])
            dla_parts.append(_join3(_mm(cumb_ref[...], jnp.concatenate([_split3(g_scr[h]), _split3(db)], axis=0))))
        dz = jnp.concatenate(dla_parts, axis=1) * (1.0 / GLA_TAU) * _sigmoid(-z_all)
        dzb = dz.astype(BF16)
        wgu_b = wgu_ref[...].astype(BF16)
        for h in range(GLA_HEADS):
            kcols = slice(h * GLA_K, (h + 1) * GLA_K)
            dglr_ref[h] = _mm_nt(dzb[:, kcols], wgu_b[:, kcols]).astype(BF16)
        dwgu_ref[...] += _mm(glr_ref[...].T.astype(BF16), dzb)
        dbg_ref[...] += _col_sum(dz)

    row = lambda w: pl.BlockSpec((TILE, w), lambda j: (nt - 1 - j, 0))
    whole = lambda *shape: pl.BlockSpec(shape, lambda j: (0,) * len(shape))
    return _call(
        body, "gla_bwd", grid=(nt,),
        out_shape=[jax.ShapeDtypeStruct((t_rows, GLA_HEADS * GLA_K), BF16), jax.ShapeDtypeStruct((t_rows, GLA_HEADS * GLA_K), BF16),
                   jax.ShapeDtypeStruct((t_rows, GLA_W), BF16), jax.ShapeDtypeStruct((t_rows, GLA_W), BF16),
                   jax.ShapeDtypeStruct((GLA_HEADS, t_rows, LANES), BF16), jax.ShapeDtypeStruct((LANES, GLA_HEADS * GLA_K), F32),
                   jax.ShapeDtypeStruct((1, GLA_HEADS * GLA_K), F32), jax.ShapeDtypeStruct((1, GLA_W), F32)],
        in_specs=_proj_specs(("gq", "gk", "gv", "gg"), 1, lambda j: (0, nt - 1 - j)) + [row(LANES),
                  whole(LANES, GLA_HEADS * GLA_K), whole(1, GLA_HEADS * GLA_K), whole(1, GLA_W),
                  whole(2 * N_TERMS, TILE, TILE), whole(2 * TILE, TILE), whole(TILE, 2 * TILE), row(GLA_W), row(GLA_W),
                  pl.BlockSpec((GLA_HEADS, None, GLA_V, GLA_K), lambda j: (0, nt - 1 - j, 0, 0)),
                  pl.BlockSpec((GLA_HEADS, TILE, TILE), lambda j: (0, nt - 1 - j, 0))],
        out_specs=[row(GLA_HEADS * GLA_K), row(GLA_HEADS * GLA_K), row(GLA_W), row(GLA_W),
                   pl.BlockSpec((GLA_HEADS, TILE, LANES), lambda j: (0, nt - 1 - j, 0)), whole(LANES, GLA_HEADS * GLA_K),
                   whole(1, GLA_HEADS * GLA_K), whole(1, GLA_W)],
        scratch_shapes=[pltpu.VMEM((GLA_HEADS, GLA_V, GLA_K), F32), pltpu.VMEM((GLA_HEADS, TILE, GLA_K), F32),
                        pltpu.VMEM((GLA_HEADS, len(GLA_LEVELS), TILE, GLA_K), F32),
                        pltpu.VMEM((GLA_HEADS, len(GLA_LEVELS), TILE, GLA_K), F32)],
        compiler_params=_params(("arbitrary",)),
    )(proj, proj, proj, proj, glr, wgu_pad, b_gate, gain, masks.astype(BF16), cum_fwd, cum_bwd, o_raw, do_gla, states, a_t)


def _merge_fwd_bwd(o_ret, o_gla, proj, x, target, g_final, w_br, w_bg, w_out):
    t_rows = x.shape[0] + TILE
    nt = t_rows // TILE

    def body(oret_ref, ogla_ref, mr_ref, mg_ref, h0_ref, tgt_ref, gf_ref, wbr_hbm, wbg_hbm, wout_hbm,
             dh1_ref, dmr_ref, dmg_ref, doret_ref, dogla_ref, loss_ref, dgf_ref, dwbr_hbm, dwbg_hbm, dwout_hbm,
             wbr, wbg, wout, abr, abg, aout, sem):
        i = pl.program_id(0)

        @pl.when(i == 0)
        def _():
            cps = [pltpu.make_async_copy(s, d, sem.at[n]) for n, (s, d) in enumerate(((wbr_hbm, wbr), (wbg_hbm, wbg), (wout_hbm, wout)))]
            for cp in cps:
                cp.start()
            abr[...] = jnp.zeros_like(abr)
            abg[...] = jnp.zeros_like(abg)
            aout[...] = jnp.zeros_like(aout)
            loss_ref[...] = jnp.zeros_like(loss_ref)
            dgf_ref[...] = jnp.zeros_like(dgf_ref)
            for cp in cps:
                cp.wait()
            dh1_ref[...] = jnp.zeros_like(dh1_ref)
            dmr_ref[...] = jnp.zeros_like(dmr_ref)
            dmg_ref[...] = jnp.zeros_like(dmg_ref)
            doret_ref[...] = jnp.zeros_like(doret_ref)
            dogla_ref[...] = jnp.zeros_like(dogla_ref)

        @pl.when(i > 0)
        def _():
            oret, ogla = oret_ref[...], ogla_ref[...]
            br, bg = _mm(oret, wbr[...]), _mm(ogla, wbg[...])
            sr, sg = _sigmoid(_cols(mr_ref).astype(F32)), _sigmoid(_cols(mg_ref).astype(F32))
            mb = (sr * br + sg * bg).astype(BF16)
            h1 = h0_ref[...] + _mm(mb, wout[...])
            r2 = lax.rsqrt(_row_mean(h1 * h1) + EPS)
            hn = h1 * r2
            gf = gf_ref[...]
            diff = hn * gf - tgt_ref[...]
            loss_ref[...] += 0.5 * jnp.sum(_row_mean(diff * diff))
            dy = diff * (1.0 / D_MODEL)
            dgf_ref[...] += _col_sum(dy * hn)
            dyg = dy * gf
            dh1 = r2 * (dyg - hn * _row_mean(dyg * hn))
            dh1_ref[...] = dh1
            dh1b = dh1.astype(BF16)
            dm = _mm_nt(dh1b, wout[...])
            aout[...] += _mm_tn(mb, dh1b)
            dbr = (dm * sr).astype(BF16)
            dbg = (dm * sg).astype(BF16)
            dmr_ref[...] = (dm * br * sr * (1.0 - sr)).astype(BF16)
            dmg_ref[...] = (dm * bg * sg * (1.0 - sg)).astype(BF16)
            doret_ref[...] = _mm_nt(dbr, wbr[...]).astype(BF16)
            dogla_ref[...] = _mm_nt(dbg, wbg[...]).astype(BF16)
            abr[...] += _mm_tn(oret, dbr)
            abg[...] += _mm_tn(ogla, dbg)

        @pl.when(i == nt - 1)
        def _():
            wbr[...] = abr[...].astype(BF16)
            wbg[...] = abg[...].astype(BF16)
            wout[...] = aout[...].astype(BF16)
            pltpu.sync_copy(wbr, dwbr_hbm)
            pltpu.sync_copy(wbg, dwbg_hbm)
            pltpu.sync_copy(wout, dwout_hbm)

    row = lambda w: pl.BlockSpec((TILE, w), lambda i: (i, 0))
    one = lambda w: pl.BlockSpec((1, w), lambda i: (0, 0))
    return _call(
        body, "merge_fwd_bwd", grid=(nt,),
        out_shape=[jax.ShapeDtypeStruct((t_rows, D_MODEL), F32), jax.ShapeDtypeStruct((t_rows, D_MODEL), BF16),
                   jax.ShapeDtypeStruct((t_rows, D_MODEL), BF16), jax.ShapeDtypeStruct((t_rows, RET_W), BF16),
                   jax.ShapeDtypeStruct((t_rows, GLA_W), BF16), jax.ShapeDtypeStruct((1, LANES), F32),
                   jax.ShapeDtypeStruct((1, D_MODEL), F32), jax.ShapeDtypeStruct((RET_W, D_MODEL), BF16),
                   jax.ShapeDtypeStruct((GLA_W, D_MODEL), BF16), jax.ShapeDtypeStruct((D_MODEL, D_MODEL), BF16)],
        in_specs=[row(RET_W), row(GLA_W)] + _proj_specs(("mr", "mg"), 1, lambda i: (0, i)) + [_x_spec(), _x_spec(), one(D_MODEL), ANY, ANY, ANY],
        out_specs=[row(D_MODEL), row(D_MODEL), row(D_MODEL), row(RET_W), row(GLA_W), one(LANES), one(D_MODEL), ANY, ANY, ANY],
        scratch_shapes=[pltpu.VMEM((RET_W, D_MODEL), BF16), pltpu.VMEM((GLA_W, D_MODEL), BF16), pltpu.VMEM((D_MODEL, D_MODEL), BF16),
                        pltpu.VMEM((RET_W, D_MODEL), F32), pltpu.VMEM((GLA_W, D_MODEL), F32), pltpu.VMEM((D_MODEL, D_MODEL), F32),
                        pltpu.SemaphoreType.DMA((3,))],
        compiler_params=_params(("arbitrary",)),
    )(o_ret, o_gla, proj, proj, x, target, g_final, w_br, w_bg, w_out)


def _inproj_bwd_x(dseg, dglr, head, x, dh1, g_norm, slabs, w_glr, chip_partials):
    t_rows = x.shape[0] + TILE
    nt = t_rows // TILE
    ne = len(chip_partials)

    def body(*refs):
        d_refs = refs[:10]
        dglr_ref, head_ref, x_ref, dh1_ref, g_ref, slabs_a, slabs_b, wg_hbm = refs[10:18]
        part_refs = refs[18:18 + ne]
        dx_ref, dhead_ref, dgn_ref = refs[18 + ne:21 + ne]
        landed = refs[21 + ne:21 + 2 * ne]
        w_vm, wg_vm, edge_vm, sem = refs[21 + 2 * ne:25 + 2 * ne]
        exchange = _Exchange(part_refs, landed, refs[25 + 2 * ne:], among_chips=True)

        @pl.when(pl.program_id(0) == 0)
        def _():
            exchange.start()
            dgn_ref[...] = jnp.zeros_like(dgn_ref)
            _load_weight((slabs_a, slabs_b), wg_hbm, w_vm, wg_vm, edge_vm, sem)

        @pl.when(pl.program_id(0) == nt - 1)
        def _():
            exchange.finish()

        dglr = dglr_ref[0].astype(F32)
        for h in range(1, GLA_HEADS):
            dglr = dglr + dglr_ref[h].astype(F32)
        du = _mm_nt(dglr.astype(BF16), wg_vm[...])
        for s, d_ref in enumerate(d_refs):
            du = du + _mm_nt(d_ref[...], w_vm[:, SEG_OFF[s]:SEG_OFF[s] + SEG_W[s]])
        x = _tile_rows(head_ref, x_ref)
        r = lax.rsqrt(_row_mean(x * x) + EPS)
        hn = x * r
        dgn_ref[...] += _col_sum(du * hn)
        dug = du * g_ref[...]
        dh0 = dh1_ref[...] + r * (dug - hn * _row_mean(dug * hn))
        dx_ref[...] = dh0

        @pl.when(pl.program_id(0) == 0)
        def _():
            dhead_ref[...] = dh0

    row = lambda w: pl.BlockSpec((TILE, w), lambda i: (i, 0))
    one = pl.BlockSpec((1, D_MODEL), lambda i: (0, 0))
    return _call(
        body, "inproj_bwd_x", grid=(nt,),
        out_shape=[jax.ShapeDtypeStruct((t_rows - TILE, D_MODEL), F32), jax.ShapeDtypeStruct((TILE, D_MODEL), F32),
                   jax.ShapeDtypeStruct((1, D_MODEL), F32)] + [jax.ShapeDtypeStruct(a.shape, a.dtype) for a in chip_partials],
        in_specs=[row(w) for w in SEG_W] + [pl.BlockSpec((GLA_HEADS, TILE, LANES), lambda i: (0, i, 0)),
                                            _head_spec(), _x_spec(), row(D_MODEL), one, ANY, ANY, ANY] + [ANY] * ne,
        out_specs=[_x_spec(), _head_spec(), one] + [ANY] * ne,
        scratch_shapes=W_SCRATCH() + _exchange_sems(ne, N_CHIP),
        compiler_params=_params(("arbitrary",)),
    )(*[dseg[n] for n in SEG_NAMES], dglr, head, x, dh1, g_norm, *slabs, w_glr, *chip_partials)


W_TILE = 512


def _inproj_bwd_w(ut, dseg, dglr, row_sends):
    nt = ut.shape[0]
    t_rows = nt * TILE
    kc = 3 if nt % 3 == 0 else 1
    tiles = [(s, c) for s in range(len(SEG_W)) for c in range(0, SEG_W[s], W_TILE)]
    bpt = W_TILE // LANES
    nr = len(row_sends)
    n = 1 + nr
    last_tile = [(SLAB_BLK0[d] + SLAB_BLOCKS - 1) // bpt for d in range(N_DEV)]

    def body(ut_hbm, *refs):
        d_refs, dglr_hbm, row_refs = refs[:10], refs[10], refs[11:11 + nr]
        out_hbm, oglr_ref, sib = refs[11 + nr], refs[12 + nr], refs[13 + nr:13 + nr + n]
        ut_vm, dbuf, obuf, acc, gbuf, sem, send_sems, recv_sems = refs[13 + nr + n:]
        x, y, core = _position()

        def handover(d, k, landed=False):
            q = d // 2
            src = out_hbm.at[pl.ds(SLAB_BLK0[d], SLAB_BLOCKS)] if k == 0 else row_refs[k - 1].at[d]
            return pltpu.make_async_remote_copy(src_ref=sib[k].at[q] if landed else src, dst_ref=sib[k].at[q],
                                                send_sem=send_sems.at[n * q + k], recv_sem=recv_sems.at[n * q + k],
                                                device_id=(x, y, 1 - core), device_id_type=MESH)

        def for_sibling(d, ks, fn):
            @pl.when(d % 2 != core)
            def _():
                for k in ks:
                    fn(handover(d, k))

        for d in range(N_DEV):
            for_sibling(d, range(1, n), lambda cp: cp.start())

        def fetch(i):
            s, c = tiles[i]
            return pltpu.make_async_copy(d_refs[s].at[:, pl.ds(c, W_TILE)], dbuf.at[i % 2], sem.at[1 + i % 2])

        def contract(rhs_refs, width):
            acc[:, :width] = jnp.zeros((D_MODEL, width), F32)

            def step(k, carry):
                part = None
                for j in range(kc):
                    kk = k * kc + j
                    for rhs_ref in rhs_refs:
                        prod = _mm(ut_vm[kk], rhs_ref[pl.ds(pl.multiple_of(kk * TILE, TILE), TILE), :])
                        part = prod if part is None else part + prod
                acc[:, :width] += part
                return carry

            lax.fori_loop(0, nt // kc, step, 0)
            return acc[:, :width]

        load_ut = pltpu.make_async_copy(ut_hbm, ut_vm, sem.at[0])
        load_glr = pltpu.make_async_copy(dglr_hbm, gbuf, sem.at[5])
        load_ut.start()
        load_glr.start()
        fetch(0).start()
        load_ut.wait()
        stores = {}

        def stored(i):
            stores[i].wait()
            for d in range(N_DEV):
                if last_tile[d] == i:
                    for_sibling(d, [0], lambda cp: cp.start())

        for i, (s, c) in enumerate(tiles):
            if i + 1 < len(tiles):
                fetch(i + 1).start()
            fetch(i).wait()
            if i >= 2:
                stored(i - 2)
            total = contract([dbuf.at[i % 2]], W_TILE)
            for j in range(bpt):
                obuf[i % 2, j] = total[:, j * LANES:(j + 1) * LANES].astype(BF16)
            blk0 = (SEG_OFF[s] + c) // LANES
            stores[i] = pltpu.make_async_copy(obuf.at[i % 2], out_hbm.at[pl.ds(blk0, bpt)], sem.at[3 + i % 2])
            stores[i].start()
        for i in range(max(0, len(tiles) - 2), len(tiles)):
            stored(i)
        load_glr.wait()
        head_sum = gbuf[0].astype(F32)
        for h in range(1, GLA_HEADS):
            head_sum = head_sum + gbuf[h].astype(F32)
        gbuf[0] = head_sum.astype(BF16)
        oglr_ref[...] = contract([gbuf.at[0]], LANES)
        for q in range(N_CHIP):
            for k in range(n):
                handover(2 * q, k, landed=True).wait_recv()
        for d in range(N_DEV):
            for_sibling(d, range(n), lambda cp: cp.wait_send())

    outs = _call(
        body, "inproj_bwd_w",
        out_shape=[jax.ShapeDtypeStruct((AL_COLS // LANES, D_MODEL, LANES), BF16), jax.ShapeDtypeStruct((D_MODEL, LANES), F32),
                   jax.ShapeDtypeStruct((N_CHIP, SLAB_BLOCKS, D_MODEL, LANES), BF16)]
                  + [jax.ShapeDtypeStruct((N_CHIP, *r.shape[1:]), BF16) for r in row_sends],
        in_specs=[ANY] * (12 + nr), out_specs=[ANY, pl.BlockSpec(memory_space=pltpu.VMEM)] + [ANY] * n,
        scratch_shapes=[pltpu.VMEM((nt, D_MODEL, TILE), BF16), pltpu.VMEM((2, t_rows, W_TILE), BF16),
                        pltpu.VMEM((2, bpt, D_MODEL, LANES), BF16), pltpu.VMEM((D_MODEL, W_TILE), F32),
                        pltpu.VMEM((GLA_HEADS, t_rows, LANES), BF16), pltpu.SemaphoreType.DMA((6,)),
                        pltpu.SemaphoreType.DMA((n * N_CHIP,)), pltpu.SemaphoreType.DMA((n * N_CHIP,))],
        compiler_params=_params(),
    )(ut, *[dseg[n_] for n_ in SEG_NAMES], dglr, *row_sends)
    return outs[0], outs[1], outs[2], outs[3:]


def _position():
    x, y, c = lax.axis_index("x"), lax.axis_index("y"), lax.axis_index("c")
    return x, y, c


def _index(px, py, pc):
    return 4 * px + 2 * py + pc


def _gather_sems(n):
    return [pltpu.SemaphoreType.DMA((7 * n,)), pltpu.SemaphoreType.DMA((7 * n,)), pltpu.SemaphoreType.DMA((n,))]


class _RelayGather:
    STAGES = 5

    def __init__(self, ins, outs, sems):
        self.ins, self.outs, self.n = ins, outs, len(ins)
        self.send_sems, self.recv_sems, self.local_sems = sems
        x, y, c = _position()
        self.c, self.me, self.sibling = c, (x, y, c), (x, y, 1 - c)
        self.chips = [(1 - x, y), (x, 1 - y), (1 - x, 1 - y)]

    def _copy(self, a, k, block, to, src=None):
        dst = self.outs[a].at[_index(*block)]
        return pltpu.make_async_remote_copy(src_ref=dst if src is None else src, dst_ref=dst,
                                            send_sem=self.send_sems.at[7 * a + k], recv_sem=self.recv_sems.at[7 * a + k],
                                            device_id=to, device_id_type=MESH)

    def _relay(self, a, j):
        return self._copy(a, 3, (*self.chips[j], self.c), (*self.chips[1 - j], self.c))

    def _mine(self):
        return [pltpu.make_async_copy(self.ins[a], self.outs[a].at[_index(*self.me)], self.local_sems.at[a]) for a in range(self.n)]

    def _first(self):
        first = []
        for a in range(self.n):
            first.append(self._copy(a, 0, self.me, self.sibling, src=self.ins[a]))
            first += [self._copy(a, 1 + j, self.me, (*self.chips[j], self.c), src=self.ins[a]) for j in range(2)]
        return first

    def _passed(self, j):
        return [self._copy(a, 4 + j, (*self.chips[j], self.c), self.sibling) for a in range(self.n)]

    def stage(self, s):
        n, c = self.n, self.c
        if s == 0:
            for cp in self._mine() + self._first():
                cp.start()
        elif s < 4:
            j = s - 1
            for a in range(n):
                self._copy(a, 1 + j, (*self.chips[j], c), self.me).wait_recv()
            for cp in self._passed(j):
                cp.start()
            if j < 2:
                @pl.when(c == j)
                def _():
                    for a in range(n):
                        self._relay(a, j).start()
        else:
            for a in range(n):
                self._copy(a, 0, self.sibling, self.me).wait_recv()
                for j in range(3):
                    self._copy(a, 4 + j, (*self.chips[j], 1 - c), self.me).wait_recv()
            for cp in self._first() + self._passed(0) + self._passed(1) + self._passed(2):
                cp.wait_send()
            for j in range(2):
                @pl.when(c == j)
                def _():
                    for a in range(n):
                        self._relay(a, j).wait_send()
            for cp in self._mine():
                cp.wait()


def _all_gather(arrs, name):
    n = len(arrs)

    def body(*refs):
        gather = _RelayGather(refs[:n], refs[n:2 * n], refs[2 * n:])
        for s in range(gather.STAGES):
            gather.stage(s)

    return _call(
        body, name,
        out_shape=[jax.ShapeDtypeStruct((N_DEV, *a.shape), a.dtype) for a in arrs],
        in_specs=[ANY] * n, out_specs=[ANY] * n, scratch_shapes=_gather_sems(n),
    )(*arrs)


N_CHIP = N_DEV // 2


def _slab_block0(owner):
    step = SLAB_BLK0[1]
    assert all(SLAB_BLK0[d] == step * d - (d == N_DEV - 1) for d in range(N_DEV))
    return step * owner - jnp.where(owner == N_DEV - 1, 1, 0)


def _add_bf16(c_ref, a_ref, b_ref, o_ref):
    o_ref[...] = (a_ref[...].astype(F32) + b_ref[...].astype(F32)).astype(BF16)


def _chip_partial_slab(dw_blocks, sib, core):
    blk = pl.BlockSpec((None, SLAB_BLOCKS, D_MODEL, LANES), lambda q, c_ref: (q, 0, 0, 0))
    return _call(
        functools.partial(_add_bf16), "chip_partial_w_in", out_shape=jax.ShapeDtypeStruct(sib.shape, BF16),
        grid_spec=pltpu.PrefetchScalarGridSpec(
            num_scalar_prefetch=1, grid=(N_CHIP,),
            in_specs=[pl.BlockSpec((pl.Element(SLAB_BLOCKS), pl.Element(D_MODEL), pl.Element(LANES)),
                                   lambda q, c_ref: (_slab_block0(2 * q + c_ref[0]), 0, 0)), blk],
            out_specs=blk),
        compiler_params=_params(("arbitrary",)),
    )(core, dw_blocks, sib)


def _chip_partial_rows(sends, sibs, core):
    n = len(sends)

    def body(c_ref, *refs):
        for k in range(n):
            _add_bf16(c_ref, refs[k], refs[n + k], refs[2 * n + k])

    own = [pl.BlockSpec((None, *a.shape[1:]), lambda q, c_ref: (2 * q + c_ref[0], 0, 0)) for a in sends]
    blk = [pl.BlockSpec((None, *a.shape[1:]), lambda q, c_ref: (q, 0, 0)) for a in sibs]
    return _call(
        body, "chip_partial_rows", out_shape=[jax.ShapeDtypeStruct(a.shape, BF16) for a in sibs],
        grid_spec=pltpu.PrefetchScalarGridSpec(num_scalar_prefetch=1, grid=(N_CHIP,), in_specs=own + blk, out_specs=blk),
        compiler_params=_params(("arbitrary",)),
    )(core, *sends, *sibs)


def _exchange_sems(n_arrays, n_peers):
    return [pltpu.SemaphoreType.DMA((n_arrays * n_peers,)), pltpu.SemaphoreType.DMA((n_arrays * n_peers,)),
            pltpu.SemaphoreType.DMA((n_arrays,))]


class _Exchange:
    def __init__(self, srcs, dsts, sems, among_chips):
        self.arrs = list(zip(srcs, dsts))
        self.n = len(self.arrs)
        self.send_sems, self.recv_sems, self.local_sems = sems
        self.among_chips = among_chips
        x, y, c = _position()
        self.c = c
        self.me = 2 * x + y if among_chips else _index(x, y, c)
        self.n_peers = N_CHIP if among_chips else N_DEV

    def _device(self, p):
        return (p // 2, p % 2, self.c) if self.among_chips else (p // 4, (p // 2) % 2, p % 2)

    def _src(self, k, p):
        src = self.arrs[k][0]
        return src.at[p] if self.among_chips else src

    def _mine(self):
        return [pltpu.make_async_copy(self._src(k, self.me), self.arrs[k][1].at[self.me], self.local_sems.at[k]) for k in range(self.n)]

    def _copy(self, p, k, landing):
        return pltpu.make_async_remote_copy(
            src_ref=self._src(k, p), dst_ref=self.arrs[k][1].at[landing], send_sem=self.send_sems.at[self.n * p + k],
            recv_sem=self.recv_sems.at[self.n * landing + k], device_id=self._device(p), device_id_type=MESH)

    def _others(self, fn):
        for p in range(self.n_peers):
            @pl.when(p != self.me)
            def _():
                for k in range(self.n):
                    fn(p, k)

    def start(self):
        for cp in self._mine():
            cp.start()
        self._others(lambda p, k: self._copy(p, k, self.me).start())

    def finish(self):
        self._others(lambda p, k: self._copy(p, k, p).wait_recv())
        self._others(lambda p, k: self._copy(p, k, self.me).wait_send())
        for cp in self._mine():
            cp.wait()


def _adamw(g, w, m, v):
    m_new = ADAM_B1 * m + (1.0 - ADAM_B1) * g
    v_new = ADAM_B2 * v + (1.0 - ADAM_B2) * (g * g)
    m_hat = m_new / (1.0 - ADAM_B1 ** ADAM_STEP)
    v_hat = v_new / (1.0 - ADAM_B2 ** ADAM_STEP)
    delta = -ADAM_LR * (m_hat / (jnp.sqrt(v_hat) + ADAM_EPS) + ADAM_WD * w)
    return delta, m_new, v_new


def _sum_partials(p_ref):
    g = p_ref[0].astype(F32)
    for d in range(1, p_ref.shape[0]):
        g = g + p_ref[d].astype(F32)
    return g


def _reduce_adam_rows(parts, ws, ms, vs):
    n = len(ws)

    def body(*refs):
        ins, outs = refs[:4 * n], refs[4 * n:]
        for k in range(n):
            p_ref, w_ref, m_ref, v_ref = ins[k], ins[n + k], ins[2 * n + k], ins[3 * n + k]
            g = _sum_partials(p_ref)
            outs[4 * k][...] = g
            outs[4 * k + 1][...], outs[4 * k + 2][...], outs[4 * k + 3][...] = _adamw(g, w_ref[...], m_ref[...], v_ref[...])

    outs = _call(
        body, "adam_row_weights", out_shape=[jax.ShapeDtypeStruct(w.shape, F32) for w in ws for _ in range(4)],
        compiler_params=_params(),
    )(*parts, *ws, *ms, *vs)
    return [tuple(outs[4 * k:4 * k + 4]) for k in range(n)]


def _reduce_adam_slab(parts, glr, w_t, m_t, v_t, me):
    cols, rows = w_t.shape
    shift = jnp.asarray(SLAB_SHIFT, jnp.int32)[me]
    glr_at = jnp.where(me == GLR_DEV, GLR_LOCAL, cols).astype(jnp.int32)

    def body(s_ref, p_ref, glr_ref, w_ref, m_ref, v_ref, g_ref, d_ref, mo_ref, vo_ref, slab_t):
        shift, glr_at = s_ref[0], s_ref[1]
        tall = jnp.concatenate([_sum_partials(p_ref.at[:, j]).T for j in range(SLAB_BLOCKS)], axis=0)
        before = pltpu.roll(tall, SLAB_W - shift, 0)
        after = pltpu.roll(tall, lax.rem(SLAB_W - shift + GLA_RANK, SLAB_W), 0)
        wide = jnp.concatenate([glr_ref[...].T, jnp.zeros((SLAB_W - LANES, LANES), F32)], axis=0)
        placed = pltpu.roll(wide, lax.rem(glr_at, SLAB_W), 0)
        row = lax.broadcasted_iota(jnp.int32, (SLAB_W, LANES), 0)
        slab_t[...] = jnp.where(row < glr_at, before, jnp.where(row < glr_at + GLA_RANK, placed, after))
        g = slab_t[pl.ds(0, cols), :]
        g_ref[...] = g
        d_ref[...], mo_ref[...], vo_ref[...] = _adamw(g, w_ref[...], m_ref[...], v_ref[...])

    blk = pl.BlockSpec((cols, LANES), lambda i, s: (0, i))
    return _call(
        body, "adam_w_in", out_shape=[jax.ShapeDtypeStruct((cols, rows), F32)] * 4,
        grid_spec=pltpu.PrefetchScalarGridSpec(
            num_scalar_prefetch=1, grid=(rows // LANES,),
            in_specs=[pl.BlockSpec((parts.shape[0], SLAB_BLOCKS, LANES, LANES), lambda i, s: (0, 0, i, 0)),
                      pl.BlockSpec((LANES, LANES), lambda i, s: (i, 0)), blk, blk, blk],
            out_specs=[blk] * 4, scratch_shapes=[pltpu.VMEM((SLAB_W, LANES), F32)]),
        compiler_params=_params(("arbitrary",)),
    )(jnp.stack([shift, glr_at]), parts, glr, w_t, m_t, v_t)


def _reduce_small(parts):
    def body(p_ref, o_ref):
        o_ref[...] = _sum_partials(p_ref)

    return _call(body, "reduce_small", out_shape=jax.ShapeDtypeStruct(parts.shape[1:], F32))(parts)


def _adam_small(g, w, m, v):
    def body(g_ref, w_ref, m_ref, v_ref, d_ref, mo_ref, vo_ref):
        d_ref[...], mo_ref[...], vo_ref[...] = _adamw(g_ref[...], w_ref[...], m_ref[...], v_ref[...])

    return _call(body, "adam_small", out_shape=[jax.ShapeDtypeStruct(g.shape, F32)] * 3)(g, w, m, v)


def _pack_rows(arrs):
    rows = []
    for a in arrs:
        flat = a.reshape(-1).astype(F32)
        pad = (-flat.shape[0]) % LANES
        rows.append(jnp.pad(flat, (0, pad)).reshape(-1, LANES))
    packed = jnp.concatenate(rows, axis=0)
    return jnp.pad(packed, ((0, (-packed.shape[0]) % 8), (0, 0)))


def _unpack_rows(packed, shapes):
    out, r = [], 0
    for shp in shapes:
        size = 1
        for s in shp:
            size *= s
        nrows = -(-size // LANES)
        out.append(packed[r:r + nrows].reshape(-1)[:size].reshape(shp))
        r += nrows
    return out


def _shard_to_slab(shard, d):
    glr = jnp.zeros((D_MODEL, GLA_RANK), shard.dtype)
    if d == GLR_DEV:
        glr = shard[:, GLR_LOCAL:GLR_LOCAL + GLA_RANK]
        shard = jnp.concatenate([shard[:, :GLR_LOCAL], shard[:, GLR_LOCAL + GLA_RANK:]], axis=1)
    return jnp.pad(shard, ((0, 0), (SLAB_SHIFT[d], SLAB_W - SLAB_SHIFT[d] - shard.shape[1]))), glr


def kernel(x, meta_tokens, norm_gain, w_in, w_gate_up, b_gate, ret_norm_gain, gla_norm_gain, w_branch_ret, w_branch_gla, w_out, final_norm_gain, loss_target, m_meta_tokens, m_norm_gain, m_w_in, m_w_gate_up, m_b_gate, m_ret_norm_gain, m_gla_norm_gain, m_w_branch_ret, m_w_branch_gla, m_w_out, m_final_norm_gain, v_meta_tokens, v_norm_gain, v_w_in, v_w_gate_up, v_b_gate, v_ret_norm_gain, v_gla_norm_gain, v_w_branch_ret, v_w_branch_gla, v_w_out, v_final_norm_gain):
    xi, yi, ci = _position()
    me = _index(xi, yi, ci)
    seq = x.shape[1]
    t_rows = seq + TILE
    in_shard = w_in.shape[2]
    gu_shard = w_gate_up.shape[2]
    meta_shard = meta_tokens.shape[1]
    ret_rows, gla_rows, out_rows = w_branch_ret.shape[1], w_branch_gla.shape[1], w_out.shape[1]

    assert in_shard == IN_SHARD
    slab_local, glr_local = lax.switch(me, [functools.partial(_shard_to_slab, d=d) for d in range(N_DEV)], w_in[0])
    small_local = jnp.concatenate([meta_tokens, jnp.pad(w_gate_up[0], ((0, 0), (0, LANES - gu_shard))),
                                   glr_local.reshape(-1, LANES)], axis=0)
    slab_local = slab_local.astype(BF16)
    first_halves, g_small = _all_gather([slab_local[:, :HALF_W], small_local], "all_gather_shards")
    n_small = N_META + GLA_RANK
    w_glr = jnp.pad(g_small[GLR_DEV, n_small:].reshape(D_MODEL, GLA_RANK), ((0, 0), (0, LANES - GLA_RANK))).astype(BF16)
    meta_full = jnp.transpose(g_small[:, :N_META, :], (1, 0, 2)).reshape(N_META, D_MODEL)
    wgu_full = jnp.transpose(g_small[:, N_META:n_small, :gu_shard], (1, 0, 2)).reshape(GLA_RANK, GLA_HEADS * GLA_K)
    wgu_pad = jnp.pad(wgu_full, ((0, LANES - GLA_RANK), (0, 0)))

    rope = _rope_tables(t_rows // TILE)
    lg = jnp.log1p(-(2.0 ** (-5.0 - jnp.arange(RET_HEADS, dtype=F32))))

    head = jnp.concatenate([jnp.zeros((PAD_ROWS, D_MODEL), F32), meta_full], axis=0)
    proj_first, second_halves = _inproj_first(head, x[0], norm_gain, first_halves, slab_local[:, HALF_W:])
    slabs = (first_halves, second_halves)
    ut, proj, glr = _inproj_tiles(head, x[0], norm_gain, slabs, w_glr, proj_first)
    o_ret_raw, o_ret, ret_states, (g_br, g_bg, g_o) = _ret_fwd(
        proj, rope, ret_norm_gain, lg, [w_branch_ret[0].astype(BF16), w_branch_gla[0].astype(BF16), w_out[0].astype(BF16)])
    w_br, w_bg, w_o = g_br.reshape(RET_W, D_MODEL), g_bg.reshape(GLA_W, D_MODEL), g_o.reshape(D_MODEL, D_MODEL)
    masks, cum_fwd, cum_bwd = _gla_tables()
    o_gla_raw, o_gla, gla_states, gla_scores_t = _gla_fwd(proj, glr, wgu_pad, b_gate, gla_norm_gain, masks, cum_fwd)
    (dh1, d_mr, d_mg, do_ret, do_gla, loss_part, d_gfinal, dw_br, dw_bg, dw_o) = _merge_fwd_bwd(
        o_ret, o_gla, proj, x[0], loss_target[0], final_norm_gain.reshape(1, D_MODEL), w_br, w_bg, w_o)

    d_rq, d_rk, d_rv, d_rg, d_gret = _ret_bwd(proj, rope, ret_norm_gain, lg, o_ret_raw, do_ret, ret_states)
    d_gq, d_gk, d_gv, d_gg, dglr_parts, d_wgu, d_bgate, d_ggla = _gla_bwd(
        proj, glr, wgu_pad, b_gate, gla_norm_gain, o_gla_raw, do_gla, gla_states, gla_scores_t, masks, cum_fwd, cum_bwd)
    dseg = dict(rq=d_rq, rk=d_rk, rv=d_rv, rg=d_rg, gq=d_gq, gk=d_gk, gv=d_gv, gg=d_gg, mr=d_mr, mg=d_mg)
    row_sends = [dw_br.reshape(N_DEV, ret_rows, D_MODEL), dw_bg.reshape(N_DEV, gla_rows, D_MODEL),
                 dw_o.reshape(N_DEV, out_rows, D_MODEL)]
    dw_blocks, dw_glr, sib_in, sib_rows = _inproj_bwd_w(ut, dseg, dglr_parts, row_sends)
    core = ci.astype(jnp.int32).reshape(1)
    chip_partials = [_chip_partial_slab(dw_blocks, sib_in, core)] + list(_chip_partial_rows(row_sends, list(sib_rows), core))
    grad_x, d_head, d_gnorm, p_in, p_br, p_bg, p_o = _inproj_bwd_x(
        dseg, dglr_parts, head, x[0], dh1, norm_gain, slabs, w_glr, chip_partials)
    small_shapes = [(N_META, D_MODEL), (1, D_MODEL), (GLA_RANK, GLA_HEADS * GLA_K), (1, GLA_HEADS * GLA_K),
                    (1, RET_W), (1, GLA_W), (1, D_MODEL), (1, LANES), (D_MODEL, GLA_RANK)]
    small_part = _pack_rows([d_head[PAD_ROWS:], d_gnorm, d_wgu[:GLA_RANK], d_bgate, d_gret, d_ggla, d_gfinal, loss_part,
                             dw_glr[:, :GLA_RANK]])
    (p_small,) = _all_gather([small_part], "all_gather_small_partials")

    (g_meta_f, g_gnorm, g_wgu_f, g_bgate, g_gret, g_ggla, g_gfinal, loss_all,
     g_wglr) = _unpack_rows(_reduce_small(p_small), small_shapes)
    g_w_in, d_w_in, nm_w_in, nv_w_in = [a.T for a in _reduce_adam_slab(
        p_in, jnp.pad(g_wglr, ((0, 0), (0, LANES - GLA_RANK))), w_in[0].T, m_w_in[0].T, v_w_in[0].T, me)]
    ((g_w_br, d_w_br, nm_w_br, nv_w_br), (g_w_bg, d_w_bg, nm_w_bg, nv_w_bg), (g_w_o, d_w_o, nm_w_o, nv_w_o)) = _reduce_adam_rows(
        [p_br, p_bg, p_o], [w_branch_ret[0], w_branch_gla[0], w_out[0]], [m_w_branch_ret[0], m_w_branch_gla[0], m_w_out[0]],
        [v_w_branch_ret[0], v_w_branch_gla[0], v_w_out[0]])
    g_meta = lax.dynamic_slice_in_dim(g_meta_f, me * meta_shard, meta_shard, axis=1)
    g_wgu = lax.dynamic_slice_in_dim(g_wgu_f, me * gu_shard, gu_shard, axis=1)
    s_g = [g_meta, g_gnorm, g_wgu, g_bgate, g_gret, g_ggla, g_gfinal]
    s_w = [meta_tokens, norm_gain, w_gate_up[0], b_gate, ret_norm_gain, gla_norm_gain, final_norm_gain]
    s_m = [m_meta_tokens, m_norm_gain, m_w_gate_up[0], m_b_gate, m_ret_norm_gain, m_gla_norm_gain, m_final_norm_gain]
    s_v = [v_meta_tokens, v_norm_gain, v_w_gate_up[0], v_b_gate, v_ret_norm_gain, v_gla_norm_gain, v_final_norm_gain]
    shapes = [a.shape for a in s_g]
    s_d, s_nm, s_nv = [_unpack_rows(p, shapes) for p in _adam_small(*[_pack_rows(l) for l in (s_g, s_w, s_m, s_v)])]

    loss = loss_all[0, 0]
    grad_x = grad_x[None]

    def order(meta, gnorm, win, wgu, bgate, gret, ggla, wbr, wbg, wo, gfin):
        return (meta, gnorm, win[None], wgu[None], bgate, gret, ggla, wbr[None], wbg[None], wo[None], gfin.reshape(final_norm_gain.shape))

    def small(l):
        return dict(meta=l[0], gnorm=l[1], wgu=l[2], bgate=l[3], gret=l[4], ggla=l[5], gfin=l[6])

    grads = order(win=g_w_in, wbr=g_w_br, wbg=g_w_bg, wo=g_w_o, **small(s_g))
    deltas = order(win=d_w_in, wbr=d_w_br, wbg=d_w_bg, wo=d_w_o, **small(s_d))
    new_m = order(win=nm_w_in, wbr=nm_w_br, wbg=nm_w_bg, wo=nm_w_o, **small(s_nm))
    new_v = order(win=nv_w_in, wbr=nv_w_br, wbg=nv_w_bg, wo=nv_w_o, **small(s_nv))
    return (loss, grad_x, *grads, *deltas, *new_m, *new_v)
```

```python
import functools

import jax
import jax.numpy as jnp
from jax import lax
from jax.experimental import pallas as pl
from jax.experimental.pallas import tpu as pltpu

F32 = jnp.float32
BF16 = jnp.bfloat16

D_MODEL = 1024
N_META = 16
TILE = 256
PAD_ROWS = TILE - N_META
RET_HEADS = 4
RET_QK = 256
RET_V = 512
RET_W = RET_HEADS * RET_V
GLA_HEADS = 4
GLA_K = 128
GLA_V = 256
GLA_W = GLA_HEADS * GLA_V
GLA_RANK = 16
GLA_TAU = 16.0
GLA_CHUNK = 16
ROPE_BASE = 10000.0
EPS = 1e-6
LANES = 128
N_DEV = 8
SEG_NAMES = ("rq", "rk", "rv", "rg", "gq", "gk", "gv", "gg", "mr", "mg")
SEG_W = (1024, 1024, 2048, 2048, 512, 512, 1024, 1024, 1024, 1024)
SEG_OFF = tuple(sum(SEG_W[:i]) for i in range(len(SEG_W)))
AL_COLS = sum(SEG_W)
IN_COLS = AL_COLS + GLA_RANK
GLR_OFF = sum(SEG_W[:8])
IN_SHARD = IN_COLS // N_DEV


def _aligned_col(c):
    assert c <= GLR_OFF or c >= GLR_OFF + GLA_RANK
    return c if c <= GLR_OFF else c - GLA_RANK


SLAB_BOUND = tuple(_aligned_col(IN_SHARD * d) for d in range(N_DEV + 1))
SLAB_BLK0 = tuple(b // LANES for b in SLAB_BOUND[:-1])
SLAB_SHIFT = tuple(b % LANES for b in SLAB_BOUND[:-1])
SLAB_BLOCKS = max(-(-SLAB_BOUND[d + 1] // LANES) - SLAB_BLK0[d] for d in range(N_DEV))
SLAB_W = SLAB_BLOCKS * LANES
GLR_DEV = GLR_OFF // IN_SHARD
GLR_LOCAL = GLR_OFF - GLR_DEV * IN_SHARD
assert all(SLAB_BLK0[d] + SLAB_BLOCKS <= AL_COLS // LANES for d in range(N_DEV))
VMEM_LIMIT = 58 * 1024 * 1024
ADAM_LR, ADAM_B1, ADAM_B2, ADAM_EPS, ADAM_WD, ADAM_STEP = 0.001, 0.9, 0.999, 1e-08, 0.01, 10
ANY = pl.BlockSpec(memory_space=pl.ANY)
MESH = pl.DeviceIdType.MESH


def _call(body, name, **kw):
    return pl.pallas_call(body, name=name, **kw)


def _params(sem=None):
    return pltpu.CompilerParams(dimension_semantics=sem, vmem_limit_bytes=VMEM_LIMIT)


def _mm(a, b):
    return jnp.dot(a, b, preferred_element_type=F32)


def _mm_nt(a, b):
    return lax.dot_general(a, b, (((1,), (1,)), ((), ())), preferred_element_type=F32)


def _mm_tn(a, b):
    return lax.dot_general(a, b, (((0,), (0,)), ((), ())), preferred_element_type=F32)


def _sigmoid(x):
    return jax.nn.sigmoid(x)


def _rope(t, cos, sin):
    half = t.shape[-1] // 2
    t1, t2 = t[:, :half], t[:, half:]
    return jnp.concatenate([t1 * cos - t2 * sin, t2 * cos + t1 * sin], axis=-1)


def _rope_bwd(g, cos, sin):
    half = g.shape[-1] // 2
    g1, g2 = g[:, :half], g[:, half:]
    return jnp.concatenate([g1 * cos + g2 * sin, g2 * cos - g1 * sin], axis=-1)


def _row_mean(x):
    return jnp.mean(x, axis=-1, keepdims=True)


def _col_sum(x):
    return jnp.sum(x, axis=0, keepdims=True)


def _tile_rows(head_ref, x_ref):
    return jnp.where(pl.program_id(0) == 0, head_ref[...], x_ref[...])


def _head_spec():
    return pl.BlockSpec((TILE, D_MODEL), lambda i: (0, 0))


def _x_spec():
    return pl.BlockSpec((TILE, D_MODEL), lambda i: (jnp.maximum(i - 1, 0), 0))


def _slab_plan():
    interior, shared = [], []
    for d in range(N_DEV):
        lo, hi = -(-SLAB_BOUND[d] // LANES), SLAB_BOUND[d + 1] // LANES
        interior.append((d, LANES * (lo - SLAB_BLK0[d]), LANES * lo, LANES * (hi - lo)))
        if d + 1 < N_DEV and SLAB_BOUND[d + 1] % LANES:
            shared.append((hi, d, hi - SLAB_BLK0[d]))
    return interior, shared


N_BLOCKS = AL_COLS // LANES
HALF_BLOCKS = 4
HALF_W = LANES * HALF_BLOCKS


def _half_blocks():
    interior, _ = _slab_plan()
    first = [dst // LANES + j for _, src, dst, width in interior for j in range(width // LANES) if src // LANES + j < HALF_BLOCKS]
    return first, [b for b in range(N_BLOCKS) if b not in first]


def _w_scratch(n_blocks=N_BLOCKS):
    return [pltpu.VMEM((D_MODEL, LANES * n_blocks), BF16), pltpu.VMEM((D_MODEL, LANES), BF16),
            pltpu.VMEM((2 * (N_DEV - 1), D_MODEL, LANES), BF16), pltpu.SemaphoreType.DMA((4 * N_DEV,))]


W_SCRATCH = _w_scratch


def _slab_cols(halves, d, lo, n):
    out = []
    for half, start, end in zip(halves, (0, HALF_W), (HALF_W, SLAB_W)):
        a, b = max(lo, start), min(lo + n, end)
        if a < b:
            out.append((half.at[d, :, pl.ds(a - start, b - a)], a - lo, b - a))
    return out


def _load_weight(halves, wg_hbm, w_vm, wg_vm, edge_vm, sem, blocks=None):
    blocks = list(range(N_BLOCKS) if blocks is None else blocks)
    place = {b: i for i, b in enumerate(blocks)}
    interior, shared = _slab_plan()
    copies = [] if wg_hbm is None else [(wg_hbm, wg_vm)]
    for d, src, dst, width in interior:
        b0 = dst // LANES
        runs = []
        for b in range(b0, b0 + width // LANES):
            if b in place and runs and b == sum(runs[-1]):
                runs[-1][1] += 1
            elif b in place:
                runs.append([b, 1])
        for b, n in runs:
            for piece, off, w in _slab_cols(halves, d, src + LANES * (b - b0), LANES * n):
                copies.append((piece, w_vm.at[:, pl.ds(LANES * place[b] + off, w)]))
    edges = []
    for blk, d, j in shared:
        if blk in place:
            ((low, _, _),), ((high, _, _),) = _slab_cols(halves, d, LANES * j, LANES), _slab_cols(halves, d + 1, 0, LANES)
            copies += [(low, edge_vm.at[2 * len(edges)]), (high, edge_vm.at[2 * len(edges) + 1])]
            edges.append(blk)
    copies = [pltpu.make_async_copy(a, b, sem.at[i]) for i, (a, b) in enumerate(copies)]
    for cp in copies:
        cp.start()
    for cp in copies:
        cp.wait()
    for n, blk in enumerate(edges):
        w_vm[:, LANES * place[blk]:LANES * (place[blk] + 1)] = edge_vm[2 * n] + edge_vm[2 * n + 1]


def _proj_specs(names, n_units, where):
    specs = []
    for name in names:
        s = SEG_NAMES.index(name)
        nblk = SEG_W[s] // n_units // LANES
        base = SEG_OFF[s] // LANES
        assert base % nblk == 0
        specs.append(pl.BlockSpec((nblk, TILE, LANES), lambda *g, base=base, nblk=nblk: (base // nblk + where(*g)[0], where(*g)[1], 0)))
    return specs


def _cols(ref, unit=0, n_units=1):
    n = ref.shape[0] // n_units
    return ref[unit * n] if n == 1 else jnp.concatenate([ref[unit * n + j] for j in range(n)], axis=1)


def _prenorm(head_ref, x_ref, g_ref):
    x = _tile_rows(head_ref, x_ref)
    r = lax.rsqrt(_row_mean(x * x) + EPS)
    return (x * r * g_ref[...]).astype(BF16).astype(F32)


def _project(u, w_vm, n, store):
    cuts = [8 * i for i in range(max(n // 8, 1))] + [n]
    for lo, hi in zip(cuts[:-1], cuts[1:]):
        res = _mm(u, w_vm[:, LANES * lo:LANES * hi]).astype(BF16)
        for j in range(lo, hi):
            store(j, res[:, LANES * (j - lo):LANES * (j - lo + 1)])


def _inproj_first(head, x, g_norm, first_halves, second_half):
    t_rows = x.shape[0] + TILE
    nt = t_rows // TILE
    first, _ = _half_blocks()
    n = len(first)

    def body(head_ref, x_ref, g_ref, first_hbm, second_hbm, proj_ref, gathered, w_vm, wg_vm, edge_vm, sem, *sems):
        gather = _RelayGather([second_hbm], [gathered], sems)
        for step, stages in ((0, (0,)), (nt // 2, (1, 2)), (9 * nt // 10, (3,)), (nt - 1, (4,))):
            @pl.when(pl.program_id(0) == step)
            def _():
                for s in stages:
                    gather.stage(s)

        @pl.when(pl.program_id(0) == 0)
        def _():
            _load_weight((first_hbm,), None, w_vm, wg_vm, edge_vm, sem, first)

        def store(i, block):
            proj_ref[i] = block
        _project(_prenorm(head_ref, x_ref, g_ref).astype(BF16), w_vm, n, store)

    return _call(
        body, "inproj_fwd_first", grid=(nt,),
        out_shape=[jax.ShapeDtypeStruct((n, t_rows, LANES), BF16), jax.ShapeDtypeStruct((N_DEV, *second_half.shape), BF16)],
        in_specs=[_head_spec(), _x_spec(), pl.BlockSpec((1, D_MODEL), lambda i: (0, 0)), ANY, ANY],
        out_specs=[pl.BlockSpec((n, TILE, LANES), lambda i: (0, i, 0)), ANY],
        scratch_shapes=_w_scratch(n) + _gather_sems(1), compiler_params=_params(("arbitrary",)),
    )(head, x, g_norm, first_halves, second_half)


def _inproj_tiles(head, x, g_norm, halves, w_glr, proj_first):
    t_rows = x.shape[0] + TILE
    nt = t_rows // TILE
    first, rest = _half_blocks()

    def body(head_ref, x_ref, g_ref, first_hbm, second_hbm, wg_hbm, pf_ref, ut_ref, proj_ref, glr_ref, w_vm, wg_vm, edge_vm, sem):
        @pl.when(pl.program_id(0) == 0)
        def _():
            _load_weight((first_hbm, second_hbm), wg_hbm, w_vm, wg_vm, edge_vm, sem, rest)

        u32 = _prenorm(head_ref, x_ref, g_ref)
        u = u32.astype(BF16)
        ut_ref[...] = u32.T.astype(BF16)

        def store(i, block):
            proj_ref[rest[i]] = block
        _project(u, w_vm, len(rest), store)
        for i, b in enumerate(first):
            proj_ref[b] = pf_ref[i]
        glr_ref[...] = _mm(u, wg_vm[...])

    return _call(
        body, "inproj_fwd_tiles", grid=(nt,),
        out_shape=[jax.ShapeDtypeStruct((nt, D_MODEL, TILE), BF16), jax.ShapeDtypeStruct((N_BLOCKS, t_rows, LANES), BF16),
                   jax.ShapeDtypeStruct((t_rows, LANES), F32)],
        in_specs=[_head_spec(), _x_spec(), pl.BlockSpec((1, D_MODEL), lambda i: (0, 0)), ANY, ANY, ANY,
                  pl.BlockSpec((len(first), TILE, LANES), lambda i: (0, i, 0))],
        out_specs=[pl.BlockSpec((None, D_MODEL, TILE), lambda i: (i, 0, 0)), pl.BlockSpec((N_BLOCKS, TILE, LANES), lambda i: (0, i, 0)),
                   pl.BlockSpec((TILE, LANES), lambda i: (i, 0))],
        scratch_shapes=_w_scratch(len(rest)), compiler_params=_params(("arbitrary",)),
    )(head, x, g_norm, *halves, w_glr, proj_first)


def _ret_decay(lgh):
    i = lax.broadcasted_iota(jnp.int32, (TILE, TILE), 0)
    j = lax.broadcasted_iota(jnp.int32, (TILE, TILE), 1)
    rel = (i - j).astype(F32)
    return jnp.where(rel >= 0, jnp.exp(jnp.maximum(rel, 0.0) * lgh), 0.0)


def _ret_vectors(lgh):
    idx = lax.broadcasted_iota(jnp.int32, (TILE, 1), 0).astype(F32)
    xi = jnp.exp((idx + 1.0) * lgh)
    zeta = jnp.exp((TILE - 1.0 - idx) * lgh)
    gc = jnp.exp(jnp.full((1, 1), float(TILE), F32) * lgh)
    return xi, zeta, gc


def _rope_tables(nt):
    half = RET_QK // 2
    inv = ROPE_BASE ** (-jnp.arange(half, dtype=F32) / half)
    base = (jnp.arange(nt, dtype=F32) * TILE - float(PAD_ROWS))[:, None, None] * inv[None, None, :]
    off = jnp.arange(TILE, dtype=F32)[:, None] * inv[None, :]
    return jnp.cos(base), jnp.sin(base), jnp.cos(off), jnp.sin(off)


def _rope_specs(tile_of):
    return [pl.BlockSpec((None, 1, RET_QK // 2), lambda i: (tile_of(i), 0, 0))] * 2 + [pl.BlockSpec((TILE, RET_QK // 2), lambda i: (0, 0))] * 2


def _rope_angles(cb_ref, sb_ref, co_ref, so_ref):
    cb, sb, co, so = cb_ref[...], sb_ref[...], co_ref[...], so_ref[...]
    return cb * co - sb * so, sb * co + cb * so


def _ret_fwd(proj, rope, gain, lg, row_shards):
    t_rows = proj.shape[1]
    nt = t_rows // TILE
    ns = len(row_shards)

    def body(lg_ref, q_ref, k_ref, v_ref, g_ref, cb_ref, sb_ref, co_ref, so_ref, gain_ref, *rest):
        shard_refs, (oraw_ref, oret_ref, st_ref), gathered = rest[:ns], rest[ns:ns + 3], rest[ns + 3:2 * ns + 3]
        s_acc, dm = rest[2 * ns + 3:2 * ns + 5]
        gather = _Exchange(shard_refs, gathered, rest[2 * ns + 5:], among_chips=False)
        t = pl.program_id(0)

        @pl.when(t == 0)
        def _():
            gather.start()
            s_acc[...] = jnp.zeros_like(s_acc)
            for h in range(RET_HEADS):
                dm[h] = _ret_decay(lg_ref[h])

        @pl.when(t == nt - 1)
        def _():
            gather.finish()

        cos_t, sin_t = _rope_angles(cb_ref, sb_ref, co_ref, so_ref)
        for h in range(RET_HEADS):
            lgh = lg_ref[h]
            q = _rope(_cols(q_ref, h, RET_HEADS).astype(F32), cos_t, sin_t)
            k = _rope(_cols(k_ref, h, RET_HEADS).astype(F32), cos_t, sin_t) * (RET_QK ** -0.5)
            xi, zeta, gc = _ret_vectors(lgh)
            v = _cols(v_ref, h, RET_HEADS)
            s_in = s_acc[h]
            p = (_mm_nt(q.astype(BF16), k.astype(BF16)) * dm[h]).astype(BF16)
            o = _mm(p, v) + _mm((q * xi).astype(BF16), s_in.astype(BF16))
            st_ref[h] = s_in.astype(BF16)
            s_acc[h] = s_in * gc + _mm_tn((k * zeta).astype(BF16), v)
            cols = slice(h * RET_V, (h + 1) * RET_V)
            oraw_ref[:, cols] = o
            oc = o - _row_mean(o)
            n = oc * lax.rsqrt(_row_mean(oc * oc) + EPS) * gain_ref[:, cols]
            g = _cols(g_ref, h, RET_HEADS).astype(F32)
            oret_ref[:, cols] = (n * g * _sigmoid(g)).astype(BF16)

    row = lambda w: pl.BlockSpec((TILE, w), lambda t: (t, 0))
    outs = _call(
        body, "ret_fwd", grid=(nt,),
        out_shape=[jax.ShapeDtypeStruct((t_rows, RET_W), F32), jax.ShapeDtypeStruct((t_rows, RET_W), BF16),
                   jax.ShapeDtypeStruct((RET_HEADS, nt, RET_QK, RET_V), BF16)]
                  + [jax.ShapeDtypeStruct((N_DEV, *a.shape), a.dtype) for a in row_shards],
        in_specs=[pl.BlockSpec(memory_space=pltpu.SMEM)] + _proj_specs(("rq", "rk", "rv", "rg"), 1, lambda t: (0, t)) + _rope_specs(lambda t: t) + [
                  pl.BlockSpec((1, RET_W), lambda t: (0, 0))] + [ANY] * ns,
        out_specs=[row(RET_W), row(RET_W), pl.BlockSpec((RET_HEADS, None, RET_QK, RET_V), lambda t: (0, t, 0, 0))] + [ANY] * ns,
        scratch_shapes=[pltpu.VMEM((RET_HEADS, RET_QK, RET_V), F32), pltpu.VMEM((RET_HEADS, TILE, TILE), F32)] + _exchange_sems(ns, N_DEV),
        compiler_params=_params(("arbitrary",)),
    )(lg, proj, proj, proj, proj, *rope, gain, *row_shards)
    return outs[0], outs[1], outs[2], outs[3:]


def _ret_bwd(proj, rope, gain, lg, o_raw, do_ret, states):
    t_rows = proj.shape[1]
    nt = t_rows // TILE

    def body(lg_ref, q_ref, k_ref, v_ref, g_ref, cb_ref, sb_ref, co_ref, so_ref, gain_ref, oraw_ref, do_ref, st_ref,
             dq_ref, dk_ref, dv_ref, dg_ref, dgain_ref, e_acc, dm):
        @pl.when(pl.program_id(0) == 0)
        def _():
            e_acc[...] = jnp.zeros_like(e_acc)
            for h in range(RET_HEADS):
                dm[h] = _ret_decay(lg_ref[h])
            dgain_ref[...] = jnp.zeros_like(dgain_ref)

        cos_t, sin_t = _rope_angles(cb_ref, sb_ref, co_ref, so_ref)
        for h in range(RET_HEADS):
            lgh = lg_ref[h]
            cols = slice(h * RET_V, (h + 1) * RET_V)
            qcols = slice(h * RET_QK, (h + 1) * RET_QK)
            q = _rope(_cols(q_ref, h, RET_HEADS).astype(F32), cos_t, sin_t)
            k = _rope(_cols(k_ref, h, RET_HEADS).astype(F32), cos_t, sin_t) * (RET_QK ** -0.5)
            xi, zeta, gc = _ret_vectors(lgh)
            v = _cols(v_ref, h, RET_HEADS)
            g = _cols(g_ref, h, RET_HEADS).astype(F32)
            o = oraw_ref[:, cols]
            do = do_ref[:, cols].astype(F32)
            oc = o - _row_mean(o)
            rstd = lax.rsqrt(_row_mean(oc * oc) + EPS)
            xh = oc * rstd
            gain_t = gain_ref[:, cols]
            sg = _sigmoid(g)
            dn = do * (g * sg)
            dg_ref[:, cols] = (do * (xh * gain_t) * (sg * (1.0 + g * (1.0 - sg)))).astype(BF16)
            dgain_ref[:, cols] += _col_sum(dn * xh)
            dxh = dn * gain_t
            dob = (rstd * (dxh - _row_mean(dxh) - xh * _row_mean(dxh * xh))).astype(BF16)
            dmat = dm[h]
            qb, kb = q.astype(BF16), k.astype(BF16)
            p = (_mm_nt(qb, kb) * dmat).astype(BF16)
            dp = (_mm_nt(dob, v) * dmat).astype(BF16)
            s_in = st_ref[h]
            e_in = e_acc[h]
            e_b = e_in.astype(BF16)
            dq = _mm(dp, kb) + _mm_nt(dob, s_in) * xi
            dk = _mm_tn(dp, qb) + _mm_nt(v, e_b) * zeta
            dv_ref[:, cols] = (_mm_tn(p, dob) + _mm((k * zeta).astype(BF16), e_b)).astype(BF16)
            e_acc[h] = e_in * gc + _mm_tn((q * xi).astype(BF16), dob)
            dq_ref[:, qcols] = _rope_bwd(dq, cos_t, sin_t).astype(BF16)
            dk_ref[:, qcols] = (_rope_bwd(dk, cos_t, sin_t) * (RET_QK ** -0.5)).astype(BF16)

    row = lambda w: pl.BlockSpec((TILE, w), lambda j: (nt - 1 - j, 0))
    vec = pl.BlockSpec((1, RET_W), lambda j: (0, 0))
    return _call(
        body, "ret_bwd", grid=(nt,),
        out_shape=[jax.ShapeDtypeStruct((t_rows, RET_HEADS * RET_QK), BF16), jax.ShapeDtypeStruct((t_rows, RET_HEADS * RET_QK), BF16),
                   jax.ShapeDtypeStruct((t_rows, RET_W), BF16), jax.ShapeDtypeStruct((t_rows, RET_W), BF16),
                   jax.ShapeDtypeStruct((1, RET_W), F32)],
        in_specs=[pl.BlockSpec(memory_space=pltpu.SMEM)] + _proj_specs(("rq", "rk", "rv", "rg"), 1, lambda j: (0, nt - 1 - j)) + _rope_specs(lambda j: nt - 1 - j) + [vec,
                  row(RET_W), row(RET_W), pl.BlockSpec((RET_HEADS, None, RET_QK, RET_V), lambda j: (0, nt - 1 - j, 0, 0))],
        out_specs=[row(RET_HEADS * RET_QK), row(RET_HEADS * RET_QK), row(RET_W), row(RET_W), vec],
        scratch_shapes=[pltpu.VMEM((RET_HEADS, RET_QK, RET_V), F32), pltpu.VMEM((RET_HEADS, TILE, TILE), F32)],
        compiler_params=_params(("arbitrary",)),
    )(lg, proj, proj, proj, proj, *rope, gain, o_raw, do_ret, states)


GLA_LEVELS = (32, 64, 128, 256)
N_TERMS = 1 + len(GLA_LEVELS)


def _gla_tables():
    p = jnp.arange(TILE)[:, None]
    r = jnp.arange(TILE)[None, :]
    masks = [(p // GLA_CHUNK == r // GLA_CHUNK) & (r <= p)]
    for blk in GLA_LEVELS:
        masks.append((p // blk == r // blk) & (p % blk >= blk // 2) & (r % blk < blk // 2))
    masks = jnp.stack(masks + [m.T for m in masks]).astype(F32)
    cum_fwd = jnp.concatenate([r <= p, masks[0] > 0], axis=0).astype(BF16)
    cum_bwd = jnp.concatenate([r >= p, masks[N_TERMS] > 0], axis=1).astype(BF16)
    return masks, cum_fwd, cum_bwd


def _split3(x):
    hi = x.astype(BF16)
    rest = x - hi.astype(F32)
    mid = rest.astype(BF16)
    lo = (rest - mid.astype(F32)).astype(BF16)
    return jnp.concatenate([hi, mid, lo], axis=1)


def _join3(y):
    w = y.shape[1] // 3
    return (y[:, 2 * w:] + y[:, w:2 * w]) + y[:, :w]


def _gla_decays(glr_ref, wgu_ref, b_ref, cum_ref):
    z = _mm(glr_ref[...].astype(BF16), wgu_ref[...].astype(BF16)) + b_ref[...]
    la = (jnp.minimum(z, 0.0) - jnp.log(1.0 + jnp.exp(-jnp.abs(z)))) / GLA_TAU
    width = la.shape[1]
    hi = la.astype(BF16)
    rest = la - hi.astype(F32)
    mid = rest.astype(BF16)
    lo = (rest - mid.astype(F32)).astype(BF16)
    y = _mm(cum_ref[...], jnp.concatenate([hi, mid, lo], axis=1))
    gb = (y[:, 2 * width:] + y[:, width:2 * width]) + y[:, :width]
    return z, gb[:TILE], gb[TILE:]


def _gla_prep(h, q_ref, k_ref, g_all, b_all, g_scr, ref_scr):
    cols = slice(h * GLA_K, (h + 1) * GLA_K)
    g, b = g_all[:, cols], b_all[:, cols]
    g_scr[h] = g
    factors = [(jnp.exp(b), jnp.exp(-b))]
    for lvl, blk in enumerate(GLA_LEVELS):
        for n in range(TILE // blk):
            ref_scr[h, lvl, n * blk:(n + 1) * blk, :] = jnp.broadcast_to(g_scr[h, pl.ds(n * blk + blk // 2 - 1, 1), :], (blk, GLA_K))
        x = g - ref_scr[h, lvl]
        factors.append((jnp.exp(jnp.minimum(x, 0.0)), jnp.exp(jnp.minimum(-x, 0.0))))
    g_last = g_scr[h, pl.ds(TILE - 1, 1), :]
    q = _cols(q_ref, h, GLA_HEADS).astype(F32) * (GLA_K ** -0.5)
    k = _cols(k_ref, h, GLA_HEADS).astype(F32)
    return q, k, factors, jnp.exp(g), jnp.exp(g_last), jnp.exp(g_last - g)


def _gla_scores(q, k, factors, m_ref):
    a = jnp.zeros((TILE, TILE), F32)
    for l, (fq, fk) in enumerate(factors):
        s = _mm_nt((q * fq).astype(BF16), (k * fk).astype(BF16))
        a = jnp.where(m_ref[l] > 0.0, s, a)
    return a


def _gla_fwd(proj, glr, wgu_pad, b_gate, gain, masks, cum_fwd):
    t_rows = glr.shape[0]
    nt = t_rows // TILE

    def body(q_ref, k_ref, v_ref, g_ref, glr_ref, wgu_ref, b_ref, gain_ref, m_ref, cum_ref, oraw_ref, ogla_ref, st_ref, at_ref,
             s_acc, g_scr, ref_scr):
        @pl.when(pl.program_id(0) == 0)
        def _():
            s_acc[...] = jnp.zeros_like(s_acc)

        _, g_all, b_all = _gla_decays(glr_ref, wgu_ref, b_ref, cum_ref)
        for h in range(GLA_HEADS):
            q, k, factors, e_g, e_last, e_end = _gla_prep(h, q_ref, k_ref, g_all, b_all, g_scr, ref_scr)
            v = _cols(v_ref, h, GLA_HEADS)
            st = s_acc[h]
            st_ref[h] = st
            a = _gla_scores(q, k, factors, m_ref)
            at_ref[h] = a.T.astype(BF16)
            o = _mm(a.astype(BF16), v) + _mm_nt((q * e_g).astype(BF16), st.astype(BF16))
            s_acc[h] = st * e_last + _mm(v.astype(F32).T.astype(BF16), (k * e_end).astype(BF16))
            cols = slice(h * GLA_V, (h + 1) * GLA_V)
            oraw_ref[:, cols] = o
            n = o * lax.rsqrt(_row_mean(o * o) + EPS) * gain_ref[:, cols]
            g = _cols(g_ref, h, GLA_HEADS).astype(F32)
            ogla_ref[:, cols] = (n * g * _sigmoid(g)).astype(BF16)

    row = lambda w: pl.BlockSpec((TILE, w), lambda t: (t, 0))
    whole = lambda *shape: pl.BlockSpec(shape, lambda t: (0,) * len(shape))
    return _call(
        body, "gla_fwd", grid=(nt,),
        out_shape=[jax.ShapeDtypeStruct((t_rows, GLA_W), F32), jax.ShapeDtypeStruct((t_rows, GLA_W), BF16),
                   jax.ShapeDtypeStruct((GLA_HEADS, nt, GLA_V, GLA_K), F32), jax.ShapeDtypeStruct((GLA_HEADS, t_rows, TILE), BF16)],
        in_specs=_proj_specs(("gq", "gk", "gv", "gg"), 1, lambda t: (0, t)) + [row(LANES), whole(LANES, GLA_HEADS * GLA_K),
                  whole(1, GLA_HEADS * GLA_K), whole(1, GLA_W), whole(N_TERMS, TILE, TILE), whole(2 * TILE, TILE)],
        out_specs=[row(GLA_W), row(GLA_W), pl.BlockSpec((GLA_HEADS, None, GLA_V, GLA_K), lambda t: (0, t, 0, 0)),
                   pl.BlockSpec((GLA_HEADS, TILE, TILE), lambda t: (0, t, 0))],
        scratch_shapes=[pltpu.VMEM((GLA_HEADS, GLA_V, GLA_K), F32), pltpu.VMEM((GLA_HEADS, TILE, GLA_K), F32),
                        pltpu.VMEM((GLA_HEADS, len(GLA_LEVELS), TILE, GLA_K), F32)],
        compiler_params=_params(("arbitrary",)),
    )(proj, proj, proj, proj, glr, wgu_pad, b_gate, gain, masks, cum_fwd)


def _gla_bwd(proj, glr, wgu_pad, b_gate, gain, o_raw, do_gla, states, a_t, masks, cum_fwd, cum_bwd):
    t_rows = glr.shape[0]
    nt = t_rows // TILE

    def body(q_ref, k_ref, v_ref, g_ref, glr_ref, wgu_ref, b_ref, gain_ref, m_ref, cum_ref, cumb_ref, oraw_ref, do_ref, st_ref, at_ref,
             dq_ref, dk_ref, dv_ref, dg_ref, dglr_ref, dwgu_ref, dbg_ref, dgain_ref, d_acc, g_scr, ref_scr, dref_scr):
        @pl.when(pl.program_id(0) == 0)
        def _():
            d_acc[...] = jnp.zeros_like(d_acc)
            dwgu_ref[...] = jnp.zeros_like(dwgu_ref)
            dbg_ref[...] = jnp.zeros_like(dbg_ref)
            dgain_ref[...] = jnp.zeros_like(dgain_ref)

        z_all, g_all, b_all = _gla_decays(glr_ref, wgu_ref, b_ref, cum_ref)
        dla_parts = []
        for h in range(GLA_HEADS):
            q, k, factors, e_g, e_last, e_end = _gla_prep(h, q_ref, k_ref, g_all, b_all, g_scr, ref_scr)
            v = _cols(v_ref, h, GLA_HEADS)
            cols = slice(h * GLA_V, (h + 1) * GLA_V)
            kcols = slice(h * GLA_K, (h + 1) * GLA_K)
            o = oraw_ref[:, cols]
            do = do_ref[:, cols].astype(F32)
            g = _cols(g_ref, h, GLA_HEADS).astype(F32)
            rinv = lax.rsqrt(_row_mean(o * o) + EPS)
            nh = o * rinv
            gain_t = gain_ref[:, cols]
            sg = _sigmoid(g)
            dn = do * (g * sg)
            dg_ref[:, cols] = (do * (nh * gain_t) * (sg * (1.0 + g * (1.0 - sg)))).astype(BF16)
            dgain_ref[:, cols] += _col_sum(dn * nh)
            dnh = dn * gain_t
            dor = rinv * (dnh - nh * _row_mean(dnh * nh))
            dob = dor.astype(BF16)
            a_t = at_ref[h]
            da = _mm_nt(dob, v).astype(BF16)
            da_t = _mm_nt(v, dob).astype(BF16)
            st_in = st_ref[h]
            d_out = d_acc[h]
            d_out_b = d_out.astype(BF16)
            qg, kg = q * e_g, k * e_end
            dqg = _mm(dob, st_in.astype(BF16))
            dkg = _mm(v, d_out_b)
            dv_ref[:, cols] = (_mm(a_t, dob) + _mm_nt(kg.astype(BF16), d_out_b)).astype(BF16)
            d_acc[h] = d_out * e_last + _mm(dor.T.astype(BF16), qg.astype(BF16))
            dq = dqg * e_g
            dk = dkg * e_end
            dkg_kg = dkg * kg
            dg_cum = dqg * qg - dkg_kg
            db = None
            for l, (fq, fk) in enumerate(factors):
                qt, kt = q * fq, k * fk
                dqt = _mm(da * m_ref[l], kt.astype(BF16))
                dkt = _mm(da_t * m_ref[N_TERMS + l], qt.astype(BF16))
                dq = dq + dqt * fq
                dk = dk + dkt * fk
                diff = dqt * qt - dkt * kt
                if l == 0:
                    db = diff
                else:
                    dg_cum = dg_cum + diff
                    dref_scr[h, l - 1] = diff
            dq_ref[:, kcols] = (dq * (GLA_K ** -0.5)).astype(BF16)
            dk_ref[:, kcols] = dk.astype(BF16)
            g_scr[h] = dg_cum
            g_scr[h, pl.ds(TILE - 1, 1), :] += e_last * _col_sum(d_out * st_in) + _col_sum(dkg_kg)
            for lvl, blk in enumerate(GLA_LEVELS):
                for n in range(TILE // blk):
                    g_scr[h, pl.ds(n * blk + blk // 2 - 1, 1), :] -= _col_sum(dref_scr[h, lvl, n * blk:(n + 1) * blk, :])
            dla_parts.append(_join3(_mm(cumb_ref[...], jnp.concatenate([_split3(g_scr[h]), _split3(db)], axis=0))))
        dz = jnp.concatenate(dla_parts, axis=1) * (1.0 / GLA_TAU) * _sigmoid(-z_all)
        dzb = dz.astype(BF16)
        wgu_b = wgu_ref[...].astype(BF16)
        for h in range(GLA_HEADS):
            kcols = slice(h * GLA_K, (h + 1) * GLA_K)
            dglr_ref[h] = _mm_nt(dzb[:, kcols], wgu_b[:, kcols]).astype(BF16)
        dwgu_ref[...] += _mm(glr_ref[...].T.astype(BF16), dzb)
        dbg_ref[...] += _col_sum(dz)

    row = lambda w: pl.BlockSpec((TILE, w), lambda j: (nt - 1 - j, 0))
    whole = lambda *shape: pl.BlockSpec(shape, lambda j: (0,) * len(shape))
    return _call(
        body, "gla_bwd", grid=(nt,),
        out_shape=[jax.ShapeDtypeStruct((t_rows, GLA_HEADS * GLA_K), BF16), jax.ShapeDtypeStruct((t_rows, GLA_HEADS * GLA_K), BF16),
                   jax.ShapeDtypeStruct((t_rows, GLA_W), BF16), jax.ShapeDtypeStruct((t_rows, GLA_W), BF16),
                   jax.ShapeDtypeStruct((GLA_HEADS, t_rows, LANES), BF16), jax.ShapeDtypeStruct((LANES, GLA_HEADS * GLA_K), F32),
                   jax.ShapeDtypeStruct((1, GLA_HEADS * GLA_K), F32), jax.ShapeDtypeStruct((1, GLA_W), F32)],
        in_specs=_proj_specs(("gq", "gk", "gv", "gg"), 1, lambda j: (0, nt - 1 - j)) + [row(LANES),
                  whole(LANES, GLA_HEADS * GLA_K), whole(1, GLA_HEADS * GLA_K), whole(1, GLA_W),
                  whole(2 * N_TERMS, TILE, TILE), whole(2 * TILE, TILE), whole(TILE, 2 * TILE), row(GLA_W), row(GLA_W),
                  pl.BlockSpec((GLA_HEADS, None, GLA_V, GLA_K), lambda j: (0, nt - 1 - j, 0, 0)),
                  pl.BlockSpec((GLA_HEADS, TILE, TILE), lambda j: (0, nt - 1 - j, 0))],
        out_specs=[row(GLA_HEADS * GLA_K), row(GLA_HEADS * GLA_K), row(GLA_W), row(GLA_W),
                   pl.BlockSpec((GLA_HEADS, TILE, LANES), lambda j: (0, nt - 1 - j, 0)), whole(LANES, GLA_HEADS * GLA_K),
                   whole(1, GLA_HEADS * GLA_K), whole(1, GLA_W)],
        scratch_shapes=[pltpu.VMEM((GLA_HEADS, GLA_V, GLA_K), F32), pltpu.VMEM((GLA_HEADS, TILE, GLA_K), F32),
                        pltpu.VMEM((GLA_HEADS, len(GLA_LEVELS), TILE, GLA_K), F32),
                        pltpu.VMEM((GLA_HEADS, len(GLA_LEVELS), TILE, GLA_K), F32)],
        compiler_params=_params(("arbitrary",)),
    )(proj, proj, proj, proj, glr, wgu_pad, b_gate, gain, masks.astype(BF16), cum_fwd, cum_bwd, o_raw, do_gla, states, a_t)


def _merge_fwd_bwd(o_ret, o_gla, proj, x, target, g_final, w_br, w_bg, w_out):
    t_rows = x.shape[0] + TILE
    nt = t_rows // TILE

    def body(oret_ref, ogla_ref, mr_ref, mg_ref, h0_ref, tgt_ref, gf_ref, wbr_hbm, wbg_hbm, wout_hbm,
             dh1_ref, dmr_ref, dmg_ref, doret_ref, dogla_ref, loss_ref, dgf_ref, dwbr_hbm, dwbg_hbm, dwout_hbm,
             wbr, wbg, wout, abr, abg, aout, sem):
        i = pl.program_id(0)

        @pl.when(i == 0)
        def _():
            cps = [pltpu.make_async_copy(s, d, sem.at[n]) for n, (s, d) in enumerate(((wbr_hbm, wbr), (wbg_hbm, wbg), (wout_hbm, wout)))]
            for cp in cps:
                cp.start()
            abr[...] = jnp.zeros_like(abr)
            abg[...] = jnp.zeros_like(abg)
            aout[...] = jnp.zeros_like(aout)
            loss_ref[...] = jnp.zeros_like(loss_ref)
            dgf_ref[...] = jnp.zeros_like(dgf_ref)
            for cp in cps:
                cp.wait()
            dh1_ref[...] = jnp.zeros_like(dh1_ref)
            dmr_ref[...] = jnp.zeros_like(dmr_ref)
            dmg_ref[...] = jnp.zeros_like(dmg_ref)
            doret_ref[...] = jnp.zeros_like(doret_ref)
            dogla_ref[...] = jnp.zeros_like(dogla_ref)

        @pl.when(i > 0)
        def _():
            oret, ogla = oret_ref[...], ogla_ref[...]
            br, bg = _mm(oret, wbr[...]), _mm(ogla, wbg[...])
            sr, sg = _sigmoid(_cols(mr_ref).astype(F32)), _sigmoid(_cols(mg_ref).astype(F32))
            mb = (sr * br + sg * bg).astype(BF16)
            h1 = h0_ref[...] + _mm(mb, wout[...])
            r2 = lax.rsqrt(_row_mean(h1 * h1) + EPS)
            hn = h1 * r2
            gf = gf_ref[...]
            diff = hn * gf - tgt_ref[...]
            loss_ref[...] += 0.5 * jnp.sum(_row_mean(diff * diff))
            dy = diff * (1.0 / D_MODEL)
            dgf_ref[...] += _col_sum(dy * hn)
            dyg = dy * gf
            dh1 = r2 * (dyg - hn * _row_mean(dyg * hn))
            dh1_ref[...] = dh1
            dh1b = dh1.astype(BF16)
            dm = _mm_nt(dh1b, wout[...])
            aout[...] += _mm_tn(mb, dh1b)
            dbr = (dm * sr).astype(BF16)
            dbg = (dm * sg).astype(BF16)
            dmr_ref[...] = (dm * br * sr * (1.0 - sr)).astype(BF16)
            dmg_ref[...] = (dm * bg * sg * (1.0 - sg)).astype(BF16)
            doret_ref[...] = _mm_nt(dbr, wbr[...]).astype(BF16)
            dogla_ref[...] = _mm_nt(dbg, wbg[...]).astype(BF16)
            abr[...] += _mm_tn(oret, dbr)
            abg[...] += _mm_tn(ogla, dbg)

        @pl.when(i == nt - 1)
        def _():
            wbr[...] = abr[...].astype(BF16)
            wbg[...] = abg[...].astype(BF16)
            wout[...] = aout[...].astype(BF16)
            pltpu.sync_copy(wbr, dwbr_hbm)
            pltpu.sync_copy(wbg, dwbg_hbm)
            pltpu.sync_copy(wout, dwout_hbm)

    row = lambda w: pl.BlockSpec((TILE, w), lambda i: (i, 0))
    one = lambda w: pl.BlockSpec((1, w), lambda i: (0, 0))
    return _call(
        body, "merge_fwd_bwd", grid=(nt,),
        out_shape=[jax.ShapeDtypeStruct((t_rows, D_MODEL), F32), jax.ShapeDtypeStruct((t_rows, D_MODEL), BF16),
                   jax.ShapeDtypeStruct((t_rows, D_MODEL), BF16), jax.ShapeDtypeStruct((t_rows, RET_W), BF16),
                   jax.ShapeDtypeStruct((t_rows, GLA_W), BF16), jax.ShapeDtypeStruct((1, LANES), F32),
                   jax.ShapeDtypeStruct((1, D_MODEL), F32), jax.ShapeDtypeStruct((RET_W, D_MODEL), BF16),
                   jax.ShapeDtypeStruct((GLA_W, D_MODEL), BF16), jax.ShapeDtypeStruct((D_MODEL, D_MODEL), BF16)],
        in_specs=[row(RET_W), row(GLA_W)] + _proj_specs(("mr", "mg"), 1, lambda i: (0, i)) + [_x_spec(), _x_spec(), one(D_MODEL), ANY, ANY, ANY],
        out_specs=[row(D_MODEL), row(D_MODEL), row(D_MODEL), row(RET_W), row(GLA_W), one(LANES), one(D_MODEL), ANY, ANY, ANY],
        scratch_shapes=[pltpu.VMEM((RET_W, D_MODEL), BF16), pltpu.VMEM((GLA_W, D_MODEL), BF16), pltpu.VMEM((D_MODEL, D_MODEL), BF16),
                        pltpu.VMEM((RET_W, D_MODEL), F32), pltpu.VMEM((GLA_W, D_MODEL), F32), pltpu.VMEM((D_MODEL, D_MODEL), F32),
                        pltpu.SemaphoreType.DMA((3,))],
        compiler_params=_params(("arbitrary",)),
    )(o_ret, o_gla, proj, proj, x, target, g_final, w_br, w_bg, w_out)


def _inproj_bwd_x(dseg, dglr, head, x, dh1, g_norm, slabs, w_glr, chip_partials):
    t_rows = x.shape[0] + TILE
    nt = t_rows // TILE
    ne = len(chip_partials)

    def body(*refs):
        d_refs = refs[:10]
        dglr_ref, head_ref, x_ref, dh1_ref, g_ref, slabs_a, slabs_b, wg_hbm = refs[10:18]
        part_refs = refs[18:18 + ne]
        dx_ref, dhead_ref, dgn_ref = refs[18 + ne:21 + ne]
        landed = refs[21 + ne:21 + 2 * ne]
        w_vm, wg_vm, edge_vm, sem = refs[21 + 2 * ne:25 + 2 * ne]
        exchange = _Exchange(part_refs, landed, refs[25 + 2 * ne:], among_chips=True)

        @pl.when(pl.program_id(0) == 0)
        def _():
            exchange.start()
            dgn_ref[...] = jnp.zeros_like(dgn_ref)
            _load_weight((slabs_a, slabs_b), wg_hbm, w_vm, wg_vm, edge_vm, sem)

        @pl.when(pl.program_id(0) == nt - 1)
        def _():
            exchange.finish()

        dglr = dglr_ref[0].astype(F32)
        for h in range(1, GLA_HEADS):
            dglr = dglr + dglr_ref[h].astype(F32)
        du = _mm_nt(dglr.astype(BF16), wg_vm[...])
        for s, d_ref in enumerate(d_refs):
            du = du + _mm_nt(d_ref[...], w_vm[:, SEG_OFF[s]:SEG_OFF[s] + SEG_W[s]])
        x = _tile_rows(head_ref, x_ref)
        r = lax.rsqrt(_row_mean(x * x) + EPS)
        hn = x * r
        dgn_ref[...] += _col_sum(du * hn)
        dug = du * g_ref[...]
        dh0 = dh1_ref[...] + r * (dug - hn * _row_mean(dug * hn))
        dx_ref[...] = dh0

        @pl.when(pl.program_id(0) == 0)
        def _():
            dhead_ref[...] = dh0

    row = lambda w: pl.BlockSpec((TILE, w), lambda i: (i, 0))
    one = pl.BlockSpec((1, D_MODEL), lambda i: (0, 0))
    return _call(
        body, "inproj_bwd_x", grid=(nt,),
        out_shape=[jax.ShapeDtypeStruct((t_rows - TILE, D_MODEL), F32), jax.ShapeDtypeStruct((TILE, D_MODEL), F32),
                   jax.ShapeDtypeStruct((1, D_MODEL), F32)] + [jax.ShapeDtypeStruct(a.shape, a.dtype) for a in chip_partials],
        in_specs=[row(w) for w in SEG_W] + [pl.BlockSpec((GLA_HEADS, TILE, LANES), lambda i: (0, i, 0)),
                                            _head_spec(), _x_spec(), row(D_MODEL), one, ANY, ANY, ANY] + [ANY] * ne,
        out_specs=[_x_spec(), _head_spec(), one] + [ANY] * ne,
        scratch_shapes=W_SCRATCH() + _exchange_sems(ne, N_CHIP),
        compiler_params=_params(("arbitrary",)),
    )(*[dseg[n] for n in SEG_NAMES], dglr, head, x, dh1, g_norm, *slabs, w_glr, *chip_partials)


W_TILE = 512


def _inproj_bwd_w(ut, dseg, dglr, row_sends):
    nt = ut.shape[0]
    t_rows = nt * TILE
    kc = 3 if nt % 3 == 0 else 1
    tiles = [(s, c) for s in range(len(SEG_W)) for c in range(0, SEG_W[s], W_TILE)]
    bpt = W_TILE // LANES
    nr = len(row_sends)
    n = 1 + nr
    last_tile = [(SLAB_BLK0[d] + SLAB_BLOCKS - 1) // bpt for d in range(N_DEV)]

    def body(ut_hbm, *refs):
        d_refs, dglr_hbm, row_refs = refs[:10], refs[10], refs[11:11 + nr]
        out_hbm, oglr_ref, sib = refs[11 + nr], refs[12 + nr], refs[13 + nr:13 + nr + n]
        ut_vm, dbuf, obuf, acc, gbuf, sem, send_sems, recv_sems = refs[13 + nr + n:]
        x, y, core = _position()

        def handover(d, k, landed=False):
            q = d // 2
            src = out_hbm.at[pl.ds(SLAB_BLK0[d], SLAB_BLOCKS)] if k == 0 else row_refs[k - 1].at[d]
            return pltpu.make_async_remote_copy(src_ref=sib[k].at[q] if landed else src, dst_ref=sib[k].at[q],
                                                send_sem=send_sems.at[n * q + k], recv_sem=recv_sems.at[n * q + k],
                                                device_id=(x, y, 1 - core), device_id_type=MESH)

        def for_sibling(d, ks, fn):
            @pl.when(d % 2 != core)
            def _():
                for k in ks:
                    fn(handover(d, k))

        for d in range(N_DEV):
            for_sibling(d, range(1, n), lambda cp: cp.start())

        def fetch(i):
            s, c = tiles[i]
            return pltpu.make_async_copy(d_refs[s].at[:, pl.ds(c, W_TILE)], dbuf.at[i % 2], sem.at[1 + i % 2])

        def contract(rhs_refs, width):
            acc[:, :width] = jnp.zeros((D_MODEL, width), F32)

            def step(k, carry):
                part = None
                for j in range(kc):
                    kk = k * kc + j
                    for rhs_ref in rhs_refs:
                        prod = _mm(ut_vm[kk], rhs_ref[pl.ds(pl.multiple_of(kk * TILE, TILE), TILE), :])
                        part = prod if part is None else part + prod
                acc[:, :width] += part
                return carry

            lax.fori_loop(0, nt // kc, step, 0)
            return acc[:, :width]

        load_ut = pltpu.make_async_copy(ut_hbm, ut_vm, sem.at[0])
        load_glr = pltpu.make_async_copy(dglr_hbm, gbuf, sem.at[5])
        load_ut.start()
        load_glr.start()
        fetch(0).start()
        load_ut.wait()
        stores = {}

        def stored(i):
            stores[i].wait()
            for d in range(N_DEV):
                if last_tile[d] == i:
                    for_sibling(d, [0], lambda cp: cp.start())

        for i, (s, c) in enumerate(tiles):
            if i + 1 < len(tiles):
                fetch(i + 1).start()
            fetch(i).wait()
            if i >= 2:
                stored(i - 2)
            total = contract([dbuf.at[i % 2]], W_TILE)
            for j in range(bpt):
                obuf[i % 2, j] = total[:, j * LANES:(j + 1) * LANES].astype(BF16)
            blk0 = (SEG_OFF[s] + c) // LANES
            stores[i] = pltpu.make_async_copy(obuf.at[i % 2], out_hbm.at[pl.ds(blk0, bpt)], sem.at[3 + i % 2])
            stores[i].start()
        for i in range(max(0, len(tiles) - 2), len(tiles)):
            stored(i)
        load_glr.wait()
        head_sum = gbuf[0].astype(F32)
        for h in range(1, GLA_HEADS):
            head_sum = head_sum + gbuf[h].astype(F32)
        gbuf[0] = head_sum.astype(BF16)
        oglr_ref[...] = contract([gbuf.at[0]], LANES)
        for q in range(N_CHIP):
            for k in range(n):
                handover(2 * q, k, landed=True).wait_recv()
        for d in range(N_DEV):
            for_sibling(d, range(n), lambda cp: cp.wait_send())

    outs = _call(
        body, "inproj_bwd_w",
        out_shape=[jax.ShapeDtypeStruct((AL_COLS // LANES, D_MODEL, LANES), BF16), jax.ShapeDtypeStruct((D_MODEL, LANES), F32),
                   jax.ShapeDtypeStruct((N_CHIP, SLAB_BLOCKS, D_MODEL, LANES), BF16)]
                  + [jax.ShapeDtypeStruct((N_CHIP, *r.shape[1:]), BF16) for r in row_sends],
        in_specs=[ANY] * (12 + nr), out_specs=[ANY, pl.BlockSpec(memory_space=pltpu.VMEM)] + [ANY] * n,
        scratch_shapes=[pltpu.VMEM((nt, D_MODEL, TILE), BF16), pltpu.VMEM((2, t_rows, W_TILE), BF16),
                        pltpu.VMEM((2, bpt, D_MODEL, LANES), BF16), pltpu.VMEM((D_MODEL, W_TILE), F32),
                        pltpu.VMEM((GLA_HEADS, t_rows, LANES), BF16), pltpu.SemaphoreType.DMA((6,)),
                        pltpu.SemaphoreType.DMA((n * N_CHIP,)), pltpu.SemaphoreType.DMA((n * N_CHIP,))],
        compiler_params=_params(),
    )(ut, *[dseg[n_] for n_ in SEG_NAMES], dglr, *row_sends)
    return outs[0], outs[1], outs[2], outs[3:]


def _position():
    x, y, c = lax.axis_index("x"), lax.axis_index("y"), lax.axis_index("c")
    return x, y, c


def _index(px, py, pc):
    return 4 * px + 2 * py + pc


def _gather_sems(n):
    return [pltpu.SemaphoreType.DMA((7 * n,)), pltpu.SemaphoreType.DMA((7 * n,)), pltpu.SemaphoreType.DMA((n,))]


class _RelayGather:
    STAGES = 5

    def __init__(self, ins, outs, sems):
        self.ins, self.outs, self.n = ins, outs, len(ins)
        self.send_sems, self.recv_sems, self.local_sems = sems
        x, y, c = _position()
        self.c, self.me, self.sibling = c, (x, y, c), (x, y, 1 - c)
        self.chips = [(1 - x, y), (x, 1 - y), (1 - x, 1 - y)]

    def _copy(self, a, k, block, to, src=None):
        dst = self.outs[a].at[_index(*block)]
        return pltpu.make_async_remote_copy(src_ref=dst if src is None else src, dst_ref=dst,
                                            send_sem=self.send_sems.at[7 * a + k], recv_sem=self.recv_sems.at[7 * a + k],
                                            device_id=to, device_id_type=MESH)

    def _relay(self, a, j):
        return self._copy(a, 3, (*self.chips[j], self.c), (*self.chips[1 - j], self.c))

    def _mine(self):
        return [pltpu.make_async_copy(self.ins[a], self.outs[a].at[_index(*self.me)], self.local_sems.at[a]) for a in range(self.n)]

    def _first(self):
        first = []
        for a in range(self.n):
            first.append(self._copy(a, 0, self.me, self.sibling, src=self.ins[a]))
            first += [self._copy(a, 1 + j, self.me, (*self.chips[j], self.c), src=self.ins[a]) for j in range(2)]
        return first

    def _passed(self, j):
        return [self._copy(a, 4 + j, (*self.chips[j], self.c), self.sibling) for a in range(self.n)]

    def stage(self, s):
        n, c = self.n, self.c
        if s == 0:
            for cp in self._mine() + self._first():
                cp.start()
        elif s < 4:
            j = s - 1
            for a in range(n):
                self._copy(a, 1 + j, (*self.chips[j], c), self.me).wait_recv()
            for cp in self._passed(j):
                cp.start()
            if j < 2:
                @pl.when(c == j)
                def _():
                    for a in range(n):
                        self._relay(a, j).start()
        else:
            for a in range(n):
                self._copy(a, 0, self.sibling, self.me).wait_recv()
                for j in range(3):
                    self._copy(a, 4 + j, (*self.chips[j], 1 - c), self.me).wait_recv()
            for cp in self._first() + self._passed(0) + self._passed(1) + self._passed(2):
                cp.wait_send()
            for j in range(2):
                @pl.when(c == j)
                def _():
                    for a in range(n):
                        self._relay(a, j).wait_send()
            for cp in self._mine():
                cp.wait()


def _all_gather(arrs, name):
    n = len(arrs)

    def body(*refs):
        gather = _RelayGather(refs[:n], refs[n:2 * n], refs[2 * n:])
        for s in range(gather.STAGES):
            gather.stage(s)

    return _call(
        body, name,
        out_shape=[jax.ShapeDtypeStruct((N_DEV, *a.shape), a.dtype) for a in arrs],
        in_specs=[ANY] * n, out_specs=[ANY] * n, scratch_shapes=_gather_sems(n),
    )(*arrs)


N_CHIP = N_DEV // 2


def _slab_block0(owner):
    step = SLAB_BLK0[1]
    assert all(SLAB_BLK0[d] == step * d - (d == N_DEV - 1) for d in range(N_DEV))
    return step * owner - jnp.where(owner == N_DEV - 1, 1, 0)


def _add_bf16(c_ref, a_ref, b_ref, o_ref):
    o_ref[...] = (a_ref[...].astype(F32) + b_ref[...].astype(F32)).astype(BF16)


def _chip_partial_slab(dw_blocks, sib, core):
    blk = pl.BlockSpec((None, SLAB_BLOCKS, D_MODEL, LANES), lambda q, c_ref: (q, 0, 0, 0))
    return _call(
        functools.partial(_add_bf16), "chip_partial_w_in", out_shape=jax.ShapeDtypeStruct(sib.shape, BF16),
        grid_spec=pltpu.PrefetchScalarGridSpec(
            num_scalar_prefetch=1, grid=(N_CHIP,),
            in_specs=[pl.BlockSpec((pl.Element(SLAB_BLOCKS), pl.Element(D_MODEL), pl.Element(LANES)),
                                   lambda q, c_ref: (_slab_block0(2 * q + c_ref[0]), 0, 0)), blk],
            out_specs=blk),
        compiler_params=_params(("arbitrary",)),
    )(core, dw_blocks, sib)


def _chip_partial_rows(sends, sibs, core):
    n = len(sends)

    def body(c_ref, *refs):
        for k in range(n):
            _add_bf16(c_ref, refs[k], refs[n + k], refs[2 * n + k])

    own = [pl.BlockSpec((None, *a.shape[1:]), lambda q, c_ref: (2 * q + c_ref[0], 0, 0)) for a in sends]
    blk = [pl.BlockSpec((None, *a.shape[1:]), lambda q, c_ref: (q, 0, 0)) for a in sibs]
    return _call(
        body, "chip_partial_rows", out_shape=[jax.ShapeDtypeStruct(a.shape, BF16) for a in sibs],
        grid_spec=pltpu.PrefetchScalarGridSpec(num_scalar_prefetch=1, grid=(N_CHIP,), in_specs=own + blk, out_specs=blk),
        compiler_params=_params(("arbitrary",)),
    )(core, *sends, *sibs)


def _exchange_sems(n_arrays, n_peers):
    return [pltpu.SemaphoreType.DMA((n_arrays * n_peers,)), pltpu.SemaphoreType.DMA((n_arrays * n_peers,)),
            pltpu.SemaphoreType.DMA((n_arrays,))]


class _Exchange:
    def __init__(self, srcs, dsts, sems, among_chips):
        self.arrs = list(zip(srcs, dsts))
        self.n = len(self.arrs)
        self.send_sems, self.recv_sems, self.local_sems = sems
        self.among_chips = among_chips
        x, y, c = _position()
        self.c = c
        self.me = 2 * x + y if among_chips else _index(x, y, c)
        self.n_peers = N_CHIP if among_chips else N_DEV

    def _device(self, p):
        return (p // 2, p % 2, self.c) if self.among_chips else (p // 4, (p // 2) % 2, p % 2)

    def _src(self, k, p):
        src = self.arrs[k][0]
        return src.at[p] if self.among_chips else src

    def _mine(self):
        return [pltpu.make_async_copy(self._src(k, self.me), self.arrs[k][1].at[self.me], self.local_sems.at[k]) for k in range(self.n)]

    def _copy(self, p, k, landing):
        return pltpu.make_async_remote_copy(
            src_ref=self._src(k, p), dst_ref=self.arrs[k][1].at[landing], send_sem=self.send_sems.at[self.n * p + k],
            recv_sem=self.recv_sems.at[self.n * landing + k], device_id=self._device(p), device_id_type=MESH)

    def _others(self, fn):
        for p in range(self.n_peers):
            @pl.when(p != self.me)
            def _():
                for k in range(self.n):
                    fn(p, k)

    def start(self):
        for cp in self._mine():
            cp.start()
        self._others(lambda p, k: self._copy(p, k, self.me).start())

    def finish(self):
        self._others(lambda p, k: self._copy(p, k, p).wait_recv())
        self._others(lambda p, k: self._copy(p, k, self.me).wait_send())
        for cp in self._mine():
            cp.wait()


def _adamw(g, w, m, v):
    m_new = ADAM_B1 * m + (1.0 - ADAM_B1) * g
    v_new = ADAM_B2 * v + (1.0 - ADAM_B2) * (g * g)
    m_hat = m_new / (1.0 - ADAM_B1 ** ADAM_STEP)
    v_hat = v_new / (1.0 - ADAM_B2 ** ADAM_STEP)
    delta = -ADAM_LR * (m_hat / (jnp.sqrt(v_hat) + ADAM_EPS) + ADAM_WD * w)
    return delta, m_new, v_new


def _sum_partials(p_ref):
    g = p_ref[0].astype(F32)
    for d in range(1, p_ref.shape[0]):
        g = g + p_ref[d].astype(F32)
    return g


def _reduce_adam_rows(parts, ws, ms, vs):
    n = len(ws)

    def body(*refs):
        ins, outs = refs[:4 * n], refs[4 * n:]
        for k in range(n):
            p_ref, w_ref, m_ref, v_ref = ins[k], ins[n + k], ins[2 * n + k], ins[3 * n + k]
            g = _sum_partials(p_ref)
            outs[4 * k][...] = g
            outs[4 * k + 1][...], outs[4 * k + 2][...], outs[4 * k + 3][...] = _adamw(g, w_ref[...], m_ref[...], v_ref[...])

    outs = _call(
        body, "adam_row_weights", out_shape=[jax.ShapeDtypeStruct(w.shape, F32) for w in ws for _ in range(4)],
        compiler_params=_params(),
    )(*parts, *ws, *ms, *vs)
    return [tuple(outs[4 * k:4 * k + 4]) for k in range(n)]


def _reduce_adam_slab(parts, glr, w_t, m_t, v_t, me):
    cols, rows = w_t.shape
    shift = jnp.asarray(SLAB_SHIFT, jnp.int32)[me]
    glr_at = jnp.where(me == GLR_DEV, GLR_LOCAL, cols).astype(jnp.int32)

    def body(s_ref, p_ref, glr_ref, w_ref, m_ref, v_ref, g_ref, d_ref, mo_ref, vo_ref, slab_t):
        shift, glr_at = s_ref[0], s_ref[1]
        tall = jnp.concatenate([_sum_partials(p_ref.at[:, j]).T for j in range(SLAB_BLOCKS)], axis=0)
        before = pltpu.roll(tall, SLAB_W - shift, 0)
        after = pltpu.roll(tall, lax.rem(SLAB_W - shift + GLA_RANK, SLAB_W), 0)
        wide = jnp.concatenate([glr_ref[...].T, jnp.zeros((SLAB_W - LANES, LANES), F32)], axis=0)
        placed = pltpu.roll(wide, lax.rem(glr_at, SLAB_W), 0)
        row = lax.broadcasted_iota(jnp.int32, (SLAB_W, LANES), 0)
        slab_t[...] = jnp.where(row < glr_at, before, jnp.where(row < glr_at + GLA_RANK, placed, after))
        g = slab_t[pl.ds(0, cols), :]
        g_ref[...] = g
        d_ref[...], mo_ref[...], vo_ref[...] = _adamw(g, w_ref[...], m_ref[...], v_ref[...])

    blk = pl.BlockSpec((cols, LANES), lambda i, s: (0, i))
    return _call(
        body, "adam_w_in", out_shape=[jax.ShapeDtypeStruct((cols, rows), F32)] * 4,
        grid_spec=pltpu.PrefetchScalarGridSpec(
            num_scalar_prefetch=1, grid=(rows // LANES,),
            in_specs=[pl.BlockSpec((parts.shape[0], SLAB_BLOCKS, LANES, LANES), lambda i, s: (0, 0, i, 0)),
                      pl.BlockSpec((LANES, LANES), lambda i, s: (i, 0)), blk, blk, blk],
            out_specs=[blk] * 4, scratch_shapes=[pltpu.VMEM((SLAB_W, LANES), F32)]),
        compiler_params=_params(("arbitrary",)),
    )(jnp.stack([shift, glr_at]), parts, glr, w_t, m_t, v_t)


def _reduce_small(parts):
    def body(p_ref, o_ref):
        o_ref[...] = _sum_partials(p_ref)

    return _call(body, "reduce_small", out_shape=jax.ShapeDtypeStruct(parts.shape[1:], F32))(parts)


def _adam_small(g, w, m, v):
    def body(g_ref, w_ref, m_ref, v_ref, d_ref, mo_ref, vo_ref):
        d_ref[...], mo_ref[...], vo_ref[...] = _adamw(g_ref[...], w_ref[...], m_ref[...], v_ref[...])

    return _call(body, "adam_small", out_shape=[jax.ShapeDtypeStruct(g.shape, F32)] * 3)(g, w, m, v)


def _pack_rows(arrs):
    rows = []
    for a in arrs:
        flat = a.reshape(-1).astype(F32)
        pad = (-flat.shape[0]) % LANES
        rows.append(jnp.pad(flat, (0, pad)).reshape(-1, LANES))
    packed = jnp.concatenate(rows, axis=0)
    return jnp.pad(packed, ((0, (-packed.shape[0]) % 8), (0, 0)))


def _unpack_rows(packed, shapes):
    out, r = [], 0
    for shp in shapes:
        size = 1
        for s in shp:
            size *= s
        nrows = -(-size // LANES)
        out.append(packed[r:r + nrows].reshape(-1)[:size].reshape(shp))
        r += nrows
    return out


def _shard_to_slab(shard, d):
    glr = jnp.zeros((D_MODEL, GLA_RANK), shard.dtype)
    if d == GLR_DEV:
        glr = shard[:, GLR_LOCAL:GLR_LOCAL + GLA_RANK]
        shard = jnp.concatenate([shard[:, :GLR_LOCAL], shard[:, GLR_LOCAL + GLA_RANK:]], axis=1)
    return jnp.pad(shard, ((0, 0), (SLAB_SHIFT[d], SLAB_W - SLAB_SHIFT[d] - shard.shape[1]))), glr


def kernel(x, meta_tokens, norm_gain, w_in, w_gate_up, b_gate, ret_norm_gain, gla_norm_gain, w_branch_ret, w_branch_gla, w_out, final_norm_gain, loss_target, m_meta_tokens, m_norm_gain, m_w_in, m_w_gate_up, m_b_gate, m_ret_norm_gain, m_gla_norm_gain, m_w_branch_ret, m_w_branch_gla, m_w_out, m_final_norm_gain, v_meta_tokens, v_norm_gain, v_w_in, v_w_gate_up, v_b_gate, v_ret_norm_gain, v_gla_norm_gain, v_w_branch_ret, v_w_branch_gla, v_w_out, v_final_norm_gain):
    xi, yi, ci = _position()
    me = _index(xi, yi, ci)
    seq = x.shape[1]
    t_rows = seq + TILE
    in_shard = w_in.shape[2]
    gu_shard = w_gate_up.shape[2]
    meta_shard = meta_tokens.shape[1]
    ret_rows, gla_rows, out_rows = w_branch_ret.shape[1], w_branch_gla.shape[1], w_out.shape[1]

    assert in_shard == IN_SHARD
    slab_local, glr_local = lax.switch(me, [functools.partial(_shard_to_slab, d=d) for d in range(N_DEV)], w_in[0])
    small_local = jnp.concatenate([meta_tokens, jnp.pad(w_gate_up[0], ((0, 0), (0, LANES - gu_shard))),
                                   glr_local.reshape(-1, LANES)], axis=0)
    slab_local = slab_local.astype(BF16)
    first_halves, g_small = _all_gather([slab_local[:, :HALF_W], small_local], "all_gather_shards")
    n_small = N_META + GLA_RANK
    w_glr = jnp.pad(g_small[GLR_DEV, n_small:].reshape(D_MODEL, GLA_RANK), ((0, 0), (0, LANES - GLA_RANK))).astype(BF16)
    meta_full = jnp.transpose(g_small[:, :N_META, :], (1, 0, 2)).reshape(N_META, D_MODEL)
    wgu_full = jnp.transpose(g_small[:, N_META:n_small, :gu_shard], (1, 0, 2)).reshape(GLA_RANK, GLA_HEADS * GLA_K)
    wgu_pad = jnp.pad(wgu_full, ((0, LANES - GLA_RANK), (0, 0)))

    rope = _rope_tables(t_rows // TILE)
    lg = jnp.log1p(-(2.0 ** (-5.0 - jnp.arange(RET_HEADS, dtype=F32))))

    head = jnp.concatenate([jnp.zeros((PAD_ROWS, D_MODEL), F32), meta_full], axis=0)
    proj_first, second_halves = _inproj_first(head, x[0], norm_gain, first_halves, slab_local[:, HALF_W:])
    slabs = (first_halves, second_halves)
    ut, proj, glr = _inproj_tiles(head, x[0], norm_gain, slabs, w_glr, proj_first)
    o_ret_raw, o_ret, ret_states, (g_br, g_bg, g_o) = _ret_fwd(
        proj, rope, ret_norm_gain, lg, [w_branch_ret[0].astype(BF16), w_branch_gla[0].astype(BF16), w_out[0].astype(BF16)])
    w_br, w_bg, w_o = g_br.reshape(RET_W, D_MODEL), g_bg.reshape(GLA_W, D_MODEL), g_o.reshape(D_MODEL, D_MODEL)
    masks, cum_fwd, cum_bwd = _gla_tables()
    o_gla_raw, o_gla, gla_states, gla_scores_t = _gla_fwd(proj, glr, wgu_pad, b_gate, gla_norm_gain, masks, cum_fwd)
    (dh1, d_mr, d_mg, do_ret, do_gla, loss_part, d_gfinal, dw_br, dw_bg, dw_o) = _merge_fwd_bwd(
        o_ret, o_gla, proj, x[0], loss_target[0], final_norm_gain.reshape(1, D_MODEL), w_br, w_bg, w_o)

    d_rq, d_rk, d_rv, d_rg, d_gret = _ret_bwd(proj, rope, ret_norm_gain, lg, o_ret_raw, do_ret, ret_states)
    d_gq, d_gk, d_gv, d_gg, dglr_parts, d_wgu, d_bgate, d_ggla = _gla_bwd(
        proj, glr, wgu_pad, b_gate, gla_norm_gain, o_gla_raw, do_gla, gla_states, gla_scores_t, masks, cum_fwd, cum_bwd)
    dseg = dict(rq=d_rq, rk=d_rk, rv=d_rv, rg=d_rg, gq=d_gq, gk=d_gk, gv=d_gv, gg=d_gg, mr=d_mr, mg=d_mg)
    row_sends = [dw_br.reshape(N_DEV, ret_rows, D_MODEL), dw_bg.reshape(N_DEV, gla_rows, D_MODEL),
                 dw_o.reshape(N_DEV, out_rows, D_MODEL)]
    dw_blocks, dw_glr, sib_in, sib_rows = _inproj_bwd_w(ut, dseg, dglr_parts, row_sends)
    core = ci.astype(jnp.int32).reshape(1)
    chip_partials = [_chip_partial_slab(dw_blocks, sib_in, core)] + list(_chip_partial_rows(row_sends, list(sib_rows), core))
    grad_x, d_head, d_gnorm, p_in, p_br, p_bg, p_o = _inproj_bwd_x(
        dseg, dglr_parts, head, x[0], dh1, norm_gain, slabs, w_glr, chip_partials)
    small_shapes = [(N_META, D_MODEL), (1, D_MODEL), (GLA_RANK, GLA_HEADS * GLA_K), (1, GLA_HEADS * GLA_K),
                    (1, RET_W), (1, GLA_W), (1, D_MODEL), (1, LANES), (D_MODEL, GLA_RANK)]
    small_part = _pack_rows([d_head[PAD_ROWS:], d_gnorm, d_wgu[:GLA_RANK], d_bgate, d_gret, d_ggla, d_gfinal, loss_part,
                             dw_glr[:, :GLA_RANK]])
    (p_small,) = _all_gather([small_part], "all_gather_small_partials")

    (g_meta_f, g_gnorm, g_wgu_f, g_bgate, g_gret, g_ggla, g_gfinal, loss_all,
     g_wglr) = _unpack_rows(_reduce_small(p_small), small_shapes)
    g_w_in, d_w_in, nm_w_in, nv_w_in = [a.T for a in _reduce_adam_slab(
        p_in, jnp.pad(g_wglr, ((0, 0), (0, LANES - GLA_RANK))), w_in[0].T, m_w_in[0].T, v_w_in[0].T, me)]
    ((g_w_br, d_w_br, nm_w_br, nv_w_br), (g_w_bg, d_w_bg, nm_w_bg, nv_w_bg), (g_w_o, d_w_o, nm_w_o, nv_w_o)) = _reduce_adam_rows(
        [p_br, p_bg, p_o], [w_branch_ret[0], w_branch_gla[0], w_out[0]], [m_w_branch_ret[0], m_w_branch_gla[0], m_w_out[0]],
        [v_w_branch_ret[0], v_w_branch_gla[0], v_w_out[0]])
    g_meta = lax.dynamic_slice_in_dim(g_meta_f, me * meta_shard, meta_shard, axis=1)
    g_wgu = lax.dynamic_slice_in_dim(g_wgu_f, me * gu_shard, gu_shard, axis=1)
    s_g = [g_meta, g_gnorm, g_wgu, g_bgate, g_gret, g_ggla, g_gfinal]
    s_w = [meta_tokens, norm_gain, w_gate_up[0], b_gate, ret_norm_gain, gla_norm_gain, final_norm_gain]
    s_m = [m_meta_tokens, m_norm_gain, m_w_gate_up[0], m_b_gate, m_ret_norm_gain, m_gla_norm_gain, m_final_norm_gain]
    s_v = [v_meta_tokens, v_norm_gain, v_w_gate_up[0], v_b_gate, v_ret_norm_gain, v_gla_norm_gain, v_final_norm_gain]
    shapes = [a.shape for a in s_g]
    s_d, s_nm, s_nv = [_unpack_rows(p, shapes) for p in _adam_small(*[_pack_rows(l) for l in (s_g, s_w, s_m, s_v)])]

    loss = loss_all[0, 0]
    grad_x = grad_x[None]

    def order(meta, gnorm, win, wgu, bgate, gret, ggla, wbr, wbg, wo, gfin):
        return (meta, gnorm, win[None], wgu[None], bgate, gret, ggla, wbr[None], wbg[None], wo[None], gfin.reshape(final_norm_gain.shape))

    def small(l):
        return dict(meta=l[0], gnorm=l[1], wgu=l[2], bgate=l[3], gret=l[4], ggla=l[5], gfin=l[6])

    grads = order(win=g_w_in, wbr=g_w_br, wbg=g_w_bg, wo=g_w_o, **small(s_g))
    deltas = order(win=d_w_in, wbr=d_w_br, wbg=d_w_bg, wo=d_w_o, **small(s_d))
    new_m = order(win=nm_w_in, wbr=nm_w_br, wbg=nm_w_bg, wo=nm_w_o, **small(s_nm))
    new_v = order(win=nv_w_in, wbr=nv_w_br, wbg=nv_w_bg, wo=nv_w_o, **small(s_nv))
    return (loss, grad_x, *grads, *deltas, *new_m, *new_v)
```
